```python
import jax, jax.numpy as jnp
from jax import lax
import numpy as np

D_MODEL = 1024
BATCH = 8
SEQ = 8192
DEPTH = 2

GROUP_WIDTH = D_MODEL // 2
D_MIX = 3 * GROUP_WIDTH
BLOCK = 128
RMS_EPS = 1e-6
NEG_INF = -1e30

MLA_HEADS = 8
MLA_NOPE_DIM = 64
MLA_ROPE_DIM = 32
MLA_QK_DIM = MLA_NOPE_DIM + MLA_ROPE_DIM
MLA_V_DIM = 64
MLA_Q_LORA = 256
MLA_KV_LORA = 128
ROPE_THETA = 10000.0

CONV_DIM = GROUP_WIDTH
CONV_WIDTH = 3

SWA_HEADS = 8
SWA_KV_HEADS = 2
SWA_GROUP = SWA_HEADS // SWA_KV_HEADS
SWA_HEAD_DIM = 64
SWA_WINDOW = 128

IN_SPLITS = (
    MLA_Q_LORA, MLA_KV_LORA, MLA_ROPE_DIM, GROUP_WIDTH,
    CONV_DIM, CONV_DIM, CONV_DIM, GROUP_WIDTH,
    SWA_HEADS * SWA_HEAD_DIM, SWA_KV_HEADS * SWA_HEAD_DIM,
    SWA_KV_HEADS * SWA_HEAD_DIM, GROUP_WIDTH,
)
IN_COLS = sum(IN_SPLITS)

kernel_name = "hymba_mla_shortconv_swa_hybrid"


def rms_norm(x, g):
    xf = x.astype(jnp.float32)
    y = xf * lax.rsqrt(jnp.mean(xf * xf, axis=-1, keepdims=True) + RMS_EPS)
    return (y * g.astype(jnp.float32)).astype(x.dtype)


def apply_rope(x, pos):
    half = x.shape[-1] // 2
    inv_freq = jnp.power(jnp.float32(ROPE_THETA), -jnp.arange(half, dtype=jnp.float32) / half)
    ang = pos[:, None] * inv_freq[None, :]
    cos = jnp.cos(ang)[:, None, :]
    sin = jnp.sin(ang)[:, None, :]
    xf = x.astype(jnp.float32)
    x1, x2 = xf[..., :half], xf[..., half:]
    out = jnp.concatenate([x1 * cos - x2 * sin, x2 * cos + x1 * sin], axis=-1)
    return out.astype(x.dtype)


def mla_mixer(q_lat, kv_lat, k_rope, q_a_norm, w_qb, kv_a_norm, w_kvb, q_norm, k_norm):
    b, s, _ = q_lat.shape
    q = (rms_norm(q_lat, q_a_norm) @ w_qb).reshape(b, s, MLA_HEADS, MLA_QK_DIM)
    kv = (rms_norm(kv_lat, kv_a_norm) @ w_kvb).reshape(b, s, MLA_HEADS, MLA_NOPE_DIM + MLA_V_DIM)
    k_nope, v = kv[..., :MLA_NOPE_DIM], kv[..., MLA_NOPE_DIM:]
    k_pe = jnp.broadcast_to(k_rope[:, :, None, :], (b, s, MLA_HEADS, MLA_ROPE_DIM))
    k = jnp.concatenate([k_nope, k_pe], axis=-1)
    q = rms_norm(q, q_norm)
    k = rms_norm(k, k_norm)
    pos = jnp.arange(s, dtype=jnp.float32)
    q = jnp.concatenate([q[..., :MLA_NOPE_DIM], apply_rope(q[..., MLA_NOPE_DIM:], pos)], axis=-1)
    k = jnp.concatenate([k[..., :MLA_NOPE_DIM], apply_rope(k[..., MLA_NOPE_DIM:], pos)], axis=-1)
    scale = MLA_QK_DIM ** -0.5
    nb = s // BLOCK
    q_blocks = q.reshape(b, nb, BLOCK, MLA_HEADS, MLA_QK_DIM).transpose(1, 0, 2, 3, 4)
    starts = jnp.arange(nb, dtype=jnp.int32) * BLOCK
    key_pos = jnp.arange(s, dtype=jnp.int32)

    def attend_block(args):
        qb, start = args
        sc = jnp.einsum('bqhd,bkhd->bhqk', qb, k, preferred_element_type=jnp.float32) * scale
        q_pos = start + jnp.arange(BLOCK, dtype=jnp.int32)
        causal = key_pos[None, :] <= q_pos[:, None]
        sc = jnp.where(causal[None, None], sc, NEG_INF)
        p = jax.nn.softmax(sc, axis=-1).astype(v.dtype)
        return jnp.einsum('bhqk,bkhd->bqhd', p, v)

    o = lax.map(attend_block, (q_blocks, starts))
    return o.transpose(1, 0, 2, 3, 4).reshape(b, s, MLA_HEADS * MLA_V_DIM)


def short_conv_mixer(h, b_gate, c_gate, conv_w):
    u = c_gate * h
    y = lax.conv_general_dilated(
        u, conv_w[:, None, :].astype(u.dtype), window_strides=(1,),
        padding=[(CONV_WIDTH - 1, 0)], dimension_numbers=('NWC', 'WIO', 'NWC'),
        feature_group_count=CONV_DIM)
    return b_gate * y


def _band(t):
    b, s, kv, d = t.shape
    nb = s // BLOCK
    tp = jnp.concatenate([jnp.zeros((b, BLOCK, kv, d), t.dtype), t], axis=1)
    tp = tp.reshape(b, nb + 1, BLOCK, kv, d)
    return jnp.concatenate([tp[:, :-1], tp[:, 1:]], axis=2)


def swa_mixer(q, k, v, q_norm, k_norm, sinks):
    b, s, _ = q.shape
    nb = s // BLOCK
    q = rms_norm(q.reshape(b, s, SWA_HEADS, SWA_HEAD_DIM), q_norm)
    k = rms_norm(k.reshape(b, s, SWA_KV_HEADS, SWA_HEAD_DIM), k_norm)
    v = v.reshape(b, s, SWA_KV_HEADS, SWA_HEAD_DIM)
    qb = q.reshape(b, nb, BLOCK, SWA_KV_HEADS, SWA_GROUP, SWA_HEAD_DIM)
    kb, vb = _band(k), _band(v)
    sc = jnp.einsum('bnqkgd,bnskd->bnkgqs', qb, kb,
                    preferred_element_type=jnp.float32) * (SWA_HEAD_DIM ** -0.5)
    q_idx = jnp.arange(BLOCK, dtype=jnp.int32)[:, None]
    k_idx = jnp.arange(2 * BLOCK, dtype=jnp.int32)[None, :]
    dist = BLOCK + q_idx - k_idx
    key_pos = (jnp.arange(nb, dtype=jnp.int32)[:, None] - 1) * BLOCK + k_idx
    valid = ((dist >= 0) & (dist < SWA_WINDOW))[None] & (key_pos >= 0)[:, None, :]
    slopes = jnp.exp2(-8.0 * jnp.arange(1, SWA_HEADS + 1, dtype=jnp.float32) / SWA_HEADS)
    slopes = slopes.reshape(SWA_KV_HEADS, SWA_GROUP)
    sc = sc - slopes[None, None, :, :, None, None] * dist.astype(jnp.float32)[None, None, None, None]
    sc = jnp.where(valid[None, :, None, None], sc, NEG_INF)
    sink = jnp.broadcast_to(
        sinks.astype(jnp.float32).reshape(SWA_KV_HEADS, SWA_GROUP)[None, None, :, :, None, None],
        sc.shape[:-1] + (1,))
    p = jax.nn.softmax(jnp.concatenate([sc, sink], axis=-1), axis=-1)[..., :-1].astype(v.dtype)
    o = jnp.einsum('bnkgqs,bnskd->bnqkgd', p, vb)
    return o.reshape(b, s, SWA_HEADS * SWA_HEAD_DIM)


def hybrid_layer(x, norm_g, w_in, mla_q_a_norm, mla_w_qb, mla_kv_a_norm, mla_w_kvb,
                 mla_q_norm, mla_k_norm, conv_w, swa_q_norm, swa_k_norm, swa_sinks, w_out):
    h = rms_norm(x, norm_g)
    proj = h @ w_in
    offsets = [int(o) for o in np.cumsum(IN_SPLITS)[:-1]]
    (q_lat, kv_lat, k_rope, g_mla,
     c_h, c_b, c_c, g_conv,
     s_q, s_k, s_v, g_swa) = jnp.split(proj, offsets, axis=-1)
    y_mla = mla_mixer(q_lat, kv_lat, k_rope, mla_q_a_norm, mla_w_qb, mla_kv_a_norm,
                      mla_w_kvb, mla_q_norm, mla_k_norm) * jax.nn.silu(g_mla)
    y_conv = short_conv_mixer(c_h, c_b, c_c, conv_w) * jax.nn.silu(g_conv)
    y_swa = swa_mixer(s_q, s_k, s_v, swa_q_norm, swa_k_norm, swa_sinks) * jax.nn.silu(g_swa)
    y = jnp.concatenate([y_mla, y_conv, y_swa], axis=-1) @ w_out
    return x + y


def _fwd_setup_inputs(seed: int = 0) -> dict:
    key = jax.random.key(seed)
    ks = jax.random.split(key, 14)
    f32 = jnp.float32

    def nrm(k, shape, scale):
        return jax.random.normal(k, shape, f32) * scale

    def gain(k, n):
        return 1.0 + 0.02 * jax.random.normal(k, (DEPTH, n), f32)

    return {
        "x": jax.random.normal(ks[0], (BATCH, SEQ, D_MODEL), f32),
        "norm_g": gain(ks[1], D_MODEL),
        "w_in": nrm(ks[2], (DEPTH, D_MODEL, IN_COLS), D_MODEL ** -0.5),
        "mla_q_a_norm": gain(ks[3], MLA_Q_LORA),
        "mla_w_qb": nrm(ks[4], (DEPTH, MLA_Q_LORA, MLA_HEADS * MLA_QK_DIM), MLA_Q_LORA ** -0.5),
        "mla_kv_a_norm": gain(ks[5], MLA_KV_LORA),
        "mla_w_kvb": nrm(ks[6], (DEPTH, MLA_KV_LORA, MLA_HEADS * (MLA_NOPE_DIM + MLA_V_DIM)), MLA_KV_LORA ** -0.5),
        "mla_q_norm": gain(ks[7], MLA_QK_DIM),
        "mla_k_norm": gain(ks[8], MLA_QK_DIM),
        "conv_w": nrm(ks[9], (DEPTH, CONV_WIDTH, CONV_DIM), CONV_WIDTH ** -0.5),
        "swa_q_norm": gain(ks[10], SWA_HEAD_DIM),
        "swa_k_norm": gain(ks[11], SWA_HEAD_DIM),
        "swa_sinks": nrm(ks[12], (DEPTH, SWA_HEADS), 0.5),
        "w_out": nrm(ks[13], (DEPTH, D_MIX, D_MODEL), D_MIX ** -0.5),
    }


def _fwd_reference(x, norm_g, w_in, mla_q_a_norm, mla_w_qb, mla_kv_a_norm, mla_w_kvb,
              mla_q_norm, mla_k_norm, conv_w, swa_q_norm, swa_k_norm, swa_sinks, w_out):
    for l in range(DEPTH):
        x = hybrid_layer(x, norm_g[l], w_in[l], mla_q_a_norm[l], mla_w_qb[l], mla_kv_a_norm[l],
                         mla_w_kvb[l], mla_q_norm[l], mla_k_norm[l], conv_w[l], swa_q_norm[l],
                         swa_k_norm[l], swa_sinks[l], w_out[l])
    return x


import jax as _jax
import jax.numpy as _jnp

TWIN_FORMAT = 'train_step'
FWD_PARAMS = ['x', 'norm_g', 'w_in', 'mla_q_a_norm', 'mla_w_qb', 'mla_kv_a_norm', 'mla_w_kvb', 'mla_q_norm', 'mla_k_norm', 'conv_w', 'swa_q_norm', 'swa_k_norm', 'swa_sinks', 'w_out']
TWIN_WEIGHTS = ['norm_g', 'w_in', 'mla_q_a_norm', 'mla_w_qb', 'mla_kv_a_norm', 'mla_w_kvb', 'mla_q_norm', 'mla_k_norm', 'conv_w', 'swa_q_norm', 'swa_k_norm', 'swa_sinks', 'w_out']
TWIN_DIFF_INPUT = 'x'
TWIN_INPUTS = ['x', 'norm_g', 'w_in', 'mla_q_a_norm', 'mla_w_qb', 'mla_kv_a_norm', 'mla_w_kvb', 'mla_q_norm', 'mla_k_norm', 'conv_w', 'swa_q_norm', 'swa_k_norm', 'swa_sinks', 'w_out', 'loss_target', 'm_norm_g', 'm_w_in', 'm_mla_q_a_norm', 'm_mla_w_qb', 'm_mla_kv_a_norm', 'm_mla_w_kvb', 'm_mla_q_norm', 'm_mla_k_norm', 'm_conv_w', 'm_swa_q_norm', 'm_swa_k_norm', 'm_swa_sinks', 'm_w_out', 'v_norm_g', 'v_w_in', 'v_mla_q_a_norm', 'v_mla_w_qb', 'v_mla_kv_a_norm', 'v_mla_w_kvb', 'v_mla_q_norm', 'v_mla_k_norm', 'v_conv_w', 'v_swa_q_norm', 'v_swa_k_norm', 'v_swa_sinks', 'v_w_out']
TWIN_OUTPUTS = ['loss', 'grad_x', 'grad_norm_g', 'grad_w_in', 'grad_mla_q_a_norm', 'grad_mla_w_qb', 'grad_mla_kv_a_norm', 'grad_mla_w_kvb', 'grad_mla_q_norm', 'grad_mla_k_norm', 'grad_conv_w', 'grad_swa_q_norm', 'grad_swa_k_norm', 'grad_swa_sinks', 'grad_w_out', 'delta_norm_g', 'delta_w_in', 'delta_mla_q_a_norm', 'delta_mla_w_qb', 'delta_mla_kv_a_norm', 'delta_mla_w_kvb', 'delta_mla_q_norm', 'delta_mla_k_norm', 'delta_conv_w', 'delta_swa_q_norm', 'delta_swa_k_norm', 'delta_swa_sinks', 'delta_w_out', 'new_m_norm_g', 'new_m_w_in', 'new_m_mla_q_a_norm', 'new_m_mla_w_qb', 'new_m_mla_kv_a_norm', 'new_m_mla_w_kvb', 'new_m_mla_q_norm', 'new_m_mla_k_norm', 'new_m_conv_w', 'new_m_swa_q_norm', 'new_m_swa_k_norm', 'new_m_swa_sinks', 'new_m_w_out', 'new_v_norm_g', 'new_v_w_in', 'new_v_mla_q_a_norm', 'new_v_mla_w_qb', 'new_v_mla_kv_a_norm', 'new_v_mla_w_kvb', 'new_v_mla_q_norm', 'new_v_mla_k_norm', 'new_v_conv_w', 'new_v_swa_q_norm', 'new_v_swa_k_norm', 'new_v_swa_sinks', 'new_v_w_out']
TWIN_LEAF_KINDS = {'loss': 'loss', 'grad_x': 'grad_x', 'grad_norm_g': 'grad_w', 'grad_w_in': 'grad_w', 'grad_mla_q_a_norm': 'grad_w', 'grad_mla_w_qb': 'grad_w', 'grad_mla_kv_a_norm': 'grad_w', 'grad_mla_w_kvb': 'grad_w', 'grad_mla_q_norm': 'grad_w', 'grad_mla_k_norm': 'grad_w', 'grad_conv_w': 'grad_w', 'grad_swa_q_norm': 'grad_w', 'grad_swa_k_norm': 'grad_w', 'grad_swa_sinks': 'grad_w', 'grad_w_out': 'grad_w', 'delta_norm_g': 'delta_w', 'delta_w_in': 'delta_w', 'delta_mla_q_a_norm': 'delta_w', 'delta_mla_w_qb': 'delta_w', 'delta_mla_kv_a_norm': 'delta_w', 'delta_mla_w_kvb': 'delta_w', 'delta_mla_q_norm': 'delta_w', 'delta_mla_k_norm': 'delta_w', 'delta_conv_w': 'delta_w', 'delta_swa_q_norm': 'delta_w', 'delta_swa_k_norm': 'delta_w', 'delta_swa_sinks': 'delta_w', 'delta_w_out': 'delta_w', 'new_m_norm_g': 'new_m', 'new_m_w_in': 'new_m', 'new_m_mla_q_a_norm': 'new_m', 'new_m_mla_w_qb': 'new_m', 'new_m_mla_kv_a_norm': 'new_m', 'new_m_mla_w_kvb': 'new_m', 'new_m_mla_q_norm': 'new_m', 'new_m_mla_k_norm': 'new_m', 'new_m_conv_w': 'new_m', 'new_m_swa_q_norm': 'new_m', 'new_m_swa_k_norm': 'new_m', 'new_m_swa_sinks': 'new_m', 'new_m_w_out': 'new_m', 'new_v_norm_g': 'new_v', 'new_v_w_in': 'new_v', 'new_v_mla_q_a_norm': 'new_v', 'new_v_mla_w_qb': 'new_v', 'new_v_mla_kv_a_norm': 'new_v', 'new_v_mla_w_kvb': 'new_v', 'new_v_mla_q_norm': 'new_v', 'new_v_mla_k_norm': 'new_v', 'new_v_conv_w': 'new_v', 'new_v_swa_q_norm': 'new_v', 'new_v_swa_k_norm': 'new_v', 'new_v_swa_sinks': 'new_v', 'new_v_w_out': 'new_v'}


def _forward(args):
    return _fwd_reference(*[args[k] for k in FWD_PARAMS])


def _output_shape():
    def fwd():
        inp = _fwd_setup_inputs(0)
        return _fwd_reference(*[inp[k] for k in FWD_PARAMS])
    out = _jax.eval_shape(fwd)
    return out.shape, out.dtype

N_MICROBATCH = 1
ADAM_LR = 0.001
ADAM_B1 = 0.9
ADAM_B2 = 0.999
ADAM_EPS = 1e-08
ADAM_WD = 0.01
ADAM_STEP = 10
PER_EXAMPLE_BATCH_AXIS = {'x': 0, 'loss_target': 0}
SHARED_INPUTS = []
_WEIGHT_DTYPES = {'norm_g': _jnp.float32, 'w_in': _jnp.float32, 'mla_q_a_norm': _jnp.float32, 'mla_w_qb': _jnp.float32, 'mla_kv_a_norm': _jnp.float32, 'mla_w_kvb': _jnp.float32, 'mla_q_norm': _jnp.float32, 'mla_k_norm': _jnp.float32, 'conv_w': _jnp.float32, 'swa_q_norm': _jnp.float32, 'swa_k_norm': _jnp.float32, 'swa_sinks': _jnp.float32, 'w_out': _jnp.float32}
MOMENT_SCALE = {'norm_g': 3.291545e+01, 'w_in': 5.284670e-01, 'mla_q_a_norm': 8.366562e-02, 'mla_w_qb': 4.839008e-02, 'mla_kv_a_norm': 5.940195e-01, 'mla_w_kvb': 6.379751e-02, 'mla_q_norm': 4.451998e-01, 'mla_k_norm': 4.467219e-01, 'conv_w': 8.428016e+00, 'swa_q_norm': 3.984644e+00, 'swa_k_norm': 3.970651e+00, 'swa_sinks': 1.268663e+01, 'w_out': 3.624080e-01}


def _to_microbatches(a, axis):
    t = _jnp.moveaxis(a, axis, 0)
    t = t.reshape((N_MICROBATCH, t.shape[0] // N_MICROBATCH) + t.shape[1:])
    return _jnp.moveaxis(t, 1, axis + 1)


def setup_inputs(seed: int = 0) -> dict:
    inp = _fwd_setup_inputs(seed)
    key = _jax.random.fold_in(_jax.random.key(seed), 7919)
    shape, _ = _output_shape()
    out = dict(inp)
    out["loss_target"] = _jax.random.normal(_jax.random.fold_in(key, 0), shape, _jnp.float32)
    for i, name in enumerate(TWIN_WEIGHTS):
        w = inp[name].astype(_jnp.float32)
        if MOMENT_SCALE is None:
            s = _jnp.sqrt(_jnp.mean(_jnp.square(w)) + 1e-30)
        else:
            s = MOMENT_SCALE[name]
        km, kv = _jax.random.split(_jax.random.fold_in(key, i + 1))
        out[name] = w
        out["m_" + name] = s * _jax.random.normal(km, w.shape, _jnp.float32)
        out["v_" + name] = (s * s) * _jax.random.uniform(kv, w.shape, _jnp.float32, 0.5, 1.5)
    if N_MICROBATCH > 1:
        for name, axis in PER_EXAMPLE_BATCH_AXIS.items():
            out[name] = _to_microbatches(out[name], axis)
    return {'x': out['x'], 'norm_g': out['norm_g'], 'w_in': out['w_in'], 'mla_q_a_norm': out['mla_q_a_norm'], 'mla_w_qb': out['mla_w_qb'], 'mla_kv_a_norm': out['mla_kv_a_norm'], 'mla_w_kvb': out['mla_w_kvb'], 'mla_q_norm': out['mla_q_norm'], 'mla_k_norm': out['mla_k_norm'], 'conv_w': out['conv_w'], 'swa_q_norm': out['swa_q_norm'], 'swa_k_norm': out['swa_k_norm'], 'swa_sinks': out['swa_sinks'], 'w_out': out['w_out'], 'loss_target': out['loss_target'], 'm_norm_g': out['m_norm_g'], 'm_w_in': out['m_w_in'], 'm_mla_q_a_norm': out['m_mla_q_a_norm'], 'm_mla_w_qb': out['m_mla_w_qb'], 'm_mla_kv_a_norm': out['m_mla_kv_a_norm'], 'm_mla_w_kvb': out['m_mla_w_kvb'], 'm_mla_q_norm': out['m_mla_q_norm'], 'm_mla_k_norm': out['m_mla_k_norm'], 'm_conv_w': out['m_conv_w'], 'm_swa_q_norm': out['m_swa_q_norm'], 'm_swa_k_norm': out['m_swa_k_norm'], 'm_swa_sinks': out['m_swa_sinks'], 'm_w_out': out['m_w_out'], 'v_norm_g': out['v_norm_g'], 'v_w_in': out['v_w_in'], 'v_mla_q_a_norm': out['v_mla_q_a_norm'], 'v_mla_w_qb': out['v_mla_w_qb'], 'v_mla_kv_a_norm': out['v_mla_kv_a_norm'], 'v_mla_w_kvb': out['v_mla_w_kvb'], 'v_mla_q_norm': out['v_mla_q_norm'], 'v_mla_k_norm': out['v_mla_k_norm'], 'v_conv_w': out['v_conv_w'], 'v_swa_q_norm': out['v_swa_q_norm'], 'v_swa_k_norm': out['v_swa_k_norm'], 'v_swa_sinks': out['v_swa_sinks'], 'v_w_out': out['v_w_out']}


def _loss(weights, diff, rest, loss_target):
    with _jax.named_scope("forward"):
        args = {**rest, TWIN_DIFF_INPUT: diff, **{k: w.astype(_WEIGHT_DTYPES[k]) for k, w in weights.items()}}
        y = _forward(args)
    with _jax.named_scope("loss_head"):
        err = _jnp.square(y.astype(_jnp.float32) - loss_target)
        return 0.5 * _jnp.sum(_jnp.mean(err, axis=-1)) if err.ndim else 0.5 * err


def _adamw(w, g, m, v):
    m = ADAM_B1 * m + (1.0 - ADAM_B1) * g
    v = ADAM_B2 * v + (1.0 - ADAM_B2) * _jnp.square(g)
    m_hat = m / (1.0 - ADAM_B1 ** ADAM_STEP)
    v_hat = v / (1.0 - ADAM_B2 ** ADAM_STEP)
    delta = -ADAM_LR * (m_hat / (_jnp.sqrt(v_hat) + ADAM_EPS) + ADAM_WD * w)
    return delta, m, v


def reference(x, norm_g, w_in, mla_q_a_norm, mla_w_qb, mla_kv_a_norm, mla_w_kvb, mla_q_norm, mla_k_norm, conv_w, swa_q_norm, swa_k_norm, swa_sinks, w_out, loss_target, m_norm_g, m_w_in, m_mla_q_a_norm, m_mla_w_qb, m_mla_kv_a_norm, m_mla_w_kvb, m_mla_q_norm, m_mla_k_norm, m_conv_w, m_swa_q_norm, m_swa_k_norm, m_swa_sinks, m_w_out, v_norm_g, v_w_in, v_mla_q_a_norm, v_mla_w_qb, v_mla_kv_a_norm, v_mla_w_kvb, v_mla_q_norm, v_mla_k_norm, v_conv_w, v_swa_q_norm, v_swa_k_norm, v_swa_sinks, v_w_out):
    given = dict(x=x, norm_g=norm_g, w_in=w_in, mla_q_a_norm=mla_q_a_norm, mla_w_qb=mla_w_qb, mla_kv_a_norm=mla_kv_a_norm, mla_w_kvb=mla_w_kvb, mla_q_norm=mla_q_norm, mla_k_norm=mla_k_norm, conv_w=conv_w, swa_q_norm=swa_q_norm, swa_k_norm=swa_k_norm, swa_sinks=swa_sinks, w_out=w_out, loss_target=loss_target, m_norm_g=m_norm_g, m_w_in=m_w_in, m_mla_q_a_norm=m_mla_q_a_norm, m_mla_w_qb=m_mla_w_qb, m_mla_kv_a_norm=m_mla_kv_a_norm, m_mla_w_kvb=m_mla_w_kvb, m_mla_q_norm=m_mla_q_norm, m_mla_k_norm=m_mla_k_norm, m_conv_w=m_conv_w, m_swa_q_norm=m_swa_q_norm, m_swa_k_norm=m_swa_k_norm, m_swa_sinks=m_swa_sinks, m_w_out=m_w_out, v_norm_g=v_norm_g, v_w_in=v_w_in, v_mla_q_a_norm=v_mla_q_a_norm, v_mla_w_qb=v_mla_w_qb, v_mla_kv_a_norm=v_mla_kv_a_norm, v_mla_w_kvb=v_mla_w_kvb, v_mla_q_norm=v_mla_q_norm, v_mla_k_norm=v_mla_k_norm, v_conv_w=v_conv_w, v_swa_q_norm=v_swa_q_norm, v_swa_k_norm=v_swa_k_norm, v_swa_sinks=v_swa_sinks, v_w_out=v_w_out)
    weights = {n: given[n] for n in TWIN_WEIGHTS}
    shared = {n: given[n] for n in SHARED_INPUTS}
    per_example = {n: given[n] for n in ['x']}
    grad_fn = _jax.value_and_grad(_loss, argnums=(0, 1))

    def one_microbatch(ex, loss_target):
        ex = dict(ex)
        diff = ex.pop(TWIN_DIFF_INPUT)
        return grad_fn(weights, diff, {**shared, **ex}, loss_target)

    if N_MICROBATCH == 1:
        loss, (grad_w, grad_x) = one_microbatch(per_example, given["loss_target"])
    else:
        def body(carry, xs):
            loss_sum, grad_sum = carry
            l_k, (gw_k, gx_k) = one_microbatch(xs[0], xs[1])
            with _jax.named_scope("update"):
                return (loss_sum + l_k, _jax.tree.map(_jnp.add, grad_sum, gw_k)), gx_k

        init = (_jnp.zeros((), _jnp.float32), _jax.tree.map(_jnp.zeros_like, weights))
        (loss, grad_w), grad_x = _jax.lax.scan(body, init, (per_example, given["loss_target"]))
    with _jax.named_scope("update"):
        delta_w, new_m, new_v = {}, {}, {}
        for n in TWIN_WEIGHTS:
            delta_w[n], new_m[n], new_v[n] = _adamw(weights[n], grad_w[n], given["m_" + n], given["v_" + n])
    return (loss, grad_x, *[grad_w[n] for n in TWIN_WEIGHTS], *[delta_w[n] for n in TWIN_WEIGHTS],
            *[new_m[n] for n in TWIN_WEIGHTS], *[new_v[n] for n in TWIN_WEIGHTS])
```

```python
import functools

import jax
import jax.numpy as jnp
from jax import lax
from jax.experimental import pallas as pl
from jax.experimental.pallas import tpu as pltpu

F32 = jnp.float32
BF16 = jnp.bfloat16

D_MODEL = 1024
DEPTH = 2
GROUP = 512
D_MIX = 3 * GROUP
BLOCK = 128
RMS_EPS = 1e-6
NEG_INF = -1e30
MLA_HEADS = 8
MLA_QK = 96
MLA_NOPE = 64
MLA_ROPE = 32
MLA_V = 64
MLA_Q_LORA = 256
MLA_KV_LORA = 128
ROPE_THETA = 10000.0
SWA_HEADS = 8
SWA_KV = 2
SWA_GROUP = 4
SWA_DIM = 64
IN_COLS = 4256
N_CHIPS = 4

NC = 4352
OFF_CONV, OFF_GATES, OFF_SQ, OFF_MLA, OFF_SKV = 0, 1536, 3072, 3584, 4096
PIECE_WIDTHS = (1536, 1536, 512, 512, 256)

VMEM_LIMIT = 56 * 1024 * 1024
LANES = 128
PACK_COLS = 1024
PACK_ROW_ALIGN = 32

ADAM_LR = 0.001
ADAM_B1 = 0.9
ADAM_B2 = 0.999
ADAM_EPS = 1e-08
ADAM_WD = 0.01
ADAM_STEP = 10

MESH = pl.DeviceIdType.MESH


def _params(sem, vmem=VMEM_LIMIT):
    return pltpu.CompilerParams(dimension_semantics=sem, vmem_limit_bytes=vmem)


def _dot(a, b, dims):
    return lax.dot_general(a.astype(BF16), b.astype(BF16), (dims, ((), ())),
                           preferred_element_type=F32)


def _mm(a, b):
    return _dot(a, b, ((1,), (0,)))


def _mm_nt(a, b):
    return _dot(a, b, ((1,), (1,)))


def _mm_tn(a, b):
    return _dot(a, b, ((0,), (0,)))


@jax.custom_vjp
def _mmd(a, b):
    return _mm(a, b)


def _mmd_fwd(a, b):
    return _mm(a, b), (a, b)


def _mmd_bwd(res, g):
    a, b = res
    return _mm_nt(g, b), _mm_tn(a, g)


_mmd.defvjp(_mmd_fwd, _mmd_bwd)


@jax.custom_vjp
def _mmd_nt(a, b):
    return _mm_nt(a, b)


def _mmd_nt_fwd(a, b):
    return _mm_nt(a, b), (a, b)


def _mmd_nt_bwd(res, g):
    a, b = res
    return _mm(g, b), _mm_tn(g, a)


_mmd_nt.defvjp(_mmd_nt_fwd, _mmd_nt_bwd)


def _rms(x, g, n=None):
    n = x.shape[-1] if n is None else n
    ms = jnp.sum(x * x, axis=-1, keepdims=True) * (1.0 / n)
    return x * lax.rsqrt(ms + RMS_EPS) * g


def _sigmoid(x):
    return 1.0 / (1.0 + jnp.exp(-x))


@jax.custom_vjp
def _rope(t, c, s1, s2):
    return t * c + pltpu.roll(t, LANES - 16, 1) * s1 + pltpu.roll(t, 16, 1) * s2


def _rope_fwd(t, c, s1, s2):
    return _rope(t, c, s1, s2), (c, s1, s2)


def _rope_bwd(res, g):
    c, s1, s2 = res
    dt = g * c + pltpu.roll(g * s1, 16, 1) + pltpu.roll(g * s2, LANES - 16, 1)
    return dt, jnp.zeros_like(c), jnp.zeros_like(s1), jnp.zeros_like(s2)


_rope.defvjp(_rope_fwd, _rope_bwd)


def _in_proj_fwd(x, g, w):
    s = x.shape[0]
    tm = min(256, s)

    def body(x_ref, g_ref, w_ref, proj_ref, hb_ref):
        hb = _rms(x_ref[...], g_ref[...]).astype(BF16)
        hb_ref[...] = hb
        proj_ref[...] = jnp.dot(hb, w_ref[...], preferred_element_type=F32)

    return pl.pallas_call(
        body, name="in_proj_fwd", grid=(s // tm,),
        in_specs=[pl.BlockSpec((tm, D_MODEL), lambda i: (i, 0)),
                  pl.BlockSpec((1, D_MODEL), lambda i: (0, 0)),
                  pl.BlockSpec((D_MODEL, NC), lambda i: (0, 0))],
        out_specs=[pl.BlockSpec((tm, NC), lambda i: (i, 0)),
                   pl.BlockSpec((tm, D_MODEL), lambda i: (i, 0))],
        out_shape=[jax.ShapeDtypeStruct((s, NC), F32), jax.ShapeDtypeStruct((s, D_MODEL), BF16)],
        compiler_params=_params(("parallel",)),
    )(x, g, w)


def _in_proj_bwd(pieces, x, g, w, dres):
    s = x.shape[0]
    tm = min(256, s)
    n_p = len(pieces)

    def body(*refs):
        p_refs = refs[:n_p]
        x_ref, g_ref, w_ref, dres_ref, dx_ref, dg_ref = refs[n_p:]
        dh = None
        off = 0
        for r in p_refs:
            width = r.shape[1]
            t = _mm_nt(r[...], w_ref[:, off:off + width])
            dh = t if dh is None else dh + t
            off += width
        _, vjp = jax.vjp(_rms, x_ref[...], g_ref[...])
        dx, dg = vjp(dh)
        dx_ref[...] = dx + dres_ref[...]

        @pl.when(pl.program_id(0) == 0)
        def _():
            dg_ref[...] = jnp.zeros_like(dg_ref)

        dg_ref[...] += dg

    in_specs = [pl.BlockSpec((tm, p.shape[1]), lambda i: (i, 0)) for p in pieces]
    in_specs += [pl.BlockSpec((tm, D_MODEL), lambda i: (i, 0)),
                 pl.BlockSpec((1, D_MODEL), lambda i: (0, 0)),
                 pl.BlockSpec((D_MODEL, NC), lambda i: (0, 0)),
                 pl.BlockSpec((tm, D_MODEL), lambda i: (i, 0))]
    return pl.pallas_call(
        body, name="in_proj_bwd", grid=(s // tm,),
        in_specs=in_specs,
        out_specs=[pl.BlockSpec((tm, D_MODEL), lambda i: (i, 0)),
                   pl.BlockSpec((1, D_MODEL), lambda i: (0, 0))],
        out_shape=[jax.ShapeDtypeStruct((s, D_MODEL), F32), jax.ShapeDtypeStruct((1, D_MODEL), F32)],
        compiler_params=_params(("arbitrary",)),
    )(*pieces, x, g, w, dres)


def _matmul_tn(a, b, name):
    s, m = a.shape
    n = b.shape[1]
    tk = min(512, s)
    tn = min(512, n)

    def body(a_ref, b_ref, o_ref):
        @pl.when(pl.program_id(1) == 0)
        def _():
            o_ref[...] = jnp.zeros_like(o_ref)

        o_ref[...] += _mm_tn(a_ref[...], b_ref[...])

    return pl.pallas_call(
        body, name=name, grid=(n // tn, s // tk),
        in_specs=[pl.BlockSpec((tk, m), lambda j, k: (k, 0)),
                  pl.BlockSpec((tk, tn), lambda j, k: (k, j))],
        out_specs=pl.BlockSpec((m, tn), lambda j, k: (0, j)),
        out_shape=jax.ShapeDtypeStruct((m, n), F32),
        compiler_params=_params(("parallel", "arbitrary")),
    )(a, b)


def _prep_fn(q_lat, kv_lat, kr, qan, kvan, qn, kn, wq, wk, wv, c, s1, s2, mm):
    rq = _rms(q_lat, qan)
    rkv = _rms(kv_lat, kvan)
    qs, ks = [], []
    for h in range(MLA_HEADS):
        qs.append(_rope(_rms(mm(rq, wq[h]), qn, MLA_QK), c, s1, s2))
        ks.append(_rope(_rms(mm(rkv, wk[h]) + kr, kn, MLA_QK), c, s1, s2))
    return tuple(qs), tuple(ks), mm(rkv, wv)


def _prep_weights(wq_ref, wkv_ref):
    wq = tuple(wq_ref[:, LANES * h:LANES * (h + 1)].astype(F32) for h in range(MLA_HEADS))
    wk = tuple(wkv_ref[:, LANES * h:LANES * (h + 1)].astype(F32) for h in range(MLA_HEADS))
    wv = wkv_ref[:, LANES * MLA_HEADS:].astype(F32)
    return wq, wk, wv


def _prep_in_specs(tm):
    const = lambda shape: pl.BlockSpec(shape, lambda i: (0, 0))
    return [pl.BlockSpec((tm, 512), lambda i: (i, OFF_MLA // 512)),
            const((1, MLA_Q_LORA)), const((1, MLA_KV_LORA)), const((1, LANES)), const((1, LANES)),
            const((MLA_Q_LORA, 1024)), const((MLA_KV_LORA, 1536)),
            pl.BlockSpec((tm, LANES), lambda i: (i, 0)),
            pl.BlockSpec((tm, LANES), lambda i: (i, 0)),
            pl.BlockSpec((tm, LANES), lambda i: (i, 0))]


def _mla_prep_fwd(proj, qan, kvan, qn, kn, wq, wkv, rope):
    s = proj.shape[0]
    tm = min(512, s)

    def body(blk_ref, qan_ref, kvan_ref, qn_ref, kn_ref, wq_ref, wkv_ref, c_ref, s1_ref, s2_ref,
             q_ref, k_ref, v_ref):
        wq_h, wk_h, wv = _prep_weights(wq_ref, wkv_ref)
        qs, ks, v = _prep_fn(blk_ref[:, 0:256], blk_ref[:, 256:384], blk_ref[:, 384:512],
                             qan_ref[...], kvan_ref[...], qn_ref[...], kn_ref[...],
                             wq_h, wk_h, wv, c_ref[...], s1_ref[...], s2_ref[...], _mm)
        for h in range(MLA_HEADS):
            q_ref[:, LANES * h:LANES * (h + 1)] = qs[h].astype(BF16)
            k_ref[:, LANES * h:LANES * (h + 1)] = ks[h].astype(BF16)
        v_ref[...] = v.astype(BF16)

    return pl.pallas_call(
        body, name="mla_prep_fwd", grid=(s // tm,),
        in_specs=_prep_in_specs(tm),
        out_specs=[pl.BlockSpec((tm, 1024), lambda i: (i, 0)),
                   pl.BlockSpec((tm, 1024), lambda i: (i, 0)),
                   pl.BlockSpec((tm, 512), lambda i: (i, 0))],
        out_shape=[jax.ShapeDtypeStruct((s, 1024), BF16), jax.ShapeDtypeStruct((s, 1024), BF16),
                   jax.ShapeDtypeStruct((s, 512), BF16)],
        compiler_params=_params(("parallel",)),
    )(proj, qan, kvan, qn, kn, wq, wkv, *rope)


def _mla_prep_bwd(proj, qan, kvan, qn, kn, wq, wkv, rope, dq, dk, dv):
    s = proj.shape[0]
    tm = min(256, s)

    def body(blk_ref, qan_ref, kvan_ref, qn_ref, kn_ref, wq_ref, wkv_ref, c_ref, s1_ref, s2_ref,
             dq_ref, dk_ref, dv_ref,
             dblk_ref, dqan_ref, dkvan_ref, dqn_ref, dkn_ref, dwq_ref, dwkv_ref):
        wq_h, wk_h, wv = _prep_weights(wq_ref, wkv_ref)
        c, s1, s2 = c_ref[...], s1_ref[...], s2_ref[...]

        def fn(q_lat, kv_lat, kr, qan_, kvan_, qn_, kn_, wq_, wk_, wv_):
            return _prep_fn(q_lat, kv_lat, kr, qan_, kvan_, qn_, kn_, wq_, wk_, wv_, c, s1, s2, _mmd)

        _, vjp = jax.vjp(fn, blk_ref[:, 0:256], blk_ref[:, 256:384], blk_ref[:, 384:512],
                         qan_ref[...], kvan_ref[...], qn_ref[...], kn_ref[...], wq_h, wk_h, wv)
        cts = (tuple(dq_ref[:, LANES * h:LANES * (h + 1)] for h in range(MLA_HEADS)),
               tuple(dk_ref[:, LANES * h:LANES * (h + 1)] for h in range(MLA_HEADS)),
               dv_ref[...])
        dq_lat, dkv_lat, dkr, dqan, dkvan, dqn, dkn, dwq_h, dwk_h, dwv = vjp(cts)
        dblk_ref[:, 0:256] = dq_lat
        dblk_ref[:, 256:384] = dkv_lat
        dblk_ref[:, 384:512] = dkr

        @pl.when(pl.program_id(0) == 0)
        def _():
            for r in (dqan_ref, dkvan_ref, dqn_ref, dkn_ref, dwq_ref, dwkv_ref):
                r[...] = jnp.zeros_like(r)

        dqan_ref[...] += dqan
        dkvan_ref[...] += dkvan
        dqn_ref[...] += dqn
        dkn_ref[...] += dkn
        for h in range(MLA_HEADS):
            dwq_ref[:, LANES * h:LANES * (h + 1)] += dwq_h[h]
            dwkv_ref[:, LANES * h:LANES * (h + 1)] += dwk_h[h]
        dwkv_ref[:, LANES * MLA_HEADS:] += dwv

    const = lambda shape: pl.BlockSpec(shape, lambda i: (0, 0))
    row = lambda width: pl.BlockSpec((tm, width), lambda i: (i, 0))
    shapes = [(s, 512), (1, MLA_Q_LORA), (1, MLA_KV_LORA), (1, LANES), (1, LANES),
              (MLA_Q_LORA, 1024), (MLA_KV_LORA, 1536)]
    return pl.pallas_call(
        body, name="mla_prep_bwd", grid=(s // tm,),
        in_specs=_prep_in_specs(tm) + [row(1024), row(1024), row(512)],
        out_specs=[row(512)] + [const(sh) for sh in shapes[1:]],
        out_shape=[jax.ShapeDtypeStruct(sh, F32) for sh in shapes],
        compiler_params=_params(("arbitrary",)),
    )(proj, qan, kvan, qn, kn, wq, wkv, *rope, dq, dk, dv)


MLA_SCALE = MLA_QK ** -0.5


def _mla_attn_fwd(q, k, v):
    s = q.shape[0]
    t = min(256, s)
    nq = s // t

    def body(q_ref, k_ref, v_ref, o_ref, lse_ref):
        i = pl.program_id(1)
        row = lax.broadcasted_iota(jnp.int32, (t, t), 0)
        col = lax.broadcasted_iota(jnp.int32, (t, t), 1)
        causal = col <= row
        for hh in range(2):
            qh = q_ref[:, LANES * hh:LANES * (hh + 1)]

            def step(j, carry, masked, hh=hh, qh=qh):
                m, l, acc = carry
                r0 = pl.multiple_of(j * t, t)
                kc = k_ref[pl.ds(r0, t), LANES * hh:LANES * (hh + 1)]
                vc = v_ref[pl.ds(r0, t), MLA_V * hh:MLA_V * (hh + 1)]
                sc = lax.dot_general(qh, kc, (((1,), (1,)), ((), ())),
                                     preferred_element_type=F32) * MLA_SCALE
                if masked:
                    sc = jnp.where(causal, sc, NEG_INF)
                m_new = jnp.maximum(m, jnp.max(sc, axis=-1, keepdims=True))
                p = jnp.exp(sc - m_new)
                alpha = jnp.exp(m - m_new)
                l = alpha * l + jnp.sum(p, axis=-1, keepdims=True)
                acc = alpha * acc + jnp.dot(p.astype(BF16), vc, preferred_element_type=F32)
                return m_new, l, acc

            init = (jnp.full((t, 1), NEG_INF, F32), jnp.zeros((t, 1), F32),
                    jnp.zeros((t, MLA_V), F32))
            carry = lax.fori_loop(0, i, lambda j, cr: step(j, cr, False), init)
            m, l, acc = step(i, carry, True)
            o_ref[:, MLA_V * hh:MLA_V * (hh + 1)] = acc / l
            lse_ref[0, :, hh:hh + 1] = m + jnp.log(l)

    return pl.pallas_call(
        body, name="mla_attn_fwd", grid=(MLA_HEADS // 2, nq),
        in_specs=[pl.BlockSpec((t, 256), lambda p, i: (i, p)),
                  pl.BlockSpec((s, 256), lambda p, i: (0, p)),
                  pl.BlockSpec((s, 128), lambda p, i: (0, p))],
        out_specs=[pl.BlockSpec((t, 128), lambda p, i: (i, p)),
                   pl.BlockSpec((1, t, 2), lambda p, i: (p, i, 0))],
        out_shape=[jax.ShapeDtypeStruct((s, 512), F32),
                   jax.ShapeDtypeStruct((MLA_HEADS // 2, s, 2), F32)],
        compiler_params=_params(("parallel", "arbitrary")),
    )(q, k, v)


def _mla_attn_bwd(q, k, v, do, lse_rows, delta_rows):
    s = q.shape[0]
    t = min(256, s)
    nq = s // t

    def body(q_ref, k_ref, v_ref, do_ref, lse_ref, dl_ref, dq_ref, dk_ref, dv_ref):
        j = pl.program_id(1)

        @pl.when(j == 0)
        def _():
            dq_ref[...] = jnp.zeros_like(dq_ref)

        row = lax.broadcasted_iota(jnp.int32, (t, t), 0)
        col = lax.broadcasted_iota(jnp.int32, (t, t), 1)
        causal_t = row <= col
        for hh in range(2):
            kh = k_ref[:, LANES * hh:LANES * (hh + 1)]
            vh = v_ref[:, MLA_V * hh:MLA_V * (hh + 1)]

            def step(i, carry, masked, hh=hh, kh=kh, vh=vh):
                dk, dv = carry
                r0 = pl.multiple_of(i * t, t)
                qh = q_ref[pl.ds(r0, t), LANES * hh:LANES * (hh + 1)]
                doh = do_ref[pl.ds(r0, t), MLA_V * hh:MLA_V * (hh + 1)]
                lse = lse_ref[0, hh:hh + 1, pl.ds(r0, t)]
                dl = dl_ref[0, hh:hh + 1, pl.ds(r0, t)]
                sc_t = lax.dot_general(kh, qh, (((1,), (1,)), ((), ())),
                                       preferred_element_type=F32) * MLA_SCALE
                if masked:
                    sc_t = jnp.where(causal_t, sc_t, NEG_INF)
                p_t = jnp.exp(sc_t - lse)
                dv = dv + jnp.dot(p_t.astype(BF16), doh, preferred_element_type=F32)
                dp_t = lax.dot_general(vh, doh, (((1,), (1,)), ((), ())),
                                       preferred_element_type=F32)
                ds_t = (p_t * (dp_t - dl) * MLA_SCALE).astype(BF16)
                dk = dk + jnp.dot(ds_t, qh, preferred_element_type=F32)
                dq_ref[pl.ds(r0, t), LANES * hh:LANES * (hh + 1)] += lax.dot_general(
                    ds_t, kh, (((0,), (0,)), ((), ())), preferred_element_type=F32)
                return dk, dv

            carry = step(j, (jnp.zeros((t, LANES), F32), jnp.zeros((t, MLA_V), F32)), True)
            dk, dv = lax.fori_loop(j + 1, nq, lambda i, cr: step(i, cr, False), carry)
            dk_ref[:, LANES * hh:LANES * (hh + 1)] = dk
            dv_ref[:, MLA_V * hh:MLA_V * (hh + 1)] = dv

    return pl.pallas_call(
        body, name="mla_attn_bwd", grid=(MLA_HEADS // 2, nq),
        in_specs=[pl.BlockSpec((s, 256), lambda p, j: (0, p)),
                  pl.BlockSpec((t, 256), lambda p, j: (j, p)),
                  pl.BlockSpec((t, 128), lambda p, j: (j, p)),
                  pl.BlockSpec((s, 128), lambda p, j: (0, p)),
                  pl.BlockSpec((1, 2, s), lambda p, j: (p, 0, 0)),
                  pl.BlockSpec((1, 2, s), lambda p, j: (p, 0, 0))],
        out_specs=[pl.BlockSpec((s, 256), lambda p, j: (0, p)),
                   pl.BlockSpec((t, 256), lambda p, j: (j, p)),
                   pl.BlockSpec((t, 128), lambda p, j: (j, p))],
        out_shape=[jax.ShapeDtypeStruct((s, 1024), F32), jax.ShapeDtypeStruct((s, 1024), F32),
                   jax.ShapeDtypeStruct((s, 512), F32)],
        compiler_params=_params(("parallel", "arbitrary")),
    )(q, k, v, do, lse_rows, delta_rows)


SWA_SCALE = SWA_DIM ** -0.5
SWA_ROWS = SWA_GROUP * BLOCK


def _swa_consts(first):
    r = lax.broadcasted_iota(jnp.int32, (SWA_ROWS, 2 * BLOCK), 0)
    kidx = lax.broadcasted_iota(jnp.int32, (SWA_ROWS, 2 * BLOCK), 1)
    dist = BLOCK + (r % BLOCK) - kidx
    valid = (dist >= 0) & (dist < BLOCK) & ((kidx >= BLOCK) | jnp.logical_not(first))
    grp = r // BLOCK
    distf = dist.astype(F32)
    rs = lax.broadcasted_iota(jnp.int32, (SWA_ROWS, SWA_HEADS), 0) // BLOCK
    hs = lax.broadcasted_iota(jnp.int32, (SWA_ROWS, SWA_HEADS), 1)
    biases, sels = [], []
    for j in range(SWA_KV):
        slope = jnp.zeros((SWA_ROWS, 2 * BLOCK), F32)
        for g in range(SWA_GROUP):
            slope = jnp.where(grp == g, 2.0 ** -(SWA_GROUP * j + g + 1), slope)
        biases.append(slope * distf)
        sels.append((hs == rs + SWA_GROUP * j).astype(F32))
    return valid, biases, sels


def _swa_block(qh, kprev, kcur, vprev, vcur, qn, kn, sinks, consts, mm_nt, mm):
    valid, biases, sels = consts
    outs = []
    for j in range(SWA_KV):
        kb = _rms(jnp.concatenate([kprev[j], kcur[j]], axis=0), kn)
        vb = jnp.concatenate([vprev[j], vcur[j]], axis=0)
        qs = jnp.concatenate([_rms(qh[SWA_GROUP * j + g], qn) for g in range(SWA_GROUP)], axis=0)
        sc = mm_nt(qs, kb) * SWA_SCALE - biases[j]
        sc = jnp.where(valid, sc, NEG_INF)
        sink = jnp.sum(sels[j] * sinks, axis=-1, keepdims=True)
        m = lax.stop_gradient(jnp.maximum(jnp.max(sc, axis=-1, keepdims=True), sink))
        e = jnp.exp(sc - m)
        den = jnp.sum(e, axis=-1, keepdims=True) + jnp.exp(sink - m)
        outs.append(mm(e / den, vb))
    return tuple(outs)


def _swa_load(sq_ref, skv_ref, halo_ref, b):
    r = slice(BLOCK * b, BLOCK * (b + 1))
    qh = tuple(sq_ref[r, SWA_DIM * h:SWA_DIM * (h + 1)] for h in range(SWA_HEADS))
    kcur = tuple(skv_ref[r, SWA_DIM * j:SWA_DIM * (j + 1)] for j in range(SWA_KV))
    vcur = tuple(skv_ref[r, 128 + SWA_DIM * j:128 + SWA_DIM * (j + 1)] for j in range(SWA_KV))
    if b == 0:
        src, rp = halo_ref, slice(0, BLOCK)
    else:
        src, rp = skv_ref, slice(BLOCK * (b - 1), BLOCK * b)
    kprev = tuple(src[rp, SWA_DIM * j:SWA_DIM * (j + 1)] for j in range(SWA_KV))
    vprev = tuple(src[rp, 128 + SWA_DIM * j:128 + SWA_DIM * (j + 1)] for j in range(SWA_KV))
    return qh, kprev, kcur, vprev, vcur


def _swa_fwd(proj, qn, kn, sinks):
    s = proj.shape[0]
    ts = min(512, s)
    nb = ts // BLOCK

    def body(sq_ref, skv_ref, halo_ref, qn_ref, kn_ref, sk_ref, o_ref):
        i = pl.program_id(0)
        for b in range(nb):
            first = (i == 0) if b == 0 else False
            consts = _swa_consts(jnp.asarray(first))
            outs = _swa_block(*_swa_load(sq_ref, skv_ref, halo_ref, b), qn_ref[...], kn_ref[...],
                              sk_ref[...], consts, _mm_nt, _mm)
            for j in range(SWA_KV):
                for g in range(SWA_GROUP):
                    h = SWA_GROUP * j + g
                    o_ref[BLOCK * b:BLOCK * (b + 1), SWA_DIM * h:SWA_DIM * (h + 1)] = (
                        outs[j][BLOCK * g:BLOCK * (g + 1)])

    const = lambda shape: pl.BlockSpec(shape, lambda i: (0, 0))
    return pl.pallas_call(
        body, name="swa_fwd", grid=(s // ts,),
        in_specs=[pl.BlockSpec((ts, 512), lambda i: (i, OFF_SQ // 512)),
                  pl.BlockSpec((ts, 256), lambda i: (i, OFF_SKV // 256)),
                  pl.BlockSpec((BLOCK, 256), lambda i: (jnp.maximum(i * nb - 1, 0), OFF_SKV // 256)),
                  const((1, SWA_DIM)), const((1, SWA_DIM)), const((1, SWA_HEADS))],
        out_specs=pl.BlockSpec((ts, 512), lambda i: (i, 0)),
        out_shape=jax.ShapeDtypeStruct((s, 512), F32),
        compiler_params=_params(("parallel",)),
    )(proj, proj, proj, qn, kn, sinks)


def _swa_bwd(proj, qn, kn, sinks, do):
    s = proj.shape[0]
    ts = min(512, s)
    nb = ts // BLOCK
    nt = s // ts

    def body(sq_ref, skv_ref, halo_ref, qn_ref, kn_ref, sk_ref, do_ref,
             dsq_ref, dskv_ref, dqn_ref, dkn_ref, dsk_ref, carry_ref):
        step = pl.program_id(0)
        i = nt - 1 - step

        @pl.when(step == 0)
        def _():
            carry_ref[...] = jnp.zeros_like(carry_ref)
            dqn_ref[...] = jnp.zeros_like(dqn_ref)
            dkn_ref[...] = jnp.zeros_like(dkn_ref)
            dsk_ref[...] = jnp.zeros_like(dsk_ref)

        dk_rows = [[None] * SWA_KV for _ in range(nb)]
        dv_rows = [[None] * SWA_KV for _ in range(nb)]
        halo_k, halo_v = None, None
        dqn = dkn = dsk = None

        def add(a, b_):
            return b_ if a is None else a + b_

        for b in range(nb):
            first = (i == 0) if b == 0 else False
            consts = _swa_consts(jnp.asarray(first))
            qh, kprev, kcur, vprev, vcur = _swa_load(sq_ref, skv_ref, halo_ref, b)

            def fn(qh_, kprev_, kcur_, vprev_, vcur_, qn_, kn_, sk_, consts=consts):
                return _swa_block(qh_, kprev_, kcur_, vprev_, vcur_, qn_, kn_, sk_, consts,
                                  _mmd_nt, _mmd)

            _, vjp = jax.vjp(fn, qh, kprev, kcur, vprev, vcur, qn_ref[...], kn_ref[...], sk_ref[...])
            r = slice(BLOCK * b, BLOCK * (b + 1))
            cts = tuple(
                jnp.concatenate([do_ref[r, SWA_DIM * (SWA_GROUP * j + g):SWA_DIM * (SWA_GROUP * j + g + 1)]
                                 for g in range(SWA_GROUP)], axis=0)
                for j in range(SWA_KV))
            dqh, dkprev, dkcur, dvprev, dvcur, dqn_b, dkn_b, dsk_b = vjp(cts)
            for h in range(SWA_HEADS):
                dsq_ref[r, SWA_DIM * h:SWA_DIM * (h + 1)] = dqh[h]
            for j in range(SWA_KV):
                dk_rows[b][j] = add(dk_rows[b][j], dkcur[j])
                dv_rows[b][j] = add(dv_rows[b][j], dvcur[j])
                if b > 0:
                    dk_rows[b - 1][j] = dk_rows[b - 1][j] + dkprev[j]
                    dv_rows[b - 1][j] = dv_rows[b - 1][j] + dvprev[j]
            if b == 0:
                halo_k, halo_v = dkprev, dvprev
            dqn, dkn, dsk = add(dqn, dqn_b), add(dkn, dkn_b), add(dsk, dsk_b)

        for b in range(nb):
            r = slice(BLOCK * b, BLOCK * (b + 1))
            for j in range(SWA_KV):
                dkv = dk_rows[b][j], dv_rows[b][j]
                for part in range(2):
                    c0 = 128 * part + SWA_DIM * j
                    val = dkv[part]
                    if b == nb - 1:
                        val = val + carry_ref[:, c0:c0 + SWA_DIM]
                    dskv_ref[r, c0:c0 + SWA_DIM] = val
        for j in range(SWA_KV):
            carry_ref[:, SWA_DIM * j:SWA_DIM * (j + 1)] = halo_k[j]
            carry_ref[:, 128 + SWA_DIM * j:128 + SWA_DIM * (j + 1)] = halo_v[j]
        dqn_ref[...] += dqn
        dkn_ref[...] += dkn
        dsk_ref[...] += dsk

    const = lambda shape: pl.BlockSpec(shape, lambda st: (0, 0))
    return pl.pallas_call(
        body, name="swa_bwd", grid=(nt,),
        in_specs=[pl.BlockSpec((ts, 512), lambda st: (nt - 1 - st, OFF_SQ // 512)),
                  pl.BlockSpec((ts, 256), lambda st: (nt - 1 - st, OFF_SKV // 256)),
                  pl.BlockSpec((BLOCK, 256),
                               lambda st: (jnp.maximum((nt - 1 - st) * nb - 1, 0), OFF_SKV // 256)),
                  const((1, SWA_DIM)), const((1, SWA_DIM)), const((1, SWA_HEADS)),
                  pl.BlockSpec((ts, 512), lambda st: (nt - 1 - st, 0))],
        out_specs=[pl.BlockSpec((ts, 512), lambda st: (nt - 1 - st, 0)),
                   pl.BlockSpec((ts, 256), lambda st: (nt - 1 - st, 0)),
                   const((1, SWA_DIM)), const((1, SWA_DIM)), const((1, SWA_HEADS))],
        out_shape=[jax.ShapeDtypeStruct((s, 512), F32), jax.ShapeDtypeStruct((s, 256), F32),
                   jax.ShapeDtypeStruct((1, SWA_DIM), F32), jax.ShapeDtypeStruct((1, SWA_DIM), F32),
                   jax.ShapeDtypeStruct((1, SWA_HEADS), F32)],
        scratch_shapes=[pltpu.VMEM((BLOCK, 256), F32)],
        compiler_params=_params(("arbitrary",)),
    )(proj, proj, proj, qn, kn, sinks, do)


HALO = 8


def _shift_down(u, halo, k):
    tm = u.shape[0]
    rid = lax.broadcasted_iota(jnp.int32, u.shape, 0)
    out = pltpu.roll(u, k, 0)
    for r in range(k):
        out = jnp.where(rid == r, halo[HALO - k + r:HALO - k + r + 1, :], out)
    return out


def _shift_up(u, halo, k):
    tm = u.shape[0]
    rid = lax.broadcasted_iota(jnp.int32, u.shape, 0)
    out = pltpu.roll(u, tm - k, 0)
    for r in range(k):
        out = jnp.where(rid == tm - k + r, halo[r:r + 1, :], out)
    return out


def _conv_fwd_vals(conv_ref, convp_ref, cw_ref, is_first):
    c_h, c_b, c_c = conv_ref[:, 0:512], conv_ref[:, 512:1024], conv_ref[:, 1024:1536]
    u = c_c * c_h
    up = jnp.where(is_first, 0.0, convp_ref[:, 1024:1536] * convp_ref[:, 0:512])
    u1 = _shift_down(u, up, 1)
    u2 = _shift_down(u, up, 2)
    yc = cw_ref[0:1, :] * u2 + cw_ref[1:2, :] * u1 + cw_ref[2:3, :] * u
    return c_h, c_b, c_c, u, u1, u2, yc


def _out_fwd(proj, o_mla, o_swa, x, w_out, cw):
    s = proj.shape[0]
    tm = min(512, s)

    def body(conv_ref, convp_ref, gates_ref, om_ref, os_ref, x_ref, w_ref, cw_ref, y_ref, z_ref):
        i = pl.program_id(0)
        _, c_b, _, _, _, _, yc = _conv_fwd_vals(conv_ref, convp_ref, cw_ref, i == 0)
        mix = (om_ref[...], c_b * yc, os_ref[...])
        for n in range(3):
            g = gates_ref[:, GROUP * n:GROUP * (n + 1)]
            z_ref[:, GROUP * n:GROUP * (n + 1)] = (mix[n] * (g * _sigmoid(g))).astype(BF16)
        y_ref[...] = x_ref[...] + jnp.dot(z_ref[...], w_ref[...], preferred_element_type=F32)

    row = lambda width: pl.BlockSpec((tm, width), lambda i: (i, 0))
    return pl.pallas_call(
        body, name="out_fwd", grid=(s // tm,),
        in_specs=[pl.BlockSpec((tm, 1536), lambda i: (i, 0)),
                  pl.BlockSpec((HALO, 1536), lambda i: (jnp.maximum(i * (tm // HALO) - 1, 0), 0)),
                  pl.BlockSpec((tm, 1536), lambda i: (i, 1)),
                  row(512), row(512), row(D_MODEL),
                  pl.BlockSpec((D_MIX, D_MODEL), lambda i: (0, 0)),
                  pl.BlockSpec((HALO, 512), lambda i: (0, 0))],
        out_specs=[row(D_MODEL), row(D_MIX)],
        out_shape=[jax.ShapeDtypeStruct((s, D_MODEL), F32), jax.ShapeDtypeStruct((s, D_MIX), BF16)],
        compiler_params=_params(("parallel",)),
    )(proj, proj, proj, o_mla, o_swa, x, w_out, cw)


def _out_bwd(dy, proj, o_mla, o_swa, w_out, cw):
    s = proj.shape[0]
    tm = min(512, s)
    nt = s // tm
    hb = tm // HALO

    def body(dy_ref, dyn_ref, conv_ref, convp_ref, convn_ref, gates_ref, gatesn_ref, om_ref, os_ref,
             w_ref, cw_ref,
             dconv_ref, dgates_ref, dom_ref, delta_ref, dos_ref, dcw_ref):
        i = pl.program_id(0)
        dz = _mm_nt(dy_ref[...], w_ref[...])

        def gate(n):
            g = gates_ref[:, GROUP * n:GROUP * (n + 1)]
            sg = _sigmoid(g)
            return g * sg, sg * (1.0 + g * (1.0 - sg))

        for n, o_ref, do_ref in ((0, om_ref, dom_ref), (2, os_ref, dos_ref)):
            silu, dsilu = gate(n)
            dzn = dz[:, GROUP * n:GROUP * (n + 1)]
            o = o_ref[...]
            do = dzn * silu
            do_ref[...] = do.astype(do_ref.dtype)
            dgates_ref[:, GROUP * n:GROUP * (n + 1)] = dzn * o * dsilu
            if n == 0:
                t = do * o
                for h in range(MLA_HEADS):
                    delta_ref[:, h:h + 1] = jnp.sum(t[:, MLA_V * h:MLA_V * (h + 1)], axis=-1,
                                                    keepdims=True)

        c_h, c_b, c_c, u, u1, u2, yc = _conv_fwd_vals(conv_ref, convp_ref, cw_ref, i == 0)
        silu, dsilu = gate(1)
        dzc = dz[:, GROUP:2 * GROUP]
        dgates_ref[:, GROUP:2 * GROUP] = dzc * (c_b * yc) * dsilu
        dycr = dzc * silu
        dyc = dycr * c_b
        gn = gatesn_ref[:, GROUP:2 * GROUP]
        dzc_n = _mm_nt(dyn_ref[...], w_ref[GROUP:2 * GROUP, :])
        dyc_n = jnp.where(i == nt - 1, 0.0, dzc_n * (gn * _sigmoid(gn)) * convn_ref[:, 512:1024])
        d1 = _shift_up(dyc, dyc_n, 1)
        d2 = _shift_up(dyc, dyc_n, 2)
        du = cw_ref[2:3, :] * dyc + cw_ref[1:2, :] * d1 + cw_ref[0:1, :] * d2
        dconv_ref[:, 0:512] = du * c_c
        dconv_ref[:, 512:1024] = dycr * yc
        dconv_ref[:, 1024:1536] = du * c_h

        @pl.when(i == 0)
        def _():
            dcw_ref[...] = jnp.zeros_like(dcw_ref)

        for k, uk in enumerate((u2, u1, u)):
            dcw_ref[k:k + 1, :] += jnp.sum(dyc * uk, axis=0, keepdims=True)

    row = lambda width: pl.BlockSpec((tm, width), lambda i: (i, 0))
    prev = lambda i: jnp.maximum(i * hb - 1, 0)
    nxt = lambda i: jnp.minimum((i + 1) * hb, s // HALO - 1)
    return pl.pallas_call(
        body, name="out_bwd", grid=(nt,),
        in_specs=[row(D_MODEL),
                  pl.BlockSpec((HALO, D_MODEL), lambda i: (nxt(i), 0)),
                  pl.BlockSpec((tm, 1536), lambda i: (i, 0)),
                  pl.BlockSpec((HALO, 1536), lambda i: (prev(i), 0)),
                  pl.BlockSpec((HALO, 1536), lambda i: (nxt(i), 0)),
                  pl.BlockSpec((tm, 1536), lambda i: (i, 1)),
                  pl.BlockSpec((HALO, 1536), lambda i: (nxt(i), 1)),
                  row(512), row(512),
                  pl.BlockSpec((D_MIX, D_MODEL), lambda i: (0, 0)),
                  pl.BlockSpec((HALO, 512), lambda i: (0, 0))],
        out_specs=[row(1536), row(1536), row(512), row(MLA_HEADS), row(512),
                   pl.BlockSpec((HALO, 512), lambda i: (0, 0))],
        out_shape=[jax.ShapeDtypeStruct((s, 1536), F32), jax.ShapeDtypeStruct((s, 1536), F32),
                   jax.ShapeDtypeStruct((s, 512), BF16), jax.ShapeDtypeStruct((s, MLA_HEADS), F32),
                   jax.ShapeDtypeStruct((s, 512), F32), jax.ShapeDtypeStruct((HALO, 512), F32)],
        compiler_params=_params(("arbitrary",)),
    )(dy, dy, proj, proj, proj, proj, proj, o_mla, o_swa, w_out, cw)


def _loss_head(y, target):
    s, d = y.shape
    tm = min(512, s)
    nt = s // tm

    def body(y_ref, t_ref, dy_ref, loss_ref):
        i = pl.program_id(0)
        err = y_ref[...] - t_ref[...]
        dy_ref[...] = err * (1.0 / d)

        @pl.when(i == 0)
        def _():
            loss_ref[...] = jnp.zeros_like(loss_ref)

        sq = jnp.sum((err * err).reshape(tm // 8, 8, d), axis=0)
        part = sq[:, 0:LANES]
        for c in range(1, d // LANES):
            part = part + sq[:, LANES * c:LANES * (c + 1)]
        loss_ref[...] += part

        @pl.when(i == nt - 1)
        def _():
            loss_ref[...] = jnp.full(loss_ref.shape, (0.5 / d) * jnp.sum(loss_ref[...]), F32)

    return pl.pallas_call(
        body, name="loss_head", grid=(nt,),
        in_specs=[pl.BlockSpec((tm, d), lambda i: (i, 0)), pl.BlockSpec((tm, d), lambda i: (i, 0))],
        out_specs=[pl.BlockSpec((tm, d), lambda i: (i, 0)), pl.BlockSpec((8, LANES), lambda i: (0, 0))],
        out_shape=[jax.ShapeDtypeStruct((s, d), F32), jax.ShapeDtypeStruct((8, LANES), F32)],
        compiler_params=_params(("arbitrary",)),
    )(y, target)


def _adamw(g, w, m, v):
    rows = g.shape[0]
    tr = min(256, rows)
    c1 = 1.0 - ADAM_B1
    c2 = 1.0 - ADAM_B2
    bc1 = 1.0 - ADAM_B1 ** ADAM_STEP
    bc2 = 1.0 - ADAM_B2 ** ADAM_STEP

    def body(g_ref, w_ref, m_ref, v_ref, d_ref, mo_ref, vo_ref):
        gg = g_ref[...]
        m_new = ADAM_B1 * m_ref[...] + c1 * gg
        v_new = ADAM_B2 * v_ref[...] + c2 * (gg * gg)
        m_hat = m_new / bc1
        v_hat = v_new / bc2
        d_ref[...] = -ADAM_LR * (m_hat / (jnp.sqrt(v_hat) + ADAM_EPS) + ADAM_WD * w_ref[...])
        mo_ref[...] = m_new
        vo_ref[...] = v_new

    spec = pl.BlockSpec((tr, PACK_COLS), lambda i: (i, 0))
    return pl.pallas_call(
        body, name="adamw", grid=(rows // tr,),
        in_specs=[spec] * 4, out_specs=[spec] * 3,
        out_shape=[jax.ShapeDtypeStruct(g.shape, F32)] * 3,
        compiler_params=_params(("parallel",)),
    )(g, w, m, v)


HBM_SPEC = pl.BlockSpec(memory_space=pltpu.HBM)


def _place():
    x, y, c = lax.axis_index("x"), lax.axis_index("y"), lax.axis_index("c")
    chips = [(1 - x, y), (x, 1 - y), (1 - x, 1 - y)]
    return x, y, c, chips


def _all_gather(shard):
    rows, cols = shard.shape
    half = rows // 2

    def body(w_ref, a_ref, send_sems, recv_sems, local_sem):
        x, y, c, chips = _place()
        k = 2 * x + y
        sib = (x, y, 1 - c)

        def slab(kk, hc):
            return a_ref.at[kk, pl.ds(hc * half, half), :]

        def copy(n, src, dst, to):
            return pltpu.make_async_remote_copy(src_ref=src, dst_ref=dst, send_sem=send_sems.at[n],
                                                recv_sem=recv_sems.at[n], device_id=to,
                                                device_id_type=MESH)

        mine = pltpu.make_async_copy(w_ref, a_ref.at[k], local_sem)
        mine.start()
        first = [copy(n, w_ref.at[pl.ds(c * half, half), :], slab(k, c), (cx, cy, c))
                 for n, (cx, cy) in enumerate(chips)]
        for cp in first:
            cp.start()
        passed = []
        for n, (cx, cy) in enumerate(chips):
            kk = 2 * cx + cy
            copy(n, slab(kk, c), slab(kk, c), (cx, cy, c)).wait_recv()
            fwd = copy(3 + n, slab(kk, c), slab(kk, c), sib)
            fwd.start()
            passed.append(fwd)
        for n, (cx, cy) in enumerate(chips):
            kk = 2 * cx + cy
            copy(3 + n, slab(kk, 1 - c), slab(kk, 1 - c), sib).wait_recv()
        for cp in first + passed:
            cp.wait_send()
        mine.wait()

    return pl.pallas_call(
        body, name="weights_all_gather",
        in_specs=[HBM_SPEC], out_specs=HBM_SPEC,
        out_shape=jax.ShapeDtypeStruct((N_CHIPS, rows, cols), shard.dtype),
        scratch_shapes=[pltpu.SemaphoreType.DMA((6,)), pltpu.SemaphoreType.DMA((6,)),
                        pltpu.SemaphoreType.DMA],
    )(shard)


def _swap_halves_to_sibling(g):
    n, rows, cols = g.shape
    half = rows // 2

    def body(g_ref, r_ref, send_sem, recv_sem):
        x, y, c, _ = _place()
        cp = pltpu.make_async_remote_copy(
            src_ref=g_ref.at[:, pl.ds((1 - c) * half, half), :], dst_ref=r_ref,
            send_sem=send_sem, recv_sem=recv_sem, device_id=(x, y, 1 - c), device_id_type=MESH)
        cp.start()
        cp.wait()

    return pl.pallas_call(
        body, name="grads_to_sibling",
        in_specs=[HBM_SPEC], out_specs=HBM_SPEC,
        out_shape=jax.ShapeDtypeStruct((n, half, cols), g.dtype),
        scratch_shapes=[pltpu.SemaphoreType.DMA, pltpu.SemaphoreType.DMA],
    )(g)


def _add_sibling(g, r, c_idx):
    n, rows, cols = g.shape
    half = rows // 2
    tr = min(256, half)
    nb = half // tr

    def body(c_ref, g_ref, r_ref, p_ref):
        p_ref[...] = g_ref[...] + r_ref[...]

    return pl.pallas_call(
        body, name="grads_add_sibling",
        grid_spec=pltpu.PrefetchScalarGridSpec(
            num_scalar_prefetch=1, grid=(n, nb),
            in_specs=[pl.BlockSpec((1, tr, cols), lambda j, t, c_ref: (j, c_ref[0] * nb + t, 0)),
                      pl.BlockSpec((1, tr, cols), lambda j, t, c_ref: (j, t, 0))],
            out_specs=pl.BlockSpec((1, tr, cols), lambda j, t, c_ref: (j, t, 0))),
        out_shape=jax.ShapeDtypeStruct((n, half, cols), F32),
        compiler_params=_params(("parallel", "parallel")),
    )(c_idx, g, r)


def _scatter_to_chips(p):
    n, half, cols = p.shape

    def body(p_ref, q_ref, send_sems, recv_sems, local_sem):
        x, y, c, chips = _place()
        k = 2 * x + y
        mine = pltpu.make_async_copy(p_ref.at[k], q_ref.at[k], local_sem)
        mine.start()
        sends = []
        for i, (cx, cy) in enumerate(chips):
            cp = pltpu.make_async_remote_copy(
                src_ref=p_ref.at[2 * cx + cy], dst_ref=q_ref.at[k], send_sem=send_sems.at[i],
                recv_sem=recv_sems.at[i], device_id=(cx, cy, c), device_id_type=MESH)
            cp.start()
            sends.append(cp)
        for i, (cx, cy) in enumerate(chips):
            kk = 2 * cx + cy
            pltpu.make_async_remote_copy(
                src_ref=p_ref.at[kk], dst_ref=q_ref.at[kk], send_sem=send_sems.at[i],
                recv_sem=recv_sems.at[i], device_id=(cx, cy, c), device_id_type=MESH).wait_recv()
        for cp in sends:
            cp.wait_send()
        mine.wait()

    return pl.pallas_call(
        body, name="grads_scatter_to_chips",
        in_specs=[HBM_SPEC], out_specs=HBM_SPEC,
        out_shape=jax.ShapeDtypeStruct((n, half, cols), p.dtype),
        scratch_shapes=[pltpu.SemaphoreType.DMA((3,)), pltpu.SemaphoreType.DMA((3,)),
                        pltpu.SemaphoreType.DMA],
    )(p)


def _sum_chips(q):
    n, half, cols = q.shape
    tr = min(256, half)

    def body(q_ref, o_ref):
        o_ref[...] = ((q_ref[0] + q_ref[1]) + q_ref[2]) + q_ref[3]

    return pl.pallas_call(
        body, name="grads_sum_chips", grid=(half // tr,),
        in_specs=[pl.BlockSpec((n, tr, cols), lambda t: (0, t, 0))],
        out_specs=pl.BlockSpec((tr, cols), lambda t: (t, 0)),
        out_shape=jax.ShapeDtypeStruct((half, cols), F32),
        compiler_params=_params(("parallel",)),
    )(q)


def _join_halves(o_half):
    half, cols = o_half.shape

    def body(h_ref, o_ref, send_sem, recv_sem, local_sem):
        x, y, c, _ = _place()
        dst = o_ref.at[pl.ds(c * half, half), :]
        mine = pltpu.make_async_copy(h_ref, dst, local_sem)
        mine.start()
        cp = pltpu.make_async_remote_copy(src_ref=h_ref, dst_ref=dst, send_sem=send_sem,
                                          recv_sem=recv_sem, device_id=(x, y, 1 - c),
                                          device_id_type=MESH)
        cp.start()
        other = o_ref.at[pl.ds((1 - c) * half, half), :]
        pltpu.make_async_remote_copy(src_ref=h_ref, dst_ref=other, send_sem=send_sem,
                                     recv_sem=recv_sem, device_id=(x, y, 1 - c),
                                     device_id_type=MESH).wait_recv()
        cp.wait_send()
        mine.wait()

    return pl.pallas_call(
        body, name="grads_join_halves",
        in_specs=[HBM_SPEC], out_specs=HBM_SPEC,
        out_shape=jax.ShapeDtypeStruct((2 * half, cols), o_half.dtype),
        scratch_shapes=[pltpu.SemaphoreType.DMA, pltpu.SemaphoreType.DMA, pltpu.SemaphoreType.DMA],
    )(o_half)


def _pack(arrays, dtype):
    flat = jnp.concatenate([a.reshape(-1).astype(dtype) for a in arrays])
    quantum = PACK_COLS * PACK_ROW_ALIGN
    total = -(-flat.shape[0] // quantum) * quantum
    flat = jnp.pad(flat, (0, total - flat.shape[0]))
    return flat.reshape(total // PACK_COLS, PACK_COLS)


def _unpack(buf, shapes):
    flat = buf.reshape(buf.shape[:-2] + (-1,))
    out, off = [], 0
    for sh in shapes:
        size = 1
        for d in sh:
            size *= d
        out.append(flat[..., off:off + size].reshape(buf.shape[:-2] + tuple(sh)))
        off += size
    return out


def _permute_w_in(w):
    z = lambda n: jnp.zeros((w.shape[0], n), w.dtype)
    return jnp.concatenate([
        w[:, 928:1440], w[:, 1440:1952], w[:, 1952:2464],
        w[:, 416:928], w[:, 2464:2976], w[:, 3744:4256],
        w[:, 2976:3488],
        w[:, 0:256], w[:, 256:384], z(64), w[:, 384:416], z(32),
        w[:, 3488:3616], w[:, 3616:3744]], axis=1)


def _unpermute_dw_in(d):
    return jnp.concatenate([
        d[:, 3584:3840], d[:, 3840:3968], d[:, 4032:4064], d[:, 1536:2048],
        d[:, 0:512], d[:, 512:1024], d[:, 1024:1536], d[:, 2048:2560],
        d[:, 3072:3584], d[:, 4096:4224], d[:, 4224:4352], d[:, 2560:3072]], axis=1)


def _rope_tables(s):
    half = MLA_ROPE // 2
    inv_freq = jnp.power(jnp.float32(ROPE_THETA), -jnp.arange(half, dtype=F32) / half)
    ang = jnp.arange(s, dtype=F32)[:, None] * inv_freq[None, :]
    cos, sin = jnp.cos(ang), jnp.sin(ang)
    z = lambda n: jnp.zeros((s, n), F32)
    c = jnp.concatenate([jnp.ones((s, MLA_NOPE), F32), cos, cos, z(32)], axis=1)
    s1 = jnp.concatenate([z(MLA_NOPE), -sin, z(16), z(32)], axis=1)
    s2 = jnp.concatenate([z(MLA_NOPE), z(16), sin, z(32)], axis=1)
    return c, s1, s2


def _pad_lanes(a, n):
    return jnp.pad(a, ((0, 0), (0, n - a.shape[1])))


SHARDED = ("w_in", "w_out", "mla_w_qb", "mla_w_kvb", "conv_w")
REPLICATED = ("norm_g", "mla_q_a_norm", "mla_kv_a_norm", "mla_q_norm", "mla_k_norm",
              "swa_q_norm", "swa_k_norm", "swa_sinks")
WEIGHT_ORDER = ("norm_g", "w_in", "mla_q_a_norm", "mla_w_qb", "mla_kv_a_norm", "mla_w_kvb",
                "mla_q_norm", "mla_k_norm", "conv_w", "swa_q_norm", "swa_k_norm", "swa_sinks", "w_out")
SHARD_AXIS = {"w_in": 2, "w_out": 1, "mla_w_qb": 2, "mla_w_kvb": 2, "conv_w": 2}


def kernel(x, norm_g, w_in, mla_q_a_norm, mla_w_qb, mla_kv_a_norm, mla_w_kvb, mla_q_norm, mla_k_norm, conv_w, swa_q_norm, swa_k_norm, swa_sinks, w_out, loss_target, m_norm_g, m_w_in, m_mla_q_a_norm, m_mla_w_qb, m_mla_kv_a_norm, m_mla_w_kvb, m_mla_q_norm, m_mla_k_norm, m_conv_w, m_swa_q_norm, m_swa_k_norm, m_swa_sinks, m_w_out, v_norm_g, v_w_in, v_mla_q_a_norm, v_mla_w_qb, v_mla_kv_a_norm, v_mla_w_kvb, v_mla_q_norm, v_mla_k_norm, v_conv_w, v_swa_q_norm, v_swa_k_norm, v_swa_sinks, v_w_out):
    weights = dict(norm_g=norm_g, w_in=w_in, mla_q_a_norm=mla_q_a_norm, mla_w_qb=mla_w_qb,
                   mla_kv_a_norm=mla_kv_a_norm, mla_w_kvb=mla_w_kvb, mla_q_norm=mla_q_norm,
                   mla_k_norm=mla_k_norm, conv_w=conv_w, swa_q_norm=swa_q_norm,
                   swa_k_norm=swa_k_norm, swa_sinks=swa_sinks, w_out=w_out)
    mom_m = dict(norm_g=m_norm_g, w_in=m_w_in, mla_q_a_norm=m_mla_q_a_norm, mla_w_qb=m_mla_w_qb,
                 mla_kv_a_norm=m_mla_kv_a_norm, mla_w_kvb=m_mla_w_kvb, mla_q_norm=m_mla_q_norm,
                 mla_k_norm=m_mla_k_norm, conv_w=m_conv_w, swa_q_norm=m_swa_q_norm,
                 swa_k_norm=m_swa_k_norm, swa_sinks=m_swa_sinks, w_out=m_w_out)
    mom_v = dict(norm_g=v_norm_g, w_in=v_w_in, mla_q_a_norm=v_mla_q_a_norm, mla_w_qb=v_mla_w_qb,
                 mla_kv_a_norm=v_mla_kv_a_norm, mla_w_kvb=v_mla_w_kvb, mla_q_norm=v_mla_q_norm,
                 mla_k_norm=v_mla_k_norm, conv_w=v_conv_w, swa_q_norm=v_swa_q_norm,
                 swa_k_norm=v_swa_k_norm, swa_sinks=v_swa_sinks, w_out=v_w_out)
    xs = x[0]
    target = loss_target[0]
    s = xs.shape[0]
    c_idx = lax.axis_index("c").astype(jnp.int32).reshape(1)

    conv_bits = lax.bitcast_convert_type(conv_w, BF16)
    gather_list = [w_in, w_out, mla_w_qb, mla_w_kvb, conv_bits]
    gathered = _all_gather(_pack(gather_list, BF16))
    parts = _unpack(gathered, [a.shape for a in gather_list])
    join = lambda p, axis: jnp.concatenate([p[k] for k in range(N_CHIPS)], axis=axis)
    w_in_full = join(parts[0], 2)
    w_out_full = join(parts[1], 1)
    w_qb_full = join(parts[2], 2)
    w_kvb_full = join(parts[3], 2)
    conv_full = lax.bitcast_convert_type(join(parts[4], 2), F32)

    rope = _rope_tables(s)
    layers = []
    for l in range(DEPTH):
        wq = jnp.pad(w_qb_full[l].reshape(MLA_Q_LORA, MLA_HEADS, MLA_QK),
                     ((0, 0), (0, 0), (0, LANES - MLA_QK))).reshape(MLA_Q_LORA, MLA_HEADS * LANES)
        kv = w_kvb_full[l].reshape(MLA_KV_LORA, MLA_HEADS, MLA_NOPE + MLA_V)
        wk = jnp.pad(kv[:, :, :MLA_NOPE], ((0, 0), (0, 0), (0, LANES - MLA_NOPE)))
        wkv = jnp.concatenate([wk.reshape(MLA_KV_LORA, MLA_HEADS * LANES),
                               kv[:, :, MLA_NOPE:].reshape(MLA_KV_LORA, MLA_HEADS * MLA_V)], axis=1)
        layers.append(dict(
            w_in=_permute_w_in(w_in_full[l]), w_out=w_out_full[l], wq=wq, wkv=wkv,
            cw=jnp.pad(conv_full[l], ((0, HALO - 3), (0, 0))),
            g=norm_g[l][None], qan=mla_q_a_norm[l][None], kvan=mla_kv_a_norm[l][None],
            qn=_pad_lanes(mla_q_norm[l][None], LANES), kn=_pad_lanes(mla_k_norm[l][None], LANES),
            sqn=swa_q_norm[l][None], skn=swa_k_norm[l][None], sinks=swa_sinks[l][None]))

    saved = []
    h_in = xs
    for l in range(DEPTH):
        p = layers[l]
        proj, hb = _in_proj_fwd(h_in, p["g"], p["w_in"])
        q, k, v = _mla_prep_fwd(proj, p["qan"], p["kvan"], p["qn"], p["kn"], p["wq"], p["wkv"], rope)
        o_mla, lse = _mla_attn_fwd(q, k, v)
        o_swa = _swa_fwd(proj, p["sqn"], p["skn"], p["sinks"])
        y, z = _out_fwd(proj, o_mla, o_swa, h_in, p["w_out"], p["cw"])
        saved.append(dict(x=h_in, proj=proj, hb=hb, q=q, k=k, v=v, o_mla=o_mla, lse=lse,
                          o_swa=o_swa, z=z))
        h_in = y

    dy, loss_acc = _loss_head(h_in, target)
    loss = lax.psum(loss_acc[0, 0], ("x", "y", "c"))

    grads = {n: [None] * DEPTH for n in WEIGHT_ORDER}
    for l in reversed(range(DEPTH)):
        p, a = layers[l], saved[l]
        dconv, dgates, do_mla, delta, do_swa, dcw = _out_bwd(dy, a["proj"], a["o_mla"], a["o_swa"],
                                                             p["w_out"], p["cw"])
        grads["w_out"][l] = _matmul_tn(a["z"], dy, "dw_out")
        grads["conv_w"][l] = dcw[0:3]
        lse_rows = jnp.transpose(a["lse"], (0, 2, 1))
        delta_rows = jnp.transpose(delta, (1, 0)).reshape(MLA_HEADS // 2, 2, s)
        dq, dk, dv = _mla_attn_bwd(a["q"], a["k"], a["v"], do_mla, lse_rows, delta_rows)
        dmla, dqan, dkvan, dqn, dkn, dwq, dwkv = _mla_prep_bwd(
            a["proj"], p["qan"], p["kvan"], p["qn"], p["kn"], p["wq"], p["wkv"], rope, dq, dk, dv)
        dsq, dskv, dsqn, dskn, dsinks = _swa_bwd(a["proj"], p["sqn"], p["skn"], p["sinks"], do_swa)
        pieces = [dconv, dgates, dsq, dmla, dskv]
        dx, dg = _in_proj_bwd(pieces, a["x"], p["g"], p["w_in"], dy)
        dw_in = jnp.concatenate([_matmul_tn(a["hb"], pc, "dw_in_%d" % n)
                                 for n, pc in enumerate(pieces)], axis=1)
        grads["w_in"][l] = _unpermute_dw_in(dw_in)
        grads["norm_g"][l] = dg[0]
        grads["mla_q_a_norm"][l] = dqan[0]
        grads["mla_kv_a_norm"][l] = dkvan[0]
        grads["mla_q_norm"][l] = dqn[0, :MLA_QK]
        grads["mla_k_norm"][l] = dkn[0, :MLA_QK]
        grads["mla_w_qb"][l] = dwq.reshape(MLA_Q_LORA, MLA_HEADS, LANES)[:, :, :MLA_QK].reshape(
            MLA_Q_LORA, MLA_HEADS * MLA_QK)
        dwk = dwkv[:, :MLA_HEADS * LANES].reshape(MLA_KV_LORA, MLA_HEADS, LANES)[:, :, :MLA_NOPE]
        dwv = dwkv[:, MLA_HEADS * LANES:].reshape(MLA_KV_LORA, MLA_HEADS, MLA_V)
        grads["mla_w_kvb"][l] = jnp.concatenate([dwk, dwv], axis=2).reshape(
            MLA_KV_LORA, MLA_HEADS * (MLA_NOPE + MLA_V))
        grads["swa_q_norm"][l] = dsqn[0]
        grads["swa_k_norm"][l] = dskn[0]
        grads["swa_sinks"][l] = dsinks[0]
        dy = dx
    grad_x = dy[None]
    full_grads = {n: jnp.stack(grads[n]) for n in WEIGHT_ORDER}

    names = SHARDED + REPLICATED

    def chunk_list(k):
        out = []
        for n in SHARDED:
            g = full_grads[n]
            width = g.shape[SHARD_AXIS[n]] // N_CHIPS
            out.append(lax.slice_in_dim(g, k * width, (k + 1) * width, axis=SHARD_AXIS[n]))
        return out + [full_grads[n] for n in REPLICATED]

    g_all = jnp.stack([_pack(chunk_list(k), F32) for k in range(N_CHIPS)])
    from_sibling = _swap_halves_to_sibling(g_all)
    partial = _add_sibling(g_all, from_sibling, c_idx)
    by_chip = _scatter_to_chips(partial)
    g_mine = _join_halves(_sum_chips(by_chip))

    d_buf, m_buf, v_buf = _adamw(g_mine, _pack([weights[n] for n in names], F32),
                                 _pack([mom_m[n] for n in names], F32),
                                 _pack([mom_v[n] for n in names], F32))
    shapes = [weights[n].shape for n in names]
    unpacked = [dict(zip(names, _unpack(buf, shapes))) for buf in (g_mine, d_buf, m_buf, v_buf)]
    outs = [loss, grad_x]
    for group in unpacked:
        outs += [group[n] for n in WEIGHT_ORDER]
    return tuple(outs)
```

```python
import functools

import jax
import jax.numpy as jnp
from jax import lax
from jax.experimental import pallas as pl
from jax.experimental.pallas import tpu as pltpu

F32 = jnp.float32
BF16 = jnp.bfloat16

D_MODEL = 1024
DEPTH = 2
GROUP = 512
D_MIX = 3 * GROUP
BLOCK = 128
RMS_EPS = 1e-6
NEG_INF = -1e30
MLA_HEADS = 8
MLA_QK = 96
MLA_NOPE = 64
MLA_ROPE = 32
MLA_V = 64
MLA_Q_LORA = 256
MLA_KV_LORA = 128
ROPE_THETA = 10000.0
SWA_HEADS = 8
SWA_KV = 2
SWA_GROUP = 4
SWA_DIM = 64
IN_COLS = 4256
N_CHIPS = 4

NC = 4352
OFF_CONV, OFF_GATES, OFF_SQ, OFF_MLA, OFF_SKV = 0, 1536, 3072, 3584, 4096
PIECE_WIDTHS = (1536, 1536, 512, 512, 256)

VMEM_LIMIT = 56 * 1024 * 1024
LANES = 128
PACK_COLS = 1024
PACK_ROW_ALIGN = 32

ADAM_LR = 0.001
ADAM_B1 = 0.9
ADAM_B2 = 0.999
ADAM_EPS = 1e-08
ADAM_WD = 0.01
ADAM_STEP = 10

MESH = pl.DeviceIdType.MESH


def _params(sem, vmem=VMEM_LIMIT):
    return pltpu.CompilerParams(dimension_semantics=sem, vmem_limit_bytes=vmem)


def _dot(a, b, dims):
    return lax.dot_general(a.astype(BF16), b.astype(BF16), (dims, ((), ())),
                           preferred_element_type=F32)


def _mm(a, b):
    return _dot(a, b, ((1,), (0,)))


def _mm_nt(a, b):
    return _dot(a, b, ((1,), (1,)))


def _mm_tn(a, b):
    return _dot(a, b, ((0,), (0,)))


@jax.custom_vjp
def _mmd(a, b):
    return _mm(a, b)


def _mmd_fwd(a, b):
    return _mm(a, b), (a, b)


def _mmd_bwd(res, g):
    a, b = res
    return _mm_nt(g, b), _mm_tn(a, g)


_mmd.defvjp(_mmd_fwd, _mmd_bwd)


@jax.custom_vjp
def _mmd_nt(a, b):
    return _mm_nt(a, b)


def _mmd_nt_fwd(a, b):
    return _mm_nt(a, b), (a, b)


def _mmd_nt_bwd(res, g):
    a, b = res
    return _mm(g, b), _mm_tn(g, a)


_mmd_nt.defvjp(_mmd_nt_fwd, _mmd_nt_bwd)


def _rms(x, g, n=None):
    n = x.shape[-1] if n is None else n
    ms = jnp.sum(x * x, axis=-1, keepdims=True) * (1.0 / n)
    return x * lax.rsqrt(ms + RMS_EPS) * g


def _sigmoid(x):
    return 1.0 / (1.0 + jnp.exp(-x))


@jax.custom_vjp
def _rope(t, c, s1, s2):
    return t * c + pltpu.roll(t, LANES - 16, 1) * s1 + pltpu.roll(t, 16, 1) * s2


def _rope_fwd(t, c, s1, s2):
    return _rope(t, c, s1, s2), (c, s1, s2)


def _rope_bwd(res, g):
    c, s1, s2 = res
    dt = g * c + pltpu.roll(g * s1, 16, 1) + pltpu.roll(g * s2, LANES - 16, 1)
    return dt, jnp.zeros_like(c), jnp.zeros_like(s1), jnp.zeros_like(s2)


_rope.defvjp(_rope_fwd, _rope_bwd)


def _in_proj_fwd(x, g, w):
    s = x.shape[0]
    tm = min(256, s)

    def body(x_ref, g_ref, w_ref, proj_ref, hb_ref):
        hb = _rms(x_ref[...], g_ref[...]).astype(BF16)
        hb_ref[...] = hb
        proj_ref[...] = jnp.dot(hb, w_ref[...], preferred_element_type=F32)

    return pl.pallas_call(
        body, name="in_proj_fwd", grid=(s // tm,),
        in_specs=[pl.BlockSpec((tm, D_MODEL), lambda i: (i, 0)),
                  pl.BlockSpec((1, D_MODEL), lambda i: (0, 0)),
                  pl.BlockSpec((D_MODEL, NC), lambda i: (0, 0))],
        out_specs=[pl.BlockSpec((tm, NC), lambda i: (i, 0)),
                   pl.BlockSpec((tm, D_MODEL), lambda i: (i, 0))],
        out_shape=[jax.ShapeDtypeStruct((s, NC), F32), jax.ShapeDtypeStruct((s, D_MODEL), BF16)],
        compiler_params=_params(("parallel",)),
    )(x, g, w)


def _in_proj_bwd(pieces, x, g, w, dres):
    s = x.shape[0]
    tm = min(256, s)
    n_p = len(pieces)

    def body(*refs):
        p_refs = refs[:n_p]
        x_ref, g_ref, w_ref, dres_ref, dx_ref, dg_ref = refs[n_p:]
        dh = None
        off = 0
        for r in p_refs:
            width = r.shape[1]
            t = _mm_nt(r[...], w_ref[:, off:off + width])
            dh = t if dh is None else dh + t
            off += width
        _, vjp = jax.vjp(_rms, x_ref[...], g_ref[...])
        dx, dg = vjp(dh)
        dx_ref[...] = dx + dres_ref[...]

        @pl.when(pl.program_id(0) == 0)
        def _():
            dg_ref[...] = jnp.zeros_like(dg_ref)

        dg_ref[...] += dg

    in_specs = [pl.BlockSpec((tm, p.shape[1]), lambda i: (i, 0)) for p in pieces]
    in_specs += [pl.BlockSpec((tm, D_MODEL), lambda i: (i, 0)),
                 pl.BlockSpec((1, D_MODEL), lambda i: (0, 0)),
                 pl.BlockSpec((D_MODEL, NC), lambda i: (0, 0)),
                 pl.BlockSpec((tm, D_MODEL), lambda i: (i, 0))]
    return pl.pallas_call(
        body, name="in_proj_bwd", grid=(s // tm,),
        in_specs=in_specs,
        out_specs=[pl.BlockSpec((tm, D_MODEL), lambda i: (i, 0)),
                   pl.BlockSpec((1, D_MODEL), lambda i: (0, 0))],
        out_shape=[jax.ShapeDtypeStruct((s, D_MODEL), F32), jax.ShapeDtypeStruct((1, D_MODEL), F32)],
        compiler_params=_params(("arbitrary",)),
    )(*pieces, x, g, w, dres)


def _matmul_tn(a, b, name):
    s, m = a.shape
    n = b.shape[1]
    tk = min(512, s)
    tn = min(512, n)

    def body(a_ref, b_ref, o_ref):
        @pl.when(pl.program_id(1) == 0)
        def _():
            o_ref[...] = jnp.zeros_like(o_ref)

        o_ref[...] += _mm_tn(a_ref[...], b_ref[...])

    return pl.pallas_call(
        body, name=name, grid=(n // tn, s // tk),
        in_specs=[pl.BlockSpec((tk, m), lambda j, k: (k, 0)),
                  pl.BlockSpec((tk, tn), lambda j, k: (k, j))],
        out_specs=pl.BlockSpec((m, tn), lambda j, k: (0, j)),
        out_shape=jax.ShapeDtypeStruct((m, n), F32),
        compiler_params=_params(("parallel", "arbitrary")),
    )(a, b)


def _prep_fn(q_lat, kv_lat, kr, qan, kvan, qn, kn, wq, wk, wv, c, s1, s2, mm):
    rq = _rms(q_lat, qan)
    rkv = _rms(kv_lat, kvan)
    qs, ks = [], []
    for h in range(MLA_HEADS):
        qs.append(_rope(_rms(mm(rq, wq[h]), qn, MLA_QK), c, s1, s2))
        ks.append(_rope(_rms(mm(rkv, wk[h]) + kr, kn, MLA_QK), c, s1, s2))
    return tuple(qs), tuple(ks), mm(rkv, wv)


def _prep_weights(wq_ref, wkv_ref):
    wq = tuple(wq_ref[:, LANES * h:LANES * (h + 1)].astype(F32) for h in range(MLA_HEADS))
    wk = tuple(wkv_ref[:, LANES * h:LANES * (h + 1)].astype(F32) for h in range(MLA_HEADS))
    wv = wkv_ref[:, LANES * MLA_HEADS:].astype(F32)
    return wq, wk, wv


def _prep_in_specs(tm):
    const = lambda shape: pl.BlockSpec(shape, lambda i: (0, 0))
    return [pl.BlockSpec((tm, 512), lambda i: (i, OFF_MLA // 512)),
            const((1, MLA_Q_LORA)), const((1, MLA_KV_LORA)), const((1, LANES)), const((1, LANES)),
            const((MLA_Q_LORA, 1024)), const((MLA_KV_LORA, 1536)),
            pl.BlockSpec((tm, LANES), lambda i: (i, 0)),
            pl.BlockSpec((tm, LANES), lambda i: (i, 0)),
            pl.BlockSpec((tm, LANES), lambda i: (i, 0))]


def _mla_prep_fwd(proj, qan, kvan, qn, kn, wq, wkv, rope):
    s = proj.shape[0]
    tm = min(512, s)

    def body(blk_ref, qan_ref, kvan_ref, qn_ref, kn_ref, wq_ref, wkv_ref, c_ref, s1_ref, s2_ref,
             q_ref, k_ref, v_ref, vt_ref):
        wq_h, wk_h, wv = _prep_weights(wq_ref, wkv_ref)
        qs, ks, v = _prep_fn(blk_ref[:, 0:256], blk_ref[:, 256:384], blk_ref[:, 384:512],
                             qan_ref[...], kvan_ref[...], qn_ref[...], kn_ref[...],
                             wq_h, wk_h, wv, c_ref[...], s1_ref[...], s2_ref[...], _mm)
        for h in range(MLA_HEADS):
            q_ref[:, LANES * h:LANES * (h + 1)] = (qs[h] * Q_PRESCALE).astype(BF16)
            k_ref[:, LANES * h:LANES * (h + 1)] = ks[h].astype(BF16)
        v_ref[...] = v.astype(BF16)
        vt_ref[...] = jnp.transpose(v).astype(BF16)

    return pl.pallas_call(
        body, name="mla_prep_fwd", grid=(s // tm,),
        in_specs=_prep_in_specs(tm),
        out_specs=[pl.BlockSpec((tm, 1024), lambda i: (i, 0)),
                   pl.BlockSpec((tm, 1024), lambda i: (i, 0)),
                   pl.BlockSpec((tm, 512), lambda i: (i, 0)),
                   pl.BlockSpec((512, tm), lambda i: (0, i))],
        out_shape=[jax.ShapeDtypeStruct((s, 1024), BF16), jax.ShapeDtypeStruct((s, 1024), BF16),
                   jax.ShapeDtypeStruct((s, 512), BF16), jax.ShapeDtypeStruct((512, s), BF16)],
        compiler_params=_params(("parallel",)),
    )(proj, qan, kvan, qn, kn, wq, wkv, *rope)


def _mla_prep_bwd(proj, qan, kvan, qn, kn, wq, wkv, rope, dq, dk, dv):
    s = proj.shape[0]
    tm = min(256, s)

    def body(blk_ref, qan_ref, kvan_ref, qn_ref, kn_ref, wq_ref, wkv_ref, c_ref, s1_ref, s2_ref,
             dq_ref, dk_ref, dv_ref,
             dblk_ref, dqan_ref, dkvan_ref, dqn_ref, dkn_ref, dwq_ref, dwkv_ref):
        wq_h, wk_h, wv = _prep_weights(wq_ref, wkv_ref)
        c, s1, s2 = c_ref[...], s1_ref[...], s2_ref[...]

        def fn(q_lat, kv_lat, kr, qan_, kvan_, qn_, kn_, wq_, wk_, wv_):
            return _prep_fn(q_lat, kv_lat, kr, qan_, kvan_, qn_, kn_, wq_, wk_, wv_, c, s1, s2, _mmd)

        _, vjp = jax.vjp(fn, blk_ref[:, 0:256], blk_ref[:, 256:384], blk_ref[:, 384:512],
                         qan_ref[...], kvan_ref[...], qn_ref[...], kn_ref[...], wq_h, wk_h, wv)
        cts = (tuple(dq_ref[:, LANES * h:LANES * (h + 1)] for h in range(MLA_HEADS)),
               tuple(dk_ref[:, LANES * h:LANES * (h + 1)] for h in range(MLA_HEADS)),
               dv_ref[...])
        dq_lat, dkv_lat, dkr, dqan, dkvan, dqn, dkn, dwq_h, dwk_h, dwv = vjp(cts)
        dblk_ref[:, 0:256] = dq_lat
        dblk_ref[:, 256:384] = dkv_lat
        dblk_ref[:, 384:512] = dkr

        @pl.when(pl.program_id(0) == 0)
        def _():
            for r in (dqan_ref, dkvan_ref, dqn_ref, dkn_ref, dwq_ref, dwkv_ref):
                r[...] = jnp.zeros_like(r)

        dqan_ref[...] += dqan
        dkvan_ref[...] += dkvan
        dqn_ref[...] += dqn
        dkn_ref[...] += dkn
        for h in range(MLA_HEADS):
            dwq_ref[:, LANES * h:LANES * (h + 1)] += dwq_h[h]
            dwkv_ref[:, LANES * h:LANES * (h + 1)] += dwk_h[h]
        dwkv_ref[:, LANES * MLA_HEADS:] += dwv

    const = lambda shape: pl.BlockSpec(shape, lambda i: (0, 0))
    row = lambda width: pl.BlockSpec((tm, width), lambda i: (i, 0))
    shapes = [(s, 512), (1, MLA_Q_LORA), (1, MLA_KV_LORA), (1, LANES), (1, LANES),
              (MLA_Q_LORA, 1024), (MLA_KV_LORA, 1536)]
    return pl.pallas_call(
        body, name="mla_prep_bwd", grid=(s // tm,),
        in_specs=_prep_in_specs(tm) + [row(1024), row(1024), row(512)],
        out_specs=[row(512)] + [const(sh) for sh in shapes[1:]],
        out_shape=[jax.ShapeDtypeStruct(sh, F32) for sh in shapes],
        compiler_params=_params(("arbitrary",)),
    )(proj, qan, kvan, qn, kn, wq, wkv, *rope, dq, dk, dv)


MLA_SCALE = MLA_QK ** -0.5


LOG2E = 1.4426950408889634
LN2 = 0.6931471805599453
Q_PRESCALE = MLA_SCALE * LOG2E


def _mla_attn_fwd(q2, k, vt):
    s = q2.shape[0]
    t = min(256, s)
    nq = s // t

    def body(q_ref, k_ref, vt_ref, o_ref, lse_ref, acc_ref):
        i = pl.program_id(1)
        row = lax.broadcasted_iota(jnp.int32, (t, t), 0)
        col = lax.broadcasted_iota(jnp.int32, (t, t), 1)
        causal_t = row <= col
        qh = [q_ref[:, LANES * hh:LANES * (hh + 1)] for hh in range(2)]
        acc_ref[...] = jnp.zeros_like(acc_ref)

        def scores(j, masked):
            r0 = pl.multiple_of(j * t, t)
            out = []
            for hh in range(2):
                kc = k_ref[pl.ds(r0, t), LANES * hh:LANES * (hh + 1)]
                sc = lax.dot_general(kc, qh[hh], (((1,), (1,)), ((), ())),
                                     preferred_element_type=F32)
                out.append(jnp.where(causal_t, sc, NEG_INF) if masked else sc)
            return tuple(out)

        def consume(j, scs, stats):
            r0 = pl.multiple_of(j * t, t)
            out, ps, alphas = [], [], []
            for hh in range(2):
                m, l = stats[hh]
                m_new = jnp.maximum(m, jnp.max(scs[hh], axis=0, keepdims=True))
                p = jnp.exp2(scs[hh] - m_new)
                alpha = jnp.exp2(m - m_new)
                out.append((m_new, alpha * l + jnp.sum(p, axis=0, keepdims=True)))
                ps.append(p.astype(BF16))
                alphas.append(alpha)
            for hh in range(2):
                vc = vt_ref[MLA_V * hh:MLA_V * (hh + 1), pl.ds(r0, t)]
                acc_ref[hh] = alphas[hh] * acc_ref[hh] + jnp.dot(vc, ps[hh],
                                                                 preferred_element_type=F32)
            return tuple(out)

        def trip(j, carry):
            scs, stats = carry
            nxt = scores(j, False)
            stats = consume(jnp.where(j == 0, i, j - 1), scs, stats)
            return nxt, stats

        init = tuple((jnp.full((1, t), NEG_INF, F32), jnp.zeros((1, t), F32)) for _ in range(2))
        scs, stats = lax.fori_loop(0, i, trip, (scores(i, True), init))
        stats = consume(jnp.where(i == 0, i, i - 1), scs, stats)
        for hh in range(2):
            m, l = stats[hh]
            o_ref[:, MLA_V * hh:MLA_V * (hh + 1)] = jnp.transpose(acc_ref[hh] / l)
            lse_ref[0, hh:hh + 1, :] = m + jnp.log2(l)

    return pl.pallas_call(
        body, name="mla_attn_fwd", grid=(MLA_HEADS // 2, nq),
        in_specs=[pl.BlockSpec((t, 256), lambda p, i: (i, p)),
                  pl.BlockSpec((s, 256), lambda p, i: (0, p)),
                  pl.BlockSpec((128, s), lambda p, i: (p, 0))],
        out_specs=[pl.BlockSpec((t, 128), lambda p, i: (i, p)),
                   pl.BlockSpec((1, 2, t), lambda p, i: (p, 0, i))],
        out_shape=[jax.ShapeDtypeStruct((s, 512), F32),
                   jax.ShapeDtypeStruct((MLA_HEADS // 2, 2, s), F32)],
        scratch_shapes=[pltpu.VMEM((2, MLA_V, t), F32)],
        compiler_params=_params(("parallel", "arbitrary")),
    )(q2, k, vt)


def _mla_attn_bwd(q2, k, v, do, lse_rows, delta_rows):
    s = q2.shape[0]
    t = min(256, s)
    nq = s // t

    def body(q_ref, k_ref, v_ref, do_ref, lse_ref, dl_ref, dq_ref, dk_ref, dv_ref):
        j = pl.program_id(1)

        @pl.when(j == 0)
        def _():
            dq_ref[...] = jnp.zeros_like(dq_ref)

        dk_ref[...] = jnp.zeros_like(dk_ref)
        dv_ref[...] = jnp.zeros_like(dv_ref)
        row = lax.broadcasted_iota(jnp.int32, (t, t), 0)
        col = lax.broadcasted_iota(jnp.int32, (t, t), 1)
        causal_t = row <= col
        kh = [k_ref[:, LANES * hh:LANES * (hh + 1)] for hh in range(2)]
        vh = [v_ref[:, MLA_V * hh:MLA_V * (hh + 1)] for hh in range(2)]

        def load_q_do(i):
            r0 = pl.multiple_of(i * t, t)
            qhs = [q_ref[pl.ds(r0, t), LANES * hh:LANES * (hh + 1)] for hh in range(2)]
            dohs = [do_ref[pl.ds(r0, t), MLA_V * hh:MLA_V * (hh + 1)] for hh in range(2)]
            return r0, qhs, dohs

        def scores(i, masked):
            _, qhs, dohs = load_q_do(i)
            out = []
            for hh in range(2):
                sc_t = lax.dot_general(kh[hh], qhs[hh], (((1,), (1,)), ((), ())),
                                       preferred_element_type=F32)
                out.append(jnp.where(causal_t, sc_t, NEG_INF) if masked else sc_t)
                out.append(lax.dot_general(vh[hh], dohs[hh], (((1,), (1,)), ((), ())),
                                           preferred_element_type=F32))
            return tuple(out)

        def consume(i, sd):
            r0, qhs, dohs = load_q_do(i)
            pts, gts = [], []
            for hh in range(2):
                lse = lse_ref[0, hh:hh + 1, pl.ds(r0, t)]
                dl = dl_ref[0, hh:hh + 1, pl.ds(r0, t)]
                p_t = jnp.exp2(sd[2 * hh] - lse)
                pts.append(p_t.astype(BF16))
                gts.append((p_t * (sd[2 * hh + 1] - dl)).astype(BF16))
            for hh in range(2):
                dv_ref[:, MLA_V * hh:MLA_V * (hh + 1)] += jnp.dot(
                    pts[hh], dohs[hh], preferred_element_type=F32)
                dk_ref[:, LANES * hh:LANES * (hh + 1)] += jnp.dot(gts[hh], qhs[hh],
                                                                  preferred_element_type=F32)
                dq_ref[pl.ds(r0, t), LANES * hh:LANES * (hh + 1)] += lax.dot_general(
                    gts[hh], kh[hh], (((0,), (0,)), ((), ())), preferred_element_type=F32)

        consume(j, scores(j, True))

        def trip(i, carry):
            consume(i, scores(i, False))
            return carry

        lax.fori_loop(j + 1, nq, trip, 0)
        dk_ref[...] = dk_ref[...] * LN2

        @pl.when(j == nq - 1)
        def _():
            dq_ref[...] = dq_ref[...] * MLA_SCALE

    return pl.pallas_call(
        body, name="mla_attn_bwd", grid=(MLA_HEADS // 2, nq),
        in_specs=[pl.BlockSpec((s, 256), lambda p, j: (0, p)),
                  pl.BlockSpec((t, 256), lambda p, j: (j, p)),
                  pl.BlockSpec((t, 128), lambda p, j: (j, p)),
                  pl.BlockSpec((s, 128), lambda p, j: (0, p)),
                  pl.BlockSpec((1, 2, s), lambda p, j: (p, 0, 0)),
                  pl.BlockSpec((1, 2, s), lambda p, j: (p, 0, 0))],
        out_specs=[pl.BlockSpec((s, 256), lambda p, j: (0, p)),
                   pl.BlockSpec((t, 256), lambda p, j: (j, p)),
                   pl.BlockSpec((t, 128), lambda p, j: (j, p))],
        out_shape=[jax.ShapeDtypeStruct((s, 1024), F32), jax.ShapeDtypeStruct((s, 1024), F32),
                   jax.ShapeDtypeStruct((s, 512), F32)],
        compiler_params=_params(("parallel", "arbitrary")),
    )(q2, k, v, do, lse_rows, delta_rows)


SWA_SCALE = SWA_DIM ** -0.5
SWA_ROWS = SWA_GROUP * BLOCK


def _swa_consts(first):
    r = lax.broadcasted_iota(jnp.int32, (SWA_ROWS, 2 * BLOCK), 0)
    kidx = lax.broadcasted_iota(jnp.int32, (SWA_ROWS, 2 * BLOCK), 1)
    dist = BLOCK + (r % BLOCK) - kidx
    valid = (dist >= 0) & (dist < BLOCK) & ((kidx >= BLOCK) | jnp.logical_not(first))
    grp = r // BLOCK
    distf = dist.astype(F32)
    rs = lax.broadcasted_iota(jnp.int32, (SWA_ROWS, SWA_HEADS), 0) // BLOCK
    hs = lax.broadcasted_iota(jnp.int32, (SWA_ROWS, SWA_HEADS), 1)
    biases, sels = [], []
    for j in range(SWA_KV):
        slope = jnp.zeros((SWA_ROWS, 2 * BLOCK), F32)
        for g in range(SWA_GROUP):
            slope = jnp.where(grp == g, 2.0 ** -(SWA_GROUP * j + g + 1), slope)
        biases.append(slope * distf)
        sels.append((hs == rs + SWA_GROUP * j).astype(F32))
    return valid, biases, sels


def _swa_block(qh, kprev, kcur, vprev, vcur, qn, kn, sinks, consts, mm_nt, mm):
    valid, biases, sels = consts
    outs = []
    for j in range(SWA_KV):
        kb = _rms(jnp.concatenate([kprev[j], kcur[j]], axis=0), kn)
        vb = jnp.concatenate([vprev[j], vcur[j]], axis=0)
        qs = jnp.concatenate([_rms(qh[SWA_GROUP * j + g], qn) for g in range(SWA_GROUP)], axis=0)
        sc = mm_nt(qs, kb) * SWA_SCALE - biases[j]
        sc = jnp.where(valid, sc, NEG_INF)
        sink = jnp.sum(sels[j] * sinks, axis=-1, keepdims=True)
        m = lax.stop_gradient(jnp.maximum(jnp.max(sc, axis=-1, keepdims=True), sink))
        e = jnp.exp(sc - m)
        den = jnp.sum(e, axis=-1, keepdims=True) + jnp.exp(sink - m)
        outs.append(mm(e / den, vb))
    return tuple(outs)


def _swa_load(sq_ref, skv_ref, halo_ref, b):
    r = slice(BLOCK * b, BLOCK * (b + 1))
    qh = tuple(sq_ref[r, SWA_DIM * h:SWA_DIM * (h + 1)] for h in range(SWA_HEADS))
    kcur = tuple(skv_ref[r, SWA_DIM * j:SWA_DIM * (j + 1)] for j in range(SWA_KV))
    vcur = tuple(skv_ref[r, 128 + SWA_DIM * j:128 + SWA_DIM * (j + 1)] for j in range(SWA_KV))
    if b == 0:
        src, rp = halo_ref, slice(0, BLOCK)
    else:
        src, rp = skv_ref, slice(BLOCK * (b - 1), BLOCK * b)
    kprev = tuple(src[rp, SWA_DIM * j:SWA_DIM * (j + 1)] for j in range(SWA_KV))
    vprev = tuple(src[rp, 128 + SWA_DIM * j:128 + SWA_DIM * (j + 1)] for j in range(SWA_KV))
    return qh, kprev, kcur, vprev, vcur


def _swa_fwd(proj, qn, kn, sinks):
    s = proj.shape[0]
    ts = min(512, s)
    nb = ts // BLOCK

    def body(sq_ref, skv_ref, halo_ref, qn_ref, kn_ref, sk_ref, o_ref):
        i = pl.program_id(0)
        for b in range(nb):
            first = (i == 0) if b == 0 else False
            consts = _swa_consts(jnp.asarray(first))
            outs = _swa_block(*_swa_load(sq_ref, skv_ref, halo_ref, b), qn_ref[...], kn_ref[...],
                              sk_ref[...], consts, _mm_nt, _mm)
            for j in range(SWA_KV):
                for g in range(SWA_GROUP):
                    h = SWA_GROUP * j + g
                    o_ref[BLOCK * b:BLOCK * (b + 1), SWA_DIM * h:SWA_DIM * (h + 1)] = (
                        outs[j][BLOCK * g:BLOCK * (g + 1)])

    const = lambda shape: pl.BlockSpec(shape, lambda i: (0, 0))
    return pl.pallas_call(
        body, name="swa_fwd", grid=(s // ts,),
        in_specs=[pl.BlockSpec((ts, 512), lambda i: (i, OFF_SQ // 512)),
                  pl.BlockSpec((ts, 256), lambda i: (i, OFF_SKV // 256)),
                  pl.BlockSpec((BLOCK, 256), lambda i: (jnp.maximum(i * nb - 1, 0), OFF_SKV // 256)),
                  const((1, SWA_DIM)), const((1, SWA_DIM)), const((1, SWA_HEADS))],
        out_specs=pl.BlockSpec((ts, 512), lambda i: (i, 0)),
        out_shape=jax.ShapeDtypeStruct((s, 512), F32),
        compiler_params=_params(("parallel",)),
    )(proj, proj, proj, qn, kn, sinks)


def _swa_bwd(proj, qn, kn, sinks, do):
    s = proj.shape[0]
    ts = min(512, s)
    nb = ts // BLOCK
    nt = s // ts

    def body(sq_ref, skv_ref, halo_ref, qn_ref, kn_ref, sk_ref, do_ref,
             dsq_ref, dskv_ref, dqn_ref, dkn_ref, dsk_ref, carry_ref):
        step = pl.program_id(0)
        i = nt - 1 - step

        @pl.when(step == 0)
        def _():
            carry_ref[...] = jnp.zeros_like(carry_ref)
            dqn_ref[...] = jnp.zeros_like(dqn_ref)
            dkn_ref[...] = jnp.zeros_like(dkn_ref)
            dsk_ref[...] = jnp.zeros_like(dsk_ref)

        dk_rows = [[None] * SWA_KV for _ in range(nb)]
        dv_rows = [[None] * SWA_KV for _ in range(nb)]
        halo_k, halo_v = None, None
        dqn = dkn = dsk = None

        def add(a, b_):
            return b_ if a is None else a + b_

        for b in range(nb):
            first = (i == 0) if b == 0 else False
            consts = _swa_consts(jnp.asarray(first))
            qh, kprev, kcur, vprev, vcur = _swa_load(sq_ref, skv_ref, halo_ref, b)

            def fn(qh_, kprev_, kcur_, vprev_, vcur_, qn_, kn_, sk_, consts=consts):
                return _swa_block(qh_, kprev_, kcur_, vprev_, vcur_, qn_, kn_, sk_, consts,
                                  _mmd_nt, _mmd)

            _, vjp = jax.vjp(fn, qh, kprev, kcur, vprev, vcur, qn_ref[...], kn_ref[...], sk_ref[...])
            r = slice(BLOCK * b, BLOCK * (b + 1))
            cts = tuple(
                jnp.concatenate([do_ref[r, SWA_DIM * (SWA_GROUP * j + g):SWA_DIM * (SWA_GROUP * j + g + 1)]
                                 for g in range(SWA_GROUP)], axis=0)
                for j in range(SWA_KV))
            dqh, dkprev, dkcur, dvprev, dvcur, dqn_b, dkn_b, dsk_b = vjp(cts)
            for h in range(SWA_HEADS):
                dsq_ref[r, SWA_DIM * h:SWA_DIM * (h + 1)] = dqh[h]
            for j in range(SWA_KV):
                dk_rows[b][j] = add(dk_rows[b][j], dkcur[j])
                dv_rows[b][j] = add(dv_rows[b][j], dvcur[j])
                if b > 0:
                    dk_rows[b - 1][j] = dk_rows[b - 1][j] + dkprev[j]
                    dv_rows[b - 1][j] = dv_rows[b - 1][j] + dvprev[j]
            if b == 0:
                halo_k, halo_v = dkprev, dvprev
            dqn, dkn, dsk = add(dqn, dqn_b), add(dkn, dkn_b), add(dsk, dsk_b)

        for b in range(nb):
            r = slice(BLOCK * b, BLOCK * (b + 1))
            for j in range(SWA_KV):
                dkv = dk_rows[b][j], dv_rows[b][j]
                for part in range(2):
                    c0 = 128 * part + SWA_DIM * j
                    val = dkv[part]
                    if b == nb - 1:
                        val = val + carry_ref[:, c0:c0 + SWA_DIM]
                    dskv_ref[r, c0:c0 + SWA_DIM] = val
        for j in range(SWA_KV):
            carry_ref[:, SWA_DIM * j:SWA_DIM * (j + 1)] = halo_k[j]
            carry_ref[:, 128 + SWA_DIM * j:128 + SWA_DIM * (j + 1)] = halo_v[j]
        dqn_ref[...] += dqn
        dkn_ref[...] += dkn
        dsk_ref[...] += dsk

    const = lambda shape: pl.BlockSpec(shape, lambda st: (0, 0))
    return pl.pallas_call(
        body, name="swa_bwd", grid=(nt,),
        in_specs=[pl.BlockSpec((ts, 512), lambda st: (nt - 1 - st, OFF_SQ // 512)),
                  pl.BlockSpec((ts, 256), lambda st: (nt - 1 - st, OFF_SKV // 256)),
                  pl.BlockSpec((BLOCK, 256),
                               lambda st: (jnp.maximum((nt - 1 - st) * nb - 1, 0), OFF_SKV // 256)),
                  const((1, SWA_DIM)), const((1, SWA_DIM)), const((1, SWA_HEADS)),
                  pl.BlockSpec((ts, 512), lambda st: (nt - 1 - st, 0))],
        out_specs=[pl.BlockSpec((ts, 512), lambda st: (nt - 1 - st, 0)),
                   pl.BlockSpec((ts, 256), lambda st: (nt - 1 - st, 0)),
                   const((1, SWA_DIM)), const((1, SWA_DIM)), const((1, SWA_HEADS))],
        out_shape=[jax.ShapeDtypeStruct((s, 512), F32), jax.ShapeDtypeStruct((s, 256), F32),
                   jax.ShapeDtypeStruct((1, SWA_DIM), F32), jax.ShapeDtypeStruct((1, SWA_DIM), F32),
                   jax.ShapeDtypeStruct((1, SWA_HEADS), F32)],
        scratch_shapes=[pltpu.VMEM((BLOCK, 256), F32)],
        compiler_params=_params(("arbitrary",)),
    )(proj, proj, proj, qn, kn, sinks, do)


HALO = 8


def _shift_down(u, halo, k):
    tm = u.shape[0]
    rid = lax.broadcasted_iota(jnp.int32, u.shape, 0)
    out = pltpu.roll(u, k, 0)
    for r in range(k):
        out = jnp.where(rid == r, halo[HALO - k + r:HALO - k + r + 1, :], out)
    return out


def _shift_up(u, halo, k):
    tm = u.shape[0]
    rid = lax.broadcasted_iota(jnp.int32, u.shape, 0)
    out = pltpu.roll(u, tm - k, 0)
    for r in range(k):
        out = jnp.where(rid == tm - k + r, halo[r:r + 1, :], out)
    return out


def _conv_fwd_vals(conv_ref, convp_ref, cw_ref, is_first):
    c_h, c_b, c_c = conv_ref[:, 0:512], conv_ref[:, 512:1024], conv_ref[:, 1024:1536]
    u = c_c * c_h
    up = jnp.where(is_first, 0.0, convp_ref[:, 1024:1536] * convp_ref[:, 0:512])
    u1 = _shift_down(u, up, 1)
    u2 = _shift_down(u, up, 2)
    yc = cw_ref[0:1, :] * u2 + cw_ref[1:2, :] * u1 + cw_ref[2:3, :] * u
    return c_h, c_b, c_c, u, u1, u2, yc


def _out_fwd(proj, o_mla, o_swa, x, w_out, cw):
    s = proj.shape[0]
    tm = min(512, s)

    def body(conv_ref, convp_ref, gates_ref, om_ref, os_ref, x_ref, w_ref, cw_ref, y_ref, z_ref):
        i = pl.program_id(0)
        _, c_b, _, _, _, _, yc = _conv_fwd_vals(conv_ref, convp_ref, cw_ref, i == 0)
        mix = (om_ref[...], c_b * yc, os_ref[...])
        for n in range(3):
            g = gates_ref[:, GROUP * n:GROUP * (n + 1)]
            z_ref[:, GROUP * n:GROUP * (n + 1)] = (mix[n] * (g * _sigmoid(g))).astype(BF16)
        y_ref[...] = x_ref[...] + jnp.dot(z_ref[...], w_ref[...], preferred_element_type=F32)

    row = lambda width: pl.BlockSpec((tm, width), lambda i: (i, 0))
    return pl.pallas_call(
        body, name="out_fwd", grid=(s // tm,),
        in_specs=[pl.BlockSpec((tm, 1536), lambda i: (i, 0)),
                  pl.BlockSpec((HALO, 1536), lambda i: (jnp.maximum(i * (tm // HALO) - 1, 0), 0)),
                  pl.BlockSpec((tm, 1536), lambda i: (i, 1)),
                  row(512), row(512), row(D_MODEL),
                  pl.BlockSpec((D_MIX, D_MODEL), lambda i: (0, 0)),
                  pl.BlockSpec((HALO, 512), lambda i: (0, 0))],
        out_specs=[row(D_MODEL), row(D_MIX)],
        out_shape=[jax.ShapeDtypeStruct((s, D_MODEL), F32), jax.ShapeDtypeStruct((s, D_MIX), BF16)],
        compiler_params=_params(("parallel",)),
    )(proj, proj, proj, o_mla, o_swa, x, w_out, cw)


def _out_bwd(dy, proj, o_mla, o_swa, w_out, cw):
    s = proj.shape[0]
    tm = min(512, s)
    nt = s // tm
    hb = tm // HALO

    def body(dy_ref, dyn_ref, conv_ref, convp_ref, convn_ref, gates_ref, gatesn_ref, om_ref, os_ref,
             w_ref, cw_ref,
             dconv_ref, dgates_ref, dom_ref, delta_ref, dos_ref, dcw_ref):
        i = pl.program_id(0)
        dz = _mm_nt(dy_ref[...], w_ref[...])

        def gate(n):
            g = gates_ref[:, GROUP * n:GROUP * (n + 1)]
            sg = _sigmoid(g)
            return g * sg, sg * (1.0 + g * (1.0 - sg))

        for n, o_ref, do_ref in ((0, om_ref, dom_ref), (2, os_ref, dos_ref)):
            silu, dsilu = gate(n)
            dzn = dz[:, GROUP * n:GROUP * (n + 1)]
            o = o_ref[...]
            do = dzn * silu
            do_ref[...] = do.astype(do_ref.dtype)
            dgates_ref[:, GROUP * n:GROUP * (n + 1)] = dzn * o * dsilu
            if n == 0:
                t = do * o
                for h in range(MLA_HEADS):
                    delta_ref[:, h:h + 1] = jnp.sum(t[:, MLA_V * h:MLA_V * (h + 1)], axis=-1,
                                                    keepdims=True)

        c_h, c_b, c_c, u, u1, u2, yc = _conv_fwd_vals(conv_ref, convp_ref, cw_ref, i == 0)
        silu, dsilu = gate(1)
        dzc = dz[:, GROUP:2 * GROUP]
        dgates_ref[:, GROUP:2 * GROUP] = dzc * (c_b * yc) * dsilu
        dycr = dzc * silu
        dyc = dycr * c_b
        gn = gatesn_ref[:, GROUP:2 * GROUP]
        dzc_n = _mm_nt(dyn_ref[...], w_ref[GROUP:2 * GROUP, :])
        dyc_n = jnp.where(i == nt - 1, 0.0, dzc_n * (gn * _sigmoid(gn)) * convn_ref[:, 512:1024])
        d1 = _shift_up(dyc, dyc_n, 1)
        d2 = _shift_up(dyc, dyc_n, 2)
        du = cw_ref[2:3, :] * dyc + cw_ref[1:2, :] * d1 + cw_ref[0:1, :] * d2
        dconv_ref[:, 0:512] = du * c_c
        dconv_ref[:, 512:1024] = dycr * yc
        dconv_ref[:, 1024:1536] = du * c_h

        @pl.when(i == 0)
        def _():
            dcw_ref[...] = jnp.zeros_like(dcw_ref)

        for k, uk in enumerate((u2, u1, u)):
            dcw_ref[k:k + 1, :] += jnp.sum(dyc * uk, axis=0, keepdims=True)

    row = lambda width: pl.BlockSpec((tm, width), lambda i: (i, 0))
    prev = lambda i: jnp.maximum(i * hb - 1, 0)
    nxt = lambda i: jnp.minimum((i + 1) * hb, s // HALO - 1)
    return pl.pallas_call(
        body, name="out_bwd", grid=(nt,),
        in_specs=[row(D_MODEL),
                  pl.BlockSpec((HALO, D_MODEL), lambda i: (nxt(i), 0)),
                  pl.BlockSpec((tm, 1536), lambda i: (i, 0)),
                  pl.BlockSpec((HALO, 1536), lambda i: (prev(i), 0)),
                  pl.BlockSpec((HALO, 1536), lambda i: (nxt(i), 0)),
                  pl.BlockSpec((tm, 1536), lambda i: (i, 1)),
                  pl.BlockSpec((HALO, 1536), lambda i: (nxt(i), 1)),
                  row(512), row(512),
                  pl.BlockSpec((D_MIX, D_MODEL), lambda i: (0, 0)),
                  pl.BlockSpec((HALO, 512), lambda i: (0, 0))],
        out_specs=[row(1536), row(1536), row(512), row(MLA_HEADS), row(512),
                   pl.BlockSpec((HALO, 512), lambda i: (0, 0))],
        out_shape=[jax.ShapeDtypeStruct((s, 1536), F32), jax.ShapeDtypeStruct((s, 1536), F32),
                   jax.ShapeDtypeStruct((s, 512), BF16), jax.ShapeDtypeStruct((s, MLA_HEADS), F32),
                   jax.ShapeDtypeStruct((s, 512), F32), jax.ShapeDtypeStruct((HALO, 512), F32)],
        compiler_params=_params(("arbitrary",)),
    )(dy, dy, proj, proj, proj, proj, proj, o_mla, o_swa, w_out, cw)


def _loss_head(y, target):
    s, d = y.shape
    tm = min(512, s)
    nt = s // tm

    def body(y_ref, t_ref, dy_ref, loss_ref):
        i = pl.program_id(0)
        err = y_ref[...] - t_ref[...]
        dy_ref[...] = err * (1.0 / d)

        @pl.when(i == 0)
        def _():
            loss_ref[...] = jnp.zeros_like(loss_ref)

        sq = jnp.sum((err * err).reshape(tm // 8, 8, d), axis=0)
        part = sq[:, 0:LANES]
        for c in range(1, d // LANES):
            part = part + sq[:, LANES * c:LANES * (c + 1)]
        loss_ref[...] += part

        @pl.when(i == nt - 1)
        def _():
            loss_ref[...] = jnp.full(loss_ref.shape, (0.5 / d) * jnp.sum(loss_ref[...]), F32)

    return pl.pallas_call(
        body, name="loss_head", grid=(nt,),
        in_specs=[pl.BlockSpec((tm, d), lambda i: (i, 0)), pl.BlockSpec((tm, d), lambda i: (i, 0))],
        out_specs=[pl.BlockSpec((tm, d), lambda i: (i, 0)), pl.BlockSpec((8, LANES), lambda i: (0, 0))],
        out_shape=[jax.ShapeDtypeStruct((s, d), F32), jax.ShapeDtypeStruct((8, LANES), F32)],
        compiler_params=_params(("arbitrary",)),
    )(y, target)


def _adamw(g, w, m, v):
    rows = g.shape[0]
    tr = min(256, rows)
    c1 = 1.0 - ADAM_B1
    c2 = 1.0 - ADAM_B2
    bc1 = 1.0 - ADAM_B1 ** ADAM_STEP
    bc2 = 1.0 - ADAM_B2 ** ADAM_STEP

    def body(g_ref, w_ref, m_ref, v_ref, d_ref, mo_ref, vo_ref):
        gg = g_ref[...]
        m_new = ADAM_B1 * m_ref[...] + c1 * gg
        v_new = ADAM_B2 * v_ref[...] + c2 * (gg * gg)
        m_hat = m_new / bc1
        v_hat = v_new / bc2
        d_ref[...] = -ADAM_LR * (m_hat / (jnp.sqrt(v_hat) + ADAM_EPS) + ADAM_WD * w_ref[...])
        mo_ref[...] = m_new
        vo_ref[...] = v_new

    spec = pl.BlockSpec((tr, PACK_COLS), lambda i: (i, 0))
    return pl.pallas_call(
        body, name="adamw", grid=(rows // tr,),
        in_specs=[spec] * 4, out_specs=[spec] * 3,
        out_shape=[jax.ShapeDtypeStruct(g.shape, F32)] * 3,
        compiler_params=_params(("parallel",)),
    )(g, w, m, v)


HBM_SPEC = pl.BlockSpec(memory_space=pltpu.HBM)


def _place():
    x, y, c = lax.axis_index("x"), lax.axis_index("y"), lax.axis_index("c")
    chips = [(1 - x, y), (x, 1 - y), (1 - x, 1 - y)]
    return x, y, c, chips


def _all_gather(shard):
    rows, cols = shard.shape
    half = rows // 2

    def body(w_ref, a_ref, send_sems, recv_sems, local_sem):
        x, y, c, chips = _place()
        k = 2 * x + y
        sib = (x, y, 1 - c)

        def slab(kk, hc):
            return a_ref.at[kk, pl.ds(hc * half, half), :]

        def copy(n, src, dst, to):
            return pltpu.make_async_remote_copy(src_ref=src, dst_ref=dst, send_sem=send_sems.at[n],
                                                recv_sem=recv_sems.at[n], device_id=to,
                                                device_id_type=MESH)

        mine = pltpu.make_async_copy(w_ref, a_ref.at[k], local_sem)
        mine.start()
        first = [copy(n, w_ref.at[pl.ds(c * half, half), :], slab(k, c), (cx, cy, c))
                 for n, (cx, cy) in enumerate(chips)]
        for cp in first:
            cp.start()
        passed = []
        for n, (cx, cy) in enumerate(chips):
            kk = 2 * cx + cy
            copy(n, slab(kk, c), slab(kk, c), (cx, cy, c)).wait_recv()
            fwd = copy(3 + n, slab(kk, c), slab(kk, c), sib)
            fwd.start()
            passed.append(fwd)
        for n, (cx, cy) in enumerate(chips):
            kk = 2 * cx + cy
            copy(3 + n, slab(kk, 1 - c), slab(kk, 1 - c), sib).wait_recv()
        for cp in first + passed:
            cp.wait_send()
        mine.wait()

    return pl.pallas_call(
        body, name="weights_all_gather",
        in_specs=[HBM_SPEC], out_specs=HBM_SPEC,
        out_shape=jax.ShapeDtypeStruct((N_CHIPS, rows, cols), shard.dtype),
        scratch_shapes=[pltpu.SemaphoreType.DMA((6,)), pltpu.SemaphoreType.DMA((6,)),
                        pltpu.SemaphoreType.DMA],
    )(shard)


def _swap_halves_to_sibling(g):
    n, rows, cols = g.shape
    half = rows // 2

    def body(g_ref, r_ref, send_sem, recv_sem):
        x, y, c, _ = _place()
        cp = pltpu.make_async_remote_copy(
            src_ref=g_ref.at[:, pl.ds((1 - c) * half, half), :], dst_ref=r_ref,
            send_sem=send_sem, recv_sem=recv_sem, device_id=(x, y, 1 - c), device_id_type=MESH)
        cp.start()
        cp.wait()

    return pl.pallas_call(
        body, name="grads_to_sibling",
        in_specs=[HBM_SPEC], out_specs=HBM_SPEC,
        out_shape=jax.ShapeDtypeStruct((n, half, cols), g.dtype),
        scratch_shapes=[pltpu.SemaphoreType.DMA, pltpu.SemaphoreType.DMA],
    )(g)


def _add_sibling(g, r, c_idx):
    n, rows, cols = g.shape
    half = rows // 2
    tr = min(256, half)
    nb = half // tr

    def body(c_ref, g_ref, r_ref, p_ref):
        p_ref[...] = g_ref[...] + r_ref[...]

    return pl.pallas_call(
        body, name="grads_add_sibling",
        grid_spec=pltpu.PrefetchScalarGridSpec(
            num_scalar_prefetch=1, grid=(n, nb),
            in_specs=[pl.BlockSpec((1, tr, cols), lambda j, t, c_ref: (j, c_ref[0] * nb + t, 0)),
                      pl.BlockSpec((1, tr, cols), lambda j, t, c_ref: (j, t, 0))],
            out_specs=pl.BlockSpec((1, tr, cols), lambda j, t, c_ref: (j, t, 0))),
        out_shape=jax.ShapeDtypeStruct((n, half, cols), F32),
        compiler_params=_params(("parallel", "parallel")),
    )(c_idx, g, r)


def _scatter_to_chips(p):
    n, half, cols = p.shape

    def body(p_ref, q_ref, send_sems, recv_sems, local_sem):
        x, y, c, chips = _place()
        k = 2 * x + y
        mine = pltpu.make_async_copy(p_ref.at[k], q_ref.at[k], local_sem)
        mine.start()
        sends = []
        for i, (cx, cy) in enumerate(chips):
            cp = pltpu.make_async_remote_copy(
                src_ref=p_ref.at[2 * cx + cy], dst_ref=q_ref.at[k], send_sem=send_sems.at[i],
                recv_sem=recv_sems.at[i], device_id=(cx, cy, c), device_id_type=MESH)
            cp.start()
            sends.append(cp)
        for i, (cx, cy) in enumerate(chips):
            kk = 2 * cx + cy
            pltpu.make_async_remote_copy(
                src_ref=p_ref.at[kk], dst_ref=q_ref.at[kk], send_sem=send_sems.at[i],
                recv_sem=recv_sems.at[i], device_id=(cx, cy, c), device_id_type=MESH).wait_recv()
        for cp in sends:
            cp.wait_send()
        mine.wait()

    return pl.pallas_call(
        body, name="grads_scatter_to_chips",
        in_specs=[HBM_SPEC], out_specs=HBM_SPEC,
        out_shape=jax.ShapeDtypeStruct((n, half, cols), p.dtype),
        scratch_shapes=[pltpu.SemaphoreType.DMA((3,)), pltpu.SemaphoreType.DMA((3,)),
                        pltpu.SemaphoreType.DMA],
    )(p)


def _sum_chips(q):
    n, half, cols = q.shape
    tr = min(256, half)

    def body(q_ref, o_ref):
        o_ref[...] = ((q_ref[0] + q_ref[1]) + q_ref[2]) + q_ref[3]

    return pl.pallas_call(
        body, name="grads_sum_chips", grid=(half // tr,),
        in_specs=[pl.BlockSpec((n, tr, cols), lambda t: (0, t, 0))],
        out_specs=pl.BlockSpec((tr, cols), lambda t: (t, 0)),
        out_shape=jax.ShapeDtypeStruct((half, cols), F32),
        compiler_params=_params(("parallel",)),
    )(q)


def _join_halves(o_half):
    half, cols = o_half.shape

    def body(h_ref, o_ref, send_sem, recv_sem, local_sem):
        x, y, c, _ = _place()
        dst = o_ref.at[pl.ds(c * half, half), :]
        mine = pltpu.make_async_copy(h_ref, dst, local_sem)
        mine.start()
        cp = pltpu.make_async_remote_copy(src_ref=h_ref, dst_ref=dst, send_sem=send_sem,
                                          recv_sem=recv_sem, device_id=(x, y, 1 - c),
                                          device_id_type=MESH)
        cp.start()
        other = o_ref.at[pl.ds((1 - c) * half, half), :]
        pltpu.make_async_remote_copy(src_ref=h_ref, dst_ref=other, send_sem=send_sem,
                                     recv_sem=recv_sem, device_id=(x, y, 1 - c),
                                     device_id_type=MESH).wait_recv()
        cp.wait_send()
        mine.wait()

    return pl.pallas_call(
        body, name="grads_join_halves",
        in_specs=[HBM_SPEC], out_specs=HBM_SPEC,
        out_shape=jax.ShapeDtypeStruct((2 * half, cols), o_half.dtype),
        scratch_shapes=[pltpu.SemaphoreType.DMA, pltpu.SemaphoreType.DMA, pltpu.SemaphoreType.DMA],
    )(o_half)


def _pack(arrays, dtype):
    flat = jnp.concatenate([a.reshape(-1).astype(dtype) for a in arrays])
    quantum = PACK_COLS * PACK_ROW_ALIGN
    total = -(-flat.shape[0] // quantum) * quantum
    flat = jnp.pad(flat, (0, total - flat.shape[0]))
    return flat.reshape(total // PACK_COLS, PACK_COLS)


def _unpack(buf, shapes):
    flat = buf.reshape(buf.shape[:-2] + (-1,))
    out, off = [], 0
    for sh in shapes:
        size = 1
        for d in sh:
            size *= d
        out.append(flat[..., off:off + size].reshape(buf.shape[:-2] + tuple(sh)))
        off += size
    return out


def _permute_w_in(w):
    z = lambda n: jnp.zeros((w.shape[0], n), w.dtype)
    return jnp.concatenate([
        w[:, 928:1440], w[:, 1440:1952], w[:, 1952:2464],
        w[:, 416:928], w[:, 2464:2976], w[:, 3744:4256],
        w[:, 2976:3488],
        w[:, 0:256], w[:, 256:384], z(64), w[:, 384:416], z(32),
        w[:, 3488:3616], w[:, 3616:3744]], axis=1)


def _unpermute_dw_in(d):
    return jnp.concatenate([
        d[:, 3584:3840], d[:, 3840:3968], d[:, 4032:4064], d[:, 1536:2048],
        d[:, 0:512], d[:, 512:1024], d[:, 1024:1536], d[:, 2048:2560],
        d[:, 3072:3584], d[:, 4096:4224], d[:, 4224:4352], d[:, 2560:3072]], axis=1)


def _rope_tables(s):
    half = MLA_ROPE // 2
    inv_freq = jnp.power(jnp.float32(ROPE_THETA), -jnp.arange(half, dtype=F32) / half)
    ang = jnp.arange(s, dtype=F32)[:, None] * inv_freq[None, :]
    cos, sin = jnp.cos(ang), jnp.sin(ang)
    z = lambda n: jnp.zeros((s, n), F32)
    c = jnp.concatenate([jnp.ones((s, MLA_NOPE), F32), cos, cos, z(32)], axis=1)
    s1 = jnp.concatenate([z(MLA_NOPE), -sin, z(16), z(32)], axis=1)
    s2 = jnp.concatenate([z(MLA_NOPE), z(16), sin, z(32)], axis=1)
    return c, s1, s2


def _pad_lanes(a, n):
    return jnp.pad(a, ((0, 0), (0, n - a.shape[1])))


SHARDED = ("w_in", "w_out", "mla_w_qb", "mla_w_kvb", "conv_w")
REPLICATED = ("norm_g", "mla_q_a_norm", "mla_kv_a_norm", "mla_q_norm", "mla_k_norm",
              "swa_q_norm", "swa_k_norm", "swa_sinks")
WEIGHT_ORDER = ("norm_g", "w_in", "mla_q_a_norm", "mla_w_qb", "mla_kv_a_norm", "mla_w_kvb",
                "mla_q_norm", "mla_k_norm", "conv_w", "swa_q_norm", "swa_k_norm", "swa_sinks", "w_out")
SHARD_AXIS = {"w_in": 2, "w_out": 1, "mla_w_qb": 2, "mla_w_kvb": 2, "conv_w": 2}


def kernel(x, norm_g, w_in, mla_q_a_norm, mla_w_qb, mla_kv_a_norm, mla_w_kvb, mla_q_norm, mla_k_norm, conv_w, swa_q_norm, swa_k_norm, swa_sinks, w_out, loss_target, m_norm_g, m_w_in, m_mla_q_a_norm, m_mla_w_qb, m_mla_kv_a_norm, m_mla_w_kvb, m_mla_q_norm, m_mla_k_norm, m_conv_w, m_swa_q_norm, m_swa_k_norm, m_swa_sinks, m_w_out, v_norm_g, v_w_in, v_mla_q_a_norm, v_mla_w_qb, v_mla_kv_a_norm, v_mla_w_kvb, v_mla_q_norm, v_mla_k_norm, v_conv_w, v_swa_q_norm, v_swa_k_norm, v_swa_sinks, v_w_out):
    weights = dict(norm_g=norm_g, w_in=w_in, mla_q_a_norm=mla_q_a_norm, mla_w_qb=mla_w_qb,
                   mla_kv_a_norm=mla_kv_a_norm, mla_w_kvb=mla_w_kvb, mla_q_norm=mla_q_norm,
                   mla_k_norm=mla_k_norm, conv_w=conv_w, swa_q_norm=swa_q_norm,
                   swa_k_norm=swa_k_norm, swa_sinks=swa_sinks, w_out=w_out)
    mom_m = dict(norm_g=m_norm_g, w_in=m_w_in, mla_q_a_norm=m_mla_q_a_norm, mla_w_qb=m_mla_w_qb,
                 mla_kv_a_norm=m_mla_kv_a_norm, mla_w_kvb=m_mla_w_kvb, mla_q_norm=m_mla_q_norm,
                 mla_k_norm=m_mla_k_norm, conv_w=m_conv_w, swa_q_norm=m_swa_q_norm,
                 swa_k_norm=m_swa_k_norm, swa_sinks=m_swa_sinks, w_out=m_w_out)
    mom_v = dict(norm_g=v_norm_g, w_in=v_w_in, mla_q_a_norm=v_mla_q_a_norm, mla_w_qb=v_mla_w_qb,
                 mla_kv_a_norm=v_mla_kv_a_norm, mla_w_kvb=v_mla_w_kvb, mla_q_norm=v_mla_q_norm,
                 mla_k_norm=v_mla_k_norm, conv_w=v_conv_w, swa_q_norm=v_swa_q_norm,
                 swa_k_norm=v_swa_k_norm, swa_sinks=v_swa_sinks, w_out=v_w_out)
    xs = x[0]
    target = loss_target[0]
    s = xs.shape[0]
    c_idx = lax.axis_index("c").astype(jnp.int32).reshape(1)

    conv_bits = lax.bitcast_convert_type(conv_w, BF16)
    gather_list = [w_in, w_out, mla_w_qb, mla_w_kvb, conv_bits]
    gathered = _all_gather(_pack(gather_list, BF16))
    parts = _unpack(gathered, [a.shape for a in gather_list])
    join = lambda p, axis: jnp.concatenate([p[k] for k in range(N_CHIPS)], axis=axis)
    w_in_full = join(parts[0], 2)
    w_out_full = join(parts[1], 1)
    w_qb_full = join(parts[2], 2)
    w_kvb_full = join(parts[3], 2)
    conv_full = lax.bitcast_convert_type(join(parts[4], 2), F32)

    rope = _rope_tables(s)
    layers = []
    for l in range(DEPTH):
        wq = jnp.pad(w_qb_full[l].reshape(MLA_Q_LORA, MLA_HEADS, MLA_QK),
                     ((0, 0), (0, 0), (0, LANES - MLA_QK))).reshape(MLA_Q_LORA, MLA_HEADS * LANES)
        kv = w_kvb_full[l].reshape(MLA_KV_LORA, MLA_HEADS, MLA_NOPE + MLA_V)
        wk = jnp.pad(kv[:, :, :MLA_NOPE], ((0, 0), (0, 0), (0, LANES - MLA_NOPE)))
        wkv = jnp.concatenate([wk.reshape(MLA_KV_LORA, MLA_HEADS * LANES),
                               kv[:, :, MLA_NOPE:].reshape(MLA_KV_LORA, MLA_HEADS * MLA_V)], axis=1)
        layers.append(dict(
            w_in=_permute_w_in(w_in_full[l]), w_out=w_out_full[l], wq=wq, wkv=wkv,
            cw=jnp.pad(conv_full[l], ((0, HALO - 3), (0, 0))),
            g=norm_g[l][None], qan=mla_q_a_norm[l][None], kvan=mla_kv_a_norm[l][None],
            qn=_pad_lanes(mla_q_norm[l][None], LANES), kn=_pad_lanes(mla_k_norm[l][None], LANES),
            sqn=swa_q_norm[l][None], skn=swa_k_norm[l][None], sinks=swa_sinks[l][None]))

    saved = []
    h_in = xs
    for l in range(DEPTH):
        p = layers[l]
        proj, hb = _in_proj_fwd(h_in, p["g"], p["w_in"])
        q, k, v, vt = _mla_prep_fwd(proj, p["qan"], p["kvan"], p["qn"], p["kn"], p["wq"], p["wkv"], rope)
        o_mla, lse = _mla_attn_fwd(q, k, vt)
        o_swa = _swa_fwd(proj, p["sqn"], p["skn"], p["sinks"])
        y, z = _out_fwd(proj, o_mla, o_swa, h_in, p["w_out"], p["cw"])
        saved.append(dict(x=h_in, proj=proj, hb=hb, q=q, k=k, v=v, o_mla=o_mla, lse=lse,
                          o_swa=o_swa, z=z))
        h_in = y

    dy, loss_acc = _loss_head(h_in, target)
    loss = lax.psum(loss_acc[0, 0], ("x", "y", "c"))

    grads = {n: [None] * DEPTH for n in WEIGHT_ORDER}
    for l in reversed(range(DEPTH)):
        p, a = layers[l], saved[l]
        dconv, dgates, do_mla, delta, do_swa, dcw = _out_bwd(dy, a["proj"], a["o_mla"], a["o_swa"],
                                                             p["w_out"], p["cw"])
        grads["w_out"][l] = _matmul_tn(a["z"], dy, "dw_out")
        grads["conv_w"][l] = dcw[0:3]
        delta_rows = jnp.transpose(delta, (1, 0)).reshape(MLA_HEADS // 2, 2, s)
        dq, dk, dv = _mla_attn_bwd(a["q"], a["k"], a["v"], do_mla, a["lse"], delta_rows)
        dmla, dqan, dkvan, dqn, dkn, dwq, dwkv = _mla_prep_bwd(
            a["proj"], p["qan"], p["kvan"], p["qn"], p["kn"], p["wq"], p["wkv"], rope, dq, dk, dv)
        dsq, dskv, dsqn, dskn, dsinks = _swa_bwd(a["proj"], p["sqn"], p["skn"], p["sinks"], do_swa)
        pieces = [dconv, dgates, dsq, dmla, dskv]
        dx, dg = _in_proj_bwd(pieces, a["x"], p["g"], p["w_in"], dy)
        dw_in = jnp.concatenate([_matmul_tn(a["hb"], pc, "dw_in_%d" % n)
                                 for n, pc in enumerate(pieces)], axis=1)
        grads["w_in"][l] = _unpermute_dw_in(dw_in)
        grads["norm_g"][l] = dg[0]
        grads["mla_q_a_norm"][l] = dqan[0]
        grads["mla_kv_a_norm"][l] = dkvan[0]
        grads["mla_q_norm"][l] = dqn[0, :MLA_QK]
        grads["mla_k_norm"][l] = dkn[0, :MLA_QK]
        grads["mla_w_qb"][l] = dwq.reshape(MLA_Q_LORA, MLA_HEADS, LANES)[:, :, :MLA_QK].reshape(
            MLA_Q_LORA, MLA_HEADS * MLA_QK)
        dwk = dwkv[:, :MLA_HEADS * LANES].reshape(MLA_KV_LORA, MLA_HEADS, LANES)[:, :, :MLA_NOPE]
        dwv = dwkv[:, MLA_HEADS * LANES:].reshape(MLA_KV_LORA, MLA_HEADS, MLA_V)
        grads["mla_w_kvb"][l] = jnp.concatenate([dwk, dwv], axis=2).reshape(
            MLA_KV_LORA, MLA_HEADS * (MLA_NOPE + MLA_V))
        grads["swa_q_norm"][l] = dsqn[0]
        grads["swa_k_norm"][l] = dskn[0]
        grads["swa_sinks"][l] = dsinks[0]
        dy = dx
    grad_x = dy[None]
    full_grads = {n: jnp.stack(grads[n]) for n in WEIGHT_ORDER}

    names = SHARDED + REPLICATED

    def chunk_list(k):
        out = []
        for n in SHARDED:
            g = full_grads[n]
            width = g.shape[SHARD_AXIS[n]] // N_CHIPS
            out.append(lax.slice_in_dim(g, k * width, (k + 1) * width, axis=SHARD_AXIS[n]))
        return out + [full_grads[n] for n in REPLICATED]

    g_all = jnp.stack([_pack(chunk_list(k), F32) for k in range(N_CHIPS)])
    from_sibling = _swap_halves_to_sibling(g_all)
    partial = _add_sibling(g_all, from_sibling, c_idx)
    by_chip = _scatter_to_chips(partial)
    g_mine = _join_halves(_sum_chips(by_chip))

    d_buf, m_buf, v_buf = _adamw(g_mine, _pack([weights[n] for n in names], F32),
                                 _pack([mom_m[n] for n in names], F32),
                                 _pack([mom_v[n] for n in names], F32))
    shapes = [weights[n].shape for n in names]
    unpacked = [dict(zip(names, _unpack(buf, shapes))) for buf in (g_mine, d_buf, m_buf, v_buf)]
    outs = [loss, grad_x]
    for group in unpacked:
        outs += [group[n] for n in WEIGHT_ORDER]
    return tuple(outs)
```

```python
import functools

import jax
import jax.numpy as jnp
from jax import lax
from jax.experimental import pallas as pl
from jax.experimental.pallas import tpu as pltpu

F32 = jnp.float32
BF16 = jnp.bfloat16

D_MODEL = 1024
DEPTH = 2
GROUP = 512
D_MIX = 3 * GROUP
BLOCK = 128
RMS_EPS = 1e-6
NEG_INF = -1e30
MLA_HEADS = 8
MLA_QK = 96
MLA_NOPE = 64
MLA_ROPE = 32
MLA_V = 64
MLA_Q_LORA = 256
MLA_KV_LORA = 128
ROPE_THETA = 10000.0
SWA_HEADS = 8
SWA_KV = 2
SWA_GROUP = 4
SWA_DIM = 64
IN_COLS = 4256
N_CHIPS = 4

NC = 4352
OFF_CONV, OFF_GATES, OFF_SQ, OFF_MLA, OFF_SKV = 0, 1536, 3072, 3584, 4096
PIECE_WIDTHS = (1536, 1536, 512, 512, 256)

VMEM_LIMIT = 56 * 1024 * 1024
LANES = 128
PACK_COLS = 1024
PACK_ROW_ALIGN = 16
PACK_ROWS = 1024

ADAM_LR = 0.001
ADAM_B1 = 0.9
ADAM_B2 = 0.999
ADAM_EPS = 1e-08
ADAM_WD = 0.01
ADAM_STEP = 10

MESH = pl.DeviceIdType.MESH


def _params(sem, vmem=VMEM_LIMIT):
    return pltpu.CompilerParams(dimension_semantics=sem, vmem_limit_bytes=vmem)


def _dot(a, b, dims):
    return lax.dot_general(a.astype(BF16), b.astype(BF16), (dims, ((), ())),
                           preferred_element_type=F32)


def _mm(a, b):
    return _dot(a, b, ((1,), (0,)))


def _mm_nt(a, b):
    return _dot(a, b, ((1,), (1,)))


def _mm_tn(a, b):
    return _dot(a, b, ((0,), (0,)))


@jax.custom_vjp
def _mmd(a, b):
    return _mm(a, b)


def _mmd_fwd(a, b):
    return _mm(a, b), (a, b)


def _mmd_bwd(res, g):
    a, b = res
    return _mm_nt(g, b), _mm_tn(a, g)


_mmd.defvjp(_mmd_fwd, _mmd_bwd)


@jax.custom_vjp
def _mmd_nt(a, b):
    return _mm_nt(a, b)


def _mmd_nt_fwd(a, b):
    return _mm_nt(a, b), (a, b)


def _mmd_nt_bwd(res, g):
    a, b = res
    return _mm(g, b), _mm_tn(g, a)


_mmd_nt.defvjp(_mmd_nt_fwd, _mmd_nt_bwd)


def _rms(x, g, n=None):
    n = x.shape[-1] if n is None else n
    ms = jnp.sum(x * x, axis=-1, keepdims=True) * (1.0 / n)
    return x * lax.rsqrt(ms + RMS_EPS) * g


def _sigmoid(x):
    return 1.0 / (1.0 + jnp.exp(-x))


@jax.custom_vjp
def _rope(t, c, s1, s2):
    return t * c + pltpu.roll(t, LANES - 16, 1) * s1 + pltpu.roll(t, 16, 1) * s2


def _rope_fwd(t, c, s1, s2):
    return _rope(t, c, s1, s2), (c, s1, s2)


def _rope_bwd(res, g):
    c, s1, s2 = res
    dt = g * c + pltpu.roll(g * s1, 16, 1) + pltpu.roll(g * s2, LANES - 16, 1)
    return dt, jnp.zeros_like(c), jnp.zeros_like(s1), jnp.zeros_like(s2)


_rope.defvjp(_rope_fwd, _rope_bwd)


def _in_proj_fwd(x, g, w):
    s = x.shape[0]
    tm = min(256, s)

    def body(x_ref, g_ref, w_ref, proj_ref, hb_ref):
        hb = _rms(x_ref[...], g_ref[...]).astype(BF16)
        hb_ref[...] = hb
        proj_ref[...] = jnp.dot(hb, w_ref[...], preferred_element_type=F32)

    return pl.pallas_call(
        body, name="in_proj_fwd", grid=(s // tm,),
        in_specs=[pl.BlockSpec((tm, D_MODEL), lambda i: (i, 0)),
                  pl.BlockSpec((1, D_MODEL), lambda i: (0, 0)),
                  pl.BlockSpec((D_MODEL, NC), lambda i: (0, 0))],
        out_specs=[pl.BlockSpec((tm, NC), lambda i: (i, 0)),
                   pl.BlockSpec((tm, D_MODEL), lambda i: (i, 0))],
        out_shape=[jax.ShapeDtypeStruct((s, NC), F32), jax.ShapeDtypeStruct((s, D_MODEL), BF16)],
        compiler_params=_params(("parallel",)),
    )(x, g, w)


def _in_proj_bwd(pieces, x, g, w, dres):
    s = x.shape[0]
    tm = min(256, s)
    n_p = len(pieces)

    def body(*refs):
        p_refs = refs[:n_p]
        x_ref, g_ref, w_ref, dres_ref, dx_ref, dg_ref = refs[n_p:]
        dh = None
        off = 0
        for r in p_refs:
            width = r.shape[1]
            t = _mm_nt(r[...], w_ref[:, off:off + width])
            dh = t if dh is None else dh + t
            off += width
        _, vjp = jax.vjp(_rms, x_ref[...], g_ref[...])
        dx, dg = vjp(dh)
        dx_ref[...] = dx + dres_ref[...]

        @pl.when(pl.program_id(0) == 0)
        def _():
            dg_ref[...] = jnp.zeros_like(dg_ref)

        dg_ref[...] += dg

    in_specs = [pl.BlockSpec((tm, p.shape[1]), lambda i: (i, 0)) for p in pieces]
    in_specs += [pl.BlockSpec((tm, D_MODEL), lambda i: (i, 0)),
                 pl.BlockSpec((1, D_MODEL), lambda i: (0, 0)),
                 pl.BlockSpec((D_MODEL, NC), lambda i: (0, 0)),
                 pl.BlockSpec((tm, D_MODEL), lambda i: (i, 0))]
    return pl.pallas_call(
        body, name="in_proj_bwd", grid=(s // tm,),
        in_specs=in_specs,
        out_specs=[pl.BlockSpec((tm, D_MODEL), lambda i: (i, 0)),
                   pl.BlockSpec((1, D_MODEL), lambda i: (0, 0))],
        out_shape=[jax.ShapeDtypeStruct((s, D_MODEL), F32), jax.ShapeDtypeStruct((1, D_MODEL), F32)],
        compiler_params=_params(("arbitrary",)),
    )(*pieces, x, g, w, dres)


def _matmul_tn(a, b, name):
    s, m = a.shape
    n = b.shape[1]
    tk = min(512, s)
    tn = min(512, n)

    def body(a_ref, b_ref, o_ref):
        @pl.when(pl.program_id(1) == 0)
        def _():
            o_ref[...] = jnp.zeros_like(o_ref)

        o_ref[...] += _mm_tn(a_ref[...], b_ref[...])

    return pl.pallas_call(
        body, name=name, grid=(n // tn, s // tk),
        in_specs=[pl.BlockSpec((tk, m), lambda j, k: (k, 0)),
                  pl.BlockSpec((tk, tn), lambda j, k: (k, j))],
        out_specs=pl.BlockSpec((m, tn), lambda j, k: (0, j)),
        out_shape=jax.ShapeDtypeStruct((m, n), F32),
        compiler_params=_params(("parallel", "arbitrary")),
    )(a, b)


def _prep_fn(q_lat, kv_lat, kr, qan, kvan, qn, kn, wq, wk, wv, c, s1, s2, mm):
    rq = _rms(q_lat, qan)
    rkv = _rms(kv_lat, kvan)
    qs, ks = [], []
    for h in range(MLA_HEADS):
        qs.append(_rope(_rms(mm(rq, wq[h]), qn, MLA_QK), c, s1, s2))
        ks.append(_rope(_rms(mm(rkv, wk[h]) + kr, kn, MLA_QK), c, s1, s2))
    return tuple(qs), tuple(ks), mm(rkv, wv)


def _prep_weights(wq_ref, wkv_ref):
    wq = tuple(wq_ref[:, LANES * h:LANES * (h + 1)].astype(F32) for h in range(MLA_HEADS))
    wk = tuple(wkv_ref[:, LANES * h:LANES * (h + 1)].astype(F32) for h in range(MLA_HEADS))
    wv = wkv_ref[:, LANES * MLA_HEADS:].astype(F32)
    return wq, wk, wv


def _prep_in_specs(tm):
    const = lambda shape: pl.BlockSpec(shape, lambda i: (0, 0))
    return [pl.BlockSpec((tm, 512), lambda i: (i, OFF_MLA // 512)),
            const((1, MLA_Q_LORA)), const((1, MLA_KV_LORA)), const((1, LANES)), const((1, LANES)),
            const((MLA_Q_LORA, 1024)), const((MLA_KV_LORA, 1536)),
            pl.BlockSpec((tm, LANES), lambda i: (i, 0)),
            pl.BlockSpec((tm, LANES), lambda i: (i, 0)),
            pl.BlockSpec((tm, LANES), lambda i: (i, 0))]


def _mla_prep_fwd(proj, qan, kvan, qn, kn, wq, wkv, rope):
    s = proj.shape[0]
    tm = min(512, s)

    def body(blk_ref, qan_ref, kvan_ref, qn_ref, kn_ref, wq_ref, wkv_ref, c_ref, s1_ref, s2_ref,
             q_ref, k_ref, v_ref, vt_ref):
        wq_h, wk_h, wv = _prep_weights(wq_ref, wkv_ref)
        qs, ks, v = _prep_fn(blk_ref[:, 0:256], blk_ref[:, 256:384], blk_ref[:, 384:512],
                             qan_ref[...], kvan_ref[...], qn_ref[...], kn_ref[...],
                             wq_h, wk_h, wv, c_ref[...], s1_ref[...], s2_ref[...], _mm)
        for h in range(MLA_HEADS):
            q_ref[:, LANES * h:LANES * (h + 1)] = (qs[h] * Q_PRESCALE).astype(BF16)
            k_ref[:, LANES * h:LANES * (h + 1)] = ks[h].astype(BF16)
        v_ref[...] = v.astype(BF16)
        vt_ref[...] = jnp.transpose(v).astype(BF16)

    return pl.pallas_call(
        body, name="mla_prep_fwd", grid=(s // tm,),
        in_specs=_prep_in_specs(tm),
        out_specs=[pl.BlockSpec((tm, 1024), lambda i: (i, 0)),
                   pl.BlockSpec((tm, 1024), lambda i: (i, 0)),
                   pl.BlockSpec((tm, 512), lambda i: (i, 0)),
                   pl.BlockSpec((512, tm), lambda i: (0, i))],
        out_shape=[jax.ShapeDtypeStruct((s, 1024), BF16), jax.ShapeDtypeStruct((s, 1024), BF16),
                   jax.ShapeDtypeStruct((s, 512), BF16), jax.ShapeDtypeStruct((512, s), BF16)],
        compiler_params=_params(("parallel",)),
    )(proj, qan, kvan, qn, kn, wq, wkv, *rope)


def _mla_prep_bwd(proj, qan, kvan, qn, kn, wq, wkv, rope, dq, dk, dv):
    s = proj.shape[0]
    tm = min(256, s)

    def body(blk_ref, qan_ref, kvan_ref, qn_ref, kn_ref, wq_ref, wkv_ref, c_ref, s1_ref, s2_ref,
             dq_ref, dk_ref, dv_ref,
             dblk_ref, dqan_ref, dkvan_ref, dqn_ref, dkn_ref, dwq_ref, dwkv_ref):
        wq_h, wk_h, wv = _prep_weights(wq_ref, wkv_ref)
        c, s1, s2 = c_ref[...], s1_ref[...], s2_ref[...]

        def fn(q_lat, kv_lat, kr, qan_, kvan_, qn_, kn_, wq_, wk_, wv_):
            return _prep_fn(q_lat, kv_lat, kr, qan_, kvan_, qn_, kn_, wq_, wk_, wv_, c, s1, s2, _mmd)

        _, vjp = jax.vjp(fn, blk_ref[:, 0:256], blk_ref[:, 256:384], blk_ref[:, 384:512],
                         qan_ref[...], kvan_ref[...], qn_ref[...], kn_ref[...], wq_h, wk_h, wv)
        cts = (tuple(dq_ref[:, LANES * h:LANES * (h + 1)] for h in range(MLA_HEADS)),
               tuple(dk_ref[:, LANES * h:LANES * (h + 1)] for h in range(MLA_HEADS)),
               dv_ref[...])
        dq_lat, dkv_lat, dkr, dqan, dkvan, dqn, dkn, dwq_h, dwk_h, dwv = vjp(cts)
        dblk_ref[:, 0:256] = dq_lat
        dblk_ref[:, 256:384] = dkv_lat
        dblk_ref[:, 384:512] = dkr

        @pl.when(pl.program_id(0) == 0)
        def _():
            for r in (dqan_ref, dkvan_ref, dqn_ref, dkn_ref, dwq_ref, dwkv_ref):
                r[...] = jnp.zeros_like(r)

        dqan_ref[...] += dqan
        dkvan_ref[...] += dkvan
        dqn_ref[...] += dqn
        dkn_ref[...] += dkn
        for h in range(MLA_HEADS):
            dwq_ref[:, LANES * h:LANES * (h + 1)] += dwq_h[h]
            dwkv_ref[:, LANES * h:LANES * (h + 1)] += dwk_h[h]
        dwkv_ref[:, LANES * MLA_HEADS:] += dwv

    const = lambda shape: pl.BlockSpec(shape, lambda i: (0, 0))
    row = lambda width: pl.BlockSpec((tm, width), lambda i: (i, 0))
    shapes = [(s, 512), (1, MLA_Q_LORA), (1, MLA_KV_LORA), (1, LANES), (1, LANES),
              (MLA_Q_LORA, 1024), (MLA_KV_LORA, 1536)]
    return pl.pallas_call(
        body, name="mla_prep_bwd", grid=(s // tm,),
        in_specs=_prep_in_specs(tm) + [row(1024), row(1024), row(512)],
        out_specs=[row(512)] + [const(sh) for sh in shapes[1:]],
        out_shape=[jax.ShapeDtypeStruct(sh, F32) for sh in shapes],
        compiler_params=_params(("arbitrary",)),
    )(proj, qan, kvan, qn, kn, wq, wkv, *rope, dq, dk, dv)


MLA_SCALE = MLA_QK ** -0.5


LOG2E = 1.4426950408889634
LN2 = 0.6931471805599453
Q_PRESCALE = MLA_SCALE * LOG2E


def _mla_attn_fwd(q2, k, vt):
    s = q2.shape[0]
    t = min(256, s)
    nq = s // t

    def body(q_ref, k_ref, vt_ref, o_ref, lse_ref, acc_ref):
        i = pl.program_id(1)
        row = lax.broadcasted_iota(jnp.int32, (t, t), 0)
        col = lax.broadcasted_iota(jnp.int32, (t, t), 1)
        causal_t = row <= col
        qh = [q_ref[:, LANES * hh:LANES * (hh + 1)] for hh in range(2)]
        acc_ref[...] = jnp.zeros_like(acc_ref)

        def scores(j, masked):
            r0 = pl.multiple_of(j * t, t)
            out = []
            for hh in range(2):
                kc = k_ref[pl.ds(r0, t), LANES * hh:LANES * (hh + 1)]
                sc = lax.dot_general(kc, qh[hh], (((1,), (1,)), ((), ())),
                                     preferred_element_type=F32)
                out.append(jnp.where(causal_t, sc, NEG_INF) if masked else sc)
            return tuple(out)

        def consume(j, scs, stats):
            r0 = pl.multiple_of(j * t, t)
            out, ps, alphas = [], [], []
            for hh in range(2):
                m, l = stats[hh]
                m_new = jnp.maximum(m, jnp.max(scs[hh], axis=0, keepdims=True))
                p = jnp.exp2(scs[hh] - m_new)
                alpha = jnp.exp2(m - m_new)
                out.append((m_new, alpha * l + jnp.sum(p, axis=0, keepdims=True)))
                ps.append(p.astype(BF16))
                alphas.append(alpha)
            for hh in range(2):
                vc = vt_ref[MLA_V * hh:MLA_V * (hh + 1), pl.ds(r0, t)]
                acc_ref[hh] = alphas[hh] * acc_ref[hh] + jnp.dot(vc, ps[hh],
                                                                 preferred_element_type=F32)
            return tuple(out)

        def trip(j, carry):
            scs, stats = carry
            nxt = scores(j, False)
            stats = consume(jnp.where(j == 0, i, j - 1), scs, stats)
            return nxt, stats

        init = tuple((jnp.full((1, t), NEG_INF, F32), jnp.zeros((1, t), F32)) for _ in range(2))
        scs, stats = lax.fori_loop(0, i, trip, (scores(i, True), init))
        stats = consume(jnp.where(i == 0, i, i - 1), scs, stats)
        for hh in range(2):
            m, l = stats[hh]
            o_ref[:, MLA_V * hh:MLA_V * (hh + 1)] = jnp.transpose(acc_ref[hh] / l)
            lse_ref[0, hh:hh + 1, :] = m + jnp.log2(l)

    return pl.pallas_call(
        body, name="mla_attn_fwd", grid=(MLA_HEADS // 2, nq),
        in_specs=[pl.BlockSpec((t, 256), lambda p, i: (i, p)),
                  pl.BlockSpec((s, 256), lambda p, i: (0, p)),
                  pl.BlockSpec((128, s), lambda p, i: (p, 0))],
        out_specs=[pl.BlockSpec((t, 128), lambda p, i: (i, p)),
                   pl.BlockSpec((1, 2, t), lambda p, i: (p, 0, i))],
        out_shape=[jax.ShapeDtypeStruct((s, 512), F32),
                   jax.ShapeDtypeStruct((MLA_HEADS // 2, 2, s), F32)],
        scratch_shapes=[pltpu.VMEM((2, MLA_V, t), F32)],
        compiler_params=_params(("parallel", "arbitrary")),
    )(q2, k, vt)


def _mla_attn_bwd(q2, k, v, do, lse_rows, delta_rows):
    s = q2.shape[0]
    t = min(256, s)
    nq = s // t

    def body(q_ref, k_ref, v_ref, do_ref, lse_ref, dl_ref, dq_ref, dk_ref, dv_ref):
        j = pl.program_id(1)

        @pl.when(j == 0)
        def _():
            dq_ref[...] = jnp.zeros_like(dq_ref)

        dk_ref[...] = jnp.zeros_like(dk_ref)
        dv_ref[...] = jnp.zeros_like(dv_ref)
        row = lax.broadcasted_iota(jnp.int32, (t, t), 0)
        col = lax.broadcasted_iota(jnp.int32, (t, t), 1)
        causal_t = row <= col
        kh = [k_ref[:, LANES * hh:LANES * (hh + 1)] for hh in range(2)]
        vh = [v_ref[:, MLA_V * hh:MLA_V * (hh + 1)] for hh in range(2)]

        def load_q_do(i):
            r0 = pl.multiple_of(i * t, t)
            qhs = [q_ref[pl.ds(r0, t), LANES * hh:LANES * (hh + 1)] for hh in range(2)]
            dohs = [do_ref[pl.ds(r0, t), MLA_V * hh:MLA_V * (hh + 1)] for hh in range(2)]
            return r0, qhs, dohs

        def scores(i, masked):
            _, qhs, dohs = load_q_do(i)
            out = []
            for hh in range(2):
                sc_t = lax.dot_general(kh[hh], qhs[hh], (((1,), (1,)), ((), ())),
                                       preferred_element_type=F32)
                out.append(jnp.where(causal_t, sc_t, NEG_INF) if masked else sc_t)
                out.append(lax.dot_general(vh[hh], dohs[hh], (((1,), (1,)), ((), ())),
                                           preferred_element_type=F32))
            return tuple(out)

        def consume(i, sd):
            r0, qhs, dohs = load_q_do(i)
            pts, gts = [], []
            for hh in range(2):
                lse = lse_ref[0, hh:hh + 1, pl.ds(r0, t)]
                dl = dl_ref[0, hh:hh + 1, pl.ds(r0, t)]
                p_t = jnp.exp2(sd[2 * hh] - lse)
                pts.append(p_t.astype(BF16))
                gts.append((p_t * (sd[2 * hh + 1] - dl)).astype(BF16))
            for hh in range(2):
                dv_ref[:, MLA_V * hh:MLA_V * (hh + 1)] += jnp.dot(
                    pts[hh], dohs[hh], preferred_element_type=F32)
                dk_ref[:, LANES * hh:LANES * (hh + 1)] += jnp.dot(gts[hh], qhs[hh],
                                                                  preferred_element_type=F32)
                dq_ref[pl.ds(r0, t), LANES * hh:LANES * (hh + 1)] += lax.dot_general(
                    gts[hh], kh[hh], (((0,), (0,)), ((), ())), preferred_element_type=F32)

        consume(j, scores(j, True))

        def trip(i, carry):
            consume(i, scores(i, False))
            return carry

        lax.fori_loop(j + 1, nq, trip, 0)
        dk_ref[...] = dk_ref[...] * LN2

        @pl.when(j == nq - 1)
        def _():
            dq_ref[...] = dq_ref[...] * MLA_SCALE

    return pl.pallas_call(
        body, name="mla_attn_bwd", grid=(MLA_HEADS // 2, nq),
        in_specs=[pl.BlockSpec((s, 256), lambda p, j: (0, p)),
                  pl.BlockSpec((t, 256), lambda p, j: (j, p)),
                  pl.BlockSpec((t, 128), lambda p, j: (j, p)),
                  pl.BlockSpec((s, 128), lambda p, j: (0, p)),
                  pl.BlockSpec((1, 2, s), lambda p, j: (p, 0, 0)),
                  pl.BlockSpec((1, 2, s), lambda p, j: (p, 0, 0))],
        out_specs=[pl.BlockSpec((s, 256), lambda p, j: (0, p)),
                   pl.BlockSpec((t, 256), lambda p, j: (j, p)),
                   pl.BlockSpec((t, 128), lambda p, j: (j, p))],
        out_shape=[jax.ShapeDtypeStruct((s, 1024), F32), jax.ShapeDtypeStruct((s, 1024), F32),
                   jax.ShapeDtypeStruct((s, 512), F32)],
        compiler_params=_params(("parallel", "arbitrary")),
    )(q2, k, v, do, lse_rows, delta_rows)


SWA_SCALE = SWA_DIM ** -0.5
SWA_ROWS = SWA_GROUP * BLOCK


def _swa_consts(first):
    r = lax.broadcasted_iota(jnp.int32, (SWA_ROWS, 2 * BLOCK), 0)
    kidx = lax.broadcasted_iota(jnp.int32, (SWA_ROWS, 2 * BLOCK), 1)
    dist = BLOCK + (r % BLOCK) - kidx
    valid = (dist >= 0) & (dist < BLOCK) & ((kidx >= BLOCK) | jnp.logical_not(first))
    grp = r // BLOCK
    distf = dist.astype(F32)
    rs = lax.broadcasted_iota(jnp.int32, (SWA_ROWS, SWA_HEADS), 0) // BLOCK
    hs = lax.broadcasted_iota(jnp.int32, (SWA_ROWS, SWA_HEADS), 1)
    biases, sels = [], []
    for j in range(SWA_KV):
        slope = jnp.zeros((SWA_ROWS, 2 * BLOCK), F32)
        for g in range(SWA_GROUP):
            slope = jnp.where(grp == g, 2.0 ** -(SWA_GROUP * j + g + 1), slope)
        biases.append(slope * distf)
        sels.append((hs == rs + SWA_GROUP * j).astype(F32))
    return valid, biases, sels


def _swa_block(qh, kprev, kcur, vprev, vcur, qn, kn, sinks, consts, mm_nt, mm):
    valid, biases, sels = consts
    outs = []
    for j in range(SWA_KV):
        kb = _rms(jnp.concatenate([kprev[j], kcur[j]], axis=0), kn)
        vb = jnp.concatenate([vprev[j], vcur[j]], axis=0)
        qs = jnp.concatenate([_rms(qh[SWA_GROUP * j + g], qn) for g in range(SWA_GROUP)], axis=0)
        sc = mm_nt(qs, kb) * SWA_SCALE - biases[j]
        sc = jnp.where(valid, sc, NEG_INF)
        sink = jnp.sum(sels[j] * sinks, axis=-1, keepdims=True)
        m = lax.stop_gradient(jnp.maximum(jnp.max(sc, axis=-1, keepdims=True), sink))
        e = jnp.exp(sc - m)
        den = jnp.sum(e, axis=-1, keepdims=True) + jnp.exp(sink - m)
        outs.append(mm(e / den, vb))
    return tuple(outs)


def _swa_load(sq_ref, skv_ref, halo_ref, b):
    r = slice(BLOCK * b, BLOCK * (b + 1))
    qh = tuple(sq_ref[r, SWA_DIM * h:SWA_DIM * (h + 1)] for h in range(SWA_HEADS))
    kcur = tuple(skv_ref[r, SWA_DIM * j:SWA_DIM * (j + 1)] for j in range(SWA_KV))
    vcur = tuple(skv_ref[r, 128 + SWA_DIM * j:128 + SWA_DIM * (j + 1)] for j in range(SWA_KV))
    if b == 0:
        src, rp = halo_ref, slice(0, BLOCK)
    else:
        src, rp = skv_ref, slice(BLOCK * (b - 1), BLOCK * b)
    kprev = tuple(src[rp, SWA_DIM * j:SWA_DIM * (j + 1)] for j in range(SWA_KV))
    vprev = tuple(src[rp, 128 + SWA_DIM * j:128 + SWA_DIM * (j + 1)] for j in range(SWA_KV))
    return qh, kprev, kcur, vprev, vcur


def _swa_fwd(proj, qn, kn, sinks):
    s = proj.shape[0]
    ts = min(512, s)
    nb = ts // BLOCK

    def body(sq_ref, skv_ref, halo_ref, qn_ref, kn_ref, sk_ref, o_ref):
        i = pl.program_id(0)
        for b in range(nb):
            first = (i == 0) if b == 0 else False
            consts = _swa_consts(jnp.asarray(first))
            outs = _swa_block(*_swa_load(sq_ref, skv_ref, halo_ref, b), qn_ref[...], kn_ref[...],
                              sk_ref[...], consts, _mm_nt, _mm)
            for j in range(SWA_KV):
                for g in range(SWA_GROUP):
                    h = SWA_GROUP * j + g
                    o_ref[BLOCK * b:BLOCK * (b + 1), SWA_DIM * h:SWA_DIM * (h + 1)] = (
                        outs[j][BLOCK * g:BLOCK * (g + 1)])

    const = lambda shape: pl.BlockSpec(shape, lambda i: (0, 0))
    return pl.pallas_call(
        body, name="swa_fwd", grid=(s // ts,),
        in_specs=[pl.BlockSpec((ts, 512), lambda i: (i, OFF_SQ // 512)),
                  pl.BlockSpec((ts, 256), lambda i: (i, OFF_SKV // 256)),
                  pl.BlockSpec((BLOCK, 256), lambda i: (jnp.maximum(i * nb - 1, 0), OFF_SKV // 256)),
                  const((1, SWA_DIM)), const((1, SWA_DIM)), const((1, SWA_HEADS))],
        out_specs=pl.BlockSpec((ts, 512), lambda i: (i, 0)),
        out_shape=jax.ShapeDtypeStruct((s, 512), F32),
        compiler_params=_params(("parallel",)),
    )(proj, proj, proj, qn, kn, sinks)


def _swa_bwd(proj, qn, kn, sinks, do):
    s = proj.shape[0]
    ts = min(512, s)
    nb = ts // BLOCK
    nt = s // ts

    def body(sq_ref, skv_ref, halo_ref, qn_ref, kn_ref, sk_ref, do_ref,
             dsq_ref, dskv_ref, dqn_ref, dkn_ref, dsk_ref, carry_ref):
        step = pl.program_id(0)
        i = nt - 1 - step

        @pl.when(step == 0)
        def _():
            carry_ref[...] = jnp.zeros_like(carry_ref)
            dqn_ref[...] = jnp.zeros_like(dqn_ref)
            dkn_ref[...] = jnp.zeros_like(dkn_ref)
            dsk_ref[...] = jnp.zeros_like(dsk_ref)

        dk_rows = [[None] * SWA_KV for _ in range(nb)]
        dv_rows = [[None] * SWA_KV for _ in range(nb)]
        halo_k, halo_v = None, None
        dqn = dkn = dsk = None

        def add(a, b_):
            return b_ if a is None else a + b_

        for b in range(nb):
            first = (i == 0) if b == 0 else False
            consts = _swa_consts(jnp.asarray(first))
            qh, kprev, kcur, vprev, vcur = _swa_load(sq_ref, skv_ref, halo_ref, b)

            def fn(qh_, kprev_, kcur_, vprev_, vcur_, qn_, kn_, sk_, consts=consts):
                return _swa_block(qh_, kprev_, kcur_, vprev_, vcur_, qn_, kn_, sk_, consts,
                                  _mmd_nt, _mmd)

            _, vjp = jax.vjp(fn, qh, kprev, kcur, vprev, vcur, qn_ref[...], kn_ref[...], sk_ref[...])
            r = slice(BLOCK * b, BLOCK * (b + 1))
            cts = tuple(
                jnp.concatenate([do_ref[r, SWA_DIM * (SWA_GROUP * j + g):SWA_DIM * (SWA_GROUP * j + g + 1)]
                                 for g in range(SWA_GROUP)], axis=0)
                for j in range(SWA_KV))
            dqh, dkprev, dkcur, dvprev, dvcur, dqn_b, dkn_b, dsk_b = vjp(cts)
            for h in range(SWA_HEADS):
                dsq_ref[r, SWA_DIM * h:SWA_DIM * (h + 1)] = dqh[h]
            for j in range(SWA_KV):
                dk_rows[b][j] = add(dk_rows[b][j], dkcur[j])
                dv_rows[b][j] = add(dv_rows[b][j], dvcur[j])
                if b > 0:
                    dk_rows[b - 1][j] = dk_rows[b - 1][j] + dkprev[j]
                    dv_rows[b - 1][j] = dv_rows[b - 1][j] + dvprev[j]
            if b == 0:
                halo_k, halo_v = dkprev, dvprev
            dqn, dkn, dsk = add(dqn, dqn_b), add(dkn, dkn_b), add(dsk, dsk_b)

        for b in range(nb):
            r = slice(BLOCK * b, BLOCK * (b + 1))
            for j in range(SWA_KV):
                dkv = dk_rows[b][j], dv_rows[b][j]
                for part in range(2):
                    c0 = 128 * part + SWA_DIM * j
                    val = dkv[part]
                    if b == nb - 1:
                        val = val + carry_ref[:, c0:c0 + SWA_DIM]
                    dskv_ref[r, c0:c0 + SWA_DIM] = val
        for j in range(SWA_KV):
            carry_ref[:, SWA_DIM * j:SWA_DIM * (j + 1)] = halo_k[j]
            carry_ref[:, 128 + SWA_DIM * j:128 + SWA_DIM * (j + 1)] = halo_v[j]
        dqn_ref[...] += dqn
        dkn_ref[...] += dkn
        dsk_ref[...] += dsk

    const = lambda shape: pl.BlockSpec(shape, lambda st: (0, 0))
    return pl.pallas_call(
        body, name="swa_bwd", grid=(nt,),
        in_specs=[pl.BlockSpec((ts, 512), lambda st: (nt - 1 - st, OFF_SQ // 512)),
                  pl.BlockSpec((ts, 256), lambda st: (nt - 1 - st, OFF_SKV // 256)),
                  pl.BlockSpec((BLOCK, 256),
                               lambda st: (jnp.maximum((nt - 1 - st) * nb - 1, 0), OFF_SKV // 256)),
                  const((1, SWA_DIM)), const((1, SWA_DIM)), const((1, SWA_HEADS)),
                  pl.BlockSpec((ts, 512), lambda st: (nt - 1 - st, 0))],
        out_specs=[pl.BlockSpec((ts, 512), lambda st: (nt - 1 - st, 0)),
                   pl.BlockSpec((ts, 256), lambda st: (nt - 1 - st, 0)),
                   const((1, SWA_DIM)), const((1, SWA_DIM)), const((1, SWA_HEADS))],
        out_shape=[jax.ShapeDtypeStruct((s, 512), F32), jax.ShapeDtypeStruct((s, 256), F32),
                   jax.ShapeDtypeStruct((1, SWA_DIM), F32), jax.ShapeDtypeStruct((1, SWA_DIM), F32),
                   jax.ShapeDtypeStruct((1, SWA_HEADS), F32)],
        scratch_shapes=[pltpu.VMEM((BLOCK, 256), F32)],
        compiler_params=_params(("arbitrary",)),
    )(proj, proj, proj, qn, kn, sinks, do)


HALO = 8


def _shift_down(u, halo, k):
    tm = u.shape[0]
    rid = lax.broadcasted_iota(jnp.int32, u.shape, 0)
    out = pltpu.roll(u, k, 0)
    for r in range(k):
        out = jnp.where(rid == r, halo[HALO - k + r:HALO - k + r + 1, :], out)
    return out


def _shift_up(u, halo, k):
    tm = u.shape[0]
    rid = lax.broadcasted_iota(jnp.int32, u.shape, 0)
    out = pltpu.roll(u, tm - k, 0)
    for r in range(k):
        out = jnp.where(rid == tm - k + r, halo[r:r + 1, :], out)
    return out


def _conv_fwd_vals(conv_ref, convp_ref, cw_ref, is_first):
    c_h, c_b, c_c = conv_ref[:, 0:512], conv_ref[:, 512:1024], conv_ref[:, 1024:1536]
    u = c_c * c_h
    up = jnp.where(is_first, 0.0, convp_ref[:, 1024:1536] * convp_ref[:, 0:512])
    u1 = _shift_down(u, up, 1)
    u2 = _shift_down(u, up, 2)
    yc = cw_ref[0:1, :] * u2 + cw_ref[1:2, :] * u1 + cw_ref[2:3, :] * u
    return c_h, c_b, c_c, u, u1, u2, yc


def _out_fwd(proj, o_mla, o_swa, x, w_out, cw):
    s = proj.shape[0]
    tm = min(512, s)

    def body(conv_ref, convp_ref, gates_ref, om_ref, os_ref, x_ref, w_ref, cw_ref, y_ref, z_ref):
        i = pl.program_id(0)
        _, c_b, _, _, _, _, yc = _conv_fwd_vals(conv_ref, convp_ref, cw_ref, i == 0)
        mix = (om_ref[...], c_b * yc, os_ref[...])
        for n in range(3):
            g = gates_ref[:, GROUP * n:GROUP * (n + 1)]
            z_ref[:, GROUP * n:GROUP * (n + 1)] = (mix[n] * (g * _sigmoid(g))).astype(BF16)
        y_ref[...] = x_ref[...] + jnp.dot(z_ref[...], w_ref[...], preferred_element_type=F32)

    row = lambda width: pl.BlockSpec((tm, width), lambda i: (i, 0))
    return pl.pallas_call(
        body, name="out_fwd", grid=(s // tm,),
        in_specs=[pl.BlockSpec((tm, 1536), lambda i: (i, 0)),
                  pl.BlockSpec((HALO, 1536), lambda i: (jnp.maximum(i * (tm // HALO) - 1, 0), 0)),
                  pl.BlockSpec((tm, 1536), lambda i: (i, 1)),
                  row(512), row(512), row(D_MODEL),
                  pl.BlockSpec((D_MIX, D_MODEL), lambda i: (0, 0)),
                  pl.BlockSpec((HALO, 512), lambda i: (0, 0))],
        out_specs=[row(D_MODEL), row(D_MIX)],
        out_shape=[jax.ShapeDtypeStruct((s, D_MODEL), F32), jax.ShapeDtypeStruct((s, D_MIX), BF16)],
        compiler_params=_params(("parallel",)),
    )(proj, proj, proj, o_mla, o_swa, x, w_out, cw)


def _out_bwd(dy, proj, o_mla, o_swa, w_out, cw):
    s = proj.shape[0]
    tm = min(512, s)
    nt = s // tm
    hb = tm // HALO

    def body(dy_ref, dyn_ref, conv_ref, convp_ref, convn_ref, gates_ref, gatesn_ref, om_ref, os_ref,
             w_ref, cw_ref,
             dconv_ref, dgates_ref, dom_ref, delta_ref, dos_ref, dcw_ref):
        i = pl.program_id(0)
        dz = _mm_nt(dy_ref[...], w_ref[...])

        def gate(n):
            g = gates_ref[:, GROUP * n:GROUP * (n + 1)]
            sg = _sigmoid(g)
            return g * sg, sg * (1.0 + g * (1.0 - sg))

        for n, o_ref, do_ref in ((0, om_ref, dom_ref), (2, os_ref, dos_ref)):
            silu, dsilu = gate(n)
            dzn = dz[:, GROUP * n:GROUP * (n + 1)]
            o = o_ref[...]
            do = dzn * silu
            do_ref[...] = do.astype(do_ref.dtype)
            dgates_ref[:, GROUP * n:GROUP * (n + 1)] = dzn * o * dsilu
            if n == 0:
                t = do * o
                for h in range(MLA_HEADS):
                    delta_ref[:, h:h + 1] = jnp.sum(t[:, MLA_V * h:MLA_V * (h + 1)], axis=-1,
                                                    keepdims=True)

        c_h, c_b, c_c, u, u1, u2, yc = _conv_fwd_vals(conv_ref, convp_ref, cw_ref, i == 0)
        silu, dsilu = gate(1)
        dzc = dz[:, GROUP:2 * GROUP]
        dgates_ref[:, GROUP:2 * GROUP] = dzc * (c_b * yc) * dsilu
        dycr = dzc * silu
        dyc = dycr * c_b
        gn = gatesn_ref[:, GROUP:2 * GROUP]
        dzc_n = _mm_nt(dyn_ref[...], w_ref[GROUP:2 * GROUP, :])
        dyc_n = jnp.where(i == nt - 1, 0.0, dzc_n * (gn * _sigmoid(gn)) * convn_ref[:, 512:1024])
        d1 = _shift_up(dyc, dyc_n, 1)
        d2 = _shift_up(dyc, dyc_n, 2)
        du = cw_ref[2:3, :] * dyc + cw_ref[1:2, :] * d1 + cw_ref[0:1, :] * d2
        dconv_ref[:, 0:512] = du * c_c
        dconv_ref[:, 512:1024] = dycr * yc
        dconv_ref[:, 1024:1536] = du * c_h

        @pl.when(i == 0)
        def _():
            dcw_ref[...] = jnp.zeros_like(dcw_ref)

        for k, uk in enumerate((u2, u1, u)):
            dcw_ref[k:k + 1, :] += jnp.sum(dyc * uk, axis=0, keepdims=True)

    row = lambda width: pl.BlockSpec((tm, width), lambda i: (i, 0))
    prev = lambda i: jnp.maximum(i * hb - 1, 0)
    nxt = lambda i: jnp.minimum((i + 1) * hb, s // HALO - 1)
    return pl.pallas_call(
        body, name="out_bwd", grid=(nt,),
        in_specs=[row(D_MODEL),
                  pl.BlockSpec((HALO, D_MODEL), lambda i: (nxt(i), 0)),
                  pl.BlockSpec((tm, 1536), lambda i: (i, 0)),
                  pl.BlockSpec((HALO, 1536), lambda i: (prev(i), 0)),
                  pl.BlockSpec((HALO, 1536), lambda i: (nxt(i), 0)),
                  pl.BlockSpec((tm, 1536), lambda i: (i, 1)),
                  pl.BlockSpec((HALO, 1536), lambda i: (nxt(i), 1)),
                  row(512), row(512),
                  pl.BlockSpec((D_MIX, D_MODEL), lambda i: (0, 0)),
                  pl.BlockSpec((HALO, 512), lambda i: (0, 0))],
        out_specs=[row(1536), row(1536), row(512), row(MLA_HEADS), row(512),
                   pl.BlockSpec((HALO, 512), lambda i: (0, 0))],
        out_shape=[jax.ShapeDtypeStruct((s, 1536), F32), jax.ShapeDtypeStruct((s, 1536), F32),
                   jax.ShapeDtypeStruct((s, 512), BF16), jax.ShapeDtypeStruct((s, MLA_HEADS), F32),
                   jax.ShapeDtypeStruct((s, 512), F32), jax.ShapeDtypeStruct((HALO, 512), F32)],
        compiler_params=_params(("arbitrary",)),
    )(dy, dy, proj, proj, proj, proj, proj, o_mla, o_swa, w_out, cw)


def _loss_head(y, target):
    s, d = y.shape
    tm = min(512, s)
    nt = s // tm

    def body(y_ref, t_ref, dy_ref, loss_ref):
        i = pl.program_id(0)
        err = y_ref[...] - t_ref[...]
        dy_ref[...] = err * (1.0 / d)

        @pl.when(i == 0)
        def _():
            loss_ref[...] = jnp.zeros_like(loss_ref)

        sq = jnp.sum((err * err).reshape(tm // 8, 8, d), axis=0)
        part = sq[:, 0:LANES]
        for c in range(1, d // LANES):
            part = part + sq[:, LANES * c:LANES * (c + 1)]
        loss_ref[...] += part

        @pl.when(i == nt - 1)
        def _():
            loss_ref[...] = jnp.full(loss_ref.shape, (0.5 / d) * jnp.sum(loss_ref[...]), F32)

    return pl.pallas_call(
        body, name="loss_head", grid=(nt,),
        in_specs=[pl.BlockSpec((tm, d), lambda i: (i, 0)), pl.BlockSpec((tm, d), lambda i: (i, 0))],
        out_specs=[pl.BlockSpec((tm, d), lambda i: (i, 0)), pl.BlockSpec((8, LANES), lambda i: (0, 0))],
        out_shape=[jax.ShapeDtypeStruct((s, d), F32), jax.ShapeDtypeStruct((8, LANES), F32)],
        compiler_params=_params(("arbitrary",)),
    )(y, target)


def _adamw(g, w, m, v):
    rows = g.shape[0]
    tr = min(256, rows)
    c1 = 1.0 - ADAM_B1
    c2 = 1.0 - ADAM_B2
    bc1 = 1.0 - ADAM_B1 ** ADAM_STEP
    bc2 = 1.0 - ADAM_B2 ** ADAM_STEP

    def body(g_ref, w_ref, m_ref, v_ref, d_ref, mo_ref, vo_ref):
        gg = g_ref[...]
        m_new = ADAM_B1 * m_ref[...] + c1 * gg
        v_new = ADAM_B2 * v_ref[...] + c2 * (gg * gg)
        m_hat = m_new / bc1
        v_hat = v_new / bc2
        d_ref[...] = -ADAM_LR * (m_hat / (jnp.sqrt(v_hat) + ADAM_EPS) + ADAM_WD * w_ref[...])
        mo_ref[...] = m_new
        vo_ref[...] = v_new

    spec = pl.BlockSpec((tr, g.shape[1]), lambda i: (i, 0))
    return pl.pallas_call(
        body, name="adamw", grid=(rows // tr,),
        in_specs=[spec] * 4, out_specs=[spec] * 3,
        out_shape=[jax.ShapeDtypeStruct(g.shape, F32)] * 3,
        compiler_params=_params(("parallel",)),
    )(g, w, m, v)


HBM_SPEC = pl.BlockSpec(memory_space=pltpu.HBM)


def _place():
    x, y, c = lax.axis_index("x"), lax.axis_index("y"), lax.axis_index("c")
    chips = [(1 - x, y), (x, 1 - y), (1 - x, 1 - y)]
    return x, y, c, chips


def _all_gather(shards):
    na = len(shards)
    halves = [sh.shape[0] // 2 for sh in shards]

    def body(*refs):
        w_refs, a_refs = refs[:na], refs[na:2 * na]
        send_sems, recv_sems, local_sems = refs[2 * na:]
        x, y, c, chips = _place()
        k = 2 * x + y
        sib = (x, y, 1 - c)

        def slab(a, kk, hc):
            return a_refs[a].at[kk, pl.ds(hc * halves[a], halves[a]), :]

        def copy(a, n, src, dst, to):
            return pltpu.make_async_remote_copy(
                src_ref=src, dst_ref=dst, send_sem=send_sems.at[6 * a + n],
                recv_sem=recv_sems.at[6 * a + n], device_id=to, device_id_type=MESH)

        mine = [pltpu.make_async_copy(w_refs[a], a_refs[a].at[k], local_sems.at[a])
                for a in range(na)]
        for cp in mine:
            cp.start()
        first = [copy(a, n, w_refs[a].at[pl.ds(c * halves[a], halves[a]), :], slab(a, k, c),
                      (cx, cy, c))
                 for n, (cx, cy) in enumerate(chips) for a in range(na)]
        for cp in first:
            cp.start()
        passed = []
        for n, (cx, cy) in enumerate(chips):
            kk = 2 * cx + cy
            for a in range(na):
                copy(a, n, slab(a, kk, c), slab(a, kk, c), (cx, cy, c)).wait_recv()
                fwd = copy(a, 3 + n, slab(a, kk, c), slab(a, kk, c), sib)
                fwd.start()
                passed.append(fwd)
        for n, (cx, cy) in enumerate(chips):
            kk = 2 * cx + cy
            for a in range(na):
                copy(a, 3 + n, slab(a, kk, 1 - c), slab(a, kk, 1 - c), sib).wait_recv()
        for cp in first + passed:
            cp.wait_send()
        for cp in mine:
            cp.wait()

    return pl.pallas_call(
        body, name="weights_all_gather",
        in_specs=[HBM_SPEC] * na, out_specs=[HBM_SPEC] * na,
        out_shape=[jax.ShapeDtypeStruct((N_CHIPS,) + sh.shape, sh.dtype) for sh in shards],
        scratch_shapes=[pltpu.SemaphoreType.DMA((6 * na,)), pltpu.SemaphoreType.DMA((6 * na,)),
                        pltpu.SemaphoreType.DMA((na,))],
    )(*shards)


def _swap_halves_to_sibling(gs):
    na = len(gs)

    def body(*refs):
        g_refs, r_refs = refs[:na], refs[na:2 * na]
        send_sems, recv_sems = refs[2 * na:]
        x, y, c, _ = _place()
        cps = []
        for a in range(na):
            half = g_refs[a].shape[1] // 2
            cps.append(pltpu.make_async_remote_copy(
                src_ref=g_refs[a].at[:, pl.ds((1 - c) * half, half), :], dst_ref=r_refs[a],
                send_sem=send_sems.at[a], recv_sem=recv_sems.at[a], device_id=(x, y, 1 - c),
                device_id_type=MESH))
        for cp in cps:
            cp.start()
        for cp in cps:
            cp.wait()

    return pl.pallas_call(
        body, name="grads_to_sibling",
        in_specs=[HBM_SPEC] * na, out_specs=[HBM_SPEC] * na,
        out_shape=[jax.ShapeDtypeStruct((g.shape[0], g.shape[1] // 2, g.shape[2]), g.dtype)
                   for g in gs],
        scratch_shapes=[pltpu.SemaphoreType.DMA((na,)), pltpu.SemaphoreType.DMA((na,))],
    )(*gs)


def _row_tile(rows):
    return 256 if rows % 256 == 0 else 128


def _add_sibling(g, r, c_idx):
    n, rows, cols = g.shape
    half = rows // 2
    tr = _row_tile(half)
    nb = half // tr

    def body(c_ref, g_ref, r_ref, p_ref):
        p_ref[...] = g_ref[...] + r_ref[...]

    return pl.pallas_call(
        body, name="grads_add_sibling",
        grid_spec=pltpu.PrefetchScalarGridSpec(
            num_scalar_prefetch=1, grid=(n, nb),
            in_specs=[pl.BlockSpec((1, tr, cols), lambda j, t, c_ref: (j, c_ref[0] * nb + t, 0)),
                      pl.BlockSpec((1, tr, cols), lambda j, t, c_ref: (j, t, 0))],
            out_specs=pl.BlockSpec((1, tr, cols), lambda j, t, c_ref: (j, t, 0))),
        out_shape=jax.ShapeDtypeStruct((n, half, cols), F32),
        compiler_params=_params(("parallel", "parallel")),
    )(c_idx, g, r)


def _scatter_to_chips(ps):
    na = len(ps)

    def body(*refs):
        p_refs, q_refs = refs[:na], refs[na:2 * na]
        send_sems, recv_sems, local_sems = refs[2 * na:]
        x, y, c, chips = _place()
        k = 2 * x + y
        mine = [pltpu.make_async_copy(p_refs[a].at[k], q_refs[a].at[k], local_sems.at[a])
                for a in range(na)]
        for cp in mine:
            cp.start()
        sends = []
        for i, (cx, cy) in enumerate(chips):
            for a in range(na):
                cp = pltpu.make_async_remote_copy(
                    src_ref=p_refs[a].at[2 * cx + cy], dst_ref=q_refs[a].at[k],
                    send_sem=send_sems.at[3 * a + i], recv_sem=recv_sems.at[3 * a + i],
                    device_id=(cx, cy, c), device_id_type=MESH)
                cp.start()
                sends.append(cp)
        for i, (cx, cy) in enumerate(chips):
            kk = 2 * cx + cy
            for a in range(na):
                pltpu.make_async_remote_copy(
                    src_ref=p_refs[a].at[kk], dst_ref=q_refs[a].at[kk],
                    send_sem=send_sems.at[3 * a + i], recv_sem=recv_sems.at[3 * a + i],
                    device_id=(cx, cy, c), device_id_type=MESH).wait_recv()
        for cp in sends:
            cp.wait_send()
        for cp in mine:
            cp.wait()

    return pl.pallas_call(
        body, name="grads_scatter_to_chips",
        in_specs=[HBM_SPEC] * na, out_specs=[HBM_SPEC] * na,
        out_shape=[jax.ShapeDtypeStruct(p.shape, p.dtype) for p in ps],
        scratch_shapes=[pltpu.SemaphoreType.DMA((3 * na,)), pltpu.SemaphoreType.DMA((3 * na,)),
                        pltpu.SemaphoreType.DMA((na,))],
    )(*ps)


def _sum_chips(q):
    n, half, cols = q.shape
    tr = _row_tile(half)

    def body(q_ref, o_ref):
        o_ref[...] = ((q_ref[0] + q_ref[1]) + q_ref[2]) + q_ref[3]

    return pl.pallas_call(
        body, name="grads_sum_chips", grid=(half // tr,),
        in_specs=[pl.BlockSpec((n, tr, cols), lambda t: (0, t, 0))],
        out_specs=pl.BlockSpec((tr, cols), lambda t: (t, 0)),
        out_shape=jax.ShapeDtypeStruct((half, cols), F32),
        compiler_params=_params(("parallel",)),
    )(q)


def _join_halves(hs):
    na = len(hs)

    def body(*refs):
        h_refs, o_refs = refs[:na], refs[na:2 * na]
        send_sems, recv_sems, local_sems = refs[2 * na:]
        x, y, c, _ = _place()
        mine, sends = [], []
        for a in range(na):
            half = h_refs[a].shape[0]
            dst = o_refs[a].at[pl.ds(c * half, half), :]
            mine.append(pltpu.make_async_copy(h_refs[a], dst, local_sems.at[a]))
            sends.append(pltpu.make_async_remote_copy(
                src_ref=h_refs[a], dst_ref=dst, send_sem=send_sems.at[a], recv_sem=recv_sems.at[a],
                device_id=(x, y, 1 - c), device_id_type=MESH))
        for cp in mine + sends:
            cp.start()
        for a in range(na):
            half = h_refs[a].shape[0]
            other = o_refs[a].at[pl.ds((1 - c) * half, half), :]
            pltpu.make_async_remote_copy(
                src_ref=h_refs[a], dst_ref=other, send_sem=send_sems.at[a], recv_sem=recv_sems.at[a],
                device_id=(x, y, 1 - c), device_id_type=MESH).wait_recv()
        for cp in sends:
            cp.wait_send()
        for cp in mine:
            cp.wait()

    return pl.pallas_call(
        body, name="grads_join_halves",
        in_specs=[HBM_SPEC] * na, out_specs=[HBM_SPEC] * na,
        out_shape=[jax.ShapeDtypeStruct((2 * h.shape[0], h.shape[1]), h.dtype) for h in hs],
        scratch_shapes=[pltpu.SemaphoreType.DMA((na,)), pltpu.SemaphoreType.DMA((na,)),
                        pltpu.SemaphoreType.DMA((na,))],
    )(*hs)


def _part_rows(shape):
    size = 1
    for d in shape:
        size *= d
    rows = -(-size // PACK_COLS)
    return size, -(-rows // PACK_ROW_ALIGN) * PACK_ROW_ALIGN


def _pack_rows(arrays, dtype, total_rows):
    parts, used = [], 0
    for a in arrays:
        size, rows = _part_rows(a.shape)
        flat = a.reshape(-1).astype(dtype)
        parts.append(jnp.pad(flat, (0, rows * PACK_COLS - size)).reshape(rows, PACK_COLS))
        used += rows
    parts.append(jnp.zeros((total_rows - used, PACK_COLS), dtype))
    return jnp.concatenate(parts, axis=0)


def _unpack_rows(buf, shapes):
    lead = buf.shape[:-2]
    out, off = [], 0
    for sh in shapes:
        size, rows = _part_rows(sh)
        part = buf[..., off:off + rows, :].reshape(lead + (-1,))[..., :size]
        out.append(part.reshape(lead + tuple(sh)))
        off += rows
    return out


def _permute_w_in(w):
    z = lambda n: jnp.zeros((w.shape[0], n), w.dtype)
    return jnp.concatenate([
        w[:, 928:1440], w[:, 1440:1952], w[:, 1952:2464],
        w[:, 416:928], w[:, 2464:2976], w[:, 3744:4256],
        w[:, 2976:3488],
        w[:, 0:256], w[:, 256:384], z(64), w[:, 384:416], z(32),
        w[:, 3488:3616], w[:, 3616:3744]], axis=1)


def _unpermute_dw_in(d):
    return jnp.concatenate([
        d[:, 3584:3840], d[:, 3840:3968], d[:, 4032:4064], d[:, 1536:2048],
        d[:, 0:512], d[:, 512:1024], d[:, 1024:1536], d[:, 2048:2560],
        d[:, 3072:3584], d[:, 4096:4224], d[:, 4224:4352], d[:, 2560:3072]], axis=1)


def _rope_tables(s):
    half = MLA_ROPE // 2
    inv_freq = jnp.power(jnp.float32(ROPE_THETA), -jnp.arange(half, dtype=F32) / half)
    ang = jnp.arange(s, dtype=F32)[:, None] * inv_freq[None, :]
    cos, sin = jnp.cos(ang), jnp.sin(ang)
    z = lambda n: jnp.zeros((s, n), F32)
    c = jnp.concatenate([jnp.ones((s, MLA_NOPE), F32), cos, cos, z(32)], axis=1)
    s1 = jnp.concatenate([z(MLA_NOPE), -sin, z(16), z(32)], axis=1)
    s2 = jnp.concatenate([z(MLA_NOPE), z(16), sin, z(32)], axis=1)
    return c, s1, s2


def _pad_lanes(a, n):
    return jnp.pad(a, ((0, 0), (0, n - a.shape[1])))


SHARDED = ("w_in", "w_out", "mla_w_qb", "mla_w_kvb", "conv_w")
REPLICATED = ("norm_g", "mla_q_a_norm", "mla_kv_a_norm", "mla_q_norm", "mla_k_norm",
              "swa_q_norm", "swa_k_norm", "swa_sinks")
WEIGHT_ORDER = ("norm_g", "w_in", "mla_q_a_norm", "mla_w_qb", "mla_kv_a_norm", "mla_w_kvb",
                "mla_q_norm", "mla_k_norm", "conv_w", "swa_q_norm", "swa_k_norm", "swa_sinks", "w_out")
SHARD_AXIS = {"w_in": 2, "w_out": 1, "mla_w_qb": 2, "mla_w_kvb": 2, "conv_w": 2}


def kernel(x, norm_g, w_in, mla_q_a_norm, mla_w_qb, mla_kv_a_norm, mla_w_kvb, mla_q_norm, mla_k_norm, conv_w, swa_q_norm, swa_k_norm, swa_sinks, w_out, loss_target, m_norm_g, m_w_in, m_mla_q_a_norm, m_mla_w_qb, m_mla_kv_a_norm, m_mla_w_kvb, m_mla_q_norm, m_mla_k_norm, m_conv_w, m_swa_q_norm, m_swa_k_norm, m_swa_sinks, m_w_out, v_norm_g, v_w_in, v_mla_q_a_norm, v_mla_w_qb, v_mla_kv_a_norm, v_mla_w_kvb, v_mla_q_norm, v_mla_k_norm, v_conv_w, v_swa_q_norm, v_swa_k_norm, v_swa_sinks, v_w_out):
    weights = dict(norm_g=norm_g, w_in=w_in, mla_q_a_norm=mla_q_a_norm, mla_w_qb=mla_w_qb,
                   mla_kv_a_norm=mla_kv_a_norm, mla_w_kvb=mla_w_kvb, mla_q_norm=mla_q_norm,
                   mla_k_norm=mla_k_norm, conv_w=conv_w, swa_q_norm=swa_q_norm,
                   swa_k_norm=swa_k_norm, swa_sinks=swa_sinks, w_out=w_out)
    mom_m = dict(norm_g=m_norm_g, w_in=m_w_in, mla_q_a_norm=m_mla_q_a_norm, mla_w_qb=m_mla_w_qb,
                 mla_kv_a_norm=m_mla_kv_a_norm, mla_w_kvb=m_mla_w_kvb, mla_q_norm=m_mla_q_norm,
                 mla_k_norm=m_mla_k_norm, conv_w=m_conv_w, swa_q_norm=m_swa_q_norm,
                 swa_k_norm=m_swa_k_norm, swa_sinks=m_swa_sinks, w_out=m_w_out)
    mom_v = dict(norm_g=v_norm_g, w_in=v_w_in, mla_q_a_norm=v_mla_q_a_norm, mla_w_qb=v_mla_w_qb,
                 mla_kv_a_norm=v_mla_kv_a_norm, mla_w_kvb=v_mla_w_kvb, mla_q_norm=v_mla_q_norm,
                 mla_k_norm=v_mla_k_norm, conv_w=v_conv_w, swa_q_norm=v_swa_q_norm,
                 swa_k_norm=v_swa_k_norm, swa_sinks=v_swa_sinks, w_out=v_w_out)
    xs = x[0]
    target = loss_target[0]
    s = xs.shape[0]
    c_idx = lax.axis_index("c").astype(jnp.int32).reshape(1)

    conv_bits = lax.bitcast_convert_type(conv_w, BF16)
    small_list = [w_out, mla_w_qb, mla_w_kvb, conv_bits]
    w_in_rows = DEPTH * D_MODEL
    gathered_in, gathered_rest = _all_gather(
        [w_in.astype(BF16).reshape(w_in_rows, w_in.shape[2]), _pack_rows(small_list, BF16, PACK_ROWS)])
    parts = _unpack_rows(gathered_rest, [a.shape for a in small_list])
    join = lambda p, axis: jnp.concatenate([p[k] for k in range(N_CHIPS)], axis=axis)
    w_in_full = join(gathered_in.reshape(N_CHIPS, DEPTH, D_MODEL, w_in.shape[2]), 2)
    w_out_full = join(parts[0], 1)
    w_qb_full = join(parts[1], 2)
    w_kvb_full = join(parts[2], 2)
    conv_full = lax.bitcast_convert_type(join(parts[3], 2), F32)

    rope = _rope_tables(s)
    layers = []
    for l in range(DEPTH):
        wq = jnp.pad(w_qb_full[l].reshape(MLA_Q_LORA, MLA_HEADS, MLA_QK),
                     ((0, 0), (0, 0), (0, LANES - MLA_QK))).reshape(MLA_Q_LORA, MLA_HEADS * LANES)
        kv = w_kvb_full[l].reshape(MLA_KV_LORA, MLA_HEADS, MLA_NOPE + MLA_V)
        wk = jnp.pad(kv[:, :, :MLA_NOPE], ((0, 0), (0, 0), (0, LANES - MLA_NOPE)))
        wkv = jnp.concatenate([wk.reshape(MLA_KV_LORA, MLA_HEADS * LANES),
                               kv[:, :, MLA_NOPE:].reshape(MLA_KV_LORA, MLA_HEADS * MLA_V)], axis=1)
        layers.append(dict(
            w_in=_permute_w_in(w_in_full[l]), w_out=w_out_full[l], wq=wq, wkv=wkv,
            cw=jnp.pad(conv_full[l], ((0, HALO - 3), (0, 0))),
            g=norm_g[l][None], qan=mla_q_a_norm[l][None], kvan=mla_kv_a_norm[l][None],
            qn=_pad_lanes(mla_q_norm[l][None], LANES), kn=_pad_lanes(mla_k_norm[l][None], LANES),
            sqn=swa_q_norm[l][None], skn=swa_k_norm[l][None], sinks=swa_sinks[l][None]))

    saved = []
    h_in = xs
    for l in range(DEPTH):
        p = layers[l]
        proj, hb = _in_proj_fwd(h_in, p["g"], p["w_in"])
        q, k, v, vt = _mla_prep_fwd(proj, p["qan"], p["kvan"], p["qn"], p["kn"], p["wq"], p["wkv"], rope)
        o_mla, lse = _mla_attn_fwd(q, k, vt)
        o_swa = _swa_fwd(proj, p["sqn"], p["skn"], p["sinks"])
        y, z = _out_fwd(proj, o_mla, o_swa, h_in, p["w_out"], p["cw"])
        saved.append(dict(x=h_in, proj=proj, hb=hb, q=q, k=k, v=v, o_mla=o_mla, lse=lse,
                          o_swa=o_swa, z=z))
        h_in = y

    dy, loss_acc = _loss_head(h_in, target)
    loss = lax.psum(loss_acc[0, 0], ("x", "y", "c"))

    grads = {n: [None] * DEPTH for n in WEIGHT_ORDER}
    for l in reversed(range(DEPTH)):
        p, a = layers[l], saved[l]
        dconv, dgates, do_mla, delta, do_swa, dcw = _out_bwd(dy, a["proj"], a["o_mla"], a["o_swa"],
                                                             p["w_out"], p["cw"])
        grads["w_out"][l] = _matmul_tn(a["z"], dy, "dw_out")
        grads["conv_w"][l] = dcw[0:3]
        delta_rows = jnp.transpose(delta, (1, 0)).reshape(MLA_HEADS // 2, 2, s)
        dq, dk, dv = _mla_attn_bwd(a["q"], a["k"], a["v"], do_mla, a["lse"], delta_rows)
        dmla, dqan, dkvan, dqn, dkn, dwq, dwkv = _mla_prep_bwd(
            a["proj"], p["qan"], p["kvan"], p["qn"], p["kn"], p["wq"], p["wkv"], rope, dq, dk, dv)
        dsq, dskv, dsqn, dskn, dsinks = _swa_bwd(a["proj"], p["sqn"], p["skn"], p["sinks"], do_swa)
        pieces = [dconv, dgates, dsq, dmla, dskv]
        dx, dg = _in_proj_bwd(pieces, a["x"], p["g"], p["w_in"], dy)
        dw_in = jnp.concatenate([_matmul_tn(a["hb"], pc, "dw_in_%d" % n)
                                 for n, pc in enumerate(pieces)], axis=1)
        grads["w_in"][l] = _unpermute_dw_in(dw_in)
        grads["norm_g"][l] = dg[0]
        grads["mla_q_a_norm"][l] = dqan[0]
        grads["mla_kv_a_norm"][l] = dkvan[0]
        grads["mla_q_norm"][l] = dqn[0, :MLA_QK]
        grads["mla_k_norm"][l] = dkn[0, :MLA_QK]
        grads["mla_w_qb"][l] = dwq.reshape(MLA_Q_LORA, MLA_HEADS, LANES)[:, :, :MLA_QK].reshape(
            MLA_Q_LORA, MLA_HEADS * MLA_QK)
        dwk = dwkv[:, :MLA_HEADS * LANES].reshape(MLA_KV_LORA, MLA_HEADS, LANES)[:, :, :MLA_NOPE]
        dwv = dwkv[:, MLA_HEADS * LANES:].reshape(MLA_KV_LORA, MLA_HEADS, MLA_V)
        grads["mla_w_kvb"][l] = jnp.concatenate([dwk, dwv], axis=2).reshape(
            MLA_KV_LORA, MLA_HEADS * (MLA_NOPE + MLA_V))
        grads["swa_q_norm"][l] = dsqn[0]
        grads["swa_k_norm"][l] = dskn[0]
        grads["swa_sinks"][l] = dsinks[0]
        dy = dx
    grad_x = dy[None]
    full_grads = {n: jnp.stack(grads[n]) for n in WEIGHT_ORDER}

    rest = tuple(n for n in SHARDED if n != "w_in")
    rep_shapes = [weights[n].shape for n in REPLICATED]
    flat_rep = lambda d: jnp.concatenate([d[n].reshape(-1) for n in REPLICATED])

    def chunk(g, n, k):
        width = g.shape[SHARD_AXIS[n]] // N_CHIPS
        return lax.slice_in_dim(g, k * width, (k + 1) * width, axis=SHARD_AXIS[n])

    g_in = jnp.stack([chunk(full_grads["w_in"], "w_in", k).reshape(w_in_rows, -1)
                      for k in range(N_CHIPS)])
    rep_grads = flat_rep(full_grads)
    g_rest = jnp.stack([_pack_rows([chunk(full_grads[n], n, k) for n in rest] + [rep_grads],
                                   F32, PACK_ROWS) for k in range(N_CHIPS)])
    from_sibling = _swap_halves_to_sibling([g_in, g_rest])
    partial = [_add_sibling(g, r, c_idx) for g, r in zip((g_in, g_rest), from_sibling)]
    by_chip = _scatter_to_chips(partial)
    g_in_mine, g_rest_mine = _join_halves([_sum_chips(q) for q in by_chip])

    pack_rest = lambda d: _pack_rows([d[n] for n in rest] + [flat_rep(d)], F32, PACK_ROWS)
    in_shape = w_in.shape
    res_in = _adamw(g_in_mine, w_in.reshape(w_in_rows, -1), m_w_in.reshape(w_in_rows, -1),
                    v_w_in.reshape(w_in_rows, -1))
    res_rest = _adamw(g_rest_mine, pack_rest(weights), pack_rest(mom_m), pack_rest(mom_v))
    rest_shapes = [weights[n].shape for n in rest] + [(rep_grads.shape[0],)]
    unpacked = []
    for buf_in, buf_rest in zip((g_in_mine,) + tuple(res_in), (g_rest_mine,) + tuple(res_rest)):
        vals = _unpack_rows(buf_rest, rest_shapes)
        group = dict(zip(rest, vals[:-1]))
        group["w_in"] = buf_in.reshape(in_shape)
        off = 0
        for n, sh in zip(REPLICATED, rep_shapes):
            size = sh[0] * sh[1]
            group[n] = vals[-1][off:off + size].reshape(sh)
            off += size
        unpacked.append(group)
    outs = [loss, grad_x]
    for group in unpacked:
        outs += [group[n] for n in WEIGHT_ORDER]
    return tuple(outs)
```

```python
import functools

import jax
import jax.numpy as jnp
from jax import lax
from jax.experimental import pallas as pl
from jax.experimental.pallas import tpu as pltpu

F32 = jnp.float32
BF16 = jnp.bfloat16

D_MODEL = 1024
DEPTH = 2
GROUP = 512
D_MIX = 3 * GROUP
BLOCK = 128
RMS_EPS = 1e-6
NEG_INF = -1e30
MLA_HEADS = 8
MLA_QK = 96
MLA_NOPE = 64
MLA_ROPE = 32
MLA_V = 64
MLA_Q_LORA = 256
MLA_KV_LORA = 128
ROPE_THETA = 10000.0
SWA_HEADS = 8
SWA_KV = 2
SWA_GROUP = 4
SWA_DIM = 64
IN_COLS = 4256
N_CHIPS = 4

NC = 4352
OFF_CONV, OFF_GATES, OFF_SQ, OFF_MLA, OFF_SKV = 0, 1536, 3072, 3584, 4096
PIECE_WIDTHS = (1536, 1536, 512, 512, 256)

VMEM_LIMIT = 56 * 1024 * 1024
LANES = 128
PACK_COLS = 1024
PACK_ROW_ALIGN = 16
PACK_ROWS = 1024

ADAM_LR = 0.001
ADAM_B1 = 0.9
ADAM_B2 = 0.999
ADAM_EPS = 1e-08
ADAM_WD = 0.01
ADAM_STEP = 10

MESH = pl.DeviceIdType.MESH


def _params(sem, vmem=VMEM_LIMIT):
    return pltpu.CompilerParams(dimension_semantics=sem, vmem_limit_bytes=vmem)


def _dot(a, b, dims):
    return lax.dot_general(a.astype(BF16), b.astype(BF16), (dims, ((), ())),
                           preferred_element_type=F32)


def _mm(a, b):
    return _dot(a, b, ((1,), (0,)))


def _mm_nt(a, b):
    return _dot(a, b, ((1,), (1,)))


def _mm_tn(a, b):
    return _dot(a, b, ((0,), (0,)))


@jax.custom_vjp
def _mmd(a, b):
    return _mm(a, b)


def _mmd_fwd(a, b):
    return _mm(a, b), (a, b)


def _mmd_bwd(res, g):
    a, b = res
    return _mm_nt(g, b), _mm_tn(a, g)


_mmd.defvjp(_mmd_fwd, _mmd_bwd)


@jax.custom_vjp
def _mmd_nt(a, b):
    return _mm_nt(a, b)


def _mmd_nt_fwd(a, b):
    return _mm_nt(a, b), (a, b)


def _mmd_nt_bwd(res, g):
    a, b = res
    return _mm(g, b), _mm_tn(g, a)


_mmd_nt.defvjp(_mmd_nt_fwd, _mmd_nt_bwd)


def _rms(x, g, n=None):
    n = x.shape[-1] if n is None else n
    ms = jnp.sum(x * x, axis=-1, keepdims=True) * (1.0 / n)
    return x * lax.rsqrt(ms + RMS_EPS) * g


def _sigmoid(x):
    return 1.0 / (1.0 + jnp.exp(-x))


@jax.custom_vjp
def _rope(t, c, s1, s2):
    return t * c + pltpu.roll(t, LANES - 16, 1) * s1 + pltpu.roll(t, 16, 1) * s2


def _rope_fwd(t, c, s1, s2):
    return _rope(t, c, s1, s2), (c, s1, s2)


def _rope_bwd(res, g):
    c, s1, s2 = res
    dt = g * c + pltpu.roll(g * s1, 16, 1) + pltpu.roll(g * s2, LANES - 16, 1)
    return dt, jnp.zeros_like(c), jnp.zeros_like(s1), jnp.zeros_like(s2)


_rope.defvjp(_rope_fwd, _rope_bwd)


def _in_proj_fwd(x, g, w):
    s = x.shape[0]
    tm = min(256, s)

    def body(x_ref, g_ref, w_ref, proj_ref, hb_ref):
        hb = _rms(x_ref[...], g_ref[...]).astype(BF16)
        hb_ref[...] = hb
        proj_ref[...] = jnp.dot(hb, w_ref[...], preferred_element_type=F32)

    return pl.pallas_call(
        body, name="in_proj_fwd", grid=(s // tm,),
        in_specs=[pl.BlockSpec((tm, D_MODEL), lambda i: (i, 0)),
                  pl.BlockSpec((1, D_MODEL), lambda i: (0, 0)),
                  pl.BlockSpec((D_MODEL, NC), lambda i: (0, 0))],
        out_specs=[pl.BlockSpec((tm, NC), lambda i: (i, 0)),
                   pl.BlockSpec((tm, D_MODEL), lambda i: (i, 0))],
        out_shape=[jax.ShapeDtypeStruct((s, NC), F32), jax.ShapeDtypeStruct((s, D_MODEL), BF16)],
        compiler_params=_params(("parallel",)),
    )(x, g, w)


def _in_proj_bwd(pieces, x, g, w, dres):
    s = x.shape[0]
    tm = min(256, s)
    n_p = len(pieces)

    def body(*refs):
        p_refs = refs[:n_p]
        x_ref, g_ref, w_ref, dres_ref, dx_ref, dg_ref = refs[n_p:]
        dh = None
        off = 0
        for r in p_refs:
            width = r.shape[1]
            t = _mm_nt(r[...], w_ref[:, off:off + width])
            dh = t if dh is None else dh + t
            off += width
        _, vjp = jax.vjp(_rms, x_ref[...], g_ref[...])
        dx, dg = vjp(dh)
        dx_ref[...] = dx + dres_ref[...]

        @pl.when(pl.program_id(0) == 0)
        def _():
            dg_ref[...] = jnp.zeros_like(dg_ref)

        dg_ref[...] += dg

    in_specs = [pl.BlockSpec((tm, p.shape[1]), lambda i: (i, 0)) for p in pieces]
    in_specs += [pl.BlockSpec((tm, D_MODEL), lambda i: (i, 0)),
                 pl.BlockSpec((1, D_MODEL), lambda i: (0, 0)),
                 pl.BlockSpec((D_MODEL, NC), lambda i: (0, 0)),
                 pl.BlockSpec((tm, D_MODEL), lambda i: (i, 0))]
    return pl.pallas_call(
        body, name="in_proj_bwd", grid=(s // tm,),
        in_specs=in_specs,
        out_specs=[pl.BlockSpec((tm, D_MODEL), lambda i: (i, 0)),
                   pl.BlockSpec((1, D_MODEL), lambda i: (0, 0))],
        out_shape=[jax.ShapeDtypeStruct((s, D_MODEL), F32), jax.ShapeDtypeStruct((1, D_MODEL), F32)],
        compiler_params=_params(("arbitrary",)),
    )(*pieces, x, g, w, dres)


def _matmul_tn(a, b, name):
    s, m = a.shape
    n = b.shape[1]
    tk = min(512, s)
    tn = min(512, n)

    def body(a_ref, b_ref, o_ref):
        @pl.when(pl.program_id(1) == 0)
        def _():
            o_ref[...] = jnp.zeros_like(o_ref)

        o_ref[...] += _mm_tn(a_ref[...], b_ref[...])

    return pl.pallas_call(
        body, name=name, grid=(n // tn, s // tk),
        in_specs=[pl.BlockSpec((tk, m), lambda j, k: (k, 0)),
                  pl.BlockSpec((tk, tn), lambda j, k: (k, j))],
        out_specs=pl.BlockSpec((m, tn), lambda j, k: (0, j)),
        out_shape=jax.ShapeDtypeStruct((m, n), F32),
        compiler_params=_params(("parallel", "arbitrary")),
    )(a, b)


def _prep_fn(q_lat, kv_lat, kr, qan, kvan, qn, kn, wq, wk, wv, c, s1, s2, mm):
    rq = _rms(q_lat, qan)
    rkv = _rms(kv_lat, kvan)
    qs, ks = [], []
    for h in range(MLA_HEADS):
        qs.append(_rope(_rms(mm(rq, wq[h]), qn, MLA_QK), c, s1, s2))
        ks.append(_rope(_rms(mm(rkv, wk[h]) + kr, kn, MLA_QK), c, s1, s2))
    return tuple(qs), tuple(ks), mm(rkv, wv)


def _prep_weights(wq_ref, wkv_ref):
    wq = tuple(wq_ref[:, LANES * h:LANES * (h + 1)].astype(F32) for h in range(MLA_HEADS))
    wk = tuple(wkv_ref[:, LANES * h:LANES * (h + 1)].astype(F32) for h in range(MLA_HEADS))
    wv = wkv_ref[:, LANES * MLA_HEADS:].astype(F32)
    return wq, wk, wv


def _prep_in_specs(tm):
    const = lambda shape: pl.BlockSpec(shape, lambda i: (0, 0))
    return [pl.BlockSpec((tm, 512), lambda i: (i, OFF_MLA // 512)),
            const((1, MLA_Q_LORA)), const((1, MLA_KV_LORA)), const((1, LANES)), const((1, LANES)),
            const((MLA_Q_LORA, 1024)), const((MLA_KV_LORA, 1536)),
            pl.BlockSpec((tm, LANES), lambda i: (i, 0)),
            pl.BlockSpec((tm, LANES), lambda i: (i, 0)),
            pl.BlockSpec((tm, LANES), lambda i: (i, 0))]


def _mla_prep_fwd(proj, qan, kvan, qn, kn, wq, wkv, rope):
    s = proj.shape[0]
    tm = min(512, s)

    def body(blk_ref, qan_ref, kvan_ref, qn_ref, kn_ref, wq_ref, wkv_ref, c_ref, s1_ref, s2_ref,
             q_ref, k_ref, v_ref, qt_ref, kt_ref, vt_ref):
        wq_h, wk_h, wv = _prep_weights(wq_ref, wkv_ref)
        qs, ks, v = _prep_fn(blk_ref[:, 0:256], blk_ref[:, 256:384], blk_ref[:, 384:512],
                             qan_ref[...], kvan_ref[...], qn_ref[...], kn_ref[...],
                             wq_h, wk_h, wv, c_ref[...], s1_ref[...], s2_ref[...], _mm)
        for h in range(MLA_HEADS):
            q2 = qs[h] * Q_PRESCALE
            q_ref[:, LANES * h:LANES * (h + 1)] = q2.astype(BF16)
            k_ref[:, LANES * h:LANES * (h + 1)] = ks[h].astype(BF16)
            qt_ref[LANES * h:LANES * (h + 1), :] = jnp.transpose(q2).astype(BF16)
            kt_ref[LANES * h:LANES * (h + 1), :] = jnp.transpose(ks[h]).astype(BF16)
        v_ref[...] = v.astype(BF16)
        vt_ref[...] = jnp.transpose(v).astype(BF16)

    row = lambda width: pl.BlockSpec((tm, width), lambda i: (i, 0))
    col = lambda height: pl.BlockSpec((height, tm), lambda i: (0, i))
    return pl.pallas_call(
        body, name="mla_prep_fwd", grid=(s // tm,),
        in_specs=_prep_in_specs(tm),
        out_specs=[row(1024), row(1024), row(512), col(1024), col(1024), col(512)],
        out_shape=[jax.ShapeDtypeStruct((s, 1024), BF16), jax.ShapeDtypeStruct((s, 1024), BF16),
                   jax.ShapeDtypeStruct((s, 512), BF16), jax.ShapeDtypeStruct((1024, s), BF16),
                   jax.ShapeDtypeStruct((1024, s), BF16), jax.ShapeDtypeStruct((512, s), BF16)],
        compiler_params=_params(("parallel",)),
    )(proj, qan, kvan, qn, kn, wq, wkv, *rope)


def _mla_prep_bwd(proj, qan, kvan, qn, kn, wq, wkv, rope, dq, dk, dv):
    s = proj.shape[0]
    tm = min(256, s)

    def body(blk_ref, qan_ref, kvan_ref, qn_ref, kn_ref, wq_ref, wkv_ref, c_ref, s1_ref, s2_ref,
             dq_ref, dk_ref, dv_ref,
             dblk_ref, dqan_ref, dkvan_ref, dqn_ref, dkn_ref, dwq_ref, dwkv_ref):
        wq_h, wk_h, wv = _prep_weights(wq_ref, wkv_ref)
        c, s1, s2 = c_ref[...], s1_ref[...], s2_ref[...]

        def fn(q_lat, kv_lat, kr, qan_, kvan_, qn_, kn_, wq_, wk_, wv_):
            return _prep_fn(q_lat, kv_lat, kr, qan_, kvan_, qn_, kn_, wq_, wk_, wv_, c, s1, s2, _mmd)

        _, vjp = jax.vjp(fn, blk_ref[:, 0:256], blk_ref[:, 256:384], blk_ref[:, 384:512],
                         qan_ref[...], kvan_ref[...], qn_ref[...], kn_ref[...], wq_h, wk_h, wv)
        cts = (tuple(jnp.transpose(dq_ref[LANES * h:LANES * (h + 1), :]) for h in range(MLA_HEADS)),
               tuple(jnp.transpose(dk_ref[LANES * h:LANES * (h + 1), :]) for h in range(MLA_HEADS)),
               jnp.transpose(dv_ref[...]))
        dq_lat, dkv_lat, dkr, dqan, dkvan, dqn, dkn, dwq_h, dwk_h, dwv = vjp(cts)
        dblk_ref[:, 0:256] = dq_lat
        dblk_ref[:, 256:384] = dkv_lat
        dblk_ref[:, 384:512] = dkr

        @pl.when(pl.program_id(0) == 0)
        def _():
            for r in (dqan_ref, dkvan_ref, dqn_ref, dkn_ref, dwq_ref, dwkv_ref):
                r[...] = jnp.zeros_like(r)

        dqan_ref[...] += dqan
        dkvan_ref[...] += dkvan
        dqn_ref[...] += dqn
        dkn_ref[...] += dkn
        for h in range(MLA_HEADS):
            dwq_ref[:, LANES * h:LANES * (h + 1)] += dwq_h[h]
            dwkv_ref[:, LANES * h:LANES * (h + 1)] += dwk_h[h]
        dwkv_ref[:, LANES * MLA_HEADS:] += dwv

    const = lambda shape: pl.BlockSpec(shape, lambda i: (0, 0))
    row = lambda width: pl.BlockSpec((tm, width), lambda i: (i, 0))
    shapes = [(s, 512), (1, MLA_Q_LORA), (1, MLA_KV_LORA), (1, LANES), (1, LANES),
              (MLA_Q_LORA, 1024), (MLA_KV_LORA, 1536)]
    return pl.pallas_call(
        body, name="mla_prep_bwd", grid=(s // tm,),
        in_specs=_prep_in_specs(tm) + [pl.BlockSpec((height, tm), lambda i: (0, i))
                                       for height in (1024, 1024, 512)],
        out_specs=[row(512)] + [const(sh) for sh in shapes[1:]],
        out_shape=[jax.ShapeDtypeStruct(sh, F32) for sh in shapes],
        compiler_params=_params(("arbitrary",)),
    )(proj, qan, kvan, qn, kn, wq, wkv, *rope, dq, dk, dv)


MLA_SCALE = MLA_QK ** -0.5


LOG2E = 1.4426950408889634
LN2 = 0.6931471805599453
Q_PRESCALE = MLA_SCALE * LOG2E


def _mla_attn_fwd(q2, k, vt):
    s = q2.shape[0]
    t = min(512, s)
    nq = s // t

    def body(q_ref, k_ref, vt_ref, o_ref, lse_ref, acc_ref):
        i = pl.program_id(1)
        row = lax.broadcasted_iota(jnp.int32, (t, t), 0)
        col = lax.broadcasted_iota(jnp.int32, (t, t), 1)
        causal_t = row <= col
        qh = [q_ref[:, LANES * hh:LANES * (hh + 1)] for hh in range(2)]
        acc_ref[...] = jnp.zeros_like(acc_ref)

        def scores(j, masked):
            r0 = pl.multiple_of(j * t, t)
            out = []
            for hh in range(2):
                kc = k_ref[pl.ds(r0, t), LANES * hh:LANES * (hh + 1)]
                sc = lax.dot_general(kc, qh[hh], (((1,), (1,)), ((), ())),
                                     preferred_element_type=F32)
                out.append(jnp.where(causal_t, sc, NEG_INF) if masked else sc)
            return tuple(out)

        def consume(j, scs, stats):
            r0 = pl.multiple_of(j * t, t)
            out, ps, alphas = [], [], []
            for hh in range(2):
                m, l = stats[hh]
                m_new = jnp.maximum(m, jnp.max(scs[hh], axis=0, keepdims=True))
                p = jnp.exp2(scs[hh] - m_new)
                alpha = jnp.exp2(m - m_new)
                out.append((m_new, alpha * l + jnp.sum(p, axis=0, keepdims=True)))
                ps.append(p.astype(BF16))
                alphas.append(alpha)
            for hh in range(2):
                vc = vt_ref[MLA_V * hh:MLA_V * (hh + 1), pl.ds(r0, t)]
                acc_ref[hh] = alphas[hh] * acc_ref[hh] + jnp.dot(vc, ps[hh],
                                                                 preferred_element_type=F32)
            return tuple(out)

        def trip(j, carry):
            scs, stats = carry
            nxt = scores(j, False)
            stats = consume(jnp.where(j == 0, i, j - 1), scs, stats)
            return nxt, stats

        init = tuple((jnp.full((1, t), NEG_INF, F32), jnp.zeros((1, t), F32)) for _ in range(2))
        scs, stats = lax.fori_loop(0, i, trip, (scores(i, True), init))
        stats = consume(jnp.where(i == 0, i, i - 1), scs, stats)
        for hh in range(2):
            m, l = stats[hh]
            o_ref[:, MLA_V * hh:MLA_V * (hh + 1)] = jnp.transpose(acc_ref[hh] / l)
            lse_ref[0, hh:hh + 1, :] = m + jnp.log2(l)

    return pl.pallas_call(
        body, name="mla_attn_fwd", grid=(MLA_HEADS // 2, nq),
        in_specs=[pl.BlockSpec((t, 256), lambda p, i: (i, p)),
                  pl.BlockSpec((s, 256), lambda p, i: (0, p)),
                  pl.BlockSpec((128, s), lambda p, i: (p, 0))],
        out_specs=[pl.BlockSpec((t, 128), lambda p, i: (i, p)),
                   pl.BlockSpec((1, 2, t), lambda p, i: (p, 0, i))],
        out_shape=[jax.ShapeDtypeStruct((s, 512), F32),
                   jax.ShapeDtypeStruct((MLA_HEADS // 2, 2, s), F32)],
        scratch_shapes=[pltpu.VMEM((2, MLA_V, t), F32)],
        compiler_params=_params(("parallel", "arbitrary")),
    )(q2, k, vt)


def _mla_attn_bwd(q2, q2t, k, kt, v, do, dot, lse_rows, delta_rows):
    s = q2.shape[0]
    t = min(256, s)
    nq = s // t

    def body(q_ref, qt_ref, k_ref, kt_ref, v_ref, do_ref, dot_ref, lse_ref, dl_ref,
             dq_ref, dk_ref, dv_ref):
        j = pl.program_id(1)

        @pl.when(j == 0)
        def _():
            dq_ref[...] = jnp.zeros_like(dq_ref)

        dk_ref[...] = jnp.zeros_like(dk_ref)
        dv_ref[...] = jnp.zeros_like(dv_ref)
        row = lax.broadcasted_iota(jnp.int32, (t, t), 0)
        col = lax.broadcasted_iota(jnp.int32, (t, t), 1)
        causal_t = row <= col
        kh = [k_ref[:, LANES * hh:LANES * (hh + 1)] for hh in range(2)]
        kth = [kt_ref[LANES * hh:LANES * (hh + 1), :] for hh in range(2)]
        vh = [v_ref[:, MLA_V * hh:MLA_V * (hh + 1)] for hh in range(2)]
        nt = (((1,), (1,)), ((), ()))

        def step(i, masked):
            r0 = pl.multiple_of(i * t, t)
            sd = []
            for hh in range(2):
                qh = q_ref[pl.ds(r0, t), LANES * hh:LANES * (hh + 1)]
                doh = do_ref[pl.ds(r0, t), MLA_V * hh:MLA_V * (hh + 1)]
                sc_t = lax.dot_general(kh[hh], qh, nt, preferred_element_type=F32)
                sd.append(jnp.where(causal_t, sc_t, NEG_INF) if masked else sc_t)
                sd.append(lax.dot_general(vh[hh], doh, nt, preferred_element_type=F32))
            pts, gts = [], []
            for hh in range(2):
                lse = lse_ref[0, hh:hh + 1, pl.ds(r0, t)]
                dl = dl_ref[0, hh:hh + 1, pl.ds(r0, t)]
                p_t = jnp.exp2(sd[2 * hh] - lse)
                pts.append(p_t.astype(BF16))
                gts.append((p_t * (sd[2 * hh + 1] - dl)).astype(BF16))
            for hh in range(2):
                qth = qt_ref[LANES * hh:LANES * (hh + 1), pl.ds(r0, t)]
                doth = dot_ref[MLA_V * hh:MLA_V * (hh + 1), pl.ds(r0, t)]
                dv_ref[MLA_V * hh:MLA_V * (hh + 1), :] += lax.dot_general(
                    doth, pts[hh], nt, preferred_element_type=F32)
                dk_ref[LANES * hh:LANES * (hh + 1), :] += lax.dot_general(
                    qth, gts[hh], nt, preferred_element_type=F32)
                dq_ref[LANES * hh:LANES * (hh + 1), pl.ds(r0, t)] += jnp.dot(
                    kth[hh], gts[hh], preferred_element_type=F32)

        step(j, True)

        def trip(i, carry):
            step(i, False)
            return carry

        lax.fori_loop(j + 1, nq, trip, 0)
        dk_ref[...] = dk_ref[...] * LN2

        @pl.when(j == nq - 1)
        def _():
            dq_ref[...] = dq_ref[...] * MLA_SCALE

    return pl.pallas_call(
        body, name="mla_attn_bwd", grid=(MLA_HEADS // 2, nq),
        in_specs=[pl.BlockSpec((s, 256), lambda p, j: (0, p)),
                  pl.BlockSpec((256, s), lambda p, j: (p, 0)),
                  pl.BlockSpec((t, 256), lambda p, j: (j, p)),
                  pl.BlockSpec((256, t), lambda p, j: (p, j)),
                  pl.BlockSpec((t, 128), lambda p, j: (j, p)),
                  pl.BlockSpec((s, 128), lambda p, j: (0, p)),
                  pl.BlockSpec((128, s), lambda p, j: (p, 0)),
                  pl.BlockSpec((1, 2, s), lambda p, j: (p, 0, 0)),
                  pl.BlockSpec((1, 2, s), lambda p, j: (p, 0, 0))],
        out_specs=[pl.BlockSpec((256, s), lambda p, j: (p, 0)),
                   pl.BlockSpec((256, t), lambda p, j: (p, j)),
                   pl.BlockSpec((128, t), lambda p, j: (p, j))],
        out_shape=[jax.ShapeDtypeStruct((1024, s), F32), jax.ShapeDtypeStruct((1024, s), F32),
                   jax.ShapeDtypeStruct((512, s), F32)],
        compiler_params=_params(("parallel", "arbitrary")),
    )(q2, q2t, k, kt, v, do, dot, lse_rows, delta_rows)


SWA_SCALE = SWA_DIM ** -0.5
SWA_ROWS = SWA_GROUP * BLOCK


def _swa_consts(first):
    r = lax.broadcasted_iota(jnp.int32, (SWA_ROWS, 2 * BLOCK), 0)
    kidx = lax.broadcasted_iota(jnp.int32, (SWA_ROWS, 2 * BLOCK), 1)
    dist = BLOCK + (r % BLOCK) - kidx
    valid = (dist >= 0) & (dist < BLOCK) & ((kidx >= BLOCK) | jnp.logical_not(first))
    grp = r // BLOCK
    distf = dist.astype(F32)
    rs = lax.broadcasted_iota(jnp.int32, (SWA_ROWS, SWA_HEADS), 0) // BLOCK
    hs = lax.broadcasted_iota(jnp.int32, (SWA_ROWS, SWA_HEADS), 1)
    biases, sels = [], []
    for j in range(SWA_KV):
        slope = jnp.zeros((SWA_ROWS, 2 * BLOCK), F32)
        for g in range(SWA_GROUP):
            slope = jnp.where(grp == g, 2.0 ** -(SWA_GROUP * j + g + 1), slope)
        biases.append(slope * distf)
        sels.append((hs == rs + SWA_GROUP * j).astype(F32))
    return valid, biases, sels


def _swa_block(qh, kprev, kcur, vprev, vcur, qn, kn, sinks, consts, mm_nt, mm):
    valid, biases, sels = consts
    outs = []
    for j in range(SWA_KV):
        kb = _rms(jnp.concatenate([kprev[j], kcur[j]], axis=0), kn)
        vb = jnp.concatenate([vprev[j], vcur[j]], axis=0)
        qs = jnp.concatenate([_rms(qh[SWA_GROUP * j + g], qn) for g in range(SWA_GROUP)], axis=0)
        sc = mm_nt(qs, kb) * SWA_SCALE - biases[j]
        sc = jnp.where(valid, sc, NEG_INF)
        sink = jnp.sum(sels[j] * sinks, axis=-1, keepdims=True)
        m = lax.stop_gradient(jnp.maximum(jnp.max(sc, axis=-1, keepdims=True), sink))
        e = jnp.exp(sc - m)
        den = jnp.sum(e, axis=-1, keepdims=True) + jnp.exp(sink - m)
        outs.append(mm(e / den, vb))
    return tuple(outs)


def _swa_load(sq_ref, skv_ref, halo_ref, b):
    r = slice(BLOCK * b, BLOCK * (b + 1))
    qh = tuple(sq_ref[r, SWA_DIM * h:SWA_DIM * (h + 1)] for h in range(SWA_HEADS))
    kcur = tuple(skv_ref[r, SWA_DIM * j:SWA_DIM * (j + 1)] for j in range(SWA_KV))
    vcur = tuple(skv_ref[r, 128 + SWA_DIM * j:128 + SWA_DIM * (j + 1)] for j in range(SWA_KV))
    if b == 0:
        src, rp = halo_ref, slice(0, BLOCK)
    else:
        src, rp = skv_ref, slice(BLOCK * (b - 1), BLOCK * b)
    kprev = tuple(src[rp, SWA_DIM * j:SWA_DIM * (j + 1)] for j in range(SWA_KV))
    vprev = tuple(src[rp, 128 + SWA_DIM * j:128 + SWA_DIM * (j + 1)] for j in range(SWA_KV))
    return qh, kprev, kcur, vprev, vcur


def _swa_fwd(proj, qn, kn, sinks):
    s = proj.shape[0]
    ts = min(512, s)
    nb = ts // BLOCK

    def body(sq_ref, skv_ref, halo_ref, qn_ref, kn_ref, sk_ref, o_ref):
        i = pl.program_id(0)
        for b in range(nb):
            first = (i == 0) if b == 0 else False
            consts = _swa_consts(jnp.asarray(first))
            outs = _swa_block(*_swa_load(sq_ref, skv_ref, halo_ref, b), qn_ref[...], kn_ref[...],
                              sk_ref[...], consts, _mm_nt, _mm)
            for j in range(SWA_KV):
                for g in range(SWA_GROUP):
                    h = SWA_GROUP * j + g
                    o_ref[BLOCK * b:BLOCK * (b + 1), SWA_DIM * h:SWA_DIM * (h + 1)] = (
                        outs[j][BLOCK * g:BLOCK * (g + 1)])

    const = lambda shape: pl.BlockSpec(shape, lambda i: (0, 0))
    return pl.pallas_call(
        body, name="swa_fwd", grid=(s // ts,),
        in_specs=[pl.BlockSpec((ts, 512), lambda i: (i, OFF_SQ // 512)),
                  pl.BlockSpec((ts, 256), lambda i: (i, OFF_SKV // 256)),
                  pl.BlockSpec((BLOCK, 256), lambda i: (jnp.maximum(i * nb - 1, 0), OFF_SKV // 256)),
                  const((1, SWA_DIM)), const((1, SWA_DIM)), const((1, SWA_HEADS))],
        out_specs=pl.BlockSpec((ts, 512), lambda i: (i, 0)),
        out_shape=jax.ShapeDtypeStruct((s, 512), F32),
        compiler_params=_params(("parallel",)),
    )(proj, proj, proj, qn, kn, sinks)


def _swa_bwd(proj, qn, kn, sinks, do):
    s = proj.shape[0]
    ts = min(512, s)
    nb = ts // BLOCK
    nt = s // ts

    def body(sq_ref, skv_ref, halo_ref, qn_ref, kn_ref, sk_ref, do_ref,
             dsq_ref, dskv_ref, dqn_ref, dkn_ref, dsk_ref, carry_ref):
        step = pl.program_id(0)
        i = nt - 1 - step

        @pl.when(step == 0)
        def _():
            carry_ref[...] = jnp.zeros_like(carry_ref)
            dqn_ref[...] = jnp.zeros_like(dqn_ref)
            dkn_ref[...] = jnp.zeros_like(dkn_ref)
            dsk_ref[...] = jnp.zeros_like(dsk_ref)

        dk_rows = [[None] * SWA_KV for _ in range(nb)]
        dv_rows = [[None] * SWA_KV for _ in range(nb)]
        halo_k, halo_v = None, None
        dqn = dkn = dsk = None

        def add(a, b_):
            return b_ if a is None else a + b_

        for b in range(nb):
            first = (i == 0) if b == 0 else False
            consts = _swa_consts(jnp.asarray(first))
            qh, kprev, kcur, vprev, vcur = _swa_load(sq_ref, skv_ref, halo_ref, b)

            def fn(qh_, kprev_, kcur_, vprev_, vcur_, qn_, kn_, sk_, consts=consts):
                return _swa_block(qh_, kprev_, kcur_, vprev_, vcur_, qn_, kn_, sk_, consts,
                                  _mmd_nt, _mmd)

            _, vjp = jax.vjp(fn, qh, kprev, kcur, vprev, vcur, qn_ref[...], kn_ref[...], sk_ref[...])
            r = slice(BLOCK * b, BLOCK * (b + 1))
            cts = tuple(
                jnp.concatenate([do_ref[r, SWA_DIM * (SWA_GROUP * j + g):SWA_DIM * (SWA_GROUP * j + g + 1)]
                                 for g in range(SWA_GROUP)], axis=0)
                for j in range(SWA_KV))
            dqh, dkprev, dkcur, dvprev, dvcur, dqn_b, dkn_b, dsk_b = vjp(cts)
            for h in range(SWA_HEADS):
                dsq_ref[r, SWA_DIM * h:SWA_DIM * (h + 1)] = dqh[h]
            for j in range(SWA_KV):
                dk_rows[b][j] = add(dk_rows[b][j], dkcur[j])
                dv_rows[b][j] = add(dv_rows[b][j], dvcur[j])
                if b > 0:
                    dk_rows[b - 1][j] = dk_rows[b - 1][j] + dkprev[j]
                    dv_rows[b - 1][j] = dv_rows[b - 1][j] + dvprev[j]
            if b == 0:
                halo_k, halo_v = dkprev, dvprev
            dqn, dkn, dsk = add(dqn, dqn_b), add(dkn, dkn_b), add(dsk, dsk_b)

        for b in range(nb):
            r = slice(BLOCK * b, BLOCK * (b + 1))
            for j in range(SWA_KV):
                dkv = dk_rows[b][j], dv_rows[b][j]
                for part in range(2):
                    c0 = 128 * part + SWA_DIM * j
                    val = dkv[part]
                    if b == nb - 1:
                        val = val + carry_ref[:, c0:c0 + SWA_DIM]
                    dskv_ref[r, c0:c0 + SWA_DIM] = val
        for j in range(SWA_KV):
            carry_ref[:, SWA_DIM * j:SWA_DIM * (j + 1)] = halo_k[j]
            carry_ref[:, 128 + SWA_DIM * j:128 + SWA_DIM * (j + 1)] = halo_v[j]
        dqn_ref[...] += dqn
        dkn_ref[...] += dkn
        dsk_ref[...] += dsk

    const = lambda shape: pl.BlockSpec(shape, lambda st: (0, 0))
    return pl.pallas_call(
        body, name="swa_bwd", grid=(nt,),
        in_specs=[pl.BlockSpec((ts, 512), lambda st: (nt - 1 - st, OFF_SQ // 512)),
                  pl.BlockSpec((ts, 256), lambda st: (nt - 1 - st, OFF_SKV // 256)),
                  pl.BlockSpec((BLOCK, 256),
                               lambda st: (jnp.maximum((nt - 1 - st) * nb - 1, 0), OFF_SKV // 256)),
                  const((1, SWA_DIM)), const((1, SWA_DIM)), const((1, SWA_HEADS)),
                  pl.BlockSpec((ts, 512), lambda st: (nt - 1 - st, 0))],
        out_specs=[pl.BlockSpec((ts, 512), lambda st: (nt - 1 - st, 0)),
                   pl.BlockSpec((ts, 256), lambda st: (nt - 1 - st, 0)),
                   const((1, SWA_DIM)), const((1, SWA_DIM)), const((1, SWA_HEADS))],
        out_shape=[jax.ShapeDtypeStruct((s, 512), F32), jax.ShapeDtypeStruct((s, 256), F32),
                   jax.ShapeDtypeStruct((1, SWA_DIM), F32), jax.ShapeDtypeStruct((1, SWA_DIM), F32),
                   jax.ShapeDtypeStruct((1, SWA_HEADS), F32)],
        scratch_shapes=[pltpu.VMEM((BLOCK, 256), F32)],
        compiler_params=_params(("arbitrary",)),
    )(proj, proj, proj, qn, kn, sinks, do)


HALO = 8


def _shift_down(u, halo, k):
    tm = u.shape[0]
    rid = lax.broadcasted_iota(jnp.int32, u.shape, 0)
    out = pltpu.roll(u, k, 0)
    for r in range(k):
        out = jnp.where(rid == r, halo[HALO - k + r:HALO - k + r + 1, :], out)
    return out


def _shift_up(u, halo, k):
    tm = u.shape[0]
    rid = lax.broadcasted_iota(jnp.int32, u.shape, 0)
    out = pltpu.roll(u, tm - k, 0)
    for r in range(k):
        out = jnp.where(rid == tm - k + r, halo[r:r + 1, :], out)
    return out


def _conv_fwd_vals(conv_ref, convp_ref, cw_ref, is_first):
    c_h, c_b, c_c = conv_ref[:, 0:512], conv_ref[:, 512:1024], conv_ref[:, 1024:1536]
    u = c_c * c_h
    up = jnp.where(is_first, 0.0, convp_ref[:, 1024:1536] * convp_ref[:, 0:512])
    u1 = _shift_down(u, up, 1)
    u2 = _shift_down(u, up, 2)
    yc = cw_ref[0:1, :] * u2 + cw_ref[1:2, :] * u1 + cw_ref[2:3, :] * u
    return c_h, c_b, c_c, u, u1, u2, yc


def _out_fwd(proj, o_mla, o_swa, x, w_out, cw):
    s = proj.shape[0]
    tm = min(512, s)

    def body(conv_ref, convp_ref, gates_ref, om_ref, os_ref, x_ref, w_ref, cw_ref, y_ref, z_ref):
        i = pl.program_id(0)
        _, c_b, _, _, _, _, yc = _conv_fwd_vals(conv_ref, convp_ref, cw_ref, i == 0)
        mix = (om_ref[...], c_b * yc, os_ref[...])
        for n in range(3):
            g = gates_ref[:, GROUP * n:GROUP * (n + 1)]
            z_ref[:, GROUP * n:GROUP * (n + 1)] = (mix[n] * (g * _sigmoid(g))).astype(BF16)
        y_ref[...] = x_ref[...] + jnp.dot(z_ref[...], w_ref[...], preferred_element_type=F32)

    row = lambda width: pl.BlockSpec((tm, width), lambda i: (i, 0))
    return pl.pallas_call(
        body, name="out_fwd", grid=(s // tm,),
        in_specs=[pl.BlockSpec((tm, 1536), lambda i: (i, 0)),
                  pl.BlockSpec((HALO, 1536), lambda i: (jnp.maximum(i * (tm // HALO) - 1, 0), 0)),
                  pl.BlockSpec((tm, 1536), lambda i: (i, 1)),
                  row(512), row(512), row(D_MODEL),
                  pl.BlockSpec((D_MIX, D_MODEL), lambda i: (0, 0)),
                  pl.BlockSpec((HALO, 512), lambda i: (0, 0))],
        out_specs=[row(D_MODEL), row(D_MIX)],
        out_shape=[jax.ShapeDtypeStruct((s, D_MODEL), F32), jax.ShapeDtypeStruct((s, D_MIX), BF16)],
        compiler_params=_params(("parallel",)),
    )(proj, proj, proj, o_mla, o_swa, x, w_out, cw)


def _out_bwd(dy, proj, o_mla, o_swa, w_out, cw):
    s = proj.shape[0]
    tm = min(512, s)
    nt = s // tm
    hb = tm // HALO

    def body(dy_ref, dyn_ref, conv_ref, convp_ref, convn_ref, gates_ref, gatesn_ref, om_ref, os_ref,
             w_ref, cw_ref,
             dconv_ref, dgates_ref, dom_ref, domt_ref, delta_ref, dos_ref, dcw_ref):
        i = pl.program_id(0)
        dz = _mm_nt(dy_ref[...], w_ref[...])

        def gate(n):
            g = gates_ref[:, GROUP * n:GROUP * (n + 1)]
            sg = _sigmoid(g)
            return g * sg, sg * (1.0 + g * (1.0 - sg))

        for n, o_ref, do_ref in ((0, om_ref, dom_ref), (2, os_ref, dos_ref)):
            silu, dsilu = gate(n)
            dzn = dz[:, GROUP * n:GROUP * (n + 1)]
            o = o_ref[...]
            do = dzn * silu
            do_ref[...] = do.astype(do_ref.dtype)
            dgates_ref[:, GROUP * n:GROUP * (n + 1)] = dzn * o * dsilu
            if n == 0:
                domt_ref[...] = jnp.transpose(do).astype(BF16)
                t = do * o
                for h in range(MLA_HEADS):
                    delta_ref[:, h:h + 1] = jnp.sum(t[:, MLA_V * h:MLA_V * (h + 1)], axis=-1,
                                                    keepdims=True)

        c_h, c_b, c_c, u, u1, u2, yc = _conv_fwd_vals(conv_ref, convp_ref, cw_ref, i == 0)
        silu, dsilu = gate(1)
        dzc = dz[:, GROUP:2 * GROUP]
        dgates_ref[:, GROUP:2 * GROUP] = dzc * (c_b * yc) * dsilu
        dycr = dzc * silu
        dyc = dycr * c_b
        gn = gatesn_ref[:, GROUP:2 * GROUP]
        dzc_n = _mm_nt(dyn_ref[...], w_ref[GROUP:2 * GROUP, :])
        dyc_n = jnp.where(i == nt - 1, 0.0, dzc_n * (gn * _sigmoid(gn)) * convn_ref[:, 512:1024])
        d1 = _shift_up(dyc, dyc_n, 1)
        d2 = _shift_up(dyc, dyc_n, 2)
        du = cw_ref[2:3, :] * dyc + cw_ref[1:2, :] * d1 + cw_ref[0:1, :] * d2
        dconv_ref[:, 0:512] = du * c_c
        dconv_ref[:, 512:1024] = dycr * yc
        dconv_ref[:, 1024:1536] = du * c_h

        @pl.when(i == 0)
        def _():
            dcw_ref[...] = jnp.zeros_like(dcw_ref)

        for k, uk in enumerate((u2, u1, u)):
            dcw_ref[k:k + 1, :] += jnp.sum(dyc * uk, axis=0, keepdims=True)

    row = lambda width: pl.BlockSpec((tm, width), lambda i: (i, 0))
    prev = lambda i: jnp.maximum(i * hb - 1, 0)
    nxt = lambda i: jnp.minimum((i + 1) * hb, s // HALO - 1)
    return pl.pallas_call(
        body, name="out_bwd", grid=(nt,),
        in_specs=[row(D_MODEL),
                  pl.BlockSpec((HALO, D_MODEL), lambda i: (nxt(i), 0)),
                  pl.BlockSpec((tm, 1536), lambda i: (i, 0)),
                  pl.BlockSpec((HALO, 1536), lambda i: (prev(i), 0)),
                  pl.BlockSpec((HALO, 1536), lambda i: (nxt(i), 0)),
                  pl.BlockSpec((tm, 1536), lambda i: (i, 1)),
                  pl.BlockSpec((HALO, 1536), lambda i: (nxt(i), 1)),
                  row(512), row(512),
                  pl.BlockSpec((D_MIX, D_MODEL), lambda i: (0, 0)),
                  pl.BlockSpec((HALO, 512), lambda i: (0, 0))],
        out_specs=[row(1536), row(1536), row(512), pl.BlockSpec((512, tm), lambda i: (0, i)),
                   row(MLA_HEADS), row(512), pl.BlockSpec((HALO, 512), lambda i: (0, 0))],
        out_shape=[jax.ShapeDtypeStruct((s, 1536), F32), jax.ShapeDtypeStruct((s, 1536), F32),
                   jax.ShapeDtypeStruct((s, 512), BF16), jax.ShapeDtypeStruct((512, s), BF16),
                   jax.ShapeDtypeStruct((s, MLA_HEADS), F32),
                   jax.ShapeDtypeStruct((s, 512), F32), jax.ShapeDtypeStruct((HALO, 512), F32)],
        compiler_params=_params(("arbitrary",)),
    )(dy, dy, proj, proj, proj, proj, proj, o_mla, o_swa, w_out, cw)


def _loss_head(y, target):
    s, d = y.shape
    tm = min(512, s)
    nt = s // tm

    def body(y_ref, t_ref, dy_ref, loss_ref):
        i = pl.program_id(0)
        err = y_ref[...] - t_ref[...]
        dy_ref[...] = err * (1.0 / d)

        @pl.when(i == 0)
        def _():
            loss_ref[...] = jnp.zeros_like(loss_ref)

        sq = jnp.sum((err * err).reshape(tm // 8, 8, d), axis=0)
        part = sq[:, 0:LANES]
        for c in range(1, d // LANES):
            part = part + sq[:, LANES * c:LANES * (c + 1)]
        loss_ref[...] += part

        @pl.when(i == nt - 1)
        def _():
            loss_ref[...] = jnp.full(loss_ref.shape, (0.5 / d) * jnp.sum(loss_ref[...]), F32)

    return pl.pallas_call(
        body, name="loss_head", grid=(nt,),
        in_specs=[pl.BlockSpec((tm, d), lambda i: (i, 0)), pl.BlockSpec((tm, d), lambda i: (i, 0))],
        out_specs=[pl.BlockSpec((tm, d), lambda i: (i, 0)), pl.BlockSpec((8, LANES), lambda i: (0, 0))],
        out_shape=[jax.ShapeDtypeStruct((s, d), F32), jax.ShapeDtypeStruct((8, LANES), F32)],
        compiler_params=_params(("arbitrary",)),
    )(y, target)


def _adamw(g, w, m, v):
    rows = g.shape[0]
    tr = min(256, rows)
    c1 = 1.0 - ADAM_B1
    c2 = 1.0 - ADAM_B2
    bc1 = 1.0 - ADAM_B1 ** ADAM_STEP
    bc2 = 1.0 - ADAM_B2 ** ADAM_STEP

    def body(g_ref, w_ref, m_ref, v_ref, d_ref, mo_ref, vo_ref):
        gg = g_ref[...]
        m_new = ADAM_B1 * m_ref[...] + c1 * gg
        v_new = ADAM_B2 * v_ref[...] + c2 * (gg * gg)
        m_hat = m_new / bc1
        v_hat = v_new / bc2
        d_ref[...] = -ADAM_LR * (m_hat / (jnp.sqrt(v_hat) + ADAM_EPS) + ADAM_WD * w_ref[...])
        mo_ref[...] = m_new
        vo_ref[...] = v_new

    spec = pl.BlockSpec((tr, g.shape[1]), lambda i: (i, 0))
    return pl.pallas_call(
        body, name="adamw", grid=(rows // tr,),
        in_specs=[spec] * 4, out_specs=[spec] * 3,
        out_shape=[jax.ShapeDtypeStruct(g.shape, F32)] * 3,
        compiler_params=_params(("parallel",)),
    )(g, w, m, v)


HBM_SPEC = pl.BlockSpec(memory_space=pltpu.HBM)


def _place():
    x, y, c = lax.axis_index("x"), lax.axis_index("y"), lax.axis_index("c")
    chips = [(1 - x, y), (x, 1 - y), (1 - x, 1 - y)]
    return x, y, c, chips


def _all_gather(shards):
    na = len(shards)
    halves = [sh.shape[0] // 2 for sh in shards]

    def body(*refs):
        w_refs, a_refs = refs[:na], refs[na:2 * na]
        send_sems, recv_sems, local_sems = refs[2 * na:]
        x, y, c, chips = _place()
        k = 2 * x + y
        sib = (x, y, 1 - c)

        def slab(a, kk, hc):
            return a_refs[a].at[kk, pl.ds(hc * halves[a], halves[a]), :]

        def copy(a, n, src, dst, to):
            return pltpu.make_async_remote_copy(
                src_ref=src, dst_ref=dst, send_sem=send_sems.at[6 * a + n],
                recv_sem=recv_sems.at[6 * a + n], device_id=to, device_id_type=MESH)

        mine = [pltpu.make_async_copy(w_refs[a], a_refs[a].at[k], local_sems.at[a])
                for a in range(na)]
        for cp in mine:
            cp.start()
        first = [copy(a, n, w_refs[a].at[pl.ds(c * halves[a], halves[a]), :], slab(a, k, c),
                      (cx, cy, c))
                 for n, (cx, cy) in enumerate(chips) for a in range(na)]
        for cp in first:
            cp.start()
        passed = []
        for n, (cx, cy) in enumerate(chips):
            kk = 2 * cx + cy
            for a in range(na):
                copy(a, n, slab(a, kk, c), slab(a, kk, c), (cx, cy, c)).wait_recv()
                fwd = copy(a, 3 + n, slab(a, kk, c), slab(a, kk, c), sib)
                fwd.start()
                passed.append(fwd)
        for n, (cx, cy) in enumerate(chips):
            kk = 2 * cx + cy
            for a in range(na):
                copy(a, 3 + n, slab(a, kk, 1 - c), slab(a, kk, 1 - c), sib).wait_recv()
        for cp in first + passed:
            cp.wait_send()
        for cp in mine:
            cp.wait()

    return pl.pallas_call(
        body, name="weights_all_gather",
        in_specs=[HBM_SPEC] * na, out_specs=[HBM_SPEC] * na,
        out_shape=[jax.ShapeDtypeStruct((N_CHIPS,) + sh.shape, sh.dtype) for sh in shards],
        scratch_shapes=[pltpu.SemaphoreType.DMA((6 * na,)), pltpu.SemaphoreType.DMA((6 * na,)),
                        pltpu.SemaphoreType.DMA((na,))],
    )(*shards)


def _swap_halves_to_sibling(gs):
    na = len(gs)

    def body(*refs):
        g_refs, r_refs = refs[:na], refs[na:2 * na]
        send_sems, recv_sems = refs[2 * na:]
        x, y, c, _ = _place()
        cps = []
        for a in range(na):
            half = g_refs[a].shape[1] // 2
            cps.append(pltpu.make_async_remote_copy(
                src_ref=g_refs[a].at[:, pl.ds((1 - c) * half, half), :], dst_ref=r_refs[a],
                send_sem=send_sems.at[a], recv_sem=recv_sems.at[a], device_id=(x, y, 1 - c),
                device_id_type=MESH))
        for cp in cps:
            cp.start()
        for cp in cps:
            cp.wait()

    return pl.pallas_call(
        body, name="grads_to_sibling",
        in_specs=[HBM_SPEC] * na, out_specs=[HBM_SPEC] * na,
        out_shape=[jax.ShapeDtypeStruct((g.shape[0], g.shape[1] // 2, g.shape[2]), g.dtype)
                   for g in gs],
        scratch_shapes=[pltpu.SemaphoreType.DMA((na,)), pltpu.SemaphoreType.DMA((na,))],
    )(*gs)


def _row_tile(rows):
    return 256 if rows % 256 == 0 else 128


def _add_sibling(g, r, c_idx, out_dtype):
    n, rows, cols = g.shape
    half = rows // 2
    tr = _row_tile(half)
    nb = half // tr

    def body(c_ref, g_ref, r_ref, p_ref):
        p_ref[...] = (g_ref[...] + r_ref[...]).astype(out_dtype)

    return pl.pallas_call(
        body, name="grads_add_sibling",
        grid_spec=pltpu.PrefetchScalarGridSpec(
            num_scalar_prefetch=1, grid=(n, nb),
            in_specs=[pl.BlockSpec((1, tr, cols), lambda j, t, c_ref: (j, c_ref[0] * nb + t, 0)),
                      pl.BlockSpec((1, tr, cols), lambda j, t, c_ref: (j, t, 0))],
            out_specs=pl.BlockSpec((1, tr, cols), lambda j, t, c_ref: (j, t, 0))),
        out_shape=jax.ShapeDtypeStruct((n, half, cols), out_dtype),
        compiler_params=_params(("parallel", "parallel")),
    )(c_idx, g, r)


def _scatter_to_chips(ps):
    na = len(ps)

    def body(*refs):
        p_refs, q_refs = refs[:na], refs[na:2 * na]
        send_sems, recv_sems, local_sems = refs[2 * na:]
        x, y, c, chips = _place()
        k = 2 * x + y
        mine = [pltpu.make_async_copy(p_refs[a].at[k], q_refs[a].at[k], local_sems.at[a])
                for a in range(na)]
        for cp in mine:
            cp.start()
        sends = []
        for i, (cx, cy) in enumerate(chips):
            for a in range(na):
                cp = pltpu.make_async_remote_copy(
                    src_ref=p_refs[a].at[2 * cx + cy], dst_ref=q_refs[a].at[k],
                    send_sem=send_sems.at[3 * a + i], recv_sem=recv_sems.at[3 * a + i],
                    device_id=(cx, cy, c), device_id_type=MESH)
                cp.start()
                sends.append(cp)
        for i, (cx, cy) in enumerate(chips):
            kk = 2 * cx + cy
            for a in range(na):
                pltpu.make_async_remote_copy(
                    src_ref=p_refs[a].at[kk], dst_ref=q_refs[a].at[kk],
                    send_sem=send_sems.at[3 * a + i], recv_sem=recv_sems.at[3 * a + i],
                    device_id=(cx, cy, c), device_id_type=MESH).wait_recv()
        for cp in sends:
            cp.wait_send()
        for cp in mine:
            cp.wait()

    return pl.pallas_call(
        body, name="grads_scatter_to_chips",
        in_specs=[HBM_SPEC] * na, out_specs=[HBM_SPEC] * na,
        out_shape=[jax.ShapeDtypeStruct(p.shape, p.dtype) for p in ps],
        scratch_shapes=[pltpu.SemaphoreType.DMA((3 * na,)), pltpu.SemaphoreType.DMA((3 * na,)),
                        pltpu.SemaphoreType.DMA((na,))],
    )(*ps)


def _sum_chips(q, c_idx):
    n, half, cols = q.shape
    tr = _row_tile(half)
    nb = half // tr

    def body(c_ref, q_ref, o_ref):
        parts = [q_ref[kk].astype(F32) for kk in range(n)]
        o_ref[...] = ((parts[0] + parts[1]) + parts[2]) + parts[3]

    return pl.pallas_call(
        body, name="grads_sum_chips",
        grid_spec=pltpu.PrefetchScalarGridSpec(
            num_scalar_prefetch=1, grid=(nb,),
            in_specs=[pl.BlockSpec((n, tr, cols), lambda t, c_ref: (0, t, 0))],
            out_specs=pl.BlockSpec((tr, cols), lambda t, c_ref: (c_ref[0] * nb + t, 0))),
        out_shape=jax.ShapeDtypeStruct((2 * half, cols), F32),
        compiler_params=_params(("parallel",)),
    )(c_idx, q)


def _join_halves(fulls):
    na = len(fulls)

    def body(*refs):
        o_refs = refs[na:2 * na]
        send_sems, recv_sems = refs[2 * na:]
        x, y, c, _ = _place()
        sends = []
        for a in range(na):
            half = o_refs[a].shape[0] // 2
            rows = o_refs[a].at[pl.ds(c * half, half), :]
            sends.append(pltpu.make_async_remote_copy(
                src_ref=rows, dst_ref=rows, send_sem=send_sems.at[a], recv_sem=recv_sems.at[a],
                device_id=(x, y, 1 - c), device_id_type=MESH))
        for cp in sends:
            cp.start()
        for a in range(na):
            half = o_refs[a].shape[0] // 2
            other = o_refs[a].at[pl.ds((1 - c) * half, half), :]
            pltpu.make_async_remote_copy(
                src_ref=other, dst_ref=other, send_sem=send_sems.at[a], recv_sem=recv_sems.at[a],
                device_id=(x, y, 1 - c), device_id_type=MESH).wait_recv()
        for cp in sends:
            cp.wait_send()

    return pl.pallas_call(
        body, name="grads_join_halves",
        in_specs=[HBM_SPEC] * na, out_specs=[HBM_SPEC] * na,
        out_shape=[jax.ShapeDtypeStruct(f.shape, f.dtype) for f in fulls],
        input_output_aliases={a: a for a in range(na)},
        scratch_shapes=[pltpu.SemaphoreType.DMA((na,)), pltpu.SemaphoreType.DMA((na,))],
    )(*fulls)


def _part_rows(shape):
    size = 1
    for d in shape:
        size *= d
    rows = -(-size // PACK_COLS)
    return size, -(-rows // PACK_ROW_ALIGN) * PACK_ROW_ALIGN


def _pack_rows(arrays, dtype, total_rows):
    parts, used = [], 0
    for a in arrays:
        size, rows = _part_rows(a.shape)
        flat = a.reshape(-1).astype(dtype)
        parts.append(jnp.pad(flat, (0, rows * PACK_COLS - size)).reshape(rows, PACK_COLS))
        used += rows
    parts.append(jnp.zeros((total_rows - used, PACK_COLS), dtype))
    return jnp.concatenate(parts, axis=0)


def _unpack_rows(buf, shapes):
    lead = buf.shape[:-2]
    out, off = [], 0
    for sh in shapes:
        size, rows = _part_rows(sh)
        part = buf[..., off:off + rows, :].reshape(lead + (-1,))[..., :size]
        out.append(part.reshape(lead + tuple(sh)))
        off += rows
    return out


def _permute_w_in(w):
    z = lambda n: jnp.zeros((w.shape[0], n), w.dtype)
    return jnp.concatenate([
        w[:, 928:1440], w[:, 1440:1952], w[:, 1952:2464],
        w[:, 416:928], w[:, 2464:2976], w[:, 3744:4256],
        w[:, 2976:3488],
        w[:, 0:256], w[:, 256:384], z(64), w[:, 384:416], z(32),
        w[:, 3488:3616], w[:, 3616:3744]], axis=1)


def _unpermute_dw_in(d):
    return jnp.concatenate([
        d[:, 3584:3840], d[:, 3840:3968], d[:, 4032:4064], d[:, 1536:2048],
        d[:, 0:512], d[:, 512:1024], d[:, 1024:1536], d[:, 2048:2560],
        d[:, 3072:3584], d[:, 4096:4224], d[:, 4224:4352], d[:, 2560:3072]], axis=1)


def _rope_tables(s):
    half = MLA_ROPE // 2
    inv_freq = jnp.power(jnp.float32(ROPE_THETA), -jnp.arange(half, dtype=F32) / half)
    ang = jnp.arange(s, dtype=F32)[:, None] * inv_freq[None, :]
    cos, sin = jnp.cos(ang), jnp.sin(ang)
    z = lambda n: jnp.zeros((s, n), F32)
    c = jnp.concatenate([jnp.ones((s, MLA_NOPE), F32), cos, cos, z(32)], axis=1)
    s1 = jnp.concatenate([z(MLA_NOPE), -sin, z(16), z(32)], axis=1)
    s2 = jnp.concatenate([z(MLA_NOPE), z(16), sin, z(32)], axis=1)
    return c, s1, s2


def _pad_lanes(a, n):
    return jnp.pad(a, ((0, 0), (0, n - a.shape[1])))


SHARDED = ("w_in", "w_out", "mla_w_qb", "mla_w_kvb", "conv_w")
REPLICATED = ("norm_g", "mla_q_a_norm", "mla_kv_a_norm", "mla_q_norm", "mla_k_norm",
              "swa_q_norm", "swa_k_norm", "swa_sinks")
WEIGHT_ORDER = ("norm_g", "w_in", "mla_q_a_norm", "mla_w_qb", "mla_kv_a_norm", "mla_w_kvb",
                "mla_q_norm", "mla_k_norm", "conv_w", "swa_q_norm", "swa_k_norm", "swa_sinks", "w_out")
SHARD_AXIS = {"w_in": 2, "w_out": 1, "mla_w_qb": 2, "mla_w_kvb": 2, "conv_w": 2}


def kernel(x, norm_g, w_in, mla_q_a_norm, mla_w_qb, mla_kv_a_norm, mla_w_kvb, mla_q_norm, mla_k_norm, conv_w, swa_q_norm, swa_k_norm, swa_sinks, w_out, loss_target, m_norm_g, m_w_in, m_mla_q_a_norm, m_mla_w_qb, m_mla_kv_a_norm, m_mla_w_kvb, m_mla_q_norm, m_mla_k_norm, m_conv_w, m_swa_q_norm, m_swa_k_norm, m_swa_sinks, m_w_out, v_norm_g, v_w_in, v_mla_q_a_norm, v_mla_w_qb, v_mla_kv_a_norm, v_mla_w_kvb, v_mla_q_norm, v_mla_k_norm, v_conv_w, v_swa_q_norm, v_swa_k_norm, v_swa_sinks, v_w_out):
    weights = dict(norm_g=norm_g, w_in=w_in, mla_q_a_norm=mla_q_a_norm, mla_w_qb=mla_w_qb,
                   mla_kv_a_norm=mla_kv_a_norm, mla_w_kvb=mla_w_kvb, mla_q_norm=mla_q_norm,
                   mla_k_norm=mla_k_norm, conv_w=conv_w, swa_q_norm=swa_q_norm,
                   swa_k_norm=swa_k_norm, swa_sinks=swa_sinks, w_out=w_out)
    mom_m = dict(norm_g=m_norm_g, w_in=m_w_in, mla_q_a_norm=m_mla_q_a_norm, mla_w_qb=m_mla_w_qb,
                 mla_kv_a_norm=m_mla_kv_a_norm, mla_w_kvb=m_mla_w_kvb, mla_q_norm=m_mla_q_norm,
                 mla_k_norm=m_mla_k_norm, conv_w=m_conv_w, swa_q_norm=m_swa_q_norm,
                 swa_k_norm=m_swa_k_norm, swa_sinks=m_swa_sinks, w_out=m_w_out)
    mom_v = dict(norm_g=v_norm_g, w_in=v_w_in, mla_q_a_norm=v_mla_q_a_norm, mla_w_qb=v_mla_w_qb,
                 mla_kv_a_norm=v_mla_kv_a_norm, mla_w_kvb=v_mla_w_kvb, mla_q_norm=v_mla_q_norm,
                 mla_k_norm=v_mla_k_norm, conv_w=v_conv_w, swa_q_norm=v_swa_q_norm,
                 swa_k_norm=v_swa_k_norm, swa_sinks=v_swa_sinks, w_out=v_w_out)
    xs = x[0]
    target = loss_target[0]
    s = xs.shape[0]
    c_idx = lax.axis_index("c").astype(jnp.int32).reshape(1)

    conv_bits = lax.bitcast_convert_type(conv_w, BF16)
    small_list = [w_out, mla_w_qb, mla_w_kvb, conv_bits]
    w_in_rows = DEPTH * D_MODEL
    gathered_in, gathered_rest = _all_gather(
        [w_in.astype(BF16).reshape(w_in_rows, w_in.shape[2]), _pack_rows(small_list, BF16, PACK_ROWS)])
    parts = _unpack_rows(gathered_rest, [a.shape for a in small_list])
    join = lambda p, axis: jnp.concatenate([p[k] for k in range(N_CHIPS)], axis=axis)
    w_in_full = join(gathered_in.reshape(N_CHIPS, DEPTH, D_MODEL, w_in.shape[2]), 2)
    w_out_full = join(parts[0], 1)
    w_qb_full = join(parts[1], 2)
    w_kvb_full = join(parts[2], 2)
    conv_full = lax.bitcast_convert_type(join(parts[3], 2), F32)

    rope = _rope_tables(s)
    layers = []
    for l in range(DEPTH):
        wq = jnp.pad(w_qb_full[l].reshape(MLA_Q_LORA, MLA_HEADS, MLA_QK),
                     ((0, 0), (0, 0), (0, LANES - MLA_QK))).reshape(MLA_Q_LORA, MLA_HEADS * LANES)
        kv = w_kvb_full[l].reshape(MLA_KV_LORA, MLA_HEADS, MLA_NOPE + MLA_V)
        wk = jnp.pad(kv[:, :, :MLA_NOPE], ((0, 0), (0, 0), (0, LANES - MLA_NOPE)))
        wkv = jnp.concatenate([wk.reshape(MLA_KV_LORA, MLA_HEADS * LANES),
                               kv[:, :, MLA_NOPE:].reshape(MLA_KV_LORA, MLA_HEADS * MLA_V)], axis=1)
        layers.append(dict(
            w_in=_permute_w_in(w_in_full[l]), w_out=w_out_full[l], wq=wq, wkv=wkv,
            cw=jnp.pad(conv_full[l], ((0, HALO - 3), (0, 0))),
            g=norm_g[l][None], qan=mla_q_a_norm[l][None], kvan=mla_kv_a_norm[l][None],
            qn=_pad_lanes(mla_q_norm[l][None], LANES), kn=_pad_lanes(mla_k_norm[l][None], LANES),
            sqn=swa_q_norm[l][None], skn=swa_k_norm[l][None], sinks=swa_sinks[l][None]))

    saved = []
    h_in = xs
    for l in range(DEPTH):
        p = layers[l]
        proj, hb = _in_proj_fwd(h_in, p["g"], p["w_in"])
        q, k, v, qt, kt, vt = _mla_prep_fwd(proj, p["qan"], p["kvan"], p["qn"], p["kn"], p["wq"], p["wkv"], rope)
        o_mla, lse = _mla_attn_fwd(q, k, vt)
        o_swa = _swa_fwd(proj, p["sqn"], p["skn"], p["sinks"])
        y, z = _out_fwd(proj, o_mla, o_swa, h_in, p["w_out"], p["cw"])
        saved.append(dict(x=h_in, proj=proj, hb=hb, q=q, k=k, v=v, qt=qt, kt=kt, o_mla=o_mla, lse=lse,
                          o_swa=o_swa, z=z))
        h_in = y

    dy, loss_acc = _loss_head(h_in, target)
    loss = lax.psum(loss_acc[0, 0], ("x", "y", "c"))

    grads = {n: [None] * DEPTH for n in WEIGHT_ORDER}
    for l in reversed(range(DEPTH)):
        p, a = layers[l], saved[l]
        dconv, dgates, do_mla, do_mla_t, delta, do_swa, dcw = _out_bwd(dy, a["proj"], a["o_mla"], a["o_swa"],
                                                             p["w_out"], p["cw"])
        grads["w_out"][l] = _matmul_tn(a["z"], dy, "dw_out")
        grads["conv_w"][l] = dcw[0:3]
        delta_rows = jnp.transpose(delta, (1, 0)).reshape(MLA_HEADS // 2, 2, s)
        dq, dk, dv = _mla_attn_bwd(a["q"], a["qt"], a["k"], a["kt"], a["v"], do_mla, do_mla_t,
                                   a["lse"], delta_rows)
        dmla, dqan, dkvan, dqn, dkn, dwq, dwkv = _mla_prep_bwd(
            a["proj"], p["qan"], p["kvan"], p["qn"], p["kn"], p["wq"], p["wkv"], rope, dq, dk, dv)
        dsq, dskv, dsqn, dskn, dsinks = _swa_bwd(a["proj"], p["sqn"], p["skn"], p["sinks"], do_swa)
        pieces = [dconv, dgates, dsq, dmla, dskv]
        dx, dg = _in_proj_bwd(pieces, a["x"], p["g"], p["w_in"], dy)
        dw_in = jnp.concatenate([_matmul_tn(a["hb"], pc, "dw_in_%d" % n)
                                 for n, pc in enumerate(pieces)], axis=1)
        grads["w_in"][l] = _unpermute_dw_in(dw_in)
        grads["norm_g"][l] = dg[0]
        grads["mla_q_a_norm"][l] = dqan[0]
        grads["mla_kv_a_norm"][l] = dkvan[0]
        grads["mla_q_norm"][l] = dqn[0, :MLA_QK]
        grads["mla_k_norm"][l] = dkn[0, :MLA_QK]
        grads["mla_w_qb"][l] = dwq.reshape(MLA_Q_LORA, MLA_HEADS, LANES)[:, :, :MLA_QK].reshape(
            MLA_Q_LORA, MLA_HEADS * MLA_QK)
        dwk = dwkv[:, :MLA_HEADS * LANES].reshape(MLA_KV_LORA, MLA_HEADS, LANES)[:, :, :MLA_NOPE]
        dwv = dwkv[:, MLA_HEADS * LANES:].reshape(MLA_KV_LORA, MLA_HEADS, MLA_V)
        grads["mla_w_kvb"][l] = jnp.concatenate([dwk, dwv], axis=2).reshape(
            MLA_KV_LORA, MLA_HEADS * (MLA_NOPE + MLA_V))
        grads["swa_q_norm"][l] = dsqn[0]
        grads["swa_k_norm"][l] = dskn[0]
        grads["swa_sinks"][l] = dsinks[0]
        dy = dx
    grad_x = dy[None]
    full_grads = {n: jnp.stack(grads[n]) for n in WEIGHT_ORDER}

    rest = tuple(n for n in SHARDED if n != "w_in")
    rep_shapes = [weights[n].shape for n in REPLICATED]
    flat_rep = lambda d: jnp.concatenate([d[n].reshape(-1) for n in REPLICATED])

    def chunk(g, n, k):
        width = g.shape[SHARD_AXIS[n]] // N_CHIPS
        return lax.slice_in_dim(g, k * width, (k + 1) * width, axis=SHARD_AXIS[n])

    g_in = jnp.stack([chunk(full_grads["w_in"], "w_in", k).reshape(w_in_rows, -1)
                      for k in range(N_CHIPS)])
    rep_grads = flat_rep(full_grads)
    g_rest = jnp.stack([_pack_rows([chunk(full_grads[n], n, k) for n in rest] + [rep_grads],
                                   F32, PACK_ROWS) for k in range(N_CHIPS)])
    from_sibling = _swap_halves_to_sibling([g_in, g_rest])
    partial = [_add_sibling(g, r, c_idx, dt)
               for g, r, dt in zip((g_in, g_rest), from_sibling, (BF16, F32))]
    by_chip = _scatter_to_chips(partial)
    g_in_mine, g_rest_mine = _join_halves([_sum_chips(q, c_idx) for q in by_chip])

    pack_rest = lambda d: _pack_rows([d[n] for n in rest] + [flat_rep(d)], F32, PACK_ROWS)
    in_shape = w_in.shape
    res_in = _adamw(g_in_mine, w_in.reshape(w_in_rows, -1), m_w_in.reshape(w_in_rows, -1),
                    v_w_in.reshape(w_in_rows, -1))
    res_rest = _adamw(g_rest_mine, pack_rest(weights), pack_rest(mom_m), pack_rest(mom_v))
    rest_shapes = [weights[n].shape for n in rest] + [(rep_grads.shape[0],)]
    unpacked = []
    for buf_in, buf_rest in zip((g_in_mine,) + tuple(res_in), (g_rest_mine,) + tuple(res_rest)):
        vals = _unpack_rows(buf_rest, rest_shapes)
        group = dict(zip(rest, vals[:-1]))
        group["w_in"] = buf_in.reshape(in_shape)
        off = 0
        for n, sh in zip(REPLICATED, rep_shapes):
            size = sh[0] * sh[1]
            group[n] = vals[-1][off:off + size].reshape(sh)
            off += size
        unpacked.append(group)
    outs = [loss, grad_x]
    for group in unpacked:
        outs += [group[n] for n in WEIGHT_ORDER]
    return tuple(outs)
```

```python
import functools

import jax
import jax.numpy as jnp
from jax import lax
from jax.experimental import pallas as pl
from jax.experimental.pallas import tpu as pltpu

F32 = jnp.float32
BF16 = jnp.bfloat16

D_MODEL = 1024
DEPTH = 2
GROUP = 512
D_MIX = 3 * GROUP
BLOCK = 128
RMS_EPS = 1e-6
NEG_INF = -1e30
MLA_HEADS = 8
MLA_QK = 96
MLA_NOPE = 64
MLA_ROPE = 32
MLA_V = 64
MLA_Q_LORA = 256
MLA_KV_LORA = 128
ROPE_THETA = 10000.0
SWA_HEADS = 8
SWA_KV = 2
SWA_GROUP = 4
SWA_DIM = 64
IN_COLS = 4256
N_CHIPS = 4

NC = 4352
OFF_CONV, OFF_GATES, OFF_SQ, OFF_MLA, OFF_SKV = 0, 1536, 3072, 3584, 4096
PIECE_WIDTHS = (1536, 1536, 512, 512, 256)

VMEM_LIMIT = 56 * 1024 * 1024
LANES = 128
PACK_COLS = 1024
PACK_ROW_ALIGN = 16
PACK_ROWS = 1024

ADAM_LR = 0.001
ADAM_B1 = 0.9
ADAM_B2 = 0.999
ADAM_EPS = 1e-08
ADAM_WD = 0.01
ADAM_STEP = 10

MESH = pl.DeviceIdType.MESH


def _params(sem, vmem=VMEM_LIMIT):
    return pltpu.CompilerParams(dimension_semantics=sem, vmem_limit_bytes=vmem)


def _dot(a, b, dims):
    return lax.dot_general(a.astype(BF16), b.astype(BF16), (dims, ((), ())),
                           preferred_element_type=F32)


def _mm(a, b):
    return _dot(a, b, ((1,), (0,)))


def _mm_nt(a, b):
    return _dot(a, b, ((1,), (1,)))


def _mm_tn(a, b):
    return _dot(a, b, ((0,), (0,)))


@jax.custom_vjp
def _mmd(a, b):
    return _mm(a, b)


def _mmd_fwd(a, b):
    return _mm(a, b), (a, b)


def _mmd_bwd(res, g):
    a, b = res
    return _mm_nt(g, b), _mm_tn(a, g)


_mmd.defvjp(_mmd_fwd, _mmd_bwd)


@jax.custom_vjp
def _mmd_nt(a, b):
    return _mm_nt(a, b)


def _mmd_nt_fwd(a, b):
    return _mm_nt(a, b), (a, b)


def _mmd_nt_bwd(res, g):
    a, b = res
    return _mm(g, b), _mm_tn(g, a)


_mmd_nt.defvjp(_mmd_nt_fwd, _mmd_nt_bwd)


def _rms(x, g, n=None):
    n = x.shape[-1] if n is None else n
    ms = jnp.sum(x * x, axis=-1, keepdims=True) * (1.0 / n)
    return x * lax.rsqrt(ms + RMS_EPS) * g


def _sigmoid(x):
    return 1.0 / (1.0 + jnp.exp(-x))


@jax.custom_vjp
def _rope(t, c, s1, s2):
    return t * c + pltpu.roll(t, LANES - 16, 1) * s1 + pltpu.roll(t, 16, 1) * s2


def _rope_fwd(t, c, s1, s2):
    return _rope(t, c, s1, s2), (c, s1, s2)


def _rope_bwd(res, g):
    c, s1, s2 = res
    dt = g * c + pltpu.roll(g * s1, 16, 1) + pltpu.roll(g * s2, LANES - 16, 1)
    return dt, jnp.zeros_like(c), jnp.zeros_like(s1), jnp.zeros_like(s2)


_rope.defvjp(_rope_fwd, _rope_bwd)


def _in_proj_fwd(x, g, w):
    s = x.shape[0]
    tm = min(256, s)

    def body(x_ref, g_ref, w_ref, proj_ref, hb_ref):
        hb = _rms(x_ref[...], g_ref[...]).astype(BF16)
        hb_ref[...] = hb
        proj_ref[...] = jnp.dot(hb, w_ref[...], preferred_element_type=F32)

    return pl.pallas_call(
        body, name="in_proj_fwd", grid=(s // tm,),
        in_specs=[pl.BlockSpec((tm, D_MODEL), lambda i: (i, 0)),
                  pl.BlockSpec((1, D_MODEL), lambda i: (0, 0)),
                  pl.BlockSpec((D_MODEL, NC), lambda i: (0, 0))],
        out_specs=[pl.BlockSpec((tm, NC), lambda i: (i, 0)),
                   pl.BlockSpec((tm, D_MODEL), lambda i: (i, 0))],
        out_shape=[jax.ShapeDtypeStruct((s, NC), F32), jax.ShapeDtypeStruct((s, D_MODEL), BF16)],
        compiler_params=_params(("parallel",)),
    )(x, g, w)


def _in_proj_bwd(pieces, x, g, w, dres):
    s = x.shape[0]
    tm = min(256, s)
    n_p = len(pieces)

    def body(*refs):
        p_refs = refs[:n_p]
        x_ref, g_ref, w_ref, dres_ref, dx_ref, dg_ref = refs[n_p:]
        dh = None
        off = 0
        for r in p_refs:
            width = r.shape[1]
            t = _mm_nt(r[...], w_ref[:, off:off + width])
            dh = t if dh is None else dh + t
            off += width
        _, vjp = jax.vjp(_rms, x_ref[...], g_ref[...])
        dx, dg = vjp(dh)
        dx_ref[...] = dx + dres_ref[...]

        @pl.when(pl.program_id(0) == 0)
        def _():
            dg_ref[...] = jnp.zeros_like(dg_ref)

        dg_ref[...] += dg

    in_specs = [pl.BlockSpec((tm, p.shape[1]), lambda i: (i, 0)) for p in pieces]
    in_specs += [pl.BlockSpec((tm, D_MODEL), lambda i: (i, 0)),
                 pl.BlockSpec((1, D_MODEL), lambda i: (0, 0)),
                 pl.BlockSpec((D_MODEL, NC), lambda i: (0, 0)),
                 pl.BlockSpec((tm, D_MODEL), lambda i: (i, 0))]
    return pl.pallas_call(
        body, name="in_proj_bwd", grid=(s // tm,),
        in_specs=in_specs,
        out_specs=[pl.BlockSpec((tm, D_MODEL), lambda i: (i, 0)),
                   pl.BlockSpec((1, D_MODEL), lambda i: (0, 0))],
        out_shape=[jax.ShapeDtypeStruct((s, D_MODEL), F32), jax.ShapeDtypeStruct((1, D_MODEL), F32)],
        compiler_params=_params(("arbitrary",)),
    )(*pieces, x, g, w, dres)


def _matmul_tn(a, b, name):
    s, m = a.shape
    n = b.shape[1]
    tk = min(512, s)
    tn = min(512, n)

    def body(a_ref, b_ref, o_ref):
        @pl.when(pl.program_id(1) == 0)
        def _():
            o_ref[...] = jnp.zeros_like(o_ref)

        o_ref[...] += _mm_tn(a_ref[...], b_ref[...])

    return pl.pallas_call(
        body, name=name, grid=(n // tn, s // tk),
        in_specs=[pl.BlockSpec((tk, m), lambda j, k: (k, 0)),
                  pl.BlockSpec((tk, tn), lambda j, k: (k, j))],
        out_specs=pl.BlockSpec((m, tn), lambda j, k: (0, j)),
        out_shape=jax.ShapeDtypeStruct((m, n), F32),
        compiler_params=_params(("parallel", "arbitrary")),
    )(a, b)


def _prep_fn(q_lat, kv_lat, kr, qan, kvan, qn, kn, wq, wk, wv, c, s1, s2, mm):
    rq = _rms(q_lat, qan)
    rkv = _rms(kv_lat, kvan)
    qs, ks = [], []
    for h in range(MLA_HEADS):
        qs.append(_rope(_rms(mm(rq, wq[h]), qn, MLA_QK), c, s1, s2))
        ks.append(_rope(_rms(mm(rkv, wk[h]) + kr, kn, MLA_QK), c, s1, s2))
    return tuple(qs), tuple(ks), mm(rkv, wv)


def _prep_weights(wq_ref, wkv_ref):
    wq = tuple(wq_ref[:, LANES * h:LANES * (h + 1)].astype(F32) for h in range(MLA_HEADS))
    wk = tuple(wkv_ref[:, LANES * h:LANES * (h + 1)].astype(F32) for h in range(MLA_HEADS))
    wv = wkv_ref[:, LANES * MLA_HEADS:].astype(F32)
    return wq, wk, wv


def _prep_in_specs(tm):
    const = lambda shape: pl.BlockSpec(shape, lambda i: (0, 0))
    return [pl.BlockSpec((tm, 512), lambda i: (i, OFF_MLA // 512)),
            const((1, MLA_Q_LORA)), const((1, MLA_KV_LORA)), const((1, LANES)), const((1, LANES)),
            const((MLA_Q_LORA, 1024)), const((MLA_KV_LORA, 1536)),
            pl.BlockSpec((tm, LANES), lambda i: (i, 0)),
            pl.BlockSpec((tm, LANES), lambda i: (i, 0)),
            pl.BlockSpec((tm, LANES), lambda i: (i, 0))]


def _mla_prep_fwd(proj, qan, kvan, qn, kn, wq, wkv, rope):
    s = proj.shape[0]
    tm = min(512, s)

    def body(blk_ref, qan_ref, kvan_ref, qn_ref, kn_ref, wq_ref, wkv_ref, c_ref, s1_ref, s2_ref,
             q_ref, k_ref, v_ref, qt_ref, kt_ref, vt_ref):
        wq_h, wk_h, wv = _prep_weights(wq_ref, wkv_ref)
        qs, ks, v = _prep_fn(blk_ref[:, 0:256], blk_ref[:, 256:384], blk_ref[:, 384:512],
                             qan_ref[...], kvan_ref[...], qn_ref[...], kn_ref[...],
                             wq_h, wk_h, wv, c_ref[...], s1_ref[...], s2_ref[...], _mm)
        for h in range(MLA_HEADS):
            q2 = qs[h] * Q_PRESCALE
            q_ref[:, LANES * h:LANES * (h + 1)] = q2.astype(BF16)
            k_ref[:, LANES * h:LANES * (h + 1)] = ks[h].astype(BF16)
            qt_ref[LANES * h:LANES * (h + 1), :] = jnp.transpose(q2).astype(BF16)
            kt_ref[LANES * h:LANES * (h + 1), :] = jnp.transpose(ks[h]).astype(BF16)
        v_ref[...] = v.astype(BF16)
        vt_ref[...] = jnp.transpose(v).astype(BF16)

    row = lambda width: pl.BlockSpec((tm, width), lambda i: (i, 0))
    col = lambda height: pl.BlockSpec((height, tm), lambda i: (0, i))
    return pl.pallas_call(
        body, name="mla_prep_fwd", grid=(s // tm,),
        in_specs=_prep_in_specs(tm),
        out_specs=[row(1024), row(1024), row(512), col(1024), col(1024), col(512)],
        out_shape=[jax.ShapeDtypeStruct((s, 1024), BF16), jax.ShapeDtypeStruct((s, 1024), BF16),
                   jax.ShapeDtypeStruct((s, 512), BF16), jax.ShapeDtypeStruct((1024, s), BF16),
                   jax.ShapeDtypeStruct((1024, s), BF16), jax.ShapeDtypeStruct((512, s), BF16)],
        compiler_params=_params(("parallel",)),
    )(proj, qan, kvan, qn, kn, wq, wkv, *rope)


def _mla_prep_bwd(proj, qan, kvan, qn, kn, wq, wkv, rope, dq, dk, dv):
    s = proj.shape[0]
    tm = min(256, s)

    def body(blk_ref, qan_ref, kvan_ref, qn_ref, kn_ref, wq_ref, wkv_ref, c_ref, s1_ref, s2_ref,
             dq_ref, dk_ref, dv_ref,
             dblk_ref, dqan_ref, dkvan_ref, dqn_ref, dkn_ref, dwq_ref, dwkv_ref):
        wq_h, wk_h, wv = _prep_weights(wq_ref, wkv_ref)
        c, s1, s2 = c_ref[...], s1_ref[...], s2_ref[...]

        def fn(q_lat, kv_lat, kr, qan_, kvan_, qn_, kn_, wq_, wk_, wv_):
            return _prep_fn(q_lat, kv_lat, kr, qan_, kvan_, qn_, kn_, wq_, wk_, wv_, c, s1, s2, _mmd)

        _, vjp = jax.vjp(fn, blk_ref[:, 0:256], blk_ref[:, 256:384], blk_ref[:, 384:512],
                         qan_ref[...], kvan_ref[...], qn_ref[...], kn_ref[...], wq_h, wk_h, wv)
        cts = (tuple(jnp.transpose(dq_ref[LANES * h:LANES * (h + 1), :]) for h in range(MLA_HEADS)),
               tuple(jnp.transpose(dk_ref[LANES * h:LANES * (h + 1), :]) for h in range(MLA_HEADS)),
               jnp.transpose(dv_ref[...]))
        dq_lat, dkv_lat, dkr, dqan, dkvan, dqn, dkn, dwq_h, dwk_h, dwv = vjp(cts)
        dblk_ref[:, 0:256] = dq_lat
        dblk_ref[:, 256:384] = dkv_lat
        dblk_ref[:, 384:512] = dkr

        @pl.when(pl.program_id(0) == 0)
        def _():
            for r in (dqan_ref, dkvan_ref, dqn_ref, dkn_ref, dwq_ref, dwkv_ref):
                r[...] = jnp.zeros_like(r)

        dqan_ref[...] += dqan
        dkvan_ref[...] += dkvan
        dqn_ref[...] += dqn
        dkn_ref[...] += dkn
        for h in range(MLA_HEADS):
            dwq_ref[:, LANES * h:LANES * (h + 1)] += dwq_h[h]
            dwkv_ref[:, LANES * h:LANES * (h + 1)] += dwk_h[h]
        dwkv_ref[:, LANES * MLA_HEADS:] += dwv

    const = lambda shape: pl.BlockSpec(shape, lambda i: (0, 0))
    row = lambda width: pl.BlockSpec((tm, width), lambda i: (i, 0))
    shapes = [(s, 512), (1, MLA_Q_LORA), (1, MLA_KV_LORA), (1, LANES), (1, LANES),
              (MLA_Q_LORA, 1024), (MLA_KV_LORA, 1536)]
    return pl.pallas_call(
        body, name="mla_prep_bwd", grid=(s // tm,),
        in_specs=_prep_in_specs(tm) + [pl.BlockSpec((height, tm), lambda i: (0, i))
                                       for height in (1024, 1024, 512)],
        out_specs=[row(512)] + [const(sh) for sh in shapes[1:]],
        out_shape=[jax.ShapeDtypeStruct(sh, F32) for sh in shapes],
        compiler_params=_params(("arbitrary",)),
    )(proj, qan, kvan, qn, kn, wq, wkv, *rope, dq, dk, dv)


MLA_SCALE = MLA_QK ** -0.5


LOG2E = 1.4426950408889634
LN2 = 0.6931471805599453
Q_PRESCALE = MLA_SCALE * LOG2E


def _mla_attn_fwd(q2, k, vt):
    s = q2.shape[0]
    t = min(512, s)
    nq = s // t

    def body(q_ref, k_ref, vt_ref, o_ref, lse_ref, acc_ref):
        i = pl.program_id(1)
        row = lax.broadcasted_iota(jnp.int32, (t, t), 0)
        col = lax.broadcasted_iota(jnp.int32, (t, t), 1)
        causal_t = row <= col
        qh = [q_ref[:, LANES * hh:LANES * (hh + 1)] for hh in range(2)]
        acc_ref[...] = jnp.zeros_like(acc_ref)

        def scores(j, masked):
            r0 = pl.multiple_of(j * t, t)
            out = []
            for hh in range(2):
                kc = k_ref[pl.ds(r0, t), LANES * hh:LANES * (hh + 1)]
                sc = lax.dot_general(kc, qh[hh], (((1,), (1,)), ((), ())),
                                     preferred_element_type=F32)
                out.append(jnp.where(causal_t, sc, NEG_INF) if masked else sc)
            return tuple(out)

        def consume(j, scs, stats):
            r0 = pl.multiple_of(j * t, t)
            out, ps, alphas = [], [], []
            for hh in range(2):
                m, l = stats[hh]
                m_new = jnp.maximum(m, jnp.max(scs[hh], axis=0, keepdims=True))
                p = jnp.exp2(scs[hh] - m_new)
                alpha = jnp.exp2(m - m_new)
                out.append((m_new, alpha * l + jnp.sum(p, axis=0, keepdims=True)))
                ps.append(p.astype(BF16))
                alphas.append(alpha)
            for hh in range(2):
                vc = vt_ref[MLA_V * hh:MLA_V * (hh + 1), pl.ds(r0, t)]
                acc_ref[hh] = alphas[hh] * acc_ref[hh] + jnp.dot(vc, ps[hh],
                                                                 preferred_element_type=F32)
            return tuple(out)

        def trip(j, carry):
            scs, stats = carry
            nxt = scores(j, False)
            stats = consume(jnp.where(j == 0, i, j - 1), scs, stats)
            return nxt, stats

        init = tuple((jnp.full((1, t), NEG_INF, F32), jnp.zeros((1, t), F32)) for _ in range(2))
        scs, stats = lax.fori_loop(0, i, trip, (scores(i, True), init))
        stats = consume(jnp.where(i == 0, i, i - 1), scs, stats)
        for hh in range(2):
            m, l = stats[hh]
            o_ref[:, MLA_V * hh:MLA_V * (hh + 1)] = jnp.transpose(acc_ref[hh] / l)
            lse_ref[0, hh:hh + 1, :] = m + jnp.log2(l)

    return pl.pallas_call(
        body, name="mla_attn_fwd", grid=(MLA_HEADS // 2, nq),
        in_specs=[pl.BlockSpec((t, 256), lambda p, i: (i, p)),
                  pl.BlockSpec((s, 256), lambda p, i: (0, p)),
                  pl.BlockSpec((128, s), lambda p, i: (p, 0))],
        out_specs=[pl.BlockSpec((t, 128), lambda p, i: (i, p)),
                   pl.BlockSpec((1, 2, t), lambda p, i: (p, 0, i))],
        out_shape=[jax.ShapeDtypeStruct((s, 512), F32),
                   jax.ShapeDtypeStruct((MLA_HEADS // 2, 2, s), F32)],
        scratch_shapes=[pltpu.VMEM((2, MLA_V, t), F32)],
        compiler_params=_params(("parallel", "arbitrary")),
    )(q2, k, vt)


def _mla_attn_bwd(q2, q2t, k, kt, v, do, dot, lse_rows, delta_rows):
    s = q2.shape[0]
    t = min(512, s)
    nq = s // t

    def body(q_ref, qt_ref, k_ref, kt_ref, v_ref, do_ref, dot_ref, lse_ref, dl_ref,
             dq_ref, dk_ref, dv_ref):
        j = pl.program_id(1)

        @pl.when(j == 0)
        def _():
            dq_ref[...] = jnp.zeros_like(dq_ref)

        dk_ref[...] = jnp.zeros_like(dk_ref)
        dv_ref[...] = jnp.zeros_like(dv_ref)
        row = lax.broadcasted_iota(jnp.int32, (t, t), 0)
        col = lax.broadcasted_iota(jnp.int32, (t, t), 1)
        causal_t = row <= col
        kh = [k_ref[:, LANES * hh:LANES * (hh + 1)] for hh in range(2)]
        kth = [kt_ref[LANES * hh:LANES * (hh + 1), :] for hh in range(2)]
        vh = [v_ref[:, MLA_V * hh:MLA_V * (hh + 1)] for hh in range(2)]
        nt = (((1,), (1,)), ((), ()))

        def step(i, masked):
            r0 = pl.multiple_of(i * t, t)
            sd = []
            for hh in range(2):
                qh = q_ref[pl.ds(r0, t), LANES * hh:LANES * (hh + 1)]
                doh = do_ref[pl.ds(r0, t), MLA_V * hh:MLA_V * (hh + 1)]
                sc_t = lax.dot_general(kh[hh], qh, nt, preferred_element_type=F32)
                sd.append(jnp.where(causal_t, sc_t, NEG_INF) if masked else sc_t)
                sd.append(lax.dot_general(vh[hh], doh, nt, preferred_element_type=F32))
            pts, gts = [], []
            for hh in range(2):
                lse = lse_ref[0, hh:hh + 1, pl.ds(r0, t)]
                dl = dl_ref[0, hh:hh + 1, pl.ds(r0, t)]
                p_t = jnp.exp2(sd[2 * hh] - lse)
                pts.append(p_t.astype(BF16))
                gts.append((p_t * (sd[2 * hh + 1] - dl)).astype(BF16))
            for hh in range(2):
                qth = qt_ref[LANES * hh:LANES * (hh + 1), pl.ds(r0, t)]
                doth = dot_ref[MLA_V * hh:MLA_V * (hh + 1), pl.ds(r0, t)]
                dv_ref[MLA_V * hh:MLA_V * (hh + 1), :] += lax.dot_general(
                    doth, pts[hh], nt, preferred_element_type=F32)
                dk_ref[LANES * hh:LANES * (hh + 1), :] += lax.dot_general(
                    qth, gts[hh], nt, preferred_element_type=F32)
                dq_ref[LANES * hh:LANES * (hh + 1), pl.ds(r0, t)] += jnp.dot(
                    kth[hh], gts[hh], preferred_element_type=F32)

        step(j, True)

        def trip(i, carry):
            step(i, False)
            return carry

        lax.fori_loop(j + 1, nq, trip, 0)
        dk_ref[...] = dk_ref[...] * LN2

        @pl.when(j == nq - 1)
        def _():
            dq_ref[...] = dq_ref[...] * MLA_SCALE

    return pl.pallas_call(
        body, name="mla_attn_bwd", grid=(MLA_HEADS // 2, nq),
        in_specs=[pl.BlockSpec((s, 256), lambda p, j: (0, p)),
                  pl.BlockSpec((256, s), lambda p, j: (p, 0)),
                  pl.BlockSpec((t, 256), lambda p, j: (j, p)),
                  pl.BlockSpec((256, t), lambda p, j: (p, j)),
                  pl.BlockSpec((t, 128), lambda p, j: (j, p)),
                  pl.BlockSpec((s, 128), lambda p, j: (0, p)),
                  pl.BlockSpec((128, s), lambda p, j: (p, 0)),
                  pl.BlockSpec((1, 2, s), lambda p, j: (p, 0, 0)),
                  pl.BlockSpec((1, 2, s), lambda p, j: (p, 0, 0))],
        out_specs=[pl.BlockSpec((256, s), lambda p, j: (p, 0)),
                   pl.BlockSpec((256, t), lambda p, j: (p, j)),
                   pl.BlockSpec((128, t), lambda p, j: (p, j))],
        out_shape=[jax.ShapeDtypeStruct((1024, s), F32), jax.ShapeDtypeStruct((1024, s), F32),
                   jax.ShapeDtypeStruct((512, s), F32)],
        compiler_params=_params(("parallel", "arbitrary")),
    )(q2, q2t, k, kt, v, do, dot, lse_rows, delta_rows)


SWA_SCALE = SWA_DIM ** -0.5
SWA_ROWS = SWA_GROUP * BLOCK


def _swa_consts(first):
    r = lax.broadcasted_iota(jnp.int32, (SWA_ROWS, 2 * BLOCK), 0)
    kidx = lax.broadcasted_iota(jnp.int32, (SWA_ROWS, 2 * BLOCK), 1)
    dist = BLOCK + (r % BLOCK) - kidx
    valid = (dist >= 0) & (dist < BLOCK) & ((kidx >= BLOCK) | jnp.logical_not(first))
    grp = r // BLOCK
    distf = dist.astype(F32)
    rs = lax.broadcasted_iota(jnp.int32, (SWA_ROWS, SWA_HEADS), 0) // BLOCK
    hs = lax.broadcasted_iota(jnp.int32, (SWA_ROWS, SWA_HEADS), 1)
    biases, sels = [], []
    for j in range(SWA_KV):
        slope = jnp.zeros((SWA_ROWS, 2 * BLOCK), F32)
        for g in range(SWA_GROUP):
            slope = jnp.where(grp == g, 2.0 ** -(SWA_GROUP * j + g + 1), slope)
        biases.append(slope * distf)
        sels.append((hs == rs + SWA_GROUP * j).astype(F32))
    return valid, biases, sels


def _swa_block(qh, kprev, kcur, vprev, vcur, qn, kn, sinks, consts, mm_nt, mm):
    valid, biases, sels = consts
    outs = []
    for j in range(SWA_KV):
        kb = _rms(jnp.concatenate([kprev[j], kcur[j]], axis=0), kn)
        vb = jnp.concatenate([vprev[j], vcur[j]], axis=0)
        qs = jnp.concatenate([_rms(qh[SWA_GROUP * j + g], qn) for g in range(SWA_GROUP)], axis=0)
        sc = mm_nt(qs, kb) * SWA_SCALE - biases[j]
        sc = jnp.where(valid, sc, NEG_INF)
        sink = jnp.sum(sels[j] * sinks, axis=-1, keepdims=True)
        m = lax.stop_gradient(jnp.maximum(jnp.max(sc, axis=-1, keepdims=True), sink))
        e = jnp.exp(sc - m)
        den = jnp.sum(e, axis=-1, keepdims=True) + jnp.exp(sink - m)
        outs.append(mm(e / den, vb))
    return tuple(outs)


def _swa_load(sq_ref, skv_ref, halo_ref, b):
    r = slice(BLOCK * b, BLOCK * (b + 1))
    qh = tuple(sq_ref[r, SWA_DIM * h:SWA_DIM * (h + 1)] for h in range(SWA_HEADS))
    kcur = tuple(skv_ref[r, SWA_DIM * j:SWA_DIM * (j + 1)] for j in range(SWA_KV))
    vcur = tuple(skv_ref[r, 128 + SWA_DIM * j:128 + SWA_DIM * (j + 1)] for j in range(SWA_KV))
    if b == 0:
        src, rp = halo_ref, slice(0, BLOCK)
    else:
        src, rp = skv_ref, slice(BLOCK * (b - 1), BLOCK * b)
    kprev = tuple(src[rp, SWA_DIM * j:SWA_DIM * (j + 1)] for j in range(SWA_KV))
    vprev = tuple(src[rp, 128 + SWA_DIM * j:128 + SWA_DIM * (j + 1)] for j in range(SWA_KV))
    return qh, kprev, kcur, vprev, vcur


def _swa_fwd(proj, qn, kn, sinks):
    s = proj.shape[0]
    ts = min(512, s)
    nb = ts // BLOCK

    def body(sq_ref, skv_ref, halo_ref, qn_ref, kn_ref, sk_ref, o_ref):
        i = pl.program_id(0)
        for b in range(nb):
            first = (i == 0) if b == 0 else False
            consts = _swa_consts(jnp.asarray(first))
            outs = _swa_block(*_swa_load(sq_ref, skv_ref, halo_ref, b), qn_ref[...], kn_ref[...],
                              sk_ref[...], consts, _mm_nt, _mm)
            for j in range(SWA_KV):
                for g in range(SWA_GROUP):
                    h = SWA_GROUP * j + g
                    o_ref[BLOCK * b:BLOCK * (b + 1), SWA_DIM * h:SWA_DIM * (h + 1)] = (
                        outs[j][BLOCK * g:BLOCK * (g + 1)])

    const = lambda shape: pl.BlockSpec(shape, lambda i: (0, 0))
    return pl.pallas_call(
        body, name="swa_fwd", grid=(s // ts,),
        in_specs=[pl.BlockSpec((ts, 512), lambda i: (i, OFF_SQ // 512)),
                  pl.BlockSpec((ts, 256), lambda i: (i, OFF_SKV // 256)),
                  pl.BlockSpec((BLOCK, 256), lambda i: (jnp.maximum(i * nb - 1, 0), OFF_SKV // 256)),
                  const((1, SWA_DIM)), const((1, SWA_DIM)), const((1, SWA_HEADS))],
        out_specs=pl.BlockSpec((ts, 512), lambda i: (i, 0)),
        out_shape=jax.ShapeDtypeStruct((s, 512), F32),
        compiler_params=_params(("parallel",)),
    )(proj, proj, proj, qn, kn, sinks)


def _swa_bwd(proj, qn, kn, sinks, do):
    s = proj.shape[0]
    ts = min(512, s)
    nb = ts // BLOCK
    nt = s // ts

    def body(sq_ref, skv_ref, halo_ref, qn_ref, kn_ref, sk_ref, do_ref,
             dsq_ref, dskv_ref, dqn_ref, dkn_ref, dsk_ref, carry_ref):
        step = pl.program_id(0)
        i = nt - 1 - step

        @pl.when(step == 0)
        def _():
            carry_ref[...] = jnp.zeros_like(carry_ref)
            dqn_ref[...] = jnp.zeros_like(dqn_ref)
            dkn_ref[...] = jnp.zeros_like(dkn_ref)
            dsk_ref[...] = jnp.zeros_like(dsk_ref)

        dk_rows = [[None] * SWA_KV for _ in range(nb)]
        dv_rows = [[None] * SWA_KV for _ in range(nb)]
        halo_k, halo_v = None, None
        dqn = dkn = dsk = None

        def add(a, b_):
            return b_ if a is None else a + b_

        for b in range(nb):
            first = (i == 0) if b == 0 else False
            consts = _swa_consts(jnp.asarray(first))
            qh, kprev, kcur, vprev, vcur = _swa_load(sq_ref, skv_ref, halo_ref, b)

            def fn(qh_, kprev_, kcur_, vprev_, vcur_, qn_, kn_, sk_, consts=consts):
                return _swa_block(qh_, kprev_, kcur_, vprev_, vcur_, qn_, kn_, sk_, consts,
                                  _mmd_nt, _mmd)

            _, vjp = jax.vjp(fn, qh, kprev, kcur, vprev, vcur, qn_ref[...], kn_ref[...], sk_ref[...])
            r = slice(BLOCK * b, BLOCK * (b + 1))
            cts = tuple(
                jnp.concatenate([do_ref[r, SWA_DIM * (SWA_GROUP * j + g):SWA_DIM * (SWA_GROUP * j + g + 1)]
                                 for g in range(SWA_GROUP)], axis=0)
                for j in range(SWA_KV))
            dqh, dkprev, dkcur, dvprev, dvcur, dqn_b, dkn_b, dsk_b = vjp(cts)
            for h in range(SWA_HEADS):
                dsq_ref[r, SWA_DIM * h:SWA_DIM * (h + 1)] = dqh[h]
            for j in range(SWA_KV):
                dk_rows[b][j] = add(dk_rows[b][j], dkcur[j])
                dv_rows[b][j] = add(dv_rows[b][j], dvcur[j])
                if b > 0:
                    dk_rows[b - 1][j] = dk_rows[b - 1][j] + dkprev[j]
                    dv_rows[b - 1][j] = dv_rows[b - 1][j] + dvprev[j]
            if b == 0:
                halo_k, halo_v = dkprev, dvprev
            dqn, dkn, dsk = add(dqn, dqn_b), add(dkn, dkn_b), add(dsk, dsk_b)

        for b in range(nb):
            r = slice(BLOCK * b, BLOCK * (b + 1))
            for j in range(SWA_KV):
                dkv = dk_rows[b][j], dv_rows[b][j]
                for part in range(2):
                    c0 = 128 * part + SWA_DIM * j
                    val = dkv[part]
                    if b == nb - 1:
                        val = val + carry_ref[:, c0:c0 + SWA_DIM]
                    dskv_ref[r, c0:c0 + SWA_DIM] = val
        for j in range(SWA_KV):
            carry_ref[:, SWA_DIM * j:SWA_DIM * (j + 1)] = halo_k[j]
            carry_ref[:, 128 + SWA_DIM * j:128 + SWA_DIM * (j + 1)] = halo_v[j]
        dqn_ref[...] += dqn
        dkn_ref[...] += dkn
        dsk_ref[...] += dsk

    const = lambda shape: pl.BlockSpec(shape, lambda st: (0, 0))
    return pl.pallas_call(
        body, name="swa_bwd", grid=(nt,),
        in_specs=[pl.BlockSpec((ts, 512), lambda st: (nt - 1 - st, OFF_SQ // 512)),
                  pl.BlockSpec((ts, 256), lambda st: (nt - 1 - st, OFF_SKV // 256)),
                  pl.BlockSpec((BLOCK, 256),
                               lambda st: (jnp.maximum((nt - 1 - st) * nb - 1, 0), OFF_SKV // 256)),
                  const((1, SWA_DIM)), const((1, SWA_DIM)), const((1, SWA_HEADS)),
                  pl.BlockSpec((ts, 512), lambda st: (nt - 1 - st, 0))],
        out_specs=[pl.BlockSpec((ts, 512), lambda st: (nt - 1 - st, 0)),
                   pl.BlockSpec((ts, 256), lambda st: (nt - 1 - st, 0)),
                   const((1, SWA_DIM)), const((1, SWA_DIM)), const((1, SWA_HEADS))],
        out_shape=[jax.ShapeDtypeStruct((s, 512), F32), jax.ShapeDtypeStruct((s, 256), F32),
                   jax.ShapeDtypeStruct((1, SWA_DIM), F32), jax.ShapeDtypeStruct((1, SWA_DIM), F32),
                   jax.ShapeDtypeStruct((1, SWA_HEADS), F32)],
        scratch_shapes=[pltpu.VMEM((BLOCK, 256), F32)],
        compiler_params=_params(("arbitrary",)),
    )(proj, proj, proj, qn, kn, sinks, do)


HALO = 8


def _shift_down(u, halo, k):
    tm = u.shape[0]
    rid = lax.broadcasted_iota(jnp.int32, u.shape, 0)
    out = pltpu.roll(u, k, 0)
    for r in range(k):
        out = jnp.where(rid == r, halo[HALO - k + r:HALO - k + r + 1, :], out)
    return out


def _shift_up(u, halo, k):
    tm = u.shape[0]
    rid = lax.broadcasted_iota(jnp.int32, u.shape, 0)
    out = pltpu.roll(u, tm - k, 0)
    for r in range(k):
        out = jnp.where(rid == tm - k + r, halo[r:r + 1, :], out)
    return out


def _conv_fwd_vals(conv_ref, convp_ref, cw_ref, is_first):
    c_h, c_b, c_c = conv_ref[:, 0:512], conv_ref[:, 512:1024], conv_ref[:, 1024:1536]
    u = c_c * c_h
    up = jnp.where(is_first, 0.0, convp_ref[:, 1024:1536] * convp_ref[:, 0:512])
    u1 = _shift_down(u, up, 1)
    u2 = _shift_down(u, up, 2)
    yc = cw_ref[0:1, :] * u2 + cw_ref[1:2, :] * u1 + cw_ref[2:3, :] * u
    return c_h, c_b, c_c, u, u1, u2, yc


def _out_fwd(proj, o_mla, o_swa, x, w_out, cw):
    s = proj.shape[0]
    tm = min(512, s)

    def body(conv_ref, convp_ref, gates_ref, om_ref, os_ref, x_ref, w_ref, cw_ref, y_ref, z_ref):
        i = pl.program_id(0)
        _, c_b, _, _, _, _, yc = _conv_fwd_vals(conv_ref, convp_ref, cw_ref, i == 0)
        mix = (om_ref[...], c_b * yc, os_ref[...])
        for n in range(3):
            g = gates_ref[:, GROUP * n:GROUP * (n + 1)]
            z_ref[:, GROUP * n:GROUP * (n + 1)] = (mix[n] * (g * _sigmoid(g))).astype(BF16)
        y_ref[...] = x_ref[...] + jnp.dot(z_ref[...], w_ref[...], preferred_element_type=F32)

    row = lambda width: pl.BlockSpec((tm, width), lambda i: (i, 0))
    return pl.pallas_call(
        body, name="out_fwd", grid=(s // tm,),
        in_specs=[pl.BlockSpec((tm, 1536), lambda i: (i, 0)),
                  pl.BlockSpec((HALO, 1536), lambda i: (jnp.maximum(i * (tm // HALO) - 1, 0), 0)),
                  pl.BlockSpec((tm, 1536), lambda i: (i, 1)),
                  row(512), row(512), row(D_MODEL),
                  pl.BlockSpec((D_MIX, D_MODEL), lambda i: (0, 0)),
                  pl.BlockSpec((HALO, 512), lambda i: (0, 0))],
        out_specs=[row(D_MODEL), row(D_MIX)],
        out_shape=[jax.ShapeDtypeStruct((s, D_MODEL), F32), jax.ShapeDtypeStruct((s, D_MIX), BF16)],
        compiler_params=_params(("parallel",)),
    )(proj, proj, proj, o_mla, o_swa, x, w_out, cw)


def _out_bwd(dy, proj, o_mla, o_swa, w_out, cw):
    s = proj.shape[0]
    tm = min(512, s)
    nt = s // tm
    hb = tm // HALO

    def body(dy_ref, dyn_ref, conv_ref, convp_ref, convn_ref, gates_ref, gatesn_ref, om_ref, os_ref,
             w_ref, cw_ref,
             dconv_ref, dgates_ref, dom_ref, domt_ref, delta_ref, dos_ref, dcw_ref):
        i = pl.program_id(0)
        dz = _mm_nt(dy_ref[...], w_ref[...])

        def gate(n):
            g = gates_ref[:, GROUP * n:GROUP * (n + 1)]
            sg = _sigmoid(g)
            return g * sg, sg * (1.0 + g * (1.0 - sg))

        for n, o_ref, do_ref in ((0, om_ref, dom_ref), (2, os_ref, dos_ref)):
            silu, dsilu = gate(n)
            dzn = dz[:, GROUP * n:GROUP * (n + 1)]
            o = o_ref[...]
            do = dzn * silu
            do_ref[...] = do.astype(do_ref.dtype)
            dgates_ref[:, GROUP * n:GROUP * (n + 1)] = dzn * o * dsilu
            if n == 0:
                domt_ref[...] = jnp.transpose(do).astype(BF16)
                t = do * o
                for h in range(MLA_HEADS):
                    delta_ref[:, h:h + 1] = jnp.sum(t[:, MLA_V * h:MLA_V * (h + 1)], axis=-1,
                                                    keepdims=True)

        c_h, c_b, c_c, u, u1, u2, yc = _conv_fwd_vals(conv_ref, convp_ref, cw_ref, i == 0)
        silu, dsilu = gate(1)
        dzc = dz[:, GROUP:2 * GROUP]
        dgates_ref[:, GROUP:2 * GROUP] = dzc * (c_b * yc) * dsilu
        dycr = dzc * silu
        dyc = dycr * c_b
        gn = gatesn_ref[:, GROUP:2 * GROUP]
        dzc_n = _mm_nt(dyn_ref[...], w_ref[GROUP:2 * GROUP, :])
        dyc_n = jnp.where(i == nt - 1, 0.0, dzc_n * (gn * _sigmoid(gn)) * convn_ref[:, 512:1024])
        d1 = _shift_up(dyc, dyc_n, 1)
        d2 = _shift_up(dyc, dyc_n, 2)
        du = cw_ref[2:3, :] * dyc + cw_ref[1:2, :] * d1 + cw_ref[0:1, :] * d2
        dconv_ref[:, 0:512] = du * c_c
        dconv_ref[:, 512:1024] = dycr * yc
        dconv_ref[:, 1024:1536] = du * c_h

        @pl.when(i == 0)
        def _():
            dcw_ref[...] = jnp.zeros_like(dcw_ref)

        for k, uk in enumerate((u2, u1, u)):
            dcw_ref[k:k + 1, :] += jnp.sum(dyc * uk, axis=0, keepdims=True)

    row = lambda width: pl.BlockSpec((tm, width), lambda i: (i, 0))
    prev = lambda i: jnp.maximum(i * hb - 1, 0)
    nxt = lambda i: jnp.minimum((i + 1) * hb, s // HALO - 1)
    return pl.pallas_call(
        body, name="out_bwd", grid=(nt,),
        in_specs=[row(D_MODEL),
                  pl.BlockSpec((HALO, D_MODEL), lambda i: (nxt(i), 0)),
                  pl.BlockSpec((tm, 1536), lambda i: (i, 0)),
                  pl.BlockSpec((HALO, 1536), lambda i: (prev(i), 0)),
                  pl.BlockSpec((HALO, 1536), lambda i: (nxt(i), 0)),
                  pl.BlockSpec((tm, 1536), lambda i: (i, 1)),
                  pl.BlockSpec((HALO, 1536), lambda i: (nxt(i), 1)),
                  row(512), row(512),
                  pl.BlockSpec((D_MIX, D_MODEL), lambda i: (0, 0)),
                  pl.BlockSpec((HALO, 512), lambda i: (0, 0))],
        out_specs=[row(1536), row(1536), row(512), pl.BlockSpec((512, tm), lambda i: (0, i)),
                   row(MLA_HEADS), row(512), pl.BlockSpec((HALO, 512), lambda i: (0, 0))],
        out_shape=[jax.ShapeDtypeStruct((s, 1536), F32), jax.ShapeDtypeStruct((s, 1536), F32),
                   jax.ShapeDtypeStruct((s, 512), BF16), jax.ShapeDtypeStruct((512, s), BF16),
                   jax.ShapeDtypeStruct((s, MLA_HEADS), F32),
                   jax.ShapeDtypeStruct((s, 512), F32), jax.ShapeDtypeStruct((HALO, 512), F32)],
        compiler_params=_params(("arbitrary",)),
    )(dy, dy, proj, proj, proj, proj, proj, o_mla, o_swa, w_out, cw)


def _loss_head(y, target):
    s, d = y.shape
    tm = min(512, s)
    nt = s // tm

    def body(y_ref, t_ref, dy_ref, loss_ref):
        i = pl.program_id(0)
        err = y_ref[...] - t_ref[...]
        dy_ref[...] = err * (1.0 / d)

        @pl.when(i == 0)
        def _():
            loss_ref[...] = jnp.zeros_like(loss_ref)

        sq = jnp.sum((err * err).reshape(tm // 8, 8, d), axis=0)
        part = sq[:, 0:LANES]
        for c in range(1, d // LANES):
            part = part + sq[:, LANES * c:LANES * (c + 1)]
        loss_ref[...] += part

        @pl.when(i == nt - 1)
        def _():
            loss_ref[...] = jnp.full(loss_ref.shape, (0.5 / d) * jnp.sum(loss_ref[...]), F32)

    return pl.pallas_call(
        body, name="loss_head", grid=(nt,),
        in_specs=[pl.BlockSpec((tm, d), lambda i: (i, 0)), pl.BlockSpec((tm, d), lambda i: (i, 0))],
        out_specs=[pl.BlockSpec((tm, d), lambda i: (i, 0)), pl.BlockSpec((8, LANES), lambda i: (0, 0))],
        out_shape=[jax.ShapeDtypeStruct((s, d), F32), jax.ShapeDtypeStruct((8, LANES), F32)],
        compiler_params=_params(("arbitrary",)),
    )(y, target)


def _adamw(g, w, m, v):
    rows = g.shape[0]
    tr = min(256, rows)
    c1 = 1.0 - ADAM_B1
    c2 = 1.0 - ADAM_B2
    bc1 = 1.0 - ADAM_B1 ** ADAM_STEP
    bc2 = 1.0 - ADAM_B2 ** ADAM_STEP

    def body(g_ref, w_ref, m_ref, v_ref, d_ref, mo_ref, vo_ref):
        gg = g_ref[...]
        m_new = ADAM_B1 * m_ref[...] + c1 * gg
        v_new = ADAM_B2 * v_ref[...] + c2 * (gg * gg)
        m_hat = m_new / bc1
        v_hat = v_new / bc2
        d_ref[...] = -ADAM_LR * (m_hat / (jnp.sqrt(v_hat) + ADAM_EPS) + ADAM_WD * w_ref[...])
        mo_ref[...] = m_new
        vo_ref[...] = v_new

    spec = pl.BlockSpec((tr, g.shape[1]), lambda i: (i, 0))
    return pl.pallas_call(
        body, name="adamw", grid=(rows // tr,),
        in_specs=[spec] * 4, out_specs=[spec] * 3,
        out_shape=[jax.ShapeDtypeStruct(g.shape, F32)] * 3,
        compiler_params=_params(("parallel",)),
    )(g, w, m, v)


HBM_SPEC = pl.BlockSpec(memory_space=pltpu.HBM)


def _place():
    x, y, c = lax.axis_index("x"), lax.axis_index("y"), lax.axis_index("c")
    chips = [(1 - x, y), (x, 1 - y), (1 - x, 1 - y)]
    return x, y, c, chips


def _all_gather(shards):
    na = len(shards)
    halves = [sh.shape[0] // 2 for sh in shards]

    def body(*refs):
        w_refs, a_refs = refs[:na], refs[na:2 * na]
        send_sems, recv_sems = refs[2 * na:]
        x, y, c, chips = _place()
        k = 2 * x + y
        sib = (x, y, 1 - c)

        def slab(a, kk, hc):
            return a_refs[a].at[kk, pl.ds(hc * halves[a], halves[a]), :]

        def copy(a, n, src, dst, to):
            return pltpu.make_async_remote_copy(
                src_ref=src, dst_ref=dst, send_sem=send_sems.at[6 * a + n],
                recv_sem=recv_sems.at[6 * a + n], device_id=to, device_id_type=MESH)

        first = [copy(a, n, w_refs[a].at[pl.ds(c * halves[a], halves[a]), :], slab(a, k, c),
                      (cx, cy, c))
                 for n, (cx, cy) in enumerate(chips) for a in range(na)]
        for cp in first:
            cp.start()
        passed = []
        for n, (cx, cy) in enumerate(chips):
            kk = 2 * cx + cy
            for a in range(na):
                copy(a, n, slab(a, kk, c), slab(a, kk, c), (cx, cy, c)).wait_recv()
                fwd = copy(a, 3 + n, slab(a, kk, c), slab(a, kk, c), sib)
                fwd.start()
                passed.append(fwd)
        for n, (cx, cy) in enumerate(chips):
            kk = 2 * cx + cy
            for a in range(na):
                copy(a, 3 + n, slab(a, kk, 1 - c), slab(a, kk, 1 - c), sib).wait_recv()
        for cp in first + passed:
            cp.wait_send()

    return pl.pallas_call(
        body, name="weights_all_gather",
        in_specs=[HBM_SPEC] * na, out_specs=[HBM_SPEC] * na,
        out_shape=[jax.ShapeDtypeStruct((N_CHIPS,) + sh.shape, sh.dtype) for sh in shards],
        scratch_shapes=[pltpu.SemaphoreType.DMA((6 * na,)), pltpu.SemaphoreType.DMA((6 * na,))],
    )(*shards)


def _fill_own_slab(buf, src, k_idx):
    n, rows, cols = buf.shape
    tr = _row_tile(rows)
    slabs = src.ndim == 3

    def body(k_ref, src_ref, buf_ref, out_ref):
        out_ref[0] = src_ref[0] if slabs else src_ref[...]

    if slabs:
        src_spec = pl.BlockSpec((1, tr, cols), lambda t, k_ref: (k_ref[0], t, 0))
    else:
        src_spec = pl.BlockSpec((tr, cols), lambda t, k_ref: (t, 0))
    return pl.pallas_call(
        body, name="fill_own_slab",
        grid_spec=pltpu.PrefetchScalarGridSpec(
            num_scalar_prefetch=1, grid=(rows // tr,),
            in_specs=[src_spec, pl.BlockSpec(memory_space=pl.ANY)],
            out_specs=pl.BlockSpec((1, tr, cols), lambda t, k_ref: (k_ref[0], t, 0))),
        out_shape=jax.ShapeDtypeStruct(buf.shape, buf.dtype),
        input_output_aliases={2: 0},
        compiler_params=_params(("parallel",)),
    )(k_idx, src, buf)


def _swap_halves_to_sibling(gs):
    na = len(gs)

    def body(*refs):
        g_refs, r_refs = refs[:na], refs[na:2 * na]
        send_sems, recv_sems = refs[2 * na:]
        x, y, c, _ = _place()
        cps = []
        for a in range(na):
            half = g_refs[a].shape[1] // 2
            cps.append(pltpu.make_async_remote_copy(
                src_ref=g_refs[a].at[:, pl.ds((1 - c) * half, half), :], dst_ref=r_refs[a],
                send_sem=send_sems.at[a], recv_sem=recv_sems.at[a], device_id=(x, y, 1 - c),
                device_id_type=MESH))
        for cp in cps:
            cp.start()
        for cp in cps:
            cp.wait()

    return pl.pallas_call(
        body, name="grads_to_sibling",
        in_specs=[HBM_SPEC] * na, out_specs=[HBM_SPEC] * na,
        out_shape=[jax.ShapeDtypeStruct((g.shape[0], g.shape[1] // 2, g.shape[2]), g.dtype)
                   for g in gs],
        scratch_shapes=[pltpu.SemaphoreType.DMA((na,)), pltpu.SemaphoreType.DMA((na,))],
    )(*gs)


def _row_tile(rows):
    return 256 if rows % 256 == 0 else 128


def _add_sibling(g, r, c_idx, out_dtype):
    n, rows, cols = g.shape
    half = rows // 2
    tr = _row_tile(half)
    nb = half // tr

    def body(c_ref, g_ref, r_ref, p_ref):
        p_ref[...] = (g_ref[...] + r_ref[...]).astype(out_dtype)

    return pl.pallas_call(
        body, name="grads_add_sibling",
        grid_spec=pltpu.PrefetchScalarGridSpec(
            num_scalar_prefetch=1, grid=(n, nb),
            in_specs=[pl.BlockSpec((1, tr, cols), lambda j, t, c_ref: (j, c_ref[0] * nb + t, 0)),
                      pl.BlockSpec((1, tr, cols), lambda j, t, c_ref: (j, t, 0))],
            out_specs=pl.BlockSpec((1, tr, cols), lambda j, t, c_ref: (j, t, 0))),
        out_shape=jax.ShapeDtypeStruct((n, half, cols), out_dtype),
        compiler_params=_params(("parallel", "parallel")),
    )(c_idx, g, r)


def _scatter_to_chips(ps):
    na = len(ps)

    def body(*refs):
        p_refs, q_refs = refs[:na], refs[na:2 * na]
        send_sems, recv_sems = refs[2 * na:]
        x, y, c, chips = _place()
        k = 2 * x + y
        sends = []
        for i, (cx, cy) in enumerate(chips):
            for a in range(na):
                cp = pltpu.make_async_remote_copy(
                    src_ref=p_refs[a].at[2 * cx + cy], dst_ref=q_refs[a].at[k],
                    send_sem=send_sems.at[3 * a + i], recv_sem=recv_sems.at[3 * a + i],
                    device_id=(cx, cy, c), device_id_type=MESH)
                cp.start()
                sends.append(cp)
        for i, (cx, cy) in enumerate(chips):
            kk = 2 * cx + cy
            for a in range(na):
                pltpu.make_async_remote_copy(
                    src_ref=p_refs[a].at[kk], dst_ref=q_refs[a].at[kk],
                    send_sem=send_sems.at[3 * a + i], recv_sem=recv_sems.at[3 * a + i],
                    device_id=(cx, cy, c), device_id_type=MESH).wait_recv()
        for cp in sends:
            cp.wait_send()

    return pl.pallas_call(
        body, name="grads_scatter_to_chips",
        in_specs=[HBM_SPEC] * na, out_specs=[HBM_SPEC] * na,
        out_shape=[jax.ShapeDtypeStruct(p.shape, p.dtype) for p in ps],
        scratch_shapes=[pltpu.SemaphoreType.DMA((3 * na,)), pltpu.SemaphoreType.DMA((3 * na,))],
    )(*ps)


def _sum_chips(q, c_idx):
    n, half, cols = q.shape
    tr = _row_tile(half)
    nb = half // tr

    def body(c_ref, q_ref, o_ref):
        parts = [q_ref[kk].astype(F32) for kk in range(n)]
        o_ref[...] = ((parts[0] + parts[1]) + parts[2]) + parts[3]

    return pl.pallas_call(
        body, name="grads_sum_chips",
        grid_spec=pltpu.PrefetchScalarGridSpec(
            num_scalar_prefetch=1, grid=(nb,),
            in_specs=[pl.BlockSpec((n, tr, cols), lambda t, c_ref: (0, t, 0))],
            out_specs=pl.BlockSpec((tr, cols), lambda t, c_ref: (c_ref[0] * nb + t, 0))),
        out_shape=jax.ShapeDtypeStruct((2 * half, cols), F32),
        compiler_params=_params(("parallel",)),
    )(c_idx, q)


def _join_halves(fulls):
    na = len(fulls)

    def body(*refs):
        o_refs = refs[na:2 * na]
        send_sems, recv_sems = refs[2 * na:]
        x, y, c, _ = _place()
        sends = []
        for a in range(na):
            half = o_refs[a].shape[0] // 2
            rows = o_refs[a].at[pl.ds(c * half, half), :]
            sends.append(pltpu.make_async_remote_copy(
                src_ref=rows, dst_ref=rows, send_sem=send_sems.at[a], recv_sem=recv_sems.at[a],
                device_id=(x, y, 1 - c), device_id_type=MESH))
        for cp in sends:
            cp.start()
        for a in range(na):
            half = o_refs[a].shape[0] // 2
            other = o_refs[a].at[pl.ds((1 - c) * half, half), :]
            pltpu.make_async_remote_copy(
                src_ref=other, dst_ref=other, send_sem=send_sems.at[a], recv_sem=recv_sems.at[a],
                device_id=(x, y, 1 - c), device_id_type=MESH).wait_recv()
        for cp in sends:
            cp.wait_send()

    return pl.pallas_call(
        body, name="grads_join_halves",
        in_specs=[HBM_SPEC] * na, out_specs=[HBM_SPEC] * na,
        out_shape=[jax.ShapeDtypeStruct(f.shape, f.dtype) for f in fulls],
        input_output_aliases={a: a for a in range(na)},
        scratch_shapes=[pltpu.SemaphoreType.DMA((na,)), pltpu.SemaphoreType.DMA((na,))],
    )(*fulls)


def _part_rows(shape):
    size = 1
    for d in shape:
        size *= d
    rows = -(-size // PACK_COLS)
    return size, -(-rows // PACK_ROW_ALIGN) * PACK_ROW_ALIGN


def _pack_rows(arrays, dtype, total_rows):
    parts, used = [], 0
    for a in arrays:
        size, rows = _part_rows(a.shape)
        flat = a.reshape(-1).astype(dtype)
        parts.append(jnp.pad(flat, (0, rows * PACK_COLS - size)).reshape(rows, PACK_COLS))
        used += rows
    parts.append(jnp.zeros((total_rows - used, PACK_COLS), dtype))
    return jnp.concatenate(parts, axis=0)


def _unpack_rows(buf, shapes):
    lead = buf.shape[:-2]
    out, off = [], 0
    for sh in shapes:
        size, rows = _part_rows(sh)
        part = buf[..., off:off + rows, :].reshape(lead + (-1,))[..., :size]
        out.append(part.reshape(lead + tuple(sh)))
        off += rows
    return out


def _permute_w_in(w):
    z = lambda n: jnp.zeros((w.shape[0], n), w.dtype)
    return jnp.concatenate([
        w[:, 928:1440], w[:, 1440:1952], w[:, 1952:2464],
        w[:, 416:928], w[:, 2464:2976], w[:, 3744:4256],
        w[:, 2976:3488],
        w[:, 0:256], w[:, 256:384], z(64), w[:, 384:416], z(32),
        w[:, 3488:3616], w[:, 3616:3744]], axis=1)


def _unpermute_dw_in(d):
    return jnp.concatenate([
        d[:, 3584:3840], d[:, 3840:3968], d[:, 4032:4064], d[:, 1536:2048],
        d[:, 0:512], d[:, 512:1024], d[:, 1024:1536], d[:, 2048:2560],
        d[:, 3072:3584], d[:, 4096:4224], d[:, 4224:4352], d[:, 2560:3072]], axis=1)


def _rope_tables(s):
    half = MLA_ROPE // 2
    inv_freq = jnp.power(jnp.float32(ROPE_THETA), -jnp.arange(half, dtype=F32) / half)
    ang = jnp.arange(s, dtype=F32)[:, None] * inv_freq[None, :]
    cos, sin = jnp.cos(ang), jnp.sin(ang)
    z = lambda n: jnp.zeros((s, n), F32)
    c = jnp.concatenate([jnp.ones((s, MLA_NOPE), F32), cos, cos, z(32)], axis=1)
    s1 = jnp.concatenate([z(MLA_NOPE), -sin, z(16), z(32)], axis=1)
    s2 = jnp.concatenate([z(MLA_NOPE), z(16), sin, z(32)], axis=1)
    return c, s1, s2


def _pad_lanes(a, n):
    return jnp.pad(a, ((0, 0), (0, n - a.shape[1])))


SHARDED = ("w_in", "w_out", "mla_w_qb", "mla_w_kvb", "conv_w")
REPLICATED = ("norm_g", "mla_q_a_norm", "mla_kv_a_norm", "mla_q_norm", "mla_k_norm",
              "swa_q_norm", "swa_k_norm", "swa_sinks")
WEIGHT_ORDER = ("norm_g", "w_in", "mla_q_a_norm", "mla_w_qb", "mla_kv_a_norm", "mla_w_kvb",
                "mla_q_norm", "mla_k_norm", "conv_w", "swa_q_norm", "swa_k_norm", "swa_sinks", "w_out")
SHARD_AXIS = {"w_in": 2, "w_out": 1, "mla_w_qb": 2, "mla_w_kvb": 2, "conv_w": 2}


def kernel(x, norm_g, w_in, mla_q_a_norm, mla_w_qb, mla_kv_a_norm, mla_w_kvb, mla_q_norm, mla_k_norm, conv_w, swa_q_norm, swa_k_norm, swa_sinks, w_out, loss_target, m_norm_g, m_w_in, m_mla_q_a_norm, m_mla_w_qb, m_mla_kv_a_norm, m_mla_w_kvb, m_mla_q_norm, m_mla_k_norm, m_conv_w, m_swa_q_norm, m_swa_k_norm, m_swa_sinks, m_w_out, v_norm_g, v_w_in, v_mla_q_a_norm, v_mla_w_qb, v_mla_kv_a_norm, v_mla_w_kvb, v_mla_q_norm, v_mla_k_norm, v_conv_w, v_swa_q_norm, v_swa_k_norm, v_swa_sinks, v_w_out):
    weights = dict(norm_g=norm_g, w_in=w_in, mla_q_a_norm=mla_q_a_norm, mla_w_qb=mla_w_qb,
                   mla_kv_a_norm=mla_kv_a_norm, mla_w_kvb=mla_w_kvb, mla_q_norm=mla_q_norm,
                   mla_k_norm=mla_k_norm, conv_w=conv_w, swa_q_norm=swa_q_norm,
                   swa_k_norm=swa_k_norm, swa_sinks=swa_sinks, w_out=w_out)
    mom_m = dict(norm_g=m_norm_g, w_in=m_w_in, mla_q_a_norm=m_mla_q_a_norm, mla_w_qb=m_mla_w_qb,
                 mla_kv_a_norm=m_mla_kv_a_norm, mla_w_kvb=m_mla_w_kvb, mla_q_norm=m_mla_q_norm,
                 mla_k_norm=m_mla_k_norm, conv_w=m_conv_w, swa_q_norm=m_swa_q_norm,
                 swa_k_norm=m_swa_k_norm, swa_sinks=m_swa_sinks, w_out=m_w_out)
    mom_v = dict(norm_g=v_norm_g, w_in=v_w_in, mla_q_a_norm=v_mla_q_a_norm, mla_w_qb=v_mla_w_qb,
                 mla_kv_a_norm=v_mla_kv_a_norm, mla_w_kvb=v_mla_w_kvb, mla_q_norm=v_mla_q_norm,
                 mla_k_norm=v_mla_k_norm, conv_w=v_conv_w, swa_q_norm=v_swa_q_norm,
                 swa_k_norm=v_swa_k_norm, swa_sinks=v_swa_sinks, w_out=v_w_out)
    xs = x[0]
    target = loss_target[0]
    s = xs.shape[0]
    c_idx = lax.axis_index("c").astype(jnp.int32).reshape(1)
    k_idx = (2 * lax.axis_index("x") + lax.axis_index("y")).astype(jnp.int32).reshape(1)

    conv_bits = lax.bitcast_convert_type(conv_w, BF16)
    small_list = [w_out, mla_w_qb, mla_w_kvb, conv_bits]
    w_in_rows = DEPTH * D_MODEL
    own = [w_in.astype(BF16).reshape(w_in_rows, w_in.shape[2]), _pack_rows(small_list, BF16, PACK_ROWS)]
    gathered_in, gathered_rest = [_fill_own_slab(buf, src, k_idx)
                                  for buf, src in zip(_all_gather(own), own)]
    parts = _unpack_rows(gathered_rest, [a.shape for a in small_list])
    join = lambda p, axis: jnp.concatenate([p[k] for k in range(N_CHIPS)], axis=axis)
    w_in_full = join(gathered_in.reshape(N_CHIPS, DEPTH, D_MODEL, w_in.shape[2]), 2)
    w_out_full = join(parts[0], 1)
    w_qb_full = join(parts[1], 2)
    w_kvb_full = join(parts[2], 2)
    conv_full = lax.bitcast_convert_type(join(parts[3], 2), F32)

    rope = _rope_tables(s)
    layers = []
    for l in range(DEPTH):
        wq = jnp.pad(w_qb_full[l].reshape(MLA_Q_LORA, MLA_HEADS, MLA_QK),
                     ((0, 0), (0, 0), (0, LANES - MLA_QK))).reshape(MLA_Q_LORA, MLA_HEADS * LANES)
        kv = w_kvb_full[l].reshape(MLA_KV_LORA, MLA_HEADS, MLA_NOPE + MLA_V)
        wk = jnp.pad(kv[:, :, :MLA_NOPE], ((0, 0), (0, 0), (0, LANES - MLA_NOPE)))
        wkv = jnp.concatenate([wk.reshape(MLA_KV_LORA, MLA_HEADS * LANES),
                               kv[:, :, MLA_NOPE:].reshape(MLA_KV_LORA, MLA_HEADS * MLA_V)], axis=1)
        layers.append(dict(
            w_in=_permute_w_in(w_in_full[l]), w_out=w_out_full[l], wq=wq, wkv=wkv,
            cw=jnp.pad(conv_full[l], ((0, HALO - 3), (0, 0))),
            g=norm_g[l][None], qan=mla_q_a_norm[l][None], kvan=mla_kv_a_norm[l][None],
            qn=_pad_lanes(mla_q_norm[l][None], LANES), kn=_pad_lanes(mla_k_norm[l][None], LANES),
            sqn=swa_q_norm[l][None], skn=swa_k_norm[l][None], sinks=swa_sinks[l][None]))

    saved = []
    h_in = xs
    for l in range(DEPTH):
        p = layers[l]
        proj, hb = _in_proj_fwd(h_in, p["g"], p["w_in"])
        q, k, v, qt, kt, vt = _mla_prep_fwd(proj, p["qan"], p["kvan"], p["qn"], p["kn"], p["wq"], p["wkv"], rope)
        o_mla, lse = _mla_attn_fwd(q, k, vt)
        o_swa = _swa_fwd(proj, p["sqn"], p["skn"], p["sinks"])
        y, z = _out_fwd(proj, o_mla, o_swa, h_in, p["w_out"], p["cw"])
        saved.append(dict(x=h_in, proj=proj, hb=hb, q=q, k=k, v=v, qt=qt, kt=kt, o_mla=o_mla, lse=lse,
                          o_swa=o_swa, z=z))
        h_in = y

    dy, loss_acc = _loss_head(h_in, target)
    loss = lax.psum(loss_acc[0, 0], ("x", "y", "c"))

    grads = {n: [None] * DEPTH for n in WEIGHT_ORDER}
    for l in reversed(range(DEPTH)):
        p, a = layers[l], saved[l]
        dconv, dgates, do_mla, do_mla_t, delta, do_swa, dcw = _out_bwd(dy, a["proj"], a["o_mla"], a["o_swa"],
                                                             p["w_out"], p["cw"])
        grads["w_out"][l] = _matmul_tn(a["z"], dy, "dw_out")
        grads["conv_w"][l] = dcw[0:3]
        delta_rows = jnp.transpose(delta, (1, 0)).reshape(MLA_HEADS // 2, 2, s)
        dq, dk, dv = _mla_attn_bwd(a["q"], a["qt"], a["k"], a["kt"], a["v"], do_mla, do_mla_t,
                                   a["lse"], delta_rows)
        dmla, dqan, dkvan, dqn, dkn, dwq, dwkv = _mla_prep_bwd(
            a["proj"], p["qan"], p["kvan"], p["qn"], p["kn"], p["wq"], p["wkv"], rope, dq, dk, dv)
        dsq, dskv, dsqn, dskn, dsinks = _swa_bwd(a["proj"], p["sqn"], p["skn"], p["sinks"], do_swa)
        pieces = [dconv, dgates, dsq, dmla, dskv]
        dx, dg = _in_proj_bwd(pieces, a["x"], p["g"], p["w_in"], dy)
        dw_in = jnp.concatenate([_matmul_tn(a["hb"], pc, "dw_in_%d" % n)
                                 for n, pc in enumerate(pieces)], axis=1)
        grads["w_in"][l] = _unpermute_dw_in(dw_in)
        grads["norm_g"][l] = dg[0]
        grads["mla_q_a_norm"][l] = dqan[0]
        grads["mla_kv_a_norm"][l] = dkvan[0]
        grads["mla_q_norm"][l] = dqn[0, :MLA_QK]
        grads["mla_k_norm"][l] = dkn[0, :MLA_QK]
        grads["mla_w_qb"][l] = dwq.reshape(MLA_Q_LORA, MLA_HEADS, LANES)[:, :, :MLA_QK].reshape(
            MLA_Q_LORA, MLA_HEADS * MLA_QK)
        dwk = dwkv[:, :MLA_HEADS * LANES].reshape(MLA_KV_LORA, MLA_HEADS, LANES)[:, :, :MLA_NOPE]
        dwv = dwkv[:, MLA_HEADS * LANES:].reshape(MLA_KV_LORA, MLA_HEADS, MLA_V)
        grads["mla_w_kvb"][l] = jnp.concatenate([dwk, dwv], axis=2).reshape(
            MLA_KV_LORA, MLA_HEADS * (MLA_NOPE + MLA_V))
        grads["swa_q_norm"][l] = dsqn[0]
        grads["swa_k_norm"][l] = dskn[0]
        grads["swa_sinks"][l] = dsinks[0]
        dy = dx
    grad_x = dy[None]
    full_grads = {n: jnp.stack(grads[n]) for n in WEIGHT_ORDER}

    rest = tuple(n for n in SHARDED if n != "w_in")
    rep_shapes = [weights[n].shape for n in REPLICATED]
    flat_rep = lambda d: jnp.concatenate([d[n].reshape(-1) for n in REPLICATED])

    def chunk(g, n, k):
        width = g.shape[SHARD_AXIS[n]] // N_CHIPS
        return lax.slice_in_dim(g, k * width, (k + 1) * width, axis=SHARD_AXIS[n])

    g_in = jnp.stack([chunk(full_grads["w_in"], "w_in", k).reshape(w_in_rows, -1)
                      for k in range(N_CHIPS)])
    rep_grads = flat_rep(full_grads)
    g_rest = jnp.stack([_pack_rows([chunk(full_grads[n], n, k) for n in rest] + [rep_grads],
                                   F32, PACK_ROWS) for k in range(N_CHIPS)])
    from_sibling = _swap_halves_to_sibling([g_in, g_rest])
    partial = [_add_sibling(g, r, c_idx, dt)
               for g, r, dt in zip((g_in, g_rest), from_sibling, (BF16, F32))]
    by_chip = [_fill_own_slab(q, p, k_idx) for q, p in zip(_scatter_to_chips(partial), partial)]
    g_in_mine, g_rest_mine = _join_halves([_sum_chips(q, c_idx) for q in by_chip])

    pack_rest = lambda d: _pack_rows([d[n] for n in rest] + [flat_rep(d)], F32, PACK_ROWS)
    in_shape = w_in.shape
    res_in = _adamw(g_in_mine, w_in.reshape(w_in_rows, -1), m_w_in.reshape(w_in_rows, -1),
                    v_w_in.reshape(w_in_rows, -1))
    res_rest = _adamw(g_rest_mine, pack_rest(weights), pack_rest(mom_m), pack_rest(mom_v))
    rest_shapes = [weights[n].shape for n in rest] + [(rep_grads.shape[0],)]
    unpacked = []
    for buf_in, buf_rest in zip((g_in_mine,) + tuple(res_in), (g_rest_mine,) + tuple(res_rest)):
        vals = _unpack_rows(buf_rest, rest_shapes)
        group = dict(zip(rest, vals[:-1]))
        group["w_in"] = buf_in.reshape(in_shape)
        off = 0
        for n, sh in zip(REPLICATED, rep_shapes):
            size = sh[0] * sh[1]
            group[n] = vals[-1][off:off + size].reshape(sh)
            off += size
        unpacked.append(group)
    outs = [loss, grad_x]
    for group in unpacked:
        outs += [group[n] for n in WEIGHT_ORDER]
    return tuple(outs)
```

```python
import jax
import numpy as np
import jax.numpy as jnp
from jax import lax
from jax.experimental import pallas as pl
from jax.experimental.pallas import tpu as pltpu

F32 = jnp.float32
BF16 = jnp.bfloat16

D_MODEL = 1024
DEPTH = 2
GROUP = 512
D_MIX = 3 * GROUP
BLOCK = 128
RMS_EPS = 1e-6
NEG_INF = -1e30
MLA_HEADS = 8
MLA_QK = 96
MLA_NOPE = 64
MLA_ROPE = 32
MLA_V = 64
MLA_Q_LORA = 256
MLA_KV_LORA = 128
ROPE_THETA = 10000.0
SWA_HEADS = 8
SWA_KV = 2
SWA_GROUP = 4
SWA_DIM = 64
IN_COLS = 4256
N_CHIPS = 4

NC = 4352
OFF_CONV, OFF_GATES, OFF_SQ, OFF_MLA, OFF_SKV = 0, 1536, 3072, 3584, 4096
PIECE_WIDTHS = (1536, 1536, 512, 512, 256)

VMEM_LIMIT = 56 * 1024 * 1024
LANES = 128
PACK_COLS = 1024
PACK_ROW_ALIGN = 16
PACK_ROWS = 1024

ADAM_LR = 0.001
ADAM_B1 = 0.9
ADAM_B2 = 0.999
ADAM_EPS = 1e-08
ADAM_WD = 0.01
ADAM_STEP = 10

MESH = pl.DeviceIdType.MESH


def _params(sem, vmem=VMEM_LIMIT):
    return pltpu.CompilerParams(dimension_semantics=sem, vmem_limit_bytes=vmem)


def _dot(a, b, dims):
    return lax.dot_general(a.astype(BF16), b.astype(BF16), (dims, ((), ())),
                           preferred_element_type=F32)


def _mm(a, b):
    return _dot(a, b, ((1,), (0,)))


def _mm_nt(a, b):
    return _dot(a, b, ((1,), (1,)))


def _mm_tn(a, b):
    return _dot(a, b, ((0,), (0,)))


@jax.custom_vjp
def _mmd(a, b):
    return _mm(a, b)


def _mmd_fwd(a, b):
    return _mm(a, b), (a, b)


def _mmd_bwd(res, g):
    a, b = res
    return _mm_nt(g, b), _mm_tn(a, g)


_mmd.defvjp(_mmd_fwd, _mmd_bwd)


@jax.custom_vjp
def _mmd_nt(a, b):
    return _mm_nt(a, b)


def _mmd_nt_fwd(a, b):
    return _mm_nt(a, b), (a, b)


def _mmd_nt_bwd(res, g):
    a, b = res
    return _mm(g, b), _mm_tn(g, a)


_mmd_nt.defvjp(_mmd_nt_fwd, _mmd_nt_bwd)


def _rms(x, g, n=None):
    n = x.shape[-1] if n is None else n
    ms = jnp.sum(x * x, axis=-1, keepdims=True) * (1.0 / n)
    return x * lax.rsqrt(ms + RMS_EPS) * g


def _sigmoid(x):
    return 1.0 / (1.0 + jnp.exp(-x))


@jax.custom_vjp
def _rope(t, c, s1, s2):
    return t * c + pltpu.roll(t, LANES - 16, 1) * s1 + pltpu.roll(t, 16, 1) * s2


def _rope_fwd(t, c, s1, s2):
    return _rope(t, c, s1, s2), (c, s1, s2)


def _rope_bwd(res, g):
    c, s1, s2 = res
    dt = g * c + pltpu.roll(g * s1, 16, 1) + pltpu.roll(g * s2, LANES - 16, 1)
    return dt, jnp.zeros_like(c), jnp.zeros_like(s1), jnp.zeros_like(s2)


_rope.defvjp(_rope_fwd, _rope_bwd)


def _in_proj_fwd(x, g, w):
    s = x.shape[0]
    tm = min(256, s)

    def body(x_ref, g_ref, w_ref, proj_ref, hb_ref):
        hb = _rms(x_ref[...], g_ref[...]).astype(BF16)
        hb_ref[...] = hb
        proj_ref[...] = jnp.dot(hb, w_ref[...], preferred_element_type=F32)

    return pl.pallas_call(
        body, name="in_proj_fwd", grid=(s // tm,),
        in_specs=[pl.BlockSpec((tm, D_MODEL), lambda i: (i, 0)),
                  pl.BlockSpec((1, D_MODEL), lambda i: (0, 0)),
                  pl.BlockSpec((D_MODEL, NC), lambda i: (0, 0))],
        out_specs=[pl.BlockSpec((tm, NC), lambda i: (i, 0)),
                   pl.BlockSpec((tm, D_MODEL), lambda i: (i, 0))],
        out_shape=[jax.ShapeDtypeStruct((s, NC), F32), jax.ShapeDtypeStruct((s, D_MODEL), BF16)],
        compiler_params=_params(("parallel",)),
    )(x, g, w)


def _in_proj_bwd(pieces, x, g, w, dres):
    s = x.shape[0]
    tm = min(256, s)
    n_p = len(pieces)

    def body(*refs):
        p_refs = refs[:n_p]
        x_ref, g_ref, w_ref, dres_ref, dx_ref, dg_ref = refs[n_p:]
        dh = None
        off = 0
        for r in p_refs:
            width = r.shape[1]
            t = _mm_nt(r[...], w_ref[:, off:off + width])
            dh = t if dh is None else dh + t
            off += width
        _, vjp = jax.vjp(_rms, x_ref[...], g_ref[...])
        dx, dg = vjp(dh)
        dx_ref[...] = dx + dres_ref[...]

        @pl.when(pl.program_id(0) == 0)
        def _():
            dg_ref[...] = jnp.zeros_like(dg_ref)

        dg_ref[...] += dg

    in_specs = [pl.BlockSpec((tm, p.shape[1]), lambda i: (i, 0)) for p in pieces]
    in_specs += [pl.BlockSpec((tm, D_MODEL), lambda i: (i, 0)),
                 pl.BlockSpec((1, D_MODEL), lambda i: (0, 0)),
                 pl.BlockSpec((D_MODEL, NC), lambda i: (0, 0)),
                 pl.BlockSpec((tm, D_MODEL), lambda i: (i, 0))]
    return pl.pallas_call(
        body, name="in_proj_bwd", grid=(s // tm,),
        in_specs=in_specs,
        out_specs=[pl.BlockSpec((tm, D_MODEL), lambda i: (i, 0)),
                   pl.BlockSpec((1, D_MODEL), lambda i: (0, 0))],
        out_shape=[jax.ShapeDtypeStruct((s, D_MODEL), F32), jax.ShapeDtypeStruct((1, D_MODEL), F32)],
        compiler_params=_params(("arbitrary",)),
    )(*pieces, x, g, w, dres)


def _matmul_tn(a, b, name):
    s, m = a.shape
    n = b.shape[1]
    tk = min(512, s)
    tn = min(512, n)

    def body(a_ref, b_ref, o_ref):
        @pl.when(pl.program_id(1) == 0)
        def _():
            o_ref[...] = jnp.zeros_like(o_ref)

        o_ref[...] += _mm_tn(a_ref[...], b_ref[...])

    return pl.pallas_call(
        body, name=name, grid=(n // tn, s // tk),
        in_specs=[pl.BlockSpec((tk, m), lambda j, k: (k, 0)),
                  pl.BlockSpec((tk, tn), lambda j, k: (k, j))],
        out_specs=pl.BlockSpec((m, tn), lambda j, k: (0, j)),
        out_shape=jax.ShapeDtypeStruct((m, n), F32),
        compiler_params=_params(("parallel", "arbitrary")),
    )(a, b)


def _prep_fn(q_lat, kv_lat, kr, qan, kvan, qn, kn, wq, wk, wv, c, s1, s2, mm):
    rq = _rms(q_lat, qan)
    rkv = _rms(kv_lat, kvan)
    qs, ks = [], []
    for h in range(MLA_HEADS):
        qs.append(_rope(_rms(mm(rq, wq[h]), qn, MLA_QK), c, s1, s2))
        ks.append(_rope(_rms(mm(rkv, wk[h]) + kr, kn, MLA_QK), c, s1, s2))
    return tuple(qs), tuple(ks), mm(rkv, wv)


def _prep_weights(wq_ref, wkv_ref):
    wq = tuple(wq_ref[:, LANES * h:LANES * (h + 1)].astype(F32) for h in range(MLA_HEADS))
    wk = tuple(wkv_ref[:, LANES * h:LANES * (h + 1)].astype(F32) for h in range(MLA_HEADS))
    wv = wkv_ref[:, LANES * MLA_HEADS:].astype(F32)
    return wq, wk, wv


def _prep_in_specs(tm):
    const = lambda shape: pl.BlockSpec(shape, lambda i: (0, 0))
    return [pl.BlockSpec((tm, 512), lambda i: (i, OFF_MLA // 512)),
            const((1, MLA_Q_LORA)), const((1, MLA_KV_LORA)), const((1, LANES)), const((1, LANES)),
            const((MLA_Q_LORA, 1024)), const((MLA_KV_LORA, 1536)),
            pl.BlockSpec((tm, LANES), lambda i: (i, 0)),
            pl.BlockSpec((tm, LANES), lambda i: (i, 0)),
            pl.BlockSpec((tm, LANES), lambda i: (i, 0))]


def _mla_prep_fwd(proj, qan, kvan, qn, kn, wq, wkv, rope):
    s = proj.shape[0]
    tm = min(512, s)

    def body(blk_ref, qan_ref, kvan_ref, qn_ref, kn_ref, wq_ref, wkv_ref, c_ref, s1_ref, s2_ref,
             q_ref, k_ref, v_ref, qt_ref, kt_ref, vt_ref):
        wq_h, wk_h, wv = _prep_weights(wq_ref, wkv_ref)
        qs, ks, v = _prep_fn(blk_ref[:, 0:256], blk_ref[:, 256:384], blk_ref[:, 384:512],
                             qan_ref[...], kvan_ref[...], qn_ref[...], kn_ref[...],
                             wq_h, wk_h, wv, c_ref[...], s1_ref[...], s2_ref[...], _mm)
        for h in range(MLA_HEADS):
            q2 = qs[h] * Q_PRESCALE
            q_ref[:, LANES * h:LANES * (h + 1)] = q2.astype(BF16)
            k_ref[:, LANES * h:LANES * (h + 1)] = ks[h].astype(BF16)
            qt_ref[LANES * h:LANES * (h + 1), :] = jnp.transpose(q2).astype(BF16)
            kt_ref[LANES * h:LANES * (h + 1), :] = jnp.transpose(ks[h]).astype(BF16)
        v_ref[...] = v.astype(BF16)
        vt_ref[...] = jnp.transpose(v).astype(BF16)

    row = lambda width: pl.BlockSpec((tm, width), lambda i: (i, 0))
    col = lambda height: pl.BlockSpec((height, tm), lambda i: (0, i))
    return pl.pallas_call(
        body, name="mla_prep_fwd", grid=(s // tm,),
        in_specs=_prep_in_specs(tm),
        out_specs=[row(1024), row(1024), row(512), col(1024), col(1024), col(512)],
        out_shape=[jax.ShapeDtypeStruct((s, 1024), BF16), jax.ShapeDtypeStruct((s, 1024), BF16),
                   jax.ShapeDtypeStruct((s, 512), BF16), jax.ShapeDtypeStruct((1024, s), BF16),
                   jax.ShapeDtypeStruct((1024, s), BF16), jax.ShapeDtypeStruct((512, s), BF16)],
        compiler_params=_params(("parallel",)),
    )(proj, qan, kvan, qn, kn, wq, wkv, *rope)


def _mla_prep_bwd(proj, qan, kvan, qn, kn, wq, wkv, rope, dq, dk, dv):
    s = proj.shape[0]
    tm = min(256, s)

    def body(blk_ref, qan_ref, kvan_ref, qn_ref, kn_ref, wq_ref, wkv_ref, c_ref, s1_ref, s2_ref,
             dq_ref, dk_ref, dv_ref,
             dblk_ref, dqan_ref, dkvan_ref, dqn_ref, dkn_ref, dwq_ref, dwkv_ref):
        wq_h, wk_h, wv = _prep_weights(wq_ref, wkv_ref)
        c, s1, s2 = c_ref[...], s1_ref[...], s2_ref[...]

        def fn(q_lat, kv_lat, kr, qan_, kvan_, qn_, kn_, wq_, wk_, wv_):
            return _prep_fn(q_lat, kv_lat, kr, qan_, kvan_, qn_, kn_, wq_, wk_, wv_, c, s1, s2, _mmd)

        _, vjp = jax.vjp(fn, blk_ref[:, 0:256], blk_ref[:, 256:384], blk_ref[:, 384:512],
                         qan_ref[...], kvan_ref[...], qn_ref[...], kn_ref[...], wq_h, wk_h, wv)
        cts = (tuple(jnp.transpose(dq_ref[LANES * h:LANES * (h + 1), :]) for h in range(MLA_HEADS)),
               tuple(jnp.transpose(dk_ref[LANES * h:LANES * (h + 1), :]) for h in range(MLA_HEADS)),
               jnp.transpose(dv_ref[...]))
        dq_lat, dkv_lat, dkr, dqan, dkvan, dqn, dkn, dwq_h, dwk_h, dwv = vjp(cts)
        dblk_ref[:, 0:256] = dq_lat
        dblk_ref[:, 256:384] = dkv_lat
        dblk_ref[:, 384:512] = dkr

        @pl.when(pl.program_id(0) == 0)
        def _():
            for r in (dqan_ref, dkvan_ref, dqn_ref, dkn_ref, dwq_ref, dwkv_ref):
                r[...] = jnp.zeros_like(r)

        dqan_ref[...] += dqan
        dkvan_ref[...] += dkvan
        dqn_ref[...] += dqn
        dkn_ref[...] += dkn
        for h in range(MLA_HEADS):
            dwq_ref[:, LANES * h:LANES * (h + 1)] += dwq_h[h]
            dwkv_ref[:, LANES * h:LANES * (h + 1)] += dwk_h[h]
        dwkv_ref[:, LANES * MLA_HEADS:] += dwv

    const = lambda shape: pl.BlockSpec(shape, lambda i: (0, 0))
    row = lambda width: pl.BlockSpec((tm, width), lambda i: (i, 0))
    shapes = [(s, 512), (1, MLA_Q_LORA), (1, MLA_KV_LORA), (1, LANES), (1, LANES),
              (MLA_Q_LORA, 1024), (MLA_KV_LORA, 1536)]
    return pl.pallas_call(
        body, name="mla_prep_bwd", grid=(s // tm,),
        in_specs=_prep_in_specs(tm) + [pl.BlockSpec((height, tm), lambda i: (0, i))
                                       for height in (1024, 1024, 512)],
        out_specs=[row(512)] + [const(sh) for sh in shapes[1:]],
        out_shape=[jax.ShapeDtypeStruct(sh, F32) for sh in shapes],
        compiler_params=_params(("arbitrary",)),
    )(proj, qan, kvan, qn, kn, wq, wkv, *rope, dq, dk, dv)


MLA_SCALE = MLA_QK ** -0.5


LOG2E = 1.4426950408889634
LN2 = 0.6931471805599453
Q_PRESCALE = MLA_SCALE * LOG2E


def _mla_attn_fwd(q2, k, vt):
    s = q2.shape[0]
    t = min(512, s)
    nq = s // t

    def body(q_ref, k_ref, vt_ref, o_ref, lse_ref, acc_ref):
        i = pl.program_id(1)
        row = lax.broadcasted_iota(jnp.int32, (t, t), 0)
        col = lax.broadcasted_iota(jnp.int32, (t, t), 1)
        causal_t = row <= col
        qh = [q_ref[:, LANES * hh:LANES * (hh + 1)] for hh in range(2)]
        acc_ref[...] = jnp.zeros_like(acc_ref)

        def scores(j, masked):
            r0 = pl.multiple_of(j * t, t)
            out = []
            for hh in range(2):
                kc = k_ref[pl.ds(r0, t), LANES * hh:LANES * (hh + 1)]
                sc = lax.dot_general(kc, qh[hh], (((1,), (1,)), ((), ())),
                                     preferred_element_type=F32)
                out.append(jnp.where(causal_t, sc, NEG_INF) if masked else sc)
            return tuple(out)

        def consume(j, scs, stats):
            r0 = pl.multiple_of(j * t, t)
            out, ps, alphas = [], [], []
            for hh in range(2):
                m, l = stats[hh]
                m_new = jnp.maximum(m, jnp.max(scs[hh], axis=0, keepdims=True))
                p = jnp.exp2(scs[hh] - m_new)
                alpha = jnp.exp2(m - m_new)
                out.append((m_new, alpha * l + jnp.sum(p, axis=0, keepdims=True)))
                ps.append(p.astype(BF16))
                alphas.append(alpha)
            for hh in range(2):
                vc = vt_ref[MLA_V * hh:MLA_V * (hh + 1), pl.ds(r0, t)]
                acc_ref[hh] = alphas[hh] * acc_ref[hh] + jnp.dot(vc, ps[hh],
                                                                 preferred_element_type=F32)
            return tuple(out)

        def trip(j, carry):
            scs, stats = carry
            nxt = scores(j, False)
            stats = consume(jnp.where(j == 0, i, j - 1), scs, stats)
            return nxt, stats

        init = tuple((jnp.full((1, t), NEG_INF, F32), jnp.zeros((1, t), F32)) for _ in range(2))
        scs, stats = lax.fori_loop(0, i, trip, (scores(i, True), init))
        stats = consume(jnp.where(i == 0, i, i - 1), scs, stats)
        for hh in range(2):
            m, l = stats[hh]
            o_ref[:, MLA_V * hh:MLA_V * (hh + 1)] = jnp.transpose(acc_ref[hh] / l)
            lse_ref[0, hh:hh + 1, :] = m + jnp.log2(l)

    return pl.pallas_call(
        body, name="mla_attn_fwd", grid=(MLA_HEADS // 2, nq),
        in_specs=[pl.BlockSpec((t, 256), lambda p, i: (i, p)),
                  pl.BlockSpec((s, 256), lambda p, i: (0, p)),
                  pl.BlockSpec((128, s), lambda p, i: (p, 0))],
        out_specs=[pl.BlockSpec((t, 128), lambda p, i: (i, p)),
                   pl.BlockSpec((1, 2, t), lambda p, i: (p, 0, i))],
        out_shape=[jax.ShapeDtypeStruct((s, 512), F32),
                   jax.ShapeDtypeStruct((MLA_HEADS // 2, 2, s), F32)],
        scratch_shapes=[pltpu.VMEM((2, MLA_V, t), F32)],
        compiler_params=_params(("parallel", "arbitrary")),
    )(q2, k, vt)


def _mla_attn_bwd(q2, q2t, k, kt, v, do, dot, lse_rows, delta_rows):
    s = q2.shape[0]
    t = min(512, s)
    nq = s // t

    def body(q_ref, qt_ref, k_ref, kt_ref, v_ref, do_ref, dot_ref, lse_ref, dl_ref,
             dq_ref, dk_ref, dv_ref):
        j = pl.program_id(1)

        @pl.when(j == 0)
        def _():
            dq_ref[...] = jnp.zeros_like(dq_ref)

        dk_ref[...] = jnp.zeros_like(dk_ref)
        dv_ref[...] = jnp.zeros_like(dv_ref)
        row = lax.broadcasted_iota(jnp.int32, (t, t), 0)
        col = lax.broadcasted_iota(jnp.int32, (t, t), 1)
        causal_t = row <= col
        kh = [k_ref[:, LANES * hh:LANES * (hh + 1)] for hh in range(2)]
        kth = [kt_ref[LANES * hh:LANES * (hh + 1), :] for hh in range(2)]
        vh = [v_ref[:, MLA_V * hh:MLA_V * (hh + 1)] for hh in range(2)]
        nt = (((1,), (1,)), ((), ()))

        def step(i, masked):
            r0 = pl.multiple_of(i * t, t)
            sd = []
            for hh in range(2):
                qh = q_ref[pl.ds(r0, t), LANES * hh:LANES * (hh + 1)]
                doh = do_ref[pl.ds(r0, t), MLA_V * hh:MLA_V * (hh + 1)]
                sc_t = lax.dot_general(kh[hh], qh, nt, preferred_element_type=F32)
                sd.append(jnp.where(causal_t, sc_t, NEG_INF) if masked else sc_t)
                sd.append(lax.dot_general(vh[hh], doh, nt, preferred_element_type=F32))
            pts, gts = [], []
            for hh in range(2):
                lse = lse_ref[0, hh:hh + 1, pl.ds(r0, t)]
                dl = dl_ref[0, hh:hh + 1, pl.ds(r0, t)]
                p_t = jnp.exp2(sd[2 * hh] - lse)
                pts.append(p_t.astype(BF16))
                gts.append((p_t * (sd[2 * hh + 1] - dl)).astype(BF16))
            for hh in range(2):
                qth = qt_ref[LANES * hh:LANES * (hh + 1), pl.ds(r0, t)]
                doth = dot_ref[MLA_V * hh:MLA_V * (hh + 1), pl.ds(r0, t)]
                dv_ref[MLA_V * hh:MLA_V * (hh + 1), :] += lax.dot_general(
                    doth, pts[hh], nt, preferred_element_type=F32)
                dk_ref[LANES * hh:LANES * (hh + 1), :] += lax.dot_general(
                    qth, gts[hh], nt, preferred_element_type=F32)
                dq_ref[LANES * hh:LANES * (hh + 1), pl.ds(r0, t)] += jnp.dot(
                    kth[hh], gts[hh], preferred_element_type=F32)

        step(j, True)

        def trip(i, carry):
            step(i, False)
            return carry

        lax.fori_loop(j + 1, nq, trip, 0)
        dk_ref[...] = dk_ref[...] * LN2

        @pl.when(j == nq - 1)
        def _():
            dq_ref[...] = dq_ref[...] * MLA_SCALE

    return pl.pallas_call(
        body, name="mla_attn_bwd", grid=(MLA_HEADS // 2, nq),
        in_specs=[pl.BlockSpec((s, 256), lambda p, j: (0, p)),
                  pl.BlockSpec((256, s), lambda p, j: (p, 0)),
                  pl.BlockSpec((t, 256), lambda p, j: (j, p)),
                  pl.BlockSpec((256, t), lambda p, j: (p, j)),
                  pl.BlockSpec((t, 128), lambda p, j: (j, p)),
                  pl.BlockSpec((s, 128), lambda p, j: (0, p)),
                  pl.BlockSpec((128, s), lambda p, j: (p, 0)),
                  pl.BlockSpec((1, 2, s), lambda p, j: (p, 0, 0)),
                  pl.BlockSpec((1, 2, s), lambda p, j: (p, 0, 0))],
        out_specs=[pl.BlockSpec((256, s), lambda p, j: (p, 0)),
                   pl.BlockSpec((256, t), lambda p, j: (p, j)),
                   pl.BlockSpec((128, t), lambda p, j: (p, j))],
        out_shape=[jax.ShapeDtypeStruct((1024, s), F32), jax.ShapeDtypeStruct((1024, s), F32),
                   jax.ShapeDtypeStruct((512, s), F32)],
        compiler_params=_params(("parallel", "arbitrary")),
    )(q2, q2t, k, kt, v, do, dot, lse_rows, delta_rows)


SWA_SCALE = SWA_DIM ** -0.5
SWA_COLS = SWA_GROUP * BLOCK
SWA_LOG2 = SWA_SCALE * LOG2E


def _swa_tables():
    k = np.arange(2 * BLOCK)[:, None]
    col = np.arange(SWA_COLS)[None, :]
    dist = BLOCK + (col % BLOCK) - k
    valid = (dist >= 0) & (dist < BLOCK)
    out = np.zeros((2, SWA_KV, 2 * BLOCK, SWA_COLS), np.float32)
    for first in range(2):
        ok = valid & ((k >= BLOCK) | (first == 0))
        for j in range(SWA_KV):
            slope = 2.0 ** -(SWA_GROUP * j + col // BLOCK + 1)
            out[first, j] = np.where(ok, -slope * dist * LOG2E, NEG_INF)
    return jnp.asarray(out)


def _swa_tile_inputs(sq_ref, skv_ref, halo_ref, qn_ref, kn_ref, sk_ref, add_ref, first):
    kv_all = jnp.concatenate([halo_ref[...], skv_ref[...]], axis=0)
    k_raw = [kv_all[:, SWA_DIM * j:SWA_DIM * (j + 1)] for j in range(SWA_KV)]
    v_all = [kv_all[:, 128 + SWA_DIM * j:128 + SWA_DIM * (j + 1)] for j in range(SWA_KV)]
    q_raw = [sq_ref[:, SWA_DIM * h:SWA_DIM * (h + 1)] for h in range(SWA_HEADS)]
    lane_grp = lax.broadcasted_iota(jnp.int32, (1, SWA_COLS), 1) // BLOCK
    sinks, adds = [], []
    for j in range(SWA_KV):
        row = jnp.zeros((1, SWA_COLS), F32)
        for g in range(SWA_GROUP):
            h = SWA_GROUP * j + g
            row = jnp.where(lane_grp == g, sk_ref[:, h:h + 1] * LOG2E, row)
        sinks.append(row)
        adds.append((jnp.where(first, add_ref[1, j], add_ref[0, j]), add_ref[0, j]))
    return k_raw, v_all, q_raw, sinks, adds


def _swa_probs(kb, qs, add, sink):
    s2 = lax.dot_general(kb, qs, (((1,), (1,)), ((), ())), preferred_element_type=F32) * SWA_LOG2 + add
    m = jnp.maximum(jnp.max(s2, axis=0, keepdims=True), sink)
    e = jnp.exp2(s2 - m)
    es = jnp.exp2(sink - m)
    inv = 1.0 / (jnp.sum(e, axis=0, keepdims=True) + es)
    return e, inv, es


def _swa_fwd(proj, qn, kn, sinks, tables):
    s = proj.shape[0]
    ts = min(512, s)
    nb = ts // BLOCK

    def body(sq_ref, skv_ref, halo_ref, qn_ref, kn_ref, sk_ref, add_ref, o_ref, vt_ref):
        first = pl.program_id(0) == 0
        k_raw, v_all, q_raw, sink_rows, adds = _swa_tile_inputs(
            sq_ref, skv_ref, halo_ref, qn_ref, kn_ref, sk_ref, add_ref, first)
        kn_ = [_rms(k, kn_ref[...]).astype(BF16) for k in k_raw]
        for j in range(SWA_KV):
            vt_ref[j] = jnp.transpose(v_all[j]).astype(BF16)
        qn_ = [_rms(q, qn_ref[...]).astype(BF16) for q in q_raw]
        for b in range(nb):
            for j in range(SWA_KV):
                qs = jnp.concatenate([qn_[SWA_GROUP * j + g][BLOCK * b:BLOCK * (b + 1)]
                                      for g in range(SWA_GROUP)], axis=0)
                e, inv, _ = _swa_probs(kn_[j][BLOCK * b:BLOCK * (b + 2)], qs,
                                       adds[j][0 if b == 0 else 1], sink_rows[j])
                o_t = jnp.dot(vt_ref[j, :, BLOCK * b:BLOCK * (b + 2)], (e * inv).astype(BF16),
                              preferred_element_type=F32)
                o = jnp.transpose(o_t)
                for g in range(SWA_GROUP):
                    h = SWA_GROUP * j + g
                    o_ref[BLOCK * b:BLOCK * (b + 1), SWA_DIM * h:SWA_DIM * (h + 1)] = (
                        o[BLOCK * g:BLOCK * (g + 1)])

    const = lambda shape: pl.BlockSpec(shape, lambda i: (0,) * len(shape))
    return pl.pallas_call(
        body, name="swa_fwd", grid=(s // ts,),
        in_specs=[pl.BlockSpec((ts, 512), lambda i: (i, OFF_SQ // 512)),
                  pl.BlockSpec((ts, 256), lambda i: (i, OFF_SKV // 256)),
                  pl.BlockSpec((BLOCK, 256), lambda i: (jnp.maximum(i * nb - 1, 0), OFF_SKV // 256)),
                  const((1, SWA_DIM)), const((1, SWA_DIM)), const((1, SWA_HEADS)),
                  const(tables.shape)],
        out_specs=pl.BlockSpec((ts, 512), lambda i: (i, 0)),
        out_shape=jax.ShapeDtypeStruct((s, 512), F32),
        scratch_shapes=[pltpu.VMEM((SWA_KV, SWA_DIM, ts + BLOCK), BF16)],
        compiler_params=_params(("parallel",)),
    )(proj, proj, proj, qn, kn, sinks, tables)


def _swa_bwd(proj, qn, kn, sinks, tables, do):
    s = proj.shape[0]
    ts = min(512, s)
    nb = ts // BLOCK
    nt = s // ts

    def body(sq_ref, skv_ref, halo_ref, qn_ref, kn_ref, sk_ref, add_ref, do_ref,
             dsq_ref, dskv_ref, dqn_ref, dkn_ref, dsk_ref, carry_ref, knt_ref):
        step = pl.program_id(0)
        first = step == nt - 1

        @pl.when(step == 0)
        def _():
            carry_ref[...] = jnp.zeros_like(carry_ref)
            dqn_ref[...] = jnp.zeros_like(dqn_ref)
            dkn_ref[...] = jnp.zeros_like(dkn_ref)
            dsk_ref[...] = jnp.zeros_like(dsk_ref)

        k_raw, v_all, q_raw, sink_rows, adds = _swa_tile_inputs(
            sq_ref, skv_ref, halo_ref, qn_ref, kn_ref, sk_ref, add_ref, first)
        kn_f = [_rms(k, kn_ref[...]) for k in k_raw]
        kn_ = [k.astype(BF16) for k in kn_f]
        for j in range(SWA_KV):
            knt_ref[j] = jnp.transpose(kn_f[j]).astype(BF16)
        vb = [v.astype(BF16) for v in v_all]
        qn_ = [_rms(q, qn_ref[...]).astype(BF16) for q in q_raw]

        zero = jnp.zeros((BLOCK, SWA_DIM), F32)
        dk_chunks = [[zero] * (nb + 1) for _ in range(SWA_KV)]
        dv_chunks = [[zero] * (nb + 1) for _ in range(SWA_KV)]
        dq_blocks = [[None] * nb for _ in range(SWA_HEADS)]
        dsink = [jnp.zeros((1, SWA_COLS), F32) for _ in range(SWA_KV)]
        for b in range(nb):
            rows = slice(BLOCK * b, BLOCK * (b + 1))
            band = slice(BLOCK * b, BLOCK * (b + 2))
            for j in range(SWA_KV):
                heads = [SWA_GROUP * j + g for g in range(SWA_GROUP)]
                qs = jnp.concatenate([qn_[h][rows] for h in heads], axis=0)
                dos = jnp.concatenate([do_ref[rows, SWA_DIM * h:SWA_DIM * (h + 1)] for h in heads],
                                      axis=0).astype(BF16)
                e, inv, es = _swa_probs(kn_[j][band], qs, adds[j][0 if b == 0 else 1], sink_rows[j])
                p = e * inv
                dp = lax.dot_general(vb[j][band], dos, (((1,), (1,)), ((), ())),
                                     preferred_element_type=F32)
                dsum = jnp.sum(p * dp, axis=0, keepdims=True)
                dsink[j] = dsink[j] - es * inv * dsum
                g_t = (p * (dp - dsum) * SWA_SCALE).astype(BF16)
                dv_b = jnp.dot(p.astype(BF16), dos, preferred_element_type=F32)
                dk_b = jnp.dot(g_t, qs, preferred_element_type=F32)
                dq_s = jnp.transpose(jnp.dot(knt_ref[j, :, band], g_t,
                                             preferred_element_type=F32))
                for half in range(2):
                    part = slice(BLOCK * half, BLOCK * (half + 1))
                    dk_chunks[j][b + half] = dk_chunks[j][b + half] + dk_b[part]
                    dv_chunks[j][b + half] = dv_chunks[j][b + half] + dv_b[part]
                for g, h in enumerate(heads):
                    dq_blocks[h][b] = dq_s[BLOCK * g:BLOCK * (g + 1)]

        dqn = jnp.zeros((1, SWA_DIM), F32)
        for h in range(SWA_HEADS):
            _, vjp = jax.vjp(_rms, q_raw[h], qn_ref[...])
            dq, dg = vjp(jnp.concatenate(dq_blocks[h], axis=0))
            dsq_ref[:, SWA_DIM * h:SWA_DIM * (h + 1)] = dq
            dqn = dqn + dg
        dqn_ref[...] += dqn
        dkn = jnp.zeros((1, SWA_DIM), F32)
        lane_grp = lax.broadcasted_iota(jnp.int32, (1, SWA_COLS), 1) // BLOCK
        for j in range(SWA_KV):
            _, vjp = jax.vjp(_rms, k_raw[j], kn_ref[...])
            dk, dg = vjp(jnp.concatenate(dk_chunks[j], axis=0))
            dkn = dkn + dg
            dv = jnp.concatenate(dv_chunks[j], axis=0)
            for part, val in ((0, dk), (1, dv)):
                c0 = 128 * part + SWA_DIM * j
                dskv_ref[0:ts - BLOCK, c0:c0 + SWA_DIM] = val[BLOCK:ts]
                dskv_ref[ts - BLOCK:ts, c0:c0 + SWA_DIM] = (
                    val[ts:ts + BLOCK] + carry_ref[:, c0:c0 + SWA_DIM])
                carry_ref[:, c0:c0 + SWA_DIM] = val[0:BLOCK]
            for g in range(SWA_GROUP):
                h = SWA_GROUP * j + g
                dsk_ref[:, h:h + 1] += jnp.sum(jnp.where(lane_grp == g, dsink[j], 0.0), axis=1,
                                               keepdims=True)
        dkn_ref[...] += dkn

    const = lambda shape: pl.BlockSpec(shape, lambda st: (0,) * len(shape))
    return pl.pallas_call(
        body, name="swa_bwd", grid=(nt,),
        in_specs=[pl.BlockSpec((ts, 512), lambda st: (nt - 1 - st, OFF_SQ // 512)),
                  pl.BlockSpec((ts, 256), lambda st: (nt - 1 - st, OFF_SKV // 256)),
                  pl.BlockSpec((BLOCK, 256),
                               lambda st: (jnp.maximum((nt - 1 - st) * nb - 1, 0), OFF_SKV // 256)),
                  const((1, SWA_DIM)), const((1, SWA_DIM)), const((1, SWA_HEADS)),
                  const(tables.shape),
                  pl.BlockSpec((ts, 512), lambda st: (nt - 1 - st, 0))],
        out_specs=[pl.BlockSpec((ts, 512), lambda st: (nt - 1 - st, 0)),
                   pl.BlockSpec((ts, 256), lambda st: (nt - 1 - st, 0)),
                   const((1, SWA_DIM)), const((1, SWA_DIM)), const((1, SWA_HEADS))],
        out_shape=[jax.ShapeDtypeStruct((s, 512), F32), jax.ShapeDtypeStruct((s, 256), F32),
                   jax.ShapeDtypeStruct((1, SWA_DIM), F32), jax.ShapeDtypeStruct((1, SWA_DIM), F32),
                   jax.ShapeDtypeStruct((1, SWA_HEADS), F32)],
        scratch_shapes=[pltpu.VMEM((BLOCK, 256), F32),
                        pltpu.VMEM((SWA_KV, SWA_DIM, ts + BLOCK), BF16)],
        compiler_params=_params(("arbitrary",)),
    )(proj, proj, proj, qn, kn, sinks, tables, do)


HALO = 8


def _shift_down(u, halo, k):
    tm = u.shape[0]
    rid = lax.broadcasted_iota(jnp.int32, u.shape, 0)
    out = pltpu.roll(u, k, 0)
    for r in range(k):
        out = jnp.where(rid == r, halo[HALO - k + r:HALO - k + r + 1, :], out)
    return out


def _shift_up(u, halo, k):
    tm = u.shape[0]
    rid = lax.broadcasted_iota(jnp.int32, u.shape, 0)
    out = pltpu.roll(u, tm - k, 0)
    for r in range(k):
        out = jnp.where(rid == tm - k + r, halo[r:r + 1, :], out)
    return out


def _conv_fwd_vals(conv_ref, convp_ref, cw_ref, is_first):
    c_h, c_b, c_c = conv_ref[:, 0:512], conv_ref[:, 512:1024], conv_ref[:, 1024:1536]
    u = c_c * c_h
    up = jnp.where(is_first, 0.0, convp_ref[:, 1024:1536] * convp_ref[:, 0:512])
    u1 = _shift_down(u, up, 1)
    u2 = _shift_down(u, up, 2)
    yc = cw_ref[0:1, :] * u2 + cw_ref[1:2, :] * u1 + cw_ref[2:3, :] * u
    return c_h, c_b, c_c, u, u1, u2, yc


def _out_fwd(proj, o_mla, o_swa, x, w_out, cw):
    s = proj.shape[0]
    tm = min(512, s)

    def body(conv_ref, convp_ref, gates_ref, om_ref, os_ref, x_ref, w_ref, cw_ref, y_ref, z_ref):
        i = pl.program_id(0)
        _, c_b, _, _, _, _, yc = _conv_fwd_vals(conv_ref, convp_ref, cw_ref, i == 0)
        mix = (om_ref[...], c_b * yc, os_ref[...])
        for n in range(3):
            g = gates_ref[:, GROUP * n:GROUP * (n + 1)]
            z_ref[:, GROUP * n:GROUP * (n + 1)] = (mix[n] * (g * _sigmoid(g))).astype(BF16)
        y_ref[...] = x_ref[...] + jnp.dot(z_ref[...], w_ref[...], preferred_element_type=F32)

    row = lambda width: pl.BlockSpec((tm, width), lambda i: (i, 0))
    return pl.pallas_call(
        body, name="out_fwd", grid=(s // tm,),
        in_specs=[pl.BlockSpec((tm, 1536), lambda i: (i, 0)),
                  pl.BlockSpec((HALO, 1536), lambda i: (jnp.maximum(i * (tm // HALO) - 1, 0), 0)),
                  pl.BlockSpec((tm, 1536), lambda i: (i, 1)),
                  row(512), row(512), row(D_MODEL),
                  pl.BlockSpec((D_MIX, D_MODEL), lambda i: (0, 0)),
                  pl.BlockSpec((HALO, 512), lambda i: (0, 0))],
        out_specs=[row(D_MODEL), row(D_MIX)],
        out_shape=[jax.ShapeDtypeStruct((s, D_MODEL), F32), jax.ShapeDtypeStruct((s, D_MIX), BF16)],
        compiler_params=_params(("parallel",)),
    )(proj, proj, proj, o_mla, o_swa, x, w_out, cw)


def _out_bwd(dy, proj, o_mla, o_swa, w_out, cw):
    s = proj.shape[0]
    tm = min(512, s)
    nt = s // tm
    hb = tm // HALO

    def body(dy_ref, dyn_ref, conv_ref, convp_ref, convn_ref, gates_ref, gatesn_ref, om_ref, os_ref,
             w_ref, cw_ref,
             dconv_ref, dgates_ref, dom_ref, domt_ref, delta_ref, dos_ref, dcw_ref):
        i = pl.program_id(0)
        dz = _mm_nt(dy_ref[...], w_ref[...])

        def gate(n):
            g = gates_ref[:, GROUP * n:GROUP * (n + 1)]
            sg = _sigmoid(g)
            return g * sg, sg * (1.0 + g * (1.0 - sg))

        for n, o_ref, do_ref in ((0, om_ref, dom_ref), (2, os_ref, dos_ref)):
            silu, dsilu = gate(n)
            dzn = dz[:, GROUP * n:GROUP * (n + 1)]
            o = o_ref[...]
            do = dzn * silu
            do_ref[...] = do.astype(do_ref.dtype)
            dgates_ref[:, GROUP * n:GROUP * (n + 1)] = dzn * o * dsilu
            if n == 0:
                domt_ref[...] = jnp.transpose(do).astype(BF16)
                t = do * o
                for h in range(MLA_HEADS):
                    delta_ref[:, h:h + 1] = jnp.sum(t[:, MLA_V * h:MLA_V * (h + 1)], axis=-1,
                                                    keepdims=True)

        c_h, c_b, c_c, u, u1, u2, yc = _conv_fwd_vals(conv_ref, convp_ref, cw_ref, i == 0)
        silu, dsilu = gate(1)
        dzc = dz[:, GROUP:2 * GROUP]
        dgates_ref[:, GROUP:2 * GROUP] = dzc * (c_b * yc) * dsilu
        dycr = dzc * silu
        dyc = dycr * c_b
        gn = gatesn_ref[:, GROUP:2 * GROUP]
        dzc_n = _mm_nt(dyn_ref[...], w_ref[GROUP:2 * GROUP, :])
        dyc_n = jnp.where(i == nt - 1, 0.0, dzc_n * (gn * _sigmoid(gn)) * convn_ref[:, 512:1024])
        d1 = _shift_up(dyc, dyc_n, 1)
        d2 = _shift_up(dyc, dyc_n, 2)
        du = cw_ref[2:3, :] * dyc + cw_ref[1:2, :] * d1 + cw_ref[0:1, :] * d2
        dconv_ref[:, 0:512] = du * c_c
        dconv_ref[:, 512:1024] = dycr * yc
        dconv_ref[:, 1024:1536] = du * c_h

        @pl.when(i == 0)
        def _():
            dcw_ref[...] = jnp.zeros_like(dcw_ref)

        for k, uk in enumerate((u2, u1, u)):
            dcw_ref[k:k + 1, :] += jnp.sum(dyc * uk, axis=0, keepdims=True)

    row = lambda width: pl.BlockSpec((tm, width), lambda i: (i, 0))
    prev = lambda i: jnp.maximum(i * hb - 1, 0)
    nxt = lambda i: jnp.minimum((i + 1) * hb, s // HALO - 1)
    return pl.pallas_call(
        body, name="out_bwd", grid=(nt,),
        in_specs=[row(D_MODEL),
                  pl.BlockSpec((HALO, D_MODEL), lambda i: (nxt(i), 0)),
                  pl.BlockSpec((tm, 1536), lambda i: (i, 0)),
                  pl.BlockSpec((HALO, 1536), lambda i: (prev(i), 0)),
                  pl.BlockSpec((HALO, 1536), lambda i: (nxt(i), 0)),
                  pl.BlockSpec((tm, 1536), lambda i: (i, 1)),
                  pl.BlockSpec((HALO, 1536), lambda i: (nxt(i), 1)),
                  row(512), row(512),
                  pl.BlockSpec((D_MIX, D_MODEL), lambda i: (0, 0)),
                  pl.BlockSpec((HALO, 512), lambda i: (0, 0))],
        out_specs=[row(1536), row(1536), row(512), pl.BlockSpec((512, tm), lambda i: (0, i)),
                   row(MLA_HEADS), row(512), pl.BlockSpec((HALO, 512), lambda i: (0, 0))],
        out_shape=[jax.ShapeDtypeStruct((s, 1536), F32), jax.ShapeDtypeStruct((s, 1536), F32),
                   jax.ShapeDtypeStruct((s, 512), BF16), jax.ShapeDtypeStruct((512, s), BF16),
                   jax.ShapeDtypeStruct((s, MLA_HEADS), F32),
                   jax.ShapeDtypeStruct((s, 512), F32), jax.ShapeDtypeStruct((HALO, 512), F32)],
        compiler_params=_params(("arbitrary",)),
    )(dy, dy, proj, proj, proj, proj, proj, o_mla, o_swa, w_out, cw)


def _loss_head(y, target):
    s, d = y.shape
    tm = min(512, s)
    nt = s // tm

    def body(y_ref, t_ref, dy_ref, loss_ref):
        i = pl.program_id(0)
        err = y_ref[...] - t_ref[...]
        dy_ref[...] = err * (1.0 / d)

        @pl.when(i == 0)
        def _():
            loss_ref[...] = jnp.zeros_like(loss_ref)

        sq = jnp.sum((err * err).reshape(tm // 8, 8, d), axis=0)
        part = sq[:, 0:LANES]
        for c in range(1, d // LANES):
            part = part + sq[:, LANES * c:LANES * (c + 1)]
        loss_ref[...] += part

        @pl.when(i == nt - 1)
        def _():
            loss_ref[...] = jnp.full(loss_ref.shape, (0.5 / d) * jnp.sum(loss_ref[...]), F32)

    return pl.pallas_call(
        body, name="loss_head", grid=(nt,),
        in_specs=[pl.BlockSpec((tm, d), lambda i: (i, 0)), pl.BlockSpec((tm, d), lambda i: (i, 0))],
        out_specs=[pl.BlockSpec((tm, d), lambda i: (i, 0)), pl.BlockSpec((8, LANES), lambda i: (0, 0))],
        out_shape=[jax.ShapeDtypeStruct((s, d), F32), jax.ShapeDtypeStruct((8, LANES), F32)],
        compiler_params=_params(("arbitrary",)),
    )(y, target)


def _adamw(g, w, m, v):
    rows = g.shape[0]
    tr = min(256, rows)
    c1 = 1.0 - ADAM_B1
    c2 = 1.0 - ADAM_B2
    bc1 = 1.0 - ADAM_B1 ** ADAM_STEP
    bc2 = 1.0 - ADAM_B2 ** ADAM_STEP

    def body(g_ref, w_ref, m_ref, v_ref, d_ref, mo_ref, vo_ref):
        gg = g_ref[...]
        m_new = ADAM_B1 * m_ref[...] + c1 * gg
        v_new = ADAM_B2 * v_ref[...] + c2 * (gg * gg)
        m_hat = m_new / bc1
        v_hat = v_new / bc2
        d_ref[...] = -ADAM_LR * (m_hat / (jnp.sqrt(v_hat) + ADAM_EPS) + ADAM_WD * w_ref[...])
        mo_ref[...] = m_new
        vo_ref[...] = v_new

    spec = pl.BlockSpec((tr, g.shape[1]), lambda i: (i, 0))
    return pl.pallas_call(
        body, name="adamw", grid=(rows // tr,),
        in_specs=[spec] * 4, out_specs=[spec] * 3,
        out_shape=[jax.ShapeDtypeStruct(g.shape, F32)] * 3,
        compiler_params=_params(("parallel",)),
    )(g, w, m, v)


HBM_SPEC = pl.BlockSpec(memory_space=pltpu.HBM)


def _place():
    x, y, c = lax.axis_index("x"), lax.axis_index("y"), lax.axis_index("c")
    chips = [(1 - x, y), (x, 1 - y), (1 - x, 1 - y)]
    return x, y, c, chips


def _all_gather(shards):
    na = len(shards)
    halves = [sh.shape[0] // 2 for sh in shards]

    def body(*refs):
        w_refs, a_refs = refs[:na], refs[na:2 * na]
        send_sems, recv_sems = refs[2 * na:]
        x, y, c, chips = _place()
        k = 2 * x + y
        sib = (x, y, 1 - c)

        def slab(a, kk, hc):
            return a_refs[a].at[kk, pl.ds(hc * halves[a], halves[a]), :]

        def copy(a, n, src, dst, to):
            return pltpu.make_async_remote_copy(
                src_ref=src, dst_ref=dst, send_sem=send_sems.at[6 * a + n],
                recv_sem=recv_sems.at[6 * a + n], device_id=to, device_id_type=MESH)

        first = [copy(a, n, w_refs[a].at[pl.ds(c * halves[a], halves[a]), :], slab(a, k, c),
                      (cx, cy, c))
                 for n, (cx, cy) in enumerate(chips) for a in range(na)]
        for cp in first:
            cp.start()
        passed = []
        for n, (cx, cy) in enumerate(chips):
            kk = 2 * cx + cy
            for a in range(na):
                copy(a, n, slab(a, kk, c), slab(a, kk, c), (cx, cy, c)).wait_recv()
                fwd = copy(a, 3 + n, slab(a, kk, c), slab(a, kk, c), sib)
                fwd.start()
                passed.append(fwd)
        for n, (cx, cy) in enumerate(chips):
            kk = 2 * cx + cy
            for a in range(na):
                copy(a, 3 + n, slab(a, kk, 1 - c), slab(a, kk, 1 - c), sib).wait_recv()
        for cp in first + passed:
            cp.wait_send()

    return pl.pallas_call(
        body, name="weights_all_gather",
        in_specs=[HBM_SPEC] * na, out_specs=[HBM_SPEC] * na,
        out_shape=[jax.ShapeDtypeStruct((N_CHIPS,) + sh.shape, sh.dtype) for sh in shards],
        scratch_shapes=[pltpu.SemaphoreType.DMA((6 * na,)), pltpu.SemaphoreType.DMA((6 * na,))],
    )(*shards)


def _fill_own_slab(buf, src, k_idx):
    n, rows, cols = buf.shape
    tr = _row_tile(rows)
    slabs = src.ndim == 3

    def body(k_ref, src_ref, buf_ref, out_ref):
        out_ref[0] = src_ref[0] if slabs else src_ref[...]

    if slabs:
        src_spec = pl.BlockSpec((1, tr, cols), lambda t, k_ref: (k_ref[0], t, 0))
    else:
        src_spec = pl.BlockSpec((tr, cols), lambda t, k_ref: (t, 0))
    return pl.pallas_call(
        body, name="fill_own_slab",
        grid_spec=pltpu.PrefetchScalarGridSpec(
            num_scalar_prefetch=1, grid=(rows // tr,),
            in_specs=[src_spec, pl.BlockSpec(memory_space=pl.ANY)],
            out_specs=pl.BlockSpec((1, tr, cols), lambda t, k_ref: (k_ref[0], t, 0))),
        out_shape=jax.ShapeDtypeStruct(buf.shape, buf.dtype),
        input_output_aliases={2: 0},
        compiler_params=_params(("parallel",)),
    )(k_idx, src, buf)


def _swap_halves_to_sibling(gs):
    na = len(gs)

    def body(*refs):
        g_refs, r_refs = refs[:na], refs[na:2 * na]
        send_sems, recv_sems = refs[2 * na:]
        x, y, c, _ = _place()
        cps = []
        for a in range(na):
            half = g_refs[a].shape[1] // 2
            cps.append(pltpu.make_async_remote_copy(
                src_ref=g_refs[a].at[:, pl.ds((1 - c) * half, half), :], dst_ref=r_refs[a],
                send_sem=send_sems.at[a], recv_sem=recv_sems.at[a], device_id=(x, y, 1 - c),
                device_id_type=MESH))
        for cp in cps:
            cp.start()
        for cp in cps:
            cp.wait()

    return pl.pallas_call(
        body, name="grads_to_sibling",
        in_specs=[HBM_SPEC] * na, out_specs=[HBM_SPEC] * na,
        out_shape=[jax.ShapeDtypeStruct((g.shape[0], g.shape[1] // 2, g.shape[2]), g.dtype)
                   for g in gs],
        scratch_shapes=[pltpu.SemaphoreType.DMA((na,)), pltpu.SemaphoreType.DMA((na,))],
    )(*gs)


def _row_tile(rows):
    return 256 if rows % 256 == 0 else 128


def _add_sibling(g, r, c_idx, out_dtype):
    n, rows, cols = g.shape
    half = rows // 2
    tr = _row_tile(half)
    nb = half // tr

    def body(c_ref, g_ref, r_ref, p_ref):
        p_ref[...] = (g_ref[...] + r_ref[...]).astype(out_dtype)

    return pl.pallas_call(
        body, name="grads_add_sibling",
        grid_spec=pltpu.PrefetchScalarGridSpec(
            num_scalar_prefetch=1, grid=(n, nb),
            in_specs=[pl.BlockSpec((1, tr, cols), lambda j, t, c_ref: (j, c_ref[0] * nb + t, 0)),
                      pl.BlockSpec((1, tr, cols), lambda j, t, c_ref: (j, t, 0))],
            out_specs=pl.BlockSpec((1, tr, cols), lambda j, t, c_ref: (j, t, 0))),
        out_shape=jax.ShapeDtypeStruct((n, half, cols), out_dtype),
        compiler_params=_params(("parallel", "parallel")),
    )(c_idx, g, r)


def _scatter_to_chips(ps):
    na = len(ps)

    def body(*refs):
        p_refs, q_refs = refs[:na], refs[na:2 * na]
        send_sems, recv_sems = refs[2 * na:]
        x, y, c, chips = _place()
        k = 2 * x + y
        sends = []
        for i, (cx, cy) in enumerate(chips):
            for a in range(na):
                cp = pltpu.make_async_remote_copy(
                    src_ref=p_refs[a].at[2 * cx + cy], dst_ref=q_refs[a].at[k],
                    send_sem=send_sems.at[3 * a + i], recv_sem=recv_sems.at[3 * a + i],
                    device_id=(cx, cy, c), device_id_type=MESH)
                cp.start()
                sends.append(cp)
        for i, (cx, cy) in enumerate(chips):
            kk = 2 * cx + cy
            for a in range(na):
                pltpu.make_async_remote_copy(
                    src_ref=p_refs[a].at[kk], dst_ref=q_refs[a].at[kk],
                    send_sem=send_sems.at[3 * a + i], recv_sem=recv_sems.at[3 * a + i],
                    device_id=(cx, cy, c), device_id_type=MESH).wait_recv()
        for cp in sends:
            cp.wait_send()

    return pl.pallas_call(
        body, name="grads_scatter_to_chips",
        in_specs=[HBM_SPEC] * na, out_specs=[HBM_SPEC] * na,
        out_shape=[jax.ShapeDtypeStruct(p.shape, p.dtype) for p in ps],
        scratch_shapes=[pltpu.SemaphoreType.DMA((3 * na,)), pltpu.SemaphoreType.DMA((3 * na,))],
    )(*ps)


def _sum_chips(q, c_idx):
    n, half, cols = q.shape
    tr = _row_tile(half)
    nb = half // tr

    def body(c_ref, q_ref, o_ref):
        parts = [q_ref[kk].astype(F32) for kk in range(n)]
        o_ref[...] = ((parts[0] + parts[1]) + parts[2]) + parts[3]

    return pl.pallas_call(
        body, name="grads_sum_chips",
        grid_spec=pltpu.PrefetchScalarGridSpec(
            num_scalar_prefetch=1, grid=(nb,),
            in_specs=[pl.BlockSpec((n, tr, cols), lambda t, c_ref: (0, t, 0))],
            out_specs=pl.BlockSpec((tr, cols), lambda t, c_ref: (c_ref[0] * nb + t, 0))),
        out_shape=jax.ShapeDtypeStruct((2 * half, cols), F32),
        compiler_params=_params(("parallel",)),
    )(c_idx, q)


def _join_halves(fulls):
    na = len(fulls)

    def body(*refs):
        o_refs = refs[na:2 * na]
        send_sems, recv_sems = refs[2 * na:]
        x, y, c, _ = _place()
        sends = []
        for a in range(na):
            half = o_refs[a].shape[0] // 2
            rows = o_refs[a].at[pl.ds(c * half, half), :]
            sends.append(pltpu.make_async_remote_copy(
                src_ref=rows, dst_ref=rows, send_sem=send_sems.at[a], recv_sem=recv_sems.at[a],
                device_id=(x, y, 1 - c), device_id_type=MESH))
        for cp in sends:
            cp.start()
        for a in range(na):
            half = o_refs[a].shape[0] // 2
            other = o_refs[a].at[pl.ds((1 - c) * half, half), :]
            pltpu.make_async_remote_copy(
                src_ref=other, dst_ref=other, send_sem=send_sems.at[a], recv_sem=recv_sems.at[a],
                device_id=(x, y, 1 - c), device_id_type=MESH).wait_recv()
        for cp in sends:
            cp.wait_send()

    return pl.pallas_call(
        body, name="grads_join_halves",
        in_specs=[HBM_SPEC] * na, out_specs=[HBM_SPEC] * na,
        out_shape=[jax.ShapeDtypeStruct(f.shape, f.dtype) for f in fulls],
        input_output_aliases={a: a for a in range(na)},
        scratch_shapes=[pltpu.SemaphoreType.DMA((na,)), pltpu.SemaphoreType.DMA((na,))],
    )(*fulls)


def _part_rows(shape):
    size = 1
    for d in shape:
        size *= d
    rows = -(-size // PACK_COLS)
    return size, -(-rows // PACK_ROW_ALIGN) * PACK_ROW_ALIGN


def _pack_rows(arrays, dtype, total_rows):
    parts, used = [], 0
    for a in arrays:
        size, rows = _part_rows(a.shape)
        flat = a.reshape(-1).astype(dtype)
        parts.append(jnp.pad(flat, (0, rows * PACK_COLS - size)).reshape(rows, PACK_COLS))
        used += rows
    parts.append(jnp.zeros((total_rows - used, PACK_COLS), dtype))
    return jnp.concatenate(parts, axis=0)


def _unpack_rows(buf, shapes):
    lead = buf.shape[:-2]
    out, off = [], 0
    for sh in shapes:
        size, rows = _part_rows(sh)
        part = buf[..., off:off + rows, :].reshape(lead + (-1,))[..., :size]
        out.append(part.reshape(lead + tuple(sh)))
        off += rows
    return out


def _permute_w_in(w):
    z = lambda n: jnp.zeros((w.shape[0], n), w.dtype)
    return jnp.concatenate([
        w[:, 928:1440], w[:, 1440:1952], w[:, 1952:2464],
        w[:, 416:928], w[:, 2464:2976], w[:, 3744:4256],
        w[:, 2976:3488],
        w[:, 0:256], w[:, 256:384], z(64), w[:, 384:416], z(32),
        w[:, 3488:3616], w[:, 3616:3744]], axis=1)


def _unpermute_dw_in(d):
    return jnp.concatenate([
        d[:, 3584:3840], d[:, 3840:3968], d[:, 4032:4064], d[:, 1536:2048],
        d[:, 0:512], d[:, 512:1024], d[:, 1024:1536], d[:, 2048:2560],
        d[:, 3072:3584], d[:, 4096:4224], d[:, 4224:4352], d[:, 2560:3072]], axis=1)


def _rope_tables(s):
    half = MLA_ROPE // 2
    inv_freq = jnp.power(jnp.float32(ROPE_THETA), -jnp.arange(half, dtype=F32) / half)
    ang = jnp.arange(s, dtype=F32)[:, None] * inv_freq[None, :]
    cos, sin = jnp.cos(ang), jnp.sin(ang)
    z = lambda n: jnp.zeros((s, n), F32)
    c = jnp.concatenate([jnp.ones((s, MLA_NOPE), F32), cos, cos, z(32)], axis=1)
    s1 = jnp.concatenate([z(MLA_NOPE), -sin, z(16), z(32)], axis=1)
    s2 = jnp.concatenate([z(MLA_NOPE), z(16), sin, z(32)], axis=1)
    return c, s1, s2


def _pad_lanes(a, n):
    return jnp.pad(a, ((0, 0), (0, n - a.shape[1])))


SHARDED = ("w_in", "w_out", "mla_w_qb", "mla_w_kvb", "conv_w")
REPLICATED = ("norm_g", "mla_q_a_norm", "mla_kv_a_norm", "mla_q_norm", "mla_k_norm",
              "swa_q_norm", "swa_k_norm", "swa_sinks")
WEIGHT_ORDER = ("norm_g", "w_in", "mla_q_a_norm", "mla_w_qb", "mla_kv_a_norm", "mla_w_kvb",
                "mla_q_norm", "mla_k_norm", "conv_w", "swa_q_norm", "swa_k_norm", "swa_sinks", "w_out")
SHARD_AXIS = {"w_in": 2, "w_out": 1, "mla_w_qb": 2, "mla_w_kvb": 2, "conv_w": 2}


def kernel(x, norm_g, w_in, mla_q_a_norm, mla_w_qb, mla_kv_a_norm, mla_w_kvb, mla_q_norm, mla_k_norm, conv_w, swa_q_norm, swa_k_norm, swa_sinks, w_out, loss_target, m_norm_g, m_w_in, m_mla_q_a_norm, m_mla_w_qb, m_mla_kv_a_norm, m_mla_w_kvb, m_mla_q_norm, m_mla_k_norm, m_conv_w, m_swa_q_norm, m_swa_k_norm, m_swa_sinks, m_w_out, v_norm_g, v_w_in, v_mla_q_a_norm, v_mla_w_qb, v_mla_kv_a_norm, v_mla_w_kvb, v_mla_q_norm, v_mla_k_norm, v_conv_w, v_swa_q_norm, v_swa_k_norm, v_swa_sinks, v_w_out):
    weights = dict(norm_g=norm_g, w_in=w_in, mla_q_a_norm=mla_q_a_norm, mla_w_qb=mla_w_qb,
                   mla_kv_a_norm=mla_kv_a_norm, mla_w_kvb=mla_w_kvb, mla_q_norm=mla_q_norm,
                   mla_k_norm=mla_k_norm, conv_w=conv_w, swa_q_norm=swa_q_norm,
                   swa_k_norm=swa_k_norm, swa_sinks=swa_sinks, w_out=w_out)
    mom_m = dict(norm_g=m_norm_g, w_in=m_w_in, mla_q_a_norm=m_mla_q_a_norm, mla_w_qb=m_mla_w_qb,
                 mla_kv_a_norm=m_mla_kv_a_norm, mla_w_kvb=m_mla_w_kvb, mla_q_norm=m_mla_q_norm,
                 mla_k_norm=m_mla_k_norm, conv_w=m_conv_w, swa_q_norm=m_swa_q_norm,
                 swa_k_norm=m_swa_k_norm, swa_sinks=m_swa_sinks, w_out=m_w_out)
    mom_v = dict(norm_g=v_norm_g, w_in=v_w_in, mla_q_a_norm=v_mla_q_a_norm, mla_w_qb=v_mla_w_qb,
                 mla_kv_a_norm=v_mla_kv_a_norm, mla_w_kvb=v_mla_w_kvb, mla_q_norm=v_mla_q_norm,
                 mla_k_norm=v_mla_k_norm, conv_w=v_conv_w, swa_q_norm=v_swa_q_norm,
                 swa_k_norm=v_swa_k_norm, swa_sinks=v_swa_sinks, w_out=v_w_out)
    xs = x[0]
    target = loss_target[0]
    s = xs.shape[0]
    c_idx = lax.axis_index("c").astype(jnp.int32).reshape(1)
    k_idx = (2 * lax.axis_index("x") + lax.axis_index("y")).astype(jnp.int32).reshape(1)

    conv_bits = lax.bitcast_convert_type(conv_w, BF16)
    small_list = [w_out, mla_w_qb, mla_w_kvb, conv_bits]
    w_in_rows = DEPTH * D_MODEL
    own = [w_in.astype(BF16).reshape(w_in_rows, w_in.shape[2]), _pack_rows(small_list, BF16, PACK_ROWS)]
    gathered_in, gathered_rest = [_fill_own_slab(buf, src, k_idx)
                                  for buf, src in zip(_all_gather(own), own)]
    parts = _unpack_rows(gathered_rest, [a.shape for a in small_list])
    join = lambda p, axis: jnp.concatenate([p[k] for k in range(N_CHIPS)], axis=axis)
    w_in_full = join(gathered_in.reshape(N_CHIPS, DEPTH, D_MODEL, w_in.shape[2]), 2)
    w_out_full = join(parts[0], 1)
    w_qb_full = join(parts[1], 2)
    w_kvb_full = join(parts[2], 2)
    conv_full = lax.bitcast_convert_type(join(parts[3], 2), F32)

    rope = _rope_tables(s)
    swa_tables = _swa_tables()
    layers = []
    for l in range(DEPTH):
        wq = jnp.pad(w_qb_full[l].reshape(MLA_Q_LORA, MLA_HEADS, MLA_QK),
                     ((0, 0), (0, 0), (0, LANES - MLA_QK))).reshape(MLA_Q_LORA, MLA_HEADS * LANES)
        kv = w_kvb_full[l].reshape(MLA_KV_LORA, MLA_HEADS, MLA_NOPE + MLA_V)
        wk = jnp.pad(kv[:, :, :MLA_NOPE], ((0, 0), (0, 0), (0, LANES - MLA_NOPE)))
        wkv = jnp.concatenate([wk.reshape(MLA_KV_LORA, MLA_HEADS * LANES),
                               kv[:, :, MLA_NOPE:].reshape(MLA_KV_LORA, MLA_HEADS * MLA_V)], axis=1)
        layers.append(dict(
            w_in=_permute_w_in(w_in_full[l]), w_out=w_out_full[l], wq=wq, wkv=wkv,
            cw=jnp.pad(conv_full[l], ((0, HALO - 3), (0, 0))),
            g=norm_g[l][None], qan=mla_q_a_norm[l][None], kvan=mla_kv_a_norm[l][None],
            qn=_pad_lanes(mla_q_norm[l][None], LANES), kn=_pad_lanes(mla_k_norm[l][None], LANES),
            sqn=swa_q_norm[l][None], skn=swa_k_norm[l][None], sinks=swa_sinks[l][None]))

    saved = []
    h_in = xs
    for l in range(DEPTH):
        p = layers[l]
        proj, hb = _in_proj_fwd(h_in, p["g"], p["w_in"])
        q, k, v, qt, kt, vt = _mla_prep_fwd(proj, p["qan"], p["kvan"], p["qn"], p["kn"], p["wq"], p["wkv"], rope)
        o_mla, lse = _mla_attn_fwd(q, k, vt)
        o_swa = _swa_fwd(proj, p["sqn"], p["skn"], p["sinks"], swa_tables)
        y, z = _out_fwd(proj, o_mla, o_swa, h_in, p["w_out"], p["cw"])
        saved.append(dict(x=h_in, proj=proj, hb=hb, q=q, k=k, v=v, qt=qt, kt=kt, o_mla=o_mla, lse=lse,
                          o_swa=o_swa, z=z))
        h_in = y

    dy, loss_acc = _loss_head(h_in, target)
    loss = lax.psum(loss_acc[0, 0], ("x", "y", "c"))

    grads = {n: [None] * DEPTH for n in WEIGHT_ORDER}
    for l in reversed(range(DEPTH)):
        p, a = layers[l], saved[l]
        dconv, dgates, do_mla, do_mla_t, delta, do_swa, dcw = _out_bwd(dy, a["proj"], a["o_mla"], a["o_swa"],
                                                             p["w_out"], p["cw"])
        grads["w_out"][l] = _matmul_tn(a["z"], dy, "dw_out")
        grads["conv_w"][l] = dcw[0:3]
        delta_rows = jnp.transpose(delta, (1, 0)).reshape(MLA_HEADS // 2, 2, s)
        dq, dk, dv = _mla_attn_bwd(a["q"], a["qt"], a["k"], a["kt"], a["v"], do_mla, do_mla_t,
                                   a["lse"], delta_rows)
        dmla, dqan, dkvan, dqn, dkn, dwq, dwkv = _mla_prep_bwd(
            a["proj"], p["qan"], p["kvan"], p["qn"], p["kn"], p["wq"], p["wkv"], rope, dq, dk, dv)
        dsq, dskv, dsqn, dskn, dsinks = _swa_bwd(a["proj"], p["sqn"], p["skn"], p["sinks"], swa_tables, do_swa)
        pieces = [dconv, dgates, dsq, dmla, dskv]
        dx, dg = _in_proj_bwd(pieces, a["x"], p["g"], p["w_in"], dy)
        dw_in = jnp.concatenate([_matmul_tn(a["hb"], pc, "dw_in_%d" % n)
                                 for n, pc in enumerate(pieces)], axis=1)
        grads["w_in"][l] = _unpermute_dw_in(dw_in)
        grads["norm_g"][l] = dg[0]
        grads["mla_q_a_norm"][l] = dqan[0]
        grads["mla_kv_a_norm"][l] = dkvan[0]
        grads["mla_q_norm"][l] = dqn[0, :MLA_QK]
        grads["mla_k_norm"][l] = dkn[0, :MLA_QK]
        grads["mla_w_qb"][l] = dwq.reshape(MLA_Q_LORA, MLA_HEADS, LANES)[:, :, :MLA_QK].reshape(
            MLA_Q_LORA, MLA_HEADS * MLA_QK)
        dwk = dwkv[:, :MLA_HEADS * LANES].reshape(MLA_KV_LORA, MLA_HEADS, LANES)[:, :, :MLA_NOPE]
        dwv = dwkv[:, MLA_HEADS * LANES:].reshape(MLA_KV_LORA, MLA_HEADS, MLA_V)
        grads["mla_w_kvb"][l] = jnp.concatenate([dwk, dwv], axis=2).reshape(
            MLA_KV_LORA, MLA_HEADS * (MLA_NOPE + MLA_V))
        grads["swa_q_norm"][l] = dsqn[0]
        grads["swa_k_norm"][l] = dskn[0]
        grads["swa_sinks"][l] = dsinks[0]
        dy = dx
    grad_x = dy[None]
    full_grads = {n: jnp.stack(grads[n]) for n in WEIGHT_ORDER}

    rest = tuple(n for n in SHARDED if n != "w_in")
    rep_shapes = [weights[n].shape for n in REPLICATED]
    flat_rep = lambda d: jnp.concatenate([d[n].reshape(-1) for n in REPLICATED])

    def chunk(g, n, k):
        width = g.shape[SHARD_AXIS[n]] // N_CHIPS
        return lax.slice_in_dim(g, k * width, (k + 1) * width, axis=SHARD_AXIS[n])

    g_in = jnp.stack([chunk(full_grads["w_in"], "w_in", k).reshape(w_in_rows, -1)
                      for k in range(N_CHIPS)])
    rep_grads = flat_rep(full_grads)
    g_rest = jnp.stack([_pack_rows([chunk(full_grads[n], n, k) for n in rest] + [rep_grads],
                                   F32, PACK_ROWS) for k in range(N_CHIPS)])
    from_sibling = _swap_halves_to_sibling([g_in, g_rest])
    partial = [_add_sibling(g, r, c_idx, dt)
               for g, r, dt in zip((g_in, g_rest), from_sibling, (BF16, F32))]
    by_chip = [_fill_own_slab(q, p, k_idx) for q, p in zip(_scatter_to_chips(partial), partial)]
    g_in_mine, g_rest_mine = _join_halves([_sum_chips(q, c_idx) for q in by_chip])

    pack_rest = lambda d: _pack_rows([d[n] for n in rest] + [flat_rep(d)], F32, PACK_ROWS)
    in_shape = w_in.shape
    res_in = _adamw(g_in_mine, w_in.reshape(w_in_rows, -1), m_w_in.reshape(w_in_rows, -1),
                    v_w_in.reshape(w_in_rows, -1))
    res_rest = _adamw(g_rest_mine, pack_rest(weights), pack_rest(mom_m), pack_rest(mom_v))
    rest_shapes = [weights[n].shape for n in rest] + [(rep_grads.shape[0],)]
    unpacked = []
    for buf_in, buf_rest in zip((g_in_mine,) + tuple(res_in), (g_rest_mine,) + tuple(res_rest)):
        vals = _unpack_rows(buf_rest, rest_shapes)
        group = dict(zip(rest, vals[:-1]))
        group["w_in"] = buf_in.reshape(in_shape)
        off = 0
        for n, sh in zip(REPLICATED, rep_shapes):
            size = sh[0] * sh[1]
            group[n] = vals[-1][off:off + size].reshape(sh)
            off += size
        unpacked.append(group)
    outs = [loss, grad_x]
    for group in unpacked:
        outs += [group[n] for n in WEIGHT_ORDER]
    return tuple(outs)
```

```python
import jax
import numpy as np
import jax.numpy as jnp
from jax import lax
from jax.experimental import pallas as pl
from jax.experimental.pallas import tpu as pltpu

F32 = jnp.float32
BF16 = jnp.bfloat16

D_MODEL = 1024
DEPTH = 2
GROUP = 512
D_MIX = 3 * GROUP
BLOCK = 128
RMS_EPS = 1e-6
NEG_INF = -1e30
MLA_HEADS = 8
MLA_QK = 96
MLA_NOPE = 64
MLA_ROPE = 32
MLA_V = 64
MLA_Q_LORA = 256
MLA_KV_LORA = 128
ROPE_THETA = 10000.0
SWA_HEADS = 8
SWA_KV = 2
SWA_GROUP = 4
SWA_DIM = 64
IN_COLS = 4256
N_CHIPS = 4

NC = 4352
OFF_CONV, OFF_GATES, OFF_SQ, OFF_MLA, OFF_SKV = 0, 1536, 3072, 3584, 4096
PIECE_WIDTHS = (1536, 1536, 512, 512, 256)

VMEM_LIMIT = 56 * 1024 * 1024
LANES = 128
PACK_COLS = 1024
PACK_ROW_ALIGN = 16
PACK_ROWS = 1024

ADAM_LR = 0.001
ADAM_B1 = 0.9
ADAM_B2 = 0.999
ADAM_EPS = 1e-08
ADAM_WD = 0.01
ADAM_STEP = 10

MESH = pl.DeviceIdType.MESH


def _params(sem, vmem=VMEM_LIMIT):
    return pltpu.CompilerParams(dimension_semantics=sem, vmem_limit_bytes=vmem)


def _dot(a, b, dims):
    return lax.dot_general(a.astype(BF16), b.astype(BF16), (dims, ((), ())),
                           preferred_element_type=F32)


def _mm(a, b):
    return _dot(a, b, ((1,), (0,)))


def _mm_nt(a, b):
    return _dot(a, b, ((1,), (1,)))


def _mm_tn(a, b):
    return _dot(a, b, ((0,), (0,)))


@jax.custom_vjp
def _mmd(a, b):
    return _mm(a, b)


def _mmd_fwd(a, b):
    return _mm(a, b), (a, b)


def _mmd_bwd(res, g):
    a, b = res
    return _mm_nt(g, b), _mm_tn(a, g)


_mmd.defvjp(_mmd_fwd, _mmd_bwd)


def _rms(x, g, n=None):
    n = x.shape[-1] if n is None else n
    ms = jnp.sum(x * x, axis=-1, keepdims=True) * (1.0 / n)
    return x * lax.rsqrt(ms + RMS_EPS) * g


def _sigmoid(x):
    return 1.0 / (1.0 + jnp.exp(-x))


@jax.custom_vjp
def _rope(t, c, s1, s2):
    return t * c + pltpu.roll(t, LANES - 16, 1) * s1 + pltpu.roll(t, 16, 1) * s2


def _rope_fwd(t, c, s1, s2):
    return _rope(t, c, s1, s2), (c, s1, s2)


def _rope_bwd(res, g):
    c, s1, s2 = res
    dt = g * c + pltpu.roll(g * s1, 16, 1) + pltpu.roll(g * s2, LANES - 16, 1)
    return dt, jnp.zeros_like(c), jnp.zeros_like(s1), jnp.zeros_like(s2)


_rope.defvjp(_rope_fwd, _rope_bwd)


def _in_proj_fwd(x, g, w):
    s = x.shape[0]
    tm = min(256, s)

    def body(x_ref, g_ref, w_ref, proj_ref, hb_ref):
        hb = _rms(x_ref[...], g_ref[...]).astype(BF16)
        hb_ref[...] = hb
        proj_ref[...] = jnp.dot(hb, w_ref[...], preferred_element_type=F32)

    return pl.pallas_call(
        body, name="in_proj_fwd", grid=(s // tm,),
        in_specs=[pl.BlockSpec((tm, D_MODEL), lambda i: (i, 0)),
                  pl.BlockSpec((1, D_MODEL), lambda i: (0, 0)),
                  pl.BlockSpec((D_MODEL, NC), lambda i: (0, 0))],
        out_specs=[pl.BlockSpec((tm, NC), lambda i: (i, 0)),
                   pl.BlockSpec((tm, D_MODEL), lambda i: (i, 0))],
        out_shape=[jax.ShapeDtypeStruct((s, NC), F32), jax.ShapeDtypeStruct((s, D_MODEL), BF16)],
        compiler_params=_params(("parallel",)),
    )(x, g, w)


def _in_proj_bwd(pieces, x, g, w, dres):
    s = x.shape[0]
    tm = min(256, s)
    n_p = len(pieces)

    def body(*refs):
        p_refs = refs[:n_p]
        x_ref, g_ref, w_ref, dres_ref, dx_ref, dg_ref = refs[n_p:]
        dh = None
        off = 0
        for r in p_refs:
            width = r.shape[1]
            t = _mm_nt(r[...], w_ref[:, off:off + width])
            dh = t if dh is None else dh + t
            off += width
        _, vjp = jax.vjp(_rms, x_ref[...], g_ref[...])
        dx, dg = vjp(dh)
        dx_ref[...] = dx + dres_ref[...]

        @pl.when(pl.program_id(0) == 0)
        def _():
            dg_ref[...] = jnp.zeros_like(dg_ref)

        dg_ref[...] += dg

    in_specs = [pl.BlockSpec((tm, p.shape[1]), lambda i: (i, 0)) for p in pieces]
    in_specs += [pl.BlockSpec((tm, D_MODEL), lambda i: (i, 0)),
                 pl.BlockSpec((1, D_MODEL), lambda i: (0, 0)),
                 pl.BlockSpec((D_MODEL, NC), lambda i: (0, 0)),
                 pl.BlockSpec((tm, D_MODEL), lambda i: (i, 0))]
    return pl.pallas_call(
        body, name="in_proj_bwd", grid=(s // tm,),
        in_specs=in_specs,
        out_specs=[pl.BlockSpec((tm, D_MODEL), lambda i: (i, 0)),
                   pl.BlockSpec((1, D_MODEL), lambda i: (0, 0))],
        out_shape=[jax.ShapeDtypeStruct((s, D_MODEL), F32), jax.ShapeDtypeStruct((1, D_MODEL), F32)],
        compiler_params=_params(("arbitrary",)),
    )(*pieces, x, g, w, dres)


def _matmul_tn(a, b, name):
    s, m = a.shape
    n = b.shape[1]
    tk = min(512, s)
    tn = min(1536, n)

    def body(a_ref, b_ref, o_ref):
        @pl.when(pl.program_id(1) == 0)
        def _():
            o_ref[...] = jnp.zeros_like(o_ref)

        o_ref[...] += _mm_tn(a_ref[...], b_ref[...])

    return pl.pallas_call(
        body, name=name, grid=(n // tn, s // tk),
        in_specs=[pl.BlockSpec((tk, m), lambda j, k: (k, 0)),
                  pl.BlockSpec((tk, tn), lambda j, k: (k, j))],
        out_specs=pl.BlockSpec((m, tn), lambda j, k: (0, j)),
        out_shape=jax.ShapeDtypeStruct((m, n), F32),
        compiler_params=_params(("parallel", "arbitrary")),
    )(a, b)


def _prep_fn(q_lat, kv_lat, kr, qan, kvan, qn, kn, wq, wk, wv, c, s1, s2, mm):
    rq = _rms(q_lat, qan)
    rkv = _rms(kv_lat, kvan)
    qs, ks = [], []
    for h in range(MLA_HEADS):
        qs.append(_rope(_rms(mm(rq, wq[h]), qn, MLA_QK), c, s1, s2))
        ks.append(_rope(_rms(mm(rkv, wk[h]) + kr, kn, MLA_QK), c, s1, s2))
    return tuple(qs), tuple(ks), mm(rkv, wv)


def _prep_weights(wq_ref, wkv_ref):
    wq = tuple(wq_ref[:, LANES * h:LANES * (h + 1)].astype(F32) for h in range(MLA_HEADS))
    wk = tuple(wkv_ref[:, LANES * h:LANES * (h + 1)].astype(F32) for h in range(MLA_HEADS))
    wv = wkv_ref[:, LANES * MLA_HEADS:].astype(F32)
    return wq, wk, wv


def _prep_in_specs(tm):
    const = lambda shape: pl.BlockSpec(shape, lambda i: (0, 0))
    return [pl.BlockSpec((tm, 512), lambda i: (i, OFF_MLA // 512)),
            const((1, MLA_Q_LORA)), const((1, MLA_KV_LORA)), const((1, LANES)), const((1, LANES)),
            const((MLA_Q_LORA, 1024)), const((MLA_KV_LORA, 1536)),
            pl.BlockSpec((tm, LANES), lambda i: (i, 0)),
            pl.BlockSpec((tm, LANES), lambda i: (i, 0)),
            pl.BlockSpec((tm, LANES), lambda i: (i, 0))]


def _mla_prep_fwd(proj, qan, kvan, qn, kn, wq, wkv, rope):
    s = proj.shape[0]
    tm = min(512, s)

    def body(blk_ref, qan_ref, kvan_ref, qn_ref, kn_ref, wq_ref, wkv_ref, c_ref, s1_ref, s2_ref,
             q_ref, k_ref, v_ref, qt_ref, kt_ref, vt_ref):
        wq_h, wk_h, wv = _prep_weights(wq_ref, wkv_ref)
        qs, ks, v = _prep_fn(blk_ref[:, 0:256], blk_ref[:, 256:384], blk_ref[:, 384:512],
                             qan_ref[...], kvan_ref[...], qn_ref[...], kn_ref[...],
                             wq_h, wk_h, wv, c_ref[...], s1_ref[...], s2_ref[...], _mm)
        for h in range(MLA_HEADS):
            q2 = qs[h] * Q_PRESCALE
            q_ref[:, LANES * h:LANES * (h + 1)] = q2.astype(BF16)
            k_ref[:, LANES * h:LANES * (h + 1)] = ks[h].astype(BF16)
            qt_ref[LANES * h:LANES * (h + 1), :] = jnp.transpose(q2).astype(BF16)
            kt_ref[LANES * h:LANES * (h + 1), :] = jnp.transpose(ks[h]).astype(BF16)
        v_ref[...] = v.astype(BF16)
        vt_ref[...] = jnp.transpose(v).astype(BF16)

    row = lambda width: pl.BlockSpec((tm, width), lambda i: (i, 0))
    col = lambda height: pl.BlockSpec((height, tm), lambda i: (0, i))
    return pl.pallas_call(
        body, name="mla_prep_fwd", grid=(s // tm,),
        in_specs=_prep_in_specs(tm),
        out_specs=[row(1024), row(1024), row(512), col(1024), col(1024), col(512)],
        out_shape=[jax.ShapeDtypeStruct((s, 1024), BF16), jax.ShapeDtypeStruct((s, 1024), BF16),
                   jax.ShapeDtypeStruct((s, 512), BF16), jax.ShapeDtypeStruct((1024, s), BF16),
                   jax.ShapeDtypeStruct((1024, s), BF16), jax.ShapeDtypeStruct((512, s), BF16)],
        compiler_params=_params(("parallel",)),
    )(proj, qan, kvan, qn, kn, wq, wkv, *rope)


def _mla_prep_bwd(proj, qan, kvan, qn, kn, wq, wkv, rope, dq, dk, dv):
    s = proj.shape[0]
    tm = min(256, s)

    def body(blk_ref, qan_ref, kvan_ref, qn_ref, kn_ref, wq_ref, wkv_ref, c_ref, s1_ref, s2_ref,
             dq_ref, dk_ref, dv_ref,
             dblk_ref, dqan_ref, dkvan_ref, dqn_ref, dkn_ref, dwq_ref, dwkv_ref):
        wq_h, wk_h, wv = _prep_weights(wq_ref, wkv_ref)
        c, s1, s2 = c_ref[...], s1_ref[...], s2_ref[...]

        def fn(q_lat, kv_lat, kr, qan_, kvan_, qn_, kn_, wq_, wk_, wv_):
            return _prep_fn(q_lat, kv_lat, kr, qan_, kvan_, qn_, kn_, wq_, wk_, wv_, c, s1, s2, _mmd)

        _, vjp = jax.vjp(fn, blk_ref[:, 0:256], blk_ref[:, 256:384], blk_ref[:, 384:512],
                         qan_ref[...], kvan_ref[...], qn_ref[...], kn_ref[...], wq_h, wk_h, wv)
        cts = (tuple(jnp.transpose(dq_ref[LANES * h:LANES * (h + 1), :]) for h in range(MLA_HEADS)),
               tuple(jnp.transpose(dk_ref[LANES * h:LANES * (h + 1), :]) for h in range(MLA_HEADS)),
               jnp.transpose(dv_ref[...]))
        dq_lat, dkv_lat, dkr, dqan, dkvan, dqn, dkn, dwq_h, dwk_h, dwv = vjp(cts)
        dblk_ref[:, 0:256] = dq_lat.astype(BF16)
        dblk_ref[:, 256:384] = dkv_lat.astype(BF16)
        dblk_ref[:, 384:512] = dkr.astype(BF16)

        @pl.when(pl.program_id(0) == 0)
        def _():
            for r in (dqan_ref, dkvan_ref, dqn_ref, dkn_ref, dwq_ref, dwkv_ref):
                r[...] = jnp.zeros_like(r)

        dqan_ref[...] += dqan
        dkvan_ref[...] += dkvan
        dqn_ref[...] += dqn
        dkn_ref[...] += dkn
        for h in range(MLA_HEADS):
            dwq_ref[:, LANES * h:LANES * (h + 1)] += dwq_h[h]
            dwkv_ref[:, LANES * h:LANES * (h + 1)] += dwk_h[h]
        dwkv_ref[:, LANES * MLA_HEADS:] += dwv

    const = lambda shape: pl.BlockSpec(shape, lambda i: (0, 0))
    row = lambda width: pl.BlockSpec((tm, width), lambda i: (i, 0))
    shapes = [(s, 512), (1, MLA_Q_LORA), (1, MLA_KV_LORA), (1, LANES), (1, LANES),
              (MLA_Q_LORA, 1024), (MLA_KV_LORA, 1536)]
    return pl.pallas_call(
        body, name="mla_prep_bwd", grid=(s // tm,),
        in_specs=_prep_in_specs(tm) + [pl.BlockSpec((height, tm), lambda i: (0, i))
                                       for height in (1024, 1024, 512)],
        out_specs=[row(512)] + [const(sh) for sh in shapes[1:]],
        out_shape=[jax.ShapeDtypeStruct(shapes[0], BF16)]
        + [jax.ShapeDtypeStruct(sh, F32) for sh in shapes[1:]],
        compiler_params=_params(("arbitrary",)),
    )(proj, qan, kvan, qn, kn, wq, wkv, *rope, dq, dk, dv)


MLA_SCALE = MLA_QK ** -0.5


LOG2E = 1.4426950408889634
LN2 = 0.6931471805599453
Q_PRESCALE = MLA_SCALE * LOG2E


def _mla_attn_fwd(q2, k, vt):
    s = q2.shape[0]
    t = min(512, s)
    nq = s // t

    def body(q_ref, k_ref, vt_ref, o_ref, lse_ref, acc_ref):
        i = pl.program_id(1)
        row = lax.broadcasted_iota(jnp.int32, (t, t), 0)
        col = lax.broadcasted_iota(jnp.int32, (t, t), 1)
        causal_t = row <= col
        qh = [q_ref[:, LANES * hh:LANES * (hh + 1)] for hh in range(2)]
        acc_ref[...] = jnp.zeros_like(acc_ref)

        def scores(j, masked):
            r0 = pl.multiple_of(j * t, t)
            out = []
            for hh in range(2):
                kc = k_ref[pl.ds(r0, t), LANES * hh:LANES * (hh + 1)]
                sc = lax.dot_general(kc, qh[hh], (((1,), (1,)), ((), ())),
                                     preferred_element_type=F32)
                out.append(jnp.where(causal_t, sc, NEG_INF) if masked else sc)
            return tuple(out)

        def consume(j, scs, stats):
            r0 = pl.multiple_of(j * t, t)
            out, ps, alphas = [], [], []
            for hh in range(2):
                m, l = stats[hh]
                m_new = jnp.maximum(m, jnp.max(scs[hh], axis=0, keepdims=True))
                p = jnp.exp2(scs[hh] - m_new)
                alpha = jnp.exp2(m - m_new)
                out.append((m_new, alpha * l + jnp.sum(p, axis=0, keepdims=True)))
                ps.append(p.astype(BF16))
                alphas.append(alpha)
            for hh in range(2):
                vc = vt_ref[MLA_V * hh:MLA_V * (hh + 1), pl.ds(r0, t)]
                acc_ref[hh] = alphas[hh] * acc_ref[hh] + jnp.dot(vc, ps[hh],
                                                                 preferred_element_type=F32)
            return tuple(out)

        def trip(j, carry):
            scs, stats = carry
            nxt = scores(j, False)
            stats = consume(jnp.where(j == 0, i, j - 1), scs, stats)
            return nxt, stats

        init = tuple((jnp.full((1, t), NEG_INF, F32), jnp.zeros((1, t), F32)) for _ in range(2))
        scs, stats = lax.fori_loop(0, i, trip, (scores(i, True), init))
        stats = consume(jnp.where(i == 0, i, i - 1), scs, stats)
        for hh in range(2):
            m, l = stats[hh]
            o_ref[:, MLA_V * hh:MLA_V * (hh + 1)] = jnp.transpose(acc_ref[hh] / l)
            lse_ref[0, hh:hh + 1, :] = m + jnp.log2(l)

    return pl.pallas_call(
        body, name="mla_attn_fwd", grid=(MLA_HEADS // 2, nq),
        in_specs=[pl.BlockSpec((t, 256), lambda p, i: (i, p)),
                  pl.BlockSpec((s, 256), lambda p, i: (0, p)),
                  pl.BlockSpec((128, s), lambda p, i: (p, 0))],
        out_specs=[pl.BlockSpec((t, 128), lambda p, i: (i, p)),
                   pl.BlockSpec((1, 2, t), lambda p, i: (p, 0, i))],
        out_shape=[jax.ShapeDtypeStruct((s, 512), F32),
                   jax.ShapeDtypeStruct((MLA_HEADS // 2, 2, s), F32)],
        scratch_shapes=[pltpu.VMEM((2, MLA_V, t), F32)],
        compiler_params=_params(("parallel", "arbitrary")),
    )(q2, k, vt)


def _mla_attn_bwd(q2, q2t, k, kt, v, do, dot, lse_rows, delta_rows):
    s = q2.shape[0]
    t = min(512, s)
    nq = s // t

    def body(q_ref, qt_ref, k_ref, kt_ref, v_ref, do_ref, dot_ref, lse_ref, dl_ref,
             dq_ref, dk_ref, dv_ref):
        j = pl.program_id(1)

        @pl.when(j == 0)
        def _():
            dq_ref[...] = jnp.zeros_like(dq_ref)

        dk_ref[...] = jnp.zeros_like(dk_ref)
        dv_ref[...] = jnp.zeros_like(dv_ref)
        row = lax.broadcasted_iota(jnp.int32, (t, t), 0)
        col = lax.broadcasted_iota(jnp.int32, (t, t), 1)
        causal_t = row <= col
        kh = [k_ref[:, LANES * hh:LANES * (hh + 1)] for hh in range(2)]
        kth = [kt_ref[LANES * hh:LANES * (hh + 1), :] for hh in range(2)]
        vh = [v_ref[:, MLA_V * hh:MLA_V * (hh + 1)] for hh in range(2)]
        nt = (((1,), (1,)), ((), ()))

        def step(i, masked):
            r0 = pl.multiple_of(i * t, t)
            sd = []
            for hh in range(2):
                qh = q_ref[pl.ds(r0, t), LANES * hh:LANES * (hh + 1)]
                doh = do_ref[pl.ds(r0, t), MLA_V * hh:MLA_V * (hh + 1)]
                sc_t = lax.dot_general(kh[hh], qh, nt, preferred_element_type=F32)
                sd.append(jnp.where(causal_t, sc_t, NEG_INF) if masked else sc_t)
                sd.append(lax.dot_general(vh[hh], doh, nt, preferred_element_type=F32))
            pts, gts = [], []
            for hh in range(2):
                lse = lse_ref[0, hh:hh + 1, pl.ds(r0, t)]
                dl = dl_ref[0, hh:hh + 1, pl.ds(r0, t)]
                p_t = jnp.exp2(sd[2 * hh] - lse)
                pts.append(p_t.astype(BF16))
                gts.append((p_t * (sd[2 * hh + 1] - dl)).astype(BF16))
            for hh in range(2):
                qth = qt_ref[LANES * hh:LANES * (hh + 1), pl.ds(r0, t)]
                doth = dot_ref[MLA_V * hh:MLA_V * (hh + 1), pl.ds(r0, t)]
                dv_ref[MLA_V * hh:MLA_V * (hh + 1), :] += lax.dot_general(
                    doth, pts[hh], nt, preferred_element_type=F32)
                dk_ref[LANES * hh:LANES * (hh + 1), :] += lax.dot_general(
                    qth, gts[hh], nt, preferred_element_type=F32)
                dq_ref[LANES * hh:LANES * (hh + 1), pl.ds(r0, t)] += jnp.dot(
                    kth[hh], gts[hh], preferred_element_type=F32)

        step(j, True)

        def trip(i, carry):
            step(i, False)
            return carry

        lax.fori_loop(j + 1, nq, trip, 0)
        dk_ref[...] = dk_ref[...] * LN2

        @pl.when(j == nq - 1)
        def _():
            dq_ref[...] = dq_ref[...] * MLA_SCALE

    return pl.pallas_call(
        body, name="mla_attn_bwd", grid=(MLA_HEADS // 2, nq),
        in_specs=[pl.BlockSpec((s, 256), lambda p, j: (0, p)),
                  pl.BlockSpec((256, s), lambda p, j: (p, 0)),
                  pl.BlockSpec((t, 256), lambda p, j: (j, p)),
                  pl.BlockSpec((256, t), lambda p, j: (p, j)),
                  pl.BlockSpec((t, 128), lambda p, j: (j, p)),
                  pl.BlockSpec((s, 128), lambda p, j: (0, p)),
                  pl.BlockSpec((128, s), lambda p, j: (p, 0)),
                  pl.BlockSpec((1, 2, s), lambda p, j: (p, 0, 0)),
                  pl.BlockSpec((1, 2, s), lambda p, j: (p, 0, 0))],
        out_specs=[pl.BlockSpec((256, s), lambda p, j: (p, 0)),
                   pl.BlockSpec((256, t), lambda p, j: (p, j)),
                   pl.BlockSpec((128, t), lambda p, j: (p, j))],
        out_shape=[jax.ShapeDtypeStruct((1024, s), F32), jax.ShapeDtypeStruct((1024, s), F32),
                   jax.ShapeDtypeStruct((512, s), F32)],
        compiler_params=_params(("parallel", "arbitrary")),
    )(q2, q2t, k, kt, v, do, dot, lse_rows, delta_rows)


SWA_SCALE = SWA_DIM ** -0.5
SWA_COLS = SWA_GROUP * BLOCK
SWA_LOG2 = SWA_SCALE * LOG2E


def _swa_tables():
    k = np.arange(2 * BLOCK)[:, None]
    col = np.arange(SWA_COLS)[None, :]
    dist = BLOCK + (col % BLOCK) - k
    valid = (dist >= 0) & (dist < BLOCK)
    out = np.zeros((2, SWA_KV, 2 * BLOCK, SWA_COLS), np.float32)
    for first in range(2):
        ok = valid & ((k >= BLOCK) | (first == 0))
        for j in range(SWA_KV):
            slope = 2.0 ** -(SWA_GROUP * j + col // BLOCK + 1)
            out[first, j] = np.where(ok, -slope * dist * LOG2E, NEG_INF)
    return jnp.asarray(out)


def _swa_tile_inputs(sq_ref, skv_ref, halo_ref, qn_ref, kn_ref, sk_ref, add_ref, first):
    kv_all = jnp.concatenate([halo_ref[...], skv_ref[...]], axis=0)
    k_raw = [kv_all[:, SWA_DIM * j:SWA_DIM * (j + 1)] for j in range(SWA_KV)]
    v_all = [kv_all[:, 128 + SWA_DIM * j:128 + SWA_DIM * (j + 1)] for j in range(SWA_KV)]
    q_raw = [sq_ref[:, SWA_DIM * h:SWA_DIM * (h + 1)] for h in range(SWA_HEADS)]
    lane_grp = lax.broadcasted_iota(jnp.int32, (1, SWA_COLS), 1) // BLOCK
    sinks, adds = [], []
    for j in range(SWA_KV):
        row = jnp.zeros((1, SWA_COLS), F32)
        for g in range(SWA_GROUP):
            h = SWA_GROUP * j + g
            row = jnp.where(lane_grp == g, sk_ref[:, h:h + 1] * LOG2E, row)
        sinks.append(row)
        adds.append((jnp.where(first, add_ref[1, j], add_ref[0, j]), add_ref[0, j]))
    return k_raw, v_all, q_raw, sinks, adds


def _swa_probs(kb, qs, add, sink):
    s2 = lax.dot_general(kb, qs, (((1,), (1,)), ((), ())), preferred_element_type=F32) * SWA_LOG2 + add
    m = jnp.maximum(jnp.max(s2, axis=0, keepdims=True), sink)
    e = jnp.exp2(s2 - m)
    es = jnp.exp2(sink - m)
    inv = 1.0 / (jnp.sum(e, axis=0, keepdims=True) + es)
    return e, inv, es


def _swa_fwd(proj, qn, kn, sinks, tables):
    s = proj.shape[0]
    ts = min(512, s)
    nb = ts // BLOCK

    def body(sq_ref, skv_ref, halo_ref, qn_ref, kn_ref, sk_ref, add_ref, o_ref, vt_ref):
        first = pl.program_id(0) == 0
        k_raw, v_all, q_raw, sink_rows, adds = _swa_tile_inputs(
            sq_ref, skv_ref, halo_ref, qn_ref, kn_ref, sk_ref, add_ref, first)
        kn_ = [_rms(k, kn_ref[...]).astype(BF16) for k in k_raw]
        for j in range(SWA_KV):
            vt_ref[j] = jnp.transpose(v_all[j]).astype(BF16)
        qn_ = [_rms(q, qn_ref[...]).astype(BF16) for q in q_raw]
        for b in range(nb):
            for j in range(SWA_KV):
                qs = jnp.concatenate([qn_[SWA_GROUP * j + g][BLOCK * b:BLOCK * (b + 1)]
                                      for g in range(SWA_GROUP)], axis=0)
                e, inv, _ = _swa_probs(kn_[j][BLOCK * b:BLOCK * (b + 2)], qs,
                                       adds[j][0 if b == 0 else 1], sink_rows[j])
                o_t = jnp.dot(vt_ref[j, :, BLOCK * b:BLOCK * (b + 2)], (e * inv).astype(BF16),
                              preferred_element_type=F32)
                o = jnp.transpose(o_t)
                for g in range(SWA_GROUP):
                    h = SWA_GROUP * j + g
                    o_ref[BLOCK * b:BLOCK * (b + 1), SWA_DIM * h:SWA_DIM * (h + 1)] = (
                        o[BLOCK * g:BLOCK * (g + 1)])

    const = lambda shape: pl.BlockSpec(shape, lambda i: (0,) * len(shape))
    return pl.pallas_call(
        body, name="swa_fwd", grid=(s // ts,),
        in_specs=[pl.BlockSpec((ts, 512), lambda i: (i, OFF_SQ // 512)),
                  pl.BlockSpec((ts, 256), lambda i: (i, OFF_SKV // 256)),
                  pl.BlockSpec((BLOCK, 256), lambda i: (jnp.maximum(i * nb - 1, 0), OFF_SKV // 256)),
                  const((1, SWA_DIM)), const((1, SWA_DIM)), const((1, SWA_HEADS)),
                  const(tables.shape)],
        out_specs=pl.BlockSpec((ts, 512), lambda i: (i, 0)),
        out_shape=jax.ShapeDtypeStruct((s, 512), F32),
        scratch_shapes=[pltpu.VMEM((SWA_KV, SWA_DIM, ts + BLOCK), BF16)],
        compiler_params=_params(("parallel",)),
    )(proj, proj, proj, qn, kn, sinks, tables)


def _swa_bwd(proj, qn, kn, sinks, tables, do):
    s = proj.shape[0]
    ts = min(512, s)
    nb = ts // BLOCK
    nt = s // ts

    def body(sq_ref, skv_ref, halo_ref, qn_ref, kn_ref, sk_ref, add_ref, do_ref,
             dsq_ref, dskv_ref, dqn_ref, dkn_ref, dsk_ref, carry_ref, knt_ref):
        step = pl.program_id(0)
        first = step == nt - 1

        @pl.when(step == 0)
        def _():
            carry_ref[...] = jnp.zeros_like(carry_ref)
            dqn_ref[...] = jnp.zeros_like(dqn_ref)
            dkn_ref[...] = jnp.zeros_like(dkn_ref)
            dsk_ref[...] = jnp.zeros_like(dsk_ref)

        k_raw, v_all, q_raw, sink_rows, adds = _swa_tile_inputs(
            sq_ref, skv_ref, halo_ref, qn_ref, kn_ref, sk_ref, add_ref, first)
        kn_f = [_rms(k, kn_ref[...]) for k in k_raw]
        kn_ = [k.astype(BF16) for k in kn_f]
        for j in range(SWA_KV):
            knt_ref[j] = jnp.transpose(kn_f[j]).astype(BF16)
        vb = [v.astype(BF16) for v in v_all]
        qn_ = [_rms(q, qn_ref[...]).astype(BF16) for q in q_raw]

        zero = jnp.zeros((BLOCK, SWA_DIM), F32)
        dk_chunks = [[zero] * (nb + 1) for _ in range(SWA_KV)]
        dv_chunks = [[zero] * (nb + 1) for _ in range(SWA_KV)]
        dq_blocks = [[None] * nb for _ in range(SWA_HEADS)]
        dsink = [jnp.zeros((1, SWA_COLS), F32) for _ in range(SWA_KV)]
        for b in range(nb):
            rows = slice(BLOCK * b, BLOCK * (b + 1))
            band = slice(BLOCK * b, BLOCK * (b + 2))
            for j in range(SWA_KV):
                heads = [SWA_GROUP * j + g for g in range(SWA_GROUP)]
                qs = jnp.concatenate([qn_[h][rows] for h in heads], axis=0)
                dos = jnp.concatenate([do_ref[rows, SWA_DIM * h:SWA_DIM * (h + 1)] for h in heads],
                                      axis=0).astype(BF16)
                e, inv, es = _swa_probs(kn_[j][band], qs, adds[j][0 if b == 0 else 1], sink_rows[j])
                p = e * inv
                dp = lax.dot_general(vb[j][band], dos, (((1,), (1,)), ((), ())),
                                     preferred_element_type=F32)
                dsum = jnp.sum(p * dp, axis=0, keepdims=True)
                dsink[j] = dsink[j] - es * inv * dsum
                g_t = (p * (dp - dsum) * SWA_SCALE).astype(BF16)
                dv_b = jnp.dot(p.astype(BF16), dos, preferred_element_type=F32)
                dk_b = jnp.dot(g_t, qs, preferred_element_type=F32)
                dq_s = jnp.transpose(jnp.dot(knt_ref[j, :, band], g_t,
                                             preferred_element_type=F32))
                for half in range(2):
                    part = slice(BLOCK * half, BLOCK * (half + 1))
                    dk_chunks[j][b + half] = dk_chunks[j][b + half] + dk_b[part]
                    dv_chunks[j][b + half] = dv_chunks[j][b + half] + dv_b[part]
                for g, h in enumerate(heads):
                    dq_blocks[h][b] = dq_s[BLOCK * g:BLOCK * (g + 1)]

        dqn = jnp.zeros((1, SWA_DIM), F32)
        for h in range(SWA_HEADS):
            _, vjp = jax.vjp(_rms, q_raw[h], qn_ref[...])
            dq, dg = vjp(jnp.concatenate(dq_blocks[h], axis=0))
            dsq_ref[:, SWA_DIM * h:SWA_DIM * (h + 1)] = dq.astype(BF16)
            dqn = dqn + dg
        dqn_ref[...] += dqn
        dkn = jnp.zeros((1, SWA_DIM), F32)
        lane_grp = lax.broadcasted_iota(jnp.int32, (1, SWA_COLS), 1) // BLOCK
        for j in range(SWA_KV):
            _, vjp = jax.vjp(_rms, k_raw[j], kn_ref[...])
            dk, dg = vjp(jnp.concatenate(dk_chunks[j], axis=0))
            dkn = dkn + dg
            dv = jnp.concatenate(dv_chunks[j], axis=0)
            for part, val in ((0, dk), (1, dv)):
                c0 = 128 * part + SWA_DIM * j
                dskv_ref[0:ts - BLOCK, c0:c0 + SWA_DIM] = val[BLOCK:ts].astype(BF16)
                dskv_ref[ts - BLOCK:ts, c0:c0 + SWA_DIM] = (
                    val[ts:ts + BLOCK] + carry_ref[:, c0:c0 + SWA_DIM]).astype(BF16)
                carry_ref[:, c0:c0 + SWA_DIM] = val[0:BLOCK]
            for g in range(SWA_GROUP):
                h = SWA_GROUP * j + g
                dsk_ref[:, h:h + 1] += jnp.sum(jnp.where(lane_grp == g, dsink[j], 0.0), axis=1,
                                               keepdims=True)
        dkn_ref[...] += dkn

    const = lambda shape: pl.BlockSpec(shape, lambda st: (0,) * len(shape))
    return pl.pallas_call(
        body, name="swa_bwd", grid=(nt,),
        in_specs=[pl.BlockSpec((ts, 512), lambda st: (nt - 1 - st, OFF_SQ // 512)),
                  pl.BlockSpec((ts, 256), lambda st: (nt - 1 - st, OFF_SKV // 256)),
                  pl.BlockSpec((BLOCK, 256),
                               lambda st: (jnp.maximum((nt - 1 - st) * nb - 1, 0), OFF_SKV // 256)),
                  const((1, SWA_DIM)), const((1, SWA_DIM)), const((1, SWA_HEADS)),
                  const(tables.shape),
                  pl.BlockSpec((ts, 512), lambda st: (nt - 1 - st, 0))],
        out_specs=[pl.BlockSpec((ts, 512), lambda st: (nt - 1 - st, 0)),
                   pl.BlockSpec((ts, 256), lambda st: (nt - 1 - st, 0)),
                   const((1, SWA_DIM)), const((1, SWA_DIM)), const((1, SWA_HEADS))],
        out_shape=[jax.ShapeDtypeStruct((s, 512), BF16), jax.ShapeDtypeStruct((s, 256), BF16),
                   jax.ShapeDtypeStruct((1, SWA_DIM), F32), jax.ShapeDtypeStruct((1, SWA_DIM), F32),
                   jax.ShapeDtypeStruct((1, SWA_HEADS), F32)],
        scratch_shapes=[pltpu.VMEM((BLOCK, 256), F32),
                        pltpu.VMEM((SWA_KV, SWA_DIM, ts + BLOCK), BF16)],
        compiler_params=_params(("arbitrary",)),
    )(proj, proj, proj, qn, kn, sinks, tables, do)


HALO = 8


def _shift_down(u, halo, k):
    tm = u.shape[0]
    rid = lax.broadcasted_iota(jnp.int32, u.shape, 0)
    out = pltpu.roll(u, k, 0)
    for r in range(k):
        out = jnp.where(rid == r, halo[HALO - k + r:HALO - k + r + 1, :], out)
    return out


def _shift_up(u, halo, k):
    tm = u.shape[0]
    rid = lax.broadcasted_iota(jnp.int32, u.shape, 0)
    out = pltpu.roll(u, tm - k, 0)
    for r in range(k):
        out = jnp.where(rid == tm - k + r, halo[r:r + 1, :], out)
    return out


def _conv_fwd_vals(conv_ref, convp_ref, cw_ref, is_first):
    c_h, c_b, c_c = conv_ref[:, 0:512], conv_ref[:, 512:1024], conv_ref[:, 1024:1536]
    u = c_c * c_h
    up = jnp.where(is_first, 0.0, convp_ref[:, 1024:1536] * convp_ref[:, 0:512])
    u1 = _shift_down(u, up, 1)
    u2 = _shift_down(u, up, 2)
    yc = cw_ref[0:1, :] * u2 + cw_ref[1:2, :] * u1 + cw_ref[2:3, :] * u
    return c_h, c_b, c_c, u, u1, u2, yc


def _out_fwd(proj, o_mla, o_swa, x, w_out, cw):
    s = proj.shape[0]
    tm = min(512, s)

    def body(conv_ref, convp_ref, gates_ref, om_ref, os_ref, x_ref, w_ref, cw_ref, y_ref, z_ref):
        i = pl.program_id(0)
        _, c_b, _, _, _, _, yc = _conv_fwd_vals(conv_ref, convp_ref, cw_ref, i == 0)
        mix = (om_ref[...], c_b * yc, os_ref[...])
        for n in range(3):
            g = gates_ref[:, GROUP * n:GROUP * (n + 1)]
            z_ref[:, GROUP * n:GROUP * (n + 1)] = (mix[n] * (g * _sigmoid(g))).astype(BF16)
        y_ref[...] = x_ref[...] + jnp.dot(z_ref[...], w_ref[...], preferred_element_type=F32)

    row = lambda width: pl.BlockSpec((tm, width), lambda i: (i, 0))
    return pl.pallas_call(
        body, name="out_fwd", grid=(s // tm,),
        in_specs=[pl.BlockSpec((tm, 1536), lambda i: (i, 0)),
                  pl.BlockSpec((HALO, 1536), lambda i: (jnp.maximum(i * (tm // HALO) - 1, 0), 0)),
                  pl.BlockSpec((tm, 1536), lambda i: (i, 1)),
                  row(512), row(512), row(D_MODEL),
                  pl.BlockSpec((D_MIX, D_MODEL), lambda i: (0, 0)),
                  pl.BlockSpec((HALO, 512), lambda i: (0, 0))],
        out_specs=[row(D_MODEL), row(D_MIX)],
        out_shape=[jax.ShapeDtypeStruct((s, D_MODEL), F32), jax.ShapeDtypeStruct((s, D_MIX), BF16)],
        compiler_params=_params(("parallel",)),
    )(proj, proj, proj, o_mla, o_swa, x, w_out, cw)


def _out_bwd(dy, proj, o_mla, o_swa, w_out, cw):
    s = proj.shape[0]
    tm = min(512, s)
    nt = s // tm
    hb = tm // HALO

    def body(dy_ref, dyn_ref, conv_ref, convp_ref, convn_ref, gates_ref, gatesn_ref, om_ref, os_ref,
             w_ref, cw_ref,
             dconv_ref, dgates_ref, dom_ref, domt_ref, delta_ref, dos_ref, dcw_ref):
        i = pl.program_id(0)
        dz = _mm_nt(dy_ref[...], w_ref[...])

        def gate(n):
            g = gates_ref[:, GROUP * n:GROUP * (n + 1)]
            sg = _sigmoid(g)
            return g * sg, sg * (1.0 + g * (1.0 - sg))

        for n, o_ref, do_ref in ((0, om_ref, dom_ref), (2, os_ref, dos_ref)):
            silu, dsilu = gate(n)
            dzn = dz[:, GROUP * n:GROUP * (n + 1)]
            o = o_ref[...]
            do = dzn * silu
            do_ref[...] = do.astype(do_ref.dtype)
            dgates_ref[:, GROUP * n:GROUP * (n + 1)] = (dzn * o * dsilu).astype(BF16)
            if n == 0:
                domt_ref[...] = jnp.transpose(do).astype(BF16)
                t = do * o
                for h in range(MLA_HEADS):
                    delta_ref[:, h:h + 1] = jnp.sum(t[:, MLA_V * h:MLA_V * (h + 1)], axis=-1,
                                                    keepdims=True)

        c_h, c_b, c_c, u, u1, u2, yc = _conv_fwd_vals(conv_ref, convp_ref, cw_ref, i == 0)
        silu, dsilu = gate(1)
        dzc = dz[:, GROUP:2 * GROUP]
        dgates_ref[:, GROUP:2 * GROUP] = (dzc * (c_b * yc) * dsilu).astype(BF16)
        dycr = dzc * silu
        dyc = dycr * c_b
        gn = gatesn_ref[:, GROUP:2 * GROUP]
        dzc_n = _mm_nt(dyn_ref[...], w_ref[GROUP:2 * GROUP, :])
        dyc_n = jnp.where(i == nt - 1, 0.0, dzc_n * (gn * _sigmoid(gn)) * convn_ref[:, 512:1024])
        d1 = _shift_up(dyc, dyc_n, 1)
        d2 = _shift_up(dyc, dyc_n, 2)
        du = cw_ref[2:3, :] * dyc + cw_ref[1:2, :] * d1 + cw_ref[0:1, :] * d2
        dconv_ref[:, 0:512] = (du * c_c).astype(BF16)
        dconv_ref[:, 512:1024] = (dycr * yc).astype(BF16)
        dconv_ref[:, 1024:1536] = (du * c_h).astype(BF16)

        @pl.when(i == 0)
        def _():
            dcw_ref[...] = jnp.zeros_like(dcw_ref)

        for k, uk in enumerate((u2, u1, u)):
            dcw_ref[k:k + 1, :] += jnp.sum(dyc * uk, axis=0, keepdims=True)

    row = lambda width: pl.BlockSpec((tm, width), lambda i: (i, 0))
    prev = lambda i: jnp.maximum(i * hb - 1, 0)
    nxt = lambda i: jnp.minimum((i + 1) * hb, s // HALO - 1)
    return pl.pallas_call(
        body, name="out_bwd", grid=(nt,),
        in_specs=[row(D_MODEL),
                  pl.BlockSpec((HALO, D_MODEL), lambda i: (nxt(i), 0)),
                  pl.BlockSpec((tm, 1536), lambda i: (i, 0)),
                  pl.BlockSpec((HALO, 1536), lambda i: (prev(i), 0)),
                  pl.BlockSpec((HALO, 1536), lambda i: (nxt(i), 0)),
                  pl.BlockSpec((tm, 1536), lambda i: (i, 1)),
                  pl.BlockSpec((HALO, 1536), lambda i: (nxt(i), 1)),
                  row(512), row(512),
                  pl.BlockSpec((D_MIX, D_MODEL), lambda i: (0, 0)),
                  pl.BlockSpec((HALO, 512), lambda i: (0, 0))],
        out_specs=[row(1536), row(1536), row(512), pl.BlockSpec((512, tm), lambda i: (0, i)),
                   row(MLA_HEADS), row(512), pl.BlockSpec((HALO, 512), lambda i: (0, 0))],
        out_shape=[jax.ShapeDtypeStruct((s, 1536), BF16), jax.ShapeDtypeStruct((s, 1536), BF16),
                   jax.ShapeDtypeStruct((s, 512), BF16), jax.ShapeDtypeStruct((512, s), BF16),
                   jax.ShapeDtypeStruct((s, MLA_HEADS), F32),
                   jax.ShapeDtypeStruct((s, 512), BF16), jax.ShapeDtypeStruct((HALO, 512), F32)],
        compiler_params=_params(("arbitrary",)),
    )(dy, dy, proj, proj, proj, proj, proj, o_mla, o_swa, w_out, cw)


def _loss_head(y, target):
    s, d = y.shape
    tm = min(512, s)
    nt = s // tm

    def body(y_ref, t_ref, dy_ref, loss_ref):
        i = pl.program_id(0)
        err = y_ref[...] - t_ref[...]
        dy_ref[...] = err * (1.0 / d)

        @pl.when(i == 0)
        def _():
            loss_ref[...] = jnp.zeros_like(loss_ref)

        sq = jnp.sum((err * err).reshape(tm // 8, 8, d), axis=0)
        part = sq[:, 0:LANES]
        for c in range(1, d // LANES):
            part = part + sq[:, LANES * c:LANES * (c + 1)]
        loss_ref[...] += part

        @pl.when(i == nt - 1)
        def _():
            loss_ref[...] = jnp.full(loss_ref.shape, (0.5 / d) * jnp.sum(loss_ref[...]), F32)

    return pl.pallas_call(
        body, name="loss_head", grid=(nt,),
        in_specs=[pl.BlockSpec((tm, d), lambda i: (i, 0)), pl.BlockSpec((tm, d), lambda i: (i, 0))],
        out_specs=[pl.BlockSpec((tm, d), lambda i: (i, 0)), pl.BlockSpec((8, LANES), lambda i: (0, 0))],
        out_shape=[jax.ShapeDtypeStruct((s, d), F32), jax.ShapeDtypeStruct((8, LANES), F32)],
        compiler_params=_params(("arbitrary",)),
    )(y, target)


def _adamw(g, w, m, v):
    rows = g.shape[0]
    tr = min(256, rows)
    c1 = 1.0 - ADAM_B1
    c2 = 1.0 - ADAM_B2
    bc1 = 1.0 - ADAM_B1 ** ADAM_STEP
    bc2 = 1.0 - ADAM_B2 ** ADAM_STEP

    def body(g_ref, w_ref, m_ref, v_ref, d_ref, mo_ref, vo_ref):
        gg = g_ref[...]
        m_new = ADAM_B1 * m_ref[...] + c1 * gg
        v_new = ADAM_B2 * v_ref[...] + c2 * (gg * gg)
        m_hat = m_new / bc1
        v_hat = v_new / bc2
        d_ref[...] = -ADAM_LR * (m_hat / (jnp.sqrt(v_hat) + ADAM_EPS) + ADAM_WD * w_ref[...])
        mo_ref[...] = m_new
        vo_ref[...] = v_new

    spec = pl.BlockSpec((tr, g.shape[1]), lambda i: (i, 0))
    return pl.pallas_call(
        body, name="adamw", grid=(rows // tr,),
        in_specs=[spec] * 4, out_specs=[spec] * 3,
        out_shape=[jax.ShapeDtypeStruct(g.shape, F32)] * 3,
        compiler_params=_params(("parallel",)),
    )(g, w, m, v)


HBM_SPEC = pl.BlockSpec(memory_space=pltpu.HBM)


def _place():
    x, y, c = lax.axis_index("x"), lax.axis_index("y"), lax.axis_index("c")
    chips = [(1 - x, y), (x, 1 - y), (1 - x, 1 - y)]
    return x, y, c, chips


def _all_gather(shards):
    na = len(shards)
    halves = [sh.shape[0] // 2 for sh in shards]

    def body(*refs):
        w_refs, a_refs = refs[:na], refs[na:2 * na]
        send_sems, recv_sems = refs[2 * na:]
        x, y, c, chips = _place()
        k = 2 * x + y
        sib = (x, y, 1 - c)

        def slab(a, kk, hc):
            return a_refs[a].at[kk, pl.ds(hc * halves[a], halves[a]), :]

        def copy(a, n, src, dst, to):
            return pltpu.make_async_remote_copy(
                src_ref=src, dst_ref=dst, send_sem=send_sems.at[6 * a + n],
                recv_sem=recv_sems.at[6 * a + n], device_id=to, device_id_type=MESH)

        first = [copy(a, n, w_refs[a].at[pl.ds(c * halves[a], halves[a]), :], slab(a, k, c),
                      (cx, cy, c))
                 for n, (cx, cy) in enumerate(chips) for a in range(na)]
        for cp in first:
            cp.start()
        passed = []
        for n, (cx, cy) in enumerate(chips):
            kk = 2 * cx + cy
            for a in range(na):
                copy(a, n, slab(a, kk, c), slab(a, kk, c), (cx, cy, c)).wait_recv()
                fwd = copy(a, 3 + n, slab(a, kk, c), slab(a, kk, c), sib)
                fwd.start()
                passed.append(fwd)
        for n, (cx, cy) in enumerate(chips):
            kk = 2 * cx + cy
            for a in range(na):
                copy(a, 3 + n, slab(a, kk, 1 - c), slab(a, kk, 1 - c), sib).wait_recv()
        for cp in first + passed:
            cp.wait_send()

    return pl.pallas_call(
        body, name="weights_all_gather",
        in_specs=[HBM_SPEC] * na, out_specs=[HBM_SPEC] * na,
        out_shape=[jax.ShapeDtypeStruct((N_CHIPS,) + sh.shape, sh.dtype) for sh in shards],
        scratch_shapes=[pltpu.SemaphoreType.DMA((6 * na,)), pltpu.SemaphoreType.DMA((6 * na,))],
    )(*shards)


def _fill_own_slab(buf, src, k_idx):
    n, rows, cols = buf.shape
    tr = _row_tile(rows)
    slabs = src.ndim == 3

    def body(k_ref, src_ref, buf_ref, out_ref):
        out_ref[0] = src_ref[0] if slabs else src_ref[...]

    if slabs:
        src_spec = pl.BlockSpec((1, tr, cols), lambda t, k_ref: (k_ref[0], t, 0))
    else:
        src_spec = pl.BlockSpec((tr, cols), lambda t, k_ref: (t, 0))
    return pl.pallas_call(
        body, name="fill_own_slab",
        grid_spec=pltpu.PrefetchScalarGridSpec(
            num_scalar_prefetch=1, grid=(rows // tr,),
            in_specs=[src_spec, pl.BlockSpec(memory_space=pl.ANY)],
            out_specs=pl.BlockSpec((1, tr, cols), lambda t, k_ref: (k_ref[0], t, 0))),
        out_shape=jax.ShapeDtypeStruct(buf.shape, buf.dtype),
        input_output_aliases={2: 0},
        compiler_params=_params(("parallel",)),
    )(k_idx, src, buf)


def _swap_halves_to_sibling(gs):
    na = len(gs)

    def body(*refs):
        g_refs, r_refs = refs[:na], refs[na:2 * na]
        send_sems, recv_sems = refs[2 * na:]
        x, y, c, _ = _place()
        cps = []
        for a in range(na):
            half = g_refs[a].shape[1] // 2
            cps.append(pltpu.make_async_remote_copy(
                src_ref=g_refs[a].at[:, pl.ds((1 - c) * half, half), :], dst_ref=r_refs[a],
                send_sem=send_sems.at[a], recv_sem=recv_sems.at[a], device_id=(x, y, 1 - c),
                device_id_type=MESH))
        for cp in cps:
            cp.start()
        for cp in cps:
            cp.wait()

    return pl.pallas_call(
        body, name="grads_to_sibling",
        in_specs=[HBM_SPEC] * na, out_specs=[HBM_SPEC] * na,
        out_shape=[jax.ShapeDtypeStruct((g.shape[0], g.shape[1] // 2, g.shape[2]), g.dtype)
                   for g in gs],
        scratch_shapes=[pltpu.SemaphoreType.DMA((na,)), pltpu.SemaphoreType.DMA((na,))],
    )(*gs)


def _row_tile(rows):
    return 256 if rows % 256 == 0 else 128


def _add_sibling(g, r, c_idx, out_dtype):
    n, rows, cols = g.shape
    half = rows // 2
    tr = _row_tile(half)
    nb = half // tr

    def body(c_ref, g_ref, r_ref, p_ref):
        p_ref[...] = (g_ref[...] + r_ref[...]).astype(out_dtype)

    return pl.pallas_call(
        body, name="grads_add_sibling",
        grid_spec=pltpu.PrefetchScalarGridSpec(
            num_scalar_prefetch=1, grid=(n, nb),
            in_specs=[pl.BlockSpec((1, tr, cols), lambda j, t, c_ref: (j, c_ref[0] * nb + t, 0)),
                      pl.BlockSpec((1, tr, cols), lambda j, t, c_ref: (j, t, 0))],
            out_specs=pl.BlockSpec((1, tr, cols), lambda j, t, c_ref: (j, t, 0))),
        out_shape=jax.ShapeDtypeStruct((n, half, cols), out_dtype),
        compiler_params=_params(("parallel", "parallel")),
    )(c_idx, g, r)


def _scatter_to_chips(ps):
    na = len(ps)

    def body(*refs):
        p_refs, q_refs = refs[:na], refs[na:2 * na]
        send_sems, recv_sems = refs[2 * na:]
        x, y, c, chips = _place()
        k = 2 * x + y
        sends = []
        for i, (cx, cy) in enumerate(chips):
            for a in range(na):
                cp = pltpu.make_async_remote_copy(
                    src_ref=p_refs[a].at[2 * cx + cy], dst_ref=q_refs[a].at[k],
                    send_sem=send_sems.at[3 * a + i], recv_sem=recv_sems.at[3 * a + i],
                    device_id=(cx, cy, c), device_id_type=MESH)
                cp.start()
                sends.append(cp)
        for i, (cx, cy) in enumerate(chips):
            kk = 2 * cx + cy
            for a in range(na):
                pltpu.make_async_remote_copy(
                    src_ref=p_refs[a].at[kk], dst_ref=q_refs[a].at[kk],
                    send_sem=send_sems.at[3 * a + i], recv_sem=recv_sems.at[3 * a + i],
                    device_id=(cx, cy, c), device_id_type=MESH).wait_recv()
        for cp in sends:
            cp.wait_send()

    return pl.pallas_call(
        body, name="grads_scatter_to_chips",
        in_specs=[HBM_SPEC] * na, out_specs=[HBM_SPEC] * na,
        out_shape=[jax.ShapeDtypeStruct(p.shape, p.dtype) for p in ps],
        scratch_shapes=[pltpu.SemaphoreType.DMA((3 * na,)), pltpu.SemaphoreType.DMA((3 * na,))],
    )(*ps)


def _sum_chips(q, c_idx):
    n, half, cols = q.shape
    tr = _row_tile(half)
    nb = half // tr

    def body(c_ref, q_ref, o_ref):
        parts = [q_ref[kk].astype(F32) for kk in range(n)]
        o_ref[...] = ((parts[0] + parts[1]) + parts[2]) + parts[3]

    return pl.pallas_call(
        body, name="grads_sum_chips",
        grid_spec=pltpu.PrefetchScalarGridSpec(
            num_scalar_prefetch=1, grid=(nb,),
            in_specs=[pl.BlockSpec((n, tr, cols), lambda t, c_ref: (0, t, 0))],
            out_specs=pl.BlockSpec((tr, cols), lambda t, c_ref: (c_ref[0] * nb + t, 0))),
        out_shape=jax.ShapeDtypeStruct((2 * half, cols), F32),
        compiler_params=_params(("parallel",)),
    )(c_idx, q)


def _join_halves(fulls):
    na = len(fulls)

    def body(*refs):
        o_refs = refs[na:2 * na]
        send_sems, recv_sems = refs[2 * na:]
        x, y, c, _ = _place()
        sends = []
        for a in range(na):
            half = o_refs[a].shape[0] // 2
            rows = o_refs[a].at[pl.ds(c * half, half), :]
            sends.append(pltpu.make_async_remote_copy(
                src_ref=rows, dst_ref=rows, send_sem=send_sems.at[a], recv_sem=recv_sems.at[a],
                device_id=(x, y, 1 - c), device_id_type=MESH))
        for cp in sends:
            cp.start()
        for a in range(na):
            half = o_refs[a].shape[0] // 2
            other = o_refs[a].at[pl.ds((1 - c) * half, half), :]
            pltpu.make_async_remote_copy(
                src_ref=other, dst_ref=other, send_sem=send_sems.at[a], recv_sem=recv_sems.at[a],
                device_id=(x, y, 1 - c), device_id_type=MESH).wait_recv()
        for cp in sends:
            cp.wait_send()

    return pl.pallas_call(
        body, name="grads_join_halves",
        in_specs=[HBM_SPEC] * na, out_specs=[HBM_SPEC] * na,
        out_shape=[jax.ShapeDtypeStruct(f.shape, f.dtype) for f in fulls],
        input_output_aliases={a: a for a in range(na)},
        scratch_shapes=[pltpu.SemaphoreType.DMA((na,)), pltpu.SemaphoreType.DMA((na,))],
    )(*fulls)


def _part_rows(shape):
    size = 1
    for d in shape:
        size *= d
    rows = -(-size // PACK_COLS)
    return size, -(-rows // PACK_ROW_ALIGN) * PACK_ROW_ALIGN


def _pack_rows(arrays, dtype, total_rows):
    parts, used = [], 0
    for a in arrays:
        size, rows = _part_rows(a.shape)
        flat = a.reshape(-1).astype(dtype)
        parts.append(jnp.pad(flat, (0, rows * PACK_COLS - size)).reshape(rows, PACK_COLS))
        used += rows
    parts.append(jnp.zeros((total_rows - used, PACK_COLS), dtype))
    return jnp.concatenate(parts, axis=0)


def _unpack_rows(buf, shapes):
    lead = buf.shape[:-2]
    out, off = [], 0
    for sh in shapes:
        size, rows = _part_rows(sh)
        part = buf[..., off:off + rows, :].reshape(lead + (-1,))[..., :size]
        out.append(part.reshape(lead + tuple(sh)))
        off += rows
    return out


NEW_ORDER = ((928, 1440), (1440, 1952), (1952, 2464), (416, 928), (2464, 2976), (3744, 4256),
             (2976, 3488), (0, 256), (256, 384), (4256, 4320), (384, 416), (4256, 4288),
             (3488, 3616), (3616, 3744))
OLD_ORDER = ((3584, 3840), (3840, 3968), (4032, 4064), (1536, 2048), (0, 512), (512, 1024),
             (1024, 1536), (2048, 2560), (3072, 3584), (4096, 4224), (4224, 4352), (2560, 3072))


def _cols(sources, ranges):
    parts = []
    for a, b in ranges:
        off = 0
        for src in sources:
            width = src.shape[-1]
            lo, hi = max(a, off), min(b, off + width)
            if lo < hi:
                parts.append(src[..., lo - off:hi - off])
            off += width
    return jnp.concatenate(parts, axis=-1)


def _sub_ranges(ranges, a, b):
    out, off = [], 0
    for lo, hi in ranges:
        width = hi - lo
        s0, s1 = max(a, off), min(b, off + width)
        if s0 < s1:
            out.append((lo + s0 - off, lo + s1 - off))
        off += width
    return out


def _rope_tables(s):
    half = MLA_ROPE // 2
    inv_freq = jnp.power(jnp.float32(ROPE_THETA), -jnp.arange(half, dtype=F32) / half)
    ang = jnp.arange(s, dtype=F32)[:, None] * inv_freq[None, :]
    cos, sin = jnp.cos(ang), jnp.sin(ang)
    z = lambda n: jnp.zeros((s, n), F32)
    c = jnp.concatenate([jnp.ones((s, MLA_NOPE), F32), cos, cos, z(32)], axis=1)
    s1 = jnp.concatenate([z(MLA_NOPE), -sin, z(16), z(32)], axis=1)
    s2 = jnp.concatenate([z(MLA_NOPE), z(16), sin, z(32)], axis=1)
    return c, s1, s2


def _pad_lanes(a, n):
    return jnp.pad(a, ((0, 0), (0, n - a.shape[1])))


SHARDED = ("w_in", "w_out", "mla_w_qb", "mla_w_kvb", "conv_w")
REPLICATED = ("norm_g", "mla_q_a_norm", "mla_kv_a_norm", "mla_q_norm", "mla_k_norm",
              "swa_q_norm", "swa_k_norm", "swa_sinks")
WEIGHT_ORDER = ("norm_g", "w_in", "mla_q_a_norm", "mla_w_qb", "mla_kv_a_norm", "mla_w_kvb",
                "mla_q_norm", "mla_k_norm", "conv_w", "swa_q_norm", "swa_k_norm", "swa_sinks", "w_out")
SHARD_AXIS = {"w_in": 2, "w_out": 1, "mla_w_qb": 2, "mla_w_kvb": 2, "conv_w": 2}


def kernel(x, norm_g, w_in, mla_q_a_norm, mla_w_qb, mla_kv_a_norm, mla_w_kvb, mla_q_norm, mla_k_norm, conv_w, swa_q_norm, swa_k_norm, swa_sinks, w_out, loss_target, m_norm_g, m_w_in, m_mla_q_a_norm, m_mla_w_qb, m_mla_kv_a_norm, m_mla_w_kvb, m_mla_q_norm, m_mla_k_norm, m_conv_w, m_swa_q_norm, m_swa_k_norm, m_swa_sinks, m_w_out, v_norm_g, v_w_in, v_mla_q_a_norm, v_mla_w_qb, v_mla_kv_a_norm, v_mla_w_kvb, v_mla_q_norm, v_mla_k_norm, v_conv_w, v_swa_q_norm, v_swa_k_norm, v_swa_sinks, v_w_out):
    weights = dict(norm_g=norm_g, w_in=w_in, mla_q_a_norm=mla_q_a_norm, mla_w_qb=mla_w_qb,
                   mla_kv_a_norm=mla_kv_a_norm, mla_w_kvb=mla_w_kvb, mla_q_norm=mla_q_norm,
                   mla_k_norm=mla_k_norm, conv_w=conv_w, swa_q_norm=swa_q_norm,
                   swa_k_norm=swa_k_norm, swa_sinks=swa_sinks, w_out=w_out)
    mom_m = dict(norm_g=m_norm_g, w_in=m_w_in, mla_q_a_norm=m_mla_q_a_norm, mla_w_qb=m_mla_w_qb,
                 mla_kv_a_norm=m_mla_kv_a_norm, mla_w_kvb=m_mla_w_kvb, mla_q_norm=m_mla_q_norm,
                 mla_k_norm=m_mla_k_norm, conv_w=m_conv_w, swa_q_norm=m_swa_q_norm,
                 swa_k_norm=m_swa_k_norm, swa_sinks=m_swa_sinks, w_out=m_w_out)
    mom_v = dict(norm_g=v_norm_g, w_in=v_w_in, mla_q_a_norm=v_mla_q_a_norm, mla_w_qb=v_mla_w_qb,
                 mla_kv_a_norm=v_mla_kv_a_norm, mla_w_kvb=v_mla_w_kvb, mla_q_norm=v_mla_q_norm,
                 mla_k_norm=v_mla_k_norm, conv_w=v_conv_w, swa_q_norm=v_swa_q_norm,
                 swa_k_norm=v_swa_k_norm, swa_sinks=v_swa_sinks, w_out=v_w_out)
    xs = x[0]
    target = loss_target[0]
    s = xs.shape[0]
    c_idx = lax.axis_index("c").astype(jnp.int32).reshape(1)
    k_idx = (2 * lax.axis_index("x") + lax.axis_index("y")).astype(jnp.int32).reshape(1)

    conv_bits = lax.bitcast_convert_type(conv_w, BF16)
    small_list = [w_out, mla_w_qb, mla_w_kvb, conv_bits]
    w_in_rows = DEPTH * D_MODEL
    own = [w_in.astype(BF16).reshape(w_in_rows, w_in.shape[2]), _pack_rows(small_list, BF16, PACK_ROWS)]
    gathered_in, gathered_rest = [_fill_own_slab(buf, src, k_idx)
                                  for buf, src in zip(_all_gather(own), own)]
    parts = _unpack_rows(gathered_rest, [a.shape for a in small_list])
    join = lambda p, axis: jnp.concatenate([p[k] for k in range(N_CHIPS)], axis=axis)
    w_in_slabs = gathered_in.reshape(N_CHIPS, DEPTH, D_MODEL, w_in.shape[2])
    w_in_zeros = jnp.zeros((D_MODEL, 64), BF16)
    w_out_full = join(parts[0], 1)
    w_qb_full = join(parts[1], 2)
    w_kvb_full = join(parts[2], 2)
    conv_full = lax.bitcast_convert_type(join(parts[3], 2), F32)

    rope = _rope_tables(s)
    swa_tables = _swa_tables()
    layers = []
    for l in range(DEPTH):
        wq = jnp.pad(w_qb_full[l].reshape(MLA_Q_LORA, MLA_HEADS, MLA_QK),
                     ((0, 0), (0, 0), (0, LANES - MLA_QK))).reshape(MLA_Q_LORA, MLA_HEADS * LANES)
        kv = w_kvb_full[l].reshape(MLA_KV_LORA, MLA_HEADS, MLA_NOPE + MLA_V)
        wk = jnp.pad(kv[:, :, :MLA_NOPE], ((0, 0), (0, 0), (0, LANES - MLA_NOPE)))
        wkv = jnp.concatenate([wk.reshape(MLA_KV_LORA, MLA_HEADS * LANES),
                               kv[:, :, MLA_NOPE:].reshape(MLA_KV_LORA, MLA_HEADS * MLA_V)], axis=1)
        layers.append(dict(
            w_in=_cols([w_in_slabs[k, l] for k in range(N_CHIPS)] + [w_in_zeros], NEW_ORDER),
            w_out=w_out_full[l], wq=wq, wkv=wkv,
            cw=jnp.pad(conv_full[l], ((0, HALO - 3), (0, 0))),
            g=norm_g[l][None], qan=mla_q_a_norm[l][None], kvan=mla_kv_a_norm[l][None],
            qn=_pad_lanes(mla_q_norm[l][None], LANES), kn=_pad_lanes(mla_k_norm[l][None], LANES),
            sqn=swa_q_norm[l][None], skn=swa_k_norm[l][None], sinks=swa_sinks[l][None]))

    saved = []
    h_in = xs
    for l in range(DEPTH):
        p = layers[l]
        proj, hb = _in_proj_fwd(h_in, p["g"], p["w_in"])
        q, k, v, qt, kt, vt = _mla_prep_fwd(proj, p["qan"], p["kvan"], p["qn"], p["kn"], p["wq"], p["wkv"], rope)
        o_mla, lse = _mla_attn_fwd(q, k, vt)
        o_swa = _swa_fwd(proj, p["sqn"], p["skn"], p["sinks"], swa_tables)
        y, z = _out_fwd(proj, o_mla, o_swa, h_in, p["w_out"], p["cw"])
        saved.append(dict(x=h_in, proj=proj, hb=hb, q=q, k=k, v=v, qt=qt, kt=kt, o_mla=o_mla, lse=lse,
                          o_swa=o_swa, z=z))
        h_in = y

    dy, loss_acc = _loss_head(h_in, target)
    loss = lax.psum(loss_acc[0, 0], ("x", "y", "c"))

    grads = {n: [None] * DEPTH for n in WEIGHT_ORDER}
    for l in reversed(range(DEPTH)):
        p, a = layers[l], saved[l]
        dconv, dgates, do_mla, do_mla_t, delta, do_swa, dcw = _out_bwd(dy, a["proj"], a["o_mla"], a["o_swa"],
                                                             p["w_out"], p["cw"])
        grads["w_out"][l] = _matmul_tn(a["z"], dy, "dw_out")
        grads["conv_w"][l] = dcw[0:3]
        delta_rows = jnp.transpose(delta, (1, 0)).reshape(MLA_HEADS // 2, 2, s)
        dq, dk, dv = _mla_attn_bwd(a["q"], a["qt"], a["k"], a["kt"], a["v"], do_mla, do_mla_t,
                                   a["lse"], delta_rows)
        dmla, dqan, dkvan, dqn, dkn, dwq, dwkv = _mla_prep_bwd(
            a["proj"], p["qan"], p["kvan"], p["qn"], p["kn"], p["wq"], p["wkv"], rope, dq, dk, dv)
        dsq, dskv, dsqn, dskn, dsinks = _swa_bwd(a["proj"], p["sqn"], p["skn"], p["sinks"], swa_tables, do_swa)
        pieces = [dconv, dgates, dsq, dmla, dskv]
        dx, dg = _in_proj_bwd(pieces, a["x"], p["g"], p["w_in"], dy)
        grads["w_in"][l] = [_matmul_tn(a["hb"], pc, "dw_in_%d" % n) for n, pc in enumerate(pieces)]
        grads["norm_g"][l] = dg[0]
        grads["mla_q_a_norm"][l] = dqan[0]
        grads["mla_kv_a_norm"][l] = dkvan[0]
        grads["mla_q_norm"][l] = dqn[0, :MLA_QK]
        grads["mla_k_norm"][l] = dkn[0, :MLA_QK]
        grads["mla_w_qb"][l] = dwq.reshape(MLA_Q_LORA, MLA_HEADS, LANES)[:, :, :MLA_QK].reshape(
            MLA_Q_LORA, MLA_HEADS * MLA_QK)
        dwk = dwkv[:, :MLA_HEADS * LANES].reshape(MLA_KV_LORA, MLA_HEADS, LANES)[:, :, :MLA_NOPE]
        dwv = dwkv[:, MLA_HEADS * LANES:].reshape(MLA_KV_LORA, MLA_HEADS, MLA_V)
        grads["mla_w_kvb"][l] = jnp.concatenate([dwk, dwv], axis=2).reshape(
            MLA_KV_LORA, MLA_HEADS * (MLA_NOPE + MLA_V))
        grads["swa_q_norm"][l] = dsqn[0]
        grads["swa_k_norm"][l] = dskn[0]
        grads["swa_sinks"][l] = dsinks[0]
        dy = dx
    grad_x = dy[None]
    full_grads = {n: jnp.stack(grads[n]) for n in WEIGHT_ORDER if n != "w_in"}

    rest = tuple(n for n in SHARDED if n != "w_in")
    rep_shapes = [weights[n].shape for n in REPLICATED]
    flat_rep = lambda d: jnp.concatenate([d[n].reshape(-1) for n in REPLICATED])

    def chunk(g, n, k):
        width = g.shape[SHARD_AXIS[n]] // N_CHIPS
        return lax.slice_in_dim(g, k * width, (k + 1) * width, axis=SHARD_AXIS[n])

    shard_cols = w_in.shape[2]
    g_in = jnp.stack([
        jnp.concatenate([_cols(grads["w_in"][l], _sub_ranges(OLD_ORDER, k * shard_cols,
                                                             (k + 1) * shard_cols))
                         for l in range(DEPTH)], axis=0)
        for k in range(N_CHIPS)])
    rep_grads = flat_rep(full_grads)
    g_rest = jnp.stack([_pack_rows([chunk(full_grads[n], n, k) for n in rest] + [rep_grads],
                                   F32, PACK_ROWS) for k in range(N_CHIPS)])
    from_sibling = _swap_halves_to_sibling([g_in, g_rest])
    partial = [_add_sibling(g, r, c_idx, dt)
               for g, r, dt in zip((g_in, g_rest), from_sibling, (BF16, F32))]
    by_chip = [_fill_own_slab(q, p, k_idx) for q, p in zip(_scatter_to_chips(partial), partial)]
    g_in_mine, g_rest_mine = _join_halves([_sum_chips(q, c_idx) for q in by_chip])

    pack_rest = lambda d: _pack_rows([d[n] for n in rest] + [flat_rep(d)], F32, PACK_ROWS)
    in_shape = w_in.shape
    res_in = _adamw(g_in_mine, w_in.reshape(w_in_rows, -1), m_w_in.reshape(w_in_rows, -1),
                    v_w_in.reshape(w_in_rows, -1))
    res_rest = _adamw(g_rest_mine, pack_rest(weights), pack_rest(mom_m), pack_rest(mom_v))
    rest_shapes = [weights[n].shape for n in rest] + [(rep_grads.shape[0],)]
    unpacked = []
    for buf_in, buf_rest in zip((g_in_mine,) + tuple(res_in), (g_rest_mine,) + tuple(res_rest)):
        vals = _unpack_rows(buf_rest, rest_shapes)
        group = dict(zip(rest, vals[:-1]))
        group["w_in"] = buf_in.reshape(in_shape)
        off = 0
        for n, sh in zip(REPLICATED, rep_shapes):
            size = sh[0] * sh[1]
            group[n] = vals[-1][off:off + size].reshape(sh)
            off += size
        unpacked.append(group)
    outs = [loss, grad_x]
    for group in unpacked:
        outs += [group[n] for n in WEIGHT_ORDER]
    return tuple(outs)
```

```python
import jax
import numpy as np
import jax.numpy as jnp
from jax import lax
from jax.experimental import pallas as pl
from jax.experimental.pallas import tpu as pltpu

F32 = jnp.float32
BF16 = jnp.bfloat16

D_MODEL = 1024
DEPTH = 2
GROUP = 512
D_MIX = 3 * GROUP
BLOCK = 128
RMS_EPS = 1e-6
NEG_INF = -1e30
MLA_HEADS = 8
MLA_QK = 96
MLA_NOPE = 64
MLA_ROPE = 32
MLA_V = 64
MLA_Q_LORA = 256
MLA_KV_LORA = 128
ROPE_THETA = 10000.0
SWA_HEADS = 8
SWA_KV = 2
SWA_GROUP = 4
SWA_DIM = 64
IN_COLS = 4256
N_CHIPS = 4

NC = 4352
OFF_CONV, OFF_GATES, OFF_SQ, OFF_MLA, OFF_SKV = 0, 1536, 3072, 3584, 4096
PIECE_WIDTHS = (1536, 1536, 512, 512, 256)

VMEM_LIMIT = 56 * 1024 * 1024
LANES = 128
PACK_COLS = 1024
PACK_ROW_ALIGN = 16
PACK_ROWS = 1024

ADAM_LR = 0.001
ADAM_B1 = 0.9
ADAM_B2 = 0.999
ADAM_EPS = 1e-08
ADAM_WD = 0.01
ADAM_STEP = 10

MESH = pl.DeviceIdType.MESH


def _params(sem, vmem=VMEM_LIMIT):
    return pltpu.CompilerParams(dimension_semantics=sem, vmem_limit_bytes=vmem)


def _dot(a, b, dims):
    return lax.dot_general(a.astype(BF16), b.astype(BF16), (dims, ((), ())),
                           preferred_element_type=F32)


def _mm(a, b):
    return _dot(a, b, ((1,), (0,)))


def _mm_nt(a, b):
    return _dot(a, b, ((1,), (1,)))


def _mm_tn(a, b):
    return _dot(a, b, ((0,), (0,)))


@jax.custom_vjp
def _mmd(a, b):
    return _mm(a, b)


def _mmd_fwd(a, b):
    return _mm(a, b), (a, b)


def _mmd_bwd(res, g):
    a, b = res
    return _mm_nt(g, b), _mm_tn(a, g)


_mmd.defvjp(_mmd_fwd, _mmd_bwd)


def _rms(x, g, n=None):
    n = x.shape[-1] if n is None else n
    ms = jnp.sum(x * x, axis=-1, keepdims=True) * (1.0 / n)
    return x * lax.rsqrt(ms + RMS_EPS) * g


def _sigmoid(x):
    return 1.0 / (1.0 + jnp.exp(-x))


@jax.custom_vjp
def _rope(t, c, s1, s2):
    return t * c + pltpu.roll(t, LANES - 16, 1) * s1 + pltpu.roll(t, 16, 1) * s2


def _rope_fwd(t, c, s1, s2):
    return _rope(t, c, s1, s2), (c, s1, s2)


def _rope_bwd(res, g):
    c, s1, s2 = res
    dt = g * c + pltpu.roll(g * s1, 16, 1) + pltpu.roll(g * s2, LANES - 16, 1)
    return dt, jnp.zeros_like(c), jnp.zeros_like(s1), jnp.zeros_like(s2)


_rope.defvjp(_rope_fwd, _rope_bwd)


def _in_proj_fwd(x, g, w):
    s = x.shape[0]
    tm = min(512, s)

    def body(x_ref, g_ref, w_ref, proj_ref, hb_ref):
        hb = _rms(x_ref[...], g_ref[...]).astype(BF16)
        hb_ref[...] = hb
        proj_ref[...] = jnp.dot(hb, w_ref[...], preferred_element_type=F32)

    return pl.pallas_call(
        body, name="in_proj_fwd", grid=(s // tm,),
        in_specs=[pl.BlockSpec((tm, D_MODEL), lambda i: (i, 0)),
                  pl.BlockSpec((1, D_MODEL), lambda i: (0, 0)),
                  pl.BlockSpec((D_MODEL, NC), lambda i: (0, 0))],
        out_specs=[pl.BlockSpec((tm, NC), lambda i: (i, 0)),
                   pl.BlockSpec((tm, D_MODEL), lambda i: (i, 0))],
        out_shape=[jax.ShapeDtypeStruct((s, NC), F32), jax.ShapeDtypeStruct((s, D_MODEL), BF16)],
        compiler_params=_params(("parallel",)),
    )(x, g, w)


def _in_proj_bwd(pieces, x, g, w, dres):
    s = x.shape[0]
    tm = min(512, s)
    n_p = len(pieces)

    def body(*refs):
        p_refs = refs[:n_p]
        x_ref, g_ref, w_ref, dres_ref, dx_ref, dg_ref = refs[n_p:]
        dh = None
        off = 0
        for r in p_refs:
            width = r.shape[1]
            t = _mm_nt(r[...], w_ref[:, off:off + width])
            dh = t if dh is None else dh + t
            off += width
        _, vjp = jax.vjp(_rms, x_ref[...], g_ref[...])
        dx, dg = vjp(dh)
        dx_ref[...] = dx + dres_ref[...]

        @pl.when(pl.program_id(0) == 0)
        def _():
            dg_ref[...] = jnp.zeros_like(dg_ref)

        dg_ref[...] += dg

    in_specs = [pl.BlockSpec((tm, p.shape[1]), lambda i: (i, 0)) for p in pieces]
    in_specs += [pl.BlockSpec((tm, D_MODEL), lambda i: (i, 0)),
                 pl.BlockSpec((1, D_MODEL), lambda i: (0, 0)),
                 pl.BlockSpec((D_MODEL, NC), lambda i: (0, 0)),
                 pl.BlockSpec((tm, D_MODEL), lambda i: (i, 0))]
    return pl.pallas_call(
        body, name="in_proj_bwd", grid=(s // tm,),
        in_specs=in_specs,
        out_specs=[pl.BlockSpec((tm, D_MODEL), lambda i: (i, 0)),
                   pl.BlockSpec((1, D_MODEL), lambda i: (0, 0))],
        out_shape=[jax.ShapeDtypeStruct((s, D_MODEL), F32), jax.ShapeDtypeStruct((1, D_MODEL), F32)],
        compiler_params=_params(("arbitrary",)),
    )(*pieces, x, g, w, dres)


def _matmul_tn(a, b, name):
    s, m = a.shape
    n = b.shape[1]
    tk = min(1024, s)
    tn = min(1536, n)

    def body(a_ref, b_ref, o_ref):
        @pl.when(pl.program_id(1) == 0)
        def _():
            o_ref[...] = jnp.zeros_like(o_ref)

        o_ref[...] += _mm_tn(a_ref[...], b_ref[...])

    return pl.pallas_call(
        body, name=name, grid=(n // tn, s // tk),
        in_specs=[pl.BlockSpec((tk, m), lambda j, k: (k, 0)),
                  pl.BlockSpec((tk, tn), lambda j, k: (k, j))],
        out_specs=pl.BlockSpec((m, tn), lambda j, k: (0, j)),
        out_shape=jax.ShapeDtypeStruct((m, n), F32),
        compiler_params=_params(("parallel", "arbitrary")),
    )(a, b)


def _rms0(x, g, n=None):
    n = x.shape[0] if n is None else n
    ms = jnp.sum(x * x, axis=0, keepdims=True) * (1.0 / n)
    return x * lax.rsqrt(ms + RMS_EPS) * g


@jax.custom_vjp
def _rope0(t, c, s1, s2):
    return t * c + pltpu.roll(t, LANES - 16, 0) * s1 + pltpu.roll(t, 16, 0) * s2


def _rope0_fwd(t, c, s1, s2):
    return _rope0(t, c, s1, s2), (c, s1, s2)


def _rope0_bwd(res, g):
    c, s1, s2 = res
    dt = g * c + pltpu.roll(g * s1, 16, 0) + pltpu.roll(g * s2, LANES - 16, 0)
    return dt, jnp.zeros_like(c), jnp.zeros_like(s1), jnp.zeros_like(s2)


_rope0.defvjp(_rope0_fwd, _rope0_bwd)


@jax.custom_vjp
def _mmw(w, wt, x):
    return _mm(w, x)


def _mmw_fwd(w, wt, x):
    return _mm(w, x), (wt, x)


def _mmw_bwd(res, g):
    wt, x = res
    return _mm_nt(g, x), jnp.zeros_like(wt), _mm(wt, g)


_mmw.defvjp(_mmw_fwd, _mmw_bwd)


def _prep_fn(q_lat, kv_lat, kr, qan, kvan, qn, kn, wq, wk, wv, wqt, wkt, wvt, c, s1, s2, mm):
    tokens = q_lat.shape[1]
    rq = _rms0(q_lat, qan)
    rkv = _rms0(kv_lat, kvan)
    qn_b = jnp.broadcast_to(qn, (LANES, tokens))
    kn_b = jnp.broadcast_to(kn, (LANES, tokens))
    qs, ks = [], []
    for h in range(MLA_HEADS):
        qs.append(_rope0(_rms0(mm(wq[h], wqt[h], rq), qn_b, MLA_QK), c, s1, s2))
        ks.append(_rope0(_rms0(mm(wk[h], wkt[h], rkv) + kr, kn_b, MLA_QK), c, s1, s2))
    return tuple(qs), tuple(ks), mm(wv, wvt, rkv)


def _prep_weights(wq_ref, wkv_ref, wqt_ref, wkvt_ref):
    heads = range(MLA_HEADS)
    wq = tuple(wqt_ref[LANES * h:LANES * (h + 1), :].astype(F32) for h in heads)
    wk = tuple(wkvt_ref[LANES * h:LANES * (h + 1), :].astype(F32) for h in heads)
    wv = wkvt_ref[LANES * MLA_HEADS:, :].astype(F32)
    wqt = tuple(wq_ref[:, LANES * h:LANES * (h + 1)].astype(F32) for h in heads)
    wkt = tuple(wkv_ref[:, LANES * h:LANES * (h + 1)].astype(F32) for h in heads)
    wvt = wkv_ref[:, LANES * MLA_HEADS:].astype(F32)
    return wq, wk, wv, wqt, wkt, wvt


def _prep_in_specs(tm):
    const = lambda shape: pl.BlockSpec(shape, lambda i: (0, 0))
    col = lambda height: pl.BlockSpec((height, tm), lambda i: (0, i))
    return [pl.BlockSpec((tm, 512), lambda i: (i, OFF_MLA // 512)),
            const((MLA_Q_LORA, 1)), const((MLA_KV_LORA, 1)), const((LANES, 1)), const((LANES, 1)),
            const((MLA_Q_LORA, 1024)), const((MLA_KV_LORA, 1536)),
            const((1024, MLA_Q_LORA)), const((1536, MLA_KV_LORA)),
            col(LANES), col(LANES), col(LANES)]


def _prep_operands(blk_ref, refs):
    qan_ref, kvan_ref, qn_ref, kn_ref, wq_ref, wkv_ref, wqt_ref, wkvt_ref, c_ref, s1_ref, s2_ref = refs
    blk_t = jnp.transpose(blk_ref[...])
    diff = (blk_t[0:256], blk_t[256:384], blk_t[384:512],
            qan_ref[...], kvan_ref[...], qn_ref[...], kn_ref[...])
    weights = _prep_weights(wq_ref, wkv_ref, wqt_ref, wkvt_ref)
    return diff, weights, (c_ref[...], s1_ref[...], s2_ref[...])


def _mla_prep_fwd(proj, norms, weights, rope):
    s = proj.shape[0]
    tm = min(512, s)

    def body(blk_ref, *refs):
        ins, (q_ref, k_ref, v_ref, qt_ref, kt_ref, vt_ref) = refs[:11], refs[11:]
        diff, (wq, wk, wv, wqt, wkt, wvt), tables = _prep_operands(blk_ref, ins)
        qs, ks, v = _prep_fn(*diff, wq, wk, wv, wqt, wkt, wvt, *tables,
                             lambda w, wt, x: _mm(w, x))
        for h in range(MLA_HEADS):
            q2 = qs[h] * Q_PRESCALE
            qt_ref[LANES * h:LANES * (h + 1), :] = q2.astype(BF16)
            kt_ref[LANES * h:LANES * (h + 1), :] = ks[h].astype(BF16)
            q_ref[:, LANES * h:LANES * (h + 1)] = jnp.transpose(q2).astype(BF16)
            k_ref[:, LANES * h:LANES * (h + 1)] = jnp.transpose(ks[h]).astype(BF16)
        vt_ref[...] = v.astype(BF16)
        v_ref[...] = jnp.transpose(v).astype(BF16)

    row = lambda width: pl.BlockSpec((tm, width), lambda i: (i, 0))
    col = lambda height: pl.BlockSpec((height, tm), lambda i: (0, i))
    return pl.pallas_call(
        body, name="mla_prep_fwd", grid=(s // tm,),
        in_specs=_prep_in_specs(tm),
        out_specs=[row(1024), row(1024), row(512), col(1024), col(1024), col(512)],
        out_shape=[jax.ShapeDtypeStruct((s, 1024), BF16), jax.ShapeDtypeStruct((s, 1024), BF16),
                   jax.ShapeDtypeStruct((s, 512), BF16), jax.ShapeDtypeStruct((1024, s), BF16),
                   jax.ShapeDtypeStruct((1024, s), BF16), jax.ShapeDtypeStruct((512, s), BF16)],
        compiler_params=_params(("parallel",)),
    )(proj, *norms, *weights, *rope)


def _mla_prep_bwd(proj, norms, weights, rope, dq, dk, dv):
    s = proj.shape[0]
    tm = min(256, s)

    def body(blk_ref, *refs):
        ins, (dq_ref, dk_ref, dv_ref) = refs[:11], refs[11:14]
        dblk_ref, dqan_ref, dkvan_ref, dqn_ref, dkn_ref, dwq_ref, dwkv_ref = refs[14:]
        diff, (wq, wk, wv, wqt, wkt, wvt), tables = _prep_operands(blk_ref, ins)

        def fn(q_lat, kv_lat, kr, qan, kvan, qn, kn, wq_, wk_, wv_):
            return _prep_fn(q_lat, kv_lat, kr, qan, kvan, qn, kn, wq_, wk_, wv_, wqt, wkt, wvt,
                            *tables, _mmw)

        _, vjp = jax.vjp(fn, *diff, wq, wk, wv)
        heads = range(MLA_HEADS)
        cts = (tuple(dq_ref[LANES * h:LANES * (h + 1), :] for h in heads),
               tuple(dk_ref[LANES * h:LANES * (h + 1), :] for h in heads), dv_ref[...])
        dq_lat, dkv_lat, dkr, dqan, dkvan, dqn, dkn, dwq_h, dwk_h, dwv = vjp(cts)
        dblk_ref[...] = jnp.transpose(
            jnp.concatenate([dq_lat, dkv_lat, dkr], axis=0)).astype(BF16)

        @pl.when(pl.program_id(0) == 0)
        def _():
            for r in (dqan_ref, dkvan_ref, dqn_ref, dkn_ref, dwq_ref, dwkv_ref):
                r[...] = jnp.zeros_like(r)

        dqan_ref[...] += dqan
        dkvan_ref[...] += dkvan
        dqn_ref[...] += dqn
        dkn_ref[...] += dkn
        for h in heads:
            dwq_ref[LANES * h:LANES * (h + 1), :] += dwq_h[h]
            dwkv_ref[LANES * h:LANES * (h + 1), :] += dwk_h[h]
        dwkv_ref[LANES * MLA_HEADS:, :] += dwv

    const = lambda shape: pl.BlockSpec(shape, lambda i: (0, 0))
    col = lambda height: pl.BlockSpec((height, tm), lambda i: (0, i))
    shapes = [(MLA_Q_LORA, 1), (MLA_KV_LORA, 1), (LANES, 1), (LANES, 1),
              (1024, MLA_Q_LORA), (1536, MLA_KV_LORA)]
    return pl.pallas_call(
        body, name="mla_prep_bwd", grid=(s // tm,),
        in_specs=_prep_in_specs(tm) + [col(1024), col(1024), col(512)],
        out_specs=[pl.BlockSpec((tm, 512), lambda i: (i, 0))] + [const(sh) for sh in shapes],
        out_shape=[jax.ShapeDtypeStruct((s, 512), BF16)]
        + [jax.ShapeDtypeStruct(sh, F32) for sh in shapes],
        compiler_params=_params(("arbitrary",)),
    )(proj, *norms, *weights, *rope, dq, dk, dv)


MLA_SCALE = MLA_QK ** -0.5


LOG2E = 1.4426950408889634
LN2 = 0.6931471805599453
Q_PRESCALE = MLA_SCALE * LOG2E


def _mla_attn_fwd(q2, k, vt):
    s = q2.shape[0]
    t = min(512, s)
    tk = min(512, s)
    nq = s // t
    r = t // tk

    def body(q_ref, k_ref, vt_ref, o_ref, lse_ref, acc_ref):
        i = pl.program_id(1)
        row = lax.broadcasted_iota(jnp.int32, (tk, t), 0)
        col = lax.broadcasted_iota(jnp.int32, (tk, t), 1)
        qh = [q_ref[:, LANES * hh:LANES * (hh + 1)] for hh in range(2)]
        acc_ref[...] = jnp.zeros_like(acc_ref)

        def scores(j, diag=None):
            r0 = pl.multiple_of(j * tk, tk)
            out = []
            for hh in range(2):
                kc = k_ref[pl.ds(r0, tk), LANES * hh:LANES * (hh + 1)]
                sc = lax.dot_general(kc, qh[hh], (((1,), (1,)), ((), ())),
                                     preferred_element_type=F32)
                out.append(sc if diag is None else jnp.where(row + diag * tk <= col, sc, NEG_INF))
            return tuple(out)

        def consume(j, scs, stats):
            r0 = pl.multiple_of(j * tk, tk)
            out, ps, alphas = [], [], []
            for hh in range(2):
                m, l = stats[hh]
                m_new = jnp.maximum(m, jnp.max(scs[hh], axis=0, keepdims=True))
                p = jnp.exp2(scs[hh] - m_new)
                alpha = jnp.exp2(m - m_new)
                out.append((m_new, alpha * l + jnp.sum(p, axis=0, keepdims=True)))
                ps.append(p.astype(BF16))
                alphas.append(alpha)
            for hh in range(2):
                vc = vt_ref[MLA_V * hh:MLA_V * (hh + 1), pl.ds(r0, tk)]
                acc_ref[hh] = alphas[hh] * acc_ref[hh] + jnp.dot(vc, ps[hh],
                                                                 preferred_element_type=F32)
            return tuple(out)

        stats = tuple((jnp.full((1, t), NEG_INF, F32), jnp.zeros((1, t), F32)) for _ in range(2))
        scs = scores(r * i, 0)
        for d in range(1, r):
            nxt = scores(r * i + d, d)
            stats = consume(r * i + d - 1, scs, stats)
            scs = nxt
        last_diag = r * i + r - 1

        def trip(j, carry):
            scs, stats = carry
            nxt = scores(j)
            stats = consume(jnp.where(j == 0, last_diag, j - 1), scs, stats)
            return nxt, stats

        scs, stats = lax.fori_loop(0, r * i, trip, (scs, stats))
        stats = consume(jnp.where(i == 0, last_diag, r * i - 1), scs, stats)
        for hh in range(2):
            m, l = stats[hh]
            o_ref[:, MLA_V * hh:MLA_V * (hh + 1)] = jnp.transpose(acc_ref[hh] / l)
            lse_ref[0, hh:hh + 1, :] = m + jnp.log2(l)

    return pl.pallas_call(
        body, name="mla_attn_fwd", grid=(MLA_HEADS // 2, nq),
        in_specs=[pl.BlockSpec((t, 256), lambda p, i: (i, p)),
                  pl.BlockSpec((s, 256), lambda p, i: (0, p)),
                  pl.BlockSpec((128, s), lambda p, i: (p, 0))],
        out_specs=[pl.BlockSpec((t, 128), lambda p, i: (i, p)),
                   pl.BlockSpec((1, 2, t), lambda p, i: (p, 0, i))],
        out_shape=[jax.ShapeDtypeStruct((s, 512), F32),
                   jax.ShapeDtypeStruct((MLA_HEADS // 2, 2, s), F32)],
        scratch_shapes=[pltpu.VMEM((2, MLA_V, t), F32)],
        compiler_params=_params(("parallel", "arbitrary")),
    )(q2, k, vt)


def _mla_attn_bwd(q2, q2t, k, kt, v, do, dot, lse_rows, delta_rows):
    s = q2.shape[0]
    t = min(512, s)
    nq = s // t

    def body(q_ref, qt_ref, k_ref, kt_ref, v_ref, do_ref, dot_ref, lse_ref, dl_ref,
             dq_ref, dk_ref, dv_ref):
        j = pl.program_id(1)

        @pl.when(j == 0)
        def _():
            dq_ref[...] = jnp.zeros_like(dq_ref)

        dk_ref[...] = jnp.zeros_like(dk_ref)
        dv_ref[...] = jnp.zeros_like(dv_ref)
        row = lax.broadcasted_iota(jnp.int32, (t, t), 0)
        col = lax.broadcasted_iota(jnp.int32, (t, t), 1)
        causal_t = row <= col
        kh = [k_ref[:, LANES * hh:LANES * (hh + 1)] for hh in range(2)]
        kth = [kt_ref[LANES * hh:LANES * (hh + 1), :] for hh in range(2)]
        vh = [v_ref[:, MLA_V * hh:MLA_V * (hh + 1)] for hh in range(2)]
        nt = (((1,), (1,)), ((), ()))

        def step(i, masked):
            r0 = pl.multiple_of(i * t, t)
            sd = []
            for hh in range(2):
                qh = q_ref[pl.ds(r0, t), LANES * hh:LANES * (hh + 1)]
                doh = do_ref[pl.ds(r0, t), MLA_V * hh:MLA_V * (hh + 1)]
                sc_t = lax.dot_general(kh[hh], qh, nt, preferred_element_type=F32)
                sd.append(jnp.where(causal_t, sc_t, NEG_INF) if masked else sc_t)
                sd.append(lax.dot_general(vh[hh], doh, nt, preferred_element_type=F32))
            pts, gts = [], []
            for hh in range(2):
                lse = lse_ref[0, hh:hh + 1, pl.ds(r0, t)]
                dl = dl_ref[0, hh:hh + 1, pl.ds(r0, t)]
                p_t = jnp.exp2(sd[2 * hh] - lse)
                pts.append(p_t.astype(BF16))
                gts.append((p_t * (sd[2 * hh + 1] - dl)).astype(BF16))
            for hh in range(2):
                qth = qt_ref[LANES * hh:LANES * (hh + 1), pl.ds(r0, t)]
                doth = dot_ref[MLA_V * hh:MLA_V * (hh + 1), pl.ds(r0, t)]
                dv_ref[MLA_V * hh:MLA_V * (hh + 1), :] += lax.dot_general(
                    doth, pts[hh], nt, preferred_element_type=F32)
                dk_ref[LANES * hh:LANES * (hh + 1), :] += lax.dot_general(
                    qth, gts[hh], nt, preferred_element_type=F32)
                dq_ref[LANES * hh:LANES * (hh + 1), pl.ds(r0, t)] += jnp.dot(
                    kth[hh], gts[hh], preferred_element_type=F32)

        step(j, True)

        def trip(i, carry):
            step(i, False)
            return carry

        lax.fori_loop(j + 1, nq, trip, 0)
        dk_ref[...] = dk_ref[...] * LN2

        @pl.when(j == nq - 1)
        def _():
            dq_ref[...] = dq_ref[...] * MLA_SCALE

    return pl.pallas_call(
        body, name="mla_attn_bwd", grid=(MLA_HEADS // 2, nq),
        in_specs=[pl.BlockSpec((s, 256), lambda p, j: (0, p)),
                  pl.BlockSpec((256, s), lambda p, j: (p, 0)),
                  pl.BlockSpec((t, 256), lambda p, j: (j, p)),
                  pl.BlockSpec((256, t), lambda p, j: (p, j)),
                  pl.BlockSpec((t, 128), lambda p, j: (j, p)),
                  pl.BlockSpec((s, 128), lambda p, j: (0, p)),
                  pl.BlockSpec((128, s), lambda p, j: (p, 0)),
                  pl.BlockSpec((1, 2, s), lambda p, j: (p, 0, 0)),
                  pl.BlockSpec((1, 2, s), lambda p, j: (p, 0, 0))],
        out_specs=[pl.BlockSpec((256, s), lambda p, j: (p, 0)),
                   pl.BlockSpec((256, t), lambda p, j: (p, j)),
                   pl.BlockSpec((128, t), lambda p, j: (p, j))],
        out_shape=[jax.ShapeDtypeStruct((1024, s), F32), jax.ShapeDtypeStruct((1024, s), F32),
                   jax.ShapeDtypeStruct((512, s), F32)],
        compiler_params=_params(("parallel", "arbitrary")),
    )(q2, q2t, k, kt, v, do, dot, lse_rows, delta_rows)


SWA_SCALE = SWA_DIM ** -0.5
SWA_COLS = SWA_GROUP * BLOCK
SWA_LOG2 = SWA_SCALE * LOG2E


def _swa_tables():
    k = np.arange(2 * BLOCK)[:, None]
    col = np.arange(SWA_COLS)[None, :]
    dist = BLOCK + (col % BLOCK) - k
    valid = (dist >= 0) & (dist < BLOCK)
    out = np.zeros((2, SWA_KV, 2 * BLOCK, SWA_COLS), np.float32)
    for first in range(2):
        ok = valid & ((k >= BLOCK) | (first == 0))
        for j in range(SWA_KV):
            slope = 2.0 ** -(SWA_GROUP * j + col // BLOCK + 1)
            out[first, j] = np.where(ok, -slope * dist * LOG2E, NEG_INF)
    return jnp.asarray(out)


def _swa_tile_inputs(sq_ref, skv_ref, halo_ref, qn_ref, kn_ref, sk_ref, add_ref, first):
    kv_all = jnp.concatenate([halo_ref[...], skv_ref[...]], axis=0)
    k_raw = [kv_all[:, SWA_DIM * j:SWA_DIM * (j + 1)] for j in range(SWA_KV)]
    v_all = [kv_all[:, 128 + SWA_DIM * j:128 + SWA_DIM * (j + 1)] for j in range(SWA_KV)]
    q_raw = [sq_ref[:, SWA_DIM * h:SWA_DIM * (h + 1)] for h in range(SWA_HEADS)]
    lane_grp = lax.broadcasted_iota(jnp.int32, (1, SWA_COLS), 1) // BLOCK
    sinks, adds = [], []
    for j in range(SWA_KV):
        row = jnp.zeros((1, SWA_COLS), F32)
        for g in range(SWA_GROUP):
            h = SWA_GROUP * j + g
            row = jnp.where(lane_grp == g, sk_ref[:, h:h + 1] * LOG2E, row)
        sinks.append(row)
        adds.append((jnp.where(first, add_ref[1, j], add_ref[0, j]), add_ref[0, j]))
    return k_raw, v_all, q_raw, sinks, adds


def _swa_probs(kb, qs, add, sink):
    s2 = lax.dot_general(kb, qs, (((1,), (1,)), ((), ())), preferred_element_type=F32) * SWA_LOG2 + add
    m = jnp.maximum(jnp.max(s2, axis=0, keepdims=True), sink)
    e = jnp.exp2(s2 - m)
    es = jnp.exp2(sink - m)
    inv = 1.0 / (jnp.sum(e, axis=0, keepdims=True) + es)
    return e, inv, es


def _swa_fwd(proj, qn, kn, sinks, tables):
    s = proj.shape[0]
    ts = min(512, s)
    nb = ts // BLOCK

    def body(sq_ref, skv_ref, halo_ref, qn_ref, kn_ref, sk_ref, add_ref, o_ref, vt_ref):
        first = pl.program_id(0) == 0
        k_raw, v_all, q_raw, sink_rows, adds = _swa_tile_inputs(
            sq_ref, skv_ref, halo_ref, qn_ref, kn_ref, sk_ref, add_ref, first)
        kn_ = [_rms(k, kn_ref[...]).astype(BF16) for k in k_raw]
        for j in range(SWA_KV):
            vt_ref[j] = jnp.transpose(v_all[j]).astype(BF16)
        qn_ = [_rms(q, qn_ref[...]).astype(BF16) for q in q_raw]
        for b in range(nb):
            for j in range(SWA_KV):
                qs = jnp.concatenate([qn_[SWA_GROUP * j + g][BLOCK * b:BLOCK * (b + 1)]
                                      for g in range(SWA_GROUP)], axis=0)
                e, inv, _ = _swa_probs(kn_[j][BLOCK * b:BLOCK * (b + 2)], qs,
                                       adds[j][0 if b == 0 else 1], sink_rows[j])
                o_t = jnp.dot(vt_ref[j, :, BLOCK * b:BLOCK * (b + 2)], (e * inv).astype(BF16),
                              preferred_element_type=F32)
                o = jnp.transpose(o_t)
                for g in range(SWA_GROUP):
                    h = SWA_GROUP * j + g
                    o_ref[BLOCK * b:BLOCK * (b + 1), SWA_DIM * h:SWA_DIM * (h + 1)] = (
                        o[BLOCK * g:BLOCK * (g + 1)])

    const = lambda shape: pl.BlockSpec(shape, lambda i: (0,) * len(shape))
    return pl.pallas_call(
        body, name="swa_fwd", grid=(s // ts,),
        in_specs=[pl.BlockSpec((ts, 512), lambda i: (i, OFF_SQ // 512)),
                  pl.BlockSpec((ts, 256), lambda i: (i, OFF_SKV // 256)),
                  pl.BlockSpec((BLOCK, 256), lambda i: (jnp.maximum(i * nb - 1, 0), OFF_SKV // 256)),
                  const((1, SWA_DIM)), const((1, SWA_DIM)), const((1, SWA_HEADS)),
                  const(tables.shape)],
        out_specs=pl.BlockSpec((ts, 512), lambda i: (i, 0)),
        out_shape=jax.ShapeDtypeStruct((s, 512), F32),
        scratch_shapes=[pltpu.VMEM((SWA_KV, SWA_DIM, ts + BLOCK), BF16)],
        compiler_params=_params(("parallel",)),
    )(proj, proj, proj, qn, kn, sinks, tables)


def _swa_bwd(proj, qn, kn, sinks, tables, do):
    s = proj.shape[0]
    ts = min(512, s)
    nb = ts // BLOCK
    nt = s // ts

    def body(sq_ref, skv_ref, halo_ref, qn_ref, kn_ref, sk_ref, add_ref, do_ref,
             dsq_ref, dskv_ref, dqn_ref, dkn_ref, dsk_ref, carry_ref, knt_ref):
        step = pl.program_id(0)
        first = step == nt - 1

        @pl.when(step == 0)
        def _():
            carry_ref[...] = jnp.zeros_like(carry_ref)
            dqn_ref[...] = jnp.zeros_like(dqn_ref)
            dkn_ref[...] = jnp.zeros_like(dkn_ref)
            dsk_ref[...] = jnp.zeros_like(dsk_ref)

        k_raw, v_all, q_raw, sink_rows, adds = _swa_tile_inputs(
            sq_ref, skv_ref, halo_ref, qn_ref, kn_ref, sk_ref, add_ref, first)
        kn_f = [_rms(k, kn_ref[...]) for k in k_raw]
        kn_ = [k.astype(BF16) for k in kn_f]
        for j in range(SWA_KV):
            knt_ref[j] = jnp.transpose(kn_f[j]).astype(BF16)
        vb = [v.astype(BF16) for v in v_all]
        qn_ = [_rms(q, qn_ref[...]).astype(BF16) for q in q_raw]

        zero = jnp.zeros((BLOCK, SWA_DIM), F32)
        dk_chunks = [[zero] * (nb + 1) for _ in range(SWA_KV)]
        dv_chunks = [[zero] * (nb + 1) for _ in range(SWA_KV)]
        dq_blocks = [[None] * nb for _ in range(SWA_HEADS)]
        dsink = [jnp.zeros((1, SWA_COLS), F32) for _ in range(SWA_KV)]
        for b in range(nb):
            rows = slice(BLOCK * b, BLOCK * (b + 1))
            band = slice(BLOCK * b, BLOCK * (b + 2))
            for j in range(SWA_KV):
                heads = [SWA_GROUP * j + g for g in range(SWA_GROUP)]
                qs = jnp.concatenate([qn_[h][rows] for h in heads], axis=0)
                dos = jnp.concatenate([do_ref[rows, SWA_DIM * h:SWA_DIM * (h + 1)] for h in heads],
                                      axis=0).astype(BF16)
                e, inv, es = _swa_probs(kn_[j][band], qs, adds[j][0 if b == 0 else 1], sink_rows[j])
                p = e * inv
                dp = lax.dot_general(vb[j][band], dos, (((1,), (1,)), ((), ())),
                                     preferred_element_type=F32)
                dsum = jnp.sum(p * dp, axis=0, keepdims=True)
                dsink[j] = dsink[j] - es * inv * dsum
                g_t = (p * (dp - dsum) * SWA_SCALE).astype(BF16)
                dv_b = jnp.dot(p.astype(BF16), dos, preferred_element_type=F32)
                dk_b = jnp.dot(g_t, qs, preferred_element_type=F32)
                dq_s = jnp.transpose(jnp.dot(knt_ref[j, :, band], g_t,
                                             preferred_element_type=F32))
                for half in range(2):
                    part = slice(BLOCK * half, BLOCK * (half + 1))
                    dk_chunks[j][b + half] = dk_chunks[j][b + half] + dk_b[part]
                    dv_chunks[j][b + half] = dv_chunks[j][b + half] + dv_b[part]
                for g, h in enumerate(heads):
                    dq_blocks[h][b] = dq_s[BLOCK * g:BLOCK * (g + 1)]

        dqn = jnp.zeros((1, SWA_DIM), F32)
        for h in range(SWA_HEADS):
            _, vjp = jax.vjp(_rms, q_raw[h], qn_ref[...])
            dq, dg = vjp(jnp.concatenate(dq_blocks[h], axis=0))
            dsq_ref[:, SWA_DIM * h:SWA_DIM * (h + 1)] = dq.astype(BF16)
            dqn = dqn + dg
        dqn_ref[...] += dqn
        dkn = jnp.zeros((1, SWA_DIM), F32)
        lane_grp = lax.broadcasted_iota(jnp.int32, (1, SWA_COLS), 1) // BLOCK
        for j in range(SWA_KV):
            _, vjp = jax.vjp(_rms, k_raw[j], kn_ref[...])
            dk, dg = vjp(jnp.concatenate(dk_chunks[j], axis=0))
            dkn = dkn + dg
            dv = jnp.concatenate(dv_chunks[j], axis=0)
            for part, val in ((0, dk), (1, dv)):
                c0 = 128 * part + SWA_DIM * j
                dskv_ref[0:ts - BLOCK, c0:c0 + SWA_DIM] = val[BLOCK:ts].astype(BF16)
                dskv_ref[ts - BLOCK:ts, c0:c0 + SWA_DIM] = (
                    val[ts:ts + BLOCK] + carry_ref[:, c0:c0 + SWA_DIM]).astype(BF16)
                carry_ref[:, c0:c0 + SWA_DIM] = val[0:BLOCK]
            for g in range(SWA_GROUP):
                h = SWA_GROUP * j + g
                dsk_ref[:, h:h + 1] += jnp.sum(jnp.where(lane_grp == g, dsink[j], 0.0), axis=1,
                                               keepdims=True)
        dkn_ref[...] += dkn

    const = lambda shape: pl.BlockSpec(shape, lambda st: (0,) * len(shape))
    return pl.pallas_call(
        body, name="swa_bwd", grid=(nt,),
        in_specs=[pl.BlockSpec((ts, 512), lambda st: (nt - 1 - st, OFF_SQ // 512)),
                  pl.BlockSpec((ts, 256), lambda st: (nt - 1 - st, OFF_SKV // 256)),
                  pl.BlockSpec((BLOCK, 256),
                               lambda st: (jnp.maximum((nt - 1 - st) * nb - 1, 0), OFF_SKV // 256)),
                  const((1, SWA_DIM)), const((1, SWA_DIM)), const((1, SWA_HEADS)),
                  const(tables.shape),
                  pl.BlockSpec((ts, 512), lambda st: (nt - 1 - st, 0))],
        out_specs=[pl.BlockSpec((ts, 512), lambda st: (nt - 1 - st, 0)),
                   pl.BlockSpec((ts, 256), lambda st: (nt - 1 - st, 0)),
                   const((1, SWA_DIM)), const((1, SWA_DIM)), const((1, SWA_HEADS))],
        out_shape=[jax.ShapeDtypeStruct((s, 512), BF16), jax.ShapeDtypeStruct((s, 256), BF16),
                   jax.ShapeDtypeStruct((1, SWA_DIM), F32), jax.ShapeDtypeStruct((1, SWA_DIM), F32),
                   jax.ShapeDtypeStruct((1, SWA_HEADS), F32)],
        scratch_shapes=[pltpu.VMEM((BLOCK, 256), F32),
                        pltpu.VMEM((SWA_KV, SWA_DIM, ts + BLOCK), BF16)],
        compiler_params=_params(("arbitrary",)),
    )(proj, proj, proj, qn, kn, sinks, tables, do)


HALO = 8


def _shift_down(u, halo, k):
    tm = u.shape[0]
    rid = lax.broadcasted_iota(jnp.int32, u.shape, 0)
    out = pltpu.roll(u, k, 0)
    for r in range(k):
        out = jnp.where(rid == r, halo[HALO - k + r:HALO - k + r + 1, :], out)
    return out


def _shift_up(u, halo, k):
    tm = u.shape[0]
    rid = lax.broadcasted_iota(jnp.int32, u.shape, 0)
    out = pltpu.roll(u, tm - k, 0)
    for r in range(k):
        out = jnp.where(rid == tm - k + r, halo[r:r + 1, :], out)
    return out


def _conv_fwd_vals(conv_ref, convp_ref, cw_ref, is_first):
    c_h, c_b, c_c = conv_ref[:, 0:512], conv_ref[:, 512:1024], conv_ref[:, 1024:1536]
    u = c_c * c_h
    up = jnp.where(is_first, 0.0, convp_ref[:, 1024:1536] * convp_ref[:, 0:512])
    u1 = _shift_down(u, up, 1)
    u2 = _shift_down(u, up, 2)
    yc = cw_ref[0:1, :] * u2 + cw_ref[1:2, :] * u1 + cw_ref[2:3, :] * u
    return c_h, c_b, c_c, u, u1, u2, yc


def _out_fwd(proj, o_mla, o_swa, x, w_out, cw):
    s = proj.shape[0]
    tm = min(512, s)

    def body(conv_ref, convp_ref, gates_ref, om_ref, os_ref, x_ref, w_ref, cw_ref, y_ref, z_ref):
        i = pl.program_id(0)
        _, c_b, _, _, _, _, yc = _conv_fwd_vals(conv_ref, convp_ref, cw_ref, i == 0)
        mix = (om_ref[...], c_b * yc, os_ref[...])
        for n in range(3):
            g = gates_ref[:, GROUP * n:GROUP * (n + 1)]
            z_ref[:, GROUP * n:GROUP * (n + 1)] = (mix[n] * (g * _sigmoid(g))).astype(BF16)
        y_ref[...] = x_ref[...] + jnp.dot(z_ref[...], w_ref[...], preferred_element_type=F32)

    row = lambda width: pl.BlockSpec((tm, width), lambda i: (i, 0))
    return pl.pallas_call(
        body, name="out_fwd", grid=(s // tm,),
        in_specs=[pl.BlockSpec((tm, 1536), lambda i: (i, 0)),
                  pl.BlockSpec((HALO, 1536), lambda i: (jnp.maximum(i * (tm // HALO) - 1, 0), 0)),
                  pl.BlockSpec((tm, 1536), lambda i: (i, 1)),
                  row(512), row(512), row(D_MODEL),
                  pl.BlockSpec((D_MIX, D_MODEL), lambda i: (0, 0)),
                  pl.BlockSpec((HALO, 512), lambda i: (0, 0))],
        out_specs=[row(D_MODEL), row(D_MIX)],
        out_shape=[jax.ShapeDtypeStruct((s, D_MODEL), F32), jax.ShapeDtypeStruct((s, D_MIX), BF16)],
        compiler_params=_params(("parallel",)),
    )(proj, proj, proj, o_mla, o_swa, x, w_out, cw)


def _out_bwd(dy, proj, o_mla, o_swa, w_out, cw):
    s = proj.shape[0]
    tm = min(512, s)
    nt = s // tm
    hb = tm // HALO

    def body(dy_ref, dyn_ref, conv_ref, convp_ref, convn_ref, gates_ref, gatesn_ref, om_ref, os_ref,
             w_ref, cw_ref,
             dconv_ref, dgates_ref, dom_ref, domt_ref, delta_ref, dos_ref, dcw_ref):
        i = pl.program_id(0)
        dz = _mm_nt(dy_ref[...], w_ref[...])

        def gate(n):
            g = gates_ref[:, GROUP * n:GROUP * (n + 1)]
            sg = _sigmoid(g)
            return g * sg, sg * (1.0 + g * (1.0 - sg))

        for n, o_ref, do_ref in ((0, om_ref, dom_ref), (2, os_ref, dos_ref)):
            silu, dsilu = gate(n)
            dzn = dz[:, GROUP * n:GROUP * (n + 1)]
            o = o_ref[...]
            do = dzn * silu
            do_ref[...] = do.astype(do_ref.dtype)
            dgates_ref[:, GROUP * n:GROUP * (n + 1)] = (dzn * o * dsilu).astype(BF16)
            if n == 0:
                domt_ref[...] = jnp.transpose(do).astype(BF16)
                t = do * o
                for h in range(MLA_HEADS):
                    delta_ref[:, h:h + 1] = jnp.sum(t[:, MLA_V * h:MLA_V * (h + 1)], axis=-1,
                                                    keepdims=True)

        c_h, c_b, c_c, u, u1, u2, yc = _conv_fwd_vals(conv_ref, convp_ref, cw_ref, i == 0)
        silu, dsilu = gate(1)
        dzc = dz[:, GROUP:2 * GROUP]
        dgates_ref[:, GROUP:2 * GROUP] = (dzc * (c_b * yc) * dsilu).astype(BF16)
        dycr = dzc * silu
        dyc = dycr * c_b
        gn = gatesn_ref[:, GROUP:2 * GROUP]
        dzc_n = _mm_nt(dyn_ref[...], w_ref[GROUP:2 * GROUP, :])
        dyc_n = jnp.where(i == nt - 1, 0.0, dzc_n * (gn * _sigmoid(gn)) * convn_ref[:, 512:1024])
        d1 = _shift_up(dyc, dyc_n, 1)
        d2 = _shift_up(dyc, dyc_n, 2)
        du = cw_ref[2:3, :] * dyc + cw_ref[1:2, :] * d1 + cw_ref[0:1, :] * d2
        dconv_ref[:, 0:512] = (du * c_c).astype(BF16)
        dconv_ref[:, 512:1024] = (dycr * yc).astype(BF16)
        dconv_ref[:, 1024:1536] = (du * c_h).astype(BF16)

        @pl.when(i == 0)
        def _():
            dcw_ref[...] = jnp.zeros_like(dcw_ref)

        for k, uk in enumerate((u2, u1, u)):
            dcw_ref[k:k + 1, :] += jnp.sum(dyc * uk, axis=0, keepdims=True)

    row = lambda width: pl.BlockSpec((tm, width), lambda i: (i, 0))
    prev = lambda i: jnp.maximum(i * hb - 1, 0)
    nxt = lambda i: jnp.minimum((i + 1) * hb, s // HALO - 1)
    return pl.pallas_call(
        body, name="out_bwd", grid=(nt,),
        in_specs=[row(D_MODEL),
                  pl.BlockSpec((HALO, D_MODEL), lambda i: (nxt(i), 0)),
                  pl.BlockSpec((tm, 1536), lambda i: (i, 0)),
                  pl.BlockSpec((HALO, 1536), lambda i: (prev(i), 0)),
                  pl.BlockSpec((HALO, 1536), lambda i: (nxt(i), 0)),
                  pl.BlockSpec((tm, 1536), lambda i: (i, 1)),
                  pl.BlockSpec((HALO, 1536), lambda i: (nxt(i), 1)),
                  row(512), row(512),
                  pl.BlockSpec((D_MIX, D_MODEL), lambda i: (0, 0)),
                  pl.BlockSpec((HALO, 512), lambda i: (0, 0))],
        out_specs=[row(1536), row(1536), row(512), pl.BlockSpec((512, tm), lambda i: (0, i)),
                   row(MLA_HEADS), row(512), pl.BlockSpec((HALO, 512), lambda i: (0, 0))],
        out_shape=[jax.ShapeDtypeStruct((s, 1536), BF16), jax.ShapeDtypeStruct((s, 1536), BF16),
                   jax.ShapeDtypeStruct((s, 512), BF16), jax.ShapeDtypeStruct((512, s), BF16),
                   jax.ShapeDtypeStruct((s, MLA_HEADS), F32),
                   jax.ShapeDtypeStruct((s, 512), BF16), jax.ShapeDtypeStruct((HALO, 512), F32)],
        compiler_params=_params(("arbitrary",)),
    )(dy, dy, proj, proj, proj, proj, proj, o_mla, o_swa, w_out, cw)


def _loss_head(y, target):
    s, d = y.shape
    tm = min(512, s)
    nt = s // tm

    def body(y_ref, t_ref, dy_ref, loss_ref):
        i = pl.program_id(0)
        err = y_ref[...] - t_ref[...]
        dy_ref[...] = err * (1.0 / d)

        @pl.when(i == 0)
        def _():
            loss_ref[...] = jnp.zeros_like(loss_ref)

        sq = jnp.sum((err * err).reshape(tm // 8, 8, d), axis=0)
        part = sq[:, 0:LANES]
        for c in range(1, d // LANES):
            part = part + sq[:, LANES * c:LANES * (c + 1)]
        loss_ref[...] += part

        @pl.when(i == nt - 1)
        def _():
            loss_ref[...] = jnp.full(loss_ref.shape, (0.5 / d) * jnp.sum(loss_ref[...]), F32)

    return pl.pallas_call(
        body, name="loss_head", grid=(nt,),
        in_specs=[pl.BlockSpec((tm, d), lambda i: (i, 0)), pl.BlockSpec((tm, d), lambda i: (i, 0))],
        out_specs=[pl.BlockSpec((tm, d), lambda i: (i, 0)), pl.BlockSpec((8, LANES), lambda i: (0, 0))],
        out_shape=[jax.ShapeDtypeStruct((s, d), F32), jax.ShapeDtypeStruct((8, LANES), F32)],
        compiler_params=_params(("arbitrary",)),
    )(y, target)


def _adamw(g, w, m, v):
    rows = g.shape[0]
    tr = min(256, rows)
    c1 = 1.0 - ADAM_B1
    c2 = 1.0 - ADAM_B2
    bc1 = 1.0 - ADAM_B1 ** ADAM_STEP
    bc2 = 1.0 - ADAM_B2 ** ADAM_STEP

    def body(g_ref, w_ref, m_ref, v_ref, d_ref, mo_ref, vo_ref):
        gg = g_ref[...]
        m_new = ADAM_B1 * m_ref[...] + c1 * gg
        v_new = ADAM_B2 * v_ref[...] + c2 * (gg * gg)
        m_hat = m_new / bc1
        v_hat = v_new / bc2
        d_ref[...] = -ADAM_LR * (m_hat / (jnp.sqrt(v_hat) + ADAM_EPS) + ADAM_WD * w_ref[...])
        mo_ref[...] = m_new
        vo_ref[...] = v_new

    spec = pl.BlockSpec((tr, g.shape[1]), lambda i: (i, 0))
    return pl.pallas_call(
        body, name="adamw", grid=(rows // tr,),
        in_specs=[spec] * 4, out_specs=[spec] * 3,
        out_shape=[jax.ShapeDtypeStruct(g.shape, F32)] * 3,
        compiler_params=_params(("parallel",)),
    )(g, w, m, v)


HBM_SPEC = pl.BlockSpec(memory_space=pltpu.HBM)


def _place():
    x, y, c = lax.axis_index("x"), lax.axis_index("y"), lax.axis_index("c")
    chips = [(1 - x, y), (x, 1 - y), (1 - x, 1 - y)]
    return x, y, c, chips


def _all_gather(shards):
    na = len(shards)
    halves = [sh.shape[0] // 2 for sh in shards]

    def body(*refs):
        w_refs, a_refs = refs[:na], refs[na:2 * na]
        send_sems, recv_sems = refs[2 * na:]
        x, y, c, chips = _place()
        k = 2 * x + y
        sib = (x, y, 1 - c)

        def slab(a, kk, hc):
            return a_refs[a].at[kk, pl.ds(hc * halves[a], halves[a]), :]

        def copy(a, n, src, dst, to):
            return pltpu.make_async_remote_copy(
                src_ref=src, dst_ref=dst, send_sem=send_sems.at[6 * a + n],
                recv_sem=recv_sems.at[6 * a + n], device_id=to, device_id_type=MESH)

        first = [copy(a, n, w_refs[a].at[pl.ds(c * halves[a], halves[a]), :], slab(a, k, c),
                      (cx, cy, c))
                 for n, (cx, cy) in enumerate(chips) for a in range(na)]
        for cp in first:
            cp.start()
        passed = []
        for n, (cx, cy) in enumerate(chips):
            kk = 2 * cx + cy
            for a in range(na):
                copy(a, n, slab(a, kk, c), slab(a, kk, c), (cx, cy, c)).wait_recv()
                fwd = copy(a, 3 + n, slab(a, kk, c), slab(a, kk, c), sib)
                fwd.start()
                passed.append(fwd)
        for n, (cx, cy) in enumerate(chips):
            kk = 2 * cx + cy
            for a in range(na):
                copy(a, 3 + n, slab(a, kk, 1 - c), slab(a, kk, 1 - c), sib).wait_recv()
        for cp in first + passed:
            cp.wait_send()

    return pl.pallas_call(
        body, name="weights_all_gather",
        in_specs=[HBM_SPEC] * na, out_specs=[HBM_SPEC] * na,
        out_shape=[jax.ShapeDtypeStruct((N_CHIPS,) + sh.shape, sh.dtype) for sh in shards],
        scratch_shapes=[pltpu.SemaphoreType.DMA((6 * na,)), pltpu.SemaphoreType.DMA((6 * na,))],
    )(*shards)


def _fill_own_slab(buf, src, k_idx):
    n, rows, cols = buf.shape
    tr = _row_tile(rows)
    slabs = src.ndim == 3

    def body(k_ref, src_ref, buf_ref, out_ref):
        out_ref[0] = src_ref[0] if slabs else src_ref[...]

    if slabs:
        src_spec = pl.BlockSpec((1, tr, cols), lambda t, k_ref: (k_ref[0], t, 0))
    else:
        src_spec = pl.BlockSpec((tr, cols), lambda t, k_ref: (t, 0))
    return pl.pallas_call(
        body, name="fill_own_slab",
        grid_spec=pltpu.PrefetchScalarGridSpec(
            num_scalar_prefetch=1, grid=(rows // tr,),
            in_specs=[src_spec, pl.BlockSpec(memory_space=pl.ANY)],
            out_specs=pl.BlockSpec((1, tr, cols), lambda t, k_ref: (k_ref[0], t, 0))),
        out_shape=jax.ShapeDtypeStruct(buf.shape, buf.dtype),
        input_output_aliases={2: 0},
        compiler_params=_params(("parallel",)),
    )(k_idx, src, buf)


def _swap_halves_to_sibling(gs):
    na = len(gs)

    def body(*refs):
        g_refs, r_refs = refs[:na], refs[na:2 * na]
        send_sems, recv_sems = refs[2 * na:]
        x, y, c, _ = _place()
        cps = []
        for a in range(na):
            half = g_refs[a].shape[1] // 2
            cps.append(pltpu.make_async_remote_copy(
                src_ref=g_refs[a].at[:, pl.ds((1 - c) * half, half), :], dst_ref=r_refs[a],
                send_sem=send_sems.at[a], recv_sem=recv_sems.at[a], device_id=(x, y, 1 - c),
                device_id_type=MESH))
        for cp in cps:
            cp.start()
        for cp in cps:
            cp.wait()

    return pl.pallas_call(
        body, name="grads_to_sibling",
        in_specs=[HBM_SPEC] * na, out_specs=[HBM_SPEC] * na,
        out_shape=[jax.ShapeDtypeStruct((g.shape[0], g.shape[1] // 2, g.shape[2]), g.dtype)
                   for g in gs],
        scratch_shapes=[pltpu.SemaphoreType.DMA((na,)), pltpu.SemaphoreType.DMA((na,))],
    )(*gs)


def _row_tile(rows):
    return 256 if rows % 256 == 0 else 128


def _add_sibling(g, r, c_idx, out_dtype):
    n, rows, cols = g.shape
    half = rows // 2
    tr = _row_tile(half)
    nb = half // tr

    def body(c_ref, g_ref, r_ref, p_ref):
        p_ref[...] = (g_ref[...] + r_ref[...]).astype(out_dtype)

    return pl.pallas_call(
        body, name="grads_add_sibling",
        grid_spec=pltpu.PrefetchScalarGridSpec(
            num_scalar_prefetch=1, grid=(n, nb),
            in_specs=[pl.BlockSpec((1, tr, cols), lambda j, t, c_ref: (j, c_ref[0] * nb + t, 0)),
                      pl.BlockSpec((1, tr, cols), lambda j, t, c_ref: (j, t, 0))],
            out_specs=pl.BlockSpec((1, tr, cols), lambda j, t, c_ref: (j, t, 0))),
        out_shape=jax.ShapeDtypeStruct((n, half, cols), out_dtype),
        compiler_params=_params(("parallel", "parallel")),
    )(c_idx, g, r)


def _scatter_to_chips(ps):
    na = len(ps)

    def body(*refs):
        p_refs, q_refs = refs[:na], refs[na:2 * na]
        send_sems, recv_sems = refs[2 * na:]
        x, y, c, chips = _place()
        k = 2 * x + y
        sends = []
        for i, (cx, cy) in enumerate(chips):
            for a in range(na):
                cp = pltpu.make_async_remote_copy(
                    src_ref=p_refs[a].at[2 * cx + cy], dst_ref=q_refs[a].at[k],
                    send_sem=send_sems.at[3 * a + i], recv_sem=recv_sems.at[3 * a + i],
                    device_id=(cx, cy, c), device_id_type=MESH)
                cp.start()
                sends.append(cp)
        for i, (cx, cy) in enumerate(chips):
            kk = 2 * cx + cy
            for a in range(na):
                pltpu.make_async_remote_copy(
                    src_ref=p_refs[a].at[kk], dst_ref=q_refs[a].at[kk],
                    send_sem=send_sems.at[3 * a + i], recv_sem=recv_sems.at[3 * a + i],
                    device_id=(cx, cy, c), device_id_type=MESH).wait_recv()
        for cp in sends:
            cp.wait_send()

    return pl.pallas_call(
        body, name="grads_scatter_to_chips",
        in_specs=[HBM_SPEC] * na, out_specs=[HBM_SPEC] * na,
        out_shape=[jax.ShapeDtypeStruct(p.shape, p.dtype) for p in ps],
        scratch_shapes=[pltpu.SemaphoreType.DMA((3 * na,)), pltpu.SemaphoreType.DMA((3 * na,))],
    )(*ps)


def _sum_chips(q, c_idx):
    n, half, cols = q.shape
    tr = _row_tile(half)
    nb = half // tr

    def body(c_ref, q_ref, o_ref):
        parts = [q_ref[kk].astype(F32) for kk in range(n)]
        o_ref[...] = ((parts[0] + parts[1]) + parts[2]) + parts[3]

    return pl.pallas_call(
        body, name="grads_sum_chips",
        grid_spec=pltpu.PrefetchScalarGridSpec(
            num_scalar_prefetch=1, grid=(nb,),
            in_specs=[pl.BlockSpec((n, tr, cols), lambda t, c_ref: (0, t, 0))],
            out_specs=pl.BlockSpec((tr, cols), lambda t, c_ref: (c_ref[0] * nb + t, 0))),
        out_shape=jax.ShapeDtypeStruct((2 * half, cols), F32),
        compiler_params=_params(("parallel",)),
    )(c_idx, q)


def _join_halves(fulls):
    na = len(fulls)

    def body(*refs):
        o_refs = refs[na:2 * na]
        send_sems, recv_sems = refs[2 * na:]
        x, y, c, _ = _place()
        sends = []
        for a in range(na):
            half = o_refs[a].shape[0] // 2
            rows = o_refs[a].at[pl.ds(c * half, half), :]
            sends.append(pltpu.make_async_remote_copy(
                src_ref=rows, dst_ref=rows, send_sem=send_sems.at[a], recv_sem=recv_sems.at[a],
                device_id=(x, y, 1 - c), device_id_type=MESH))
        for cp in sends:
            cp.start()
        for a in range(na):
            half = o_refs[a].shape[0] // 2
            other = o_refs[a].at[pl.ds((1 - c) * half, half), :]
            pltpu.make_async_remote_copy(
                src_ref=other, dst_ref=other, send_sem=send_sems.at[a], recv_sem=recv_sems.at[a],
                device_id=(x, y, 1 - c), device_id_type=MESH).wait_recv()
        for cp in sends:
            cp.wait_send()

    return pl.pallas_call(
        body, name="grads_join_halves",
        in_specs=[HBM_SPEC] * na, out_specs=[HBM_SPEC] * na,
        out_shape=[jax.ShapeDtypeStruct(f.shape, f.dtype) for f in fulls],
        input_output_aliases={a: a for a in range(na)},
        scratch_shapes=[pltpu.SemaphoreType.DMA((na,)), pltpu.SemaphoreType.DMA((na,))],
    )(*fulls)


def _part_rows(shape):
    size = 1
    for d in shape:
        size *= d
    rows = -(-size // PACK_COLS)
    return size, -(-rows // PACK_ROW_ALIGN) * PACK_ROW_ALIGN


def _pack_rows(arrays, dtype, total_rows):
    parts, used = [], 0
    for a in arrays:
        size, rows = _part_rows(a.shape)
        flat = a.reshape(-1).astype(dtype)
        parts.append(jnp.pad(flat, (0, rows * PACK_COLS - size)).reshape(rows, PACK_COLS))
        used += rows
    parts.append(jnp.zeros((total_rows - used, PACK_COLS), dtype))
    return jnp.concatenate(parts, axis=0)


def _unpack_rows(buf, shapes):
    lead = buf.shape[:-2]
    out, off = [], 0
    for sh in shapes:
        size, rows = _part_rows(sh)
        part = buf[..., off:off + rows, :].reshape(lead + (-1,))[..., :size]
        out.append(part.reshape(lead + tuple(sh)))
        off += rows
    return out


NEW_ORDER = ((928, 1440), (1440, 1952), (1952, 2464), (416, 928), (2464, 2976), (3744, 4256),
             (2976, 3488), (0, 256), (256, 384), (4256, 4320), (384, 416), (4256, 4288),
             (3488, 3616), (3616, 3744))
OLD_ORDER = ((3584, 3840), (3840, 3968), (4032, 4064), (1536, 2048), (0, 512), (512, 1024),
             (1024, 1536), (2048, 2560), (3072, 3584), (4096, 4224), (4224, 4352), (2560, 3072))


def _cols(sources, ranges):
    parts = []
    for a, b in ranges:
        off = 0
        for src in sources:
            width = src.shape[-1]
            lo, hi = max(a, off), min(b, off + width)
            if lo < hi:
                parts.append(src[..., lo - off:hi - off])
            off += width
    return jnp.concatenate(parts, axis=-1)


def _sub_ranges(ranges, a, b):
    out, off = [], 0
    for lo, hi in ranges:
        width = hi - lo
        s0, s1 = max(a, off), min(b, off + width)
        if s0 < s1:
            out.append((lo + s0 - off, lo + s1 - off))
        off += width
    return out


def _rope_tables(s):
    half = MLA_ROPE // 2
    inv_freq = jnp.power(jnp.float32(ROPE_THETA), -jnp.arange(half, dtype=F32) / half)
    ang = inv_freq[:, None] * jnp.arange(s, dtype=F32)[None, :]
    cos, sin = jnp.cos(ang), jnp.sin(ang)
    z = lambda n: jnp.zeros((n, s), F32)
    c = jnp.concatenate([jnp.ones((MLA_NOPE, s), F32), cos, cos, z(32)], axis=0)
    s1 = jnp.concatenate([z(MLA_NOPE), -sin, z(16), z(32)], axis=0)
    s2 = jnp.concatenate([z(MLA_NOPE), z(16), sin, z(32)], axis=0)
    return c, s1, s2


def _pad_lanes(a, n):
    return jnp.pad(a, ((0, 0), (0, n - a.shape[1])))


SHARDED = ("w_in", "w_out", "mla_w_qb", "mla_w_kvb", "conv_w")
REPLICATED = ("norm_g", "mla_q_a_norm", "mla_kv_a_norm", "mla_q_norm", "mla_k_norm",
              "swa_q_norm", "swa_k_norm", "swa_sinks")
WEIGHT_ORDER = ("norm_g", "w_in", "mla_q_a_norm", "mla_w_qb", "mla_kv_a_norm", "mla_w_kvb",
                "mla_q_norm", "mla_k_norm", "conv_w", "swa_q_norm", "swa_k_norm", "swa_sinks", "w_out")
SHARD_AXIS = {"w_in": 2, "w_out": 1, "mla_w_qb": 2, "mla_w_kvb": 2, "conv_w": 2}


def kernel(x, norm_g, w_in, mla_q_a_norm, mla_w_qb, mla_kv_a_norm, mla_w_kvb, mla_q_norm, mla_k_norm, conv_w, swa_q_norm, swa_k_norm, swa_sinks, w_out, loss_target, m_norm_g, m_w_in, m_mla_q_a_norm, m_mla_w_qb, m_mla_kv_a_norm, m_mla_w_kvb, m_mla_q_norm, m_mla_k_norm, m_conv_w, m_swa_q_norm, m_swa_k_norm, m_swa_sinks, m_w_out, v_norm_g, v_w_in, v_mla_q_a_norm, v_mla_w_qb, v_mla_kv_a_norm, v_mla_w_kvb, v_mla_q_norm, v_mla_k_norm, v_conv_w, v_swa_q_norm, v_swa_k_norm, v_swa_sinks, v_w_out):
    weights = dict(norm_g=norm_g, w_in=w_in, mla_q_a_norm=mla_q_a_norm, mla_w_qb=mla_w_qb,
                   mla_kv_a_norm=mla_kv_a_norm, mla_w_kvb=mla_w_kvb, mla_q_norm=mla_q_norm,
                   mla_k_norm=mla_k_norm, conv_w=conv_w, swa_q_norm=swa_q_norm,
                   swa_k_norm=swa_k_norm, swa_sinks=swa_sinks, w_out=w_out)
    mom_m = dict(norm_g=m_norm_g, w_in=m_w_in, mla_q_a_norm=m_mla_q_a_norm, mla_w_qb=m_mla_w_qb,
                 mla_kv_a_norm=m_mla_kv_a_norm, mla_w_kvb=m_mla_w_kvb, mla_q_norm=m_mla_q_norm,
                 mla_k_norm=m_mla_k_norm, conv_w=m_conv_w, swa_q_norm=m_swa_q_norm,
                 swa_k_norm=m_swa_k_norm, swa_sinks=m_swa_sinks, w_out=m_w_out)
    mom_v = dict(norm_g=v_norm_g, w_in=v_w_in, mla_q_a_norm=v_mla_q_a_norm, mla_w_qb=v_mla_w_qb,
                 mla_kv_a_norm=v_mla_kv_a_norm, mla_w_kvb=v_mla_w_kvb, mla_q_norm=v_mla_q_norm,
                 mla_k_norm=v_mla_k_norm, conv_w=v_conv_w, swa_q_norm=v_swa_q_norm,
                 swa_k_norm=v_swa_k_norm, swa_sinks=v_swa_sinks, w_out=v_w_out)
    xs = x[0]
    target = loss_target[0]
    s = xs.shape[0]
    c_idx = lax.axis_index("c").astype(jnp.int32).reshape(1)
    k_idx = (2 * lax.axis_index("x") + lax.axis_index("y")).astype(jnp.int32).reshape(1)

    conv_bits = lax.bitcast_convert_type(conv_w, BF16)
    small_list = [w_out, mla_w_qb, mla_w_kvb, conv_bits]
    w_in_rows = DEPTH * D_MODEL
    own = [w_in.astype(BF16).reshape(w_in_rows, w_in.shape[2]), _pack_rows(small_list, BF16, PACK_ROWS)]
    gathered_in, gathered_rest = [_fill_own_slab(buf, src, k_idx)
                                  for buf, src in zip(_all_gather(own), own)]
    parts = _unpack_rows(gathered_rest, [a.shape for a in small_list])
    join = lambda p, axis: jnp.concatenate([p[k] for k in range(N_CHIPS)], axis=axis)
    w_in_slabs = gathered_in.reshape(N_CHIPS, DEPTH, D_MODEL, w_in.shape[2])
    w_in_zeros = jnp.zeros((D_MODEL, 64), BF16)
    w_out_full = join(parts[0], 1)
    w_qb_full = join(parts[1], 2)
    w_kvb_full = join(parts[2], 2)
    conv_full = lax.bitcast_convert_type(join(parts[3], 2), F32)

    rope = _rope_tables(s)
    swa_tables = _swa_tables()
    layers = []
    for l in range(DEPTH):
        wq = jnp.pad(w_qb_full[l].reshape(MLA_Q_LORA, MLA_HEADS, MLA_QK),
                     ((0, 0), (0, 0), (0, LANES - MLA_QK))).reshape(MLA_Q_LORA, MLA_HEADS * LANES)
        kv = w_kvb_full[l].reshape(MLA_KV_LORA, MLA_HEADS, MLA_NOPE + MLA_V)
        wk = jnp.pad(kv[:, :, :MLA_NOPE], ((0, 0), (0, 0), (0, LANES - MLA_NOPE)))
        wkv = jnp.concatenate([wk.reshape(MLA_KV_LORA, MLA_HEADS * LANES),
                               kv[:, :, MLA_NOPE:].reshape(MLA_KV_LORA, MLA_HEADS * MLA_V)], axis=1)
        layers.append(dict(
            w_in=_cols([w_in_slabs[k, l] for k in range(N_CHIPS)] + [w_in_zeros], NEW_ORDER),
            w_out=w_out_full[l], wq=wq, wkv=wkv,
            cw=jnp.pad(conv_full[l], ((0, HALO - 3), (0, 0))),
            g=norm_g[l][None],
            mla_norms=(mla_q_a_norm[l][:, None], mla_kv_a_norm[l][:, None],
                       _pad_lanes(mla_q_norm[l][None], LANES).T, _pad_lanes(mla_k_norm[l][None], LANES).T),
            mla_weights=(wq, wkv, wq.T, wkv.T),
            sqn=swa_q_norm[l][None], skn=swa_k_norm[l][None], sinks=swa_sinks[l][None]))

    saved = []
    h_in = xs
    for l in range(DEPTH):
        p = layers[l]
        proj, hb = _in_proj_fwd(h_in, p["g"], p["w_in"])
        q, k, v, qt, kt, vt = _mla_prep_fwd(proj, p["mla_norms"], p["mla_weights"], rope)
        o_mla, lse = _mla_attn_fwd(q, k, vt)
        o_swa = _swa_fwd(proj, p["sqn"], p["skn"], p["sinks"], swa_tables)
        y, z = _out_fwd(proj, o_mla, o_swa, h_in, p["w_out"], p["cw"])
        saved.append(dict(x=h_in, proj=proj, hb=hb, q=q, k=k, v=v, qt=qt, kt=kt, o_mla=o_mla, lse=lse,
                          o_swa=o_swa, z=z))
        h_in = y

    dy, loss_acc = _loss_head(h_in, target)
    loss = lax.psum(loss_acc[0, 0], ("x", "y", "c"))

    grads = {n: [None] * DEPTH for n in WEIGHT_ORDER}
    for l in reversed(range(DEPTH)):
        p, a = layers[l], saved[l]
        dconv, dgates, do_mla, do_mla_t, delta, do_swa, dcw = _out_bwd(dy, a["proj"], a["o_mla"], a["o_swa"],
                                                             p["w_out"], p["cw"])
        grads["w_out"][l] = _matmul_tn(a["z"], dy, "dw_out")
        grads["conv_w"][l] = dcw[0:3]
        delta_rows = jnp.transpose(delta, (1, 0)).reshape(MLA_HEADS // 2, 2, s)
        dq, dk, dv = _mla_attn_bwd(a["q"], a["qt"], a["k"], a["kt"], a["v"], do_mla, do_mla_t,
                                   a["lse"], delta_rows)
        dmla, dqan, dkvan, dqn, dkn, dwq_t, dwkv_t = _mla_prep_bwd(
            a["proj"], p["mla_norms"], p["mla_weights"], rope, dq, dk, dv)
        dwq, dwkv = dwq_t.T, dwkv_t.T
        dsq, dskv, dsqn, dskn, dsinks = _swa_bwd(a["proj"], p["sqn"], p["skn"], p["sinks"], swa_tables, do_swa)
        pieces = [dconv, dgates, dsq, dmla, dskv]
        dx, dg = _in_proj_bwd(pieces, a["x"], p["g"], p["w_in"], dy)
        grads["w_in"][l] = [_matmul_tn(a["hb"], pc, "dw_in_%d" % n) for n, pc in enumerate(pieces)]
        grads["norm_g"][l] = dg[0]
        grads["mla_q_a_norm"][l] = dqan[:, 0]
        grads["mla_kv_a_norm"][l] = dkvan[:, 0]
        grads["mla_q_norm"][l] = dqn[:MLA_QK, 0]
        grads["mla_k_norm"][l] = dkn[:MLA_QK, 0]
        grads["mla_w_qb"][l] = dwq.reshape(MLA_Q_LORA, MLA_HEADS, LANES)[:, :, :MLA_QK].reshape(
            MLA_Q_LORA, MLA_HEADS * MLA_QK)
        dwk = dwkv[:, :MLA_HEADS * LANES].reshape(MLA_KV_LORA, MLA_HEADS, LANES)[:, :, :MLA_NOPE]
        dwv = dwkv[:, MLA_HEADS * LANES:].reshape(MLA_KV_LORA, MLA_HEADS, MLA_V)
        grads["mla_w_kvb"][l] = jnp.concatenate([dwk, dwv], axis=2).reshape(
            MLA_KV_LORA, MLA_HEADS * (MLA_NOPE + MLA_V))
        grads["swa_q_norm"][l] = dsqn[0]
        grads["swa_k_norm"][l] = dskn[0]
        grads["swa_sinks"][l] = dsinks[0]
        dy = dx
    grad_x = dy[None]
    full_grads = {n: jnp.stack(grads[n]) for n in WEIGHT_ORDER if n != "w_in"}

    rest = tuple(n for n in SHARDED if n != "w_in")
    rep_shapes = [weights[n].shape for n in REPLICATED]
    flat_rep = lambda d: jnp.concatenate([d[n].reshape(-1) for n in REPLICATED])

    def chunk(g, n, k):
        width = g.shape[SHARD_AXIS[n]] // N_CHIPS
        return lax.slice_in_dim(g, k * width, (k + 1) * width, axis=SHARD_AXIS[n])

    shard_cols = w_in.shape[2]
    g_in = jnp.stack([
        jnp.concatenate([_cols(grads["w_in"][l], _sub_ranges(OLD_ORDER, k * shard_cols,
                                                             (k + 1) * shard_cols))
                         for l in range(DEPTH)], axis=0)
        for k in range(N_CHIPS)])
    rep_grads = flat_rep(full_grads)
    g_rest = jnp.stack([_pack_rows([chunk(full_grads[n], n, k) for n in rest] + [rep_grads],
                                   F32, PACK_ROWS) for k in range(N_CHIPS)])
    from_sibling = _swap_halves_to_sibling([g_in, g_rest])
    partial = [_add_sibling(g, r, c_idx, dt)
               for g, r, dt in zip((g_in, g_rest), from_sibling, (BF16, F32))]
    by_chip = [_fill_own_slab(q, p, k_idx) for q, p in zip(_scatter_to_chips(partial), partial)]
    g_in_mine, g_rest_mine = _join_halves([_sum_chips(q, c_idx) for q in by_chip])

    pack_rest = lambda d: _pack_rows([d[n] for n in rest] + [flat_rep(d)], F32, PACK_ROWS)
    in_shape = w_in.shape
    res_in = _adamw(g_in_mine, w_in.reshape(w_in_rows, -1), m_w_in.reshape(w_in_rows, -1),
                    v_w_in.reshape(w_in_rows, -1))
    res_rest = _adamw(g_rest_mine, pack_rest(weights), pack_rest(mom_m), pack_rest(mom_v))
    rest_shapes = [weights[n].shape for n in rest] + [(rep_grads.shape[0],)]
    unpacked = []
    for buf_in, buf_rest in zip((g_in_mine,) + tuple(res_in), (g_rest_mine,) + tuple(res_rest)):
        vals = _unpack_rows(buf_rest, rest_shapes)
        group = dict(zip(rest, vals[:-1]))
        group["w_in"] = buf_in.reshape(in_shape)
        off = 0
        for n, sh in zip(REPLICATED, rep_shapes):
            size = sh[0] * sh[1]
            group[n] = vals[-1][off:off + size].reshape(sh)
            off += size
        unpacked.append(group)
    outs = [loss, grad_x]
    for group in unpacked:
        outs += [group[n] for n in WEIGHT_ORDER]
    return tuple(outs)
```

```python
import jax
import numpy as np
import jax.numpy as jnp
from jax import lax
from jax.experimental import pallas as pl
from jax.experimental.pallas import tpu as pltpu

F32 = jnp.float32
BF16 = jnp.bfloat16

D_MODEL = 1024
DEPTH = 2
GROUP = 512
D_MIX = 3 * GROUP
BLOCK = 128
RMS_EPS = 1e-6
NEG_INF = -1e30
MLA_HEADS = 8
MLA_QK = 96
MLA_NOPE = 64
MLA_ROPE = 32
MLA_V = 64
MLA_Q_LORA = 256
MLA_KV_LORA = 128
ROPE_THETA = 10000.0
SWA_HEADS = 8
SWA_KV = 2
SWA_GROUP = 4
SWA_DIM = 64
IN_COLS = 4256
N_CHIPS = 4

NC = 4352
OFF_CONV, OFF_GATES, OFF_SQ, OFF_MLA, OFF_SKV = 0, 1536, 3072, 3584, 4096
PIECE_WIDTHS = (1536, 1536, 512, 512, 256)

VMEM_LIMIT = 56 * 1024 * 1024
LANES = 128
PACK_COLS = 1024
PACK_ROW_ALIGN = 16
PACK_ROWS = 1024

ADAM_LR = 0.001
ADAM_B1 = 0.9
ADAM_B2 = 0.999
ADAM_EPS = 1e-08
ADAM_WD = 0.01
ADAM_STEP = 10

MESH = pl.DeviceIdType.MESH


def _params(sem, vmem=VMEM_LIMIT):
    return pltpu.CompilerParams(dimension_semantics=sem, vmem_limit_bytes=vmem)


def _dot(a, b, dims):
    return lax.dot_general(a.astype(BF16), b.astype(BF16), (dims, ((), ())),
                           preferred_element_type=F32)


def _mm(a, b):
    return _dot(a, b, ((1,), (0,)))


def _mm_nt(a, b):
    return _dot(a, b, ((1,), (1,)))


def _mm_tn(a, b):
    return _dot(a, b, ((0,), (0,)))


@jax.custom_vjp
def _mmd(a, b):
    return _mm(a, b)


def _mmd_fwd(a, b):
    return _mm(a, b), (a, b)


def _mmd_bwd(res, g):
    a, b = res
    return _mm_nt(g, b), _mm_tn(a, g)


_mmd.defvjp(_mmd_fwd, _mmd_bwd)


def _rms(x, g, n=None):
    n = x.shape[-1] if n is None else n
    ms = jnp.sum(x * x, axis=-1, keepdims=True) * (1.0 / n)
    return x * lax.rsqrt(ms + RMS_EPS) * g


def _sigmoid(x):
    return 1.0 / (1.0 + jnp.exp(-x))


@jax.custom_vjp
def _rope(t, c, s1, s2):
    return t * c + pltpu.roll(t, LANES - 16, 1) * s1 + pltpu.roll(t, 16, 1) * s2


def _rope_fwd(t, c, s1, s2):
    return _rope(t, c, s1, s2), (c, s1, s2)


def _rope_bwd(res, g):
    c, s1, s2 = res
    dt = g * c + pltpu.roll(g * s1, 16, 1) + pltpu.roll(g * s2, LANES - 16, 1)
    return dt, jnp.zeros_like(c), jnp.zeros_like(s1), jnp.zeros_like(s2)


_rope.defvjp(_rope_fwd, _rope_bwd)


def _in_proj_fwd(x, g, w):
    s = x.shape[0]
    tm = min(512, s)

    def body(x_ref, g_ref, w_ref, proj_ref, hb_ref):
        hb = _rms(x_ref[...], g_ref[...]).astype(BF16)
        hb_ref[...] = hb
        proj_ref[...] = jnp.dot(hb, w_ref[...], preferred_element_type=F32)

    return pl.pallas_call(
        body, name="in_proj_fwd", grid=(s // tm,),
        in_specs=[pl.BlockSpec((tm, D_MODEL), lambda i: (i, 0)),
                  pl.BlockSpec((1, D_MODEL), lambda i: (0, 0)),
                  pl.BlockSpec((D_MODEL, NC), lambda i: (0, 0))],
        out_specs=[pl.BlockSpec((tm, NC), lambda i: (i, 0)),
                   pl.BlockSpec((tm, D_MODEL), lambda i: (i, 0))],
        out_shape=[jax.ShapeDtypeStruct((s, NC), F32), jax.ShapeDtypeStruct((s, D_MODEL), BF16)],
        compiler_params=_params(("parallel",)),
    )(x, g, w)


def _in_proj_bwd(pieces, x, g, w, dres):
    s = x.shape[0]
    tm = min(512, s)
    n_p = len(pieces)

    def body(*refs):
        p_refs = refs[:n_p]
        x_ref, g_ref, w_ref, dres_ref, dx_ref, dg_ref = refs[n_p:]
        dh = None
        off = 0
        for r in p_refs:
            width = r.shape[1]
            t = _mm_nt(r[...], w_ref[:, off:off + width])
            dh = t if dh is None else dh + t
            off += width
        _, vjp = jax.vjp(_rms, x_ref[...], g_ref[...])
        dx, dg = vjp(dh)
        dx_ref[...] = dx + dres_ref[...]

        @pl.when(pl.program_id(0) == 0)
        def _():
            dg_ref[...] = jnp.zeros_like(dg_ref)

        dg_ref[...] += dg

    in_specs = [pl.BlockSpec((tm, p.shape[1]), lambda i: (i, 0)) for p in pieces]
    in_specs += [pl.BlockSpec((tm, D_MODEL), lambda i: (i, 0)),
                 pl.BlockSpec((1, D_MODEL), lambda i: (0, 0)),
                 pl.BlockSpec((D_MODEL, NC), lambda i: (0, 0)),
                 pl.BlockSpec((tm, D_MODEL), lambda i: (i, 0))]
    return pl.pallas_call(
        body, name="in_proj_bwd", grid=(s // tm,),
        in_specs=in_specs,
        out_specs=[pl.BlockSpec((tm, D_MODEL), lambda i: (i, 0)),
                   pl.BlockSpec((1, D_MODEL), lambda i: (0, 0))],
        out_shape=[jax.ShapeDtypeStruct((s, D_MODEL), F32), jax.ShapeDtypeStruct((1, D_MODEL), F32)],
        compiler_params=_params(("arbitrary",)),
    )(*pieces, x, g, w, dres)


def _matmul_tn(a, b, name):
    s, m = a.shape
    n = b.shape[1]
    tk = min(1024, s)
    tn = min(1536, n)

    def body(a_ref, b_ref, o_ref):
        @pl.when(pl.program_id(1) == 0)
        def _():
            o_ref[...] = jnp.zeros_like(o_ref)

        o_ref[...] += _mm_tn(a_ref[...], b_ref[...])

    return pl.pallas_call(
        body, name=name, grid=(n // tn, s // tk),
        in_specs=[pl.BlockSpec((tk, m), lambda j, k: (k, 0)),
                  pl.BlockSpec((tk, tn), lambda j, k: (k, j))],
        out_specs=pl.BlockSpec((m, tn), lambda j, k: (0, j)),
        out_shape=jax.ShapeDtypeStruct((m, n), F32),
        compiler_params=_params(("parallel", "arbitrary")),
    )(a, b)


def _rms0(x, g, n=None):
    n = x.shape[0] if n is None else n
    ms = jnp.sum(x * x, axis=0, keepdims=True) * (1.0 / n)
    return x * lax.rsqrt(ms + RMS_EPS) * g


@jax.custom_vjp
def _rope0(t, c, s1, s2):
    return t * c + pltpu.roll(t, LANES - 16, 0) * s1 + pltpu.roll(t, 16, 0) * s2


def _rope0_fwd(t, c, s1, s2):
    return _rope0(t, c, s1, s2), (c, s1, s2)


def _rope0_bwd(res, g):
    c, s1, s2 = res
    dt = g * c + pltpu.roll(g * s1, 16, 0) + pltpu.roll(g * s2, LANES - 16, 0)
    return dt, jnp.zeros_like(c), jnp.zeros_like(s1), jnp.zeros_like(s2)


_rope0.defvjp(_rope0_fwd, _rope0_bwd)


@jax.custom_vjp
def _mmw(w, wt, x):
    return _mm(w, x)


def _mmw_fwd(w, wt, x):
    return _mm(w, x), (wt, x)


def _mmw_bwd(res, g):
    wt, x = res
    return _mm_nt(g, x), jnp.zeros_like(wt), _mm(wt, g)


_mmw.defvjp(_mmw_fwd, _mmw_bwd)


def _prep_fn(q_lat, kv_lat, kr, qan, kvan, qn, kn, wq, wk, wv, wqt, wkt, wvt, c, s1, s2, mm):
    tokens = q_lat.shape[1]
    rq = _rms0(q_lat, qan)
    rkv = _rms0(kv_lat, kvan)
    qn_b = jnp.broadcast_to(qn, (LANES, tokens))
    kn_b = jnp.broadcast_to(kn, (LANES, tokens))
    qs, ks = [], []
    for h in range(MLA_HEADS):
        qs.append(_rope0(_rms0(mm(wq[h], wqt[h], rq), qn_b, MLA_QK), c, s1, s2))
        ks.append(_rope0(_rms0(mm(wk[h], wkt[h], rkv) + kr, kn_b, MLA_QK), c, s1, s2))
    return tuple(qs), tuple(ks), mm(wv, wvt, rkv)


def _prep_weights(wq_ref, wkv_ref, wqt_ref, wkvt_ref):
    heads = range(MLA_HEADS)
    wq = tuple(wqt_ref[LANES * h:LANES * (h + 1), :].astype(F32) for h in heads)
    wk = tuple(wkvt_ref[LANES * h:LANES * (h + 1), :].astype(F32) for h in heads)
    wv = wkvt_ref[LANES * MLA_HEADS:, :].astype(F32)
    wqt = tuple(wq_ref[:, LANES * h:LANES * (h + 1)].astype(F32) for h in heads)
    wkt = tuple(wkv_ref[:, LANES * h:LANES * (h + 1)].astype(F32) for h in heads)
    wvt = wkv_ref[:, LANES * MLA_HEADS:].astype(F32)
    return wq, wk, wv, wqt, wkt, wvt


def _prep_in_specs(tm):
    const = lambda shape: pl.BlockSpec(shape, lambda i: (0, 0))
    col = lambda height: pl.BlockSpec((height, tm), lambda i: (0, i))
    return [pl.BlockSpec((tm, 512), lambda i: (i, OFF_MLA // 512)),
            const((MLA_Q_LORA, 1)), const((MLA_KV_LORA, 1)), const((LANES, 1)), const((LANES, 1)),
            const((MLA_Q_LORA, 1024)), const((MLA_KV_LORA, 1536)),
            const((1024, MLA_Q_LORA)), const((1536, MLA_KV_LORA)),
            col(LANES), col(LANES), col(LANES)]


def _prep_operands(blk_ref, refs):
    qan_ref, kvan_ref, qn_ref, kn_ref, wq_ref, wkv_ref, wqt_ref, wkvt_ref, c_ref, s1_ref, s2_ref = refs
    blk_t = jnp.transpose(blk_ref[...])
    diff = (blk_t[0:256], blk_t[256:384], blk_t[384:512],
            qan_ref[...], kvan_ref[...], qn_ref[...], kn_ref[...])
    weights = _prep_weights(wq_ref, wkv_ref, wqt_ref, wkvt_ref)
    return diff, weights, (c_ref[...], s1_ref[...], s2_ref[...])


def _mla_prep_fwd(proj, norms, weights, rope):
    s = proj.shape[0]
    tm = min(512, s)

    def body(blk_ref, *refs):
        ins, (q_ref, k_ref, v_ref, qt_ref, kt_ref, vt_ref) = refs[:11], refs[11:]
        diff, (wq, wk, wv, wqt, wkt, wvt), tables = _prep_operands(blk_ref, ins)
        qs, ks, v = _prep_fn(*diff, wq, wk, wv, wqt, wkt, wvt, *tables,
                             lambda w, wt, x: _mm(w, x))
        for h in range(MLA_HEADS):
            q2 = qs[h] * Q_PRESCALE
            qt_ref[LANES * h:LANES * (h + 1), :] = q2.astype(BF16)
            kt_ref[LANES * h:LANES * (h + 1), :] = ks[h].astype(BF16)
            q_ref[:, LANES * h:LANES * (h + 1)] = jnp.transpose(q2).astype(BF16)
            k_ref[:, LANES * h:LANES * (h + 1)] = jnp.transpose(ks[h]).astype(BF16)
        vt_ref[...] = v.astype(BF16)
        v_ref[...] = jnp.transpose(v).astype(BF16)

    row = lambda width: pl.BlockSpec((tm, width), lambda i: (i, 0))
    col = lambda height: pl.BlockSpec((height, tm), lambda i: (0, i))
    return pl.pallas_call(
        body, name="mla_prep_fwd", grid=(s // tm,),
        in_specs=_prep_in_specs(tm),
        out_specs=[row(1024), row(1024), row(512), col(1024), col(1024), col(512)],
        out_shape=[jax.ShapeDtypeStruct((s, 1024), BF16), jax.ShapeDtypeStruct((s, 1024), BF16),
                   jax.ShapeDtypeStruct((s, 512), BF16), jax.ShapeDtypeStruct((1024, s), BF16),
                   jax.ShapeDtypeStruct((1024, s), BF16), jax.ShapeDtypeStruct((512, s), BF16)],
        compiler_params=_params(("parallel",)),
    )(proj, *norms, *weights, *rope)


def _mla_prep_bwd(proj, norms, weights, rope, dq, dk, dv):
    s = proj.shape[0]
    tm = min(256, s)

    def body(blk_ref, *refs):
        ins, (dq_ref, dk_ref, dv_ref) = refs[:11], refs[11:14]
        dblk_ref, dqan_ref, dkvan_ref, dqn_ref, dkn_ref, dwq_ref, dwkv_ref = refs[14:]
        diff, (wq, wk, wv, wqt, wkt, wvt), tables = _prep_operands(blk_ref, ins)

        def fn(q_lat, kv_lat, kr, qan, kvan, qn, kn, wq_, wk_, wv_):
            return _prep_fn(q_lat, kv_lat, kr, qan, kvan, qn, kn, wq_, wk_, wv_, wqt, wkt, wvt,
                            *tables, _mmw)

        _, vjp = jax.vjp(fn, *diff, wq, wk, wv)
        heads = range(MLA_HEADS)
        cts = (tuple(dq_ref[LANES * h:LANES * (h + 1), :] for h in heads),
               tuple(dk_ref[LANES * h:LANES * (h + 1), :] for h in heads), dv_ref[...])
        dq_lat, dkv_lat, dkr, dqan, dkvan, dqn, dkn, dwq_h, dwk_h, dwv = vjp(cts)
        dblk_ref[...] = jnp.transpose(
            jnp.concatenate([dq_lat, dkv_lat, dkr], axis=0)).astype(BF16)

        @pl.when(pl.program_id(0) == 0)
        def _():
            for r in (dqan_ref, dkvan_ref, dqn_ref, dkn_ref, dwq_ref, dwkv_ref):
                r[...] = jnp.zeros_like(r)

        dqan_ref[...] += dqan
        dkvan_ref[...] += dkvan
        dqn_ref[...] += dqn
        dkn_ref[...] += dkn
        for h in heads:
            dwq_ref[LANES * h:LANES * (h + 1), :] += dwq_h[h]
            dwkv_ref[LANES * h:LANES * (h + 1), :] += dwk_h[h]
        dwkv_ref[LANES * MLA_HEADS:, :] += dwv

    const = lambda shape: pl.BlockSpec(shape, lambda i: (0, 0))
    col = lambda height: pl.BlockSpec((height, tm), lambda i: (0, i))
    shapes = [(MLA_Q_LORA, 1), (MLA_KV_LORA, 1), (LANES, 1), (LANES, 1),
              (1024, MLA_Q_LORA), (1536, MLA_KV_LORA)]
    return pl.pallas_call(
        body, name="mla_prep_bwd", grid=(s // tm,),
        in_specs=_prep_in_specs(tm) + [col(1024), col(1024), col(512)],
        out_specs=[pl.BlockSpec((tm, 512), lambda i: (i, 0))] + [const(sh) for sh in shapes],
        out_shape=[jax.ShapeDtypeStruct((s, 512), BF16)]
        + [jax.ShapeDtypeStruct(sh, F32) for sh in shapes],
        compiler_params=_params(("arbitrary",)),
    )(proj, *norms, *weights, *rope, dq, dk, dv)


MLA_SCALE = MLA_QK ** -0.5


LOG2E = 1.4426950408889634
LN2 = 0.6931471805599453
Q_PRESCALE = MLA_SCALE * LOG2E


def _mla_attn_fwd(q2, k, vt):
    s = q2.shape[0]
    t = min(512, s)
    tk = min(512, s)
    nq = s // t
    r = t // tk

    def body(q_ref, k_ref, vt_ref, o_ref, lse_ref, acc_ref):
        i = pl.program_id(1)
        row = lax.broadcasted_iota(jnp.int32, (tk, t), 0)
        col = lax.broadcasted_iota(jnp.int32, (tk, t), 1)
        qh = [q_ref[:, LANES * hh:LANES * (hh + 1)] for hh in range(2)]
        acc_ref[...] = jnp.zeros_like(acc_ref)

        def scores(j, diag=None):
            r0 = pl.multiple_of(j * tk, tk)
            out = []
            for hh in range(2):
                kc = k_ref[pl.ds(r0, tk), LANES * hh:LANES * (hh + 1)]
                sc = lax.dot_general(kc, qh[hh], (((1,), (1,)), ((), ())),
                                     preferred_element_type=F32)
                out.append(sc if diag is None else jnp.where(row + diag * tk <= col, sc, NEG_INF))
            return tuple(out)

        def consume(j, scs, stats):
            r0 = pl.multiple_of(j * tk, tk)
            out, ps, alphas = [], [], []
            for hh in range(2):
                m, l = stats[hh]
                m_new = jnp.maximum(m, jnp.max(scs[hh], axis=0, keepdims=True))
                p = jnp.exp2(scs[hh] - m_new)
                alpha = jnp.exp2(m - m_new)
                out.append((m_new, alpha * l + jnp.sum(p, axis=0, keepdims=True)))
                ps.append(p.astype(BF16))
                alphas.append(alpha)
            for hh in range(2):
                vc = vt_ref[MLA_V * hh:MLA_V * (hh + 1), pl.ds(r0, tk)]
                acc_ref[hh] = alphas[hh] * acc_ref[hh] + jnp.dot(vc, ps[hh],
                                                                 preferred_element_type=F32)
            return tuple(out)

        stats = tuple((jnp.full((1, t), NEG_INF, F32), jnp.zeros((1, t), F32)) for _ in range(2))
        scs = scores(r * i, 0)
        for d in range(1, r):
            nxt = scores(r * i + d, d)
            stats = consume(r * i + d - 1, scs, stats)
            scs = nxt
        last_diag = r * i + r - 1

        def trip(j, carry):
            scs, stats = carry
            nxt = scores(j)
            stats = consume(jnp.where(j == 0, last_diag, j - 1), scs, stats)
            return nxt, stats

        scs, stats = lax.fori_loop(0, r * i, trip, (scs, stats))
        stats = consume(jnp.where(i == 0, last_diag, r * i - 1), scs, stats)
        for hh in range(2):
            m, l = stats[hh]
            o_ref[:, MLA_V * hh:MLA_V * (hh + 1)] = jnp.transpose(acc_ref[hh] / l)
            lse_ref[0, hh:hh + 1, :] = m + jnp.log2(l)

    return pl.pallas_call(
        body, name="mla_attn_fwd", grid=(MLA_HEADS // 2, nq),
        in_specs=[pl.BlockSpec((t, 256), lambda p, i: (i, p)),
                  pl.BlockSpec((s, 256), lambda p, i: (0, p)),
                  pl.BlockSpec((128, s), lambda p, i: (p, 0))],
        out_specs=[pl.BlockSpec((t, 128), lambda p, i: (i, p)),
                   pl.BlockSpec((1, 2, t), lambda p, i: (p, 0, i))],
        out_shape=[jax.ShapeDtypeStruct((s, 512), F32),
                   jax.ShapeDtypeStruct((MLA_HEADS // 2, 2, s), F32)],
        scratch_shapes=[pltpu.VMEM((2, MLA_V, t), F32)],
        compiler_params=_params(("parallel", "arbitrary")),
    )(q2, k, vt)


def _mla_attn_bwd(q2, q2t, k, kt, v, do, dot, lse_rows, delta_rows):
    s = q2.shape[0]
    t = min(512, s)
    nq = s // t

    def body(q_ref, qt_ref, k_ref, kt_ref, v_ref, do_ref, dot_ref, lse_ref, dl_ref,
             dq_ref, dk_ref, dv_ref):
        j = pl.program_id(1)

        @pl.when(j == 0)
        def _():
            dq_ref[...] = jnp.zeros_like(dq_ref)

        dk_ref[...] = jnp.zeros_like(dk_ref)
        dv_ref[...] = jnp.zeros_like(dv_ref)
        row = lax.broadcasted_iota(jnp.int32, (t, t), 0)
        col = lax.broadcasted_iota(jnp.int32, (t, t), 1)
        causal_t = row <= col
        kh = [k_ref[:, LANES * hh:LANES * (hh + 1)] for hh in range(2)]
        kth = [kt_ref[LANES * hh:LANES * (hh + 1), :] for hh in range(2)]
        vh = [v_ref[:, MLA_V * hh:MLA_V * (hh + 1)] for hh in range(2)]
        nt = (((1,), (1,)), ((), ()))

        def step(i, masked):
            r0 = pl.multiple_of(i * t, t)
            sd = []
            for hh in range(2):
                qh = q_ref[pl.ds(r0, t), LANES * hh:LANES * (hh + 1)]
                doh = do_ref[pl.ds(r0, t), MLA_V * hh:MLA_V * (hh + 1)]
                sc_t = lax.dot_general(kh[hh], qh, nt, preferred_element_type=F32)
                sd.append(jnp.where(causal_t, sc_t, NEG_INF) if masked else sc_t)
                sd.append(lax.dot_general(vh[hh], doh, nt, preferred_element_type=F32))
            pts, gts = [], []
            for hh in range(2):
                lse = lse_ref[0, hh:hh + 1, pl.ds(r0, t)]
                dl = dl_ref[0, hh:hh + 1, pl.ds(r0, t)]
                p_t = jnp.exp2(sd[2 * hh] - lse)
                pts.append(p_t.astype(BF16))
                gts.append((p_t * (sd[2 * hh + 1] - dl)).astype(BF16))
            for hh in range(2):
                qth = qt_ref[LANES * hh:LANES * (hh + 1), pl.ds(r0, t)]
                doth = dot_ref[MLA_V * hh:MLA_V * (hh + 1), pl.ds(r0, t)]
                dv_ref[MLA_V * hh:MLA_V * (hh + 1), :] += lax.dot_general(
                    doth, pts[hh], nt, preferred_element_type=F32)
                dk_ref[LANES * hh:LANES * (hh + 1), :] += lax.dot_general(
                    qth, gts[hh], nt, preferred_element_type=F32)
                dq_ref[LANES * hh:LANES * (hh + 1), pl.ds(r0, t)] += jnp.dot(
                    kth[hh], gts[hh], preferred_element_type=F32)

        step(j, True)

        def trip(i, carry):
            step(i, False)
            return carry

        lax.fori_loop(j + 1, nq, trip, 0)
        dk_ref[...] = dk_ref[...] * LN2

        @pl.when(j == nq - 1)
        def _():
            dq_ref[...] = dq_ref[...] * MLA_SCALE

    return pl.pallas_call(
        body, name="mla_attn_bwd", grid=(MLA_HEADS // 2, nq),
        in_specs=[pl.BlockSpec((s, 256), lambda p, j: (0, p)),
                  pl.BlockSpec((256, s), lambda p, j: (p, 0)),
                  pl.BlockSpec((t, 256), lambda p, j: (j, p)),
                  pl.BlockSpec((256, t), lambda p, j: (p, j)),
                  pl.BlockSpec((t, 128), lambda p, j: (j, p)),
                  pl.BlockSpec((s, 128), lambda p, j: (0, p)),
                  pl.BlockSpec((128, s), lambda p, j: (p, 0)),
                  pl.BlockSpec((1, 2, s), lambda p, j: (p, 0, 0)),
                  pl.BlockSpec((1, 2, s), lambda p, j: (p, 0, 0))],
        out_specs=[pl.BlockSpec((256, s), lambda p, j: (p, 0)),
                   pl.BlockSpec((256, t), lambda p, j: (p, j)),
                   pl.BlockSpec((128, t), lambda p, j: (p, j))],
        out_shape=[jax.ShapeDtypeStruct((1024, s), F32), jax.ShapeDtypeStruct((1024, s), F32),
                   jax.ShapeDtypeStruct((512, s), F32)],
        compiler_params=_params(("parallel", "arbitrary")),
    )(q2, q2t, k, kt, v, do, dot, lse_rows, delta_rows)


SWA_SCALE = SWA_DIM ** -0.5
SWA_COLS = SWA_GROUP * BLOCK
SWA_LOG2 = SWA_SCALE * LOG2E


def _swa_tables():
    k = np.arange(2 * BLOCK)[:, None]
    col = np.arange(SWA_COLS)[None, :]
    dist = BLOCK + (col % BLOCK) - k
    valid = (dist >= 0) & (dist < BLOCK)
    out = np.zeros((2, SWA_KV, 2 * BLOCK, SWA_COLS), np.float32)
    for first in range(2):
        ok = valid & ((k >= BLOCK) | (first == 0))
        for j in range(SWA_KV):
            slope = 2.0 ** -(SWA_GROUP * j + col // BLOCK + 1)
            out[first, j] = np.where(ok, -slope * dist * LOG2E, NEG_INF)
    return jnp.asarray(out)


def _swa_tile_inputs(sq_ref, skv_ref, halo_ref, qn_ref, kn_ref, sk_ref, add_ref, first):
    tokens = sq_ref.shape[0]
    kv_all = jnp.concatenate([halo_ref[...], skv_ref[...]], axis=0)
    kv_t = jnp.transpose(kv_all)
    sq_t = jnp.transpose(sq_ref[...])
    k_raw = [kv_t[SWA_DIM * j:SWA_DIM * (j + 1)] for j in range(SWA_KV)]
    v_t = [kv_t[128 + SWA_DIM * j:128 + SWA_DIM * (j + 1)] for j in range(SWA_KV)]
    v_nat = [kv_all[:, 128 + SWA_DIM * j:128 + SWA_DIM * (j + 1)] for j in range(SWA_KV)]
    q_raw = [sq_t[SWA_DIM * h:SWA_DIM * (h + 1)] for h in range(SWA_HEADS)]
    qn_b = jnp.broadcast_to(qn_ref[...], (SWA_DIM, tokens))
    kn_b = jnp.broadcast_to(kn_ref[...], (SWA_DIM, tokens + BLOCK))
    lane_grp = lax.broadcasted_iota(jnp.int32, (1, SWA_COLS), 1) // BLOCK
    sinks, adds = [], []
    for j in range(SWA_KV):
        row = jnp.zeros((1, SWA_COLS), F32)
        for g in range(SWA_GROUP):
            h = SWA_GROUP * j + g
            row = jnp.where(lane_grp == g, sk_ref[:, h:h + 1] * LOG2E, row)
        sinks.append(row)
        adds.append((jnp.where(first, add_ref[1, j], add_ref[0, j]), add_ref[0, j]))
    return k_raw, v_t, v_nat, q_raw, qn_b, kn_b, sinks, adds


def _swa_probs(kb, qs_t, add, sink):
    s2 = jnp.dot(kb, qs_t, preferred_element_type=F32) * SWA_LOG2 + add
    m = jnp.maximum(jnp.max(s2, axis=0, keepdims=True), sink)
    e = jnp.exp2(s2 - m)
    es = jnp.exp2(sink - m)
    inv = 1.0 / (jnp.sum(e, axis=0, keepdims=True) + es)
    return e, inv, es


def _swa_queries(qn_t, j, b):
    return jnp.concatenate([qn_t[SWA_GROUP * j + g][:, BLOCK * b:BLOCK * (b + 1)]
                            for g in range(SWA_GROUP)], axis=1)


def _swa_fwd(proj, qn, kn, sinks, tables):
    s = proj.shape[0]
    ts = min(512, s)
    nb = ts // BLOCK

    def body(sq_ref, skv_ref, halo_ref, qn_ref, kn_ref, sk_ref, add_ref, o_ref, ot_ref):
        first = pl.program_id(0) == 0
        k_raw, v_t, _, q_raw, qn_b, kn_b, sink_rows, adds = _swa_tile_inputs(
            sq_ref, skv_ref, halo_ref, qn_ref, kn_ref, sk_ref, add_ref, first)
        kn_nat = [jnp.transpose(_rms0(k, kn_b)).astype(BF16) for k in k_raw]
        v_t = [v.astype(BF16) for v in v_t]
        qn_t = [_rms0(q, qn_b).astype(BF16) for q in q_raw]
        for b in range(nb):
            band = slice(BLOCK * b, BLOCK * (b + 2))
            for j in range(SWA_KV):
                e, inv, _ = _swa_probs(kn_nat[j][band], _swa_queries(qn_t, j, b),
                                       adds[j][0 if b == 0 else 1], sink_rows[j])
                o_t = jnp.dot(v_t[j][:, band], (e * inv).astype(BF16),
                              preferred_element_type=F32)
                for g in range(SWA_GROUP):
                    h = SWA_GROUP * j + g
                    ot_ref[SWA_DIM * h:SWA_DIM * (h + 1), BLOCK * b:BLOCK * (b + 1)] = (
                        o_t[:, BLOCK * g:BLOCK * (g + 1)])
        o_ref[...] = jnp.transpose(ot_ref[...])

    const = lambda shape: pl.BlockSpec(shape, lambda i: (0,) * len(shape))
    return pl.pallas_call(
        body, name="swa_fwd", grid=(s // ts,),
        in_specs=[pl.BlockSpec((ts, 512), lambda i: (i, OFF_SQ // 512)),
                  pl.BlockSpec((ts, 256), lambda i: (i, OFF_SKV // 256)),
                  pl.BlockSpec((BLOCK, 256), lambda i: (jnp.maximum(i * nb - 1, 0), OFF_SKV // 256)),
                  const((SWA_DIM, 1)), const((SWA_DIM, 1)), const((1, SWA_HEADS)),
                  const(tables.shape)],
        out_specs=pl.BlockSpec((ts, 512), lambda i: (i, 0)),
        out_shape=jax.ShapeDtypeStruct((s, 512), F32),
        scratch_shapes=[pltpu.VMEM((512, ts), F32)],
        compiler_params=_params(("parallel",)),
    )(proj, proj, proj, qn, kn, sinks, tables)


def _swa_bwd(proj, qn, kn, sinks, tables, do):
    s = proj.shape[0]
    ts = min(512, s)
    nb = ts // BLOCK
    nt = s // ts

    def body(sq_ref, skv_ref, halo_ref, qn_ref, kn_ref, sk_ref, add_ref, do_ref,
             dsq_ref, dskv_ref, dqn_ref, dkn_ref, dsk_ref, carry_ref, dqt_ref, dkvt_ref):
        step = pl.program_id(0)
        first = step == nt - 1

        @pl.when(step == 0)
        def _():
            carry_ref[...] = jnp.zeros_like(carry_ref)
            dqn_ref[...] = jnp.zeros_like(dqn_ref)
            dkn_ref[...] = jnp.zeros_like(dkn_ref)
            dsk_ref[...] = jnp.zeros_like(dsk_ref)

        k_raw, v_t, v_nat, q_raw, qn_b, kn_b, sink_rows, adds = _swa_tile_inputs(
            sq_ref, skv_ref, halo_ref, qn_ref, kn_ref, sk_ref, add_ref, first)
        kn_f = [_rms0(k, kn_b) for k in k_raw]
        kn_t = [k.astype(BF16) for k in kn_f]
        kn_nat = [jnp.transpose(k).astype(BF16) for k in kn_f]
        v_nat = [v.astype(BF16) for v in v_nat]
        qn_t = [_rms0(q, qn_b).astype(BF16) for q in q_raw]
        do_t = jnp.transpose(do_ref[...].astype(F32)).astype(BF16)

        dkvt_ref[...] = jnp.zeros_like(dkvt_ref)
        dsink = [jnp.zeros((1, SWA_COLS), F32) for _ in range(SWA_KV)]
        nt_dims = (((1,), (1,)), ((), ()))
        for b in range(nb):
            rows = slice(BLOCK * b, BLOCK * (b + 1))
            band = slice(BLOCK * b, BLOCK * (b + 2))
            for j in range(SWA_KV):
                heads = [SWA_GROUP * j + g for g in range(SWA_GROUP)]
                qs_t = _swa_queries(qn_t, j, b)
                dos_t = jnp.concatenate([do_t[SWA_DIM * h:SWA_DIM * (h + 1), rows] for h in heads],
                                        axis=1)
                e, inv, es = _swa_probs(kn_nat[j][band], qs_t, adds[j][0 if b == 0 else 1],
                                        sink_rows[j])
                p = e * inv
                dp = jnp.dot(v_nat[j][band], dos_t, preferred_element_type=F32)
                dsum = jnp.sum(p * dp, axis=0, keepdims=True)
                dsink[j] = dsink[j] - es * inv * dsum
                g_t = (p * (dp - dsum) * SWA_SCALE).astype(BF16)
                dv_t = lax.dot_general(dos_t, p.astype(BF16), nt_dims,
                                       preferred_element_type=F32)
                dk_t = lax.dot_general(qs_t, g_t, nt_dims, preferred_element_type=F32)
                dq_t = jnp.dot(kn_t[j][:, band], g_t, preferred_element_type=F32)
                dkvt_ref[SWA_DIM * j:SWA_DIM * (j + 1), band] += dk_t
                dkvt_ref[128 + SWA_DIM * j:128 + SWA_DIM * (j + 1), band] += dv_t
                for g, h in enumerate(heads):
                    dqt_ref[SWA_DIM * h:SWA_DIM * (h + 1), rows] = dq_t[:, BLOCK * g:BLOCK * (g + 1)]

        dqn = jnp.zeros((SWA_DIM, 1), F32)
        for h in range(SWA_HEADS):
            _, vjp = jax.vjp(_rms0, q_raw[h], qn_ref[...])
            dq, dg = vjp(dqt_ref[SWA_DIM * h:SWA_DIM * (h + 1), :])
            dqt_ref[SWA_DIM * h:SWA_DIM * (h + 1), :] = dq
            dqn = dqn + dg
        dqn_ref[...] += dqn
        dsq_ref[...] = jnp.transpose(dqt_ref[...]).astype(BF16)
        dkn = jnp.zeros((SWA_DIM, 1), F32)
        lane_grp = lax.broadcasted_iota(jnp.int32, (1, SWA_COLS), 1) // BLOCK
        for j in range(SWA_KV):
            _, vjp = jax.vjp(_rms0, k_raw[j], kn_ref[...])
            dk, dg = vjp(dkvt_ref[SWA_DIM * j:SWA_DIM * (j + 1), :])
            dkvt_ref[SWA_DIM * j:SWA_DIM * (j + 1), :] = dk
            dkn = dkn + dg
            for g in range(SWA_GROUP):
                h = SWA_GROUP * j + g
                dsk_ref[:, h:h + 1] += jnp.sum(jnp.where(lane_grp == g, dsink[j], 0.0), axis=1,
                                               keepdims=True)
        dkn_ref[...] += dkn
        dkv = jnp.transpose(dkvt_ref[...])
        dskv_ref[0:ts - BLOCK, :] = dkv[BLOCK:ts].astype(BF16)
        dskv_ref[ts - BLOCK:ts, :] = (dkv[ts:ts + BLOCK] + carry_ref[...]).astype(BF16)
        carry_ref[...] = dkv[0:BLOCK]

    const = lambda shape: pl.BlockSpec(shape, lambda st: (0,) * len(shape))
    return pl.pallas_call(
        body, name="swa_bwd", grid=(nt,),
        in_specs=[pl.BlockSpec((ts, 512), lambda st: (nt - 1 - st, OFF_SQ // 512)),
                  pl.BlockSpec((ts, 256), lambda st: (nt - 1 - st, OFF_SKV // 256)),
                  pl.BlockSpec((BLOCK, 256),
                               lambda st: (jnp.maximum((nt - 1 - st) * nb - 1, 0), OFF_SKV // 256)),
                  const((SWA_DIM, 1)), const((SWA_DIM, 1)), const((1, SWA_HEADS)),
                  const(tables.shape),
                  pl.BlockSpec((ts, 512), lambda st: (nt - 1 - st, 0))],
        out_specs=[pl.BlockSpec((ts, 512), lambda st: (nt - 1 - st, 0)),
                   pl.BlockSpec((ts, 256), lambda st: (nt - 1 - st, 0)),
                   const((SWA_DIM, 1)), const((SWA_DIM, 1)), const((1, SWA_HEADS))],
        out_shape=[jax.ShapeDtypeStruct((s, 512), BF16), jax.ShapeDtypeStruct((s, 256), BF16),
                   jax.ShapeDtypeStruct((SWA_DIM, 1), F32), jax.ShapeDtypeStruct((SWA_DIM, 1), F32),
                   jax.ShapeDtypeStruct((1, SWA_HEADS), F32)],
        scratch_shapes=[pltpu.VMEM((BLOCK, 256), F32), pltpu.VMEM((512, ts), F32),
                        pltpu.VMEM((256, ts + BLOCK), F32)],
        compiler_params=_params(("arbitrary",)),
    )(proj, proj, proj, qn, kn, sinks, tables, do)


HALO = 8


def _shift_down(u, halo, k):
    tm = u.shape[0]
    rid = lax.broadcasted_iota(jnp.int32, u.shape, 0)
    out = pltpu.roll(u, k, 0)
    for r in range(k):
        out = jnp.where(rid == r, halo[HALO - k + r:HALO - k + r + 1, :], out)
    return out


def _shift_up(u, halo, k):
    tm = u.shape[0]
    rid = lax.broadcasted_iota(jnp.int32, u.shape, 0)
    out = pltpu.roll(u, tm - k, 0)
    for r in range(k):
        out = jnp.where(rid == tm - k + r, halo[r:r + 1, :], out)
    return out


def _conv_fwd_vals(conv_ref, convp_ref, cw_ref, is_first):
    c_h, c_b, c_c = conv_ref[:, 0:512], conv_ref[:, 512:1024], conv_ref[:, 1024:1536]
    u = c_c * c_h
    up = jnp.where(is_first, 0.0, convp_ref[:, 1024:1536] * convp_ref[:, 0:512])
    u1 = _shift_down(u, up, 1)
    u2 = _shift_down(u, up, 2)
    yc = cw_ref[0:1, :] * u2 + cw_ref[1:2, :] * u1 + cw_ref[2:3, :] * u
    return c_h, c_b, c_c, u, u1, u2, yc


def _out_fwd(proj, o_mla, o_swa, x, w_out, cw):
    s = proj.shape[0]
    tm = min(512, s)

    def body(conv_ref, convp_ref, gates_ref, om_ref, os_ref, x_ref, w_ref, cw_ref, y_ref, z_ref):
        i = pl.program_id(0)
        _, c_b, _, _, _, _, yc = _conv_fwd_vals(conv_ref, convp_ref, cw_ref, i == 0)
        mix = (om_ref[...], c_b * yc, os_ref[...])
        for n in range(3):
            g = gates_ref[:, GROUP * n:GROUP * (n + 1)]
            z_ref[:, GROUP * n:GROUP * (n + 1)] = (mix[n] * (g * _sigmoid(g))).astype(BF16)
        y_ref[...] = x_ref[...] + jnp.dot(z_ref[...], w_ref[...], preferred_element_type=F32)

    row = lambda width: pl.BlockSpec((tm, width), lambda i: (i, 0))
    return pl.pallas_call(
        body, name="out_fwd", grid=(s // tm,),
        in_specs=[pl.BlockSpec((tm, 1536), lambda i: (i, 0)),
                  pl.BlockSpec((HALO, 1536), lambda i: (jnp.maximum(i * (tm // HALO) - 1, 0), 0)),
                  pl.BlockSpec((tm, 1536), lambda i: (i, 1)),
                  row(512), row(512), row(D_MODEL),
                  pl.BlockSpec((D_MIX, D_MODEL), lambda i: (0, 0)),
                  pl.BlockSpec((HALO, 512), lambda i: (0, 0))],
        out_specs=[row(D_MODEL), row(D_MIX)],
        out_shape=[jax.ShapeDtypeStruct((s, D_MODEL), F32), jax.ShapeDtypeStruct((s, D_MIX), BF16)],
        compiler_params=_params(("parallel",)),
    )(proj, proj, proj, o_mla, o_swa, x, w_out, cw)


def _out_bwd(dy, proj, o_mla, o_swa, w_out, cw):
    s = proj.shape[0]
    tm = min(512, s)
    nt = s // tm
    hb = tm // HALO

    def body(dy_ref, dyn_ref, conv_ref, convp_ref, convn_ref, gates_ref, gatesn_ref, om_ref, os_ref,
             w_ref, cw_ref,
             dconv_ref, dgates_ref, dom_ref, domt_ref, delta_ref, dos_ref, dcw_ref):
        i = pl.program_id(0)
        dz = _mm_nt(dy_ref[...], w_ref[...])

        def gate(n):
            g = gates_ref[:, GROUP * n:GROUP * (n + 1)]
            sg = _sigmoid(g)
            return g * sg, sg * (1.0 + g * (1.0 - sg))

        for n, o_ref, do_ref in ((0, om_ref, dom_ref), (2, os_ref, dos_ref)):
            silu, dsilu = gate(n)
            dzn = dz[:, GROUP * n:GROUP * (n + 1)]
            o = o_ref[...]
            do = dzn * silu
            do_ref[...] = do.astype(do_ref.dtype)
            dgates_ref[:, GROUP * n:GROUP * (n + 1)] = (dzn * o * dsilu).astype(BF16)
            if n == 0:
                domt_ref[...] = jnp.transpose(do).astype(BF16)
                t = do * o
                for h in range(MLA_HEADS):
                    delta_ref[:, h:h + 1] = jnp.sum(t[:, MLA_V * h:MLA_V * (h + 1)], axis=-1,
                                                    keepdims=True)

        c_h, c_b, c_c, u, u1, u2, yc = _conv_fwd_vals(conv_ref, convp_ref, cw_ref, i == 0)
        silu, dsilu = gate(1)
        dzc = dz[:, GROUP:2 * GROUP]
        dgates_ref[:, GROUP:2 * GROUP] = (dzc * (c_b * yc) * dsilu).astype(BF16)
        dycr = dzc * silu
        dyc = dycr * c_b
        gn = gatesn_ref[:, GROUP:2 * GROUP]
        dzc_n = _mm_nt(dyn_ref[...], w_ref[GROUP:2 * GROUP, :])
        dyc_n = jnp.where(i == nt - 1, 0.0, dzc_n * (gn * _sigmoid(gn)) * convn_ref[:, 512:1024])
        d1 = _shift_up(dyc, dyc_n, 1)
        d2 = _shift_up(dyc, dyc_n, 2)
        du = cw_ref[2:3, :] * dyc + cw_ref[1:2, :] * d1 + cw_ref[0:1, :] * d2
        dconv_ref[:, 0:512] = (du * c_c).astype(BF16)
        dconv_ref[:, 512:1024] = (dycr * yc).astype(BF16)
        dconv_ref[:, 1024:1536] = (du * c_h).astype(BF16)

        @pl.when(i == 0)
        def _():
            dcw_ref[...] = jnp.zeros_like(dcw_ref)

        for k, uk in enumerate((u2, u1, u)):
            dcw_ref[k:k + 1, :] += jnp.sum(dyc * uk, axis=0, keepdims=True)

    row = lambda width: pl.BlockSpec((tm, width), lambda i: (i, 0))
    prev = lambda i: jnp.maximum(i * hb - 1, 0)
    nxt = lambda i: jnp.minimum((i + 1) * hb, s // HALO - 1)
    return pl.pallas_call(
        body, name="out_bwd", grid=(nt,),
        in_specs=[row(D_MODEL),
                  pl.BlockSpec((HALO, D_MODEL), lambda i: (nxt(i), 0)),
                  pl.BlockSpec((tm, 1536), lambda i: (i, 0)),
                  pl.BlockSpec((HALO, 1536), lambda i: (prev(i), 0)),
                  pl.BlockSpec((HALO, 1536), lambda i: (nxt(i), 0)),
                  pl.BlockSpec((tm, 1536), lambda i: (i, 1)),
                  pl.BlockSpec((HALO, 1536), lambda i: (nxt(i), 1)),
                  row(512), row(512),
                  pl.BlockSpec((D_MIX, D_MODEL), lambda i: (0, 0)),
                  pl.BlockSpec((HALO, 512), lambda i: (0, 0))],
        out_specs=[row(1536), row(1536), row(512), pl.BlockSpec((512, tm), lambda i: (0, i)),
                   row(MLA_HEADS), row(512), pl.BlockSpec((HALO, 512), lambda i: (0, 0))],
        out_shape=[jax.ShapeDtypeStruct((s, 1536), BF16), jax.ShapeDtypeStruct((s, 1536), BF16),
                   jax.ShapeDtypeStruct((s, 512), BF16), jax.ShapeDtypeStruct((512, s), BF16),
                   jax.ShapeDtypeStruct((s, MLA_HEADS), F32),
                   jax.ShapeDtypeStruct((s, 512), BF16), jax.ShapeDtypeStruct((HALO, 512), F32)],
        compiler_params=_params(("arbitrary",)),
    )(dy, dy, proj, proj, proj, proj, proj, o_mla, o_swa, w_out, cw)


def _loss_head(y, target):
    s, d = y.shape
    tm = min(512, s)
    nt = s // tm

    def body(y_ref, t_ref, dy_ref, loss_ref):
        i = pl.program_id(0)
        err = y_ref[...] - t_ref[...]
        dy_ref[...] = err * (1.0 / d)

        @pl.when(i == 0)
        def _():
            loss_ref[...] = jnp.zeros_like(loss_ref)

        sq = jnp.sum((err * err).reshape(tm // 8, 8, d), axis=0)
        part = sq[:, 0:LANES]
        for c in range(1, d // LANES):
            part = part + sq[:, LANES * c:LANES * (c + 1)]
        loss_ref[...] += part

        @pl.when(i == nt - 1)
        def _():
            loss_ref[...] = jnp.full(loss_ref.shape, (0.5 / d) * jnp.sum(loss_ref[...]), F32)

    return pl.pallas_call(
        body, name="loss_head", grid=(nt,),
        in_specs=[pl.BlockSpec((tm, d), lambda i: (i, 0)), pl.BlockSpec((tm, d), lambda i: (i, 0))],
        out_specs=[pl.BlockSpec((tm, d), lambda i: (i, 0)), pl.BlockSpec((8, LANES), lambda i: (0, 0))],
        out_shape=[jax.ShapeDtypeStruct((s, d), F32), jax.ShapeDtypeStruct((8, LANES), F32)],
        compiler_params=_params(("arbitrary",)),
    )(y, target)


def _adamw(g, w, m, v):
    rows = g.shape[0]
    tr = min(256, rows)
    c1 = 1.0 - ADAM_B1
    c2 = 1.0 - ADAM_B2
    bc1 = 1.0 - ADAM_B1 ** ADAM_STEP
    bc2 = 1.0 - ADAM_B2 ** ADAM_STEP

    def body(g_ref, w_ref, m_ref, v_ref, d_ref, mo_ref, vo_ref):
        gg = g_ref[...]
        m_new = ADAM_B1 * m_ref[...] + c1 * gg
        v_new = ADAM_B2 * v_ref[...] + c2 * (gg * gg)
        m_hat = m_new / bc1
        v_hat = v_new / bc2
        d_ref[...] = -ADAM_LR * (m_hat / (jnp.sqrt(v_hat) + ADAM_EPS) + ADAM_WD * w_ref[...])
        mo_ref[...] = m_new
        vo_ref[...] = v_new

    spec = pl.BlockSpec((tr, g.shape[1]), lambda i: (i, 0))
    return pl.pallas_call(
        body, name="adamw", grid=(rows // tr,),
        in_specs=[spec] * 4, out_specs=[spec] * 3,
        out_shape=[jax.ShapeDtypeStruct(g.shape, F32)] * 3,
        compiler_params=_params(("parallel",)),
    )(g, w, m, v)


HBM_SPEC = pl.BlockSpec(memory_space=pltpu.HBM)


def _place():
    x, y, c = lax.axis_index("x"), lax.axis_index("y"), lax.axis_index("c")
    chips = [(1 - x, y), (x, 1 - y), (1 - x, 1 - y)]
    return x, y, c, chips


def _all_gather(shards):
    na = len(shards)
    halves = [sh.shape[0] // 2 for sh in shards]

    def body(*refs):
        w_refs, a_refs = refs[:na], refs[na:2 * na]
        send_sems, recv_sems = refs[2 * na:]
        x, y, c, chips = _place()
        k = 2 * x + y
        sib = (x, y, 1 - c)

        def slab(a, kk, hc):
            return a_refs[a].at[kk, pl.ds(hc * halves[a], halves[a]), :]

        def copy(a, n, src, dst, to):
            return pltpu.make_async_remote_copy(
                src_ref=src, dst_ref=dst, send_sem=send_sems.at[6 * a + n],
                recv_sem=recv_sems.at[6 * a + n], device_id=to, device_id_type=MESH)

        first = [copy(a, n, w_refs[a].at[pl.ds(c * halves[a], halves[a]), :], slab(a, k, c),
                      (cx, cy, c))
                 for n, (cx, cy) in enumerate(chips) for a in range(na)]
        for cp in first:
            cp.start()
        passed = []
        for n, (cx, cy) in enumerate(chips):
            kk = 2 * cx + cy
            for a in range(na):
                copy(a, n, slab(a, kk, c), slab(a, kk, c), (cx, cy, c)).wait_recv()
                fwd = copy(a, 3 + n, slab(a, kk, c), slab(a, kk, c), sib)
                fwd.start()
                passed.append(fwd)
        for n, (cx, cy) in enumerate(chips):
            kk = 2 * cx + cy
            for a in range(na):
                copy(a, 3 + n, slab(a, kk, 1 - c), slab(a, kk, 1 - c), sib).wait_recv()
        for cp in first + passed:
            cp.wait_send()

    return pl.pallas_call(
        body, name="weights_all_gather",
        in_specs=[HBM_SPEC] * na, out_specs=[HBM_SPEC] * na,
        out_shape=[jax.ShapeDtypeStruct((N_CHIPS,) + sh.shape, sh.dtype) for sh in shards],
        scratch_shapes=[pltpu.SemaphoreType.DMA((6 * na,)), pltpu.SemaphoreType.DMA((6 * na,))],
    )(*shards)


def _fill_own_slab(buf, src, k_idx):
    n, rows, cols = buf.shape
    tr = _row_tile(rows)
    slabs = src.ndim == 3

    def body(k_ref, src_ref, buf_ref, out_ref):
        out_ref[0] = src_ref[0] if slabs else src_ref[...]

    if slabs:
        src_spec = pl.BlockSpec((1, tr, cols), lambda t, k_ref: (k_ref[0], t, 0))
    else:
        src_spec = pl.BlockSpec((tr, cols), lambda t, k_ref: (t, 0))
    return pl.pallas_call(
        body, name="fill_own_slab",
        grid_spec=pltpu.PrefetchScalarGridSpec(
            num_scalar_prefetch=1, grid=(rows // tr,),
            in_specs=[src_spec, pl.BlockSpec(memory_space=pl.ANY)],
            out_specs=pl.BlockSpec((1, tr, cols), lambda t, k_ref: (k_ref[0], t, 0))),
        out_shape=jax.ShapeDtypeStruct(buf.shape, buf.dtype),
        input_output_aliases={2: 0},
        compiler_params=_params(("parallel",)),
    )(k_idx, src, buf)


def _swap_halves_to_sibling(gs):
    na = len(gs)

    def body(*refs):
        g_refs, r_refs = refs[:na], refs[na:2 * na]
        send_sems, recv_sems = refs[2 * na:]
        x, y, c, _ = _place()
        cps = []
        for a in range(na):
            half = g_refs[a].shape[1] // 2
            cps.append(pltpu.make_async_remote_copy(
                src_ref=g_refs[a].at[:, pl.ds((1 - c) * half, half), :], dst_ref=r_refs[a],
                send_sem=send_sems.at[a], recv_sem=recv_sems.at[a], device_id=(x, y, 1 - c),
                device_id_type=MESH))
        for cp in cps:
            cp.start()
        for cp in cps:
            cp.wait()

    return pl.pallas_call(
        body, name="grads_to_sibling",
        in_specs=[HBM_SPEC] * na, out_specs=[HBM_SPEC] * na,
        out_shape=[jax.ShapeDtypeStruct((g.shape[0], g.shape[1] // 2, g.shape[2]), g.dtype)
                   for g in gs],
        scratch_shapes=[pltpu.SemaphoreType.DMA((na,)), pltpu.SemaphoreType.DMA((na,))],
    )(*gs)


def _row_tile(rows):
    return 256 if rows % 256 == 0 else 128


def _add_sibling(g, r, c_idx, out_dtype):
    n, rows, cols = g.shape
    half = rows // 2
    tr = _row_tile(half)
    nb = half // tr

    def body(c_ref, g_ref, r_ref, p_ref):
        p_ref[...] = (g_ref[...] + r_ref[...]).astype(out_dtype)

    return pl.pallas_call(
        body, name="grads_add_sibling",
        grid_spec=pltpu.PrefetchScalarGridSpec(
            num_scalar_prefetch=1, grid=(n, nb),
            in_specs=[pl.BlockSpec((1, tr, cols), lambda j, t, c_ref: (j, c_ref[0] * nb + t, 0)),
                      pl.BlockSpec((1, tr, cols), lambda j, t, c_ref: (j, t, 0))],
            out_specs=pl.BlockSpec((1, tr, cols), lambda j, t, c_ref: (j, t, 0))),
        out_shape=jax.ShapeDtypeStruct((n, half, cols), out_dtype),
        compiler_params=_params(("parallel", "parallel")),
    )(c_idx, g, r)


def _scatter_to_chips(ps):
    na = len(ps)

    def body(*refs):
        p_refs, q_refs = refs[:na], refs[na:2 * na]
        send_sems, recv_sems = refs[2 * na:]
        x, y, c, chips = _place()
        k = 2 * x + y
        sends = []
        for i, (cx, cy) in enumerate(chips):
            for a in range(na):
                cp = pltpu.make_async_remote_copy(
                    src_ref=p_refs[a].at[2 * cx + cy], dst_ref=q_refs[a].at[k],
                    send_sem=send_sems.at[3 * a + i], recv_sem=recv_sems.at[3 * a + i],
                    device_id=(cx, cy, c), device_id_type=MESH)
                cp.start()
                sends.append(cp)
        for i, (cx, cy) in enumerate(chips):
            kk = 2 * cx + cy
            for a in range(na):
                pltpu.make_async_remote_copy(
                    src_ref=p_refs[a].at[kk], dst_ref=q_refs[a].at[kk],
                    send_sem=send_sems.at[3 * a + i], recv_sem=recv_sems.at[3 * a + i],
                    device_id=(cx, cy, c), device_id_type=MESH).wait_recv()
        for cp in sends:
            cp.wait_send()

    return pl.pallas_call(
        body, name="grads_scatter_to_chips",
        in_specs=[HBM_SPEC] * na, out_specs=[HBM_SPEC] * na,
        out_shape=[jax.ShapeDtypeStruct(p.shape, p.dtype) for p in ps],
        scratch_shapes=[pltpu.SemaphoreType.DMA((3 * na,)), pltpu.SemaphoreType.DMA((3 * na,))],
    )(*ps)


def _sum_chips(q, c_idx):
    n, half, cols = q.shape
    tr = _row_tile(half)
    nb = half // tr

    def body(c_ref, q_ref, o_ref):
        parts = [q_ref[kk].astype(F32) for kk in range(n)]
        o_ref[...] = ((parts[0] + parts[1]) + parts[2]) + parts[3]

    return pl.pallas_call(
        body, name="grads_sum_chips",
        grid_spec=pltpu.PrefetchScalarGridSpec(
            num_scalar_prefetch=1, grid=(nb,),
            in_specs=[pl.BlockSpec((n, tr, cols), lambda t, c_ref: (0, t, 0))],
            out_specs=pl.BlockSpec((tr, cols), lambda t, c_ref: (c_ref[0] * nb + t, 0))),
        out_shape=jax.ShapeDtypeStruct((2 * half, cols), F32),
        compiler_params=_params(("parallel",)),
    )(c_idx, q)


def _join_halves(fulls):
    na = len(fulls)

    def body(*refs):
        o_refs = refs[na:2 * na]
        send_sems, recv_sems = refs[2 * na:]
        x, y, c, _ = _place()
        sends = []
        for a in range(na):
            half = o_refs[a].shape[0] // 2
            rows = o_refs[a].at[pl.ds(c * half, half), :]
            sends.append(pltpu.make_async_remote_copy(
                src_ref=rows, dst_ref=rows, send_sem=send_sems.at[a], recv_sem=recv_sems.at[a],
                device_id=(x, y, 1 - c), device_id_type=MESH))
        for cp in sends:
            cp.start()
        for a in range(na):
            half = o_refs[a].shape[0] // 2
            other = o_refs[a].at[pl.ds((1 - c) * half, half), :]
            pltpu.make_async_remote_copy(
                src_ref=other, dst_ref=other, send_sem=send_sems.at[a], recv_sem=recv_sems.at[a],
                device_id=(x, y, 1 - c), device_id_type=MESH).wait_recv()
        for cp in sends:
            cp.wait_send()

    return pl.pallas_call(
        body, name="grads_join_halves",
        in_specs=[HBM_SPEC] * na, out_specs=[HBM_SPEC] * na,
        out_shape=[jax.ShapeDtypeStruct(f.shape, f.dtype) for f in fulls],
        input_output_aliases={a: a for a in range(na)},
        scratch_shapes=[pltpu.SemaphoreType.DMA((na,)), pltpu.SemaphoreType.DMA((na,))],
    )(*fulls)


def _part_rows(shape):
    size = 1
    for d in shape:
        size *= d
    rows = -(-size // PACK_COLS)
    return size, -(-rows // PACK_ROW_ALIGN) * PACK_ROW_ALIGN


def _pack_rows(arrays, dtype, total_rows):
    parts, used = [], 0
    for a in arrays:
        size, rows = _part_rows(a.shape)
        flat = a.reshape(-1).astype(dtype)
        parts.append(jnp.pad(flat, (0, rows * PACK_COLS - size)).reshape(rows, PACK_COLS))
        used += rows
    parts.append(jnp.zeros((total_rows - used, PACK_COLS), dtype))
    return jnp.concatenate(parts, axis=0)


def _unpack_rows(buf, shapes):
    lead = buf.shape[:-2]
    out, off = [], 0
    for sh in shapes:
        size, rows = _part_rows(sh)
        part = buf[..., off:off + rows, :].reshape(lead + (-1,))[..., :size]
        out.append(part.reshape(lead + tuple(sh)))
        off += rows
    return out


NEW_ORDER = ((928, 1440), (1440, 1952), (1952, 2464), (416, 928), (2464, 2976), (3744, 4256),
             (2976, 3488), (0, 256), (256, 384), (4256, 4320), (384, 416), (4256, 4288),
             (3488, 3616), (3616, 3744))
OLD_ORDER = ((3584, 3840), (3840, 3968), (4032, 4064), (1536, 2048), (0, 512), (512, 1024),
             (1024, 1536), (2048, 2560), (3072, 3584), (4096, 4224), (4224, 4352), (2560, 3072))


def _cols(sources, ranges):
    parts = []
    for a, b in ranges:
        off = 0
        for src in sources:
            width = src.shape[-1]
            lo, hi = max(a, off), min(b, off + width)
            if lo < hi:
                parts.append(src[..., lo - off:hi - off])
            off += width
    return jnp.concatenate(parts, axis=-1)


def _sub_ranges(ranges, a, b):
    out, off = [], 0
    for lo, hi in ranges:
        width = hi - lo
        s0, s1 = max(a, off), min(b, off + width)
        if s0 < s1:
            out.append((lo + s0 - off, lo + s1 - off))
        off += width
    return out


def _rope_tables(s):
    half = MLA_ROPE // 2
    inv_freq = jnp.power(jnp.float32(ROPE_THETA), -jnp.arange(half, dtype=F32) / half)
    ang = inv_freq[:, None] * jnp.arange(s, dtype=F32)[None, :]
    cos, sin = jnp.cos(ang), jnp.sin(ang)
    z = lambda n: jnp.zeros((n, s), F32)
    c = jnp.concatenate([jnp.ones((MLA_NOPE, s), F32), cos, cos, z(32)], axis=0)
    s1 = jnp.concatenate([z(MLA_NOPE), -sin, z(16), z(32)], axis=0)
    s2 = jnp.concatenate([z(MLA_NOPE), z(16), sin, z(32)], axis=0)
    return c, s1, s2


def _pad_lanes(a, n):
    return jnp.pad(a, ((0, 0), (0, n - a.shape[1])))


SHARDED = ("w_in", "w_out", "mla_w_qb", "mla_w_kvb", "conv_w")
REPLICATED = ("norm_g", "mla_q_a_norm", "mla_kv_a_norm", "mla_q_norm", "mla_k_norm",
              "swa_q_norm", "swa_k_norm", "swa_sinks")
WEIGHT_ORDER = ("norm_g", "w_in", "mla_q_a_norm", "mla_w_qb", "mla_kv_a_norm", "mla_w_kvb",
                "mla_q_norm", "mla_k_norm", "conv_w", "swa_q_norm", "swa_k_norm", "swa_sinks", "w_out")
SHARD_AXIS = {"w_in": 2, "w_out": 1, "mla_w_qb": 2, "mla_w_kvb": 2, "conv_w": 2}


def kernel(x, norm_g, w_in, mla_q_a_norm, mla_w_qb, mla_kv_a_norm, mla_w_kvb, mla_q_norm, mla_k_norm, conv_w, swa_q_norm, swa_k_norm, swa_sinks, w_out, loss_target, m_norm_g, m_w_in, m_mla_q_a_norm, m_mla_w_qb, m_mla_kv_a_norm, m_mla_w_kvb, m_mla_q_norm, m_mla_k_norm, m_conv_w, m_swa_q_norm, m_swa_k_norm, m_swa_sinks, m_w_out, v_norm_g, v_w_in, v_mla_q_a_norm, v_mla_w_qb, v_mla_kv_a_norm, v_mla_w_kvb, v_mla_q_norm, v_mla_k_norm, v_conv_w, v_swa_q_norm, v_swa_k_norm, v_swa_sinks, v_w_out):
    weights = dict(norm_g=norm_g, w_in=w_in, mla_q_a_norm=mla_q_a_norm, mla_w_qb=mla_w_qb,
                   mla_kv_a_norm=mla_kv_a_norm, mla_w_kvb=mla_w_kvb, mla_q_norm=mla_q_norm,
                   mla_k_norm=mla_k_norm, conv_w=conv_w, swa_q_norm=swa_q_norm,
                   swa_k_norm=swa_k_norm, swa_sinks=swa_sinks, w_out=w_out)
    mom_m = dict(norm_g=m_norm_g, w_in=m_w_in, mla_q_a_norm=m_mla_q_a_norm, mla_w_qb=m_mla_w_qb,
                 mla_kv_a_norm=m_mla_kv_a_norm, mla_w_kvb=m_mla_w_kvb, mla_q_norm=m_mla_q_norm,
                 mla_k_norm=m_mla_k_norm, conv_w=m_conv_w, swa_q_norm=m_swa_q_norm,
                 swa_k_norm=m_swa_k_norm, swa_sinks=m_swa_sinks, w_out=m_w_out)
    mom_v = dict(norm_g=v_norm_g, w_in=v_w_in, mla_q_a_norm=v_mla_q_a_norm, mla_w_qb=v_mla_w_qb,
                 mla_kv_a_norm=v_mla_kv_a_norm, mla_w_kvb=v_mla_w_kvb, mla_q_norm=v_mla_q_norm,
                 mla_k_norm=v_mla_k_norm, conv_w=v_conv_w, swa_q_norm=v_swa_q_norm,
                 swa_k_norm=v_swa_k_norm, swa_sinks=v_swa_sinks, w_out=v_w_out)
    xs = x[0]
    target = loss_target[0]
    s = xs.shape[0]
    c_idx = lax.axis_index("c").astype(jnp.int32).reshape(1)
    k_idx = (2 * lax.axis_index("x") + lax.axis_index("y")).astype(jnp.int32).reshape(1)

    conv_bits = lax.bitcast_convert_type(conv_w, BF16)
    small_list = [w_out, mla_w_qb, mla_w_kvb, conv_bits]
    w_in_rows = DEPTH * D_MODEL
    own = [w_in.astype(BF16).reshape(w_in_rows, w_in.shape[2]), _pack_rows(small_list, BF16, PACK_ROWS)]
    gathered_in, gathered_rest = [_fill_own_slab(buf, src, k_idx)
                                  for buf, src in zip(_all_gather(own), own)]
    parts = _unpack_rows(gathered_rest, [a.shape for a in small_list])
    join = lambda p, axis: jnp.concatenate([p[k] for k in range(N_CHIPS)], axis=axis)
    w_in_slabs = gathered_in.reshape(N_CHIPS, DEPTH, D_MODEL, w_in.shape[2])
    w_in_zeros = jnp.zeros((D_MODEL, 64), BF16)
    w_out_full = join(parts[0], 1)
    w_qb_full = join(parts[1], 2)
    w_kvb_full = join(parts[2], 2)
    conv_full = lax.bitcast_convert_type(join(parts[3], 2), F32)

    rope = _rope_tables(s)
    swa_tables = _swa_tables()
    layers = []
    for l in range(DEPTH):
        wq = jnp.pad(w_qb_full[l].reshape(MLA_Q_LORA, MLA_HEADS, MLA_QK),
                     ((0, 0), (0, 0), (0, LANES - MLA_QK))).reshape(MLA_Q_LORA, MLA_HEADS * LANES)
        kv = w_kvb_full[l].reshape(MLA_KV_LORA, MLA_HEADS, MLA_NOPE + MLA_V)
        wk = jnp.pad(kv[:, :, :MLA_NOPE], ((0, 0), (0, 0), (0, LANES - MLA_NOPE)))
        wkv = jnp.concatenate([wk.reshape(MLA_KV_LORA, MLA_HEADS * LANES),
                               kv[:, :, MLA_NOPE:].reshape(MLA_KV_LORA, MLA_HEADS * MLA_V)], axis=1)
        layers.append(dict(
            w_in=_cols([w_in_slabs[k, l] for k in range(N_CHIPS)] + [w_in_zeros], NEW_ORDER),
            w_out=w_out_full[l], wq=wq, wkv=wkv,
            cw=jnp.pad(conv_full[l], ((0, HALO - 3), (0, 0))),
            g=norm_g[l][None],
            mla_norms=(mla_q_a_norm[l][:, None], mla_kv_a_norm[l][:, None],
                       _pad_lanes(mla_q_norm[l][None], LANES).T, _pad_lanes(mla_k_norm[l][None], LANES).T),
            mla_weights=(wq, wkv, wq.T, wkv.T),
            sqn=swa_q_norm[l][:, None], skn=swa_k_norm[l][:, None], sinks=swa_sinks[l][None]))

    saved = []
    h_in = xs
    for l in range(DEPTH):
        p = layers[l]
        proj, hb = _in_proj_fwd(h_in, p["g"], p["w_in"])
        q, k, v, qt, kt, vt = _mla_prep_fwd(proj, p["mla_norms"], p["mla_weights"], rope)
        o_mla, lse = _mla_attn_fwd(q, k, vt)
        o_swa = _swa_fwd(proj, p["sqn"], p["skn"], p["sinks"], swa_tables)
        y, z = _out_fwd(proj, o_mla, o_swa, h_in, p["w_out"], p["cw"])
        saved.append(dict(x=h_in, proj=proj, hb=hb, q=q, k=k, v=v, qt=qt, kt=kt, o_mla=o_mla, lse=lse,
                          o_swa=o_swa, z=z))
        h_in = y

    dy, loss_acc = _loss_head(h_in, target)
    loss = lax.psum(loss_acc[0, 0], ("x", "y", "c"))

    grads = {n: [None] * DEPTH for n in WEIGHT_ORDER}
    for l in reversed(range(DEPTH)):
        p, a = layers[l], saved[l]
        dconv, dgates, do_mla, do_mla_t, delta, do_swa, dcw = _out_bwd(dy, a["proj"], a["o_mla"], a["o_swa"],
                                                             p["w_out"], p["cw"])
        grads["w_out"][l] = _matmul_tn(a["z"], dy, "dw_out")
        grads["conv_w"][l] = dcw[0:3]
        delta_rows = jnp.transpose(delta, (1, 0)).reshape(MLA_HEADS // 2, 2, s)
        dq, dk, dv = _mla_attn_bwd(a["q"], a["qt"], a["k"], a["kt"], a["v"], do_mla, do_mla_t,
                                   a["lse"], delta_rows)
        dmla, dqan, dkvan, dqn, dkn, dwq_t, dwkv_t = _mla_prep_bwd(
            a["proj"], p["mla_norms"], p["mla_weights"], rope, dq, dk, dv)
        dwq, dwkv = dwq_t.T, dwkv_t.T
        dsq, dskv, dsqn, dskn, dsinks = _swa_bwd(a["proj"], p["sqn"], p["skn"], p["sinks"], swa_tables, do_swa)
        pieces = [dconv, dgates, dsq, dmla, dskv]
        dx, dg = _in_proj_bwd(pieces, a["x"], p["g"], p["w_in"], dy)
        grads["w_in"][l] = [_matmul_tn(a["hb"], pc, "dw_in_%d" % n) for n, pc in enumerate(pieces)]
        grads["norm_g"][l] = dg[0]
        grads["mla_q_a_norm"][l] = dqan[:, 0]
        grads["mla_kv_a_norm"][l] = dkvan[:, 0]
        grads["mla_q_norm"][l] = dqn[:MLA_QK, 0]
        grads["mla_k_norm"][l] = dkn[:MLA_QK, 0]
        grads["mla_w_qb"][l] = dwq.reshape(MLA_Q_LORA, MLA_HEADS, LANES)[:, :, :MLA_QK].reshape(
            MLA_Q_LORA, MLA_HEADS * MLA_QK)
        dwk = dwkv[:, :MLA_HEADS * LANES].reshape(MLA_KV_LORA, MLA_HEADS, LANES)[:, :, :MLA_NOPE]
        dwv = dwkv[:, MLA_HEADS * LANES:].reshape(MLA_KV_LORA, MLA_HEADS, MLA_V)
        grads["mla_w_kvb"][l] = jnp.concatenate([dwk, dwv], axis=2).reshape(
            MLA_KV_LORA, MLA_HEADS * (MLA_NOPE + MLA_V))
        grads["swa_q_norm"][l] = dsqn[:, 0]
        grads["swa_k_norm"][l] = dskn[:, 0]
        grads["swa_sinks"][l] = dsinks[0]
        dy = dx
    grad_x = dy[None]
    full_grads = {n: jnp.stack(grads[n]) for n in WEIGHT_ORDER if n != "w_in"}

    rest = tuple(n for n in SHARDED if n != "w_in")
    rep_shapes = [weights[n].shape for n in REPLICATED]
    flat_rep = lambda d: jnp.concatenate([d[n].reshape(-1) for n in REPLICATED])

    def chunk(g, n, k):
        width = g.shape[SHARD_AXIS[n]] // N_CHIPS
        return lax.slice_in_dim(g, k * width, (k + 1) * width, axis=SHARD_AXIS[n])

    shard_cols = w_in.shape[2]
    g_in = jnp.stack([
        jnp.concatenate([_cols(grads["w_in"][l], _sub_ranges(OLD_ORDER, k * shard_cols,
                                                             (k + 1) * shard_cols))
                         for l in range(DEPTH)], axis=0)
        for k in range(N_CHIPS)])
    rep_grads = flat_rep(full_grads)
    g_rest = jnp.stack([_pack_rows([chunk(full_grads[n], n, k) for n in rest] + [rep_grads],
                                   F32, PACK_ROWS) for k in range(N_CHIPS)])
    from_sibling = _swap_halves_to_sibling([g_in, g_rest])
    partial = [_add_sibling(g, r, c_idx, dt)
               for g, r, dt in zip((g_in, g_rest), from_sibling, (BF16, F32))]
    by_chip = [_fill_own_slab(q, p, k_idx) for q, p in zip(_scatter_to_chips(partial), partial)]
    g_in_mine, g_rest_mine = _join_halves([_sum_chips(q, c_idx) for q in by_chip])

    pack_rest = lambda d: _pack_rows([d[n] for n in rest] + [flat_rep(d)], F32, PACK_ROWS)
    in_shape = w_in.shape
    res_in = _adamw(g_in_mine, w_in.reshape(w_in_rows, -1), m_w_in.reshape(w_in_rows, -1),
                    v_w_in.reshape(w_in_rows, -1))
    res_rest = _adamw(g_rest_mine, pack_rest(weights), pack_rest(mom_m), pack_rest(mom_v))
    rest_shapes = [weights[n].shape for n in rest] + [(rep_grads.shape[0],)]
    unpacked = []
    for buf_in, buf_rest in zip((g_in_mine,) + tuple(res_in), (g_rest_mine,) + tuple(res_rest)):
        vals = _unpack_rows(buf_rest, rest_shapes)
        group = dict(zip(rest, vals[:-1]))
        group["w_in"] = buf_in.reshape(in_shape)
        off = 0
        for n, sh in zip(REPLICATED, rep_shapes):
            size = sh[0] * sh[1]
            group[n] = vals[-1][off:off + size].reshape(sh)
            off += size
        unpacked.append(group)
    outs = [loss, grad_x]
    for group in unpacked:
        outs += [group[n] for n in WEIGHT_ORDER]
    return tuple(outs)
```

```python
import jax
import numpy as np
import jax.numpy as jnp
from jax import lax
from jax.experimental import pallas as pl
from jax.experimental.pallas import tpu as pltpu

F32 = jnp.float32
BF16 = jnp.bfloat16

D_MODEL = 1024
DEPTH = 2
GROUP = 512
D_MIX = 3 * GROUP
BLOCK = 128
RMS_EPS = 1e-6
NEG_INF = -1e30
MLA_HEADS = 8
MLA_QK = 96
MLA_NOPE = 64
MLA_ROPE = 32
MLA_V = 64
MLA_Q_LORA = 256
MLA_KV_LORA = 128
ROPE_THETA = 10000.0
SWA_HEADS = 8
SWA_KV = 2
SWA_GROUP = 4
SWA_DIM = 64
N_CHIPS = 4

NC = 4352
OFF_SQ, OFF_MLA, OFF_SKV = 3072, 3584, 4096

VMEM_LIMIT = 56 * 1024 * 1024
LANES = 128
PACK_COLS = 1024
PACK_ROW_ALIGN = 16
PACK_ROWS = 1024

ADAM_LR = 0.001
ADAM_B1 = 0.9
ADAM_B2 = 0.999
ADAM_EPS = 1e-08
ADAM_WD = 0.01
ADAM_STEP = 10

MESH = pl.DeviceIdType.MESH


def _params(sem, vmem=VMEM_LIMIT):
    return pltpu.CompilerParams(dimension_semantics=sem, vmem_limit_bytes=vmem)


def _dot(a, b, dims):
    return lax.dot_general(a.astype(BF16), b.astype(BF16), (dims, ((), ())),
                           preferred_element_type=F32)


def _mm(a, b):
    return _dot(a, b, ((1,), (0,)))


def _mm_nt(a, b):
    return _dot(a, b, ((1,), (1,)))


def _mm_tn(a, b):
    return _dot(a, b, ((0,), (0,)))


def _rms(x, g, n=None):
    n = x.shape[-1] if n is None else n
    ms = jnp.sum(x * x, axis=-1, keepdims=True) * (1.0 / n)
    return x * lax.rsqrt(ms + RMS_EPS) * g


def _sigmoid(x):
    return 1.0 / (1.0 + jnp.exp(-x))


def _in_proj_fwd(x, g, w):
    s = x.shape[0]
    tm = min(512, s)

    def body(x_ref, g_ref, w_ref, proj_ref, hb_ref):
        hb = _rms(x_ref[...], g_ref[...]).astype(BF16)
        hb_ref[...] = hb
        proj_ref[...] = jnp.dot(hb, w_ref[...], preferred_element_type=F32)

    return pl.pallas_call(
        body, name="in_proj_fwd", grid=(s // tm,),
        in_specs=[pl.BlockSpec((tm, D_MODEL), lambda i: (i, 0)),
                  pl.BlockSpec((1, D_MODEL), lambda i: (0, 0)),
                  pl.BlockSpec((D_MODEL, NC), lambda i: (0, 0))],
        out_specs=[pl.BlockSpec((tm, NC), lambda i: (i, 0)),
                   pl.BlockSpec((tm, D_MODEL), lambda i: (i, 0))],
        out_shape=[jax.ShapeDtypeStruct((s, NC), F32), jax.ShapeDtypeStruct((s, D_MODEL), BF16)],
        compiler_params=_params(("parallel",)),
    )(x, g, w)


def _in_proj_bwd(pieces, x, g, w, dres):
    s = x.shape[0]
    tm = min(512, s)
    n_p = len(pieces)

    def body(*refs):
        p_refs = refs[:n_p]
        x_ref, g_ref, w_ref, dres_ref, dx_ref, dg_ref = refs[n_p:]
        dh = None
        off = 0
        for r in p_refs:
            width = r.shape[1]
            t = _mm_nt(r[...], w_ref[:, off:off + width])
            dh = t if dh is None else dh + t
            off += width
        _, vjp = jax.vjp(_rms, x_ref[...], g_ref[...])
        dx, dg = vjp(dh)
        dx_ref[...] = dx + dres_ref[...]

        @pl.when(pl.program_id(0) == 0)
        def _():
            dg_ref[...] = jnp.zeros_like(dg_ref)

        dg_ref[...] += dg

    in_specs = [pl.BlockSpec((tm, p.shape[1]), lambda i: (i, 0)) for p in pieces]
    in_specs += [pl.BlockSpec((tm, D_MODEL), lambda i: (i, 0)),
                 pl.BlockSpec((1, D_MODEL), lambda i: (0, 0)),
                 pl.BlockSpec((D_MODEL, NC), lambda i: (0, 0)),
                 pl.BlockSpec((tm, D_MODEL), lambda i: (i, 0))]
    return pl.pallas_call(
        body, name="in_proj_bwd", grid=(s // tm,),
        in_specs=in_specs,
        out_specs=[pl.BlockSpec((tm, D_MODEL), lambda i: (i, 0)),
                   pl.BlockSpec((1, D_MODEL), lambda i: (0, 0))],
        out_shape=[jax.ShapeDtypeStruct((s, D_MODEL), F32), jax.ShapeDtypeStruct((1, D_MODEL), F32)],
        compiler_params=_params(("arbitrary",)),
    )(*pieces, x, g, w, dres)


def _matmul_tn(a, b, name):
    s, m = a.shape
    n = b.shape[1]
    tk = min(1024, s)
    tn = min(1536, n)

    def body(a_ref, b_ref, o_ref):
        @pl.when(pl.program_id(1) == 0)
        def _():
            o_ref[...] = jnp.zeros_like(o_ref)

        o_ref[...] += _mm_tn(a_ref[...], b_ref[...])

    return pl.pallas_call(
        body, name=name, grid=(n // tn, s // tk),
        in_specs=[pl.BlockSpec((tk, m), lambda j, k: (k, 0)),
                  pl.BlockSpec((tk, tn), lambda j, k: (k, j))],
        out_specs=pl.BlockSpec((m, tn), lambda j, k: (0, j)),
        out_shape=jax.ShapeDtypeStruct((m, n), F32),
        compiler_params=_params(("parallel", "arbitrary")),
    )(a, b)


def _rms0(x, g, n=None):
    n = x.shape[0] if n is None else n
    ms = jnp.sum(x * x, axis=0, keepdims=True) * (1.0 / n)
    return x * lax.rsqrt(ms + RMS_EPS) * g


@jax.custom_vjp
def _rope0(t, c, s1, s2):
    return t * c + pltpu.roll(t, LANES - 16, 0) * s1 + pltpu.roll(t, 16, 0) * s2


def _rope0_fwd(t, c, s1, s2):
    return _rope0(t, c, s1, s2), (c, s1, s2)


def _rope0_bwd(res, g):
    c, s1, s2 = res
    dt = g * c + pltpu.roll(g * s1, 16, 0) + pltpu.roll(g * s2, LANES - 16, 0)
    return dt, jnp.zeros_like(c), jnp.zeros_like(s1), jnp.zeros_like(s2)


_rope0.defvjp(_rope0_fwd, _rope0_bwd)


@jax.custom_vjp
def _mmw(w, wt, x):
    return _mm(w, x)


def _mmw_fwd(w, wt, x):
    return _mm(w, x), (wt, x)


def _mmw_bwd(res, g):
    wt, x = res
    return _mm_nt(g, x), jnp.zeros_like(wt), _mm(wt, g)


_mmw.defvjp(_mmw_fwd, _mmw_bwd)


def _prep_fn(q_lat, kv_lat, kr, qan, kvan, qn, kn, wq, wk, wv, wqt, wkt, wvt, c, s1, s2, mm):
    tokens = q_lat.shape[1]
    rq = _rms0(q_lat, qan)
    rkv = _rms0(kv_lat, kvan)
    qn_b = jnp.broadcast_to(qn, (LANES, tokens))
    kn_b = jnp.broadcast_to(kn, (LANES, tokens))
    qs, ks = [], []
    for h in range(MLA_HEADS):
        qs.append(_rope0(_rms0(mm(wq[h], wqt[h], rq), qn_b, MLA_QK), c, s1, s2))
        ks.append(_rope0(_rms0(mm(wk[h], wkt[h], rkv) + kr, kn_b, MLA_QK), c, s1, s2))
    return tuple(qs), tuple(ks), mm(wv, wvt, rkv)


def _prep_weights(wq_ref, wkv_ref, wqt_ref, wkvt_ref):
    heads = range(MLA_HEADS)
    wq = tuple(wqt_ref[LANES * h:LANES * (h + 1), :].astype(F32) for h in heads)
    wk = tuple(wkvt_ref[LANES * h:LANES * (h + 1), :].astype(F32) for h in heads)
    wv = wkvt_ref[LANES * MLA_HEADS:, :].astype(F32)
    wqt = tuple(wq_ref[:, LANES * h:LANES * (h + 1)].astype(F32) for h in heads)
    wkt = tuple(wkv_ref[:, LANES * h:LANES * (h + 1)].astype(F32) for h in heads)
    wvt = wkv_ref[:, LANES * MLA_HEADS:].astype(F32)
    return wq, wk, wv, wqt, wkt, wvt


def _prep_in_specs(tm):
    const = lambda shape: pl.BlockSpec(shape, lambda i: (0, 0))
    col = lambda height: pl.BlockSpec((height, tm), lambda i: (0, i))
    return [pl.BlockSpec((tm, 512), lambda i: (i, OFF_MLA // 512)),
            const((MLA_Q_LORA, 1)), const((MLA_KV_LORA, 1)), const((LANES, 1)), const((LANES, 1)),
            const((MLA_Q_LORA, 1024)), const((MLA_KV_LORA, 1536)),
            const((1024, MLA_Q_LORA)), const((1536, MLA_KV_LORA)),
            col(LANES), col(LANES), col(LANES)]


def _prep_operands(blk_ref, refs):
    qan_ref, kvan_ref, qn_ref, kn_ref, wq_ref, wkv_ref, wqt_ref, wkvt_ref, c_ref, s1_ref, s2_ref = refs
    blk_t = jnp.transpose(blk_ref[...])
    diff = (blk_t[0:256], blk_t[256:384], blk_t[384:512],
            qan_ref[...], kvan_ref[...], qn_ref[...], kn_ref[...])
    weights = _prep_weights(wq_ref, wkv_ref, wqt_ref, wkvt_ref)
    return diff, weights, (c_ref[...], s1_ref[...], s2_ref[...])


def _mla_prep_fwd(proj, norms, weights, rope):
    s = proj.shape[0]
    tm = min(512, s)

    def body(blk_ref, *refs):
        ins, (q_ref, k_ref, v_ref, qt_ref, kt_ref, vt_ref) = refs[:11], refs[11:]
        diff, (wq, wk, wv, wqt, wkt, wvt), tables = _prep_operands(blk_ref, ins)
        qs, ks, v = _prep_fn(*diff, wq, wk, wv, wqt, wkt, wvt, *tables,
                             lambda w, wt, x: _mm(w, x))
        for h in range(MLA_HEADS):
            q2 = qs[h] * Q_PRESCALE
            qt_ref[LANES * h:LANES * (h + 1), :] = q2.astype(BF16)
            kt_ref[LANES * h:LANES * (h + 1), :] = ks[h].astype(BF16)
            q_ref[:, LANES * h:LANES * (h + 1)] = jnp.transpose(q2).astype(BF16)
            k_ref[:, LANES * h:LANES * (h + 1)] = jnp.transpose(ks[h]).astype(BF16)
        vt_ref[...] = v.astype(BF16)
        v_ref[...] = jnp.transpose(v).astype(BF16)

    row = lambda width: pl.BlockSpec((tm, width), lambda i: (i, 0))
    col = lambda height: pl.BlockSpec((height, tm), lambda i: (0, i))
    return pl.pallas_call(
        body, name="mla_prep_fwd", grid=(s // tm,),
        in_specs=_prep_in_specs(tm),
        out_specs=[row(1024), row(1024), row(512), col(1024), col(1024), col(512)],
        out_shape=[jax.ShapeDtypeStruct((s, 1024), BF16), jax.ShapeDtypeStruct((s, 1024), BF16),
                   jax.ShapeDtypeStruct((s, 512), BF16), jax.ShapeDtypeStruct((1024, s), BF16),
                   jax.ShapeDtypeStruct((1024, s), BF16), jax.ShapeDtypeStruct((512, s), BF16)],
        compiler_params=_params(("parallel",)),
    )(proj, *norms, *weights, *rope)


def _mla_prep_bwd(proj, norms, weights, rope, dq, dk, dv):
    s = proj.shape[0]
    tm = min(256, s)

    def body(blk_ref, *refs):
        ins, (dq_ref, dk_ref, dv_ref) = refs[:11], refs[11:14]
        dblk_ref, dqan_ref, dkvan_ref, dqn_ref, dkn_ref, dwq_ref, dwkv_ref = refs[14:]
        diff, (wq, wk, wv, wqt, wkt, wvt), tables = _prep_operands(blk_ref, ins)

        def fn(q_lat, kv_lat, kr, qan, kvan, qn, kn, wq_, wk_, wv_):
            return _prep_fn(q_lat, kv_lat, kr, qan, kvan, qn, kn, wq_, wk_, wv_, wqt, wkt, wvt,
                            *tables, _mmw)

        _, vjp = jax.vjp(fn, *diff, wq, wk, wv)
        heads = range(MLA_HEADS)
        cts = (tuple(dq_ref[LANES * h:LANES * (h + 1), :] for h in heads),
               tuple(dk_ref[LANES * h:LANES * (h + 1), :] for h in heads), dv_ref[...])
        dq_lat, dkv_lat, dkr, dqan, dkvan, dqn, dkn, dwq_h, dwk_h, dwv = vjp(cts)
        dblk_ref[...] = jnp.transpose(
            jnp.concatenate([dq_lat, dkv_lat, dkr], axis=0)).astype(BF16)

        @pl.when(pl.program_id(0) == 0)
        def _():
            for r in (dqan_ref, dkvan_ref, dqn_ref, dkn_ref, dwq_ref, dwkv_ref):
                r[...] = jnp.zeros_like(r)

        dqan_ref[...] += dqan
        dkvan_ref[...] += dkvan
        dqn_ref[...] += dqn
        dkn_ref[...] += dkn
        for h in heads:
            dwq_ref[LANES * h:LANES * (h + 1), :] += dwq_h[h]
            dwkv_ref[LANES * h:LANES * (h + 1), :] += dwk_h[h]
        dwkv_ref[LANES * MLA_HEADS:, :] += dwv

    const = lambda shape: pl.BlockSpec(shape, lambda i: (0, 0))
    col = lambda height: pl.BlockSpec((height, tm), lambda i: (0, i))
    shapes = [(MLA_Q_LORA, 1), (MLA_KV_LORA, 1), (LANES, 1), (LANES, 1),
              (1024, MLA_Q_LORA), (1536, MLA_KV_LORA)]
    return pl.pallas_call(
        body, name="mla_prep_bwd", grid=(s // tm,),
        in_specs=_prep_in_specs(tm) + [col(1024), col(1024), col(512)],
        out_specs=[pl.BlockSpec((tm, 512), lambda i: (i, 0))] + [const(sh) for sh in shapes],
        out_shape=[jax.ShapeDtypeStruct((s, 512), BF16)]
        + [jax.ShapeDtypeStruct(sh, F32) for sh in shapes],
        compiler_params=_params(("arbitrary",)),
    )(proj, *norms, *weights, *rope, dq, dk, dv)


MLA_SCALE = MLA_QK ** -0.5


LOG2E = 1.4426950408889634
LN2 = 0.6931471805599453
Q_PRESCALE = MLA_SCALE * LOG2E
HEAD_GROUPS = ((0, 1),)


def _mla_attn_fwd(q2, k, vt):
    s = q2.shape[0]
    t = min(512, s)
    tk = min(128, s)
    nq = s // t
    r = t // tk

    def body(q_ref, k_ref, vt_ref, o_ref, lse_ref, acc_ref):
        i = pl.program_id(1)
        row = lax.broadcasted_iota(jnp.int32, (tk, t), 0)
        col = lax.broadcasted_iota(jnp.int32, (tk, t), 1)
        qh = [q_ref[:, LANES * hh:LANES * (hh + 1)] for hh in range(2)]
        acc_ref[...] = jnp.zeros_like(acc_ref)

        def scores(j, heads, diag=None):
            r0 = pl.multiple_of(j * tk, tk)
            out = []
            for hh in heads:
                kc = k_ref[pl.ds(r0, tk), LANES * hh:LANES * (hh + 1)]
                sc = lax.dot_general(kc, qh[hh], (((1,), (1,)), ((), ())),
                                     preferred_element_type=F32)
                out.append(sc if diag is None else jnp.where(row + diag * tk <= col, sc, NEG_INF))
            return tuple(out)

        for heads in HEAD_GROUPS:
            stats = tuple((jnp.full((1, t), NEG_INF, F32), jnp.zeros((1, t), F32)) for _ in heads)

            def consume(j, scs, stats, heads=heads):
                r0 = pl.multiple_of(j * tk, tk)
                out, ps, alphas = [], [], []
                for n, hh in enumerate(heads):
                    m, l = stats[n]
                    m_new = jnp.maximum(m, jnp.max(scs[n], axis=0, keepdims=True))
                    p = jnp.exp2(scs[n] - m_new)
                    alpha = jnp.exp2(m - m_new)
                    out.append((m_new, alpha * l + jnp.sum(p, axis=0, keepdims=True)))
                    ps.append(p.astype(BF16))
                    alphas.append(alpha)
                for n, hh in enumerate(heads):
                    vc = vt_ref[MLA_V * hh:MLA_V * (hh + 1), pl.ds(r0, tk)]
                    acc_ref[hh] = alphas[n] * acc_ref[hh] + jnp.dot(vc, ps[n],
                                                                    preferred_element_type=F32)
                return tuple(out)

            def group(j0, stats, diag, heads=heads):
                scs = [scores(j0 + d, heads, d if diag else None) for d in range(r)]
                for d in range(r):
                    stats = consume(j0 + d, scs[d], stats)
                return stats

            stats = group(r * i, stats, True)
            stats = lax.fori_loop(0, i, lambda j, st: group(r * j, st, False), stats)
            for n, hh in enumerate(heads):
                m, l = stats[n]
                o_ref[:, MLA_V * hh:MLA_V * (hh + 1)] = jnp.transpose(acc_ref[hh] / l)
                lse_ref[0, hh:hh + 1, :] = m + jnp.log2(l)

    return pl.pallas_call(
        body, name="mla_attn_fwd", grid=(MLA_HEADS // 2, nq),
        in_specs=[pl.BlockSpec((t, 256), lambda p, i: (i, p)),
                  pl.BlockSpec((s, 256), lambda p, i: (0, p)),
                  pl.BlockSpec((128, s), lambda p, i: (p, 0))],
        out_specs=[pl.BlockSpec((t, 128), lambda p, i: (i, p)),
                   pl.BlockSpec((1, 2, t), lambda p, i: (p, 0, i))],
        out_shape=[jax.ShapeDtypeStruct((s, 512), F32),
                   jax.ShapeDtypeStruct((MLA_HEADS // 2, 2, s), F32)],
        scratch_shapes=[pltpu.VMEM((2, MLA_V, t), F32)],
        compiler_params=_params(("parallel", "arbitrary")),
    )(q2, k, vt)


def _mla_attn_bwd(q2, q2t, k, kt, v, do, dot, lse_rows, delta_rows):
    s = q2.shape[0]
    t = min(512, s)
    nq = s // t

    def body(q_ref, qt_ref, k_ref, kt_ref, v_ref, do_ref, dot_ref, lse_ref, dl_ref,
             dq_ref, dk_ref, dv_ref):
        j = pl.program_id(1)

        @pl.when(j == 0)
        def _():
            dq_ref[...] = jnp.zeros_like(dq_ref)

        dk_ref[...] = jnp.zeros_like(dk_ref)
        dv_ref[...] = jnp.zeros_like(dv_ref)
        row = lax.broadcasted_iota(jnp.int32, (t, t), 0)
        col = lax.broadcasted_iota(jnp.int32, (t, t), 1)
        causal_t = row <= col
        kh = [k_ref[:, LANES * hh:LANES * (hh + 1)] for hh in range(2)]
        kth = [kt_ref[LANES * hh:LANES * (hh + 1), :] for hh in range(2)]
        vh = [v_ref[:, MLA_V * hh:MLA_V * (hh + 1)] for hh in range(2)]
        nt = (((1,), (1,)), ((), ()))

        def step(i, masked):
            r0 = pl.multiple_of(i * t, t)
            sd = []
            for hh in range(2):
                qh = q_ref[pl.ds(r0, t), LANES * hh:LANES * (hh + 1)]
                doh = do_ref[pl.ds(r0, t), MLA_V * hh:MLA_V * (hh + 1)]
                sc_t = lax.dot_general(kh[hh], qh, nt, preferred_element_type=F32)
                sd.append(jnp.where(causal_t, sc_t, NEG_INF) if masked else sc_t)
                sd.append(lax.dot_general(vh[hh], doh, nt, preferred_element_type=F32))
            for hh in range(2):
                lse = lse_ref[0, hh:hh + 1, pl.ds(r0, t)]
                dl = dl_ref[0, hh:hh + 1, pl.ds(r0, t)]
                p_t = jnp.exp2(sd[2 * hh] - lse)
                g_t = (p_t * (sd[2 * hh + 1] - dl)).astype(BF16)
                qth = qt_ref[LANES * hh:LANES * (hh + 1), pl.ds(r0, t)]
                doth = dot_ref[MLA_V * hh:MLA_V * (hh + 1), pl.ds(r0, t)]
                dv_ref[MLA_V * hh:MLA_V * (hh + 1), :] += lax.dot_general(
                    doth, p_t.astype(BF16), nt, preferred_element_type=F32)
                dk_ref[LANES * hh:LANES * (hh + 1), :] += lax.dot_general(
                    qth, g_t, nt, preferred_element_type=F32)
                dq_ref[LANES * hh:LANES * (hh + 1), pl.ds(r0, t)] += jnp.dot(
                    kth[hh], g_t, preferred_element_type=F32)

        step(j, True)

        def trip(i, carry):
            step(i, False)
            return carry

        lax.fori_loop(j + 1, nq, trip, 0)
        dk_ref[...] = dk_ref[...] * LN2

        @pl.when(j == nq - 1)
        def _():
            dq_ref[...] = dq_ref[...] * MLA_SCALE

    return pl.pallas_call(
        body, name="mla_attn_bwd", grid=(MLA_HEADS // 2, nq),
        in_specs=[pl.BlockSpec((s, 256), lambda p, j: (0, p)),
                  pl.BlockSpec((256, s), lambda p, j: (p, 0)),
                  pl.BlockSpec((t, 256), lambda p, j: (j, p)),
                  pl.BlockSpec((256, t), lambda p, j: (p, j)),
                  pl.BlockSpec((t, 128), lambda p, j: (j, p)),
                  pl.BlockSpec((s, 128), lambda p, j: (0, p)),
                  pl.BlockSpec((128, s), lambda p, j: (p, 0)),
                  pl.BlockSpec((1, 2, s), lambda p, j: (p, 0, 0)),
                  pl.BlockSpec((1, 2, s), lambda p, j: (p, 0, 0))],
        out_specs=[pl.BlockSpec((256, s), lambda p, j: (p, 0)),
                   pl.BlockSpec((256, t), lambda p, j: (p, j)),
                   pl.BlockSpec((128, t), lambda p, j: (p, j))],
        out_shape=[jax.ShapeDtypeStruct((1024, s), F32), jax.ShapeDtypeStruct((1024, s), F32),
                   jax.ShapeDtypeStruct((512, s), F32)],
        compiler_params=_params(("parallel", "arbitrary")),
    )(q2, q2t, k, kt, v, do, dot, lse_rows, delta_rows)


SWA_SCALE = SWA_DIM ** -0.5
SWA_COLS = SWA_GROUP * BLOCK
SWA_LOG2 = SWA_SCALE * LOG2E


def _swa_tables():
    k = np.arange(2 * BLOCK)[:, None]
    col = np.arange(SWA_COLS)[None, :]
    dist = BLOCK + (col % BLOCK) - k
    valid = (dist >= 0) & (dist < BLOCK)
    out = np.zeros((2, SWA_KV, 2 * BLOCK, SWA_COLS), np.float32)
    for first in range(2):
        ok = valid & ((k >= BLOCK) | (first == 0))
        for j in range(SWA_KV):
            slope = 2.0 ** -(SWA_GROUP * j + col // BLOCK + 1)
            out[first, j] = np.where(ok, -slope * dist * LOG2E, NEG_INF)
    return jnp.asarray(out)


def _swa_tile_inputs(sq_ref, skv_ref, halo_ref, qn_ref, kn_ref, sk_ref, add_ref, first):
    tokens = sq_ref.shape[0]
    kv_all = jnp.concatenate([halo_ref[...], skv_ref[...]], axis=0)
    kv_t = jnp.transpose(kv_all)
    sq_t = jnp.transpose(sq_ref[...])
    k_raw = [kv_t[SWA_DIM * j:SWA_DIM * (j + 1)] for j in range(SWA_KV)]
    v_t = [kv_t[128 + SWA_DIM * j:128 + SWA_DIM * (j + 1)] for j in range(SWA_KV)]
    v_nat = [kv_all[:, 128 + SWA_DIM * j:128 + SWA_DIM * (j + 1)] for j in range(SWA_KV)]
    q_raw = [sq_t[SWA_DIM * h:SWA_DIM * (h + 1)] for h in range(SWA_HEADS)]
    qn_b = jnp.broadcast_to(qn_ref[...], (SWA_DIM, tokens))
    kn_b = jnp.broadcast_to(kn_ref[...], (SWA_DIM, tokens + BLOCK))
    lane_grp = lax.broadcasted_iota(jnp.int32, (1, SWA_COLS), 1) // BLOCK
    sinks, adds = [], []
    for j in range(SWA_KV):
        row = jnp.zeros((1, SWA_COLS), F32)
        for g in range(SWA_GROUP):
            h = SWA_GROUP * j + g
            row = jnp.where(lane_grp == g, sk_ref[:, h:h + 1] * LOG2E, row)
        sinks.append(row)
        adds.append((jnp.where(first, add_ref[1, j], add_ref[0, j]), add_ref[0, j]))
    return k_raw, v_t, v_nat, q_raw, qn_b, kn_b, sinks, adds


def _swa_probs(kb, qs_t, add, sink):
    s2 = jnp.dot(kb, qs_t, preferred_element_type=F32) * SWA_LOG2 + add
    m = jnp.maximum(jnp.max(s2, axis=0, keepdims=True), sink)
    e = jnp.exp2(s2 - m)
    es = jnp.exp2(sink - m)
    inv = 1.0 / (jnp.sum(e, axis=0, keepdims=True) + es)
    return e, inv, es


def _swa_queries(qn_t, j, b):
    return jnp.concatenate([qn_t[SWA_GROUP * j + g][:, BLOCK * b:BLOCK * (b + 1)]
                            for g in range(SWA_GROUP)], axis=1)


def _swa_fwd(proj, qn, kn, sinks, tables):
    s = proj.shape[0]
    ts = min(512, s)
    nb = ts // BLOCK

    def body(sq_ref, skv_ref, halo_ref, qn_ref, kn_ref, sk_ref, add_ref, o_ref, ot_ref):
        first = pl.program_id(0) == 0
        k_raw, v_t, _, q_raw, qn_b, kn_b, sink_rows, adds = _swa_tile_inputs(
            sq_ref, skv_ref, halo_ref, qn_ref, kn_ref, sk_ref, add_ref, first)
        kn_nat = [jnp.transpose(_rms0(k, kn_b)).astype(BF16) for k in k_raw]
        v_t = [v.astype(BF16) for v in v_t]
        qn_t = [_rms0(q, qn_b).astype(BF16) for q in q_raw]
        for b in range(nb):
            band = slice(BLOCK * b, BLOCK * (b + 2))
            for j in range(SWA_KV):
                e, inv, _ = _swa_probs(kn_nat[j][band], _swa_queries(qn_t, j, b),
                                       adds[j][0 if b == 0 else 1], sink_rows[j])
                o_t = jnp.dot(v_t[j][:, band], (e * inv).astype(BF16),
                              preferred_element_type=F32)
                for g in range(SWA_GROUP):
                    h = SWA_GROUP * j + g
                    ot_ref[SWA_DIM * h:SWA_DIM * (h + 1), BLOCK * b:BLOCK * (b + 1)] = (
                        o_t[:, BLOCK * g:BLOCK * (g + 1)])
        o_ref[...] = jnp.transpose(ot_ref[...])

    const = lambda shape: pl.BlockSpec(shape, lambda i: (0,) * len(shape))
    return pl.pallas_call(
        body, name="swa_fwd", grid=(s // ts,),
        in_specs=[pl.BlockSpec((ts, 512), lambda i: (i, OFF_SQ // 512)),
                  pl.BlockSpec((ts, 256), lambda i: (i, OFF_SKV // 256)),
                  pl.BlockSpec((BLOCK, 256), lambda i: (jnp.maximum(i * nb - 1, 0), OFF_SKV // 256)),
                  const((SWA_DIM, 1)), const((SWA_DIM, 1)), const((1, SWA_HEADS)),
                  const(tables.shape)],
        out_specs=pl.BlockSpec((ts, 512), lambda i: (i, 0)),
        out_shape=jax.ShapeDtypeStruct((s, 512), F32),
        scratch_shapes=[pltpu.VMEM((512, ts), F32)],
        compiler_params=_params(("parallel",)),
    )(proj, proj, proj, qn, kn, sinks, tables)


def _swa_bwd(proj, qn, kn, sinks, tables, do):
    s = proj.shape[0]
    ts = min(512, s)
    nb = ts // BLOCK
    nt = s // ts

    def body(sq_ref, skv_ref, halo_ref, qn_ref, kn_ref, sk_ref, add_ref, do_ref,
             dsq_ref, dskv_ref, dqn_ref, dkn_ref, dsk_ref, carry_ref, dqt_ref, dkvt_ref):
        step = pl.program_id(0)
        first = step == nt - 1

        @pl.when(step == 0)
        def _():
            carry_ref[...] = jnp.zeros_like(carry_ref)
            dqn_ref[...] = jnp.zeros_like(dqn_ref)
            dkn_ref[...] = jnp.zeros_like(dkn_ref)
            dsk_ref[...] = jnp.zeros_like(dsk_ref)

        k_raw, v_t, v_nat, q_raw, qn_b, kn_b, sink_rows, adds = _swa_tile_inputs(
            sq_ref, skv_ref, halo_ref, qn_ref, kn_ref, sk_ref, add_ref, first)
        kn_f = [_rms0(k, kn_b) for k in k_raw]
        kn_t = [k.astype(BF16) for k in kn_f]
        kn_nat = [jnp.transpose(k).astype(BF16) for k in kn_f]
        v_nat = [v.astype(BF16) for v in v_nat]
        qn_t = [_rms0(q, qn_b).astype(BF16) for q in q_raw]
        do_t = jnp.transpose(do_ref[...].astype(F32)).astype(BF16)

        dkvt_ref[...] = jnp.zeros_like(dkvt_ref)
        dsink = [jnp.zeros((1, SWA_COLS), F32) for _ in range(SWA_KV)]
        nt_dims = (((1,), (1,)), ((), ()))
        for b in range(nb):
            rows = slice(BLOCK * b, BLOCK * (b + 1))
            band = slice(BLOCK * b, BLOCK * (b + 2))
            for j in range(SWA_KV):
                heads = [SWA_GROUP * j + g for g in range(SWA_GROUP)]
                qs_t = _swa_queries(qn_t, j, b)
                dos_t = jnp.concatenate([do_t[SWA_DIM * h:SWA_DIM * (h + 1), rows] for h in heads],
                                        axis=1)
                e, inv, es = _swa_probs(kn_nat[j][band], qs_t, adds[j][0 if b == 0 else 1],
                                        sink_rows[j])
                p = e * inv
                dp = jnp.dot(v_nat[j][band], dos_t, preferred_element_type=F32)
                dsum = jnp.sum(p * dp, axis=0, keepdims=True)
                dsink[j] = dsink[j] - es * inv * dsum
                g_t = (p * (dp - dsum) * SWA_SCALE).astype(BF16)
                dv_t = lax.dot_general(dos_t, p.astype(BF16), nt_dims,
                                       preferred_element_type=F32)
                dk_t = lax.dot_general(qs_t, g_t, nt_dims, preferred_element_type=F32)
                dq_t = jnp.dot(kn_t[j][:, band], g_t, preferred_element_type=F32)
                dkvt_ref[SWA_DIM * j:SWA_DIM * (j + 1), band] += dk_t
                dkvt_ref[128 + SWA_DIM * j:128 + SWA_DIM * (j + 1), band] += dv_t
                for g, h in enumerate(heads):
                    dqt_ref[SWA_DIM * h:SWA_DIM * (h + 1), rows] = dq_t[:, BLOCK * g:BLOCK * (g + 1)]

        dqn = jnp.zeros((SWA_DIM, 1), F32)
        for h in range(SWA_HEADS):
            _, vjp = jax.vjp(_rms0, q_raw[h], qn_ref[...])
            dq, dg = vjp(dqt_ref[SWA_DIM * h:SWA_DIM * (h + 1), :])
            dqt_ref[SWA_DIM * h:SWA_DIM * (h + 1), :] = dq
            dqn = dqn + dg
        dqn_ref[...] += dqn
        dsq_ref[...] = jnp.transpose(dqt_ref[...]).astype(BF16)
        dkn = jnp.zeros((SWA_DIM, 1), F32)
        lane_grp = lax.broadcasted_iota(jnp.int32, (1, SWA_COLS), 1) // BLOCK
        for j in range(SWA_KV):
            _, vjp = jax.vjp(_rms0, k_raw[j], kn_ref[...])
            dk, dg = vjp(dkvt_ref[SWA_DIM * j:SWA_DIM * (j + 1), :])
            dkvt_ref[SWA_DIM * j:SWA_DIM * (j + 1), :] = dk
            dkn = dkn + dg
            for g in range(SWA_GROUP):
                h = SWA_GROUP * j + g
                dsk_ref[:, h:h + 1] += jnp.sum(jnp.where(lane_grp == g, dsink[j], 0.0), axis=1,
                                               keepdims=True)
        dkn_ref[...] += dkn
        dkv = jnp.transpose(dkvt_ref[...])
        dskv_ref[0:ts - BLOCK, :] = dkv[BLOCK:ts].astype(BF16)
        dskv_ref[ts - BLOCK:ts, :] = (dkv[ts:ts + BLOCK] + carry_ref[...]).astype(BF16)
        carry_ref[...] = dkv[0:BLOCK]

    const = lambda shape: pl.BlockSpec(shape, lambda st: (0,) * len(shape))
    return pl.pallas_call(
        body, name="swa_bwd", grid=(nt,),
        in_specs=[pl.BlockSpec((ts, 512), lambda st: (nt - 1 - st, OFF_SQ // 512)),
                  pl.BlockSpec((ts, 256), lambda st: (nt - 1 - st, OFF_SKV // 256)),
                  pl.BlockSpec((BLOCK, 256),
                               lambda st: (jnp.maximum((nt - 1 - st) * nb - 1, 0), OFF_SKV // 256)),
                  const((SWA_DIM, 1)), const((SWA_DIM, 1)), const((1, SWA_HEADS)),
                  const(tables.shape),
                  pl.BlockSpec((ts, 512), lambda st: (nt - 1 - st, 0))],
        out_specs=[pl.BlockSpec((ts, 512), lambda st: (nt - 1 - st, 0)),
                   pl.BlockSpec((ts, 256), lambda st: (nt - 1 - st, 0)),
                   const((SWA_DIM, 1)), const((SWA_DIM, 1)), const((1, SWA_HEADS))],
        out_shape=[jax.ShapeDtypeStruct((s, 512), BF16), jax.ShapeDtypeStruct((s, 256), BF16),
                   jax.ShapeDtypeStruct((SWA_DIM, 1), F32), jax.ShapeDtypeStruct((SWA_DIM, 1), F32),
                   jax.ShapeDtypeStruct((1, SWA_HEADS), F32)],
        scratch_shapes=[pltpu.VMEM((BLOCK, 256), F32), pltpu.VMEM((512, ts), F32),
                        pltpu.VMEM((256, ts + BLOCK), F32)],
        compiler_params=_params(("arbitrary",)),
    )(proj, proj, proj, qn, kn, sinks, tables, do)


HALO = 8


def _shift_down(u, halo, k):
    tm = u.shape[0]
    rid = lax.broadcasted_iota(jnp.int32, u.shape, 0)
    out = pltpu.roll(u, k, 0)
    for r in range(k):
        out = jnp.where(rid == r, halo[HALO - k + r:HALO - k + r + 1, :], out)
    return out


def _shift_up(u, halo, k):
    tm = u.shape[0]
    rid = lax.broadcasted_iota(jnp.int32, u.shape, 0)
    out = pltpu.roll(u, tm - k, 0)
    for r in range(k):
        out = jnp.where(rid == tm - k + r, halo[r:r + 1, :], out)
    return out


def _conv_fwd_vals(conv_ref, convp_ref, cw_ref, is_first):
    c_h, c_b, c_c = conv_ref[:, 0:512], conv_ref[:, 512:1024], conv_ref[:, 1024:1536]
    u = c_c * c_h
    up = jnp.where(is_first, 0.0, convp_ref[:, 1024:1536] * convp_ref[:, 0:512])
    u1 = _shift_down(u, up, 1)
    u2 = _shift_down(u, up, 2)
    yc = cw_ref[0:1, :] * u2 + cw_ref[1:2, :] * u1 + cw_ref[2:3, :] * u
    return c_h, c_b, c_c, u, u1, u2, yc


def _out_fwd(proj, o_mla, o_swa, x, w_out, cw):
    s = proj.shape[0]
    tm = min(512, s)

    def body(conv_ref, convp_ref, gates_ref, om_ref, os_ref, x_ref, w_ref, cw_ref, y_ref, z_ref):
        i = pl.program_id(0)
        _, c_b, _, _, _, _, yc = _conv_fwd_vals(conv_ref, convp_ref, cw_ref, i == 0)
        mix = (om_ref[...], c_b * yc, os_ref[...])
        for n in range(3):
            g = gates_ref[:, GROUP * n:GROUP * (n + 1)]
            z_ref[:, GROUP * n:GROUP * (n + 1)] = (mix[n] * (g * _sigmoid(g))).astype(BF16)
        y_ref[...] = x_ref[...] + jnp.dot(z_ref[...], w_ref[...], preferred_element_type=F32)

    row = lambda width: pl.BlockSpec((tm, width), lambda i: (i, 0))
    return pl.pallas_call(
        body, name="out_fwd", grid=(s // tm,),
        in_specs=[pl.BlockSpec((tm, 1536), lambda i: (i, 0)),
                  pl.BlockSpec((HALO, 1536), lambda i: (jnp.maximum(i * (tm // HALO) - 1, 0), 0)),
                  pl.BlockSpec((tm, 1536), lambda i: (i, 1)),
                  row(512), row(512), row(D_MODEL),
                  pl.BlockSpec((D_MIX, D_MODEL), lambda i: (0, 0)),
                  pl.BlockSpec((HALO, 512), lambda i: (0, 0))],
        out_specs=[row(D_MODEL), row(D_MIX)],
        out_shape=[jax.ShapeDtypeStruct((s, D_MODEL), F32), jax.ShapeDtypeStruct((s, D_MIX), BF16)],
        compiler_params=_params(("parallel",)),
    )(proj, proj, proj, o_mla, o_swa, x, w_out, cw)


def _out_bwd(dy, proj, o_mla, o_swa, w_out, cw):
    s = proj.shape[0]
    tm = min(512, s)
    nt = s // tm
    hb = tm // HALO

    def body(dy_ref, dyn_ref, conv_ref, convp_ref, convn_ref, gates_ref, gatesn_ref, om_ref, os_ref,
             w_ref, cw_ref,
             dconv_ref, dgates_ref, dom_ref, domt_ref, delta_ref, dos_ref, dcw_ref):
        i = pl.program_id(0)
        dz = _mm_nt(dy_ref[...], w_ref[...])

        def gate(n):
            g = gates_ref[:, GROUP * n:GROUP * (n + 1)]
            sg = _sigmoid(g)
            return g * sg, sg * (1.0 + g * (1.0 - sg))

        for n, o_ref, do_ref in ((0, om_ref, dom_ref), (2, os_ref, dos_ref)):
            silu, dsilu = gate(n)
            dzn = dz[:, GROUP * n:GROUP * (n + 1)]
            o = o_ref[...]
            do = dzn * silu
            do_ref[...] = do.astype(do_ref.dtype)
            dgates_ref[:, GROUP * n:GROUP * (n + 1)] = (dzn * o * dsilu).astype(BF16)
            if n == 0:
                domt_ref[...] = jnp.transpose(do).astype(BF16)
                t = do * o
                for h in range(MLA_HEADS):
                    delta_ref[:, h:h + 1] = jnp.sum(t[:, MLA_V * h:MLA_V * (h + 1)], axis=-1,
                                                    keepdims=True)

        c_h, c_b, c_c, u, u1, u2, yc = _conv_fwd_vals(conv_ref, convp_ref, cw_ref, i == 0)
        silu, dsilu = gate(1)
        dzc = dz[:, GROUP:2 * GROUP]
        dgates_ref[:, GROUP:2 * GROUP] = (dzc * (c_b * yc) * dsilu).astype(BF16)
        dycr = dzc * silu
        dyc = dycr * c_b
        gn = gatesn_ref[:, GROUP:2 * GROUP]
        dzc_n = _mm_nt(dyn_ref[...], w_ref[GROUP:2 * GROUP, :])
        dyc_n = jnp.where(i == nt - 1, 0.0, dzc_n * (gn * _sigmoid(gn)) * convn_ref[:, 512:1024])
        d1 = _shift_up(dyc, dyc_n, 1)
        d2 = _shift_up(dyc, dyc_n, 2)
        du = cw_ref[2:3, :] * dyc + cw_ref[1:2, :] * d1 + cw_ref[0:1, :] * d2
        dconv_ref[:, 0:512] = (du * c_c).astype(BF16)
        dconv_ref[:, 512:1024] = (dycr * yc).astype(BF16)
        dconv_ref[:, 1024:1536] = (du * c_h).astype(BF16)

        @pl.when(i == 0)
        def _():
            dcw_ref[...] = jnp.zeros_like(dcw_ref)

        for k, uk in enumerate((u2, u1, u)):
            dcw_ref[k:k + 1, :] += jnp.sum(dyc * uk, axis=0, keepdims=True)

    row = lambda width: pl.BlockSpec((tm, width), lambda i: (i, 0))
    prev = lambda i: jnp.maximum(i * hb - 1, 0)
    nxt = lambda i: jnp.minimum((i + 1) * hb, s // HALO - 1)
    return pl.pallas_call(
        body, name="out_bwd", grid=(nt,),
        in_specs=[row(D_MODEL),
                  pl.BlockSpec((HALO, D_MODEL), lambda i: (nxt(i), 0)),
                  pl.BlockSpec((tm, 1536), lambda i: (i, 0)),
                  pl.BlockSpec((HALO, 1536), lambda i: (prev(i), 0)),
                  pl.BlockSpec((HALO, 1536), lambda i: (nxt(i), 0)),
                  pl.BlockSpec((tm, 1536), lambda i: (i, 1)),
                  pl.BlockSpec((HALO, 1536), lambda i: (nxt(i), 1)),
                  row(512), row(512),
                  pl.BlockSpec((D_MIX, D_MODEL), lambda i: (0, 0)),
                  pl.BlockSpec((HALO, 512), lambda i: (0, 0))],
        out_specs=[row(1536), row(1536), row(512), pl.BlockSpec((512, tm), lambda i: (0, i)),
                   row(MLA_HEADS), row(512), pl.BlockSpec((HALO, 512), lambda i: (0, 0))],
        out_shape=[jax.ShapeDtypeStruct((s, 1536), BF16), jax.ShapeDtypeStruct((s, 1536), BF16),
                   jax.ShapeDtypeStruct((s, 512), BF16), jax.ShapeDtypeStruct((512, s), BF16),
                   jax.ShapeDtypeStruct((s, MLA_HEADS), F32),
                   jax.ShapeDtypeStruct((s, 512), BF16), jax.ShapeDtypeStruct((HALO, 512), F32)],
        compiler_params=_params(("arbitrary",)),
    )(dy, dy, proj, proj, proj, proj, proj, o_mla, o_swa, w_out, cw)


def _loss_head(y, target):
    s, d = y.shape
    tm = min(512, s)
    nt = s // tm

    def body(y_ref, t_ref, dy_ref, loss_ref):
        i = pl.program_id(0)
        err = y_ref[...] - t_ref[...]
        dy_ref[...] = err * (1.0 / d)

        @pl.when(i == 0)
        def _():
            loss_ref[...] = jnp.zeros_like(loss_ref)

        sq = jnp.sum((err * err).reshape(tm // 8, 8, d), axis=0)
        part = sq[:, 0:LANES]
        for c in range(1, d // LANES):
            part = part + sq[:, LANES * c:LANES * (c + 1)]
        loss_ref[...] += part

        @pl.when(i == nt - 1)
        def _():
            loss_ref[...] = jnp.full(loss_ref.shape, (0.5 / d) * jnp.sum(loss_ref[...]), F32)

    return pl.pallas_call(
        body, name="loss_head", grid=(nt,),
        in_specs=[pl.BlockSpec((tm, d), lambda i: (i, 0)), pl.BlockSpec((tm, d), lambda i: (i, 0))],
        out_specs=[pl.BlockSpec((tm, d), lambda i: (i, 0)), pl.BlockSpec((8, LANES), lambda i: (0, 0))],
        out_shape=[jax.ShapeDtypeStruct((s, d), F32), jax.ShapeDtypeStruct((8, LANES), F32)],
        compiler_params=_params(("arbitrary",)),
    )(y, target)


def _adamw(g, w, m, v):
    rows = g.shape[0]
    tr = min(256, rows)
    c1 = 1.0 - ADAM_B1
    c2 = 1.0 - ADAM_B2
    bc1 = 1.0 - ADAM_B1 ** ADAM_STEP
    bc2 = 1.0 - ADAM_B2 ** ADAM_STEP

    def body(g_ref, w_ref, m_ref, v_ref, d_ref, mo_ref, vo_ref):
        gg = g_ref[...]
        m_new = ADAM_B1 * m_ref[...] + c1 * gg
        v_new = ADAM_B2 * v_ref[...] + c2 * (gg * gg)
        m_hat = m_new / bc1
        v_hat = v_new / bc2
        d_ref[...] = -ADAM_LR * (m_hat / (jnp.sqrt(v_hat) + ADAM_EPS) + ADAM_WD * w_ref[...])
        mo_ref[...] = m_new
        vo_ref[...] = v_new

    spec = pl.BlockSpec((tr, g.shape[1]), lambda i: (i, 0))
    return pl.pallas_call(
        body, name="adamw", grid=(rows // tr,),
        in_specs=[spec] * 4, out_specs=[spec] * 3,
        out_shape=[jax.ShapeDtypeStruct(g.shape, F32)] * 3,
        compiler_params=_params(("parallel",)),
    )(g, w, m, v)


HBM_SPEC = pl.BlockSpec(memory_space=pltpu.HBM)


def _place():
    x, y, c = lax.axis_index("x"), lax.axis_index("y"), lax.axis_index("c")
    chips = [(1 - x, y), (x, 1 - y), (1 - x, 1 - y)]
    return x, y, c, chips


def _all_gather(shards):
    na = len(shards)
    halves = [sh.shape[0] // 2 for sh in shards]

    def body(*refs):
        w_refs, a_refs = refs[:na], refs[na:2 * na]
        send_sems, recv_sems = refs[2 * na:]
        x, y, c, chips = _place()
        k = 2 * x + y
        sib = (x, y, 1 - c)

        def slab(a, kk, hc):
            return a_refs[a].at[kk, pl.ds(hc * halves[a], halves[a]), :]

        def copy(a, n, src, dst, to):
            return pltpu.make_async_remote_copy(
                src_ref=src, dst_ref=dst, send_sem=send_sems.at[6 * a + n],
                recv_sem=recv_sems.at[6 * a + n], device_id=to, device_id_type=MESH)

        first = [copy(a, n, w_refs[a].at[pl.ds(c * halves[a], halves[a]), :], slab(a, k, c),
                      (cx, cy, c))
                 for n, (cx, cy) in enumerate(chips) for a in range(na)]
        for cp in first:
            cp.start()
        passed = []
        for n, (cx, cy) in enumerate(chips):
            kk = 2 * cx + cy
            for a in range(na):
                copy(a, n, slab(a, kk, c), slab(a, kk, c), (cx, cy, c)).wait_recv()
                fwd = copy(a, 3 + n, slab(a, kk, c), slab(a, kk, c), sib)
                fwd.start()
                passed.append(fwd)
        for n, (cx, cy) in enumerate(chips):
            kk = 2 * cx + cy
            for a in range(na):
                copy(a, 3 + n, slab(a, kk, 1 - c), slab(a, kk, 1 - c), sib).wait_recv()
        for cp in first + passed:
            cp.wait_send()

    return pl.pallas_call(
        body, name="weights_all_gather",
        in_specs=[HBM_SPEC] * na, out_specs=[HBM_SPEC] * na,
        out_shape=[jax.ShapeDtypeStruct((N_CHIPS,) + sh.shape, sh.dtype) for sh in shards],
        scratch_shapes=[pltpu.SemaphoreType.DMA((6 * na,)), pltpu.SemaphoreType.DMA((6 * na,))],
    )(*shards)


def _fill_own_slab(buf, src, k_idx):
    n, rows, cols = buf.shape
    tr = _row_tile(rows)
    slabs = src.ndim == 3

    def body(k_ref, src_ref, buf_ref, out_ref):
        out_ref[0] = src_ref[0] if slabs else src_ref[...]

    if slabs:
        src_spec = pl.BlockSpec((1, tr, cols), lambda t, k_ref: (k_ref[0], t, 0))
    else:
        src_spec = pl.BlockSpec((tr, cols), lambda t, k_ref: (t, 0))
    return pl.pallas_call(
        body, name="fill_own_slab",
        grid_spec=pltpu.PrefetchScalarGridSpec(
            num_scalar_prefetch=1, grid=(rows // tr,),
            in_specs=[src_spec, pl.BlockSpec(memory_space=pl.ANY)],
            out_specs=pl.BlockSpec((1, tr, cols), lambda t, k_ref: (k_ref[0], t, 0))),
        out_shape=jax.ShapeDtypeStruct(buf.shape, buf.dtype),
        input_output_aliases={2: 0},
        compiler_params=_params(("parallel",)),
    )(k_idx, src, buf)


def _swap_halves_to_sibling(gs):
    na = len(gs)

    def body(*refs):
        g_refs, r_refs = refs[:na], refs[na:2 * na]
        send_sems, recv_sems = refs[2 * na:]
        x, y, c, _ = _place()
        cps = []
        for a in range(na):
            half = g_refs[a].shape[1] // 2
            cps.append(pltpu.make_async_remote_copy(
                src_ref=g_refs[a].at[:, pl.ds((1 - c) * half, half), :], dst_ref=r_refs[a],
                send_sem=send_sems.at[a], recv_sem=recv_sems.at[a], device_id=(x, y, 1 - c),
                device_id_type=MESH))
        for cp in cps:
            cp.start()
        for cp in cps:
            cp.wait()

    return pl.pallas_call(
        body, name="grads_to_sibling",
        in_specs=[HBM_SPEC] * na, out_specs=[HBM_SPEC] * na,
        out_shape=[jax.ShapeDtypeStruct((g.shape[0], g.shape[1] // 2, g.shape[2]), g.dtype)
                   for g in gs],
        scratch_shapes=[pltpu.SemaphoreType.DMA((na,)), pltpu.SemaphoreType.DMA((na,))],
    )(*gs)


def _row_tile(rows):
    return 256 if rows % 256 == 0 else 128


def _add_sibling(g, r, c_idx, out_dtype):
    n, rows, cols = g.shape
    half = rows // 2
    tr = _row_tile(half)
    nb = half // tr

    def body(c_ref, g_ref, r_ref, p_ref):
        p_ref[...] = (g_ref[...] + r_ref[...]).astype(out_dtype)

    return pl.pallas_call(
        body, name="grads_add_sibling",
        grid_spec=pltpu.PrefetchScalarGridSpec(
            num_scalar_prefetch=1, grid=(n, nb),
            in_specs=[pl.BlockSpec((1, tr, cols), lambda j, t, c_ref: (j, c_ref[0] * nb + t, 0)),
                      pl.BlockSpec((1, tr, cols), lambda j, t, c_ref: (j, t, 0))],
            out_specs=pl.BlockSpec((1, tr, cols), lambda j, t, c_ref: (j, t, 0))),
        out_shape=jax.ShapeDtypeStruct((n, half, cols), out_dtype),
        compiler_params=_params(("parallel", "parallel")),
    )(c_idx, g, r)


def _scatter_to_chips(ps):
    na = len(ps)

    def body(*refs):
        p_refs, q_refs = refs[:na], refs[na:2 * na]
        send_sems, recv_sems = refs[2 * na:]
        x, y, c, chips = _place()
        k = 2 * x + y
        sends = []
        for i, (cx, cy) in enumerate(chips):
            for a in range(na):
                cp = pltpu.make_async_remote_copy(
                    src_ref=p_refs[a].at[2 * cx + cy], dst_ref=q_refs[a].at[k],
                    send_sem=send_sems.at[3 * a + i], recv_sem=recv_sems.at[3 * a + i],
                    device_id=(cx, cy, c), device_id_type=MESH)
                cp.start()
                sends.append(cp)
        for i, (cx, cy) in enumerate(chips):
            kk = 2 * cx + cy
            for a in range(na):
                pltpu.make_async_remote_copy(
                    src_ref=p_refs[a].at[kk], dst_ref=q_refs[a].at[kk],
                    send_sem=send_sems.at[3 * a + i], recv_sem=recv_sems.at[3 * a + i],
                    device_id=(cx, cy, c), device_id_type=MESH).wait_recv()
        for cp in sends:
            cp.wait_send()

    return pl.pallas_call(
        body, name="grads_scatter_to_chips",
        in_specs=[HBM_SPEC] * na, out_specs=[HBM_SPEC] * na,
        out_shape=[jax.ShapeDtypeStruct(p.shape, p.dtype) for p in ps],
        scratch_shapes=[pltpu.SemaphoreType.DMA((3 * na,)), pltpu.SemaphoreType.DMA((3 * na,))],
    )(*ps)


def _sum_chips(q, c_idx):
    n, half, cols = q.shape
    tr = _row_tile(half)
    nb = half // tr

    def body(c_ref, q_ref, o_ref):
        parts = [q_ref[kk].astype(F32) for kk in range(n)]
        o_ref[...] = ((parts[0] + parts[1]) + parts[2]) + parts[3]

    return pl.pallas_call(
        body, name="grads_sum_chips",
        grid_spec=pltpu.PrefetchScalarGridSpec(
            num_scalar_prefetch=1, grid=(nb,),
            in_specs=[pl.BlockSpec((n, tr, cols), lambda t, c_ref: (0, t, 0))],
            out_specs=pl.BlockSpec((tr, cols), lambda t, c_ref: (c_ref[0] * nb + t, 0))),
        out_shape=jax.ShapeDtypeStruct((2 * half, cols), F32),
        compiler_params=_params(("parallel",)),
    )(c_idx, q)


def _join_halves(fulls):
    na = len(fulls)

    def body(*refs):
        o_refs = refs[na:2 * na]
        send_sems, recv_sems = refs[2 * na:]
        x, y, c, _ = _place()
        sends = []
        for a in range(na):
            half = o_refs[a].shape[0] // 2
            rows = o_refs[a].at[pl.ds(c * half, half), :]
            sends.append(pltpu.make_async_remote_copy(
                src_ref=rows, dst_ref=rows, send_sem=send_sems.at[a], recv_sem=recv_sems.at[a],
                device_id=(x, y, 1 - c), device_id_type=MESH))
        for cp in sends:
            cp.start()
        for a in range(na):
            half = o_refs[a].shape[0] // 2
            other = o_refs[a].at[pl.ds((1 - c) * half, half), :]
            pltpu.make_async_remote_copy(
                src_ref=other, dst_ref=other, send_sem=send_sems.at[a], recv_sem=recv_sems.at[a],
                device_id=(x, y, 1 - c), device_id_type=MESH).wait_recv()
        for cp in sends:
            cp.wait_send()

    return pl.pallas_call(
        body, name="grads_join_halves",
        in_specs=[HBM_SPEC] * na, out_specs=[HBM_SPEC] * na,
        out_shape=[jax.ShapeDtypeStruct(f.shape, f.dtype) for f in fulls],
        input_output_aliases={a: a for a in range(na)},
        scratch_shapes=[pltpu.SemaphoreType.DMA((na,)), pltpu.SemaphoreType.DMA((na,))],
    )(*fulls)


def _part_rows(shape):
    size = 1
    for d in shape:
        size *= d
    rows = -(-size // PACK_COLS)
    return size, -(-rows // PACK_ROW_ALIGN) * PACK_ROW_ALIGN


def _pack_rows(arrays, dtype, total_rows):
    parts, used = [], 0
    for a in arrays:
        size, rows = _part_rows(a.shape)
        flat = a.reshape(-1).astype(dtype)
        parts.append(jnp.pad(flat, (0, rows * PACK_COLS - size)).reshape(rows, PACK_COLS))
        used += rows
    parts.append(jnp.zeros((total_rows - used, PACK_COLS), dtype))
    return jnp.concatenate(parts, axis=0)


def _unpack_rows(buf, shapes):
    lead = buf.shape[:-2]
    out, off = [], 0
    for sh in shapes:
        size, rows = _part_rows(sh)
        part = buf[..., off:off + rows, :].reshape(lead + (-1,))[..., :size]
        out.append(part.reshape(lead + tuple(sh)))
        off += rows
    return out


NEW_ORDER = ((928, 1440), (1440, 1952), (1952, 2464), (416, 928), (2464, 2976), (3744, 4256),
             (2976, 3488), (0, 256), (256, 384), (4256, 4320), (384, 416), (4256, 4288),
             (3488, 3616), (3616, 3744))
OLD_ORDER = ((3584, 3840), (3840, 3968), (4032, 4064), (1536, 2048), (0, 512), (512, 1024),
             (1024, 1536), (2048, 2560), (3072, 3584), (4096, 4224), (4224, 4352), (2560, 3072))


def _cols(sources, ranges):
    parts = []
    for a, b in ranges:
        off = 0
        for src in sources:
            width = src.shape[-1]
            lo, hi = max(a, off), min(b, off + width)
            if lo < hi:
                parts.append(src[..., lo - off:hi - off])
            off += width
    return jnp.concatenate(parts, axis=-1)


def _sub_ranges(ranges, a, b):
    out, off = [], 0
    for lo, hi in ranges:
        width = hi - lo
        s0, s1 = max(a, off), min(b, off + width)
        if s0 < s1:
            out.append((lo + s0 - off, lo + s1 - off))
        off += width
    return out


def _rope_tables(s):
    half = MLA_ROPE // 2
    inv_freq = jnp.power(jnp.float32(ROPE_THETA), -jnp.arange(half, dtype=F32) / half)
    ang = inv_freq[:, None] * jnp.arange(s, dtype=F32)[None, :]
    cos, sin = jnp.cos(ang), jnp.sin(ang)
    z = lambda n: jnp.zeros((n, s), F32)
    c = jnp.concatenate([jnp.ones((MLA_NOPE, s), F32), cos, cos, z(32)], axis=0)
    s1 = jnp.concatenate([z(MLA_NOPE), -sin, z(16), z(32)], axis=0)
    s2 = jnp.concatenate([z(MLA_NOPE), z(16), sin, z(32)], axis=0)
    return c, s1, s2


def _pad_lanes(a, n):
    return jnp.pad(a, ((0, 0), (0, n - a.shape[1])))


SHARDED = ("w_in", "w_out", "mla_w_qb", "mla_w_kvb", "conv_w")
REPLICATED = ("norm_g", "mla_q_a_norm", "mla_kv_a_norm", "mla_q_norm", "mla_k_norm",
              "swa_q_norm", "swa_k_norm", "swa_sinks")
WEIGHT_ORDER = ("norm_g", "w_in", "mla_q_a_norm", "mla_w_qb", "mla_kv_a_norm", "mla_w_kvb",
                "mla_q_norm", "mla_k_norm", "conv_w", "swa_q_norm", "swa_k_norm", "swa_sinks", "w_out")
SHARD_AXIS = {"w_in": 2, "w_out": 1, "mla_w_qb": 2, "mla_w_kvb": 2, "conv_w": 2}


def kernel(x, norm_g, w_in, mla_q_a_norm, mla_w_qb, mla_kv_a_norm, mla_w_kvb, mla_q_norm, mla_k_norm, conv_w, swa_q_norm, swa_k_norm, swa_sinks, w_out, loss_target, m_norm_g, m_w_in, m_mla_q_a_norm, m_mla_w_qb, m_mla_kv_a_norm, m_mla_w_kvb, m_mla_q_norm, m_mla_k_norm, m_conv_w, m_swa_q_norm, m_swa_k_norm, m_swa_sinks, m_w_out, v_norm_g, v_w_in, v_mla_q_a_norm, v_mla_w_qb, v_mla_kv_a_norm, v_mla_w_kvb, v_mla_q_norm, v_mla_k_norm, v_conv_w, v_swa_q_norm, v_swa_k_norm, v_swa_sinks, v_w_out):
    weights = dict(norm_g=norm_g, w_in=w_in, mla_q_a_norm=mla_q_a_norm, mla_w_qb=mla_w_qb,
                   mla_kv_a_norm=mla_kv_a_norm, mla_w_kvb=mla_w_kvb, mla_q_norm=mla_q_norm,
                   mla_k_norm=mla_k_norm, conv_w=conv_w, swa_q_norm=swa_q_norm,
                   swa_k_norm=swa_k_norm, swa_sinks=swa_sinks, w_out=w_out)
    mom_m = dict(norm_g=m_norm_g, w_in=m_w_in, mla_q_a_norm=m_mla_q_a_norm, mla_w_qb=m_mla_w_qb,
                 mla_kv_a_norm=m_mla_kv_a_norm, mla_w_kvb=m_mla_w_kvb, mla_q_norm=m_mla_q_norm,
                 mla_k_norm=m_mla_k_norm, conv_w=m_conv_w, swa_q_norm=m_swa_q_norm,
                 swa_k_norm=m_swa_k_norm, swa_sinks=m_swa_sinks, w_out=m_w_out)
    mom_v = dict(norm_g=v_norm_g, w_in=v_w_in, mla_q_a_norm=v_mla_q_a_norm, mla_w_qb=v_mla_w_qb,
                 mla_kv_a_norm=v_mla_kv_a_norm, mla_w_kvb=v_mla_w_kvb, mla_q_norm=v_mla_q_norm,
                 mla_k_norm=v_mla_k_norm, conv_w=v_conv_w, swa_q_norm=v_swa_q_norm,
                 swa_k_norm=v_swa_k_norm, swa_sinks=v_swa_sinks, w_out=v_w_out)
    xs = x[0]
    target = loss_target[0]
    s = xs.shape[0]
    c_idx = lax.axis_index("c").astype(jnp.int32).reshape(1)
    k_idx = (2 * lax.axis_index("x") + lax.axis_index("y")).astype(jnp.int32).reshape(1)

    conv_bits = lax.bitcast_convert_type(conv_w, BF16)
    small_list = [w_out, mla_w_qb, mla_w_kvb, conv_bits]
    w_in_rows = DEPTH * D_MODEL
    own = [w_in.astype(BF16).reshape(w_in_rows, w_in.shape[2]), _pack_rows(small_list, BF16, PACK_ROWS)]
    gathered_in, gathered_rest = [_fill_own_slab(buf, src, k_idx)
                                  for buf, src in zip(_all_gather(own), own)]
    parts = _unpack_rows(gathered_rest, [a.shape for a in small_list])
    join = lambda p, axis: jnp.concatenate([p[k] for k in range(N_CHIPS)], axis=axis)
    w_in_slabs = gathered_in.reshape(N_CHIPS, DEPTH, D_MODEL, w_in.shape[2])
    w_in_zeros = jnp.zeros((D_MODEL, 64), BF16)
    w_out_full = join(parts[0], 1)
    w_qb_full = join(parts[1], 2)
    w_kvb_full = join(parts[2], 2)
    conv_full = lax.bitcast_convert_type(join(parts[3], 2), F32)

    rope = _rope_tables(s)
    swa_tables = _swa_tables()
    layers = []
    for l in range(DEPTH):
        wq = jnp.pad(w_qb_full[l].reshape(MLA_Q_LORA, MLA_HEADS, MLA_QK),
                     ((0, 0), (0, 0), (0, LANES - MLA_QK))).reshape(MLA_Q_LORA, MLA_HEADS * LANES)
        kv = w_kvb_full[l].reshape(MLA_KV_LORA, MLA_HEADS, MLA_NOPE + MLA_V)
        wk = jnp.pad(kv[:, :, :MLA_NOPE], ((0, 0), (0, 0), (0, LANES - MLA_NOPE)))
        wkv = jnp.concatenate([wk.reshape(MLA_KV_LORA, MLA_HEADS * LANES),
                               kv[:, :, MLA_NOPE:].reshape(MLA_KV_LORA, MLA_HEADS * MLA_V)], axis=1)
        layers.append(dict(
            w_in=_cols([w_in_slabs[k, l] for k in range(N_CHIPS)] + [w_in_zeros], NEW_ORDER),
            w_out=w_out_full[l], wq=wq, wkv=wkv,
            cw=jnp.pad(conv_full[l], ((0, HALO - 3), (0, 0))),
            g=norm_g[l][None],
            mla_norms=(mla_q_a_norm[l][:, None], mla_kv_a_norm[l][:, None],
                       _pad_lanes(mla_q_norm[l][None], LANES).T, _pad_lanes(mla_k_norm[l][None], LANES).T),
            mla_weights=(wq, wkv, wq.T, wkv.T),
            sqn=swa_q_norm[l][:, None], skn=swa_k_norm[l][:, None], sinks=swa_sinks[l][None]))

    saved = []
    h_in = xs
    for l in range(DEPTH):
        p = layers[l]
        proj, hb = _in_proj_fwd(h_in, p["g"], p["w_in"])
        q, k, v, qt, kt, vt = _mla_prep_fwd(proj, p["mla_norms"], p["mla_weights"], rope)
        o_mla, lse = _mla_attn_fwd(q, k, vt)
        o_swa = _swa_fwd(proj, p["sqn"], p["skn"], p["sinks"], swa_tables)
        y, z = _out_fwd(proj, o_mla, o_swa, h_in, p["w_out"], p["cw"])
        saved.append(dict(x=h_in, proj=proj, hb=hb, q=q, k=k, v=v, qt=qt, kt=kt, o_mla=o_mla, lse=lse,
                          o_swa=o_swa, z=z))
        h_in = y

    dy, loss_acc = _loss_head(h_in, target)
    loss = lax.psum(loss_acc[0, 0], ("x", "y", "c"))

    grads = {n: [None] * DEPTH for n in WEIGHT_ORDER}
    for l in reversed(range(DEPTH)):
        p, a = layers[l], saved[l]
        dconv, dgates, do_mla, do_mla_t, delta, do_swa, dcw = _out_bwd(dy, a["proj"], a["o_mla"], a["o_swa"],
                                                             p["w_out"], p["cw"])
        grads["w_out"][l] = _matmul_tn(a["z"], dy, "dw_out")
        grads["conv_w"][l] = dcw[0:3]
        delta_rows = jnp.transpose(delta, (1, 0)).reshape(MLA_HEADS // 2, 2, s)
        dq, dk, dv = _mla_attn_bwd(a["q"], a["qt"], a["k"], a["kt"], a["v"], do_mla, do_mla_t,
                                   a["lse"], delta_rows)
        dmla, dqan, dkvan, dqn, dkn, dwq_t, dwkv_t = _mla_prep_bwd(
            a["proj"], p["mla_norms"], p["mla_weights"], rope, dq, dk, dv)
        dwq, dwkv = dwq_t.T, dwkv_t.T
        dsq, dskv, dsqn, dskn, dsinks = _swa_bwd(a["proj"], p["sqn"], p["skn"], p["sinks"], swa_tables, do_swa)
        pieces = [dconv, dgates, dsq, dmla, dskv]
        dx, dg = _in_proj_bwd(pieces, a["x"], p["g"], p["w_in"], dy)
        grads["w_in"][l] = [_matmul_tn(a["hb"], pc, "dw_in_%d" % n) for n, pc in enumerate(pieces)]
        grads["norm_g"][l] = dg[0]
        grads["mla_q_a_norm"][l] = dqan[:, 0]
        grads["mla_kv_a_norm"][l] = dkvan[:, 0]
        grads["mla_q_norm"][l] = dqn[:MLA_QK, 0]
        grads["mla_k_norm"][l] = dkn[:MLA_QK, 0]
        grads["mla_w_qb"][l] = dwq.reshape(MLA_Q_LORA, MLA_HEADS, LANES)[:, :, :MLA_QK].reshape(
            MLA_Q_LORA, MLA_HEADS * MLA_QK)
        dwk = dwkv[:, :MLA_HEADS * LANES].reshape(MLA_KV_LORA, MLA_HEADS, LANES)[:, :, :MLA_NOPE]
        dwv = dwkv[:, MLA_HEADS * LANES:].reshape(MLA_KV_LORA, MLA_HEADS, MLA_V)
        grads["mla_w_kvb"][l] = jnp.concatenate([dwk, dwv], axis=2).reshape(
            MLA_KV_LORA, MLA_HEADS * (MLA_NOPE + MLA_V))
        grads["swa_q_norm"][l] = dsqn[:, 0]
        grads["swa_k_norm"][l] = dskn[:, 0]
        grads["swa_sinks"][l] = dsinks[0]
        dy = dx
    grad_x = dy[None]
    full_grads = {n: jnp.stack(grads[n]) for n in WEIGHT_ORDER if n != "w_in"}

    rest = tuple(n for n in SHARDED if n != "w_in")
    rep_shapes = [weights[n].shape for n in REPLICATED]
    flat_rep = lambda d: jnp.concatenate([d[n].reshape(-1) for n in REPLICATED])

    def chunk(g, n, k):
        width = g.shape[SHARD_AXIS[n]] // N_CHIPS
        return lax.slice_in_dim(g, k * width, (k + 1) * width, axis=SHARD_AXIS[n])

    shard_cols = w_in.shape[2]
    g_in = jnp.stack([
        jnp.concatenate([_cols(grads["w_in"][l], _sub_ranges(OLD_ORDER, k * shard_cols,
                                                             (k + 1) * shard_cols))
                         for l in range(DEPTH)], axis=0)
        for k in range(N_CHIPS)])
    rep_grads = flat_rep(full_grads)
    g_rest = jnp.stack([_pack_rows([chunk(full_grads[n], n, k) for n in rest] + [rep_grads],
                                   F32, PACK_ROWS) for k in range(N_CHIPS)])
    from_sibling = _swap_halves_to_sibling([g_in, g_rest])
    partial = [_add_sibling(g, r, c_idx, dt)
               for g, r, dt in zip((g_in, g_rest), from_sibling, (BF16, F32))]
    by_chip = [_fill_own_slab(q, p, k_idx) for q, p in zip(_scatter_to_chips(partial), partial)]
    g_in_mine, g_rest_mine = _join_halves([_sum_chips(q, c_idx) for q in by_chip])

    pack_rest = lambda d: _pack_rows([d[n] for n in rest] + [flat_rep(d)], F32, PACK_ROWS)
    in_shape = w_in.shape
    res_in = _adamw(g_in_mine, w_in.reshape(w_in_rows, -1), m_w_in.reshape(w_in_rows, -1),
                    v_w_in.reshape(w_in_rows, -1))
    res_rest = _adamw(g_rest_mine, pack_rest(weights), pack_rest(mom_m), pack_rest(mom_v))
    rest_shapes = [weights[n].shape for n in rest] + [(rep_grads.shape[0],)]
    unpacked = []
    for buf_in, buf_rest in zip((g_in_mine,) + tuple(res_in), (g_rest_mine,) + tuple(res_rest)):
        vals = _unpack_rows(buf_rest, rest_shapes)
        group = dict(zip(rest, vals[:-1]))
        group["w_in"] = buf_in.reshape(in_shape)
        off = 0
        for n, sh in zip(REPLICATED, rep_shapes):
            size = sh[0] * sh[1]
            group[n] = vals[-1][off:off + size].reshape(sh)
            off += size
        unpacked.append(group)
    outs = [loss, grad_x]
    for group in unpacked:
        outs += [group[n] for n in WEIGHT_ORDER]
    return tuple(outs)
```

```python
import jax
import numpy as np
import jax.numpy as jnp
from jax import lax
from jax.experimental import pallas as pl
from jax.experimental.pallas import tpu as pltpu

F32 = jnp.float32
BF16 = jnp.bfloat16

D_MODEL = 1024
DEPTH = 2
GROUP = 512
D_MIX = 3 * GROUP
BLOCK = 128
RMS_EPS = 1e-6
NEG_INF = -1e30
MLA_HEADS = 8
MLA_QK = 96
MLA_NOPE = 64
MLA_ROPE = 32
MLA_V = 64
V_AUG = 80
MLA_Q_LORA = 256
MLA_KV_LORA = 128
ROPE_THETA = 10000.0
SWA_HEADS = 8
SWA_KV = 2
SWA_GROUP = 4
SWA_DIM = 64
N_CHIPS = 4

NC = 4352
OFF_SQ, OFF_MLA, OFF_SKV = 3072, 3584, 4096

VMEM_LIMIT = 56 * 1024 * 1024
LANES = 128
PACK_COLS = 1024
PACK_ROW_ALIGN = 16
PACK_ROWS = 1024

ADAM_LR = 0.001
ADAM_B1 = 0.9
ADAM_B2 = 0.999
ADAM_EPS = 1e-08
ADAM_WD = 0.01
ADAM_STEP = 10

MESH = pl.DeviceIdType.MESH


def _params(sem, vmem=VMEM_LIMIT):
    return pltpu.CompilerParams(dimension_semantics=sem, vmem_limit_bytes=vmem)


def _dot(a, b, dims):
    return lax.dot_general(a.astype(BF16), b.astype(BF16), (dims, ((), ())),
                           preferred_element_type=F32)


def _mm(a, b):
    return _dot(a, b, ((1,), (0,)))


def _mm_nt(a, b):
    return _dot(a, b, ((1,), (1,)))


def _mm_tn(a, b):
    return _dot(a, b, ((0,), (0,)))


def _rms(x, g, n=None):
    n = x.shape[-1] if n is None else n
    ms = jnp.sum(x * x, axis=-1, keepdims=True) * (1.0 / n)
    return x * lax.rsqrt(ms + RMS_EPS) * g


def _sigmoid(x):
    return 1.0 / (1.0 + jnp.exp(-x))


def _in_proj_fwd(x, g, w):
    s = x.shape[0]
    tm = min(512, s)

    def body(x_ref, g_ref, w_ref, proj_ref, hb_ref):
        hb = _rms(x_ref[...], g_ref[...]).astype(BF16)
        hb_ref[...] = hb
        proj_ref[...] = jnp.dot(hb, w_ref[...], preferred_element_type=F32)

    return pl.pallas_call(
        body, name="in_proj_fwd", grid=(s // tm,),
        in_specs=[pl.BlockSpec((tm, D_MODEL), lambda i: (i, 0)),
                  pl.BlockSpec((1, D_MODEL), lambda i: (0, 0)),
                  pl.BlockSpec((D_MODEL, NC), lambda i: (0, 0))],
        out_specs=[pl.BlockSpec((tm, NC), lambda i: (i, 0)),
                   pl.BlockSpec((tm, D_MODEL), lambda i: (i, 0))],
        out_shape=[jax.ShapeDtypeStruct((s, NC), F32), jax.ShapeDtypeStruct((s, D_MODEL), BF16)],
        compiler_params=_params(("parallel",)),
    )(x, g, w)


def _in_proj_bwd(pieces, x, g, w, dres):
    s = x.shape[0]
    tm = min(512, s)
    n_p = len(pieces)

    def body(*refs):
        p_refs = refs[:n_p]
        x_ref, g_ref, w_ref, dres_ref, dx_ref, dg_ref = refs[n_p:]
        dh = None
        off = 0
        for r in p_refs:
            width = r.shape[1]
            t = _mm_nt(r[...], w_ref[:, off:off + width])
            dh = t if dh is None else dh + t
            off += width
        _, vjp = jax.vjp(_rms, x_ref[...], g_ref[...])
        dx, dg = vjp(dh)
        dx_ref[...] = dx + dres_ref[...]

        @pl.when(pl.program_id(0) == 0)
        def _():
            dg_ref[...] = jnp.zeros_like(dg_ref)

        dg_ref[...] += dg

    in_specs = [pl.BlockSpec((tm, p.shape[1]), lambda i: (i, 0)) for p in pieces]
    in_specs += [pl.BlockSpec((tm, D_MODEL), lambda i: (i, 0)),
                 pl.BlockSpec((1, D_MODEL), lambda i: (0, 0)),
                 pl.BlockSpec((D_MODEL, NC), lambda i: (0, 0)),
                 pl.BlockSpec((tm, D_MODEL), lambda i: (i, 0))]
    return pl.pallas_call(
        body, name="in_proj_bwd", grid=(s // tm,),
        in_specs=in_specs,
        out_specs=[pl.BlockSpec((tm, D_MODEL), lambda i: (i, 0)),
                   pl.BlockSpec((1, D_MODEL), lambda i: (0, 0))],
        out_shape=[jax.ShapeDtypeStruct((s, D_MODEL), F32), jax.ShapeDtypeStruct((1, D_MODEL), F32)],
        compiler_params=_params(("arbitrary",)),
    )(*pieces, x, g, w, dres)


def _matmul_tn(a, b, name):
    s, m = a.shape
    n = b.shape[1]
    tk = min(1024, s)
    tn = min(1536, n)

    def body(a_ref, b_ref, o_ref):
        @pl.when(pl.program_id(1) == 0)
        def _():
            o_ref[...] = jnp.zeros_like(o_ref)

        o_ref[...] += _mm_tn(a_ref[...], b_ref[...])

    return pl.pallas_call(
        body, name=name, grid=(n // tn, s // tk),
        in_specs=[pl.BlockSpec((tk, m), lambda j, k: (k, 0)),
                  pl.BlockSpec((tk, tn), lambda j, k: (k, j))],
        out_specs=pl.BlockSpec((m, tn), lambda j, k: (0, j)),
        out_shape=jax.ShapeDtypeStruct((m, n), F32),
        compiler_params=_params(("parallel", "arbitrary")),
    )(a, b)


def _rms0(x, g, n=None):
    n = x.shape[0] if n is None else n
    ms = jnp.sum(x * x, axis=0, keepdims=True) * (1.0 / n)
    return x * lax.rsqrt(ms + RMS_EPS) * g


@jax.custom_vjp
def _rope0(t, c, s1, s2):
    return t * c + pltpu.roll(t, LANES - 16, 0) * s1 + pltpu.roll(t, 16, 0) * s2


def _rope0_fwd(t, c, s1, s2):
    return _rope0(t, c, s1, s2), (c, s1, s2)


def _rope0_bwd(res, g):
    c, s1, s2 = res
    dt = g * c + pltpu.roll(g * s1, 16, 0) + pltpu.roll(g * s2, LANES - 16, 0)
    return dt, jnp.zeros_like(c), jnp.zeros_like(s1), jnp.zeros_like(s2)


_rope0.defvjp(_rope0_fwd, _rope0_bwd)


@jax.custom_vjp
def _mmw(w, wt, x):
    return _mm(w, x)


def _mmw_fwd(w, wt, x):
    return _mm(w, x), (wt, x)


def _mmw_bwd(res, g):
    wt, x = res
    return _mm_nt(g, x), jnp.zeros_like(wt), _mm(wt, g)


_mmw.defvjp(_mmw_fwd, _mmw_bwd)


def _prep_fn(q_lat, kv_lat, kr, qan, kvan, qn, kn, wq, wk, wv, wqt, wkt, wvt, c, s1, s2, mm):
    tokens = q_lat.shape[1]
    rq = _rms0(q_lat, qan)
    rkv = _rms0(kv_lat, kvan)
    qn_b = jnp.broadcast_to(qn, (LANES, tokens))
    kn_b = jnp.broadcast_to(kn, (LANES, tokens))
    qs, ks = [], []
    for h in range(MLA_HEADS):
        qs.append(_rope0(_rms0(mm(wq[h], wqt[h], rq), qn_b, MLA_QK), c, s1, s2))
        ks.append(_rope0(_rms0(mm(wk[h], wkt[h], rkv) + kr, kn_b, MLA_QK), c, s1, s2))
    return tuple(qs), tuple(ks), mm(wv, wvt, rkv)


def _prep_weights(wq_ref, wkv_ref, wqt_ref, wkvt_ref):
    heads = range(MLA_HEADS)
    wq = tuple(wqt_ref[LANES * h:LANES * (h + 1), :].astype(F32) for h in heads)
    wk = tuple(wkvt_ref[LANES * h:LANES * (h + 1), :].astype(F32) for h in heads)
    wv = wkvt_ref[LANES * MLA_HEADS:, :].astype(F32)
    wqt = tuple(wq_ref[:, LANES * h:LANES * (h + 1)].astype(F32) for h in heads)
    wkt = tuple(wkv_ref[:, LANES * h:LANES * (h + 1)].astype(F32) for h in heads)
    wvt = wkv_ref[:, LANES * MLA_HEADS:].astype(F32)
    return wq, wk, wv, wqt, wkt, wvt


def _prep_in_specs(tm):
    const = lambda shape: pl.BlockSpec(shape, lambda i: (0, 0))
    col = lambda height: pl.BlockSpec((height, tm), lambda i: (0, i))
    return [pl.BlockSpec((tm, 512), lambda i: (i, OFF_MLA // 512)),
            const((MLA_Q_LORA, 1)), const((MLA_KV_LORA, 1)), const((LANES, 1)), const((LANES, 1)),
            const((MLA_Q_LORA, 1024)), const((MLA_KV_LORA, 1536)),
            const((1024, MLA_Q_LORA)), const((1536, MLA_KV_LORA)),
            col(LANES), col(LANES), col(LANES)]


def _prep_operands(blk_ref, refs):
    qan_ref, kvan_ref, qn_ref, kn_ref, wq_ref, wkv_ref, wqt_ref, wkvt_ref, c_ref, s1_ref, s2_ref = refs
    blk_t = jnp.transpose(blk_ref[...])
    diff = (blk_t[0:256], blk_t[256:384], blk_t[384:512],
            qan_ref[...], kvan_ref[...], qn_ref[...], kn_ref[...])
    weights = _prep_weights(wq_ref, wkv_ref, wqt_ref, wkvt_ref)
    return diff, weights, (c_ref[...], s1_ref[...], s2_ref[...])


def _mla_prep_fwd(proj, norms, weights, rope):
    s = proj.shape[0]
    tm = min(512, s)

    def body(blk_ref, *refs):
        ins, (q_ref, k_ref, v_ref, qt_ref, kt_ref, vt_ref) = refs[:11], refs[11:]
        diff, (wq, wk, wv, wqt, wkt, wvt), tables = _prep_operands(blk_ref, ins)
        qs, ks, v = _prep_fn(*diff, wq, wk, wv, wqt, wkt, wvt, *tables,
                             lambda w, wt, x: _mm(w, x))
        for h in range(MLA_HEADS):
            q2 = qs[h] * Q_PRESCALE
            qt_ref[LANES * h:LANES * (h + 1), :] = q2.astype(BF16)
            kt_ref[LANES * h:LANES * (h + 1), :] = ks[h].astype(BF16)
            q_ref[:, LANES * h:LANES * (h + 1)] = jnp.transpose(q2).astype(BF16)
            k_ref[:, LANES * h:LANES * (h + 1)] = jnp.transpose(ks[h]).astype(BF16)
        ones_row = (lax.broadcasted_iota(jnp.int32, (V_AUG - MLA_V, v.shape[1]), 0) == 0).astype(BF16)
        for h in range(MLA_HEADS):
            vt_ref[V_AUG * h:V_AUG * h + MLA_V, :] = v[MLA_V * h:MLA_V * (h + 1)].astype(BF16)
            vt_ref[V_AUG * h + MLA_V:V_AUG * (h + 1), :] = ones_row
        v_ref[...] = jnp.transpose(v).astype(BF16)

    row = lambda width: pl.BlockSpec((tm, width), lambda i: (i, 0))
    col = lambda height: pl.BlockSpec((height, tm), lambda i: (0, i))
    return pl.pallas_call(
        body, name="mla_prep_fwd", grid=(s // tm,),
        in_specs=_prep_in_specs(tm),
        out_specs=[row(1024), row(1024), row(512), col(1024), col(1024), col(MLA_HEADS * V_AUG)],
        out_shape=[jax.ShapeDtypeStruct((s, 1024), BF16), jax.ShapeDtypeStruct((s, 1024), BF16),
                   jax.ShapeDtypeStruct((s, 512), BF16), jax.ShapeDtypeStruct((1024, s), BF16),
                   jax.ShapeDtypeStruct((1024, s), BF16),
                   jax.ShapeDtypeStruct((MLA_HEADS * V_AUG, s), BF16)],
        compiler_params=_params(("parallel",)),
    )(proj, *norms, *weights, *rope)


def _mla_prep_bwd(proj, norms, weights, rope, dq, dk, dv):
    s = proj.shape[0]
    tm = min(256, s)

    def body(blk_ref, *refs):
        ins, (dq_ref, dk_ref, dv_ref) = refs[:11], refs[11:14]
        dblk_ref, dqan_ref, dkvan_ref, dqn_ref, dkn_ref, dwq_ref, dwkv_ref = refs[14:]
        diff, (wq, wk, wv, wqt, wkt, wvt), tables = _prep_operands(blk_ref, ins)

        def fn(q_lat, kv_lat, kr, qan, kvan, qn, kn, wq_, wk_, wv_):
            return _prep_fn(q_lat, kv_lat, kr, qan, kvan, qn, kn, wq_, wk_, wv_, wqt, wkt, wvt,
                            *tables, _mmw)

        _, vjp = jax.vjp(fn, *diff, wq, wk, wv)
        heads = range(MLA_HEADS)
        cts = (tuple(dq_ref[LANES * h:LANES * (h + 1), :] for h in heads),
               tuple(dk_ref[LANES * h:LANES * (h + 1), :] for h in heads), dv_ref[...])
        dq_lat, dkv_lat, dkr, dqan, dkvan, dqn, dkn, dwq_h, dwk_h, dwv = vjp(cts)
        dblk_ref[...] = jnp.transpose(
            jnp.concatenate([dq_lat, dkv_lat, dkr], axis=0)).astype(BF16)

        @pl.when(pl.program_id(0) == 0)
        def _():
            for r in (dqan_ref, dkvan_ref, dqn_ref, dkn_ref, dwq_ref, dwkv_ref):
                r[...] = jnp.zeros_like(r)

        dqan_ref[...] += dqan
        dkvan_ref[...] += dkvan
        dqn_ref[...] += dqn
        dkn_ref[...] += dkn
        for h in heads:
            dwq_ref[LANES * h:LANES * (h + 1), :] += dwq_h[h]
            dwkv_ref[LANES * h:LANES * (h + 1), :] += dwk_h[h]
        dwkv_ref[LANES * MLA_HEADS:, :] += dwv

    const = lambda shape: pl.BlockSpec(shape, lambda i: (0, 0))
    col = lambda height: pl.BlockSpec((height, tm), lambda i: (0, i))
    shapes = [(MLA_Q_LORA, 1), (MLA_KV_LORA, 1), (LANES, 1), (LANES, 1),
              (1024, MLA_Q_LORA), (1536, MLA_KV_LORA)]
    return pl.pallas_call(
        body, name="mla_prep_bwd", grid=(s // tm,),
        in_specs=_prep_in_specs(tm) + [col(1024), col(1024), col(512)],
        out_specs=[pl.BlockSpec((tm, 512), lambda i: (i, 0))] + [const(sh) for sh in shapes],
        out_shape=[jax.ShapeDtypeStruct((s, 512), BF16)]
        + [jax.ShapeDtypeStruct(sh, F32) for sh in shapes],
        compiler_params=_params(("arbitrary",)),
    )(proj, *norms, *weights, *rope, dq, dk, dv)


MLA_SCALE = MLA_QK ** -0.5


LOG2E = 1.4426950408889634
LN2 = 0.6931471805599453
Q_PRESCALE = MLA_SCALE * LOG2E
HEAD_GROUPS = ((0, 1),)


def _mla_attn_fwd(q2, k, vt):
    s = q2.shape[0]
    t = min(512, s)
    tk = min(128, s)
    nq = s // t
    r = t // tk

    def body(q_ref, k_ref, vt_ref, o_ref, lse_ref, acc_ref):
        i = pl.program_id(1)
        row = lax.broadcasted_iota(jnp.int32, (tk, t), 0)
        col = lax.broadcasted_iota(jnp.int32, (tk, t), 1)
        qh = [q_ref[:, LANES * hh:LANES * (hh + 1)] for hh in range(2)]
        acc_ref[...] = jnp.zeros_like(acc_ref)

        def scores(j, heads, diag=None):
            r0 = pl.multiple_of(j * tk, tk)
            out = []
            for hh in heads:
                kc = k_ref[pl.ds(r0, tk), LANES * hh:LANES * (hh + 1)]
                sc = lax.dot_general(kc, qh[hh], (((1,), (1,)), ((), ())),
                                     preferred_element_type=F32)
                out.append(sc if diag is None else jnp.where(row + diag * tk <= col, sc, NEG_INF))
            return tuple(out)

        for heads in HEAD_GROUPS:
            stats = tuple(jnp.full((1, t), NEG_INF, F32) for _ in heads)

            def consume(j, scs, stats, heads=heads):
                r0 = pl.multiple_of(j * tk, tk)
                out, ps, alphas = [], [], []
                for n, hh in enumerate(heads):
                    m_new = jnp.maximum(stats[n], jnp.max(scs[n], axis=0, keepdims=True))
                    ps.append(jnp.exp2(scs[n] - m_new).astype(BF16))
                    alphas.append(jnp.exp2(stats[n] - m_new))
                    out.append(m_new)
                for n, hh in enumerate(heads):
                    vc = vt_ref[V_AUG * hh:V_AUG * (hh + 1), pl.ds(r0, tk)]
                    acc_ref[hh] = alphas[n] * acc_ref[hh] + jnp.dot(vc, ps[n],
                                                                    preferred_element_type=F32)
                return tuple(out)

            def group(j0, stats, diag, heads=heads):
                scs = [scores(j0 + d, heads, d if diag else None) for d in range(r)]
                for d in range(r):
                    stats = consume(j0 + d, scs[d], stats)
                return stats

            stats = group(r * i, stats, True)
            stats = lax.fori_loop(0, i, lambda j, st: group(r * j, st, False), stats)
            for n, hh in enumerate(heads):
                l = acc_ref[hh, MLA_V:MLA_V + 1, :]
                o_ref[:, MLA_V * hh:MLA_V * (hh + 1)] = jnp.transpose(acc_ref[hh, 0:MLA_V, :] / l)
                lse_ref[0, hh:hh + 1, :] = stats[n] + jnp.log2(l)

    return pl.pallas_call(
        body, name="mla_attn_fwd", grid=(MLA_HEADS // 2, nq),
        in_specs=[pl.BlockSpec((t, 256), lambda p, i: (i, p)),
                  pl.BlockSpec((s, 256), lambda p, i: (0, p)),
                  pl.BlockSpec((2 * V_AUG, s), lambda p, i: (p, 0))],
        out_specs=[pl.BlockSpec((t, 128), lambda p, i: (i, p)),
                   pl.BlockSpec((1, 2, t), lambda p, i: (p, 0, i))],
        out_shape=[jax.ShapeDtypeStruct((s, 512), F32),
                   jax.ShapeDtypeStruct((MLA_HEADS // 2, 2, s), F32)],
        scratch_shapes=[pltpu.VMEM((2, V_AUG, t), F32)],
        compiler_params=_params(("parallel", "arbitrary")),
    )(q2, k, vt)


def _mla_attn_bwd(q2, q2t, k, kt, v, do, dot, lse_rows, delta_rows):
    s = q2.shape[0]
    t = min(512, s)
    nq = s // t

    def body(q_ref, qt_ref, k_ref, kt_ref, v_ref, do_ref, dot_ref, lse_ref, dl_ref,
             dq_ref, dk_ref, dv_ref):
        j = pl.program_id(1)

        @pl.when(j == 0)
        def _():
            dq_ref[...] = jnp.zeros_like(dq_ref)

        dk_ref[...] = jnp.zeros_like(dk_ref)
        dv_ref[...] = jnp.zeros_like(dv_ref)
        row = lax.broadcasted_iota(jnp.int32, (t, t), 0)
        col = lax.broadcasted_iota(jnp.int32, (t, t), 1)
        causal_t = row <= col
        kh = [k_ref[:, LANES * hh:LANES * (hh + 1)] for hh in range(2)]
        kth = [kt_ref[LANES * hh:LANES * (hh + 1), :] for hh in range(2)]
        vh = [v_ref[:, MLA_V * hh:MLA_V * (hh + 1)] for hh in range(2)]
        nt = (((1,), (1,)), ((), ()))

        def step(i, masked):
            r0 = pl.multiple_of(i * t, t)
            sd = []
            for hh in range(2):
                qh = q_ref[pl.ds(r0, t), LANES * hh:LANES * (hh + 1)]
                doh = do_ref[pl.ds(r0, t), MLA_V * hh:MLA_V * (hh + 1)]
                sc_t = lax.dot_general(kh[hh], qh, nt, preferred_element_type=F32)
                sd.append(jnp.where(causal_t, sc_t, NEG_INF) if masked else sc_t)
                sd.append(lax.dot_general(vh[hh], doh, nt, preferred_element_type=F32))
            for hh in range(2):
                lse = lse_ref[0, hh:hh + 1, pl.ds(r0, t)]
                dl = dl_ref[0, hh:hh + 1, pl.ds(r0, t)]
                p_t = jnp.exp2(sd[2 * hh] - lse)
                g_t = (p_t * (sd[2 * hh + 1] - dl)).astype(BF16)
                qth = qt_ref[LANES * hh:LANES * (hh + 1), pl.ds(r0, t)]
                doth = dot_ref[MLA_V * hh:MLA_V * (hh + 1), pl.ds(r0, t)]
                dv_ref[MLA_V * hh:MLA_V * (hh + 1), :] += lax.dot_general(
                    doth, p_t.astype(BF16), nt, preferred_element_type=F32)
                dk_ref[LANES * hh:LANES * (hh + 1), :] += lax.dot_general(
                    qth, g_t, nt, preferred_element_type=F32)
                dq_ref[LANES * hh:LANES * (hh + 1), pl.ds(r0, t)] += jnp.dot(
                    kth[hh], g_t, preferred_element_type=F32)

        step(j, True)

        def trip(i, carry):
            step(i, False)
            return carry

        lax.fori_loop(j + 1, nq, trip, 0)
        dk_ref[...] = dk_ref[...] * LN2

        @pl.when(j == nq - 1)
        def _():
            dq_ref[...] = dq_ref[...] * MLA_SCALE

    return pl.pallas_call(
        body, name="mla_attn_bwd", grid=(MLA_HEADS // 2, nq),
        in_specs=[pl.BlockSpec((s, 256), lambda p, j: (0, p)),
                  pl.BlockSpec((256, s), lambda p, j: (p, 0)),
                  pl.BlockSpec((t, 256), lambda p, j: (j, p)),
                  pl.BlockSpec((256, t), lambda p, j: (p, j)),
                  pl.BlockSpec((t, 128), lambda p, j: (j, p)),
                  pl.BlockSpec((s, 128), lambda p, j: (0, p)),
                  pl.BlockSpec((128, s), lambda p, j: (p, 0)),
                  pl.BlockSpec((1, 2, s), lambda p, j: (p, 0, 0)),
                  pl.BlockSpec((1, 2, s), lambda p, j: (p, 0, 0))],
        out_specs=[pl.BlockSpec((256, s), lambda p, j: (p, 0)),
                   pl.BlockSpec((256, t), lambda p, j: (p, j)),
                   pl.BlockSpec((128, t), lambda p, j: (p, j))],
        out_shape=[jax.ShapeDtypeStruct((1024, s), F32), jax.ShapeDtypeStruct((1024, s), F32),
                   jax.ShapeDtypeStruct((512, s), F32)],
        compiler_params=_params(("parallel", "arbitrary")),
    )(q2, q2t, k, kt, v, do, dot, lse_rows, delta_rows)


SWA_SCALE = SWA_DIM ** -0.5
SWA_COLS = SWA_GROUP * BLOCK
SWA_LOG2 = SWA_SCALE * LOG2E


def _swa_tables():
    k = np.arange(2 * BLOCK)[:, None]
    col = np.arange(SWA_COLS)[None, :]
    dist = BLOCK + (col % BLOCK) - k
    valid = (dist >= 0) & (dist < BLOCK)
    out = np.zeros((2, SWA_KV, 2 * BLOCK, SWA_COLS), np.float32)
    for first in range(2):
        ok = valid & ((k >= BLOCK) | (first == 0))
        for j in range(SWA_KV):
            slope = 2.0 ** -(SWA_GROUP * j + col // BLOCK + 1)
            out[first, j] = np.where(ok, -slope * dist * LOG2E, NEG_INF)
    return jnp.asarray(out)


def _swa_tile_inputs(sq_ref, skv_ref, halo_ref, qn_ref, kn_ref, sk_ref, add_ref, first):
    tokens = sq_ref.shape[0]
    kv_all = jnp.concatenate([halo_ref[...], skv_ref[...]], axis=0)
    kv_t = jnp.transpose(kv_all)
    sq_t = jnp.transpose(sq_ref[...])
    k_raw = [kv_t[SWA_DIM * j:SWA_DIM * (j + 1)] for j in range(SWA_KV)]
    v_t = [kv_t[128 + SWA_DIM * j:128 + SWA_DIM * (j + 1)] for j in range(SWA_KV)]
    v_nat = [kv_all[:, 128 + SWA_DIM * j:128 + SWA_DIM * (j + 1)] for j in range(SWA_KV)]
    q_raw = [sq_t[SWA_DIM * h:SWA_DIM * (h + 1)] for h in range(SWA_HEADS)]
    qn_b = jnp.broadcast_to(qn_ref[...], (SWA_DIM, tokens))
    kn_b = jnp.broadcast_to(kn_ref[...], (SWA_DIM, tokens + BLOCK))
    lane_grp = lax.broadcasted_iota(jnp.int32, (1, SWA_COLS), 1) // BLOCK
    sinks, adds = [], []
    for j in range(SWA_KV):
        row = jnp.zeros((1, SWA_COLS), F32)
        for g in range(SWA_GROUP):
            h = SWA_GROUP * j + g
            row = jnp.where(lane_grp == g, sk_ref[:, h:h + 1] * LOG2E, row)
        sinks.append(row)
        adds.append((jnp.where(first, add_ref[1, j], add_ref[0, j]), add_ref[0, j]))
    return k_raw, v_t, v_nat, q_raw, qn_b, kn_b, sinks, adds


def _swa_probs(kb, qs_t, add, sink):
    s2 = jnp.dot(kb, qs_t, preferred_element_type=F32) * SWA_LOG2 + add
    m = jnp.maximum(jnp.max(s2, axis=0, keepdims=True), sink)
    e = jnp.exp2(s2 - m)
    es = jnp.exp2(sink - m)
    inv = 1.0 / (jnp.sum(e, axis=0, keepdims=True) + es)
    return e, inv, es


def _swa_queries(qn_t, j, b):
    return jnp.concatenate([qn_t[SWA_GROUP * j + g][:, BLOCK * b:BLOCK * (b + 1)]
                            for g in range(SWA_GROUP)], axis=1)


def _swa_fwd(proj, qn, kn, sinks, tables):
    s = proj.shape[0]
    ts = min(512, s)
    nb = ts // BLOCK

    def body(sq_ref, skv_ref, halo_ref, qn_ref, kn_ref, sk_ref, add_ref, o_ref, ot_ref):
        first = pl.program_id(0) == 0
        k_raw, v_t, _, q_raw, qn_b, kn_b, sink_rows, adds = _swa_tile_inputs(
            sq_ref, skv_ref, halo_ref, qn_ref, kn_ref, sk_ref, add_ref, first)
        kn_nat = [jnp.transpose(_rms0(k, kn_b)).astype(BF16) for k in k_raw]
        v_t = [v.astype(BF16) for v in v_t]
        qn_t = [_rms0(q, qn_b).astype(BF16) for q in q_raw]
        for b in range(nb):
            band = slice(BLOCK * b, BLOCK * (b + 2))
            for j in range(SWA_KV):
                e, inv, _ = _swa_probs(kn_nat[j][band], _swa_queries(qn_t, j, b),
                                       adds[j][0 if b == 0 else 1], sink_rows[j])
                o_t = jnp.dot(v_t[j][:, band], (e * inv).astype(BF16),
                              preferred_element_type=F32)
                for g in range(SWA_GROUP):
                    h = SWA_GROUP * j + g
                    ot_ref[SWA_DIM * h:SWA_DIM * (h + 1), BLOCK * b:BLOCK * (b + 1)] = (
                        o_t[:, BLOCK * g:BLOCK * (g + 1)])
        o_ref[...] = jnp.transpose(ot_ref[...])

    const = lambda shape: pl.BlockSpec(shape, lambda i: (0,) * len(shape))
    return pl.pallas_call(
        body, name="swa_fwd", grid=(s // ts,),
        in_specs=[pl.BlockSpec((ts, 512), lambda i: (i, OFF_SQ // 512)),
                  pl.BlockSpec((ts, 256), lambda i: (i, OFF_SKV // 256)),
                  pl.BlockSpec((BLOCK, 256), lambda i: (jnp.maximum(i * nb - 1, 0), OFF_SKV // 256)),
                  const((SWA_DIM, 1)), const((SWA_DIM, 1)), const((1, SWA_HEADS)),
                  const(tables.shape)],
        out_specs=pl.BlockSpec((ts, 512), lambda i: (i, 0)),
        out_shape=jax.ShapeDtypeStruct((s, 512), F32),
        scratch_shapes=[pltpu.VMEM((512, ts), F32)],
        compiler_params=_params(("parallel",)),
    )(proj, proj, proj, qn, kn, sinks, tables)


def _swa_bwd(proj, qn, kn, sinks, tables, do):
    s = proj.shape[0]
    ts = min(512, s)
    nb = ts // BLOCK
    nt = s // ts

    def body(sq_ref, skv_ref, halo_ref, qn_ref, kn_ref, sk_ref, add_ref, do_ref,
             dsq_ref, dskv_ref, dqn_ref, dkn_ref, dsk_ref, carry_ref, dqt_ref, dkvt_ref):
        step = pl.program_id(0)
        first = step == nt - 1

        @pl.when(step == 0)
        def _():
            carry_ref[...] = jnp.zeros_like(carry_ref)
            dqn_ref[...] = jnp.zeros_like(dqn_ref)
            dkn_ref[...] = jnp.zeros_like(dkn_ref)
            dsk_ref[...] = jnp.zeros_like(dsk_ref)

        k_raw, v_t, v_nat, q_raw, qn_b, kn_b, sink_rows, adds = _swa_tile_inputs(
            sq_ref, skv_ref, halo_ref, qn_ref, kn_ref, sk_ref, add_ref, first)
        kn_f = [_rms0(k, kn_b) for k in k_raw]
        kn_t = [k.astype(BF16) for k in kn_f]
        kn_nat = [jnp.transpose(k).astype(BF16) for k in kn_f]
        v_nat = [v.astype(BF16) for v in v_nat]
        qn_t = [_rms0(q, qn_b).astype(BF16) for q in q_raw]
        do_t = jnp.transpose(do_ref[...].astype(F32)).astype(BF16)

        dkvt_ref[...] = jnp.zeros_like(dkvt_ref)
        dsink = [jnp.zeros((1, SWA_COLS), F32) for _ in range(SWA_KV)]
        nt_dims = (((1,), (1,)), ((), ()))
        for b in range(nb):
            rows = slice(BLOCK * b, BLOCK * (b + 1))
            band = slice(BLOCK * b, BLOCK * (b + 2))
            for j in range(SWA_KV):
                heads = [SWA_GROUP * j + g for g in range(SWA_GROUP)]
                qs_t = _swa_queries(qn_t, j, b)
                dos_t = jnp.concatenate([do_t[SWA_DIM * h:SWA_DIM * (h + 1), rows] for h in heads],
                                        axis=1)
                e, inv, es = _swa_probs(kn_nat[j][band], qs_t, adds[j][0 if b == 0 else 1],
                                        sink_rows[j])
                p = e * inv
                dp = jnp.dot(v_nat[j][band], dos_t, preferred_element_type=F32)
                dsum = jnp.sum(p * dp, axis=0, keepdims=True)
                dsink[j] = dsink[j] - es * inv * dsum
                g_t = (p * (dp - dsum) * SWA_SCALE).astype(BF16)
                dv_t = lax.dot_general(dos_t, p.astype(BF16), nt_dims,
                                       preferred_element_type=F32)
                dk_t = lax.dot_general(qs_t, g_t, nt_dims, preferred_element_type=F32)
                dq_t = jnp.dot(kn_t[j][:, band], g_t, preferred_element_type=F32)
                dkvt_ref[SWA_DIM * j:SWA_DIM * (j + 1), band] += dk_t
                dkvt_ref[128 + SWA_DIM * j:128 + SWA_DIM * (j + 1), band] += dv_t
                for g, h in enumerate(heads):
                    dqt_ref[SWA_DIM * h:SWA_DIM * (h + 1), rows] = dq_t[:, BLOCK * g:BLOCK * (g + 1)]

        dqn = jnp.zeros((SWA_DIM, 1), F32)
        for h in range(SWA_HEADS):
            _, vjp = jax.vjp(_rms0, q_raw[h], qn_ref[...])
            dq, dg = vjp(dqt_ref[SWA_DIM * h:SWA_DIM * (h + 1), :])
            dqt_ref[SWA_DIM * h:SWA_DIM * (h + 1), :] = dq
            dqn = dqn + dg
        dqn_ref[...] += dqn
        dsq_ref[...] = jnp.transpose(dqt_ref[...]).astype(BF16)
        dkn = jnp.zeros((SWA_DIM, 1), F32)
        lane_grp = lax.broadcasted_iota(jnp.int32, (1, SWA_COLS), 1) // BLOCK
        for j in range(SWA_KV):
            _, vjp = jax.vjp(_rms0, k_raw[j], kn_ref[...])
            dk, dg = vjp(dkvt_ref[SWA_DIM * j:SWA_DIM * (j + 1), :])
            dkvt_ref[SWA_DIM * j:SWA_DIM * (j + 1), :] = dk
            dkn = dkn + dg
            for g in range(SWA_GROUP):
                h = SWA_GROUP * j + g
                dsk_ref[:, h:h + 1] += jnp.sum(jnp.where(lane_grp == g, dsink[j], 0.0), axis=1,
                                               keepdims=True)
        dkn_ref[...] += dkn
        dkv = jnp.transpose(dkvt_ref[...])
        dskv_ref[0:ts - BLOCK, :] = dkv[BLOCK:ts].astype(BF16)
        dskv_ref[ts - BLOCK:ts, :] = (dkv[ts:ts + BLOCK] + carry_ref[...]).astype(BF16)
        carry_ref[...] = dkv[0:BLOCK]

    const = lambda shape: pl.BlockSpec(shape, lambda st: (0,) * len(shape))
    return pl.pallas_call(
        body, name="swa_bwd", grid=(nt,),
        in_specs=[pl.BlockSpec((ts, 512), lambda st: (nt - 1 - st, OFF_SQ // 512)),
                  pl.BlockSpec((ts, 256), lambda st: (nt - 1 - st, OFF_SKV // 256)),
                  pl.BlockSpec((BLOCK, 256),
                               lambda st: (jnp.maximum((nt - 1 - st) * nb - 1, 0), OFF_SKV // 256)),
                  const((SWA_DIM, 1)), const((SWA_DIM, 1)), const((1, SWA_HEADS)),
                  const(tables.shape),
                  pl.BlockSpec((ts, 512), lambda st: (nt - 1 - st, 0))],
        out_specs=[pl.BlockSpec((ts, 512), lambda st: (nt - 1 - st, 0)),
                   pl.BlockSpec((ts, 256), lambda st: (nt - 1 - st, 0)),
                   const((SWA_DIM, 1)), const((SWA_DIM, 1)), const((1, SWA_HEADS))],
        out_shape=[jax.ShapeDtypeStruct((s, 512), BF16), jax.ShapeDtypeStruct((s, 256), BF16),
                   jax.ShapeDtypeStruct((SWA_DIM, 1), F32), jax.ShapeDtypeStruct((SWA_DIM, 1), F32),
                   jax.ShapeDtypeStruct((1, SWA_HEADS), F32)],
        scratch_shapes=[pltpu.VMEM((BLOCK, 256), F32), pltpu.VMEM((512, ts), F32),
                        pltpu.VMEM((256, ts + BLOCK), F32)],
        compiler_params=_params(("arbitrary",)),
    )(proj, proj, proj, qn, kn, sinks, tables, do)


HALO = 8


def _shift_down(u, halo, k):
    tm = u.shape[0]
    rid = lax.broadcasted_iota(jnp.int32, u.shape, 0)
    out = pltpu.roll(u, k, 0)
    for r in range(k):
        out = jnp.where(rid == r, halo[HALO - k + r:HALO - k + r + 1, :], out)
    return out


def _shift_up(u, halo, k):
    tm = u.shape[0]
    rid = lax.broadcasted_iota(jnp.int32, u.shape, 0)
    out = pltpu.roll(u, tm - k, 0)
    for r in range(k):
        out = jnp.where(rid == tm - k + r, halo[r:r + 1, :], out)
    return out


def _conv_fwd_vals(conv_ref, convp_ref, cw_ref, is_first):
    c_h, c_b, c_c = conv_ref[:, 0:512], conv_ref[:, 512:1024], conv_ref[:, 1024:1536]
    u = c_c * c_h
    up = jnp.where(is_first, 0.0, convp_ref[:, 1024:1536] * convp_ref[:, 0:512])
    u1 = _shift_down(u, up, 1)
    u2 = _shift_down(u, up, 2)
    yc = cw_ref[0:1, :] * u2 + cw_ref[1:2, :] * u1 + cw_ref[2:3, :] * u
    return c_h, c_b, c_c, u, u1, u2, yc


def _out_fwd(proj, o_mla, o_swa, x, w_out, cw):
    s = proj.shape[0]
    tm = min(512, s)

    def body(conv_ref, convp_ref, gates_ref, om_ref, os_ref, x_ref, w_ref, cw_ref, y_ref, z_ref):
        i = pl.program_id(0)
        _, c_b, _, _, _, _, yc = _conv_fwd_vals(conv_ref, convp_ref, cw_ref, i == 0)
        mix = (om_ref[...], c_b * yc, os_ref[...])
        for n in range(3):
            g = gates_ref[:, GROUP * n:GROUP * (n + 1)]
            z_ref[:, GROUP * n:GROUP * (n + 1)] = (mix[n] * (g * _sigmoid(g))).astype(BF16)
        y_ref[...] = x_ref[...] + jnp.dot(z_ref[...], w_ref[...], preferred_element_type=F32)

    row = lambda width: pl.BlockSpec((tm, width), lambda i: (i, 0))
    return pl.pallas_call(
        body, name="out_fwd", grid=(s // tm,),
        in_specs=[pl.BlockSpec((tm, 1536), lambda i: (i, 0)),
                  pl.BlockSpec((HALO, 1536), lambda i: (jnp.maximum(i * (tm // HALO) - 1, 0), 0)),
                  pl.BlockSpec((tm, 1536), lambda i: (i, 1)),
                  row(512), row(512), row(D_MODEL),
                  pl.BlockSpec((D_MIX, D_MODEL), lambda i: (0, 0)),
                  pl.BlockSpec((HALO, 512), lambda i: (0, 0))],
        out_specs=[row(D_MODEL), row(D_MIX)],
        out_shape=[jax.ShapeDtypeStruct((s, D_MODEL), F32), jax.ShapeDtypeStruct((s, D_MIX), BF16)],
        compiler_params=_params(("parallel",)),
    )(proj, proj, proj, o_mla, o_swa, x, w_out, cw)


def _out_bwd(dy, proj, o_mla, o_swa, w_out, cw):
    s = proj.shape[0]
    tm = min(512, s)
    nt = s // tm
    hb = tm // HALO

    def body(dy_ref, dyn_ref, conv_ref, convp_ref, convn_ref, gates_ref, gatesn_ref, om_ref, os_ref,
             w_ref, cw_ref,
             dconv_ref, dgates_ref, dom_ref, domt_ref, delta_ref, dos_ref, dcw_ref):
        i = pl.program_id(0)
        dz = _mm_nt(dy_ref[...], w_ref[...])

        def gate(n):
            g = gates_ref[:, GROUP * n:GROUP * (n + 1)]
            sg = _sigmoid(g)
            return g * sg, sg * (1.0 + g * (1.0 - sg))

        for n, o_ref, do_ref in ((0, om_ref, dom_ref), (2, os_ref, dos_ref)):
            silu, dsilu = gate(n)
            dzn = dz[:, GROUP * n:GROUP * (n + 1)]
            o = o_ref[...]
            do = dzn * silu
            do_ref[...] = do.astype(do_ref.dtype)
            dgates_ref[:, GROUP * n:GROUP * (n + 1)] = (dzn * o * dsilu).astype(BF16)
            if n == 0:
                domt_ref[...] = jnp.transpose(do).astype(BF16)
                t = do * o
                for h in range(MLA_HEADS):
                    delta_ref[:, h:h + 1] = jnp.sum(t[:, MLA_V * h:MLA_V * (h + 1)], axis=-1,
                                                    keepdims=True)

        c_h, c_b, c_c, u, u1, u2, yc = _conv_fwd_vals(conv_ref, convp_ref, cw_ref, i == 0)
        silu, dsilu = gate(1)
        dzc = dz[:, GROUP:2 * GROUP]
        dgates_ref[:, GROUP:2 * GROUP] = (dzc * (c_b * yc) * dsilu).astype(BF16)
        dycr = dzc * silu
        dyc = dycr * c_b
        gn = gatesn_ref[:, GROUP:2 * GROUP]
        dzc_n = _mm_nt(dyn_ref[...], w_ref[GROUP:2 * GROUP, :])
        dyc_n = jnp.where(i == nt - 1, 0.0, dzc_n * (gn * _sigmoid(gn)) * convn_ref[:, 512:1024])
        d1 = _shift_up(dyc, dyc_n, 1)
        d2 = _shift_up(dyc, dyc_n, 2)
        du = cw_ref[2:3, :] * dyc + cw_ref[1:2, :] * d1 + cw_ref[0:1, :] * d2
        dconv_ref[:, 0:512] = (du * c_c).astype(BF16)
        dconv_ref[:, 512:1024] = (dycr * yc).astype(BF16)
        dconv_ref[:, 1024:1536] = (du * c_h).astype(BF16)

        @pl.when(i == 0)
        def _():
            dcw_ref[...] = jnp.zeros_like(dcw_ref)

        for k, uk in enumerate((u2, u1, u)):
            dcw_ref[k:k + 1, :] += jnp.sum(dyc * uk, axis=0, keepdims=True)

    row = lambda width: pl.BlockSpec((tm, width), lambda i: (i, 0))
    prev = lambda i: jnp.maximum(i * hb - 1, 0)
    nxt = lambda i: jnp.minimum((i + 1) * hb, s // HALO - 1)
    return pl.pallas_call(
        body, name="out_bwd", grid=(nt,),
        in_specs=[row(D_MODEL),
                  pl.BlockSpec((HALO, D_MODEL), lambda i: (nxt(i), 0)),
                  pl.BlockSpec((tm, 1536), lambda i: (i, 0)),
                  pl.BlockSpec((HALO, 1536), lambda i: (prev(i), 0)),
                  pl.BlockSpec((HALO, 1536), lambda i: (nxt(i), 0)),
                  pl.BlockSpec((tm, 1536), lambda i: (i, 1)),
                  pl.BlockSpec((HALO, 1536), lambda i: (nxt(i), 1)),
                  row(512), row(512),
                  pl.BlockSpec((D_MIX, D_MODEL), lambda i: (0, 0)),
                  pl.BlockSpec((HALO, 512), lambda i: (0, 0))],
        out_specs=[row(1536), row(1536), row(512), pl.BlockSpec((512, tm), lambda i: (0, i)),
                   row(MLA_HEADS), row(512), pl.BlockSpec((HALO, 512), lambda i: (0, 0))],
        out_shape=[jax.ShapeDtypeStruct((s, 1536), BF16), jax.ShapeDtypeStruct((s, 1536), BF16),
                   jax.ShapeDtypeStruct((s, 512), BF16), jax.ShapeDtypeStruct((512, s), BF16),
                   jax.ShapeDtypeStruct((s, MLA_HEADS), F32),
                   jax.ShapeDtypeStruct((s, 512), BF16), jax.ShapeDtypeStruct((HALO, 512), F32)],
        compiler_params=_params(("arbitrary",)),
    )(dy, dy, proj, proj, proj, proj, proj, o_mla, o_swa, w_out, cw)


def _loss_head(y, target):
    s, d = y.shape
    tm = min(512, s)
    nt = s // tm

    def body(y_ref, t_ref, dy_ref, loss_ref):
        i = pl.program_id(0)
        err = y_ref[...] - t_ref[...]
        dy_ref[...] = err * (1.0 / d)

        @pl.when(i == 0)
        def _():
            loss_ref[...] = jnp.zeros_like(loss_ref)

        sq = jnp.sum((err * err).reshape(tm // 8, 8, d), axis=0)
        part = sq[:, 0:LANES]
        for c in range(1, d // LANES):
            part = part + sq[:, LANES * c:LANES * (c + 1)]
        loss_ref[...] += part

        @pl.when(i == nt - 1)
        def _():
            loss_ref[...] = jnp.full(loss_ref.shape, (0.5 / d) * jnp.sum(loss_ref[...]), F32)

    return pl.pallas_call(
        body, name="loss_head", grid=(nt,),
        in_specs=[pl.BlockSpec((tm, d), lambda i: (i, 0)), pl.BlockSpec((tm, d), lambda i: (i, 0))],
        out_specs=[pl.BlockSpec((tm, d), lambda i: (i, 0)), pl.BlockSpec((8, LANES), lambda i: (0, 0))],
        out_shape=[jax.ShapeDtypeStruct((s, d), F32), jax.ShapeDtypeStruct((8, LANES), F32)],
        compiler_params=_params(("arbitrary",)),
    )(y, target)


def _adamw(g, w, m, v):
    rows = g.shape[0]
    tr = min(256, rows)
    c1 = 1.0 - ADAM_B1
    c2 = 1.0 - ADAM_B2
    bc1 = 1.0 - ADAM_B1 ** ADAM_STEP
    bc2 = 1.0 - ADAM_B2 ** ADAM_STEP

    def body(g_ref, w_ref, m_ref, v_ref, d_ref, mo_ref, vo_ref):
        gg = g_ref[...]
        m_new = ADAM_B1 * m_ref[...] + c1 * gg
        v_new = ADAM_B2 * v_ref[...] + c2 * (gg * gg)
        m_hat = m_new / bc1
        v_hat = v_new / bc2
        d_ref[...] = -ADAM_LR * (m_hat / (jnp.sqrt(v_hat) + ADAM_EPS) + ADAM_WD * w_ref[...])
        mo_ref[...] = m_new
        vo_ref[...] = v_new

    spec = pl.BlockSpec((tr, g.shape[1]), lambda i: (i, 0))
    return pl.pallas_call(
        body, name="adamw", grid=(rows // tr,),
        in_specs=[spec] * 4, out_specs=[spec] * 3,
        out_shape=[jax.ShapeDtypeStruct(g.shape, F32)] * 3,
        compiler_params=_params(("parallel",)),
    )(g, w, m, v)


HBM_SPEC = pl.BlockSpec(memory_space=pltpu.HBM)


def _place():
    x, y, c = lax.axis_index("x"), lax.axis_index("y"), lax.axis_index("c")
    chips = [(1 - x, y), (x, 1 - y), (1 - x, 1 - y)]
    return x, y, c, chips


def _all_gather(shards):
    na = len(shards)
    halves = [sh.shape[0] // 2 for sh in shards]

    def body(*refs):
        w_refs, a_refs = refs[:na], refs[na:2 * na]
        send_sems, recv_sems = refs[2 * na:]
        x, y, c, chips = _place()
        k = 2 * x + y
        sib = (x, y, 1 - c)

        def slab(a, kk, hc):
            return a_refs[a].at[kk, pl.ds(hc * halves[a], halves[a]), :]

        def copy(a, n, src, dst, to):
            return pltpu.make_async_remote_copy(
                src_ref=src, dst_ref=dst, send_sem=send_sems.at[6 * a + n],
                recv_sem=recv_sems.at[6 * a + n], device_id=to, device_id_type=MESH)

        first = [copy(a, n, w_refs[a].at[pl.ds(c * halves[a], halves[a]), :], slab(a, k, c),
                      (cx, cy, c))
                 for n, (cx, cy) in enumerate(chips) for a in range(na)]
        for cp in first:
            cp.start()
        passed = []
        for n, (cx, cy) in enumerate(chips):
            kk = 2 * cx + cy
            for a in range(na):
                copy(a, n, slab(a, kk, c), slab(a, kk, c), (cx, cy, c)).wait_recv()
                fwd = copy(a, 3 + n, slab(a, kk, c), slab(a, kk, c), sib)
                fwd.start()
                passed.append(fwd)
        for n, (cx, cy) in enumerate(chips):
            kk = 2 * cx + cy
            for a in range(na):
                copy(a, 3 + n, slab(a, kk, 1 - c), slab(a, kk, 1 - c), sib).wait_recv()
        for cp in first + passed:
            cp.wait_send()

    return pl.pallas_call(
        body, name="weights_all_gather",
        in_specs=[HBM_SPEC] * na, out_specs=[HBM_SPEC] * na,
        out_shape=[jax.ShapeDtypeStruct((N_CHIPS,) + sh.shape, sh.dtype) for sh in shards],
        scratch_shapes=[pltpu.SemaphoreType.DMA((6 * na,)), pltpu.SemaphoreType.DMA((6 * na,))],
    )(*shards)


def _fill_own_slab(buf, src, k_idx):
    n, rows, cols = buf.shape
    tr = _row_tile(rows)
    slabs = src.ndim == 3

    def body(k_ref, src_ref, buf_ref, out_ref):
        out_ref[0] = src_ref[0] if slabs else src_ref[...]

    if slabs:
        src_spec = pl.BlockSpec((1, tr, cols), lambda t, k_ref: (k_ref[0], t, 0))
    else:
        src_spec = pl.BlockSpec((tr, cols), lambda t, k_ref: (t, 0))
    return pl.pallas_call(
        body, name="fill_own_slab",
        grid_spec=pltpu.PrefetchScalarGridSpec(
            num_scalar_prefetch=1, grid=(rows // tr,),
            in_specs=[src_spec, pl.BlockSpec(memory_space=pl.ANY)],
            out_specs=pl.BlockSpec((1, tr, cols), lambda t, k_ref: (k_ref[0], t, 0))),
        out_shape=jax.ShapeDtypeStruct(buf.shape, buf.dtype),
        input_output_aliases={2: 0},
        compiler_params=_params(("parallel",)),
    )(k_idx, src, buf)


def _swap_halves_to_sibling(gs):
    na = len(gs)

    def body(*refs):
        g_refs, r_refs = refs[:na], refs[na:2 * na]
        send_sems, recv_sems = refs[2 * na:]
        x, y, c, _ = _place()
        cps = []
        for a in range(na):
            half = g_refs[a].shape[1] // 2
            cps.append(pltpu.make_async_remote_copy(
                src_ref=g_refs[a].at[:, pl.ds((1 - c) * half, half), :], dst_ref=r_refs[a],
                send_sem=send_sems.at[a], recv_sem=recv_sems.at[a], device_id=(x, y, 1 - c),
                device_id_type=MESH))
        for cp in cps:
            cp.start()
        for cp in cps:
            cp.wait()

    return pl.pallas_call(
        body, name="grads_to_sibling",
        in_specs=[HBM_SPEC] * na, out_specs=[HBM_SPEC] * na,
        out_shape=[jax.ShapeDtypeStruct((g.shape[0], g.shape[1] // 2, g.shape[2]), g.dtype)
                   for g in gs],
        scratch_shapes=[pltpu.SemaphoreType.DMA((na,)), pltpu.SemaphoreType.DMA((na,))],
    )(*gs)


def _row_tile(rows):
    return 256 if rows % 256 == 0 else 128


def _add_sibling(g, r, c_idx, out_dtype):
    n, rows, cols = g.shape
    half = rows // 2
    tr = _row_tile(half)
    nb = half // tr

    def body(c_ref, g_ref, r_ref, p_ref):
        p_ref[...] = (g_ref[...] + r_ref[...]).astype(out_dtype)

    return pl.pallas_call(
        body, name="grads_add_sibling",
        grid_spec=pltpu.PrefetchScalarGridSpec(
            num_scalar_prefetch=1, grid=(n, nb),
            in_specs=[pl.BlockSpec((1, tr, cols), lambda j, t, c_ref: (j, c_ref[0] * nb + t, 0)),
                      pl.BlockSpec((1, tr, cols), lambda j, t, c_ref: (j, t, 0))],
            out_specs=pl.BlockSpec((1, tr, cols), lambda j, t, c_ref: (j, t, 0))),
        out_shape=jax.ShapeDtypeStruct((n, half, cols), out_dtype),
        compiler_params=_params(("parallel", "parallel")),
    )(c_idx, g, r)


def _scatter_to_chips(ps):
    na = len(ps)

    def body(*refs):
        p_refs, q_refs = refs[:na], refs[na:2 * na]
        send_sems, recv_sems = refs[2 * na:]
        x, y, c, chips = _place()
        k = 2 * x + y
        sends = []
        for i, (cx, cy) in enumerate(chips):
            for a in range(na):
                cp = pltpu.make_async_remote_copy(
                    src_ref=p_refs[a].at[2 * cx + cy], dst_ref=q_refs[a].at[k],
                    send_sem=send_sems.at[3 * a + i], recv_sem=recv_sems.at[3 * a + i],
                    device_id=(cx, cy, c), device_id_type=MESH)
                cp.start()
                sends.append(cp)
        for i, (cx, cy) in enumerate(chips):
            kk = 2 * cx + cy
            for a in range(na):
                pltpu.make_async_remote_copy(
                    src_ref=p_refs[a].at[kk], dst_ref=q_refs[a].at[kk],
                    send_sem=send_sems.at[3 * a + i], recv_sem=recv_sems.at[3 * a + i],
                    device_id=(cx, cy, c), device_id_type=MESH).wait_recv()
        for cp in sends:
            cp.wait_send()

    return pl.pallas_call(
        body, name="grads_scatter_to_chips",
        in_specs=[HBM_SPEC] * na, out_specs=[HBM_SPEC] * na,
        out_shape=[jax.ShapeDtypeStruct(p.shape, p.dtype) for p in ps],
        scratch_shapes=[pltpu.SemaphoreType.DMA((3 * na,)), pltpu.SemaphoreType.DMA((3 * na,))],
    )(*ps)


def _sum_chips(q, c_idx):
    n, half, cols = q.shape
    tr = _row_tile(half)
    nb = half // tr

    def body(c_ref, q_ref, o_ref):
        parts = [q_ref[kk].astype(F32) for kk in range(n)]
        o_ref[...] = ((parts[0] + parts[1]) + parts[2]) + parts[3]

    return pl.pallas_call(
        body, name="grads_sum_chips",
        grid_spec=pltpu.PrefetchScalarGridSpec(
            num_scalar_prefetch=1, grid=(nb,),
            in_specs=[pl.BlockSpec((n, tr, cols), lambda t, c_ref: (0, t, 0))],
            out_specs=pl.BlockSpec((tr, cols), lambda t, c_ref: (c_ref[0] * nb + t, 0))),
        out_shape=jax.ShapeDtypeStruct((2 * half, cols), F32),
        compiler_params=_params(("parallel",)),
    )(c_idx, q)


def _join_halves(fulls):
    na = len(fulls)

    def body(*refs):
        o_refs = refs[na:2 * na]
        send_sems, recv_sems = refs[2 * na:]
        x, y, c, _ = _place()
        sends = []
        for a in range(na):
            half = o_refs[a].shape[0] // 2
            rows = o_refs[a].at[pl.ds(c * half, half), :]
            sends.append(pltpu.make_async_remote_copy(
                src_ref=rows, dst_ref=rows, send_sem=send_sems.at[a], recv_sem=recv_sems.at[a],
                device_id=(x, y, 1 - c), device_id_type=MESH))
        for cp in sends:
            cp.start()
        for a in range(na):
            half = o_refs[a].shape[0] // 2
            other = o_refs[a].at[pl.ds((1 - c) * half, half), :]
            pltpu.make_async_remote_copy(
                src_ref=other, dst_ref=other, send_sem=send_sems.at[a], recv_sem=recv_sems.at[a],
                device_id=(x, y, 1 - c), device_id_type=MESH).wait_recv()
        for cp in sends:
            cp.wait_send()

    return pl.pallas_call(
        body, name="grads_join_halves",
        in_specs=[HBM_SPEC] * na, out_specs=[HBM_SPEC] * na,
        out_shape=[jax.ShapeDtypeStruct(f.shape, f.dtype) for f in fulls],
        input_output_aliases={a: a for a in range(na)},
        scratch_shapes=[pltpu.SemaphoreType.DMA((na,)), pltpu.SemaphoreType.DMA((na,))],
    )(*fulls)


def _part_rows(shape):
    size = 1
    for d in shape:
        size *= d
    rows = -(-size // PACK_COLS)
    return size, -(-rows // PACK_ROW_ALIGN) * PACK_ROW_ALIGN


def _pack_rows(arrays, dtype, total_rows):
    parts, used = [], 0
    for a in arrays:
        size, rows = _part_rows(a.shape)
        flat = a.reshape(-1).astype(dtype)
        parts.append(jnp.pad(flat, (0, rows * PACK_COLS - size)).reshape(rows, PACK_COLS))
        used += rows
    parts.append(jnp.zeros((total_rows - used, PACK_COLS), dtype))
    return jnp.concatenate(parts, axis=0)


def _unpack_rows(buf, shapes):
    lead = buf.shape[:-2]
    out, off = [], 0
    for sh in shapes:
        size, rows = _part_rows(sh)
        part = buf[..., off:off + rows, :].reshape(lead + (-1,))[..., :size]
        out.append(part.reshape(lead + tuple(sh)))
        off += rows
    return out


NEW_ORDER = ((928, 1440), (1440, 1952), (1952, 2464), (416, 928), (2464, 2976), (3744, 4256),
             (2976, 3488), (0, 256), (256, 384), (4256, 4320), (384, 416), (4256, 4288),
             (3488, 3616), (3616, 3744))
OLD_ORDER = ((3584, 3840), (3840, 3968), (4032, 4064), (1536, 2048), (0, 512), (512, 1024),
             (1024, 1536), (2048, 2560), (3072, 3584), (4096, 4224), (4224, 4352), (2560, 3072))


def _cols(sources, ranges):
    parts = []
    for a, b in ranges:
        off = 0
        for src in sources:
            width = src.shape[-1]
            lo, hi = max(a, off), min(b, off + width)
            if lo < hi:
                parts.append(src[..., lo - off:hi - off])
            off += width
    return jnp.concatenate(parts, axis=-1)


def _sub_ranges(ranges, a, b):
    out, off = [], 0
    for lo, hi in ranges:
        width = hi - lo
        s0, s1 = max(a, off), min(b, off + width)
        if s0 < s1:
            out.append((lo + s0 - off, lo + s1 - off))
        off += width
    return out


def _rope_tables(s):
    half = MLA_ROPE // 2
    inv_freq = jnp.power(jnp.float32(ROPE_THETA), -jnp.arange(half, dtype=F32) / half)
    ang = inv_freq[:, None] * jnp.arange(s, dtype=F32)[None, :]
    cos, sin = jnp.cos(ang), jnp.sin(ang)
    z = lambda n: jnp.zeros((n, s), F32)
    c = jnp.concatenate([jnp.ones((MLA_NOPE, s), F32), cos, cos, z(32)], axis=0)
    s1 = jnp.concatenate([z(MLA_NOPE), -sin, z(16), z(32)], axis=0)
    s2 = jnp.concatenate([z(MLA_NOPE), z(16), sin, z(32)], axis=0)
    return c, s1, s2


def _pad_lanes(a, n):
    return jnp.pad(a, ((0, 0), (0, n - a.shape[1])))


SHARDED = ("w_in", "w_out", "mla_w_qb", "mla_w_kvb", "conv_w")
REPLICATED = ("norm_g", "mla_q_a_norm", "mla_kv_a_norm", "mla_q_norm", "mla_k_norm",
              "swa_q_norm", "swa_k_norm", "swa_sinks")
WEIGHT_ORDER = ("norm_g", "w_in", "mla_q_a_norm", "mla_w_qb", "mla_kv_a_norm", "mla_w_kvb",
                "mla_q_norm", "mla_k_norm", "conv_w", "swa_q_norm", "swa_k_norm", "swa_sinks", "w_out")
SHARD_AXIS = {"w_in": 2, "w_out": 1, "mla_w_qb": 2, "mla_w_kvb": 2, "conv_w": 2}


def kernel(x, norm_g, w_in, mla_q_a_norm, mla_w_qb, mla_kv_a_norm, mla_w_kvb, mla_q_norm, mla_k_norm, conv_w, swa_q_norm, swa_k_norm, swa_sinks, w_out, loss_target, m_norm_g, m_w_in, m_mla_q_a_norm, m_mla_w_qb, m_mla_kv_a_norm, m_mla_w_kvb, m_mla_q_norm, m_mla_k_norm, m_conv_w, m_swa_q_norm, m_swa_k_norm, m_swa_sinks, m_w_out, v_norm_g, v_w_in, v_mla_q_a_norm, v_mla_w_qb, v_mla_kv_a_norm, v_mla_w_kvb, v_mla_q_norm, v_mla_k_norm, v_conv_w, v_swa_q_norm, v_swa_k_norm, v_swa_sinks, v_w_out):
    weights = dict(norm_g=norm_g, w_in=w_in, mla_q_a_norm=mla_q_a_norm, mla_w_qb=mla_w_qb,
                   mla_kv_a_norm=mla_kv_a_norm, mla_w_kvb=mla_w_kvb, mla_q_norm=mla_q_norm,
                   mla_k_norm=mla_k_norm, conv_w=conv_w, swa_q_norm=swa_q_norm,
                   swa_k_norm=swa_k_norm, swa_sinks=swa_sinks, w_out=w_out)
    mom_m = dict(norm_g=m_norm_g, w_in=m_w_in, mla_q_a_norm=m_mla_q_a_norm, mla_w_qb=m_mla_w_qb,
                 mla_kv_a_norm=m_mla_kv_a_norm, mla_w_kvb=m_mla_w_kvb, mla_q_norm=m_mla_q_norm,
                 mla_k_norm=m_mla_k_norm, conv_w=m_conv_w, swa_q_norm=m_swa_q_norm,
                 swa_k_norm=m_swa_k_norm, swa_sinks=m_swa_sinks, w_out=m_w_out)
    mom_v = dict(norm_g=v_norm_g, w_in=v_w_in, mla_q_a_norm=v_mla_q_a_norm, mla_w_qb=v_mla_w_qb,
                 mla_kv_a_norm=v_mla_kv_a_norm, mla_w_kvb=v_mla_w_kvb, mla_q_norm=v_mla_q_norm,
                 mla_k_norm=v_mla_k_norm, conv_w=v_conv_w, swa_q_norm=v_swa_q_norm,
                 swa_k_norm=v_swa_k_norm, swa_sinks=v_swa_sinks, w_out=v_w_out)
    xs = x[0]
    target = loss_target[0]
    s = xs.shape[0]
    c_idx = lax.axis_index("c").astype(jnp.int32).reshape(1)
    k_idx = (2 * lax.axis_index("x") + lax.axis_index("y")).astype(jnp.int32).reshape(1)

    conv_bits = lax.bitcast_convert_type(conv_w, BF16)
    small_list = [w_out, mla_w_qb, mla_w_kvb, conv_bits]
    w_in_rows = DEPTH * D_MODEL
    own = [w_in.astype(BF16).reshape(w_in_rows, w_in.shape[2]), _pack_rows(small_list, BF16, PACK_ROWS)]
    gathered_in, gathered_rest = [_fill_own_slab(buf, src, k_idx)
                                  for buf, src in zip(_all_gather(own), own)]
    parts = _unpack_rows(gathered_rest, [a.shape for a in small_list])
    join = lambda p, axis: jnp.concatenate([p[k] for k in range(N_CHIPS)], axis=axis)
    w_in_slabs = gathered_in.reshape(N_CHIPS, DEPTH, D_MODEL, w_in.shape[2])
    w_in_zeros = jnp.zeros((D_MODEL, 64), BF16)
    w_out_full = join(parts[0], 1)
    w_qb_full = join(parts[1], 2)
    w_kvb_full = join(parts[2], 2)
    conv_full = lax.bitcast_convert_type(join(parts[3], 2), F32)

    rope = _rope_tables(s)
    swa_tables = _swa_tables()
    layers = []
    for l in range(DEPTH):
        wq = jnp.pad(w_qb_full[l].reshape(MLA_Q_LORA, MLA_HEADS, MLA_QK),
                     ((0, 0), (0, 0), (0, LANES - MLA_QK))).reshape(MLA_Q_LORA, MLA_HEADS * LANES)
        kv = w_kvb_full[l].reshape(MLA_KV_LORA, MLA_HEADS, MLA_NOPE + MLA_V)
        wk = jnp.pad(kv[:, :, :MLA_NOPE], ((0, 0), (0, 0), (0, LANES - MLA_NOPE)))
        wkv = jnp.concatenate([wk.reshape(MLA_KV_LORA, MLA_HEADS * LANES),
                               kv[:, :, MLA_NOPE:].reshape(MLA_KV_LORA, MLA_HEADS * MLA_V)], axis=1)
        layers.append(dict(
            w_in=_cols([w_in_slabs[k, l] for k in range(N_CHIPS)] + [w_in_zeros], NEW_ORDER),
            w_out=w_out_full[l], wq=wq, wkv=wkv,
            cw=jnp.pad(conv_full[l], ((0, HALO - 3), (0, 0))),
            g=norm_g[l][None],
            mla_norms=(mla_q_a_norm[l][:, None], mla_kv_a_norm[l][:, None],
                       _pad_lanes(mla_q_norm[l][None], LANES).T, _pad_lanes(mla_k_norm[l][None], LANES).T),
            mla_weights=(wq, wkv, wq.T, wkv.T),
            sqn=swa_q_norm[l][:, None], skn=swa_k_norm[l][:, None], sinks=swa_sinks[l][None]))

    saved = []
    h_in = xs
    for l in range(DEPTH):
        p = layers[l]
        proj, hb = _in_proj_fwd(h_in, p["g"], p["w_in"])
        q, k, v, qt, kt, vt = _mla_prep_fwd(proj, p["mla_norms"], p["mla_weights"], rope)
        o_mla, lse = _mla_attn_fwd(q, k, vt)
        o_swa = _swa_fwd(proj, p["sqn"], p["skn"], p["sinks"], swa_tables)
        y, z = _out_fwd(proj, o_mla, o_swa, h_in, p["w_out"], p["cw"])
        saved.append(dict(x=h_in, proj=proj, hb=hb, q=q, k=k, v=v, qt=qt, kt=kt, o_mla=o_mla, lse=lse,
                          o_swa=o_swa, z=z))
        h_in = y

    dy, loss_acc = _loss_head(h_in, target)
    loss = lax.psum(loss_acc[0, 0], ("x", "y", "c"))

    grads = {n: [None] * DEPTH for n in WEIGHT_ORDER}
    for l in reversed(range(DEPTH)):
        p, a = layers[l], saved[l]
        dconv, dgates, do_mla, do_mla_t, delta, do_swa, dcw = _out_bwd(dy, a["proj"], a["o_mla"], a["o_swa"],
                                                             p["w_out"], p["cw"])
        grads["w_out"][l] = _matmul_tn(a["z"], dy, "dw_out")
        grads["conv_w"][l] = dcw[0:3]
        delta_rows = jnp.transpose(delta, (1, 0)).reshape(MLA_HEADS // 2, 2, s)
        dq, dk, dv = _mla_attn_bwd(a["q"], a["qt"], a["k"], a["kt"], a["v"], do_mla, do_mla_t,
                                   a["lse"], delta_rows)
        dmla, dqan, dkvan, dqn, dkn, dwq_t, dwkv_t = _mla_prep_bwd(
            a["proj"], p["mla_norms"], p["mla_weights"], rope, dq, dk, dv)
        dwq, dwkv = dwq_t.T, dwkv_t.T
        dsq, dskv, dsqn, dskn, dsinks = _swa_bwd(a["proj"], p["sqn"], p["skn"], p["sinks"], swa_tables, do_swa)
        pieces = [dconv, dgates, dsq, dmla, dskv]
        dx, dg = _in_proj_bwd(pieces, a["x"], p["g"], p["w_in"], dy)
        grads["w_in"][l] = [_matmul_tn(a["hb"], pc, "dw_in_%d" % n) for n, pc in enumerate(pieces)]
        grads["norm_g"][l] = dg[0]
        grads["mla_q_a_norm"][l] = dqan[:, 0]
        grads["mla_kv_a_norm"][l] = dkvan[:, 0]
        grads["mla_q_norm"][l] = dqn[:MLA_QK, 0]
        grads["mla_k_norm"][l] = dkn[:MLA_QK, 0]
        grads["mla_w_qb"][l] = dwq.reshape(MLA_Q_LORA, MLA_HEADS, LANES)[:, :, :MLA_QK].reshape(
            MLA_Q_LORA, MLA_HEADS * MLA_QK)
        dwk = dwkv[:, :MLA_HEADS * LANES].reshape(MLA_KV_LORA, MLA_HEADS, LANES)[:, :, :MLA_NOPE]
        dwv = dwkv[:, MLA_HEADS * LANES:].reshape(MLA_KV_LORA, MLA_HEADS, MLA_V)
        grads["mla_w_kvb"][l] = jnp.concatenate([dwk, dwv], axis=2).reshape(
            MLA_KV_LORA, MLA_HEADS * (MLA_NOPE + MLA_V))
        grads["swa_q_norm"][l] = dsqn[:, 0]
        grads["swa_k_norm"][l] = dskn[:, 0]
        grads["swa_sinks"][l] = dsinks[0]
        dy = dx
    grad_x = dy[None]
    full_grads = {n: jnp.stack(grads[n]) for n in WEIGHT_ORDER if n != "w_in"}

    rest = tuple(n for n in SHARDED if n != "w_in")
    rep_shapes = [weights[n].shape for n in REPLICATED]
    flat_rep = lambda d: jnp.concatenate([d[n].reshape(-1) for n in REPLICATED])

    def chunk(g, n, k):
        width = g.shape[SHARD_AXIS[n]] // N_CHIPS
        return lax.slice_in_dim(g, k * width, (k + 1) * width, axis=SHARD_AXIS[n])

    shard_cols = w_in.shape[2]
    g_in = jnp.stack([
        jnp.concatenate([_cols(grads["w_in"][l], _sub_ranges(OLD_ORDER, k * shard_cols,
                                                             (k + 1) * shard_cols))
                         for l in range(DEPTH)], axis=0)
        for k in range(N_CHIPS)])
    rep_grads = flat_rep(full_grads)
    g_rest = jnp.stack([_pack_rows([chunk(full_grads[n], n, k) for n in rest] + [rep_grads],
                                   F32, PACK_ROWS) for k in range(N_CHIPS)])
    from_sibling = _swap_halves_to_sibling([g_in, g_rest])
    partial = [_add_sibling(g, r, c_idx, dt)
               for g, r, dt in zip((g_in, g_rest), from_sibling, (BF16, F32))]
    by_chip = [_fill_own_slab(q, p, k_idx) for q, p in zip(_scatter_to_chips(partial), partial)]
    g_in_mine, g_rest_mine = _join_halves([_sum_chips(q, c_idx) for q in by_chip])

    pack_rest = lambda d: _pack_rows([d[n] for n in rest] + [flat_rep(d)], F32, PACK_ROWS)
    in_shape = w_in.shape
    res_in = _adamw(g_in_mine, w_in.reshape(w_in_rows, -1), m_w_in.reshape(w_in_rows, -1),
                    v_w_in.reshape(w_in_rows, -1))
    res_rest = _adamw(g_rest_mine, pack_rest(weights), pack_rest(mom_m), pack_rest(mom_v))
    rest_shapes = [weights[n].shape for n in rest] + [(rep_grads.shape[0],)]
    unpacked = []
    for buf_in, buf_rest in zip((g_in_mine,) + tuple(res_in), (g_rest_mine,) + tuple(res_rest)):
        vals = _unpack_rows(buf_rest, rest_shapes)
        group = dict(zip(rest, vals[:-1]))
        group["w_in"] = buf_in.reshape(in_shape)
        off = 0
        for n, sh in zip(REPLICATED, rep_shapes):
            size = sh[0] * sh[1]
            group[n] = vals[-1][off:off + size].reshape(sh)
            off += size
        unpacked.append(group)
    outs = [loss, grad_x]
    for group in unpacked:
        outs += [group[n] for n in WEIGHT_ORDER]
    return tuple(outs)
```

```python
import jax
import numpy as np
import jax.numpy as jnp
from jax import lax
from jax.experimental import pallas as pl
from jax.experimental.pallas import tpu as pltpu

F32 = jnp.float32
BF16 = jnp.bfloat16

D_MODEL = 1024
DEPTH = 2
GROUP = 512
D_MIX = 3 * GROUP
BLOCK = 128
RMS_EPS = 1e-6
NEG_INF = -1e30
MLA_HEADS = 8
MLA_QK = 96
MLA_NOPE = 64
MLA_ROPE = 32
MLA_V = 64
V_AUG = 80
MLA_Q_LORA = 256
MLA_KV_LORA = 128
ROPE_THETA = 10000.0
SWA_HEADS = 8
SWA_KV = 2
SWA_GROUP = 4
SWA_DIM = 64
N_CHIPS = 4

NC = 4352
OFF_SQ, OFF_MLA, OFF_SKV = 3072, 3584, 4096

VMEM_LIMIT = 56 * 1024 * 1024
LANES = 128
PACK_COLS = 1024
PACK_ROW_ALIGN = 16
PACK_ROWS = 1024

ADAM_LR = 0.001
ADAM_B1 = 0.9
ADAM_B2 = 0.999
ADAM_EPS = 1e-08
ADAM_WD = 0.01
ADAM_STEP = 10

MESH = pl.DeviceIdType.MESH


def _params(sem, vmem=VMEM_LIMIT):
    return pltpu.CompilerParams(dimension_semantics=sem, vmem_limit_bytes=vmem)


def _dot(a, b, dims):
    return lax.dot_general(a.astype(BF16), b.astype(BF16), (dims, ((), ())),
                           preferred_element_type=F32)


def _mm(a, b):
    return _dot(a, b, ((1,), (0,)))


def _mm_nt(a, b):
    return _dot(a, b, ((1,), (1,)))


def _rms(x, g, n=None):
    n = x.shape[-1] if n is None else n
    ms = jnp.sum(x * x, axis=-1, keepdims=True) * (1.0 / n)
    return x * lax.rsqrt(ms + RMS_EPS) * g


def _sigmoid(x):
    return 1.0 / (1.0 + jnp.exp(-x))


def _in_proj_fwd(x, g, w):
    s = x.shape[0]
    tm = min(512, s)

    def body(x_ref, g_ref, w_ref, proj_ref, hbt_ref):
        h = _rms(x_ref[...], g_ref[...])
        hbt_ref[...] = jnp.transpose(h).astype(BF16)
        proj_ref[...] = jnp.dot(h.astype(BF16), w_ref[...], preferred_element_type=F32)

    return pl.pallas_call(
        body, name="in_proj_fwd", grid=(s // tm,),
        in_specs=[pl.BlockSpec((tm, D_MODEL), lambda i: (i, 0)),
                  pl.BlockSpec((1, D_MODEL), lambda i: (0, 0)),
                  pl.BlockSpec((D_MODEL, NC), lambda i: (0, 0))],
        out_specs=[pl.BlockSpec((tm, NC), lambda i: (i, 0)),
                   pl.BlockSpec((D_MODEL, tm), lambda i: (0, i))],
        out_shape=[jax.ShapeDtypeStruct((s, NC), F32), jax.ShapeDtypeStruct((D_MODEL, s), BF16)],
        compiler_params=_params(("parallel",)),
    )(x, g, w)


def _in_proj_bwd(pieces, x, g, w, dres):
    s = x.shape[0]
    tm = min(512, s)
    n_p = len(pieces)

    def body(*refs):
        p_refs = refs[:n_p]
        x_ref, g_ref, w_ref, dres_ref, dx_ref, dg_ref = refs[n_p:]
        dh = None
        off = 0
        for r in p_refs:
            width = r.shape[1]
            t = _mm_nt(r[...], w_ref[:, off:off + width])
            dh = t if dh is None else dh + t
            off += width
        _, vjp = jax.vjp(_rms, x_ref[...], g_ref[...])
        dx, dg = vjp(dh)
        dx_ref[...] = dx + dres_ref[...]

        @pl.when(pl.program_id(0) == 0)
        def _():
            dg_ref[...] = jnp.zeros_like(dg_ref)

        dg_ref[...] += dg

    in_specs = [pl.BlockSpec((tm, p.shape[1]), lambda i: (i, 0)) for p in pieces]
    in_specs += [pl.BlockSpec((tm, D_MODEL), lambda i: (i, 0)),
                 pl.BlockSpec((1, D_MODEL), lambda i: (0, 0)),
                 pl.BlockSpec((D_MODEL, NC), lambda i: (0, 0)),
                 pl.BlockSpec((tm, D_MODEL), lambda i: (i, 0))]
    return pl.pallas_call(
        body, name="in_proj_bwd", grid=(s // tm,),
        in_specs=in_specs,
        out_specs=[pl.BlockSpec((tm, D_MODEL), lambda i: (i, 0)),
                   pl.BlockSpec((1, D_MODEL), lambda i: (0, 0))],
        out_shape=[jax.ShapeDtypeStruct((s, D_MODEL), F32), jax.ShapeDtypeStruct((1, D_MODEL), F32)],
        compiler_params=_params(("arbitrary",)),
    )(*pieces, x, g, w, dres)


def _weight_grads(at, bs, name):
    m, s = at.shape
    nb = len(bs)
    tk = min(512, s)

    def body(a_ref, *refs):
        b_refs, o_refs = refs[:nb], refs[nb:]

        @pl.when(pl.program_id(0) == 0)
        def _():
            for o_ref in o_refs:
                o_ref[...] = jnp.zeros_like(o_ref)

        a = a_ref[...]
        for b_ref, o_ref in zip(b_refs, o_refs):
            o_ref[...] += _mm(a, b_ref[...])

    return pl.pallas_call(
        body, name=name, grid=(s // tk,),
        in_specs=[pl.BlockSpec((m, tk), lambda k: (0, k))]
        + [pl.BlockSpec((tk, b.shape[1]), lambda k: (k, 0)) for b in bs],
        out_specs=[pl.BlockSpec((m, b.shape[1]), lambda k: (0, 0)) for b in bs],
        out_shape=[jax.ShapeDtypeStruct((m, b.shape[1]), F32) for b in bs],
        compiler_params=_params(("arbitrary",)),
    )(at, *bs)


def _rms0(x, g, n=None):
    n = x.shape[0] if n is None else n
    ms = jnp.sum(x * x, axis=0, keepdims=True) * (1.0 / n)
    return x * lax.rsqrt(ms + RMS_EPS) * g


@jax.custom_vjp
def _rope0(t, c, s1, s2):
    return t * c + pltpu.roll(t, LANES - 16, 0) * s1 + pltpu.roll(t, 16, 0) * s2


def _rope0_fwd(t, c, s1, s2):
    return _rope0(t, c, s1, s2), (c, s1, s2)


def _rope0_bwd(res, g):
    c, s1, s2 = res
    dt = g * c + pltpu.roll(g * s1, 16, 0) + pltpu.roll(g * s2, LANES - 16, 0)
    return dt, jnp.zeros_like(c), jnp.zeros_like(s1), jnp.zeros_like(s2)


_rope0.defvjp(_rope0_fwd, _rope0_bwd)


@jax.custom_vjp
def _mmw(w, wt, x):
    return _mm(w, x)


def _mmw_fwd(w, wt, x):
    return _mm(w, x), (wt, x)


def _mmw_bwd(res, g):
    wt, x = res
    return _mm_nt(g, x), jnp.zeros_like(wt), _mm(wt, g)


_mmw.defvjp(_mmw_fwd, _mmw_bwd)


def _prep_fn(q_lat, kv_lat, kr, qan, kvan, qn, kn, wq, wk, wv, wqt, wkt, wvt, c, s1, s2, mm):
    tokens = q_lat.shape[1]
    rq = _rms0(q_lat, qan)
    rkv = _rms0(kv_lat, kvan)
    qn_b = jnp.broadcast_to(qn, (LANES, tokens))
    kn_b = jnp.broadcast_to(kn, (LANES, tokens))
    qs, ks = [], []
    for h in range(MLA_HEADS):
        qs.append(_rope0(_rms0(mm(wq[h], wqt[h], rq), qn_b, MLA_QK), c, s1, s2))
        ks.append(_rope0(_rms0(mm(wk[h], wkt[h], rkv) + kr, kn_b, MLA_QK), c, s1, s2))
    return tuple(qs), tuple(ks), mm(wv, wvt, rkv)


def _prep_weights(wq_ref, wkv_ref, wqt_ref, wkvt_ref):
    heads = range(MLA_HEADS)
    wq = tuple(wqt_ref[LANES * h:LANES * (h + 1), :].astype(F32) for h in heads)
    wk = tuple(wkvt_ref[LANES * h:LANES * (h + 1), :].astype(F32) for h in heads)
    wv = wkvt_ref[LANES * MLA_HEADS:, :].astype(F32)
    wqt = tuple(wq_ref[:, LANES * h:LANES * (h + 1)].astype(F32) for h in heads)
    wkt = tuple(wkv_ref[:, LANES * h:LANES * (h + 1)].astype(F32) for h in heads)
    wvt = wkv_ref[:, LANES * MLA_HEADS:].astype(F32)
    return wq, wk, wv, wqt, wkt, wvt


def _prep_in_specs(tm):
    const = lambda shape: pl.BlockSpec(shape, lambda i: (0, 0))
    col = lambda height: pl.BlockSpec((height, tm), lambda i: (0, i))
    return [pl.BlockSpec((tm, 512), lambda i: (i, OFF_MLA // 512)),
            const((MLA_Q_LORA, 1)), const((MLA_KV_LORA, 1)), const((LANES, 1)), const((LANES, 1)),
            const((MLA_Q_LORA, 1024)), const((MLA_KV_LORA, 1536)),
            const((1024, MLA_Q_LORA)), const((1536, MLA_KV_LORA)),
            col(LANES), col(LANES), col(LANES)]


def _prep_operands(blk_ref, refs):
    qan_ref, kvan_ref, qn_ref, kn_ref, wq_ref, wkv_ref, wqt_ref, wkvt_ref, c_ref, s1_ref, s2_ref = refs
    blk_t = jnp.transpose(blk_ref[...])
    diff = (blk_t[0:256], blk_t[256:384], blk_t[384:512],
            qan_ref[...], kvan_ref[...], qn_ref[...], kn_ref[...])
    weights = _prep_weights(wq_ref, wkv_ref, wqt_ref, wkvt_ref)
    return diff, weights, (c_ref[...], s1_ref[...], s2_ref[...])


def _mla_prep_fwd(proj, norms, weights, rope):
    s = proj.shape[0]
    tm = min(512, s)

    def body(blk_ref, *refs):
        ins, (q_ref, k_ref, v_ref, qt_ref, kt_ref, vt_ref) = refs[:11], refs[11:]
        diff, (wq, wk, wv, wqt, wkt, wvt), tables = _prep_operands(blk_ref, ins)
        qs, ks, v = _prep_fn(*diff, wq, wk, wv, wqt, wkt, wvt, *tables,
                             lambda w, wt, x: _mm(w, x))
        for h in range(MLA_HEADS):
            q2 = qs[h] * Q_PRESCALE
            qt_ref[LANES * h:LANES * (h + 1), :] = q2.astype(BF16)
            kt_ref[LANES * h:LANES * (h + 1), :] = ks[h].astype(BF16)
            q_ref[:, LANES * h:LANES * (h + 1)] = jnp.transpose(q2).astype(BF16)
            k_ref[:, LANES * h:LANES * (h + 1)] = jnp.transpose(ks[h]).astype(BF16)
        ones_row = (lax.broadcasted_iota(jnp.int32, (V_AUG - MLA_V, v.shape[1]), 0) == 0).astype(BF16)
        for h in range(MLA_HEADS):
            vt_ref[V_AUG * h:V_AUG * h + MLA_V, :] = v[MLA_V * h:MLA_V * (h + 1)].astype(BF16)
            vt_ref[V_AUG * h + MLA_V:V_AUG * (h + 1), :] = ones_row
        v_ref[...] = jnp.transpose(v).astype(BF16)

    row = lambda width: pl.BlockSpec((tm, width), lambda i: (i, 0))
    col = lambda height: pl.BlockSpec((height, tm), lambda i: (0, i))
    return pl.pallas_call(
        body, name="mla_prep_fwd", grid=(s // tm,),
        in_specs=_prep_in_specs(tm),
        out_specs=[row(1024), row(1024), row(512), col(1024), col(1024), col(MLA_HEADS * V_AUG)],
        out_shape=[jax.ShapeDtypeStruct((s, 1024), BF16), jax.ShapeDtypeStruct((s, 1024), BF16),
                   jax.ShapeDtypeStruct((s, 512), BF16), jax.ShapeDtypeStruct((1024, s), BF16),
                   jax.ShapeDtypeStruct((1024, s), BF16),
                   jax.ShapeDtypeStruct((MLA_HEADS * V_AUG, s), BF16)],
        compiler_params=_params(("parallel",)),
    )(proj, *norms, *weights, *rope)


def _mla_prep_bwd(proj, norms, weights, rope, dq, dk, dv):
    s = proj.shape[0]
    tm = min(512, s)

    def body(blk_ref, *refs):
        ins, (dq_ref, dk_ref, dv_ref) = refs[:11], refs[11:14]
        dblk_ref, dqan_ref, dkvan_ref, dqn_ref, dkn_ref, dwq_ref, dwkv_ref = refs[14:]
        diff, (wq, wk, wv, wqt, wkt, wvt), tables = _prep_operands(blk_ref, ins)

        def fn(q_lat, kv_lat, kr, qan, kvan, qn, kn, wq_, wk_, wv_):
            return _prep_fn(q_lat, kv_lat, kr, qan, kvan, qn, kn, wq_, wk_, wv_, wqt, wkt, wvt,
                            *tables, _mmw)

        _, vjp = jax.vjp(fn, *diff, wq, wk, wv)
        heads = range(MLA_HEADS)
        cts = (tuple(dq_ref[LANES * h:LANES * (h + 1), :] for h in heads),
               tuple(dk_ref[LANES * h:LANES * (h + 1), :] for h in heads), dv_ref[...])
        dq_lat, dkv_lat, dkr, dqan, dkvan, dqn, dkn, dwq_h, dwk_h, dwv = vjp(cts)
        dblk_ref[...] = jnp.transpose(
            jnp.concatenate([dq_lat, dkv_lat, dkr], axis=0)).astype(BF16)

        @pl.when(pl.program_id(0) == 0)
        def _():
            for r in (dqan_ref, dkvan_ref, dqn_ref, dkn_ref, dwq_ref, dwkv_ref):
                r[...] = jnp.zeros_like(r)

        dqan_ref[...] += dqan
        dkvan_ref[...] += dkvan
        dqn_ref[...] += dqn
        dkn_ref[...] += dkn
        for h in heads:
            dwq_ref[LANES * h:LANES * (h + 1), :] += dwq_h[h]
            dwkv_ref[LANES * h:LANES * (h + 1), :] += dwk_h[h]
        dwkv_ref[LANES * MLA_HEADS:, :] += dwv

    const = lambda shape: pl.BlockSpec(shape, lambda i: (0, 0))
    col = lambda height: pl.BlockSpec((height, tm), lambda i: (0, i))
    shapes = [(MLA_Q_LORA, 1), (MLA_KV_LORA, 1), (LANES, 1), (LANES, 1),
              (1024, MLA_Q_LORA), (1536, MLA_KV_LORA)]
    return pl.pallas_call(
        body, name="mla_prep_bwd", grid=(s // tm,),
        in_specs=_prep_in_specs(tm) + [col(1024), col(1024), col(512)],
        out_specs=[pl.BlockSpec((tm, 512), lambda i: (i, 0))] + [const(sh) for sh in shapes],
        out_shape=[jax.ShapeDtypeStruct((s, 512), BF16)]
        + [jax.ShapeDtypeStruct(sh, F32) for sh in shapes],
        compiler_params=_params(("arbitrary",)),
    )(proj, *norms, *weights, *rope, dq, dk, dv)


MLA_SCALE = MLA_QK ** -0.5


LOG2E = 1.4426950408889634
LN2 = 0.6931471805599453
Q_PRESCALE = MLA_SCALE * LOG2E
HEAD_GROUPS = ((0, 1),)


def _mla_attn_fwd(q2, k, vt):
    s = q2.shape[0]
    t = min(512, s)
    tk = min(128, s)
    nq = s // t
    r = t // tk

    def body(q_ref, k_ref, vt_ref, o_ref, lse_ref, acc_ref):
        i = pl.program_id(1)
        row = lax.broadcasted_iota(jnp.int32, (tk, t), 0)
        col = lax.broadcasted_iota(jnp.int32, (tk, t), 1)
        qh = [q_ref[:, LANES * hh:LANES * (hh + 1)] for hh in range(2)]
        acc_ref[...] = jnp.zeros_like(acc_ref)

        def scores(j, heads, diag=None):
            r0 = pl.multiple_of(j * tk, tk)
            out = []
            for hh in heads:
                kc = k_ref[pl.ds(r0, tk), LANES * hh:LANES * (hh + 1)]
                sc = lax.dot_general(kc, qh[hh], (((1,), (1,)), ((), ())),
                                     preferred_element_type=F32)
                out.append(sc if diag is None else jnp.where(row + diag * tk <= col, sc, NEG_INF))
            return tuple(out)

        for heads in HEAD_GROUPS:
            stats = tuple(jnp.full((1, t), NEG_INF, F32) for _ in heads)

            def consume(j, scs, stats, heads=heads):
                r0 = pl.multiple_of(j * tk, tk)
                out, ps, alphas = [], [], []
                for n, hh in enumerate(heads):
                    m_new = jnp.maximum(stats[n], jnp.max(scs[n], axis=0, keepdims=True))
                    ps.append(jnp.exp2(scs[n] - m_new).astype(BF16))
                    alphas.append(jnp.exp2(stats[n] - m_new))
                    out.append(m_new)
                for n, hh in enumerate(heads):
                    vc = vt_ref[V_AUG * hh:V_AUG * (hh + 1), pl.ds(r0, tk)]
                    acc_ref[hh] = alphas[n] * acc_ref[hh] + jnp.dot(vc, ps[n],
                                                                    preferred_element_type=F32)
                return tuple(out)

            def group(j0, stats, diag, heads=heads):
                scs = [scores(j0 + d, heads, d if diag else None) for d in range(r)]
                for d in range(r):
                    stats = consume(j0 + d, scs[d], stats)
                return stats

            stats = group(r * i, stats, True)
            stats = lax.fori_loop(0, i, lambda j, st: group(r * j, st, False), stats)
            for n, hh in enumerate(heads):
                l = acc_ref[hh, MLA_V:MLA_V + 1, :]
                o_ref[:, MLA_V * hh:MLA_V * (hh + 1)] = jnp.transpose(acc_ref[hh, 0:MLA_V, :] / l)
                lse_ref[0, hh:hh + 1, :] = stats[n] + jnp.log2(l)

    return pl.pallas_call(
        body, name="mla_attn_fwd", grid=(MLA_HEADS // 2, nq),
        in_specs=[pl.BlockSpec((t, 256), lambda p, i: (i, p)),
                  pl.BlockSpec((s, 256), lambda p, i: (0, p)),
                  pl.BlockSpec((2 * V_AUG, s), lambda p, i: (p, 0))],
        out_specs=[pl.BlockSpec((t, 128), lambda p, i: (i, p)),
                   pl.BlockSpec((1, 2, t), lambda p, i: (p, 0, i))],
        out_shape=[jax.ShapeDtypeStruct((s, 512), F32),
                   jax.ShapeDtypeStruct((MLA_HEADS // 2, 2, s), F32)],
        scratch_shapes=[pltpu.VMEM((2, V_AUG, t), F32)],
        compiler_params=_params(("parallel", "arbitrary")),
    )(q2, k, vt)


def _mla_attn_bwd(q2, q2t, k, kt, v, do, dot, lse_rows, delta_rows):
    s = q2.shape[0]
    t = min(512, s)
    nq = s // t

    def body(q_ref, qt_ref, k_ref, kt_ref, v_ref, do_ref, dot_ref, lse_ref, dl_ref,
             dq_ref, dk_ref, dv_ref):
        j = pl.program_id(1)

        @pl.when(j == 0)
        def _():
            dq_ref[...] = jnp.zeros_like(dq_ref)

        dk_ref[...] = jnp.zeros_like(dk_ref)
        dv_ref[...] = jnp.zeros_like(dv_ref)
        row = lax.broadcasted_iota(jnp.int32, (t, t), 0)
        col = lax.broadcasted_iota(jnp.int32, (t, t), 1)
        causal_t = row <= col
        kh = [k_ref[:, LANES * hh:LANES * (hh + 1)] for hh in range(2)]
        kth = [kt_ref[LANES * hh:LANES * (hh + 1), :] for hh in range(2)]
        vh = [v_ref[:, MLA_V * hh:MLA_V * (hh + 1)] for hh in range(2)]
        nt = (((1,), (1,)), ((), ()))

        def step(i, masked):
            r0 = pl.multiple_of(i * t, t)
            sd = []
            for hh in range(2):
                qh = q_ref[pl.ds(r0, t), LANES * hh:LANES * (hh + 1)]
                doh = do_ref[pl.ds(r0, t), MLA_V * hh:MLA_V * (hh + 1)]
                sc_t = lax.dot_general(kh[hh], qh, nt, preferred_element_type=F32)
                sd.append(jnp.where(causal_t, sc_t, NEG_INF) if masked else sc_t)
                sd.append(lax.dot_general(vh[hh], doh, nt, preferred_element_type=F32))
            for hh in range(2):
                lse = lse_ref[0, hh:hh + 1, pl.ds(r0, t)]
                dl = dl_ref[0, hh:hh + 1, pl.ds(r0, t)]
                p_t = jnp.exp2(sd[2 * hh] - lse)
                g_t = (p_t * (sd[2 * hh + 1] - dl)).astype(BF16)
                qth = qt_ref[LANES * hh:LANES * (hh + 1), pl.ds(r0, t)]
                doth = dot_ref[MLA_V * hh:MLA_V * (hh + 1), pl.ds(r0, t)]
                dv_ref[MLA_V * hh:MLA_V * (hh + 1), :] += lax.dot_general(
                    doth, p_t.astype(BF16), nt, preferred_element_type=F32)
                dk_ref[LANES * hh:LANES * (hh + 1), :] += lax.dot_general(
                    qth, g_t, nt, preferred_element_type=F32)
                dq_ref[LANES * hh:LANES * (hh + 1), pl.ds(r0, t)] += jnp.dot(
                    kth[hh], g_t, preferred_element_type=F32)

        step(j, True)

        def trip(i, carry):
            step(i, False)
            return carry

        lax.fori_loop(j + 1, nq, trip, 0)
        dk_ref[...] = dk_ref[...] * LN2

        @pl.when(j == nq - 1)
        def _():
            dq_ref[...] = dq_ref[...] * MLA_SCALE

    return pl.pallas_call(
        body, name="mla_attn_bwd", grid=(MLA_HEADS // 2, nq),
        in_specs=[pl.BlockSpec((s, 256), lambda p, j: (0, p)),
                  pl.BlockSpec((256, s), lambda p, j: (p, 0)),
                  pl.BlockSpec((t, 256), lambda p, j: (j, p)),
                  pl.BlockSpec((256, t), lambda p, j: (p, j)),
                  pl.BlockSpec((t, 128), lambda p, j: (j, p)),
                  pl.BlockSpec((s, 128), lambda p, j: (0, p)),
                  pl.BlockSpec((128, s), lambda p, j: (p, 0)),
                  pl.BlockSpec((1, 2, s), lambda p, j: (p, 0, 0)),
                  pl.BlockSpec((1, 2, s), lambda p, j: (p, 0, 0))],
        out_specs=[pl.BlockSpec((256, s), lambda p, j: (p, 0)),
                   pl.BlockSpec((256, t), lambda p, j: (p, j)),
                   pl.BlockSpec((128, t), lambda p, j: (p, j))],
        out_shape=[jax.ShapeDtypeStruct((1024, s), F32), jax.ShapeDtypeStruct((1024, s), F32),
                   jax.ShapeDtypeStruct((512, s), F32)],
        compiler_params=_params(("parallel", "arbitrary")),
    )(q2, q2t, k, kt, v, do, dot, lse_rows, delta_rows)


SWA_SCALE = SWA_DIM ** -0.5
SWA_COLS = SWA_GROUP * BLOCK
SWA_LOG2 = SWA_SCALE * LOG2E


def _swa_tables():
    k = np.arange(2 * BLOCK)[:, None]
    col = np.arange(SWA_COLS)[None, :]
    dist = BLOCK + (col % BLOCK) - k
    valid = (dist >= 0) & (dist < BLOCK)
    out = np.zeros((2, SWA_KV, 2 * BLOCK, SWA_COLS), np.float32)
    for first in range(2):
        ok = valid & ((k >= BLOCK) | (first == 0))
        for j in range(SWA_KV):
            slope = 2.0 ** -(SWA_GROUP * j + col // BLOCK + 1)
            out[first, j] = np.where(ok, -slope * dist * LOG2E, NEG_INF)
    return jnp.asarray(out)


def _swa_tile_inputs(sq_ref, skv_ref, halo_ref, qn_ref, kn_ref, sk_ref, add_ref, first):
    tokens = sq_ref.shape[0]
    kv_all = jnp.concatenate([halo_ref[...], skv_ref[...]], axis=0)
    kv_t = jnp.transpose(kv_all)
    sq_t = jnp.transpose(sq_ref[...])
    k_raw = [kv_t[SWA_DIM * j:SWA_DIM * (j + 1)] for j in range(SWA_KV)]
    v_t = [kv_t[128 + SWA_DIM * j:128 + SWA_DIM * (j + 1)] for j in range(SWA_KV)]
    v_nat = [kv_all[:, 128 + SWA_DIM * j:128 + SWA_DIM * (j + 1)] for j in range(SWA_KV)]
    q_raw = [sq_t[SWA_DIM * h:SWA_DIM * (h + 1)] for h in range(SWA_HEADS)]
    qn_b = jnp.broadcast_to(qn_ref[...], (SWA_DIM, tokens))
    kn_b = jnp.broadcast_to(kn_ref[...], (SWA_DIM, tokens + BLOCK))
    lane_grp = lax.broadcasted_iota(jnp.int32, (1, SWA_COLS), 1) // BLOCK
    sinks, adds = [], []
    for j in range(SWA_KV):
        row = jnp.zeros((1, SWA_COLS), F32)
        for g in range(SWA_GROUP):
            h = SWA_GROUP * j + g
            row = jnp.where(lane_grp == g, sk_ref[:, h:h + 1] * LOG2E, row)
        sinks.append(row)
        adds.append((jnp.where(first, add_ref[1, j], add_ref[0, j]), add_ref[0, j]))
    return k_raw, v_t, v_nat, q_raw, qn_b, kn_b, sinks, adds


def _swa_probs(kb, qs_t, add, sink):
    s2 = jnp.dot(kb, qs_t, preferred_element_type=F32) * SWA_LOG2 + add
    m = jnp.maximum(jnp.max(s2, axis=0, keepdims=True), sink)
    e = jnp.exp2(s2 - m)
    es = jnp.exp2(sink - m)
    inv = 1.0 / (jnp.sum(e, axis=0, keepdims=True) + es)
    return e, inv, es


def _swa_queries(qn_t, j, b):
    return jnp.concatenate([qn_t[SWA_GROUP * j + g][:, BLOCK * b:BLOCK * (b + 1)]
                            for g in range(SWA_GROUP)], axis=1)


def _swa_fwd(proj, qn, kn, sinks, tables):
    s = proj.shape[0]
    ts = min(1024, s)
    nb = ts // BLOCK

    def body(sq_ref, skv_ref, halo_ref, qn_ref, kn_ref, sk_ref, add_ref, o_ref, ot_ref):
        first = pl.program_id(0) == 0
        k_raw, v_t, _, q_raw, qn_b, kn_b, sink_rows, adds = _swa_tile_inputs(
            sq_ref, skv_ref, halo_ref, qn_ref, kn_ref, sk_ref, add_ref, first)
        kn_nat = [jnp.transpose(_rms0(k, kn_b)).astype(BF16) for k in k_raw]
        v_t = [v.astype(BF16) for v in v_t]
        qn_t = [_rms0(q, qn_b).astype(BF16) for q in q_raw]
        for b in range(nb):
            band = slice(BLOCK * b, BLOCK * (b + 2))
            for j in range(SWA_KV):
                e, inv, _ = _swa_probs(kn_nat[j][band], _swa_queries(qn_t, j, b),
                                       adds[j][0 if b == 0 else 1], sink_rows[j])
                o_t = jnp.dot(v_t[j][:, band], (e * inv).astype(BF16),
                              preferred_element_type=F32)
                for g in range(SWA_GROUP):
                    h = SWA_GROUP * j + g
                    ot_ref[SWA_DIM * h:SWA_DIM * (h + 1), BLOCK * b:BLOCK * (b + 1)] = (
                        o_t[:, BLOCK * g:BLOCK * (g + 1)])
        o_ref[...] = jnp.transpose(ot_ref[...])

    const = lambda shape: pl.BlockSpec(shape, lambda i: (0,) * len(shape))
    return pl.pallas_call(
        body, name="swa_fwd", grid=(s // ts,),
        in_specs=[pl.BlockSpec((ts, 512), lambda i: (i, OFF_SQ // 512)),
                  pl.BlockSpec((ts, 256), lambda i: (i, OFF_SKV // 256)),
                  pl.BlockSpec((BLOCK, 256), lambda i: (jnp.maximum(i * nb - 1, 0), OFF_SKV // 256)),
                  const((SWA_DIM, 1)), const((SWA_DIM, 1)), const((1, SWA_HEADS)),
                  const(tables.shape)],
        out_specs=pl.BlockSpec((ts, 512), lambda i: (i, 0)),
        out_shape=jax.ShapeDtypeStruct((s, 512), F32),
        scratch_shapes=[pltpu.VMEM((512, ts), F32)],
        compiler_params=_params(("parallel",)),
    )(proj, proj, proj, qn, kn, sinks, tables)


def _swa_bwd(proj, qn, kn, sinks, tables, do):
    s = proj.shape[0]
    ts = min(1024, s)
    nb = ts // BLOCK
    nt = s // ts

    def body(sq_ref, skv_ref, halo_ref, qn_ref, kn_ref, sk_ref, add_ref, do_ref,
             dsq_ref, dskv_ref, dqn_ref, dkn_ref, dsk_ref, carry_ref, dqt_ref, dkvt_ref):
        step = pl.program_id(0)
        first = step == nt - 1

        @pl.when(step == 0)
        def _():
            carry_ref[...] = jnp.zeros_like(carry_ref)
            dqn_ref[...] = jnp.zeros_like(dqn_ref)
            dkn_ref[...] = jnp.zeros_like(dkn_ref)
            dsk_ref[...] = jnp.zeros_like(dsk_ref)

        k_raw, v_t, v_nat, q_raw, qn_b, kn_b, sink_rows, adds = _swa_tile_inputs(
            sq_ref, skv_ref, halo_ref, qn_ref, kn_ref, sk_ref, add_ref, first)
        kn_f = [_rms0(k, kn_b) for k in k_raw]
        kn_t = [k.astype(BF16) for k in kn_f]
        kn_nat = [jnp.transpose(k).astype(BF16) for k in kn_f]
        v_nat = [v.astype(BF16) for v in v_nat]
        qn_t = [_rms0(q, qn_b).astype(BF16) for q in q_raw]
        do_t = jnp.transpose(do_ref[...].astype(F32)).astype(BF16)

        dkvt_ref[...] = jnp.zeros_like(dkvt_ref)
        dsink = [jnp.zeros((1, SWA_COLS), F32) for _ in range(SWA_KV)]
        nt_dims = (((1,), (1,)), ((), ()))
        for b in range(nb):
            rows = slice(BLOCK * b, BLOCK * (b + 1))
            band = slice(BLOCK * b, BLOCK * (b + 2))
            for j in range(SWA_KV):
                heads = [SWA_GROUP * j + g for g in range(SWA_GROUP)]
                qs_t = _swa_queries(qn_t, j, b)
                dos_t = jnp.concatenate([do_t[SWA_DIM * h:SWA_DIM * (h + 1), rows] for h in heads],
                                        axis=1)
                e, inv, es = _swa_probs(kn_nat[j][band], qs_t, adds[j][0 if b == 0 else 1],
                                        sink_rows[j])
                p = e * inv
                dp = jnp.dot(v_nat[j][band], dos_t, preferred_element_type=F32)
                dsum = jnp.sum(p * dp, axis=0, keepdims=True)
                dsink[j] = dsink[j] - es * inv * dsum
                g_t = (p * (dp - dsum) * SWA_SCALE).astype(BF16)
                dv_t = lax.dot_general(dos_t, p.astype(BF16), nt_dims,
                                       preferred_element_type=F32)
                dk_t = lax.dot_general(qs_t, g_t, nt_dims, preferred_element_type=F32)
                dq_t = jnp.dot(kn_t[j][:, band], g_t, preferred_element_type=F32)
                dkvt_ref[SWA_DIM * j:SWA_DIM * (j + 1), band] += dk_t
                dkvt_ref[128 + SWA_DIM * j:128 + SWA_DIM * (j + 1), band] += dv_t
                for g, h in enumerate(heads):
                    dqt_ref[SWA_DIM * h:SWA_DIM * (h + 1), rows] = dq_t[:, BLOCK * g:BLOCK * (g + 1)]

        dqn = jnp.zeros((SWA_DIM, 1), F32)
        for h in range(SWA_HEADS):
            _, vjp = jax.vjp(_rms0, q_raw[h], qn_ref[...])
            dq, dg = vjp(dqt_ref[SWA_DIM * h:SWA_DIM * (h + 1), :])
            dqt_ref[SWA_DIM * h:SWA_DIM * (h + 1), :] = dq
            dqn = dqn + dg
        dqn_ref[...] += dqn
        dsq_ref[...] = jnp.transpose(dqt_ref[...]).astype(BF16)
        dkn = jnp.zeros((SWA_DIM, 1), F32)
        lane_grp = lax.broadcasted_iota(jnp.int32, (1, SWA_COLS), 1) // BLOCK
        for j in range(SWA_KV):
            _, vjp = jax.vjp(_rms0, k_raw[j], kn_ref[...])
            dk, dg = vjp(dkvt_ref[SWA_DIM * j:SWA_DIM * (j + 1), :])
            dkvt_ref[SWA_DIM * j:SWA_DIM * (j + 1), :] = dk
            dkn = dkn + dg
            for g in range(SWA_GROUP):
                h = SWA_GROUP * j + g
                dsk_ref[:, h:h + 1] += jnp.sum(jnp.where(lane_grp == g, dsink[j], 0.0), axis=1,
                                               keepdims=True)
        dkn_ref[...] += dkn
        dkv = jnp.transpose(dkvt_ref[...])
        dskv_ref[0:ts - BLOCK, :] = dkv[BLOCK:ts].astype(BF16)
        dskv_ref[ts - BLOCK:ts, :] = (dkv[ts:ts + BLOCK] + carry_ref[...]).astype(BF16)
        carry_ref[...] = dkv[0:BLOCK]

    const = lambda shape: pl.BlockSpec(shape, lambda st: (0,) * len(shape))
    return pl.pallas_call(
        body, name="swa_bwd", grid=(nt,),
        in_specs=[pl.BlockSpec((ts, 512), lambda st: (nt - 1 - st, OFF_SQ // 512)),
                  pl.BlockSpec((ts, 256), lambda st: (nt - 1 - st, OFF_SKV // 256)),
                  pl.BlockSpec((BLOCK, 256),
                               lambda st: (jnp.maximum((nt - 1 - st) * nb - 1, 0), OFF_SKV // 256)),
                  const((SWA_DIM, 1)), const((SWA_DIM, 1)), const((1, SWA_HEADS)),
                  const(tables.shape),
                  pl.BlockSpec((ts, 512), lambda st: (nt - 1 - st, 0))],
        out_specs=[pl.BlockSpec((ts, 512), lambda st: (nt - 1 - st, 0)),
                   pl.BlockSpec((ts, 256), lambda st: (nt - 1 - st, 0)),
                   const((SWA_DIM, 1)), const((SWA_DIM, 1)), const((1, SWA_HEADS))],
        out_shape=[jax.ShapeDtypeStruct((s, 512), BF16), jax.ShapeDtypeStruct((s, 256), BF16),
                   jax.ShapeDtypeStruct((SWA_DIM, 1), F32), jax.ShapeDtypeStruct((SWA_DIM, 1), F32),
                   jax.ShapeDtypeStruct((1, SWA_HEADS), F32)],
        scratch_shapes=[pltpu.VMEM((BLOCK, 256), F32), pltpu.VMEM((512, ts), F32),
                        pltpu.VMEM((256, ts + BLOCK), F32)],
        compiler_params=_params(("arbitrary",)),
    )(proj, proj, proj, qn, kn, sinks, tables, do)


HALO = 8


def _shift_down(u, halo, k):
    tm = u.shape[0]
    rid = lax.broadcasted_iota(jnp.int32, u.shape, 0)
    out = pltpu.roll(u, k, 0)
    for r in range(k):
        out = jnp.where(rid == r, halo[HALO - k + r:HALO - k + r + 1, :], out)
    return out


def _shift_up(u, halo, k):
    tm = u.shape[0]
    rid = lax.broadcasted_iota(jnp.int32, u.shape, 0)
    out = pltpu.roll(u, tm - k, 0)
    for r in range(k):
        out = jnp.where(rid == tm - k + r, halo[r:r + 1, :], out)
    return out


def _conv_fwd_vals(conv_ref, convp_ref, cw_ref, is_first):
    c_h, c_b, c_c = conv_ref[:, 0:512], conv_ref[:, 512:1024], conv_ref[:, 1024:1536]
    u = c_c * c_h
    up = jnp.where(is_first, 0.0, convp_ref[:, 1024:1536] * convp_ref[:, 0:512])
    u1 = _shift_down(u, up, 1)
    u2 = _shift_down(u, up, 2)
    yc = cw_ref[0:1, :] * u2 + cw_ref[1:2, :] * u1 + cw_ref[2:3, :] * u
    return c_h, c_b, c_c, u, u1, u2, yc


def _out_fwd(proj, o_mla, o_swa, x, w_out, cw):
    s = proj.shape[0]
    tm = min(512, s)

    def body(conv_ref, convp_ref, gates_ref, om_ref, os_ref, x_ref, w_ref, cw_ref, y_ref, zt_ref,
             z_ref):
        i = pl.program_id(0)
        _, c_b, _, _, _, _, yc = _conv_fwd_vals(conv_ref, convp_ref, cw_ref, i == 0)
        mix = (om_ref[...], c_b * yc, os_ref[...])
        for n in range(3):
            g = gates_ref[:, GROUP * n:GROUP * (n + 1)]
            z = mix[n] * (g * _sigmoid(g))
            z_ref[:, GROUP * n:GROUP * (n + 1)] = z.astype(BF16)
            zt_ref[GROUP * n:GROUP * (n + 1), :] = jnp.transpose(z).astype(BF16)
        y_ref[...] = x_ref[...] + jnp.dot(z_ref[...], w_ref[...], preferred_element_type=F32)

    row = lambda width: pl.BlockSpec((tm, width), lambda i: (i, 0))
    return pl.pallas_call(
        body, name="out_fwd", grid=(s // tm,),
        in_specs=[pl.BlockSpec((tm, 1536), lambda i: (i, 0)),
                  pl.BlockSpec((HALO, 1536), lambda i: (jnp.maximum(i * (tm // HALO) - 1, 0), 0)),
                  pl.BlockSpec((tm, 1536), lambda i: (i, 1)),
                  row(512), row(512), row(D_MODEL),
                  pl.BlockSpec((D_MIX, D_MODEL), lambda i: (0, 0)),
                  pl.BlockSpec((HALO, 512), lambda i: (0, 0))],
        out_specs=[row(D_MODEL), pl.BlockSpec((D_MIX, tm), lambda i: (0, i))],
        out_shape=[jax.ShapeDtypeStruct((s, D_MODEL), F32), jax.ShapeDtypeStruct((D_MIX, s), BF16)],
        scratch_shapes=[pltpu.VMEM((tm, D_MIX), BF16)],
        compiler_params=_params(("parallel",)),
    )(proj, proj, proj, o_mla, o_swa, x, w_out, cw)


def _out_bwd(dy, proj, o_mla, o_swa, w_out, cw):
    s = proj.shape[0]
    tm = min(512, s)
    nt = s // tm
    hb = tm // HALO

    def body(dy_ref, dyn_ref, conv_ref, convp_ref, convn_ref, gates_ref, gatesn_ref, om_ref, os_ref,
             w_ref, cw_ref,
             dconv_ref, dgates_ref, dom_ref, domt_ref, delta_ref, dos_ref, dcw_ref):
        i = pl.program_id(0)
        dz = _mm_nt(dy_ref[...], w_ref[...])

        def gate(n):
            g = gates_ref[:, GROUP * n:GROUP * (n + 1)]
            sg = _sigmoid(g)
            return g * sg, sg * (1.0 + g * (1.0 - sg))

        for n, o_ref, do_ref in ((0, om_ref, dom_ref), (2, os_ref, dos_ref)):
            silu, dsilu = gate(n)
            dzn = dz[:, GROUP * n:GROUP * (n + 1)]
            o = o_ref[...]
            do = dzn * silu
            do_ref[...] = do.astype(do_ref.dtype)
            dgates_ref[:, GROUP * n:GROUP * (n + 1)] = (dzn * o * dsilu).astype(BF16)
            if n == 0:
                domt_ref[...] = jnp.transpose(do).astype(BF16)
                t = do * o
                for h in range(MLA_HEADS):
                    delta_ref[:, h:h + 1] = jnp.sum(t[:, MLA_V * h:MLA_V * (h + 1)], axis=-1,
                                                    keepdims=True)

        c_h, c_b, c_c, u, u1, u2, yc = _conv_fwd_vals(conv_ref, convp_ref, cw_ref, i == 0)
        silu, dsilu = gate(1)
        dzc = dz[:, GROUP:2 * GROUP]
        dgates_ref[:, GROUP:2 * GROUP] = (dzc * (c_b * yc) * dsilu).astype(BF16)
        dycr = dzc * silu
        dyc = dycr * c_b
        gn = gatesn_ref[:, GROUP:2 * GROUP]
        dzc_n = _mm_nt(dyn_ref[...], w_ref[GROUP:2 * GROUP, :])
        dyc_n = jnp.where(i == nt - 1, 0.0, dzc_n * (gn * _sigmoid(gn)) * convn_ref[:, 512:1024])
        d1 = _shift_up(dyc, dyc_n, 1)
        d2 = _shift_up(dyc, dyc_n, 2)
        du = cw_ref[2:3, :] * dyc + cw_ref[1:2, :] * d1 + cw_ref[0:1, :] * d2
        dconv_ref[:, 0:512] = (du * c_c).astype(BF16)
        dconv_ref[:, 512:1024] = (dycr * yc).astype(BF16)
        dconv_ref[:, 1024:1536] = (du * c_h).astype(BF16)

        @pl.when(i == 0)
        def _():
            dcw_ref[...] = jnp.zeros_like(dcw_ref)

        for k, uk in enumerate((u2, u1, u)):
            dcw_ref[k:k + 1, :] += jnp.sum(dyc * uk, axis=0, keepdims=True)

    row = lambda width: pl.BlockSpec((tm, width), lambda i: (i, 0))
    prev = lambda i: jnp.maximum(i * hb - 1, 0)
    nxt = lambda i: jnp.minimum((i + 1) * hb, s // HALO - 1)
    return pl.pallas_call(
        body, name="out_bwd", grid=(nt,),
        in_specs=[row(D_MODEL),
                  pl.BlockSpec((HALO, D_MODEL), lambda i: (nxt(i), 0)),
                  pl.BlockSpec((tm, 1536), lambda i: (i, 0)),
                  pl.BlockSpec((HALO, 1536), lambda i: (prev(i), 0)),
                  pl.BlockSpec((HALO, 1536), lambda i: (nxt(i), 0)),
                  pl.BlockSpec((tm, 1536), lambda i: (i, 1)),
                  pl.BlockSpec((HALO, 1536), lambda i: (nxt(i), 1)),
                  row(512), row(512),
                  pl.BlockSpec((D_MIX, D_MODEL), lambda i: (0, 0)),
                  pl.BlockSpec((HALO, 512), lambda i: (0, 0))],
        out_specs=[row(1536), row(1536), row(512), pl.BlockSpec((512, tm), lambda i: (0, i)),
                   row(MLA_HEADS), row(512), pl.BlockSpec((HALO, 512), lambda i: (0, 0))],
        out_shape=[jax.ShapeDtypeStruct((s, 1536), BF16), jax.ShapeDtypeStruct((s, 1536), BF16),
                   jax.ShapeDtypeStruct((s, 512), BF16), jax.ShapeDtypeStruct((512, s), BF16),
                   jax.ShapeDtypeStruct((s, MLA_HEADS), F32),
                   jax.ShapeDtypeStruct((s, 512), BF16), jax.ShapeDtypeStruct((HALO, 512), F32)],
        compiler_params=_params(("arbitrary",)),
    )(dy, dy, proj, proj, proj, proj, proj, o_mla, o_swa, w_out, cw)


def _loss_head(y, target):
    s, d = y.shape
    tm = min(512, s)
    nt = s // tm

    def body(y_ref, t_ref, dy_ref, loss_ref):
        i = pl.program_id(0)
        err = y_ref[...] - t_ref[...]
        dy_ref[...] = err * (1.0 / d)

        @pl.when(i == 0)
        def _():
            loss_ref[...] = jnp.zeros_like(loss_ref)

        sq = jnp.sum((err * err).reshape(tm // 8, 8, d), axis=0)
        part = sq[:, 0:LANES]
        for c in range(1, d // LANES):
            part = part + sq[:, LANES * c:LANES * (c + 1)]
        loss_ref[...] += part

        @pl.when(i == nt - 1)
        def _():
            loss_ref[...] = jnp.full(loss_ref.shape, (0.5 / d) * jnp.sum(loss_ref[...]), F32)

    return pl.pallas_call(
        body, name="loss_head", grid=(nt,),
        in_specs=[pl.BlockSpec((tm, d), lambda i: (i, 0)), pl.BlockSpec((tm, d), lambda i: (i, 0))],
        out_specs=[pl.BlockSpec((tm, d), lambda i: (i, 0)), pl.BlockSpec((8, LANES), lambda i: (0, 0))],
        out_shape=[jax.ShapeDtypeStruct((s, d), F32), jax.ShapeDtypeStruct((8, LANES), F32)],
        compiler_params=_params(("arbitrary",)),
    )(y, target)


def _adamw(g, w, m, v):
    rows = g.shape[0]
    tr = min(256, rows)
    c1 = 1.0 - ADAM_B1
    c2 = 1.0 - ADAM_B2
    bc1 = 1.0 - ADAM_B1 ** ADAM_STEP
    bc2 = 1.0 - ADAM_B2 ** ADAM_STEP

    def body(g_ref, w_ref, m_ref, v_ref, d_ref, mo_ref, vo_ref):
        gg = g_ref[...]
        m_new = ADAM_B1 * m_ref[...] + c1 * gg
        v_new = ADAM_B2 * v_ref[...] + c2 * (gg * gg)
        m_hat = m_new / bc1
        v_hat = v_new / bc2
        d_ref[...] = -ADAM_LR * (m_hat / (jnp.sqrt(v_hat) + ADAM_EPS) + ADAM_WD * w_ref[...])
        mo_ref[...] = m_new
        vo_ref[...] = v_new

    spec = pl.BlockSpec((tr, g.shape[1]), lambda i: (i, 0))
    return pl.pallas_call(
        body, name="adamw", grid=(rows // tr,),
        in_specs=[spec] * 4, out_specs=[spec] * 3,
        out_shape=[jax.ShapeDtypeStruct(g.shape, F32)] * 3,
        compiler_params=_params(("parallel",)),
    )(g, w, m, v)


HBM_SPEC = pl.BlockSpec(memory_space=pltpu.HBM)


def _place():
    x, y, c = lax.axis_index("x"), lax.axis_index("y"), lax.axis_index("c")
    chips = [(1 - x, y), (x, 1 - y), (1 - x, 1 - y)]
    return x, y, c, chips


def _all_gather(shards):
    na = len(shards)
    halves = [sh.shape[0] // 2 for sh in shards]

    def body(*refs):
        w_refs, a_refs = refs[:na], refs[na:2 * na]
        send_sems, recv_sems = refs[2 * na:]
        x, y, c, chips = _place()
        k = 2 * x + y
        sib = (x, y, 1 - c)

        def slab(a, kk, hc):
            return a_refs[a].at[kk, pl.ds(hc * halves[a], halves[a]), :]

        def copy(a, n, src, dst, to):
            return pltpu.make_async_remote_copy(
                src_ref=src, dst_ref=dst, send_sem=send_sems.at[6 * a + n],
                recv_sem=recv_sems.at[6 * a + n], device_id=to, device_id_type=MESH)

        first = [copy(a, n, w_refs[a].at[pl.ds(c * halves[a], halves[a]), :], slab(a, k, c),
                      (cx, cy, c))
                 for n, (cx, cy) in enumerate(chips) for a in range(na)]
        for cp in first:
            cp.start()
        passed = []
        for n, (cx, cy) in enumerate(chips):
            kk = 2 * cx + cy
            for a in range(na):
                copy(a, n, slab(a, kk, c), slab(a, kk, c), (cx, cy, c)).wait_recv()
                fwd = copy(a, 3 + n, slab(a, kk, c), slab(a, kk, c), sib)
                fwd.start()
                passed.append(fwd)
        for n, (cx, cy) in enumerate(chips):
            kk = 2 * cx + cy
            for a in range(na):
                copy(a, 3 + n, slab(a, kk, 1 - c), slab(a, kk, 1 - c), sib).wait_recv()
        for cp in first + passed:
            cp.wait_send()

    return pl.pallas_call(
        body, name="weights_all_gather",
        in_specs=[HBM_SPEC] * na, out_specs=[HBM_SPEC] * na,
        out_shape=[jax.ShapeDtypeStruct((N_CHIPS,) + sh.shape, sh.dtype) for sh in shards],
        scratch_shapes=[pltpu.SemaphoreType.DMA((6 * na,)), pltpu.SemaphoreType.DMA((6 * na,))],
    )(*shards)


def _fill_own_slab(buf, src, k_idx):
    n, rows, cols = buf.shape
    tr = _row_tile(rows)
    slabs = src.ndim == 3

    def body(k_ref, src_ref, buf_ref, out_ref):
        out_ref[0] = src_ref[0] if slabs else src_ref[...]

    if slabs:
        src_spec = pl.BlockSpec((1, tr, cols), lambda t, k_ref: (k_ref[0], t, 0))
    else:
        src_spec = pl.BlockSpec((tr, cols), lambda t, k_ref: (t, 0))
    return pl.pallas_call(
        body, name="fill_own_slab",
        grid_spec=pltpu.PrefetchScalarGridSpec(
            num_scalar_prefetch=1, grid=(rows // tr,),
            in_specs=[src_spec, pl.BlockSpec(memory_space=pl.ANY)],
            out_specs=pl.BlockSpec((1, tr, cols), lambda t, k_ref: (k_ref[0], t, 0))),
        out_shape=jax.ShapeDtypeStruct(buf.shape, buf.dtype),
        input_output_aliases={2: 0},
        compiler_params=_params(("parallel",)),
    )(k_idx, src, buf)


def _swap_halves_to_sibling(gs):
    na = len(gs)

    def body(*refs):
        g_refs, r_refs = refs[:na], refs[na:2 * na]
        send_sems, recv_sems = refs[2 * na:]
        x, y, c, _ = _place()
        cps = []
        for a in range(na):
            half = g_refs[a].shape[1] // 2
            cps.append(pltpu.make_async_remote_copy(
                src_ref=g_refs[a].at[:, pl.ds((1 - c) * half, half), :], dst_ref=r_refs[a],
                send_sem=send_sems.at[a], recv_sem=recv_sems.at[a], device_id=(x, y, 1 - c),
                device_id_type=MESH))
        for cp in cps:
            cp.start()
        for cp in cps:
            cp.wait()

    return pl.pallas_call(
        body, name="grads_to_sibling",
        in_specs=[HBM_SPEC] * na, out_specs=[HBM_SPEC] * na,
        out_shape=[jax.ShapeDtypeStruct((g.shape[0], g.shape[1] // 2, g.shape[2]), g.dtype)
                   for g in gs],
        scratch_shapes=[pltpu.SemaphoreType.DMA((na,)), pltpu.SemaphoreType.DMA((na,))],
    )(*gs)


def _row_tile(rows):
    return 256 if rows % 256 == 0 else 128


def _add_sibling(g, r, c_idx, out_dtype):
    n, rows, cols = g.shape
    half = rows // 2
    tr = _row_tile(half)
    nb = half // tr

    def body(c_ref, g_ref, r_ref, p_ref):
        p_ref[...] = (g_ref[...] + r_ref[...]).astype(out_dtype)

    return pl.pallas_call(
        body, name="grads_add_sibling",
        grid_spec=pltpu.PrefetchScalarGridSpec(
            num_scalar_prefetch=1, grid=(n, nb),
            in_specs=[pl.BlockSpec((1, tr, cols), lambda j, t, c_ref: (j, c_ref[0] * nb + t, 0)),
                      pl.BlockSpec((1, tr, cols), lambda j, t, c_ref: (j, t, 0))],
            out_specs=pl.BlockSpec((1, tr, cols), lambda j, t, c_ref: (j, t, 0))),
        out_shape=jax.ShapeDtypeStruct((n, half, cols), out_dtype),
        compiler_params=_params(("parallel", "parallel")),
    )(c_idx, g, r)


def _scatter_to_chips(ps):
    na = len(ps)

    def body(*refs):
        p_refs, q_refs = refs[:na], refs[na:2 * na]
        send_sems, recv_sems = refs[2 * na:]
        x, y, c, chips = _place()
        k = 2 * x + y
        sends = []
        for i, (cx, cy) in enumerate(chips):
            for a in range(na):
                cp = pltpu.make_async_remote_copy(
                    src_ref=p_refs[a].at[2 * cx + cy], dst_ref=q_refs[a].at[k],
                    send_sem=send_sems.at[3 * a + i], recv_sem=recv_sems.at[3 * a + i],
                    device_id=(cx, cy, c), device_id_type=MESH)
                cp.start()
                sends.append(cp)
        for i, (cx, cy) in enumerate(chips):
            kk = 2 * cx + cy
            for a in range(na):
                pltpu.make_async_remote_copy(
                    src_ref=p_refs[a].at[kk], dst_ref=q_refs[a].at[kk],
                    send_sem=send_sems.at[3 * a + i], recv_sem=recv_sems.at[3 * a + i],
                    device_id=(cx, cy, c), device_id_type=MESH).wait_recv()
        for cp in sends:
            cp.wait_send()

    return pl.pallas_call(
        body, name="grads_scatter_to_chips",
        in_specs=[HBM_SPEC] * na, out_specs=[HBM_SPEC] * na,
        out_shape=[jax.ShapeDtypeStruct(p.shape, p.dtype) for p in ps],
        scratch_shapes=[pltpu.SemaphoreType.DMA((3 * na,)), pltpu.SemaphoreType.DMA((3 * na,))],
    )(*ps)


def _sum_chips(q, c_idx):
    n, half, cols = q.shape
    tr = _row_tile(half)
    nb = half // tr

    def body(c_ref, q_ref, o_ref):
        parts = [q_ref[kk].astype(F32) for kk in range(n)]
        o_ref[...] = ((parts[0] + parts[1]) + parts[2]) + parts[3]

    return pl.pallas_call(
        body, name="grads_sum_chips",
        grid_spec=pltpu.PrefetchScalarGridSpec(
            num_scalar_prefetch=1, grid=(nb,),
            in_specs=[pl.BlockSpec((n, tr, cols), lambda t, c_ref: (0, t, 0))],
            out_specs=pl.BlockSpec((tr, cols), lambda t, c_ref: (c_ref[0] * nb + t, 0))),
        out_shape=jax.ShapeDtypeStruct((2 * half, cols), F32),
        compiler_params=_params(("parallel",)),
    )(c_idx, q)


def _join_halves(fulls):
    na = len(fulls)

    def body(*refs):
        o_refs = refs[na:2 * na]
        send_sems, recv_sems = refs[2 * na:]
        x, y, c, _ = _place()
        sends = []
        for a in range(na):
            half = o_refs[a].shape[0] // 2
            rows = o_refs[a].at[pl.ds(c * half, half), :]
            sends.append(pltpu.make_async_remote_copy(
                src_ref=rows, dst_ref=rows, send_sem=send_sems.at[a], recv_sem=recv_sems.at[a],
                device_id=(x, y, 1 - c), device_id_type=MESH))
        for cp in sends:
            cp.start()
        for a in range(na):
            half = o_refs[a].shape[0] // 2
            other = o_refs[a].at[pl.ds((1 - c) * half, half), :]
            pltpu.make_async_remote_copy(
                src_ref=other, dst_ref=other, send_sem=send_sems.at[a], recv_sem=recv_sems.at[a],
                device_id=(x, y, 1 - c), device_id_type=MESH).wait_recv()
        for cp in sends:
            cp.wait_send()

    return pl.pallas_call(
        body, name="grads_join_halves",
        in_specs=[HBM_SPEC] * na, out_specs=[HBM_SPEC] * na,
        out_shape=[jax.ShapeDtypeStruct(f.shape, f.dtype) for f in fulls],
        input_output_aliases={a: a for a in range(na)},
        scratch_shapes=[pltpu.SemaphoreType.DMA((na,)), pltpu.SemaphoreType.DMA((na,))],
    )(*fulls)


def _part_rows(shape):
    size = 1
    for d in shape:
        size *= d
    rows = -(-size // PACK_COLS)
    return size, -(-rows // PACK_ROW_ALIGN) * PACK_ROW_ALIGN


def _pack_rows(arrays, dtype, total_rows):
    parts, used = [], 0
    for a in arrays:
        size, rows = _part_rows(a.shape)
        flat = a.reshape(-1).astype(dtype)
        parts.append(jnp.pad(flat, (0, rows * PACK_COLS - size)).reshape(rows, PACK_COLS))
        used += rows
    parts.append(jnp.zeros((total_rows - used, PACK_COLS), dtype))
    return jnp.concatenate(parts, axis=0)


def _unpack_rows(buf, shapes):
    lead = buf.shape[:-2]
    out, off = [], 0
    for sh in shapes:
        size, rows = _part_rows(sh)
        part = buf[..., off:off + rows, :].reshape(lead + (-1,))[..., :size]
        out.append(part.reshape(lead + tuple(sh)))
        off += rows
    return out


NEW_ORDER = ((928, 1440), (1440, 1952), (1952, 2464), (416, 928), (2464, 2976), (3744, 4256),
             (2976, 3488), (0, 256), (256, 384), (4256, 4320), (384, 416), (4256, 4288),
             (3488, 3616), (3616, 3744))
OLD_ORDER = ((3584, 3840), (3840, 3968), (4032, 4064), (1536, 2048), (0, 512), (512, 1024),
             (1024, 1536), (2048, 2560), (3072, 3584), (4096, 4224), (4224, 4352), (2560, 3072))


def _cols(sources, ranges):
    parts = []
    for a, b in ranges:
        off = 0
        for src in sources:
            width = src.shape[-1]
            lo, hi = max(a, off), min(b, off + width)
            if lo < hi:
                parts.append(src[..., lo - off:hi - off])
            off += width
    return jnp.concatenate(parts, axis=-1)


def _sub_ranges(ranges, a, b):
    out, off = [], 0
    for lo, hi in ranges:
        width = hi - lo
        s0, s1 = max(a, off), min(b, off + width)
        if s0 < s1:
            out.append((lo + s0 - off, lo + s1 - off))
        off += width
    return out


def _rope_tables(s):
    half = MLA_ROPE // 2
    inv_freq = jnp.power(jnp.float32(ROPE_THETA), -jnp.arange(half, dtype=F32) / half)
    ang = inv_freq[:, None] * jnp.arange(s, dtype=F32)[None, :]
    cos, sin = jnp.cos(ang), jnp.sin(ang)
    z = lambda n: jnp.zeros((n, s), F32)
    c = jnp.concatenate([jnp.ones((MLA_NOPE, s), F32), cos, cos, z(32)], axis=0)
    s1 = jnp.concatenate([z(MLA_NOPE), -sin, z(16), z(32)], axis=0)
    s2 = jnp.concatenate([z(MLA_NOPE), z(16), sin, z(32)], axis=0)
    return c, s1, s2


def _pad_lanes(a, n):
    return jnp.pad(a, ((0, 0), (0, n - a.shape[1])))


SHARDED = ("w_in", "w_out", "mla_w_qb", "mla_w_kvb", "conv_w")
REPLICATED = ("norm_g", "mla_q_a_norm", "mla_kv_a_norm", "mla_q_norm", "mla_k_norm",
              "swa_q_norm", "swa_k_norm", "swa_sinks")
WEIGHT_ORDER = ("norm_g", "w_in", "mla_q_a_norm", "mla_w_qb", "mla_kv_a_norm", "mla_w_kvb",
                "mla_q_norm", "mla_k_norm", "conv_w", "swa_q_norm", "swa_k_norm", "swa_sinks", "w_out")
SHARD_AXIS = {"w_in": 2, "w_out": 1, "mla_w_qb": 2, "mla_w_kvb": 2, "conv_w": 2}


def kernel(x, norm_g, w_in, mla_q_a_norm, mla_w_qb, mla_kv_a_norm, mla_w_kvb, mla_q_norm, mla_k_norm, conv_w, swa_q_norm, swa_k_norm, swa_sinks, w_out, loss_target, m_norm_g, m_w_in, m_mla_q_a_norm, m_mla_w_qb, m_mla_kv_a_norm, m_mla_w_kvb, m_mla_q_norm, m_mla_k_norm, m_conv_w, m_swa_q_norm, m_swa_k_norm, m_swa_sinks, m_w_out, v_norm_g, v_w_in, v_mla_q_a_norm, v_mla_w_qb, v_mla_kv_a_norm, v_mla_w_kvb, v_mla_q_norm, v_mla_k_norm, v_conv_w, v_swa_q_norm, v_swa_k_norm, v_swa_sinks, v_w_out):
    weights = dict(norm_g=norm_g, w_in=w_in, mla_q_a_norm=mla_q_a_norm, mla_w_qb=mla_w_qb,
                   mla_kv_a_norm=mla_kv_a_norm, mla_w_kvb=mla_w_kvb, mla_q_norm=mla_q_norm,
                   mla_k_norm=mla_k_norm, conv_w=conv_w, swa_q_norm=swa_q_norm,
                   swa_k_norm=swa_k_norm, swa_sinks=swa_sinks, w_out=w_out)
    mom_m = dict(norm_g=m_norm_g, w_in=m_w_in, mla_q_a_norm=m_mla_q_a_norm, mla_w_qb=m_mla_w_qb,
                 mla_kv_a_norm=m_mla_kv_a_norm, mla_w_kvb=m_mla_w_kvb, mla_q_norm=m_mla_q_norm,
                 mla_k_norm=m_mla_k_norm, conv_w=m_conv_w, swa_q_norm=m_swa_q_norm,
                 swa_k_norm=m_swa_k_norm, swa_sinks=m_swa_sinks, w_out=m_w_out)
    mom_v = dict(norm_g=v_norm_g, w_in=v_w_in, mla_q_a_norm=v_mla_q_a_norm, mla_w_qb=v_mla_w_qb,
                 mla_kv_a_norm=v_mla_kv_a_norm, mla_w_kvb=v_mla_w_kvb, mla_q_norm=v_mla_q_norm,
                 mla_k_norm=v_mla_k_norm, conv_w=v_conv_w, swa_q_norm=v_swa_q_norm,
                 swa_k_norm=v_swa_k_norm, swa_sinks=v_swa_sinks, w_out=v_w_out)
    xs = x[0]
    target = loss_target[0]
    s = xs.shape[0]
    c_idx = lax.axis_index("c").astype(jnp.int32).reshape(1)
    k_idx = (2 * lax.axis_index("x") + lax.axis_index("y")).astype(jnp.int32).reshape(1)

    conv_bits = lax.bitcast_convert_type(conv_w, BF16)
    small_list = [w_out, mla_w_qb, mla_w_kvb, conv_bits]
    w_in_rows = DEPTH * D_MODEL
    own = [w_in.astype(BF16).reshape(w_in_rows, w_in.shape[2]), _pack_rows(small_list, BF16, PACK_ROWS)]
    gathered_in, gathered_rest = [_fill_own_slab(buf, src, k_idx)
                                  for buf, src in zip(_all_gather(own), own)]
    parts = _unpack_rows(gathered_rest, [a.shape for a in small_list])
    join = lambda p, axis: jnp.concatenate([p[k] for k in range(N_CHIPS)], axis=axis)
    w_in_slabs = gathered_in.reshape(N_CHIPS, DEPTH, D_MODEL, w_in.shape[2])
    w_in_zeros = jnp.zeros((D_MODEL, 64), BF16)
    w_out_full = join(parts[0], 1)
    w_qb_full = join(parts[1], 2)
    w_kvb_full = join(parts[2], 2)
    conv_full = lax.bitcast_convert_type(join(parts[3], 2), F32)

    rope = _rope_tables(s)
    swa_tables = _swa_tables()
    layers = []
    for l in range(DEPTH):
        wq = jnp.pad(w_qb_full[l].reshape(MLA_Q_LORA, MLA_HEADS, MLA_QK),
                     ((0, 0), (0, 0), (0, LANES - MLA_QK))).reshape(MLA_Q_LORA, MLA_HEADS * LANES)
        kv = w_kvb_full[l].reshape(MLA_KV_LORA, MLA_HEADS, MLA_NOPE + MLA_V)
        wk = jnp.pad(kv[:, :, :MLA_NOPE], ((0, 0), (0, 0), (0, LANES - MLA_NOPE)))
        wkv = jnp.concatenate([wk.reshape(MLA_KV_LORA, MLA_HEADS * LANES),
                               kv[:, :, MLA_NOPE:].reshape(MLA_KV_LORA, MLA_HEADS * MLA_V)], axis=1)
        layers.append(dict(
            w_in=_cols([w_in_slabs[k, l] for k in range(N_CHIPS)] + [w_in_zeros], NEW_ORDER),
            w_out=w_out_full[l], wq=wq, wkv=wkv,
            cw=jnp.pad(conv_full[l], ((0, HALO - 3), (0, 0))),
            g=norm_g[l][None],
            mla_norms=(mla_q_a_norm[l][:, None], mla_kv_a_norm[l][:, None],
                       _pad_lanes(mla_q_norm[l][None], LANES).T, _pad_lanes(mla_k_norm[l][None], LANES).T),
            mla_weights=(wq, wkv, wq.T, wkv.T),
            sqn=swa_q_norm[l][:, None], skn=swa_k_norm[l][:, None], sinks=swa_sinks[l][None]))

    saved = []
    h_in = xs
    for l in range(DEPTH):
        p = layers[l]
        proj, hb = _in_proj_fwd(h_in, p["g"], p["w_in"])
        q, k, v, qt, kt, vt = _mla_prep_fwd(proj, p["mla_norms"], p["mla_weights"], rope)
        o_mla, lse = _mla_attn_fwd(q, k, vt)
        o_swa = _swa_fwd(proj, p["sqn"], p["skn"], p["sinks"], swa_tables)
        y, z = _out_fwd(proj, o_mla, o_swa, h_in, p["w_out"], p["cw"])
        saved.append(dict(x=h_in, proj=proj, hb=hb, q=q, k=k, v=v, qt=qt, kt=kt, o_mla=o_mla, lse=lse,
                          o_swa=o_swa, z=z))
        h_in = y

    dy, loss_acc = _loss_head(h_in, target)
    loss = lax.psum(loss_acc[0, 0], ("x", "y", "c"))

    grads = {n: [None] * DEPTH for n in WEIGHT_ORDER}
    for l in reversed(range(DEPTH)):
        p, a = layers[l], saved[l]
        dconv, dgates, do_mla, do_mla_t, delta, do_swa, dcw = _out_bwd(dy, a["proj"], a["o_mla"], a["o_swa"],
                                                             p["w_out"], p["cw"])
        grads["w_out"][l] = _weight_grads(a["z"], [dy], "dw_out")[0]
        grads["conv_w"][l] = dcw[0:3]
        delta_rows = jnp.transpose(delta, (1, 0)).reshape(MLA_HEADS // 2, 2, s)
        dq, dk, dv = _mla_attn_bwd(a["q"], a["qt"], a["k"], a["kt"], a["v"], do_mla, do_mla_t,
                                   a["lse"], delta_rows)
        dmla, dqan, dkvan, dqn, dkn, dwq_t, dwkv_t = _mla_prep_bwd(
            a["proj"], p["mla_norms"], p["mla_weights"], rope, dq, dk, dv)
        dwq, dwkv = dwq_t.T, dwkv_t.T
        dsq, dskv, dsqn, dskn, dsinks = _swa_bwd(a["proj"], p["sqn"], p["skn"], p["sinks"], swa_tables, do_swa)
        pieces = [dconv, dgates, dsq, dmla, dskv]
        dx, dg = _in_proj_bwd(pieces, a["x"], p["g"], p["w_in"], dy)
        grads["w_in"][l] = _weight_grads(a["hb"], pieces, "dw_in")
        grads["norm_g"][l] = dg[0]
        grads["mla_q_a_norm"][l] = dqan[:, 0]
        grads["mla_kv_a_norm"][l] = dkvan[:, 0]
        grads["mla_q_norm"][l] = dqn[:MLA_QK, 0]
        grads["mla_k_norm"][l] = dkn[:MLA_QK, 0]
        grads["mla_w_qb"][l] = dwq.reshape(MLA_Q_LORA, MLA_HEADS, LANES)[:, :, :MLA_QK].reshape(
            MLA_Q_LORA, MLA_HEADS * MLA_QK)
        dwk = dwkv[:, :MLA_HEADS * LANES].reshape(MLA_KV_LORA, MLA_HEADS, LANES)[:, :, :MLA_NOPE]
        dwv = dwkv[:, MLA_HEADS * LANES:].reshape(MLA_KV_LORA, MLA_HEADS, MLA_V)
        grads["mla_w_kvb"][l] = jnp.concatenate([dwk, dwv], axis=2).reshape(
            MLA_KV_LORA, MLA_HEADS * (MLA_NOPE + MLA_V))
        grads["swa_q_norm"][l] = dsqn[:, 0]
        grads["swa_k_norm"][l] = dskn[:, 0]
        grads["swa_sinks"][l] = dsinks[0]
        dy = dx
    grad_x = dy[None]
    full_grads = {n: jnp.stack(grads[n]) for n in WEIGHT_ORDER if n != "w_in"}

    rest = tuple(n for n in SHARDED if n != "w_in")
    rep_shapes = [weights[n].shape for n in REPLICATED]
    flat_rep = lambda d: jnp.concatenate([d[n].reshape(-1) for n in REPLICATED])

    def chunk(g, n, k):
        width = g.shape[SHARD_AXIS[n]] // N_CHIPS
        return lax.slice_in_dim(g, k * width, (k + 1) * width, axis=SHARD_AXIS[n])

    shard_cols = w_in.shape[2]
    g_in = jnp.stack([
        jnp.concatenate([_cols(grads["w_in"][l], _sub_ranges(OLD_ORDER, k * shard_cols,
                                                             (k + 1) * shard_cols))
                         for l in range(DEPTH)], axis=0)
        for k in range(N_CHIPS)])
    rep_grads = flat_rep(full_grads)
    g_rest = jnp.stack([_pack_rows([chunk(full_grads[n], n, k) for n in rest] + [rep_grads],
                                   F32, PACK_ROWS) for k in range(N_CHIPS)])
    from_sibling = _swap_halves_to_sibling([g_in, g_rest])
    partial = [_add_sibling(g, r, c_idx, dt)
               for g, r, dt in zip((g_in, g_rest), from_sibling, (BF16, F32))]
    by_chip = [_fill_own_slab(q, p, k_idx) for q, p in zip(_scatter_to_chips(partial), partial)]
    g_in_mine, g_rest_mine = _join_halves([_sum_chips(q, c_idx) for q in by_chip])

    pack_rest = lambda d: _pack_rows([d[n] for n in rest] + [flat_rep(d)], F32, PACK_ROWS)
    in_shape = w_in.shape
    res_in = _adamw(g_in_mine, w_in.reshape(w_in_rows, -1), m_w_in.reshape(w_in_rows, -1),
                    v_w_in.reshape(w_in_rows, -1))
    res_rest = _adamw(g_rest_mine, pack_rest(weights), pack_rest(mom_m), pack_rest(mom_v))
    rest_shapes = [weights[n].shape for n in rest] + [(rep_grads.shape[0],)]
    unpacked = []
    for buf_in, buf_rest in zip((g_in_mine,) + tuple(res_in), (g_rest_mine,) + tuple(res_rest)):
        vals = _unpack_rows(buf_rest, rest_shapes)
        group = dict(zip(rest, vals[:-1]))
        group["w_in"] = buf_in.reshape(in_shape)
        off = 0
        for n, sh in zip(REPLICATED, rep_shapes):
            size = sh[0] * sh[1]
            group[n] = vals[-1][off:off + size].reshape(sh)
            off += size
        unpacked.append(group)
    outs = [loss, grad_x]
    for group in unpacked:
        outs += [group[n] for n in WEIGHT_ORDER]
    return tuple(outs)
```

```python
import jax
import numpy as np
import jax.numpy as jnp
from jax import lax
from jax.experimental import pallas as pl
from jax.experimental.pallas import tpu as pltpu

F32 = jnp.float32
BF16 = jnp.bfloat16

D_MODEL = 1024
DEPTH = 2
GROUP = 512
D_MIX = 3 * GROUP
BLOCK = 128
RMS_EPS = 1e-6
NEG_INF = -1e30
MLA_HEADS = 8
MLA_QK = 96
MLA_NOPE = 64
MLA_ROPE = 32
MLA_V = 64
V_AUG = 80
MLA_Q_LORA = 256
MLA_KV_LORA = 128
ROPE_THETA = 10000.0
SWA_HEADS = 8
SWA_KV = 2
SWA_GROUP = 4
SWA_DIM = 64
N_CHIPS = 4

NC = 4352
OFF_SQ, OFF_MLA, OFF_SKV = 3072, 3584, 4096

VMEM_LIMIT = 56 * 1024 * 1024
LANES = 128
PACK_COLS = 1024
PACK_ROW_ALIGN = 16
PACK_ROWS = 1024

ADAM_LR = 0.001
ADAM_B1 = 0.9
ADAM_B2 = 0.999
ADAM_EPS = 1e-08
ADAM_WD = 0.01
ADAM_STEP = 10

MESH = pl.DeviceIdType.MESH


def _params(sem, vmem=VMEM_LIMIT):
    return pltpu.CompilerParams(dimension_semantics=sem, vmem_limit_bytes=vmem)


def _dot(a, b, dims):
    return lax.dot_general(a.astype(BF16), b.astype(BF16), (dims, ((), ())),
                           preferred_element_type=F32)


def _mm(a, b):
    return _dot(a, b, ((1,), (0,)))


def _mm_nt(a, b):
    return _dot(a, b, ((1,), (1,)))


def _rms(x, g, n=None):
    n = x.shape[-1] if n is None else n
    ms = jnp.sum(x * x, axis=-1, keepdims=True) * (1.0 / n)
    return x * lax.rsqrt(ms + RMS_EPS) * g


def _sigmoid(x):
    return 1.0 / (1.0 + jnp.exp(-x))


def _in_proj_fwd(x, g, w):
    s = x.shape[0]
    tm = min(512, s)

    def body(x_ref, g_ref, w_ref, proj_ref, hbt_ref):
        h = _rms(x_ref[...], g_ref[...])
        hbt_ref[...] = jnp.transpose(h).astype(BF16)
        proj_ref[...] = jnp.dot(h.astype(BF16), w_ref[...], preferred_element_type=F32)

    return pl.pallas_call(
        body, name="in_proj_fwd", grid=(s // tm,),
        in_specs=[pl.BlockSpec((tm, D_MODEL), lambda i: (i, 0)),
                  pl.BlockSpec((1, D_MODEL), lambda i: (0, 0)),
                  pl.BlockSpec((D_MODEL, NC), lambda i: (0, 0))],
        out_specs=[pl.BlockSpec((tm, NC), lambda i: (i, 0)),
                   pl.BlockSpec((D_MODEL, tm), lambda i: (0, i))],
        out_shape=[jax.ShapeDtypeStruct((s, NC), F32), jax.ShapeDtypeStruct((D_MODEL, s), BF16)],
        compiler_params=_params(("parallel",)),
    )(x, g, w)


def _in_proj_bwd(pieces, x, g, w, dres):
    s = x.shape[0]
    tm = min(512, s)
    n_p = len(pieces)

    def body(*refs):
        p_refs = refs[:n_p]
        x_ref, g_ref, w_ref, dres_ref, dx_ref, dg_ref = refs[n_p:]
        dh = None
        off = 0
        for r in p_refs:
            width = r.shape[1]
            t = _mm_nt(r[...], w_ref[:, off:off + width])
            dh = t if dh is None else dh + t
            off += width
        _, vjp = jax.vjp(_rms, x_ref[...], g_ref[...])
        dx, dg = vjp(dh)
        dx_ref[...] = dx + dres_ref[...]

        @pl.when(pl.program_id(0) == 0)
        def _():
            dg_ref[...] = jnp.zeros_like(dg_ref)

        dg_ref[...] += dg

    in_specs = [pl.BlockSpec((tm, p.shape[1]), lambda i: (i, 0)) for p in pieces]
    in_specs += [pl.BlockSpec((tm, D_MODEL), lambda i: (i, 0)),
                 pl.BlockSpec((1, D_MODEL), lambda i: (0, 0)),
                 pl.BlockSpec((D_MODEL, NC), lambda i: (0, 0)),
                 pl.BlockSpec((tm, D_MODEL), lambda i: (i, 0))]
    return pl.pallas_call(
        body, name="in_proj_bwd", grid=(s // tm,),
        in_specs=in_specs,
        out_specs=[pl.BlockSpec((tm, D_MODEL), lambda i: (i, 0)),
                   pl.BlockSpec((1, D_MODEL), lambda i: (0, 0))],
        out_shape=[jax.ShapeDtypeStruct((s, D_MODEL), F32), jax.ShapeDtypeStruct((1, D_MODEL), F32)],
        compiler_params=_params(("arbitrary",)),
    )(*pieces, x, g, w, dres)


def _weight_grads(at, bs, name):
    m, s = at.shape
    nb = len(bs)
    tk = min(512, s)

    def body(a_ref, *refs):
        b_refs, o_refs = refs[:nb], refs[nb:]

        @pl.when(pl.program_id(0) == 0)
        def _():
            for o_ref in o_refs:
                o_ref[...] = jnp.zeros_like(o_ref)

        a = a_ref[...]
        for b_ref, o_ref in zip(b_refs, o_refs):
            o_ref[...] += _mm(a, b_ref[...])

    return pl.pallas_call(
        body, name=name, grid=(s // tk,),
        in_specs=[pl.BlockSpec((m, tk), lambda k: (0, k))]
        + [pl.BlockSpec((tk, b.shape[1]), lambda k: (k, 0)) for b in bs],
        out_specs=[pl.BlockSpec((m, b.shape[1]), lambda k: (0, 0)) for b in bs],
        out_shape=[jax.ShapeDtypeStruct((m, b.shape[1]), F32) for b in bs],
        compiler_params=_params(("arbitrary",)),
    )(at, *bs)


def _rms0(x, g, n=None):
    n = x.shape[0] if n is None else n
    ms = jnp.sum(x * x, axis=0, keepdims=True) * (1.0 / n)
    return x * lax.rsqrt(ms + RMS_EPS) * g


@jax.custom_vjp
def _rope0(t, c, s1, s2):
    return t * c + pltpu.roll(t, LANES - 16, 0) * s1 + pltpu.roll(t, 16, 0) * s2


def _rope0_fwd(t, c, s1, s2):
    return _rope0(t, c, s1, s2), (c, s1, s2)


def _rope0_bwd(res, g):
    c, s1, s2 = res
    dt = g * c + pltpu.roll(g * s1, 16, 0) + pltpu.roll(g * s2, LANES - 16, 0)
    return dt, jnp.zeros_like(c), jnp.zeros_like(s1), jnp.zeros_like(s2)


_rope0.defvjp(_rope0_fwd, _rope0_bwd)


@jax.custom_vjp
def _mmw(w, wt, x):
    return _mm(w, x)


def _mmw_fwd(w, wt, x):
    return _mm(w, x), (wt, x)


def _mmw_bwd(res, g):
    wt, x = res
    return _mm_nt(g, x), jnp.zeros_like(wt), _mm(wt, g)


_mmw.defvjp(_mmw_fwd, _mmw_bwd)


def _prep_fn(q_lat, kv_lat, kr, qan, kvan, qn, kn, wq, wk, wv, wqt, wkt, wvt, c, s1, s2, mm):
    tokens = q_lat.shape[1]
    rq = _rms0(q_lat, qan)
    rkv = _rms0(kv_lat, kvan)
    qn_b = jnp.broadcast_to(qn, (LANES, tokens))
    kn_b = jnp.broadcast_to(kn, (LANES, tokens))
    qs, ks = [], []
    for h in range(MLA_HEADS):
        qs.append(_rope0(_rms0(mm(wq[h], wqt[h], rq), qn_b, MLA_QK), c, s1, s2))
        ks.append(_rope0(_rms0(mm(wk[h], wkt[h], rkv) + kr, kn_b, MLA_QK), c, s1, s2))
    return tuple(qs), tuple(ks), mm(wv, wvt, rkv)


def _prep_weights(wq_ref, wkv_ref, wqt_ref, wkvt_ref):
    heads = range(MLA_HEADS)
    wq = tuple(wqt_ref[LANES * h:LANES * (h + 1), :].astype(F32) for h in heads)
    wk = tuple(wkvt_ref[LANES * h:LANES * (h + 1), :].astype(F32) for h in heads)
    wv = wkvt_ref[LANES * MLA_HEADS:, :].astype(F32)
    wqt = tuple(wq_ref[:, LANES * h:LANES * (h + 1)].astype(F32) for h in heads)
    wkt = tuple(wkv_ref[:, LANES * h:LANES * (h + 1)].astype(F32) for h in heads)
    wvt = wkv_ref[:, LANES * MLA_HEADS:].astype(F32)
    return wq, wk, wv, wqt, wkt, wvt


def _prep_in_specs(tm):
    const = lambda shape: pl.BlockSpec(shape, lambda i: (0, 0))
    col = lambda height: pl.BlockSpec((height, tm), lambda i: (0, i))
    return [pl.BlockSpec((tm, 512), lambda i: (i, OFF_MLA // 512)),
            const((MLA_Q_LORA, 1)), const((MLA_KV_LORA, 1)), const((LANES, 1)), const((LANES, 1)),
            const((MLA_Q_LORA, 1024)), const((MLA_KV_LORA, 1536)),
            const((1024, MLA_Q_LORA)), const((1536, MLA_KV_LORA)),
            col(LANES), col(LANES), col(LANES)]


def _prep_operands(blk_ref, refs):
    qan_ref, kvan_ref, qn_ref, kn_ref, wq_ref, wkv_ref, wqt_ref, wkvt_ref, c_ref, s1_ref, s2_ref = refs
    blk_t = jnp.transpose(blk_ref[...])
    diff = (blk_t[0:256], blk_t[256:384], blk_t[384:512],
            qan_ref[...], kvan_ref[...], qn_ref[...], kn_ref[...])
    weights = _prep_weights(wq_ref, wkv_ref, wqt_ref, wkvt_ref)
    return diff, weights, (c_ref[...], s1_ref[...], s2_ref[...])


def _mla_prep_fwd(proj, norms, weights, rope):
    s = proj.shape[0]
    tm = min(512, s)

    def body(blk_ref, *refs):
        ins, (q_ref, k_ref, v_ref, qt_ref, kt_ref, vt_ref) = refs[:11], refs[11:]
        diff, (wq, wk, wv, wqt, wkt, wvt), tables = _prep_operands(blk_ref, ins)
        qs, ks, v = _prep_fn(*diff, wq, wk, wv, wqt, wkt, wvt, *tables,
                             lambda w, wt, x: _mm(w, x))
        for h in range(MLA_HEADS):
            q2 = qs[h] * Q_PRESCALE
            qt_ref[LANES * h:LANES * (h + 1), :] = q2.astype(BF16)
            kt_ref[LANES * h:LANES * (h + 1), :] = ks[h].astype(BF16)
            q_ref[:, LANES * h:LANES * (h + 1)] = jnp.transpose(q2).astype(BF16)
            k_ref[:, LANES * h:LANES * (h + 1)] = jnp.transpose(ks[h]).astype(BF16)
        ones_row = (lax.broadcasted_iota(jnp.int32, (V_AUG - MLA_V, v.shape[1]), 0) == 0).astype(BF16)
        for h in range(MLA_HEADS):
            vt_ref[V_AUG * h:V_AUG * h + MLA_V, :] = v[MLA_V * h:MLA_V * (h + 1)].astype(BF16)
            vt_ref[V_AUG * h + MLA_V:V_AUG * (h + 1), :] = ones_row
        v_ref[...] = jnp.transpose(v).astype(BF16)

    row = lambda width: pl.BlockSpec((tm, width), lambda i: (i, 0))
    col = lambda height: pl.BlockSpec((height, tm), lambda i: (0, i))
    return pl.pallas_call(
        body, name="mla_prep_fwd", grid=(s // tm,),
        in_specs=_prep_in_specs(tm),
        out_specs=[row(1024), row(1024), row(512), col(1024), col(1024), col(MLA_HEADS * V_AUG)],
        out_shape=[jax.ShapeDtypeStruct((s, 1024), BF16), jax.ShapeDtypeStruct((s, 1024), BF16),
                   jax.ShapeDtypeStruct((s, 512), BF16), jax.ShapeDtypeStruct((1024, s), BF16),
                   jax.ShapeDtypeStruct((1024, s), BF16),
                   jax.ShapeDtypeStruct((MLA_HEADS * V_AUG, s), BF16)],
        compiler_params=_params(("parallel",)),
    )(proj, *norms, *weights, *rope)


def _mla_prep_bwd(proj, norms, weights, rope, dq, dk, dv):
    s = proj.shape[0]
    tm = min(512, s)

    def body(blk_ref, *refs):
        ins, (dq_ref, dk_ref, dv_ref) = refs[:11], refs[11:14]
        dblk_ref, dqan_ref, dkvan_ref, dqn_ref, dkn_ref, dwq_ref, dwkv_ref = refs[14:]
        diff, (wq, wk, wv, wqt, wkt, wvt), tables = _prep_operands(blk_ref, ins)

        def fn(q_lat, kv_lat, kr, qan, kvan, qn, kn, wq_, wk_, wv_):
            return _prep_fn(q_lat, kv_lat, kr, qan, kvan, qn, kn, wq_, wk_, wv_, wqt, wkt, wvt,
                            *tables, _mmw)

        _, vjp = jax.vjp(fn, *diff, wq, wk, wv)
        heads = range(MLA_HEADS)
        cts = (tuple(dq_ref[LANES * h:LANES * (h + 1), :] for h in heads),
               tuple(dk_ref[LANES * h:LANES * (h + 1), :] for h in heads), dv_ref[...])
        dq_lat, dkv_lat, dkr, dqan, dkvan, dqn, dkn, dwq_h, dwk_h, dwv = vjp(cts)
        dblk_ref[...] = jnp.transpose(
            jnp.concatenate([dq_lat, dkv_lat, dkr], axis=0)).astype(BF16)

        @pl.when(pl.program_id(0) == 0)
        def _():
            for r in (dqan_ref, dkvan_ref, dqn_ref, dkn_ref, dwq_ref, dwkv_ref):
                r[...] = jnp.zeros_like(r)

        dqan_ref[...] += dqan
        dkvan_ref[...] += dkvan
        dqn_ref[...] += dqn
        dkn_ref[...] += dkn
        for h in heads:
            dwq_ref[LANES * h:LANES * (h + 1), :] += dwq_h[h]
            dwkv_ref[LANES * h:LANES * (h + 1), :] += dwk_h[h]
        dwkv_ref[LANES * MLA_HEADS:, :] += dwv

    const = lambda shape: pl.BlockSpec(shape, lambda i: (0, 0))
    col = lambda height: pl.BlockSpec((height, tm), lambda i: (0, i))
    shapes = [(MLA_Q_LORA, 1), (MLA_KV_LORA, 1), (LANES, 1), (LANES, 1),
              (1024, MLA_Q_LORA), (1536, MLA_KV_LORA)]
    return pl.pallas_call(
        body, name="mla_prep_bwd", grid=(s // tm,),
        in_specs=_prep_in_specs(tm) + [col(1024), col(1024), col(512)],
        out_specs=[pl.BlockSpec((tm, 512), lambda i: (i, 0))] + [const(sh) for sh in shapes],
        out_shape=[jax.ShapeDtypeStruct((s, 512), BF16)]
        + [jax.ShapeDtypeStruct(sh, F32) for sh in shapes],
        compiler_params=_params(("arbitrary",)),
    )(proj, *norms, *weights, *rope, dq, dk, dv)


MLA_SCALE = MLA_QK ** -0.5


LOG2E = 1.4426950408889634
LN2 = 0.6931471805599453
Q_PRESCALE = MLA_SCALE * LOG2E
HEAD_GROUPS = ((0, 1),)


def _mla_attn_fwd(q2, k, vt):
    s = q2.shape[0]
    t = min(512, s)
    tk = min(128, s)
    nq = s // t
    r = t // tk

    def body(q_ref, k_ref, vt_ref, o_ref, lse_ref, acc_ref):
        i = pl.program_id(1)
        row = lax.broadcasted_iota(jnp.int32, (tk, t), 0)
        col = lax.broadcasted_iota(jnp.int32, (tk, t), 1)
        qh = [q_ref[:, LANES * hh:LANES * (hh + 1)] for hh in range(2)]
        acc_ref[...] = jnp.zeros_like(acc_ref)

        def scores(j, heads, diag=None):
            r0 = pl.multiple_of(j * tk, tk)
            out = []
            for hh in heads:
                kc = k_ref[pl.ds(r0, tk), LANES * hh:LANES * (hh + 1)]
                sc = lax.dot_general(kc, qh[hh], (((1,), (1,)), ((), ())),
                                     preferred_element_type=F32)
                out.append(sc if diag is None else jnp.where(row + diag * tk <= col, sc, NEG_INF))
            return tuple(out)

        for heads in HEAD_GROUPS:
            stats = tuple(jnp.full((1, t), NEG_INF, F32) for _ in heads)

            def consume(j, scs, stats, heads=heads):
                r0 = pl.multiple_of(j * tk, tk)
                out, ps, alphas = [], [], []
                for n, hh in enumerate(heads):
                    m_new = jnp.maximum(stats[n], jnp.max(scs[n], axis=0, keepdims=True))
                    ps.append(jnp.exp2(scs[n] - m_new).astype(BF16))
                    alphas.append(jnp.exp2(stats[n] - m_new))
                    out.append(m_new)
                for n, hh in enumerate(heads):
                    vc = vt_ref[V_AUG * hh:V_AUG * (hh + 1), pl.ds(r0, tk)]
                    acc_ref[hh] = alphas[n] * acc_ref[hh] + jnp.dot(vc, ps[n],
                                                                    preferred_element_type=F32)
                return tuple(out)

            def group(j0, stats, diag, heads=heads):
                scs = [scores(j0 + d, heads, d if diag else None) for d in range(r)]
                for d in range(r):
                    stats = consume(j0 + d, scs[d], stats)
                return stats

            stats = group(r * i, stats, True)
            stats = lax.fori_loop(0, i, lambda j, st: group(r * j, st, False), stats)
            for n, hh in enumerate(heads):
                l = acc_ref[hh, MLA_V:MLA_V + 1, :]
                o_ref[:, MLA_V * hh:MLA_V * (hh + 1)] = jnp.transpose(acc_ref[hh, 0:MLA_V, :] / l)
                lse_ref[0, hh:hh + 1, :] = stats[n] + jnp.log2(l)

    return pl.pallas_call(
        body, name="mla_attn_fwd", grid=(MLA_HEADS // 2, nq),
        in_specs=[pl.BlockSpec((t, 256), lambda p, i: (i, p)),
                  pl.BlockSpec((s, 256), lambda p, i: (0, p)),
                  pl.BlockSpec((2 * V_AUG, s), lambda p, i: (p, 0))],
        out_specs=[pl.BlockSpec((t, 128), lambda p, i: (i, p)),
                   pl.BlockSpec((1, 2, t), lambda p, i: (p, 0, i))],
        out_shape=[jax.ShapeDtypeStruct((s, 512), F32),
                   jax.ShapeDtypeStruct((MLA_HEADS // 2, 2, s), F32)],
        scratch_shapes=[pltpu.VMEM((2, V_AUG, t), F32)],
        compiler_params=_params(("parallel", "arbitrary")),
    )(q2, k, vt)


def _mla_attn_bwd(q2, q2t, k, kt, v, do, dot, lse_rows, delta_rows):
    s = q2.shape[0]
    t = min(512, s)
    nq = s // t

    def body(q_ref, qt_ref, k_ref, kt_ref, v_ref, do_ref, dot_ref, lse_ref, dl_ref,
             dq_ref, dk_ref, dv_ref):
        j = pl.program_id(1)

        @pl.when(j == 0)
        def _():
            dq_ref[...] = jnp.zeros_like(dq_ref)

        dk_ref[...] = jnp.zeros_like(dk_ref)
        dv_ref[...] = jnp.zeros_like(dv_ref)
        row = lax.broadcasted_iota(jnp.int32, (t, t), 0)
        col = lax.broadcasted_iota(jnp.int32, (t, t), 1)
        causal_t = row <= col
        kh = [k_ref[:, LANES * hh:LANES * (hh + 1)] for hh in range(2)]
        kth = [kt_ref[LANES * hh:LANES * (hh + 1), :] for hh in range(2)]
        vh = [v_ref[:, MLA_V * hh:MLA_V * (hh + 1)] for hh in range(2)]
        nt = (((1,), (1,)), ((), ()))

        def step(i, masked):
            r0 = pl.multiple_of(i * t, t)
            sd = []
            for hh in range(2):
                qh = q_ref[pl.ds(r0, t), LANES * hh:LANES * (hh + 1)]
                doh = do_ref[pl.ds(r0, t), MLA_V * hh:MLA_V * (hh + 1)]
                sc_t = lax.dot_general(kh[hh], qh, nt, preferred_element_type=F32)
                sd.append(jnp.where(causal_t, sc_t, NEG_INF) if masked else sc_t)
                sd.append(lax.dot_general(vh[hh], doh, nt, preferred_element_type=F32))
            for hh in range(2):
                lse = lse_ref[0, hh:hh + 1, pl.ds(r0, t)]
                dl = dl_ref[0, hh:hh + 1, pl.ds(r0, t)]
                p_t = jnp.exp2(sd[2 * hh] - lse)
                g_t = (p_t * (sd[2 * hh + 1] - dl)).astype(BF16)
                qth = qt_ref[LANES * hh:LANES * (hh + 1), pl.ds(r0, t)]
                doth = dot_ref[MLA_V * hh:MLA_V * (hh + 1), pl.ds(r0, t)]
                dv_ref[MLA_V * hh:MLA_V * (hh + 1), :] += lax.dot_general(
                    doth, p_t.astype(BF16), nt, preferred_element_type=F32)
                dk_ref[LANES * hh:LANES * (hh + 1), :] += lax.dot_general(
                    qth, g_t, nt, preferred_element_type=F32)
                dq_ref[LANES * hh:LANES * (hh + 1), pl.ds(r0, t)] += jnp.dot(
                    kth[hh], g_t, preferred_element_type=F32)

        step(j, True)

        def trip(i, carry):
            step(i, False)
            return carry

        lax.fori_loop(j + 1, nq, trip, 0)
        dk_ref[...] = dk_ref[...] * LN2

        @pl.when(j == nq - 1)
        def _():
            dq_ref[...] = dq_ref[...] * MLA_SCALE

    return pl.pallas_call(
        body, name="mla_attn_bwd", grid=(MLA_HEADS // 2, nq),
        in_specs=[pl.BlockSpec((s, 256), lambda p, j: (0, p)),
                  pl.BlockSpec((256, s), lambda p, j: (p, 0)),
                  pl.BlockSpec((t, 256), lambda p, j: (j, p)),
                  pl.BlockSpec((256, t), lambda p, j: (p, j)),
                  pl.BlockSpec((t, 128), lambda p, j: (j, p)),
                  pl.BlockSpec((s, 128), lambda p, j: (0, p)),
                  pl.BlockSpec((128, s), lambda p, j: (p, 0)),
                  pl.BlockSpec((1, 2, s), lambda p, j: (p, 0, 0)),
                  pl.BlockSpec((1, 2, s), lambda p, j: (p, 0, 0))],
        out_specs=[pl.BlockSpec((256, s), lambda p, j: (p, 0)),
                   pl.BlockSpec((256, t), lambda p, j: (p, j)),
                   pl.BlockSpec((128, t), lambda p, j: (p, j))],
        out_shape=[jax.ShapeDtypeStruct((1024, s), F32), jax.ShapeDtypeStruct((1024, s), F32),
                   jax.ShapeDtypeStruct((512, s), F32)],
        compiler_params=_params(("parallel", "arbitrary")),
    )(q2, q2t, k, kt, v, do, dot, lse_rows, delta_rows)


SWA_SCALE = SWA_DIM ** -0.5
SWA_COLS = SWA_GROUP * BLOCK
SWA_LOG2 = SWA_SCALE * LOG2E


def _swa_tables():
    k = np.arange(2 * BLOCK)[:, None]
    col = np.arange(SWA_COLS)[None, :]
    dist = BLOCK + (col % BLOCK) - k
    valid = (dist >= 0) & (dist < BLOCK)
    out = np.zeros((2, SWA_KV, 2 * BLOCK, SWA_COLS), np.float32)
    for first in range(2):
        ok = valid & ((k >= BLOCK) | (first == 0))
        for j in range(SWA_KV):
            slope = 2.0 ** -(SWA_GROUP * j + col // BLOCK + 1)
            out[first, j] = np.where(ok, -slope * dist * LOG2E, NEG_INF)
    return jnp.asarray(out)


def _swa_tile_inputs(sq_ref, skv_ref, halo_ref, qn_ref, kn_ref, sk_ref, add_ref, first):
    tokens = sq_ref.shape[0]
    kv_all = jnp.concatenate([halo_ref[...], skv_ref[...]], axis=0)
    kv_t = jnp.transpose(kv_all)
    sq_t = jnp.transpose(sq_ref[...])
    k_raw = [kv_t[SWA_DIM * j:SWA_DIM * (j + 1)] for j in range(SWA_KV)]
    v_t = [kv_t[128 + SWA_DIM * j:128 + SWA_DIM * (j + 1)] for j in range(SWA_KV)]
    v_nat = [kv_all[:, 128 + SWA_DIM * j:128 + SWA_DIM * (j + 1)] for j in range(SWA_KV)]
    q_raw = [sq_t[SWA_DIM * h:SWA_DIM * (h + 1)] for h in range(SWA_HEADS)]
    qn_b = jnp.broadcast_to(qn_ref[...], (SWA_DIM, tokens))
    kn_b = jnp.broadcast_to(kn_ref[...], (SWA_DIM, tokens + BLOCK))
    lane_grp = lax.broadcasted_iota(jnp.int32, (1, SWA_COLS), 1) // BLOCK
    sinks, adds = [], []
    for j in range(SWA_KV):
        row = jnp.zeros((1, SWA_COLS), F32)
        for g in range(SWA_GROUP):
            h = SWA_GROUP * j + g
            row = jnp.where(lane_grp == g, sk_ref[:, h:h + 1] * LOG2E, row)
        sinks.append(row)
        adds.append((jnp.where(first, add_ref[1, j], add_ref[0, j]), add_ref[0, j]))
    return k_raw, v_t, v_nat, q_raw, qn_b, kn_b, sinks, adds


def _swa_probs(kb, qs_t, add, sink):
    s2 = jnp.dot(kb, qs_t, preferred_element_type=F32) * SWA_LOG2 + add
    m = jnp.maximum(jnp.max(s2, axis=0, keepdims=True), sink)
    e = jnp.exp2(s2 - m)
    es = jnp.exp2(sink - m)
    inv = 1.0 / (jnp.sum(e, axis=0, keepdims=True) + es)
    return e, inv, es


def _swa_queries(qn_t, j, b):
    return jnp.concatenate([qn_t[SWA_GROUP * j + g][:, BLOCK * b:BLOCK * (b + 1)]
                            for g in range(SWA_GROUP)], axis=1)


def _swa_fwd(proj, qn, kn, sinks, tables):
    s = proj.shape[0]
    ts = min(1024, s)
    nb = ts // BLOCK

    def body(sq_ref, skv_ref, halo_ref, qn_ref, kn_ref, sk_ref, add_ref, o_ref, ot_ref):
        first = pl.program_id(0) == 0
        k_raw, v_t, _, q_raw, qn_b, kn_b, sink_rows, adds = _swa_tile_inputs(
            sq_ref, skv_ref, halo_ref, qn_ref, kn_ref, sk_ref, add_ref, first)
        kn_nat = [jnp.transpose(_rms0(k, kn_b)).astype(BF16) for k in k_raw]
        v_t = [v.astype(BF16) for v in v_t]
        qn_t = [_rms0(q, qn_b).astype(BF16) for q in q_raw]
        for b in range(nb):
            band = slice(BLOCK * b, BLOCK * (b + 2))
            for j in range(SWA_KV):
                e, inv, _ = _swa_probs(kn_nat[j][band], _swa_queries(qn_t, j, b),
                                       adds[j][0 if b == 0 else 1], sink_rows[j])
                o_t = jnp.dot(v_t[j][:, band], (e * inv).astype(BF16),
                              preferred_element_type=F32)
                for g in range(SWA_GROUP):
                    h = SWA_GROUP * j + g
                    ot_ref[SWA_DIM * h:SWA_DIM * (h + 1), BLOCK * b:BLOCK * (b + 1)] = (
                        o_t[:, BLOCK * g:BLOCK * (g + 1)])
        o_ref[...] = jnp.transpose(ot_ref[...])

    const = lambda shape: pl.BlockSpec(shape, lambda i: (0,) * len(shape))
    return pl.pallas_call(
        body, name="swa_fwd", grid=(s // ts,),
        in_specs=[pl.BlockSpec((ts, 512), lambda i: (i, OFF_SQ // 512)),
                  pl.BlockSpec((ts, 256), lambda i: (i, OFF_SKV // 256)),
                  pl.BlockSpec((BLOCK, 256), lambda i: (jnp.maximum(i * nb - 1, 0), OFF_SKV // 256)),
                  const((SWA_DIM, 1)), const((SWA_DIM, 1)), const((1, SWA_HEADS)),
                  const(tables.shape)],
        out_specs=pl.BlockSpec((ts, 512), lambda i: (i, 0)),
        out_shape=jax.ShapeDtypeStruct((s, 512), F32),
        scratch_shapes=[pltpu.VMEM((512, ts), F32)],
        compiler_params=_params(("parallel",)),
    )(proj, proj, proj, qn, kn, sinks, tables)


def _swa_bwd(proj, qn, kn, sinks, tables, do):
    s = proj.shape[0]
    ts = min(1024, s)
    nb = ts // BLOCK
    nt = s // ts

    def body(sq_ref, skv_ref, halo_ref, qn_ref, kn_ref, sk_ref, add_ref, do_ref,
             dsq_ref, dskv_ref, dqn_ref, dkn_ref, dsk_ref, carry_ref, dqt_ref, dkvt_ref):
        step = pl.program_id(0)
        first = step == nt - 1

        @pl.when(step == 0)
        def _():
            carry_ref[...] = jnp.zeros_like(carry_ref)
            dqn_ref[...] = jnp.zeros_like(dqn_ref)
            dkn_ref[...] = jnp.zeros_like(dkn_ref)
            dsk_ref[...] = jnp.zeros_like(dsk_ref)

        k_raw, v_t, v_nat, q_raw, qn_b, kn_b, sink_rows, adds = _swa_tile_inputs(
            sq_ref, skv_ref, halo_ref, qn_ref, kn_ref, sk_ref, add_ref, first)
        kn_f = [_rms0(k, kn_b) for k in k_raw]
        kn_t = [k.astype(BF16) for k in kn_f]
        kn_nat = [jnp.transpose(k).astype(BF16) for k in kn_f]
        v_nat = [v.astype(BF16) for v in v_nat]
        qn_t = [_rms0(q, qn_b).astype(BF16) for q in q_raw]
        do_t = jnp.transpose(do_ref[...].astype(F32)).astype(BF16)

        dkvt_ref[...] = jnp.zeros_like(dkvt_ref)
        dsink = [jnp.zeros((1, SWA_COLS), F32) for _ in range(SWA_KV)]
        nt_dims = (((1,), (1,)), ((), ()))
        for b in range(nb):
            rows = slice(BLOCK * b, BLOCK * (b + 1))
            band = slice(BLOCK * b, BLOCK * (b + 2))
            for j in range(SWA_KV):
                heads = [SWA_GROUP * j + g for g in range(SWA_GROUP)]
                qs_t = _swa_queries(qn_t, j, b)
                dos_t = jnp.concatenate([do_t[SWA_DIM * h:SWA_DIM * (h + 1), rows] for h in heads],
                                        axis=1)
                e, inv, es = _swa_probs(kn_nat[j][band], qs_t, adds[j][0 if b == 0 else 1],
                                        sink_rows[j])
                p = e * inv
                dp = jnp.dot(v_nat[j][band], dos_t, preferred_element_type=F32)
                dsum = jnp.sum(p * dp, axis=0, keepdims=True)
                dsink[j] = dsink[j] - es * inv * dsum
                g_t = (p * (dp - dsum) * SWA_SCALE).astype(BF16)
                dv_t = lax.dot_general(dos_t, p.astype(BF16), nt_dims,
                                       preferred_element_type=F32)
                dk_t = lax.dot_general(qs_t, g_t, nt_dims, preferred_element_type=F32)
                dq_t = jnp.dot(kn_t[j][:, band], g_t, preferred_element_type=F32)
                dkvt_ref[SWA_DIM * j:SWA_DIM * (j + 1), band] += dk_t
                dkvt_ref[128 + SWA_DIM * j:128 + SWA_DIM * (j + 1), band] += dv_t
                for g, h in enumerate(heads):
                    dqt_ref[SWA_DIM * h:SWA_DIM * (h + 1), rows] = dq_t[:, BLOCK * g:BLOCK * (g + 1)]

        dqn = jnp.zeros((SWA_DIM, 1), F32)
        for h in range(SWA_HEADS):
            _, vjp = jax.vjp(_rms0, q_raw[h], qn_ref[...])
            dq, dg = vjp(dqt_ref[SWA_DIM * h:SWA_DIM * (h + 1), :])
            dqt_ref[SWA_DIM * h:SWA_DIM * (h + 1), :] = dq
            dqn = dqn + dg
        dqn_ref[...] += dqn
        dsq_ref[...] = jnp.transpose(dqt_ref[...]).astype(BF16)
        dkn = jnp.zeros((SWA_DIM, 1), F32)
        lane_grp = lax.broadcasted_iota(jnp.int32, (1, SWA_COLS), 1) // BLOCK
        for j in range(SWA_KV):
            _, vjp = jax.vjp(_rms0, k_raw[j], kn_ref[...])
            dk, dg = vjp(dkvt_ref[SWA_DIM * j:SWA_DIM * (j + 1), :])
            dkvt_ref[SWA_DIM * j:SWA_DIM * (j + 1), :] = dk
            dkn = dkn + dg
            for g in range(SWA_GROUP):
                h = SWA_GROUP * j + g
                dsk_ref[:, h:h + 1] += jnp.sum(jnp.where(lane_grp == g, dsink[j], 0.0), axis=1,
                                               keepdims=True)
        dkn_ref[...] += dkn
        dkv = jnp.transpose(dkvt_ref[...])
        dskv_ref[0:ts - BLOCK, :] = dkv[BLOCK:ts].astype(BF16)
        dskv_ref[ts - BLOCK:ts, :] = (dkv[ts:ts + BLOCK] + carry_ref[...]).astype(BF16)
        carry_ref[...] = dkv[0:BLOCK]

    const = lambda shape: pl.BlockSpec(shape, lambda st: (0,) * len(shape))
    return pl.pallas_call(
        body, name="swa_bwd", grid=(nt,),
        in_specs=[pl.BlockSpec((ts, 512), lambda st: (nt - 1 - st, OFF_SQ // 512)),
                  pl.BlockSpec((ts, 256), lambda st: (nt - 1 - st, OFF_SKV // 256)),
                  pl.BlockSpec((BLOCK, 256),
                               lambda st: (jnp.maximum((nt - 1 - st) * nb - 1, 0), OFF_SKV // 256)),
                  const((SWA_DIM, 1)), const((SWA_DIM, 1)), const((1, SWA_HEADS)),
                  const(tables.shape),
                  pl.BlockSpec((ts, 512), lambda st: (nt - 1 - st, 0))],
        out_specs=[pl.BlockSpec((ts, 512), lambda st: (nt - 1 - st, 0)),
                   pl.BlockSpec((ts, 256), lambda st: (nt - 1 - st, 0)),
                   const((SWA_DIM, 1)), const((SWA_DIM, 1)), const((1, SWA_HEADS))],
        out_shape=[jax.ShapeDtypeStruct((s, 512), BF16), jax.ShapeDtypeStruct((s, 256), BF16),
                   jax.ShapeDtypeStruct((SWA_DIM, 1), F32), jax.ShapeDtypeStruct((SWA_DIM, 1), F32),
                   jax.ShapeDtypeStruct((1, SWA_HEADS), F32)],
        scratch_shapes=[pltpu.VMEM((BLOCK, 256), F32), pltpu.VMEM((512, ts), F32),
                        pltpu.VMEM((256, ts + BLOCK), F32)],
        compiler_params=_params(("arbitrary",)),
    )(proj, proj, proj, qn, kn, sinks, tables, do)


HALO = 8


def _shift_down(u, halo, k):
    tm = u.shape[0]
    rid = lax.broadcasted_iota(jnp.int32, u.shape, 0)
    out = pltpu.roll(u, k, 0)
    for r in range(k):
        out = jnp.where(rid == r, halo[HALO - k + r:HALO - k + r + 1, :], out)
    return out


def _shift_up(u, halo, k):
    tm = u.shape[0]
    rid = lax.broadcasted_iota(jnp.int32, u.shape, 0)
    out = pltpu.roll(u, tm - k, 0)
    for r in range(k):
        out = jnp.where(rid == tm - k + r, halo[r:r + 1, :], out)
    return out


def _conv_fwd_vals(conv_ref, convp_ref, cw_ref, is_first):
    c_h, c_b, c_c = conv_ref[:, 0:512], conv_ref[:, 512:1024], conv_ref[:, 1024:1536]
    u = c_c * c_h
    up = jnp.where(is_first, 0.0, convp_ref[:, 1024:1536] * convp_ref[:, 0:512])
    u1 = _shift_down(u, up, 1)
    u2 = _shift_down(u, up, 2)
    yc = cw_ref[0:1, :] * u2 + cw_ref[1:2, :] * u1 + cw_ref[2:3, :] * u
    return c_h, c_b, c_c, u, u1, u2, yc


def _out_fwd(proj, o_mla, o_swa, x, w_out, cw):
    s = proj.shape[0]
    tm = min(512, s)

    def body(conv_ref, convp_ref, gates_ref, om_ref, os_ref, x_ref, w_ref, cw_ref, y_ref, zt_ref,
             z_ref):
        i = pl.program_id(0)
        _, c_b, _, _, _, _, yc = _conv_fwd_vals(conv_ref, convp_ref, cw_ref, i == 0)
        mix = (om_ref[...], c_b * yc, os_ref[...])
        for n in range(3):
            g = gates_ref[:, GROUP * n:GROUP * (n + 1)]
            z = mix[n] * (g * _sigmoid(g))
            z_ref[:, GROUP * n:GROUP * (n + 1)] = z.astype(BF16)
            zt_ref[GROUP * n:GROUP * (n + 1), :] = jnp.transpose(z).astype(BF16)
        y_ref[...] = x_ref[...] + jnp.dot(z_ref[...], w_ref[...], preferred_element_type=F32)

    row = lambda width: pl.BlockSpec((tm, width), lambda i: (i, 0))
    return pl.pallas_call(
        body, name="out_fwd", grid=(s // tm,),
        in_specs=[pl.BlockSpec((tm, 1536), lambda i: (i, 0)),
                  pl.BlockSpec((HALO, 1536), lambda i: (jnp.maximum(i * (tm // HALO) - 1, 0), 0)),
                  pl.BlockSpec((tm, 1536), lambda i: (i, 1)),
                  row(512), row(512), row(D_MODEL),
                  pl.BlockSpec((D_MIX, D_MODEL), lambda i: (0, 0)),
                  pl.BlockSpec((HALO, 512), lambda i: (0, 0))],
        out_specs=[row(D_MODEL), pl.BlockSpec((D_MIX, tm), lambda i: (0, i))],
        out_shape=[jax.ShapeDtypeStruct((s, D_MODEL), F32), jax.ShapeDtypeStruct((D_MIX, s), BF16)],
        scratch_shapes=[pltpu.VMEM((tm, D_MIX), BF16)],
        compiler_params=_params(("parallel",)),
    )(proj, proj, proj, o_mla, o_swa, x, w_out, cw)


def _out_bwd(dy, proj, o_mla, o_swa, w_out, cw):
    s = proj.shape[0]
    tm = min(512, s)
    nt = s // tm
    hb = tm // HALO

    def body(dy_ref, dyn_ref, conv_ref, convp_ref, convn_ref, gates_ref, gatesn_ref, om_ref, os_ref,
             w_ref, cw_ref,
             dconv_ref, dgates_ref, dom_ref, domt_ref, delta_ref, dos_ref, dcw_ref):
        i = pl.program_id(0)
        dz = _mm_nt(dy_ref[...], w_ref[...])

        def gate(n):
            g = gates_ref[:, GROUP * n:GROUP * (n + 1)]
            sg = _sigmoid(g)
            return g * sg, sg * (1.0 + g * (1.0 - sg))

        for n, o_ref, do_ref in ((0, om_ref, dom_ref), (2, os_ref, dos_ref)):
            silu, dsilu = gate(n)
            dzn = dz[:, GROUP * n:GROUP * (n + 1)]
            o = o_ref[...]
            do = dzn * silu
            do_ref[...] = do.astype(do_ref.dtype)
            dgates_ref[:, GROUP * n:GROUP * (n + 1)] = (dzn * o * dsilu).astype(BF16)
            if n == 0:
                domt_ref[...] = jnp.transpose(do).astype(BF16)
                t = do * o
                for h in range(MLA_HEADS):
                    delta_ref[:, h:h + 1] = jnp.sum(t[:, MLA_V * h:MLA_V * (h + 1)], axis=-1,
                                                    keepdims=True)

        c_h, c_b, c_c, u, u1, u2, yc = _conv_fwd_vals(conv_ref, convp_ref, cw_ref, i == 0)
        silu, dsilu = gate(1)
        dzc = dz[:, GROUP:2 * GROUP]
        dgates_ref[:, GROUP:2 * GROUP] = (dzc * (c_b * yc) * dsilu).astype(BF16)
        dycr = dzc * silu
        dyc = dycr * c_b
        gn = gatesn_ref[:, GROUP:2 * GROUP]
        dzc_n = _mm_nt(dyn_ref[...], w_ref[GROUP:2 * GROUP, :])
        dyc_n = jnp.where(i == nt - 1, 0.0, dzc_n * (gn * _sigmoid(gn)) * convn_ref[:, 512:1024])
        d1 = _shift_up(dyc, dyc_n, 1)
        d2 = _shift_up(dyc, dyc_n, 2)
        du = cw_ref[2:3, :] * dyc + cw_ref[1:2, :] * d1 + cw_ref[0:1, :] * d2
        dconv_ref[:, 0:512] = (du * c_c).astype(BF16)
        dconv_ref[:, 512:1024] = (dycr * yc).astype(BF16)
        dconv_ref[:, 1024:1536] = (du * c_h).astype(BF16)

        @pl.when(i == 0)
        def _():
            dcw_ref[...] = jnp.zeros_like(dcw_ref)

        for k, uk in enumerate((u2, u1, u)):
            dcw_ref[k:k + 1, :] += jnp.sum(dyc * uk, axis=0, keepdims=True)

    row = lambda width: pl.BlockSpec((tm, width), lambda i: (i, 0))
    prev = lambda i: jnp.maximum(i * hb - 1, 0)
    nxt = lambda i: jnp.minimum((i + 1) * hb, s // HALO - 1)
    return pl.pallas_call(
        body, name="out_bwd", grid=(nt,),
        in_specs=[row(D_MODEL),
                  pl.BlockSpec((HALO, D_MODEL), lambda i: (nxt(i), 0)),
                  pl.BlockSpec((tm, 1536), lambda i: (i, 0)),
                  pl.BlockSpec((HALO, 1536), lambda i: (prev(i), 0)),
                  pl.BlockSpec((HALO, 1536), lambda i: (nxt(i), 0)),
                  pl.BlockSpec((tm, 1536), lambda i: (i, 1)),
                  pl.BlockSpec((HALO, 1536), lambda i: (nxt(i), 1)),
                  row(512), row(512),
                  pl.BlockSpec((D_MIX, D_MODEL), lambda i: (0, 0)),
                  pl.BlockSpec((HALO, 512), lambda i: (0, 0))],
        out_specs=[row(1536), row(1536), row(512), pl.BlockSpec((512, tm), lambda i: (0, i)),
                   row(MLA_HEADS), row(512), pl.BlockSpec((HALO, 512), lambda i: (0, 0))],
        out_shape=[jax.ShapeDtypeStruct((s, 1536), BF16), jax.ShapeDtypeStruct((s, 1536), BF16),
                   jax.ShapeDtypeStruct((s, 512), BF16), jax.ShapeDtypeStruct((512, s), BF16),
                   jax.ShapeDtypeStruct((s, MLA_HEADS), F32),
                   jax.ShapeDtypeStruct((s, 512), BF16), jax.ShapeDtypeStruct((HALO, 512), F32)],
        compiler_params=_params(("arbitrary",)),
    )(dy, dy, proj, proj, proj, proj, proj, o_mla, o_swa, w_out, cw)


def _loss_head(y, target):
    s, d = y.shape
    tm = min(512, s)
    nt = s // tm

    def body(y_ref, t_ref, dy_ref, loss_ref):
        i = pl.program_id(0)
        err = y_ref[...] - t_ref[...]
        dy_ref[...] = err * (1.0 / d)

        @pl.when(i == 0)
        def _():
            loss_ref[...] = jnp.zeros_like(loss_ref)

        sq = jnp.sum((err * err).reshape(tm // 8, 8, d), axis=0)
        part = sq[:, 0:LANES]
        for c in range(1, d // LANES):
            part = part + sq[:, LANES * c:LANES * (c + 1)]
        loss_ref[...] += part

        @pl.when(i == nt - 1)
        def _():
            loss_ref[...] = jnp.full(loss_ref.shape, (0.5 / d) * jnp.sum(loss_ref[...]), F32)

    return pl.pallas_call(
        body, name="loss_head", grid=(nt,),
        in_specs=[pl.BlockSpec((tm, d), lambda i: (i, 0)), pl.BlockSpec((tm, d), lambda i: (i, 0))],
        out_specs=[pl.BlockSpec((tm, d), lambda i: (i, 0)), pl.BlockSpec((8, LANES), lambda i: (0, 0))],
        out_shape=[jax.ShapeDtypeStruct((s, d), F32), jax.ShapeDtypeStruct((8, LANES), F32)],
        compiler_params=_params(("arbitrary",)),
    )(y, target)


def _adam_update(g, w, m, v):
    c1 = 1.0 - ADAM_B1
    c2 = 1.0 - ADAM_B2
    bc1 = 1.0 - ADAM_B1 ** ADAM_STEP
    bc2 = 1.0 - ADAM_B2 ** ADAM_STEP
    m_new = ADAM_B1 * m + c1 * g
    v_new = ADAM_B2 * v + c2 * (g * g)
    delta = -ADAM_LR * ((m_new / bc1) / (jnp.sqrt(v_new / bc2) + ADAM_EPS) + ADAM_WD * w)
    return delta, m_new, v_new


def _adamw(g, w, m, v):
    rows = g.shape[0]
    tr = min(256, rows)

    def body(g_ref, w_ref, m_ref, v_ref, d_ref, mo_ref, vo_ref):
        d_ref[...], mo_ref[...], vo_ref[...] = _adam_update(g_ref[...], w_ref[...], m_ref[...],
                                                            v_ref[...])

    spec = pl.BlockSpec((tr, g.shape[1]), lambda i: (i, 0))
    return pl.pallas_call(
        body, name="adamw", grid=(rows // tr,),
        in_specs=[spec] * 4, out_specs=[spec] * 3,
        out_shape=[jax.ShapeDtypeStruct(g.shape, F32)] * 3,
        compiler_params=_params(("parallel",)),
    )(g, w, m, v)


def _adamw_small(gs, ws, ms, vs):
    n = len(gs)

    def body(*refs):
        ins, outs = refs[:4 * n], refs[4 * n:]
        for a in range(n):
            res = _adam_update(*(ins[kind * n + a][...] for kind in range(4)))
            for kind in range(3):
                outs[kind * n + a][...] = res[kind]

    vmem = pl.BlockSpec(memory_space=pltpu.VMEM)
    out = pl.pallas_call(
        body, name="adamw_small",
        in_specs=[vmem] * (4 * n), out_specs=[vmem] * (3 * n),
        out_shape=[jax.ShapeDtypeStruct(g.shape, F32) for _ in range(3) for g in gs],
    )(*gs, *ws, *ms, *vs)
    return out[:n], out[n:2 * n], out[2 * n:]


HBM_SPEC = pl.BlockSpec(memory_space=pltpu.HBM)


def _place():
    x, y, c = lax.axis_index("x"), lax.axis_index("y"), lax.axis_index("c")
    chips = [(1 - x, y), (x, 1 - y), (1 - x, 1 - y)]
    return x, y, c, chips


def _all_gather(shards):
    na = len(shards)
    halves = [sh.shape[0] // 2 for sh in shards]

    def body(*refs):
        w_refs, a_refs = refs[:na], refs[na:2 * na]
        send_sems, recv_sems = refs[2 * na:]
        x, y, c, chips = _place()
        k = 2 * x + y
        sib = (x, y, 1 - c)

        def slab(a, kk, hc):
            return a_refs[a].at[kk, pl.ds(hc * halves[a], halves[a]), :]

        def copy(a, n, src, dst, to):
            return pltpu.make_async_remote_copy(
                src_ref=src, dst_ref=dst, send_sem=send_sems.at[6 * a + n],
                recv_sem=recv_sems.at[6 * a + n], device_id=to, device_id_type=MESH)

        first = [copy(a, n, w_refs[a].at[pl.ds(c * halves[a], halves[a]), :], slab(a, k, c),
                      (cx, cy, c))
                 for n, (cx, cy) in enumerate(chips) for a in range(na)]
        for cp in first:
            cp.start()
        passed = []
        for n, (cx, cy) in enumerate(chips):
            kk = 2 * cx + cy
            for a in range(na):
                copy(a, n, slab(a, kk, c), slab(a, kk, c), (cx, cy, c)).wait_recv()
                fwd = copy(a, 3 + n, slab(a, kk, c), slab(a, kk, c), sib)
                fwd.start()
                passed.append(fwd)
        for n, (cx, cy) in enumerate(chips):
            kk = 2 * cx + cy
            for a in range(na):
                copy(a, 3 + n, slab(a, kk, 1 - c), slab(a, kk, 1 - c), sib).wait_recv()
        for cp in first + passed:
            cp.wait_send()

    return pl.pallas_call(
        body, name="weights_all_gather",
        in_specs=[HBM_SPEC] * na, out_specs=[HBM_SPEC] * na,
        out_shape=[jax.ShapeDtypeStruct((N_CHIPS,) + sh.shape, sh.dtype) for sh in shards],
        scratch_shapes=[pltpu.SemaphoreType.DMA((6 * na,)), pltpu.SemaphoreType.DMA((6 * na,))],
    )(*shards)


def _fill_own_slab(buf, src, k_idx):
    n, rows, cols = buf.shape
    tr = _row_tile(rows)
    slabs = src.ndim == 3

    def body(k_ref, src_ref, buf_ref, out_ref):
        out_ref[0] = src_ref[0] if slabs else src_ref[...]

    if slabs:
        src_spec = pl.BlockSpec((1, tr, cols), lambda t, k_ref: (k_ref[0], t, 0))
    else:
        src_spec = pl.BlockSpec((tr, cols), lambda t, k_ref: (t, 0))
    return pl.pallas_call(
        body, name="fill_own_slab",
        grid_spec=pltpu.PrefetchScalarGridSpec(
            num_scalar_prefetch=1, grid=(rows // tr,),
            in_specs=[src_spec, pl.BlockSpec(memory_space=pl.ANY)],
            out_specs=pl.BlockSpec((1, tr, cols), lambda t, k_ref: (k_ref[0], t, 0))),
        out_shape=jax.ShapeDtypeStruct(buf.shape, buf.dtype),
        input_output_aliases={2: 0},
        compiler_params=_params(("parallel",)),
    )(k_idx, src, buf)


def _swap_halves_to_sibling(gs):
    na = len(gs)

    def body(*refs):
        g_refs, r_refs = refs[:na], refs[na:2 * na]
        send_sems, recv_sems = refs[2 * na:]
        x, y, c, _ = _place()
        cps = []
        for a in range(na):
            half = g_refs[a].shape[1] // 2
            cps.append(pltpu.make_async_remote_copy(
                src_ref=g_refs[a].at[:, pl.ds((1 - c) * half, half), :], dst_ref=r_refs[a],
                send_sem=send_sems.at[a], recv_sem=recv_sems.at[a], device_id=(x, y, 1 - c),
                device_id_type=MESH))
        for cp in cps:
            cp.start()
        for cp in cps:
            cp.wait()

    return pl.pallas_call(
        body, name="grads_to_sibling",
        in_specs=[HBM_SPEC] * na, out_specs=[HBM_SPEC] * na,
        out_shape=[jax.ShapeDtypeStruct((g.shape[0], g.shape[1] // 2, g.shape[2]), g.dtype)
                   for g in gs],
        scratch_shapes=[pltpu.SemaphoreType.DMA((na,)), pltpu.SemaphoreType.DMA((na,))],
    )(*gs)


def _row_tile(rows):
    return 256 if rows % 256 == 0 else 128


def _add_sibling(g, r, c_idx, out_dtype):
    n, rows, cols = g.shape
    half = rows // 2
    tr = _row_tile(half)
    nb = half // tr

    def body(c_ref, g_ref, r_ref, p_ref):
        p_ref[...] = (g_ref[...] + r_ref[...]).astype(out_dtype)

    return pl.pallas_call(
        body, name="grads_add_sibling",
        grid_spec=pltpu.PrefetchScalarGridSpec(
            num_scalar_prefetch=1, grid=(n, nb),
            in_specs=[pl.BlockSpec((1, tr, cols), lambda j, t, c_ref: (j, c_ref[0] * nb + t, 0)),
                      pl.BlockSpec((1, tr, cols), lambda j, t, c_ref: (j, t, 0))],
            out_specs=pl.BlockSpec((1, tr, cols), lambda j, t, c_ref: (j, t, 0))),
        out_shape=jax.ShapeDtypeStruct((n, half, cols), out_dtype),
        compiler_params=_params(("parallel", "parallel")),
    )(c_idx, g, r)


def _scatter_to_chips(ps):
    na = len(ps)

    def body(*refs):
        p_refs, q_refs = refs[:na], refs[na:2 * na]
        send_sems, recv_sems = refs[2 * na:]
        x, y, c, chips = _place()
        k = 2 * x + y
        sends = []
        for i, (cx, cy) in enumerate(chips):
            for a in range(na):
                cp = pltpu.make_async_remote_copy(
                    src_ref=p_refs[a].at[2 * cx + cy], dst_ref=q_refs[a].at[k],
                    send_sem=send_sems.at[3 * a + i], recv_sem=recv_sems.at[3 * a + i],
                    device_id=(cx, cy, c), device_id_type=MESH)
                cp.start()
                sends.append(cp)
        for i, (cx, cy) in enumerate(chips):
            kk = 2 * cx + cy
            for a in range(na):
                pltpu.make_async_remote_copy(
                    src_ref=p_refs[a].at[kk], dst_ref=q_refs[a].at[kk],
                    send_sem=send_sems.at[3 * a + i], recv_sem=recv_sems.at[3 * a + i],
                    device_id=(cx, cy, c), device_id_type=MESH).wait_recv()
        for cp in sends:
            cp.wait_send()

    return pl.pallas_call(
        body, name="grads_scatter_to_chips",
        in_specs=[HBM_SPEC] * na, out_specs=[HBM_SPEC] * na,
        out_shape=[jax.ShapeDtypeStruct(p.shape, p.dtype) for p in ps],
        scratch_shapes=[pltpu.SemaphoreType.DMA((3 * na,)), pltpu.SemaphoreType.DMA((3 * na,))],
    )(*ps)


def _sum_chips(q, c_idx):
    n, half, cols = q.shape
    tr = _row_tile(half)
    nb = half // tr

    def body(c_ref, q_ref, o_ref):
        parts = [q_ref[kk].astype(F32) for kk in range(n)]
        o_ref[...] = ((parts[0] + parts[1]) + parts[2]) + parts[3]

    return pl.pallas_call(
        body, name="grads_sum_chips",
        grid_spec=pltpu.PrefetchScalarGridSpec(
            num_scalar_prefetch=1, grid=(nb,),
            in_specs=[pl.BlockSpec((n, tr, cols), lambda t, c_ref: (0, t, 0))],
            out_specs=pl.BlockSpec((tr, cols), lambda t, c_ref: (c_ref[0] * nb + t, 0))),
        out_shape=jax.ShapeDtypeStruct((2 * half, cols), F32),
        compiler_params=_params(("parallel",)),
    )(c_idx, q)


def _join_halves(fulls):
    na = len(fulls)

    def body(*refs):
        o_refs = refs[na:2 * na]
        send_sems, recv_sems = refs[2 * na:]
        x, y, c, _ = _place()
        sends = []
        for a in range(na):
            half = o_refs[a].shape[0] // 2
            rows = o_refs[a].at[pl.ds(c * half, half), :]
            sends.append(pltpu.make_async_remote_copy(
                src_ref=rows, dst_ref=rows, send_sem=send_sems.at[a], recv_sem=recv_sems.at[a],
                device_id=(x, y, 1 - c), device_id_type=MESH))
        for cp in sends:
            cp.start()
        for a in range(na):
            half = o_refs[a].shape[0] // 2
            other = o_refs[a].at[pl.ds((1 - c) * half, half), :]
            pltpu.make_async_remote_copy(
                src_ref=other, dst_ref=other, send_sem=send_sems.at[a], recv_sem=recv_sems.at[a],
                device_id=(x, y, 1 - c), device_id_type=MESH).wait_recv()
        for cp in sends:
            cp.wait_send()

    return pl.pallas_call(
        body, name="grads_join_halves",
        in_specs=[HBM_SPEC] * na, out_specs=[HBM_SPEC] * na,
        out_shape=[jax.ShapeDtypeStruct(f.shape, f.dtype) for f in fulls],
        input_output_aliases={a: a for a in range(na)},
        scratch_shapes=[pltpu.SemaphoreType.DMA((na,)), pltpu.SemaphoreType.DMA((na,))],
    )(*fulls)


def _part_rows(shape):
    size = 1
    for d in shape:
        size *= d
    rows = -(-size // PACK_COLS)
    return size, -(-rows // PACK_ROW_ALIGN) * PACK_ROW_ALIGN


def _pack_rows(arrays, dtype, total_rows):
    parts, used = [], 0
    for a in arrays:
        size, rows = _part_rows(a.shape)
        flat = a.reshape(-1).astype(dtype)
        parts.append(jnp.pad(flat, (0, rows * PACK_COLS - size)).reshape(rows, PACK_COLS))
        used += rows
    parts.append(jnp.zeros((total_rows - used, PACK_COLS), dtype))
    return jnp.concatenate(parts, axis=0)


def _unpack_rows(buf, shapes):
    lead = buf.shape[:-2]
    out, off = [], 0
    for sh in shapes:
        size, rows = _part_rows(sh)
        part = buf[..., off:off + rows, :].reshape(lead + (-1,))[..., :size]
        out.append(part.reshape(lead + tuple(sh)))
        off += rows
    return out


NEW_ORDER = ((928, 1440), (1440, 1952), (1952, 2464), (416, 928), (2464, 2976), (3744, 4256),
             (2976, 3488), (0, 256), (256, 384), (4256, 4320), (384, 416), (4256, 4288),
             (3488, 3616), (3616, 3744))
OLD_ORDER = ((3584, 3840), (3840, 3968), (4032, 4064), (1536, 2048), (0, 512), (512, 1024),
             (1024, 1536), (2048, 2560), (3072, 3584), (4096, 4224), (4224, 4352), (2560, 3072))


def _cols(sources, ranges):
    parts = []
    for a, b in ranges:
        off = 0
        for src in sources:
            width = src.shape[-1]
            lo, hi = max(a, off), min(b, off + width)
            if lo < hi:
                parts.append(src[..., lo - off:hi - off])
            off += width
    return jnp.concatenate(parts, axis=-1)


def _sub_ranges(ranges, a, b):
    out, off = [], 0
    for lo, hi in ranges:
        width = hi - lo
        s0, s1 = max(a, off), min(b, off + width)
        if s0 < s1:
            out.append((lo + s0 - off, lo + s1 - off))
        off += width
    return out


def _rope_tables(s):
    half = MLA_ROPE // 2
    inv_freq = jnp.power(jnp.float32(ROPE_THETA), -jnp.arange(half, dtype=F32) / half)
    ang = inv_freq[:, None] * jnp.arange(s, dtype=F32)[None, :]
    cos, sin = jnp.cos(ang), jnp.sin(ang)
    z = lambda n: jnp.zeros((n, s), F32)
    c = jnp.concatenate([jnp.ones((MLA_NOPE, s), F32), cos, cos, z(32)], axis=0)
    s1 = jnp.concatenate([z(MLA_NOPE), -sin, z(16), z(32)], axis=0)
    s2 = jnp.concatenate([z(MLA_NOPE), z(16), sin, z(32)], axis=0)
    return c, s1, s2


def _pad_lanes(a, n):
    return jnp.pad(a, ((0, 0), (0, n - a.shape[1])))


SHARDED = ("w_in", "w_out", "mla_w_qb", "mla_w_kvb", "conv_w")
REPLICATED = ("norm_g", "mla_q_a_norm", "mla_kv_a_norm", "mla_q_norm", "mla_k_norm",
              "swa_q_norm", "swa_k_norm", "swa_sinks")
WEIGHT_ORDER = ("norm_g", "w_in", "mla_q_a_norm", "mla_w_qb", "mla_kv_a_norm", "mla_w_kvb",
                "mla_q_norm", "mla_k_norm", "conv_w", "swa_q_norm", "swa_k_norm", "swa_sinks", "w_out")
SHARD_AXIS = {"w_in": 2, "w_out": 1, "mla_w_qb": 2, "mla_w_kvb": 2, "conv_w": 2}


def kernel(x, norm_g, w_in, mla_q_a_norm, mla_w_qb, mla_kv_a_norm, mla_w_kvb, mla_q_norm, mla_k_norm, conv_w, swa_q_norm, swa_k_norm, swa_sinks, w_out, loss_target, m_norm_g, m_w_in, m_mla_q_a_norm, m_mla_w_qb, m_mla_kv_a_norm, m_mla_w_kvb, m_mla_q_norm, m_mla_k_norm, m_conv_w, m_swa_q_norm, m_swa_k_norm, m_swa_sinks, m_w_out, v_norm_g, v_w_in, v_mla_q_a_norm, v_mla_w_qb, v_mla_kv_a_norm, v_mla_w_kvb, v_mla_q_norm, v_mla_k_norm, v_conv_w, v_swa_q_norm, v_swa_k_norm, v_swa_sinks, v_w_out):
    weights = dict(norm_g=norm_g, w_in=w_in, mla_q_a_norm=mla_q_a_norm, mla_w_qb=mla_w_qb,
                   mla_kv_a_norm=mla_kv_a_norm, mla_w_kvb=mla_w_kvb, mla_q_norm=mla_q_norm,
                   mla_k_norm=mla_k_norm, conv_w=conv_w, swa_q_norm=swa_q_norm,
                   swa_k_norm=swa_k_norm, swa_sinks=swa_sinks, w_out=w_out)
    mom_m = dict(norm_g=m_norm_g, w_in=m_w_in, mla_q_a_norm=m_mla_q_a_norm, mla_w_qb=m_mla_w_qb,
                 mla_kv_a_norm=m_mla_kv_a_norm, mla_w_kvb=m_mla_w_kvb, mla_q_norm=m_mla_q_norm,
                 mla_k_norm=m_mla_k_norm, conv_w=m_conv_w, swa_q_norm=m_swa_q_norm,
                 swa_k_norm=m_swa_k_norm, swa_sinks=m_swa_sinks, w_out=m_w_out)
    mom_v = dict(norm_g=v_norm_g, w_in=v_w_in, mla_q_a_norm=v_mla_q_a_norm, mla_w_qb=v_mla_w_qb,
                 mla_kv_a_norm=v_mla_kv_a_norm, mla_w_kvb=v_mla_w_kvb, mla_q_norm=v_mla_q_norm,
                 mla_k_norm=v_mla_k_norm, conv_w=v_conv_w, swa_q_norm=v_swa_q_norm,
                 swa_k_norm=v_swa_k_norm, swa_sinks=v_swa_sinks, w_out=v_w_out)
    xs = x[0]
    target = loss_target[0]
    s = xs.shape[0]
    c_idx = lax.axis_index("c").astype(jnp.int32).reshape(1)
    k_idx = (2 * lax.axis_index("x") + lax.axis_index("y")).astype(jnp.int32).reshape(1)

    conv_bits = lax.bitcast_convert_type(conv_w, BF16)
    small_list = [w_out, mla_w_qb, mla_w_kvb, conv_bits]
    w_in_rows = DEPTH * D_MODEL
    own = [w_in.astype(BF16).reshape(w_in_rows, w_in.shape[2]), _pack_rows(small_list, BF16, PACK_ROWS)]
    gathered_in, gathered_rest = [_fill_own_slab(buf, src, k_idx)
                                  for buf, src in zip(_all_gather(own), own)]
    parts = _unpack_rows(gathered_rest, [a.shape for a in small_list])
    join = lambda p, axis: jnp.concatenate([p[k] for k in range(N_CHIPS)], axis=axis)
    w_in_slabs = gathered_in.reshape(N_CHIPS, DEPTH, D_MODEL, w_in.shape[2])
    w_in_zeros = jnp.zeros((D_MODEL, 64), BF16)
    w_out_full = join(parts[0], 1)
    w_qb_full = join(parts[1], 2)
    w_kvb_full = join(parts[2], 2)
    conv_full = lax.bitcast_convert_type(join(parts[3], 2), F32)

    rope = _rope_tables(s)
    swa_tables = _swa_tables()
    layers = []
    for l in range(DEPTH):
        wq = jnp.pad(w_qb_full[l].reshape(MLA_Q_LORA, MLA_HEADS, MLA_QK),
                     ((0, 0), (0, 0), (0, LANES - MLA_QK))).reshape(MLA_Q_LORA, MLA_HEADS * LANES)
        kv = w_kvb_full[l].reshape(MLA_KV_LORA, MLA_HEADS, MLA_NOPE + MLA_V)
        wk = jnp.pad(kv[:, :, :MLA_NOPE], ((0, 0), (0, 0), (0, LANES - MLA_NOPE)))
        wkv = jnp.concatenate([wk.reshape(MLA_KV_LORA, MLA_HEADS * LANES),
                               kv[:, :, MLA_NOPE:].reshape(MLA_KV_LORA, MLA_HEADS * MLA_V)], axis=1)
        layers.append(dict(
            w_in=_cols([w_in_slabs[k, l] for k in range(N_CHIPS)] + [w_in_zeros], NEW_ORDER),
            w_out=w_out_full[l], wq=wq, wkv=wkv,
            cw=jnp.pad(conv_full[l], ((0, HALO - 3), (0, 0))),
            g=norm_g[l][None],
            mla_norms=(mla_q_a_norm[l][:, None], mla_kv_a_norm[l][:, None],
                       _pad_lanes(mla_q_norm[l][None], LANES).T, _pad_lanes(mla_k_norm[l][None], LANES).T),
            mla_weights=(wq, wkv, wq.T, wkv.T),
            sqn=swa_q_norm[l][:, None], skn=swa_k_norm[l][:, None], sinks=swa_sinks[l][None]))

    saved = []
    h_in = xs
    for l in range(DEPTH):
        p = layers[l]
        proj, hb = _in_proj_fwd(h_in, p["g"], p["w_in"])
        q, k, v, qt, kt, vt = _mla_prep_fwd(proj, p["mla_norms"], p["mla_weights"], rope)
        o_mla, lse = _mla_attn_fwd(q, k, vt)
        o_swa = _swa_fwd(proj, p["sqn"], p["skn"], p["sinks"], swa_tables)
        y, z = _out_fwd(proj, o_mla, o_swa, h_in, p["w_out"], p["cw"])
        saved.append(dict(x=h_in, proj=proj, hb=hb, q=q, k=k, v=v, qt=qt, kt=kt, o_mla=o_mla, lse=lse,
                          o_swa=o_swa, z=z))
        h_in = y

    dy, loss_acc = _loss_head(h_in, target)
    loss = lax.psum(loss_acc[0, 0], ("x", "y", "c"))

    grads = {n: [None] * DEPTH for n in WEIGHT_ORDER}
    for l in reversed(range(DEPTH)):
        p, a = layers[l], saved[l]
        dconv, dgates, do_mla, do_mla_t, delta, do_swa, dcw = _out_bwd(dy, a["proj"], a["o_mla"], a["o_swa"],
                                                             p["w_out"], p["cw"])
        grads["w_out"][l] = _weight_grads(a["z"], [dy], "dw_out")[0]
        grads["conv_w"][l] = dcw[0:3]
        delta_rows = jnp.transpose(delta, (1, 0)).reshape(MLA_HEADS // 2, 2, s)
        dq, dk, dv = _mla_attn_bwd(a["q"], a["qt"], a["k"], a["kt"], a["v"], do_mla, do_mla_t,
                                   a["lse"], delta_rows)
        dmla, dqan, dkvan, dqn, dkn, dwq_t, dwkv_t = _mla_prep_bwd(
            a["proj"], p["mla_norms"], p["mla_weights"], rope, dq, dk, dv)
        dwq, dwkv = dwq_t.T, dwkv_t.T
        dsq, dskv, dsqn, dskn, dsinks = _swa_bwd(a["proj"], p["sqn"], p["skn"], p["sinks"], swa_tables, do_swa)
        pieces = [dconv, dgates, dsq, dmla, dskv]
        dx, dg = _in_proj_bwd(pieces, a["x"], p["g"], p["w_in"], dy)
        grads["w_in"][l] = _weight_grads(a["hb"], pieces, "dw_in")
        grads["norm_g"][l] = dg[0]
        grads["mla_q_a_norm"][l] = dqan[:, 0]
        grads["mla_kv_a_norm"][l] = dkvan[:, 0]
        grads["mla_q_norm"][l] = dqn[:MLA_QK, 0]
        grads["mla_k_norm"][l] = dkn[:MLA_QK, 0]
        grads["mla_w_qb"][l] = dwq.reshape(MLA_Q_LORA, MLA_HEADS, LANES)[:, :, :MLA_QK].reshape(
            MLA_Q_LORA, MLA_HEADS * MLA_QK)
        dwk = dwkv[:, :MLA_HEADS * LANES].reshape(MLA_KV_LORA, MLA_HEADS, LANES)[:, :, :MLA_NOPE]
        dwv = dwkv[:, MLA_HEADS * LANES:].reshape(MLA_KV_LORA, MLA_HEADS, MLA_V)
        grads["mla_w_kvb"][l] = jnp.concatenate([dwk, dwv], axis=2).reshape(
            MLA_KV_LORA, MLA_HEADS * (MLA_NOPE + MLA_V))
        grads["swa_q_norm"][l] = dsqn[:, 0]
        grads["swa_k_norm"][l] = dskn[:, 0]
        grads["swa_sinks"][l] = dsinks[0]
        dy = dx
    grad_x = dy[None]
    full_grads = {n: jnp.stack(grads[n]) for n in WEIGHT_ORDER if n != "w_in"}

    rest = tuple(n for n in SHARDED if n != "w_in")
    rep_shapes = [weights[n].shape for n in REPLICATED]
    flat_rep = lambda d: jnp.concatenate([d[n].reshape(-1) for n in REPLICATED])

    def chunk(g, n, k):
        width = g.shape[SHARD_AXIS[n]] // N_CHIPS
        return lax.slice_in_dim(g, k * width, (k + 1) * width, axis=SHARD_AXIS[n])

    shard_cols = w_in.shape[2]
    g_in = jnp.stack([
        jnp.concatenate([_cols(grads["w_in"][l], _sub_ranges(OLD_ORDER, k * shard_cols,
                                                             (k + 1) * shard_cols))
                         for l in range(DEPTH)], axis=0)
        for k in range(N_CHIPS)])
    rep_grads = flat_rep(full_grads)
    g_rest = jnp.stack([_pack_rows([chunk(full_grads[n], n, k) for n in rest] + [rep_grads],
                                   F32, PACK_ROWS) for k in range(N_CHIPS)])
    from_sibling = _swap_halves_to_sibling([g_in, g_rest])
    partial = [_add_sibling(g, r, c_idx, dt)
               for g, r, dt in zip((g_in, g_rest), from_sibling, (BF16, F32))]
    by_chip = [_fill_own_slab(q, p, k_idx) for q, p in zip(_scatter_to_chips(partial), partial)]
    g_in_mine, g_rest_mine = _join_halves([_sum_chips(q, c_idx) for q in by_chip])

    vals = _unpack_rows(g_rest_mine, [weights[n].shape for n in rest] + [(rep_grads.shape[0],)])
    grad = dict(zip(rest, vals[:-1]))
    grad["w_in"] = g_in_mine.reshape(w_in.shape)
    off = 0
    for n, sh in zip(REPLICATED, rep_shapes):
        grad[n] = vals[-1][off:off + sh[0] * sh[1]].reshape(sh)
        off += sh[0] * sh[1]
    results = {}
    for n in ("w_in", "w_out"):
        view = lambda a, n=n: a.reshape(-1, weights[n].shape[-1])
        res = _adamw(view(grad[n]), view(weights[n]), view(mom_m[n]), view(mom_v[n]))
        results[n] = [r.reshape(weights[n].shape) for r in res]
    small = tuple(n for n in WEIGHT_ORDER if n not in results)
    res = _adamw_small(*([d[n] for n in small] for d in (grad, weights, mom_m, mom_v)))
    for a, n in enumerate(small):
        results[n] = [res[kind][a] for kind in range(3)]
    unpacked = [grad] + [{n: results[n][kind] for n in WEIGHT_ORDER} for kind in range(3)]
    outs = [loss, grad_x]
    for group in unpacked:
        outs += [group[n] for n in WEIGHT_ORDER]
    return tuple(outs)
```

```python
import jax
import numpy as np
import jax.numpy as jnp
from jax import lax
from jax.experimental import pallas as pl
from jax.experimental.pallas import tpu as pltpu

F32 = jnp.float32
BF16 = jnp.bfloat16

D_MODEL = 1024
DEPTH = 2
GROUP = 512
D_MIX = 3 * GROUP
BLOCK = 128
RMS_EPS = 1e-6
NEG_INF = -1e30
MLA_HEADS = 8
MLA_QK = 96
MLA_NOPE = 64
MLA_ROPE = 32
MLA_V = 64
V_AUG = 80
MLA_Q_LORA = 256
MLA_KV_LORA = 128
ROPE_THETA = 10000.0
SWA_HEADS = 8
SWA_KV = 2
SWA_GROUP = 4
SWA_DIM = 64
N_CHIPS = 4

NC = 4352
OFF_SQ, OFF_MLA, OFF_SKV = 3072, 3584, 4096

VMEM_LIMIT = 56 * 1024 * 1024
LANES = 128
PACK_COLS = 1024
PACK_ROW_ALIGN = 16
PACK_ROWS = 1024
SMALL_ROWS = 256

ADAM_LR = 0.001
ADAM_B1 = 0.9
ADAM_B2 = 0.999
ADAM_EPS = 1e-08
ADAM_WD = 0.01
ADAM_STEP = 10

MESH = pl.DeviceIdType.MESH


def _params(sem, vmem=VMEM_LIMIT):
    return pltpu.CompilerParams(dimension_semantics=sem, vmem_limit_bytes=vmem)


def _dot(a, b, dims):
    return lax.dot_general(a.astype(BF16), b.astype(BF16), (dims, ((), ())),
                           preferred_element_type=F32)


def _mm(a, b):
    return _dot(a, b, ((1,), (0,)))


def _mm_nt(a, b):
    return _dot(a, b, ((1,), (1,)))


def _rms(x, g, n=None):
    n = x.shape[-1] if n is None else n
    ms = jnp.sum(x * x, axis=-1, keepdims=True) * (1.0 / n)
    return x * lax.rsqrt(ms + RMS_EPS) * g


def _sigmoid(x):
    return 1.0 / (1.0 + jnp.exp(-x))


def _in_proj_fwd(x, g, w):
    s = x.shape[0]
    tm = min(512, s)

    def body(x_ref, g_ref, w_ref, proj_ref, hbt_ref):
        h = _rms(x_ref[...], g_ref[...])
        hbt_ref[...] = jnp.transpose(h).astype(BF16)
        proj_ref[...] = jnp.dot(h.astype(BF16), w_ref[...], preferred_element_type=F32)

    return pl.pallas_call(
        body, name="in_proj_fwd", grid=(s // tm,),
        in_specs=[pl.BlockSpec((tm, D_MODEL), lambda i: (i, 0)),
                  pl.BlockSpec((1, D_MODEL), lambda i: (0, 0)),
                  pl.BlockSpec((D_MODEL, NC), lambda i: (0, 0))],
        out_specs=[pl.BlockSpec((tm, NC), lambda i: (i, 0)),
                   pl.BlockSpec((D_MODEL, tm), lambda i: (0, i))],
        out_shape=[jax.ShapeDtypeStruct((s, NC), F32), jax.ShapeDtypeStruct((D_MODEL, s), BF16)],
        compiler_params=_params(("parallel",)),
    )(x, g, w)


def _in_proj_bwd(pieces, x, g, w, dres):
    s = x.shape[0]
    tm = min(512, s)
    n_p = len(pieces)

    def body(*refs):
        p_refs = refs[:n_p]
        x_ref, g_ref, w_ref, dres_ref, dx_ref, dg_ref = refs[n_p:]
        dh = None
        off = 0
        for r in p_refs:
            width = r.shape[1]
            t = _mm_nt(r[...], w_ref[:, off:off + width])
            dh = t if dh is None else dh + t
            off += width
        _, vjp = jax.vjp(_rms, x_ref[...], g_ref[...])
        dx, dg = vjp(dh)
        dx_ref[...] = dx + dres_ref[...]

        @pl.when(pl.program_id(0) == 0)
        def _():
            dg_ref[...] = jnp.zeros_like(dg_ref)

        dg_ref[...] += dg

    in_specs = [pl.BlockSpec((tm, p.shape[1]), lambda i: (i, 0)) for p in pieces]
    in_specs += [pl.BlockSpec((tm, D_MODEL), lambda i: (i, 0)),
                 pl.BlockSpec((1, D_MODEL), lambda i: (0, 0)),
                 pl.BlockSpec((D_MODEL, NC), lambda i: (0, 0)),
                 pl.BlockSpec((tm, D_MODEL), lambda i: (i, 0))]
    return pl.pallas_call(
        body, name="in_proj_bwd", grid=(s // tm,),
        in_specs=in_specs,
        out_specs=[pl.BlockSpec((tm, D_MODEL), lambda i: (i, 0)),
                   pl.BlockSpec((1, D_MODEL), lambda i: (0, 0))],
        out_shape=[jax.ShapeDtypeStruct((s, D_MODEL), F32), jax.ShapeDtypeStruct((1, D_MODEL), F32)],
        compiler_params=_params(("arbitrary",)),
    )(*pieces, x, g, w, dres)


def _weight_grads(at, bs, name):
    m, s = at.shape
    nb = len(bs)
    tk = min(512, s)

    def body(a_ref, *refs):
        b_refs, o_refs = refs[:nb], refs[nb:]

        @pl.when(pl.program_id(0) == 0)
        def _():
            for o_ref in o_refs:
                o_ref[...] = jnp.zeros_like(o_ref)

        a = a_ref[...]
        for b_ref, o_ref in zip(b_refs, o_refs):
            o_ref[...] += _mm(a, b_ref[...])

    return pl.pallas_call(
        body, name=name, grid=(s // tk,),
        in_specs=[pl.BlockSpec((m, tk), lambda k: (0, k))]
        + [pl.BlockSpec((tk, b.shape[1]), lambda k: (k, 0)) for b in bs],
        out_specs=[pl.BlockSpec((m, b.shape[1]), lambda k: (0, 0)) for b in bs],
        out_shape=[jax.ShapeDtypeStruct((m, b.shape[1]), F32) for b in bs],
        compiler_params=_params(("arbitrary",)),
    )(at, *bs)


def _rms0(x, g, n=None):
    n = x.shape[0] if n is None else n
    ms = jnp.sum(x * x, axis=0, keepdims=True) * (1.0 / n)
    return x * lax.rsqrt(ms + RMS_EPS) * g


@jax.custom_vjp
def _rope0(t, c, s1, s2):
    return t * c + pltpu.roll(t, LANES - 16, 0) * s1 + pltpu.roll(t, 16, 0) * s2


def _rope0_fwd(t, c, s1, s2):
    return _rope0(t, c, s1, s2), (c, s1, s2)


def _rope0_bwd(res, g):
    c, s1, s2 = res
    dt = g * c + pltpu.roll(g * s1, 16, 0) + pltpu.roll(g * s2, LANES - 16, 0)
    return dt, jnp.zeros_like(c), jnp.zeros_like(s1), jnp.zeros_like(s2)


_rope0.defvjp(_rope0_fwd, _rope0_bwd)


@jax.custom_vjp
def _mmw(w, wt, x):
    return _mm(w, x)


def _mmw_fwd(w, wt, x):
    return _mm(w, x), (wt, x)


def _mmw_bwd(res, g):
    wt, x = res
    return _mm_nt(g, x), jnp.zeros_like(wt), _mm(wt, g)


_mmw.defvjp(_mmw_fwd, _mmw_bwd)


def _prep_fn(q_lat, kv_lat, kr, qan, kvan, qn, kn, wq, wk, wv, wqt, wkt, wvt, c, s1, s2, mm):
    tokens = q_lat.shape[1]
    rq = _rms0(q_lat, qan)
    rkv = _rms0(kv_lat, kvan)
    qn_b = jnp.broadcast_to(qn, (LANES, tokens))
    kn_b = jnp.broadcast_to(kn, (LANES, tokens))
    qs, ks = [], []
    for h in range(MLA_HEADS):
        qs.append(_rope0(_rms0(mm(wq[h], wqt[h], rq), qn_b, MLA_QK), c, s1, s2))
        ks.append(_rope0(_rms0(mm(wk[h], wkt[h], rkv) + kr, kn_b, MLA_QK), c, s1, s2))
    return tuple(qs), tuple(ks), mm(wv, wvt, rkv)


def _prep_weights(wq_ref, wkv_ref, wqt_ref, wkvt_ref):
    heads = range(MLA_HEADS)
    wq = tuple(wqt_ref[LANES * h:LANES * (h + 1), :].astype(F32) for h in heads)
    wk = tuple(wkvt_ref[LANES * h:LANES * (h + 1), :].astype(F32) for h in heads)
    wv = wkvt_ref[LANES * MLA_HEADS:, :].astype(F32)
    wqt = tuple(wq_ref[:, LANES * h:LANES * (h + 1)].astype(F32) for h in heads)
    wkt = tuple(wkv_ref[:, LANES * h:LANES * (h + 1)].astype(F32) for h in heads)
    wvt = wkv_ref[:, LANES * MLA_HEADS:].astype(F32)
    return wq, wk, wv, wqt, wkt, wvt


def _prep_in_specs(tm):
    const = lambda shape: pl.BlockSpec(shape, lambda i: (0, 0))
    col = lambda height: pl.BlockSpec((height, tm), lambda i: (0, i))
    return [pl.BlockSpec((tm, 512), lambda i: (i, OFF_MLA // 512)),
            const((MLA_Q_LORA, 1)), const((MLA_KV_LORA, 1)), const((LANES, 1)), const((LANES, 1)),
            const((MLA_Q_LORA, 1024)), const((MLA_KV_LORA, 1536)),
            const((1024, MLA_Q_LORA)), const((1536, MLA_KV_LORA)),
            col(LANES), col(LANES), col(LANES)]


def _prep_operands(blk_ref, refs):
    qan_ref, kvan_ref, qn_ref, kn_ref, wq_ref, wkv_ref, wqt_ref, wkvt_ref, c_ref, s1_ref, s2_ref = refs
    blk_t = jnp.transpose(blk_ref[...])
    diff = (blk_t[0:256], blk_t[256:384], blk_t[384:512],
            qan_ref[...], kvan_ref[...], qn_ref[...], kn_ref[...])
    weights = _prep_weights(wq_ref, wkv_ref, wqt_ref, wkvt_ref)
    return diff, weights, (c_ref[...], s1_ref[...], s2_ref[...])


def _mla_prep_fwd(proj, norms, weights, rope):
    s = proj.shape[0]
    tm = min(512, s)

    def body(blk_ref, *refs):
        ins, (q_ref, k_ref, v_ref, qt_ref, kt_ref, vt_ref) = refs[:11], refs[11:]
        diff, (wq, wk, wv, wqt, wkt, wvt), tables = _prep_operands(blk_ref, ins)
        qs, ks, v = _prep_fn(*diff, wq, wk, wv, wqt, wkt, wvt, *tables,
                             lambda w, wt, x: _mm(w, x))
        for h in range(MLA_HEADS):
            q2 = qs[h] * Q_PRESCALE
            qt_ref[LANES * h:LANES * (h + 1), :] = q2.astype(BF16)
            kt_ref[LANES * h:LANES * (h + 1), :] = ks[h].astype(BF16)
            q_ref[:, LANES * h:LANES * (h + 1)] = jnp.transpose(q2).astype(BF16)
            k_ref[:, LANES * h:LANES * (h + 1)] = jnp.transpose(ks[h]).astype(BF16)
        ones_row = (lax.broadcasted_iota(jnp.int32, (V_AUG - MLA_V, v.shape[1]), 0) == 0).astype(BF16)
        for h in range(MLA_HEADS):
            vt_ref[V_AUG * h:V_AUG * h + MLA_V, :] = v[MLA_V * h:MLA_V * (h + 1)].astype(BF16)
            vt_ref[V_AUG * h + MLA_V:V_AUG * (h + 1), :] = ones_row
        v_ref[...] = jnp.transpose(v).astype(BF16)

    row = lambda width: pl.BlockSpec((tm, width), lambda i: (i, 0))
    col = lambda height: pl.BlockSpec((height, tm), lambda i: (0, i))
    return pl.pallas_call(
        body, name="mla_prep_fwd", grid=(s // tm,),
        in_specs=_prep_in_specs(tm),
        out_specs=[row(1024), row(1024), row(512), col(1024), col(1024), col(MLA_HEADS * V_AUG)],
        out_shape=[jax.ShapeDtypeStruct((s, 1024), BF16), jax.ShapeDtypeStruct((s, 1024), BF16),
                   jax.ShapeDtypeStruct((s, 512), BF16), jax.ShapeDtypeStruct((1024, s), BF16),
                   jax.ShapeDtypeStruct((1024, s), BF16),
                   jax.ShapeDtypeStruct((MLA_HEADS * V_AUG, s), BF16)],
        compiler_params=_params(("parallel",)),
    )(proj, *norms, *weights, *rope)


def _mla_prep_bwd(proj, norms, weights, rope, dq, dk, dv):
    s = proj.shape[0]
    tm = min(512, s)

    def body(blk_ref, *refs):
        ins, (dq_ref, dk_ref, dv_ref) = refs[:11], refs[11:14]
        dblk_ref, dqan_ref, dkvan_ref, dqn_ref, dkn_ref, dwq_ref, dwkv_ref = refs[14:]
        diff, (wq, wk, wv, wqt, wkt, wvt), tables = _prep_operands(blk_ref, ins)

        def fn(q_lat, kv_lat, kr, qan, kvan, qn, kn, wq_, wk_, wv_):
            return _prep_fn(q_lat, kv_lat, kr, qan, kvan, qn, kn, wq_, wk_, wv_, wqt, wkt, wvt,
                            *tables, _mmw)

        _, vjp = jax.vjp(fn, *diff, wq, wk, wv)
        heads = range(MLA_HEADS)
        cts = (tuple(dq_ref[LANES * h:LANES * (h + 1), :] for h in heads),
               tuple(dk_ref[LANES * h:LANES * (h + 1), :] for h in heads), dv_ref[...])
        dq_lat, dkv_lat, dkr, dqan, dkvan, dqn, dkn, dwq_h, dwk_h, dwv = vjp(cts)
        dblk_ref[...] = jnp.transpose(
            jnp.concatenate([dq_lat, dkv_lat, dkr], axis=0)).astype(BF16)

        @pl.when(pl.program_id(0) == 0)
        def _():
            for r in (dqan_ref, dkvan_ref, dqn_ref, dkn_ref, dwq_ref, dwkv_ref):
                r[...] = jnp.zeros_like(r)

        dqan_ref[...] += dqan
        dkvan_ref[...] += dkvan
        dqn_ref[...] += dqn
        dkn_ref[...] += dkn
        for h in heads:
            dwq_ref[LANES * h:LANES * (h + 1), :] += dwq_h[h]
            dwkv_ref[LANES * h:LANES * (h + 1), :] += dwk_h[h]
        dwkv_ref[LANES * MLA_HEADS:, :] += dwv

    const = lambda shape: pl.BlockSpec(shape, lambda i: (0, 0))
    col = lambda height: pl.BlockSpec((height, tm), lambda i: (0, i))
    shapes = [(MLA_Q_LORA, 1), (MLA_KV_LORA, 1), (LANES, 1), (LANES, 1),
              (1024, MLA_Q_LORA), (1536, MLA_KV_LORA)]
    return pl.pallas_call(
        body, name="mla_prep_bwd", grid=(s // tm,),
        in_specs=_prep_in_specs(tm) + [col(1024), col(1024), col(512)],
        out_specs=[pl.BlockSpec((tm, 512), lambda i: (i, 0))] + [const(sh) for sh in shapes],
        out_shape=[jax.ShapeDtypeStruct((s, 512), BF16)]
        + [jax.ShapeDtypeStruct(sh, F32) for sh in shapes],
        compiler_params=_params(("arbitrary",)),
    )(proj, *norms, *weights, *rope, dq, dk, dv)


MLA_SCALE = MLA_QK ** -0.5


LOG2E = 1.4426950408889634
LN2 = 0.6931471805599453
Q_PRESCALE = MLA_SCALE * LOG2E
HEAD_GROUPS = ((0, 1),)


def _mla_attn_fwd(q2, k, vt):
    s = q2.shape[0]
    t = min(512, s)
    tk = min(128, s)
    nq = s // t
    r = t // tk

    def body(q_ref, k_ref, vt_ref, o_ref, lse_ref, acc_ref):
        i = pl.program_id(1)
        row = lax.broadcasted_iota(jnp.int32, (tk, t), 0)
        col = lax.broadcasted_iota(jnp.int32, (tk, t), 1)
        qh = [q_ref[:, LANES * hh:LANES * (hh + 1)] for hh in range(2)]
        acc_ref[...] = jnp.zeros_like(acc_ref)

        def scores(j, heads, diag=None):
            r0 = pl.multiple_of(j * tk, tk)
            out = []
            for hh in heads:
                kc = k_ref[pl.ds(r0, tk), LANES * hh:LANES * (hh + 1)]
                sc = lax.dot_general(kc, qh[hh], (((1,), (1,)), ((), ())),
                                     preferred_element_type=F32)
                out.append(sc if diag is None else jnp.where(row + diag * tk <= col, sc, NEG_INF))
            return tuple(out)

        for heads in HEAD_GROUPS:
            stats = tuple(jnp.full((1, t), NEG_INF, F32) for _ in heads)

            def consume(j, scs, stats, heads=heads):
                r0 = pl.multiple_of(j * tk, tk)
                out, ps, alphas = [], [], []
                for n, hh in enumerate(heads):
                    m_new = jnp.maximum(stats[n], jnp.max(scs[n], axis=0, keepdims=True))
                    ps.append(jnp.exp2(scs[n] - m_new).astype(BF16))
                    alphas.append(jnp.exp2(stats[n] - m_new))
                    out.append(m_new)
                for n, hh in enumerate(heads):
                    vc = vt_ref[V_AUG * hh:V_AUG * (hh + 1), pl.ds(r0, tk)]
                    acc_ref[hh] = alphas[n] * acc_ref[hh] + jnp.dot(vc, ps[n],
                                                                    preferred_element_type=F32)
                return tuple(out)

            def group(j0, stats, diag, heads=heads):
                scs = [scores(j0 + d, heads, d if diag else None) for d in range(r)]
                for d in range(r):
                    stats = consume(j0 + d, scs[d], stats)
                return stats

            stats = group(r * i, stats, True)
            stats = lax.fori_loop(0, i, lambda j, st: group(r * j, st, False), stats)
            for n, hh in enumerate(heads):
                l = acc_ref[hh, MLA_V:MLA_V + 1, :]
                o_ref[:, MLA_V * hh:MLA_V * (hh + 1)] = jnp.transpose(acc_ref[hh, 0:MLA_V, :] / l)
                lse_ref[0, hh:hh + 1, :] = stats[n] + jnp.log2(l)

    return pl.pallas_call(
        body, name="mla_attn_fwd", grid=(MLA_HEADS // 2, nq),
        in_specs=[pl.BlockSpec((t, 256), lambda p, i: (i, p)),
                  pl.BlockSpec((s, 256), lambda p, i: (0, p)),
                  pl.BlockSpec((2 * V_AUG, s), lambda p, i: (p, 0))],
        out_specs=[pl.BlockSpec((t, 128), lambda p, i: (i, p)),
                   pl.BlockSpec((1, 2, t), lambda p, i: (p, 0, i))],
        out_shape=[jax.ShapeDtypeStruct((s, 512), F32),
                   jax.ShapeDtypeStruct((MLA_HEADS // 2, 2, s), F32)],
        scratch_shapes=[pltpu.VMEM((2, V_AUG, t), F32)],
        compiler_params=_params(("parallel", "arbitrary")),
    )(q2, k, vt)


def _mla_attn_bwd(q2, q2t, k, kt, v, do, dot, lse_rows, delta_rows):
    s = q2.shape[0]
    t = min(512, s)
    nq = s // t

    def body(q_ref, qt_ref, k_ref, kt_ref, v_ref, do_ref, dot_ref, lse_ref, dl_ref,
             dq_ref, dk_ref, dv_ref):
        j = pl.program_id(1)

        @pl.when(j == 0)
        def _():
            dq_ref[...] = jnp.zeros_like(dq_ref)

        dk_ref[...] = jnp.zeros_like(dk_ref)
        dv_ref[...] = jnp.zeros_like(dv_ref)
        row = lax.broadcasted_iota(jnp.int32, (t, t), 0)
        col = lax.broadcasted_iota(jnp.int32, (t, t), 1)
        causal_t = row <= col
        kh = [k_ref[:, LANES * hh:LANES * (hh + 1)] for hh in range(2)]
        kth = [kt_ref[LANES * hh:LANES * (hh + 1), :] for hh in range(2)]
        vh = [v_ref[:, MLA_V * hh:MLA_V * (hh + 1)] for hh in range(2)]
        nt = (((1,), (1,)), ((), ()))

        def step(i, masked):
            r0 = pl.multiple_of(i * t, t)
            sd = []
            for hh in range(2):
                qh = q_ref[pl.ds(r0, t), LANES * hh:LANES * (hh + 1)]
                doh = do_ref[pl.ds(r0, t), MLA_V * hh:MLA_V * (hh + 1)]
                sc_t = lax.dot_general(kh[hh], qh, nt, preferred_element_type=F32)
                sd.append(jnp.where(causal_t, sc_t, NEG_INF) if masked else sc_t)
                sd.append(lax.dot_general(vh[hh], doh, nt, preferred_element_type=F32))
            for hh in range(2):
                lse = lse_ref[0, hh:hh + 1, pl.ds(r0, t)]
                dl = dl_ref[0, hh:hh + 1, pl.ds(r0, t)]
                p_t = jnp.exp2(sd[2 * hh] - lse)
                g_t = (p_t * (sd[2 * hh + 1] - dl)).astype(BF16)
                qth = qt_ref[LANES * hh:LANES * (hh + 1), pl.ds(r0, t)]
                doth = dot_ref[MLA_V * hh:MLA_V * (hh + 1), pl.ds(r0, t)]
                dv_ref[MLA_V * hh:MLA_V * (hh + 1), :] += lax.dot_general(
                    doth, p_t.astype(BF16), nt, preferred_element_type=F32)
                dk_ref[LANES * hh:LANES * (hh + 1), :] += lax.dot_general(
                    qth, g_t, nt, preferred_element_type=F32)
                dq_ref[LANES * hh:LANES * (hh + 1), pl.ds(r0, t)] += jnp.dot(
                    kth[hh], g_t, preferred_element_type=F32)

        step(j, True)

        def trip(i, carry):
            step(i, False)
            return carry

        lax.fori_loop(j + 1, nq, trip, 0)
        dk_ref[...] = dk_ref[...] * LN2

        @pl.when(j == nq - 1)
        def _():
            dq_ref[...] = dq_ref[...] * MLA_SCALE

    return pl.pallas_call(
        body, name="mla_attn_bwd", grid=(MLA_HEADS // 2, nq),
        in_specs=[pl.BlockSpec((s, 256), lambda p, j: (0, p)),
                  pl.BlockSpec((256, s), lambda p, j: (p, 0)),
                  pl.BlockSpec((t, 256), lambda p, j: (j, p)),
                  pl.BlockSpec((256, t), lambda p, j: (p, j)),
                  pl.BlockSpec((t, 128), lambda p, j: (j, p)),
                  pl.BlockSpec((s, 128), lambda p, j: (0, p)),
                  pl.BlockSpec((128, s), lambda p, j: (p, 0)),
                  pl.BlockSpec((1, 2, s), lambda p, j: (p, 0, 0)),
                  pl.BlockSpec((1, 2, s), lambda p, j: (p, 0, 0))],
        out_specs=[pl.BlockSpec((256, s), lambda p, j: (p, 0)),
                   pl.BlockSpec((256, t), lambda p, j: (p, j)),
                   pl.BlockSpec((128, t), lambda p, j: (p, j))],
        out_shape=[jax.ShapeDtypeStruct((1024, s), F32), jax.ShapeDtypeStruct((1024, s), F32),
                   jax.ShapeDtypeStruct((512, s), F32)],
        compiler_params=_params(("parallel", "arbitrary")),
    )(q2, q2t, k, kt, v, do, dot, lse_rows, delta_rows)


SWA_SCALE = SWA_DIM ** -0.5
SWA_COLS = SWA_GROUP * BLOCK
SWA_LOG2 = SWA_SCALE * LOG2E


def _swa_tables():
    k = np.arange(2 * BLOCK)[:, None]
    col = np.arange(SWA_COLS)[None, :]
    dist = BLOCK + (col % BLOCK) - k
    valid = (dist >= 0) & (dist < BLOCK)
    out = np.zeros((2, SWA_KV, 2 * BLOCK, SWA_COLS), np.float32)
    for first in range(2):
        ok = valid & ((k >= BLOCK) | (first == 0))
        for j in range(SWA_KV):
            slope = 2.0 ** -(SWA_GROUP * j + col // BLOCK + 1)
            out[first, j] = np.where(ok, -slope * dist * LOG2E, NEG_INF)
    return jnp.asarray(out)


def _swa_tile_inputs(sq_ref, skv_ref, halo_ref, qn_ref, kn_ref, sk_ref, add_ref, first):
    tokens = sq_ref.shape[0]
    kv_all = jnp.concatenate([halo_ref[...], skv_ref[...]], axis=0)
    kv_t = jnp.transpose(kv_all)
    sq_t = jnp.transpose(sq_ref[...])
    k_raw = [kv_t[SWA_DIM * j:SWA_DIM * (j + 1)] for j in range(SWA_KV)]
    v_t = [kv_t[128 + SWA_DIM * j:128 + SWA_DIM * (j + 1)] for j in range(SWA_KV)]
    v_nat = [kv_all[:, 128 + SWA_DIM * j:128 + SWA_DIM * (j + 1)] for j in range(SWA_KV)]
    q_raw = [sq_t[SWA_DIM * h:SWA_DIM * (h + 1)] for h in range(SWA_HEADS)]
    qn_b = jnp.broadcast_to(qn_ref[...], (SWA_DIM, tokens))
    kn_b = jnp.broadcast_to(kn_ref[...], (SWA_DIM, tokens + BLOCK))
    lane_grp = lax.broadcasted_iota(jnp.int32, (1, SWA_COLS), 1) // BLOCK
    sinks, adds = [], []
    for j in range(SWA_KV):
        row = jnp.zeros((1, SWA_COLS), F32)
        for g in range(SWA_GROUP):
            h = SWA_GROUP * j + g
            row = jnp.where(lane_grp == g, sk_ref[:, h:h + 1] * LOG2E, row)
        sinks.append(row)
        adds.append((jnp.where(first, add_ref[1, j], add_ref[0, j]), add_ref[0, j]))
    return k_raw, v_t, v_nat, q_raw, qn_b, kn_b, sinks, adds


def _swa_probs(kb, qs_t, add, sink):
    s2 = jnp.dot(kb, qs_t, preferred_element_type=F32) * SWA_LOG2 + add
    m = jnp.maximum(jnp.max(s2, axis=0, keepdims=True), sink)
    e = jnp.exp2(s2 - m)
    es = jnp.exp2(sink - m)
    inv = 1.0 / (jnp.sum(e, axis=0, keepdims=True) + es)
    return e, inv, es


def _swa_queries(qn_t, j, b):
    return jnp.concatenate([qn_t[SWA_GROUP * j + g][:, BLOCK * b:BLOCK * (b + 1)]
                            for g in range(SWA_GROUP)], axis=1)


def _swa_fwd(proj, qn, kn, sinks, tables):
    s = proj.shape[0]
    ts = min(1024, s)
    nb = ts // BLOCK

    def body(sq_ref, skv_ref, halo_ref, qn_ref, kn_ref, sk_ref, add_ref, o_ref, ot_ref):
        first = pl.program_id(0) == 0
        k_raw, v_t, _, q_raw, qn_b, kn_b, sink_rows, adds = _swa_tile_inputs(
            sq_ref, skv_ref, halo_ref, qn_ref, kn_ref, sk_ref, add_ref, first)
        kn_nat = [jnp.transpose(_rms0(k, kn_b)).astype(BF16) for k in k_raw]
        v_t = [v.astype(BF16) for v in v_t]
        qn_t = [_rms0(q, qn_b).astype(BF16) for q in q_raw]
        for b in range(nb):
            band = slice(BLOCK * b, BLOCK * (b + 2))
            for j in range(SWA_KV):
                e, inv, _ = _swa_probs(kn_nat[j][band], _swa_queries(qn_t, j, b),
                                       adds[j][0 if b == 0 else 1], sink_rows[j])
                o_t = jnp.dot(v_t[j][:, band], (e * inv).astype(BF16),
                              preferred_element_type=F32)
                for g in range(SWA_GROUP):
                    h = SWA_GROUP * j + g
                    ot_ref[SWA_DIM * h:SWA_DIM * (h + 1), BLOCK * b:BLOCK * (b + 1)] = (
                        o_t[:, BLOCK * g:BLOCK * (g + 1)])
        o_ref[...] = jnp.transpose(ot_ref[...])

    const = lambda shape: pl.BlockSpec(shape, lambda i: (0,) * len(shape))
    return pl.pallas_call(
        body, name="swa_fwd", grid=(s // ts,),
        in_specs=[pl.BlockSpec((ts, 512), lambda i: (i, OFF_SQ // 512)),
                  pl.BlockSpec((ts, 256), lambda i: (i, OFF_SKV // 256)),
                  pl.BlockSpec((BLOCK, 256), lambda i: (jnp.maximum(i * nb - 1, 0), OFF_SKV // 256)),
                  const((SWA_DIM, 1)), const((SWA_DIM, 1)), const((1, SWA_HEADS)),
                  const(tables.shape)],
        out_specs=pl.BlockSpec((ts, 512), lambda i: (i, 0)),
        out_shape=jax.ShapeDtypeStruct((s, 512), F32),
        scratch_shapes=[pltpu.VMEM((512, ts), F32)],
        compiler_params=_params(("parallel",)),
    )(proj, proj, proj, qn, kn, sinks, tables)


def _swa_bwd(proj, qn, kn, sinks, tables, do):
    s = proj.shape[0]
    ts = min(1024, s)
    nb = ts // BLOCK
    nt = s // ts

    def body(sq_ref, skv_ref, halo_ref, qn_ref, kn_ref, sk_ref, add_ref, do_ref,
             dsq_ref, dskv_ref, dqn_ref, dkn_ref, dsk_ref, carry_ref, dqt_ref, dkvt_ref):
        step = pl.program_id(0)
        first = step == nt - 1

        @pl.when(step == 0)
        def _():
            carry_ref[...] = jnp.zeros_like(carry_ref)
            dqn_ref[...] = jnp.zeros_like(dqn_ref)
            dkn_ref[...] = jnp.zeros_like(dkn_ref)
            dsk_ref[...] = jnp.zeros_like(dsk_ref)

        k_raw, v_t, v_nat, q_raw, qn_b, kn_b, sink_rows, adds = _swa_tile_inputs(
            sq_ref, skv_ref, halo_ref, qn_ref, kn_ref, sk_ref, add_ref, first)
        kn_f = [_rms0(k, kn_b) for k in k_raw]
        kn_t = [k.astype(BF16) for k in kn_f]
        kn_nat = [jnp.transpose(k).astype(BF16) for k in kn_f]
        v_nat = [v.astype(BF16) for v in v_nat]
        qn_t = [_rms0(q, qn_b).astype(BF16) for q in q_raw]
        do_t = jnp.transpose(do_ref[...].astype(F32)).astype(BF16)

        dkvt_ref[...] = jnp.zeros_like(dkvt_ref)
        dsink = [jnp.zeros((1, SWA_COLS), F32) for _ in range(SWA_KV)]
        nt_dims = (((1,), (1,)), ((), ()))
        for b in range(nb):
            rows = slice(BLOCK * b, BLOCK * (b + 1))
            band = slice(BLOCK * b, BLOCK * (b + 2))
            for j in range(SWA_KV):
                heads = [SWA_GROUP * j + g for g in range(SWA_GROUP)]
                qs_t = _swa_queries(qn_t, j, b)
                dos_t = jnp.concatenate([do_t[SWA_DIM * h:SWA_DIM * (h + 1), rows] for h in heads],
                                        axis=1)
                e, inv, es = _swa_probs(kn_nat[j][band], qs_t, adds[j][0 if b == 0 else 1],
                                        sink_rows[j])
                p = e * inv
                dp = jnp.dot(v_nat[j][band], dos_t, preferred_element_type=F32)
                dsum = jnp.sum(p * dp, axis=0, keepdims=True)
                dsink[j] = dsink[j] - es * inv * dsum
                g_t = (p * (dp - dsum) * SWA_SCALE).astype(BF16)
                dv_t = lax.dot_general(dos_t, p.astype(BF16), nt_dims,
                                       preferred_element_type=F32)
                dk_t = lax.dot_general(qs_t, g_t, nt_dims, preferred_element_type=F32)
                dq_t = jnp.dot(kn_t[j][:, band], g_t, preferred_element_type=F32)
                dkvt_ref[SWA_DIM * j:SWA_DIM * (j + 1), band] += dk_t
                dkvt_ref[128 + SWA_DIM * j:128 + SWA_DIM * (j + 1), band] += dv_t
                for g, h in enumerate(heads):
                    dqt_ref[SWA_DIM * h:SWA_DIM * (h + 1), rows] = dq_t[:, BLOCK * g:BLOCK * (g + 1)]

        dqn = jnp.zeros((SWA_DIM, 1), F32)
        for h in range(SWA_HEADS):
            _, vjp = jax.vjp(_rms0, q_raw[h], qn_ref[...])
            dq, dg = vjp(dqt_ref[SWA_DIM * h:SWA_DIM * (h + 1), :])
            dqt_ref[SWA_DIM * h:SWA_DIM * (h + 1), :] = dq
            dqn = dqn + dg
        dqn_ref[...] += dqn
        dsq_ref[...] = jnp.transpose(dqt_ref[...]).astype(BF16)
        dkn = jnp.zeros((SWA_DIM, 1), F32)
        lane_grp = lax.broadcasted_iota(jnp.int32, (1, SWA_COLS), 1) // BLOCK
        for j in range(SWA_KV):
            _, vjp = jax.vjp(_rms0, k_raw[j], kn_ref[...])
            dk, dg = vjp(dkvt_ref[SWA_DIM * j:SWA_DIM * (j + 1), :])
            dkvt_ref[SWA_DIM * j:SWA_DIM * (j + 1), :] = dk
            dkn = dkn + dg
            for g in range(SWA_GROUP):
                h = SWA_GROUP * j + g
                dsk_ref[:, h:h + 1] += jnp.sum(jnp.where(lane_grp == g, dsink[j], 0.0), axis=1,
                                               keepdims=True)
        dkn_ref[...] += dkn
        dkv = jnp.transpose(dkvt_ref[...])
        dskv_ref[0:ts - BLOCK, :] = dkv[BLOCK:ts].astype(BF16)
        dskv_ref[ts - BLOCK:ts, :] = (dkv[ts:ts + BLOCK] + carry_ref[...]).astype(BF16)
        carry_ref[...] = dkv[0:BLOCK]

    const = lambda shape: pl.BlockSpec(shape, lambda st: (0,) * len(shape))
    return pl.pallas_call(
        body, name="swa_bwd", grid=(nt,),
        in_specs=[pl.BlockSpec((ts, 512), lambda st: (nt - 1 - st, OFF_SQ // 512)),
                  pl.BlockSpec((ts, 256), lambda st: (nt - 1 - st, OFF_SKV // 256)),
                  pl.BlockSpec((BLOCK, 256),
                               lambda st: (jnp.maximum((nt - 1 - st) * nb - 1, 0), OFF_SKV // 256)),
                  const((SWA_DIM, 1)), const((SWA_DIM, 1)), const((1, SWA_HEADS)),
                  const(tables.shape),
                  pl.BlockSpec((ts, 512), lambda st: (nt - 1 - st, 0))],
        out_specs=[pl.BlockSpec((ts, 512), lambda st: (nt - 1 - st, 0)),
                   pl.BlockSpec((ts, 256), lambda st: (nt - 1 - st, 0)),
                   const((SWA_DIM, 1)), const((SWA_DIM, 1)), const((1, SWA_HEADS))],
        out_shape=[jax.ShapeDtypeStruct((s, 512), BF16), jax.ShapeDtypeStruct((s, 256), BF16),
                   jax.ShapeDtypeStruct((SWA_DIM, 1), F32), jax.ShapeDtypeStruct((SWA_DIM, 1), F32),
                   jax.ShapeDtypeStruct((1, SWA_HEADS), F32)],
        scratch_shapes=[pltpu.VMEM((BLOCK, 256), F32), pltpu.VMEM((512, ts), F32),
                        pltpu.VMEM((256, ts + BLOCK), F32)],
        compiler_params=_params(("arbitrary",)),
    )(proj, proj, proj, qn, kn, sinks, tables, do)


HALO = 8


def _shift_down(u, halo, k):
    tm = u.shape[0]
    rid = lax.broadcasted_iota(jnp.int32, u.shape, 0)
    out = pltpu.roll(u, k, 0)
    for r in range(k):
        out = jnp.where(rid == r, halo[HALO - k + r:HALO - k + r + 1, :], out)
    return out


def _shift_up(u, halo, k):
    tm = u.shape[0]
    rid = lax.broadcasted_iota(jnp.int32, u.shape, 0)
    out = pltpu.roll(u, tm - k, 0)
    for r in range(k):
        out = jnp.where(rid == tm - k + r, halo[r:r + 1, :], out)
    return out


def _conv_fwd_vals(conv_ref, convp_ref, cw_ref, is_first):
    c_h, c_b, c_c = conv_ref[:, 0:512], conv_ref[:, 512:1024], conv_ref[:, 1024:1536]
    u = c_c * c_h
    up = jnp.where(is_first, 0.0, convp_ref[:, 1024:1536] * convp_ref[:, 0:512])
    u1 = _shift_down(u, up, 1)
    u2 = _shift_down(u, up, 2)
    yc = cw_ref[0:1, :] * u2 + cw_ref[1:2, :] * u1 + cw_ref[2:3, :] * u
    return c_h, c_b, c_c, u, u1, u2, yc


def _out_fwd(proj, o_mla, o_swa, x, w_out, cw):
    s = proj.shape[0]
    tm = min(512, s)

    def body(conv_ref, convp_ref, gates_ref, om_ref, os_ref, x_ref, w_ref, cw_ref, y_ref, zt_ref,
             z_ref):
        i = pl.program_id(0)
        _, c_b, _, _, _, _, yc = _conv_fwd_vals(conv_ref, convp_ref, cw_ref, i == 0)
        mix = (om_ref[...], c_b * yc, os_ref[...])
        for n in range(3):
            g = gates_ref[:, GROUP * n:GROUP * (n + 1)]
            z = mix[n] * (g * _sigmoid(g))
            z_ref[:, GROUP * n:GROUP * (n + 1)] = z.astype(BF16)
            zt_ref[GROUP * n:GROUP * (n + 1), :] = jnp.transpose(z).astype(BF16)
        y_ref[...] = x_ref[...] + jnp.dot(z_ref[...], w_ref[...], preferred_element_type=F32)

    row = lambda width: pl.BlockSpec((tm, width), lambda i: (i, 0))
    return pl.pallas_call(
        body, name="out_fwd", grid=(s // tm,),
        in_specs=[pl.BlockSpec((tm, 1536), lambda i: (i, 0)),
                  pl.BlockSpec((HALO, 1536), lambda i: (jnp.maximum(i * (tm // HALO) - 1, 0), 0)),
                  pl.BlockSpec((tm, 1536), lambda i: (i, 1)),
                  row(512), row(512), row(D_MODEL),
                  pl.BlockSpec((D_MIX, D_MODEL), lambda i: (0, 0)),
                  pl.BlockSpec((HALO, 512), lambda i: (0, 0))],
        out_specs=[row(D_MODEL), pl.BlockSpec((D_MIX, tm), lambda i: (0, i))],
        out_shape=[jax.ShapeDtypeStruct((s, D_MODEL), F32), jax.ShapeDtypeStruct((D_MIX, s), BF16)],
        scratch_shapes=[pltpu.VMEM((tm, D_MIX), BF16)],
        compiler_params=_params(("parallel",)),
    )(proj, proj, proj, o_mla, o_swa, x, w_out, cw)


def _out_bwd(dy, proj, o_mla, o_swa, w_out, cw):
    s = proj.shape[0]
    tm = min(512, s)
    nt = s // tm
    hb = tm // HALO

    def body(dy_ref, dyn_ref, conv_ref, convp_ref, convn_ref, gates_ref, gatesn_ref, om_ref, os_ref,
             w_ref, cw_ref,
             dconv_ref, dgates_ref, dom_ref, domt_ref, delta_ref, dos_ref, dcw_ref):
        i = pl.program_id(0)
        dz = _mm_nt(dy_ref[...], w_ref[...])

        def gate(n):
            g = gates_ref[:, GROUP * n:GROUP * (n + 1)]
            sg = _sigmoid(g)
            return g * sg, sg * (1.0 + g * (1.0 - sg))

        for n, o_ref, do_ref in ((0, om_ref, dom_ref), (2, os_ref, dos_ref)):
            silu, dsilu = gate(n)
            dzn = dz[:, GROUP * n:GROUP * (n + 1)]
            o = o_ref[...]
            do = dzn * silu
            do_ref[...] = do.astype(do_ref.dtype)
            dgates_ref[:, GROUP * n:GROUP * (n + 1)] = (dzn * o * dsilu).astype(BF16)
            if n == 0:
                domt_ref[...] = jnp.transpose(do).astype(BF16)
                t = do * o
                for h in range(MLA_HEADS):
                    delta_ref[:, h:h + 1] = jnp.sum(t[:, MLA_V * h:MLA_V * (h + 1)], axis=-1,
                                                    keepdims=True)

        c_h, c_b, c_c, u, u1, u2, yc = _conv_fwd_vals(conv_ref, convp_ref, cw_ref, i == 0)
        silu, dsilu = gate(1)
        dzc = dz[:, GROUP:2 * GROUP]
        dgates_ref[:, GROUP:2 * GROUP] = (dzc * (c_b * yc) * dsilu).astype(BF16)
        dycr = dzc * silu
        dyc = dycr * c_b
        gn = gatesn_ref[:, GROUP:2 * GROUP]
        dzc_n = _mm_nt(dyn_ref[...], w_ref[GROUP:2 * GROUP, :])
        dyc_n = jnp.where(i == nt - 1, 0.0, dzc_n * (gn * _sigmoid(gn)) * convn_ref[:, 512:1024])
        d1 = _shift_up(dyc, dyc_n, 1)
        d2 = _shift_up(dyc, dyc_n, 2)
        du = cw_ref[2:3, :] * dyc + cw_ref[1:2, :] * d1 + cw_ref[0:1, :] * d2
        dconv_ref[:, 0:512] = (du * c_c).astype(BF16)
        dconv_ref[:, 512:1024] = (dycr * yc).astype(BF16)
        dconv_ref[:, 1024:1536] = (du * c_h).astype(BF16)

        @pl.when(i == 0)
        def _():
            dcw_ref[...] = jnp.zeros_like(dcw_ref)

        for k, uk in enumerate((u2, u1, u)):
            dcw_ref[k:k + 1, :] += jnp.sum(dyc * uk, axis=0, keepdims=True)

    row = lambda width: pl.BlockSpec((tm, width), lambda i: (i, 0))
    prev = lambda i: jnp.maximum(i * hb - 1, 0)
    nxt = lambda i: jnp.minimum((i + 1) * hb, s // HALO - 1)
    return pl.pallas_call(
        body, name="out_bwd", grid=(nt,),
        in_specs=[row(D_MODEL),
                  pl.BlockSpec((HALO, D_MODEL), lambda i: (nxt(i), 0)),
                  pl.BlockSpec((tm, 1536), lambda i: (i, 0)),
                  pl.BlockSpec((HALO, 1536), lambda i: (prev(i), 0)),
                  pl.BlockSpec((HALO, 1536), lambda i: (nxt(i), 0)),
                  pl.BlockSpec((tm, 1536), lambda i: (i, 1)),
                  pl.BlockSpec((HALO, 1536), lambda i: (nxt(i), 1)),
                  row(512), row(512),
                  pl.BlockSpec((D_MIX, D_MODEL), lambda i: (0, 0)),
                  pl.BlockSpec((HALO, 512), lambda i: (0, 0))],
        out_specs=[row(1536), row(1536), row(512), pl.BlockSpec((512, tm), lambda i: (0, i)),
                   row(MLA_HEADS), row(512), pl.BlockSpec((HALO, 512), lambda i: (0, 0))],
        out_shape=[jax.ShapeDtypeStruct((s, 1536), BF16), jax.ShapeDtypeStruct((s, 1536), BF16),
                   jax.ShapeDtypeStruct((s, 512), BF16), jax.ShapeDtypeStruct((512, s), BF16),
                   jax.ShapeDtypeStruct((s, MLA_HEADS), F32),
                   jax.ShapeDtypeStruct((s, 512), BF16), jax.ShapeDtypeStruct((HALO, 512), F32)],
        compiler_params=_params(("arbitrary",)),
    )(dy, dy, proj, proj, proj, proj, proj, o_mla, o_swa, w_out, cw)


def _loss_head(y, target):
    s, d = y.shape
    tm = min(512, s)
    nt = s // tm

    def body(y_ref, t_ref, dy_ref, loss_ref):
        i = pl.program_id(0)
        err = y_ref[...] - t_ref[...]
        dy_ref[...] = err * (1.0 / d)

        @pl.when(i == 0)
        def _():
            loss_ref[...] = jnp.zeros_like(loss_ref)

        sq = jnp.sum((err * err).reshape(tm // 8, 8, d), axis=0)
        part = sq[:, 0:LANES]
        for c in range(1, d // LANES):
            part = part + sq[:, LANES * c:LANES * (c + 1)]
        loss_ref[...] += part

        @pl.when(i == nt - 1)
        def _():
            loss_ref[...] = jnp.full(loss_ref.shape, (0.5 / d) * jnp.sum(loss_ref[...]), F32)

    return pl.pallas_call(
        body, name="loss_head", grid=(nt,),
        in_specs=[pl.BlockSpec((tm, d), lambda i: (i, 0)), pl.BlockSpec((tm, d), lambda i: (i, 0))],
        out_specs=[pl.BlockSpec((tm, d), lambda i: (i, 0)), pl.BlockSpec((8, LANES), lambda i: (0, 0))],
        out_shape=[jax.ShapeDtypeStruct((s, d), F32), jax.ShapeDtypeStruct((8, LANES), F32)],
        compiler_params=_params(("arbitrary",)),
    )(y, target)


def _adam_update(g, w, m, v):
    c1 = 1.0 - ADAM_B1
    c2 = 1.0 - ADAM_B2
    bc1 = 1.0 - ADAM_B1 ** ADAM_STEP
    bc2 = 1.0 - ADAM_B2 ** ADAM_STEP
    m_new = ADAM_B1 * m + c1 * g
    v_new = ADAM_B2 * v + c2 * (g * g)
    delta = -ADAM_LR * ((m_new / bc1) / (jnp.sqrt(v_new / bc2) + ADAM_EPS) + ADAM_WD * w)
    return delta, m_new, v_new


def _adamw(g, w, m, v):
    rows = g.shape[0]
    tr = min(256, rows)

    def body(g_ref, w_ref, m_ref, v_ref, d_ref, mo_ref, vo_ref):
        d_ref[...], mo_ref[...], vo_ref[...] = _adam_update(g_ref[...], w_ref[...], m_ref[...],
                                                            v_ref[...])

    spec = pl.BlockSpec((tr, g.shape[1]), lambda i: (i, 0))
    return pl.pallas_call(
        body, name="adamw", grid=(rows // tr,),
        in_specs=[spec] * 4, out_specs=[spec] * 3,
        out_shape=[jax.ShapeDtypeStruct(g.shape, F32)] * 3,
        compiler_params=_params(("parallel",)),
    )(g, w, m, v)


def _adamw_small(gs, ws, ms, vs):
    n = len(gs)

    def body(*refs):
        ins, outs = refs[:4 * n], refs[4 * n:]
        for a in range(n):
            res = _adam_update(*(ins[kind * n + a][...] for kind in range(4)))
            for kind in range(3):
                outs[kind * n + a][...] = res[kind]

    vmem = pl.BlockSpec(memory_space=pltpu.VMEM)
    out = pl.pallas_call(
        body, name="adamw_small",
        in_specs=[vmem] * (4 * n), out_specs=[vmem] * (3 * n),
        out_shape=[jax.ShapeDtypeStruct(g.shape, F32) for _ in range(3) for g in gs],
    )(*gs, *ws, *ms, *vs)
    return out[:n], out[n:2 * n], out[2 * n:]


HBM_SPEC = pl.BlockSpec(memory_space=pltpu.HBM)


def _place():
    x, y, c = lax.axis_index("x"), lax.axis_index("y"), lax.axis_index("c")
    chips = [(1 - x, y), (x, 1 - y), (1 - x, 1 - y)]
    return x, y, c, chips


def _all_gather(shards):
    na = len(shards)
    halves = [sh.shape[0] // 2 for sh in shards]

    def body(*refs):
        w_refs, a_refs = refs[:na], refs[na:2 * na]
        send_sems, recv_sems = refs[2 * na:]
        x, y, c, chips = _place()
        k = 2 * x + y
        sib = (x, y, 1 - c)

        def slab(a, kk, hc):
            return a_refs[a].at[kk, pl.ds(hc * halves[a], halves[a]), :]

        def copy(a, n, src, dst, to):
            return pltpu.make_async_remote_copy(
                src_ref=src, dst_ref=dst, send_sem=send_sems.at[6 * a + n],
                recv_sem=recv_sems.at[6 * a + n], device_id=to, device_id_type=MESH)

        first = [copy(a, n, w_refs[a].at[pl.ds(c * halves[a], halves[a]), :], slab(a, k, c),
                      (cx, cy, c))
                 for n, (cx, cy) in enumerate(chips) for a in range(na)]
        for cp in first:
            cp.start()
        passed = []
        for n, (cx, cy) in enumerate(chips):
            kk = 2 * cx + cy
            for a in range(na):
                copy(a, n, slab(a, kk, c), slab(a, kk, c), (cx, cy, c)).wait_recv()
                fwd = copy(a, 3 + n, slab(a, kk, c), slab(a, kk, c), sib)
                fwd.start()
                passed.append(fwd)
        for n, (cx, cy) in enumerate(chips):
            kk = 2 * cx + cy
            for a in range(na):
                copy(a, 3 + n, slab(a, kk, 1 - c), slab(a, kk, 1 - c), sib).wait_recv()
        for cp in first + passed:
            cp.wait_send()

    return pl.pallas_call(
        body, name="weights_all_gather",
        in_specs=[HBM_SPEC] * na, out_specs=[HBM_SPEC] * na,
        out_shape=[jax.ShapeDtypeStruct((N_CHIPS,) + sh.shape, sh.dtype) for sh in shards],
        scratch_shapes=[pltpu.SemaphoreType.DMA((6 * na,)), pltpu.SemaphoreType.DMA((6 * na,))],
    )(*shards)


def _fill_own_slab(buf, src, k_idx):
    n, rows, cols = buf.shape
    tr = _row_tile(rows)
    slabs = src.ndim == 3

    def body(k_ref, src_ref, buf_ref, out_ref):
        out_ref[0] = src_ref[0] if slabs else src_ref[...]

    if slabs:
        src_spec = pl.BlockSpec((1, tr, cols), lambda t, k_ref: (k_ref[0], t, 0))
    else:
        src_spec = pl.BlockSpec((tr, cols), lambda t, k_ref: (t, 0))
    return pl.pallas_call(
        body, name="fill_own_slab",
        grid_spec=pltpu.PrefetchScalarGridSpec(
            num_scalar_prefetch=1, grid=(rows // tr,),
            in_specs=[src_spec, pl.BlockSpec(memory_space=pl.ANY)],
            out_specs=pl.BlockSpec((1, tr, cols), lambda t, k_ref: (k_ref[0], t, 0))),
        out_shape=jax.ShapeDtypeStruct(buf.shape, buf.dtype),
        input_output_aliases={2: 0},
        compiler_params=_params(("parallel",)),
    )(k_idx, src, buf)


def _swap_halves_to_sibling(gs):
    na = len(gs)

    def body(*refs):
        g_refs, r_refs = refs[:na], refs[na:2 * na]
        send_sems, recv_sems = refs[2 * na:]
        x, y, c, _ = _place()
        cps = []
        for a in range(na):
            half = g_refs[a].shape[1] // 2
            cps.append(pltpu.make_async_remote_copy(
                src_ref=g_refs[a].at[:, pl.ds((1 - c) * half, half), :], dst_ref=r_refs[a],
                send_sem=send_sems.at[a], recv_sem=recv_sems.at[a], device_id=(x, y, 1 - c),
                device_id_type=MESH))
        for cp in cps:
            cp.start()
        for cp in cps:
            cp.wait()

    return pl.pallas_call(
        body, name="grads_to_sibling",
        in_specs=[HBM_SPEC] * na, out_specs=[HBM_SPEC] * na,
        out_shape=[jax.ShapeDtypeStruct((g.shape[0], g.shape[1] // 2, g.shape[2]), g.dtype)
                   for g in gs],
        scratch_shapes=[pltpu.SemaphoreType.DMA((na,)), pltpu.SemaphoreType.DMA((na,))],
    )(*gs)


def _row_tile(rows):
    return 256 if rows % 256 == 0 else 128


def _add_sibling(g, r, c_idx, out_dtype):
    n, rows, cols = g.shape
    half = rows // 2
    tr = _row_tile(half)
    nb = half // tr

    def body(c_ref, g_ref, r_ref, p_ref):
        p_ref[...] = (g_ref[...] + r_ref[...]).astype(out_dtype)

    return pl.pallas_call(
        body, name="grads_add_sibling",
        grid_spec=pltpu.PrefetchScalarGridSpec(
            num_scalar_prefetch=1, grid=(n, nb),
            in_specs=[pl.BlockSpec((1, tr, cols), lambda j, t, c_ref: (j, c_ref[0] * nb + t, 0)),
                      pl.BlockSpec((1, tr, cols), lambda j, t, c_ref: (j, t, 0))],
            out_specs=pl.BlockSpec((1, tr, cols), lambda j, t, c_ref: (j, t, 0))),
        out_shape=jax.ShapeDtypeStruct((n, half, cols), out_dtype),
        compiler_params=_params(("parallel", "parallel")),
    )(c_idx, g, r)


def _scatter_to_chips(ps):
    na = len(ps)

    def body(*refs):
        p_refs, q_refs = refs[:na], refs[na:2 * na]
        send_sems, recv_sems = refs[2 * na:]
        x, y, c, chips = _place()
        k = 2 * x + y
        sends = []
        for i, (cx, cy) in enumerate(chips):
            for a in range(na):
                cp = pltpu.make_async_remote_copy(
                    src_ref=p_refs[a].at[2 * cx + cy], dst_ref=q_refs[a].at[k],
                    send_sem=send_sems.at[3 * a + i], recv_sem=recv_sems.at[3 * a + i],
                    device_id=(cx, cy, c), device_id_type=MESH)
                cp.start()
                sends.append(cp)
        for i, (cx, cy) in enumerate(chips):
            kk = 2 * cx + cy
            for a in range(na):
                pltpu.make_async_remote_copy(
                    src_ref=p_refs[a].at[kk], dst_ref=q_refs[a].at[kk],
                    send_sem=send_sems.at[3 * a + i], recv_sem=recv_sems.at[3 * a + i],
                    device_id=(cx, cy, c), device_id_type=MESH).wait_recv()
        for cp in sends:
            cp.wait_send()

    return pl.pallas_call(
        body, name="grads_scatter_to_chips",
        in_specs=[HBM_SPEC] * na, out_specs=[HBM_SPEC] * na,
        out_shape=[jax.ShapeDtypeStruct(p.shape, p.dtype) for p in ps],
        scratch_shapes=[pltpu.SemaphoreType.DMA((3 * na,)), pltpu.SemaphoreType.DMA((3 * na,))],
    )(*ps)


def _sum_chips(q, c_idx):
    n, half, cols = q.shape
    tr = _row_tile(half)
    nb = half // tr

    def body(c_ref, q_ref, o_ref):
        parts = [q_ref[kk].astype(F32) for kk in range(n)]
        o_ref[...] = ((parts[0] + parts[1]) + parts[2]) + parts[3]

    return pl.pallas_call(
        body, name="grads_sum_chips",
        grid_spec=pltpu.PrefetchScalarGridSpec(
            num_scalar_prefetch=1, grid=(nb,),
            in_specs=[pl.BlockSpec((n, tr, cols), lambda t, c_ref: (0, t, 0))],
            out_specs=pl.BlockSpec((tr, cols), lambda t, c_ref: (c_ref[0] * nb + t, 0))),
        out_shape=jax.ShapeDtypeStruct((2 * half, cols), F32),
        compiler_params=_params(("parallel",)),
    )(c_idx, q)


def _join_halves(fulls):
    na = len(fulls)

    def body(*refs):
        o_refs = refs[na:2 * na]
        send_sems, recv_sems = refs[2 * na:]
        x, y, c, _ = _place()
        sends = []
        for a in range(na):
            half = o_refs[a].shape[0] // 2
            rows = o_refs[a].at[pl.ds(c * half, half), :]
            sends.append(pltpu.make_async_remote_copy(
                src_ref=rows, dst_ref=rows, send_sem=send_sems.at[a], recv_sem=recv_sems.at[a],
                device_id=(x, y, 1 - c), device_id_type=MESH))
        for cp in sends:
            cp.start()
        for a in range(na):
            half = o_refs[a].shape[0] // 2
            other = o_refs[a].at[pl.ds((1 - c) * half, half), :]
            pltpu.make_async_remote_copy(
                src_ref=other, dst_ref=other, send_sem=send_sems.at[a], recv_sem=recv_sems.at[a],
                device_id=(x, y, 1 - c), device_id_type=MESH).wait_recv()
        for cp in sends:
            cp.wait_send()

    return pl.pallas_call(
        body, name="grads_join_halves",
        in_specs=[HBM_SPEC] * na, out_specs=[HBM_SPEC] * na,
        out_shape=[jax.ShapeDtypeStruct(f.shape, f.dtype) for f in fulls],
        input_output_aliases={a: a for a in range(na)},
        scratch_shapes=[pltpu.SemaphoreType.DMA((na,)), pltpu.SemaphoreType.DMA((na,))],
    )(*fulls)


def _part_rows(shape):
    size = 1
    for d in shape:
        size *= d
    rows = -(-size // PACK_COLS)
    return size, -(-rows // PACK_ROW_ALIGN) * PACK_ROW_ALIGN


def _pack_rows(arrays, dtype, total_rows):
    parts, used = [], 0
    for a in arrays:
        size, rows = _part_rows(a.shape)
        flat = a.reshape(-1).astype(dtype)
        parts.append(jnp.pad(flat, (0, rows * PACK_COLS - size)).reshape(rows, PACK_COLS))
        used += rows
    parts.append(jnp.zeros((total_rows - used, PACK_COLS), dtype))
    return jnp.concatenate(parts, axis=0)


def _unpack_rows(buf, shapes):
    lead = buf.shape[:-2]
    out, off = [], 0
    for sh in shapes:
        size, rows = _part_rows(sh)
        part = buf[..., off:off + rows, :].reshape(lead + (-1,))[..., :size]
        out.append(part.reshape(lead + tuple(sh)))
        off += rows
    return out


NEW_ORDER = ((928, 1440), (1440, 1952), (1952, 2464), (416, 928), (2464, 2976), (3744, 4256),
             (2976, 3488), (0, 256), (256, 384), (4256, 4320), (384, 416), (4256, 4288),
             (3488, 3616), (3616, 3744))
OLD_ORDER = ((3584, 3840), (3840, 3968), (4032, 4064), (1536, 2048), (0, 512), (512, 1024),
             (1024, 1536), (2048, 2560), (3072, 3584), (4096, 4224), (4224, 4352), (2560, 3072))


def _cols(sources, ranges):
    parts = []
    for a, b in ranges:
        off = 0
        for src in sources:
            width = src.shape[-1]
            lo, hi = max(a, off), min(b, off + width)
            if lo < hi:
                parts.append(src[..., lo - off:hi - off])
            off += width
    return jnp.concatenate(parts, axis=-1)


def _sub_ranges(ranges, a, b):
    out, off = [], 0
    for lo, hi in ranges:
        width = hi - lo
        s0, s1 = max(a, off), min(b, off + width)
        if s0 < s1:
            out.append((lo + s0 - off, lo + s1 - off))
        off += width
    return out


def _rope_tables(s):
    half = MLA_ROPE // 2
    inv_freq = jnp.power(jnp.float32(ROPE_THETA), -jnp.arange(half, dtype=F32) / half)
    ang = inv_freq[:, None] * jnp.arange(s, dtype=F32)[None, :]
    cos, sin = jnp.cos(ang), jnp.sin(ang)
    z = lambda n: jnp.zeros((n, s), F32)
    c = jnp.concatenate([jnp.ones((MLA_NOPE, s), F32), cos, cos, z(32)], axis=0)
    s1 = jnp.concatenate([z(MLA_NOPE), -sin, z(16), z(32)], axis=0)
    s2 = jnp.concatenate([z(MLA_NOPE), z(16), sin, z(32)], axis=0)
    return c, s1, s2


def _pad_lanes(a, n):
    return jnp.pad(a, ((0, 0), (0, n - a.shape[1])))


SHARDED = ("w_in", "w_out", "mla_w_qb", "mla_w_kvb", "conv_w")
REPLICATED = ("norm_g", "mla_q_a_norm", "mla_kv_a_norm", "mla_q_norm", "mla_k_norm",
              "swa_q_norm", "swa_k_norm", "swa_sinks")
WEIGHT_ORDER = ("norm_g", "w_in", "mla_q_a_norm", "mla_w_qb", "mla_kv_a_norm", "mla_w_kvb",
                "mla_q_norm", "mla_k_norm", "conv_w", "swa_q_norm", "swa_k_norm", "swa_sinks", "w_out")
SHARD_AXIS = {"w_in": 2, "w_out": 1, "mla_w_qb": 2, "mla_w_kvb": 2, "conv_w": 2}


def kernel(x, norm_g, w_in, mla_q_a_norm, mla_w_qb, mla_kv_a_norm, mla_w_kvb, mla_q_norm, mla_k_norm, conv_w, swa_q_norm, swa_k_norm, swa_sinks, w_out, loss_target, m_norm_g, m_w_in, m_mla_q_a_norm, m_mla_w_qb, m_mla_kv_a_norm, m_mla_w_kvb, m_mla_q_norm, m_mla_k_norm, m_conv_w, m_swa_q_norm, m_swa_k_norm, m_swa_sinks, m_w_out, v_norm_g, v_w_in, v_mla_q_a_norm, v_mla_w_qb, v_mla_kv_a_norm, v_mla_w_kvb, v_mla_q_norm, v_mla_k_norm, v_conv_w, v_swa_q_norm, v_swa_k_norm, v_swa_sinks, v_w_out):
    weights = dict(norm_g=norm_g, w_in=w_in, mla_q_a_norm=mla_q_a_norm, mla_w_qb=mla_w_qb,
                   mla_kv_a_norm=mla_kv_a_norm, mla_w_kvb=mla_w_kvb, mla_q_norm=mla_q_norm,
                   mla_k_norm=mla_k_norm, conv_w=conv_w, swa_q_norm=swa_q_norm,
                   swa_k_norm=swa_k_norm, swa_sinks=swa_sinks, w_out=w_out)
    mom_m = dict(norm_g=m_norm_g, w_in=m_w_in, mla_q_a_norm=m_mla_q_a_norm, mla_w_qb=m_mla_w_qb,
                 mla_kv_a_norm=m_mla_kv_a_norm, mla_w_kvb=m_mla_w_kvb, mla_q_norm=m_mla_q_norm,
                 mla_k_norm=m_mla_k_norm, conv_w=m_conv_w, swa_q_norm=m_swa_q_norm,
                 swa_k_norm=m_swa_k_norm, swa_sinks=m_swa_sinks, w_out=m_w_out)
    mom_v = dict(norm_g=v_norm_g, w_in=v_w_in, mla_q_a_norm=v_mla_q_a_norm, mla_w_qb=v_mla_w_qb,
                 mla_kv_a_norm=v_mla_kv_a_norm, mla_w_kvb=v_mla_w_kvb, mla_q_norm=v_mla_q_norm,
                 mla_k_norm=v_mla_k_norm, conv_w=v_conv_w, swa_q_norm=v_swa_q_norm,
                 swa_k_norm=v_swa_k_norm, swa_sinks=v_swa_sinks, w_out=v_w_out)
    xs = x[0]
    target = loss_target[0]
    s = xs.shape[0]
    c_idx = lax.axis_index("c").astype(jnp.int32).reshape(1)
    k_idx = (2 * lax.axis_index("x") + lax.axis_index("y")).astype(jnp.int32).reshape(1)

    conv_bits = lax.bitcast_convert_type(conv_w, BF16)
    small_list = [w_out, mla_w_qb, mla_w_kvb, conv_bits]
    w_in_rows = DEPTH * D_MODEL
    own = [w_in.astype(BF16).reshape(w_in_rows, w_in.shape[2]), _pack_rows(small_list, BF16, PACK_ROWS)]
    gathered_in, gathered_rest = [_fill_own_slab(buf, src, k_idx)
                                  for buf, src in zip(_all_gather(own), own)]
    parts = _unpack_rows(gathered_rest, [a.shape for a in small_list])
    join = lambda p, axis: jnp.concatenate([p[k] for k in range(N_CHIPS)], axis=axis)
    w_in_slabs = gathered_in.reshape(N_CHIPS, DEPTH, D_MODEL, w_in.shape[2])
    w_in_zeros = jnp.zeros((D_MODEL, 64), BF16)
    w_out_full = join(parts[0], 1)
    w_qb_full = join(parts[1], 2)
    w_kvb_full = join(parts[2], 2)
    conv_full = lax.bitcast_convert_type(join(parts[3], 2), F32)

    rope = _rope_tables(s)
    swa_tables = _swa_tables()
    layers = []
    for l in range(DEPTH):
        wq = jnp.pad(w_qb_full[l].reshape(MLA_Q_LORA, MLA_HEADS, MLA_QK),
                     ((0, 0), (0, 0), (0, LANES - MLA_QK))).reshape(MLA_Q_LORA, MLA_HEADS * LANES)
        kv = w_kvb_full[l].reshape(MLA_KV_LORA, MLA_HEADS, MLA_NOPE + MLA_V)
        wk = jnp.pad(kv[:, :, :MLA_NOPE], ((0, 0), (0, 0), (0, LANES - MLA_NOPE)))
        wkv = jnp.concatenate([wk.reshape(MLA_KV_LORA, MLA_HEADS * LANES),
                               kv[:, :, MLA_NOPE:].reshape(MLA_KV_LORA, MLA_HEADS * MLA_V)], axis=1)
        layers.append(dict(
            w_in=_cols([w_in_slabs[k, l] for k in range(N_CHIPS)] + [w_in_zeros], NEW_ORDER),
            w_out=w_out_full[l], wq=wq, wkv=wkv,
            cw=jnp.pad(conv_full[l], ((0, HALO - 3), (0, 0))),
            g=norm_g[l][None],
            mla_norms=(mla_q_a_norm[l][:, None], mla_kv_a_norm[l][:, None],
                       _pad_lanes(mla_q_norm[l][None], LANES).T, _pad_lanes(mla_k_norm[l][None], LANES).T),
            mla_weights=(wq, wkv, wq.T, wkv.T),
            sqn=swa_q_norm[l][:, None], skn=swa_k_norm[l][:, None], sinks=swa_sinks[l][None]))

    saved = []
    h_in = xs
    for l in range(DEPTH):
        p = layers[l]
        proj, hb = _in_proj_fwd(h_in, p["g"], p["w_in"])
        q, k, v, qt, kt, vt = _mla_prep_fwd(proj, p["mla_norms"], p["mla_weights"], rope)
        o_mla, lse = _mla_attn_fwd(q, k, vt)
        o_swa = _swa_fwd(proj, p["sqn"], p["skn"], p["sinks"], swa_tables)
        y, z = _out_fwd(proj, o_mla, o_swa, h_in, p["w_out"], p["cw"])
        saved.append(dict(x=h_in, proj=proj, hb=hb, q=q, k=k, v=v, qt=qt, kt=kt, o_mla=o_mla, lse=lse,
                          o_swa=o_swa, z=z))
        h_in = y

    dy, loss_acc = _loss_head(h_in, target)
    loss = lax.psum(loss_acc[0, 0], ("x", "y", "c"))

    grads = {n: [None] * DEPTH for n in WEIGHT_ORDER}
    for l in reversed(range(DEPTH)):
        p, a = layers[l], saved[l]
        dconv, dgates, do_mla, do_mla_t, delta, do_swa, dcw = _out_bwd(dy, a["proj"], a["o_mla"], a["o_swa"],
                                                             p["w_out"], p["cw"])
        grads["w_out"][l] = _weight_grads(a["z"], [dy], "dw_out")[0]
        grads["conv_w"][l] = dcw[0:3]
        delta_rows = jnp.transpose(delta, (1, 0)).reshape(MLA_HEADS // 2, 2, s)
        dq, dk, dv = _mla_attn_bwd(a["q"], a["qt"], a["k"], a["kt"], a["v"], do_mla, do_mla_t,
                                   a["lse"], delta_rows)
        dmla, dqan, dkvan, dqn, dkn, dwq_t, dwkv_t = _mla_prep_bwd(
            a["proj"], p["mla_norms"], p["mla_weights"], rope, dq, dk, dv)
        dwq, dwkv = dwq_t.T, dwkv_t.T
        dsq, dskv, dsqn, dskn, dsinks = _swa_bwd(a["proj"], p["sqn"], p["skn"], p["sinks"], swa_tables, do_swa)
        pieces = [dconv, dgates, dsq, dmla, dskv]
        dx, dg = _in_proj_bwd(pieces, a["x"], p["g"], p["w_in"], dy)
        grads["w_in"][l] = _weight_grads(a["hb"], pieces, "dw_in")
        grads["norm_g"][l] = dg[0]
        grads["mla_q_a_norm"][l] = dqan[:, 0]
        grads["mla_kv_a_norm"][l] = dkvan[:, 0]
        grads["mla_q_norm"][l] = dqn[:MLA_QK, 0]
        grads["mla_k_norm"][l] = dkn[:MLA_QK, 0]
        grads["mla_w_qb"][l] = dwq.reshape(MLA_Q_LORA, MLA_HEADS, LANES)[:, :, :MLA_QK].reshape(
            MLA_Q_LORA, MLA_HEADS * MLA_QK)
        dwk = dwkv[:, :MLA_HEADS * LANES].reshape(MLA_KV_LORA, MLA_HEADS, LANES)[:, :, :MLA_NOPE]
        dwv = dwkv[:, MLA_HEADS * LANES:].reshape(MLA_KV_LORA, MLA_HEADS, MLA_V)
        grads["mla_w_kvb"][l] = jnp.concatenate([dwk, dwv], axis=2).reshape(
            MLA_KV_LORA, MLA_HEADS * (MLA_NOPE + MLA_V))
        grads["swa_q_norm"][l] = dsqn[:, 0]
        grads["swa_k_norm"][l] = dskn[:, 0]
        grads["swa_sinks"][l] = dsinks[0]
        dy = dx
    grad_x = dy[None]
    full_grads = {n: jnp.stack(grads[n]) for n in WEIGHT_ORDER if n != "w_in"}

    rest = tuple(n for n in SHARDED if n not in ("w_in", "w_out"))
    rep_shapes = [weights[n].shape for n in REPLICATED]
    rep_grads = jnp.concatenate([full_grads[n].reshape(-1) for n in REPLICATED])

    def chunk(g, n, k):
        width = g.shape[SHARD_AXIS[n]] // N_CHIPS
        return lax.slice_in_dim(g, k * width, (k + 1) * width, axis=SHARD_AXIS[n])

    shard_cols = w_in.shape[2]
    g_in = jnp.stack([
        jnp.concatenate([_cols(grads["w_in"][l], _sub_ranges(OLD_ORDER, k * shard_cols,
                                                             (k + 1) * shard_cols))
                         for l in range(DEPTH)], axis=0)
        for k in range(N_CHIPS)])
    g_out = jnp.stack([chunk(full_grads["w_out"], "w_out", k).reshape(-1, D_MODEL)
                       for k in range(N_CHIPS)])
    g_small = jnp.stack([_pack_rows([chunk(full_grads[n], n, k) for n in rest] + [rep_grads],
                                    F32, SMALL_ROWS) for k in range(N_CHIPS)])
    gs = (g_in, g_out, g_small)
    from_sibling = _swap_halves_to_sibling(gs)
    partial = [_add_sibling(g, r, c_idx, dt) for g, r, dt in zip(gs, from_sibling, (BF16, BF16, F32))]
    by_chip = [_fill_own_slab(q, p, k_idx) for q, p in zip(_scatter_to_chips(partial), partial)]
    g_in_mine, g_out_mine, g_small_mine = _join_halves([_sum_chips(q, c_idx) for q in by_chip])

    vals = _unpack_rows(g_small_mine, [weights[n].shape for n in rest] + [(rep_grads.shape[0],)])
    grad = dict(zip(rest, vals[:-1]))
    grad["w_in"] = g_in_mine.reshape(w_in.shape)
    grad["w_out"] = g_out_mine.reshape(w_out.shape)
    off = 0
    for n, sh in zip(REPLICATED, rep_shapes):
        grad[n] = vals[-1][off:off + sh[0] * sh[1]].reshape(sh)
        off += sh[0] * sh[1]
    results = {}
    for n in ("w_in", "w_out"):
        view = lambda a, n=n: a.reshape(-1, weights[n].shape[-1])
        res = _adamw(view(grad[n]), view(weights[n]), view(mom_m[n]), view(mom_v[n]))
        results[n] = [r.reshape(weights[n].shape) for r in res]
    small = tuple(n for n in WEIGHT_ORDER if n not in results)
    res = _adamw_small(*([d[n] for n in small] for d in (grad, weights, mom_m, mom_v)))
    for a, n in enumerate(small):
        results[n] = [res[kind][a] for kind in range(3)]
    unpacked = [grad] + [{n: results[n][kind] for n in WEIGHT_ORDER} for kind in range(3)]
    outs = [loss, grad_x]
    for group in unpacked:
        outs += [group[n] for n in WEIGHT_ORDER]
    return tuple(outs)
```

```python
import jax
import numpy as np
import jax.numpy as jnp
from jax import lax
from jax.experimental import pallas as pl
from jax.experimental.pallas import tpu as pltpu

F32 = jnp.float32
BF16 = jnp.bfloat16

D_MODEL = 1024
DEPTH = 2
GROUP = 512
D_MIX = 3 * GROUP
BLOCK = 128
RMS_EPS = 1e-6
NEG_INF = -1e30
MLA_HEADS = 8
MLA_QK = 96
MLA_NOPE = 64
MLA_ROPE = 32
MLA_V = 64
V_AUG = 80
MLA_Q_LORA = 256
MLA_KV_LORA = 128
ROPE_THETA = 10000.0
SWA_HEADS = 8
SWA_KV = 2
SWA_GROUP = 4
SWA_DIM = 64
N_CHIPS = 4

NC = 4352
OFF_SQ, OFF_MLA, OFF_SKV = 3072, 3584, 4096

VMEM_LIMIT = 56 * 1024 * 1024
LANES = 128
PACK_COLS = 1024
PACK_ROW_ALIGN = 16
PACK_ROWS = 1024
SMALL_ROWS = 256

ADAM_LR = 0.001
ADAM_B1 = 0.9
ADAM_B2 = 0.999
ADAM_EPS = 1e-08
ADAM_WD = 0.01
ADAM_STEP = 10

MESH = pl.DeviceIdType.MESH


def _params(sem, vmem=VMEM_LIMIT):
    return pltpu.CompilerParams(dimension_semantics=sem, vmem_limit_bytes=vmem)


def _dot(a, b, dims):
    return lax.dot_general(a.astype(BF16), b.astype(BF16), (dims, ((), ())),
                           preferred_element_type=F32)


def _mm(a, b):
    return _dot(a, b, ((1,), (0,)))


def _mm_nt(a, b):
    return _dot(a, b, ((1,), (1,)))


def _rms(x, g, n=None):
    n = x.shape[-1] if n is None else n
    ms = jnp.sum(x * x, axis=-1, keepdims=True) * (1.0 / n)
    return x * lax.rsqrt(ms + RMS_EPS) * g


def _sigmoid(x):
    return 1.0 / (1.0 + jnp.exp(-x))


def _in_proj_fwd(x, g, w):
    s = x.shape[0]
    tm = min(512, s)

    def body(x_ref, g_ref, w_ref, proj_ref, hbt_ref):
        h = _rms(x_ref[...], g_ref[...])
        hbt_ref[...] = jnp.transpose(h).astype(BF16)
        proj_ref[...] = jnp.dot(h.astype(BF16), w_ref[...], preferred_element_type=F32)

    return pl.pallas_call(
        body, name="in_proj_fwd", grid=(s // tm,),
        in_specs=[pl.BlockSpec((tm, D_MODEL), lambda i: (i, 0)),
                  pl.BlockSpec((1, D_MODEL), lambda i: (0, 0)),
                  pl.BlockSpec((D_MODEL, NC), lambda i: (0, 0))],
        out_specs=[pl.BlockSpec((tm, NC), lambda i: (i, 0)),
                   pl.BlockSpec((D_MODEL, tm), lambda i: (0, i))],
        out_shape=[jax.ShapeDtypeStruct((s, NC), F32), jax.ShapeDtypeStruct((D_MODEL, s), BF16)],
        compiler_params=_params(("parallel",)),
    )(x, g, w)


def _in_proj_bwd(pieces, x, g, w, dres):
    s = x.shape[0]
    tm = min(512, s)
    n_p = len(pieces)

    def body(*refs):
        p_refs = refs[:n_p]
        x_ref, g_ref, w_ref, dres_ref, dx_ref, dg_ref = refs[n_p:]
        dh = None
        off = 0
        for r in p_refs:
            width = r.shape[1]
            t = _mm_nt(r[...], w_ref[:, off:off + width])
            dh = t if dh is None else dh + t
            off += width
        _, vjp = jax.vjp(_rms, x_ref[...], g_ref[...])
        dx, dg = vjp(dh)
        dx_ref[...] = dx + dres_ref[...]

        @pl.when(pl.program_id(0) == 0)
        def _():
            dg_ref[...] = jnp.zeros_like(dg_ref)

        dg_ref[...] += dg

    in_specs = [pl.BlockSpec((tm, p.shape[1]), lambda i: (i, 0)) for p in pieces]
    in_specs += [pl.BlockSpec((tm, D_MODEL), lambda i: (i, 0)),
                 pl.BlockSpec((1, D_MODEL), lambda i: (0, 0)),
                 pl.BlockSpec((D_MODEL, NC), lambda i: (0, 0)),
                 pl.BlockSpec((tm, D_MODEL), lambda i: (i, 0))]
    return pl.pallas_call(
        body, name="in_proj_bwd", grid=(s // tm,),
        in_specs=in_specs,
        out_specs=[pl.BlockSpec((tm, D_MODEL), lambda i: (i, 0)),
                   pl.BlockSpec((1, D_MODEL), lambda i: (0, 0))],
        out_shape=[jax.ShapeDtypeStruct((s, D_MODEL), F32), jax.ShapeDtypeStruct((1, D_MODEL), F32)],
        compiler_params=_params(("arbitrary",)),
    )(*pieces, x, g, w, dres)


def _weight_grads(at, bs, name):
    m, s = at.shape
    nb = len(bs)
    tk = min(512, s)

    def body(a_ref, *refs):
        b_refs, o_refs = refs[:nb], refs[nb:]

        @pl.when(pl.program_id(0) == 0)
        def _():
            for o_ref in o_refs:
                o_ref[...] = jnp.zeros_like(o_ref)

        a = a_ref[...]
        for b_ref, o_ref in zip(b_refs, o_refs):
            o_ref[...] += _mm(a, b_ref[...])

    return pl.pallas_call(
        body, name=name, grid=(s // tk,),
        in_specs=[pl.BlockSpec((m, tk), lambda k: (0, k))]
        + [pl.BlockSpec((tk, b.shape[1]), lambda k: (k, 0)) for b in bs],
        out_specs=[pl.BlockSpec((m, b.shape[1]), lambda k: (0, 0)) for b in bs],
        out_shape=[jax.ShapeDtypeStruct((m, b.shape[1]), F32) for b in bs],
        compiler_params=_params(("arbitrary",)),
    )(at, *bs)


def _rms0(x, g, n=None):
    n = x.shape[0] if n is None else n
    ms = jnp.sum(x * x, axis=0, keepdims=True) * (1.0 / n)
    return x * lax.rsqrt(ms + RMS_EPS) * g


@jax.custom_vjp
def _rope0(t, c, s1, s2):
    return t * c + pltpu.roll(t, LANES - 16, 0) * s1 + pltpu.roll(t, 16, 0) * s2


def _rope0_fwd(t, c, s1, s2):
    return _rope0(t, c, s1, s2), (c, s1, s2)


def _rope0_bwd(res, g):
    c, s1, s2 = res
    dt = g * c + pltpu.roll(g * s1, 16, 0) + pltpu.roll(g * s2, LANES - 16, 0)
    return dt, jnp.zeros_like(c), jnp.zeros_like(s1), jnp.zeros_like(s2)


_rope0.defvjp(_rope0_fwd, _rope0_bwd)


@jax.custom_vjp
def _mmw(w, wt, x):
    return _mm(w, x)


def _mmw_fwd(w, wt, x):
    return _mm(w, x), (wt, x)


def _mmw_bwd(res, g):
    wt, x = res
    return _mm_nt(g, x), jnp.zeros_like(wt), _mm(wt, g)


_mmw.defvjp(_mmw_fwd, _mmw_bwd)


def _prep_fn(q_lat, kv_lat, kr, qan, kvan, qn, kn, wq, wk, wv, wqt, wkt, wvt, c, s1, s2, mm):
    tokens = q_lat.shape[1]
    rq = _rms0(q_lat, qan)
    rkv = _rms0(kv_lat, kvan)
    qn_b = jnp.broadcast_to(qn, (LANES, tokens))
    kn_b = jnp.broadcast_to(kn, (LANES, tokens))
    qs, ks = [], []
    for h in range(MLA_HEADS):
        qs.append(_rope0(_rms0(mm(wq[h], wqt[h], rq), qn_b, MLA_QK), c, s1, s2))
        ks.append(_rope0(_rms0(mm(wk[h], wkt[h], rkv) + kr, kn_b, MLA_QK), c, s1, s2))
    return tuple(qs), tuple(ks), mm(wv, wvt, rkv)


def _prep_weights(wq_ref, wkv_ref, wqt_ref, wkvt_ref):
    heads = range(MLA_HEADS)
    wq = tuple(wqt_ref[LANES * h:LANES * (h + 1), :].astype(F32) for h in heads)
    wk = tuple(wkvt_ref[LANES * h:LANES * (h + 1), :].astype(F32) for h in heads)
    wv = wkvt_ref[LANES * MLA_HEADS:, :].astype(F32)
    wqt = tuple(wq_ref[:, LANES * h:LANES * (h + 1)].astype(F32) for h in heads)
    wkt = tuple(wkv_ref[:, LANES * h:LANES * (h + 1)].astype(F32) for h in heads)
    wvt = wkv_ref[:, LANES * MLA_HEADS:].astype(F32)
    return wq, wk, wv, wqt, wkt, wvt


def _prep_in_specs(tm):
    const = lambda shape: pl.BlockSpec(shape, lambda i: (0, 0))
    col = lambda height: pl.BlockSpec((height, tm), lambda i: (0, i))
    return [pl.BlockSpec((tm, 512), lambda i: (i, OFF_MLA // 512)),
            const((MLA_Q_LORA, 1)), const((MLA_KV_LORA, 1)), const((LANES, 1)), const((LANES, 1)),
            const((MLA_Q_LORA, 1024)), const((MLA_KV_LORA, 1536)),
            const((1024, MLA_Q_LORA)), const((1536, MLA_KV_LORA)),
            col(LANES), col(LANES), col(LANES)]


def _prep_operands(blk_ref, refs):
    qan_ref, kvan_ref, qn_ref, kn_ref, wq_ref, wkv_ref, wqt_ref, wkvt_ref, c_ref, s1_ref, s2_ref = refs
    blk_t = jnp.transpose(blk_ref[...])
    diff = (blk_t[0:256], blk_t[256:384], blk_t[384:512],
            qan_ref[...], kvan_ref[...], qn_ref[...], kn_ref[...])
    weights = _prep_weights(wq_ref, wkv_ref, wqt_ref, wkvt_ref)
    return diff, weights, (c_ref[...], s1_ref[...], s2_ref[...])


def _mla_prep_fwd(proj, norms, weights, rope):
    s = proj.shape[0]
    tm = min(512, s)

    def body(blk_ref, *refs):
        ins, (q_ref, k_ref, v_ref, qt_ref, kt_ref, vt_ref) = refs[:11], refs[11:]
        diff, (wq, wk, wv, wqt, wkt, wvt), tables = _prep_operands(blk_ref, ins)
        qs, ks, v = _prep_fn(*diff, wq, wk, wv, wqt, wkt, wvt, *tables,
                             lambda w, wt, x: _mm(w, x))
        for h in range(MLA_HEADS):
            q2 = qs[h] * Q_PRESCALE
            qt_ref[LANES * h:LANES * (h + 1), :] = q2.astype(BF16)
            kt_ref[LANES * h:LANES * (h + 1), :] = ks[h].astype(BF16)
            q_ref[:, LANES * h:LANES * (h + 1)] = jnp.transpose(q2).astype(BF16)
            k_ref[:, LANES * h:LANES * (h + 1)] = jnp.transpose(ks[h]).astype(BF16)
        ones_row = (lax.broadcasted_iota(jnp.int32, (V_AUG - MLA_V, v.shape[1]), 0) == 0).astype(BF16)
        for h in range(MLA_HEADS):
            vt_ref[V_AUG * h:V_AUG * h + MLA_V, :] = v[MLA_V * h:MLA_V * (h + 1)].astype(BF16)
            vt_ref[V_AUG * h + MLA_V:V_AUG * (h + 1), :] = ones_row
        v_ref[...] = jnp.transpose(v).astype(BF16)

    row = lambda width: pl.BlockSpec((tm, width), lambda i: (i, 0))
    col = lambda height: pl.BlockSpec((height, tm), lambda i: (0, i))
    return pl.pallas_call(
        body, name="mla_prep_fwd", grid=(s // tm,),
        in_specs=_prep_in_specs(tm),
        out_specs=[row(1024), row(1024), row(512), col(1024), col(1024), col(MLA_HEADS * V_AUG)],
        out_shape=[jax.ShapeDtypeStruct((s, 1024), BF16), jax.ShapeDtypeStruct((s, 1024), BF16),
                   jax.ShapeDtypeStruct((s, 512), BF16), jax.ShapeDtypeStruct((1024, s), BF16),
                   jax.ShapeDtypeStruct((1024, s), BF16),
                   jax.ShapeDtypeStruct((MLA_HEADS * V_AUG, s), BF16)],
        compiler_params=_params(("parallel",)),
    )(proj, *norms, *weights, *rope)


def _mla_prep_bwd(proj, norms, weights, rope, dq, dk, dv):
    s = proj.shape[0]
    tm = min(512, s)

    def body(blk_ref, *refs):
        ins, (dq_ref, dk_ref, dv_ref) = refs[:11], refs[11:14]
        dblk_ref, dqan_ref, dkvan_ref, dqn_ref, dkn_ref, dwq_ref, dwkv_ref = refs[14:]
        diff, (wq, wk, wv, wqt, wkt, wvt), tables = _prep_operands(blk_ref, ins)

        def fn(q_lat, kv_lat, kr, qan, kvan, qn, kn, wq_, wk_, wv_):
            return _prep_fn(q_lat, kv_lat, kr, qan, kvan, qn, kn, wq_, wk_, wv_, wqt, wkt, wvt,
                            *tables, _mmw)

        _, vjp = jax.vjp(fn, *diff, wq, wk, wv)
        heads = range(MLA_HEADS)
        cts = (tuple(dq_ref[LANES * h:LANES * (h + 1), :] for h in heads),
               tuple(dk_ref[LANES * h:LANES * (h + 1), :] for h in heads), dv_ref[...])
        dq_lat, dkv_lat, dkr, dqan, dkvan, dqn, dkn, dwq_h, dwk_h, dwv = vjp(cts)
        dblk_ref[...] = jnp.transpose(
            jnp.concatenate([dq_lat, dkv_lat, dkr], axis=0)).astype(BF16)

        @pl.when(pl.program_id(0) == 0)
        def _():
            for r in (dqan_ref, dkvan_ref, dqn_ref, dkn_ref, dwq_ref, dwkv_ref):
                r[...] = jnp.zeros_like(r)

        dqan_ref[...] += dqan
        dkvan_ref[...] += dkvan
        dqn_ref[...] += dqn
        dkn_ref[...] += dkn
        for h in heads:
            dwq_ref[LANES * h:LANES * (h + 1), :] += dwq_h[h]
            dwkv_ref[LANES * h:LANES * (h + 1), :] += dwk_h[h]
        dwkv_ref[LANES * MLA_HEADS:, :] += dwv

    const = lambda shape: pl.BlockSpec(shape, lambda i: (0, 0))
    col = lambda height: pl.BlockSpec((height, tm), lambda i: (0, i))
    shapes = [(MLA_Q_LORA, 1), (MLA_KV_LORA, 1), (LANES, 1), (LANES, 1),
              (1024, MLA_Q_LORA), (1536, MLA_KV_LORA)]
    return pl.pallas_call(
        body, name="mla_prep_bwd", grid=(s // tm,),
        in_specs=_prep_in_specs(tm) + [col(1024), col(1024), col(512)],
        out_specs=[pl.BlockSpec((tm, 512), lambda i: (i, 0))] + [const(sh) for sh in shapes],
        out_shape=[jax.ShapeDtypeStruct((s, 512), BF16)]
        + [jax.ShapeDtypeStruct(sh, F32) for sh in shapes],
        compiler_params=_params(("arbitrary",)),
    )(proj, *norms, *weights, *rope, dq, dk, dv)


MLA_SCALE = MLA_QK ** -0.5


LOG2E = 1.4426950408889634
LN2 = 0.6931471805599453
Q_PRESCALE = MLA_SCALE * LOG2E
HEAD_GROUPS = ((0, 1),)


def _mla_attn_fwd(q2, k, vt):
    s = q2.shape[0]
    t = min(512, s)
    tk = min(128, s)
    nq = s // t
    r = t // tk

    def body(q_ref, k_ref, vt_ref, o_ref, lse_ref, acc_ref):
        i = pl.program_id(1)
        row = lax.broadcasted_iota(jnp.int32, (tk, t), 0)
        col = lax.broadcasted_iota(jnp.int32, (tk, t), 1)
        qh = [q_ref[:, LANES * hh:LANES * (hh + 1)] for hh in range(2)]
        acc_ref[...] = jnp.zeros_like(acc_ref)

        def scores(j, heads, diag=None):
            r0 = pl.multiple_of(j * tk, tk)
            q_lo = 0 if diag is None else diag * tk
            out = []
            for hh in heads:
                kc = k_ref[pl.ds(r0, tk), LANES * hh:LANES * (hh + 1)]
                sc = lax.dot_general(kc, qh[hh][q_lo:], (((1,), (1,)), ((), ())),
                                     preferred_element_type=F32)
                out.append(sc if diag is None
                           else jnp.where(row[:, :t - q_lo] <= col[:, :t - q_lo], sc, NEG_INF))
            return tuple(out)

        for heads in HEAD_GROUPS:
            stats = tuple(jnp.full((1, t), NEG_INF, F32) for _ in heads)

            def consume(j, scs, stats, q_lo=0, heads=heads):
                r0 = pl.multiple_of(j * tk, tk)
                out, ps, alphas = [], [], []
                for n, hh in enumerate(heads):
                    m_old = stats[n][:, q_lo:]
                    m_new = jnp.maximum(m_old, jnp.max(scs[n], axis=0, keepdims=True))
                    ps.append(jnp.exp2(scs[n] - m_new).astype(BF16))
                    alphas.append(jnp.exp2(m_old - m_new))
                    out.append(m_new if q_lo == 0
                               else jnp.concatenate([stats[n][:, :q_lo], m_new], axis=1))
                for n, hh in enumerate(heads):
                    vc = vt_ref[V_AUG * hh:V_AUG * (hh + 1), pl.ds(r0, tk)]
                    acc_ref[hh, :, q_lo:] = alphas[n] * acc_ref[hh, :, q_lo:] + jnp.dot(
                        vc, ps[n], preferred_element_type=F32)
                return tuple(out)

            def group(j0, stats, diag, heads=heads):
                scs = [scores(j0 + d, heads, d if diag else None) for d in range(r)]
                for d in range(r):
                    stats = consume(j0 + d, scs[d], stats, d * tk if diag else 0)
                return stats

            stats = group(r * i, stats, True)
            stats = lax.fori_loop(0, i, lambda j, st: group(r * j, st, False), stats)
            for n, hh in enumerate(heads):
                l = acc_ref[hh, MLA_V:MLA_V + 1, :]
                o_ref[:, MLA_V * hh:MLA_V * (hh + 1)] = jnp.transpose(acc_ref[hh, 0:MLA_V, :] / l)
                lse_ref[0, hh:hh + 1, :] = stats[n] + jnp.log2(l)

    return pl.pallas_call(
        body, name="mla_attn_fwd", grid=(MLA_HEADS // 2, nq),
        in_specs=[pl.BlockSpec((t, 256), lambda p, i: (i, p)),
                  pl.BlockSpec((s, 256), lambda p, i: (0, p)),
                  pl.BlockSpec((2 * V_AUG, s), lambda p, i: (p, 0))],
        out_specs=[pl.BlockSpec((t, 128), lambda p, i: (i, p)),
                   pl.BlockSpec((1, 2, t), lambda p, i: (p, 0, i))],
        out_shape=[jax.ShapeDtypeStruct((s, 512), F32),
                   jax.ShapeDtypeStruct((MLA_HEADS // 2, 2, s), F32)],
        scratch_shapes=[pltpu.VMEM((2, V_AUG, t), F32)],
        compiler_params=_params(("parallel", "arbitrary")),
    )(q2, k, vt)


def _mla_attn_bwd(q2, q2t, k, kt, v, do, dot, lse_rows, delta_rows):
    s = q2.shape[0]
    t = min(512, s)
    nq = s // t

    def body(q_ref, qt_ref, k_ref, kt_ref, v_ref, do_ref, dot_ref, lse_ref, dl_ref,
             dq_ref, dk_ref, dv_ref):
        j = pl.program_id(1)

        @pl.when(j == 0)
        def _():
            dq_ref[...] = jnp.zeros_like(dq_ref)

        dk_ref[...] = jnp.zeros_like(dk_ref)
        dv_ref[...] = jnp.zeros_like(dv_ref)
        row = lax.broadcasted_iota(jnp.int32, (t, t), 0)
        col = lax.broadcasted_iota(jnp.int32, (t, t), 1)
        causal_t = row <= col
        kh = [k_ref[:, LANES * hh:LANES * (hh + 1)] for hh in range(2)]
        kth = [kt_ref[LANES * hh:LANES * (hh + 1), :] for hh in range(2)]
        vh = [v_ref[:, MLA_V * hh:MLA_V * (hh + 1)] for hh in range(2)]
        nt = (((1,), (1,)), ((), ()))

        def step(i, masked):
            r0 = pl.multiple_of(i * t, t)
            sd = []
            for hh in range(2):
                qh = q_ref[pl.ds(r0, t), LANES * hh:LANES * (hh + 1)]
                doh = do_ref[pl.ds(r0, t), MLA_V * hh:MLA_V * (hh + 1)]
                sc_t = lax.dot_general(kh[hh], qh, nt, preferred_element_type=F32)
                sd.append(jnp.where(causal_t, sc_t, NEG_INF) if masked else sc_t)
                sd.append(lax.dot_general(vh[hh], doh, nt, preferred_element_type=F32))
            for hh in range(2):
                lse = lse_ref[0, hh:hh + 1, pl.ds(r0, t)]
                dl = dl_ref[0, hh:hh + 1, pl.ds(r0, t)]
                p_t = jnp.exp2(sd[2 * hh] - lse)
                g_t = (p_t * (sd[2 * hh + 1] - dl)).astype(BF16)
                qth = qt_ref[LANES * hh:LANES * (hh + 1), pl.ds(r0, t)]
                doth = dot_ref[MLA_V * hh:MLA_V * (hh + 1), pl.ds(r0, t)]
                dv_ref[MLA_V * hh:MLA_V * (hh + 1), :] += lax.dot_general(
                    doth, p_t.astype(BF16), nt, preferred_element_type=F32)
                dk_ref[LANES * hh:LANES * (hh + 1), :] += lax.dot_general(
                    qth, g_t, nt, preferred_element_type=F32)
                dq_ref[LANES * hh:LANES * (hh + 1), pl.ds(r0, t)] += jnp.dot(
                    kth[hh], g_t, preferred_element_type=F32)

        step(j, True)

        def trip(i, carry):
            step(i, False)
            return carry

        lax.fori_loop(j + 1, nq, trip, 0)
        dk_ref[...] = dk_ref[...] * LN2

        @pl.when(j == nq - 1)
        def _():
            dq_ref[...] = dq_ref[...] * MLA_SCALE

    return pl.pallas_call(
        body, name="mla_attn_bwd", grid=(MLA_HEADS // 2, nq),
        in_specs=[pl.BlockSpec((s, 256), lambda p, j: (0, p)),
                  pl.BlockSpec((256, s), lambda p, j: (p, 0)),
                  pl.BlockSpec((t, 256), lambda p, j: (j, p)),
                  pl.BlockSpec((256, t), lambda p, j: (p, j)),
                  pl.BlockSpec((t, 128), lambda p, j: (j, p)),
                  pl.BlockSpec((s, 128), lambda p, j: (0, p)),
                  pl.BlockSpec((128, s), lambda p, j: (p, 0)),
                  pl.BlockSpec((1, 2, s), lambda p, j: (p, 0, 0)),
                  pl.BlockSpec((1, 2, s), lambda p, j: (p, 0, 0))],
        out_specs=[pl.BlockSpec((256, s), lambda p, j: (p, 0)),
                   pl.BlockSpec((256, t), lambda p, j: (p, j)),
                   pl.BlockSpec((128, t), lambda p, j: (p, j))],
        out_shape=[jax.ShapeDtypeStruct((1024, s), F32), jax.ShapeDtypeStruct((1024, s), F32),
                   jax.ShapeDtypeStruct((512, s), F32)],
        compiler_params=_params(("parallel", "arbitrary")),
    )(q2, q2t, k, kt, v, do, dot, lse_rows, delta_rows)


SWA_SCALE = SWA_DIM ** -0.5
SWA_COLS = SWA_GROUP * BLOCK
SWA_LOG2 = SWA_SCALE * LOG2E


def _swa_tables():
    k = np.arange(2 * BLOCK)[:, None]
    col = np.arange(SWA_COLS)[None, :]
    dist = BLOCK + (col % BLOCK) - k
    valid = (dist >= 0) & (dist < BLOCK)
    out = np.zeros((2, SWA_KV, 2 * BLOCK, SWA_COLS), np.float32)
    for first in range(2):
        ok = valid & ((k >= BLOCK) | (first == 0))
        for j in range(SWA_KV):
            slope = 2.0 ** -(SWA_GROUP * j + col // BLOCK + 1)
            out[first, j] = np.where(ok, -slope * dist * LOG2E, NEG_INF)
    return jnp.asarray(out)


def _swa_tile_inputs(sq_ref, skv_ref, halo_ref, qn_ref, kn_ref, sk_ref, add_ref, first):
    tokens = sq_ref.shape[0]
    kv_all = jnp.concatenate([halo_ref[...], skv_ref[...]], axis=0)
    kv_t = jnp.transpose(kv_all)
    sq_t = jnp.transpose(sq_ref[...])
    k_raw = [kv_t[SWA_DIM * j:SWA_DIM * (j + 1)] for j in range(SWA_KV)]
    v_t = [kv_t[128 + SWA_DIM * j:128 + SWA_DIM * (j + 1)] for j in range(SWA_KV)]
    v_nat = [kv_all[:, 128 + SWA_DIM * j:128 + SWA_DIM * (j + 1)] for j in range(SWA_KV)]
    q_raw = [sq_t[SWA_DIM * h:SWA_DIM * (h + 1)] for h in range(SWA_HEADS)]
    qn_b = jnp.broadcast_to(qn_ref[...], (SWA_DIM, tokens))
    kn_b = jnp.broadcast_to(kn_ref[...], (SWA_DIM, tokens + BLOCK))
    lane_grp = lax.broadcasted_iota(jnp.int32, (1, SWA_COLS), 1) // BLOCK
    sinks, adds = [], []
    for j in range(SWA_KV):
        row = jnp.zeros((1, SWA_COLS), F32)
        for g in range(SWA_GROUP):
            h = SWA_GROUP * j + g
            row = jnp.where(lane_grp == g, sk_ref[:, h:h + 1] * LOG2E, row)
        sinks.append(row)
        adds.append((jnp.where(first, add_ref[1, j], add_ref[0, j]), add_ref[0, j]))
    return k_raw, v_t, v_nat, q_raw, qn_b, kn_b, sinks, adds


def _swa_probs(kb, qs_t, add, sink):
    s2 = jnp.dot(kb, qs_t, preferred_element_type=F32) * SWA_LOG2 + add
    m = jnp.maximum(jnp.max(s2, axis=0, keepdims=True), sink)
    e = jnp.exp2(s2 - m)
    es = jnp.exp2(sink - m)
    inv = 1.0 / (jnp.sum(e, axis=0, keepdims=True) + es)
    return e, inv, es


def _swa_queries(qn_t, j, b):
    return jnp.concatenate([qn_t[SWA_GROUP * j + g][:, BLOCK * b:BLOCK * (b + 1)]
                            for g in range(SWA_GROUP)], axis=1)


def _swa_fwd(proj, qn, kn, sinks, tables):
    s = proj.shape[0]
    ts = min(1024, s)
    nb = ts // BLOCK

    def body(sq_ref, skv_ref, halo_ref, qn_ref, kn_ref, sk_ref, add_ref, o_ref, ot_ref):
        first = pl.program_id(0) == 0
        k_raw, v_t, _, q_raw, qn_b, kn_b, sink_rows, adds = _swa_tile_inputs(
            sq_ref, skv_ref, halo_ref, qn_ref, kn_ref, sk_ref, add_ref, first)
        kn_nat = [jnp.transpose(_rms0(k, kn_b)).astype(BF16) for k in k_raw]
        v_t = [v.astype(BF16) for v in v_t]
        qn_t = [_rms0(q, qn_b).astype(BF16) for q in q_raw]
        for b in range(nb):
            band = slice(BLOCK * b, BLOCK * (b + 2))
            for j in range(SWA_KV):
                e, inv, _ = _swa_probs(kn_nat[j][band], _swa_queries(qn_t, j, b),
                                       adds[j][0 if b == 0 else 1], sink_rows[j])
                o_t = jnp.dot(v_t[j][:, band], (e * inv).astype(BF16),
                              preferred_element_type=F32)
                for g in range(SWA_GROUP):
                    h = SWA_GROUP * j + g
                    ot_ref[SWA_DIM * h:SWA_DIM * (h + 1), BLOCK * b:BLOCK * (b + 1)] = (
                        o_t[:, BLOCK * g:BLOCK * (g + 1)])
        o_ref[...] = jnp.transpose(ot_ref[...])

    const = lambda shape: pl.BlockSpec(shape, lambda i: (0,) * len(shape))
    return pl.pallas_call(
        body, name="swa_fwd", grid=(s // ts,),
        in_specs=[pl.BlockSpec((ts, 512), lambda i: (i, OFF_SQ // 512)),
                  pl.BlockSpec((ts, 256), lambda i: (i, OFF_SKV // 256)),
                  pl.BlockSpec((BLOCK, 256), lambda i: (jnp.maximum(i * nb - 1, 0), OFF_SKV // 256)),
                  const((SWA_DIM, 1)), const((SWA_DIM, 1)), const((1, SWA_HEADS)),
                  const(tables.shape)],
        out_specs=pl.BlockSpec((ts, 512), lambda i: (i, 0)),
        out_shape=jax.ShapeDtypeStruct((s, 512), F32),
        scratch_shapes=[pltpu.VMEM((512, ts), F32)],
        compiler_params=_params(("parallel",)),
    )(proj, proj, proj, qn, kn, sinks, tables)


def _swa_bwd(proj, qn, kn, sinks, tables, do):
    s = proj.shape[0]
    ts = min(1024, s)
    nb = ts // BLOCK
    nt = s // ts

    def body(sq_ref, skv_ref, halo_ref, qn_ref, kn_ref, sk_ref, add_ref, do_ref,
             dsq_ref, dskv_ref, dqn_ref, dkn_ref, dsk_ref, carry_ref, dqt_ref, dkvt_ref):
        step = pl.program_id(0)
        first = step == nt - 1

        @pl.when(step == 0)
        def _():
            carry_ref[...] = jnp.zeros_like(carry_ref)
            dqn_ref[...] = jnp.zeros_like(dqn_ref)
            dkn_ref[...] = jnp.zeros_like(dkn_ref)
            dsk_ref[...] = jnp.zeros_like(dsk_ref)

        k_raw, v_t, v_nat, q_raw, qn_b, kn_b, sink_rows, adds = _swa_tile_inputs(
            sq_ref, skv_ref, halo_ref, qn_ref, kn_ref, sk_ref, add_ref, first)
        kn_f = [_rms0(k, kn_b) for k in k_raw]
        kn_t = [k.astype(BF16) for k in kn_f]
        kn_nat = [jnp.transpose(k).astype(BF16) for k in kn_f]
        v_nat = [v.astype(BF16) for v in v_nat]
        qn_t = [_rms0(q, qn_b).astype(BF16) for q in q_raw]
        do_t = jnp.transpose(do_ref[...].astype(F32)).astype(BF16)

        dkvt_ref[...] = jnp.zeros_like(dkvt_ref)
        dsink = [jnp.zeros((1, SWA_COLS), F32) for _ in range(SWA_KV)]
        nt_dims = (((1,), (1,)), ((), ()))
        for b in range(nb):
            rows = slice(BLOCK * b, BLOCK * (b + 1))
            band = slice(BLOCK * b, BLOCK * (b + 2))
            for j in range(SWA_KV):
                heads = [SWA_GROUP * j + g for g in range(SWA_GROUP)]
                qs_t = _swa_queries(qn_t, j, b)
                dos_t = jnp.concatenate([do_t[SWA_DIM * h:SWA_DIM * (h + 1), rows] for h in heads],
                                        axis=1)
                e, inv, es = _swa_probs(kn_nat[j][band], qs_t, adds[j][0 if b == 0 else 1],
                                        sink_rows[j])
                p = e * inv
                dp = jnp.dot(v_nat[j][band], dos_t, preferred_element_type=F32)
                dsum = jnp.sum(p * dp, axis=0, keepdims=True)
                dsink[j] = dsink[j] - es * inv * dsum
                g_t = (p * (dp - dsum) * SWA_SCALE).astype(BF16)
                dv_t = lax.dot_general(dos_t, p.astype(BF16), nt_dims,
                                       preferred_element_type=F32)
                dk_t = lax.dot_general(qs_t, g_t, nt_dims, preferred_element_type=F32)
                dq_t = jnp.dot(kn_t[j][:, band], g_t, preferred_element_type=F32)
                dkvt_ref[SWA_DIM * j:SWA_DIM * (j + 1), band] += dk_t
                dkvt_ref[128 + SWA_DIM * j:128 + SWA_DIM * (j + 1), band] += dv_t
                for g, h in enumerate(heads):
                    dqt_ref[SWA_DIM * h:SWA_DIM * (h + 1), rows] = dq_t[:, BLOCK * g:BLOCK * (g + 1)]

        dqn = jnp.zeros((SWA_DIM, 1), F32)
        for h in range(SWA_HEADS):
            _, vjp = jax.vjp(_rms0, q_raw[h], qn_ref[...])
            dq, dg = vjp(dqt_ref[SWA_DIM * h:SWA_DIM * (h + 1), :])
            dqt_ref[SWA_DIM * h:SWA_DIM * (h + 1), :] = dq
            dqn = dqn + dg
        dqn_ref[...] += dqn
        dsq_ref[...] = jnp.transpose(dqt_ref[...]).astype(BF16)
        dkn = jnp.zeros((SWA_DIM, 1), F32)
        lane_grp = lax.broadcasted_iota(jnp.int32, (1, SWA_COLS), 1) // BLOCK
        for j in range(SWA_KV):
            _, vjp = jax.vjp(_rms0, k_raw[j], kn_ref[...])
            dk, dg = vjp(dkvt_ref[SWA_DIM * j:SWA_DIM * (j + 1), :])
            dkvt_ref[SWA_DIM * j:SWA_DIM * (j + 1), :] = dk
            dkn = dkn + dg
            for g in range(SWA_GROUP):
                h = SWA_GROUP * j + g
                dsk_ref[:, h:h + 1] += jnp.sum(jnp.where(lane_grp == g, dsink[j], 0.0), axis=1,
                                               keepdims=True)
        dkn_ref[...] += dkn
        dkv = jnp.transpose(dkvt_ref[...])
        dskv_ref[0:ts - BLOCK, :] = dkv[BLOCK:ts].astype(BF16)
        dskv_ref[ts - BLOCK:ts, :] = (dkv[ts:ts + BLOCK] + carry_ref[...]).astype(BF16)
        carry_ref[...] = dkv[0:BLOCK]

    const = lambda shape: pl.BlockSpec(shape, lambda st: (0,) * len(shape))
    return pl.pallas_call(
        body, name="swa_bwd", grid=(nt,),
        in_specs=[pl.BlockSpec((ts, 512), lambda st: (nt - 1 - st, OFF_SQ // 512)),
                  pl.BlockSpec((ts, 256), lambda st: (nt - 1 - st, OFF_SKV // 256)),
                  pl.BlockSpec((BLOCK, 256),
                               lambda st: (jnp.maximum((nt - 1 - st) * nb - 1, 0), OFF_SKV // 256)),
                  const((SWA_DIM, 1)), const((SWA_DIM, 1)), const((1, SWA_HEADS)),
                  const(tables.shape),
                  pl.BlockSpec((ts, 512), lambda st: (nt - 1 - st, 0))],
        out_specs=[pl.BlockSpec((ts, 512), lambda st: (nt - 1 - st, 0)),
                   pl.BlockSpec((ts, 256), lambda st: (nt - 1 - st, 0)),
                   const((SWA_DIM, 1)), const((SWA_DIM, 1)), const((1, SWA_HEADS))],
        out_shape=[jax.ShapeDtypeStruct((s, 512), BF16), jax.ShapeDtypeStruct((s, 256), BF16),
                   jax.ShapeDtypeStruct((SWA_DIM, 1), F32), jax.ShapeDtypeStruct((SWA_DIM, 1), F32),
                   jax.ShapeDtypeStruct((1, SWA_HEADS), F32)],
        scratch_shapes=[pltpu.VMEM((BLOCK, 256), F32), pltpu.VMEM((512, ts), F32),
                        pltpu.VMEM((256, ts + BLOCK), F32)],
        compiler_params=_params(("arbitrary",)),
    )(proj, proj, proj, qn, kn, sinks, tables, do)


HALO = 8


def _shift_down(u, halo, k):
    tm = u.shape[0]
    rid = lax.broadcasted_iota(jnp.int32, u.shape, 0)
    out = pltpu.roll(u, k, 0)
    for r in range(k):
        out = jnp.where(rid == r, halo[HALO - k + r:HALO - k + r + 1, :], out)
    return out


def _shift_up(u, halo, k):
    tm = u.shape[0]
    rid = lax.broadcasted_iota(jnp.int32, u.shape, 0)
    out = pltpu.roll(u, tm - k, 0)
    for r in range(k):
        out = jnp.where(rid == tm - k + r, halo[r:r + 1, :], out)
    return out


def _conv_fwd_vals(conv_ref, convp_ref, cw_ref, is_first):
    c_h, c_b, c_c = conv_ref[:, 0:512], conv_ref[:, 512:1024], conv_ref[:, 1024:1536]
    u = c_c * c_h
    up = jnp.where(is_first, 0.0, convp_ref[:, 1024:1536] * convp_ref[:, 0:512])
    u1 = _shift_down(u, up, 1)
    u2 = _shift_down(u, up, 2)
    yc = cw_ref[0:1, :] * u2 + cw_ref[1:2, :] * u1 + cw_ref[2:3, :] * u
    return c_h, c_b, c_c, u, u1, u2, yc


def _out_fwd(proj, o_mla, o_swa, x, w_out, cw, target=None):
    s = proj.shape[0]
    tm = min(512, s)
    nt = s // tm
    with_loss = target is not None

    def body(*refs):
        conv_ref, convp_ref, gates_ref, om_ref, os_ref, x_ref, w_ref, cw_ref = refs[:8]
        if with_loss:
            t_ref, y_ref, zt_ref, loss_ref, z_ref = refs[8:]
        else:
            y_ref, zt_ref, z_ref = refs[8:]
        i = pl.program_id(0)
        _, c_b, _, _, _, _, yc = _conv_fwd_vals(conv_ref, convp_ref, cw_ref, i == 0)
        mix = (om_ref[...], c_b * yc, os_ref[...])
        for n in range(3):
            g = gates_ref[:, GROUP * n:GROUP * (n + 1)]
            z = mix[n] * (g * _sigmoid(g))
            z_ref[:, GROUP * n:GROUP * (n + 1)] = z.astype(BF16)
            zt_ref[GROUP * n:GROUP * (n + 1), :] = jnp.transpose(z).astype(BF16)
        y = x_ref[...] + jnp.dot(z_ref[...], w_ref[...], preferred_element_type=F32)
        if not with_loss:
            y_ref[...] = y
            return
        err = y - t_ref[...]
        y_ref[...] = err * (1.0 / D_MODEL)

        @pl.when(i == 0)
        def _():
            loss_ref[...] = jnp.zeros_like(loss_ref)

        sq = jnp.sum((err * err).reshape(tm // 8, 8, D_MODEL), axis=0)
        part = sq[:, 0:LANES]
        for c in range(1, D_MODEL // LANES):
            part = part + sq[:, LANES * c:LANES * (c + 1)]
        loss_ref[...] += part

        @pl.when(i == nt - 1)
        def _():
            loss_ref[...] = jnp.full(loss_ref.shape, (0.5 / D_MODEL) * jnp.sum(loss_ref[...]), F32)

    row = lambda width: pl.BlockSpec((tm, width), lambda i: (i, 0))
    in_specs = [pl.BlockSpec((tm, 1536), lambda i: (i, 0)),
                pl.BlockSpec((HALO, 1536), lambda i: (jnp.maximum(i * (tm // HALO) - 1, 0), 0)),
                pl.BlockSpec((tm, 1536), lambda i: (i, 1)),
                row(512), row(512), row(D_MODEL),
                pl.BlockSpec((D_MIX, D_MODEL), lambda i: (0, 0)),
                pl.BlockSpec((HALO, 512), lambda i: (0, 0))]
    out_specs = [row(D_MODEL), pl.BlockSpec((D_MIX, tm), lambda i: (0, i))]
    out_shape = [jax.ShapeDtypeStruct((s, D_MODEL), F32), jax.ShapeDtypeStruct((D_MIX, s), BF16)]
    operands = [proj, proj, proj, o_mla, o_swa, x, w_out, cw]
    if with_loss:
        in_specs.append(row(D_MODEL))
        out_specs.append(pl.BlockSpec((8, LANES), lambda i: (0, 0)))
        out_shape.append(jax.ShapeDtypeStruct((8, LANES), F32))
        operands.append(target)
    return pl.pallas_call(
        body, name="out_fwd_loss" if with_loss else "out_fwd", grid=(nt,),
        in_specs=in_specs, out_specs=out_specs, out_shape=out_shape,
        scratch_shapes=[pltpu.VMEM((tm, D_MIX), BF16)],
        compiler_params=_params(("arbitrary",) if with_loss else ("parallel",)),
    )(*operands)


def _out_bwd(dy, proj, o_mla, o_swa, w_out, cw):
    s = proj.shape[0]
    tm = min(512, s)
    nt = s // tm
    hb = tm // HALO

    def body(dy_ref, dyn_ref, conv_ref, convp_ref, convn_ref, gates_ref, gatesn_ref, om_ref, os_ref,
             w_ref, cw_ref,
             dconv_ref, dgates_ref, dom_ref, domt_ref, delta_ref, dos_ref, dcw_ref):
        i = pl.program_id(0)
        dz = _mm_nt(dy_ref[...], w_ref[...])

        def gate(n):
            g = gates_ref[:, GROUP * n:GROUP * (n + 1)]
            sg = _sigmoid(g)
            return g * sg, sg * (1.0 + g * (1.0 - sg))

        for n, o_ref, do_ref in ((0, om_ref, dom_ref), (2, os_ref, dos_ref)):
            silu, dsilu = gate(n)
            dzn = dz[:, GROUP * n:GROUP * (n + 1)]
            o = o_ref[...]
            do = dzn * silu
            do_ref[...] = do.astype(do_ref.dtype)
            dgates_ref[:, GROUP * n:GROUP * (n + 1)] = (dzn * o * dsilu).astype(BF16)
            if n == 0:
                domt_ref[...] = jnp.transpose(do).astype(BF16)
                t = do * o
                for h in range(MLA_HEADS):
                    delta_ref[:, h:h + 1] = jnp.sum(t[:, MLA_V * h:MLA_V * (h + 1)], axis=-1,
                                                    keepdims=True)

        c_h, c_b, c_c, u, u1, u2, yc = _conv_fwd_vals(conv_ref, convp_ref, cw_ref, i == 0)
        silu, dsilu = gate(1)
        dzc = dz[:, GROUP:2 * GROUP]
        dgates_ref[:, GROUP:2 * GROUP] = (dzc * (c_b * yc) * dsilu).astype(BF16)
        dycr = dzc * silu
        dyc = dycr * c_b
        gn = gatesn_ref[:, GROUP:2 * GROUP]
        dzc_n = _mm_nt(dyn_ref[...], w_ref[GROUP:2 * GROUP, :])
        dyc_n = jnp.where(i == nt - 1, 0.0, dzc_n * (gn * _sigmoid(gn)) * convn_ref[:, 512:1024])
        d1 = _shift_up(dyc, dyc_n, 1)
        d2 = _shift_up(dyc, dyc_n, 2)
        du = cw_ref[2:3, :] * dyc + cw_ref[1:2, :] * d1 + cw_ref[0:1, :] * d2
        dconv_ref[:, 0:512] = (du * c_c).astype(BF16)
        dconv_ref[:, 512:1024] = (dycr * yc).astype(BF16)
        dconv_ref[:, 1024:1536] = (du * c_h).astype(BF16)

        @pl.when(i == 0)
        def _():
            dcw_ref[...] = jnp.zeros_like(dcw_ref)

        for k, uk in enumerate((u2, u1, u)):
            dcw_ref[k:k + 1, :] += jnp.sum(dyc * uk, axis=0, keepdims=True)

    row = lambda width: pl.BlockSpec((tm, width), lambda i: (i, 0))
    prev = lambda i: jnp.maximum(i * hb - 1, 0)
    nxt = lambda i: jnp.minimum((i + 1) * hb, s // HALO - 1)
    return pl.pallas_call(
        body, name="out_bwd", grid=(nt,),
        in_specs=[row(D_MODEL),
                  pl.BlockSpec((HALO, D_MODEL), lambda i: (nxt(i), 0)),
                  pl.BlockSpec((tm, 1536), lambda i: (i, 0)),
                  pl.BlockSpec((HALO, 1536), lambda i: (prev(i), 0)),
                  pl.BlockSpec((HALO, 1536), lambda i: (nxt(i), 0)),
                  pl.BlockSpec((tm, 1536), lambda i: (i, 1)),
                  pl.BlockSpec((HALO, 1536), lambda i: (nxt(i), 1)),
                  row(512), row(512),
                  pl.BlockSpec((D_MIX, D_MODEL), lambda i: (0, 0)),
                  pl.BlockSpec((HALO, 512), lambda i: (0, 0))],
        out_specs=[row(1536), row(1536), row(512), pl.BlockSpec((512, tm), lambda i: (0, i)),
                   row(MLA_HEADS), row(512), pl.BlockSpec((HALO, 512), lambda i: (0, 0))],
        out_shape=[jax.ShapeDtypeStruct((s, 1536), BF16), jax.ShapeDtypeStruct((s, 1536), BF16),
                   jax.ShapeDtypeStruct((s, 512), BF16), jax.ShapeDtypeStruct((512, s), BF16),
                   jax.ShapeDtypeStruct((s, MLA_HEADS), F32),
                   jax.ShapeDtypeStruct((s, 512), BF16), jax.ShapeDtypeStruct((HALO, 512), F32)],
        compiler_params=_params(("arbitrary",)),
    )(dy, dy, proj, proj, proj, proj, proj, o_mla, o_swa, w_out, cw)


def _adam_update(g, w, m, v):
    c1 = 1.0 - ADAM_B1
    c2 = 1.0 - ADAM_B2
    bc1 = 1.0 - ADAM_B1 ** ADAM_STEP
    bc2 = 1.0 - ADAM_B2 ** ADAM_STEP
    m_new = ADAM_B1 * m + c1 * g
    v_new = ADAM_B2 * v + c2 * (g * g)
    delta = -ADAM_LR * ((m_new / bc1) / (jnp.sqrt(v_new / bc2) + ADAM_EPS) + ADAM_WD * w)
    return delta, m_new, v_new


def _adamw(g, w, m, v):
    rows = g.shape[0]
    tr = min(256, rows)

    def body(g_ref, w_ref, m_ref, v_ref, d_ref, mo_ref, vo_ref):
        d_ref[...], mo_ref[...], vo_ref[...] = _adam_update(g_ref[...], w_ref[...], m_ref[...],
                                                            v_ref[...])

    spec = pl.BlockSpec((tr, g.shape[1]), lambda i: (i, 0))
    return pl.pallas_call(
        body, name="adamw", grid=(rows // tr,),
        in_specs=[spec] * 4, out_specs=[spec] * 3,
        out_shape=[jax.ShapeDtypeStruct(g.shape, F32)] * 3,
        compiler_params=_params(("parallel",)),
    )(g, w, m, v)


def _adamw_small(gs, ws, ms, vs):
    n = len(gs)

    def body(*refs):
        ins, outs = refs[:4 * n], refs[4 * n:]
        for a in range(n):
            res = _adam_update(*(ins[kind * n + a][...] for kind in range(4)))
            for kind in range(3):
                outs[kind * n + a][...] = res[kind]

    vmem = pl.BlockSpec(memory_space=pltpu.VMEM)
    out = pl.pallas_call(
        body, name="adamw_small",
        in_specs=[vmem] * (4 * n), out_specs=[vmem] * (3 * n),
        out_shape=[jax.ShapeDtypeStruct(g.shape, F32) for _ in range(3) for g in gs],
    )(*gs, *ws, *ms, *vs)
    return out[:n], out[n:2 * n], out[2 * n:]


HBM_SPEC = pl.BlockSpec(memory_space=pltpu.HBM)


def _place():
    x, y, c = lax.axis_index("x"), lax.axis_index("y"), lax.axis_index("c")
    chips = [(1 - x, y), (x, 1 - y), (1 - x, 1 - y)]
    return x, y, c, chips


def _all_gather(shards):
    na = len(shards)
    halves = [sh.shape[0] // 2 for sh in shards]

    def body(*refs):
        w_refs, a_refs = refs[:na], refs[na:2 * na]
        send_sems, recv_sems = refs[2 * na:]
        x, y, c, chips = _place()
        k = 2 * x + y
        sib = (x, y, 1 - c)

        def slab(a, kk, hc):
            return a_refs[a].at[kk, pl.ds(hc * halves[a], halves[a]), :]

        def copy(a, n, src, dst, to):
            return pltpu.make_async_remote_copy(
                src_ref=src, dst_ref=dst, send_sem=send_sems.at[6 * a + n],
                recv_sem=recv_sems.at[6 * a + n], device_id=to, device_id_type=MESH)

        first = [copy(a, n, w_refs[a].at[pl.ds(c * halves[a], halves[a]), :], slab(a, k, c),
                      (cx, cy, c))
                 for n, (cx, cy) in enumerate(chips) for a in range(na)]
        for cp in first:
            cp.start()
        passed = []
        for n, (cx, cy) in enumerate(chips):
            kk = 2 * cx + cy
            for a in range(na):
                copy(a, n, slab(a, kk, c), slab(a, kk, c), (cx, cy, c)).wait_recv()
                fwd = copy(a, 3 + n, slab(a, kk, c), slab(a, kk, c), sib)
                fwd.start()
                passed.append(fwd)
        for n, (cx, cy) in enumerate(chips):
            kk = 2 * cx + cy
            for a in range(na):
                copy(a, 3 + n, slab(a, kk, 1 - c), slab(a, kk, 1 - c), sib).wait_recv()
        for cp in first + passed:
            cp.wait_send()

    return pl.pallas_call(
        body, name="weights_all_gather",
        in_specs=[HBM_SPEC] * na, out_specs=[HBM_SPEC] * na,
        out_shape=[jax.ShapeDtypeStruct((N_CHIPS,) + sh.shape, sh.dtype) for sh in shards],
        scratch_shapes=[pltpu.SemaphoreType.DMA((6 * na,)), pltpu.SemaphoreType.DMA((6 * na,))],
    )(*shards)


def _fill_own_slab(buf, src, k_idx):
    n, rows, cols = buf.shape
    tr = _row_tile(rows)
    slabs = src.ndim == 3

    def body(k_ref, src_ref, buf_ref, out_ref):
        out_ref[0] = src_ref[0] if slabs else src_ref[...]

    if slabs:
        src_spec = pl.BlockSpec((1, tr, cols), lambda t, k_ref: (k_ref[0], t, 0))
    else:
        src_spec = pl.BlockSpec((tr, cols), lambda t, k_ref: (t, 0))
    return pl.pallas_call(
        body, name="fill_own_slab",
        grid_spec=pltpu.PrefetchScalarGridSpec(
            num_scalar_prefetch=1, grid=(rows // tr,),
            in_specs=[src_spec, pl.BlockSpec(memory_space=pl.ANY)],
            out_specs=pl.BlockSpec((1, tr, cols), lambda t, k_ref: (k_ref[0], t, 0))),
        out_shape=jax.ShapeDtypeStruct(buf.shape, buf.dtype),
        input_output_aliases={2: 0},
        compiler_params=_params(("parallel",)),
    )(k_idx, src, buf)


def _swap_halves_to_sibling(gs):
    na = len(gs)

    def body(*refs):
        g_refs, r_refs = refs[:na], refs[na:2 * na]
        send_sems, recv_sems = refs[2 * na:]
        x, y, c, _ = _place()
        cps = []
        for a in range(na):
            half = g_refs[a].shape[1] // 2
            cps.append(pltpu.make_async_remote_copy(
                src_ref=g_refs[a].at[:, pl.ds((1 - c) * half, half), :], dst_ref=r_refs[a],
                send_sem=send_sems.at[a], recv_sem=recv_sems.at[a], device_id=(x, y, 1 - c),
                device_id_type=MESH))
        for cp in cps:
            cp.start()
        for cp in cps:
            cp.wait()

    return pl.pallas_call(
        body, name="grads_to_sibling",
        in_specs=[HBM_SPEC] * na, out_specs=[HBM_SPEC] * na,
        out_shape=[jax.ShapeDtypeStruct((g.shape[0], g.shape[1] // 2, g.shape[2]), g.dtype)
                   for g in gs],
        scratch_shapes=[pltpu.SemaphoreType.DMA((na,)), pltpu.SemaphoreType.DMA((na,))],
    )(*gs)


def _row_tile(rows):
    return 256 if rows % 256 == 0 else 128


def _add_sibling(g, r, c_idx, out_dtype):
    n, rows, cols = g.shape
    half = rows // 2
    tr = _row_tile(half)
    nb = half // tr

    def body(c_ref, g_ref, r_ref, p_ref):
        p_ref[...] = (g_ref[...] + r_ref[...]).astype(out_dtype)

    return pl.pallas_call(
        body, name="grads_add_sibling",
        grid_spec=pltpu.PrefetchScalarGridSpec(
            num_scalar_prefetch=1, grid=(n, nb),
            in_specs=[pl.BlockSpec((1, tr, cols), lambda j, t, c_ref: (j, c_ref[0] * nb + t, 0)),
                      pl.BlockSpec((1, tr, cols), lambda j, t, c_ref: (j, t, 0))],
            out_specs=pl.BlockSpec((1, tr, cols), lambda j, t, c_ref: (j, t, 0))),
        out_shape=jax.ShapeDtypeStruct((n, half, cols), out_dtype),
        compiler_params=_params(("parallel", "parallel")),
    )(c_idx, g, r)


def _scatter_to_chips(ps):
    na = len(ps)

    def body(*refs):
        p_refs, q_refs = refs[:na], refs[na:2 * na]
        send_sems, recv_sems = refs[2 * na:]
        x, y, c, chips = _place()
        k = 2 * x + y
        sends = []
        for i, (cx, cy) in enumerate(chips):
            for a in range(na):
                cp = pltpu.make_async_remote_copy(
                    src_ref=p_refs[a].at[2 * cx + cy], dst_ref=q_refs[a].at[k],
                    send_sem=send_sems.at[3 * a + i], recv_sem=recv_sems.at[3 * a + i],
                    device_id=(cx, cy, c), device_id_type=MESH)
                cp.start()
                sends.append(cp)
        for i, (cx, cy) in enumerate(chips):
            kk = 2 * cx + cy
            for a in range(na):
                pltpu.make_async_remote_copy(
                    src_ref=p_refs[a].at[kk], dst_ref=q_refs[a].at[kk],
                    send_sem=send_sems.at[3 * a + i], recv_sem=recv_sems.at[3 * a + i],
                    device_id=(cx, cy, c), device_id_type=MESH).wait_recv()
        for cp in sends:
            cp.wait_send()

    return pl.pallas_call(
        body, name="grads_scatter_to_chips",
        in_specs=[HBM_SPEC] * na, out_specs=[HBM_SPEC] * na,
        out_shape=[jax.ShapeDtypeStruct(p.shape, p.dtype) for p in ps],
        scratch_shapes=[pltpu.SemaphoreType.DMA((3 * na,)), pltpu.SemaphoreType.DMA((3 * na,))],
    )(*ps)


def _sum_chips(q, c_idx):
    n, half, cols = q.shape
    tr = _row_tile(half)
    nb = half // tr

    def body(c_ref, q_ref, o_ref):
        parts = [q_ref[kk].astype(F32) for kk in range(n)]
        o_ref[...] = ((parts[0] + parts[1]) + parts[2]) + parts[3]

    return pl.pallas_call(
        body, name="grads_sum_chips",
        grid_spec=pltpu.PrefetchScalarGridSpec(
            num_scalar_prefetch=1, grid=(nb,),
            in_specs=[pl.BlockSpec((n, tr, cols), lambda t, c_ref: (0, t, 0))],
            out_specs=pl.BlockSpec((tr, cols), lambda t, c_ref: (c_ref[0] * nb + t, 0))),
        out_shape=jax.ShapeDtypeStruct((2 * half, cols), F32),
        compiler_params=_params(("parallel",)),
    )(c_idx, q)


def _join_halves(fulls):
    na = len(fulls)

    def body(*refs):
        o_refs = refs[na:2 * na]
        send_sems, recv_sems = refs[2 * na:]
        x, y, c, _ = _place()
        sends = []
        for a in range(na):
            half = o_refs[a].shape[0] // 2
            rows = o_refs[a].at[pl.ds(c * half, half), :]
            sends.append(pltpu.make_async_remote_copy(
                src_ref=rows, dst_ref=rows, send_sem=send_sems.at[a], recv_sem=recv_sems.at[a],
                device_id=(x, y, 1 - c), device_id_type=MESH))
        for cp in sends:
            cp.start()
        for a in range(na):
            half = o_refs[a].shape[0] // 2
            other = o_refs[a].at[pl.ds((1 - c) * half, half), :]
            pltpu.make_async_remote_copy(
                src_ref=other, dst_ref=other, send_sem=send_sems.at[a], recv_sem=recv_sems.at[a],
                device_id=(x, y, 1 - c), device_id_type=MESH).wait_recv()
        for cp in sends:
            cp.wait_send()

    return pl.pallas_call(
        body, name="grads_join_halves",
        in_specs=[HBM_SPEC] * na, out_specs=[HBM_SPEC] * na,
        out_shape=[jax.ShapeDtypeStruct(f.shape, f.dtype) for f in fulls],
        input_output_aliases={a: a for a in range(na)},
        scratch_shapes=[pltpu.SemaphoreType.DMA((na,)), pltpu.SemaphoreType.DMA((na,))],
    )(*fulls)


def _part_rows(shape):
    size = 1
    for d in shape:
        size *= d
    rows = -(-size // PACK_COLS)
    return size, -(-rows // PACK_ROW_ALIGN) * PACK_ROW_ALIGN


def _pack_rows(arrays, dtype, total_rows):
    parts, used = [], 0
    for a in arrays:
        size, rows = _part_rows(a.shape)
        flat = a.reshape(-1).astype(dtype)
        parts.append(jnp.pad(flat, (0, rows * PACK_COLS - size)).reshape(rows, PACK_COLS))
        used += rows
    parts.append(jnp.zeros((total_rows - used, PACK_COLS), dtype))
    return jnp.concatenate(parts, axis=0)


def _unpack_rows(buf, shapes):
    lead = buf.shape[:-2]
    out, off = [], 0
    for sh in shapes:
        size, rows = _part_rows(sh)
        part = buf[..., off:off + rows, :].reshape(lead + (-1,))[..., :size]
        out.append(part.reshape(lead + tuple(sh)))
        off += rows
    return out


NEW_ORDER = ((928, 1440), (1440, 1952), (1952, 2464), (416, 928), (2464, 2976), (3744, 4256),
             (2976, 3488), (0, 256), (256, 384), (4256, 4320), (384, 416), (4256, 4288),
             (3488, 3616), (3616, 3744))
OLD_ORDER = ((3584, 3840), (3840, 3968), (4032, 4064), (1536, 2048), (0, 512), (512, 1024),
             (1024, 1536), (2048, 2560), (3072, 3584), (4096, 4224), (4224, 4352), (2560, 3072))


def _cols(sources, ranges):
    parts = []
    for a, b in ranges:
        off = 0
        for src in sources:
            width = src.shape[-1]
            lo, hi = max(a, off), min(b, off + width)
            if lo < hi:
                parts.append(src[..., lo - off:hi - off])
            off += width
    return jnp.concatenate(parts, axis=-1)


def _sub_ranges(ranges, a, b):
    out, off = [], 0
    for lo, hi in ranges:
        width = hi - lo
        s0, s1 = max(a, off), min(b, off + width)
        if s0 < s1:
            out.append((lo + s0 - off, lo + s1 - off))
        off += width
    return out


def _rope_tables(s):
    half = MLA_ROPE // 2
    inv_freq = jnp.power(jnp.float32(ROPE_THETA), -jnp.arange(half, dtype=F32) / half)
    ang = inv_freq[:, None] * jnp.arange(s, dtype=F32)[None, :]
    cos, sin = jnp.cos(ang), jnp.sin(ang)
    z = lambda n: jnp.zeros((n, s), F32)
    c = jnp.concatenate([jnp.ones((MLA_NOPE, s), F32), cos, cos, z(32)], axis=0)
    s1 = jnp.concatenate([z(MLA_NOPE), -sin, z(16), z(32)], axis=0)
    s2 = jnp.concatenate([z(MLA_NOPE), z(16), sin, z(32)], axis=0)
    return c, s1, s2


def _pad_lanes(a, n):
    return jnp.pad(a, ((0, 0), (0, n - a.shape[1])))


SHARDED = ("w_in", "w_out", "mla_w_qb", "mla_w_kvb", "conv_w")
REPLICATED = ("norm_g", "mla_q_a_norm", "mla_kv_a_norm", "mla_q_norm", "mla_k_norm",
              "swa_q_norm", "swa_k_norm", "swa_sinks")
WEIGHT_ORDER = ("norm_g", "w_in", "mla_q_a_norm", "mla_w_qb", "mla_kv_a_norm", "mla_w_kvb",
                "mla_q_norm", "mla_k_norm", "conv_w", "swa_q_norm", "swa_k_norm", "swa_sinks", "w_out")
SHARD_AXIS = {"w_in": 2, "w_out": 1, "mla_w_qb": 2, "mla_w_kvb": 2, "conv_w": 2}


def kernel(x, norm_g, w_in, mla_q_a_norm, mla_w_qb, mla_kv_a_norm, mla_w_kvb, mla_q_norm, mla_k_norm, conv_w, swa_q_norm, swa_k_norm, swa_sinks, w_out, loss_target, m_norm_g, m_w_in, m_mla_q_a_norm, m_mla_w_qb, m_mla_kv_a_norm, m_mla_w_kvb, m_mla_q_norm, m_mla_k_norm, m_conv_w, m_swa_q_norm, m_swa_k_norm, m_swa_sinks, m_w_out, v_norm_g, v_w_in, v_mla_q_a_norm, v_mla_w_qb, v_mla_kv_a_norm, v_mla_w_kvb, v_mla_q_norm, v_mla_k_norm, v_conv_w, v_swa_q_norm, v_swa_k_norm, v_swa_sinks, v_w_out):
    weights = dict(norm_g=norm_g, w_in=w_in, mla_q_a_norm=mla_q_a_norm, mla_w_qb=mla_w_qb,
                   mla_kv_a_norm=mla_kv_a_norm, mla_w_kvb=mla_w_kvb, mla_q_norm=mla_q_norm,
                   mla_k_norm=mla_k_norm, conv_w=conv_w, swa_q_norm=swa_q_norm,
                   swa_k_norm=swa_k_norm, swa_sinks=swa_sinks, w_out=w_out)
    mom_m = dict(norm_g=m_norm_g, w_in=m_w_in, mla_q_a_norm=m_mla_q_a_norm, mla_w_qb=m_mla_w_qb,
                 mla_kv_a_norm=m_mla_kv_a_norm, mla_w_kvb=m_mla_w_kvb, mla_q_norm=m_mla_q_norm,
                 mla_k_norm=m_mla_k_norm, conv_w=m_conv_w, swa_q_norm=m_swa_q_norm,
                 swa_k_norm=m_swa_k_norm, swa_sinks=m_swa_sinks, w_out=m_w_out)
    mom_v = dict(norm_g=v_norm_g, w_in=v_w_in, mla_q_a_norm=v_mla_q_a_norm, mla_w_qb=v_mla_w_qb,
                 mla_kv_a_norm=v_mla_kv_a_norm, mla_w_kvb=v_mla_w_kvb, mla_q_norm=v_mla_q_norm,
                 mla_k_norm=v_mla_k_norm, conv_w=v_conv_w, swa_q_norm=v_swa_q_norm,
                 swa_k_norm=v_swa_k_norm, swa_sinks=v_swa_sinks, w_out=v_w_out)
    xs = x[0]
    target = loss_target[0]
    s = xs.shape[0]
    c_idx = lax.axis_index("c").astype(jnp.int32).reshape(1)
    k_idx = (2 * lax.axis_index("x") + lax.axis_index("y")).astype(jnp.int32).reshape(1)

    conv_bits = lax.bitcast_convert_type(conv_w, BF16)
    small_list = [w_out, mla_w_qb, mla_w_kvb, conv_bits]
    w_in_rows = DEPTH * D_MODEL
    own = [w_in.astype(BF16).reshape(w_in_rows, w_in.shape[2]), _pack_rows(small_list, BF16, PACK_ROWS)]
    gathered_in, gathered_rest = [_fill_own_slab(buf, src, k_idx)
                                  for buf, src in zip(_all_gather(own), own)]
    parts = _unpack_rows(gathered_rest, [a.shape for a in small_list])
    join = lambda p, axis: jnp.concatenate([p[k] for k in range(N_CHIPS)], axis=axis)
    w_in_slabs = gathered_in.reshape(N_CHIPS, DEPTH, D_MODEL, w_in.shape[2])
    w_in_zeros = jnp.zeros((D_MODEL, 64), BF16)
    w_out_full = join(parts[0], 1)
    w_qb_full = join(parts[1], 2)
    w_kvb_full = join(parts[2], 2)
    conv_full = lax.bitcast_convert_type(join(parts[3], 2), F32)

    rope = _rope_tables(s)
    swa_tables = _swa_tables()
    layers = []
    for l in range(DEPTH):
        wq = jnp.pad(w_qb_full[l].reshape(MLA_Q_LORA, MLA_HEADS, MLA_QK),
                     ((0, 0), (0, 0), (0, LANES - MLA_QK))).reshape(MLA_Q_LORA, MLA_HEADS * LANES)
        kv = w_kvb_full[l].reshape(MLA_KV_LORA, MLA_HEADS, MLA_NOPE + MLA_V)
        wk = jnp.pad(kv[:, :, :MLA_NOPE], ((0, 0), (0, 0), (0, LANES - MLA_NOPE)))
        wkv = jnp.concatenate([wk.reshape(MLA_KV_LORA, MLA_HEADS * LANES),
                               kv[:, :, MLA_NOPE:].reshape(MLA_KV_LORA, MLA_HEADS * MLA_V)], axis=1)
        layers.append(dict(
            w_in=_cols([w_in_slabs[k, l] for k in range(N_CHIPS)] + [w_in_zeros], NEW_ORDER),
            w_out=w_out_full[l], wq=wq, wkv=wkv,
            cw=jnp.pad(conv_full[l], ((0, HALO - 3), (0, 0))),
            g=norm_g[l][None],
            mla_norms=(mla_q_a_norm[l][:, None], mla_kv_a_norm[l][:, None],
                       _pad_lanes(mla_q_norm[l][None], LANES).T, _pad_lanes(mla_k_norm[l][None], LANES).T),
            mla_weights=(wq, wkv, wq.T, wkv.T),
            sqn=swa_q_norm[l][:, None], skn=swa_k_norm[l][:, None], sinks=swa_sinks[l][None]))

    saved = []
    h_in = xs
    for l in range(DEPTH):
        p = layers[l]
        proj, hb = _in_proj_fwd(h_in, p["g"], p["w_in"])
        q, k, v, qt, kt, vt = _mla_prep_fwd(proj, p["mla_norms"], p["mla_weights"], rope)
        o_mla, lse = _mla_attn_fwd(q, k, vt)
        o_swa = _swa_fwd(proj, p["sqn"], p["skn"], p["sinks"], swa_tables)
        last = l == DEPTH - 1
        y, z, *loss_acc = _out_fwd(proj, o_mla, o_swa, h_in, p["w_out"], p["cw"],
                                   target if last else None)
        saved.append(dict(x=h_in, proj=proj, hb=hb, q=q, k=k, v=v, qt=qt, kt=kt, o_mla=o_mla, lse=lse,
                          o_swa=o_swa, z=z))
        h_in = y

    dy, loss_acc = h_in, loss_acc[0]
    loss = lax.psum(loss_acc[0, 0], ("x", "y", "c"))

    grads = {n: [None] * DEPTH for n in WEIGHT_ORDER}
    for l in reversed(range(DEPTH)):
        p, a = layers[l], saved[l]
        dconv, dgates, do_mla, do_mla_t, delta, do_swa, dcw = _out_bwd(dy, a["proj"], a["o_mla"], a["o_swa"],
                                                             p["w_out"], p["cw"])
        grads["w_out"][l] = _weight_grads(a["z"], [dy], "dw_out")[0]
        grads["conv_w"][l] = dcw[0:3]
        delta_rows = jnp.transpose(delta, (1, 0)).reshape(MLA_HEADS // 2, 2, s)
        dq, dk, dv = _mla_attn_bwd(a["q"], a["qt"], a["k"], a["kt"], a["v"], do_mla, do_mla_t,
                                   a["lse"], delta_rows)
        dmla, dqan, dkvan, dqn, dkn, dwq_t, dwkv_t = _mla_prep_bwd(
            a["proj"], p["mla_norms"], p["mla_weights"], rope, dq, dk, dv)
        dwq, dwkv = dwq_t.T, dwkv_t.T
        dsq, dskv, dsqn, dskn, dsinks = _swa_bwd(a["proj"], p["sqn"], p["skn"], p["sinks"], swa_tables, do_swa)
        pieces = [dconv, dgates, dsq, dmla, dskv]
        dx, dg = _in_proj_bwd(pieces, a["x"], p["g"], p["w_in"], dy)
        grads["w_in"][l] = _weight_grads(a["hb"], pieces, "dw_in")
        grads["norm_g"][l] = dg[0]
        grads["mla_q_a_norm"][l] = dqan[:, 0]
        grads["mla_kv_a_norm"][l] = dkvan[:, 0]
        grads["mla_q_norm"][l] = dqn[:MLA_QK, 0]
        grads["mla_k_norm"][l] = dkn[:MLA_QK, 0]
        grads["mla_w_qb"][l] = dwq.reshape(MLA_Q_LORA, MLA_HEADS, LANES)[:, :, :MLA_QK].reshape(
            MLA_Q_LORA, MLA_HEADS * MLA_QK)
        dwk = dwkv[:, :MLA_HEADS * LANES].reshape(MLA_KV_LORA, MLA_HEADS, LANES)[:, :, :MLA_NOPE]
        dwv = dwkv[:, MLA_HEADS * LANES:].reshape(MLA_KV_LORA, MLA_HEADS, MLA_V)
        grads["mla_w_kvb"][l] = jnp.concatenate([dwk, dwv], axis=2).reshape(
            MLA_KV_LORA, MLA_HEADS * (MLA_NOPE + MLA_V))
        grads["swa_q_norm"][l] = dsqn[:, 0]
        grads["swa_k_norm"][l] = dskn[:, 0]
        grads["swa_sinks"][l] = dsinks[0]
        dy = dx
    grad_x = dy[None]
    full_grads = {n: jnp.stack(grads[n]) for n in WEIGHT_ORDER if n != "w_in"}

    rest = tuple(n for n in SHARDED if n not in ("w_in", "w_out"))
    rep_shapes = [weights[n].shape for n in REPLICATED]
    rep_grads = jnp.concatenate([full_grads[n].reshape(-1) for n in REPLICATED])

    def chunk(g, n, k):
        width = g.shape[SHARD_AXIS[n]] // N_CHIPS
        return lax.slice_in_dim(g, k * width, (k + 1) * width, axis=SHARD_AXIS[n])

    shard_cols = w_in.shape[2]
    g_in = jnp.stack([
        jnp.concatenate([_cols(grads["w_in"][l], _sub_ranges(OLD_ORDER, k * shard_cols,
                                                             (k + 1) * shard_cols))
                         for l in range(DEPTH)], axis=0)
        for k in range(N_CHIPS)])
    g_out = jnp.stack([chunk(full_grads["w_out"], "w_out", k).reshape(-1, D_MODEL)
                       for k in range(N_CHIPS)])
    g_small = jnp.stack([_pack_rows([chunk(full_grads[n], n, k) for n in rest] + [rep_grads],
                                    F32, SMALL_ROWS) for k in range(N_CHIPS)])
    gs = (g_in, g_out, g_small)
    from_sibling = _swap_halves_to_sibling(gs)
    partial = [_add_sibling(g, r, c_idx, dt) for g, r, dt in zip(gs, from_sibling, (BF16, BF16, F32))]
    by_chip = [_fill_own_slab(q, p, k_idx) for q, p in zip(_scatter_to_chips(partial), partial)]
    g_in_mine, g_out_mine, g_small_mine = _join_halves([_sum_chips(q, c_idx) for q in by_chip])

    vals = _unpack_rows(g_small_mine, [weights[n].shape for n in rest] + [(rep_grads.shape[0],)])
    grad = dict(zip(rest, vals[:-1]))
    grad["w_in"] = g_in_mine.reshape(w_in.shape)
    grad["w_out"] = g_out_mine.reshape(w_out.shape)
    off = 0
    for n, sh in zip(REPLICATED, rep_shapes):
        grad[n] = vals[-1][off:off + sh[0] * sh[1]].reshape(sh)
        off += sh[0] * sh[1]
    results = {}
    for n in ("w_in", "w_out"):
        view = lambda a, n=n: a.reshape(-1, weights[n].shape[-1])
        res = _adamw(view(grad[n]), view(weights[n]), view(mom_m[n]), view(mom_v[n]))
        results[n] = [r.reshape(weights[n].shape) for r in res]
    small = tuple(n for n in WEIGHT_ORDER if n not in results)
    res = _adamw_small(*([d[n] for n in small] for d in (grad, weights, mom_m, mom_v)))
    for a, n in enumerate(small):
        results[n] = [res[kind][a] for kind in range(3)]
    unpacked = [grad] + [{n: results[n][kind] for n in WEIGHT_ORDER} for kind in range(3)]
    outs = [loss, grad_x]
    for group in unpacked:
        outs += [group[n] for n in WEIGHT_ORDER]
    return tuple(outs)
```

```python
import jax
import numpy as np
import jax.numpy as jnp
from jax import lax
from jax.experimental import pallas as pl
from jax.experimental.pallas import tpu as pltpu

F32 = jnp.float32
BF16 = jnp.bfloat16

D_MODEL = 1024
DEPTH = 2
GROUP = 512
D_MIX = 3 * GROUP
BLOCK = 128
RMS_EPS = 1e-6
NEG_INF = -1e30
MLA_HEADS = 8
MLA_QK = 96
MLA_NOPE = 64
MLA_ROPE = 32
MLA_V = 64
V_AUG = 80
MLA_Q_LORA = 256
MLA_KV_LORA = 128
ROPE_THETA = 10000.0
SWA_HEADS = 8
SWA_KV = 2
SWA_GROUP = 4
SWA_DIM = 64
N_CHIPS = 4

NC = 4352
OFF_SQ, OFF_MLA, OFF_SKV = 3072, 3584, 4096

VMEM_LIMIT = 56 * 1024 * 1024
LANES = 128
PACK_COLS = 1024
PACK_ROW_ALIGN = 16
SMALL_ROWS = 256

ADAM_LR = 0.001
ADAM_B1 = 0.9
ADAM_B2 = 0.999
ADAM_EPS = 1e-08
ADAM_WD = 0.01
ADAM_STEP = 10

MESH = pl.DeviceIdType.MESH


def _params(sem, vmem=VMEM_LIMIT):
    return pltpu.CompilerParams(dimension_semantics=sem, vmem_limit_bytes=vmem)


def _dot(a, b, dims):
    return lax.dot_general(a.astype(BF16), b.astype(BF16), (dims, ((), ())),
                           preferred_element_type=F32)


def _mm(a, b):
    return _dot(a, b, ((1,), (0,)))


def _mm_nt(a, b):
    return _dot(a, b, ((1,), (1,)))


def _rms(x, g, n=None):
    n = x.shape[-1] if n is None else n
    ms = jnp.sum(x * x, axis=-1, keepdims=True) * (1.0 / n)
    return x * lax.rsqrt(ms + RMS_EPS) * g


def _sigmoid(x):
    return 1.0 / (1.0 + jnp.exp(-x))


def _in_proj_fwd(x, g, w):
    s = x.shape[0]
    tm = min(512, s)

    def body(x_ref, g_ref, w_ref, proj_ref, hbt_ref):
        h = _rms(x_ref[...], g_ref[...])
        hbt_ref[...] = jnp.transpose(h).astype(BF16)
        proj_ref[...] = jnp.dot(h.astype(BF16), w_ref[...], preferred_element_type=F32)

    return pl.pallas_call(
        body, name="in_proj_fwd", grid=(s // tm,),
        in_specs=[pl.BlockSpec((tm, D_MODEL), lambda i: (i, 0)),
                  pl.BlockSpec((1, D_MODEL), lambda i: (0, 0)),
                  pl.BlockSpec((D_MODEL, NC), lambda i: (0, 0))],
        out_specs=[pl.BlockSpec((tm, NC), lambda i: (i, 0)),
                   pl.BlockSpec((D_MODEL, tm), lambda i: (0, i))],
        out_shape=[jax.ShapeDtypeStruct((s, NC), F32), jax.ShapeDtypeStruct((D_MODEL, s), BF16)],
        compiler_params=_params(("parallel",)),
    )(x, g, w)


def _in_proj_bwd(pieces, x, g, w, dres):
    s = x.shape[0]
    tm = min(512, s)
    n_p = len(pieces)

    def body(*refs):
        p_refs = refs[:n_p]
        x_ref, g_ref, w_ref, dres_ref, dx_ref, dg_ref = refs[n_p:]
        dh = None
        off = 0
        for r in p_refs:
            width = r.shape[1]
            t = _mm_nt(r[...], w_ref[:, off:off + width])
            dh = t if dh is None else dh + t
            off += width
        _, vjp = jax.vjp(_rms, x_ref[...], g_ref[...])
        dx, dg = vjp(dh)
        dx_ref[...] = dx + dres_ref[...]

        @pl.when(pl.program_id(0) == 0)
        def _():
            dg_ref[...] = jnp.zeros_like(dg_ref)

        dg_ref[...] += dg

    in_specs = [pl.BlockSpec((tm, p.shape[1]), lambda i: (i, 0)) for p in pieces]
    in_specs += [pl.BlockSpec((tm, D_MODEL), lambda i: (i, 0)),
                 pl.BlockSpec((1, D_MODEL), lambda i: (0, 0)),
                 pl.BlockSpec((D_MODEL, NC), lambda i: (0, 0)),
                 pl.BlockSpec((tm, D_MODEL), lambda i: (i, 0))]
    return pl.pallas_call(
        body, name="in_proj_bwd", grid=(s // tm,),
        in_specs=in_specs,
        out_specs=[pl.BlockSpec((tm, D_MODEL), lambda i: (i, 0)),
                   pl.BlockSpec((1, D_MODEL), lambda i: (0, 0))],
        out_shape=[jax.ShapeDtypeStruct((s, D_MODEL), F32), jax.ShapeDtypeStruct((1, D_MODEL), F32)],
        compiler_params=_params(("arbitrary",)),
    )(*pieces, x, g, w, dres)


def _weight_grads(at, bs, name):
    m, s = at.shape
    nb = len(bs)
    tk = min(512, s)

    def body(a_ref, *refs):
        b_refs, o_refs = refs[:nb], refs[nb:]

        @pl.when(pl.program_id(0) == 0)
        def _():
            for o_ref in o_refs:
                o_ref[...] = jnp.zeros_like(o_ref)

        a = a_ref[...]
        for b_ref, o_ref in zip(b_refs, o_refs):
            o_ref[...] += _mm(a, b_ref[...])

    return pl.pallas_call(
        body, name=name, grid=(s // tk,),
        in_specs=[pl.BlockSpec((m, tk), lambda k: (0, k))]
        + [pl.BlockSpec((tk, b.shape[1]), lambda k: (k, 0)) for b in bs],
        out_specs=[pl.BlockSpec((m, b.shape[1]), lambda k: (0, 0)) for b in bs],
        out_shape=[jax.ShapeDtypeStruct((m, b.shape[1]), F32) for b in bs],
        compiler_params=_params(("arbitrary",)),
    )(at, *bs)


def _rms0(x, g, n=None):
    n = x.shape[0] if n is None else n
    ms = jnp.sum(x * x, axis=0, keepdims=True) * (1.0 / n)
    return x * lax.rsqrt(ms + RMS_EPS) * g


@jax.custom_vjp
def _rope0(t, c, s1, s2):
    return t * c + pltpu.roll(t, LANES - 16, 0) * s1 + pltpu.roll(t, 16, 0) * s2


def _rope0_fwd(t, c, s1, s2):
    return _rope0(t, c, s1, s2), (c, s1, s2)


def _rope0_bwd(res, g):
    c, s1, s2 = res
    dt = g * c + pltpu.roll(g * s1, 16, 0) + pltpu.roll(g * s2, LANES - 16, 0)
    return dt, jnp.zeros_like(c), jnp.zeros_like(s1), jnp.zeros_like(s2)


_rope0.defvjp(_rope0_fwd, _rope0_bwd)


@jax.custom_vjp
def _mmw(w, wt, x):
    return _mm(w, x)


def _mmw_fwd(w, wt, x):
    return _mm(w, x), (wt, x)


def _mmw_bwd(res, g):
    wt, x = res
    return _mm_nt(g, x), jnp.zeros_like(wt), _mm(wt, g)


_mmw.defvjp(_mmw_fwd, _mmw_bwd)


def _prep_fn(q_lat, kv_lat, kr, qan, kvan, qn, kn, wq, wk, wv, wqt, wkt, wvt, c, s1, s2, mm):
    tokens = q_lat.shape[1]
    rq = _rms0(q_lat, qan)
    rkv = _rms0(kv_lat, kvan)
    qn_b = jnp.broadcast_to(qn, (LANES, tokens))
    kn_b = jnp.broadcast_to(kn, (LANES, tokens))
    qs, ks = [], []
    for h in range(MLA_HEADS):
        qs.append(_rope0(_rms0(mm(wq[h], wqt[h], rq), qn_b, MLA_QK), c, s1, s2))
        ks.append(_rope0(_rms0(mm(wk[h], wkt[h], rkv) + kr, kn_b, MLA_QK), c, s1, s2))
    return tuple(qs), tuple(ks), mm(wv, wvt, rkv)


def _prep_weights(wq_ref, wkv_ref, wqt_ref, wkvt_ref):
    heads = range(MLA_HEADS)
    wq = tuple(wqt_ref[LANES * h:LANES * (h + 1), :].astype(F32) for h in heads)
    wk = tuple(wkvt_ref[LANES * h:LANES * (h + 1), :].astype(F32) for h in heads)
    wv = wkvt_ref[LANES * MLA_HEADS:, :].astype(F32)
    wqt = tuple(wq_ref[:, LANES * h:LANES * (h + 1)].astype(F32) for h in heads)
    wkt = tuple(wkv_ref[:, LANES * h:LANES * (h + 1)].astype(F32) for h in heads)
    wvt = wkv_ref[:, LANES * MLA_HEADS:].astype(F32)
    return wq, wk, wv, wqt, wkt, wvt


def _prep_in_specs(tm):
    const = lambda shape: pl.BlockSpec(shape, lambda i: (0, 0))
    col = lambda height: pl.BlockSpec((height, tm), lambda i: (0, i))
    return [pl.BlockSpec((tm, 512), lambda i: (i, OFF_MLA // 512)),
            const((MLA_Q_LORA, 1)), const((MLA_KV_LORA, 1)), const((LANES, 1)), const((LANES, 1)),
            const((MLA_Q_LORA, 1024)), const((MLA_KV_LORA, 1536)),
            const((1024, MLA_Q_LORA)), const((1536, MLA_KV_LORA)),
            col(LANES), col(LANES), col(LANES)]


def _prep_operands(blk_ref, refs):
    qan_ref, kvan_ref, qn_ref, kn_ref, wq_ref, wkv_ref, wqt_ref, wkvt_ref, c_ref, s1_ref, s2_ref = refs
    blk_t = jnp.transpose(blk_ref[...])
    diff = (blk_t[0:256], blk_t[256:384], blk_t[384:512],
            qan_ref[...], kvan_ref[...], qn_ref[...], kn_ref[...])
    weights = _prep_weights(wq_ref, wkv_ref, wqt_ref, wkvt_ref)
    return diff, weights, (c_ref[...], s1_ref[...], s2_ref[...])


def _mla_prep_fwd(proj, norms, weights, rope):
    s = proj.shape[0]
    tm = min(512, s)

    def body(blk_ref, *refs):
        ins, (q_ref, k_ref, v_ref, qt_ref, kt_ref, vt_ref) = refs[:11], refs[11:]
        diff, (wq, wk, wv, wqt, wkt, wvt), tables = _prep_operands(blk_ref, ins)
        qs, ks, v = _prep_fn(*diff, wq, wk, wv, wqt, wkt, wvt, *tables,
                             lambda w, wt, x: _mm(w, x))
        for h in range(MLA_HEADS):
            q2 = qs[h] * Q_PRESCALE
            qt_ref[LANES * h:LANES * (h + 1), :] = q2.astype(BF16)
            kt_ref[LANES * h:LANES * (h + 1), :] = ks[h].astype(BF16)
            q_ref[:, LANES * h:LANES * (h + 1)] = jnp.transpose(q2).astype(BF16)
            k_ref[:, LANES * h:LANES * (h + 1)] = jnp.transpose(ks[h]).astype(BF16)
        ones_row = (lax.broadcasted_iota(jnp.int32, (V_AUG - MLA_V, v.shape[1]), 0) == 0).astype(BF16)
        for h in range(MLA_HEADS):
            vt_ref[V_AUG * h:V_AUG * h + MLA_V, :] = v[MLA_V * h:MLA_V * (h + 1)].astype(BF16)
            vt_ref[V_AUG * h + MLA_V:V_AUG * (h + 1), :] = ones_row
        v_ref[...] = jnp.transpose(v).astype(BF16)

    row = lambda width: pl.BlockSpec((tm, width), lambda i: (i, 0))
    col = lambda height: pl.BlockSpec((height, tm), lambda i: (0, i))
    return pl.pallas_call(
        body, name="mla_prep_fwd", grid=(s // tm,),
        in_specs=_prep_in_specs(tm),
        out_specs=[row(1024), row(1024), row(512), col(1024), col(1024), col(MLA_HEADS * V_AUG)],
        out_shape=[jax.ShapeDtypeStruct((s, 1024), BF16), jax.ShapeDtypeStruct((s, 1024), BF16),
                   jax.ShapeDtypeStruct((s, 512), BF16), jax.ShapeDtypeStruct((1024, s), BF16),
                   jax.ShapeDtypeStruct((1024, s), BF16),
                   jax.ShapeDtypeStruct((MLA_HEADS * V_AUG, s), BF16)],
        compiler_params=_params(("parallel",)),
    )(proj, *norms, *weights, *rope)


def _mla_prep_bwd(proj, norms, weights, rope, dq, dk, dv):
    s = proj.shape[0]
    tm = min(512, s)

    def body(blk_ref, *refs):
        ins, (dq_ref, dk_ref, dv_ref) = refs[:11], refs[11:14]
        dblk_ref, dqan_ref, dkvan_ref, dqn_ref, dkn_ref, dwq_ref, dwkv_ref = refs[14:]
        diff, (wq, wk, wv, wqt, wkt, wvt), tables = _prep_operands(blk_ref, ins)

        def fn(q_lat, kv_lat, kr, qan, kvan, qn, kn, wq_, wk_, wv_):
            return _prep_fn(q_lat, kv_lat, kr, qan, kvan, qn, kn, wq_, wk_, wv_, wqt, wkt, wvt,
                            *tables, _mmw)

        _, vjp = jax.vjp(fn, *diff, wq, wk, wv)
        heads = range(MLA_HEADS)
        cts = (tuple(dq_ref[LANES * h:LANES * (h + 1), :] for h in heads),
               tuple(dk_ref[LANES * h:LANES * (h + 1), :] for h in heads), dv_ref[...])
        dq_lat, dkv_lat, dkr, dqan, dkvan, dqn, dkn, dwq_h, dwk_h, dwv = vjp(cts)
        dblk_ref[...] = jnp.transpose(
            jnp.concatenate([dq_lat, dkv_lat, dkr], axis=0)).astype(BF16)

        @pl.when(pl.program_id(0) == 0)
        def _():
            for r in (dqan_ref, dkvan_ref, dqn_ref, dkn_ref, dwq_ref, dwkv_ref):
                r[...] = jnp.zeros_like(r)

        dqan_ref[...] += dqan
        dkvan_ref[...] += dkvan
        dqn_ref[...] += dqn
        dkn_ref[...] += dkn
        for h in heads:
            dwq_ref[LANES * h:LANES * (h + 1), :] += dwq_h[h]
            dwkv_ref[LANES * h:LANES * (h + 1), :] += dwk_h[h]
        dwkv_ref[LANES * MLA_HEADS:, :] += dwv

    const = lambda shape: pl.BlockSpec(shape, lambda i: (0, 0))
    col = lambda height: pl.BlockSpec((height, tm), lambda i: (0, i))
    shapes = [(MLA_Q_LORA, 1), (MLA_KV_LORA, 1), (LANES, 1), (LANES, 1),
              (1024, MLA_Q_LORA), (1536, MLA_KV_LORA)]
    return pl.pallas_call(
        body, name="mla_prep_bwd", grid=(s // tm,),
        in_specs=_prep_in_specs(tm) + [col(1024), col(1024), col(512)],
        out_specs=[pl.BlockSpec((tm, 512), lambda i: (i, 0))] + [const(sh) for sh in shapes],
        out_shape=[jax.ShapeDtypeStruct((s, 512), BF16)]
        + [jax.ShapeDtypeStruct(sh, F32) for sh in shapes],
        compiler_params=_params(("arbitrary",)),
    )(proj, *norms, *weights, *rope, dq, dk, dv)


MLA_SCALE = MLA_QK ** -0.5


LOG2E = 1.4426950408889634
LN2 = 0.6931471805599453
Q_PRESCALE = MLA_SCALE * LOG2E
HEAD_GROUPS = ((0, 1),)


def _mla_attn_fwd(q2, k, vt):
    s = q2.shape[0]
    t = min(512, s)
    tk = min(128, s)
    nq = s // t
    r = t // tk

    def body(q_ref, k_ref, vt_ref, o_ref, lse_ref, acc_ref):
        i = pl.program_id(1)
        row = lax.broadcasted_iota(jnp.int32, (tk, t), 0)
        col = lax.broadcasted_iota(jnp.int32, (tk, t), 1)
        qh = [q_ref[:, LANES * hh:LANES * (hh + 1)] for hh in range(2)]
        acc_ref[...] = jnp.zeros_like(acc_ref)

        def scores(j, heads, diag=None):
            r0 = pl.multiple_of(j * tk, tk)
            q_lo = 0 if diag is None else diag * tk
            out = []
            for hh in heads:
                kc = k_ref[pl.ds(r0, tk), LANES * hh:LANES * (hh + 1)]
                sc = lax.dot_general(kc, qh[hh][q_lo:], (((1,), (1,)), ((), ())),
                                     preferred_element_type=F32)
                out.append(sc if diag is None
                           else jnp.where(row[:, :t - q_lo] <= col[:, :t - q_lo], sc, NEG_INF))
            return tuple(out)

        for heads in HEAD_GROUPS:
            stats = tuple(jnp.full((1, t), NEG_INF, F32) for _ in heads)

            def consume(j, scs, stats, q_lo=0, heads=heads):
                r0 = pl.multiple_of(j * tk, tk)
                out, ps, alphas = [], [], []
                for n, hh in enumerate(heads):
                    m_old = stats[n][:, q_lo:]
                    m_new = jnp.maximum(m_old, jnp.max(scs[n], axis=0, keepdims=True))
                    ps.append(jnp.exp2(scs[n] - m_new).astype(BF16))
                    alphas.append(jnp.exp2(m_old - m_new))
                    out.append(m_new if q_lo == 0
                               else jnp.concatenate([stats[n][:, :q_lo], m_new], axis=1))
                for n, hh in enumerate(heads):
                    vc = vt_ref[V_AUG * hh:V_AUG * (hh + 1), pl.ds(r0, tk)]
                    acc_ref[hh, :, q_lo:] = alphas[n] * acc_ref[hh, :, q_lo:] + jnp.dot(
                        vc, ps[n], preferred_element_type=F32)
                return tuple(out)

            def group(j0, stats, diag, heads=heads):
                scs = [scores(j0 + d, heads, d if diag else None) for d in range(r)]
                for d in range(r):
                    stats = consume(j0 + d, scs[d], stats, d * tk if diag else 0)
                return stats

            stats = group(r * i, stats, True)
            stats = lax.fori_loop(0, i, lambda j, st: group(r * j, st, False), stats)
            for n, hh in enumerate(heads):
                l = acc_ref[hh, MLA_V:MLA_V + 1, :]
                o_ref[:, MLA_V * hh:MLA_V * (hh + 1)] = jnp.transpose(acc_ref[hh, 0:MLA_V, :] / l)
                lse_ref[0, hh:hh + 1, :] = stats[n] + jnp.log2(l)

    return pl.pallas_call(
        body, name="mla_attn_fwd", grid=(MLA_HEADS // 2, nq),
        in_specs=[pl.BlockSpec((t, 256), lambda p, i: (i, p)),
                  pl.BlockSpec((s, 256), lambda p, i: (0, p)),
                  pl.BlockSpec((2 * V_AUG, s), lambda p, i: (p, 0))],
        out_specs=[pl.BlockSpec((t, 128), lambda p, i: (i, p)),
                   pl.BlockSpec((1, 2, t), lambda p, i: (p, 0, i))],
        out_shape=[jax.ShapeDtypeStruct((s, 512), F32),
                   jax.ShapeDtypeStruct((MLA_HEADS // 2, 2, s), F32)],
        scratch_shapes=[pltpu.VMEM((2, V_AUG, t), F32)],
        compiler_params=_params(("parallel", "arbitrary")),
    )(q2, k, vt)


def _mla_attn_bwd(q2, q2t, k, kt, v, do, dot, lse_rows, delta_rows):
    s = q2.shape[0]
    t = min(512, s)
    nq = s // t

    def body(q_ref, qt_ref, k_ref, kt_ref, v_ref, do_ref, dot_ref, lse_ref, dl_ref,
             dq_ref, dk_ref, dv_ref):
        j = pl.program_id(1)

        @pl.when(j == 0)
        def _():
            dq_ref[...] = jnp.zeros_like(dq_ref)

        dk_ref[...] = jnp.zeros_like(dk_ref)
        dv_ref[...] = jnp.zeros_like(dv_ref)
        row = lax.broadcasted_iota(jnp.int32, (t, t), 0)
        col = lax.broadcasted_iota(jnp.int32, (t, t), 1)
        causal_t = row <= col
        kh = [k_ref[:, LANES * hh:LANES * (hh + 1)] for hh in range(2)]
        kth = [kt_ref[LANES * hh:LANES * (hh + 1), :] for hh in range(2)]
        vh = [v_ref[:, MLA_V * hh:MLA_V * (hh + 1)] for hh in range(2)]
        nt = (((1,), (1,)), ((), ()))

        def step(i, masked):
            r0 = pl.multiple_of(i * t, t)
            sd = []
            for hh in range(2):
                qh = q_ref[pl.ds(r0, t), LANES * hh:LANES * (hh + 1)]
                doh = do_ref[pl.ds(r0, t), MLA_V * hh:MLA_V * (hh + 1)]
                sc_t = lax.dot_general(kh[hh], qh, nt, preferred_element_type=F32)
                sd.append(jnp.where(causal_t, sc_t, NEG_INF) if masked else sc_t)
                sd.append(lax.dot_general(vh[hh], doh, nt, preferred_element_type=F32))
            for hh in range(2):
                lse = lse_ref[0, hh:hh + 1, pl.ds(r0, t)]
                dl = dl_ref[0, hh:hh + 1, pl.ds(r0, t)]
                p_t = jnp.exp2(sd[2 * hh] - lse)
                g_t = (p_t * (sd[2 * hh + 1] - dl)).astype(BF16)
                qth = qt_ref[LANES * hh:LANES * (hh + 1), pl.ds(r0, t)]
                doth = dot_ref[MLA_V * hh:MLA_V * (hh + 1), pl.ds(r0, t)]
                dv_ref[MLA_V * hh:MLA_V * (hh + 1), :] += lax.dot_general(
                    doth, p_t.astype(BF16), nt, preferred_element_type=F32)
                dk_ref[LANES * hh:LANES * (hh + 1), :] += lax.dot_general(
                    qth, g_t, nt, preferred_element_type=F32)
                dq_ref[LANES * hh:LANES * (hh + 1), pl.ds(r0, t)] += jnp.dot(
                    kth[hh], g_t, preferred_element_type=F32)

        step(j, True)

        def trip(i, carry):
            step(i, False)
            return carry

        lax.fori_loop(j + 1, nq, trip, 0)
        dk_ref[...] = dk_ref[...] * LN2

        @pl.when(j == nq - 1)
        def _():
            dq_ref[...] = dq_ref[...] * MLA_SCALE

    return pl.pallas_call(
        body, name="mla_attn_bwd", grid=(MLA_HEADS // 2, nq),
        in_specs=[pl.BlockSpec((s, 256), lambda p, j: (0, p)),
                  pl.BlockSpec((256, s), lambda p, j: (p, 0)),
                  pl.BlockSpec((t, 256), lambda p, j: (j, p)),
                  pl.BlockSpec((256, t), lambda p, j: (p, j)),
                  pl.BlockSpec((t, 128), lambda p, j: (j, p)),
                  pl.BlockSpec((s, 128), lambda p, j: (0, p)),
                  pl.BlockSpec((128, s), lambda p, j: (p, 0)),
                  pl.BlockSpec((1, 2, s), lambda p, j: (p, 0, 0)),
                  pl.BlockSpec((1, 2, s), lambda p, j: (p, 0, 0))],
        out_specs=[pl.BlockSpec((256, s), lambda p, j: (p, 0)),
                   pl.BlockSpec((256, t), lambda p, j: (p, j)),
                   pl.BlockSpec((128, t), lambda p, j: (p, j))],
        out_shape=[jax.ShapeDtypeStruct((1024, s), F32), jax.ShapeDtypeStruct((1024, s), F32),
                   jax.ShapeDtypeStruct((512, s), F32)],
        compiler_params=_params(("parallel", "arbitrary")),
    )(q2, q2t, k, kt, v, do, dot, lse_rows, delta_rows)


SWA_SCALE = SWA_DIM ** -0.5
SWA_COLS = SWA_GROUP * BLOCK
SWA_LOG2 = SWA_SCALE * LOG2E


def _swa_tables():
    k = np.arange(2 * BLOCK)[:, None]
    col = np.arange(SWA_COLS)[None, :]
    dist = BLOCK + (col % BLOCK) - k
    valid = (dist >= 0) & (dist < BLOCK)
    out = np.zeros((2, SWA_KV, 2 * BLOCK, SWA_COLS), np.float32)
    for first in range(2):
        ok = valid & ((k >= BLOCK) | (first == 0))
        for j in range(SWA_KV):
            slope = 2.0 ** -(SWA_GROUP * j + col // BLOCK + 1)
            out[first, j] = np.where(ok, -slope * dist * LOG2E, NEG_INF)
    return jnp.asarray(out)


def _swa_tile_inputs(sq_ref, skv_ref, halo_ref, qn_ref, kn_ref, sk_ref, add_ref, first):
    tokens = sq_ref.shape[0]
    kv_all = jnp.concatenate([halo_ref[...], skv_ref[...]], axis=0)
    kv_t = jnp.transpose(kv_all)
    sq_t = jnp.transpose(sq_ref[...])
    k_raw = [kv_t[SWA_DIM * j:SWA_DIM * (j + 1)] for j in range(SWA_KV)]
    v_t = [kv_t[128 + SWA_DIM * j:128 + SWA_DIM * (j + 1)] for j in range(SWA_KV)]
    v_nat = [kv_all[:, 128 + SWA_DIM * j:128 + SWA_DIM * (j + 1)] for j in range(SWA_KV)]
    q_raw = [sq_t[SWA_DIM * h:SWA_DIM * (h + 1)] for h in range(SWA_HEADS)]
    qn_b = jnp.broadcast_to(qn_ref[...], (SWA_DIM, tokens))
    kn_b = jnp.broadcast_to(kn_ref[...], (SWA_DIM, tokens + BLOCK))
    lane_grp = lax.broadcasted_iota(jnp.int32, (1, SWA_COLS), 1) // BLOCK
    sinks, adds = [], []
    for j in range(SWA_KV):
        row = jnp.zeros((1, SWA_COLS), F32)
        for g in range(SWA_GROUP):
            h = SWA_GROUP * j + g
            row = jnp.where(lane_grp == g, sk_ref[:, h:h + 1] * LOG2E, row)
        sinks.append(row)
        adds.append((jnp.where(first, add_ref[1, j], add_ref[0, j]), add_ref[0, j]))
    return k_raw, v_t, v_nat, q_raw, qn_b, kn_b, sinks, adds


def _swa_probs(kb, qs_t, add, sink):
    s2 = jnp.dot(kb, qs_t, preferred_element_type=F32) * SWA_LOG2 + add
    m = jnp.maximum(jnp.max(s2, axis=0, keepdims=True), sink)
    e = jnp.exp2(s2 - m)
    es = jnp.exp2(sink - m)
    inv = 1.0 / (jnp.sum(e, axis=0, keepdims=True) + es)
    return e, inv, es


def _swa_queries(qn_t, j, b):
    return jnp.concatenate([qn_t[SWA_GROUP * j + g][:, BLOCK * b:BLOCK * (b + 1)]
                            for g in range(SWA_GROUP)], axis=1)


def _swa_fwd(proj, qn, kn, sinks, tables):
    s = proj.shape[0]
    ts = min(1024, s)
    nb = ts // BLOCK

    def body(sq_ref, skv_ref, halo_ref, qn_ref, kn_ref, sk_ref, add_ref, o_ref, ot_ref):
        first = pl.program_id(0) == 0
        k_raw, v_t, _, q_raw, qn_b, kn_b, sink_rows, adds = _swa_tile_inputs(
            sq_ref, skv_ref, halo_ref, qn_ref, kn_ref, sk_ref, add_ref, first)
        kn_nat = [jnp.transpose(_rms0(k, kn_b)).astype(BF16) for k in k_raw]
        v_t = [v.astype(BF16) for v in v_t]
        qn_t = [_rms0(q, qn_b).astype(BF16) for q in q_raw]
        for b in range(nb):
            band = slice(BLOCK * b, BLOCK * (b + 2))
            for j in range(SWA_KV):
                e, inv, _ = _swa_probs(kn_nat[j][band], _swa_queries(qn_t, j, b),
                                       adds[j][0 if b == 0 else 1], sink_rows[j])
                o_t = jnp.dot(v_t[j][:, band], (e * inv).astype(BF16),
                              preferred_element_type=F32)
                for g in range(SWA_GROUP):
                    h = SWA_GROUP * j + g
                    ot_ref[SWA_DIM * h:SWA_DIM * (h + 1), BLOCK * b:BLOCK * (b + 1)] = (
                        o_t[:, BLOCK * g:BLOCK * (g + 1)])
        o_ref[...] = jnp.transpose(ot_ref[...])

    const = lambda shape: pl.BlockSpec(shape, lambda i: (0,) * len(shape))
    return pl.pallas_call(
        body, name="swa_fwd", grid=(s // ts,),
        in_specs=[pl.BlockSpec((ts, 512), lambda i: (i, OFF_SQ // 512)),
                  pl.BlockSpec((ts, 256), lambda i: (i, OFF_SKV // 256)),
                  pl.BlockSpec((BLOCK, 256), lambda i: (jnp.maximum(i * nb - 1, 0), OFF_SKV // 256)),
                  const((SWA_DIM, 1)), const((SWA_DIM, 1)), const((1, SWA_HEADS)),
                  const(tables.shape)],
        out_specs=pl.BlockSpec((ts, 512), lambda i: (i, 0)),
        out_shape=jax.ShapeDtypeStruct((s, 512), F32),
        scratch_shapes=[pltpu.VMEM((512, ts), F32)],
        compiler_params=_params(("parallel",)),
    )(proj, proj, proj, qn, kn, sinks, tables)


def _swa_bwd(proj, qn, kn, sinks, tables, do):
    s = proj.shape[0]
    ts = min(1024, s)
    nb = ts // BLOCK
    nt = s // ts

    def body(sq_ref, skv_ref, halo_ref, qn_ref, kn_ref, sk_ref, add_ref, do_ref,
             dsq_ref, dskv_ref, dqn_ref, dkn_ref, dsk_ref, carry_ref, dqt_ref, dkvt_ref):
        step = pl.program_id(0)
        first = step == nt - 1

        @pl.when(step == 0)
        def _():
            carry_ref[...] = jnp.zeros_like(carry_ref)
            dqn_ref[...] = jnp.zeros_like(dqn_ref)
            dkn_ref[...] = jnp.zeros_like(dkn_ref)
            dsk_ref[...] = jnp.zeros_like(dsk_ref)

        k_raw, v_t, v_nat, q_raw, qn_b, kn_b, sink_rows, adds = _swa_tile_inputs(
            sq_ref, skv_ref, halo_ref, qn_ref, kn_ref, sk_ref, add_ref, first)
        kn_f = [_rms0(k, kn_b) for k in k_raw]
        kn_t = [k.astype(BF16) for k in kn_f]
        kn_nat = [jnp.transpose(k).astype(BF16) for k in kn_f]
        v_nat = [v.astype(BF16) for v in v_nat]
        qn_t = [_rms0(q, qn_b).astype(BF16) for q in q_raw]
        do_t = jnp.transpose(do_ref[...].astype(F32)).astype(BF16)

        dkvt_ref[...] = jnp.zeros_like(dkvt_ref)
        dsink = [jnp.zeros((1, SWA_COLS), F32) for _ in range(SWA_KV)]
        nt_dims = (((1,), (1,)), ((), ()))
        for b in range(nb):
            rows = slice(BLOCK * b, BLOCK * (b + 1))
            band = slice(BLOCK * b, BLOCK * (b + 2))
            for j in range(SWA_KV):
                heads = [SWA_GROUP * j + g for g in range(SWA_GROUP)]
                qs_t = _swa_queries(qn_t, j, b)
                dos_t = jnp.concatenate([do_t[SWA_DIM * h:SWA_DIM * (h + 1), rows] for h in heads],
                                        axis=1)
                e, inv, es = _swa_probs(kn_nat[j][band], qs_t, adds[j][0 if b == 0 else 1],
                                        sink_rows[j])
                p = e * inv
                dp = jnp.dot(v_nat[j][band], dos_t, preferred_element_type=F32)
                dsum = jnp.sum(p * dp, axis=0, keepdims=True)
                dsink[j] = dsink[j] - es * inv * dsum
                g_t = (p * (dp - dsum) * SWA_SCALE).astype(BF16)
                dv_t = lax.dot_general(dos_t, p.astype(BF16), nt_dims,
                                       preferred_element_type=F32)
                dk_t = lax.dot_general(qs_t, g_t, nt_dims, preferred_element_type=F32)
                dq_t = jnp.dot(kn_t[j][:, band], g_t, preferred_element_type=F32)
                dkvt_ref[SWA_DIM * j:SWA_DIM * (j + 1), band] += dk_t
                dkvt_ref[128 + SWA_DIM * j:128 + SWA_DIM * (j + 1), band] += dv_t
                for g, h in enumerate(heads):
                    dqt_ref[SWA_DIM * h:SWA_DIM * (h + 1), rows] = dq_t[:, BLOCK * g:BLOCK * (g + 1)]

        dqn = jnp.zeros((SWA_DIM, 1), F32)
        for h in range(SWA_HEADS):
            _, vjp = jax.vjp(_rms0, q_raw[h], qn_ref[...])
            dq, dg = vjp(dqt_ref[SWA_DIM * h:SWA_DIM * (h + 1), :])
            dqt_ref[SWA_DIM * h:SWA_DIM * (h + 1), :] = dq
            dqn = dqn + dg
        dqn_ref[...] += dqn
        dsq_ref[...] = jnp.transpose(dqt_ref[...]).astype(BF16)
        dkn = jnp.zeros((SWA_DIM, 1), F32)
        lane_grp = lax.broadcasted_iota(jnp.int32, (1, SWA_COLS), 1) // BLOCK
        for j in range(SWA_KV):
            _, vjp = jax.vjp(_rms0, k_raw[j], kn_ref[...])
            dk, dg = vjp(dkvt_ref[SWA_DIM * j:SWA_DIM * (j + 1), :])
            dkvt_ref[SWA_DIM * j:SWA_DIM * (j + 1), :] = dk
            dkn = dkn + dg
            for g in range(SWA_GROUP):
                h = SWA_GROUP * j + g
                dsk_ref[:, h:h + 1] += jnp.sum(jnp.where(lane_grp == g, dsink[j], 0.0), axis=1,
                                               keepdims=True)
        dkn_ref[...] += dkn
        dkv = jnp.transpose(dkvt_ref[...])
        dskv_ref[0:ts - BLOCK, :] = dkv[BLOCK:ts].astype(BF16)
        dskv_ref[ts - BLOCK:ts, :] = (dkv[ts:ts + BLOCK] + carry_ref[...]).astype(BF16)
        carry_ref[...] = dkv[0:BLOCK]

    const = lambda shape: pl.BlockSpec(shape, lambda st: (0,) * len(shape))
    return pl.pallas_call(
        body, name="swa_bwd", grid=(nt,),
        in_specs=[pl.BlockSpec((ts, 512), lambda st: (nt - 1 - st, OFF_SQ // 512)),
                  pl.BlockSpec((ts, 256), lambda st: (nt - 1 - st, OFF_SKV // 256)),
                  pl.BlockSpec((BLOCK, 256),
                               lambda st: (jnp.maximum((nt - 1 - st) * nb - 1, 0), OFF_SKV // 256)),
                  const((SWA_DIM, 1)), const((SWA_DIM, 1)), const((1, SWA_HEADS)),
                  const(tables.shape),
                  pl.BlockSpec((ts, 512), lambda st: (nt - 1 - st, 0))],
        out_specs=[pl.BlockSpec((ts, 512), lambda st: (nt - 1 - st, 0)),
                   pl.BlockSpec((ts, 256), lambda st: (nt - 1 - st, 0)),
                   const((SWA_DIM, 1)), const((SWA_DIM, 1)), const((1, SWA_HEADS))],
        out_shape=[jax.ShapeDtypeStruct((s, 512), BF16), jax.ShapeDtypeStruct((s, 256), BF16),
                   jax.ShapeDtypeStruct((SWA_DIM, 1), F32), jax.ShapeDtypeStruct((SWA_DIM, 1), F32),
                   jax.ShapeDtypeStruct((1, SWA_HEADS), F32)],
        scratch_shapes=[pltpu.VMEM((BLOCK, 256), F32), pltpu.VMEM((512, ts), F32),
                        pltpu.VMEM((256, ts + BLOCK), F32)],
        compiler_params=_params(("arbitrary",)),
    )(proj, proj, proj, qn, kn, sinks, tables, do)


HALO = 8


def _shift_down(u, halo, k):
    tm = u.shape[0]
    rid = lax.broadcasted_iota(jnp.int32, u.shape, 0)
    out = pltpu.roll(u, k, 0)
    for r in range(k):
        out = jnp.where(rid == r, halo[HALO - k + r:HALO - k + r + 1, :], out)
    return out


def _shift_up(u, halo, k):
    tm = u.shape[0]
    rid = lax.broadcasted_iota(jnp.int32, u.shape, 0)
    out = pltpu.roll(u, tm - k, 0)
    for r in range(k):
        out = jnp.where(rid == tm - k + r, halo[r:r + 1, :], out)
    return out


def _conv_fwd_vals(conv_ref, convp_ref, cw_ref, is_first):
    c_h, c_b, c_c = conv_ref[:, 0:512], conv_ref[:, 512:1024], conv_ref[:, 1024:1536]
    u = c_c * c_h
    up = jnp.where(is_first, 0.0, convp_ref[:, 1024:1536] * convp_ref[:, 0:512])
    u1 = _shift_down(u, up, 1)
    u2 = _shift_down(u, up, 2)
    yc = cw_ref[0:1, :] * u2 + cw_ref[1:2, :] * u1 + cw_ref[2:3, :] * u
    return c_h, c_b, c_c, u, u1, u2, yc


def _out_fwd(proj, o_mla, o_swa, x, w_out, cw, target=None):
    s = proj.shape[0]
    tm = min(512, s)
    nt = s // tm
    with_loss = target is not None

    def body(*refs):
        conv_ref, convp_ref, gates_ref, om_ref, os_ref, x_ref, w_ref, cw_ref = refs[:8]
        if with_loss:
            t_ref, y_ref, zt_ref, loss_ref, z_ref = refs[8:]
        else:
            y_ref, zt_ref, z_ref = refs[8:]
        i = pl.program_id(0)
        _, c_b, _, _, _, _, yc = _conv_fwd_vals(conv_ref, convp_ref, cw_ref, i == 0)
        mix = (om_ref[...], c_b * yc, os_ref[...])
        for n in range(3):
            g = gates_ref[:, GROUP * n:GROUP * (n + 1)]
            z = mix[n] * (g * _sigmoid(g))
            z_ref[:, GROUP * n:GROUP * (n + 1)] = z.astype(BF16)
            zt_ref[GROUP * n:GROUP * (n + 1), :] = jnp.transpose(z).astype(BF16)
        y = x_ref[...] + jnp.dot(z_ref[...], w_ref[...], preferred_element_type=F32)
        if not with_loss:
            y_ref[...] = y
            return
        err = y - t_ref[...]
        y_ref[...] = err * (1.0 / D_MODEL)

        @pl.when(i == 0)
        def _():
            loss_ref[...] = jnp.zeros_like(loss_ref)

        sq = jnp.sum((err * err).reshape(tm // 8, 8, D_MODEL), axis=0)
        part = sq[:, 0:LANES]
        for c in range(1, D_MODEL // LANES):
            part = part + sq[:, LANES * c:LANES * (c + 1)]
        loss_ref[...] += part

        @pl.when(i == nt - 1)
        def _():
            loss_ref[...] = jnp.full(loss_ref.shape, (0.5 / D_MODEL) * jnp.sum(loss_ref[...]), F32)

    row = lambda width: pl.BlockSpec((tm, width), lambda i: (i, 0))
    in_specs = [pl.BlockSpec((tm, 1536), lambda i: (i, 0)),
                pl.BlockSpec((HALO, 1536), lambda i: (jnp.maximum(i * (tm // HALO) - 1, 0), 0)),
                pl.BlockSpec((tm, 1536), lambda i: (i, 1)),
                row(512), row(512), row(D_MODEL),
                pl.BlockSpec((D_MIX, D_MODEL), lambda i: (0, 0)),
                pl.BlockSpec((HALO, 512), lambda i: (0, 0))]
    out_specs = [row(D_MODEL), pl.BlockSpec((D_MIX, tm), lambda i: (0, i))]
    out_shape = [jax.ShapeDtypeStruct((s, D_MODEL), F32), jax.ShapeDtypeStruct((D_MIX, s), BF16)]
    operands = [proj, proj, proj, o_mla, o_swa, x, w_out, cw]
    if with_loss:
        in_specs.append(row(D_MODEL))
        out_specs.append(pl.BlockSpec((8, LANES), lambda i: (0, 0)))
        out_shape.append(jax.ShapeDtypeStruct((8, LANES), F32))
        operands.append(target)
    return pl.pallas_call(
        body, name="out_fwd_loss" if with_loss else "out_fwd", grid=(nt,),
        in_specs=in_specs, out_specs=out_specs, out_shape=out_shape,
        scratch_shapes=[pltpu.VMEM((tm, D_MIX), BF16)],
        compiler_params=_params(("arbitrary",) if with_loss else ("parallel",)),
    )(*operands)


def _out_bwd(dy, proj, o_mla, o_swa, w_out, cw):
    s = proj.shape[0]
    tm = min(512, s)
    nt = s // tm
    hb = tm // HALO

    def body(dy_ref, dyn_ref, conv_ref, convp_ref, convn_ref, gates_ref, gatesn_ref, om_ref, os_ref,
             w_ref, cw_ref,
             dconv_ref, dgates_ref, dom_ref, domt_ref, delta_ref, dos_ref, dcw_ref):
        i = pl.program_id(0)
        dz = _mm_nt(dy_ref[...], w_ref[...])

        def gate(n):
            g = gates_ref[:, GROUP * n:GROUP * (n + 1)]
            sg = _sigmoid(g)
            return g * sg, sg * (1.0 + g * (1.0 - sg))

        for n, o_ref, do_ref in ((0, om_ref, dom_ref), (2, os_ref, dos_ref)):
            silu, dsilu = gate(n)
            dzn = dz[:, GROUP * n:GROUP * (n + 1)]
            o = o_ref[...]
            do = dzn * silu
            do_ref[...] = do.astype(do_ref.dtype)
            dgates_ref[:, GROUP * n:GROUP * (n + 1)] = (dzn * o * dsilu).astype(BF16)
            if n == 0:
                domt_ref[...] = jnp.transpose(do).astype(BF16)
                t = do * o
                for h in range(MLA_HEADS):
                    delta_ref[:, h:h + 1] = jnp.sum(t[:, MLA_V * h:MLA_V * (h + 1)], axis=-1,
                                                    keepdims=True)

        c_h, c_b, c_c, u, u1, u2, yc = _conv_fwd_vals(conv_ref, convp_ref, cw_ref, i == 0)
        silu, dsilu = gate(1)
        dzc = dz[:, GROUP:2 * GROUP]
        dgates_ref[:, GROUP:2 * GROUP] = (dzc * (c_b * yc) * dsilu).astype(BF16)
        dycr = dzc * silu
        dyc = dycr * c_b
        gn = gatesn_ref[:, GROUP:2 * GROUP]
        dzc_n = _mm_nt(dyn_ref[...], w_ref[GROUP:2 * GROUP, :])
        dyc_n = jnp.where(i == nt - 1, 0.0, dzc_n * (gn * _sigmoid(gn)) * convn_ref[:, 512:1024])
        d1 = _shift_up(dyc, dyc_n, 1)
        d2 = _shift_up(dyc, dyc_n, 2)
        du = cw_ref[2:3, :] * dyc + cw_ref[1:2, :] * d1 + cw_ref[0:1, :] * d2
        dconv_ref[:, 0:512] = (du * c_c).astype(BF16)
        dconv_ref[:, 512:1024] = (dycr * yc).astype(BF16)
        dconv_ref[:, 1024:1536] = (du * c_h).astype(BF16)

        @pl.when(i == 0)
        def _():
            dcw_ref[...] = jnp.zeros_like(dcw_ref)

        for k, uk in enumerate((u2, u1, u)):
            dcw_ref[k:k + 1, :] += jnp.sum(dyc * uk, axis=0, keepdims=True)

    row = lambda width: pl.BlockSpec((tm, width), lambda i: (i, 0))
    prev = lambda i: jnp.maximum(i * hb - 1, 0)
    nxt = lambda i: jnp.minimum((i + 1) * hb, s // HALO - 1)
    return pl.pallas_call(
        body, name="out_bwd", grid=(nt,),
        in_specs=[row(D_MODEL),
                  pl.BlockSpec((HALO, D_MODEL), lambda i: (nxt(i), 0)),
                  pl.BlockSpec((tm, 1536), lambda i: (i, 0)),
                  pl.BlockSpec((HALO, 1536), lambda i: (prev(i), 0)),
                  pl.BlockSpec((HALO, 1536), lambda i: (nxt(i), 0)),
                  pl.BlockSpec((tm, 1536), lambda i: (i, 1)),
                  pl.BlockSpec((HALO, 1536), lambda i: (nxt(i), 1)),
                  row(512), row(512),
                  pl.BlockSpec((D_MIX, D_MODEL), lambda i: (0, 0)),
                  pl.BlockSpec((HALO, 512), lambda i: (0, 0))],
        out_specs=[row(1536), row(1536), row(512), pl.BlockSpec((512, tm), lambda i: (0, i)),
                   row(MLA_HEADS), row(512), pl.BlockSpec((HALO, 512), lambda i: (0, 0))],
        out_shape=[jax.ShapeDtypeStruct((s, 1536), BF16), jax.ShapeDtypeStruct((s, 1536), BF16),
                   jax.ShapeDtypeStruct((s, 512), BF16), jax.ShapeDtypeStruct((512, s), BF16),
                   jax.ShapeDtypeStruct((s, MLA_HEADS), F32),
                   jax.ShapeDtypeStruct((s, 512), BF16), jax.ShapeDtypeStruct((HALO, 512), F32)],
        compiler_params=_params(("arbitrary",)),
    )(dy, dy, proj, proj, proj, proj, proj, o_mla, o_swa, w_out, cw)


def _adam_update(g, w, m, v):
    c1 = 1.0 - ADAM_B1
    c2 = 1.0 - ADAM_B2
    bc1 = 1.0 - ADAM_B1 ** ADAM_STEP
    bc2 = 1.0 - ADAM_B2 ** ADAM_STEP
    m_new = ADAM_B1 * m + c1 * g
    v_new = ADAM_B2 * v + c2 * (g * g)
    delta = -ADAM_LR * ((m_new / bc1) / (jnp.sqrt(v_new / bc2) + ADAM_EPS) + ADAM_WD * w)
    return delta, m_new, v_new


def _adamw(g, w, m, v):
    rows = g.shape[0]
    tr = min(256, rows)

    def body(g_ref, w_ref, m_ref, v_ref, d_ref, mo_ref, vo_ref):
        d_ref[...], mo_ref[...], vo_ref[...] = _adam_update(g_ref[...], w_ref[...], m_ref[...],
                                                            v_ref[...])

    spec = pl.BlockSpec((tr, g.shape[1]), lambda i: (i, 0))
    return pl.pallas_call(
        body, name="adamw", grid=(rows // tr,),
        in_specs=[spec] * 4, out_specs=[spec] * 3,
        out_shape=[jax.ShapeDtypeStruct(g.shape, F32)] * 3,
        compiler_params=_params(("parallel",)),
    )(g, w, m, v)


def _adamw_small(gs, ws, ms, vs):
    n = len(gs)

    def body(*refs):
        ins, outs = refs[:4 * n], refs[4 * n:]
        for a in range(n):
            res = _adam_update(*(ins[kind * n + a][...] for kind in range(4)))
            for kind in range(3):
                outs[kind * n + a][...] = res[kind]

    vmem = pl.BlockSpec(memory_space=pltpu.VMEM)
    out = pl.pallas_call(
        body, name="adamw_small",
        in_specs=[vmem] * (4 * n), out_specs=[vmem] * (3 * n),
        out_shape=[jax.ShapeDtypeStruct(g.shape, F32) for _ in range(3) for g in gs],
    )(*gs, *ws, *ms, *vs)
    return out[:n], out[n:2 * n], out[2 * n:]


HBM_SPEC = pl.BlockSpec(memory_space=pltpu.HBM)


def _place():
    x, y, c = lax.axis_index("x"), lax.axis_index("y"), lax.axis_index("c")
    chips = [(1 - x, y), (x, 1 - y), (1 - x, 1 - y)]
    return x, y, c, chips


def _all_gather(shards):
    na = len(shards)
    halves = [sh.shape[0] // 2 for sh in shards]

    def body(*refs):
        w_refs, a_refs = refs[:na], refs[na:2 * na]
        send_sems, recv_sems = refs[2 * na:]
        x, y, c, chips = _place()
        k = 2 * x + y
        sib = (x, y, 1 - c)

        def slab(a, kk, hc):
            return a_refs[a].at[kk, pl.ds(hc * halves[a], halves[a]), :]

        def copy(a, n, src, dst, to):
            return pltpu.make_async_remote_copy(
                src_ref=src, dst_ref=dst, send_sem=send_sems.at[6 * a + n],
                recv_sem=recv_sems.at[6 * a + n], device_id=to, device_id_type=MESH)

        first = [copy(a, n, w_refs[a].at[pl.ds(c * halves[a], halves[a]), :], slab(a, k, c),
                      (cx, cy, c))
                 for n, (cx, cy) in enumerate(chips) for a in range(na)]
        for cp in first:
            cp.start()
        passed = []
        for n, (cx, cy) in enumerate(chips):
            kk = 2 * cx + cy
            for a in range(na):
                copy(a, n, slab(a, kk, c), slab(a, kk, c), (cx, cy, c)).wait_recv()
                fwd = copy(a, 3 + n, slab(a, kk, c), slab(a, kk, c), sib)
                fwd.start()
                passed.append(fwd)
        for n, (cx, cy) in enumerate(chips):
            kk = 2 * cx + cy
            for a in range(na):
                copy(a, 3 + n, slab(a, kk, 1 - c), slab(a, kk, 1 - c), sib).wait_recv()
        for cp in first + passed:
            cp.wait_send()

    return pl.pallas_call(
        body, name="weights_all_gather",
        in_specs=[HBM_SPEC] * na, out_specs=[HBM_SPEC] * na,
        out_shape=[jax.ShapeDtypeStruct((N_CHIPS,) + sh.shape, sh.dtype) for sh in shards],
        scratch_shapes=[pltpu.SemaphoreType.DMA((6 * na,)), pltpu.SemaphoreType.DMA((6 * na,))],
    )(*shards)


def _fill_own_slab(buf, src, k_idx):
    n, rows, cols = buf.shape
    tr = _row_tile(rows)
    slabs = src.ndim == 3

    def body(k_ref, src_ref, buf_ref, out_ref):
        out_ref[0] = src_ref[0] if slabs else src_ref[...]

    if slabs:
        src_spec = pl.BlockSpec((1, tr, cols), lambda t, k_ref: (k_ref[0], t, 0))
    else:
        src_spec = pl.BlockSpec((tr, cols), lambda t, k_ref: (t, 0))
    return pl.pallas_call(
        body, name="fill_own_slab",
        grid_spec=pltpu.PrefetchScalarGridSpec(
            num_scalar_prefetch=1, grid=(rows // tr,),
            in_specs=[src_spec, pl.BlockSpec(memory_space=pl.ANY)],
            out_specs=pl.BlockSpec((1, tr, cols), lambda t, k_ref: (k_ref[0], t, 0))),
        out_shape=jax.ShapeDtypeStruct(buf.shape, buf.dtype),
        input_output_aliases={2: 0},
        compiler_params=_params(("parallel",)),
    )(k_idx, src, buf)


SEM_SPEC = pl.BlockSpec(memory_space=pltpu.SEMAPHORE)
LATE_COPIES = 6


def _late_copy(a, j, peer_core, src, dst, send_sems, recv_sems, to, sender_core):
    return pltpu.make_async_remote_copy(
        src_ref=src, dst_ref=dst, send_sem=send_sems.at[LATE_COPIES * a + 2 * j + peer_core],
        recv_sem=recv_sems.at[LATE_COPIES * a + 2 * j + sender_core], device_id=to,
        device_id_type=MESH)


def _gather_start(shards):
    na = len(shards)

    def body(*refs):
        w_refs, land_refs = refs[:na], refs[na:2 * na]
        send_sems, recv_sems = refs[2 * na], refs[2 * na + 1]
        token = refs[-1]
        x, y, c, chips = _place()
        k = 2 * x + y
        for a in range(na):
            half = w_refs[a].shape[0] // 2
            src = w_refs[a].at[pl.ds(c * half, half), :]
            dst = land_refs[a].at[k, pl.ds(c * half, half), :]
            for j, (cx, cy) in enumerate(chips):
                for tc in range(2):
                    _late_copy(a, j, tc, src, dst, send_sems, recv_sems, (cx, cy, tc), c).start()
        token[...] = jnp.zeros_like(token)

    lands = [pltpu.with_memory_space_constraint(lax.empty((N_CHIPS,) + sh.shape, sh.dtype), pltpu.HBM)
             for sh in shards]
    srcs = [pltpu.with_memory_space_constraint(sh, pltpu.HBM) for sh in shards]
    sems = pltpu.SemaphoreType.DMA((LATE_COPIES * na,))
    aliases = {a: 2 + a for a in range(2 * na)}
    return pl.pallas_call(
        body, name="late_weights_gather_start",
        in_specs=[HBM_SPEC] * (2 * na),
        out_specs=[SEM_SPEC, SEM_SPEC] + [HBM_SPEC] * (2 * na) + [pl.BlockSpec(memory_space=pltpu.VMEM)],
        out_shape=[sems, sems] + [pltpu.HBM(v.shape, v.dtype) for v in srcs + lands]
        + [jax.ShapeDtypeStruct((8, LANES), F32)],
        input_output_aliases=aliases,
        compiler_params=pltpu.CompilerParams(
            has_side_effects=pltpu.SideEffectType.DATAFLOW_SIDE_EFFECTING),
    )(*srcs, *lands)


def _gather_wait(started, na, after):
    send_sems, recv_sems = started[0], started[1]
    bufs = started[2:2 + 2 * na]

    def body(*refs):
        w_refs, land_refs = refs[:na], refs[na:2 * na]
        send_sems, recv_sems = refs[2 * na], refs[2 * na + 1]
        x, y, c, chips = _place()
        k = 2 * x + y
        for a in range(na):
            half = w_refs[a].shape[0] // 2
            src = w_refs[a].at[pl.ds(c * half, half), :]
            for j, (cx, cy) in enumerate(chips):
                kk = 2 * cx + cy
                for pc in range(2):
                    _late_copy(a, j, pc, src, land_refs[a].at[k, pl.ds(c * half, half), :],
                               send_sems, recv_sems, (cx, cy, pc), c).wait_send()
                    pltpu.make_async_remote_copy(
                        src_ref=src, dst_ref=land_refs[a].at[kk, pl.ds(pc * half, half), :],
                        send_sem=send_sems.at[LATE_COPIES * a + 2 * j + pc],
                        recv_sem=recv_sems.at[LATE_COPIES * a + 2 * j + pc],
                        device_id=(cx, cy, pc), device_id_type=MESH).wait_recv()

    out = pl.pallas_call(
        body, name="late_weights_gather_wait",
        in_specs=[HBM_SPEC] * (2 * na) + [SEM_SPEC, SEM_SPEC, pl.BlockSpec(memory_space=pl.ANY)],
        out_specs=[HBM_SPEC] * (2 * na),
        out_shape=[pltpu.HBM(v.shape, v.dtype) for v in bufs],
        input_output_aliases={a: a for a in range(2 * na)},
        compiler_params=pltpu.CompilerParams(
            has_side_effects=pltpu.SideEffectType.DATAFLOW_SIDE_EFFECTING),
    )(*bufs, send_sems, recv_sems, after)
    return out[na:]


def _swap_halves_to_sibling(gs):
    na = len(gs)

    def body(*refs):
        g_refs, r_refs = refs[:na], refs[na:2 * na]
        send_sems, recv_sems = refs[2 * na:]
        x, y, c, _ = _place()
        cps = []
        for a in range(na):
            half = g_refs[a].shape[1] // 2
            cps.append(pltpu.make_async_remote_copy(
                src_ref=g_refs[a].at[:, pl.ds((1 - c) * half, half), :], dst_ref=r_refs[a],
                send_sem=send_sems.at[a], recv_sem=recv_sems.at[a], device_id=(x, y, 1 - c),
                device_id_type=MESH))
        for cp in cps:
            cp.start()
        for cp in cps:
            cp.wait()

    return pl.pallas_call(
        body, name="grads_to_sibling",
        in_specs=[HBM_SPEC] * na, out_specs=[HBM_SPEC] * na,
        out_shape=[jax.ShapeDtypeStruct((g.shape[0], g.shape[1] // 2, g.shape[2]), g.dtype)
                   for g in gs],
        scratch_shapes=[pltpu.SemaphoreType.DMA((na,)), pltpu.SemaphoreType.DMA((na,))],
    )(*gs)


def _row_tile(rows):
    return 256 if rows % 256 == 0 else 128


def _add_sibling(g, r, c_idx, out_dtype):
    n, rows, cols = g.shape
    half = rows // 2
    tr = _row_tile(half)
    nb = half // tr

    def body(c_ref, g_ref, r_ref, p_ref):
        p_ref[...] = (g_ref[...] + r_ref[...]).astype(out_dtype)

    return pl.pallas_call(
        body, name="grads_add_sibling",
        grid_spec=pltpu.PrefetchScalarGridSpec(
            num_scalar_prefetch=1, grid=(n, nb),
            in_specs=[pl.BlockSpec((1, tr, cols), lambda j, t, c_ref: (j, c_ref[0] * nb + t, 0)),
                      pl.BlockSpec((1, tr, cols), lambda j, t, c_ref: (j, t, 0))],
            out_specs=pl.BlockSpec((1, tr, cols), lambda j, t, c_ref: (j, t, 0))),
        out_shape=jax.ShapeDtypeStruct((n, half, cols), out_dtype),
        compiler_params=_params(("parallel", "parallel")),
    )(c_idx, g, r)


def _scatter_to_chips(ps):
    na = len(ps)

    def body(*refs):
        p_refs, q_refs = refs[:na], refs[na:2 * na]
        send_sems, recv_sems = refs[2 * na:]
        x, y, c, chips = _place()
        k = 2 * x + y
        sends = []
        for i, (cx, cy) in enumerate(chips):
            for a in range(na):
                cp = pltpu.make_async_remote_copy(
                    src_ref=p_refs[a].at[2 * cx + cy], dst_ref=q_refs[a].at[k],
                    send_sem=send_sems.at[3 * a + i], recv_sem=recv_sems.at[3 * a + i],
                    device_id=(cx, cy, c), device_id_type=MESH)
                cp.start()
                sends.append(cp)
        for i, (cx, cy) in enumerate(chips):
            kk = 2 * cx + cy
            for a in range(na):
                pltpu.make_async_remote_copy(
                    src_ref=p_refs[a].at[kk], dst_ref=q_refs[a].at[kk],
                    send_sem=send_sems.at[3 * a + i], recv_sem=recv_sems.at[3 * a + i],
                    device_id=(cx, cy, c), device_id_type=MESH).wait_recv()
        for cp in sends:
            cp.wait_send()

    return pl.pallas_call(
        body, name="grads_scatter_to_chips",
        in_specs=[HBM_SPEC] * na, out_specs=[HBM_SPEC] * na,
        out_shape=[jax.ShapeDtypeStruct(p.shape, p.dtype) for p in ps],
        scratch_shapes=[pltpu.SemaphoreType.DMA((3 * na,)), pltpu.SemaphoreType.DMA((3 * na,))],
    )(*ps)


def _sum_chips(q, c_idx):
    n, half, cols = q.shape
    tr = _row_tile(half)
    nb = half // tr

    def body(c_ref, q_ref, o_ref):
        parts = [q_ref[kk].astype(F32) for kk in range(n)]
        o_ref[...] = ((parts[0] + parts[1]) + parts[2]) + parts[3]

    return pl.pallas_call(
        body, name="grads_sum_chips",
        grid_spec=pltpu.PrefetchScalarGridSpec(
            num_scalar_prefetch=1, grid=(nb,),
            in_specs=[pl.BlockSpec((n, tr, cols), lambda t, c_ref: (0, t, 0))],
            out_specs=pl.BlockSpec((tr, cols), lambda t, c_ref: (c_ref[0] * nb + t, 0))),
        out_shape=jax.ShapeDtypeStruct((2 * half, cols), F32),
        compiler_params=_params(("parallel",)),
    )(c_idx, q)


def _join_halves(fulls):
    na = len(fulls)

    def body(*refs):
        o_refs = refs[na:2 * na]
        send_sems, recv_sems = refs[2 * na:]
        x, y, c, _ = _place()
        sends = []
        for a in range(na):
            half = o_refs[a].shape[0] // 2
            rows = o_refs[a].at[pl.ds(c * half, half), :]
            sends.append(pltpu.make_async_remote_copy(
                src_ref=rows, dst_ref=rows, send_sem=send_sems.at[a], recv_sem=recv_sems.at[a],
                device_id=(x, y, 1 - c), device_id_type=MESH))
        for cp in sends:
            cp.start()
        for a in range(na):
            half = o_refs[a].shape[0] // 2
            other = o_refs[a].at[pl.ds((1 - c) * half, half), :]
            pltpu.make_async_remote_copy(
                src_ref=other, dst_ref=other, send_sem=send_sems.at[a], recv_sem=recv_sems.at[a],
                device_id=(x, y, 1 - c), device_id_type=MESH).wait_recv()
        for cp in sends:
            cp.wait_send()

    return pl.pallas_call(
        body, name="grads_join_halves",
        in_specs=[HBM_SPEC] * na, out_specs=[HBM_SPEC] * na,
        out_shape=[jax.ShapeDtypeStruct(f.shape, f.dtype) for f in fulls],
        input_output_aliases={a: a for a in range(na)},
        scratch_shapes=[pltpu.SemaphoreType.DMA((na,)), pltpu.SemaphoreType.DMA((na,))],
    )(*fulls)


def _part_rows(shape):
    size = 1
    for d in shape:
        size *= d
    rows = -(-size // PACK_COLS)
    return size, -(-rows // PACK_ROW_ALIGN) * PACK_ROW_ALIGN


def _pack_rows(arrays, dtype, total_rows):
    parts, used = [], 0
    for a in arrays:
        size, rows = _part_rows(a.shape)
        flat = a.reshape(-1).astype(dtype)
        parts.append(jnp.pad(flat, (0, rows * PACK_COLS - size)).reshape(rows, PACK_COLS))
        used += rows
    parts.append(jnp.zeros((total_rows - used, PACK_COLS), dtype))
    return jnp.concatenate(parts, axis=0)


def _unpack_rows(buf, shapes):
    lead = buf.shape[:-2]
    out, off = [], 0
    for sh in shapes:
        size, rows = _part_rows(sh)
        part = buf[..., off:off + rows, :].reshape(lead + (-1,))[..., :size]
        out.append(part.reshape(lead + tuple(sh)))
        off += rows
    return out


NEW_ORDER = ((928, 1440), (1440, 1952), (1952, 2464), (416, 928), (2464, 2976), (3744, 4256),
             (2976, 3488), (0, 256), (256, 384), (4256, 4320), (384, 416), (4256, 4288),
             (3488, 3616), (3616, 3744))
OLD_ORDER = ((3584, 3840), (3840, 3968), (4032, 4064), (1536, 2048), (0, 512), (512, 1024),
             (1024, 1536), (2048, 2560), (3072, 3584), (4096, 4224), (4224, 4352), (2560, 3072))


def _cols(sources, ranges):
    parts = []
    for a, b in ranges:
        off = 0
        for src in sources:
            width = src.shape[-1]
            lo, hi = max(a, off), min(b, off + width)
            if lo < hi:
                parts.append(src[..., lo - off:hi - off])
            off += width
    return jnp.concatenate(parts, axis=-1)


def _sub_ranges(ranges, a, b):
    out, off = [], 0
    for lo, hi in ranges:
        width = hi - lo
        s0, s1 = max(a, off), min(b, off + width)
        if s0 < s1:
            out.append((lo + s0 - off, lo + s1 - off))
        off += width
    return out


def _rope_tables(s):
    half = MLA_ROPE // 2
    inv_freq = jnp.power(jnp.float32(ROPE_THETA), -jnp.arange(half, dtype=F32) / half)
    ang = inv_freq[:, None] * jnp.arange(s, dtype=F32)[None, :]
    cos, sin = jnp.cos(ang), jnp.sin(ang)
    z = lambda n: jnp.zeros((n, s), F32)
    c = jnp.concatenate([jnp.ones((MLA_NOPE, s), F32), cos, cos, z(32)], axis=0)
    s1 = jnp.concatenate([z(MLA_NOPE), -sin, z(16), z(32)], axis=0)
    s2 = jnp.concatenate([z(MLA_NOPE), z(16), sin, z(32)], axis=0)
    return c, s1, s2


def _pad_lanes(a, n):
    return jnp.pad(a, ((0, 0), (0, n - a.shape[1])))


SHARDED = ("w_in", "w_out", "mla_w_qb", "mla_w_kvb", "conv_w")
REPLICATED = ("norm_g", "mla_q_a_norm", "mla_kv_a_norm", "mla_q_norm", "mla_k_norm",
              "swa_q_norm", "swa_k_norm", "swa_sinks")
WEIGHT_ORDER = ("norm_g", "w_in", "mla_q_a_norm", "mla_w_qb", "mla_kv_a_norm", "mla_w_kvb",
                "mla_q_norm", "mla_k_norm", "conv_w", "swa_q_norm", "swa_k_norm", "swa_sinks", "w_out")
SHARD_AXIS = {"w_in": 2, "w_out": 1, "mla_w_qb": 2, "mla_w_kvb": 2, "conv_w": 2}


def kernel(x, norm_g, w_in, mla_q_a_norm, mla_w_qb, mla_kv_a_norm, mla_w_kvb, mla_q_norm, mla_k_norm, conv_w, swa_q_norm, swa_k_norm, swa_sinks, w_out, loss_target, m_norm_g, m_w_in, m_mla_q_a_norm, m_mla_w_qb, m_mla_kv_a_norm, m_mla_w_kvb, m_mla_q_norm, m_mla_k_norm, m_conv_w, m_swa_q_norm, m_swa_k_norm, m_swa_sinks, m_w_out, v_norm_g, v_w_in, v_mla_q_a_norm, v_mla_w_qb, v_mla_kv_a_norm, v_mla_w_kvb, v_mla_q_norm, v_mla_k_norm, v_conv_w, v_swa_q_norm, v_swa_k_norm, v_swa_sinks, v_w_out):
    weights = dict(norm_g=norm_g, w_in=w_in, mla_q_a_norm=mla_q_a_norm, mla_w_qb=mla_w_qb,
                   mla_kv_a_norm=mla_kv_a_norm, mla_w_kvb=mla_w_kvb, mla_q_norm=mla_q_norm,
                   mla_k_norm=mla_k_norm, conv_w=conv_w, swa_q_norm=swa_q_norm,
                   swa_k_norm=swa_k_norm, swa_sinks=swa_sinks, w_out=w_out)
    mom_m = dict(norm_g=m_norm_g, w_in=m_w_in, mla_q_a_norm=m_mla_q_a_norm, mla_w_qb=m_mla_w_qb,
                 mla_kv_a_norm=m_mla_kv_a_norm, mla_w_kvb=m_mla_w_kvb, mla_q_norm=m_mla_q_norm,
                 mla_k_norm=m_mla_k_norm, conv_w=m_conv_w, swa_q_norm=m_swa_q_norm,
                 swa_k_norm=m_swa_k_norm, swa_sinks=m_swa_sinks, w_out=m_w_out)
    mom_v = dict(norm_g=v_norm_g, w_in=v_w_in, mla_q_a_norm=v_mla_q_a_norm, mla_w_qb=v_mla_w_qb,
                 mla_kv_a_norm=v_mla_kv_a_norm, mla_w_kvb=v_mla_w_kvb, mla_q_norm=v_mla_q_norm,
                 mla_k_norm=v_mla_k_norm, conv_w=v_conv_w, swa_q_norm=v_swa_q_norm,
                 swa_k_norm=v_swa_k_norm, swa_sinks=v_swa_sinks, w_out=v_w_out)
    xs = x[0]
    target = loss_target[0]
    s = xs.shape[0]
    c_idx = lax.axis_index("c").astype(jnp.int32).reshape(1)
    k_idx = (2 * lax.axis_index("x") + lax.axis_index("y")).astype(jnp.int32).reshape(1)

    conv_bits = lax.bitcast_convert_type(conv_w, BF16)
    small_list = [mla_w_qb, mla_w_kvb, conv_bits]
    shard_cols = w_in.shape[2]
    w_in_b = w_in.astype(BF16)
    late = [w_in_b[1], w_out.astype(BF16).reshape(-1, D_MODEL)]
    started = _gather_start(late)
    own = [w_in_b[0], _pack_rows(small_list, BF16, SMALL_ROWS)]
    gathered_in0, gathered_small = [_fill_own_slab(buf, src, k_idx)
                                    for buf, src in zip(_all_gather(own), own)]
    parts = _unpack_rows(gathered_small, [a.shape for a in small_list])
    join = lambda p, axis: jnp.concatenate([p[k] for k in range(N_CHIPS)], axis=axis)
    w_in_zeros = jnp.zeros((D_MODEL, 64), BF16)
    permuted = lambda slabs: _cols([slabs[k] for k in range(N_CHIPS)] + [w_in_zeros], NEW_ORDER)
    w_qb_full = join(parts[0], 2)
    w_kvb_full = join(parts[1], 2)
    conv_full = lax.bitcast_convert_type(join(parts[2], 2), F32)

    rope = _rope_tables(s)
    swa_tables = _swa_tables()
    layers = []
    for l in range(DEPTH):
        wq = jnp.pad(w_qb_full[l].reshape(MLA_Q_LORA, MLA_HEADS, MLA_QK),
                     ((0, 0), (0, 0), (0, LANES - MLA_QK))).reshape(MLA_Q_LORA, MLA_HEADS * LANES)
        kv = w_kvb_full[l].reshape(MLA_KV_LORA, MLA_HEADS, MLA_NOPE + MLA_V)
        wk = jnp.pad(kv[:, :, :MLA_NOPE], ((0, 0), (0, 0), (0, LANES - MLA_NOPE)))
        wkv = jnp.concatenate([wk.reshape(MLA_KV_LORA, MLA_HEADS * LANES),
                               kv[:, :, MLA_NOPE:].reshape(MLA_KV_LORA, MLA_HEADS * MLA_V)], axis=1)
        layers.append(dict(
            wq=wq, wkv=wkv,
            cw=jnp.pad(conv_full[l], ((0, HALO - 3), (0, 0))),
            g=norm_g[l][None],
            mla_norms=(mla_q_a_norm[l][:, None], mla_kv_a_norm[l][:, None],
                       _pad_lanes(mla_q_norm[l][None], LANES).T, _pad_lanes(mla_k_norm[l][None], LANES).T),
            mla_weights=(wq, wkv, wq.T, wkv.T),
            sqn=swa_q_norm[l][:, None], skn=swa_k_norm[l][:, None], sinks=swa_sinks[l][None]))

    saved = []
    h_in = xs
    layers[0]["w_in"] = permuted(gathered_in0)
    layers[0]["g"] = layers[0]["g"] + started[-1][0:1, 0:1]
    for l in range(DEPTH):
        p = layers[l]
        proj, hb = _in_proj_fwd(h_in, p["g"], p["w_in"])
        q, k, v, qt, kt, vt = _mla_prep_fwd(proj, p["mla_norms"], p["mla_weights"], rope)
        o_mla, lse = _mla_attn_fwd(q, k, vt)
        if l == 0:
            late_in1, late_out = [_fill_own_slab(buf, src, k_idx) for buf, src in
                                  zip(_gather_wait(started, len(late), o_mla), late)]
            layers[1]["w_in"] = permuted(late_in1)
            w_out_full = join(late_out.reshape(N_CHIPS, DEPTH, -1, D_MODEL), 1)
            for n in range(DEPTH):
                layers[n]["w_out"] = w_out_full[n]
        o_swa = _swa_fwd(proj, p["sqn"], p["skn"], p["sinks"], swa_tables)
        last = l == DEPTH - 1
        y, z, *loss_acc = _out_fwd(proj, o_mla, o_swa, h_in, p["w_out"], p["cw"],
                                   target if last else None)
        saved.append(dict(x=h_in, proj=proj, hb=hb, q=q, k=k, v=v, qt=qt, kt=kt, o_mla=o_mla, lse=lse,
                          o_swa=o_swa, z=z))
        h_in = y

    dy, loss_acc = h_in, loss_acc[0]
    loss = lax.psum(loss_acc[0, 0], ("x", "y", "c"))

    grads = {n: [None] * DEPTH for n in WEIGHT_ORDER}
    for l in reversed(range(DEPTH)):
        p, a = layers[l], saved[l]
        dconv, dgates, do_mla, do_mla_t, delta, do_swa, dcw = _out_bwd(dy, a["proj"], a["o_mla"], a["o_swa"],
                                                             p["w_out"], p["cw"])
        grads["w_out"][l] = _weight_grads(a["z"], [dy], "dw_out")[0]
        grads["conv_w"][l] = dcw[0:3]
        delta_rows = jnp.transpose(delta, (1, 0)).reshape(MLA_HEADS // 2, 2, s)
        dq, dk, dv = _mla_attn_bwd(a["q"], a["qt"], a["k"], a["kt"], a["v"], do_mla, do_mla_t,
                                   a["lse"], delta_rows)
        dmla, dqan, dkvan, dqn, dkn, dwq_t, dwkv_t = _mla_prep_bwd(
            a["proj"], p["mla_norms"], p["mla_weights"], rope, dq, dk, dv)
        dwq, dwkv = dwq_t.T, dwkv_t.T
        dsq, dskv, dsqn, dskn, dsinks = _swa_bwd(a["proj"], p["sqn"], p["skn"], p["sinks"], swa_tables, do_swa)
        pieces = [dconv, dgates, dsq, dmla, dskv]
        dx, dg = _in_proj_bwd(pieces, a["x"], p["g"], p["w_in"], dy)
        grads["w_in"][l] = _weight_grads(a["hb"], pieces, "dw_in")
        grads["norm_g"][l] = dg[0]
        grads["mla_q_a_norm"][l] = dqan[:, 0]
        grads["mla_kv_a_norm"][l] = dkvan[:, 0]
        grads["mla_q_norm"][l] = dqn[:MLA_QK, 0]
        grads["mla_k_norm"][l] = dkn[:MLA_QK, 0]
        grads["mla_w_qb"][l] = dwq.reshape(MLA_Q_LORA, MLA_HEADS, LANES)[:, :, :MLA_QK].reshape(
            MLA_Q_LORA, MLA_HEADS * MLA_QK)
        dwk = dwkv[:, :MLA_HEADS * LANES].reshape(MLA_KV_LORA, MLA_HEADS, LANES)[:, :, :MLA_NOPE]
        dwv = dwkv[:, MLA_HEADS * LANES:].reshape(MLA_KV_LORA, MLA_HEADS, MLA_V)
        grads["mla_w_kvb"][l] = jnp.concatenate([dwk, dwv], axis=2).reshape(
            MLA_KV_LORA, MLA_HEADS * (MLA_NOPE + MLA_V))
        grads["swa_q_norm"][l] = dsqn[:, 0]
        grads["swa_k_norm"][l] = dskn[:, 0]
        grads["swa_sinks"][l] = dsinks[0]
        dy = dx
    grad_x = dy[None]
    full_grads = {n: jnp.stack(grads[n]) for n in WEIGHT_ORDER if n != "w_in"}

    rest = tuple(n for n in SHARDED if n not in ("w_in", "w_out"))
    rep_shapes = [weights[n].shape for n in REPLICATED]
    rep_grads = jnp.concatenate([full_grads[n].reshape(-1) for n in REPLICATED])

    def chunk(g, n, k):
        width = g.shape[SHARD_AXIS[n]] // N_CHIPS
        return lax.slice_in_dim(g, k * width, (k + 1) * width, axis=SHARD_AXIS[n])

    g_in = jnp.stack([
        jnp.concatenate([_cols(grads["w_in"][l], _sub_ranges(OLD_ORDER, k * shard_cols,
                                                             (k + 1) * shard_cols))
                         for l in range(DEPTH)], axis=0)
        for k in range(N_CHIPS)])
    g_out = jnp.stack([chunk(full_grads["w_out"], "w_out", k).reshape(-1, D_MODEL)
                       for k in range(N_CHIPS)])
    g_small = jnp.stack([_pack_rows([chunk(full_grads[n], n, k) for n in rest] + [rep_grads],
                                    F32, SMALL_ROWS) for k in range(N_CHIPS)])
    gs = (g_in, g_out, g_small)
    from_sibling = _swap_halves_to_sibling(gs)
    partial = [_add_sibling(g, r, c_idx, dt) for g, r, dt in zip(gs, from_sibling, (BF16, BF16, F32))]
    by_chip = [_fill_own_slab(q, p, k_idx) for q, p in zip(_scatter_to_chips(partial), partial)]
    g_in_mine, g_out_mine, g_small_mine = _join_halves([_sum_chips(q, c_idx) for q in by_chip])

    vals = _unpack_rows(g_small_mine, [weights[n].shape for n in rest] + [(rep_grads.shape[0],)])
    grad = dict(zip(rest, vals[:-1]))
    grad["w_in"] = g_in_mine.reshape(w_in.shape)
    grad["w_out"] = g_out_mine.reshape(w_out.shape)
    off = 0
    for n, sh in zip(REPLICATED, rep_shapes):
        grad[n] = vals[-1][off:off + sh[0] * sh[1]].reshape(sh)
        off += sh[0] * sh[1]
    results = {}
    for n in ("w_in", "w_out"):
        view = lambda a, n=n: a.reshape(-1, weights[n].shape[-1])
        res = _adamw(view(grad[n]), view(weights[n]), view(mom_m[n]), view(mom_v[n]))
        results[n] = [r.reshape(weights[n].shape) for r in res]
    small = tuple(n for n in WEIGHT_ORDER if n not in results)
    res = _adamw_small(*([d[n] for n in small] for d in (grad, weights, mom_m, mom_v)))
    for a, n in enumerate(small):
        results[n] = [res[kind][a] for kind in range(3)]
    unpacked = [grad] + [{n: results[n][kind] for n in WEIGHT_ORDER} for kind in range(3)]
    outs = [loss, grad_x]
    for group in unpacked:
        outs += [group[n] for n in WEIGHT_ORDER]
    return tuple(outs)
```

```python
import jax
import numpy as np
import jax.numpy as jnp
from jax import lax
from jax.experimental import pallas as pl
from jax.experimental.pallas import tpu as pltpu

F32 = jnp.float32
BF16 = jnp.bfloat16

D_MODEL = 1024
DEPTH = 2
GROUP = 512
D_MIX = 3 * GROUP
BLOCK = 128
RMS_EPS = 1e-6
NEG_INF = -1e30
MLA_HEADS = 8
MLA_QK = 96
MLA_NOPE = 64
MLA_ROPE = 32
MLA_V = 64
V_AUG = 80
MLA_Q_LORA = 256
MLA_KV_LORA = 128
ROPE_THETA = 10000.0
SWA_HEADS = 8
SWA_KV = 2
SWA_GROUP = 4
SWA_DIM = 64
N_CHIPS = 4

NC = 4352
OFF_SQ, OFF_MLA, OFF_SKV = 3072, 3584, 4096

VMEM_LIMIT = 56 * 1024 * 1024
LANES = 128
PACK_COLS = 1024
PACK_ROW_ALIGN = 16
SMALL_ROWS = 256

ADAM_LR = 0.001
ADAM_B1 = 0.9
ADAM_B2 = 0.999
ADAM_EPS = 1e-08
ADAM_WD = 0.01
ADAM_STEP = 10

MESH = pl.DeviceIdType.MESH


def _params(sem, vmem=VMEM_LIMIT):
    return pltpu.CompilerParams(dimension_semantics=sem, vmem_limit_bytes=vmem)


def _dot(a, b, dims):
    return lax.dot_general(a.astype(BF16), b.astype(BF16), (dims, ((), ())),
                           preferred_element_type=F32)


def _mm(a, b):
    return _dot(a, b, ((1,), (0,)))


def _mm_nt(a, b):
    return _dot(a, b, ((1,), (1,)))


def _rms(x, g, n=None):
    n = x.shape[-1] if n is None else n
    ms = jnp.sum(x * x, axis=-1, keepdims=True) * (1.0 / n)
    return x * lax.rsqrt(ms + RMS_EPS) * g


def _sigmoid(x):
    return 1.0 / (1.0 + jnp.exp(-x))


def _in_proj_fwd(x, g, w):
    s = x.shape[0]
    tm = min(512, s)

    def body(x_ref, g_ref, w_ref, proj_ref, hbt_ref):
        h = _rms(x_ref[...], g_ref[...])
        hbt_ref[...] = jnp.transpose(h).astype(BF16)
        proj_ref[...] = jnp.dot(h.astype(BF16), w_ref[...], preferred_element_type=F32)

    return pl.pallas_call(
        body, name="in_proj_fwd", grid=(s // tm,),
        in_specs=[pl.BlockSpec((tm, D_MODEL), lambda i: (i, 0)),
                  pl.BlockSpec((1, D_MODEL), lambda i: (0, 0)),
                  pl.BlockSpec((D_MODEL, NC), lambda i: (0, 0))],
        out_specs=[pl.BlockSpec((tm, NC), lambda i: (i, 0)),
                   pl.BlockSpec((D_MODEL, tm), lambda i: (0, i))],
        out_shape=[jax.ShapeDtypeStruct((s, NC), F32), jax.ShapeDtypeStruct((D_MODEL, s), BF16)],
        compiler_params=_params(("parallel",)),
    )(x, g, w)


def _in_proj_bwd(pieces, x, g, w, dres):
    s = x.shape[0]
    tm = min(512, s)
    n_p = len(pieces)

    def body(*refs):
        p_refs = refs[:n_p]
        x_ref, g_ref, w_ref, dres_ref, dx_ref, dg_ref = refs[n_p:]
        dh = None
        off = 0
        for r in p_refs:
            width = r.shape[1]
            t = _mm_nt(r[...], w_ref[:, off:off + width])
            dh = t if dh is None else dh + t
            off += width
        _, vjp = jax.vjp(_rms, x_ref[...], g_ref[...])
        dx, dg = vjp(dh)
        dx_ref[...] = dx + dres_ref[...]

        @pl.when(pl.program_id(0) == 0)
        def _():
            dg_ref[...] = jnp.zeros_like(dg_ref)

        dg_ref[...] += dg

    in_specs = [pl.BlockSpec((tm, p.shape[1]), lambda i: (i, 0)) for p in pieces]
    in_specs += [pl.BlockSpec((tm, D_MODEL), lambda i: (i, 0)),
                 pl.BlockSpec((1, D_MODEL), lambda i: (0, 0)),
                 pl.BlockSpec((D_MODEL, NC), lambda i: (0, 0)),
                 pl.BlockSpec((tm, D_MODEL), lambda i: (i, 0))]
    return pl.pallas_call(
        body, name="in_proj_bwd", grid=(s // tm,),
        in_specs=in_specs,
        out_specs=[pl.BlockSpec((tm, D_MODEL), lambda i: (i, 0)),
                   pl.BlockSpec((1, D_MODEL), lambda i: (0, 0))],
        out_shape=[jax.ShapeDtypeStruct((s, D_MODEL), F32), jax.ShapeDtypeStruct((1, D_MODEL), F32)],
        compiler_params=_params(("arbitrary",)),
    )(*pieces, x, g, w, dres)


def _weight_grads(at, bs, name):
    m, s = at.shape
    nb = len(bs)
    tk = min(512, s)

    def body(a_ref, *refs):
        b_refs, o_refs = refs[:nb], refs[nb:]

        @pl.when(pl.program_id(0) == 0)
        def _():
            for o_ref in o_refs:
                o_ref[...] = jnp.zeros_like(o_ref)

        a = a_ref[...]
        for b_ref, o_ref in zip(b_refs, o_refs):
            o_ref[...] += _mm(a, b_ref[...])

    return pl.pallas_call(
        body, name=name, grid=(s // tk,),
        in_specs=[pl.BlockSpec((m, tk), lambda k: (0, k))]
        + [pl.BlockSpec((tk, b.shape[1]), lambda k: (k, 0)) for b in bs],
        out_specs=[pl.BlockSpec((m, b.shape[1]), lambda k: (0, 0)) for b in bs],
        out_shape=[jax.ShapeDtypeStruct((m, b.shape[1]), F32) for b in bs],
        compiler_params=_params(("arbitrary",)),
    )(at, *bs)


def _rms0(x, g, n=None):
    n = x.shape[0] if n is None else n
    ms = jnp.sum(x * x, axis=0, keepdims=True) * (1.0 / n)
    return x * lax.rsqrt(ms + RMS_EPS) * g


@jax.custom_vjp
def _rope0(t, c, s1, s2):
    return t * c + pltpu.roll(t, LANES - 16, 0) * s1 + pltpu.roll(t, 16, 0) * s2


def _rope0_fwd(t, c, s1, s2):
    return _rope0(t, c, s1, s2), (c, s1, s2)


def _rope0_bwd(res, g):
    c, s1, s2 = res
    dt = g * c + pltpu.roll(g * s1, 16, 0) + pltpu.roll(g * s2, LANES - 16, 0)
    return dt, jnp.zeros_like(c), jnp.zeros_like(s1), jnp.zeros_like(s2)


_rope0.defvjp(_rope0_fwd, _rope0_bwd)


@jax.custom_vjp
def _mmw(w, wt, x):
    return _mm(w, x)


def _mmw_fwd(w, wt, x):
    return _mm(w, x), (wt, x)


def _mmw_bwd(res, g):
    wt, x = res
    return _mm_nt(g, x), jnp.zeros_like(wt), _mm(wt, g)


_mmw.defvjp(_mmw_fwd, _mmw_bwd)


def _prep_fn(q_lat, kv_lat, kr, qan, kvan, qn, kn, wq, wk, wv, wqt, wkt, wvt, c, s1, s2, mm):
    tokens = q_lat.shape[1]
    rq = _rms0(q_lat, qan)
    rkv = _rms0(kv_lat, kvan)
    qn_b = jnp.broadcast_to(qn, (LANES, tokens))
    kn_b = jnp.broadcast_to(kn, (LANES, tokens))
    qs, ks = [], []
    for h in range(MLA_HEADS):
        qs.append(_rope0(_rms0(mm(wq[h], wqt[h], rq), qn_b, MLA_QK), c, s1, s2))
        ks.append(_rope0(_rms0(mm(wk[h], wkt[h], rkv) + kr, kn_b, MLA_QK), c, s1, s2))
    return tuple(qs), tuple(ks), mm(wv, wvt, rkv)


def _prep_weights(wq_ref, wkv_ref, wqt_ref, wkvt_ref):
    heads = range(MLA_HEADS)
    wq = tuple(wqt_ref[LANES * h:LANES * (h + 1), :].astype(F32) for h in heads)
    wk = tuple(wkvt_ref[LANES * h:LANES * (h + 1), :].astype(F32) for h in heads)
    wv = wkvt_ref[LANES * MLA_HEADS:, :].astype(F32)
    wqt = tuple(wq_ref[:, LANES * h:LANES * (h + 1)].astype(F32) for h in heads)
    wkt = tuple(wkv_ref[:, LANES * h:LANES * (h + 1)].astype(F32) for h in heads)
    wvt = wkv_ref[:, LANES * MLA_HEADS:].astype(F32)
    return wq, wk, wv, wqt, wkt, wvt


def _prep_in_specs(tm):
    const = lambda shape: pl.BlockSpec(shape, lambda i: (0, 0))
    col = lambda height: pl.BlockSpec((height, tm), lambda i: (0, i))
    return [pl.BlockSpec((tm, 512), lambda i: (i, OFF_MLA // 512)),
            const((MLA_Q_LORA, 1)), const((MLA_KV_LORA, 1)), const((LANES, 1)), const((LANES, 1)),
            const((MLA_Q_LORA, 1024)), const((MLA_KV_LORA, 1536)),
            const((1024, MLA_Q_LORA)), const((1536, MLA_KV_LORA)),
            col(LANES), col(LANES), col(LANES)]


def _prep_operands(blk_ref, refs):
    qan_ref, kvan_ref, qn_ref, kn_ref, wq_ref, wkv_ref, wqt_ref, wkvt_ref, c_ref, s1_ref, s2_ref = refs
    blk_t = jnp.transpose(blk_ref[...])
    diff = (blk_t[0:256], blk_t[256:384], blk_t[384:512],
            qan_ref[...], kvan_ref[...], qn_ref[...], kn_ref[...])
    weights = _prep_weights(wq_ref, wkv_ref, wqt_ref, wkvt_ref)
    return diff, weights, (c_ref[...], s1_ref[...], s2_ref[...])


def _mla_prep_fwd(proj, norms, weights, rope):
    s = proj.shape[0]
    tm = min(512, s)

    def body(blk_ref, *refs):
        ins, (q_ref, k_ref, v_ref, qt_ref, kt_ref, vt_ref) = refs[:11], refs[11:]
        diff, (wq, wk, wv, wqt, wkt, wvt), tables = _prep_operands(blk_ref, ins)
        qs, ks, v = _prep_fn(*diff, wq, wk, wv, wqt, wkt, wvt, *tables,
                             lambda w, wt, x: _mm(w, x))
        for h in range(MLA_HEADS):
            q2 = qs[h] * Q_PRESCALE
            qt_ref[LANES * h:LANES * (h + 1), :] = q2.astype(BF16)
            kt_ref[LANES * h:LANES * (h + 1), :] = ks[h].astype(BF16)
            q_ref[:, LANES * h:LANES * (h + 1)] = jnp.transpose(q2).astype(BF16)
            k_ref[:, LANES * h:LANES * (h + 1)] = jnp.transpose(ks[h]).astype(BF16)
        ones_row = (lax.broadcasted_iota(jnp.int32, (V_AUG - MLA_V, v.shape[1]), 0) == 0).astype(BF16)
        for h in range(MLA_HEADS):
            vt_ref[V_AUG * h:V_AUG * h + MLA_V, :] = v[MLA_V * h:MLA_V * (h + 1)].astype(BF16)
            vt_ref[V_AUG * h + MLA_V:V_AUG * (h + 1), :] = ones_row
        v_ref[...] = jnp.transpose(v).astype(BF16)

    row = lambda width: pl.BlockSpec((tm, width), lambda i: (i, 0))
    col = lambda height: pl.BlockSpec((height, tm), lambda i: (0, i))
    return pl.pallas_call(
        body, name="mla_prep_fwd", grid=(s // tm,),
        in_specs=_prep_in_specs(tm),
        out_specs=[row(1024), row(1024), row(512), col(1024), col(1024), col(MLA_HEADS * V_AUG)],
        out_shape=[jax.ShapeDtypeStruct((s, 1024), BF16), jax.ShapeDtypeStruct((s, 1024), BF16),
                   jax.ShapeDtypeStruct((s, 512), BF16), jax.ShapeDtypeStruct((1024, s), BF16),
                   jax.ShapeDtypeStruct((1024, s), BF16),
                   jax.ShapeDtypeStruct((MLA_HEADS * V_AUG, s), BF16)],
        compiler_params=_params(("parallel",)),
    )(proj, *norms, *weights, *rope)


def _mla_prep_bwd(proj, norms, weights, rope, dq, dk, dv):
    s = proj.shape[0]
    tm = min(512, s)

    def body(blk_ref, *refs):
        ins, (dq_ref, dk_ref, dv_ref) = refs[:11], refs[11:14]
        dblk_ref, dqan_ref, dkvan_ref, dqn_ref, dkn_ref, dwq_ref, dwkv_ref = refs[14:]
        diff, (wq, wk, wv, wqt, wkt, wvt), tables = _prep_operands(blk_ref, ins)

        def fn(q_lat, kv_lat, kr, qan, kvan, qn, kn, wq_, wk_, wv_):
            return _prep_fn(q_lat, kv_lat, kr, qan, kvan, qn, kn, wq_, wk_, wv_, wqt, wkt, wvt,
                            *tables, _mmw)

        _, vjp = jax.vjp(fn, *diff, wq, wk, wv)
        heads = range(MLA_HEADS)
        cts = (tuple(dq_ref[LANES * h:LANES * (h + 1), :] for h in heads),
               tuple(dk_ref[LANES * h:LANES * (h + 1), :] for h in heads), dv_ref[...])
        dq_lat, dkv_lat, dkr, dqan, dkvan, dqn, dkn, dwq_h, dwk_h, dwv = vjp(cts)
        dblk_ref[...] = jnp.transpose(
            jnp.concatenate([dq_lat, dkv_lat, dkr], axis=0)).astype(BF16)

        @pl.when(pl.program_id(0) == 0)
        def _():
            for r in (dqan_ref, dkvan_ref, dqn_ref, dkn_ref, dwq_ref, dwkv_ref):
                r[...] = jnp.zeros_like(r)

        dqan_ref[...] += dqan
        dkvan_ref[...] += dkvan
        dqn_ref[...] += dqn
        dkn_ref[...] += dkn
        for h in heads:
            dwq_ref[LANES * h:LANES * (h + 1), :] += dwq_h[h]
            dwkv_ref[LANES * h:LANES * (h + 1), :] += dwk_h[h]
        dwkv_ref[LANES * MLA_HEADS:, :] += dwv

    const = lambda shape: pl.BlockSpec(shape, lambda i: (0, 0))
    col = lambda height: pl.BlockSpec((height, tm), lambda i: (0, i))
    shapes = [(MLA_Q_LORA, 1), (MLA_KV_LORA, 1), (LANES, 1), (LANES, 1),
              (1024, MLA_Q_LORA), (1536, MLA_KV_LORA)]
    return pl.pallas_call(
        body, name="mla_prep_bwd", grid=(s // tm,),
        in_specs=_prep_in_specs(tm) + [col(1024), col(1024), col(512)],
        out_specs=[pl.BlockSpec((tm, 512), lambda i: (i, 0))] + [const(sh) for sh in shapes],
        out_shape=[jax.ShapeDtypeStruct((s, 512), BF16)]
        + [jax.ShapeDtypeStruct(sh, F32) for sh in shapes],
        compiler_params=_params(("arbitrary",)),
    )(proj, *norms, *weights, *rope, dq, dk, dv)


MLA_SCALE = MLA_QK ** -0.5


LOG2E = 1.4426950408889634
LN2 = 0.6931471805599453
Q_PRESCALE = MLA_SCALE * LOG2E
HEAD_GROUPS = ((0, 1),)


def _mla_attn_fwd(q2, k, vt):
    s = q2.shape[0]
    t = min(512, s)
    tk = min(128, s)
    nq = s // t
    r = t // tk

    def body(q_ref, k_ref, vt_ref, o_ref, lse_ref, acc_ref):
        i = pl.program_id(1)
        row = lax.broadcasted_iota(jnp.int32, (tk, t), 0)
        col = lax.broadcasted_iota(jnp.int32, (tk, t), 1)
        qh = [q_ref[:, LANES * hh:LANES * (hh + 1)] for hh in range(2)]
        acc_ref[...] = jnp.zeros_like(acc_ref)

        def scores(j, heads, diag=None):
            r0 = pl.multiple_of(j * tk, tk)
            q_lo = 0 if diag is None else diag * tk
            out = []
            for hh in heads:
                kc = k_ref[pl.ds(r0, tk), LANES * hh:LANES * (hh + 1)]
                sc = lax.dot_general(kc, qh[hh][q_lo:], (((1,), (1,)), ((), ())),
                                     preferred_element_type=F32)
                out.append(sc if diag is None
                           else jnp.where(row[:, :t - q_lo] <= col[:, :t - q_lo], sc, NEG_INF))
            return tuple(out)

        for heads in HEAD_GROUPS:
            stats = tuple(jnp.full((1, t), NEG_INF, F32) for _ in heads)

            def consume(j, scs, stats, q_lo=0, heads=heads):
                r0 = pl.multiple_of(j * tk, tk)
                out, ps, alphas = [], [], []
                for n, hh in enumerate(heads):
                    m_old = stats[n][:, q_lo:]
                    m_new = jnp.maximum(m_old, jnp.max(scs[n], axis=0, keepdims=True))
                    ps.append(jnp.exp2(scs[n] - m_new).astype(BF16))
                    alphas.append(jnp.exp2(m_old - m_new))
                    out.append(m_new if q_lo == 0
                               else jnp.concatenate([stats[n][:, :q_lo], m_new], axis=1))
                for n, hh in enumerate(heads):
                    vc = vt_ref[V_AUG * hh:V_AUG * (hh + 1), pl.ds(r0, tk)]
                    acc_ref[hh, :, q_lo:] = alphas[n] * acc_ref[hh, :, q_lo:] + jnp.dot(
                        vc, ps[n], preferred_element_type=F32)
                return tuple(out)

            def group(j0, stats, diag, heads=heads):
                scs = [scores(j0 + d, heads, d if diag else None) for d in range(r)]
                for d in range(r):
                    stats = consume(j0 + d, scs[d], stats, d * tk if diag else 0)
                return stats

            stats = group(r * i, stats, True)
            stats = lax.fori_loop(0, i, lambda j, st: group(r * j, st, False), stats)
            for n, hh in enumerate(heads):
                l = acc_ref[hh, MLA_V:MLA_V + 1, :]
                o_ref[:, MLA_V * hh:MLA_V * (hh + 1)] = jnp.transpose(acc_ref[hh, 0:MLA_V, :] / l)
                lse_ref[0, hh:hh + 1, :] = stats[n] + jnp.log2(l)

    return pl.pallas_call(
        body, name="mla_attn_fwd", grid=(MLA_HEADS // 2, nq),
        in_specs=[pl.BlockSpec((t, 256), lambda p, i: (i, p)),
                  pl.BlockSpec((s, 256), lambda p, i: (0, p)),
                  pl.BlockSpec((2 * V_AUG, s), lambda p, i: (p, 0))],
        out_specs=[pl.BlockSpec((t, 128), lambda p, i: (i, p)),
                   pl.BlockSpec((1, 2, t), lambda p, i: (p, 0, i))],
        out_shape=[jax.ShapeDtypeStruct((s, 512), F32),
                   jax.ShapeDtypeStruct((MLA_HEADS // 2, 2, s), F32)],
        scratch_shapes=[pltpu.VMEM((2, V_AUG, t), F32)],
        compiler_params=_params(("parallel", "arbitrary")),
    )(q2, k, vt)


def _mla_attn_bwd(q2, q2t, k, kt, v, do, dot, lse_rows, delta_rows):
    s = q2.shape[0]
    t = min(512, s)
    nq = s // t

    def body(q_ref, qt_ref, k_ref, kt_ref, v_ref, do_ref, dot_ref, lse_ref, dl_ref,
             dq_ref, dk_ref, dv_ref):
        j = pl.program_id(1)

        @pl.when(j == 0)
        def _():
            dq_ref[...] = jnp.zeros_like(dq_ref)

        dk_ref[...] = jnp.zeros_like(dk_ref)
        dv_ref[...] = jnp.zeros_like(dv_ref)
        row = lax.broadcasted_iota(jnp.int32, (t, t), 0)
        col = lax.broadcasted_iota(jnp.int32, (t, t), 1)
        causal_t = row <= col
        kh = [k_ref[:, LANES * hh:LANES * (hh + 1)] for hh in range(2)]
        kth = [kt_ref[LANES * hh:LANES * (hh + 1), :] for hh in range(2)]
        vh = [v_ref[:, MLA_V * hh:MLA_V * (hh + 1)] for hh in range(2)]
        nt = (((1,), (1,)), ((), ()))

        def step(i, masked):
            r0 = pl.multiple_of(i * t, t)
            sd = []
            for hh in range(2):
                qh = q_ref[pl.ds(r0, t), LANES * hh:LANES * (hh + 1)]
                doh = do_ref[pl.ds(r0, t), MLA_V * hh:MLA_V * (hh + 1)]
                sc_t = lax.dot_general(kh[hh], qh, nt, preferred_element_type=F32)
                sd.append(jnp.where(causal_t, sc_t, NEG_INF) if masked else sc_t)
                sd.append(lax.dot_general(vh[hh], doh, nt, preferred_element_type=F32))
            for hh in range(2):
                lse = lse_ref[0, hh:hh + 1, pl.ds(r0, t)]
                dl = dl_ref[0, hh:hh + 1, pl.ds(r0, t)]
                p_t = jnp.exp2(sd[2 * hh] - lse)
                g_t = (p_t * (sd[2 * hh + 1] - dl)).astype(BF16)
                qth = qt_ref[LANES * hh:LANES * (hh + 1), pl.ds(r0, t)]
                doth = dot_ref[MLA_V * hh:MLA_V * (hh + 1), pl.ds(r0, t)]
                dv_ref[MLA_V * hh:MLA_V * (hh + 1), :] += lax.dot_general(
                    doth, p_t.astype(BF16), nt, preferred_element_type=F32)
                dk_ref[LANES * hh:LANES * (hh + 1), :] += lax.dot_general(
                    qth, g_t, nt, preferred_element_type=F32)
                dq_ref[LANES * hh:LANES * (hh + 1), pl.ds(r0, t)] += jnp.dot(
                    kth[hh], g_t, preferred_element_type=F32)

        step(j, True)

        def trip(i, carry):
            step(i, False)
            return carry

        lax.fori_loop(j + 1, nq, trip, 0)
        dk_ref[...] = dk_ref[...] * LN2

        @pl.when(j == nq - 1)
        def _():
            dq_ref[...] = dq_ref[...] * MLA_SCALE

    return pl.pallas_call(
        body, name="mla_attn_bwd", grid=(MLA_HEADS // 2, nq),
        in_specs=[pl.BlockSpec((s, 256), lambda p, j: (0, p)),
                  pl.BlockSpec((256, s), lambda p, j: (p, 0)),
                  pl.BlockSpec((t, 256), lambda p, j: (j, p)),
                  pl.BlockSpec((256, t), lambda p, j: (p, j)),
                  pl.BlockSpec((t, 128), lambda p, j: (j, p)),
                  pl.BlockSpec((s, 128), lambda p, j: (0, p)),
                  pl.BlockSpec((128, s), lambda p, j: (p, 0)),
                  pl.BlockSpec((1, 2, s), lambda p, j: (p, 0, 0)),
                  pl.BlockSpec((1, 2, s), lambda p, j: (p, 0, 0))],
        out_specs=[pl.BlockSpec((256, s), lambda p, j: (p, 0)),
                   pl.BlockSpec((256, t), lambda p, j: (p, j)),
                   pl.BlockSpec((128, t), lambda p, j: (p, j))],
        out_shape=[jax.ShapeDtypeStruct((1024, s), F32), jax.ShapeDtypeStruct((1024, s), F32),
                   jax.ShapeDtypeStruct((512, s), F32)],
        compiler_params=_params(("parallel", "arbitrary")),
    )(q2, q2t, k, kt, v, do, dot, lse_rows, delta_rows)


SWA_SCALE = SWA_DIM ** -0.5
SWA_COLS = SWA_GROUP * BLOCK
SWA_LOG2 = SWA_SCALE * LOG2E


def _swa_tables():
    k = np.arange(2 * BLOCK)[:, None]
    col = np.arange(SWA_COLS)[None, :]
    dist = BLOCK + (col % BLOCK) - k
    valid = (dist >= 0) & (dist < BLOCK)
    out = np.zeros((2, SWA_KV, 2 * BLOCK, SWA_COLS), np.float32)
    for first in range(2):
        ok = valid & ((k >= BLOCK) | (first == 0))
        for j in range(SWA_KV):
            slope = 2.0 ** -(SWA_GROUP * j + col // BLOCK + 1)
            out[first, j] = np.where(ok, -slope * dist * LOG2E, NEG_INF)
    return jnp.asarray(out)


def _swa_tile_inputs(sq_ref, skv_ref, halo_ref, qn_ref, kn_ref, sk_ref, add_ref, first):
    tokens = sq_ref.shape[0]
    kv_all = jnp.concatenate([halo_ref[...], skv_ref[...]], axis=0)
    kv_t = jnp.transpose(kv_all)
    sq_t = jnp.transpose(sq_ref[...])
    k_raw = [kv_t[SWA_DIM * j:SWA_DIM * (j + 1)] for j in range(SWA_KV)]
    v_t = [kv_t[128 + SWA_DIM * j:128 + SWA_DIM * (j + 1)] for j in range(SWA_KV)]
    v_nat = [kv_all[:, 128 + SWA_DIM * j:128 + SWA_DIM * (j + 1)] for j in range(SWA_KV)]
    q_raw = [sq_t[SWA_DIM * h:SWA_DIM * (h + 1)] for h in range(SWA_HEADS)]
    qn_b = jnp.broadcast_to(qn_ref[...], (SWA_DIM, tokens))
    kn_b = jnp.broadcast_to(kn_ref[...], (SWA_DIM, tokens + BLOCK))
    lane_grp = lax.broadcasted_iota(jnp.int32, (1, SWA_COLS), 1) // BLOCK
    sinks, adds = [], []
    for j in range(SWA_KV):
        row = jnp.zeros((1, SWA_COLS), F32)
        for g in range(SWA_GROUP):
            h = SWA_GROUP * j + g
            row = jnp.where(lane_grp == g, sk_ref[:, h:h + 1] * LOG2E, row)
        sinks.append(row)
        adds.append((jnp.where(first, add_ref[1, j], add_ref[0, j]), add_ref[0, j]))
    return k_raw, v_t, v_nat, q_raw, qn_b, kn_b, sinks, adds


def _swa_probs(kb, qs_t, add, sink):
    s2 = jnp.dot(kb, qs_t, preferred_element_type=F32) * SWA_LOG2 + add
    m = jnp.maximum(jnp.max(s2, axis=0, keepdims=True), sink)
    e = jnp.exp2(s2 - m)
    es = jnp.exp2(sink - m)
    inv = 1.0 / (jnp.sum(e, axis=0, keepdims=True) + es)
    return e, inv, es


def _swa_queries(qn_t, j, b):
    return jnp.concatenate([qn_t[SWA_GROUP * j + g][:, BLOCK * b:BLOCK * (b + 1)]
                            for g in range(SWA_GROUP)], axis=1)


def _swa_fwd(proj, qn, kn, sinks, tables):
    s = proj.shape[0]
    ts = min(1024, s)
    nb = ts // BLOCK

    def body(sq_ref, skv_ref, halo_ref, qn_ref, kn_ref, sk_ref, add_ref, o_ref, ot_ref):
        first = pl.program_id(0) == 0
        k_raw, v_t, _, q_raw, qn_b, kn_b, sink_rows, adds = _swa_tile_inputs(
            sq_ref, skv_ref, halo_ref, qn_ref, kn_ref, sk_ref, add_ref, first)
        kn_nat = [jnp.transpose(_rms0(k, kn_b)).astype(BF16) for k in k_raw]
        v_t = [v.astype(BF16) for v in v_t]
        qn_t = [_rms0(q, qn_b).astype(BF16) for q in q_raw]
        for b in range(nb):
            band = slice(BLOCK * b, BLOCK * (b + 2))
            for j in range(SWA_KV):
                e, inv, _ = _swa_probs(kn_nat[j][band], _swa_queries(qn_t, j, b),
                                       adds[j][0 if b == 0 else 1], sink_rows[j])
                o_t = jnp.dot(v_t[j][:, band], (e * inv).astype(BF16),
                              preferred_element_type=F32)
                for g in range(SWA_GROUP):
                    h = SWA_GROUP * j + g
                    ot_ref[SWA_DIM * h:SWA_DIM * (h + 1), BLOCK * b:BLOCK * (b + 1)] = (
                        o_t[:, BLOCK * g:BLOCK * (g + 1)])
        o_ref[...] = jnp.transpose(ot_ref[...])

    const = lambda shape: pl.BlockSpec(shape, lambda i: (0,) * len(shape))
    return pl.pallas_call(
        body, name="swa_fwd", grid=(s // ts,),
        in_specs=[pl.BlockSpec((ts, 512), lambda i: (i, OFF_SQ // 512)),
                  pl.BlockSpec((ts, 256), lambda i: (i, OFF_SKV // 256)),
                  pl.BlockSpec((BLOCK, 256), lambda i: (jnp.maximum(i * nb - 1, 0), OFF_SKV // 256)),
                  const((SWA_DIM, 1)), const((SWA_DIM, 1)), const((1, SWA_HEADS)),
                  const(tables.shape)],
        out_specs=pl.BlockSpec((ts, 512), lambda i: (i, 0)),
        out_shape=jax.ShapeDtypeStruct((s, 512), F32),
        scratch_shapes=[pltpu.VMEM((512, ts), F32)],
        compiler_params=_params(("parallel",)),
    )(proj, proj, proj, qn, kn, sinks, tables)


def _swa_bwd(proj, qn, kn, sinks, tables, do):
    s = proj.shape[0]
    ts = min(1024, s)
    nb = ts // BLOCK
    nt = s // ts

    def body(sq_ref, skv_ref, halo_ref, qn_ref, kn_ref, sk_ref, add_ref, do_ref,
             dsq_ref, dskv_ref, dqn_ref, dkn_ref, dsk_ref, carry_ref, dqt_ref, dkvt_ref):
        step = pl.program_id(0)
        first = step == nt - 1

        @pl.when(step == 0)
        def _():
            carry_ref[...] = jnp.zeros_like(carry_ref)
            dqn_ref[...] = jnp.zeros_like(dqn_ref)
            dkn_ref[...] = jnp.zeros_like(dkn_ref)
            dsk_ref[...] = jnp.zeros_like(dsk_ref)

        k_raw, v_t, v_nat, q_raw, qn_b, kn_b, sink_rows, adds = _swa_tile_inputs(
            sq_ref, skv_ref, halo_ref, qn_ref, kn_ref, sk_ref, add_ref, first)
        kn_f = [_rms0(k, kn_b) for k in k_raw]
        kn_t = [k.astype(BF16) for k in kn_f]
        kn_nat = [jnp.transpose(k).astype(BF16) for k in kn_f]
        v_nat = [v.astype(BF16) for v in v_nat]
        qn_t = [_rms0(q, qn_b).astype(BF16) for q in q_raw]
        do_t = jnp.transpose(do_ref[...].astype(F32)).astype(BF16)

        dkvt_ref[...] = jnp.zeros_like(dkvt_ref)
        dsink = [jnp.zeros((1, SWA_COLS), F32) for _ in range(SWA_KV)]
        nt_dims = (((1,), (1,)), ((), ()))
        for b in range(nb):
            rows = slice(BLOCK * b, BLOCK * (b + 1))
            band = slice(BLOCK * b, BLOCK * (b + 2))
            for j in range(SWA_KV):
                heads = [SWA_GROUP * j + g for g in range(SWA_GROUP)]
                qs_t = _swa_queries(qn_t, j, b)
                dos_t = jnp.concatenate([do_t[SWA_DIM * h:SWA_DIM * (h + 1), rows] for h in heads],
                                        axis=1)
                e, inv, es = _swa_probs(kn_nat[j][band], qs_t, adds[j][0 if b == 0 else 1],
                                        sink_rows[j])
                p = e * inv
                dp = jnp.dot(v_nat[j][band], dos_t, preferred_element_type=F32)
                dsum = jnp.sum(p * dp, axis=0, keepdims=True)
                dsink[j] = dsink[j] - es * inv * dsum
                g_t = (p * (dp - dsum) * SWA_SCALE).astype(BF16)
                dv_t = lax.dot_general(dos_t, p.astype(BF16), nt_dims,
                                       preferred_element_type=F32)
                dk_t = lax.dot_general(qs_t, g_t, nt_dims, preferred_element_type=F32)
                dq_t = jnp.dot(kn_t[j][:, band], g_t, preferred_element_type=F32)
                dkvt_ref[SWA_DIM * j:SWA_DIM * (j + 1), band] += dk_t
                dkvt_ref[128 + SWA_DIM * j:128 + SWA_DIM * (j + 1), band] += dv_t
                for g, h in enumerate(heads):
                    dqt_ref[SWA_DIM * h:SWA_DIM * (h + 1), rows] = dq_t[:, BLOCK * g:BLOCK * (g + 1)]

        dqn = jnp.zeros((SWA_DIM, 1), F32)
        for h in range(SWA_HEADS):
            _, vjp = jax.vjp(_rms0, q_raw[h], qn_ref[...])
            dq, dg = vjp(dqt_ref[SWA_DIM * h:SWA_DIM * (h + 1), :])
            dqt_ref[SWA_DIM * h:SWA_DIM * (h + 1), :] = dq
            dqn = dqn + dg
        dqn_ref[...] += dqn
        dsq_ref[...] = jnp.transpose(dqt_ref[...]).astype(BF16)
        dkn = jnp.zeros((SWA_DIM, 1), F32)
        lane_grp = lax.broadcasted_iota(jnp.int32, (1, SWA_COLS), 1) // BLOCK
        for j in range(SWA_KV):
            _, vjp = jax.vjp(_rms0, k_raw[j], kn_ref[...])
            dk, dg = vjp(dkvt_ref[SWA_DIM * j:SWA_DIM * (j + 1), :])
            dkvt_ref[SWA_DIM * j:SWA_DIM * (j + 1), :] = dk
            dkn = dkn + dg
            for g in range(SWA_GROUP):
                h = SWA_GROUP * j + g
                dsk_ref[:, h:h + 1] += jnp.sum(jnp.where(lane_grp == g, dsink[j], 0.0), axis=1,
                                               keepdims=True)
        dkn_ref[...] += dkn
        dkv = jnp.transpose(dkvt_ref[...])
        dskv_ref[0:ts - BLOCK, :] = dkv[BLOCK:ts].astype(BF16)
        dskv_ref[ts - BLOCK:ts, :] = (dkv[ts:ts + BLOCK] + carry_ref[...]).astype(BF16)
        carry_ref[...] = dkv[0:BLOCK]

    const = lambda shape: pl.BlockSpec(shape, lambda st: (0,) * len(shape))
    return pl.pallas_call(
        body, name="swa_bwd", grid=(nt,),
        in_specs=[pl.BlockSpec((ts, 512), lambda st: (nt - 1 - st, OFF_SQ // 512)),
                  pl.BlockSpec((ts, 256), lambda st: (nt - 1 - st, OFF_SKV // 256)),
                  pl.BlockSpec((BLOCK, 256),
                               lambda st: (jnp.maximum((nt - 1 - st) * nb - 1, 0), OFF_SKV // 256)),
                  const((SWA_DIM, 1)), const((SWA_DIM, 1)), const((1, SWA_HEADS)),
                  const(tables.shape),
                  pl.BlockSpec((ts, 512), lambda st: (nt - 1 - st, 0))],
        out_specs=[pl.BlockSpec((ts, 512), lambda st: (nt - 1 - st, 0)),
                   pl.BlockSpec((ts, 256), lambda st: (nt - 1 - st, 0)),
                   const((SWA_DIM, 1)), const((SWA_DIM, 1)), const((1, SWA_HEADS))],
        out_shape=[jax.ShapeDtypeStruct((s, 512), BF16), jax.ShapeDtypeStruct((s, 256), BF16),
                   jax.ShapeDtypeStruct((SWA_DIM, 1), F32), jax.ShapeDtypeStruct((SWA_DIM, 1), F32),
                   jax.ShapeDtypeStruct((1, SWA_HEADS), F32)],
        scratch_shapes=[pltpu.VMEM((BLOCK, 256), F32), pltpu.VMEM((512, ts), F32),
                        pltpu.VMEM((256, ts + BLOCK), F32)],
        compiler_params=_params(("arbitrary",)),
    )(proj, proj, proj, qn, kn, sinks, tables, do)


HALO = 8


def _shift_down(u, halo, k):
    tm = u.shape[0]
    rid = lax.broadcasted_iota(jnp.int32, u.shape, 0)
    out = pltpu.roll(u, k, 0)
    for r in range(k):
        out = jnp.where(rid == r, halo[HALO - k + r:HALO - k + r + 1, :], out)
    return out


def _shift_up(u, halo, k):
    tm = u.shape[0]
    rid = lax.broadcasted_iota(jnp.int32, u.shape, 0)
    out = pltpu.roll(u, tm - k, 0)
    for r in range(k):
        out = jnp.where(rid == tm - k + r, halo[r:r + 1, :], out)
    return out


def _conv_fwd_vals(conv_ref, convp_ref, cw_ref, is_first):
    c_h, c_b, c_c = conv_ref[:, 0:512], conv_ref[:, 512:1024], conv_ref[:, 1024:1536]
    u = c_c * c_h
    up = jnp.where(is_first, 0.0, convp_ref[:, 1024:1536] * convp_ref[:, 0:512])
    u1 = _shift_down(u, up, 1)
    u2 = _shift_down(u, up, 2)
    yc = cw_ref[0:1, :] * u2 + cw_ref[1:2, :] * u1 + cw_ref[2:3, :] * u
    return c_h, c_b, c_c, u, u1, u2, yc


def _out_fwd(proj, o_mla, o_swa, x, w_out, cw, target=None):
    s = proj.shape[0]
    tm = min(512, s)
    nt = s // tm
    with_loss = target is not None

    def body(*refs):
        conv_ref, convp_ref, gates_ref, om_ref, os_ref, x_ref, w_ref, cw_ref = refs[:8]
        if with_loss:
            t_ref, y_ref, zt_ref, loss_ref, z_ref = refs[8:]
        else:
            y_ref, zt_ref, z_ref = refs[8:]
        i = pl.program_id(0)
        _, c_b, _, _, _, _, yc = _conv_fwd_vals(conv_ref, convp_ref, cw_ref, i == 0)
        mix = (om_ref[...], c_b * yc, os_ref[...])
        for n in range(3):
            g = gates_ref[:, GROUP * n:GROUP * (n + 1)]
            z = mix[n] * (g * _sigmoid(g))
            z_ref[:, GROUP * n:GROUP * (n + 1)] = z.astype(BF16)
            zt_ref[GROUP * n:GROUP * (n + 1), :] = jnp.transpose(z).astype(BF16)
        y = x_ref[...] + jnp.dot(z_ref[...], w_ref[...], preferred_element_type=F32)
        if not with_loss:
            y_ref[...] = y
            return
        err = y - t_ref[...]
        y_ref[...] = err * (1.0 / D_MODEL)

        @pl.when(i == 0)
        def _():
            loss_ref[...] = jnp.zeros_like(loss_ref)

        sq = jnp.sum((err * err).reshape(tm // 8, 8, D_MODEL), axis=0)
        part = sq[:, 0:LANES]
        for c in range(1, D_MODEL // LANES):
            part = part + sq[:, LANES * c:LANES * (c + 1)]
        loss_ref[...] += part

        @pl.when(i == nt - 1)
        def _():
            loss_ref[...] = jnp.full(loss_ref.shape, (0.5 / D_MODEL) * jnp.sum(loss_ref[...]), F32)

    row = lambda width: pl.BlockSpec((tm, width), lambda i: (i, 0))
    in_specs = [pl.BlockSpec((tm, 1536), lambda i: (i, 0)),
                pl.BlockSpec((HALO, 1536), lambda i: (jnp.maximum(i * (tm // HALO) - 1, 0), 0)),
                pl.BlockSpec((tm, 1536), lambda i: (i, 1)),
                row(512), row(512), row(D_MODEL),
                pl.BlockSpec((D_MIX, D_MODEL), lambda i: (0, 0)),
                pl.BlockSpec((HALO, 512), lambda i: (0, 0))]
    out_specs = [row(D_MODEL), pl.BlockSpec((D_MIX, tm), lambda i: (0, i))]
    out_shape = [jax.ShapeDtypeStruct((s, D_MODEL), F32), jax.ShapeDtypeStruct((D_MIX, s), BF16)]
    operands = [proj, proj, proj, o_mla, o_swa, x, w_out, cw]
    if with_loss:
        in_specs.append(row(D_MODEL))
        out_specs.append(pl.BlockSpec((8, LANES), lambda i: (0, 0)))
        out_shape.append(jax.ShapeDtypeStruct((8, LANES), F32))
        operands.append(target)
    return pl.pallas_call(
        body, name="out_fwd_loss" if with_loss else "out_fwd", grid=(nt,),
        in_specs=in_specs, out_specs=out_specs, out_shape=out_shape,
        scratch_shapes=[pltpu.VMEM((tm, D_MIX), BF16)],
        compiler_params=_params(("arbitrary",) if with_loss else ("parallel",)),
    )(*operands)


def _out_bwd(dy, proj, o_mla, o_swa, w_out, cw):
    s = proj.shape[0]
    tm = min(512, s)
    nt = s // tm
    hb = tm // HALO

    def body(dy_ref, dyn_ref, conv_ref, convp_ref, convn_ref, gates_ref, gatesn_ref, om_ref, os_ref,
             w_ref, cw_ref,
             dconv_ref, dgates_ref, dom_ref, domt_ref, delta_ref, dos_ref, dcw_ref):
        i = pl.program_id(0)
        dz = _mm_nt(dy_ref[...], w_ref[...])

        def gate(n):
            g = gates_ref[:, GROUP * n:GROUP * (n + 1)]
            sg = _sigmoid(g)
            return g * sg, sg * (1.0 + g * (1.0 - sg))

        for n, o_ref, do_ref in ((0, om_ref, dom_ref), (2, os_ref, dos_ref)):
            silu, dsilu = gate(n)
            dzn = dz[:, GROUP * n:GROUP * (n + 1)]
            o = o_ref[...]
            do = dzn * silu
            do_ref[...] = do.astype(do_ref.dtype)
            dgates_ref[:, GROUP * n:GROUP * (n + 1)] = (dzn * o * dsilu).astype(BF16)
            if n == 0:
                domt_ref[...] = jnp.transpose(do).astype(BF16)
                t = do * o
                for h in range(MLA_HEADS):
                    delta_ref[:, h:h + 1] = jnp.sum(t[:, MLA_V * h:MLA_V * (h + 1)], axis=-1,
                                                    keepdims=True)

        c_h, c_b, c_c, u, u1, u2, yc = _conv_fwd_vals(conv_ref, convp_ref, cw_ref, i == 0)
        silu, dsilu = gate(1)
        dzc = dz[:, GROUP:2 * GROUP]
        dgates_ref[:, GROUP:2 * GROUP] = (dzc * (c_b * yc) * dsilu).astype(BF16)
        dycr = dzc * silu
        dyc = dycr * c_b
        gn = gatesn_ref[:, GROUP:2 * GROUP]
        dzc_n = _mm_nt(dyn_ref[...], w_ref[GROUP:2 * GROUP, :])
        dyc_n = jnp.where(i == nt - 1, 0.0, dzc_n * (gn * _sigmoid(gn)) * convn_ref[:, 512:1024])
        d1 = _shift_up(dyc, dyc_n, 1)
        d2 = _shift_up(dyc, dyc_n, 2)
        du = cw_ref[2:3, :] * dyc + cw_ref[1:2, :] * d1 + cw_ref[0:1, :] * d2
        dconv_ref[:, 0:512] = (du * c_c).astype(BF16)
        dconv_ref[:, 512:1024] = (dycr * yc).astype(BF16)
        dconv_ref[:, 1024:1536] = (du * c_h).astype(BF16)

        @pl.when(i == 0)
        def _():
            dcw_ref[...] = jnp.zeros_like(dcw_ref)

        for k, uk in enumerate((u2, u1, u)):
            dcw_ref[k:k + 1, :] += jnp.sum(dyc * uk, axis=0, keepdims=True)

    row = lambda width: pl.BlockSpec((tm, width), lambda i: (i, 0))
    prev = lambda i: jnp.maximum(i * hb - 1, 0)
    nxt = lambda i: jnp.minimum((i + 1) * hb, s // HALO - 1)
    return pl.pallas_call(
        body, name="out_bwd", grid=(nt,),
        in_specs=[row(D_MODEL),
                  pl.BlockSpec((HALO, D_MODEL), lambda i: (nxt(i), 0)),
                  pl.BlockSpec((tm, 1536), lambda i: (i, 0)),
                  pl.BlockSpec((HALO, 1536), lambda i: (prev(i), 0)),
                  pl.BlockSpec((HALO, 1536), lambda i: (nxt(i), 0)),
                  pl.BlockSpec((tm, 1536), lambda i: (i, 1)),
                  pl.BlockSpec((HALO, 1536), lambda i: (nxt(i), 1)),
                  row(512), row(512),
                  pl.BlockSpec((D_MIX, D_MODEL), lambda i: (0, 0)),
                  pl.BlockSpec((HALO, 512), lambda i: (0, 0))],
        out_specs=[row(1536), row(1536), row(512), pl.BlockSpec((512, tm), lambda i: (0, i)),
                   row(MLA_HEADS), row(512), pl.BlockSpec((HALO, 512), lambda i: (0, 0))],
        out_shape=[jax.ShapeDtypeStruct((s, 1536), BF16), jax.ShapeDtypeStruct((s, 1536), BF16),
                   jax.ShapeDtypeStruct((s, 512), BF16), jax.ShapeDtypeStruct((512, s), BF16),
                   jax.ShapeDtypeStruct((s, MLA_HEADS), F32),
                   jax.ShapeDtypeStruct((s, 512), BF16), jax.ShapeDtypeStruct((HALO, 512), F32)],
        compiler_params=_params(("arbitrary",)),
    )(dy, dy, proj, proj, proj, proj, proj, o_mla, o_swa, w_out, cw)


def _adam_update(g, w, m, v):
    c1 = 1.0 - ADAM_B1
    c2 = 1.0 - ADAM_B2
    bc1 = 1.0 - ADAM_B1 ** ADAM_STEP
    bc2 = 1.0 - ADAM_B2 ** ADAM_STEP
    m_new = ADAM_B1 * m + c1 * g
    v_new = ADAM_B2 * v + c2 * (g * g)
    delta = -ADAM_LR * ((m_new / bc1) / (jnp.sqrt(v_new / bc2) + ADAM_EPS) + ADAM_WD * w)
    return delta, m_new, v_new


def _adamw(g, w, m, v):
    rows = g.shape[0]
    tr = min(256, rows)

    def body(g_ref, w_ref, m_ref, v_ref, d_ref, mo_ref, vo_ref):
        d_ref[...], mo_ref[...], vo_ref[...] = _adam_update(g_ref[...], w_ref[...], m_ref[...],
                                                            v_ref[...])

    spec = pl.BlockSpec((tr, g.shape[1]), lambda i: (i, 0))
    return pl.pallas_call(
        body, name="adamw", grid=(rows // tr,),
        in_specs=[spec] * 4, out_specs=[spec] * 3,
        out_shape=[jax.ShapeDtypeStruct(g.shape, F32)] * 3,
        compiler_params=_params(("parallel",)),
    )(g, w, m, v)


def _adamw_small(gs, ws, ms, vs):
    n = len(gs)

    def body(*refs):
        ins, outs = refs[:4 * n], refs[4 * n:]
        for a in range(n):
            res = _adam_update(*(ins[kind * n + a][...] for kind in range(4)))
            for kind in range(3):
                outs[kind * n + a][...] = res[kind]

    vmem = pl.BlockSpec(memory_space=pltpu.VMEM)
    out = pl.pallas_call(
        body, name="adamw_small",
        in_specs=[vmem] * (4 * n), out_specs=[vmem] * (3 * n),
        out_shape=[jax.ShapeDtypeStruct(g.shape, F32) for _ in range(3) for g in gs],
    )(*gs, *ws, *ms, *vs)
    return out[:n], out[n:2 * n], out[2 * n:]


HBM_SPEC = pl.BlockSpec(memory_space=pltpu.HBM)


def _place():
    x, y, c = lax.axis_index("x"), lax.axis_index("y"), lax.axis_index("c")
    chips = [(1 - x, y), (x, 1 - y), (1 - x, 1 - y)]
    return x, y, c, chips


def _all_gather(shards):
    na = len(shards)
    halves = [sh.shape[0] // 2 for sh in shards]

    def body(*refs):
        w_refs, a_refs = refs[:na], refs[na:2 * na]
        send_sems, recv_sems = refs[2 * na:]
        x, y, c, chips = _place()
        k = 2 * x + y
        sib = (x, y, 1 - c)

        def slab(a, kk, hc):
            return a_refs[a].at[kk, pl.ds(hc * halves[a], halves[a]), :]

        def copy(a, n, src, dst, to):
            return pltpu.make_async_remote_copy(
                src_ref=src, dst_ref=dst, send_sem=send_sems.at[6 * a + n],
                recv_sem=recv_sems.at[6 * a + n], device_id=to, device_id_type=MESH)

        first = [copy(a, n, w_refs[a].at[pl.ds(c * halves[a], halves[a]), :], slab(a, k, c),
                      (cx, cy, c))
                 for n, (cx, cy) in enumerate(chips) for a in range(na)]
        for cp in first:
            cp.start()
        passed = []
        for n, (cx, cy) in enumerate(chips):
            kk = 2 * cx + cy
            for a in range(na):
                copy(a, n, slab(a, kk, c), slab(a, kk, c), (cx, cy, c)).wait_recv()
                fwd = copy(a, 3 + n, slab(a, kk, c), slab(a, kk, c), sib)
                fwd.start()
                passed.append(fwd)
        for n, (cx, cy) in enumerate(chips):
            kk = 2 * cx + cy
            for a in range(na):
                copy(a, 3 + n, slab(a, kk, 1 - c), slab(a, kk, 1 - c), sib).wait_recv()
        for cp in first + passed:
            cp.wait_send()

    return pl.pallas_call(
        body, name="weights_all_gather",
        in_specs=[HBM_SPEC] * na, out_specs=[HBM_SPEC] * na,
        out_shape=[jax.ShapeDtypeStruct((N_CHIPS,) + sh.shape, sh.dtype) for sh in shards],
        scratch_shapes=[pltpu.SemaphoreType.DMA((6 * na,)), pltpu.SemaphoreType.DMA((6 * na,))],
    )(*shards)


def _fill_own_slab(buf, src, k_idx):
    n, rows, cols = buf.shape
    tr = _row_tile(rows)
    slabs = src.ndim == 3

    def body(k_ref, src_ref, buf_ref, out_ref):
        out_ref[0] = src_ref[0] if slabs else src_ref[...]

    if slabs:
        src_spec = pl.BlockSpec((1, tr, cols), lambda t, k_ref: (k_ref[0], t, 0))
    else:
        src_spec = pl.BlockSpec((tr, cols), lambda t, k_ref: (t, 0))
    return pl.pallas_call(
        body, name="fill_own_slab",
        grid_spec=pltpu.PrefetchScalarGridSpec(
            num_scalar_prefetch=1, grid=(rows // tr,),
            in_specs=[src_spec, pl.BlockSpec(memory_space=pl.ANY)],
            out_specs=pl.BlockSpec((1, tr, cols), lambda t, k_ref: (k_ref[0], t, 0))),
        out_shape=jax.ShapeDtypeStruct(buf.shape, buf.dtype),
        input_output_aliases={2: 0},
        compiler_params=_params(("parallel",)),
    )(k_idx, src, buf)


SEM_SPEC = pl.BlockSpec(memory_space=pltpu.SEMAPHORE)
LATE_COPIES = 6


def _late_copy(a, j, peer_core, src, dst, send_sems, recv_sems, to, sender_core):
    return pltpu.make_async_remote_copy(
        src_ref=src, dst_ref=dst, send_sem=send_sems.at[LATE_COPIES * a + 2 * j + peer_core],
        recv_sem=recv_sems.at[LATE_COPIES * a + 2 * j + sender_core], device_id=to,
        device_id_type=MESH)


def _gather_start(shards):
    na = len(shards)

    def body(*refs):
        w_refs, land_refs = refs[:na], refs[na:2 * na]
        send_sems, recv_sems = refs[2 * na], refs[2 * na + 1]
        token = refs[-1]
        x, y, c, chips = _place()
        k = 2 * x + y
        for a in range(na):
            half = w_refs[a].shape[0] // 2
            src = w_refs[a].at[pl.ds(c * half, half), :]
            dst = land_refs[a].at[k, pl.ds(c * half, half), :]
            for j, (cx, cy) in enumerate(chips):
                for tc in range(2):
                    _late_copy(a, j, tc, src, dst, send_sems, recv_sems, (cx, cy, tc), c).start()
        token[...] = jnp.zeros_like(token)

    lands = [pltpu.with_memory_space_constraint(lax.empty((N_CHIPS,) + sh.shape, sh.dtype), pltpu.HBM)
             for sh in shards]
    srcs = [pltpu.with_memory_space_constraint(sh, pltpu.HBM) for sh in shards]
    sems = pltpu.SemaphoreType.DMA((LATE_COPIES * na,))
    aliases = {a: 2 + a for a in range(2 * na)}
    return pl.pallas_call(
        body, name="late_weights_gather_start",
        in_specs=[HBM_SPEC] * (2 * na),
        out_specs=[SEM_SPEC, SEM_SPEC] + [HBM_SPEC] * (2 * na) + [pl.BlockSpec(memory_space=pltpu.VMEM)],
        out_shape=[sems, sems] + [pltpu.HBM(v.shape, v.dtype) for v in srcs + lands]
        + [jax.ShapeDtypeStruct((8, LANES), F32)],
        input_output_aliases=aliases,
        compiler_params=pltpu.CompilerParams(
            has_side_effects=pltpu.SideEffectType.DATAFLOW_SIDE_EFFECTING),
    )(*srcs, *lands)


def _gather_wait(started, na, after):
    send_sems, recv_sems = started[0], started[1]
    bufs = started[2:2 + 2 * na]

    def body(*refs):
        w_refs, land_refs = refs[:na], refs[na:2 * na]
        send_sems, recv_sems = refs[2 * na], refs[2 * na + 1]
        x, y, c, chips = _place()
        k = 2 * x + y
        for a in range(na):
            half = w_refs[a].shape[0] // 2
            src = w_refs[a].at[pl.ds(c * half, half), :]
            for j, (cx, cy) in enumerate(chips):
                kk = 2 * cx + cy
                for pc in range(2):
                    _late_copy(a, j, pc, src, land_refs[a].at[k, pl.ds(c * half, half), :],
                               send_sems, recv_sems, (cx, cy, pc), c).wait_send()
                    pltpu.make_async_remote_copy(
                        src_ref=src, dst_ref=land_refs[a].at[kk, pl.ds(pc * half, half), :],
                        send_sem=send_sems.at[LATE_COPIES * a + 2 * j + pc],
                        recv_sem=recv_sems.at[LATE_COPIES * a + 2 * j + pc],
                        device_id=(cx, cy, pc), device_id_type=MESH).wait_recv()

    out = pl.pallas_call(
        body, name="late_weights_gather_wait",
        in_specs=[HBM_SPEC] * (2 * na) + [SEM_SPEC, SEM_SPEC, pl.BlockSpec(memory_space=pl.ANY)],
        out_specs=[HBM_SPEC] * (2 * na),
        out_shape=[pltpu.HBM(v.shape, v.dtype) for v in bufs],
        input_output_aliases={a: a for a in range(2 * na)},
        compiler_params=pltpu.CompilerParams(
            has_side_effects=pltpu.SideEffectType.DATAFLOW_SIDE_EFFECTING),
    )(*bufs, send_sems, recv_sems, after)
    return out[na:]


def _split_start(name, srcs, land_shapes, n_sems, plan):
    na = len(srcs)

    def body(*refs):
        sends, _ = plan(refs[:na], refs[na:2 * na], refs[2 * na], refs[2 * na + 1])
        for cp in sends:
            cp.start()
        refs[-1][...] = jnp.zeros_like(refs[-1])

    lands = [pltpu.with_memory_space_constraint(lax.empty(shape, dtype), pltpu.HBM)
             for shape, dtype in land_shapes]
    srcs = [pltpu.with_memory_space_constraint(v, pltpu.HBM) for v in srcs]
    sems = pltpu.SemaphoreType.DMA((n_sems,))
    return pl.pallas_call(
        body, name=name,
        in_specs=[HBM_SPEC] * (2 * na),
        out_specs=[SEM_SPEC, SEM_SPEC] + [HBM_SPEC] * (2 * na) + [pl.BlockSpec(memory_space=pltpu.VMEM)],
        out_shape=[sems, sems] + [pltpu.HBM(v.shape, v.dtype) for v in srcs + lands]
        + [jax.ShapeDtypeStruct((8, LANES), F32)],
        input_output_aliases={a: 2 + a for a in range(2 * na)},
        compiler_params=pltpu.CompilerParams(
            has_side_effects=pltpu.SideEffectType.DATAFLOW_SIDE_EFFECTING),
    )(*srcs, *lands)


def _split_wait(name, started, na, plan, after):
    bufs = started[2:2 + 2 * na]

    def body(*refs):
        sends, recvs = plan(refs[:na], refs[na:2 * na], refs[2 * na], refs[2 * na + 1])
        for cp in sends:
            cp.wait_send()
        for cp in recvs:
            cp.wait_recv()

    out = pl.pallas_call(
        body, name=name,
        in_specs=[HBM_SPEC] * (2 * na) + [SEM_SPEC, SEM_SPEC, pl.BlockSpec(memory_space=pl.ANY)],
        out_specs=[HBM_SPEC] * (2 * na),
        out_shape=[pltpu.HBM(v.shape, v.dtype) for v in bufs],
        input_output_aliases={a: a for a in range(2 * na)},
        compiler_params=pltpu.CompilerParams(
            has_side_effects=pltpu.SideEffectType.DATAFLOW_SIDE_EFFECTING),
    )(*bufs, started[0], started[1], after)
    return out[:na], out[na:]


def _plan_to_sibling(g_refs, r_refs, send_sems, recv_sems):
    x, y, c, _ = _place()
    cps = []
    for a, (g, r) in enumerate(zip(g_refs, r_refs)):
        half = g.shape[1] // 2
        cps.append(pltpu.make_async_remote_copy(
            src_ref=g.at[:, pl.ds((1 - c) * half, half), :], dst_ref=r, send_sem=send_sems.at[a],
            recv_sem=recv_sems.at[a], device_id=(x, y, 1 - c), device_id_type=MESH))
    return cps, cps


def _plan_scatter(p_refs, q_refs, send_sems, recv_sems):
    x, y, c, chips = _place()
    k = 2 * x + y
    sends, recvs = [], []
    for a, (p, q) in enumerate(zip(p_refs, q_refs)):
        for i, (cx, cy) in enumerate(chips):
            kk = 2 * cx + cy
            for dst, out in ((q.at[k], sends), (q.at[kk], recvs)):
                out.append(pltpu.make_async_remote_copy(
                    src_ref=p.at[kk], dst_ref=dst, send_sem=send_sems.at[3 * a + i],
                    recv_sem=recv_sems.at[3 * a + i], device_id=(cx, cy, c), device_id_type=MESH))
    return sends, recvs


def _swap_halves_to_sibling(gs):
    na = len(gs)

    def body(*refs):
        g_refs, r_refs = refs[:na], refs[na:2 * na]
        send_sems, recv_sems = refs[2 * na:]
        x, y, c, _ = _place()
        cps = []
        for a in range(na):
            half = g_refs[a].shape[1] // 2
            cps.append(pltpu.make_async_remote_copy(
                src_ref=g_refs[a].at[:, pl.ds((1 - c) * half, half), :], dst_ref=r_refs[a],
                send_sem=send_sems.at[a], recv_sem=recv_sems.at[a], device_id=(x, y, 1 - c),
                device_id_type=MESH))
        for cp in cps:
            cp.start()
        for cp in cps:
            cp.wait()

    return pl.pallas_call(
        body, name="grads_to_sibling",
        in_specs=[HBM_SPEC] * na, out_specs=[HBM_SPEC] * na,
        out_shape=[jax.ShapeDtypeStruct((g.shape[0], g.shape[1] // 2, g.shape[2]), g.dtype)
                   for g in gs],
        scratch_shapes=[pltpu.SemaphoreType.DMA((na,)), pltpu.SemaphoreType.DMA((na,))],
    )(*gs)


def _row_tile(rows):
    return next(t for t in (256, 128, 64) if rows % t == 0)


def _add_sibling(g, r, c_idx, out_dtype):
    n, rows, cols = g.shape
    half = rows // 2
    tr = _row_tile(half)
    nb = half // tr

    def body(c_ref, g_ref, r_ref, p_ref):
        p_ref[...] = (g_ref[...] + r_ref[...]).astype(out_dtype)

    return pl.pallas_call(
        body, name="grads_add_sibling",
        grid_spec=pltpu.PrefetchScalarGridSpec(
            num_scalar_prefetch=1, grid=(n, nb),
            in_specs=[pl.BlockSpec((1, tr, cols), lambda j, t, c_ref: (j, c_ref[0] * nb + t, 0)),
                      pl.BlockSpec((1, tr, cols), lambda j, t, c_ref: (j, t, 0))],
            out_specs=pl.BlockSpec((1, tr, cols), lambda j, t, c_ref: (j, t, 0))),
        out_shape=jax.ShapeDtypeStruct((n, half, cols), out_dtype),
        compiler_params=_params(("parallel", "parallel")),
    )(c_idx, g, r)


def _scatter_to_chips(ps):
    na = len(ps)

    def body(*refs):
        p_refs, q_refs = refs[:na], refs[na:2 * na]
        send_sems, recv_sems = refs[2 * na:]
        x, y, c, chips = _place()
        k = 2 * x + y
        sends = []
        for i, (cx, cy) in enumerate(chips):
            for a in range(na):
                cp = pltpu.make_async_remote_copy(
                    src_ref=p_refs[a].at[2 * cx + cy], dst_ref=q_refs[a].at[k],
                    send_sem=send_sems.at[3 * a + i], recv_sem=recv_sems.at[3 * a + i],
                    device_id=(cx, cy, c), device_id_type=MESH)
                cp.start()
                sends.append(cp)
        for i, (cx, cy) in enumerate(chips):
            kk = 2 * cx + cy
            for a in range(na):
                pltpu.make_async_remote_copy(
                    src_ref=p_refs[a].at[kk], dst_ref=q_refs[a].at[kk],
                    send_sem=send_sems.at[3 * a + i], recv_sem=recv_sems.at[3 * a + i],
                    device_id=(cx, cy, c), device_id_type=MESH).wait_recv()
        for cp in sends:
            cp.wait_send()

    return pl.pallas_call(
        body, name="grads_scatter_to_chips",
        in_specs=[HBM_SPEC] * na, out_specs=[HBM_SPEC] * na,
        out_shape=[jax.ShapeDtypeStruct(p.shape, p.dtype) for p in ps],
        scratch_shapes=[pltpu.SemaphoreType.DMA((3 * na,)), pltpu.SemaphoreType.DMA((3 * na,))],
    )(*ps)


def _sum_chips(q, c_idx):
    n, half, cols = q.shape
    tr = _row_tile(half)
    nb = half // tr

    def body(c_ref, q_ref, o_ref):
        parts = [q_ref[kk].astype(F32) for kk in range(n)]
        o_ref[...] = ((parts[0] + parts[1]) + parts[2]) + parts[3]

    return pl.pallas_call(
        body, name="grads_sum_chips",
        grid_spec=pltpu.PrefetchScalarGridSpec(
            num_scalar_prefetch=1, grid=(nb,),
            in_specs=[pl.BlockSpec((n, tr, cols), lambda t, c_ref: (0, t, 0))],
            out_specs=pl.BlockSpec((tr, cols), lambda t, c_ref: (c_ref[0] * nb + t, 0))),
        out_shape=jax.ShapeDtypeStruct((2 * half, cols), F32),
        compiler_params=_params(("parallel",)),
    )(c_idx, q)


def _join_halves(fulls):
    na = len(fulls)

    def body(*refs):
        o_refs = refs[na:2 * na]
        send_sems, recv_sems = refs[2 * na:]
        x, y, c, _ = _place()
        sends = []
        for a in range(na):
            half = o_refs[a].shape[0] // 2
            rows = o_refs[a].at[pl.ds(c * half, half), :]
            sends.append(pltpu.make_async_remote_copy(
                src_ref=rows, dst_ref=rows, send_sem=send_sems.at[a], recv_sem=recv_sems.at[a],
                device_id=(x, y, 1 - c), device_id_type=MESH))
        for cp in sends:
            cp.start()
        for a in range(na):
            half = o_refs[a].shape[0] // 2
            other = o_refs[a].at[pl.ds((1 - c) * half, half), :]
            pltpu.make_async_remote_copy(
                src_ref=other, dst_ref=other, send_sem=send_sems.at[a], recv_sem=recv_sems.at[a],
                device_id=(x, y, 1 - c), device_id_type=MESH).wait_recv()
        for cp in sends:
            cp.wait_send()

    return pl.pallas_call(
        body, name="grads_join_halves",
        in_specs=[HBM_SPEC] * na, out_specs=[HBM_SPEC] * na,
        out_shape=[jax.ShapeDtypeStruct(f.shape, f.dtype) for f in fulls],
        input_output_aliases={a: a for a in range(na)},
        scratch_shapes=[pltpu.SemaphoreType.DMA((na,)), pltpu.SemaphoreType.DMA((na,))],
    )(*fulls)


def _part_rows(shape):
    size = 1
    for d in shape:
        size *= d
    rows = -(-size // PACK_COLS)
    return size, -(-rows // PACK_ROW_ALIGN) * PACK_ROW_ALIGN


def _pack_rows(arrays, dtype, total_rows):
    parts, used = [], 0
    for a in arrays:
        size, rows = _part_rows(a.shape)
        flat = a.reshape(-1).astype(dtype)
        parts.append(jnp.pad(flat, (0, rows * PACK_COLS - size)).reshape(rows, PACK_COLS))
        used += rows
    parts.append(jnp.zeros((total_rows - used, PACK_COLS), dtype))
    return jnp.concatenate(parts, axis=0)


def _unpack_rows(buf, shapes):
    lead = buf.shape[:-2]
    out, off = [], 0
    for sh in shapes:
        size, rows = _part_rows(sh)
        part = buf[..., off:off + rows, :].reshape(lead + (-1,))[..., :size]
        out.append(part.reshape(lead + tuple(sh)))
        off += rows
    return out


NEW_ORDER = ((928, 1440), (1440, 1952), (1952, 2464), (416, 928), (2464, 2976), (3744, 4256),
             (2976, 3488), (0, 256), (256, 384), (4256, 4320), (384, 416), (4256, 4288),
             (3488, 3616), (3616, 3744))
OLD_ORDER = ((3584, 3840), (3840, 3968), (4032, 4064), (1536, 2048), (0, 512), (512, 1024),
             (1024, 1536), (2048, 2560), (3072, 3584), (4096, 4224), (4224, 4352), (2560, 3072))


def _cols(sources, ranges):
    parts = []
    for a, b in ranges:
        off = 0
        for src in sources:
            width = src.shape[-1]
            lo, hi = max(a, off), min(b, off + width)
            if lo < hi:
                parts.append(src[..., lo - off:hi - off])
            off += width
    return jnp.concatenate(parts, axis=-1)


def _sub_ranges(ranges, a, b):
    out, off = [], 0
    for lo, hi in ranges:
        width = hi - lo
        s0, s1 = max(a, off), min(b, off + width)
        if s0 < s1:
            out.append((lo + s0 - off, lo + s1 - off))
        off += width
    return out


def _rope_tables(s):
    half = MLA_ROPE // 2
    inv_freq = jnp.power(jnp.float32(ROPE_THETA), -jnp.arange(half, dtype=F32) / half)
    ang = inv_freq[:, None] * jnp.arange(s, dtype=F32)[None, :]
    cos, sin = jnp.cos(ang), jnp.sin(ang)
    z = lambda n: jnp.zeros((n, s), F32)
    c = jnp.concatenate([jnp.ones((MLA_NOPE, s), F32), cos, cos, z(32)], axis=0)
    s1 = jnp.concatenate([z(MLA_NOPE), -sin, z(16), z(32)], axis=0)
    s2 = jnp.concatenate([z(MLA_NOPE), z(16), sin, z(32)], axis=0)
    return c, s1, s2


def _pad_lanes(a, n):
    return jnp.pad(a, ((0, 0), (0, n - a.shape[1])))


SHARDED = ("w_in", "w_out", "mla_w_qb", "mla_w_kvb", "conv_w")
REPLICATED = ("norm_g", "mla_q_a_norm", "mla_kv_a_norm", "mla_q_norm", "mla_k_norm",
              "swa_q_norm", "swa_k_norm", "swa_sinks")
WEIGHT_ORDER = ("norm_g", "w_in", "mla_q_a_norm", "mla_w_qb", "mla_kv_a_norm", "mla_w_kvb",
                "mla_q_norm", "mla_k_norm", "conv_w", "swa_q_norm", "swa_k_norm", "swa_sinks", "w_out")
SHARD_AXIS = {"w_in": 2, "w_out": 1, "mla_w_qb": 2, "mla_w_kvb": 2, "conv_w": 2}


def kernel(x, norm_g, w_in, mla_q_a_norm, mla_w_qb, mla_kv_a_norm, mla_w_kvb, mla_q_norm, mla_k_norm, conv_w, swa_q_norm, swa_k_norm, swa_sinks, w_out, loss_target, m_norm_g, m_w_in, m_mla_q_a_norm, m_mla_w_qb, m_mla_kv_a_norm, m_mla_w_kvb, m_mla_q_norm, m_mla_k_norm, m_conv_w, m_swa_q_norm, m_swa_k_norm, m_swa_sinks, m_w_out, v_norm_g, v_w_in, v_mla_q_a_norm, v_mla_w_qb, v_mla_kv_a_norm, v_mla_w_kvb, v_mla_q_norm, v_mla_k_norm, v_conv_w, v_swa_q_norm, v_swa_k_norm, v_swa_sinks, v_w_out):
    weights = dict(norm_g=norm_g, w_in=w_in, mla_q_a_norm=mla_q_a_norm, mla_w_qb=mla_w_qb,
                   mla_kv_a_norm=mla_kv_a_norm, mla_w_kvb=mla_w_kvb, mla_q_norm=mla_q_norm,
                   mla_k_norm=mla_k_norm, conv_w=conv_w, swa_q_norm=swa_q_norm,
                   swa_k_norm=swa_k_norm, swa_sinks=swa_sinks, w_out=w_out)
    mom_m = dict(norm_g=m_norm_g, w_in=m_w_in, mla_q_a_norm=m_mla_q_a_norm, mla_w_qb=m_mla_w_qb,
                 mla_kv_a_norm=m_mla_kv_a_norm, mla_w_kvb=m_mla_w_kvb, mla_q_norm=m_mla_q_norm,
                 mla_k_norm=m_mla_k_norm, conv_w=m_conv_w, swa_q_norm=m_swa_q_norm,
                 swa_k_norm=m_swa_k_norm, swa_sinks=m_swa_sinks, w_out=m_w_out)
    mom_v = dict(norm_g=v_norm_g, w_in=v_w_in, mla_q_a_norm=v_mla_q_a_norm, mla_w_qb=v_mla_w_qb,
                 mla_kv_a_norm=v_mla_kv_a_norm, mla_w_kvb=v_mla_w_kvb, mla_q_norm=v_mla_q_norm,
                 mla_k_norm=v_mla_k_norm, conv_w=v_conv_w, swa_q_norm=v_swa_q_norm,
                 swa_k_norm=v_swa_k_norm, swa_sinks=v_swa_sinks, w_out=v_w_out)
    xs = x[0]
    target = loss_target[0]
    s = xs.shape[0]
    c_idx = lax.axis_index("c").astype(jnp.int32).reshape(1)
    k_idx = (2 * lax.axis_index("x") + lax.axis_index("y")).astype(jnp.int32).reshape(1)

    conv_bits = lax.bitcast_convert_type(conv_w, BF16)
    small_list = [mla_w_qb, mla_w_kvb, conv_bits]
    shard_cols = w_in.shape[2]
    w_in_b = w_in.astype(BF16)
    late = [w_in_b[1], w_out.astype(BF16).reshape(-1, D_MODEL)]
    started = _gather_start(late)
    own = [w_in_b[0], _pack_rows(small_list, BF16, SMALL_ROWS)]
    gathered_in0, gathered_small = [_fill_own_slab(buf, src, k_idx)
                                    for buf, src in zip(_all_gather(own), own)]
    parts = _unpack_rows(gathered_small, [a.shape for a in small_list])
    join = lambda p, axis: jnp.concatenate([p[k] for k in range(N_CHIPS)], axis=axis)
    w_in_zeros = jnp.zeros((D_MODEL, 64), BF16)
    permuted = lambda slabs: _cols([slabs[k] for k in range(N_CHIPS)] + [w_in_zeros], NEW_ORDER)
    w_qb_full = join(parts[0], 2)
    w_kvb_full = join(parts[1], 2)
    conv_full = lax.bitcast_convert_type(join(parts[2], 2), F32)

    rope = _rope_tables(s)
    swa_tables = _swa_tables()
    layers = []
    for l in range(DEPTH):
        wq = jnp.pad(w_qb_full[l].reshape(MLA_Q_LORA, MLA_HEADS, MLA_QK),
                     ((0, 0), (0, 0), (0, LANES - MLA_QK))).reshape(MLA_Q_LORA, MLA_HEADS * LANES)
        kv = w_kvb_full[l].reshape(MLA_KV_LORA, MLA_HEADS, MLA_NOPE + MLA_V)
        wk = jnp.pad(kv[:, :, :MLA_NOPE], ((0, 0), (0, 0), (0, LANES - MLA_NOPE)))
        wkv = jnp.concatenate([wk.reshape(MLA_KV_LORA, MLA_HEADS * LANES),
                               kv[:, :, MLA_NOPE:].reshape(MLA_KV_LORA, MLA_HEADS * MLA_V)], axis=1)
        layers.append(dict(
            wq=wq, wkv=wkv,
            cw=jnp.pad(conv_full[l], ((0, HALO - 3), (0, 0))),
            g=norm_g[l][None],
            mla_norms=(mla_q_a_norm[l][:, None], mla_kv_a_norm[l][:, None],
                       _pad_lanes(mla_q_norm[l][None], LANES).T, _pad_lanes(mla_k_norm[l][None], LANES).T),
            mla_weights=(wq, wkv, wq.T, wkv.T),
            sqn=swa_q_norm[l][:, None], skn=swa_k_norm[l][:, None], sinks=swa_sinks[l][None]))

    saved = []
    h_in = xs
    layers[0]["w_in"] = permuted(gathered_in0)
    layers[0]["g"] = layers[0]["g"] + started[-1][0:1, 0:1]
    for l in range(DEPTH):
        p = layers[l]
        proj, hb = _in_proj_fwd(h_in, p["g"], p["w_in"])
        q, k, v, qt, kt, vt = _mla_prep_fwd(proj, p["mla_norms"], p["mla_weights"], rope)
        o_mla, lse = _mla_attn_fwd(q, k, vt)
        if l == 0:
            late_in1, late_out = [_fill_own_slab(buf, src, k_idx) for buf, src in
                                  zip(_gather_wait(started, len(late), o_mla), late)]
            layers[1]["w_in"] = permuted(late_in1)
            w_out_full = join(late_out.reshape(N_CHIPS, DEPTH, -1, D_MODEL), 1)
            for n in range(DEPTH):
                layers[n]["w_out"] = w_out_full[n]
        o_swa = _swa_fwd(proj, p["sqn"], p["skn"], p["sinks"], swa_tables)
        last = l == DEPTH - 1
        y, z, *loss_acc = _out_fwd(proj, o_mla, o_swa, h_in, p["w_out"], p["cw"],
                                   target if last else None)
        saved.append(dict(x=h_in, proj=proj, hb=hb, q=q, k=k, v=v, qt=qt, kt=kt, o_mla=o_mla, lse=lse,
                          o_swa=o_swa, z=z))
        h_in = y

    dy, loss_acc = h_in, loss_acc[0]
    loss = lax.psum(loss_acc[0, 0], ("x", "y", "c"))

    grads = {n: [None] * DEPTH for n in WEIGHT_ORDER}

    def in_chunks(l):
        return jnp.stack([_cols(grads["w_in"][l], _sub_ranges(OLD_ORDER, k * shard_cols,
                                                              (k + 1) * shard_cols))
                          for k in range(N_CHIPS)])

    def out_chunks(l):
        return grads["w_out"][l].reshape(N_CHIPS, -1, D_MODEL)

    for l in reversed(range(DEPTH)):
        p, a = layers[l], saved[l]
        dconv, dgates, do_mla, do_mla_t, delta, do_swa, dcw = _out_bwd(dy, a["proj"], a["o_mla"], a["o_swa"],
                                                             p["w_out"], p["cw"])
        if l == 0:
            late_gs, from_sib = _split_wait("late_grads_to_sibling_wait", late_st1, 2,
                                            _plan_to_sibling, dconv)
            late_p = [_add_sibling(g, r, c_idx, BF16) for g, r in zip(late_gs, from_sib)]
            late_st2 = _split_start("late_grads_scatter_start", late_p,
                                    [(v.shape, v.dtype) for v in late_p], 6, _plan_scatter)
        grads["w_out"][l] = _weight_grads(a["z"], [dy], "dw_out")[0]
        grads["conv_w"][l] = dcw[0:3]
        delta_rows = jnp.transpose(delta, (1, 0)).reshape(MLA_HEADS // 2, 2, s)
        dq, dk, dv = _mla_attn_bwd(a["q"], a["qt"], a["k"], a["kt"], a["v"], do_mla, do_mla_t,
                                   a["lse"], delta_rows)
        if l == 0:
            late_p, late_q = _split_wait("late_grads_scatter_wait", late_st2, 2, _plan_scatter, dq)
            late_full = [_sum_chips(_fill_own_slab(q_, p_, k_idx), c_idx)
                         for q_, p_ in zip(late_q, late_p)]
        dmla, dqan, dkvan, dqn, dkn, dwq_t, dwkv_t = _mla_prep_bwd(
            a["proj"], p["mla_norms"], p["mla_weights"], rope, dq, dk, dv)
        dwq, dwkv = dwq_t.T, dwkv_t.T
        dsq, dskv, dsqn, dskn, dsinks = _swa_bwd(a["proj"], p["sqn"], p["skn"], p["sinks"], swa_tables, do_swa)
        pieces = [dconv, dgates, dsq, dmla, dskv]
        dx, dg = _in_proj_bwd(pieces, a["x"], p["g"], p["w_in"], dy)
        grads["w_in"][l] = _weight_grads(a["hb"], pieces, "dw_in")
        grads["norm_g"][l] = dg[0]
        grads["mla_q_a_norm"][l] = dqan[:, 0]
        grads["mla_kv_a_norm"][l] = dkvan[:, 0]
        grads["mla_q_norm"][l] = dqn[:MLA_QK, 0]
        grads["mla_k_norm"][l] = dkn[:MLA_QK, 0]
        grads["mla_w_qb"][l] = dwq.reshape(MLA_Q_LORA, MLA_HEADS, LANES)[:, :, :MLA_QK].reshape(
            MLA_Q_LORA, MLA_HEADS * MLA_QK)
        dwk = dwkv[:, :MLA_HEADS * LANES].reshape(MLA_KV_LORA, MLA_HEADS, LANES)[:, :, :MLA_NOPE]
        dwv = dwkv[:, MLA_HEADS * LANES:].reshape(MLA_KV_LORA, MLA_HEADS, MLA_V)
        grads["mla_w_kvb"][l] = jnp.concatenate([dwk, dwv], axis=2).reshape(
            MLA_KV_LORA, MLA_HEADS * (MLA_NOPE + MLA_V))
        grads["swa_q_norm"][l] = dsqn[:, 0]
        grads["swa_k_norm"][l] = dskn[:, 0]
        grads["swa_sinks"][l] = dsinks[0]
        dy = dx
        if l == DEPTH - 1:
            late_st1 = _split_start(
                "late_grads_to_sibling_start", [in_chunks(l), out_chunks(l)],
                [((N_CHIPS, D_MODEL // 2, shard_cols), F32),
                 ((N_CHIPS, D_MIX // N_CHIPS // 2, D_MODEL), F32)], 2, _plan_to_sibling)
    grad_x = dy[None]
    full_grads = {n: jnp.stack(grads[n]) for n in WEIGHT_ORDER if n not in ("w_in", "w_out")}

    rest = tuple(n for n in SHARDED if n not in ("w_in", "w_out"))
    rep_shapes = [weights[n].shape for n in REPLICATED]
    rep_grads = jnp.concatenate([full_grads[n].reshape(-1) for n in REPLICATED])

    def chunk(g, n, k):
        width = g.shape[SHARD_AXIS[n]] // N_CHIPS
        return lax.slice_in_dim(g, k * width, (k + 1) * width, axis=SHARD_AXIS[n])

    g_small = jnp.stack([_pack_rows([chunk(full_grads[n], n, k) for n in rest] + [rep_grads],
                                    F32, SMALL_ROWS) for k in range(N_CHIPS)])
    gs = (in_chunks(0), out_chunks(0), g_small)
    from_sibling = _swap_halves_to_sibling(gs)
    partial = [_add_sibling(g, r, c_idx, dt) for g, r, dt in zip(gs, from_sibling, (BF16, BF16, F32))]
    by_chip = [_fill_own_slab(q, p, k_idx) for q, p in zip(_scatter_to_chips(partial), partial)]
    g_in0, g_out0, g_small_mine, g_in1, g_out1 = _join_halves(
        [_sum_chips(q, c_idx) for q in by_chip] + late_full)

    vals = _unpack_rows(g_small_mine, [weights[n].shape for n in rest] + [(rep_grads.shape[0],)])
    grad = dict(zip(rest, vals[:-1]))
    grad["w_in"] = jnp.stack([g_in0, g_in1])
    grad["w_out"] = jnp.stack([g_out0, g_out1])
    off = 0
    for n, sh in zip(REPLICATED, rep_shapes):
        grad[n] = vals[-1][off:off + sh[0] * sh[1]].reshape(sh)
        off += sh[0] * sh[1]
    results = {}
    for n in ("w_in", "w_out"):
        view = lambda a, n=n: a.reshape(-1, weights[n].shape[-1])
        res = _adamw(view(grad[n]), view(weights[n]), view(mom_m[n]), view(mom_v[n]))
        results[n] = [r.reshape(weights[n].shape) for r in res]
    small = tuple(n for n in WEIGHT_ORDER if n not in results)
    res = _adamw_small(*([d[n] for n in small] for d in (grad, weights, mom_m, mom_v)))
    for a, n in enumerate(small):
        results[n] = [res[kind][a] for kind in range(3)]
    unpacked = [grad] + [{n: results[n][kind] for n in WEIGHT_ORDER} for kind in range(3)]
    outs = [loss, grad_x]
    for group in unpacked:
        outs += [group[n] for n in WEIGHT_ORDER]
    return tuple(outs)
```

```python
import jax
import numpy as np
import jax.numpy as jnp
from jax import lax
from jax.experimental import pallas as pl
from jax.experimental.pallas import tpu as pltpu

F32 = jnp.float32
BF16 = jnp.bfloat16

D_MODEL = 1024
DEPTH = 2
GROUP = 512
D_MIX = 3 * GROUP
BLOCK = 128
RMS_EPS = 1e-6
NEG_INF = -1e30
MLA_HEADS = 8
MLA_QK = 96
MLA_NOPE = 64
MLA_ROPE = 32
MLA_V = 64
V_AUG = 80
MLA_Q_LORA = 256
MLA_KV_LORA = 128
ROPE_THETA = 10000.0
SWA_HEADS = 8
SWA_KV = 2
SWA_GROUP = 4
SWA_DIM = 64
N_CHIPS = 4

NC = 4352
OFF_SQ, OFF_MLA, OFF_SKV = 3072, 3584, 4096

VMEM_LIMIT = 56 * 1024 * 1024
LANES = 128
PACK_COLS = 1024
PACK_ROW_ALIGN = 16
SMALL_ROWS = 256

ADAM_LR = 0.001
ADAM_B1 = 0.9
ADAM_B2 = 0.999
ADAM_EPS = 1e-08
ADAM_WD = 0.01
ADAM_STEP = 10

MESH = pl.DeviceIdType.MESH


def _params(sem, vmem=VMEM_LIMIT):
    return pltpu.CompilerParams(dimension_semantics=sem, vmem_limit_bytes=vmem)


def _dot(a, b, dims):
    return lax.dot_general(a.astype(BF16), b.astype(BF16), (dims, ((), ())),
                           preferred_element_type=F32)


def _mm(a, b):
    return _dot(a, b, ((1,), (0,)))


def _mm_nt(a, b):
    return _dot(a, b, ((1,), (1,)))


def _rms(x, g, n=None):
    n = x.shape[-1] if n is None else n
    ms = jnp.sum(x * x, axis=-1, keepdims=True) * (1.0 / n)
    return x * lax.rsqrt(ms + RMS_EPS) * g


def _sigmoid(x):
    return 1.0 / (1.0 + jnp.exp(-x))


def _in_proj_fwd(x, g, w):
    s = x.shape[0]
    tm = min(512, s)

    def body(x_ref, g_ref, w_ref, proj_ref, hbt_ref):
        h = _rms(x_ref[...], g_ref[...])
        hbt_ref[...] = jnp.transpose(h).astype(BF16)
        proj_ref[...] = jnp.dot(h.astype(BF16), w_ref[...], preferred_element_type=F32)

    return pl.pallas_call(
        body, name="in_proj_fwd", grid=(s // tm,),
        in_specs=[pl.BlockSpec((tm, D_MODEL), lambda i: (i, 0)),
                  pl.BlockSpec((1, D_MODEL), lambda i: (0, 0)),
                  pl.BlockSpec((D_MODEL, NC), lambda i: (0, 0))],
        out_specs=[pl.BlockSpec((tm, NC), lambda i: (i, 0)),
                   pl.BlockSpec((D_MODEL, tm), lambda i: (0, i))],
        out_shape=[jax.ShapeDtypeStruct((s, NC), F32), jax.ShapeDtypeStruct((D_MODEL, s), BF16)],
        compiler_params=_params(("parallel",)),
    )(x, g, w)


def _in_proj_bwd(pieces, x, g, w, dres):
    s = x.shape[0]
    tm = min(512, s)
    n_p = len(pieces)

    def body(*refs):
        p_refs = refs[:n_p]
        x_ref, g_ref, w_ref, dres_ref, dx_ref, dg_ref = refs[n_p:]
        dh = None
        off = 0
        for r in p_refs:
            width = r.shape[1]
            t = _mm_nt(r[...], w_ref[:, off:off + width])
            dh = t if dh is None else dh + t
            off += width
        _, vjp = jax.vjp(_rms, x_ref[...], g_ref[...])
        dx, dg = vjp(dh)
        dx_ref[...] = dx + dres_ref[...]

        @pl.when(pl.program_id(0) == 0)
        def _():
            dg_ref[...] = jnp.zeros_like(dg_ref)

        dg_ref[...] += dg

    in_specs = [pl.BlockSpec((tm, p.shape[1]), lambda i: (i, 0)) for p in pieces]
    in_specs += [pl.BlockSpec((tm, D_MODEL), lambda i: (i, 0)),
                 pl.BlockSpec((1, D_MODEL), lambda i: (0, 0)),
                 pl.BlockSpec((D_MODEL, NC), lambda i: (0, 0)),
                 pl.BlockSpec((tm, D_MODEL), lambda i: (i, 0))]
    return pl.pallas_call(
        body, name="in_proj_bwd", grid=(s // tm,),
        in_specs=in_specs,
        out_specs=[pl.BlockSpec((tm, D_MODEL), lambda i: (i, 0)),
                   pl.BlockSpec((1, D_MODEL), lambda i: (0, 0))],
        out_shape=[jax.ShapeDtypeStruct((s, D_MODEL), F32), jax.ShapeDtypeStruct((1, D_MODEL), F32)],
        compiler_params=_params(("arbitrary",)),
    )(*pieces, x, g, w, dres)


def _weight_grads(at, bs, name):
    m, s = at.shape
    nb = len(bs)
    tk = min(512, s)

    def body(a_ref, *refs):
        b_refs, o_refs = refs[:nb], refs[nb:]

        @pl.when(pl.program_id(0) == 0)
        def _():
            for o_ref in o_refs:
                o_ref[...] = jnp.zeros_like(o_ref)

        a = a_ref[...]
        for b_ref, o_ref in zip(b_refs, o_refs):
            o_ref[...] += _mm(a, b_ref[...])

    return pl.pallas_call(
        body, name=name, grid=(s // tk,),
        in_specs=[pl.BlockSpec((m, tk), lambda k: (0, k))]
        + [pl.BlockSpec((tk, b.shape[1]), lambda k: (k, 0)) for b in bs],
        out_specs=[pl.BlockSpec((m, b.shape[1]), lambda k: (0, 0)) for b in bs],
        out_shape=[jax.ShapeDtypeStruct((m, b.shape[1]), F32) for b in bs],
        compiler_params=_params(("arbitrary",)),
    )(at, *bs)


def _rms0(x, g, n=None):
    n = x.shape[0] if n is None else n
    ms = jnp.sum(x * x, axis=0, keepdims=True) * (1.0 / n)
    return x * lax.rsqrt(ms + RMS_EPS) * g


@jax.custom_vjp
def _rope0(t, c, s1, s2):
    return t * c + pltpu.roll(t, LANES - 16, 0) * s1 + pltpu.roll(t, 16, 0) * s2


def _rope0_fwd(t, c, s1, s2):
    return _rope0(t, c, s1, s2), (c, s1, s2)


def _rope0_bwd(res, g):
    c, s1, s2 = res
    dt = g * c + pltpu.roll(g * s1, 16, 0) + pltpu.roll(g * s2, LANES - 16, 0)
    return dt, jnp.zeros_like(c), jnp.zeros_like(s1), jnp.zeros_like(s2)


_rope0.defvjp(_rope0_fwd, _rope0_bwd)


@jax.custom_vjp
def _mmw(w, wt, x):
    return _mm(w, x)


def _mmw_fwd(w, wt, x):
    return _mm(w, x), (wt, x)


def _mmw_bwd(res, g):
    wt, x = res
    return _mm_nt(g, x), jnp.zeros_like(wt), _mm(wt, g)


_mmw.defvjp(_mmw_fwd, _mmw_bwd)


def _prep_fn(q_lat, kv_lat, kr, qan, kvan, qn, kn, wq, wk, wv, wqt, wkt, wvt, c, s1, s2, mm):
    tokens = q_lat.shape[1]
    rq = _rms0(q_lat, qan)
    rkv = _rms0(kv_lat, kvan)
    qn_b = jnp.broadcast_to(qn, (LANES, tokens))
    kn_b = jnp.broadcast_to(kn, (LANES, tokens))
    qs, ks = [], []
    for h in range(MLA_HEADS):
        qs.append(_rope0(_rms0(mm(wq[h], wqt[h], rq), qn_b, MLA_QK), c, s1, s2))
        ks.append(_rope0(_rms0(mm(wk[h], wkt[h], rkv) + kr, kn_b, MLA_QK), c, s1, s2))
    return tuple(qs), tuple(ks), mm(wv, wvt, rkv)


def _prep_weights(wq_ref, wkv_ref, wqt_ref, wkvt_ref):
    heads = range(MLA_HEADS)
    wq = tuple(wqt_ref[LANES * h:LANES * (h + 1), :].astype(F32) for h in heads)
    wk = tuple(wkvt_ref[LANES * h:LANES * (h + 1), :].astype(F32) for h in heads)
    wv = wkvt_ref[LANES * MLA_HEADS:, :].astype(F32)
    wqt = tuple(wq_ref[:, LANES * h:LANES * (h + 1)].astype(F32) for h in heads)
    wkt = tuple(wkv_ref[:, LANES * h:LANES * (h + 1)].astype(F32) for h in heads)
    wvt = wkv_ref[:, LANES * MLA_HEADS:].astype(F32)
    return wq, wk, wv, wqt, wkt, wvt


def _prep_in_specs(tm):
    const = lambda shape: pl.BlockSpec(shape, lambda i: (0, 0))
    col = lambda height: pl.BlockSpec((height, tm), lambda i: (0, i))
    return [pl.BlockSpec((tm, 512), lambda i: (i, OFF_MLA // 512)),
            const((MLA_Q_LORA, 1)), const((MLA_KV_LORA, 1)), const((LANES, 1)), const((LANES, 1)),
            const((MLA_Q_LORA, 1024)), const((MLA_KV_LORA, 1536)),
            const((1024, MLA_Q_LORA)), const((1536, MLA_KV_LORA)),
            col(LANES), col(LANES), col(LANES)]


def _prep_operands(blk_ref, refs):
    qan_ref, kvan_ref, qn_ref, kn_ref, wq_ref, wkv_ref, wqt_ref, wkvt_ref, c_ref, s1_ref, s2_ref = refs
    blk_t = jnp.transpose(blk_ref[...])
    diff = (blk_t[0:256], blk_t[256:384], blk_t[384:512],
            qan_ref[...], kvan_ref[...], qn_ref[...], kn_ref[...])
    weights = _prep_weights(wq_ref, wkv_ref, wqt_ref, wkvt_ref)
    return diff, weights, (c_ref[...], s1_ref[...], s2_ref[...])


def _mla_prep_fwd(proj, norms, weights, rope):
    s = proj.shape[0]
    tm = min(512, s)

    def body(blk_ref, *refs):
        ins, (q_ref, k_ref, v_ref, qt_ref, kt_ref, vt_ref) = refs[:11], refs[11:]
        diff, (wq, wk, wv, wqt, wkt, wvt), tables = _prep_operands(blk_ref, ins)
        qs, ks, v = _prep_fn(*diff, wq, wk, wv, wqt, wkt, wvt, *tables,
                             lambda w, wt, x: _mm(w, x))
        for h in range(MLA_HEADS):
            q2 = qs[h] * Q_PRESCALE
            qt_ref[LANES * h:LANES * (h + 1), :] = q2.astype(BF16)
            kt_ref[LANES * h:LANES * (h + 1), :] = ks[h].astype(BF16)
            q_ref[:, LANES * h:LANES * (h + 1)] = jnp.transpose(q2).astype(BF16)
            k_ref[:, LANES * h:LANES * (h + 1)] = jnp.transpose(ks[h]).astype(BF16)
        ones_row = (lax.broadcasted_iota(jnp.int32, (V_AUG - MLA_V, v.shape[1]), 0) == 0).astype(BF16)
        for h in range(MLA_HEADS):
            vt_ref[V_AUG * h:V_AUG * h + MLA_V, :] = v[MLA_V * h:MLA_V * (h + 1)].astype(BF16)
            vt_ref[V_AUG * h + MLA_V:V_AUG * (h + 1), :] = ones_row
        v_ref[...] = jnp.transpose(v).astype(BF16)

    row = lambda width: pl.BlockSpec((tm, width), lambda i: (i, 0))
    col = lambda height: pl.BlockSpec((height, tm), lambda i: (0, i))
    return pl.pallas_call(
        body, name="mla_prep_fwd", grid=(s // tm,),
        in_specs=_prep_in_specs(tm),
        out_specs=[row(1024), row(1024), row(512), col(1024), col(1024), col(MLA_HEADS * V_AUG)],
        out_shape=[jax.ShapeDtypeStruct((s, 1024), BF16), jax.ShapeDtypeStruct((s, 1024), BF16),
                   jax.ShapeDtypeStruct((s, 512), BF16), jax.ShapeDtypeStruct((1024, s), BF16),
                   jax.ShapeDtypeStruct((1024, s), BF16),
                   jax.ShapeDtypeStruct((MLA_HEADS * V_AUG, s), BF16)],
        compiler_params=_params(("parallel",)),
    )(proj, *norms, *weights, *rope)


def _mla_prep_bwd(proj, norms, weights, rope, dq, dk, dv):
    s = proj.shape[0]
    tm = min(512, s)

    def body(blk_ref, *refs):
        ins, (dq_ref, dk_ref, dv_ref) = refs[:11], refs[11:14]
        dblk_ref, dqan_ref, dkvan_ref, dqn_ref, dkn_ref, dwq_ref, dwkv_ref = refs[14:]
        diff, (wq, wk, wv, wqt, wkt, wvt), tables = _prep_operands(blk_ref, ins)

        def fn(q_lat, kv_lat, kr, qan, kvan, qn, kn, wq_, wk_, wv_):
            return _prep_fn(q_lat, kv_lat, kr, qan, kvan, qn, kn, wq_, wk_, wv_, wqt, wkt, wvt,
                            *tables, _mmw)

        _, vjp = jax.vjp(fn, *diff, wq, wk, wv)
        heads = range(MLA_HEADS)
        cts = (tuple(dq_ref[LANES * h:LANES * (h + 1), :] for h in heads),
               tuple(dk_ref[LANES * h:LANES * (h + 1), :] for h in heads), dv_ref[...])
        dq_lat, dkv_lat, dkr, dqan, dkvan, dqn, dkn, dwq_h, dwk_h, dwv = vjp(cts)
        dblk_ref[...] = jnp.transpose(
            jnp.concatenate([dq_lat, dkv_lat, dkr], axis=0)).astype(BF16)

        @pl.when(pl.program_id(0) == 0)
        def _():
            for r in (dqan_ref, dkvan_ref, dqn_ref, dkn_ref, dwq_ref, dwkv_ref):
                r[...] = jnp.zeros_like(r)

        dqan_ref[...] += dqan
        dkvan_ref[...] += dkvan
        dqn_ref[...] += dqn
        dkn_ref[...] += dkn
        for h in heads:
            dwq_ref[LANES * h:LANES * (h + 1), :] += dwq_h[h]
            dwkv_ref[LANES * h:LANES * (h + 1), :] += dwk_h[h]
        dwkv_ref[LANES * MLA_HEADS:, :] += dwv

    const = lambda shape: pl.BlockSpec(shape, lambda i: (0, 0))
    col = lambda height: pl.BlockSpec((height, tm), lambda i: (0, i))
    shapes = [(MLA_Q_LORA, 1), (MLA_KV_LORA, 1), (LANES, 1), (LANES, 1),
              (1024, MLA_Q_LORA), (1536, MLA_KV_LORA)]
    return pl.pallas_call(
        body, name="mla_prep_bwd", grid=(s // tm,),
        in_specs=_prep_in_specs(tm) + [col(1024), col(1024), col(512)],
        out_specs=[pl.BlockSpec((tm, 512), lambda i: (i, 0))] + [const(sh) for sh in shapes],
        out_shape=[jax.ShapeDtypeStruct((s, 512), BF16)]
        + [jax.ShapeDtypeStruct(sh, F32) for sh in shapes],
        compiler_params=_params(("arbitrary",)),
    )(proj, *norms, *weights, *rope, dq, dk, dv)


MLA_SCALE = MLA_QK ** -0.5


LOG2E = 1.4426950408889634
LN2 = 0.6931471805599453
Q_PRESCALE = MLA_SCALE * LOG2E
HEAD_GROUPS = ((0, 1),)


def _mla_attn_fwd(q2, k, vt):
    s = q2.shape[0]
    t = min(512, s)
    tk = min(128, s)
    nq = s // t
    r = t // tk

    def body(q_ref, k_ref, vt_ref, o_ref, lse_ref, acc_ref):
        i = pl.program_id(1)
        row = lax.broadcasted_iota(jnp.int32, (tk, t), 0)
        col = lax.broadcasted_iota(jnp.int32, (tk, t), 1)
        qh = [q_ref[:, LANES * hh:LANES * (hh + 1)] for hh in range(2)]
        acc_ref[...] = jnp.zeros_like(acc_ref)

        def scores(j, heads, diag=None):
            r0 = pl.multiple_of(j * tk, tk)
            q_lo = 0 if diag is None else diag * tk
            out = []
            for hh in heads:
                kc = k_ref[pl.ds(r0, tk), LANES * hh:LANES * (hh + 1)]
                sc = lax.dot_general(kc, qh[hh][q_lo:], (((1,), (1,)), ((), ())),
                                     preferred_element_type=F32)
                out.append(sc if diag is None
                           else jnp.where(row[:, :t - q_lo] <= col[:, :t - q_lo], sc, NEG_INF))
            return tuple(out)

        for heads in HEAD_GROUPS:
            stats = tuple(jnp.full((1, t), NEG_INF, F32) for _ in heads)

            def consume(j, scs, stats, q_lo=0, heads=heads):
                r0 = pl.multiple_of(j * tk, tk)
                out, ps, alphas = [], [], []
                for n, hh in enumerate(heads):
                    m_old = stats[n][:, q_lo:]
                    m_new = jnp.maximum(m_old, jnp.max(scs[n], axis=0, keepdims=True))
                    ps.append(jnp.exp2(scs[n] - m_new).astype(BF16))
                    alphas.append(jnp.exp2(m_old - m_new))
                    out.append(m_new if q_lo == 0
                               else jnp.concatenate([stats[n][:, :q_lo], m_new], axis=1))
                for n, hh in enumerate(heads):
                    vc = vt_ref[V_AUG * hh:V_AUG * (hh + 1), pl.ds(r0, tk)]
                    acc_ref[hh, :, q_lo:] = alphas[n] * acc_ref[hh, :, q_lo:] + jnp.dot(
                        vc, ps[n], preferred_element_type=F32)
                return tuple(out)

            def group(j0, stats, diag, heads=heads):
                scs = [scores(j0 + d, heads, d if diag else None) for d in range(r)]
                for d in range(r):
                    stats = consume(j0 + d, scs[d], stats, d * tk if diag else 0)
                return stats

            stats = group(r * i, stats, True)
            stats = lax.fori_loop(0, i, lambda j, st: group(r * j, st, False), stats)
            for n, hh in enumerate(heads):
                l = acc_ref[hh, MLA_V:MLA_V + 1, :]
                o_ref[:, MLA_V * hh:MLA_V * (hh + 1)] = jnp.transpose(acc_ref[hh, 0:MLA_V, :] / l)
                lse_ref[0, hh:hh + 1, :] = stats[n] + jnp.log2(l)

    return pl.pallas_call(
        body, name="mla_attn_fwd", grid=(MLA_HEADS // 2, nq),
        in_specs=[pl.BlockSpec((t, 256), lambda p, i: (i, p)),
                  pl.BlockSpec((s, 256), lambda p, i: (0, p)),
                  pl.BlockSpec((2 * V_AUG, s), lambda p, i: (p, 0))],
        out_specs=[pl.BlockSpec((t, 128), lambda p, i: (i, p)),
                   pl.BlockSpec((1, 2, t), lambda p, i: (p, 0, i))],
        out_shape=[jax.ShapeDtypeStruct((s, 512), F32),
                   jax.ShapeDtypeStruct((MLA_HEADS // 2, 2, s), F32)],
        scratch_shapes=[pltpu.VMEM((2, V_AUG, t), F32)],
        compiler_params=_params(("parallel", "arbitrary")),
    )(q2, k, vt)


def _mla_attn_bwd(q2, q2t, k, kt, v, do, dot, lse_rows, delta_rows):
    s = q2.shape[0]
    t = min(512, s)
    nq = s // t

    def body(q_ref, qt_ref, k_ref, kt_ref, v_ref, do_ref, dot_ref, lse_ref, dl_ref,
             dq_ref, dk_ref, dv_ref):
        j = pl.program_id(1)

        @pl.when(j == 0)
        def _():
            dq_ref[...] = jnp.zeros_like(dq_ref)

        dk_ref[...] = jnp.zeros_like(dk_ref)
        dv_ref[...] = jnp.zeros_like(dv_ref)
        row = lax.broadcasted_iota(jnp.int32, (t, t), 0)
        col = lax.broadcasted_iota(jnp.int32, (t, t), 1)
        causal_t = row <= col
        kh = [k_ref[:, LANES * hh:LANES * (hh + 1)] for hh in range(2)]
        kth = [kt_ref[LANES * hh:LANES * (hh + 1), :] for hh in range(2)]
        vh = [v_ref[:, MLA_V * hh:MLA_V * (hh + 1)] for hh in range(2)]
        nt = (((1,), (1,)), ((), ()))

        def step(i, masked):
            r0 = pl.multiple_of(i * t, t)
            sd = []
            for hh in range(2):
                qh = q_ref[pl.ds(r0, t), LANES * hh:LANES * (hh + 1)]
                doh = do_ref[pl.ds(r0, t), MLA_V * hh:MLA_V * (hh + 1)]
                sc_t = lax.dot_general(kh[hh], qh, nt, preferred_element_type=F32)
                sd.append(jnp.where(causal_t, sc_t, NEG_INF) if masked else sc_t)
                sd.append(lax.dot_general(vh[hh], doh, nt, preferred_element_type=F32))
            for hh in range(2):
                lse = lse_ref[0, hh:hh + 1, pl.ds(r0, t)]
                dl = dl_ref[0, hh:hh + 1, pl.ds(r0, t)]
                p_t = jnp.exp2(sd[2 * hh] - lse)
                g_t = (p_t * (sd[2 * hh + 1] - dl)).astype(BF16)
                qth = qt_ref[LANES * hh:LANES * (hh + 1), pl.ds(r0, t)]
                doth = dot_ref[MLA_V * hh:MLA_V * (hh + 1), pl.ds(r0, t)]
                dv_ref[MLA_V * hh:MLA_V * (hh + 1), :] += lax.dot_general(
                    doth, p_t.astype(BF16), nt, preferred_element_type=F32)
                dk_ref[LANES * hh:LANES * (hh + 1), :] += lax.dot_general(
                    qth, g_t, nt, preferred_element_type=F32)
                dq_ref[LANES * hh:LANES * (hh + 1), pl.ds(r0, t)] += jnp.dot(
                    kth[hh], g_t, preferred_element_type=F32)

        step(j, True)

        def trip(i, carry):
            step(i, False)
            return carry

        lax.fori_loop(j + 1, nq, trip, 0)
        dk_ref[...] = dk_ref[...] * LN2

        @pl.when(j == nq - 1)
        def _():
            dq_ref[...] = dq_ref[...] * MLA_SCALE

    return pl.pallas_call(
        body, name="mla_attn_bwd", grid=(MLA_HEADS // 2, nq),
        in_specs=[pl.BlockSpec((s, 256), lambda p, j: (0, p)),
                  pl.BlockSpec((256, s), lambda p, j: (p, 0)),
                  pl.BlockSpec((t, 256), lambda p, j: (j, p)),
                  pl.BlockSpec((256, t), lambda p, j: (p, j)),
                  pl.BlockSpec((t, 128), lambda p, j: (j, p)),
                  pl.BlockSpec((s, 128), lambda p, j: (0, p)),
                  pl.BlockSpec((128, s), lambda p, j: (p, 0)),
                  pl.BlockSpec((1, 2, s), lambda p, j: (p, 0, 0)),
                  pl.BlockSpec((1, 2, s), lambda p, j: (p, 0, 0))],
        out_specs=[pl.BlockSpec((256, s), lambda p, j: (p, 0)),
                   pl.BlockSpec((256, t), lambda p, j: (p, j)),
                   pl.BlockSpec((128, t), lambda p, j: (p, j))],
        out_shape=[jax.ShapeDtypeStruct((1024, s), F32), jax.ShapeDtypeStruct((1024, s), F32),
                   jax.ShapeDtypeStruct((512, s), F32)],
        compiler_params=_params(("parallel", "arbitrary")),
    )(q2, q2t, k, kt, v, do, dot, lse_rows, delta_rows)


SWA_SCALE = SWA_DIM ** -0.5
SWA_COLS = SWA_GROUP * BLOCK
SWA_LOG2 = SWA_SCALE * LOG2E


def _swa_tables():
    k = np.arange(2 * BLOCK)[:, None]
    col = np.arange(SWA_COLS)[None, :]
    dist = BLOCK + (col % BLOCK) - k
    valid = (dist >= 0) & (dist < BLOCK)
    out = np.zeros((2, SWA_KV, 2 * BLOCK, SWA_COLS), np.float32)
    for first in range(2):
        ok = valid & ((k >= BLOCK) | (first == 0))
        for j in range(SWA_KV):
            slope = 2.0 ** -(SWA_GROUP * j + col // BLOCK + 1)
            out[first, j] = np.where(ok, -slope * dist * LOG2E, NEG_INF)
    return jnp.asarray(out)


def _swa_tile_inputs(sq_ref, skv_ref, halo_ref, qn_ref, kn_ref, sk_ref, add_ref, first):
    tokens = sq_ref.shape[0]
    kv_all = jnp.concatenate([halo_ref[...], skv_ref[...]], axis=0)
    kv_t = jnp.transpose(kv_all)
    sq_t = jnp.transpose(sq_ref[...])
    k_raw = [kv_t[SWA_DIM * j:SWA_DIM * (j + 1)] for j in range(SWA_KV)]
    v_t = [kv_t[128 + SWA_DIM * j:128 + SWA_DIM * (j + 1)] for j in range(SWA_KV)]
    v_nat = [kv_all[:, 128 + SWA_DIM * j:128 + SWA_DIM * (j + 1)] for j in range(SWA_KV)]
    q_raw = [sq_t[SWA_DIM * h:SWA_DIM * (h + 1)] for h in range(SWA_HEADS)]
    qn_b = jnp.broadcast_to(qn_ref[...], (SWA_DIM, tokens))
    kn_b = jnp.broadcast_to(kn_ref[...], (SWA_DIM, tokens + BLOCK))
    lane_grp = lax.broadcasted_iota(jnp.int32, (1, SWA_COLS), 1) // BLOCK
    sinks, adds = [], []
    for j in range(SWA_KV):
        row = jnp.zeros((1, SWA_COLS), F32)
        for g in range(SWA_GROUP):
            h = SWA_GROUP * j + g
            row = jnp.where(lane_grp == g, sk_ref[:, h:h + 1] * LOG2E, row)
        sinks.append(row)
        adds.append((jnp.where(first, add_ref[1, j], add_ref[0, j]), add_ref[0, j]))
    return k_raw, v_t, v_nat, q_raw, qn_b, kn_b, sinks, adds


def _swa_probs(kb, qs_t, add, sink):
    s2 = jnp.dot(kb, qs_t, preferred_element_type=F32) * SWA_LOG2 + add
    m = jnp.maximum(jnp.max(s2, axis=0, keepdims=True), sink)
    e = jnp.exp2(s2 - m)
    es = jnp.exp2(sink - m)
    inv = 1.0 / (jnp.sum(e, axis=0, keepdims=True) + es)
    return e, inv, es


def _swa_queries(qn_t, j, b):
    return jnp.concatenate([qn_t[SWA_GROUP * j + g][:, BLOCK * b:BLOCK * (b + 1)]
                            for g in range(SWA_GROUP)], axis=1)


def _swa_fwd(proj, qn, kn, sinks, tables):
    s = proj.shape[0]
    ts = min(1024, s)
    nb = ts // BLOCK

    def body(sq_ref, skv_ref, halo_ref, qn_ref, kn_ref, sk_ref, add_ref, o_ref, ot_ref):
        first = pl.program_id(0) == 0
        k_raw, v_t, _, q_raw, qn_b, kn_b, sink_rows, adds = _swa_tile_inputs(
            sq_ref, skv_ref, halo_ref, qn_ref, kn_ref, sk_ref, add_ref, first)
        kn_nat = [jnp.transpose(_rms0(k, kn_b)).astype(BF16) for k in k_raw]
        v_t = [v.astype(BF16) for v in v_t]
        qn_t = [_rms0(q, qn_b).astype(BF16) for q in q_raw]
        for b in range(nb):
            band = slice(BLOCK * b, BLOCK * (b + 2))
            for j in range(SWA_KV):
                e, inv, _ = _swa_probs(kn_nat[j][band], _swa_queries(qn_t, j, b),
                                       adds[j][0 if b == 0 else 1], sink_rows[j])
                o_t = jnp.dot(v_t[j][:, band], (e * inv).astype(BF16),
                              preferred_element_type=F32)
                for g in range(SWA_GROUP):
                    h = SWA_GROUP * j + g
                    ot_ref[SWA_DIM * h:SWA_DIM * (h + 1), BLOCK * b:BLOCK * (b + 1)] = (
                        o_t[:, BLOCK * g:BLOCK * (g + 1)])
        o_ref[...] = jnp.transpose(ot_ref[...])

    const = lambda shape: pl.BlockSpec(shape, lambda i: (0,) * len(shape))
    return pl.pallas_call(
        body, name="swa_fwd", grid=(s // ts,),
        in_specs=[pl.BlockSpec((ts, 512), lambda i: (i, OFF_SQ // 512)),
                  pl.BlockSpec((ts, 256), lambda i: (i, OFF_SKV // 256)),
                  pl.BlockSpec((BLOCK, 256), lambda i: (jnp.maximum(i * nb - 1, 0), OFF_SKV // 256)),
                  const((SWA_DIM, 1)), const((SWA_DIM, 1)), const((1, SWA_HEADS)),
                  const(tables.shape)],
        out_specs=pl.BlockSpec((ts, 512), lambda i: (i, 0)),
        out_shape=jax.ShapeDtypeStruct((s, 512), F32),
        scratch_shapes=[pltpu.VMEM((512, ts), F32)],
        compiler_params=_params(("parallel",)),
    )(proj, proj, proj, qn, kn, sinks, tables)


def _swa_bwd(proj, qn, kn, sinks, tables, do):
    s = proj.shape[0]
    ts = min(1024, s)
    nb = ts // BLOCK
    nt = s // ts

    def body(sq_ref, skv_ref, halo_ref, qn_ref, kn_ref, sk_ref, add_ref, do_ref,
             dsq_ref, dskv_ref, dqn_ref, dkn_ref, dsk_ref, carry_ref, dqt_ref, dkvt_ref):
        step = pl.program_id(0)
        first = step == nt - 1

        @pl.when(step == 0)
        def _():
            carry_ref[...] = jnp.zeros_like(carry_ref)
            dqn_ref[...] = jnp.zeros_like(dqn_ref)
            dkn_ref[...] = jnp.zeros_like(dkn_ref)
            dsk_ref[...] = jnp.zeros_like(dsk_ref)

        k_raw, v_t, v_nat, q_raw, qn_b, kn_b, sink_rows, adds = _swa_tile_inputs(
            sq_ref, skv_ref, halo_ref, qn_ref, kn_ref, sk_ref, add_ref, first)
        kn_f = [_rms0(k, kn_b) for k in k_raw]
        kn_t = [k.astype(BF16) for k in kn_f]
        kn_nat = [jnp.transpose(k).astype(BF16) for k in kn_f]
        v_nat = [v.astype(BF16) for v in v_nat]
        qn_t = [_rms0(q, qn_b).astype(BF16) for q in q_raw]
        do_t = jnp.transpose(do_ref[...].astype(F32)).astype(BF16)

        dkvt_ref[...] = jnp.zeros_like(dkvt_ref)
        dsink = [jnp.zeros((1, SWA_COLS), F32) for _ in range(SWA_KV)]
        nt_dims = (((1,), (1,)), ((), ()))
        for b in range(nb):
            rows = slice(BLOCK * b, BLOCK * (b + 1))
            band = slice(BLOCK * b, BLOCK * (b + 2))
            for j in range(SWA_KV):
                heads = [SWA_GROUP * j + g for g in range(SWA_GROUP)]
                qs_t = _swa_queries(qn_t, j, b)
                dos_t = jnp.concatenate([do_t[SWA_DIM * h:SWA_DIM * (h + 1), rows] for h in heads],
                                        axis=1)
                e, inv, es = _swa_probs(kn_nat[j][band], qs_t, adds[j][0 if b == 0 else 1],
                                        sink_rows[j])
                p = e * inv
                dp = jnp.dot(v_nat[j][band], dos_t, preferred_element_type=F32)
                dsum = jnp.sum(p * dp, axis=0, keepdims=True)
                dsink[j] = dsink[j] - es * inv * dsum
                g_t = (p * (dp - dsum) * SWA_SCALE).astype(BF16)
                dv_t = lax.dot_general(dos_t, p.astype(BF16), nt_dims,
                                       preferred_element_type=F32)
                dk_t = lax.dot_general(qs_t, g_t, nt_dims, preferred_element_type=F32)
                dq_t = jnp.dot(kn_t[j][:, band], g_t, preferred_element_type=F32)
                dkvt_ref[SWA_DIM * j:SWA_DIM * (j + 1), band] += dk_t
                dkvt_ref[128 + SWA_DIM * j:128 + SWA_DIM * (j + 1), band] += dv_t
                for g, h in enumerate(heads):
                    dqt_ref[SWA_DIM * h:SWA_DIM * (h + 1), rows] = dq_t[:, BLOCK * g:BLOCK * (g + 1)]

        dqn = jnp.zeros((SWA_DIM, 1), F32)
        for h in range(SWA_HEADS):
            _, vjp = jax.vjp(_rms0, q_raw[h], qn_ref[...])
            dq, dg = vjp(dqt_ref[SWA_DIM * h:SWA_DIM * (h + 1), :])
            dqt_ref[SWA_DIM * h:SWA_DIM * (h + 1), :] = dq
            dqn = dqn + dg
        dqn_ref[...] += dqn
        dsq_ref[...] = jnp.transpose(dqt_ref[...]).astype(BF16)
        dkn = jnp.zeros((SWA_DIM, 1), F32)
        lane_grp = lax.broadcasted_iota(jnp.int32, (1, SWA_COLS), 1) // BLOCK
        for j in range(SWA_KV):
            _, vjp = jax.vjp(_rms0, k_raw[j], kn_ref[...])
            dk, dg = vjp(dkvt_ref[SWA_DIM * j:SWA_DIM * (j + 1), :])
            dkvt_ref[SWA_DIM * j:SWA_DIM * (j + 1), :] = dk
            dkn = dkn + dg
            for g in range(SWA_GROUP):
                h = SWA_GROUP * j + g
                dsk_ref[:, h:h + 1] += jnp.sum(jnp.where(lane_grp == g, dsink[j], 0.0), axis=1,
                                               keepdims=True)
        dkn_ref[...] += dkn
        dkv = jnp.transpose(dkvt_ref[...])
        dskv_ref[0:ts - BLOCK, :] = dkv[BLOCK:ts].astype(BF16)
        dskv_ref[ts - BLOCK:ts, :] = (dkv[ts:ts + BLOCK] + carry_ref[...]).astype(BF16)
        carry_ref[...] = dkv[0:BLOCK]

    const = lambda shape: pl.BlockSpec(shape, lambda st: (0,) * len(shape))
    return pl.pallas_call(
        body, name="swa_bwd", grid=(nt,),
        in_specs=[pl.BlockSpec((ts, 512), lambda st: (nt - 1 - st, OFF_SQ // 512)),
                  pl.BlockSpec((ts, 256), lambda st: (nt - 1 - st, OFF_SKV // 256)),
                  pl.BlockSpec((BLOCK, 256),
                               lambda st: (jnp.maximum((nt - 1 - st) * nb - 1, 0), OFF_SKV // 256)),
                  const((SWA_DIM, 1)), const((SWA_DIM, 1)), const((1, SWA_HEADS)),
                  const(tables.shape),
                  pl.BlockSpec((ts, 512), lambda st: (nt - 1 - st, 0))],
        out_specs=[pl.BlockSpec((ts, 512), lambda st: (nt - 1 - st, 0)),
                   pl.BlockSpec((ts, 256), lambda st: (nt - 1 - st, 0)),
                   const((SWA_DIM, 1)), const((SWA_DIM, 1)), const((1, SWA_HEADS))],
        out_shape=[jax.ShapeDtypeStruct((s, 512), BF16), jax.ShapeDtypeStruct((s, 256), BF16),
                   jax.ShapeDtypeStruct((SWA_DIM, 1), F32), jax.ShapeDtypeStruct((SWA_DIM, 1), F32),
                   jax.ShapeDtypeStruct((1, SWA_HEADS), F32)],
        scratch_shapes=[pltpu.VMEM((BLOCK, 256), F32), pltpu.VMEM((512, ts), F32),
                        pltpu.VMEM((256, ts + BLOCK), F32)],
        compiler_params=_params(("arbitrary",)),
    )(proj, proj, proj, qn, kn, sinks, tables, do)


HALO = 8


def _shift_down(u, halo, k):
    tm = u.shape[0]
    rid = lax.broadcasted_iota(jnp.int32, u.shape, 0)
    out = pltpu.roll(u, k, 0)
    for r in range(k):
        out = jnp.where(rid == r, halo[HALO - k + r:HALO - k + r + 1, :], out)
    return out


def _shift_up(u, halo, k):
    tm = u.shape[0]
    rid = lax.broadcasted_iota(jnp.int32, u.shape, 0)
    out = pltpu.roll(u, tm - k, 0)
    for r in range(k):
        out = jnp.where(rid == tm - k + r, halo[r:r + 1, :], out)
    return out


def _conv_fwd_vals(conv_ref, convp_ref, cw_ref, is_first):
    c_h, c_b, c_c = conv_ref[:, 0:512], conv_ref[:, 512:1024], conv_ref[:, 1024:1536]
    u = c_c * c_h
    up = jnp.where(is_first, 0.0, convp_ref[:, 1024:1536] * convp_ref[:, 0:512])
    u1 = _shift_down(u, up, 1)
    u2 = _shift_down(u, up, 2)
    yc = cw_ref[0:1, :] * u2 + cw_ref[1:2, :] * u1 + cw_ref[2:3, :] * u
    return c_h, c_b, c_c, u, u1, u2, yc


def _out_fwd(proj, o_mla, o_swa, x, w_out, cw, target=None):
    s = proj.shape[0]
    tm = min(512, s)
    nt = s // tm
    with_loss = target is not None

    def body(*refs):
        conv_ref, convp_ref, gates_ref, om_ref, os_ref, x_ref, w_ref, cw_ref = refs[:8]
        if with_loss:
            t_ref, y_ref, zt_ref, loss_ref, z_ref = refs[8:]
        else:
            y_ref, zt_ref, z_ref = refs[8:]
        i = pl.program_id(0)
        _, c_b, _, _, _, _, yc = _conv_fwd_vals(conv_ref, convp_ref, cw_ref, i == 0)
        mix = (om_ref[...], c_b * yc, os_ref[...])
        for n in range(3):
            g = gates_ref[:, GROUP * n:GROUP * (n + 1)]
            z = mix[n] * (g * _sigmoid(g))
            z_ref[:, GROUP * n:GROUP * (n + 1)] = z.astype(BF16)
            zt_ref[GROUP * n:GROUP * (n + 1), :] = jnp.transpose(z).astype(BF16)
        y = x_ref[...] + jnp.dot(z_ref[...], w_ref[...], preferred_element_type=F32)
        if not with_loss:
            y_ref[...] = y
            return
        err = y - t_ref[...]
        y_ref[...] = err * (1.0 / D_MODEL)

        @pl.when(i == 0)
        def _():
            loss_ref[...] = jnp.zeros_like(loss_ref)

        sq = jnp.sum((err * err).reshape(tm // 8, 8, D_MODEL), axis=0)
        part = sq[:, 0:LANES]
        for c in range(1, D_MODEL // LANES):
            part = part + sq[:, LANES * c:LANES * (c + 1)]
        loss_ref[...] += part

        @pl.when(i == nt - 1)
        def _():
            loss_ref[...] = jnp.full(loss_ref.shape, (0.5 / D_MODEL) * jnp.sum(loss_ref[...]), F32)

    row = lambda width: pl.BlockSpec((tm, width), lambda i: (i, 0))
    in_specs = [pl.BlockSpec((tm, 1536), lambda i: (i, 0)),
                pl.BlockSpec((HALO, 1536), lambda i: (jnp.maximum(i * (tm // HALO) - 1, 0), 0)),
                pl.BlockSpec((tm, 1536), lambda i: (i, 1)),
                row(512), row(512), row(D_MODEL),
                pl.BlockSpec((D_MIX, D_MODEL), lambda i: (0, 0)),
                pl.BlockSpec((HALO, 512), lambda i: (0, 0))]
    out_specs = [row(D_MODEL), pl.BlockSpec((D_MIX, tm), lambda i: (0, i))]
    out_shape = [jax.ShapeDtypeStruct((s, D_MODEL), F32), jax.ShapeDtypeStruct((D_MIX, s), BF16)]
    operands = [proj, proj, proj, o_mla, o_swa, x, w_out, cw]
    if with_loss:
        in_specs.append(row(D_MODEL))
        out_specs.append(pl.BlockSpec((8, LANES), lambda i: (0, 0)))
        out_shape.append(jax.ShapeDtypeStruct((8, LANES), F32))
        operands.append(target)
    return pl.pallas_call(
        body, name="out_fwd_loss" if with_loss else "out_fwd", grid=(nt,),
        in_specs=in_specs, out_specs=out_specs, out_shape=out_shape,
        scratch_shapes=[pltpu.VMEM((tm, D_MIX), BF16)],
        compiler_params=_params(("arbitrary",) if with_loss else ("parallel",)),
    )(*operands)


def _out_bwd(dy, proj, o_mla, o_swa, w_out, cw):
    s = proj.shape[0]
    tm = min(512, s)
    nt = s // tm
    hb = tm // HALO

    def body(dy_ref, dyn_ref, conv_ref, convp_ref, convn_ref, gates_ref, gatesn_ref, om_ref, os_ref,
             w_ref, cw_ref,
             dconv_ref, dgates_ref, dom_ref, domt_ref, delta_ref, dos_ref, dcw_ref):
        i = pl.program_id(0)
        dz = _mm_nt(dy_ref[...], w_ref[...])

        def gate(n):
            g = gates_ref[:, GROUP * n:GROUP * (n + 1)]
            sg = _sigmoid(g)
            return g * sg, sg * (1.0 + g * (1.0 - sg))

        for n, o_ref, do_ref in ((0, om_ref, dom_ref), (2, os_ref, dos_ref)):
            silu, dsilu = gate(n)
            dzn = dz[:, GROUP * n:GROUP * (n + 1)]
            o = o_ref[...]
            do = dzn * silu
            do_ref[...] = do.astype(do_ref.dtype)
            dgates_ref[:, GROUP * n:GROUP * (n + 1)] = (dzn * o * dsilu).astype(BF16)
            if n == 0:
                domt_ref[...] = jnp.transpose(do).astype(BF16)
                t = do * o
                for h in range(MLA_HEADS):
                    delta_ref[:, h:h + 1] = jnp.sum(t[:, MLA_V * h:MLA_V * (h + 1)], axis=-1,
                                                    keepdims=True)

        c_h, c_b, c_c, u, u1, u2, yc = _conv_fwd_vals(conv_ref, convp_ref, cw_ref, i == 0)
        silu, dsilu = gate(1)
        dzc = dz[:, GROUP:2 * GROUP]
        dgates_ref[:, GROUP:2 * GROUP] = (dzc * (c_b * yc) * dsilu).astype(BF16)
        dycr = dzc * silu
        dyc = dycr * c_b
        gn = gatesn_ref[:, GROUP:2 * GROUP]
        dzc_n = _mm_nt(dyn_ref[...], w_ref[GROUP:2 * GROUP, :])
        dyc_n = jnp.where(i == nt - 1, 0.0, dzc_n * (gn * _sigmoid(gn)) * convn_ref[:, 512:1024])
        d1 = _shift_up(dyc, dyc_n, 1)
        d2 = _shift_up(dyc, dyc_n, 2)
        du = cw_ref[2:3, :] * dyc + cw_ref[1:2, :] * d1 + cw_ref[0:1, :] * d2
        dconv_ref[:, 0:512] = (du * c_c).astype(BF16)
        dconv_ref[:, 512:1024] = (dycr * yc).astype(BF16)
        dconv_ref[:, 1024:1536] = (du * c_h).astype(BF16)

        @pl.when(i == 0)
        def _():
            dcw_ref[...] = jnp.zeros_like(dcw_ref)

        for k, uk in enumerate((u2, u1, u)):
            dcw_ref[k:k + 1, :] += jnp.sum(dyc * uk, axis=0, keepdims=True)

    row = lambda width: pl.BlockSpec((tm, width), lambda i: (i, 0))
    prev = lambda i: jnp.maximum(i * hb - 1, 0)
    nxt = lambda i: jnp.minimum((i + 1) * hb, s // HALO - 1)
    return pl.pallas_call(
        body, name="out_bwd", grid=(nt,),
        in_specs=[row(D_MODEL),
                  pl.BlockSpec((HALO, D_MODEL), lambda i: (nxt(i), 0)),
                  pl.BlockSpec((tm, 1536), lambda i: (i, 0)),
                  pl.BlockSpec((HALO, 1536), lambda i: (prev(i), 0)),
                  pl.BlockSpec((HALO, 1536), lambda i: (nxt(i), 0)),
                  pl.BlockSpec((tm, 1536), lambda i: (i, 1)),
                  pl.BlockSpec((HALO, 1536), lambda i: (nxt(i), 1)),
                  row(512), row(512),
                  pl.BlockSpec((D_MIX, D_MODEL), lambda i: (0, 0)),
                  pl.BlockSpec((HALO, 512), lambda i: (0, 0))],
        out_specs=[row(1536), row(1536), row(512), pl.BlockSpec((512, tm), lambda i: (0, i)),
                   row(MLA_HEADS), row(512), pl.BlockSpec((HALO, 512), lambda i: (0, 0))],
        out_shape=[jax.ShapeDtypeStruct((s, 1536), BF16), jax.ShapeDtypeStruct((s, 1536), BF16),
                   jax.ShapeDtypeStruct((s, 512), BF16), jax.ShapeDtypeStruct((512, s), BF16),
                   jax.ShapeDtypeStruct((s, MLA_HEADS), F32),
                   jax.ShapeDtypeStruct((s, 512), BF16), jax.ShapeDtypeStruct((HALO, 512), F32)],
        compiler_params=_params(("arbitrary",)),
    )(dy, dy, proj, proj, proj, proj, proj, o_mla, o_swa, w_out, cw)


def _adam_update(g, w, m, v):
    c1 = 1.0 - ADAM_B1
    c2 = 1.0 - ADAM_B2
    bc1 = 1.0 - ADAM_B1 ** ADAM_STEP
    bc2 = 1.0 - ADAM_B2 ** ADAM_STEP
    m_new = ADAM_B1 * m + c1 * g
    v_new = ADAM_B2 * v + c2 * (g * g)
    delta = -ADAM_LR * ((m_new / bc1) / (jnp.sqrt(v_new / bc2) + ADAM_EPS) + ADAM_WD * w)
    return delta, m_new, v_new


def _adamw(g, w, m, v):
    rows = g.shape[0]
    tr = min(256, rows)

    def body(g_ref, w_ref, m_ref, v_ref, d_ref, mo_ref, vo_ref):
        d_ref[...], mo_ref[...], vo_ref[...] = _adam_update(g_ref[...], w_ref[...], m_ref[...],
                                                            v_ref[...])

    spec = pl.BlockSpec((tr, g.shape[1]), lambda i: (i, 0))
    return pl.pallas_call(
        body, name="adamw", grid=(rows // tr,),
        in_specs=[spec] * 4, out_specs=[spec] * 3,
        out_shape=[jax.ShapeDtypeStruct(g.shape, F32)] * 3,
        compiler_params=_params(("parallel",)),
    )(g, w, m, v)


def _adamw_small(gs, ws, ms, vs):
    n = len(gs)

    def body(*refs):
        ins, outs = refs[:4 * n], refs[4 * n:]
        for a in range(n):
            res = _adam_update(*(ins[kind * n + a][...] for kind in range(4)))
            for kind in range(3):
                outs[kind * n + a][...] = res[kind]

    vmem = pl.BlockSpec(memory_space=pltpu.VMEM)
    out = pl.pallas_call(
        body, name="adamw_small",
        in_specs=[vmem] * (4 * n), out_specs=[vmem] * (3 * n),
        out_shape=[jax.ShapeDtypeStruct(g.shape, F32) for _ in range(3) for g in gs],
    )(*gs, *ws, *ms, *vs)
    return out[:n], out[n:2 * n], out[2 * n:]


HBM_SPEC = pl.BlockSpec(memory_space=pltpu.HBM)


def _place():
    x, y, c = lax.axis_index("x"), lax.axis_index("y"), lax.axis_index("c")
    chips = [(1 - x, y), (x, 1 - y), (1 - x, 1 - y)]
    return x, y, c, chips


def _all_gather(shards):
    na = len(shards)
    halves = [sh.shape[0] // 2 for sh in shards]

    def body(*refs):
        w_refs, a_refs = refs[:na], refs[na:2 * na]
        send_sems, recv_sems = refs[2 * na:]
        x, y, c, chips = _place()
        k = 2 * x + y
        sib = (x, y, 1 - c)

        def slab(a, kk, hc):
            return a_refs[a].at[kk, pl.ds(hc * halves[a], halves[a]), :]

        def copy(a, n, src, dst, to):
            return pltpu.make_async_remote_copy(
                src_ref=src, dst_ref=dst, send_sem=send_sems.at[6 * a + n],
                recv_sem=recv_sems.at[6 * a + n], device_id=to, device_id_type=MESH)

        first = [copy(a, n, w_refs[a].at[pl.ds(c * halves[a], halves[a]), :], slab(a, k, c),
                      (cx, cy, c))
                 for n, (cx, cy) in enumerate(chips) for a in range(na)]
        for cp in first:
            cp.start()
        passed = []
        for n, (cx, cy) in enumerate(chips):
            kk = 2 * cx + cy
            for a in range(na):
                copy(a, n, slab(a, kk, c), slab(a, kk, c), (cx, cy, c)).wait_recv()
                fwd = copy(a, 3 + n, slab(a, kk, c), slab(a, kk, c), sib)
                fwd.start()
                passed.append(fwd)
        for n, (cx, cy) in enumerate(chips):
            kk = 2 * cx + cy
            for a in range(na):
                copy(a, 3 + n, slab(a, kk, 1 - c), slab(a, kk, 1 - c), sib).wait_recv()
        for cp in first + passed:
            cp.wait_send()

    return pl.pallas_call(
        body, name="weights_all_gather",
        in_specs=[HBM_SPEC] * na, out_specs=[HBM_SPEC] * na,
        out_shape=[jax.ShapeDtypeStruct((N_CHIPS,) + sh.shape, sh.dtype) for sh in shards],
        scratch_shapes=[pltpu.SemaphoreType.DMA((6 * na,)), pltpu.SemaphoreType.DMA((6 * na,))],
    )(*shards)


def _fill_own_slab(buf, src, k_idx):
    n, rows, cols = buf.shape
    tr = _row_tile(rows)
    slabs = src.ndim == 3

    def body(k_ref, src_ref, buf_ref, out_ref):
        out_ref[0] = src_ref[0] if slabs else src_ref[...]

    if slabs:
        src_spec = pl.BlockSpec((1, tr, cols), lambda t, k_ref: (k_ref[0], t, 0))
    else:
        src_spec = pl.BlockSpec((tr, cols), lambda t, k_ref: (t, 0))
    return pl.pallas_call(
        body, name="fill_own_slab",
        grid_spec=pltpu.PrefetchScalarGridSpec(
            num_scalar_prefetch=1, grid=(rows // tr,),
            in_specs=[src_spec, pl.BlockSpec(memory_space=pl.ANY)],
            out_specs=pl.BlockSpec((1, tr, cols), lambda t, k_ref: (k_ref[0], t, 0))),
        out_shape=jax.ShapeDtypeStruct(buf.shape, buf.dtype),
        input_output_aliases={2: 0},
        compiler_params=_params(("parallel",)),
    )(k_idx, src, buf)


SEM_SPEC = pl.BlockSpec(memory_space=pltpu.SEMAPHORE)
LATE_COPIES = 6


def _late_copy(a, j, peer_core, src, dst, send_sems, recv_sems, to, sender_core):
    return pltpu.make_async_remote_copy(
        src_ref=src, dst_ref=dst, send_sem=send_sems.at[LATE_COPIES * a + 2 * j + peer_core],
        recv_sem=recv_sems.at[LATE_COPIES * a + 2 * j + sender_core], device_id=to,
        device_id_type=MESH)


def _gather_start(shards):
    na = len(shards)

    def body(*refs):
        w_refs, land_refs = refs[:na], refs[na:2 * na]
        send_sems, recv_sems = refs[2 * na], refs[2 * na + 1]
        token = refs[-1]
        x, y, c, chips = _place()
        k = 2 * x + y
        for a in range(na):
            half = w_refs[a].shape[0] // 2
            src = w_refs[a].at[pl.ds(c * half, half), :]
            dst = land_refs[a].at[k, pl.ds(c * half, half), :]
            for j, (cx, cy) in enumerate(chips):
                for tc in range(2):
                    _late_copy(a, j, tc, src, dst, send_sems, recv_sems, (cx, cy, tc), c).start()
        token[...] = jnp.zeros_like(token)

    lands = [pltpu.with_memory_space_constraint(lax.empty((N_CHIPS,) + sh.shape, sh.dtype), pltpu.HBM)
             for sh in shards]
    srcs = [pltpu.with_memory_space_constraint(sh, pltpu.HBM) for sh in shards]
    sems = pltpu.SemaphoreType.DMA((LATE_COPIES * na,))
    aliases = {a: 2 + a for a in range(2 * na)}
    return pl.pallas_call(
        body, name="late_weights_gather_start",
        in_specs=[HBM_SPEC] * (2 * na),
        out_specs=[SEM_SPEC, SEM_SPEC] + [HBM_SPEC] * (2 * na) + [pl.BlockSpec(memory_space=pltpu.VMEM)],
        out_shape=[sems, sems] + [pltpu.HBM(v.shape, v.dtype) for v in srcs + lands]
        + [jax.ShapeDtypeStruct((8, LANES), F32)],
        input_output_aliases=aliases,
        compiler_params=pltpu.CompilerParams(
            has_side_effects=pltpu.SideEffectType.DATAFLOW_SIDE_EFFECTING),
    )(*srcs, *lands)


def _gather_wait(started, na, after):
    send_sems, recv_sems = started[0], started[1]
    bufs = started[2:2 + 2 * na]

    def body(*refs):
        w_refs, land_refs = refs[:na], refs[na:2 * na]
        send_sems, recv_sems = refs[2 * na], refs[2 * na + 1]
        x, y, c, chips = _place()
        k = 2 * x + y
        for a in range(na):
            half = w_refs[a].shape[0] // 2
            src = w_refs[a].at[pl.ds(c * half, half), :]
            for j, (cx, cy) in enumerate(chips):
                kk = 2 * cx + cy
                for pc in range(2):
                    _late_copy(a, j, pc, src, land_refs[a].at[k, pl.ds(c * half, half), :],
                               send_sems, recv_sems, (cx, cy, pc), c).wait_send()
                    pltpu.make_async_remote_copy(
                        src_ref=src, dst_ref=land_refs[a].at[kk, pl.ds(pc * half, half), :],
                        send_sem=send_sems.at[LATE_COPIES * a + 2 * j + pc],
                        recv_sem=recv_sems.at[LATE_COPIES * a + 2 * j + pc],
                        device_id=(cx, cy, pc), device_id_type=MESH).wait_recv()

    out = pl.pallas_call(
        body, name="late_weights_gather_wait",
        in_specs=[HBM_SPEC] * (2 * na) + [SEM_SPEC, SEM_SPEC, pl.BlockSpec(memory_space=pl.ANY)],
        out_specs=[HBM_SPEC] * (2 * na),
        out_shape=[pltpu.HBM(v.shape, v.dtype) for v in bufs],
        input_output_aliases={a: a for a in range(2 * na)},
        compiler_params=pltpu.CompilerParams(
            has_side_effects=pltpu.SideEffectType.DATAFLOW_SIDE_EFFECTING),
    )(*bufs, send_sems, recv_sems, after)
    return out[na:]


def _split_start(name, srcs, land_shapes, n_sems, plan):
    na = len(srcs)

    def body(*refs):
        sends, _ = plan(refs[:na], refs[na:2 * na], refs[2 * na], refs[2 * na + 1])
        for cp in sends:
            cp.start()
        refs[-1][...] = jnp.zeros_like(refs[-1])

    lands = [pltpu.with_memory_space_constraint(lax.empty(shape, dtype), pltpu.HBM)
             for shape, dtype in land_shapes]
    srcs = [pltpu.with_memory_space_constraint(v, pltpu.HBM) for v in srcs]
    sems = pltpu.SemaphoreType.DMA((n_sems,))
    return pl.pallas_call(
        body, name=name,
        in_specs=[HBM_SPEC] * (2 * na),
        out_specs=[SEM_SPEC, SEM_SPEC] + [HBM_SPEC] * (2 * na) + [pl.BlockSpec(memory_space=pltpu.VMEM)],
        out_shape=[sems, sems] + [pltpu.HBM(v.shape, v.dtype) for v in srcs + lands]
        + [jax.ShapeDtypeStruct((8, LANES), F32)],
        input_output_aliases={a: 2 + a for a in range(2 * na)},
        compiler_params=pltpu.CompilerParams(
            has_side_effects=pltpu.SideEffectType.DATAFLOW_SIDE_EFFECTING),
    )(*srcs, *lands)


def _split_wait(name, started, na, plan, after):
    bufs = started[2:2 + 2 * na]

    def body(*refs):
        sends, recvs = plan(refs[:na], refs[na:2 * na], refs[2 * na], refs[2 * na + 1])
        for cp in sends:
            cp.wait_send()
        for cp in recvs:
            cp.wait_recv()

    out = pl.pallas_call(
        body, name=name,
        in_specs=[HBM_SPEC] * (2 * na) + [SEM_SPEC, SEM_SPEC, pl.BlockSpec(memory_space=pl.ANY)],
        out_specs=[HBM_SPEC] * (2 * na),
        out_shape=[pltpu.HBM(v.shape, v.dtype) for v in bufs],
        input_output_aliases={a: a for a in range(2 * na)},
        compiler_params=pltpu.CompilerParams(
            has_side_effects=pltpu.SideEffectType.DATAFLOW_SIDE_EFFECTING),
    )(*bufs, started[0], started[1], after)
    return out[:na], out[na:]


def _plan_to_sibling(g_refs, r_refs, send_sems, recv_sems):
    x, y, c, _ = _place()
    cps = []
    for a, (g, r) in enumerate(zip(g_refs, r_refs)):
        half = g.shape[1] // 2
        cps.append(pltpu.make_async_remote_copy(
            src_ref=g.at[:, pl.ds((1 - c) * half, half), :], dst_ref=r, send_sem=send_sems.at[a],
            recv_sem=recv_sems.at[a], device_id=(x, y, 1 - c), device_id_type=MESH))
    return cps, cps


def _plan_scatter(p_refs, q_refs, send_sems, recv_sems):
    x, y, c, chips = _place()
    k = 2 * x + y
    sends, recvs = [], []
    for a, (p, q) in enumerate(zip(p_refs, q_refs)):
        for i, (cx, cy) in enumerate(chips):
            kk = 2 * cx + cy
            for dst, out in ((q.at[k], sends), (q.at[kk], recvs)):
                out.append(pltpu.make_async_remote_copy(
                    src_ref=p.at[kk], dst_ref=dst, send_sem=send_sems.at[3 * a + i],
                    recv_sem=recv_sems.at[3 * a + i], device_id=(cx, cy, c), device_id_type=MESH))
    return sends, recvs


def _swap_halves_to_sibling(gs):
    na = len(gs)

    def body(*refs):
        g_refs, r_refs = refs[:na], refs[na:2 * na]
        send_sems, recv_sems = refs[2 * na:]
        x, y, c, _ = _place()
        cps = []
        for a in range(na):
            half = g_refs[a].shape[1] // 2
            cps.append(pltpu.make_async_remote_copy(
                src_ref=g_refs[a].at[:, pl.ds((1 - c) * half, half), :], dst_ref=r_refs[a],
                send_sem=send_sems.at[a], recv_sem=recv_sems.at[a], device_id=(x, y, 1 - c),
                device_id_type=MESH))
        for cp in cps:
            cp.start()
        for cp in cps:
            cp.wait()

    return pl.pallas_call(
        body, name="grads_to_sibling",
        in_specs=[HBM_SPEC] * na, out_specs=[HBM_SPEC] * na,
        out_shape=[jax.ShapeDtypeStruct((g.shape[0], g.shape[1] // 2, g.shape[2]), g.dtype)
                   for g in gs],
        scratch_shapes=[pltpu.SemaphoreType.DMA((na,)), pltpu.SemaphoreType.DMA((na,))],
    )(*gs)


def _row_tile(rows):
    return next(t for t in (256, 128, 64) if rows % t == 0)


def _add_sibling(g, r, c_idx, out_dtype):
    n, rows, cols = g.shape
    half = rows // 2
    tr = _row_tile(half)
    nb = half // tr

    def body(c_ref, g_ref, r_ref, p_ref):
        p_ref[...] = (g_ref[...] + r_ref[...]).astype(out_dtype)

    return pl.pallas_call(
        body, name="grads_add_sibling",
        grid_spec=pltpu.PrefetchScalarGridSpec(
            num_scalar_prefetch=1, grid=(n, nb),
            in_specs=[pl.BlockSpec((1, tr, cols), lambda j, t, c_ref: (j, c_ref[0] * nb + t, 0)),
                      pl.BlockSpec((1, tr, cols), lambda j, t, c_ref: (j, t, 0))],
            out_specs=pl.BlockSpec((1, tr, cols), lambda j, t, c_ref: (j, t, 0))),
        out_shape=jax.ShapeDtypeStruct((n, half, cols), out_dtype),
        compiler_params=_params(("parallel", "parallel")),
    )(c_idx, g, r)


def _scatter_to_chips(ps):
    na = len(ps)

    def body(*refs):
        p_refs, q_refs = refs[:na], refs[na:2 * na]
        send_sems, recv_sems = refs[2 * na:]
        x, y, c, chips = _place()
        k = 2 * x + y
        sends = []
        for i, (cx, cy) in enumerate(chips):
            for a in range(na):
                cp = pltpu.make_async_remote_copy(
                    src_ref=p_refs[a].at[2 * cx + cy], dst_ref=q_refs[a].at[k],
                    send_sem=send_sems.at[3 * a + i], recv_sem=recv_sems.at[3 * a + i],
                    device_id=(cx, cy, c), device_id_type=MESH)
                cp.start()
                sends.append(cp)
        for i, (cx, cy) in enumerate(chips):
            kk = 2 * cx + cy
            for a in range(na):
                pltpu.make_async_remote_copy(
                    src_ref=p_refs[a].at[kk], dst_ref=q_refs[a].at[kk],
                    send_sem=send_sems.at[3 * a + i], recv_sem=recv_sems.at[3 * a + i],
                    device_id=(cx, cy, c), device_id_type=MESH).wait_recv()
        for cp in sends:
            cp.wait_send()

    return pl.pallas_call(
        body, name="grads_scatter_to_chips",
        in_specs=[HBM_SPEC] * na, out_specs=[HBM_SPEC] * na,
        out_shape=[jax.ShapeDtypeStruct(p.shape, p.dtype) for p in ps],
        scratch_shapes=[pltpu.SemaphoreType.DMA((3 * na,)), pltpu.SemaphoreType.DMA((3 * na,))],
    )(*ps)


def _sum_chips(q, c_idx):
    n, half, cols = q.shape
    tr = _row_tile(half)
    nb = half // tr

    def body(c_ref, q_ref, o_ref):
        parts = [q_ref[kk].astype(F32) for kk in range(n)]
        o_ref[...] = ((parts[0] + parts[1]) + parts[2]) + parts[3]

    return pl.pallas_call(
        body, name="grads_sum_chips",
        grid_spec=pltpu.PrefetchScalarGridSpec(
            num_scalar_prefetch=1, grid=(nb,),
            in_specs=[pl.BlockSpec((n, tr, cols), lambda t, c_ref: (0, t, 0))],
            out_specs=pl.BlockSpec((tr, cols), lambda t, c_ref: (c_ref[0] * nb + t, 0))),
        out_shape=jax.ShapeDtypeStruct((2 * half, cols), F32),
        compiler_params=_params(("parallel",)),
    )(c_idx, q)


def _join_halves(fulls):
    na = len(fulls)

    def body(*refs):
        o_refs = refs[na:2 * na]
        send_sems, recv_sems = refs[2 * na:]
        x, y, c, _ = _place()
        sends = []
        for a in range(na):
            half = o_refs[a].shape[0] // 2
            rows = o_refs[a].at[pl.ds(c * half, half), :]
            sends.append(pltpu.make_async_remote_copy(
                src_ref=rows, dst_ref=rows, send_sem=send_sems.at[a], recv_sem=recv_sems.at[a],
                device_id=(x, y, 1 - c), device_id_type=MESH))
        for cp in sends:
            cp.start()
        for a in range(na):
            half = o_refs[a].shape[0] // 2
            other = o_refs[a].at[pl.ds((1 - c) * half, half), :]
            pltpu.make_async_remote_copy(
                src_ref=other, dst_ref=other, send_sem=send_sems.at[a], recv_sem=recv_sems.at[a],
                device_id=(x, y, 1 - c), device_id_type=MESH).wait_recv()
        for cp in sends:
            cp.wait_send()

    return pl.pallas_call(
        body, name="grads_join_halves",
        in_specs=[HBM_SPEC] * na, out_specs=[HBM_SPEC] * na,
        out_shape=[jax.ShapeDtypeStruct(f.shape, f.dtype) for f in fulls],
        input_output_aliases={a: a for a in range(na)},
        scratch_shapes=[pltpu.SemaphoreType.DMA((na,)), pltpu.SemaphoreType.DMA((na,))],
    )(*fulls)


def _part_rows(shape):
    size = 1
    for d in shape:
        size *= d
    rows = -(-size // PACK_COLS)
    return size, -(-rows // PACK_ROW_ALIGN) * PACK_ROW_ALIGN


def _pack_rows(arrays, dtype, total_rows):
    parts, used = [], 0
    for a in arrays:
        size, rows = _part_rows(a.shape)
        flat = a.reshape(-1).astype(dtype)
        parts.append(jnp.pad(flat, (0, rows * PACK_COLS - size)).reshape(rows, PACK_COLS))
        used += rows
    parts.append(jnp.zeros((total_rows - used, PACK_COLS), dtype))
    return jnp.concatenate(parts, axis=0)


def _unpack_rows(buf, shapes):
    lead = buf.shape[:-2]
    out, off = [], 0
    for sh in shapes:
        size, rows = _part_rows(sh)
        part = buf[..., off:off + rows, :].reshape(lead + (-1,))[..., :size]
        out.append(part.reshape(lead + tuple(sh)))
        off += rows
    return out


NEW_ORDER = ((928, 1440), (1440, 1952), (1952, 2464), (416, 928), (2464, 2976), (3744, 4256),
             (2976, 3488), (0, 256), (256, 384), (4256, 4320), (384, 416), (4256, 4288),
             (3488, 3616), (3616, 3744))
OLD_ORDER = ((3584, 3840), (3840, 3968), (4032, 4064), (1536, 2048), (0, 512), (512, 1024),
             (1024, 1536), (2048, 2560), (3072, 3584), (4096, 4224), (4224, 4352), (2560, 3072))


def _cols(sources, ranges):
    parts = []
    for a, b in ranges:
        off = 0
        for src in sources:
            width = src.shape[-1]
            lo, hi = max(a, off), min(b, off + width)
            if lo < hi:
                parts.append(src[..., lo - off:hi - off])
            off += width
    return jnp.concatenate(parts, axis=-1)


def _sub_ranges(ranges, a, b):
    out, off = [], 0
    for lo, hi in ranges:
        width = hi - lo
        s0, s1 = max(a, off), min(b, off + width)
        if s0 < s1:
            out.append((lo + s0 - off, lo + s1 - off))
        off += width
    return out


def _rope_tables(s):
    half = MLA_ROPE // 2
    inv_freq = jnp.power(jnp.float32(ROPE_THETA), -jnp.arange(half, dtype=F32) / half)
    ang = inv_freq[:, None] * jnp.arange(s, dtype=F32)[None, :]
    cos, sin = jnp.cos(ang), jnp.sin(ang)
    z = lambda n: jnp.zeros((n, s), F32)
    c = jnp.concatenate([jnp.ones((MLA_NOPE, s), F32), cos, cos, z(32)], axis=0)
    s1 = jnp.concatenate([z(MLA_NOPE), -sin, z(16), z(32)], axis=0)
    s2 = jnp.concatenate([z(MLA_NOPE), z(16), sin, z(32)], axis=0)
    return c, s1, s2


def _pad_lanes(a, n):
    return jnp.pad(a, ((0, 0), (0, n - a.shape[1])))


SHARDED = ("w_in", "w_out", "mla_w_qb", "mla_w_kvb", "conv_w")
REPLICATED = ("norm_g", "mla_q_a_norm", "mla_kv_a_norm", "mla_q_norm", "mla_k_norm",
              "swa_q_norm", "swa_k_norm", "swa_sinks")
WEIGHT_ORDER = ("norm_g", "w_in", "mla_q_a_norm", "mla_w_qb", "mla_kv_a_norm", "mla_w_kvb",
                "mla_q_norm", "mla_k_norm", "conv_w", "swa_q_norm", "swa_k_norm", "swa_sinks", "w_out")
SHARD_AXIS = {"w_in": 2, "w_out": 1, "mla_w_qb": 2, "mla_w_kvb": 2, "conv_w": 2}


def kernel(x, norm_g, w_in, mla_q_a_norm, mla_w_qb, mla_kv_a_norm, mla_w_kvb, mla_q_norm, mla_k_norm, conv_w, swa_q_norm, swa_k_norm, swa_sinks, w_out, loss_target, m_norm_g, m_w_in, m_mla_q_a_norm, m_mla_w_qb, m_mla_kv_a_norm, m_mla_w_kvb, m_mla_q_norm, m_mla_k_norm, m_conv_w, m_swa_q_norm, m_swa_k_norm, m_swa_sinks, m_w_out, v_norm_g, v_w_in, v_mla_q_a_norm, v_mla_w_qb, v_mla_kv_a_norm, v_mla_w_kvb, v_mla_q_norm, v_mla_k_norm, v_conv_w, v_swa_q_norm, v_swa_k_norm, v_swa_sinks, v_w_out):
    weights = dict(norm_g=norm_g, w_in=w_in, mla_q_a_norm=mla_q_a_norm, mla_w_qb=mla_w_qb,
                   mla_kv_a_norm=mla_kv_a_norm, mla_w_kvb=mla_w_kvb, mla_q_norm=mla_q_norm,
                   mla_k_norm=mla_k_norm, conv_w=conv_w, swa_q_norm=swa_q_norm,
                   swa_k_norm=swa_k_norm, swa_sinks=swa_sinks, w_out=w_out)
    mom_m = dict(norm_g=m_norm_g, w_in=m_w_in, mla_q_a_norm=m_mla_q_a_norm, mla_w_qb=m_mla_w_qb,
                 mla_kv_a_norm=m_mla_kv_a_norm, mla_w_kvb=m_mla_w_kvb, mla_q_norm=m_mla_q_norm,
                 mla_k_norm=m_mla_k_norm, conv_w=m_conv_w, swa_q_norm=m_swa_q_norm,
                 swa_k_norm=m_swa_k_norm, swa_sinks=m_swa_sinks, w_out=m_w_out)
    mom_v = dict(norm_g=v_norm_g, w_in=v_w_in, mla_q_a_norm=v_mla_q_a_norm, mla_w_qb=v_mla_w_qb,
                 mla_kv_a_norm=v_mla_kv_a_norm, mla_w_kvb=v_mla_w_kvb, mla_q_norm=v_mla_q_norm,
                 mla_k_norm=v_mla_k_norm, conv_w=v_conv_w, swa_q_norm=v_swa_q_norm,
                 swa_k_norm=v_swa_k_norm, swa_sinks=v_swa_sinks, w_out=v_w_out)
    xs = x[0]
    target = loss_target[0]
    s = xs.shape[0]
    c_idx = lax.axis_index("c").astype(jnp.int32).reshape(1)
    k_idx = (2 * lax.axis_index("x") + lax.axis_index("y")).astype(jnp.int32).reshape(1)

    conv_bits = lax.bitcast_convert_type(conv_w, BF16)
    small_list = [mla_w_qb, mla_w_kvb, conv_bits]
    shard_cols = w_in.shape[2]
    w_in_b = w_in.astype(BF16)
    late = [w_in_b[1], w_out.astype(BF16).reshape(-1, D_MODEL)]
    started = _gather_start(late)
    own = [w_in_b[0], _pack_rows(small_list, BF16, SMALL_ROWS)]
    gathered_in0, gathered_small = [_fill_own_slab(buf, src, k_idx)
                                    for buf, src in zip(_all_gather(own), own)]
    parts = _unpack_rows(gathered_small, [a.shape for a in small_list])
    join = lambda p, axis: jnp.concatenate([p[k] for k in range(N_CHIPS)], axis=axis)
    w_in_zeros = jnp.zeros((D_MODEL, 64), BF16)
    permuted = lambda slabs: _cols([slabs[k] for k in range(N_CHIPS)] + [w_in_zeros], NEW_ORDER)
    w_qb_full = join(parts[0], 2)
    w_kvb_full = join(parts[1], 2)
    conv_full = lax.bitcast_convert_type(join(parts[2], 2), F32)

    rope = _rope_tables(s)
    swa_tables = _swa_tables()
    layers = []
    for l in range(DEPTH):
        wq = jnp.pad(w_qb_full[l].reshape(MLA_Q_LORA, MLA_HEADS, MLA_QK),
                     ((0, 0), (0, 0), (0, LANES - MLA_QK))).reshape(MLA_Q_LORA, MLA_HEADS * LANES)
        kv = w_kvb_full[l].reshape(MLA_KV_LORA, MLA_HEADS, MLA_NOPE + MLA_V)
        wk = jnp.pad(kv[:, :, :MLA_NOPE], ((0, 0), (0, 0), (0, LANES - MLA_NOPE)))
        wkv = jnp.concatenate([wk.reshape(MLA_KV_LORA, MLA_HEADS * LANES),
                               kv[:, :, MLA_NOPE:].reshape(MLA_KV_LORA, MLA_HEADS * MLA_V)], axis=1)
        layers.append(dict(
            wq=wq, wkv=wkv,
            cw=jnp.pad(conv_full[l], ((0, HALO - 3), (0, 0))),
            g=norm_g[l][None],
            mla_norms=(mla_q_a_norm[l][:, None], mla_kv_a_norm[l][:, None],
                       _pad_lanes(mla_q_norm[l][None], LANES).T, _pad_lanes(mla_k_norm[l][None], LANES).T),
            mla_weights=(wq, wkv, wq.T, wkv.T),
            sqn=swa_q_norm[l][:, None], skn=swa_k_norm[l][:, None], sinks=swa_sinks[l][None]))

    saved = []
    h_in = xs
    layers[0]["w_in"] = permuted(gathered_in0)
    layers[0]["g"] = layers[0]["g"] + started[-1][0:1, 0:1]
    for l in range(DEPTH):
        p = layers[l]
        proj, hb = _in_proj_fwd(h_in, p["g"], p["w_in"])
        q, k, v, qt, kt, vt = _mla_prep_fwd(proj, p["mla_norms"], p["mla_weights"], rope)
        o_mla, lse = _mla_attn_fwd(q, k, vt)
        if l == 0:
            late_in1, late_out = [_fill_own_slab(buf, src, k_idx) for buf, src in
                                  zip(_gather_wait(started, len(late), o_mla), late)]
            layers[1]["w_in"] = permuted(late_in1)
            w_out_full = join(late_out.reshape(N_CHIPS, DEPTH, -1, D_MODEL), 1)
            for n in range(DEPTH):
                layers[n]["w_out"] = w_out_full[n]
        o_swa = _swa_fwd(proj, p["sqn"], p["skn"], p["sinks"], swa_tables)
        last = l == DEPTH - 1
        y, z, *loss_acc = _out_fwd(proj, o_mla, o_swa, h_in, p["w_out"], p["cw"],
                                   target if last else None)
        saved.append(dict(x=h_in, proj=proj, hb=hb, q=q, k=k, v=v, qt=qt, kt=kt, o_mla=o_mla, lse=lse,
                          o_swa=o_swa, z=z))
        h_in = y

    dy, loss_acc = h_in, loss_acc[0]
    loss = lax.psum(loss_acc[0, 0], ("x", "y", "c"))

    grads = {n: [None] * DEPTH for n in WEIGHT_ORDER}

    def in_chunks(l):
        return jnp.stack([_cols(grads["w_in"][l], _sub_ranges(OLD_ORDER, k * shard_cols,
                                                              (k + 1) * shard_cols))
                          for k in range(N_CHIPS)])

    def out_chunks(l):
        return grads["w_out"][l].reshape(N_CHIPS, -1, D_MODEL)

    for l in reversed(range(DEPTH)):
        p, a = layers[l], saved[l]
        dconv, dgates, do_mla, do_mla_t, delta, do_swa, dcw = _out_bwd(dy, a["proj"], a["o_mla"], a["o_swa"],
                                                             p["w_out"], p["cw"])
        if l == 0:
            late_gs, from_sib = _split_wait("late_grads_to_sibling_wait", late_st1, 2,
                                            _plan_to_sibling, dconv)
            late_p = [_add_sibling(g, r, c_idx, BF16) for g, r in zip(late_gs, from_sib)]
            late_st2 = _split_start("late_grads_scatter_start", late_p,
                                    [(v.shape, v.dtype) for v in late_p], 6, _plan_scatter)
        grads["w_out"][l] = _weight_grads(a["z"], [dy], "dw_out")[0]
        grads["conv_w"][l] = dcw[0:3]
        delta_rows = jnp.transpose(delta, (1, 0)).reshape(MLA_HEADS // 2, 2, s)
        if l == 0:
            delta_rows = delta_rows + late_st2[-1][0, 0]
        dq, dk, dv = _mla_attn_bwd(a["q"], a["qt"], a["k"], a["kt"], a["v"], do_mla, do_mla_t,
                                   a["lse"], delta_rows)
        if l == 0:
            late_p, late_q = _split_wait("late_grads_scatter_wait", late_st2, 2, _plan_scatter, dq)
            late_full = [_sum_chips(_fill_own_slab(q_, p_, k_idx), c_idx)
                         for q_, p_ in zip(late_q, late_p)]
        dmla, dqan, dkvan, dqn, dkn, dwq_t, dwkv_t = _mla_prep_bwd(
            a["proj"], p["mla_norms"], p["mla_weights"], rope, dq, dk, dv)
        dwq, dwkv = dwq_t.T, dwkv_t.T
        dsq, dskv, dsqn, dskn, dsinks = _swa_bwd(a["proj"], p["sqn"], p["skn"], p["sinks"], swa_tables, do_swa)
        pieces = [dconv, dgates, dsq, dmla, dskv]
        dx, dg = _in_proj_bwd(pieces, a["x"], p["g"], p["w_in"], dy)
        grads["w_in"][l] = _weight_grads(a["hb"], pieces, "dw_in")
        grads["norm_g"][l] = dg[0]
        grads["mla_q_a_norm"][l] = dqan[:, 0]
        grads["mla_kv_a_norm"][l] = dkvan[:, 0]
        grads["mla_q_norm"][l] = dqn[:MLA_QK, 0]
        grads["mla_k_norm"][l] = dkn[:MLA_QK, 0]
        grads["mla_w_qb"][l] = dwq.reshape(MLA_Q_LORA, MLA_HEADS, LANES)[:, :, :MLA_QK].reshape(
            MLA_Q_LORA, MLA_HEADS * MLA_QK)
        dwk = dwkv[:, :MLA_HEADS * LANES].reshape(MLA_KV_LORA, MLA_HEADS, LANES)[:, :, :MLA_NOPE]
        dwv = dwkv[:, MLA_HEADS * LANES:].reshape(MLA_KV_LORA, MLA_HEADS, MLA_V)
        grads["mla_w_kvb"][l] = jnp.concatenate([dwk, dwv], axis=2).reshape(
            MLA_KV_LORA, MLA_HEADS * (MLA_NOPE + MLA_V))
        grads["swa_q_norm"][l] = dsqn[:, 0]
        grads["swa_k_norm"][l] = dskn[:, 0]
        grads["swa_sinks"][l] = dsinks[0]
        dy = dx
        if l == DEPTH - 1:
            late_st1 = _split_start(
                "late_grads_to_sibling_start", [in_chunks(l), out_chunks(l)],
                [((N_CHIPS, D_MODEL // 2, shard_cols), F32),
                 ((N_CHIPS, D_MIX // N_CHIPS // 2, D_MODEL), F32)], 2, _plan_to_sibling)
    grad_x = dy[None]
    full_grads = {n: jnp.stack(grads[n]) for n in WEIGHT_ORDER if n not in ("w_in", "w_out")}

    rest = tuple(n for n in SHARDED if n not in ("w_in", "w_out"))
    rep_shapes = [weights[n].shape for n in REPLICATED]
    rep_grads = jnp.concatenate([full_grads[n].reshape(-1) for n in REPLICATED])

    def chunk(g, n, k):
        width = g.shape[SHARD_AXIS[n]] // N_CHIPS
        return lax.slice_in_dim(g, k * width, (k + 1) * width, axis=SHARD_AXIS[n])

    g_small = jnp.stack([_pack_rows([chunk(full_grads[n], n, k) for n in rest] + [rep_grads],
                                    F32, SMALL_ROWS) for k in range(N_CHIPS)])
    gs = (in_chunks(0), out_chunks(0), g_small)
    from_sibling = _swap_halves_to_sibling(gs)
    partial = [_add_sibling(g, r, c_idx, dt) for g, r, dt in zip(gs, from_sibling, (BF16, BF16, F32))]
    by_chip = [_fill_own_slab(q, p, k_idx) for q, p in zip(_scatter_to_chips(partial), partial)]
    g_in0, g_out0, g_small_mine, g_in1, g_out1 = _join_halves(
        [_sum_chips(q, c_idx) for q in by_chip] + late_full)

    vals = _unpack_rows(g_small_mine, [weights[n].shape for n in rest] + [(rep_grads.shape[0],)])
    grad = dict(zip(rest, vals[:-1]))
    grad["w_in"] = jnp.stack([g_in0, g_in1])
    grad["w_out"] = jnp.stack([g_out0, g_out1])
    off = 0
    for n, sh in zip(REPLICATED, rep_shapes):
        grad[n] = vals[-1][off:off + sh[0] * sh[1]].reshape(sh)
        off += sh[0] * sh[1]
    results = {}
    for n in ("w_in", "w_out"):
        view = lambda a, n=n: a.reshape(-1, weights[n].shape[-1])
        res = _adamw(view(grad[n]), view(weights[n]), view(mom_m[n]), view(mom_v[n]))
        results[n] = [r.reshape(weights[n].shape) for r in res]
    small = tuple(n for n in WEIGHT_ORDER if n not in results)
    res = _adamw_small(*([d[n] for n in small] for d in (grad, weights, mom_m, mom_v)))
    for a, n in enumerate(small):
        results[n] = [res[kind][a] for kind in range(3)]
    unpacked = [grad] + [{n: results[n][kind] for n in WEIGHT_ORDER} for kind in range(3)]
    outs = [loss, grad_x]
    for group in unpacked:
        outs += [group[n] for n in WEIGHT_ORDER]
    return tuple(outs)
```

```python
import jax
import numpy as np
import jax.numpy as jnp
from jax import lax
from jax.experimental import pallas as pl
from jax.experimental.pallas import tpu as pltpu

F32 = jnp.float32
BF16 = jnp.bfloat16

D_MODEL = 1024
DEPTH = 2
GROUP = 512
D_MIX = 3 * GROUP
BLOCK = 128
RMS_EPS = 1e-6
NEG_INF = -1e30
MLA_HEADS = 8
MLA_QK = 96
MLA_NOPE = 64
MLA_ROPE = 32
MLA_V = 64
V_AUG = 80
MLA_Q_LORA = 256
MLA_KV_LORA = 128
ROPE_THETA = 10000.0
SWA_HEADS = 8
SWA_KV = 2
SWA_GROUP = 4
SWA_DIM = 64
N_CHIPS = 4

NC = 4352
OFF_SQ, OFF_MLA, OFF_SKV = 3072, 3584, 4096

VMEM_LIMIT = 56 * 1024 * 1024
LANES = 128
PACK_COLS = 1024
PACK_ROW_ALIGN = 16
SMALL_ROWS = 256

ADAM_LR = 0.001
ADAM_B1 = 0.9
ADAM_B2 = 0.999
ADAM_EPS = 1e-08
ADAM_WD = 0.01
ADAM_STEP = 10

MESH = pl.DeviceIdType.MESH


def _params(sem, vmem=VMEM_LIMIT):
    return pltpu.CompilerParams(dimension_semantics=sem, vmem_limit_bytes=vmem)


def _dot(a, b, dims):
    return lax.dot_general(a.astype(BF16), b.astype(BF16), (dims, ((), ())),
                           preferred_element_type=F32)


def _mm(a, b):
    return _dot(a, b, ((1,), (0,)))


def _mm_nt(a, b):
    return _dot(a, b, ((1,), (1,)))


def _rms(x, g, n=None):
    n = x.shape[-1] if n is None else n
    ms = jnp.sum(x * x, axis=-1, keepdims=True) * (1.0 / n)
    return x * lax.rsqrt(ms + RMS_EPS) * g


def _sigmoid(x):
    return 1.0 / (1.0 + jnp.exp(-x))


def _in_proj_fwd(x, g, w):
    s = x.shape[0]
    tm = min(512, s)

    def body(x_ref, g_ref, w_ref, proj_ref, hbt_ref):
        h = _rms(x_ref[...], g_ref[...])
        hbt_ref[...] = jnp.transpose(h).astype(BF16)
        proj_ref[...] = jnp.dot(h.astype(BF16), w_ref[...], preferred_element_type=F32)

    return pl.pallas_call(
        body, name="in_proj_fwd", grid=(s // tm,),
        in_specs=[pl.BlockSpec((tm, D_MODEL), lambda i: (i, 0)),
                  pl.BlockSpec((1, D_MODEL), lambda i: (0, 0)),
                  pl.BlockSpec((D_MODEL, NC), lambda i: (0, 0))],
        out_specs=[pl.BlockSpec((tm, NC), lambda i: (i, 0)),
                   pl.BlockSpec((D_MODEL, tm), lambda i: (0, i))],
        out_shape=[jax.ShapeDtypeStruct((s, NC), F32), jax.ShapeDtypeStruct((D_MODEL, s), BF16)],
        compiler_params=_params(("parallel",)),
    )(x, g, w)


def _in_proj_bwd(pieces, x, g, w, dres):
    s = x.shape[0]
    tm = min(512, s)
    n_p = len(pieces)

    def body(*refs):
        p_refs = refs[:n_p]
        x_ref, g_ref, w_ref, dres_ref, dx_ref, dg_ref = refs[n_p:]
        dh = None
        off = 0
        for r in p_refs:
            width = r.shape[1]
            t = _mm_nt(r[...], w_ref[:, off:off + width])
            dh = t if dh is None else dh + t
            off += width
        _, vjp = jax.vjp(_rms, x_ref[...], g_ref[...])
        dx, dg = vjp(dh)
        dx_ref[...] = dx + dres_ref[...]

        @pl.when(pl.program_id(0) == 0)
        def _():
            dg_ref[...] = jnp.zeros_like(dg_ref)

        dg_ref[...] += dg

    in_specs = [pl.BlockSpec((tm, p.shape[1]), lambda i: (i, 0)) for p in pieces]
    in_specs += [pl.BlockSpec((tm, D_MODEL), lambda i: (i, 0)),
                 pl.BlockSpec((1, D_MODEL), lambda i: (0, 0)),
                 pl.BlockSpec((D_MODEL, NC), lambda i: (0, 0)),
                 pl.BlockSpec((tm, D_MODEL), lambda i: (i, 0))]
    return pl.pallas_call(
        body, name="in_proj_bwd", grid=(s // tm,),
        in_specs=in_specs,
        out_specs=[pl.BlockSpec((tm, D_MODEL), lambda i: (i, 0)),
                   pl.BlockSpec((1, D_MODEL), lambda i: (0, 0))],
        out_shape=[jax.ShapeDtypeStruct((s, D_MODEL), F32), jax.ShapeDtypeStruct((1, D_MODEL), F32)],
        compiler_params=_params(("arbitrary",)),
    )(*pieces, x, g, w, dres)


def _weight_grads(at, bs, name):
    m, s = at.shape
    nb = len(bs)
    tk = min(512, s)

    def body(a_ref, *refs):
        b_refs, o_refs = refs[:nb], refs[nb:]

        @pl.when(pl.program_id(0) == 0)
        def _():
            for o_ref in o_refs:
                o_ref[...] = jnp.zeros_like(o_ref)

        a = a_ref[...]
        for b_ref, o_ref in zip(b_refs, o_refs):
            o_ref[...] += _mm(a, b_ref[...])

    return pl.pallas_call(
        body, name=name, grid=(s // tk,),
        in_specs=[pl.BlockSpec((m, tk), lambda k: (0, k))]
        + [pl.BlockSpec((tk, b.shape[1]), lambda k: (k, 0)) for b in bs],
        out_specs=[pl.BlockSpec((m, b.shape[1]), lambda k: (0, 0)) for b in bs],
        out_shape=[jax.ShapeDtypeStruct((m, b.shape[1]), F32) for b in bs],
        compiler_params=_params(("arbitrary",)),
    )(at, *bs)


def _rms0(x, g, n=None):
    n = x.shape[0] if n is None else n
    ms = jnp.sum(x * x, axis=0, keepdims=True) * (1.0 / n)
    return x * lax.rsqrt(ms + RMS_EPS) * g


@jax.custom_vjp
def _rope0(t, c, s1, s2):
    return t * c + pltpu.roll(t, LANES - 16, 0) * s1 + pltpu.roll(t, 16, 0) * s2


def _rope0_fwd(t, c, s1, s2):
    return _rope0(t, c, s1, s2), (c, s1, s2)


def _rope0_bwd(res, g):
    c, s1, s2 = res
    dt = g * c + pltpu.roll(g * s1, 16, 0) + pltpu.roll(g * s2, LANES - 16, 0)
    return dt, jnp.zeros_like(c), jnp.zeros_like(s1), jnp.zeros_like(s2)


_rope0.defvjp(_rope0_fwd, _rope0_bwd)


@jax.custom_vjp
def _mmw(w, wt, x):
    return _mm(w, x)


def _mmw_fwd(w, wt, x):
    return _mm(w, x), (wt, x)


def _mmw_bwd(res, g):
    wt, x = res
    return _mm_nt(g, x), jnp.zeros_like(wt), _mm(wt, g)


_mmw.defvjp(_mmw_fwd, _mmw_bwd)


def _prep_fn(q_lat, kv_lat, kr, qan, kvan, qn, kn, wq, wk, wv, wqt, wkt, wvt, c, s1, s2, mm):
    tokens = q_lat.shape[1]
    rq = _rms0(q_lat, qan)
    rkv = _rms0(kv_lat, kvan)
    qn_b = jnp.broadcast_to(qn, (LANES, tokens))
    kn_b = jnp.broadcast_to(kn, (LANES, tokens))
    qs, ks = [], []
    for h in range(MLA_HEADS):
        qs.append(_rope0(_rms0(mm(wq[h], wqt[h], rq), qn_b, MLA_QK), c, s1, s2))
        ks.append(_rope0(_rms0(mm(wk[h], wkt[h], rkv) + kr, kn_b, MLA_QK), c, s1, s2))
    return tuple(qs), tuple(ks), mm(wv, wvt, rkv)


def _prep_weights(wq_ref, wkv_ref, wqt_ref, wkvt_ref):
    heads = range(MLA_HEADS)
    wq = tuple(wqt_ref[LANES * h:LANES * (h + 1), :].astype(F32) for h in heads)
    wk = tuple(wkvt_ref[LANES * h:LANES * (h + 1), :].astype(F32) for h in heads)
    wv = wkvt_ref[LANES * MLA_HEADS:, :].astype(F32)
    wqt = tuple(wq_ref[:, LANES * h:LANES * (h + 1)].astype(F32) for h in heads)
    wkt = tuple(wkv_ref[:, LANES * h:LANES * (h + 1)].astype(F32) for h in heads)
    wvt = wkv_ref[:, LANES * MLA_HEADS:].astype(F32)
    return wq, wk, wv, wqt, wkt, wvt


def _prep_in_specs(tm):
    const = lambda shape: pl.BlockSpec(shape, lambda i: (0, 0))
    col = lambda height: pl.BlockSpec((height, tm), lambda i: (0, i))
    return [pl.BlockSpec((tm, 512), lambda i: (i, OFF_MLA // 512)),
            const((MLA_Q_LORA, 1)), const((MLA_KV_LORA, 1)), const((LANES, 1)), const((LANES, 1)),
            const((MLA_Q_LORA, 1024)), const((MLA_KV_LORA, 1536)),
            const((1024, MLA_Q_LORA)), const((1536, MLA_KV_LORA)),
            col(LANES), col(LANES), col(LANES)]


def _prep_operands(blk_ref, refs):
    qan_ref, kvan_ref, qn_ref, kn_ref, wq_ref, wkv_ref, wqt_ref, wkvt_ref, c_ref, s1_ref, s2_ref = refs
    blk_t = jnp.transpose(blk_ref[...])
    diff = (blk_t[0:256], blk_t[256:384], blk_t[384:512],
            qan_ref[...], kvan_ref[...], qn_ref[...], kn_ref[...])
    weights = _prep_weights(wq_ref, wkv_ref, wqt_ref, wkvt_ref)
    return diff, weights, (c_ref[...], s1_ref[...], s2_ref[...])


def _mla_prep_fwd(proj, norms, weights, rope):
    s = proj.shape[0]
    tm = min(512, s)

    def body(blk_ref, *refs):
        ins, (q_ref, k_ref, v_ref, qt_ref, kt_ref, vt_ref) = refs[:11], refs[11:]
        diff, (wq, wk, wv, wqt, wkt, wvt), tables = _prep_operands(blk_ref, ins)
        qs, ks, v = _prep_fn(*diff, wq, wk, wv, wqt, wkt, wvt, *tables,
                             lambda w, wt, x: _mm(w, x))
        for h in range(MLA_HEADS):
            q2 = qs[h] * Q_PRESCALE
            qt_ref[LANES * h:LANES * (h + 1), :] = q2.astype(BF16)
            kt_ref[LANES * h:LANES * (h + 1), :] = ks[h].astype(BF16)
            q_ref[:, LANES * h:LANES * (h + 1)] = jnp.transpose(q2).astype(BF16)
            k_ref[:, LANES * h:LANES * (h + 1)] = jnp.transpose(ks[h]).astype(BF16)
        ones_row = (lax.broadcasted_iota(jnp.int32, (V_AUG - MLA_V, v.shape[1]), 0) == 0).astype(BF16)
        for h in range(MLA_HEADS):
            vt_ref[V_AUG * h:V_AUG * h + MLA_V, :] = v[MLA_V * h:MLA_V * (h + 1)].astype(BF16)
            vt_ref[V_AUG * h + MLA_V:V_AUG * (h + 1), :] = ones_row
        v_ref[...] = jnp.transpose(v).astype(BF16)

    row = lambda width: pl.BlockSpec((tm, width), lambda i: (i, 0))
    col = lambda height: pl.BlockSpec((height, tm), lambda i: (0, i))
    return pl.pallas_call(
        body, name="mla_prep_fwd", grid=(s // tm,),
        in_specs=_prep_in_specs(tm),
        out_specs=[row(1024), row(1024), row(512), col(1024), col(1024), col(MLA_HEADS * V_AUG)],
        out_shape=[jax.ShapeDtypeStruct((s, 1024), BF16), jax.ShapeDtypeStruct((s, 1024), BF16),
                   jax.ShapeDtypeStruct((s, 512), BF16), jax.ShapeDtypeStruct((1024, s), BF16),
                   jax.ShapeDtypeStruct((1024, s), BF16),
                   jax.ShapeDtypeStruct((MLA_HEADS * V_AUG, s), BF16)],
        compiler_params=_params(("parallel",)),
    )(proj, *norms, *weights, *rope)


def _mla_prep_bwd(proj, norms, weights, rope, dq, dk, dv):
    s = proj.shape[0]
    tm = min(512, s)

    def body(blk_ref, *refs):
        ins, (dq_ref, dk_ref, dv_ref) = refs[:11], refs[11:14]
        dblk_ref, dqan_ref, dkvan_ref, dqn_ref, dkn_ref, dwq_ref, dwkv_ref = refs[14:]
        diff, (wq, wk, wv, wqt, wkt, wvt), tables = _prep_operands(blk_ref, ins)

        def fn(q_lat, kv_lat, kr, qan, kvan, qn, kn, wq_, wk_, wv_):
            return _prep_fn(q_lat, kv_lat, kr, qan, kvan, qn, kn, wq_, wk_, wv_, wqt, wkt, wvt,
                            *tables, _mmw)

        _, vjp = jax.vjp(fn, *diff, wq, wk, wv)
        heads = range(MLA_HEADS)
        cts = (tuple(dq_ref[LANES * h:LANES * (h + 1), :] for h in heads),
               tuple(dk_ref[LANES * h:LANES * (h + 1), :] for h in heads), dv_ref[...])
        dq_lat, dkv_lat, dkr, dqan, dkvan, dqn, dkn, dwq_h, dwk_h, dwv = vjp(cts)
        dblk_ref[...] = jnp.transpose(
            jnp.concatenate([dq_lat, dkv_lat, dkr], axis=0)).astype(BF16)

        @pl.when(pl.program_id(0) == 0)
        def _():
            for r in (dqan_ref, dkvan_ref, dqn_ref, dkn_ref, dwq_ref, dwkv_ref):
                r[...] = jnp.zeros_like(r)

        dqan_ref[...] += dqan
        dkvan_ref[...] += dkvan
        dqn_ref[...] += dqn
        dkn_ref[...] += dkn
        for h in heads:
            dwq_ref[LANES * h:LANES * (h + 1), :] += dwq_h[h]
            dwkv_ref[LANES * h:LANES * (h + 1), :] += dwk_h[h]
        dwkv_ref[LANES * MLA_HEADS:, :] += dwv

    const = lambda shape: pl.BlockSpec(shape, lambda i: (0, 0))
    col = lambda height: pl.BlockSpec((height, tm), lambda i: (0, i))
    shapes = [(MLA_Q_LORA, 1), (MLA_KV_LORA, 1), (LANES, 1), (LANES, 1),
              (1024, MLA_Q_LORA), (1536, MLA_KV_LORA)]
    return pl.pallas_call(
        body, name="mla_prep_bwd", grid=(s // tm,),
        in_specs=_prep_in_specs(tm) + [col(1024), col(1024), col(512)],
        out_specs=[pl.BlockSpec((tm, 512), lambda i: (i, 0))] + [const(sh) for sh in shapes],
        out_shape=[jax.ShapeDtypeStruct((s, 512), BF16)]
        + [jax.ShapeDtypeStruct(sh, F32) for sh in shapes],
        compiler_params=_params(("arbitrary",)),
    )(proj, *norms, *weights, *rope, dq, dk, dv)


MLA_SCALE = MLA_QK ** -0.5


LOG2E = 1.4426950408889634
LN2 = 0.6931471805599453
Q_PRESCALE = MLA_SCALE * LOG2E
HEAD_GROUPS = ((0, 1),)


def _mla_attn_fwd(q2, k, vt):
    s = q2.shape[0]
    t = min(512, s)
    tk = min(128, s)
    nq = s // t
    r = t // tk

    def body(q_ref, k_ref, vt_ref, o_ref, lse_ref, acc_ref):
        i = pl.program_id(1)
        row = lax.broadcasted_iota(jnp.int32, (tk, t), 0)
        col = lax.broadcasted_iota(jnp.int32, (tk, t), 1)
        qh = [q_ref[:, LANES * hh:LANES * (hh + 1)] for hh in range(2)]
        acc_ref[...] = jnp.zeros_like(acc_ref)

        def scores(j, heads, diag=None):
            r0 = pl.multiple_of(j * tk, tk)
            q_lo = 0 if diag is None else diag * tk
            out = []
            for hh in heads:
                kc = k_ref[pl.ds(r0, tk), LANES * hh:LANES * (hh + 1)]
                sc = lax.dot_general(kc, qh[hh][q_lo:], (((1,), (1,)), ((), ())),
                                     preferred_element_type=F32)
                out.append(sc if diag is None
                           else jnp.where(row[:, :t - q_lo] <= col[:, :t - q_lo], sc, NEG_INF))
            return tuple(out)

        for heads in HEAD_GROUPS:
            stats = tuple(jnp.full((1, t), NEG_INF, F32) for _ in heads)

            def consume(j, scs, stats, q_lo=0, heads=heads):
                r0 = pl.multiple_of(j * tk, tk)
                out, ps, alphas = [], [], []
                for n, hh in enumerate(heads):
                    m_old = stats[n][:, q_lo:]
                    m_new = jnp.maximum(m_old, jnp.max(scs[n], axis=0, keepdims=True))
                    ps.append(jnp.exp2(scs[n] - m_new).astype(BF16))
                    alphas.append(jnp.exp2(m_old - m_new))
                    out.append(m_new if q_lo == 0
                               else jnp.concatenate([stats[n][:, :q_lo], m_new], axis=1))
                for n, hh in enumerate(heads):
                    vc = vt_ref[V_AUG * hh:V_AUG * (hh + 1), pl.ds(r0, tk)]
                    acc_ref[hh, :, q_lo:] = alphas[n] * acc_ref[hh, :, q_lo:] + jnp.dot(
                        vc, ps[n], preferred_element_type=F32)
                return tuple(out)

            def group(j0, stats, diag, heads=heads):
                scs = [scores(j0 + d, heads, d if diag else None) for d in range(r)]
                for d in range(r):
                    stats = consume(j0 + d, scs[d], stats, d * tk if diag else 0)
                return stats

            stats = group(r * i, stats, True)
            stats = lax.fori_loop(0, i, lambda j, st: group(r * j, st, False), stats)
            for n, hh in enumerate(heads):
                l = acc_ref[hh, MLA_V:MLA_V + 1, :]
                o_ref[:, MLA_V * hh:MLA_V * (hh + 1)] = jnp.transpose(acc_ref[hh, 0:MLA_V, :] / l)
                lse_ref[0, hh:hh + 1, :] = stats[n] + jnp.log2(l)

    return pl.pallas_call(
        body, name="mla_attn_fwd", grid=(MLA_HEADS // 2, nq),
        in_specs=[pl.BlockSpec((t, 256), lambda p, i: (i, p)),
                  pl.BlockSpec((s, 256), lambda p, i: (0, p)),
                  pl.BlockSpec((2 * V_AUG, s), lambda p, i: (p, 0))],
        out_specs=[pl.BlockSpec((t, 128), lambda p, i: (i, p)),
                   pl.BlockSpec((1, 2, t), lambda p, i: (p, 0, i))],
        out_shape=[jax.ShapeDtypeStruct((s, 512), F32),
                   jax.ShapeDtypeStruct((MLA_HEADS // 2, 2, s), F32)],
        scratch_shapes=[pltpu.VMEM((2, V_AUG, t), F32)],
        compiler_params=_params(("parallel", "arbitrary")),
    )(q2, k, vt)


def _mla_attn_bwd(q2, q2t, k, kt, v, do, dot, lse_rows, delta_rows):
    s = q2.shape[0]
    t = min(512, s)
    nq = s // t

    def body(q_ref, qt_ref, k_ref, kt_ref, v_ref, do_ref, dot_ref, lse_ref, dl_ref,
             dq_ref, dk_ref, dv_ref):
        j = pl.program_id(1)

        @pl.when(j == 0)
        def _():
            dq_ref[...] = jnp.zeros_like(dq_ref)

        dk_ref[...] = jnp.zeros_like(dk_ref)
        dv_ref[...] = jnp.zeros_like(dv_ref)
        row = lax.broadcasted_iota(jnp.int32, (t, t), 0)
        col = lax.broadcasted_iota(jnp.int32, (t, t), 1)
        causal_t = row <= col
        kh = [k_ref[:, LANES * hh:LANES * (hh + 1)] for hh in range(2)]
        kth = [kt_ref[LANES * hh:LANES * (hh + 1), :] for hh in range(2)]
        vh = [v_ref[:, MLA_V * hh:MLA_V * (hh + 1)] for hh in range(2)]
        nt = (((1,), (1,)), ((), ()))

        def step(i, masked):
            r0 = pl.multiple_of(i * t, t)
            sd = []
            for hh in range(2):
                qh = q_ref[pl.ds(r0, t), LANES * hh:LANES * (hh + 1)]
                doh = do_ref[pl.ds(r0, t), MLA_V * hh:MLA_V * (hh + 1)]
                sc_t = lax.dot_general(kh[hh], qh, nt, preferred_element_type=F32)
                sd.append(jnp.where(causal_t, sc_t, NEG_INF) if masked else sc_t)
                sd.append(lax.dot_general(vh[hh], doh, nt, preferred_element_type=F32))
            for hh in range(2):
                lse = lse_ref[0, hh:hh + 1, pl.ds(r0, t)]
                dl = dl_ref[0, hh:hh + 1, pl.ds(r0, t)]
                p_t = jnp.exp2(sd[2 * hh] - lse)
                g_t = (p_t * (sd[2 * hh + 1] - dl)).astype(BF16)
                qth = qt_ref[LANES * hh:LANES * (hh + 1), pl.ds(r0, t)]
                doth = dot_ref[MLA_V * hh:MLA_V * (hh + 1), pl.ds(r0, t)]
                dv_ref[MLA_V * hh:MLA_V * (hh + 1), :] += lax.dot_general(
                    doth, p_t.astype(BF16), nt, preferred_element_type=F32)
                dk_ref[LANES * hh:LANES * (hh + 1), :] += lax.dot_general(
                    qth, g_t, nt, preferred_element_type=F32)
                dq_ref[LANES * hh:LANES * (hh + 1), pl.ds(r0, t)] += jnp.dot(
                    kth[hh], g_t, preferred_element_type=F32)

        step(j, True)

        def trip(i, carry):
            step(i, False)
            return carry

        lax.fori_loop(j + 1, nq, trip, 0)
        dk_ref[...] = dk_ref[...] * LN2

        @pl.when(j == nq - 1)
        def _():
            dq_ref[...] = dq_ref[...] * MLA_SCALE

    return pl.pallas_call(
        body, name="mla_attn_bwd", grid=(MLA_HEADS // 2, nq),
        in_specs=[pl.BlockSpec((s, 256), lambda p, j: (0, p)),
                  pl.BlockSpec((256, s), lambda p, j: (p, 0)),
                  pl.BlockSpec((t, 256), lambda p, j: (j, p)),
                  pl.BlockSpec((256, t), lambda p, j: (p, j)),
                  pl.BlockSpec((t, 128), lambda p, j: (j, p)),
                  pl.BlockSpec((s, 128), lambda p, j: (0, p)),
                  pl.BlockSpec((128, s), lambda p, j: (p, 0)),
                  pl.BlockSpec((1, 2, s), lambda p, j: (p, 0, 0)),
                  pl.BlockSpec((1, 2, s), lambda p, j: (p, 0, 0))],
        out_specs=[pl.BlockSpec((256, s), lambda p, j: (p, 0)),
                   pl.BlockSpec((256, t), lambda p, j: (p, j)),
                   pl.BlockSpec((128, t), lambda p, j: (p, j))],
        out_shape=[jax.ShapeDtypeStruct((1024, s), F32), jax.ShapeDtypeStruct((1024, s), F32),
                   jax.ShapeDtypeStruct((512, s), F32)],
        compiler_params=_params(("parallel", "arbitrary")),
    )(q2, q2t, k, kt, v, do, dot, lse_rows, delta_rows)


SWA_SCALE = SWA_DIM ** -0.5
SWA_COLS = SWA_GROUP * BLOCK
SWA_LOG2 = SWA_SCALE * LOG2E


def _swa_tables():
    k = np.arange(2 * BLOCK)[:, None]
    col = np.arange(SWA_COLS)[None, :]
    dist = BLOCK + (col % BLOCK) - k
    valid = (dist >= 0) & (dist < BLOCK)
    out = np.zeros((2, SWA_KV, 2 * BLOCK, SWA_COLS), np.float32)
    for first in range(2):
        ok = valid & ((k >= BLOCK) | (first == 0))
        for j in range(SWA_KV):
            slope = 2.0 ** -(SWA_GROUP * j + col // BLOCK + 1)
            out[first, j] = np.where(ok, -slope * dist * LOG2E, NEG_INF)
    return jnp.asarray(out)


def _swa_tile_inputs(sq_ref, skv_ref, halo_ref, qn_ref, kn_ref, sk_ref, add_ref, first):
    tokens = sq_ref.shape[0]
    kv_all = jnp.concatenate([halo_ref[...], skv_ref[...]], axis=0)
    kv_t = jnp.transpose(kv_all)
    sq_t = jnp.transpose(sq_ref[...])
    k_raw = [kv_t[SWA_DIM * j:SWA_DIM * (j + 1)] for j in range(SWA_KV)]
    v_t = [kv_t[128 + SWA_DIM * j:128 + SWA_DIM * (j + 1)] for j in range(SWA_KV)]
    v_nat = [kv_all[:, 128 + SWA_DIM * j:128 + SWA_DIM * (j + 1)] for j in range(SWA_KV)]
    q_raw = [sq_t[SWA_DIM * h:SWA_DIM * (h + 1)] for h in range(SWA_HEADS)]
    qn_b = jnp.broadcast_to(qn_ref[...], (SWA_DIM, tokens))
    kn_b = jnp.broadcast_to(kn_ref[...], (SWA_DIM, tokens + BLOCK))
    lane_grp = lax.broadcasted_iota(jnp.int32, (1, SWA_COLS), 1) // BLOCK
    sinks, adds = [], []
    for j in range(SWA_KV):
        row = jnp.zeros((1, SWA_COLS), F32)
        for g in range(SWA_GROUP):
            h = SWA_GROUP * j + g
            row = jnp.where(lane_grp == g, sk_ref[:, h:h + 1] * LOG2E, row)
        sinks.append(row)
        adds.append((jnp.where(first, add_ref[1, j], add_ref[0, j]), add_ref[0, j]))
    return k_raw, v_t, v_nat, q_raw, qn_b, kn_b, sinks, adds


def _swa_probs(kb, qs_t, add, sink):
    s2 = jnp.dot(kb, qs_t, preferred_element_type=F32) * SWA_LOG2 + add
    m = jnp.maximum(jnp.max(s2, axis=0, keepdims=True), sink)
    e = jnp.exp2(s2 - m)
    es = jnp.exp2(sink - m)
    inv = 1.0 / (jnp.sum(e, axis=0, keepdims=True) + es)
    return e, inv, es


def _swa_queries(qn_t, j, b):
    return jnp.concatenate([qn_t[SWA_GROUP * j + g][:, BLOCK * b:BLOCK * (b + 1)]
                            for g in range(SWA_GROUP)], axis=1)


def _swa_fwd(proj, qn, kn, sinks, tables):
    s = proj.shape[0]
    ts = min(1024, s)
    nb = ts // BLOCK

    def body(sq_ref, skv_ref, halo_ref, qn_ref, kn_ref, sk_ref, add_ref, o_ref, ot_ref):
        first = pl.program_id(0) == 0
        k_raw, v_t, _, q_raw, qn_b, kn_b, sink_rows, adds = _swa_tile_inputs(
            sq_ref, skv_ref, halo_ref, qn_ref, kn_ref, sk_ref, add_ref, first)
        kn_nat = [jnp.transpose(_rms0(k, kn_b)).astype(BF16) for k in k_raw]
        v_t = [v.astype(BF16) for v in v_t]
        qn_t = [_rms0(q, qn_b).astype(BF16) for q in q_raw]
        for b in range(nb):
            band = slice(BLOCK * b, BLOCK * (b + 2))
            for j in range(SWA_KV):
                e, inv, _ = _swa_probs(kn_nat[j][band], _swa_queries(qn_t, j, b),
                                       adds[j][0 if b == 0 else 1], sink_rows[j])
                o_t = jnp.dot(v_t[j][:, band], (e * inv).astype(BF16),
                              preferred_element_type=F32)
                for g in range(SWA_GROUP):
                    h = SWA_GROUP * j + g
                    ot_ref[SWA_DIM * h:SWA_DIM * (h + 1), BLOCK * b:BLOCK * (b + 1)] = (
                        o_t[:, BLOCK * g:BLOCK * (g + 1)])
        o_ref[...] = jnp.transpose(ot_ref[...])

    const = lambda shape: pl.BlockSpec(shape, lambda i: (0,) * len(shape))
    return pl.pallas_call(
        body, name="swa_fwd", grid=(s // ts,),
        in_specs=[pl.BlockSpec((ts, 512), lambda i: (i, OFF_SQ // 512)),
                  pl.BlockSpec((ts, 256), lambda i: (i, OFF_SKV // 256)),
                  pl.BlockSpec((BLOCK, 256), lambda i: (jnp.maximum(i * nb - 1, 0), OFF_SKV // 256)),
                  const((SWA_DIM, 1)), const((SWA_DIM, 1)), const((1, SWA_HEADS)),
                  const(tables.shape)],
        out_specs=pl.BlockSpec((ts, 512), lambda i: (i, 0)),
        out_shape=jax.ShapeDtypeStruct((s, 512), F32),
        scratch_shapes=[pltpu.VMEM((512, ts), F32)],
        compiler_params=_params(("parallel",)),
    )(proj, proj, proj, qn, kn, sinks, tables)


def _swa_bwd(proj, qn, kn, sinks, tables, do):
    s = proj.shape[0]
    ts = min(1024, s)
    nb = ts // BLOCK
    nt = s // ts

    def body(sq_ref, skv_ref, halo_ref, qn_ref, kn_ref, sk_ref, add_ref, do_ref,
             dsq_ref, dskv_ref, dqn_ref, dkn_ref, dsk_ref, carry_ref, dqt_ref, dkvt_ref):
        step = pl.program_id(0)
        first = step == nt - 1

        @pl.when(step == 0)
        def _():
            carry_ref[...] = jnp.zeros_like(carry_ref)
            dqn_ref[...] = jnp.zeros_like(dqn_ref)
            dkn_ref[...] = jnp.zeros_like(dkn_ref)
            dsk_ref[...] = jnp.zeros_like(dsk_ref)

        k_raw, v_t, v_nat, q_raw, qn_b, kn_b, sink_rows, adds = _swa_tile_inputs(
            sq_ref, skv_ref, halo_ref, qn_ref, kn_ref, sk_ref, add_ref, first)
        kn_f = [_rms0(k, kn_b) for k in k_raw]
        kn_t = [k.astype(BF16) for k in kn_f]
        kn_nat = [jnp.transpose(k).astype(BF16) for k in kn_f]
        v_nat = [v.astype(BF16) for v in v_nat]
        qn_t = [_rms0(q, qn_b).astype(BF16) for q in q_raw]
        do_t = jnp.transpose(do_ref[...].astype(F32)).astype(BF16)

        dkvt_ref[...] = jnp.zeros_like(dkvt_ref)
        dsink = [jnp.zeros((1, SWA_COLS), F32) for _ in range(SWA_KV)]
        nt_dims = (((1,), (1,)), ((), ()))
        for b in range(nb):
            rows = slice(BLOCK * b, BLOCK * (b + 1))
            band = slice(BLOCK * b, BLOCK * (b + 2))
            for j in range(SWA_KV):
                heads = [SWA_GROUP * j + g for g in range(SWA_GROUP)]
                qs_t = _swa_queries(qn_t, j, b)
                dos_t = jnp.concatenate([do_t[SWA_DIM * h:SWA_DIM * (h + 1), rows] for h in heads],
                                        axis=1)
                e, inv, es = _swa_probs(kn_nat[j][band], qs_t, adds[j][0 if b == 0 else 1],
                                        sink_rows[j])
                p = e * inv
                dp = jnp.dot(v_nat[j][band], dos_t, preferred_element_type=F32)
                dsum = jnp.sum(p * dp, axis=0, keepdims=True)
                dsink[j] = dsink[j] - es * inv * dsum
                g_t = (p * (dp - dsum) * SWA_SCALE).astype(BF16)
                dv_t = lax.dot_general(dos_t, p.astype(BF16), nt_dims,
                                       preferred_element_type=F32)
                dk_t = lax.dot_general(qs_t, g_t, nt_dims, preferred_element_type=F32)
                dq_t = jnp.dot(kn_t[j][:, band], g_t, preferred_element_type=F32)
                dkvt_ref[SWA_DIM * j:SWA_DIM * (j + 1), band] += dk_t
                dkvt_ref[128 + SWA_DIM * j:128 + SWA_DIM * (j + 1), band] += dv_t
                for g, h in enumerate(heads):
                    dqt_ref[SWA_DIM * h:SWA_DIM * (h + 1), rows] = dq_t[:, BLOCK * g:BLOCK * (g + 1)]

        dqn = jnp.zeros((SWA_DIM, 1), F32)
        for h in range(SWA_HEADS):
            _, vjp = jax.vjp(_rms0, q_raw[h], qn_ref[...])
            dq, dg = vjp(dqt_ref[SWA_DIM * h:SWA_DIM * (h + 1), :])
            dqt_ref[SWA_DIM * h:SWA_DIM * (h + 1), :] = dq
            dqn = dqn + dg
        dqn_ref[...] += dqn
        dsq_ref[...] = jnp.transpose(dqt_ref[...]).astype(BF16)
        dkn = jnp.zeros((SWA_DIM, 1), F32)
        lane_grp = lax.broadcasted_iota(jnp.int32, (1, SWA_COLS), 1) // BLOCK
        for j in range(SWA_KV):
            _, vjp = jax.vjp(_rms0, k_raw[j], kn_ref[...])
            dk, dg = vjp(dkvt_ref[SWA_DIM * j:SWA_DIM * (j + 1), :])
            dkvt_ref[SWA_DIM * j:SWA_DIM * (j + 1), :] = dk
            dkn = dkn + dg
            for g in range(SWA_GROUP):
                h = SWA_GROUP * j + g
                dsk_ref[:, h:h + 1] += jnp.sum(jnp.where(lane_grp == g, dsink[j], 0.0), axis=1,
                                               keepdims=True)
        dkn_ref[...] += dkn
        dkv = jnp.transpose(dkvt_ref[...])
        dskv_ref[0:ts - BLOCK, :] = dkv[BLOCK:ts].astype(BF16)
        dskv_ref[ts - BLOCK:ts, :] = (dkv[ts:ts + BLOCK] + carry_ref[...]).astype(BF16)
        carry_ref[...] = dkv[0:BLOCK]

    const = lambda shape: pl.BlockSpec(shape, lambda st: (0,) * len(shape))
    return pl.pallas_call(
        body, name="swa_bwd", grid=(nt,),
        in_specs=[pl.BlockSpec((ts, 512), lambda st: (nt - 1 - st, OFF_SQ // 512)),
                  pl.BlockSpec((ts, 256), lambda st: (nt - 1 - st, OFF_SKV // 256)),
                  pl.BlockSpec((BLOCK, 256),
                               lambda st: (jnp.maximum((nt - 1 - st) * nb - 1, 0), OFF_SKV // 256)),
                  const((SWA_DIM, 1)), const((SWA_DIM, 1)), const((1, SWA_HEADS)),
                  const(tables.shape),
                  pl.BlockSpec((ts, 512), lambda st: (nt - 1 - st, 0))],
        out_specs=[pl.BlockSpec((ts, 512), lambda st: (nt - 1 - st, 0)),
                   pl.BlockSpec((ts, 256), lambda st: (nt - 1 - st, 0)),
                   const((SWA_DIM, 1)), const((SWA_DIM, 1)), const((1, SWA_HEADS))],
        out_shape=[jax.ShapeDtypeStruct((s, 512), BF16), jax.ShapeDtypeStruct((s, 256), BF16),
                   jax.ShapeDtypeStruct((SWA_DIM, 1), F32), jax.ShapeDtypeStruct((SWA_DIM, 1), F32),
                   jax.ShapeDtypeStruct((1, SWA_HEADS), F32)],
        scratch_shapes=[pltpu.VMEM((BLOCK, 256), F32), pltpu.VMEM((512, ts), F32),
                        pltpu.VMEM((256, ts + BLOCK), F32)],
        compiler_params=_params(("arbitrary",)),
    )(proj, proj, proj, qn, kn, sinks, tables, do)


HALO = 8


def _shift_down(u, halo, k):
    tm = u.shape[0]
    rid = lax.broadcasted_iota(jnp.int32, u.shape, 0)
    out = pltpu.roll(u, k, 0)
    for r in range(k):
        out = jnp.where(rid == r, halo[HALO - k + r:HALO - k + r + 1, :], out)
    return out


def _shift_up(u, halo, k):
    tm = u.shape[0]
    rid = lax.broadcasted_iota(jnp.int32, u.shape, 0)
    out = pltpu.roll(u, tm - k, 0)
    for r in range(k):
        out = jnp.where(rid == tm - k + r, halo[r:r + 1, :], out)
    return out


def _conv_fwd_vals(conv_ref, convp_ref, cw_ref, is_first):
    c_h, c_b, c_c = conv_ref[:, 0:512], conv_ref[:, 512:1024], conv_ref[:, 1024:1536]
    u = c_c * c_h
    up = jnp.where(is_first, 0.0, convp_ref[:, 1024:1536] * convp_ref[:, 0:512])
    u1 = _shift_down(u, up, 1)
    u2 = _shift_down(u, up, 2)
    yc = cw_ref[0:1, :] * u2 + cw_ref[1:2, :] * u1 + cw_ref[2:3, :] * u
    return c_h, c_b, c_c, u, u1, u2, yc


def _out_fwd(proj, o_mla, o_swa, x, w_out, cw, target=None):
    s = proj.shape[0]
    tm = min(512, s)
    nt = s // tm
    with_loss = target is not None

    def body(*refs):
        conv_ref, convp_ref, gates_ref, om_ref, os_ref, x_ref, w_ref, cw_ref = refs[:8]
        if with_loss:
            t_ref, y_ref, zt_ref, loss_ref, z_ref = refs[8:]
        else:
            y_ref, zt_ref, z_ref = refs[8:]
        i = pl.program_id(0)
        _, c_b, _, _, _, _, yc = _conv_fwd_vals(conv_ref, convp_ref, cw_ref, i == 0)
        mix = (om_ref[...], c_b * yc, os_ref[...])
        for n in range(3):
            g = gates_ref[:, GROUP * n:GROUP * (n + 1)]
            z = mix[n] * (g * _sigmoid(g))
            z_ref[:, GROUP * n:GROUP * (n + 1)] = z.astype(BF16)
            zt_ref[GROUP * n:GROUP * (n + 1), :] = jnp.transpose(z).astype(BF16)
        y = x_ref[...] + jnp.dot(z_ref[...], w_ref[...], preferred_element_type=F32)
        if not with_loss:
            y_ref[...] = y
            return
        err = y - t_ref[...]
        y_ref[...] = err * (1.0 / D_MODEL)

        @pl.when(i == 0)
        def _():
            loss_ref[...] = jnp.zeros_like(loss_ref)

        sq = jnp.sum((err * err).reshape(tm // 8, 8, D_MODEL), axis=0)
        part = sq[:, 0:LANES]
        for c in range(1, D_MODEL // LANES):
            part = part + sq[:, LANES * c:LANES * (c + 1)]
        loss_ref[...] += part

        @pl.when(i == nt - 1)
        def _():
            loss_ref[...] = jnp.full(loss_ref.shape, (0.5 / D_MODEL) * jnp.sum(loss_ref[...]), F32)

    row = lambda width: pl.BlockSpec((tm, width), lambda i: (i, 0))
    in_specs = [pl.BlockSpec((tm, 1536), lambda i: (i, 0)),
                pl.BlockSpec((HALO, 1536), lambda i: (jnp.maximum(i * (tm // HALO) - 1, 0), 0)),
                pl.BlockSpec((tm, 1536), lambda i: (i, 1)),
                row(512), row(512), row(D_MODEL),
                pl.BlockSpec((D_MIX, D_MODEL), lambda i: (0, 0)),
                pl.BlockSpec((HALO, 512), lambda i: (0, 0))]
    out_specs = [row(D_MODEL), pl.BlockSpec((D_MIX, tm), lambda i: (0, i))]
    out_shape = [jax.ShapeDtypeStruct((s, D_MODEL), F32), jax.ShapeDtypeStruct((D_MIX, s), BF16)]
    operands = [proj, proj, proj, o_mla, o_swa, x, w_out, cw]
    if with_loss:
        in_specs.append(row(D_MODEL))
        out_specs.append(pl.BlockSpec((8, LANES), lambda i: (0, 0)))
        out_shape.append(jax.ShapeDtypeStruct((8, LANES), F32))
        operands.append(target)
    return pl.pallas_call(
        body, name="out_fwd_loss" if with_loss else "out_fwd", grid=(nt,),
        in_specs=in_specs, out_specs=out_specs, out_shape=out_shape,
        scratch_shapes=[pltpu.VMEM((tm, D_MIX), BF16)],
        compiler_params=_params(("arbitrary",) if with_loss else ("parallel",)),
    )(*operands)


def _out_bwd(dy, proj, o_mla, o_swa, w_out, cw):
    s = proj.shape[0]
    tm = min(512, s)
    nt = s // tm
    hb = tm // HALO

    def body(dy_ref, dyn_ref, conv_ref, convp_ref, convn_ref, gates_ref, gatesn_ref, om_ref, os_ref,
             w_ref, cw_ref,
             dconv_ref, dgates_ref, dom_ref, domt_ref, delta_ref, dos_ref, dcw_ref):
        i = pl.program_id(0)
        dz = _mm_nt(dy_ref[...], w_ref[...])

        def gate(n):
            g = gates_ref[:, GROUP * n:GROUP * (n + 1)]
            sg = _sigmoid(g)
            return g * sg, sg * (1.0 + g * (1.0 - sg))

        for n, o_ref, do_ref in ((0, om_ref, dom_ref), (2, os_ref, dos_ref)):
            silu, dsilu = gate(n)
            dzn = dz[:, GROUP * n:GROUP * (n + 1)]
            o = o_ref[...]
            do = dzn * silu
            do_ref[...] = do.astype(do_ref.dtype)
            dgates_ref[:, GROUP * n:GROUP * (n + 1)] = (dzn * o * dsilu).astype(BF16)
            if n == 0:
                domt_ref[...] = jnp.transpose(do).astype(BF16)
                t = do * o
                for h in range(MLA_HEADS):
                    delta_ref[:, h:h + 1] = jnp.sum(t[:, MLA_V * h:MLA_V * (h + 1)], axis=-1,
                                                    keepdims=True)

        c_h, c_b, c_c, u, u1, u2, yc = _conv_fwd_vals(conv_ref, convp_ref, cw_ref, i == 0)
        silu, dsilu = gate(1)
        dzc = dz[:, GROUP:2 * GROUP]
        dgates_ref[:, GROUP:2 * GROUP] = (dzc * (c_b * yc) * dsilu).astype(BF16)
        dycr = dzc * silu
        dyc = dycr * c_b
        gn = gatesn_ref[:, GROUP:2 * GROUP]
        dzc_n = _mm_nt(dyn_ref[...], w_ref[GROUP:2 * GROUP, :])
        dyc_n = jnp.where(i == nt - 1, 0.0, dzc_n * (gn * _sigmoid(gn)) * convn_ref[:, 512:1024])
        d1 = _shift_up(dyc, dyc_n, 1)
        d2 = _shift_up(dyc, dyc_n, 2)
        du = cw_ref[2:3, :] * dyc + cw_ref[1:2, :] * d1 + cw_ref[0:1, :] * d2
        dconv_ref[:, 0:512] = (du * c_c).astype(BF16)
        dconv_ref[:, 512:1024] = (dycr * yc).astype(BF16)
        dconv_ref[:, 1024:1536] = (du * c_h).astype(BF16)

        @pl.when(i == 0)
        def _():
            dcw_ref[...] = jnp.zeros_like(dcw_ref)

        for k, uk in enumerate((u2, u1, u)):
            dcw_ref[k:k + 1, :] += jnp.sum(dyc * uk, axis=0, keepdims=True)

    row = lambda width: pl.BlockSpec((tm, width), lambda i: (i, 0))
    prev = lambda i: jnp.maximum(i * hb - 1, 0)
    nxt = lambda i: jnp.minimum((i + 1) * hb, s // HALO - 1)
    return pl.pallas_call(
        body, name="out_bwd", grid=(nt,),
        in_specs=[row(D_MODEL),
                  pl.BlockSpec((HALO, D_MODEL), lambda i: (nxt(i), 0)),
                  pl.BlockSpec((tm, 1536), lambda i: (i, 0)),
                  pl.BlockSpec((HALO, 1536), lambda i: (prev(i), 0)),
                  pl.BlockSpec((HALO, 1536), lambda i: (nxt(i), 0)),
                  pl.BlockSpec((tm, 1536), lambda i: (i, 1)),
                  pl.BlockSpec((HALO, 1536), lambda i: (nxt(i), 1)),
                  row(512), row(512),
                  pl.BlockSpec((D_MIX, D_MODEL), lambda i: (0, 0)),
                  pl.BlockSpec((HALO, 512), lambda i: (0, 0))],
        out_specs=[row(1536), row(1536), row(512), pl.BlockSpec((512, tm), lambda i: (0, i)),
                   row(MLA_HEADS), row(512), pl.BlockSpec((HALO, 512), lambda i: (0, 0))],
        out_shape=[jax.ShapeDtypeStruct((s, 1536), BF16), jax.ShapeDtypeStruct((s, 1536), BF16),
                   jax.ShapeDtypeStruct((s, 512), BF16), jax.ShapeDtypeStruct((512, s), BF16),
                   jax.ShapeDtypeStruct((s, MLA_HEADS), F32),
                   jax.ShapeDtypeStruct((s, 512), BF16), jax.ShapeDtypeStruct((HALO, 512), F32)],
        compiler_params=_params(("arbitrary",)),
    )(dy, dy, proj, proj, proj, proj, proj, o_mla, o_swa, w_out, cw)


def _adam_update(g, w, m, v):
    c1 = 1.0 - ADAM_B1
    c2 = 1.0 - ADAM_B2
    bc1 = 1.0 - ADAM_B1 ** ADAM_STEP
    bc2 = 1.0 - ADAM_B2 ** ADAM_STEP
    m_new = ADAM_B1 * m + c1 * g
    v_new = ADAM_B2 * v + c2 * (g * g)
    delta = -ADAM_LR * ((m_new / bc1) / (jnp.sqrt(v_new / bc2) + ADAM_EPS) + ADAM_WD * w)
    return delta, m_new, v_new


def _adamw(g, w, m, v):
    rows = g.shape[0]
    tr = min(256, rows)

    def body(g_ref, w_ref, m_ref, v_ref, d_ref, mo_ref, vo_ref):
        d_ref[...], mo_ref[...], vo_ref[...] = _adam_update(g_ref[...], w_ref[...], m_ref[...],
                                                            v_ref[...])

    spec = pl.BlockSpec((tr, g.shape[1]), lambda i: (i, 0))
    return pl.pallas_call(
        body, name="adamw", grid=(rows // tr,),
        in_specs=[spec] * 4, out_specs=[spec] * 3,
        out_shape=[jax.ShapeDtypeStruct(g.shape, F32)] * 3,
        compiler_params=_params(("parallel",)),
    )(g, w, m, v)


def _adamw_small(gs, ws, ms, vs):
    n = len(gs)

    def body(*refs):
        ins, outs = refs[:4 * n], refs[4 * n:]
        for a in range(n):
            res = _adam_update(*(ins[kind * n + a][...] for kind in range(4)))
            for kind in range(3):
                outs[kind * n + a][...] = res[kind]

    vmem = pl.BlockSpec(memory_space=pltpu.VMEM)
    out = pl.pallas_call(
        body, name="adamw_small",
        in_specs=[vmem] * (4 * n), out_specs=[vmem] * (3 * n),
        out_shape=[jax.ShapeDtypeStruct(g.shape, F32) for _ in range(3) for g in gs],
    )(*gs, *ws, *ms, *vs)
    return out[:n], out[n:2 * n], out[2 * n:]


HBM_SPEC = pl.BlockSpec(memory_space=pltpu.HBM)


def _place():
    x, y, c = lax.axis_index("x"), lax.axis_index("y"), lax.axis_index("c")
    chips = [(1 - x, y), (x, 1 - y), (1 - x, 1 - y)]
    return x, y, c, chips


def _all_gather(shards):
    na = len(shards)
    halves = [sh.shape[0] // 2 for sh in shards]

    def body(*refs):
        w_refs, a_refs = refs[:na], refs[na:2 * na]
        send_sems, recv_sems = refs[2 * na:]
        x, y, c, chips = _place()
        k = 2 * x + y
        sib = (x, y, 1 - c)

        def slab(a, kk, hc):
            return a_refs[a].at[kk, pl.ds(hc * halves[a], halves[a]), :]

        def copy(a, n, src, dst, to):
            return pltpu.make_async_remote_copy(
                src_ref=src, dst_ref=dst, send_sem=send_sems.at[6 * a + n],
                recv_sem=recv_sems.at[6 * a + n], device_id=to, device_id_type=MESH)

        first = [copy(a, n, w_refs[a].at[pl.ds(c * halves[a], halves[a]), :], slab(a, k, c),
                      (cx, cy, c))
                 for n, (cx, cy) in enumerate(chips) for a in range(na)]
        for cp in first:
            cp.start()
        passed = []
        for n, (cx, cy) in enumerate(chips):
            kk = 2 * cx + cy
            for a in range(na):
                copy(a, n, slab(a, kk, c), slab(a, kk, c), (cx, cy, c)).wait_recv()
                fwd = copy(a, 3 + n, slab(a, kk, c), slab(a, kk, c), sib)
                fwd.start()
                passed.append(fwd)
        for n, (cx, cy) in enumerate(chips):
            kk = 2 * cx + cy
            for a in range(na):
                copy(a, 3 + n, slab(a, kk, 1 - c), slab(a, kk, 1 - c), sib).wait_recv()
        for cp in first + passed:
            cp.wait_send()

    return pl.pallas_call(
        body, name="weights_all_gather",
        in_specs=[HBM_SPEC] * na, out_specs=[HBM_SPEC] * na,
        out_shape=[jax.ShapeDtypeStruct((N_CHIPS,) + sh.shape, sh.dtype) for sh in shards],
        scratch_shapes=[pltpu.SemaphoreType.DMA((6 * na,)), pltpu.SemaphoreType.DMA((6 * na,))],
    )(*shards)


def _fill_own_slab(buf, src, k_idx):
    n, rows, cols = buf.shape
    tr = _row_tile(rows)
    slabs = src.ndim == 3

    def body(k_ref, src_ref, buf_ref, out_ref):
        out_ref[0] = src_ref[0] if slabs else src_ref[...]

    if slabs:
        src_spec = pl.BlockSpec((1, tr, cols), lambda t, k_ref: (k_ref[0], t, 0))
    else:
        src_spec = pl.BlockSpec((tr, cols), lambda t, k_ref: (t, 0))
    return pl.pallas_call(
        body, name="fill_own_slab",
        grid_spec=pltpu.PrefetchScalarGridSpec(
            num_scalar_prefetch=1, grid=(rows // tr,),
            in_specs=[src_spec, pl.BlockSpec(memory_space=pl.ANY)],
            out_specs=pl.BlockSpec((1, tr, cols), lambda t, k_ref: (k_ref[0], t, 0))),
        out_shape=jax.ShapeDtypeStruct(buf.shape, buf.dtype),
        input_output_aliases={2: 0},
        compiler_params=_params(("parallel",)),
    )(k_idx, src, buf)


SEM_SPEC = pl.BlockSpec(memory_space=pltpu.SEMAPHORE)
LATE_COPIES = 6


def _late_copy(a, j, peer_core, src, dst, send_sems, recv_sems, to, sender_core):
    return pltpu.make_async_remote_copy(
        src_ref=src, dst_ref=dst, send_sem=send_sems.at[LATE_COPIES * a + 2 * j + peer_core],
        recv_sem=recv_sems.at[LATE_COPIES * a + 2 * j + sender_core], device_id=to,
        device_id_type=MESH)


def _gather_start(shards):
    na = len(shards)

    def body(*refs):
        w_refs, land_refs = refs[:na], refs[na:2 * na]
        send_sems, recv_sems = refs[2 * na], refs[2 * na + 1]
        token = refs[-1]
        x, y, c, chips = _place()
        k = 2 * x + y
        for a in range(na):
            half = w_refs[a].shape[0] // 2
            src = w_refs[a].at[pl.ds(c * half, half), :]
            dst = land_refs[a].at[k, pl.ds(c * half, half), :]
            for j, (cx, cy) in enumerate(chips):
                for tc in range(2):
                    _late_copy(a, j, tc, src, dst, send_sems, recv_sems, (cx, cy, tc), c).start()
        token[...] = jnp.zeros_like(token)

    lands = [pltpu.with_memory_space_constraint(lax.empty((N_CHIPS,) + sh.shape, sh.dtype), pltpu.HBM)
             for sh in shards]
    srcs = [pltpu.with_memory_space_constraint(sh, pltpu.HBM) for sh in shards]
    sems = pltpu.SemaphoreType.DMA((LATE_COPIES * na,))
    aliases = {a: 2 + a for a in range(2 * na)}
    return pl.pallas_call(
        body, name="late_weights_gather_start",
        in_specs=[HBM_SPEC] * (2 * na),
        out_specs=[SEM_SPEC, SEM_SPEC] + [HBM_SPEC] * (2 * na) + [pl.BlockSpec(memory_space=pltpu.VMEM)],
        out_shape=[sems, sems] + [pltpu.HBM(v.shape, v.dtype) for v in srcs + lands]
        + [jax.ShapeDtypeStruct((8, LANES), F32)],
        input_output_aliases=aliases,
        compiler_params=pltpu.CompilerParams(
            has_side_effects=pltpu.SideEffectType.DATAFLOW_SIDE_EFFECTING),
    )(*srcs, *lands)


def _gather_wait(started, na, after):
    send_sems, recv_sems = started[0], started[1]
    bufs = started[2:2 + 2 * na]

    def body(*refs):
        w_refs, land_refs = refs[:na], refs[na:2 * na]
        send_sems, recv_sems = refs[2 * na], refs[2 * na + 1]
        x, y, c, chips = _place()
        k = 2 * x + y
        for a in range(na):
            half = w_refs[a].shape[0] // 2
            src = w_refs[a].at[pl.ds(c * half, half), :]
            for j, (cx, cy) in enumerate(chips):
                kk = 2 * cx + cy
                for pc in range(2):
                    _late_copy(a, j, pc, src, land_refs[a].at[k, pl.ds(c * half, half), :],
                               send_sems, recv_sems, (cx, cy, pc), c).wait_send()
                    pltpu.make_async_remote_copy(
                        src_ref=src, dst_ref=land_refs[a].at[kk, pl.ds(pc * half, half), :],
                        send_sem=send_sems.at[LATE_COPIES * a + 2 * j + pc],
                        recv_sem=recv_sems.at[LATE_COPIES * a + 2 * j + pc],
                        device_id=(cx, cy, pc), device_id_type=MESH).wait_recv()

    out = pl.pallas_call(
        body, name="late_weights_gather_wait",
        in_specs=[HBM_SPEC] * (2 * na) + [SEM_SPEC, SEM_SPEC, pl.BlockSpec(memory_space=pl.ANY)],
        out_specs=[HBM_SPEC] * (2 * na),
        out_shape=[pltpu.HBM(v.shape, v.dtype) for v in bufs],
        input_output_aliases={a: a for a in range(2 * na)},
        compiler_params=pltpu.CompilerParams(
            has_side_effects=pltpu.SideEffectType.DATAFLOW_SIDE_EFFECTING),
    )(*bufs, send_sems, recv_sems, after)
    return out[na:]


def _split_start(name, srcs, land_shapes, n_sems, plan):
    na = len(srcs)

    def body(*refs):
        sends, _ = plan(refs[:na], refs[na:2 * na], refs[2 * na], refs[2 * na + 1])
        for cp in sends:
            cp.start()
        refs[-1][...] = jnp.zeros_like(refs[-1])

    lands = [pltpu.with_memory_space_constraint(lax.empty(shape, dtype), pltpu.HBM)
             for shape, dtype in land_shapes]
    srcs = [pltpu.with_memory_space_constraint(v, pltpu.HBM) for v in srcs]
    sems = pltpu.SemaphoreType.DMA((n_sems,))
    return pl.pallas_call(
        body, name=name,
        in_specs=[HBM_SPEC] * (2 * na),
        out_specs=[SEM_SPEC, SEM_SPEC] + [HBM_SPEC] * (2 * na) + [pl.BlockSpec(memory_space=pltpu.VMEM)],
        out_shape=[sems, sems] + [pltpu.HBM(v.shape, v.dtype) for v in srcs + lands]
        + [jax.ShapeDtypeStruct((8, LANES), F32)],
        input_output_aliases={a: 2 + a for a in range(2 * na)},
        compiler_params=pltpu.CompilerParams(
            has_side_effects=pltpu.SideEffectType.DATAFLOW_SIDE_EFFECTING),
    )(*srcs, *lands)


def _split_wait(name, started, na, plan, after):
    bufs = started[2:2 + 2 * na]

    def body(*refs):
        sends, recvs = plan(refs[:na], refs[na:2 * na], refs[2 * na], refs[2 * na + 1])
        for cp in sends:
            cp.wait_send()
        for cp in recvs:
            cp.wait_recv()

    out = pl.pallas_call(
        body, name=name,
        in_specs=[HBM_SPEC] * (2 * na) + [SEM_SPEC, SEM_SPEC, pl.BlockSpec(memory_space=pl.ANY)],
        out_specs=[HBM_SPEC] * (2 * na),
        out_shape=[pltpu.HBM(v.shape, v.dtype) for v in bufs],
        input_output_aliases={a: a for a in range(2 * na)},
        compiler_params=pltpu.CompilerParams(
            has_side_effects=pltpu.SideEffectType.DATAFLOW_SIDE_EFFECTING),
    )(*bufs, started[0], started[1], after)
    return out[:na], out[na:]


def _plan_to_sibling(g_refs, r_refs, send_sems, recv_sems):
    x, y, c, _ = _place()
    cps = []
    for a, (g, r) in enumerate(zip(g_refs, r_refs)):
        half = g.shape[1] // 2
        cps.append(pltpu.make_async_remote_copy(
            src_ref=g.at[:, pl.ds((1 - c) * half, half), :], dst_ref=r, send_sem=send_sems.at[a],
            recv_sem=recv_sems.at[a], device_id=(x, y, 1 - c), device_id_type=MESH))
    return cps, cps


def _plan_scatter(p_refs, q_refs, send_sems, recv_sems):
    x, y, c, chips = _place()
    k = 2 * x + y
    sends, recvs = [], []
    for a, (p, q) in enumerate(zip(p_refs, q_refs)):
        for i, (cx, cy) in enumerate(chips):
            kk = 2 * cx + cy
            for dst, out in ((q.at[k], sends), (q.at[kk], recvs)):
                out.append(pltpu.make_async_remote_copy(
                    src_ref=p.at[kk], dst_ref=dst, send_sem=send_sems.at[3 * a + i],
                    recv_sem=recv_sems.at[3 * a + i], device_id=(cx, cy, c), device_id_type=MESH))
    return sends, recvs


def _swap_halves_to_sibling(gs):
    na = len(gs)

    def body(*refs):
        g_refs, r_refs = refs[:na], refs[na:2 * na]
        send_sems, recv_sems = refs[2 * na:]
        x, y, c, _ = _place()
        cps = []
        for a in range(na):
            half = g_refs[a].shape[1] // 2
            cps.append(pltpu.make_async_remote_copy(
                src_ref=g_refs[a].at[:, pl.ds((1 - c) * half, half), :], dst_ref=r_refs[a],
                send_sem=send_sems.at[a], recv_sem=recv_sems.at[a], device_id=(x, y, 1 - c),
                device_id_type=MESH))
        for cp in cps:
            cp.start()
        for cp in cps:
            cp.wait()

    return pl.pallas_call(
        body, name="grads_to_sibling",
        in_specs=[HBM_SPEC] * na, out_specs=[HBM_SPEC] * na,
        out_shape=[jax.ShapeDtypeStruct((g.shape[0], g.shape[1] // 2, g.shape[2]), g.dtype)
                   for g in gs],
        scratch_shapes=[pltpu.SemaphoreType.DMA((na,)), pltpu.SemaphoreType.DMA((na,))],
    )(*gs)


def _row_tile(rows):
    return next(t for t in (256, 128, 64) if rows % t == 0)


def _add_sibling(g, r, c_idx, out_dtype):
    n, rows, cols = g.shape
    half = rows // 2
    tr = _row_tile(half)
    nb = half // tr

    def body(c_ref, g_ref, r_ref, p_ref):
        p_ref[...] = (g_ref[...] + r_ref[...]).astype(out_dtype)

    return pl.pallas_call(
        body, name="grads_add_sibling",
        grid_spec=pltpu.PrefetchScalarGridSpec(
            num_scalar_prefetch=1, grid=(n, nb),
            in_specs=[pl.BlockSpec((1, tr, cols), lambda j, t, c_ref: (j, c_ref[0] * nb + t, 0)),
                      pl.BlockSpec((1, tr, cols), lambda j, t, c_ref: (j, t, 0))],
            out_specs=pl.BlockSpec((1, tr, cols), lambda j, t, c_ref: (j, t, 0))),
        out_shape=jax.ShapeDtypeStruct((n, half, cols), out_dtype),
        compiler_params=_params(("parallel", "parallel")),
    )(c_idx, g, r)


def _scatter_to_chips(ps):
    na = len(ps)

    def body(*refs):
        p_refs, q_refs = refs[:na], refs[na:2 * na]
        send_sems, recv_sems = refs[2 * na:]
        x, y, c, chips = _place()
        k = 2 * x + y
        sends = []
        for i, (cx, cy) in enumerate(chips):
            for a in range(na):
                cp = pltpu.make_async_remote_copy(
                    src_ref=p_refs[a].at[2 * cx + cy], dst_ref=q_refs[a].at[k],
                    send_sem=send_sems.at[3 * a + i], recv_sem=recv_sems.at[3 * a + i],
                    device_id=(cx, cy, c), device_id_type=MESH)
                cp.start()
                sends.append(cp)
        for i, (cx, cy) in enumerate(chips):
            kk = 2 * cx + cy
            for a in range(na):
                pltpu.make_async_remote_copy(
                    src_ref=p_refs[a].at[kk], dst_ref=q_refs[a].at[kk],
                    send_sem=send_sems.at[3 * a + i], recv_sem=recv_sems.at[3 * a + i],
                    device_id=(cx, cy, c), device_id_type=MESH).wait_recv()
        for cp in sends:
            cp.wait_send()

    return pl.pallas_call(
        body, name="grads_scatter_to_chips",
        in_specs=[HBM_SPEC] * na, out_specs=[HBM_SPEC] * na,
        out_shape=[jax.ShapeDtypeStruct(p.shape, p.dtype) for p in ps],
        scratch_shapes=[pltpu.SemaphoreType.DMA((3 * na,)), pltpu.SemaphoreType.DMA((3 * na,))],
    )(*ps)


def _sum_chips(q, c_idx):
    n, half, cols = q.shape
    tr = _row_tile(half)
    nb = half // tr

    def body(c_ref, q_ref, o_ref):
        parts = [q_ref[kk].astype(F32) for kk in range(n)]
        o_ref[...] = ((parts[0] + parts[1]) + parts[2]) + parts[3]

    return pl.pallas_call(
        body, name="grads_sum_chips",
        grid_spec=pltpu.PrefetchScalarGridSpec(
            num_scalar_prefetch=1, grid=(nb,),
            in_specs=[pl.BlockSpec((n, tr, cols), lambda t, c_ref: (0, t, 0))],
            out_specs=pl.BlockSpec((tr, cols), lambda t, c_ref: (c_ref[0] * nb + t, 0))),
        out_shape=jax.ShapeDtypeStruct((2 * half, cols), F32),
        compiler_params=_params(("parallel",)),
    )(c_idx, q)


def _join_halves(fulls):
    na = len(fulls)

    def body(*refs):
        o_refs = refs[na:2 * na]
        send_sems, recv_sems = refs[2 * na:]
        x, y, c, _ = _place()
        sends = []
        for a in range(na):
            half = o_refs[a].shape[0] // 2
            rows = o_refs[a].at[pl.ds(c * half, half), :]
            sends.append(pltpu.make_async_remote_copy(
                src_ref=rows, dst_ref=rows, send_sem=send_sems.at[a], recv_sem=recv_sems.at[a],
                device_id=(x, y, 1 - c), device_id_type=MESH))
        for cp in sends:
            cp.start()
        for a in range(na):
            half = o_refs[a].shape[0] // 2
            other = o_refs[a].at[pl.ds((1 - c) * half, half), :]
            pltpu.make_async_remote_copy(
                src_ref=other, dst_ref=other, send_sem=send_sems.at[a], recv_sem=recv_sems.at[a],
                device_id=(x, y, 1 - c), device_id_type=MESH).wait_recv()
        for cp in sends:
            cp.wait_send()

    return pl.pallas_call(
        body, name="grads_join_halves",
        in_specs=[HBM_SPEC] * na, out_specs=[HBM_SPEC] * na,
        out_shape=[jax.ShapeDtypeStruct(f.shape, f.dtype) for f in fulls],
        input_output_aliases={a: a for a in range(na)},
        scratch_shapes=[pltpu.SemaphoreType.DMA((na,)), pltpu.SemaphoreType.DMA((na,))],
    )(*fulls)


def _part_rows(shape):
    size = 1
    for d in shape:
        size *= d
    rows = -(-size // PACK_COLS)
    return size, -(-rows // PACK_ROW_ALIGN) * PACK_ROW_ALIGN


def _pack_rows(arrays, dtype, total_rows):
    parts, used = [], 0
    for a in arrays:
        size, rows = _part_rows(a.shape)
        flat = a.reshape(-1).astype(dtype)
        parts.append(jnp.pad(flat, (0, rows * PACK_COLS - size)).reshape(rows, PACK_COLS))
        used += rows
    parts.append(jnp.zeros((total_rows - used, PACK_COLS), dtype))
    return jnp.concatenate(parts, axis=0)


def _unpack_rows(buf, shapes):
    lead = buf.shape[:-2]
    out, off = [], 0
    for sh in shapes:
        size, rows = _part_rows(sh)
        part = buf[..., off:off + rows, :].reshape(lead + (-1,))[..., :size]
        out.append(part.reshape(lead + tuple(sh)))
        off += rows
    return out


NEW_ORDER = ((928, 1440), (1440, 1952), (1952, 2464), (416, 928), (2464, 2976), (3744, 4256),
             (2976, 3488), (0, 256), (256, 384), (4256, 4320), (384, 416), (4256, 4288),
             (3488, 3616), (3616, 3744))
OLD_ORDER = ((3584, 3840), (3840, 3968), (4032, 4064), (1536, 2048), (0, 512), (512, 1024),
             (1024, 1536), (2048, 2560), (3072, 3584), (4096, 4224), (4224, 4352), (2560, 3072))


def _cols(sources, ranges):
    parts = []
    for a, b in ranges:
        off = 0
        for src in sources:
            width = src.shape[-1]
            lo, hi = max(a, off), min(b, off + width)
            if lo < hi:
                parts.append(src[..., lo - off:hi - off])
            off += width
    return jnp.concatenate(parts, axis=-1)


def _sub_ranges(ranges, a, b):
    out, off = [], 0
    for lo, hi in ranges:
        width = hi - lo
        s0, s1 = max(a, off), min(b, off + width)
        if s0 < s1:
            out.append((lo + s0 - off, lo + s1 - off))
        off += width
    return out


def _rope_tables(s):
    half = MLA_ROPE // 2
    inv_freq = jnp.power(jnp.float32(ROPE_THETA), -jnp.arange(half, dtype=F32) / half)
    ang = inv_freq[:, None] * jnp.arange(s, dtype=F32)[None, :]
    cos, sin = jnp.cos(ang), jnp.sin(ang)
    z = lambda n: jnp.zeros((n, s), F32)
    c = jnp.concatenate([jnp.ones((MLA_NOPE, s), F32), cos, cos, z(32)], axis=0)
    s1 = jnp.concatenate([z(MLA_NOPE), -sin, z(16), z(32)], axis=0)
    s2 = jnp.concatenate([z(MLA_NOPE), z(16), sin, z(32)], axis=0)
    return c, s1, s2


def _pad_lanes(a, n):
    return jnp.pad(a, ((0, 0), (0, n - a.shape[1])))


SHARDED = ("w_in", "w_out", "mla_w_qb", "mla_w_kvb", "conv_w")
REPLICATED = ("norm_g", "mla_q_a_norm", "mla_kv_a_norm", "mla_q_norm", "mla_k_norm",
              "swa_q_norm", "swa_k_norm", "swa_sinks")
WEIGHT_ORDER = ("norm_g", "w_in", "mla_q_a_norm", "mla_w_qb", "mla_kv_a_norm", "mla_w_kvb",
                "mla_q_norm", "mla_k_norm", "conv_w", "swa_q_norm", "swa_k_norm", "swa_sinks", "w_out")
SHARD_AXIS = {"w_in": 2, "w_out": 1, "mla_w_qb": 2, "mla_w_kvb": 2, "conv_w": 2}


def kernel(x, norm_g, w_in, mla_q_a_norm, mla_w_qb, mla_kv_a_norm, mla_w_kvb, mla_q_norm, mla_k_norm, conv_w, swa_q_norm, swa_k_norm, swa_sinks, w_out, loss_target, m_norm_g, m_w_in, m_mla_q_a_norm, m_mla_w_qb, m_mla_kv_a_norm, m_mla_w_kvb, m_mla_q_norm, m_mla_k_norm, m_conv_w, m_swa_q_norm, m_swa_k_norm, m_swa_sinks, m_w_out, v_norm_g, v_w_in, v_mla_q_a_norm, v_mla_w_qb, v_mla_kv_a_norm, v_mla_w_kvb, v_mla_q_norm, v_mla_k_norm, v_conv_w, v_swa_q_norm, v_swa_k_norm, v_swa_sinks, v_w_out):
    weights = dict(norm_g=norm_g, w_in=w_in, mla_q_a_norm=mla_q_a_norm, mla_w_qb=mla_w_qb,
                   mla_kv_a_norm=mla_kv_a_norm, mla_w_kvb=mla_w_kvb, mla_q_norm=mla_q_norm,
                   mla_k_norm=mla_k_norm, conv_w=conv_w, swa_q_norm=swa_q_norm,
                   swa_k_norm=swa_k_norm, swa_sinks=swa_sinks, w_out=w_out)
    mom_m = dict(norm_g=m_norm_g, w_in=m_w_in, mla_q_a_norm=m_mla_q_a_norm, mla_w_qb=m_mla_w_qb,
                 mla_kv_a_norm=m_mla_kv_a_norm, mla_w_kvb=m_mla_w_kvb, mla_q_norm=m_mla_q_norm,
                 mla_k_norm=m_mla_k_norm, conv_w=m_conv_w, swa_q_norm=m_swa_q_norm,
                 swa_k_norm=m_swa_k_norm, swa_sinks=m_swa_sinks, w_out=m_w_out)
    mom_v = dict(norm_g=v_norm_g, w_in=v_w_in, mla_q_a_norm=v_mla_q_a_norm, mla_w_qb=v_mla_w_qb,
                 mla_kv_a_norm=v_mla_kv_a_norm, mla_w_kvb=v_mla_w_kvb, mla_q_norm=v_mla_q_norm,
                 mla_k_norm=v_mla_k_norm, conv_w=v_conv_w, swa_q_norm=v_swa_q_norm,
                 swa_k_norm=v_swa_k_norm, swa_sinks=v_swa_sinks, w_out=v_w_out)
    xs = x[0]
    target = loss_target[0]
    s = xs.shape[0]
    c_idx = lax.axis_index("c").astype(jnp.int32).reshape(1)
    k_idx = (2 * lax.axis_index("x") + lax.axis_index("y")).astype(jnp.int32).reshape(1)

    conv_bits = lax.bitcast_convert_type(conv_w, BF16)
    small_list = [mla_w_qb, mla_w_kvb, conv_bits]
    shard_cols = w_in.shape[2]
    w_in_b = w_in.astype(BF16)
    late = [w_in_b[1], w_out.astype(BF16).reshape(-1, D_MODEL)]
    started = _gather_start(late)
    own = [w_in_b[0], _pack_rows(small_list, BF16, SMALL_ROWS)]
    gathered_in0, gathered_small = [_fill_own_slab(buf, src, k_idx)
                                    for buf, src in zip(_all_gather(own), own)]
    parts = _unpack_rows(gathered_small, [a.shape for a in small_list])
    join = lambda p, axis: jnp.concatenate([p[k] for k in range(N_CHIPS)], axis=axis)
    w_in_zeros = jnp.zeros((D_MODEL, 64), BF16)
    permuted = lambda slabs: _cols([slabs[k] for k in range(N_CHIPS)] + [w_in_zeros], NEW_ORDER)
    w_qb_full = join(parts[0], 2)
    w_kvb_full = join(parts[1], 2)
    conv_full = lax.bitcast_convert_type(join(parts[2], 2), F32)

    rope = _rope_tables(s)
    swa_tables = _swa_tables()
    layers = []
    for l in range(DEPTH):
        wq = jnp.pad(w_qb_full[l].reshape(MLA_Q_LORA, MLA_HEADS, MLA_QK),
                     ((0, 0), (0, 0), (0, LANES - MLA_QK))).reshape(MLA_Q_LORA, MLA_HEADS * LANES)
        kv = w_kvb_full[l].reshape(MLA_KV_LORA, MLA_HEADS, MLA_NOPE + MLA_V)
        wk = jnp.pad(kv[:, :, :MLA_NOPE], ((0, 0), (0, 0), (0, LANES - MLA_NOPE)))
        wkv = jnp.concatenate([wk.reshape(MLA_KV_LORA, MLA_HEADS * LANES),
                               kv[:, :, MLA_NOPE:].reshape(MLA_KV_LORA, MLA_HEADS * MLA_V)], axis=1)
        layers.append(dict(
            wq=wq, wkv=wkv,
            cw=jnp.pad(conv_full[l], ((0, HALO - 3), (0, 0))),
            g=norm_g[l][None],
            mla_norms=(mla_q_a_norm[l][:, None], mla_kv_a_norm[l][:, None],
                       _pad_lanes(mla_q_norm[l][None], LANES).T, _pad_lanes(mla_k_norm[l][None], LANES).T),
            mla_weights=(wq, wkv, wq.T, wkv.T),
            sqn=swa_q_norm[l][:, None], skn=swa_k_norm[l][:, None], sinks=swa_sinks[l][None]))

    saved = []
    h_in = xs
    layers[0]["w_in"] = permuted(gathered_in0)
    layers[0]["g"] = layers[0]["g"] + started[-1][0:1, 0:1]
    for l in range(DEPTH):
        p = layers[l]
        proj, hb = _in_proj_fwd(h_in, p["g"], p["w_in"])
        q, k, v, qt, kt, vt = _mla_prep_fwd(proj, p["mla_norms"], p["mla_weights"], rope)
        o_mla, lse = _mla_attn_fwd(q, k, vt)
        if l == 0:
            late_in1, late_out = [_fill_own_slab(buf, src, k_idx) for buf, src in
                                  zip(_gather_wait(started, len(late), o_mla), late)]
            layers[1]["w_in"] = permuted(late_in1)
            w_out_full = join(late_out.reshape(N_CHIPS, DEPTH, -1, D_MODEL), 1)
            for n in range(DEPTH):
                layers[n]["w_out"] = w_out_full[n]
        o_swa = _swa_fwd(proj, p["sqn"], p["skn"], p["sinks"], swa_tables)
        last = l == DEPTH - 1
        y, z, *loss_acc = _out_fwd(proj, o_mla, o_swa, h_in, p["w_out"], p["cw"],
                                   target if last else None)
        saved.append(dict(x=h_in, proj=proj, hb=hb, q=q, k=k, v=v, qt=qt, kt=kt, o_mla=o_mla, lse=lse,
                          o_swa=o_swa, z=z))
        h_in = y

    dy, loss_acc = h_in, loss_acc[0]
    loss = lax.psum(loss_acc[0, 0], ("x", "y", "c"))

    grads = {n: [None] * DEPTH for n in WEIGHT_ORDER}

    def in_chunks(l):
        return jnp.stack([_cols(grads["w_in"][l], _sub_ranges(OLD_ORDER, k * shard_cols,
                                                              (k + 1) * shard_cols))
                          for k in range(N_CHIPS)])

    def out_chunks(l):
        return grads["w_out"][l].reshape(N_CHIPS, -1, D_MODEL)

    for l in reversed(range(DEPTH)):
        p, a = layers[l], saved[l]
        dconv, dgates, do_mla, do_mla_t, delta, do_swa, dcw = _out_bwd(dy, a["proj"], a["o_mla"], a["o_swa"],
                                                             p["w_out"], p["cw"])
        if l == 0:
            late_gs, from_sib = _split_wait("late_grads_to_sibling_wait", late_st1, 2,
                                            _plan_to_sibling, dconv)
            late_p = [_add_sibling(g, r, c_idx, BF16) for g, r in zip(late_gs, from_sib)]
            late_st2 = _split_start("late_grads_scatter_start", late_p,
                                    [(v.shape, v.dtype) for v in late_p], 6, _plan_scatter)
        grads["w_out"][l] = _weight_grads(a["z"], [dy], "dw_out")[0]
        grads["conv_w"][l] = dcw[0:3]
        delta_rows = jnp.transpose(delta, (1, 0)).reshape(MLA_HEADS // 2, 2, s)
        if l == 0:
            delta_rows = delta_rows + late_st2[-1][0, 0]
        dq, dk, dv = _mla_attn_bwd(a["q"], a["qt"], a["k"], a["kt"], a["v"], do_mla, do_mla_t,
                                   a["lse"], delta_rows)
        if l == 0:
            late_p, late_q = _split_wait("late_grads_scatter_wait", late_st2, 2, _plan_scatter, dq)
            late_full = [_sum_chips(_fill_own_slab(q_, p_, k_idx), c_idx)
                         for q_, p_ in zip(late_q, late_p)]
        dmla, dqan, dkvan, dqn, dkn, dwq_t, dwkv_t = _mla_prep_bwd(
            a["proj"], p["mla_norms"], p["mla_weights"], rope, dq, dk, dv)
        dwq, dwkv = dwq_t.T, dwkv_t.T
        dsq, dskv, dsqn, dskn, dsinks = _swa_bwd(a["proj"], p["sqn"], p["skn"], p["sinks"], swa_tables, do_swa)
        pieces = [dconv, dgates, dsq, dmla, dskv]
        grads["w_in"][l] = _weight_grads(a["hb"], pieces, "dw_in")
        gain = p["g"]
        if l == 0:
            gs0 = (in_chunks(0), out_chunks(0))
            last_p = [_add_sibling(g, r, c_idx, BF16)
                      for g, r in zip(gs0, _swap_halves_to_sibling(gs0))]
            last_st = _split_start("last_grads_scatter_start", last_p,
                                   [(v.shape, v.dtype) for v in last_p], 6, _plan_scatter)
            gain = gain + last_st[-1][0:1, 0:1]
        dx, dg = _in_proj_bwd(pieces, a["x"], gain, p["w_in"], dy)
        if l == 0:
            last_p, last_q = _split_wait("last_grads_scatter_wait", last_st, 2, _plan_scatter, dx)
            last_full = [_sum_chips(_fill_own_slab(q_, p_, k_idx), c_idx)
                         for q_, p_ in zip(last_q, last_p)]
        grads["norm_g"][l] = dg[0]
        grads["mla_q_a_norm"][l] = dqan[:, 0]
        grads["mla_kv_a_norm"][l] = dkvan[:, 0]
        grads["mla_q_norm"][l] = dqn[:MLA_QK, 0]
        grads["mla_k_norm"][l] = dkn[:MLA_QK, 0]
        grads["mla_w_qb"][l] = dwq.reshape(MLA_Q_LORA, MLA_HEADS, LANES)[:, :, :MLA_QK].reshape(
            MLA_Q_LORA, MLA_HEADS * MLA_QK)
        dwk = dwkv[:, :MLA_HEADS * LANES].reshape(MLA_KV_LORA, MLA_HEADS, LANES)[:, :, :MLA_NOPE]
        dwv = dwkv[:, MLA_HEADS * LANES:].reshape(MLA_KV_LORA, MLA_HEADS, MLA_V)
        grads["mla_w_kvb"][l] = jnp.concatenate([dwk, dwv], axis=2).reshape(
            MLA_KV_LORA, MLA_HEADS * (MLA_NOPE + MLA_V))
        grads["swa_q_norm"][l] = dsqn[:, 0]
        grads["swa_k_norm"][l] = dskn[:, 0]
        grads["swa_sinks"][l] = dsinks[0]
        dy = dx
        if l == DEPTH - 1:
            late_st1 = _split_start(
                "late_grads_to_sibling_start", [in_chunks(l), out_chunks(l)],
                [((N_CHIPS, D_MODEL // 2, shard_cols), F32),
                 ((N_CHIPS, D_MIX // N_CHIPS // 2, D_MODEL), F32)], 2, _plan_to_sibling)
    grad_x = dy[None]
    full_grads = {n: jnp.stack(grads[n]) for n in WEIGHT_ORDER if n not in ("w_in", "w_out")}

    rest = tuple(n for n in SHARDED if n not in ("w_in", "w_out"))
    rep_shapes = [weights[n].shape for n in REPLICATED]
    rep_grads = jnp.concatenate([full_grads[n].reshape(-1) for n in REPLICATED])

    def chunk(g, n, k):
        width = g.shape[SHARD_AXIS[n]] // N_CHIPS
        return lax.slice_in_dim(g, k * width, (k + 1) * width, axis=SHARD_AXIS[n])

    g_small = jnp.stack([_pack_rows([chunk(full_grads[n], n, k) for n in rest] + [rep_grads],
                                    F32, SMALL_ROWS) for k in range(N_CHIPS)])
    gs = (g_small,)
    partial = [_add_sibling(g, r, c_idx, F32) for g, r in zip(gs, _swap_halves_to_sibling(gs))]
    by_chip = [_fill_own_slab(q, p, k_idx) for q, p in zip(_scatter_to_chips(partial), partial)]
    g_small_mine, g_in0, g_out0, g_in1, g_out1 = _join_halves(
        [_sum_chips(q, c_idx) for q in by_chip] + last_full + late_full)

    vals = _unpack_rows(g_small_mine, [weights[n].shape for n in rest] + [(rep_grads.shape[0],)])
    grad = dict(zip(rest, vals[:-1]))
    grad["w_in"] = jnp.stack([g_in0, g_in1])
    grad["w_out"] = jnp.stack([g_out0, g_out1])
    off = 0
    for n, sh in zip(REPLICATED, rep_shapes):
        grad[n] = vals[-1][off:off + sh[0] * sh[1]].reshape(sh)
        off += sh[0] * sh[1]
    results = {}
    for n in ("w_in", "w_out"):
        view = lambda a, n=n: a.reshape(-1, weights[n].shape[-1])
        res = _adamw(view(grad[n]), view(weights[n]), view(mom_m[n]), view(mom_v[n]))
        results[n] = [r.reshape(weights[n].shape) for r in res]
    small = tuple(n for n in WEIGHT_ORDER if n not in results)
    res = _adamw_small(*([d[n] for n in small] for d in (grad, weights, mom_m, mom_v)))
    for a, n in enumerate(small):
        results[n] = [res[kind][a] for kind in range(3)]
    unpacked = [grad] + [{n: results[n][kind] for n in WEIGHT_ORDER} for kind in range(3)]
    outs = [loss, grad_x]
    for group in unpacked:
        outs += [group[n] for n in WEIGHT_ORDER]
    return tuple(outs)
```

```python
import jax
import numpy as np
import jax.numpy as jnp
from jax import lax
from jax.experimental import pallas as pl
from jax.experimental.pallas import tpu as pltpu

F32 = jnp.float32
BF16 = jnp.bfloat16

D_MODEL = 1024
DEPTH = 2
GROUP = 512
D_MIX = 3 * GROUP
BLOCK = 128
RMS_EPS = 1e-6
NEG_INF = -1e30
MLA_HEADS = 8
MLA_QK = 96
MLA_NOPE = 64
MLA_ROPE = 32
MLA_V = 64
V_AUG = 80
MLA_Q_LORA = 256
MLA_KV_LORA = 128
ROPE_THETA = 10000.0
SWA_HEADS = 8
SWA_KV = 2
SWA_GROUP = 4
SWA_DIM = 64
N_CHIPS = 4

NC = 4352
OFF_SQ, OFF_MLA, OFF_SKV = 3072, 3584, 4096

VMEM_LIMIT = 56 * 1024 * 1024
LANES = 128
PACK_COLS = 1024
PACK_ROW_ALIGN = 16
SMALL_ROWS = 256

ADAM_LR = 0.001
ADAM_B1 = 0.9
ADAM_B2 = 0.999
ADAM_EPS = 1e-08
ADAM_WD = 0.01
ADAM_STEP = 10

MESH = pl.DeviceIdType.MESH


def _params(sem, vmem=VMEM_LIMIT):
    return pltpu.CompilerParams(dimension_semantics=sem, vmem_limit_bytes=vmem)


def _dot(a, b, dims):
    return lax.dot_general(a.astype(BF16), b.astype(BF16), (dims, ((), ())),
                           preferred_element_type=F32)


def _mm(a, b):
    return _dot(a, b, ((1,), (0,)))


def _mm_nt(a, b):
    return _dot(a, b, ((1,), (1,)))


def _rms(x, g, n=None):
    n = x.shape[-1] if n is None else n
    ms = jnp.sum(x * x, axis=-1, keepdims=True) * (1.0 / n)
    return x * lax.rsqrt(ms + RMS_EPS) * g


def _sigmoid(x):
    return 1.0 / (1.0 + jnp.exp(-x))


def _in_proj_fwd(x, g, w):
    s = x.shape[0]
    tm = min(512, s)

    def body(x_ref, g_ref, w_ref, proj_ref, hbt_ref):
        h = _rms(x_ref[...], g_ref[...])
        hbt_ref[...] = jnp.transpose(h).astype(BF16)
        proj_ref[...] = jnp.dot(h.astype(BF16), w_ref[...], preferred_element_type=F32)

    return pl.pallas_call(
        body, name="in_proj_fwd", grid=(s // tm,),
        in_specs=[pl.BlockSpec((tm, D_MODEL), lambda i: (i, 0)),
                  pl.BlockSpec((1, D_MODEL), lambda i: (0, 0)),
                  pl.BlockSpec((D_MODEL, NC), lambda i: (0, 0))],
        out_specs=[pl.BlockSpec((tm, NC), lambda i: (i, 0)),
                   pl.BlockSpec((D_MODEL, tm), lambda i: (0, i))],
        out_shape=[jax.ShapeDtypeStruct((s, NC), F32), jax.ShapeDtypeStruct((D_MODEL, s), BF16)],
        compiler_params=_params(("parallel",)),
    )(x, g, w)


def _in_proj_bwd(pieces, x, g, w, dres):
    s = x.shape[0]
    tm = min(512, s)
    n_p = len(pieces)

    def body(*refs):
        p_refs = refs[:n_p]
        x_ref, g_ref, w_ref, dres_ref, dx_ref, dg_ref = refs[n_p:]
        dh = None
        off = 0
        for r in p_refs:
            width = r.shape[1]
            t = _mm_nt(r[...], w_ref[:, off:off + width])
            dh = t if dh is None else dh + t
            off += width
        _, vjp = jax.vjp(_rms, x_ref[...], g_ref[...])
        dx, dg = vjp(dh)
        dx_ref[...] = dx + dres_ref[...]

        @pl.when(pl.program_id(0) == 0)
        def _():
            dg_ref[...] = jnp.zeros_like(dg_ref)

        dg_ref[...] += dg

    in_specs = [pl.BlockSpec((tm, p.shape[1]), lambda i: (i, 0)) for p in pieces]
    in_specs += [pl.BlockSpec((tm, D_MODEL), lambda i: (i, 0)),
                 pl.BlockSpec((1, D_MODEL), lambda i: (0, 0)),
                 pl.BlockSpec((D_MODEL, NC), lambda i: (0, 0)),
                 pl.BlockSpec((tm, D_MODEL), lambda i: (i, 0))]
    return pl.pallas_call(
        body, name="in_proj_bwd", grid=(s // tm,),
        in_specs=in_specs,
        out_specs=[pl.BlockSpec((tm, D_MODEL), lambda i: (i, 0)),
                   pl.BlockSpec((1, D_MODEL), lambda i: (0, 0))],
        out_shape=[jax.ShapeDtypeStruct((s, D_MODEL), F32), jax.ShapeDtypeStruct((1, D_MODEL), F32)],
        compiler_params=_params(("arbitrary",)),
    )(*pieces, x, g, w, dres)


def _weight_grads(at, bs, name):
    m, s = at.shape
    nb = len(bs)
    tk = min(512, s)

    def body(a_ref, *refs):
        b_refs, o_refs = refs[:nb], refs[nb:]

        @pl.when(pl.program_id(0) == 0)
        def _():
            for o_ref in o_refs:
                o_ref[...] = jnp.zeros_like(o_ref)

        a = a_ref[...]
        for b_ref, o_ref in zip(b_refs, o_refs):
            o_ref[...] += _mm(a, b_ref[...])

    return pl.pallas_call(
        body, name=name, grid=(s // tk,),
        in_specs=[pl.BlockSpec((m, tk), lambda k: (0, k))]
        + [pl.BlockSpec((tk, b.shape[1]), lambda k: (k, 0)) for b in bs],
        out_specs=[pl.BlockSpec((m, b.shape[1]), lambda k: (0, 0)) for b in bs],
        out_shape=[jax.ShapeDtypeStruct((m, b.shape[1]), F32) for b in bs],
        compiler_params=_params(("arbitrary",)),
    )(at, *bs)


def _rms0(x, g, n=None):
    n = x.shape[0] if n is None else n
    ms = jnp.sum(x * x, axis=0, keepdims=True) * (1.0 / n)
    return x * lax.rsqrt(ms + RMS_EPS) * g


@jax.custom_vjp
def _rope0(t, c, s1, s2):
    return t * c + pltpu.roll(t, LANES - 16, 0) * s1 + pltpu.roll(t, 16, 0) * s2


def _rope0_fwd(t, c, s1, s2):
    return _rope0(t, c, s1, s2), (c, s1, s2)


def _rope0_bwd(res, g):
    c, s1, s2 = res
    dt = g * c + pltpu.roll(g * s1, 16, 0) + pltpu.roll(g * s2, LANES - 16, 0)
    return dt, jnp.zeros_like(c), jnp.zeros_like(s1), jnp.zeros_like(s2)


_rope0.defvjp(_rope0_fwd, _rope0_bwd)


@jax.custom_vjp
def _mmw(w, wt, x):
    return _mm(w, x)


def _mmw_fwd(w, wt, x):
    return _mm(w, x), (wt, x)


def _mmw_bwd(res, g):
    wt, x = res
    return _mm_nt(g, x), jnp.zeros_like(wt), _mm(wt, g)


_mmw.defvjp(_mmw_fwd, _mmw_bwd)


def _prep_fn(q_lat, kv_lat, kr, qan, kvan, qn, kn, wq, wk, wv, wqt, wkt, wvt, c, s1, s2, mm):
    tokens = q_lat.shape[1]
    rq = _rms0(q_lat, qan)
    rkv = _rms0(kv_lat, kvan)
    qn_b = jnp.broadcast_to(qn, (LANES, tokens))
    kn_b = jnp.broadcast_to(kn, (LANES, tokens))
    qs, ks = [], []
    for h in range(MLA_HEADS):
        qs.append(_rope0(_rms0(mm(wq[h], wqt[h], rq), qn_b, MLA_QK), c, s1, s2))
        ks.append(_rope0(_rms0(mm(wk[h], wkt[h], rkv) + kr, kn_b, MLA_QK), c, s1, s2))
    return tuple(qs), tuple(ks), mm(wv, wvt, rkv)


def _prep_weights(wq_ref, wkv_ref, wqt_ref, wkvt_ref):
    heads = range(MLA_HEADS)
    wq = tuple(wqt_ref[LANES * h:LANES * (h + 1), :].astype(F32) for h in heads)
    wk = tuple(wkvt_ref[LANES * h:LANES * (h + 1), :].astype(F32) for h in heads)
    wv = wkvt_ref[LANES * MLA_HEADS:, :].astype(F32)
    wqt = tuple(wq_ref[:, LANES * h:LANES * (h + 1)].astype(F32) for h in heads)
    wkt = tuple(wkv_ref[:, LANES * h:LANES * (h + 1)].astype(F32) for h in heads)
    wvt = wkv_ref[:, LANES * MLA_HEADS:].astype(F32)
    return wq, wk, wv, wqt, wkt, wvt


def _prep_in_specs(tm):
    const = lambda shape: pl.BlockSpec(shape, lambda i: (0, 0))
    col = lambda height: pl.BlockSpec((height, tm), lambda i: (0, i))
    return [pl.BlockSpec((tm, 512), lambda i: (i, OFF_MLA // 512)),
            const((MLA_Q_LORA, 1)), const((MLA_KV_LORA, 1)), const((LANES, 1)), const((LANES, 1)),
            const((MLA_Q_LORA, 1024)), const((MLA_KV_LORA, 1536)),
            const((1024, MLA_Q_LORA)), const((1536, MLA_KV_LORA)),
            col(LANES), col(LANES), col(LANES)]


def _prep_operands(blk_ref, refs):
    qan_ref, kvan_ref, qn_ref, kn_ref, wq_ref, wkv_ref, wqt_ref, wkvt_ref, c_ref, s1_ref, s2_ref = refs
    blk_t = jnp.transpose(blk_ref[...])
    diff = (blk_t[0:256], blk_t[256:384], blk_t[384:512],
            qan_ref[...], kvan_ref[...], qn_ref[...], kn_ref[...])
    weights = _prep_weights(wq_ref, wkv_ref, wqt_ref, wkvt_ref)
    return diff, weights, (c_ref[...], s1_ref[...], s2_ref[...])


def _mla_prep_fwd(proj, norms, weights, rope):
    s = proj.shape[0]
    tm = min(512, s)

    def body(blk_ref, *refs):
        ins, (q_ref, k_ref, v_ref, qt_ref, kt_ref, vt_ref) = refs[:11], refs[11:]
        diff, (wq, wk, wv, wqt, wkt, wvt), tables = _prep_operands(blk_ref, ins)
        qs, ks, v = _prep_fn(*diff, wq, wk, wv, wqt, wkt, wvt, *tables,
                             lambda w, wt, x: _mm(w, x))
        for h in range(MLA_HEADS):
            q2 = qs[h] * Q_PRESCALE
            qt_ref[LANES * h:LANES * (h + 1), :] = q2.astype(BF16)
            kt_ref[LANES * h:LANES * (h + 1), :] = ks[h].astype(BF16)
            q_ref[:, LANES * h:LANES * (h + 1)] = jnp.transpose(q2).astype(BF16)
            k_ref[:, LANES * h:LANES * (h + 1)] = jnp.transpose(ks[h]).astype(BF16)
        ones_row = (lax.broadcasted_iota(jnp.int32, (V_AUG - MLA_V, v.shape[1]), 0) == 0).astype(BF16)
        for h in range(MLA_HEADS):
            vt_ref[V_AUG * h:V_AUG * h + MLA_V, :] = v[MLA_V * h:MLA_V * (h + 1)].astype(BF16)
            vt_ref[V_AUG * h + MLA_V:V_AUG * (h + 1), :] = ones_row
        v_ref[...] = jnp.transpose(v).astype(BF16)

    row = lambda width: pl.BlockSpec((tm, width), lambda i: (i, 0))
    col = lambda height: pl.BlockSpec((height, tm), lambda i: (0, i))
    return pl.pallas_call(
        body, name="mla_prep_fwd", grid=(s // tm,),
        in_specs=_prep_in_specs(tm),
        out_specs=[row(1024), row(1024), row(512), col(1024), col(1024), col(MLA_HEADS * V_AUG)],
        out_shape=[jax.ShapeDtypeStruct((s, 1024), BF16), jax.ShapeDtypeStruct((s, 1024), BF16),
                   jax.ShapeDtypeStruct((s, 512), BF16), jax.ShapeDtypeStruct((1024, s), BF16),
                   jax.ShapeDtypeStruct((1024, s), BF16),
                   jax.ShapeDtypeStruct((MLA_HEADS * V_AUG, s), BF16)],
        compiler_params=_params(("parallel",)),
    )(proj, *norms, *weights, *rope)


def _mla_prep_bwd(proj, norms, weights, rope, dq, dk, dv):
    s = proj.shape[0]
    tm = min(512, s)

    def body(blk_ref, *refs):
        ins, (dq_ref, dk_ref, dv_ref) = refs[:11], refs[11:14]
        dblk_ref, dqan_ref, dkvan_ref, dqn_ref, dkn_ref, dwq_ref, dwkv_ref = refs[14:]
        diff, (wq, wk, wv, wqt, wkt, wvt), tables = _prep_operands(blk_ref, ins)

        def fn(q_lat, kv_lat, kr, qan, kvan, qn, kn, wq_, wk_, wv_):
            return _prep_fn(q_lat, kv_lat, kr, qan, kvan, qn, kn, wq_, wk_, wv_, wqt, wkt, wvt,
                            *tables, _mmw)

        _, vjp = jax.vjp(fn, *diff, wq, wk, wv)
        heads = range(MLA_HEADS)
        cts = (tuple(dq_ref[LANES * h:LANES * (h + 1), :] for h in heads),
               tuple(dk_ref[LANES * h:LANES * (h + 1), :] for h in heads), dv_ref[...])
        dq_lat, dkv_lat, dkr, dqan, dkvan, dqn, dkn, dwq_h, dwk_h, dwv = vjp(cts)
        dblk_ref[...] = jnp.transpose(
            jnp.concatenate([dq_lat, dkv_lat, dkr], axis=0)).astype(BF16)

        @pl.when(pl.program_id(0) == 0)
        def _():
            for r in (dqan_ref, dkvan_ref, dqn_ref, dkn_ref, dwq_ref, dwkv_ref):
                r[...] = jnp.zeros_like(r)

        dqan_ref[...] += dqan
        dkvan_ref[...] += dkvan
        dqn_ref[...] += dqn
        dkn_ref[...] += dkn
        for h in heads:
            dwq_ref[LANES * h:LANES * (h + 1), :] += dwq_h[h]
            dwkv_ref[LANES * h:LANES * (h + 1), :] += dwk_h[h]
        dwkv_ref[LANES * MLA_HEADS:, :] += dwv

    const = lambda shape: pl.BlockSpec(shape, lambda i: (0, 0))
    col = lambda height: pl.BlockSpec((height, tm), lambda i: (0, i))
    shapes = [(MLA_Q_LORA, 1), (MLA_KV_LORA, 1), (LANES, 1), (LANES, 1),
              (1024, MLA_Q_LORA), (1536, MLA_KV_LORA)]
    return pl.pallas_call(
        body, name="mla_prep_bwd", grid=(s // tm,),
        in_specs=_prep_in_specs(tm) + [col(1024), col(1024), col(512)],
        out_specs=[pl.BlockSpec((tm, 512), lambda i: (i, 0))] + [const(sh) for sh in shapes],
        out_shape=[jax.ShapeDtypeStruct((s, 512), BF16)]
        + [jax.ShapeDtypeStruct(sh, F32) for sh in shapes],
        compiler_params=_params(("arbitrary",)),
    )(proj, *norms, *weights, *rope, dq, dk, dv)


MLA_SCALE = MLA_QK ** -0.5


LOG2E = 1.4426950408889634
LN2 = 0.6931471805599453
Q_PRESCALE = MLA_SCALE * LOG2E
HEAD_GROUPS = ((0, 1),)


def _mla_attn_fwd(q2, k, vt):
    s = q2.shape[0]
    t = min(512, s)
    tk = min(128, s)
    nq = s // t
    r = t // tk

    def body(q_ref, k_ref, vt_ref, o_ref, lse_ref, acc_ref):
        i = pl.program_id(1)
        row = lax.broadcasted_iota(jnp.int32, (tk, t), 0)
        col = lax.broadcasted_iota(jnp.int32, (tk, t), 1)
        qh = [q_ref[:, LANES * hh:LANES * (hh + 1)] for hh in range(2)]
        acc_ref[...] = jnp.zeros_like(acc_ref)

        def scores(j, heads, diag=None):
            r0 = pl.multiple_of(j * tk, tk)
            q_lo = 0 if diag is None else diag * tk
            out = []
            for hh in heads:
                kc = k_ref[pl.ds(r0, tk), LANES * hh:LANES * (hh + 1)]
                sc = lax.dot_general(kc, qh[hh][q_lo:], (((1,), (1,)), ((), ())),
                                     preferred_element_type=F32)
                out.append(sc if diag is None
                           else jnp.where(row[:, :t - q_lo] <= col[:, :t - q_lo], sc, NEG_INF))
            return tuple(out)

        for heads in HEAD_GROUPS:
            stats = tuple(jnp.full((1, t), NEG_INF, F32) for _ in heads)

            def consume(j, scs, stats, q_lo=0, heads=heads):
                r0 = pl.multiple_of(j * tk, tk)
                out, ps, alphas = [], [], []
                for n, hh in enumerate(heads):
                    m_old = stats[n][:, q_lo:]
                    m_new = jnp.maximum(m_old, jnp.max(scs[n], axis=0, keepdims=True))
                    ps.append(jnp.exp2(scs[n] - m_new).astype(BF16))
                    alphas.append(jnp.exp2(m_old - m_new))
                    out.append(m_new if q_lo == 0
                               else jnp.concatenate([stats[n][:, :q_lo], m_new], axis=1))
                for n, hh in enumerate(heads):
                    vc = vt_ref[V_AUG * hh:V_AUG * (hh + 1), pl.ds(r0, tk)]
                    acc_ref[hh, :, q_lo:] = alphas[n] * acc_ref[hh, :, q_lo:] + jnp.dot(
                        vc, ps[n], preferred_element_type=F32)
                return tuple(out)

            def group(j0, stats, diag, heads=heads):
                scs = [scores(j0 + d, heads, d if diag else None) for d in range(r)]
                for d in range(r):
                    stats = consume(j0 + d, scs[d], stats, d * tk if diag else 0)
                return stats

            stats = group(r * i, stats, True)
            stats = lax.fori_loop(0, i, lambda j, st: group(r * j, st, False), stats)
            for n, hh in enumerate(heads):
                l = acc_ref[hh, MLA_V:MLA_V + 1, :]
                o_ref[:, MLA_V * hh:MLA_V * (hh + 1)] = jnp.transpose(acc_ref[hh, 0:MLA_V, :] / l)
                lse_ref[0, hh:hh + 1, :] = stats[n] + jnp.log2(l)

    return pl.pallas_call(
        body, name="mla_attn_fwd", grid=(MLA_HEADS // 2, nq),
        in_specs=[pl.BlockSpec((t, 256), lambda p, i: (i, p)),
                  pl.BlockSpec((s, 256), lambda p, i: (0, p)),
                  pl.BlockSpec((2 * V_AUG, s), lambda p, i: (p, 0))],
        out_specs=[pl.BlockSpec((t, 128), lambda p, i: (i, p)),
                   pl.BlockSpec((1, 2, t), lambda p, i: (p, 0, i))],
        out_shape=[jax.ShapeDtypeStruct((s, 512), F32),
                   jax.ShapeDtypeStruct((MLA_HEADS // 2, 2, s), F32)],
        scratch_shapes=[pltpu.VMEM((2, V_AUG, t), F32)],
        compiler_params=_params(("parallel", "arbitrary")),
    )(q2, k, vt)


def _mla_attn_bwd(q2, q2t, k, kt, v, do, dot, lse_rows, delta_rows):
    s = q2.shape[0]
    t = min(512, s)
    nq = s // t

    def body(q_ref, qt_ref, k_ref, kt_ref, v_ref, do_ref, dot_ref, lse_ref, dl_ref,
             dq_ref, dk_ref, dv_ref):
        j = pl.program_id(1)

        @pl.when(j == 0)
        def _():
            dq_ref[...] = jnp.zeros_like(dq_ref)

        dk_ref[...] = jnp.zeros_like(dk_ref)
        dv_ref[...] = jnp.zeros_like(dv_ref)
        row = lax.broadcasted_iota(jnp.int32, (t, t), 0)
        col = lax.broadcasted_iota(jnp.int32, (t, t), 1)
        causal_t = row <= col
        kh = [k_ref[:, LANES * hh:LANES * (hh + 1)] for hh in range(2)]
        kth = [kt_ref[LANES * hh:LANES * (hh + 1), :] for hh in range(2)]
        vh = [v_ref[:, MLA_V * hh:MLA_V * (hh + 1)] for hh in range(2)]
        nt = (((1,), (1,)), ((), ()))

        def step(i, masked):
            r0 = pl.multiple_of(i * t, t)
            sd = []
            for hh in range(2):
                qh = q_ref[pl.ds(r0, t), LANES * hh:LANES * (hh + 1)]
                doh = do_ref[pl.ds(r0, t), MLA_V * hh:MLA_V * (hh + 1)]
                sc_t = lax.dot_general(kh[hh], qh, nt, preferred_element_type=F32)
                sd.append(jnp.where(causal_t, sc_t, NEG_INF) if masked else sc_t)
                sd.append(lax.dot_general(vh[hh], doh, nt, preferred_element_type=F32))
            for hh in range(2):
                lse = lse_ref[0, hh:hh + 1, pl.ds(r0, t)]
                dl = dl_ref[0, hh:hh + 1, pl.ds(r0, t)]
                p_t = jnp.exp2(sd[2 * hh] - lse)
                g_t = (p_t * (sd[2 * hh + 1] - dl)).astype(BF16)
                qth = qt_ref[LANES * hh:LANES * (hh + 1), pl.ds(r0, t)]
                doth = dot_ref[MLA_V * hh:MLA_V * (hh + 1), pl.ds(r0, t)]
                dv_ref[MLA_V * hh:MLA_V * (hh + 1), :] += lax.dot_general(
                    doth, p_t.astype(BF16), nt, preferred_element_type=F32)
                dk_ref[LANES * hh:LANES * (hh + 1), :] += lax.dot_general(
                    qth, g_t, nt, preferred_element_type=F32)
                dq_ref[LANES * hh:LANES * (hh + 1), pl.ds(r0, t)] += jnp.dot(
                    kth[hh], g_t, preferred_element_type=F32)

        step(j, True)

        def trip(i, carry):
            step(i, False)
            return carry

        lax.fori_loop(j + 1, nq, trip, 0)
        dk_ref[...] = dk_ref[...] * LN2

        @pl.when(j == nq - 1)
        def _():
            dq_ref[...] = dq_ref[...] * MLA_SCALE

    return pl.pallas_call(
        body, name="mla_attn_bwd", grid=(MLA_HEADS // 2, nq),
        in_specs=[pl.BlockSpec((s, 256), lambda p, j: (0, p)),
                  pl.BlockSpec((256, s), lambda p, j: (p, 0)),
                  pl.BlockSpec((t, 256), lambda p, j: (j, p)),
                  pl.BlockSpec((256, t), lambda p, j: (p, j)),
                  pl.BlockSpec((t, 128), lambda p, j: (j, p)),
                  pl.BlockSpec((s, 128), lambda p, j: (0, p)),
                  pl.BlockSpec((128, s), lambda p, j: (p, 0)),
                  pl.BlockSpec((1, 2, s), lambda p, j: (p, 0, 0)),
                  pl.BlockSpec((1, 2, s), lambda p, j: (p, 0, 0))],
        out_specs=[pl.BlockSpec((256, s), lambda p, j: (p, 0)),
                   pl.BlockSpec((256, t), lambda p, j: (p, j)),
                   pl.BlockSpec((128, t), lambda p, j: (p, j))],
        out_shape=[jax.ShapeDtypeStruct((1024, s), F32), jax.ShapeDtypeStruct((1024, s), F32),
                   jax.ShapeDtypeStruct((512, s), F32)],
        compiler_params=_params(("parallel", "arbitrary")),
    )(q2, q2t, k, kt, v, do, dot, lse_rows, delta_rows)


SWA_SCALE = SWA_DIM ** -0.5
SWA_COLS = SWA_GROUP * BLOCK
SWA_LOG2 = SWA_SCALE * LOG2E


def _swa_tables():
    k = np.arange(2 * BLOCK)[:, None]
    col = np.arange(SWA_COLS)[None, :]
    dist = BLOCK + (col % BLOCK) - k
    valid = (dist >= 0) & (dist < BLOCK)
    out = np.zeros((2, SWA_KV, 2 * BLOCK, SWA_COLS), np.float32)
    for first in range(2):
        ok = valid & ((k >= BLOCK) | (first == 0))
        for j in range(SWA_KV):
            slope = 2.0 ** -(SWA_GROUP * j + col // BLOCK + 1)
            out[first, j] = np.where(ok, -slope * dist * LOG2E, NEG_INF)
    return jnp.asarray(out)


def _swa_tile_inputs(sq_ref, skv_ref, halo_ref, qn_ref, kn_ref, sk_ref, add_ref, first):
    tokens = sq_ref.shape[0]
    kv_all = jnp.concatenate([halo_ref[...], skv_ref[...]], axis=0)
    kv_t = jnp.transpose(kv_all)
    sq_t = jnp.transpose(sq_ref[...])
    k_raw = [kv_t[SWA_DIM * j:SWA_DIM * (j + 1)] for j in range(SWA_KV)]
    v_t = [kv_t[128 + SWA_DIM * j:128 + SWA_DIM * (j + 1)] for j in range(SWA_KV)]
    v_nat = [kv_all[:, 128 + SWA_DIM * j:128 + SWA_DIM * (j + 1)] for j in range(SWA_KV)]
    q_raw = [sq_t[SWA_DIM * h:SWA_DIM * (h + 1)] for h in range(SWA_HEADS)]
    qn_b = jnp.broadcast_to(qn_ref[...], (SWA_DIM, tokens))
    kn_b = jnp.broadcast_to(kn_ref[...], (SWA_DIM, tokens + BLOCK))
    lane_grp = lax.broadcasted_iota(jnp.int32, (1, SWA_COLS), 1) // BLOCK
    sinks, adds = [], []
    for j in range(SWA_KV):
        row = jnp.zeros((1, SWA_COLS), F32)
        for g in range(SWA_GROUP):
            h = SWA_GROUP * j + g
            row = jnp.where(lane_grp == g, sk_ref[:, h:h + 1] * LOG2E, row)
        sinks.append(row)
        adds.append((jnp.where(first, add_ref[1, j], add_ref[0, j]), add_ref[0, j]))
    return k_raw, v_t, v_nat, q_raw, qn_b, kn_b, sinks, adds


def _swa_probs(kb, qs_t, add, sink):
    s2 = jnp.dot(kb, qs_t, preferred_element_type=F32) * SWA_LOG2 + add
    m = jnp.maximum(jnp.max(s2, axis=0, keepdims=True), sink)
    e = jnp.exp2(s2 - m)
    es = jnp.exp2(sink - m)
    inv = 1.0 / (jnp.sum(e, axis=0, keepdims=True) + es)
    return e, inv, es


def _swa_queries(qn_t, j, b):
    return jnp.concatenate([qn_t[SWA_GROUP * j + g][:, BLOCK * b:BLOCK * (b + 1)]
                            for g in range(SWA_GROUP)], axis=1)


def _swa_fwd(proj, qn, kn, sinks, tables):
    s = proj.shape[0]
    ts = min(1024, s)
    nb = ts // BLOCK

    def body(sq_ref, skv_ref, halo_ref, qn_ref, kn_ref, sk_ref, add_ref, o_ref, ot_ref):
        first = pl.program_id(0) == 0
        k_raw, v_t, _, q_raw, qn_b, kn_b, sink_rows, adds = _swa_tile_inputs(
            sq_ref, skv_ref, halo_ref, qn_ref, kn_ref, sk_ref, add_ref, first)
        kn_nat = [jnp.transpose(_rms0(k, kn_b)).astype(BF16) for k in k_raw]
        v_t = [v.astype(BF16) for v in v_t]
        qn_t = [_rms0(q, qn_b).astype(BF16) for q in q_raw]
        for b in range(nb):
            band = slice(BLOCK * b, BLOCK * (b + 2))
            for j in range(SWA_KV):
                e, inv, _ = _swa_probs(kn_nat[j][band], _swa_queries(qn_t, j, b),
                                       adds[j][0 if b == 0 else 1], sink_rows[j])
                o_t = jnp.dot(v_t[j][:, band], (e * inv).astype(BF16),
                              preferred_element_type=F32)
                for g in range(SWA_GROUP):
                    h = SWA_GROUP * j + g
                    ot_ref[SWA_DIM * h:SWA_DIM * (h + 1), BLOCK * b:BLOCK * (b + 1)] = (
                        o_t[:, BLOCK * g:BLOCK * (g + 1)])
        o_ref[...] = jnp.transpose(ot_ref[...])

    const = lambda shape: pl.BlockSpec(shape, lambda i: (0,) * len(shape))
    return pl.pallas_call(
        body, name="swa_fwd", grid=(s // ts,),
        in_specs=[pl.BlockSpec((ts, 512), lambda i: (i, OFF_SQ // 512)),
                  pl.BlockSpec((ts, 256), lambda i: (i, OFF_SKV // 256)),
                  pl.BlockSpec((BLOCK, 256), lambda i: (jnp.maximum(i * nb - 1, 0), OFF_SKV // 256)),
                  const((SWA_DIM, 1)), const((SWA_DIM, 1)), const((1, SWA_HEADS)),
                  const(tables.shape)],
        out_specs=pl.BlockSpec((ts, 512), lambda i: (i, 0)),
        out_shape=jax.ShapeDtypeStruct((s, 512), F32),
        scratch_shapes=[pltpu.VMEM((512, ts), F32)],
        compiler_params=_params(("parallel",)),
    )(proj, proj, proj, qn, kn, sinks, tables)


def _swa_bwd(proj, qn, kn, sinks, tables, do):
    s = proj.shape[0]
    ts = min(1024, s)
    nb = ts // BLOCK
    nt = s // ts

    def body(sq_ref, skv_ref, halo_ref, qn_ref, kn_ref, sk_ref, add_ref, do_ref,
             dsq_ref, dskv_ref, dqn_ref, dkn_ref, dsk_ref, carry_ref, dqt_ref, dkvt_ref):
        step = pl.program_id(0)
        first = step == nt - 1

        @pl.when(step == 0)
        def _():
            carry_ref[...] = jnp.zeros_like(carry_ref)
            dqn_ref[...] = jnp.zeros_like(dqn_ref)
            dkn_ref[...] = jnp.zeros_like(dkn_ref)
            dsk_ref[...] = jnp.zeros_like(dsk_ref)

        k_raw, v_t, v_nat, q_raw, qn_b, kn_b, sink_rows, adds = _swa_tile_inputs(
            sq_ref, skv_ref, halo_ref, qn_ref, kn_ref, sk_ref, add_ref, first)
        kn_f = [_rms0(k, kn_b) for k in k_raw]
        kn_t = [k.astype(BF16) for k in kn_f]
        kn_nat = [jnp.transpose(k).astype(BF16) for k in kn_f]
        v_nat = [v.astype(BF16) for v in v_nat]
        qn_t = [_rms0(q, qn_b).astype(BF16) for q in q_raw]
        do_t = jnp.transpose(do_ref[...].astype(F32)).astype(BF16)

        dkvt_ref[...] = jnp.zeros_like(dkvt_ref)
        dsink = [jnp.zeros((1, SWA_COLS), F32) for _ in range(SWA_KV)]
        nt_dims = (((1,), (1,)), ((), ()))
        for b in range(nb):
            rows = slice(BLOCK * b, BLOCK * (b + 1))
            band = slice(BLOCK * b, BLOCK * (b + 2))
            for j in range(SWA_KV):
                heads = [SWA_GROUP * j + g for g in range(SWA_GROUP)]
                qs_t = _swa_queries(qn_t, j, b)
                dos_t = jnp.concatenate([do_t[SWA_DIM * h:SWA_DIM * (h + 1), rows] for h in heads],
                                        axis=1)
                e, inv, es = _swa_probs(kn_nat[j][band], qs_t, adds[j][0 if b == 0 else 1],
                                        sink_rows[j])
                p = e * inv
                dp = jnp.dot(v_nat[j][band], dos_t, preferred_element_type=F32)
                dsum = jnp.sum(p * dp, axis=0, keepdims=True)
                dsink[j] = dsink[j] - es * inv * dsum
                g_t = (p * (dp - dsum) * SWA_SCALE).astype(BF16)
                dv_t = lax.dot_general(dos_t, p.astype(BF16), nt_dims,
                                       preferred_element_type=F32)
                dk_t = lax.dot_general(qs_t, g_t, nt_dims, preferred_element_type=F32)
                dq_t = jnp.dot(kn_t[j][:, band], g_t, preferred_element_type=F32)
                dkvt_ref[SWA_DIM * j:SWA_DIM * (j + 1), band] += dk_t
                dkvt_ref[128 + SWA_DIM * j:128 + SWA_DIM * (j + 1), band] += dv_t
                for g, h in enumerate(heads):
                    dqt_ref[SWA_DIM * h:SWA_DIM * (h + 1), rows] = dq_t[:, BLOCK * g:BLOCK * (g + 1)]

        dqn = jnp.zeros((SWA_DIM, 1), F32)
        for h in range(SWA_HEADS):
            _, vjp = jax.vjp(_rms0, q_raw[h], qn_ref[...])
            dq, dg = vjp(dqt_ref[SWA_DIM * h:SWA_DIM * (h + 1), :])
            dqt_ref[SWA_DIM * h:SWA_DIM * (h + 1), :] = dq
            dqn = dqn + dg
        dqn_ref[...] += dqn
        dsq_ref[...] = jnp.transpose(dqt_ref[...]).astype(BF16)
        dkn = jnp.zeros((SWA_DIM, 1), F32)
        lane_grp = lax.broadcasted_iota(jnp.int32, (1, SWA_COLS), 1) // BLOCK
        for j in range(SWA_KV):
            _, vjp = jax.vjp(_rms0, k_raw[j], kn_ref[...])
            dk, dg = vjp(dkvt_ref[SWA_DIM * j:SWA_DIM * (j + 1), :])
            dkvt_ref[SWA_DIM * j:SWA_DIM * (j + 1), :] = dk
            dkn = dkn + dg
            for g in range(SWA_GROUP):
                h = SWA_GROUP * j + g
                dsk_ref[:, h:h + 1] += jnp.sum(jnp.where(lane_grp == g, dsink[j], 0.0), axis=1,
                                               keepdims=True)
        dkn_ref[...] += dkn
        dkv = jnp.transpose(dkvt_ref[...])
        dskv_ref[0:ts - BLOCK, :] = dkv[BLOCK:ts].astype(BF16)
        dskv_ref[ts - BLOCK:ts, :] = (dkv[ts:ts + BLOCK] + carry_ref[...]).astype(BF16)
        carry_ref[...] = dkv[0:BLOCK]

    const = lambda shape: pl.BlockSpec(shape, lambda st: (0,) * len(shape))
    return pl.pallas_call(
        body, name="swa_bwd", grid=(nt,),
        in_specs=[pl.BlockSpec((ts, 512), lambda st: (nt - 1 - st, OFF_SQ // 512)),
                  pl.BlockSpec((ts, 256), lambda st: (nt - 1 - st, OFF_SKV // 256)),
                  pl.BlockSpec((BLOCK, 256),
                               lambda st: (jnp.maximum((nt - 1 - st) * nb - 1, 0), OFF_SKV // 256)),
                  const((SWA_DIM, 1)), const((SWA_DIM, 1)), const((1, SWA_HEADS)),
                  const(tables.shape),
                  pl.BlockSpec((ts, 512), lambda st: (nt - 1 - st, 0))],
        out_specs=[pl.BlockSpec((ts, 512), lambda st: (nt - 1 - st, 0)),
                   pl.BlockSpec((ts, 256), lambda st: (nt - 1 - st, 0)),
                   const((SWA_DIM, 1)), const((SWA_DIM, 1)), const((1, SWA_HEADS))],
        out_shape=[jax.ShapeDtypeStruct((s, 512), BF16), jax.ShapeDtypeStruct((s, 256), BF16),
                   jax.ShapeDtypeStruct((SWA_DIM, 1), F32), jax.ShapeDtypeStruct((SWA_DIM, 1), F32),
                   jax.ShapeDtypeStruct((1, SWA_HEADS), F32)],
        scratch_shapes=[pltpu.VMEM((BLOCK, 256), F32), pltpu.VMEM((512, ts), F32),
                        pltpu.VMEM((256, ts + BLOCK), F32)],
        compiler_params=_params(("arbitrary",)),
    )(proj, proj, proj, qn, kn, sinks, tables, do)


HALO = 8


def _shift_down(u, halo, k):
    tm = u.shape[0]
    rid = lax.broadcasted_iota(jnp.int32, u.shape, 0)
    out = pltpu.roll(u, k, 0)
    for r in range(k):
        out = jnp.where(rid == r, halo[HALO - k + r:HALO - k + r + 1, :], out)
    return out


def _shift_up(u, halo, k):
    tm = u.shape[0]
    rid = lax.broadcasted_iota(jnp.int32, u.shape, 0)
    out = pltpu.roll(u, tm - k, 0)
    for r in range(k):
        out = jnp.where(rid == tm - k + r, halo[r:r + 1, :], out)
    return out


def _conv_fwd_vals(conv_ref, convp_ref, cw_ref, is_first):
    c_h, c_b, c_c = conv_ref[:, 0:512], conv_ref[:, 512:1024], conv_ref[:, 1024:1536]
    u = c_c * c_h
    up = jnp.where(is_first, 0.0, convp_ref[:, 1024:1536] * convp_ref[:, 0:512])
    u1 = _shift_down(u, up, 1)
    u2 = _shift_down(u, up, 2)
    yc = cw_ref[0:1, :] * u2 + cw_ref[1:2, :] * u1 + cw_ref[2:3, :] * u
    return c_h, c_b, c_c, u, u1, u2, yc


def _out_fwd(proj, o_mla, o_swa, x, w_out, cw, target=None):
    s = proj.shape[0]
    tm = min(512, s)
    nt = s // tm
    with_loss = target is not None

    def body(*refs):
        conv_ref, convp_ref, gates_ref, om_ref, os_ref, x_ref, w_ref, cw_ref = refs[:8]
        if with_loss:
            t_ref, y_ref, zt_ref, loss_ref, z_ref = refs[8:]
        else:
            y_ref, zt_ref, z_ref = refs[8:]
        i = pl.program_id(0)
        _, c_b, _, _, _, _, yc = _conv_fwd_vals(conv_ref, convp_ref, cw_ref, i == 0)
        mix = (om_ref[...], c_b * yc, os_ref[...])
        for n in range(3):
            g = gates_ref[:, GROUP * n:GROUP * (n + 1)]
            z = mix[n] * (g * _sigmoid(g))
            z_ref[:, GROUP * n:GROUP * (n + 1)] = z.astype(BF16)
            zt_ref[GROUP * n:GROUP * (n + 1), :] = jnp.transpose(z).astype(BF16)
        y = x_ref[...] + jnp.dot(z_ref[...], w_ref[...], preferred_element_type=F32)
        if not with_loss:
            y_ref[...] = y
            return
        err = y - t_ref[...]
        y_ref[...] = err * (1.0 / D_MODEL)

        @pl.when(i == 0)
        def _():
            loss_ref[...] = jnp.zeros_like(loss_ref)

        sq = jnp.sum((err * err).reshape(tm // 8, 8, D_MODEL), axis=0)
        part = sq[:, 0:LANES]
        for c in range(1, D_MODEL // LANES):
            part = part + sq[:, LANES * c:LANES * (c + 1)]
        loss_ref[...] += part

        @pl.when(i == nt - 1)
        def _():
            loss_ref[...] = jnp.full(loss_ref.shape, (0.5 / D_MODEL) * jnp.sum(loss_ref[...]), F32)

    row = lambda width: pl.BlockSpec((tm, width), lambda i: (i, 0))
    in_specs = [pl.BlockSpec((tm, 1536), lambda i: (i, 0)),
                pl.BlockSpec((HALO, 1536), lambda i: (jnp.maximum(i * (tm // HALO) - 1, 0), 0)),
                pl.BlockSpec((tm, 1536), lambda i: (i, 1)),
                row(512), row(512), row(D_MODEL),
                pl.BlockSpec((D_MIX, D_MODEL), lambda i: (0, 0)),
                pl.BlockSpec((HALO, 512), lambda i: (0, 0))]
    out_specs = [row(D_MODEL), pl.BlockSpec((D_MIX, tm), lambda i: (0, i))]
    out_shape = [jax.ShapeDtypeStruct((s, D_MODEL), F32), jax.ShapeDtypeStruct((D_MIX, s), BF16)]
    operands = [proj, proj, proj, o_mla, o_swa, x, w_out, cw]
    if with_loss:
        in_specs.append(row(D_MODEL))
        out_specs.append(pl.BlockSpec((8, LANES), lambda i: (0, 0)))
        out_shape.append(jax.ShapeDtypeStruct((8, LANES), F32))
        operands.append(target)
    return pl.pallas_call(
        body, name="out_fwd_loss" if with_loss else "out_fwd", grid=(nt,),
        in_specs=in_specs, out_specs=out_specs, out_shape=out_shape,
        scratch_shapes=[pltpu.VMEM((tm, D_MIX), BF16)],
        compiler_params=_params(("arbitrary",) if with_loss else ("parallel",)),
    )(*operands)


def _out_bwd(dy, proj, o_mla, o_swa, w_out, cw):
    s = proj.shape[0]
    tm = min(512, s)
    nt = s // tm
    hb = tm // HALO

    def body(dy_ref, dyn_ref, conv_ref, convp_ref, convn_ref, gates_ref, gatesn_ref, om_ref, os_ref,
             w_ref, cw_ref,
             dconv_ref, dgates_ref, dom_ref, domt_ref, delta_ref, dos_ref, dcw_ref):
        i = pl.program_id(0)
        dz = _mm_nt(dy_ref[...], w_ref[...])

        def gate(n):
            g = gates_ref[:, GROUP * n:GROUP * (n + 1)]
            sg = _sigmoid(g)
            return g * sg, sg * (1.0 + g * (1.0 - sg))

        for n, o_ref, do_ref in ((0, om_ref, dom_ref), (2, os_ref, dos_ref)):
            silu, dsilu = gate(n)
            dzn = dz[:, GROUP * n:GROUP * (n + 1)]
            o = o_ref[...]
            do = dzn * silu
            do_ref[...] = do.astype(do_ref.dtype)
            dgates_ref[:, GROUP * n:GROUP * (n + 1)] = (dzn * o * dsilu).astype(BF16)
            if n == 0:
                domt_ref[...] = jnp.transpose(do).astype(BF16)
                t = do * o
                for h in range(MLA_HEADS):
                    delta_ref[:, h:h + 1] = jnp.sum(t[:, MLA_V * h:MLA_V * (h + 1)], axis=-1,
                                                    keepdims=True)

        c_h, c_b, c_c, u, u1, u2, yc = _conv_fwd_vals(conv_ref, convp_ref, cw_ref, i == 0)
        silu, dsilu = gate(1)
        dzc = dz[:, GROUP:2 * GROUP]
        dgates_ref[:, GROUP:2 * GROUP] = (dzc * (c_b * yc) * dsilu).astype(BF16)
        dycr = dzc * silu
        dyc = dycr * c_b
        gn = gatesn_ref[:, GROUP:2 * GROUP]
        dzc_n = _mm_nt(dyn_ref[...], w_ref[GROUP:2 * GROUP, :])
        dyc_n = jnp.where(i == nt - 1, 0.0, dzc_n * (gn * _sigmoid(gn)) * convn_ref[:, 512:1024])
        d1 = _shift_up(dyc, dyc_n, 1)
        d2 = _shift_up(dyc, dyc_n, 2)
        du = cw_ref[2:3, :] * dyc + cw_ref[1:2, :] * d1 + cw_ref[0:1, :] * d2
        dconv_ref[:, 0:512] = (du * c_c).astype(BF16)
        dconv_ref[:, 512:1024] = (dycr * yc).astype(BF16)
        dconv_ref[:, 1024:1536] = (du * c_h).astype(BF16)

        @pl.when(i == 0)
        def _():
            dcw_ref[...] = jnp.zeros_like(dcw_ref)

        for k, uk in enumerate((u2, u1, u)):
            dcw_ref[k:k + 1, :] += jnp.sum(dyc * uk, axis=0, keepdims=True)

    row = lambda width: pl.BlockSpec((tm, width), lambda i: (i, 0))
    prev = lambda i: jnp.maximum(i * hb - 1, 0)
    nxt = lambda i: jnp.minimum((i + 1) * hb, s // HALO - 1)
    return pl.pallas_call(
        body, name="out_bwd", grid=(nt,),
        in_specs=[row(D_MODEL),
                  pl.BlockSpec((HALO, D_MODEL), lambda i: (nxt(i), 0)),
                  pl.BlockSpec((tm, 1536), lambda i: (i, 0)),
                  pl.BlockSpec((HALO, 1536), lambda i: (prev(i), 0)),
                  pl.BlockSpec((HALO, 1536), lambda i: (nxt(i), 0)),
                  pl.BlockSpec((tm, 1536), lambda i: (i, 1)),
                  pl.BlockSpec((HALO, 1536), lambda i: (nxt(i), 1)),
                  row(512), row(512),
                  pl.BlockSpec((D_MIX, D_MODEL), lambda i: (0, 0)),
                  pl.BlockSpec((HALO, 512), lambda i: (0, 0))],
        out_specs=[row(1536), row(1536), row(512), pl.BlockSpec((512, tm), lambda i: (0, i)),
                   row(MLA_HEADS), row(512), pl.BlockSpec((HALO, 512), lambda i: (0, 0))],
        out_shape=[jax.ShapeDtypeStruct((s, 1536), BF16), jax.ShapeDtypeStruct((s, 1536), BF16),
                   jax.ShapeDtypeStruct((s, 512), BF16), jax.ShapeDtypeStruct((512, s), BF16),
                   jax.ShapeDtypeStruct((s, MLA_HEADS), F32),
                   jax.ShapeDtypeStruct((s, 512), BF16), jax.ShapeDtypeStruct((HALO, 512), F32)],
        compiler_params=_params(("arbitrary",)),
    )(dy, dy, proj, proj, proj, proj, proj, o_mla, o_swa, w_out, cw)


def _adam_update(g, w, m, v):
    c1 = 1.0 - ADAM_B1
    c2 = 1.0 - ADAM_B2
    bc1 = 1.0 - ADAM_B1 ** ADAM_STEP
    bc2 = 1.0 - ADAM_B2 ** ADAM_STEP
    m_new = ADAM_B1 * m + c1 * g
    v_new = ADAM_B2 * v + c2 * (g * g)
    delta = -ADAM_LR * ((m_new / bc1) / (jnp.sqrt(v_new / bc2) + ADAM_EPS) + ADAM_WD * w)
    return delta, m_new, v_new


def _adamw(g, w, m, v):
    rows = g.shape[0]
    tr = min(256, rows)

    def body(g_ref, w_ref, m_ref, v_ref, d_ref, mo_ref, vo_ref):
        d_ref[...], mo_ref[...], vo_ref[...] = _adam_update(g_ref[...], w_ref[...], m_ref[...],
                                                            v_ref[...])

    spec = pl.BlockSpec((tr, g.shape[1]), lambda i: (i, 0))
    return pl.pallas_call(
        body, name="adamw", grid=(rows // tr,),
        in_specs=[spec] * 4, out_specs=[spec] * 3,
        out_shape=[jax.ShapeDtypeStruct(g.shape, F32)] * 3,
        compiler_params=_params(("parallel",)),
    )(g, w, m, v)


def _adamw_small(gs, ws, ms, vs):
    n = len(gs)

    def body(*refs):
        ins, outs = refs[:4 * n], refs[4 * n:]
        for a in range(n):
            res = _adam_update(*(ins[kind * n + a][...] for kind in range(4)))
            for kind in range(3):
                outs[kind * n + a][...] = res[kind]

    vmem = pl.BlockSpec(memory_space=pltpu.VMEM)
    out = pl.pallas_call(
        body, name="adamw_small",
        in_specs=[vmem] * (4 * n), out_specs=[vmem] * (3 * n),
        out_shape=[jax.ShapeDtypeStruct(g.shape, F32) for _ in range(3) for g in gs],
    )(*gs, *ws, *ms, *vs)
    return out[:n], out[n:2 * n], out[2 * n:]


HBM_SPEC = pl.BlockSpec(memory_space=pltpu.HBM)


def _place():
    x, y, c = lax.axis_index("x"), lax.axis_index("y"), lax.axis_index("c")
    chips = [(1 - x, y), (x, 1 - y), (1 - x, 1 - y)]
    return x, y, c, chips


def _all_gather(shards):
    na = len(shards)
    halves = [sh.shape[0] // 2 for sh in shards]

    def body(*refs):
        w_refs, a_refs = refs[:na], refs[na:2 * na]
        send_sems, recv_sems = refs[2 * na:]
        x, y, c, chips = _place()
        k = 2 * x + y
        sib = (x, y, 1 - c)

        def slab(a, kk, hc):
            return a_refs[a].at[kk, pl.ds(hc * halves[a], halves[a]), :]

        def copy(a, n, src, dst, to):
            return pltpu.make_async_remote_copy(
                src_ref=src, dst_ref=dst, send_sem=send_sems.at[6 * a + n],
                recv_sem=recv_sems.at[6 * a + n], device_id=to, device_id_type=MESH)

        first = [copy(a, n, w_refs[a].at[pl.ds(c * halves[a], halves[a]), :], slab(a, k, c),
                      (cx, cy, c))
                 for n, (cx, cy) in enumerate(chips) for a in range(na)]
        for cp in first:
            cp.start()
        passed = []
        for n, (cx, cy) in enumerate(chips):
            kk = 2 * cx + cy
            for a in range(na):
                copy(a, n, slab(a, kk, c), slab(a, kk, c), (cx, cy, c)).wait_recv()
                fwd = copy(a, 3 + n, slab(a, kk, c), slab(a, kk, c), sib)
                fwd.start()
                passed.append(fwd)
        for n, (cx, cy) in enumerate(chips):
            kk = 2 * cx + cy
            for a in range(na):
                copy(a, 3 + n, slab(a, kk, 1 - c), slab(a, kk, 1 - c), sib).wait_recv()
        for cp in first + passed:
            cp.wait_send()

    return pl.pallas_call(
        body, name="weights_all_gather",
        in_specs=[HBM_SPEC] * na, out_specs=[HBM_SPEC] * na,
        out_shape=[jax.ShapeDtypeStruct((N_CHIPS,) + sh.shape, sh.dtype) for sh in shards],
        scratch_shapes=[pltpu.SemaphoreType.DMA((6 * na,)), pltpu.SemaphoreType.DMA((6 * na,))],
    )(*shards)


def _fill_own_slab(buf, src, k_idx):
    n, rows, cols = buf.shape
    tr = _row_tile(rows)
    slabs = src.ndim == 3

    def body(k_ref, src_ref, buf_ref, out_ref):
        out_ref[0] = src_ref[0] if slabs else src_ref[...]

    if slabs:
        src_spec = pl.BlockSpec((1, tr, cols), lambda t, k_ref: (k_ref[0], t, 0))
    else:
        src_spec = pl.BlockSpec((tr, cols), lambda t, k_ref: (t, 0))
    return pl.pallas_call(
        body, name="fill_own_slab",
        grid_spec=pltpu.PrefetchScalarGridSpec(
            num_scalar_prefetch=1, grid=(rows // tr,),
            in_specs=[src_spec, pl.BlockSpec(memory_space=pl.ANY)],
            out_specs=pl.BlockSpec((1, tr, cols), lambda t, k_ref: (k_ref[0], t, 0))),
        out_shape=jax.ShapeDtypeStruct(buf.shape, buf.dtype),
        input_output_aliases={2: 0},
        compiler_params=_params(("parallel",)),
    )(k_idx, src, buf)


SEM_SPEC = pl.BlockSpec(memory_space=pltpu.SEMAPHORE)
LATE_COPIES = 6


def _late_copy(a, j, peer_core, src, dst, send_sems, recv_sems, to, sender_core):
    return pltpu.make_async_remote_copy(
        src_ref=src, dst_ref=dst, send_sem=send_sems.at[LATE_COPIES * a + 2 * j + peer_core],
        recv_sem=recv_sems.at[LATE_COPIES * a + 2 * j + sender_core], device_id=to,
        device_id_type=MESH)


def _gather_start(shards):
    na = len(shards)

    def body(*refs):
        w_refs, land_refs = refs[:na], refs[na:2 * na]
        send_sems, recv_sems = refs[2 * na], refs[2 * na + 1]
        token = refs[-1]
        x, y, c, chips = _place()
        k = 2 * x + y
        for a in range(na):
            half = w_refs[a].shape[0] // 2
            src = w_refs[a].at[pl.ds(c * half, half), :]
            dst = land_refs[a].at[k, pl.ds(c * half, half), :]
            for j, (cx, cy) in enumerate(chips):
                for tc in range(2):
                    _late_copy(a, j, tc, src, dst, send_sems, recv_sems, (cx, cy, tc), c).start()
        token[...] = jnp.zeros_like(token)

    lands = [pltpu.with_memory_space_constraint(lax.empty((N_CHIPS,) + sh.shape, sh.dtype), pltpu.HBM)
             for sh in shards]
    srcs = [pltpu.with_memory_space_constraint(sh, pltpu.HBM) for sh in shards]
    sems = pltpu.SemaphoreType.DMA((LATE_COPIES * na,))
    aliases = {a: 2 + a for a in range(2 * na)}
    return pl.pallas_call(
        body, name="late_weights_gather_start",
        in_specs=[HBM_SPEC] * (2 * na),
        out_specs=[SEM_SPEC, SEM_SPEC] + [HBM_SPEC] * (2 * na) + [pl.BlockSpec(memory_space=pltpu.VMEM)],
        out_shape=[sems, sems] + [pltpu.HBM(v.shape, v.dtype) for v in srcs + lands]
        + [jax.ShapeDtypeStruct((8, LANES), F32)],
        input_output_aliases=aliases,
        compiler_params=pltpu.CompilerParams(
            has_side_effects=pltpu.SideEffectType.DATAFLOW_SIDE_EFFECTING),
    )(*srcs, *lands)


def _gather_wait(started, na, after):
    send_sems, recv_sems = started[0], started[1]
    bufs = started[2:2 + 2 * na]

    def body(*refs):
        w_refs, land_refs = refs[:na], refs[na:2 * na]
        send_sems, recv_sems = refs[2 * na], refs[2 * na + 1]
        x, y, c, chips = _place()
        k = 2 * x + y
        for a in range(na):
            half = w_refs[a].shape[0] // 2
            src = w_refs[a].at[pl.ds(c * half, half), :]
            for j, (cx, cy) in enumerate(chips):
                kk = 2 * cx + cy
                for pc in range(2):
                    _late_copy(a, j, pc, src, land_refs[a].at[k, pl.ds(c * half, half), :],
                               send_sems, recv_sems, (cx, cy, pc), c).wait_send()
                    pltpu.make_async_remote_copy(
                        src_ref=src, dst_ref=land_refs[a].at[kk, pl.ds(pc * half, half), :],
                        send_sem=send_sems.at[LATE_COPIES * a + 2 * j + pc],
                        recv_sem=recv_sems.at[LATE_COPIES * a + 2 * j + pc],
                        device_id=(cx, cy, pc), device_id_type=MESH).wait_recv()

    out = pl.pallas_call(
        body, name="late_weights_gather_wait",
        in_specs=[HBM_SPEC] * (2 * na) + [SEM_SPEC, SEM_SPEC, pl.BlockSpec(memory_space=pl.ANY)],
        out_specs=[HBM_SPEC] * (2 * na),
        out_shape=[pltpu.HBM(v.shape, v.dtype) for v in bufs],
        input_output_aliases={a: a for a in range(2 * na)},
        compiler_params=pltpu.CompilerParams(
            has_side_effects=pltpu.SideEffectType.DATAFLOW_SIDE_EFFECTING),
    )(*bufs, send_sems, recv_sems, after)
    return out[na:]


def _split_start(name, srcs, land_shapes, n_sems, plan):
    na = len(srcs)

    def body(*refs):
        sends, _ = plan(refs[:na], refs[na:2 * na], refs[2 * na], refs[2 * na + 1])
        for cp in sends:
            cp.start()
        refs[-1][...] = jnp.zeros_like(refs[-1])

    lands = [pltpu.with_memory_space_constraint(lax.empty(shape, dtype), pltpu.HBM)
             for shape, dtype in land_shapes]
    srcs = [pltpu.with_memory_space_constraint(v, pltpu.HBM) for v in srcs]
    sems = pltpu.SemaphoreType.DMA((n_sems,))
    return pl.pallas_call(
        body, name=name,
        in_specs=[HBM_SPEC] * (2 * na),
        out_specs=[SEM_SPEC, SEM_SPEC] + [HBM_SPEC] * (2 * na) + [pl.BlockSpec(memory_space=pltpu.VMEM)],
        out_shape=[sems, sems] + [pltpu.HBM(v.shape, v.dtype) for v in srcs + lands]
        + [jax.ShapeDtypeStruct((8, LANES), F32)],
        input_output_aliases={a: 2 + a for a in range(2 * na)},
        compiler_params=pltpu.CompilerParams(
            has_side_effects=pltpu.SideEffectType.DATAFLOW_SIDE_EFFECTING),
    )(*srcs, *lands)


def _split_wait(name, started, na, plan, after):
    bufs = started[2:2 + 2 * na]

    def body(*refs):
        sends, recvs = plan(refs[:na], refs[na:2 * na], refs[2 * na], refs[2 * na + 1])
        for cp in sends:
            cp.wait_send()
        for cp in recvs:
            cp.wait_recv()

    out = pl.pallas_call(
        body, name=name,
        in_specs=[HBM_SPEC] * (2 * na) + [SEM_SPEC, SEM_SPEC, pl.BlockSpec(memory_space=pl.ANY)],
        out_specs=[HBM_SPEC] * (2 * na),
        out_shape=[pltpu.HBM(v.shape, v.dtype) for v in bufs],
        input_output_aliases={a: a for a in range(2 * na)},
        compiler_params=pltpu.CompilerParams(
            has_side_effects=pltpu.SideEffectType.DATAFLOW_SIDE_EFFECTING),
    )(*bufs, started[0], started[1], after)
    return out[:na], out[na:]


def _plan_to_sibling(g_refs, r_refs, send_sems, recv_sems):
    x, y, c, _ = _place()
    cps = []
    for a, (g, r) in enumerate(zip(g_refs, r_refs)):
        half = g.shape[1] // 2
        cps.append(pltpu.make_async_remote_copy(
            src_ref=g.at[:, pl.ds((1 - c) * half, half), :], dst_ref=r, send_sem=send_sems.at[a],
            recv_sem=recv_sems.at[a], device_id=(x, y, 1 - c), device_id_type=MESH))
    return cps, cps


def _plan_scatter(p_refs, q_refs, send_sems, recv_sems):
    x, y, c, chips = _place()
    k = 2 * x + y
    sends, recvs = [], []
    for a, (p, q) in enumerate(zip(p_refs, q_refs)):
        for i, (cx, cy) in enumerate(chips):
            kk = 2 * cx + cy
            for dst, out in ((q.at[k], sends), (q.at[kk], recvs)):
                out.append(pltpu.make_async_remote_copy(
                    src_ref=p.at[kk], dst_ref=dst, send_sem=send_sems.at[3 * a + i],
                    recv_sem=recv_sems.at[3 * a + i], device_id=(cx, cy, c), device_id_type=MESH))
    return sends, recvs


def _swap_halves_to_sibling(gs):
    na = len(gs)

    def body(*refs):
        g_refs, r_refs = refs[:na], refs[na:2 * na]
        send_sems, recv_sems = refs[2 * na:]
        x, y, c, _ = _place()
        cps = []
        for a in range(na):
            half = g_refs[a].shape[1] // 2
            cps.append(pltpu.make_async_remote_copy(
                src_ref=g_refs[a].at[:, pl.ds((1 - c) * half, half), :], dst_ref=r_refs[a],
                send_sem=send_sems.at[a], recv_sem=recv_sems.at[a], device_id=(x, y, 1 - c),
                device_id_type=MESH))
        for cp in cps:
            cp.start()
        for cp in cps:
            cp.wait()

    return pl.pallas_call(
        body, name="grads_to_sibling",
        in_specs=[HBM_SPEC] * na, out_specs=[HBM_SPEC] * na,
        out_shape=[jax.ShapeDtypeStruct((g.shape[0], g.shape[1] // 2, g.shape[2]), g.dtype)
                   for g in gs],
        scratch_shapes=[pltpu.SemaphoreType.DMA((na,)), pltpu.SemaphoreType.DMA((na,))],
    )(*gs)


def _row_tile(rows):
    return next(t for t in (256, 128, 64) if rows % t == 0)


def _add_sibling(g, r, c_idx, out_dtype):
    n, rows, cols = g.shape
    half = rows // 2
    tr = _row_tile(half)
    nb = half // tr

    def body(c_ref, g_ref, r_ref, p_ref):
        p_ref[...] = (g_ref[...] + r_ref[...]).astype(out_dtype)

    return pl.pallas_call(
        body, name="grads_add_sibling",
        grid_spec=pltpu.PrefetchScalarGridSpec(
            num_scalar_prefetch=1, grid=(n, nb),
            in_specs=[pl.BlockSpec((1, tr, cols), lambda j, t, c_ref: (j, c_ref[0] * nb + t, 0)),
                      pl.BlockSpec((1, tr, cols), lambda j, t, c_ref: (j, t, 0))],
            out_specs=pl.BlockSpec((1, tr, cols), lambda j, t, c_ref: (j, t, 0))),
        out_shape=jax.ShapeDtypeStruct((n, half, cols), out_dtype),
        compiler_params=_params(("parallel", "parallel")),
    )(c_idx, g, r)


def _scatter_to_chips(ps):
    na = len(ps)

    def body(*refs):
        p_refs, q_refs = refs[:na], refs[na:2 * na]
        send_sems, recv_sems = refs[2 * na:]
        x, y, c, chips = _place()
        k = 2 * x + y
        sends = []
        for i, (cx, cy) in enumerate(chips):
            for a in range(na):
                cp = pltpu.make_async_remote_copy(
                    src_ref=p_refs[a].at[2 * cx + cy], dst_ref=q_refs[a].at[k],
                    send_sem=send_sems.at[3 * a + i], recv_sem=recv_sems.at[3 * a + i],
                    device_id=(cx, cy, c), device_id_type=MESH)
                cp.start()
                sends.append(cp)
        for i, (cx, cy) in enumerate(chips):
            kk = 2 * cx + cy
            for a in range(na):
                pltpu.make_async_remote_copy(
                    src_ref=p_refs[a].at[kk], dst_ref=q_refs[a].at[kk],
                    send_sem=send_sems.at[3 * a + i], recv_sem=recv_sems.at[3 * a + i],
                    device_id=(cx, cy, c), device_id_type=MESH).wait_recv()
        for cp in sends:
            cp.wait_send()

    return pl.pallas_call(
        body, name="grads_scatter_to_chips",
        in_specs=[HBM_SPEC] * na, out_specs=[HBM_SPEC] * na,
        out_shape=[jax.ShapeDtypeStruct(p.shape, p.dtype) for p in ps],
        scratch_shapes=[pltpu.SemaphoreType.DMA((3 * na,)), pltpu.SemaphoreType.DMA((3 * na,))],
    )(*ps)


def _sum_chips(q, c_idx):
    n, half, cols = q.shape
    tr = _row_tile(half)
    nb = half // tr

    def body(c_ref, q_ref, o_ref):
        parts = [q_ref[kk].astype(F32) for kk in range(n)]
        o_ref[...] = ((parts[0] + parts[1]) + parts[2]) + parts[3]

    return pl.pallas_call(
        body, name="grads_sum_chips",
        grid_spec=pltpu.PrefetchScalarGridSpec(
            num_scalar_prefetch=1, grid=(nb,),
            in_specs=[pl.BlockSpec((n, tr, cols), lambda t, c_ref: (0, t, 0))],
            out_specs=pl.BlockSpec((tr, cols), lambda t, c_ref: (c_ref[0] * nb + t, 0))),
        out_shape=jax.ShapeDtypeStruct((2 * half, cols), F32),
        compiler_params=_params(("parallel",)),
    )(c_idx, q)


def _join_halves(fulls):
    na = len(fulls)

    def body(*refs):
        o_refs = refs[na:2 * na]
        send_sems, recv_sems = refs[2 * na:]
        x, y, c, _ = _place()
        sends = []
        for a in range(na):
            half = o_refs[a].shape[0] // 2
            rows = o_refs[a].at[pl.ds(c * half, half), :]
            sends.append(pltpu.make_async_remote_copy(
                src_ref=rows, dst_ref=rows, send_sem=send_sems.at[a], recv_sem=recv_sems.at[a],
                device_id=(x, y, 1 - c), device_id_type=MESH))
        for cp in sends:
            cp.start()
        for a in range(na):
            half = o_refs[a].shape[0] // 2
            other = o_refs[a].at[pl.ds((1 - c) * half, half), :]
            pltpu.make_async_remote_copy(
                src_ref=other, dst_ref=other, send_sem=send_sems.at[a], recv_sem=recv_sems.at[a],
                device_id=(x, y, 1 - c), device_id_type=MESH).wait_recv()
        for cp in sends:
            cp.wait_send()

    return pl.pallas_call(
        body, name="grads_join_halves",
        in_specs=[HBM_SPEC] * na, out_specs=[HBM_SPEC] * na,
        out_shape=[jax.ShapeDtypeStruct(f.shape, f.dtype) for f in fulls],
        input_output_aliases={a: a for a in range(na)},
        scratch_shapes=[pltpu.SemaphoreType.DMA((na,)), pltpu.SemaphoreType.DMA((na,))],
    )(*fulls)


def _part_rows(shape):
    size = 1
    for d in shape:
        size *= d
    rows = -(-size // PACK_COLS)
    return size, -(-rows // PACK_ROW_ALIGN) * PACK_ROW_ALIGN


def _pack_rows(arrays, dtype, total_rows):
    parts, used = [], 0
    for a in arrays:
        size, rows = _part_rows(a.shape)
        flat = a.reshape(-1).astype(dtype)
        parts.append(jnp.pad(flat, (0, rows * PACK_COLS - size)).reshape(rows, PACK_COLS))
        used += rows
    parts.append(jnp.zeros((total_rows - used, PACK_COLS), dtype))
    return jnp.concatenate(parts, axis=0)


def _unpack_rows(buf, shapes):
    lead = buf.shape[:-2]
    out, off = [], 0
    for sh in shapes:
        size, rows = _part_rows(sh)
        part = buf[..., off:off + rows, :].reshape(lead + (-1,))[..., :size]
        out.append(part.reshape(lead + tuple(sh)))
        off += rows
    return out


NEW_ORDER = ((928, 1440), (1440, 1952), (1952, 2464), (416, 928), (2464, 2976), (3744, 4256),
             (2976, 3488), (0, 256), (256, 384), (4256, 4320), (384, 416), (4256, 4288),
             (3488, 3616), (3616, 3744))
OLD_ORDER = ((3584, 3840), (3840, 3968), (4032, 4064), (1536, 2048), (0, 512), (512, 1024),
             (1024, 1536), (2048, 2560), (3072, 3584), (4096, 4224), (4224, 4352), (2560, 3072))


def _cols(sources, ranges):
    parts = []
    for a, b in ranges:
        off = 0
        for src in sources:
            width = src.shape[-1]
            lo, hi = max(a, off), min(b, off + width)
            if lo < hi:
                parts.append(src[..., lo - off:hi - off])
            off += width
    return jnp.concatenate(parts, axis=-1)


def _sub_ranges(ranges, a, b):
    out, off = [], 0
    for lo, hi in ranges:
        width = hi - lo
        s0, s1 = max(a, off), min(b, off + width)
        if s0 < s1:
            out.append((lo + s0 - off, lo + s1 - off))
        off += width
    return out


def _rope_tables(s):
    half = MLA_ROPE // 2
    inv_freq = jnp.power(jnp.float32(ROPE_THETA), -jnp.arange(half, dtype=F32) / half)
    ang = inv_freq[:, None] * jnp.arange(s, dtype=F32)[None, :]
    cos, sin = jnp.cos(ang), jnp.sin(ang)
    z = lambda n: jnp.zeros((n, s), F32)
    c = jnp.concatenate([jnp.ones((MLA_NOPE, s), F32), cos, cos, z(32)], axis=0)
    s1 = jnp.concatenate([z(MLA_NOPE), -sin, z(16), z(32)], axis=0)
    s2 = jnp.concatenate([z(MLA_NOPE), z(16), sin, z(32)], axis=0)
    return c, s1, s2


def _pad_lanes(a, n):
    return jnp.pad(a, ((0, 0), (0, n - a.shape[1])))


SHARDED = ("w_in", "w_out", "mla_w_qb", "mla_w_kvb", "conv_w")
REPLICATED = ("norm_g", "mla_q_a_norm", "mla_kv_a_norm", "mla_q_norm", "mla_k_norm",
              "swa_q_norm", "swa_k_norm", "swa_sinks")
WEIGHT_ORDER = ("norm_g", "w_in", "mla_q_a_norm", "mla_w_qb", "mla_kv_a_norm", "mla_w_kvb",
                "mla_q_norm", "mla_k_norm", "conv_w", "swa_q_norm", "swa_k_norm", "swa_sinks", "w_out")
SHARD_AXIS = {"w_in": 2, "w_out": 1, "mla_w_qb": 2, "mla_w_kvb": 2, "conv_w": 2}


def kernel(x, norm_g, w_in, mla_q_a_norm, mla_w_qb, mla_kv_a_norm, mla_w_kvb, mla_q_norm, mla_k_norm, conv_w, swa_q_norm, swa_k_norm, swa_sinks, w_out, loss_target, m_norm_g, m_w_in, m_mla_q_a_norm, m_mla_w_qb, m_mla_kv_a_norm, m_mla_w_kvb, m_mla_q_norm, m_mla_k_norm, m_conv_w, m_swa_q_norm, m_swa_k_norm, m_swa_sinks, m_w_out, v_norm_g, v_w_in, v_mla_q_a_norm, v_mla_w_qb, v_mla_kv_a_norm, v_mla_w_kvb, v_mla_q_norm, v_mla_k_norm, v_conv_w, v_swa_q_norm, v_swa_k_norm, v_swa_sinks, v_w_out):
    weights = dict(norm_g=norm_g, w_in=w_in, mla_q_a_norm=mla_q_a_norm, mla_w_qb=mla_w_qb,
                   mla_kv_a_norm=mla_kv_a_norm, mla_w_kvb=mla_w_kvb, mla_q_norm=mla_q_norm,
                   mla_k_norm=mla_k_norm, conv_w=conv_w, swa_q_norm=swa_q_norm,
                   swa_k_norm=swa_k_norm, swa_sinks=swa_sinks, w_out=w_out)
    mom_m = dict(norm_g=m_norm_g, w_in=m_w_in, mla_q_a_norm=m_mla_q_a_norm, mla_w_qb=m_mla_w_qb,
                 mla_kv_a_norm=m_mla_kv_a_norm, mla_w_kvb=m_mla_w_kvb, mla_q_norm=m_mla_q_norm,
                 mla_k_norm=m_mla_k_norm, conv_w=m_conv_w, swa_q_norm=m_swa_q_norm,
                 swa_k_norm=m_swa_k_norm, swa_sinks=m_swa_sinks, w_out=m_w_out)
    mom_v = dict(norm_g=v_norm_g, w_in=v_w_in, mla_q_a_norm=v_mla_q_a_norm, mla_w_qb=v_mla_w_qb,
                 mla_kv_a_norm=v_mla_kv_a_norm, mla_w_kvb=v_mla_w_kvb, mla_q_norm=v_mla_q_norm,
                 mla_k_norm=v_mla_k_norm, conv_w=v_conv_w, swa_q_norm=v_swa_q_norm,
                 swa_k_norm=v_swa_k_norm, swa_sinks=v_swa_sinks, w_out=v_w_out)
    xs = x[0]
    target = loss_target[0]
    s = xs.shape[0]
    c_idx = lax.axis_index("c").astype(jnp.int32).reshape(1)
    k_idx = (2 * lax.axis_index("x") + lax.axis_index("y")).astype(jnp.int32).reshape(1)

    conv_bits = lax.bitcast_convert_type(conv_w, BF16)
    small_list = [mla_w_qb, mla_w_kvb, conv_bits]
    shard_cols = w_in.shape[2]
    w_in_b = w_in.astype(BF16)
    late = [w_in_b[1], w_out.astype(BF16).reshape(-1, D_MODEL)]
    started = _gather_start(late)
    own = [w_in_b[0], _pack_rows(small_list, BF16, SMALL_ROWS)]
    gathered_in0, gathered_small = [_fill_own_slab(buf, src, k_idx)
                                    for buf, src in zip(_all_gather(own), own)]
    parts = _unpack_rows(gathered_small, [a.shape for a in small_list])
    join = lambda p, axis: jnp.concatenate([p[k] for k in range(N_CHIPS)], axis=axis)
    w_in_zeros = jnp.zeros((D_MODEL, 64), BF16)
    permuted = lambda slabs: _cols([slabs[k] for k in range(N_CHIPS)] + [w_in_zeros], NEW_ORDER)
    w_qb_full = join(parts[0], 2)
    w_kvb_full = join(parts[1], 2)
    conv_full = lax.bitcast_convert_type(join(parts[2], 2), F32)

    rope = _rope_tables(s)
    swa_tables = _swa_tables()
    layers = []
    for l in range(DEPTH):
        wq = jnp.pad(w_qb_full[l].reshape(MLA_Q_LORA, MLA_HEADS, MLA_QK),
                     ((0, 0), (0, 0), (0, LANES - MLA_QK))).reshape(MLA_Q_LORA, MLA_HEADS * LANES)
        kv = w_kvb_full[l].reshape(MLA_KV_LORA, MLA_HEADS, MLA_NOPE + MLA_V)
        wk = jnp.pad(kv[:, :, :MLA_NOPE], ((0, 0), (0, 0), (0, LANES - MLA_NOPE)))
        wkv = jnp.concatenate([wk.reshape(MLA_KV_LORA, MLA_HEADS * LANES),
                               kv[:, :, MLA_NOPE:].reshape(MLA_KV_LORA, MLA_HEADS * MLA_V)], axis=1)
        layers.append(dict(
            wq=wq, wkv=wkv,
            cw=jnp.pad(conv_full[l], ((0, HALO - 3), (0, 0))),
            g=norm_g[l][None],
            mla_norms=(mla_q_a_norm[l][:, None], mla_kv_a_norm[l][:, None],
                       _pad_lanes(mla_q_norm[l][None], LANES).T, _pad_lanes(mla_k_norm[l][None], LANES).T),
            mla_weights=(wq, wkv, wq.T, wkv.T),
            sqn=swa_q_norm[l][:, None], skn=swa_k_norm[l][:, None], sinks=swa_sinks[l][None]))

    saved = []
    h_in = xs
    layers[0]["w_in"] = permuted(gathered_in0)
    layers[0]["g"] = layers[0]["g"] + started[-1][0:1, 0:1]
    for l in range(DEPTH):
        p = layers[l]
        proj, hb = _in_proj_fwd(h_in, p["g"], p["w_in"])
        q, k, v, qt, kt, vt = _mla_prep_fwd(proj, p["mla_norms"], p["mla_weights"], rope)
        o_mla, lse = _mla_attn_fwd(q, k, vt)
        if l == 0:
            late_in1, late_out = [_fill_own_slab(buf, src, k_idx) for buf, src in
                                  zip(_gather_wait(started, len(late), o_mla), late)]
            layers[1]["w_in"] = permuted(late_in1)
            w_out_full = join(late_out.reshape(N_CHIPS, DEPTH, -1, D_MODEL), 1)
            for n in range(DEPTH):
                layers[n]["w_out"] = w_out_full[n]
        o_swa = _swa_fwd(proj, p["sqn"], p["skn"], p["sinks"], swa_tables)
        last = l == DEPTH - 1
        y, z, *loss_acc = _out_fwd(proj, o_mla, o_swa, h_in, p["w_out"], p["cw"],
                                   target if last else None)
        saved.append(dict(x=h_in, proj=proj, hb=hb, q=q, k=k, v=v, qt=qt, kt=kt, o_mla=o_mla, lse=lse,
                          o_swa=o_swa, z=z))
        h_in = y

    dy, loss_acc = h_in, loss_acc[0]
    loss = lax.psum(loss_acc[0, 0], ("x", "y", "c"))

    grads = {n: [None] * DEPTH for n in WEIGHT_ORDER}

    def in_chunks(l):
        return jnp.stack([_cols(grads["w_in"][l], _sub_ranges(OLD_ORDER, k * shard_cols,
                                                              (k + 1) * shard_cols))
                          for k in range(N_CHIPS)])

    def out_chunks(l):
        return grads["w_out"][l].reshape(N_CHIPS, -1, D_MODEL)

    for l in reversed(range(DEPTH)):
        p, a = layers[l], saved[l]
        dconv, dgates, do_mla, do_mla_t, delta, do_swa, dcw = _out_bwd(dy, a["proj"], a["o_mla"], a["o_swa"],
                                                             p["w_out"], p["cw"])
        grads["w_out"][l] = _weight_grads(a["z"], [dy], "dw_out")[0]
        grads["conv_w"][l] = dcw[0:3]
        delta_rows = jnp.transpose(delta, (1, 0)).reshape(MLA_HEADS // 2, 2, s)
        dq, dk, dv = _mla_attn_bwd(a["q"], a["qt"], a["k"], a["kt"], a["v"], do_mla, do_mla_t,
                                   a["lse"], delta_rows)
        mla_norms = p["mla_norms"]
        if l == 0:
            late_gs, from_sib = _split_wait("late_grads_to_sibling_wait", late_st1, 2,
                                            _plan_to_sibling, dq)
            late_p = [_add_sibling(g, r, c_idx, BF16) for g, r in zip(late_gs, from_sib)]
            late_st2 = _split_start("late_grads_scatter_start", late_p,
                                    [(v.shape, v.dtype) for v in late_p], 6, _plan_scatter)
            mla_norms = (mla_norms[0] + late_st2[-1][0:1, 0:1],) + tuple(mla_norms[1:])
        dmla, dqan, dkvan, dqn, dkn, dwq_t, dwkv_t = _mla_prep_bwd(
            a["proj"], mla_norms, p["mla_weights"], rope, dq, dk, dv)
        dwq, dwkv = dwq_t.T, dwkv_t.T
        dsq, dskv, dsqn, dskn, dsinks = _swa_bwd(a["proj"], p["sqn"], p["skn"], p["sinks"], swa_tables, do_swa)
        if l == 0:
            late_p, late_q = _split_wait("late_grads_scatter_wait", late_st2, 2, _plan_scatter, dsq)
            late_full = [_sum_chips(_fill_own_slab(q_, p_, k_idx), c_idx)
                         for q_, p_ in zip(late_q, late_p)]
        pieces = [dconv, dgates, dsq, dmla, dskv]
        grads["w_in"][l] = _weight_grads(a["hb"], pieces, "dw_in")
        gain = p["g"]
        if l == 0:
            gs0 = (in_chunks(0), out_chunks(0))
            last_p = [_add_sibling(g, r, c_idx, BF16)
                      for g, r in zip(gs0, _swap_halves_to_sibling(gs0))]
            last_st = _split_start("last_grads_scatter_start", last_p,
                                   [(v.shape, v.dtype) for v in last_p], 6, _plan_scatter)
            gain = gain + last_st[-1][0:1, 0:1]
        dx, dg = _in_proj_bwd(pieces, a["x"], gain, p["w_in"], dy)
        if l == 0:
            last_p, last_q = _split_wait("last_grads_scatter_wait", last_st, 2, _plan_scatter, dx)
            last_full = [_sum_chips(_fill_own_slab(q_, p_, k_idx), c_idx)
                         for q_, p_ in zip(last_q, last_p)]
        grads["norm_g"][l] = dg[0]
        grads["mla_q_a_norm"][l] = dqan[:, 0]
        grads["mla_kv_a_norm"][l] = dkvan[:, 0]
        grads["mla_q_norm"][l] = dqn[:MLA_QK, 0]
        grads["mla_k_norm"][l] = dkn[:MLA_QK, 0]
        grads["mla_w_qb"][l] = dwq.reshape(MLA_Q_LORA, MLA_HEADS, LANES)[:, :, :MLA_QK].reshape(
            MLA_Q_LORA, MLA_HEADS * MLA_QK)
        dwk = dwkv[:, :MLA_HEADS * LANES].reshape(MLA_KV_LORA, MLA_HEADS, LANES)[:, :, :MLA_NOPE]
        dwv = dwkv[:, MLA_HEADS * LANES:].reshape(MLA_KV_LORA, MLA_HEADS, MLA_V)
        grads["mla_w_kvb"][l] = jnp.concatenate([dwk, dwv], axis=2).reshape(
            MLA_KV_LORA, MLA_HEADS * (MLA_NOPE + MLA_V))
        grads["swa_q_norm"][l] = dsqn[:, 0]
        grads["swa_k_norm"][l] = dskn[:, 0]
        grads["swa_sinks"][l] = dsinks[0]
        dy = dx
        if l == DEPTH - 1:
            late_st1 = _split_start(
                "late_grads_to_sibling_start", [in_chunks(l), out_chunks(l)],
                [((N_CHIPS, D_MODEL // 2, shard_cols), F32),
                 ((N_CHIPS, D_MIX // N_CHIPS // 2, D_MODEL), F32)], 2, _plan_to_sibling)
    grad_x = dy[None]
    full_grads = {n: jnp.stack(grads[n]) for n in WEIGHT_ORDER if n not in ("w_in", "w_out")}

    rest = tuple(n for n in SHARDED if n not in ("w_in", "w_out"))
    rep_shapes = [weights[n].shape for n in REPLICATED]
    rep_grads = jnp.concatenate([full_grads[n].reshape(-1) for n in REPLICATED])

    def chunk(g, n, k):
        width = g.shape[SHARD_AXIS[n]] // N_CHIPS
        return lax.slice_in_dim(g, k * width, (k + 1) * width, axis=SHARD_AXIS[n])

    g_small = jnp.stack([_pack_rows([chunk(full_grads[n], n, k) for n in rest] + [rep_grads],
                                    F32, SMALL_ROWS) for k in range(N_CHIPS)])
    gs = (g_small,)
    partial = [_add_sibling(g, r, c_idx, F32) for g, r in zip(gs, _swap_halves_to_sibling(gs))]
    by_chip = [_fill_own_slab(q, p, k_idx) for q, p in zip(_scatter_to_chips(partial), partial)]
    g_small_mine, g_in0, g_out0, g_in1, g_out1 = _join_halves(
        [_sum_chips(q, c_idx) for q in by_chip] + last_full + late_full)

    vals = _unpack_rows(g_small_mine, [weights[n].shape for n in rest] + [(rep_grads.shape[0],)])
    grad = dict(zip(rest, vals[:-1]))
    grad["w_in"] = jnp.stack([g_in0, g_in1])
    grad["w_out"] = jnp.stack([g_out0, g_out1])
    off = 0
    for n, sh in zip(REPLICATED, rep_shapes):
        grad[n] = vals[-1][off:off + sh[0] * sh[1]].reshape(sh)
        off += sh[0] * sh[1]
    results = {}
    for n in ("w_in", "w_out"):
        view = lambda a, n=n: a.reshape(-1, weights[n].shape[-1])
        res = _adamw(view(grad[n]), view(weights[n]), view(mom_m[n]), view(mom_v[n]))
        results[n] = [r.reshape(weights[n].shape) for r in res]
    small = tuple(n for n in WEIGHT_ORDER if n not in results)
    res = _adamw_small(*([d[n] for n in small] for d in (grad, weights, mom_m, mom_v)))
    for a, n in enumerate(small):
        results[n] = [res[kind][a] for kind in range(3)]
    unpacked = [grad] + [{n: results[n][kind] for n in WEIGHT_ORDER} for kind in range(3)]
    outs = [loss, grad_x]
    for group in unpacked:
        outs += [group[n] for n in WEIGHT_ORDER]
    return tuple(outs)
```

```python
import jax
import numpy as np
import jax.numpy as jnp
from jax import lax
from jax.experimental import pallas as pl
from jax.experimental.pallas import tpu as pltpu

F32 = jnp.float32
BF16 = jnp.bfloat16

D_MODEL = 1024
DEPTH = 2
GROUP = 512
D_MIX = 3 * GROUP
BLOCK = 128
RMS_EPS = 1e-6
NEG_INF = -1e30
MLA_HEADS = 8
MLA_QK = 96
MLA_NOPE = 64
MLA_ROPE = 32
MLA_V = 64
V_AUG = 80
MLA_Q_LORA = 256
MLA_KV_LORA = 128
ROPE_THETA = 10000.0
SWA_HEADS = 8
SWA_KV = 2
SWA_GROUP = 4
SWA_DIM = 64
N_CHIPS = 4

NC = 4352
OFF_SQ, OFF_MLA, OFF_SKV = 3072, 3584, 4096

VMEM_LIMIT = 56 * 1024 * 1024
LANES = 128
PACK_COLS = 1024
PACK_ROW_ALIGN = 16
SMALL_ROWS = 256

ADAM_LR = 0.001
ADAM_B1 = 0.9
ADAM_B2 = 0.999
ADAM_EPS = 1e-08
ADAM_WD = 0.01
ADAM_STEP = 10

MESH = pl.DeviceIdType.MESH


def _params(sem, vmem=VMEM_LIMIT):
    return pltpu.CompilerParams(dimension_semantics=sem, vmem_limit_bytes=vmem)


def _dot(a, b, dims):
    return lax.dot_general(a.astype(BF16), b.astype(BF16), (dims, ((), ())),
                           preferred_element_type=F32)


def _mm(a, b):
    return _dot(a, b, ((1,), (0,)))


def _mm_nt(a, b):
    return _dot(a, b, ((1,), (1,)))


def _rms(x, g, n=None):
    n = x.shape[-1] if n is None else n
    ms = jnp.sum(x * x, axis=-1, keepdims=True) * (1.0 / n)
    return x * lax.rsqrt(ms + RMS_EPS) * g


def _sigmoid(x):
    return 1.0 / (1.0 + jnp.exp(-x))


def _in_proj_fwd(x, g, w):
    s = x.shape[0]
    tm = min(512, s)

    def body(x_ref, g_ref, w_ref, proj_ref, hbt_ref):
        h = _rms(x_ref[...], g_ref[...])
        hbt_ref[...] = jnp.transpose(h).astype(BF16)
        proj_ref[...] = jnp.dot(h.astype(BF16), w_ref[...], preferred_element_type=F32)

    return pl.pallas_call(
        body, name="in_proj_fwd", grid=(s // tm,),
        in_specs=[pl.BlockSpec((tm, D_MODEL), lambda i: (i, 0)),
                  pl.BlockSpec((1, D_MODEL), lambda i: (0, 0)),
                  pl.BlockSpec((D_MODEL, NC), lambda i: (0, 0))],
        out_specs=[pl.BlockSpec((tm, NC), lambda i: (i, 0)),
                   pl.BlockSpec((D_MODEL, tm), lambda i: (0, i))],
        out_shape=[jax.ShapeDtypeStruct((s, NC), F32), jax.ShapeDtypeStruct((D_MODEL, s), BF16)],
        compiler_params=_params(("parallel",)),
    )(x, g, w)


def _in_proj_bwd(pieces, x, g, w, dres):
    s = x.shape[0]
    tm = min(512, s)
    n_p = len(pieces)

    def body(*refs):
        p_refs = refs[:n_p]
        x_ref, g_ref, w_ref, dres_ref, dx_ref, dg_ref = refs[n_p:]
        dh = None
        off = 0
        for r in p_refs:
            width = r.shape[1]
            t = _mm_nt(r[...], w_ref[:, off:off + width])
            dh = t if dh is None else dh + t
            off += width
        _, vjp = jax.vjp(_rms, x_ref[...], g_ref[...])
        dx, dg = vjp(dh)
        dx_ref[...] = dx + dres_ref[...]

        @pl.when(pl.program_id(0) == 0)
        def _():
            dg_ref[...] = jnp.zeros_like(dg_ref)

        dg_ref[...] += dg

    in_specs = [pl.BlockSpec((tm, p.shape[1]), lambda i: (i, 0)) for p in pieces]
    in_specs += [pl.BlockSpec((tm, D_MODEL), lambda i: (i, 0)),
                 pl.BlockSpec((1, D_MODEL), lambda i: (0, 0)),
                 pl.BlockSpec((D_MODEL, NC), lambda i: (0, 0)),
                 pl.BlockSpec((tm, D_MODEL), lambda i: (i, 0))]
    return pl.pallas_call(
        body, name="in_proj_bwd", grid=(s // tm,),
        in_specs=in_specs,
        out_specs=[pl.BlockSpec((tm, D_MODEL), lambda i: (i, 0)),
                   pl.BlockSpec((1, D_MODEL), lambda i: (0, 0))],
        out_shape=[jax.ShapeDtypeStruct((s, D_MODEL), F32), jax.ShapeDtypeStruct((1, D_MODEL), F32)],
        compiler_params=_params(("arbitrary",)),
    )(*pieces, x, g, w, dres)


def _weight_grads(at, bs, name):
    m, s = at.shape
    nb = len(bs)
    tk = min(512, s)

    def body(a_ref, *refs):
        b_refs, o_refs = refs[:nb], refs[nb:]

        @pl.when(pl.program_id(0) == 0)
        def _():
            for o_ref in o_refs:
                o_ref[...] = jnp.zeros_like(o_ref)

        a = a_ref[...]
        for b_ref, o_ref in zip(b_refs, o_refs):
            o_ref[...] += _mm(a, b_ref[...])

    return pl.pallas_call(
        body, name=name, grid=(s // tk,),
        in_specs=[pl.BlockSpec((m, tk), lambda k: (0, k))]
        + [pl.BlockSpec((tk, b.shape[1]), lambda k: (k, 0)) for b in bs],
        out_specs=[pl.BlockSpec((m, b.shape[1]), lambda k: (0, 0)) for b in bs],
        out_shape=[jax.ShapeDtypeStruct((m, b.shape[1]), F32) for b in bs],
        compiler_params=_params(("arbitrary",)),
    )(at, *bs)


def _rms0(x, g, n=None):
    n = x.shape[0] if n is None else n
    ms = jnp.sum(x * x, axis=0, keepdims=True) * (1.0 / n)
    return x * lax.rsqrt(ms + RMS_EPS) * g


@jax.custom_vjp
def _rope0(t, c, s1, s2):
    return t * c + pltpu.roll(t, LANES - 16, 0) * s1 + pltpu.roll(t, 16, 0) * s2


def _rope0_fwd(t, c, s1, s2):
    return _rope0(t, c, s1, s2), (c, s1, s2)


def _rope0_bwd(res, g):
    c, s1, s2 = res
    dt = g * c + pltpu.roll(g * s1, 16, 0) + pltpu.roll(g * s2, LANES - 16, 0)
    return dt, jnp.zeros_like(c), jnp.zeros_like(s1), jnp.zeros_like(s2)


_rope0.defvjp(_rope0_fwd, _rope0_bwd)


@jax.custom_vjp
def _mmw(w, wt, x):
    return _mm(w, x)


def _mmw_fwd(w, wt, x):
    return _mm(w, x), (wt, x)


def _mmw_bwd(res, g):
    wt, x = res
    return _mm_nt(g, x), jnp.zeros_like(wt), _mm(wt, g)


_mmw.defvjp(_mmw_fwd, _mmw_bwd)


def _prep_fn(q_lat, kv_lat, kr, qan, kvan, qn, kn, wq, wk, wv, wqt, wkt, wvt, c, s1, s2, mm):
    tokens = q_lat.shape[1]
    rq = _rms0(q_lat, qan)
    rkv = _rms0(kv_lat, kvan)
    qn_b = jnp.broadcast_to(qn, (LANES, tokens))
    kn_b = jnp.broadcast_to(kn, (LANES, tokens))
    qs, ks = [], []
    for h in range(MLA_HEADS):
        qs.append(_rope0(_rms0(mm(wq[h], wqt[h], rq), qn_b, MLA_QK), c, s1, s2))
        ks.append(_rope0(_rms0(mm(wk[h], wkt[h], rkv) + kr, kn_b, MLA_QK), c, s1, s2))
    return tuple(qs), tuple(ks), mm(wv, wvt, rkv)


def _prep_weights(wq_ref, wkv_ref, wqt_ref, wkvt_ref):
    heads = range(MLA_HEADS)
    wq = tuple(wqt_ref[LANES * h:LANES * (h + 1), :].astype(F32) for h in heads)
    wk = tuple(wkvt_ref[LANES * h:LANES * (h + 1), :].astype(F32) for h in heads)
    wv = wkvt_ref[LANES * MLA_HEADS:, :].astype(F32)
    wqt = tuple(wq_ref[:, LANES * h:LANES * (h + 1)].astype(F32) for h in heads)
    wkt = tuple(wkv_ref[:, LANES * h:LANES * (h + 1)].astype(F32) for h in heads)
    wvt = wkv_ref[:, LANES * MLA_HEADS:].astype(F32)
    return wq, wk, wv, wqt, wkt, wvt


def _prep_in_specs(tm):
    const = lambda shape: pl.BlockSpec(shape, lambda i: (0, 0))
    col = lambda height: pl.BlockSpec((height, tm), lambda i: (0, i))
    return [pl.BlockSpec((tm, 512), lambda i: (i, OFF_MLA // 512)),
            const((MLA_Q_LORA, 1)), const((MLA_KV_LORA, 1)), const((LANES, 1)), const((LANES, 1)),
            const((MLA_Q_LORA, 1024)), const((MLA_KV_LORA, 1536)),
            const((1024, MLA_Q_LORA)), const((1536, MLA_KV_LORA)),
            col(LANES), col(LANES), col(LANES)]


def _prep_operands(blk_ref, refs):
    qan_ref, kvan_ref, qn_ref, kn_ref, wq_ref, wkv_ref, wqt_ref, wkvt_ref, c_ref, s1_ref, s2_ref = refs
    blk_t = jnp.transpose(blk_ref[...])
    diff = (blk_t[0:256], blk_t[256:384], blk_t[384:512],
            qan_ref[...], kvan_ref[...], qn_ref[...], kn_ref[...])
    weights = _prep_weights(wq_ref, wkv_ref, wqt_ref, wkvt_ref)
    return diff, weights, (c_ref[...], s1_ref[...], s2_ref[...])


def _mla_prep_fwd(proj, norms, weights, rope):
    s = proj.shape[0]
    tm = min(512, s)

    def body(blk_ref, *refs):
        ins, (q_ref, k_ref, v_ref, qt_ref, kt_ref, vt_ref) = refs[:11], refs[11:]
        diff, (wq, wk, wv, wqt, wkt, wvt), tables = _prep_operands(blk_ref, ins)
        qs, ks, v = _prep_fn(*diff, wq, wk, wv, wqt, wkt, wvt, *tables,
                             lambda w, wt, x: _mm(w, x))
        for h in range(MLA_HEADS):
            q2 = qs[h] * Q_PRESCALE
            qt_ref[LANES * h:LANES * (h + 1), :] = q2.astype(BF16)
            kt_ref[LANES * h:LANES * (h + 1), :] = ks[h].astype(BF16)
            q_ref[:, LANES * h:LANES * (h + 1)] = jnp.transpose(q2).astype(BF16)
            k_ref[:, LANES * h:LANES * (h + 1)] = jnp.transpose(ks[h]).astype(BF16)
        ones_row = (lax.broadcasted_iota(jnp.int32, (V_AUG - MLA_V, v.shape[1]), 0) == 0).astype(BF16)
        for h in range(MLA_HEADS):
            vt_ref[V_AUG * h:V_AUG * h + MLA_V, :] = v[MLA_V * h:MLA_V * (h + 1)].astype(BF16)
            vt_ref[V_AUG * h + MLA_V:V_AUG * (h + 1), :] = ones_row
        v_ref[...] = jnp.transpose(v).astype(BF16)

    row = lambda width: pl.BlockSpec((tm, width), lambda i: (i, 0))
    col = lambda height: pl.BlockSpec((height, tm), lambda i: (0, i))
    return pl.pallas_call(
        body, name="mla_prep_fwd", grid=(s // tm,),
        in_specs=_prep_in_specs(tm),
        out_specs=[row(1024), row(1024), row(512), col(1024), col(1024), col(MLA_HEADS * V_AUG)],
        out_shape=[jax.ShapeDtypeStruct((s, 1024), BF16), jax.ShapeDtypeStruct((s, 1024), BF16),
                   jax.ShapeDtypeStruct((s, 512), BF16), jax.ShapeDtypeStruct((1024, s), BF16),
                   jax.ShapeDtypeStruct((1024, s), BF16),
                   jax.ShapeDtypeStruct((MLA_HEADS * V_AUG, s), BF16)],
        compiler_params=_params(("parallel",)),
    )(proj, *norms, *weights, *rope)


def _mla_prep_bwd(proj, norms, weights, rope, dq, dk, dv):
    s = proj.shape[0]
    tm = min(512, s)

    def body(blk_ref, *refs):
        ins, (dq_ref, dk_ref, dv_ref) = refs[:11], refs[11:14]
        dblk_ref, dqan_ref, dkvan_ref, dqn_ref, dkn_ref, dwq_ref, dwkv_ref = refs[14:]
        diff, (wq, wk, wv, wqt, wkt, wvt), tables = _prep_operands(blk_ref, ins)

        def fn(q_lat, kv_lat, kr, qan, kvan, qn, kn, wq_, wk_, wv_):
            return _prep_fn(q_lat, kv_lat, kr, qan, kvan, qn, kn, wq_, wk_, wv_, wqt, wkt, wvt,
                            *tables, _mmw)

        _, vjp = jax.vjp(fn, *diff, wq, wk, wv)
        heads = range(MLA_HEADS)
        cts = (tuple(dq_ref[LANES * h:LANES * (h + 1), :] for h in heads),
               tuple(dk_ref[LANES * h:LANES * (h + 1), :] for h in heads), dv_ref[...])
        dq_lat, dkv_lat, dkr, dqan, dkvan, dqn, dkn, dwq_h, dwk_h, dwv = vjp(cts)
        dblk_ref[...] = jnp.transpose(
            jnp.concatenate([dq_lat, dkv_lat, dkr], axis=0)).astype(BF16)

        @pl.when(pl.program_id(0) == 0)
        def _():
            for r in (dqan_ref, dkvan_ref, dqn_ref, dkn_ref, dwq_ref, dwkv_ref):
                r[...] = jnp.zeros_like(r)

        dqan_ref[...] += dqan
        dkvan_ref[...] += dkvan
        dqn_ref[...] += dqn
        dkn_ref[...] += dkn
        for h in heads:
            dwq_ref[LANES * h:LANES * (h + 1), :] += dwq_h[h]
            dwkv_ref[LANES * h:LANES * (h + 1), :] += dwk_h[h]
        dwkv_ref[LANES * MLA_HEADS:, :] += dwv

    const = lambda shape: pl.BlockSpec(shape, lambda i: (0, 0))
    col = lambda height: pl.BlockSpec((height, tm), lambda i: (0, i))
    shapes = [(MLA_Q_LORA, 1), (MLA_KV_LORA, 1), (LANES, 1), (LANES, 1),
              (1024, MLA_Q_LORA), (1536, MLA_KV_LORA)]
    return pl.pallas_call(
        body, name="mla_prep_bwd", grid=(s // tm,),
        in_specs=_prep_in_specs(tm) + [col(1024), col(1024), col(512)],
        out_specs=[pl.BlockSpec((tm, 512), lambda i: (i, 0))] + [const(sh) for sh in shapes],
        out_shape=[jax.ShapeDtypeStruct((s, 512), BF16)]
        + [jax.ShapeDtypeStruct(sh, F32) for sh in shapes],
        compiler_params=_params(("arbitrary",)),
    )(proj, *norms, *weights, *rope, dq, dk, dv)


MLA_SCALE = MLA_QK ** -0.5


LOG2E = 1.4426950408889634
LN2 = 0.6931471805599453
Q_PRESCALE = MLA_SCALE * LOG2E
HEAD_GROUPS = ((0, 1),)


def _mla_attn_fwd(q2, k, vt):
    s = q2.shape[0]
    t = min(512, s)
    tk = min(128, s)
    nq = s // t
    r = t // tk

    def body(q_ref, k_ref, vt_ref, o_ref, lse_ref, acc_ref):
        i = pl.program_id(1)
        row = lax.broadcasted_iota(jnp.int32, (tk, t), 0)
        col = lax.broadcasted_iota(jnp.int32, (tk, t), 1)
        qh = [q_ref[:, LANES * hh:LANES * (hh + 1)] for hh in range(2)]
        acc_ref[...] = jnp.zeros_like(acc_ref)

        def scores(j, heads, diag=None):
            r0 = pl.multiple_of(j * tk, tk)
            q_lo = 0 if diag is None else diag * tk
            out = []
            for hh in heads:
                kc = k_ref[pl.ds(r0, tk), LANES * hh:LANES * (hh + 1)]
                sc = lax.dot_general(kc, qh[hh][q_lo:], (((1,), (1,)), ((), ())),
                                     preferred_element_type=F32)
                out.append(sc if diag is None
                           else jnp.where(row[:, :t - q_lo] <= col[:, :t - q_lo], sc, NEG_INF))
            return tuple(out)

        for heads in HEAD_GROUPS:
            stats = tuple(jnp.full((1, t), NEG_INF, F32) for _ in heads)

            def consume(j, scs, stats, q_lo=0, heads=heads):
                r0 = pl.multiple_of(j * tk, tk)
                out, ps, alphas = [], [], []
                for n, hh in enumerate(heads):
                    m_old = stats[n][:, q_lo:]
                    m_new = jnp.maximum(m_old, jnp.max(scs[n], axis=0, keepdims=True))
                    ps.append(jnp.exp2(scs[n] - m_new).astype(BF16))
                    alphas.append(jnp.exp2(m_old - m_new))
                    out.append(m_new if q_lo == 0
                               else jnp.concatenate([stats[n][:, :q_lo], m_new], axis=1))
                for n, hh in enumerate(heads):
                    vc = vt_ref[V_AUG * hh:V_AUG * (hh + 1), pl.ds(r0, tk)]
                    acc_ref[hh, :, q_lo:] = alphas[n] * acc_ref[hh, :, q_lo:] + jnp.dot(
                        vc, ps[n], preferred_element_type=F32)
                return tuple(out)

            def group(j0, stats, diag, heads=heads):
                scs = [scores(j0 + d, heads, d if diag else None) for d in range(r)]
                for d in range(r):
                    stats = consume(j0 + d, scs[d], stats, d * tk if diag else 0)
                return stats

            stats = group(r * i, stats, True)
            stats = lax.fori_loop(0, i, lambda j, st: group(r * j, st, False), stats)
            for n, hh in enumerate(heads):
                l = acc_ref[hh, MLA_V:MLA_V + 1, :]
                o_ref[:, MLA_V * hh:MLA_V * (hh + 1)] = jnp.transpose(acc_ref[hh, 0:MLA_V, :] / l)
                lse_ref[0, hh:hh + 1, :] = stats[n] + jnp.log2(l)

    return pl.pallas_call(
        body, name="mla_attn_fwd", grid=(MLA_HEADS // 2, nq),
        in_specs=[pl.BlockSpec((t, 256), lambda p, i: (i, p)),
                  pl.BlockSpec((s, 256), lambda p, i: (0, p)),
                  pl.BlockSpec((2 * V_AUG, s), lambda p, i: (p, 0))],
        out_specs=[pl.BlockSpec((t, 128), lambda p, i: (i, p)),
                   pl.BlockSpec((1, 2, t), lambda p, i: (p, 0, i))],
        out_shape=[jax.ShapeDtypeStruct((s, 512), F32),
                   jax.ShapeDtypeStruct((MLA_HEADS // 2, 2, s), F32)],
        scratch_shapes=[pltpu.VMEM((2, V_AUG, t), F32)],
        compiler_params=_params(("parallel", "arbitrary")),
    )(q2, k, vt)


def _mla_attn_bwd(q2, q2t, k, kt, v, do, dot, lse_rows, delta_rows):
    s = q2.shape[0]
    t = min(512, s)
    nq = s // t

    def body(q_ref, qt_ref, k_ref, kt_ref, v_ref, do_ref, dot_ref, lse_ref, dl_ref,
             dq_ref, dk_ref, dv_ref):
        j = pl.program_id(1)

        @pl.when(j == 0)
        def _():
            dq_ref[...] = jnp.zeros_like(dq_ref)

        dk_ref[...] = jnp.zeros_like(dk_ref)
        dv_ref[...] = jnp.zeros_like(dv_ref)
        row = lax.broadcasted_iota(jnp.int32, (t, t), 0)
        col = lax.broadcasted_iota(jnp.int32, (t, t), 1)
        causal_t = row <= col
        kh = [k_ref[:, LANES * hh:LANES * (hh + 1)] for hh in range(2)]
        kth = [kt_ref[LANES * hh:LANES * (hh + 1), :] for hh in range(2)]
        vh = [v_ref[:, MLA_V * hh:MLA_V * (hh + 1)] for hh in range(2)]
        nt = (((1,), (1,)), ((), ()))

        def step(i, masked):
            r0 = pl.multiple_of(i * t, t)
            sd = []
            for hh in range(2):
                qh = q_ref[pl.ds(r0, t), LANES * hh:LANES * (hh + 1)]
                doh = do_ref[pl.ds(r0, t), MLA_V * hh:MLA_V * (hh + 1)]
                sc_t = lax.dot_general(kh[hh], qh, nt, preferred_element_type=F32)
                sd.append(jnp.where(causal_t, sc_t, NEG_INF) if masked else sc_t)
                sd.append(lax.dot_general(vh[hh], doh, nt, preferred_element_type=F32))
            for hh in range(2):
                lse = lse_ref[0, hh:hh + 1, pl.ds(r0, t)]
                dl = dl_ref[0, hh:hh + 1, pl.ds(r0, t)]
                p_t = jnp.exp2(sd[2 * hh] - lse)
                g_t = (p_t * (sd[2 * hh + 1] - dl)).astype(BF16)
                qth = qt_ref[LANES * hh:LANES * (hh + 1), pl.ds(r0, t)]
                doth = dot_ref[MLA_V * hh:MLA_V * (hh + 1), pl.ds(r0, t)]
                dv_ref[MLA_V * hh:MLA_V * (hh + 1), :] += lax.dot_general(
                    doth, p_t.astype(BF16), nt, preferred_element_type=F32)
                dk_ref[LANES * hh:LANES * (hh + 1), :] += lax.dot_general(
                    qth, g_t, nt, preferred_element_type=F32)
                dq_ref[LANES * hh:LANES * (hh + 1), pl.ds(r0, t)] += jnp.dot(
                    kth[hh], g_t, preferred_element_type=F32)

        step(j, True)

        def trip(i, carry):
            step(i, False)
            return carry

        lax.fori_loop(j + 1, nq, trip, 0)
        dk_ref[...] = dk_ref[...] * LN2

        @pl.when(j == nq - 1)
        def _():
            dq_ref[...] = dq_ref[...] * MLA_SCALE

    return pl.pallas_call(
        body, name="mla_attn_bwd", grid=(MLA_HEADS // 2, nq),
        in_specs=[pl.BlockSpec((s, 256), lambda p, j: (0, p)),
                  pl.BlockSpec((256, s), lambda p, j: (p, 0)),
                  pl.BlockSpec((t, 256), lambda p, j: (j, p)),
                  pl.BlockSpec((256, t), lambda p, j: (p, j)),
                  pl.BlockSpec((t, 128), lambda p, j: (j, p)),
                  pl.BlockSpec((s, 128), lambda p, j: (0, p)),
                  pl.BlockSpec((128, s), lambda p, j: (p, 0)),
                  pl.BlockSpec((1, 2, s), lambda p, j: (p, 0, 0)),
                  pl.BlockSpec((1, 2, s), lambda p, j: (p, 0, 0))],
        out_specs=[pl.BlockSpec((256, s), lambda p, j: (p, 0)),
                   pl.BlockSpec((256, t), lambda p, j: (p, j)),
                   pl.BlockSpec((128, t), lambda p, j: (p, j))],
        out_shape=[jax.ShapeDtypeStruct((1024, s), F32), jax.ShapeDtypeStruct((1024, s), F32),
                   jax.ShapeDtypeStruct((512, s), F32)],
        compiler_params=_params(("parallel", "arbitrary")),
    )(q2, q2t, k, kt, v, do, dot, lse_rows, delta_rows)


SWA_SCALE = SWA_DIM ** -0.5
SWA_COLS = SWA_GROUP * BLOCK
SWA_LOG2 = SWA_SCALE * LOG2E


def _swa_tables():
    k = np.arange(2 * BLOCK)[:, None]
    col = np.arange(SWA_COLS)[None, :]
    dist = BLOCK + (col % BLOCK) - k
    valid = (dist >= 0) & (dist < BLOCK)
    out = np.zeros((2, SWA_KV, 2 * BLOCK, SWA_COLS), np.float32)
    for first in range(2):
        ok = valid & ((k >= BLOCK) | (first == 0))
        for j in range(SWA_KV):
            slope = 2.0 ** -(SWA_GROUP * j + col // BLOCK + 1)
            out[first, j] = np.where(ok, -slope * dist * LOG2E, NEG_INF)
    return jnp.asarray(out)


def _swa_tile_inputs(sq_ref, skv_ref, halo_ref, qn_ref, kn_ref, sk_ref, add_ref, first):
    tokens = sq_ref.shape[0]
    kv_all = jnp.concatenate([halo_ref[...], skv_ref[...]], axis=0)
    kv_t = jnp.transpose(kv_all)
    sq_t = jnp.transpose(sq_ref[...])
    k_raw = [kv_t[SWA_DIM * j:SWA_DIM * (j + 1)] for j in range(SWA_KV)]
    v_t = [kv_t[128 + SWA_DIM * j:128 + SWA_DIM * (j + 1)] for j in range(SWA_KV)]
    v_nat = [kv_all[:, 128 + SWA_DIM * j:128 + SWA_DIM * (j + 1)] for j in range(SWA_KV)]
    q_raw = [sq_t[SWA_DIM * h:SWA_DIM * (h + 1)] for h in range(SWA_HEADS)]
    qn_b = jnp.broadcast_to(qn_ref[...], (SWA_DIM, tokens))
    kn_b = jnp.broadcast_to(kn_ref[...], (SWA_DIM, tokens + BLOCK))
    lane_grp = lax.broadcasted_iota(jnp.int32, (1, SWA_COLS), 1) // BLOCK
    sinks, adds = [], []
    for j in range(SWA_KV):
        row = jnp.zeros((1, SWA_COLS), F32)
        for g in range(SWA_GROUP):
            h = SWA_GROUP * j + g
            row = jnp.where(lane_grp == g, sk_ref[:, h:h + 1] * LOG2E, row)
        sinks.append(row)
        adds.append((jnp.where(first, add_ref[1, j], add_ref[0, j]), add_ref[0, j]))
    return k_raw, v_t, v_nat, q_raw, qn_b, kn_b, sinks, adds


def _swa_probs(kb, qs_t, add, sink):
    s2 = jnp.dot(kb, qs_t, preferred_element_type=F32) * SWA_LOG2 + add
    m = jnp.maximum(jnp.max(s2, axis=0, keepdims=True), sink)
    e = jnp.exp2(s2 - m)
    es = jnp.exp2(sink - m)
    inv = 1.0 / (jnp.sum(e, axis=0, keepdims=True) + es)
    return e, inv, es


def _swa_queries(qn_t, j, b):
    return jnp.concatenate([qn_t[SWA_GROUP * j + g][:, BLOCK * b:BLOCK * (b + 1)]
                            for g in range(SWA_GROUP)], axis=1)


def _swa_fwd(proj, qn, kn, sinks, tables):
    s = proj.shape[0]
    ts = min(1024, s)
    nb = ts // BLOCK

    def body(sq_ref, skv_ref, halo_ref, qn_ref, kn_ref, sk_ref, add_ref, o_ref, ot_ref):
        first = pl.program_id(0) == 0
        k_raw, v_t, _, q_raw, qn_b, kn_b, sink_rows, adds = _swa_tile_inputs(
            sq_ref, skv_ref, halo_ref, qn_ref, kn_ref, sk_ref, add_ref, first)
        kn_nat = [jnp.transpose(_rms0(k, kn_b)).astype(BF16) for k in k_raw]
        v_t = [v.astype(BF16) for v in v_t]
        qn_t = [_rms0(q, qn_b).astype(BF16) for q in q_raw]
        for b in range(nb):
            band = slice(BLOCK * b, BLOCK * (b + 2))
            for j in range(SWA_KV):
                e, inv, _ = _swa_probs(kn_nat[j][band], _swa_queries(qn_t, j, b),
                                       adds[j][0 if b == 0 else 1], sink_rows[j])
                o_t = jnp.dot(v_t[j][:, band], (e * inv).astype(BF16),
                              preferred_element_type=F32)
                for g in range(SWA_GROUP):
                    h = SWA_GROUP * j + g
                    ot_ref[SWA_DIM * h:SWA_DIM * (h + 1), BLOCK * b:BLOCK * (b + 1)] = (
                        o_t[:, BLOCK * g:BLOCK * (g + 1)])
        o_ref[...] = jnp.transpose(ot_ref[...])

    const = lambda shape: pl.BlockSpec(shape, lambda i: (0,) * len(shape))
    return pl.pallas_call(
        body, name="swa_fwd", grid=(s // ts,),
        in_specs=[pl.BlockSpec((ts, 512), lambda i: (i, OFF_SQ // 512)),
                  pl.BlockSpec((ts, 256), lambda i: (i, OFF_SKV // 256)),
                  pl.BlockSpec((BLOCK, 256), lambda i: (jnp.maximum(i * nb - 1, 0), OFF_SKV // 256)),
                  const((SWA_DIM, 1)), const((SWA_DIM, 1)), const((1, SWA_HEADS)),
                  const(tables.shape)],
        out_specs=pl.BlockSpec((ts, 512), lambda i: (i, 0)),
        out_shape=jax.ShapeDtypeStruct((s, 512), F32),
        scratch_shapes=[pltpu.VMEM((512, ts), F32)],
        compiler_params=_params(("parallel",)),
    )(proj, proj, proj, qn, kn, sinks, tables)


def _swa_bwd(proj, qn, kn, sinks, tables, do):
    s = proj.shape[0]
    ts = min(1024, s)
    nb = ts // BLOCK
    nt = s // ts

    def body(sq_ref, skv_ref, halo_ref, qn_ref, kn_ref, sk_ref, add_ref, do_ref,
             dsq_ref, dskv_ref, dqn_ref, dkn_ref, dsk_ref, carry_ref, dqt_ref, dkvt_ref):
        step = pl.program_id(0)
        first = step == nt - 1

        @pl.when(step == 0)
        def _():
            carry_ref[...] = jnp.zeros_like(carry_ref)
            dqn_ref[...] = jnp.zeros_like(dqn_ref)
            dkn_ref[...] = jnp.zeros_like(dkn_ref)
            dsk_ref[...] = jnp.zeros_like(dsk_ref)

        k_raw, v_t, v_nat, q_raw, qn_b, kn_b, sink_rows, adds = _swa_tile_inputs(
            sq_ref, skv_ref, halo_ref, qn_ref, kn_ref, sk_ref, add_ref, first)
        kn_f = [_rms0(k, kn_b) for k in k_raw]
        kn_t = [k.astype(BF16) for k in kn_f]
        kn_nat = [jnp.transpose(k).astype(BF16) for k in kn_f]
        v_nat = [v.astype(BF16) for v in v_nat]
        qn_t = [_rms0(q, qn_b).astype(BF16) for q in q_raw]
        do_t = jnp.transpose(do_ref[...].astype(F32)).astype(BF16)

        dkvt_ref[...] = jnp.zeros_like(dkvt_ref)
        dsink = [jnp.zeros((1, SWA_COLS), F32) for _ in range(SWA_KV)]
        nt_dims = (((1,), (1,)), ((), ()))
        for b in range(nb):
            rows = slice(BLOCK * b, BLOCK * (b + 1))
            band = slice(BLOCK * b, BLOCK * (b + 2))
            for j in range(SWA_KV):
                heads = [SWA_GROUP * j + g for g in range(SWA_GROUP)]
                qs_t = _swa_queries(qn_t, j, b)
                dos_t = jnp.concatenate([do_t[SWA_DIM * h:SWA_DIM * (h + 1), rows] for h in heads],
                                        axis=1)
                e, inv, es = _swa_probs(kn_nat[j][band], qs_t, adds[j][0 if b == 0 else 1],
                                        sink_rows[j])
                p = e * inv
                dp = jnp.dot(v_nat[j][band], dos_t, preferred_element_type=F32)
                dsum = jnp.sum(p * dp, axis=0, keepdims=True)
                dsink[j] = dsink[j] - es * inv * dsum
                g_t = (p * (dp - dsum) * SWA_SCALE).astype(BF16)
                dv_t = lax.dot_general(dos_t, p.astype(BF16), nt_dims,
                                       preferred_element_type=F32)
                dk_t = lax.dot_general(qs_t, g_t, nt_dims, preferred_element_type=F32)
                dq_t = jnp.dot(kn_t[j][:, band], g_t, preferred_element_type=F32)
                dkvt_ref[SWA_DIM * j:SWA_DIM * (j + 1), band] += dk_t
                dkvt_ref[128 + SWA_DIM * j:128 + SWA_DIM * (j + 1), band] += dv_t
                for g, h in enumerate(heads):
                    dqt_ref[SWA_DIM * h:SWA_DIM * (h + 1), rows] = dq_t[:, BLOCK * g:BLOCK * (g + 1)]

        dqn = jnp.zeros((SWA_DIM, 1), F32)
        for h in range(SWA_HEADS):
            _, vjp = jax.vjp(_rms0, q_raw[h], qn_ref[...])
            dq, dg = vjp(dqt_ref[SWA_DIM * h:SWA_DIM * (h + 1), :])
            dqt_ref[SWA_DIM * h:SWA_DIM * (h + 1), :] = dq
            dqn = dqn + dg
        dqn_ref[...] += dqn
        dsq_ref[...] = jnp.transpose(dqt_ref[...]).astype(BF16)
        dkn = jnp.zeros((SWA_DIM, 1), F32)
        lane_grp = lax.broadcasted_iota(jnp.int32, (1, SWA_COLS), 1) // BLOCK
        for j in range(SWA_KV):
            _, vjp = jax.vjp(_rms0, k_raw[j], kn_ref[...])
            dk, dg = vjp(dkvt_ref[SWA_DIM * j:SWA_DIM * (j + 1), :])
            dkvt_ref[SWA_DIM * j:SWA_DIM * (j + 1), :] = dk
            dkn = dkn + dg
            for g in range(SWA_GROUP):
                h = SWA_GROUP * j + g
                dsk_ref[:, h:h + 1] += jnp.sum(jnp.where(lane_grp == g, dsink[j], 0.0), axis=1,
                                               keepdims=True)
        dkn_ref[...] += dkn
        dkv = jnp.transpose(dkvt_ref[...])
        dskv_ref[0:ts - BLOCK, :] = dkv[BLOCK:ts].astype(BF16)
        dskv_ref[ts - BLOCK:ts, :] = (dkv[ts:ts + BLOCK] + carry_ref[...]).astype(BF16)
        carry_ref[...] = dkv[0:BLOCK]

    const = lambda shape: pl.BlockSpec(shape, lambda st: (0,) * len(shape))
    return pl.pallas_call(
        body, name="swa_bwd", grid=(nt,),
        in_specs=[pl.BlockSpec((ts, 512), lambda st: (nt - 1 - st, OFF_SQ // 512)),
                  pl.BlockSpec((ts, 256), lambda st: (nt - 1 - st, OFF_SKV // 256)),
                  pl.BlockSpec((BLOCK, 256),
                               lambda st: (jnp.maximum((nt - 1 - st) * nb - 1, 0), OFF_SKV // 256)),
                  const((SWA_DIM, 1)), const((SWA_DIM, 1)), const((1, SWA_HEADS)),
                  const(tables.shape),
                  pl.BlockSpec((ts, 512), lambda st: (nt - 1 - st, 0))],
        out_specs=[pl.BlockSpec((ts, 512), lambda st: (nt - 1 - st, 0)),
                   pl.BlockSpec((ts, 256), lambda st: (nt - 1 - st, 0)),
                   const((SWA_DIM, 1)), const((SWA_DIM, 1)), const((1, SWA_HEADS))],
        out_shape=[jax.ShapeDtypeStruct((s, 512), BF16), jax.ShapeDtypeStruct((s, 256), BF16),
                   jax.ShapeDtypeStruct((SWA_DIM, 1), F32), jax.ShapeDtypeStruct((SWA_DIM, 1), F32),
                   jax.ShapeDtypeStruct((1, SWA_HEADS), F32)],
        scratch_shapes=[pltpu.VMEM((BLOCK, 256), F32), pltpu.VMEM((512, ts), F32),
                        pltpu.VMEM((256, ts + BLOCK), F32)],
        compiler_params=_params(("arbitrary",)),
    )(proj, proj, proj, qn, kn, sinks, tables, do)


HALO = 8


def _shift_down(u, halo, k):
    tm = u.shape[0]
    rid = lax.broadcasted_iota(jnp.int32, u.shape, 0)
    out = pltpu.roll(u, k, 0)
    for r in range(k):
        out = jnp.where(rid == r, halo[HALO - k + r:HALO - k + r + 1, :], out)
    return out


def _shift_up(u, halo, k):
    tm = u.shape[0]
    rid = lax.broadcasted_iota(jnp.int32, u.shape, 0)
    out = pltpu.roll(u, tm - k, 0)
    for r in range(k):
        out = jnp.where(rid == tm - k + r, halo[r:r + 1, :], out)
    return out


def _conv_fwd_vals(conv_ref, convp_ref, cw_ref, is_first):
    c_h, c_b, c_c = conv_ref[:, 0:512], conv_ref[:, 512:1024], conv_ref[:, 1024:1536]
    u = c_c * c_h
    up = jnp.where(is_first, 0.0, convp_ref[:, 1024:1536] * convp_ref[:, 0:512])
    u1 = _shift_down(u, up, 1)
    u2 = _shift_down(u, up, 2)
    yc = cw_ref[0:1, :] * u2 + cw_ref[1:2, :] * u1 + cw_ref[2:3, :] * u
    return c_h, c_b, c_c, u, u1, u2, yc


def _out_fwd(proj, o_mla, o_swa, x, w_out, cw, target=None):
    s = proj.shape[0]
    tm = min(512, s)
    nt = s // tm
    with_loss = target is not None

    def body(*refs):
        conv_ref, convp_ref, gates_ref, om_ref, os_ref, x_ref, w_ref, cw_ref = refs[:8]
        if with_loss:
            t_ref, y_ref, zt_ref, loss_ref, z_ref = refs[8:]
        else:
            y_ref, zt_ref, z_ref = refs[8:]
        i = pl.program_id(0)
        _, c_b, _, _, _, _, yc = _conv_fwd_vals(conv_ref, convp_ref, cw_ref, i == 0)
        mix = (om_ref[...], c_b * yc, os_ref[...])
        for n in range(3):
            g = gates_ref[:, GROUP * n:GROUP * (n + 1)]
            z = mix[n] * (g * _sigmoid(g))
            z_ref[:, GROUP * n:GROUP * (n + 1)] = z.astype(BF16)
            zt_ref[GROUP * n:GROUP * (n + 1), :] = jnp.transpose(z).astype(BF16)
        y = x_ref[...] + jnp.dot(z_ref[...], w_ref[...], preferred_element_type=F32)
        if not with_loss:
            y_ref[...] = y
            return
        err = y - t_ref[...]
        y_ref[...] = err * (1.0 / D_MODEL)

        @pl.when(i == 0)
        def _():
            loss_ref[...] = jnp.zeros_like(loss_ref)

        sq = jnp.sum((err * err).reshape(tm // 8, 8, D_MODEL), axis=0)
        part = sq[:, 0:LANES]
        for c in range(1, D_MODEL // LANES):
            part = part + sq[:, LANES * c:LANES * (c + 1)]
        loss_ref[...] += part

        @pl.when(i == nt - 1)
        def _():
            loss_ref[...] = jnp.full(loss_ref.shape, (0.5 / D_MODEL) * jnp.sum(loss_ref[...]), F32)

    row = lambda width: pl.BlockSpec((tm, width), lambda i: (i, 0))
    in_specs = [pl.BlockSpec((tm, 1536), lambda i: (i, 0)),
                pl.BlockSpec((HALO, 1536), lambda i: (jnp.maximum(i * (tm // HALO) - 1, 0), 0)),
                pl.BlockSpec((tm, 1536), lambda i: (i, 1)),
                row(512), row(512), row(D_MODEL),
                pl.BlockSpec((D_MIX, D_MODEL), lambda i: (0, 0)),
                pl.BlockSpec((HALO, 512), lambda i: (0, 0))]
    out_specs = [row(D_MODEL), pl.BlockSpec((D_MIX, tm), lambda i: (0, i))]
    out_shape = [jax.ShapeDtypeStruct((s, D_MODEL), F32), jax.ShapeDtypeStruct((D_MIX, s), BF16)]
    operands = [proj, proj, proj, o_mla, o_swa, x, w_out, cw]
    if with_loss:
        in_specs.append(row(D_MODEL))
        out_specs.append(pl.BlockSpec((8, LANES), lambda i: (0, 0)))
        out_shape.append(jax.ShapeDtypeStruct((8, LANES), F32))
        operands.append(target)
    return pl.pallas_call(
        body, name="out_fwd_loss" if with_loss else "out_fwd", grid=(nt,),
        in_specs=in_specs, out_specs=out_specs, out_shape=out_shape,
        scratch_shapes=[pltpu.VMEM((tm, D_MIX), BF16)],
        compiler_params=_params(("arbitrary",) if with_loss else ("parallel",)),
    )(*operands)


def _out_bwd(dy, proj, o_mla, o_swa, w_out, cw):
    s = proj.shape[0]
    tm = min(512, s)
    nt = s // tm
    hb = tm // HALO

    def body(dy_ref, dyn_ref, conv_ref, convp_ref, convn_ref, gates_ref, gatesn_ref, om_ref, os_ref,
             w_ref, cw_ref,
             dconv_ref, dgates_ref, dom_ref, domt_ref, delta_ref, dos_ref, dcw_ref):
        i = pl.program_id(0)
        dz = _mm_nt(dy_ref[...], w_ref[...])

        def gate(n):
            g = gates_ref[:, GROUP * n:GROUP * (n + 1)]
            sg = _sigmoid(g)
            return g * sg, sg * (1.0 + g * (1.0 - sg))

        for n, o_ref, do_ref in ((0, om_ref, dom_ref), (2, os_ref, dos_ref)):
            silu, dsilu = gate(n)
            dzn = dz[:, GROUP * n:GROUP * (n + 1)]
            o = o_ref[...]
            do = dzn * silu
            do_ref[...] = do.astype(do_ref.dtype)
            dgates_ref[:, GROUP * n:GROUP * (n + 1)] = (dzn * o * dsilu).astype(BF16)
            if n == 0:
                domt_ref[...] = jnp.transpose(do).astype(BF16)
                t = do * o
                for h in range(MLA_HEADS):
                    delta_ref[:, h:h + 1] = jnp.sum(t[:, MLA_V * h:MLA_V * (h + 1)], axis=-1,
                                                    keepdims=True)

        c_h, c_b, c_c, u, u1, u2, yc = _conv_fwd_vals(conv_ref, convp_ref, cw_ref, i == 0)
        silu, dsilu = gate(1)
        dzc = dz[:, GROUP:2 * GROUP]
        dgates_ref[:, GROUP:2 * GROUP] = (dzc * (c_b * yc) * dsilu).astype(BF16)
        dycr = dzc * silu
        dyc = dycr * c_b
        gn = gatesn_ref[:, GROUP:2 * GROUP]
        dzc_n = _mm_nt(dyn_ref[...], w_ref[GROUP:2 * GROUP, :])
        dyc_n = jnp.where(i == nt - 1, 0.0, dzc_n * (gn * _sigmoid(gn)) * convn_ref[:, 512:1024])
        d1 = _shift_up(dyc, dyc_n, 1)
        d2 = _shift_up(dyc, dyc_n, 2)
        du = cw_ref[2:3, :] * dyc + cw_ref[1:2, :] * d1 + cw_ref[0:1, :] * d2
        dconv_ref[:, 0:512] = (du * c_c).astype(BF16)
        dconv_ref[:, 512:1024] = (dycr * yc).astype(BF16)
        dconv_ref[:, 1024:1536] = (du * c_h).astype(BF16)

        @pl.when(i == 0)
        def _():
            dcw_ref[...] = jnp.zeros_like(dcw_ref)

        for k, uk in enumerate((u2, u1, u)):
            dcw_ref[k:k + 1, :] += jnp.sum(dyc * uk, axis=0, keepdims=True)

    row = lambda width: pl.BlockSpec((tm, width), lambda i: (i, 0))
    prev = lambda i: jnp.maximum(i * hb - 1, 0)
    nxt = lambda i: jnp.minimum((i + 1) * hb, s // HALO - 1)
    return pl.pallas_call(
        body, name="out_bwd", grid=(nt,),
        in_specs=[row(D_MODEL),
                  pl.BlockSpec((HALO, D_MODEL), lambda i: (nxt(i), 0)),
                  pl.BlockSpec((tm, 1536), lambda i: (i, 0)),
                  pl.BlockSpec((HALO, 1536), lambda i: (prev(i), 0)),
                  pl.BlockSpec((HALO, 1536), lambda i: (nxt(i), 0)),
                  pl.BlockSpec((tm, 1536), lambda i: (i, 1)),
                  pl.BlockSpec((HALO, 1536), lambda i: (nxt(i), 1)),
                  row(512), row(512),
                  pl.BlockSpec((D_MIX, D_MODEL), lambda i: (0, 0)),
                  pl.BlockSpec((HALO, 512), lambda i: (0, 0))],
        out_specs=[row(1536), row(1536), row(512), pl.BlockSpec((512, tm), lambda i: (0, i)),
                   row(MLA_HEADS), row(512), pl.BlockSpec((HALO, 512), lambda i: (0, 0))],
        out_shape=[jax.ShapeDtypeStruct((s, 1536), BF16), jax.ShapeDtypeStruct((s, 1536), BF16),
                   jax.ShapeDtypeStruct((s, 512), BF16), jax.ShapeDtypeStruct((512, s), BF16),
                   jax.ShapeDtypeStruct((s, MLA_HEADS), F32),
                   jax.ShapeDtypeStruct((s, 512), BF16), jax.ShapeDtypeStruct((HALO, 512), F32)],
        compiler_params=_params(("arbitrary",)),
    )(dy, dy, proj, proj, proj, proj, proj, o_mla, o_swa, w_out, cw)


def _adam_update(g, w, m, v):
    c1 = 1.0 - ADAM_B1
    c2 = 1.0 - ADAM_B2
    bc1 = 1.0 - ADAM_B1 ** ADAM_STEP
    bc2 = 1.0 - ADAM_B2 ** ADAM_STEP
    m_new = ADAM_B1 * m + c1 * g
    v_new = ADAM_B2 * v + c2 * (g * g)
    delta = -ADAM_LR * ((m_new / bc1) / (jnp.sqrt(v_new / bc2) + ADAM_EPS) + ADAM_WD * w)
    return delta, m_new, v_new


def _adamw(g, w, m, v):
    rows = g.shape[0]
    tr = min(256, rows)

    def body(g_ref, w_ref, m_ref, v_ref, d_ref, mo_ref, vo_ref):
        d_ref[...], mo_ref[...], vo_ref[...] = _adam_update(g_ref[...], w_ref[...], m_ref[...],
                                                            v_ref[...])

    spec = pl.BlockSpec((tr, g.shape[1]), lambda i: (i, 0))
    return pl.pallas_call(
        body, name="adamw", grid=(rows // tr,),
        in_specs=[spec] * 4, out_specs=[spec] * 3,
        out_shape=[jax.ShapeDtypeStruct(g.shape, F32)] * 3,
        compiler_params=_params(("parallel",)),
    )(g, w, m, v)


def _adamw_small(gs, ws, ms, vs):
    n = len(gs)

    def body(*refs):
        ins, outs = refs[:4 * n], refs[4 * n:]
        for a in range(n):
            res = _adam_update(*(ins[kind * n + a][...] for kind in range(4)))
            for kind in range(3):
                outs[kind * n + a][...] = res[kind]

    vmem = pl.BlockSpec(memory_space=pltpu.VMEM)
    out = pl.pallas_call(
        body, name="adamw_small",
        in_specs=[vmem] * (4 * n), out_specs=[vmem] * (3 * n),
        out_shape=[jax.ShapeDtypeStruct(g.shape, F32) for _ in range(3) for g in gs],
    )(*gs, *ws, *ms, *vs)
    return out[:n], out[n:2 * n], out[2 * n:]


HBM_SPEC = pl.BlockSpec(memory_space=pltpu.HBM)


def _place():
    x, y, c = lax.axis_index("x"), lax.axis_index("y"), lax.axis_index("c")
    chips = [(1 - x, y), (x, 1 - y), (1 - x, 1 - y)]
    return x, y, c, chips


def _all_gather(shards):
    na = len(shards)
    halves = [sh.shape[0] // 2 for sh in shards]

    def body(*refs):
        w_refs, a_refs = refs[:na], refs[na:2 * na]
        send_sems, recv_sems = refs[2 * na:]
        x, y, c, chips = _place()
        k = 2 * x + y
        sib = (x, y, 1 - c)

        def slab(a, kk, hc):
            return a_refs[a].at[kk, pl.ds(hc * halves[a], halves[a]), :]

        def copy(a, n, src, dst, to):
            return pltpu.make_async_remote_copy(
                src_ref=src, dst_ref=dst, send_sem=send_sems.at[6 * a + n],
                recv_sem=recv_sems.at[6 * a + n], device_id=to, device_id_type=MESH)

        first = [copy(a, n, w_refs[a].at[pl.ds(c * halves[a], halves[a]), :], slab(a, k, c),
                      (cx, cy, c))
                 for n, (cx, cy) in enumerate(chips) for a in range(na)]
        for cp in first:
            cp.start()
        passed = []
        for n, (cx, cy) in enumerate(chips):
            kk = 2 * cx + cy
            for a in range(na):
                copy(a, n, slab(a, kk, c), slab(a, kk, c), (cx, cy, c)).wait_recv()
                fwd = copy(a, 3 + n, slab(a, kk, c), slab(a, kk, c), sib)
                fwd.start()
                passed.append(fwd)
        for n, (cx, cy) in enumerate(chips):
            kk = 2 * cx + cy
            for a in range(na):
                copy(a, 3 + n, slab(a, kk, 1 - c), slab(a, kk, 1 - c), sib).wait_recv()
        for cp in first + passed:
            cp.wait_send()

    return pl.pallas_call(
        body, name="weights_all_gather",
        in_specs=[HBM_SPEC] * na, out_specs=[HBM_SPEC] * na,
        out_shape=[jax.ShapeDtypeStruct((N_CHIPS,) + sh.shape, sh.dtype) for sh in shards],
        scratch_shapes=[pltpu.SemaphoreType.DMA((6 * na,)), pltpu.SemaphoreType.DMA((6 * na,))],
    )(*shards)


def _fill_own_slab(buf, src, k_idx):
    n, rows, cols = buf.shape
    tr = _row_tile(rows)
    slabs = src.ndim == 3

    def body(k_ref, src_ref, buf_ref, out_ref):
        out_ref[0] = src_ref[0] if slabs else src_ref[...]

    if slabs:
        src_spec = pl.BlockSpec((1, tr, cols), lambda t, k_ref: (k_ref[0], t, 0))
    else:
        src_spec = pl.BlockSpec((tr, cols), lambda t, k_ref: (t, 0))
    return pl.pallas_call(
        body, name="fill_own_slab",
        grid_spec=pltpu.PrefetchScalarGridSpec(
            num_scalar_prefetch=1, grid=(rows // tr,),
            in_specs=[src_spec, pl.BlockSpec(memory_space=pl.ANY)],
            out_specs=pl.BlockSpec((1, tr, cols), lambda t, k_ref: (k_ref[0], t, 0))),
        out_shape=jax.ShapeDtypeStruct(buf.shape, buf.dtype),
        input_output_aliases={2: 0},
        compiler_params=_params(("parallel",)),
    )(k_idx, src, buf)


SEM_SPEC = pl.BlockSpec(memory_space=pltpu.SEMAPHORE)
LATE_COPIES = 6


def _late_copy(a, j, peer_core, src, dst, send_sems, recv_sems, to, sender_core):
    return pltpu.make_async_remote_copy(
        src_ref=src, dst_ref=dst, send_sem=send_sems.at[LATE_COPIES * a + 2 * j + peer_core],
        recv_sem=recv_sems.at[LATE_COPIES * a + 2 * j + sender_core], device_id=to,
        device_id_type=MESH)


def _gather_start(shards, after):
    na = len(shards)

    def body(*refs):
        w_refs, land_refs = refs[:na], refs[na:2 * na]
        send_sems, recv_sems = refs[2 * na + 1], refs[2 * na + 2]
        token = refs[-1]
        x, y, c, chips = _place()
        k = 2 * x + y
        for a in range(na):
            half = w_refs[a].shape[0] // 2
            src = w_refs[a].at[pl.ds(c * half, half), :]
            dst = land_refs[a].at[k, pl.ds(c * half, half), :]
            for j, (cx, cy) in enumerate(chips):
                for tc in range(2):
                    _late_copy(a, j, tc, src, dst, send_sems, recv_sems, (cx, cy, tc), c).start()
        token[...] = jnp.zeros_like(token)

    lands = [pltpu.with_memory_space_constraint(lax.empty((N_CHIPS,) + sh.shape, sh.dtype), pltpu.HBM)
             for sh in shards]
    srcs = [pltpu.with_memory_space_constraint(sh, pltpu.HBM) for sh in shards]
    sems = pltpu.SemaphoreType.DMA((LATE_COPIES * na,))
    aliases = {a: 2 + a for a in range(2 * na)}
    return pl.pallas_call(
        body, name="late_weights_gather_start",
        in_specs=[HBM_SPEC] * (2 * na) + [pl.BlockSpec(memory_space=pl.ANY)],
        out_specs=[SEM_SPEC, SEM_SPEC] + [HBM_SPEC] * (2 * na) + [pl.BlockSpec(memory_space=pltpu.VMEM)],
        out_shape=[sems, sems] + [pltpu.HBM(v.shape, v.dtype) for v in srcs + lands]
        + [jax.ShapeDtypeStruct((8, LANES), F32)],
        input_output_aliases=aliases,
        compiler_params=pltpu.CompilerParams(
            has_side_effects=pltpu.SideEffectType.DATAFLOW_SIDE_EFFECTING),
    )(*srcs, *lands, after)


def _gather_wait(started, na, after):
    send_sems, recv_sems = started[0], started[1]
    bufs = started[2:2 + 2 * na]

    def body(*refs):
        w_refs, land_refs = refs[:na], refs[na:2 * na]
        send_sems, recv_sems = refs[2 * na], refs[2 * na + 1]
        x, y, c, chips = _place()
        k = 2 * x + y
        for a in range(na):
            half = w_refs[a].shape[0] // 2
            src = w_refs[a].at[pl.ds(c * half, half), :]
            for j, (cx, cy) in enumerate(chips):
                kk = 2 * cx + cy
                for pc in range(2):
                    _late_copy(a, j, pc, src, land_refs[a].at[k, pl.ds(c * half, half), :],
                               send_sems, recv_sems, (cx, cy, pc), c).wait_send()
                    pltpu.make_async_remote_copy(
                        src_ref=src, dst_ref=land_refs[a].at[kk, pl.ds(pc * half, half), :],
                        send_sem=send_sems.at[LATE_COPIES * a + 2 * j + pc],
                        recv_sem=recv_sems.at[LATE_COPIES * a + 2 * j + pc],
                        device_id=(cx, cy, pc), device_id_type=MESH).wait_recv()

    out = pl.pallas_call(
        body, name="late_weights_gather_wait",
        in_specs=[HBM_SPEC] * (2 * na) + [SEM_SPEC, SEM_SPEC, pl.BlockSpec(memory_space=pl.ANY)],
        out_specs=[HBM_SPEC] * (2 * na),
        out_shape=[pltpu.HBM(v.shape, v.dtype) for v in bufs],
        input_output_aliases={a: a for a in range(2 * na)},
        compiler_params=pltpu.CompilerParams(
            has_side_effects=pltpu.SideEffectType.DATAFLOW_SIDE_EFFECTING),
    )(*bufs, send_sems, recv_sems, after)
    return out[na:]


def _split_start(name, srcs, land_shapes, n_sems, plan):
    na = len(srcs)

    def body(*refs):
        sends, _ = plan(refs[:na], refs[na:2 * na], refs[2 * na], refs[2 * na + 1])
        for cp in sends:
            cp.start()
        refs[-1][...] = jnp.zeros_like(refs[-1])

    lands = [pltpu.with_memory_space_constraint(lax.empty(shape, dtype), pltpu.HBM)
             for shape, dtype in land_shapes]
    srcs = [pltpu.with_memory_space_constraint(v, pltpu.HBM) for v in srcs]
    sems = pltpu.SemaphoreType.DMA((n_sems,))
    return pl.pallas_call(
        body, name=name,
        in_specs=[HBM_SPEC] * (2 * na),
        out_specs=[SEM_SPEC, SEM_SPEC] + [HBM_SPEC] * (2 * na) + [pl.BlockSpec(memory_space=pltpu.VMEM)],
        out_shape=[sems, sems] + [pltpu.HBM(v.shape, v.dtype) for v in srcs + lands]
        + [jax.ShapeDtypeStruct((8, LANES), F32)],
        input_output_aliases={a: 2 + a for a in range(2 * na)},
        compiler_params=pltpu.CompilerParams(
            has_side_effects=pltpu.SideEffectType.DATAFLOW_SIDE_EFFECTING),
    )(*srcs, *lands)


def _split_wait(name, started, na, plan, after):
    bufs = started[2:2 + 2 * na]

    def body(*refs):
        sends, recvs = plan(refs[:na], refs[na:2 * na], refs[2 * na], refs[2 * na + 1])
        for cp in sends:
            cp.wait_send()
        for cp in recvs:
            cp.wait_recv()

    out = pl.pallas_call(
        body, name=name,
        in_specs=[HBM_SPEC] * (2 * na) + [SEM_SPEC, SEM_SPEC, pl.BlockSpec(memory_space=pl.ANY)],
        out_specs=[HBM_SPEC] * (2 * na),
        out_shape=[pltpu.HBM(v.shape, v.dtype) for v in bufs],
        input_output_aliases={a: a for a in range(2 * na)},
        compiler_params=pltpu.CompilerParams(
            has_side_effects=pltpu.SideEffectType.DATAFLOW_SIDE_EFFECTING),
    )(*bufs, started[0], started[1], after)
    return out[:na], out[na:]


def _plan_to_sibling(g_refs, r_refs, send_sems, recv_sems):
    x, y, c, _ = _place()
    cps = []
    for a, (g, r) in enumerate(zip(g_refs, r_refs)):
        half = g.shape[1] // 2
        cps.append(pltpu.make_async_remote_copy(
            src_ref=g.at[:, pl.ds((1 - c) * half, half), :], dst_ref=r, send_sem=send_sems.at[a],
            recv_sem=recv_sems.at[a], device_id=(x, y, 1 - c), device_id_type=MESH))
    return cps, cps


def _plan_scatter(p_refs, q_refs, send_sems, recv_sems):
    x, y, c, chips = _place()
    k = 2 * x + y
    sends, recvs = [], []
    for a, (p, q) in enumerate(zip(p_refs, q_refs)):
        for i, (cx, cy) in enumerate(chips):
            kk = 2 * cx + cy
            for dst, out in ((q.at[k], sends), (q.at[kk], recvs)):
                out.append(pltpu.make_async_remote_copy(
                    src_ref=p.at[kk], dst_ref=dst, send_sem=send_sems.at[3 * a + i],
                    recv_sem=recv_sems.at[3 * a + i], device_id=(cx, cy, c), device_id_type=MESH))
    return sends, recvs


def _swap_halves_to_sibling(gs):
    na = len(gs)

    def body(*refs):
        g_refs, r_refs = refs[:na], refs[na:2 * na]
        send_sems, recv_sems = refs[2 * na:]
        x, y, c, _ = _place()
        cps = []
        for a in range(na):
            half = g_refs[a].shape[1] // 2
            cps.append(pltpu.make_async_remote_copy(
                src_ref=g_refs[a].at[:, pl.ds((1 - c) * half, half), :], dst_ref=r_refs[a],
                send_sem=send_sems.at[a], recv_sem=recv_sems.at[a], device_id=(x, y, 1 - c),
                device_id_type=MESH))
        for cp in cps:
            cp.start()
        for cp in cps:
            cp.wait()

    return pl.pallas_call(
        body, name="grads_to_sibling",
        in_specs=[HBM_SPEC] * na, out_specs=[HBM_SPEC] * na,
        out_shape=[jax.ShapeDtypeStruct((g.shape[0], g.shape[1] // 2, g.shape[2]), g.dtype)
                   for g in gs],
        scratch_shapes=[pltpu.SemaphoreType.DMA((na,)), pltpu.SemaphoreType.DMA((na,))],
    )(*gs)


def _row_tile(rows):
    return next(t for t in (256, 128, 64) if rows % t == 0)


def _add_sibling(g, r, c_idx, out_dtype):
    n, rows, cols = g.shape
    half = rows // 2
    tr = _row_tile(half)
    nb = half // tr

    def body(c_ref, g_ref, r_ref, p_ref):
        p_ref[...] = (g_ref[...] + r_ref[...]).astype(out_dtype)

    return pl.pallas_call(
        body, name="grads_add_sibling",
        grid_spec=pltpu.PrefetchScalarGridSpec(
            num_scalar_prefetch=1, grid=(n, nb),
            in_specs=[pl.BlockSpec((1, tr, cols), lambda j, t, c_ref: (j, c_ref[0] * nb + t, 0)),
                      pl.BlockSpec((1, tr, cols), lambda j, t, c_ref: (j, t, 0))],
            out_specs=pl.BlockSpec((1, tr, cols), lambda j, t, c_ref: (j, t, 0))),
        out_shape=jax.ShapeDtypeStruct((n, half, cols), out_dtype),
        compiler_params=_params(("parallel", "parallel")),
    )(c_idx, g, r)


def _scatter_to_chips(ps):
    na = len(ps)

    def body(*refs):
        p_refs, q_refs = refs[:na], refs[na:2 * na]
        send_sems, recv_sems = refs[2 * na:]
        x, y, c, chips = _place()
        k = 2 * x + y
        sends = []
        for i, (cx, cy) in enumerate(chips):
            for a in range(na):
                cp = pltpu.make_async_remote_copy(
                    src_ref=p_refs[a].at[2 * cx + cy], dst_ref=q_refs[a].at[k],
                    send_sem=send_sems.at[3 * a + i], recv_sem=recv_sems.at[3 * a + i],
                    device_id=(cx, cy, c), device_id_type=MESH)
                cp.start()
                sends.append(cp)
        for i, (cx, cy) in enumerate(chips):
            kk = 2 * cx + cy
            for a in range(na):
                pltpu.make_async_remote_copy(
                    src_ref=p_refs[a].at[kk], dst_ref=q_refs[a].at[kk],
                    send_sem=send_sems.at[3 * a + i], recv_sem=recv_sems.at[3 * a + i],
                    device_id=(cx, cy, c), device_id_type=MESH).wait_recv()
        for cp in sends:
            cp.wait_send()

    return pl.pallas_call(
        body, name="grads_scatter_to_chips",
        in_specs=[HBM_SPEC] * na, out_specs=[HBM_SPEC] * na,
        out_shape=[jax.ShapeDtypeStruct(p.shape, p.dtype) for p in ps],
        scratch_shapes=[pltpu.SemaphoreType.DMA((3 * na,)), pltpu.SemaphoreType.DMA((3 * na,))],
    )(*ps)


def _sum_chips(q, c_idx):
    n, half, cols = q.shape
    tr = _row_tile(half)
    nb = half // tr

    def body(c_ref, q_ref, o_ref):
        parts = [q_ref[kk].astype(F32) for kk in range(n)]
        o_ref[...] = ((parts[0] + parts[1]) + parts[2]) + parts[3]

    return pl.pallas_call(
        body, name="grads_sum_chips",
        grid_spec=pltpu.PrefetchScalarGridSpec(
            num_scalar_prefetch=1, grid=(nb,),
            in_specs=[pl.BlockSpec((n, tr, cols), lambda t, c_ref: (0, t, 0))],
            out_specs=pl.BlockSpec((tr, cols), lambda t, c_ref: (c_ref[0] * nb + t, 0))),
        out_shape=jax.ShapeDtypeStruct((2 * half, cols), F32),
        compiler_params=_params(("parallel",)),
    )(c_idx, q)


def _join_halves(fulls):
    na = len(fulls)

    def body(*refs):
        o_refs = refs[na:2 * na]
        send_sems, recv_sems = refs[2 * na:]
        x, y, c, _ = _place()
        sends = []
        for a in range(na):
            half = o_refs[a].shape[0] // 2
            rows = o_refs[a].at[pl.ds(c * half, half), :]
            sends.append(pltpu.make_async_remote_copy(
                src_ref=rows, dst_ref=rows, send_sem=send_sems.at[a], recv_sem=recv_sems.at[a],
                device_id=(x, y, 1 - c), device_id_type=MESH))
        for cp in sends:
            cp.start()
        for a in range(na):
            half = o_refs[a].shape[0] // 2
            other = o_refs[a].at[pl.ds((1 - c) * half, half), :]
            pltpu.make_async_remote_copy(
                src_ref=other, dst_ref=other, send_sem=send_sems.at[a], recv_sem=recv_sems.at[a],
                device_id=(x, y, 1 - c), device_id_type=MESH).wait_recv()
        for cp in sends:
            cp.wait_send()

    return pl.pallas_call(
        body, name="grads_join_halves",
        in_specs=[HBM_SPEC] * na, out_specs=[HBM_SPEC] * na,
        out_shape=[jax.ShapeDtypeStruct(f.shape, f.dtype) for f in fulls],
        input_output_aliases={a: a for a in range(na)},
        scratch_shapes=[pltpu.SemaphoreType.DMA((na,)), pltpu.SemaphoreType.DMA((na,))],
    )(*fulls)


def _part_rows(shape):
    size = 1
    for d in shape:
        size *= d
    rows = -(-size // PACK_COLS)
    return size, -(-rows // PACK_ROW_ALIGN) * PACK_ROW_ALIGN


def _pack_rows(arrays, dtype, total_rows):
    parts, used = [], 0
    for a in arrays:
        size, rows = _part_rows(a.shape)
        flat = a.reshape(-1).astype(dtype)
        parts.append(jnp.pad(flat, (0, rows * PACK_COLS - size)).reshape(rows, PACK_COLS))
        used += rows
    parts.append(jnp.zeros((total_rows - used, PACK_COLS), dtype))
    return jnp.concatenate(parts, axis=0)


def _unpack_rows(buf, shapes):
    lead = buf.shape[:-2]
    out, off = [], 0
    for sh in shapes:
        size, rows = _part_rows(sh)
        part = buf[..., off:off + rows, :].reshape(lead + (-1,))[..., :size]
        out.append(part.reshape(lead + tuple(sh)))
        off += rows
    return out


NEW_ORDER = ((928, 1440), (1440, 1952), (1952, 2464), (416, 928), (2464, 2976), (3744, 4256),
             (2976, 3488), (0, 256), (256, 384), (4256, 4320), (384, 416), (4256, 4288),
             (3488, 3616), (3616, 3744))
OLD_ORDER = ((3584, 3840), (3840, 3968), (4032, 4064), (1536, 2048), (0, 512), (512, 1024),
             (1024, 1536), (2048, 2560), (3072, 3584), (4096, 4224), (4224, 4352), (2560, 3072))


def _cols(sources, ranges):
    parts = []
    for a, b in ranges:
        off = 0
        for src in sources:
            width = src.shape[-1]
            lo, hi = max(a, off), min(b, off + width)
            if lo < hi:
                parts.append(src[..., lo - off:hi - off])
            off += width
    return jnp.concatenate(parts, axis=-1)


def _sub_ranges(ranges, a, b):
    out, off = [], 0
    for lo, hi in ranges:
        width = hi - lo
        s0, s1 = max(a, off), min(b, off + width)
        if s0 < s1:
            out.append((lo + s0 - off, lo + s1 - off))
        off += width
    return out


def _rope_tables(s):
    half = MLA_ROPE // 2
    inv_freq = jnp.power(jnp.float32(ROPE_THETA), -jnp.arange(half, dtype=F32) / half)
    ang = inv_freq[:, None] * jnp.arange(s, dtype=F32)[None, :]
    cos, sin = jnp.cos(ang), jnp.sin(ang)
    z = lambda n: jnp.zeros((n, s), F32)
    c = jnp.concatenate([jnp.ones((MLA_NOPE, s), F32), cos, cos, z(32)], axis=0)
    s1 = jnp.concatenate([z(MLA_NOPE), -sin, z(16), z(32)], axis=0)
    s2 = jnp.concatenate([z(MLA_NOPE), z(16), sin, z(32)], axis=0)
    return c, s1, s2


def _pad_lanes(a, n):
    return jnp.pad(a, ((0, 0), (0, n - a.shape[1])))


SHARDED = ("w_in", "w_out", "mla_w_qb", "mla_w_kvb", "conv_w")
REPLICATED = ("norm_g", "mla_q_a_norm", "mla_kv_a_norm", "mla_q_norm", "mla_k_norm",
              "swa_q_norm", "swa_k_norm", "swa_sinks")
WEIGHT_ORDER = ("norm_g", "w_in", "mla_q_a_norm", "mla_w_qb", "mla_kv_a_norm", "mla_w_kvb",
                "mla_q_norm", "mla_k_norm", "conv_w", "swa_q_norm", "swa_k_norm", "swa_sinks", "w_out")
SHARD_AXIS = {"w_in": 2, "w_out": 1, "mla_w_qb": 2, "mla_w_kvb": 2, "conv_w": 2}


def kernel(x, norm_g, w_in, mla_q_a_norm, mla_w_qb, mla_kv_a_norm, mla_w_kvb, mla_q_norm, mla_k_norm, conv_w, swa_q_norm, swa_k_norm, swa_sinks, w_out, loss_target, m_norm_g, m_w_in, m_mla_q_a_norm, m_mla_w_qb, m_mla_kv_a_norm, m_mla_w_kvb, m_mla_q_norm, m_mla_k_norm, m_conv_w, m_swa_q_norm, m_swa_k_norm, m_swa_sinks, m_w_out, v_norm_g, v_w_in, v_mla_q_a_norm, v_mla_w_qb, v_mla_kv_a_norm, v_mla_w_kvb, v_mla_q_norm, v_mla_k_norm, v_conv_w, v_swa_q_norm, v_swa_k_norm, v_swa_sinks, v_w_out):
    weights = dict(norm_g=norm_g, w_in=w_in, mla_q_a_norm=mla_q_a_norm, mla_w_qb=mla_w_qb,
                   mla_kv_a_norm=mla_kv_a_norm, mla_w_kvb=mla_w_kvb, mla_q_norm=mla_q_norm,
                   mla_k_norm=mla_k_norm, conv_w=conv_w, swa_q_norm=swa_q_norm,
                   swa_k_norm=swa_k_norm, swa_sinks=swa_sinks, w_out=w_out)
    mom_m = dict(norm_g=m_norm_g, w_in=m_w_in, mla_q_a_norm=m_mla_q_a_norm, mla_w_qb=m_mla_w_qb,
                 mla_kv_a_norm=m_mla_kv_a_norm, mla_w_kvb=m_mla_w_kvb, mla_q_norm=m_mla_q_norm,
                 mla_k_norm=m_mla_k_norm, conv_w=m_conv_w, swa_q_norm=m_swa_q_norm,
                 swa_k_norm=m_swa_k_norm, swa_sinks=m_swa_sinks, w_out=m_w_out)
    mom_v = dict(norm_g=v_norm_g, w_in=v_w_in, mla_q_a_norm=v_mla_q_a_norm, mla_w_qb=v_mla_w_qb,
                 mla_kv_a_norm=v_mla_kv_a_norm, mla_w_kvb=v_mla_w_kvb, mla_q_norm=v_mla_q_norm,
                 mla_k_norm=v_mla_k_norm, conv_w=v_conv_w, swa_q_norm=v_swa_q_norm,
                 swa_k_norm=v_swa_k_norm, swa_sinks=v_swa_sinks, w_out=v_w_out)
    xs = x[0]
    target = loss_target[0]
    s = xs.shape[0]
    c_idx = lax.axis_index("c").astype(jnp.int32).reshape(1)
    k_idx = (2 * lax.axis_index("x") + lax.axis_index("y")).astype(jnp.int32).reshape(1)

    conv_bits = lax.bitcast_convert_type(conv_w, BF16)
    small_list = [mla_w_qb, mla_w_kvb, conv_bits]
    shard_cols = w_in.shape[2]
    w_in_b = w_in.astype(BF16)
    late = [w_in_b[1], w_out.astype(BF16).reshape(-1, D_MODEL)]
    own = [w_in_b[0], _pack_rows(small_list, BF16, SMALL_ROWS)]
    gathered_in0, gathered_small = [_fill_own_slab(buf, src, k_idx)
                                    for buf, src in zip(_all_gather(own), own)]
    started = _gather_start(late, gathered_in0)
    parts = _unpack_rows(gathered_small, [a.shape for a in small_list])
    join = lambda p, axis: jnp.concatenate([p[k] for k in range(N_CHIPS)], axis=axis)
    w_in_zeros = jnp.zeros((D_MODEL, 64), BF16)
    permuted = lambda slabs: _cols([slabs[k] for k in range(N_CHIPS)] + [w_in_zeros], NEW_ORDER)
    w_qb_full = join(parts[0], 2)
    w_kvb_full = join(parts[1], 2)
    conv_full = lax.bitcast_convert_type(join(parts[2], 2), F32)

    rope = _rope_tables(s)
    swa_tables = _swa_tables()
    layers = []
    for l in range(DEPTH):
        wq = jnp.pad(w_qb_full[l].reshape(MLA_Q_LORA, MLA_HEADS, MLA_QK),
                     ((0, 0), (0, 0), (0, LANES - MLA_QK))).reshape(MLA_Q_LORA, MLA_HEADS * LANES)
        kv = w_kvb_full[l].reshape(MLA_KV_LORA, MLA_HEADS, MLA_NOPE + MLA_V)
        wk = jnp.pad(kv[:, :, :MLA_NOPE], ((0, 0), (0, 0), (0, LANES - MLA_NOPE)))
        wkv = jnp.concatenate([wk.reshape(MLA_KV_LORA, MLA_HEADS * LANES),
                               kv[:, :, MLA_NOPE:].reshape(MLA_KV_LORA, MLA_HEADS * MLA_V)], axis=1)
        layers.append(dict(
            wq=wq, wkv=wkv,
            cw=jnp.pad(conv_full[l], ((0, HALO - 3), (0, 0))),
            g=norm_g[l][None],
            mla_norms=(mla_q_a_norm[l][:, None], mla_kv_a_norm[l][:, None],
                       _pad_lanes(mla_q_norm[l][None], LANES).T, _pad_lanes(mla_k_norm[l][None], LANES).T),
            mla_weights=(wq, wkv, wq.T, wkv.T),
            sqn=swa_q_norm[l][:, None], skn=swa_k_norm[l][:, None], sinks=swa_sinks[l][None]))

    saved = []
    h_in = xs
    layers[0]["w_in"] = permuted(gathered_in0)
    layers[0]["g"] = layers[0]["g"] + started[-1][0:1, 0:1]
    for l in range(DEPTH):
        p = layers[l]
        proj, hb = _in_proj_fwd(h_in, p["g"], p["w_in"])
        q, k, v, qt, kt, vt = _mla_prep_fwd(proj, p["mla_norms"], p["mla_weights"], rope)
        o_mla, lse = _mla_attn_fwd(q, k, vt)
        if l == 0:
            late_in1, late_out = [_fill_own_slab(buf, src, k_idx) for buf, src in
                                  zip(_gather_wait(started, len(late), o_mla), late)]
            layers[1]["w_in"] = permuted(late_in1)
            w_out_full = join(late_out.reshape(N_CHIPS, DEPTH, -1, D_MODEL), 1)
            for n in range(DEPTH):
                layers[n]["w_out"] = w_out_full[n]
        o_swa = _swa_fwd(proj, p["sqn"], p["skn"], p["sinks"], swa_tables)
        last = l == DEPTH - 1
        y, z, *loss_acc = _out_fwd(proj, o_mla, o_swa, h_in, p["w_out"], p["cw"],
                                   target if last else None)
        saved.append(dict(x=h_in, proj=proj, hb=hb, q=q, k=k, v=v, qt=qt, kt=kt, o_mla=o_mla, lse=lse,
                          o_swa=o_swa, z=z))
        h_in = y

    dy, loss_acc = h_in, loss_acc[0]
    loss = lax.psum(loss_acc[0, 0], ("x", "y", "c"))

    grads = {n: [None] * DEPTH for n in WEIGHT_ORDER}

    def in_chunks(l):
        return jnp.stack([_cols(grads["w_in"][l], _sub_ranges(OLD_ORDER, k * shard_cols,
                                                              (k + 1) * shard_cols))
                          for k in range(N_CHIPS)])

    def out_chunks(l):
        return grads["w_out"][l].reshape(N_CHIPS, -1, D_MODEL)

    for l in reversed(range(DEPTH)):
        p, a = layers[l], saved[l]
        dconv, dgates, do_mla, do_mla_t, delta, do_swa, dcw = _out_bwd(dy, a["proj"], a["o_mla"], a["o_swa"],
                                                             p["w_out"], p["cw"])
        grads["w_out"][l] = _weight_grads(a["z"], [dy], "dw_out")[0]
        grads["conv_w"][l] = dcw[0:3]
        delta_rows = jnp.transpose(delta, (1, 0)).reshape(MLA_HEADS // 2, 2, s)
        dq, dk, dv = _mla_attn_bwd(a["q"], a["qt"], a["k"], a["kt"], a["v"], do_mla, do_mla_t,
                                   a["lse"], delta_rows)
        mla_norms = p["mla_norms"]
        if l == 0:
            late_gs, from_sib = _split_wait("late_grads_to_sibling_wait", late_st1, 2,
                                            _plan_to_sibling, dq)
            late_p = [_add_sibling(g, r, c_idx, BF16) for g, r in zip(late_gs, from_sib)]
            late_st2 = _split_start("late_grads_scatter_start", late_p,
                                    [(v.shape, v.dtype) for v in late_p], 6, _plan_scatter)
            mla_norms = (mla_norms[0] + late_st2[-1][0:1, 0:1],) + tuple(mla_norms[1:])
        dmla, dqan, dkvan, dqn, dkn, dwq_t, dwkv_t = _mla_prep_bwd(
            a["proj"], mla_norms, p["mla_weights"], rope, dq, dk, dv)
        dwq, dwkv = dwq_t.T, dwkv_t.T
        dsq, dskv, dsqn, dskn, dsinks = _swa_bwd(a["proj"], p["sqn"], p["skn"], p["sinks"], swa_tables, do_swa)
        if l == 0:
            late_p, late_q = _split_wait("late_grads_scatter_wait", late_st2, 2, _plan_scatter, dsq)
            late_full = [_sum_chips(_fill_own_slab(q_, p_, k_idx), c_idx)
                         for q_, p_ in zip(late_q, late_p)]
        pieces = [dconv, dgates, dsq, dmla, dskv]
        grads["w_in"][l] = _weight_grads(a["hb"], pieces, "dw_in")
        gain = p["g"]
        if l == 0:
            gs0 = (in_chunks(0), out_chunks(0))
            last_p = [_add_sibling(g, r, c_idx, BF16)
                      for g, r in zip(gs0, _swap_halves_to_sibling(gs0))]
            last_st = _split_start("last_grads_scatter_start", last_p,
                                   [(v.shape, v.dtype) for v in last_p], 6, _plan_scatter)
            gain = gain + last_st[-1][0:1, 0:1]
        dx, dg = _in_proj_bwd(pieces, a["x"], gain, p["w_in"], dy)
        if l == 0:
            last_p, last_q = _split_wait("last_grads_scatter_wait", last_st, 2, _plan_scatter, dx)
            last_full = [_sum_chips(_fill_own_slab(q_, p_, k_idx), c_idx)
                         for q_, p_ in zip(last_q, last_p)]
        grads["norm_g"][l] = dg[0]
        grads["mla_q_a_norm"][l] = dqan[:, 0]
        grads["mla_kv_a_norm"][l] = dkvan[:, 0]
        grads["mla_q_norm"][l] = dqn[:MLA_QK, 0]
        grads["mla_k_norm"][l] = dkn[:MLA_QK, 0]
        grads["mla_w_qb"][l] = dwq.reshape(MLA_Q_LORA, MLA_HEADS, LANES)[:, :, :MLA_QK].reshape(
            MLA_Q_LORA, MLA_HEADS * MLA_QK)
        dwk = dwkv[:, :MLA_HEADS * LANES].reshape(MLA_KV_LORA, MLA_HEADS, LANES)[:, :, :MLA_NOPE]
        dwv = dwkv[:, MLA_HEADS * LANES:].reshape(MLA_KV_LORA, MLA_HEADS, MLA_V)
        grads["mla_w_kvb"][l] = jnp.concatenate([dwk, dwv], axis=2).reshape(
            MLA_KV_LORA, MLA_HEADS * (MLA_NOPE + MLA_V))
        grads["swa_q_norm"][l] = dsqn[:, 0]
        grads["swa_k_norm"][l] = dskn[:, 0]
        grads["swa_sinks"][l] = dsinks[0]
        dy = dx
        if l == DEPTH - 1:
            late_st1 = _split_start(
                "late_grads_to_sibling_start", [in_chunks(l), out_chunks(l)],
                [((N_CHIPS, D_MODEL // 2, shard_cols), F32),
                 ((N_CHIPS, D_MIX // N_CHIPS // 2, D_MODEL), F32)], 2, _plan_to_sibling)
    grad_x = dy[None]
    full_grads = {n: jnp.stack(grads[n]) for n in WEIGHT_ORDER if n not in ("w_in", "w_out")}

    rest = tuple(n for n in SHARDED if n not in ("w_in", "w_out"))
    rep_shapes = [weights[n].shape for n in REPLICATED]
    rep_grads = jnp.concatenate([full_grads[n].reshape(-1) for n in REPLICATED])

    def chunk(g, n, k):
        width = g.shape[SHARD_AXIS[n]] // N_CHIPS
        return lax.slice_in_dim(g, k * width, (k + 1) * width, axis=SHARD_AXIS[n])

    g_small = jnp.stack([_pack_rows([chunk(full_grads[n], n, k) for n in rest] + [rep_grads],
                                    F32, SMALL_ROWS) for k in range(N_CHIPS)])
    gs = (g_small,)
    partial = [_add_sibling(g, r, c_idx, F32) for g, r in zip(gs, _swap_halves_to_sibling(gs))]
    by_chip = [_fill_own_slab(q, p, k_idx) for q, p in zip(_scatter_to_chips(partial), partial)]
    g_small_mine, g_in0, g_out0, g_in1, g_out1 = _join_halves(
        [_sum_chips(q, c_idx) for q in by_chip] + last_full + late_full)

    vals = _unpack_rows(g_small_mine, [weights[n].shape for n in rest] + [(rep_grads.shape[0],)])
    grad = dict(zip(rest, vals[:-1]))
    grad["w_in"] = jnp.stack([g_in0, g_in1])
    grad["w_out"] = jnp.stack([g_out0, g_out1])
    off = 0
    for n, sh in zip(REPLICATED, rep_shapes):
        grad[n] = vals[-1][off:off + sh[0] * sh[1]].reshape(sh)
        off += sh[0] * sh[1]
    results = {}
    for n in ("w_in", "w_out"):
        view = lambda a, n=n: a.reshape(-1, weights[n].shape[-1])
        res = _adamw(view(grad[n]), view(weights[n]), view(mom_m[n]), view(mom_v[n]))
        results[n] = [r.reshape(weights[n].shape) for r in res]
    small = tuple(n for n in WEIGHT_ORDER if n not in results)
    res = _adamw_small(*([d[n] for n in small] for d in (grad, weights, mom_m, mom_v)))
    for a, n in enumerate(small):
        results[n] = [res[kind][a] for kind in range(3)]
    unpacked = [grad] + [{n: results[n][kind] for n in WEIGHT_ORDER} for kind in range(3)]
    outs = [loss, grad_x]
    for group in unpacked:
        outs += [group[n] for n in WEIGHT_ORDER]
    return tuple(outs)
```

```python
import jax
import numpy as np
import jax.numpy as jnp
from jax import lax
from jax.experimental import pallas as pl
from jax.experimental.pallas import tpu as pltpu

F32 = jnp.float32
BF16 = jnp.bfloat16

D_MODEL = 1024
DEPTH = 2
GROUP = 512
D_MIX = 3 * GROUP
BLOCK = 128
RMS_EPS = 1e-6
NEG_INF = -1e30
MLA_HEADS = 8
MLA_QK = 96
MLA_NOPE = 64
MLA_ROPE = 32
MLA_V = 64
V_AUG = 80
MLA_Q_LORA = 256
MLA_KV_LORA = 128
ROPE_THETA = 10000.0
SWA_HEADS = 8
SWA_KV = 2
SWA_GROUP = 4
SWA_DIM = 64
N_CHIPS = 4

NC = 4352
OFF_SQ, OFF_MLA, OFF_SKV = 3072, 3584, 4096

VMEM_LIMIT = 56 * 1024 * 1024
LANES = 128
PACK_COLS = 1024
PACK_ROW_ALIGN = 16
SMALL_ROWS = 256

ADAM_LR = 0.001
ADAM_B1 = 0.9
ADAM_B2 = 0.999
ADAM_EPS = 1e-08
ADAM_WD = 0.01
ADAM_STEP = 10

MESH = pl.DeviceIdType.MESH


def _params(sem, vmem=VMEM_LIMIT):
    return pltpu.CompilerParams(dimension_semantics=sem, vmem_limit_bytes=vmem)


def _dot(a, b, dims):
    return lax.dot_general(a.astype(BF16), b.astype(BF16), (dims, ((), ())),
                           preferred_element_type=F32)


def _mm(a, b):
    return _dot(a, b, ((1,), (0,)))


def _mm_nt(a, b):
    return _dot(a, b, ((1,), (1,)))


def _rms(x, g, n=None):
    n = x.shape[-1] if n is None else n
    ms = jnp.sum(x * x, axis=-1, keepdims=True) * (1.0 / n)
    return x * lax.rsqrt(ms + RMS_EPS) * g


def _sigmoid(x):
    return 1.0 / (1.0 + jnp.exp(-x))


def _in_proj_fwd(x, g, w):
    s = x.shape[0]
    tm = min(512, s)

    def body(x_ref, g_ref, w_ref, proj_ref, hbt_ref):
        h = _rms(x_ref[...], g_ref[...])
        hbt_ref[...] = jnp.transpose(h).astype(BF16)
        proj_ref[...] = jnp.dot(h.astype(BF16), w_ref[...], preferred_element_type=F32)

    return pl.pallas_call(
        body, name="in_proj_fwd", grid=(s // tm,),
        in_specs=[pl.BlockSpec((tm, D_MODEL), lambda i: (i, 0)),
                  pl.BlockSpec((1, D_MODEL), lambda i: (0, 0)),
                  pl.BlockSpec((D_MODEL, NC), lambda i: (0, 0))],
        out_specs=[pl.BlockSpec((tm, NC), lambda i: (i, 0)),
                   pl.BlockSpec((D_MODEL, tm), lambda i: (0, i))],
        out_shape=[jax.ShapeDtypeStruct((s, NC), F32), jax.ShapeDtypeStruct((D_MODEL, s), BF16)],
        compiler_params=_params(("parallel",)),
    )(x, g, w)


def _in_proj_bwd(pieces, x, g, w, dres):
    s = x.shape[0]
    tm = min(512, s)
    n_p = len(pieces)

    def body(*refs):
        p_refs = refs[:n_p]
        x_ref, g_ref, w_ref, dres_ref, dx_ref, dg_ref = refs[n_p:]
        dh = None
        off = 0
        for r in p_refs:
            width = r.shape[1]
            t = _mm_nt(r[...], w_ref[:, off:off + width])
            dh = t if dh is None else dh + t
            off += width
        _, vjp = jax.vjp(_rms, x_ref[...], g_ref[...])
        dx, dg = vjp(dh)
        dx_ref[...] = dx + dres_ref[...]

        @pl.when(pl.program_id(0) == 0)
        def _():
            dg_ref[...] = jnp.zeros_like(dg_ref)

        dg_ref[...] += dg

    in_specs = [pl.BlockSpec((tm, p.shape[1]), lambda i: (i, 0)) for p in pieces]
    in_specs += [pl.BlockSpec((tm, D_MODEL), lambda i: (i, 0)),
                 pl.BlockSpec((1, D_MODEL), lambda i: (0, 0)),
                 pl.BlockSpec((D_MODEL, NC), lambda i: (0, 0)),
                 pl.BlockSpec((tm, D_MODEL), lambda i: (i, 0))]
    return pl.pallas_call(
        body, name="in_proj_bwd", grid=(s // tm,),
        in_specs=in_specs,
        out_specs=[pl.BlockSpec((tm, D_MODEL), lambda i: (i, 0)),
                   pl.BlockSpec((1, D_MODEL), lambda i: (0, 0))],
        out_shape=[jax.ShapeDtypeStruct((s, D_MODEL), F32), jax.ShapeDtypeStruct((1, D_MODEL), F32)],
        compiler_params=_params(("arbitrary",)),
    )(*pieces, x, g, w, dres)


def _weight_grads(at, bs, name):
    m, s = at.shape
    nb = len(bs)
    tk = min(512, s)

    def body(a_ref, *refs):
        b_refs, o_refs = refs[:nb], refs[nb:]

        @pl.when(pl.program_id(0) == 0)
        def _():
            for o_ref in o_refs:
                o_ref[...] = jnp.zeros_like(o_ref)

        a = a_ref[...]
        for b_ref, o_ref in zip(b_refs, o_refs):
            o_ref[...] += _mm(a, b_ref[...])

    return pl.pallas_call(
        body, name=name, grid=(s // tk,),
        in_specs=[pl.BlockSpec((m, tk), lambda k: (0, k))]
        + [pl.BlockSpec((tk, b.shape[1]), lambda k: (k, 0)) for b in bs],
        out_specs=[pl.BlockSpec((m, b.shape[1]), lambda k: (0, 0)) for b in bs],
        out_shape=[jax.ShapeDtypeStruct((m, b.shape[1]), F32) for b in bs],
        compiler_params=_params(("arbitrary",)),
    )(at, *bs)


def _rms0(x, g, n=None):
    n = x.shape[0] if n is None else n
    ms = jnp.sum(x * x, axis=0, keepdims=True) * (1.0 / n)
    return x * lax.rsqrt(ms + RMS_EPS) * g


@jax.custom_vjp
def _rope0(t, c, s1, s2):
    return t * c + pltpu.roll(t, LANES - 16, 0) * s1 + pltpu.roll(t, 16, 0) * s2


def _rope0_fwd(t, c, s1, s2):
    return _rope0(t, c, s1, s2), (c, s1, s2)


def _rope0_bwd(res, g):
    c, s1, s2 = res
    dt = g * c + pltpu.roll(g * s1, 16, 0) + pltpu.roll(g * s2, LANES - 16, 0)
    return dt, jnp.zeros_like(c), jnp.zeros_like(s1), jnp.zeros_like(s2)


_rope0.defvjp(_rope0_fwd, _rope0_bwd)


@jax.custom_vjp
def _mmw(w, wt, x):
    return _mm(w, x)


def _mmw_fwd(w, wt, x):
    return _mm(w, x), (wt, x)


def _mmw_bwd(res, g):
    wt, x = res
    return _mm_nt(g, x), jnp.zeros_like(wt), _mm(wt, g)


_mmw.defvjp(_mmw_fwd, _mmw_bwd)


def _prep_fn(q_lat, kv_lat, kr, qan, kvan, qn, kn, wq, wk, wv, wqt, wkt, wvt, c, s1, s2, mm):
    tokens = q_lat.shape[1]
    rq = _rms0(q_lat, qan)
    rkv = _rms0(kv_lat, kvan)
    qn_b = jnp.broadcast_to(qn, (LANES, tokens))
    kn_b = jnp.broadcast_to(kn, (LANES, tokens))
    qs, ks = [], []
    for h in range(MLA_HEADS):
        qs.append(_rope0(_rms0(mm(wq[h], wqt[h], rq), qn_b, MLA_QK), c, s1, s2))
        ks.append(_rope0(_rms0(mm(wk[h], wkt[h], rkv) + kr, kn_b, MLA_QK), c, s1, s2))
    return tuple(qs), tuple(ks), mm(wv, wvt, rkv)


def _prep_weights(wq_ref, wkv_ref, wqt_ref, wkvt_ref):
    heads = range(MLA_HEADS)
    wq = tuple(wqt_ref[LANES * h:LANES * (h + 1), :].astype(F32) for h in heads)
    wk = tuple(wkvt_ref[LANES * h:LANES * (h + 1), :].astype(F32) for h in heads)
    wv = wkvt_ref[LANES * MLA_HEADS:, :].astype(F32)
    wqt = tuple(wq_ref[:, LANES * h:LANES * (h + 1)].astype(F32) for h in heads)
    wkt = tuple(wkv_ref[:, LANES * h:LANES * (h + 1)].astype(F32) for h in heads)
    wvt = wkv_ref[:, LANES * MLA_HEADS:].astype(F32)
    return wq, wk, wv, wqt, wkt, wvt


def _prep_in_specs(tm):
    const = lambda shape: pl.BlockSpec(shape, lambda i: (0, 0))
    col = lambda height: pl.BlockSpec((height, tm), lambda i: (0, i))
    return [pl.BlockSpec((tm, 512), lambda i: (i, OFF_MLA // 512)),
            const((MLA_Q_LORA, 1)), const((MLA_KV_LORA, 1)), const((LANES, 1)), const((LANES, 1)),
            const((MLA_Q_LORA, 1024)), const((MLA_KV_LORA, 1536)),
            const((1024, MLA_Q_LORA)), const((1536, MLA_KV_LORA)),
            col(LANES), col(LANES), col(LANES)]


def _prep_operands(blk_ref, refs):
    qan_ref, kvan_ref, qn_ref, kn_ref, wq_ref, wkv_ref, wqt_ref, wkvt_ref, c_ref, s1_ref, s2_ref = refs
    blk_t = jnp.transpose(blk_ref[...])
    diff = (blk_t[0:256], blk_t[256:384], blk_t[384:512],
            qan_ref[...], kvan_ref[...], qn_ref[...], kn_ref[...])
    weights = _prep_weights(wq_ref, wkv_ref, wqt_ref, wkvt_ref)
    return diff, weights, (c_ref[...], s1_ref[...], s2_ref[...])


def _mla_prep_fwd(proj, norms, weights, rope):
    s = proj.shape[0]
    tm = min(512, s)

    def body(blk_ref, *refs):
        ins, (q_ref, k_ref, v_ref, qt_ref, kt_ref, vt_ref) = refs[:11], refs[11:]
        diff, (wq, wk, wv, wqt, wkt, wvt), tables = _prep_operands(blk_ref, ins)
        qs, ks, v = _prep_fn(*diff, wq, wk, wv, wqt, wkt, wvt, *tables,
                             lambda w, wt, x: _mm(w, x))
        for h in range(MLA_HEADS):
            q2 = qs[h] * Q_PRESCALE
            qt_ref[LANES * h:LANES * (h + 1), :] = q2.astype(BF16)
            kt_ref[LANES * h:LANES * (h + 1), :] = ks[h].astype(BF16)
            q_ref[:, LANES * h:LANES * (h + 1)] = jnp.transpose(q2).astype(BF16)
            k_ref[:, LANES * h:LANES * (h + 1)] = jnp.transpose(ks[h]).astype(BF16)
        ones_row = (lax.broadcasted_iota(jnp.int32, (V_AUG - MLA_V, v.shape[1]), 0) == 0).astype(BF16)
        for h in range(MLA_HEADS):
            vt_ref[V_AUG * h:V_AUG * h + MLA_V, :] = v[MLA_V * h:MLA_V * (h + 1)].astype(BF16)
            vt_ref[V_AUG * h + MLA_V:V_AUG * (h + 1), :] = ones_row
        v_ref[...] = jnp.transpose(v).astype(BF16)

    row = lambda width: pl.BlockSpec((tm, width), lambda i: (i, 0))
    col = lambda height: pl.BlockSpec((height, tm), lambda i: (0, i))
    return pl.pallas_call(
        body, name="mla_prep_fwd", grid=(s // tm,),
        in_specs=_prep_in_specs(tm),
        out_specs=[row(1024), row(1024), row(512), col(1024), col(1024), col(MLA_HEADS * V_AUG)],
        out_shape=[jax.ShapeDtypeStruct((s, 1024), BF16), jax.ShapeDtypeStruct((s, 1024), BF16),
                   jax.ShapeDtypeStruct((s, 512), BF16), jax.ShapeDtypeStruct((1024, s), BF16),
                   jax.ShapeDtypeStruct((1024, s), BF16),
                   jax.ShapeDtypeStruct((MLA_HEADS * V_AUG, s), BF16)],
        compiler_params=_params(("parallel",)),
    )(proj, *norms, *weights, *rope)


def _mla_prep_bwd(proj, norms, weights, rope, dq, dk, dv):
    s = proj.shape[0]
    tm = min(512, s)

    def body(blk_ref, *refs):
        ins, (dq_ref, dk_ref, dv_ref) = refs[:11], refs[11:14]
        dblk_ref, dqan_ref, dkvan_ref, dqn_ref, dkn_ref, dwq_ref, dwkv_ref = refs[14:]
        diff, (wq, wk, wv, wqt, wkt, wvt), tables = _prep_operands(blk_ref, ins)

        def fn(q_lat, kv_lat, kr, qan, kvan, qn, kn, wq_, wk_, wv_):
            return _prep_fn(q_lat, kv_lat, kr, qan, kvan, qn, kn, wq_, wk_, wv_, wqt, wkt, wvt,
                            *tables, _mmw)

        _, vjp = jax.vjp(fn, *diff, wq, wk, wv)
        heads = range(MLA_HEADS)
        cts = (tuple(dq_ref[LANES * h:LANES * (h + 1), :] for h in heads),
               tuple(dk_ref[LANES * h:LANES * (h + 1), :] for h in heads), dv_ref[...])
        dq_lat, dkv_lat, dkr, dqan, dkvan, dqn, dkn, dwq_h, dwk_h, dwv = vjp(cts)
        dblk_ref[...] = jnp.transpose(
            jnp.concatenate([dq_lat, dkv_lat, dkr], axis=0)).astype(BF16)

        @pl.when(pl.program_id(0) == 0)
        def _():
            for r in (dqan_ref, dkvan_ref, dqn_ref, dkn_ref, dwq_ref, dwkv_ref):
                r[...] = jnp.zeros_like(r)

        dqan_ref[...] += dqan
        dkvan_ref[...] += dkvan
        dqn_ref[...] += dqn
        dkn_ref[...] += dkn
        for h in heads:
            dwq_ref[LANES * h:LANES * (h + 1), :] += dwq_h[h]
            dwkv_ref[LANES * h:LANES * (h + 1), :] += dwk_h[h]
        dwkv_ref[LANES * MLA_HEADS:, :] += dwv

    const = lambda shape: pl.BlockSpec(shape, lambda i: (0, 0))
    col = lambda height: pl.BlockSpec((height, tm), lambda i: (0, i))
    shapes = [(MLA_Q_LORA, 1), (MLA_KV_LORA, 1), (LANES, 1), (LANES, 1),
              (1024, MLA_Q_LORA), (1536, MLA_KV_LORA)]
    return pl.pallas_call(
        body, name="mla_prep_bwd", grid=(s // tm,),
        in_specs=_prep_in_specs(tm) + [col(1024), col(1024), col(512)],
        out_specs=[pl.BlockSpec((tm, 512), lambda i: (i, 0))] + [const(sh) for sh in shapes],
        out_shape=[jax.ShapeDtypeStruct((s, 512), BF16)]
        + [jax.ShapeDtypeStruct(sh, F32) for sh in shapes],
        compiler_params=_params(("arbitrary",)),
    )(proj, *norms, *weights, *rope, dq, dk, dv)


MLA_SCALE = MLA_QK ** -0.5


LOG2E = 1.4426950408889634
LN2 = 0.6931471805599453
Q_PRESCALE = MLA_SCALE * LOG2E
HEAD_GROUPS = ((0, 1),)


def _mla_attn_fwd(q2, k, vt):
    s = q2.shape[0]
    t = min(512, s)
    tk = min(128, s)
    nq = s // t
    r = t // tk

    def body(q_ref, k_ref, vt_ref, o_ref, lse_ref, acc_ref):
        i = pl.program_id(1)
        row = lax.broadcasted_iota(jnp.int32, (tk, t), 0)
        col = lax.broadcasted_iota(jnp.int32, (tk, t), 1)
        qh = [q_ref[:, LANES * hh:LANES * (hh + 1)] for hh in range(2)]
        acc_ref[...] = jnp.zeros_like(acc_ref)

        def scores(j, heads, diag=None):
            r0 = pl.multiple_of(j * tk, tk)
            q_lo = 0 if diag is None else diag * tk
            out = []
            for hh in heads:
                kc = k_ref[pl.ds(r0, tk), LANES * hh:LANES * (hh + 1)]
                sc = lax.dot_general(kc, qh[hh][q_lo:], (((1,), (1,)), ((), ())),
                                     preferred_element_type=F32)
                out.append(sc if diag is None
                           else jnp.where(row[:, :t - q_lo] <= col[:, :t - q_lo], sc, NEG_INF))
            return tuple(out)

        for heads in HEAD_GROUPS:
            stats = tuple(jnp.full((1, t), NEG_INF, F32) for _ in heads)

            def consume(j, scs, stats, q_lo=0, heads=heads):
                r0 = pl.multiple_of(j * tk, tk)
                out, ps, alphas = [], [], []
                for n, hh in enumerate(heads):
                    m_old = stats[n][:, q_lo:]
                    m_new = jnp.maximum(m_old, jnp.max(scs[n], axis=0, keepdims=True))
                    ps.append(jnp.exp2(scs[n] - m_new).astype(BF16))
                    alphas.append(jnp.exp2(m_old - m_new))
                    out.append(m_new if q_lo == 0
                               else jnp.concatenate([stats[n][:, :q_lo], m_new], axis=1))
                for n, hh in enumerate(heads):
                    vc = vt_ref[V_AUG * hh:V_AUG * (hh + 1), pl.ds(r0, tk)]
                    acc_ref[hh, :, q_lo:] = alphas[n] * acc_ref[hh, :, q_lo:] + jnp.dot(
                        vc, ps[n], preferred_element_type=F32)
                return tuple(out)

            def group(j0, stats, diag, heads=heads):
                scs = [scores(j0 + d, heads, d if diag else None) for d in range(r)]
                for d in range(r):
                    stats = consume(j0 + d, scs[d], stats, d * tk if diag else 0)
                return stats

            stats = group(r * i, stats, True)
            stats = lax.fori_loop(0, i, lambda j, st: group(r * j, st, False), stats)
            for n, hh in enumerate(heads):
                l = acc_ref[hh, MLA_V:MLA_V + 1, :]
                o_ref[:, MLA_V * hh:MLA_V * (hh + 1)] = jnp.transpose(acc_ref[hh, 0:MLA_V, :] / l)
                lse_ref[0, hh:hh + 1, :] = stats[n] + jnp.log2(l)

    return pl.pallas_call(
        body, name="mla_attn_fwd", grid=(MLA_HEADS // 2, nq),
        in_specs=[pl.BlockSpec((t, 256), lambda p, i: (i, p)),
                  pl.BlockSpec((s, 256), lambda p, i: (0, p)),
                  pl.BlockSpec((2 * V_AUG, s), lambda p, i: (p, 0))],
        out_specs=[pl.BlockSpec((t, 128), lambda p, i: (i, p)),
                   pl.BlockSpec((1, 2, t), lambda p, i: (p, 0, i))],
        out_shape=[jax.ShapeDtypeStruct((s, 512), F32),
                   jax.ShapeDtypeStruct((MLA_HEADS // 2, 2, s), F32)],
        scratch_shapes=[pltpu.VMEM((2, V_AUG, t), F32)],
        compiler_params=_params(("parallel", "arbitrary")),
    )(q2, k, vt)


def _mla_attn_bwd(q2, q2t, k, kt, v, do, dot, lse_rows, delta_rows):
    s = q2.shape[0]
    t = min(512, s)
    nq = s // t

    def body(q_ref, qt_ref, k_ref, kt_ref, v_ref, do_ref, dot_ref, lse_ref, dl_ref,
             dq_ref, dk_ref, dv_ref):
        j = pl.program_id(1)

        @pl.when(j == 0)
        def _():
            dq_ref[...] = jnp.zeros_like(dq_ref)

        dk_ref[...] = jnp.zeros_like(dk_ref)
        dv_ref[...] = jnp.zeros_like(dv_ref)
        row = lax.broadcasted_iota(jnp.int32, (t, t), 0)
        col = lax.broadcasted_iota(jnp.int32, (t, t), 1)
        causal_t = row <= col
        kh = [k_ref[:, LANES * hh:LANES * (hh + 1)] for hh in range(2)]
        kth = [kt_ref[LANES * hh:LANES * (hh + 1), :] for hh in range(2)]
        vh = [v_ref[:, MLA_V * hh:MLA_V * (hh + 1)] for hh in range(2)]
        nt = (((1,), (1,)), ((), ()))

        def step(i, masked):
            r0 = pl.multiple_of(i * t, t)
            sd = []
            for hh in range(2):
                qh = q_ref[pl.ds(r0, t), LANES * hh:LANES * (hh + 1)]
                doh = do_ref[pl.ds(r0, t), MLA_V * hh:MLA_V * (hh + 1)]
                sc_t = lax.dot_general(kh[hh], qh, nt, preferred_element_type=F32)
                sd.append(jnp.where(causal_t, sc_t, NEG_INF) if masked else sc_t)
                sd.append(lax.dot_general(vh[hh], doh, nt, preferred_element_type=F32))
            for hh in range(2):
                lse = lse_ref[0, hh:hh + 1, pl.ds(r0, t)]
                dl = dl_ref[0, hh:hh + 1, pl.ds(r0, t)]
                p_t = jnp.exp2(sd[2 * hh] - lse)
                g_t = (p_t * (sd[2 * hh + 1] - dl)).astype(BF16)
                qth = qt_ref[LANES * hh:LANES * (hh + 1), pl.ds(r0, t)]
                doth = dot_ref[MLA_V * hh:MLA_V * (hh + 1), pl.ds(r0, t)]
                dv_ref[MLA_V * hh:MLA_V * (hh + 1), :] += lax.dot_general(
                    doth, p_t.astype(BF16), nt, preferred_element_type=F32)
                dk_ref[LANES * hh:LANES * (hh + 1), :] += lax.dot_general(
                    qth, g_t, nt, preferred_element_type=F32)
                dq_ref[LANES * hh:LANES * (hh + 1), pl.ds(r0, t)] += jnp.dot(
                    kth[hh], g_t, preferred_element_type=F32)

        step(j, True)

        rest = nq - 1 - j

        @pl.when(rest % 2 == 1)
        def _():
            step(j + 1, False)

        first = j + 1 + rest % 2

        def trip(p, carry):
            step(first + 2 * p, False)
            step(first + 2 * p + 1, False)
            return carry

        lax.fori_loop(0, rest // 2, trip, 0)
        dk_ref[...] = dk_ref[...] * LN2

        @pl.when(j == nq - 1)
        def _():
            dq_ref[...] = dq_ref[...] * MLA_SCALE

    return pl.pallas_call(
        body, name="mla_attn_bwd", grid=(MLA_HEADS // 2, nq),
        in_specs=[pl.BlockSpec((s, 256), lambda p, j: (0, p)),
                  pl.BlockSpec((256, s), lambda p, j: (p, 0)),
                  pl.BlockSpec((t, 256), lambda p, j: (j, p)),
                  pl.BlockSpec((256, t), lambda p, j: (p, j)),
                  pl.BlockSpec((t, 128), lambda p, j: (j, p)),
                  pl.BlockSpec((s, 128), lambda p, j: (0, p)),
                  pl.BlockSpec((128, s), lambda p, j: (p, 0)),
                  pl.BlockSpec((1, 2, s), lambda p, j: (p, 0, 0)),
                  pl.BlockSpec((1, 2, s), lambda p, j: (p, 0, 0))],
        out_specs=[pl.BlockSpec((256, s), lambda p, j: (p, 0)),
                   pl.BlockSpec((256, t), lambda p, j: (p, j)),
                   pl.BlockSpec((128, t), lambda p, j: (p, j))],
        out_shape=[jax.ShapeDtypeStruct((1024, s), F32), jax.ShapeDtypeStruct((1024, s), F32),
                   jax.ShapeDtypeStruct((512, s), F32)],
        compiler_params=_params(("parallel", "arbitrary")),
    )(q2, q2t, k, kt, v, do, dot, lse_rows, delta_rows)


SWA_SCALE = SWA_DIM ** -0.5
SWA_COLS = SWA_GROUP * BLOCK
SWA_LOG2 = SWA_SCALE * LOG2E


def _swa_tables():
    k = np.arange(2 * BLOCK)[:, None]
    col = np.arange(SWA_COLS)[None, :]
    dist = BLOCK + (col % BLOCK) - k
    valid = (dist >= 0) & (dist < BLOCK)
    out = np.zeros((2, SWA_KV, 2 * BLOCK, SWA_COLS), np.float32)
    for first in range(2):
        ok = valid & ((k >= BLOCK) | (first == 0))
        for j in range(SWA_KV):
            slope = 2.0 ** -(SWA_GROUP * j + col // BLOCK + 1)
            out[first, j] = np.where(ok, -slope * dist * LOG2E, NEG_INF)
    return jnp.asarray(out)


def _swa_tile_inputs(sq_ref, skv_ref, halo_ref, qn_ref, kn_ref, sk_ref, add_ref, first):
    tokens = sq_ref.shape[0]
    kv_all = jnp.concatenate([halo_ref[...], skv_ref[...]], axis=0)
    kv_t = jnp.transpose(kv_all)
    sq_t = jnp.transpose(sq_ref[...])
    k_raw = [kv_t[SWA_DIM * j:SWA_DIM * (j + 1)] for j in range(SWA_KV)]
    v_t = [kv_t[128 + SWA_DIM * j:128 + SWA_DIM * (j + 1)] for j in range(SWA_KV)]
    v_nat = [kv_all[:, 128 + SWA_DIM * j:128 + SWA_DIM * (j + 1)] for j in range(SWA_KV)]
    q_raw = [sq_t[SWA_DIM * h:SWA_DIM * (h + 1)] for h in range(SWA_HEADS)]
    qn_b = jnp.broadcast_to(qn_ref[...], (SWA_DIM, tokens))
    kn_b = jnp.broadcast_to(kn_ref[...], (SWA_DIM, tokens + BLOCK))
    lane_grp = lax.broadcasted_iota(jnp.int32, (1, SWA_COLS), 1) // BLOCK
    sinks, adds = [], []
    for j in range(SWA_KV):
        row = jnp.zeros((1, SWA_COLS), F32)
        for g in range(SWA_GROUP):
            h = SWA_GROUP * j + g
            row = jnp.where(lane_grp == g, sk_ref[:, h:h + 1] * LOG2E, row)
        sinks.append(row)
        adds.append((jnp.where(first, add_ref[1, j], add_ref[0, j]), add_ref[0, j]))
    return k_raw, v_t, v_nat, q_raw, qn_b, kn_b, sinks, adds


def _swa_probs(kb, qs_t, add, sink):
    s2 = jnp.dot(kb, qs_t, preferred_element_type=F32) * SWA_LOG2 + add
    m = jnp.maximum(jnp.max(s2, axis=0, keepdims=True), sink)
    e = jnp.exp2(s2 - m)
    es = jnp.exp2(sink - m)
    inv = 1.0 / (jnp.sum(e, axis=0, keepdims=True) + es)
    return e, inv, es


def _swa_queries(qn_t, j, b):
    return jnp.concatenate([qn_t[SWA_GROUP * j + g][:, BLOCK * b:BLOCK * (b + 1)]
                            for g in range(SWA_GROUP)], axis=1)


def _swa_fwd(proj, qn, kn, sinks, tables):
    s = proj.shape[0]
    ts = min(1024, s)
    nb = ts // BLOCK

    def body(sq_ref, skv_ref, halo_ref, qn_ref, kn_ref, sk_ref, add_ref, o_ref, ot_ref):
        first = pl.program_id(0) == 0
        k_raw, v_t, _, q_raw, qn_b, kn_b, sink_rows, adds = _swa_tile_inputs(
            sq_ref, skv_ref, halo_ref, qn_ref, kn_ref, sk_ref, add_ref, first)
        kn_nat = [jnp.transpose(_rms0(k, kn_b)).astype(BF16) for k in k_raw]
        v_t = [v.astype(BF16) for v in v_t]
        qn_t = [_rms0(q, qn_b).astype(BF16) for q in q_raw]
        for b in range(nb):
            band = slice(BLOCK * b, BLOCK * (b + 2))
            for j in range(SWA_KV):
                e, inv, _ = _swa_probs(kn_nat[j][band], _swa_queries(qn_t, j, b),
                                       adds[j][0 if b == 0 else 1], sink_rows[j])
                o_t = jnp.dot(v_t[j][:, band], (e * inv).astype(BF16),
                              preferred_element_type=F32)
                for g in range(SWA_GROUP):
                    h = SWA_GROUP * j + g
                    ot_ref[SWA_DIM * h:SWA_DIM * (h + 1), BLOCK * b:BLOCK * (b + 1)] = (
                        o_t[:, BLOCK * g:BLOCK * (g + 1)])
        o_ref[...] = jnp.transpose(ot_ref[...])

    const = lambda shape: pl.BlockSpec(shape, lambda i: (0,) * len(shape))
    return pl.pallas_call(
        body, name="swa_fwd", grid=(s // ts,),
        in_specs=[pl.BlockSpec((ts, 512), lambda i: (i, OFF_SQ // 512)),
                  pl.BlockSpec((ts, 256), lambda i: (i, OFF_SKV // 256)),
                  pl.BlockSpec((BLOCK, 256), lambda i: (jnp.maximum(i * nb - 1, 0), OFF_SKV // 256)),
                  const((SWA_DIM, 1)), const((SWA_DIM, 1)), const((1, SWA_HEADS)),
                  const(tables.shape)],
        out_specs=pl.BlockSpec((ts, 512), lambda i: (i, 0)),
        out_shape=jax.ShapeDtypeStruct((s, 512), F32),
        scratch_shapes=[pltpu.VMEM((512, ts), F32)],
        compiler_params=_params(("parallel",)),
    )(proj, proj, proj, qn, kn, sinks, tables)


def _swa_bwd(proj, qn, kn, sinks, tables, do):
    s = proj.shape[0]
    ts = min(1024, s)
    nb = ts // BLOCK
    nt = s // ts

    def body(sq_ref, skv_ref, halo_ref, qn_ref, kn_ref, sk_ref, add_ref, do_ref,
             dsq_ref, dskv_ref, dqn_ref, dkn_ref, dsk_ref, carry_ref, dqt_ref, dkvt_ref):
        step = pl.program_id(0)
        first = step == nt - 1

        @pl.when(step == 0)
        def _():
            carry_ref[...] = jnp.zeros_like(carry_ref)
            dqn_ref[...] = jnp.zeros_like(dqn_ref)
            dkn_ref[...] = jnp.zeros_like(dkn_ref)
            dsk_ref[...] = jnp.zeros_like(dsk_ref)

        k_raw, v_t, v_nat, q_raw, qn_b, kn_b, sink_rows, adds = _swa_tile_inputs(
            sq_ref, skv_ref, halo_ref, qn_ref, kn_ref, sk_ref, add_ref, first)
        kn_f = [_rms0(k, kn_b) for k in k_raw]
        kn_t = [k.astype(BF16) for k in kn_f]
        kn_nat = [jnp.transpose(k).astype(BF16) for k in kn_f]
        v_nat = [v.astype(BF16) for v in v_nat]
        qn_t = [_rms0(q, qn_b).astype(BF16) for q in q_raw]
        do_t = jnp.transpose(do_ref[...].astype(F32)).astype(BF16)

        dkvt_ref[...] = jnp.zeros_like(dkvt_ref)
        dsink = [jnp.zeros((1, SWA_COLS), F32) for _ in range(SWA_KV)]
        nt_dims = (((1,), (1,)), ((), ()))
        for b in range(nb):
            rows = slice(BLOCK * b, BLOCK * (b + 1))
            band = slice(BLOCK * b, BLOCK * (b + 2))
            for j in range(SWA_KV):
                heads = [SWA_GROUP * j + g for g in range(SWA_GROUP)]
                qs_t = _swa_queries(qn_t, j, b)
                dos_t = jnp.concatenate([do_t[SWA_DIM * h:SWA_DIM * (h + 1), rows] for h in heads],
                                        axis=1)
                e, inv, es = _swa_probs(kn_nat[j][band], qs_t, adds[j][0 if b == 0 else 1],
                                        sink_rows[j])
                p = e * inv
                dp = jnp.dot(v_nat[j][band], dos_t, preferred_element_type=F32)
                dsum = jnp.sum(p * dp, axis=0, keepdims=True)
                dsink[j] = dsink[j] - es * inv * dsum
                g_t = (p * (dp - dsum) * SWA_SCALE).astype(BF16)
                dv_t = lax.dot_general(dos_t, p.astype(BF16), nt_dims,
                                       preferred_element_type=F32)
                dk_t = lax.dot_general(qs_t, g_t, nt_dims, preferred_element_type=F32)
                dq_t = jnp.dot(kn_t[j][:, band], g_t, preferred_element_type=F32)
                dkvt_ref[SWA_DIM * j:SWA_DIM * (j + 1), band] += dk_t
                dkvt_ref[128 + SWA_DIM * j:128 + SWA_DIM * (j + 1), band] += dv_t
                for g, h in enumerate(heads):
                    dqt_ref[SWA_DIM * h:SWA_DIM * (h + 1), rows] = dq_t[:, BLOCK * g:BLOCK * (g + 1)]

        dqn = jnp.zeros((SWA_DIM, 1), F32)
        for h in range(SWA_HEADS):
            _, vjp = jax.vjp(_rms0, q_raw[h], qn_ref[...])
            dq, dg = vjp(dqt_ref[SWA_DIM * h:SWA_DIM * (h + 1), :])
            dqt_ref[SWA_DIM * h:SWA_DIM * (h + 1), :] = dq
            dqn = dqn + dg
        dqn_ref[...] += dqn
        dsq_ref[...] = jnp.transpose(dqt_ref[...]).astype(BF16)
        dkn = jnp.zeros((SWA_DIM, 1), F32)
        lane_grp = lax.broadcasted_iota(jnp.int32, (1, SWA_COLS), 1) // BLOCK
        for j in range(SWA_KV):
            _, vjp = jax.vjp(_rms0, k_raw[j], kn_ref[...])
            dk, dg = vjp(dkvt_ref[SWA_DIM * j:SWA_DIM * (j + 1), :])
            dkvt_ref[SWA_DIM * j:SWA_DIM * (j + 1), :] = dk
            dkn = dkn + dg
            for g in range(SWA_GROUP):
                h = SWA_GROUP * j + g
                dsk_ref[:, h:h + 1] += jnp.sum(jnp.where(lane_grp == g, dsink[j], 0.0), axis=1,
                                               keepdims=True)
        dkn_ref[...] += dkn
        dkv = jnp.transpose(dkvt_ref[...])
        dskv_ref[0:ts - BLOCK, :] = dkv[BLOCK:ts].astype(BF16)
        dskv_ref[ts - BLOCK:ts, :] = (dkv[ts:ts + BLOCK] + carry_ref[...]).astype(BF16)
        carry_ref[...] = dkv[0:BLOCK]

    const = lambda shape: pl.BlockSpec(shape, lambda st: (0,) * len(shape))
    return pl.pallas_call(
        body, name="swa_bwd", grid=(nt,),
        in_specs=[pl.BlockSpec((ts, 512), lambda st: (nt - 1 - st, OFF_SQ // 512)),
                  pl.BlockSpec((ts, 256), lambda st: (nt - 1 - st, OFF_SKV // 256)),
                  pl.BlockSpec((BLOCK, 256),
                               lambda st: (jnp.maximum((nt - 1 - st) * nb - 1, 0), OFF_SKV // 256)),
                  const((SWA_DIM, 1)), const((SWA_DIM, 1)), const((1, SWA_HEADS)),
                  const(tables.shape),
                  pl.BlockSpec((ts, 512), lambda st: (nt - 1 - st, 0))],
        out_specs=[pl.BlockSpec((ts, 512), lambda st: (nt - 1 - st, 0)),
                   pl.BlockSpec((ts, 256), lambda st: (nt - 1 - st, 0)),
                   const((SWA_DIM, 1)), const((SWA_DIM, 1)), const((1, SWA_HEADS))],
        out_shape=[jax.ShapeDtypeStruct((s, 512), BF16), jax.ShapeDtypeStruct((s, 256), BF16),
                   jax.ShapeDtypeStruct((SWA_DIM, 1), F32), jax.ShapeDtypeStruct((SWA_DIM, 1), F32),
                   jax.ShapeDtypeStruct((1, SWA_HEADS), F32)],
        scratch_shapes=[pltpu.VMEM((BLOCK, 256), F32), pltpu.VMEM((512, ts), F32),
                        pltpu.VMEM((256, ts + BLOCK), F32)],
        compiler_params=_params(("arbitrary",)),
    )(proj, proj, proj, qn, kn, sinks, tables, do)


HALO = 8


def _shift_down(u, halo, k):
    tm = u.shape[0]
    rid = lax.broadcasted_iota(jnp.int32, u.shape, 0)
    out = pltpu.roll(u, k, 0)
    for r in range(k):
        out = jnp.where(rid == r, halo[HALO - k + r:HALO - k + r + 1, :], out)
    return out


def _shift_up(u, halo, k):
    tm = u.shape[0]
    rid = lax.broadcasted_iota(jnp.int32, u.shape, 0)
    out = pltpu.roll(u, tm - k, 0)
    for r in range(k):
        out = jnp.where(rid == tm - k + r, halo[r:r + 1, :], out)
    return out


def _conv_fwd_vals(conv_ref, convp_ref, cw_ref, is_first):
    c_h, c_b, c_c = conv_ref[:, 0:512], conv_ref[:, 512:1024], conv_ref[:, 1024:1536]
    u = c_c * c_h
    up = jnp.where(is_first, 0.0, convp_ref[:, 1024:1536] * convp_ref[:, 0:512])
    u1 = _shift_down(u, up, 1)
    u2 = _shift_down(u, up, 2)
    yc = cw_ref[0:1, :] * u2 + cw_ref[1:2, :] * u1 + cw_ref[2:3, :] * u
    return c_h, c_b, c_c, u, u1, u2, yc


def _out_fwd(proj, o_mla, o_swa, x, w_out, cw, target=None):
    s = proj.shape[0]
    tm = min(512, s)
    nt = s // tm
    with_loss = target is not None

    def body(*refs):
        conv_ref, convp_ref, gates_ref, om_ref, os_ref, x_ref, w_ref, cw_ref = refs[:8]
        if with_loss:
            t_ref, y_ref, zt_ref, loss_ref, z_ref = refs[8:]
        else:
            y_ref, zt_ref, z_ref = refs[8:]
        i = pl.program_id(0)
        _, c_b, _, _, _, _, yc = _conv_fwd_vals(conv_ref, convp_ref, cw_ref, i == 0)
        mix = (om_ref[...], c_b * yc, os_ref[...])
        for n in range(3):
            g = gates_ref[:, GROUP * n:GROUP * (n + 1)]
            z = mix[n] * (g * _sigmoid(g))
            z_ref[:, GROUP * n:GROUP * (n + 1)] = z.astype(BF16)
            zt_ref[GROUP * n:GROUP * (n + 1), :] = jnp.transpose(z).astype(BF16)
        y = x_ref[...] + jnp.dot(z_ref[...], w_ref[...], preferred_element_type=F32)
        if not with_loss:
            y_ref[...] = y
            return
        err = y - t_ref[...]
        y_ref[...] = err * (1.0 / D_MODEL)

        @pl.when(i == 0)
        def _():
            loss_ref[...] = jnp.zeros_like(loss_ref)

        sq = jnp.sum((err * err).reshape(tm // 8, 8, D_MODEL), axis=0)
        part = sq[:, 0:LANES]
        for c in range(1, D_MODEL // LANES):
            part = part + sq[:, LANES * c:LANES * (c + 1)]
        loss_ref[...] += part

        @pl.when(i == nt - 1)
        def _():
            loss_ref[...] = jnp.full(loss_ref.shape, (0.5 / D_MODEL) * jnp.sum(loss_ref[...]), F32)

    row = lambda width: pl.BlockSpec((tm, width), lambda i: (i, 0))
    in_specs = [pl.BlockSpec((tm, 1536), lambda i: (i, 0)),
                pl.BlockSpec((HALO, 1536), lambda i: (jnp.maximum(i * (tm // HALO) - 1, 0), 0)),
                pl.BlockSpec((tm, 1536), lambda i: (i, 1)),
                row(512), row(512), row(D_MODEL),
                pl.BlockSpec((D_MIX, D_MODEL), lambda i: (0, 0)),
                pl.BlockSpec((HALO, 512), lambda i: (0, 0))]
    out_specs = [row(D_MODEL), pl.BlockSpec((D_MIX, tm), lambda i: (0, i))]
    out_shape = [jax.ShapeDtypeStruct((s, D_MODEL), F32), jax.ShapeDtypeStruct((D_MIX, s), BF16)]
    operands = [proj, proj, proj, o_mla, o_swa, x, w_out, cw]
    if with_loss:
        in_specs.append(row(D_MODEL))
        out_specs.append(pl.BlockSpec((8, LANES), lambda i: (0, 0)))
        out_shape.append(jax.ShapeDtypeStruct((8, LANES), F32))
        operands.append(target)
    return pl.pallas_call(
        body, name="out_fwd_loss" if with_loss else "out_fwd", grid=(nt,),
        in_specs=in_specs, out_specs=out_specs, out_shape=out_shape,
        scratch_shapes=[pltpu.VMEM((tm, D_MIX), BF16)],
        compiler_params=_params(("arbitrary",) if with_loss else ("parallel",)),
    )(*operands)


def _out_bwd(dy, proj, o_mla, o_swa, w_out, cw):
    s = proj.shape[0]
    tm = min(512, s)
    nt = s // tm
    hb = tm // HALO

    def body(dy_ref, dyn_ref, conv_ref, convp_ref, convn_ref, gates_ref, gatesn_ref, om_ref, os_ref,
             w_ref, cw_ref,
             dconv_ref, dgates_ref, dom_ref, domt_ref, delta_ref, dos_ref, dcw_ref):
        i = pl.program_id(0)
        dz = _mm_nt(dy_ref[...], w_ref[...])

        def gate(n):
            g = gates_ref[:, GROUP * n:GROUP * (n + 1)]
            sg = _sigmoid(g)
            return g * sg, sg * (1.0 + g * (1.0 - sg))

        for n, o_ref, do_ref in ((0, om_ref, dom_ref), (2, os_ref, dos_ref)):
            silu, dsilu = gate(n)
            dzn = dz[:, GROUP * n:GROUP * (n + 1)]
            o = o_ref[...]
            do = dzn * silu
            do_ref[...] = do.astype(do_ref.dtype)
            dgates_ref[:, GROUP * n:GROUP * (n + 1)] = (dzn * o * dsilu).astype(BF16)
            if n == 0:
                domt_ref[...] = jnp.transpose(do).astype(BF16)
                t = do * o
                for h in range(MLA_HEADS):
                    delta_ref[:, h:h + 1] = jnp.sum(t[:, MLA_V * h:MLA_V * (h + 1)], axis=-1,
                                                    keepdims=True)

        c_h, c_b, c_c, u, u1, u2, yc = _conv_fwd_vals(conv_ref, convp_ref, cw_ref, i == 0)
        silu, dsilu = gate(1)
        dzc = dz[:, GROUP:2 * GROUP]
        dgates_ref[:, GROUP:2 * GROUP] = (dzc * (c_b * yc) * dsilu).astype(BF16)
        dycr = dzc * silu
        dyc = dycr * c_b
        gn = gatesn_ref[:, GROUP:2 * GROUP]
        dzc_n = _mm_nt(dyn_ref[...], w_ref[GROUP:2 * GROUP, :])
        dyc_n = jnp.where(i == nt - 1, 0.0, dzc_n * (gn * _sigmoid(gn)) * convn_ref[:, 512:1024])
        d1 = _shift_up(dyc, dyc_n, 1)
        d2 = _shift_up(dyc, dyc_n, 2)
        du = cw_ref[2:3, :] * dyc + cw_ref[1:2, :] * d1 + cw_ref[0:1, :] * d2
        dconv_ref[:, 0:512] = (du * c_c).astype(BF16)
        dconv_ref[:, 512:1024] = (dycr * yc).astype(BF16)
        dconv_ref[:, 1024:1536] = (du * c_h).astype(BF16)

        @pl.when(i == 0)
        def _():
            dcw_ref[...] = jnp.zeros_like(dcw_ref)

        for k, uk in enumerate((u2, u1, u)):
            dcw_ref[k:k + 1, :] += jnp.sum(dyc * uk, axis=0, keepdims=True)

    row = lambda width: pl.BlockSpec((tm, width), lambda i: (i, 0))
    prev = lambda i: jnp.maximum(i * hb - 1, 0)
    nxt = lambda i: jnp.minimum((i + 1) * hb, s // HALO - 1)
    return pl.pallas_call(
        body, name="out_bwd", grid=(nt,),
        in_specs=[row(D_MODEL),
                  pl.BlockSpec((HALO, D_MODEL), lambda i: (nxt(i), 0)),
                  pl.BlockSpec((tm, 1536), lambda i: (i, 0)),
                  pl.BlockSpec((HALO, 1536), lambda i: (prev(i), 0)),
                  pl.BlockSpec((HALO, 1536), lambda i: (nxt(i), 0)),
                  pl.BlockSpec((tm, 1536), lambda i: (i, 1)),
                  pl.BlockSpec((HALO, 1536), lambda i: (nxt(i), 1)),
                  row(512), row(512),
                  pl.BlockSpec((D_MIX, D_MODEL), lambda i: (0, 0)),
                  pl.BlockSpec((HALO, 512), lambda i: (0, 0))],
        out_specs=[row(1536), row(1536), row(512), pl.BlockSpec((512, tm), lambda i: (0, i)),
                   row(MLA_HEADS), row(512), pl.BlockSpec((HALO, 512), lambda i: (0, 0))],
        out_shape=[jax.ShapeDtypeStruct((s, 1536), BF16), jax.ShapeDtypeStruct((s, 1536), BF16),
                   jax.ShapeDtypeStruct((s, 512), BF16), jax.ShapeDtypeStruct((512, s), BF16),
                   jax.ShapeDtypeStruct((s, MLA_HEADS), F32),
                   jax.ShapeDtypeStruct((s, 512), BF16), jax.ShapeDtypeStruct((HALO, 512), F32)],
        compiler_params=_params(("arbitrary",)),
    )(dy, dy, proj, proj, proj, proj, proj, o_mla, o_swa, w_out, cw)


def _adam_update(g, w, m, v):
    c1 = 1.0 - ADAM_B1
    c2 = 1.0 - ADAM_B2
    bc1 = 1.0 - ADAM_B1 ** ADAM_STEP
    bc2 = 1.0 - ADAM_B2 ** ADAM_STEP
    m_new = ADAM_B1 * m + c1 * g
    v_new = ADAM_B2 * v + c2 * (g * g)
    delta = -ADAM_LR * ((m_new / bc1) / (jnp.sqrt(v_new / bc2) + ADAM_EPS) + ADAM_WD * w)
    return delta, m_new, v_new


def _adamw(g, w, m, v):
    rows = g.shape[0]
    tr = min(256, rows)

    def body(g_ref, w_ref, m_ref, v_ref, d_ref, mo_ref, vo_ref):
        d_ref[...], mo_ref[...], vo_ref[...] = _adam_update(g_ref[...], w_ref[...], m_ref[...],
                                                            v_ref[...])

    spec = pl.BlockSpec((tr, g.shape[1]), lambda i: (i, 0))
    return pl.pallas_call(
        body, name="adamw", grid=(rows // tr,),
        in_specs=[spec] * 4, out_specs=[spec] * 3,
        out_shape=[jax.ShapeDtypeStruct(g.shape, F32)] * 3,
        compiler_params=_params(("parallel",)),
    )(g, w, m, v)


def _adamw_small(gs, ws, ms, vs):
    n = len(gs)

    def body(*refs):
        ins, outs = refs[:4 * n], refs[4 * n:]
        for a in range(n):
            res = _adam_update(*(ins[kind * n + a][...] for kind in range(4)))
            for kind in range(3):
                outs[kind * n + a][...] = res[kind]

    vmem = pl.BlockSpec(memory_space=pltpu.VMEM)
    out = pl.pallas_call(
        body, name="adamw_small",
        in_specs=[vmem] * (4 * n), out_specs=[vmem] * (3 * n),
        out_shape=[jax.ShapeDtypeStruct(g.shape, F32) for _ in range(3) for g in gs],
    )(*gs, *ws, *ms, *vs)
    return out[:n], out[n:2 * n], out[2 * n:]


HBM_SPEC = pl.BlockSpec(memory_space=pltpu.HBM)


def _place():
    x, y, c = lax.axis_index("x"), lax.axis_index("y"), lax.axis_index("c")
    chips = [(1 - x, y), (x, 1 - y), (1 - x, 1 - y)]
    return x, y, c, chips


def _all_gather(shards):
    na = len(shards)
    halves = [sh.shape[0] // 2 for sh in shards]

    def body(*refs):
        w_refs, a_refs = refs[:na], refs[na:2 * na]
        send_sems, recv_sems = refs[2 * na:]
        x, y, c, chips = _place()
        k = 2 * x + y
        sib = (x, y, 1 - c)

        def slab(a, kk, hc):
            return a_refs[a].at[kk, pl.ds(hc * halves[a], halves[a]), :]

        def copy(a, n, src, dst, to):
            return pltpu.make_async_remote_copy(
                src_ref=src, dst_ref=dst, send_sem=send_sems.at[6 * a + n],
                recv_sem=recv_sems.at[6 * a + n], device_id=to, device_id_type=MESH)

        first = [copy(a, n, w_refs[a].at[pl.ds(c * halves[a], halves[a]), :], slab(a, k, c),
                      (cx, cy, c))
                 for n, (cx, cy) in enumerate(chips) for a in range(na)]
        for cp in first:
            cp.start()
        passed = []
        for n, (cx, cy) in enumerate(chips):
            kk = 2 * cx + cy
            for a in range(na):
                copy(a, n, slab(a, kk, c), slab(a, kk, c), (cx, cy, c)).wait_recv()
                fwd = copy(a, 3 + n, slab(a, kk, c), slab(a, kk, c), sib)
                fwd.start()
                passed.append(fwd)
        for n, (cx, cy) in enumerate(chips):
            kk = 2 * cx + cy
            for a in range(na):
                copy(a, 3 + n, slab(a, kk, 1 - c), slab(a, kk, 1 - c), sib).wait_recv()
        for cp in first + passed:
            cp.wait_send()

    return pl.pallas_call(
        body, name="weights_all_gather",
        in_specs=[HBM_SPEC] * na, out_specs=[HBM_SPEC] * na,
        out_shape=[jax.ShapeDtypeStruct((N_CHIPS,) + sh.shape, sh.dtype) for sh in shards],
        scratch_shapes=[pltpu.SemaphoreType.DMA((6 * na,)), pltpu.SemaphoreType.DMA((6 * na,))],
    )(*shards)


def _fill_own_slab(buf, src, k_idx):
    n, rows, cols = buf.shape
    tr = _row_tile(rows)
    slabs = src.ndim == 3

    def body(k_ref, src_ref, buf_ref, out_ref):
        out_ref[0] = src_ref[0] if slabs else src_ref[...]

    if slabs:
        src_spec = pl.BlockSpec((1, tr, cols), lambda t, k_ref: (k_ref[0], t, 0))
    else:
        src_spec = pl.BlockSpec((tr, cols), lambda t, k_ref: (t, 0))
    return pl.pallas_call(
        body, name="fill_own_slab",
        grid_spec=pltpu.PrefetchScalarGridSpec(
            num_scalar_prefetch=1, grid=(rows // tr,),
            in_specs=[src_spec, pl.BlockSpec(memory_space=pl.ANY)],
            out_specs=pl.BlockSpec((1, tr, cols), lambda t, k_ref: (k_ref[0], t, 0))),
        out_shape=jax.ShapeDtypeStruct(buf.shape, buf.dtype),
        input_output_aliases={2: 0},
        compiler_params=_params(("parallel",)),
    )(k_idx, src, buf)


SEM_SPEC = pl.BlockSpec(memory_space=pltpu.SEMAPHORE)
LATE_COPIES = 6


def _late_copy(a, j, peer_core, src, dst, send_sems, recv_sems, to, sender_core):
    return pltpu.make_async_remote_copy(
        src_ref=src, dst_ref=dst, send_sem=send_sems.at[LATE_COPIES * a + 2 * j + peer_core],
        recv_sem=recv_sems.at[LATE_COPIES * a + 2 * j + sender_core], device_id=to,
        device_id_type=MESH)


def _gather_start(shards, after):
    na = len(shards)

    def body(*refs):
        w_refs, land_refs = refs[:na], refs[na:2 * na]
        send_sems, recv_sems = refs[2 * na + 1], refs[2 * na + 2]
        token = refs[-1]
        x, y, c, chips = _place()
        k = 2 * x + y
        for a in range(na):
            half = w_refs[a].shape[0] // 2
            src = w_refs[a].at[pl.ds(c * half, half), :]
            dst = land_refs[a].at[k, pl.ds(c * half, half), :]
            for j, (cx, cy) in enumerate(chips):
                for tc in range(2):
                    _late_copy(a, j, tc, src, dst, send_sems, recv_sems, (cx, cy, tc), c).start()
        token[...] = jnp.zeros_like(token)

    lands = [pltpu.with_memory_space_constraint(lax.empty((N_CHIPS,) + sh.shape, sh.dtype), pltpu.HBM)
             for sh in shards]
    srcs = [pltpu.with_memory_space_constraint(sh, pltpu.HBM) for sh in shards]
    sems = pltpu.SemaphoreType.DMA((LATE_COPIES * na,))
    aliases = {a: 2 + a for a in range(2 * na)}
    return pl.pallas_call(
        body, name="late_weights_gather_start",
        in_specs=[HBM_SPEC] * (2 * na) + [pl.BlockSpec(memory_space=pl.ANY)],
        out_specs=[SEM_SPEC, SEM_SPEC] + [HBM_SPEC] * (2 * na) + [pl.BlockSpec(memory_space=pltpu.VMEM)],
        out_shape=[sems, sems] + [pltpu.HBM(v.shape, v.dtype) for v in srcs + lands]
        + [jax.ShapeDtypeStruct((8, LANES), F32)],
        input_output_aliases=aliases,
        compiler_params=pltpu.CompilerParams(
            has_side_effects=pltpu.SideEffectType.DATAFLOW_SIDE_EFFECTING),
    )(*srcs, *lands, after)


def _gather_wait(started, na, after):
    send_sems, recv_sems = started[0], started[1]
    bufs = started[2:2 + 2 * na]

    def body(*refs):
        w_refs, land_refs = refs[:na], refs[na:2 * na]
        send_sems, recv_sems = refs[2 * na], refs[2 * na + 1]
        x, y, c, chips = _place()
        k = 2 * x + y
        for a in range(na):
            half = w_refs[a].shape[0] // 2
            src = w_refs[a].at[pl.ds(c * half, half), :]
            for j, (cx, cy) in enumerate(chips):
                kk = 2 * cx + cy
                for pc in range(2):
                    _late_copy(a, j, pc, src, land_refs[a].at[k, pl.ds(c * half, half), :],
                               send_sems, recv_sems, (cx, cy, pc), c).wait_send()
                    pltpu.make_async_remote_copy(
                        src_ref=src, dst_ref=land_refs[a].at[kk, pl.ds(pc * half, half), :],
                        send_sem=send_sems.at[LATE_COPIES * a + 2 * j + pc],
                        recv_sem=recv_sems.at[LATE_COPIES * a + 2 * j + pc],
                        device_id=(cx, cy, pc), device_id_type=MESH).wait_recv()

    out = pl.pallas_call(
        body, name="late_weights_gather_wait",
        in_specs=[HBM_SPEC] * (2 * na) + [SEM_SPEC, SEM_SPEC, pl.BlockSpec(memory_space=pl.ANY)],
        out_specs=[HBM_SPEC] * (2 * na),
        out_shape=[pltpu.HBM(v.shape, v.dtype) for v in bufs],
        input_output_aliases={a: a for a in range(2 * na)},
        compiler_params=pltpu.CompilerParams(
            has_side_effects=pltpu.SideEffectType.DATAFLOW_SIDE_EFFECTING),
    )(*bufs, send_sems, recv_sems, after)
    return out[na:]


def _split_start(name, srcs, land_shapes, n_sems, plan):
    na = len(srcs)

    def body(*refs):
        sends, _ = plan(refs[:na], refs[na:2 * na], refs[2 * na], refs[2 * na + 1])
        for cp in sends:
            cp.start()
        refs[-1][...] = jnp.zeros_like(refs[-1])

    lands = [pltpu.with_memory_space_constraint(lax.empty(shape, dtype), pltpu.HBM)
             for shape, dtype in land_shapes]
    srcs = [pltpu.with_memory_space_constraint(v, pltpu.HBM) for v in srcs]
    sems = pltpu.SemaphoreType.DMA((n_sems,))
    return pl.pallas_call(
        body, name=name,
        in_specs=[HBM_SPEC] * (2 * na),
        out_specs=[SEM_SPEC, SEM_SPEC] + [HBM_SPEC] * (2 * na) + [pl.BlockSpec(memory_space=pltpu.VMEM)],
        out_shape=[sems, sems] + [pltpu.HBM(v.shape, v.dtype) for v in srcs + lands]
        + [jax.ShapeDtypeStruct((8, LANES), F32)],
        input_output_aliases={a: 2 + a for a in range(2 * na)},
        compiler_params=pltpu.CompilerParams(
            has_side_effects=pltpu.SideEffectType.DATAFLOW_SIDE_EFFECTING),
    )(*srcs, *lands)


def _split_wait(name, started, na, plan, after):
    bufs = started[2:2 + 2 * na]

    def body(*refs):
        sends, recvs = plan(refs[:na], refs[na:2 * na], refs[2 * na], refs[2 * na + 1])
        for cp in sends:
            cp.wait_send()
        for cp in recvs:
            cp.wait_recv()

    out = pl.pallas_call(
        body, name=name,
        in_specs=[HBM_SPEC] * (2 * na) + [SEM_SPEC, SEM_SPEC, pl.BlockSpec(memory_space=pl.ANY)],
        out_specs=[HBM_SPEC] * (2 * na),
        out_shape=[pltpu.HBM(v.shape, v.dtype) for v in bufs],
        input_output_aliases={a: a for a in range(2 * na)},
        compiler_params=pltpu.CompilerParams(
            has_side_effects=pltpu.SideEffectType.DATAFLOW_SIDE_EFFECTING),
    )(*bufs, started[0], started[1], after)
    return out[:na], out[na:]


def _plan_to_sibling(g_refs, r_refs, send_sems, recv_sems):
    x, y, c, _ = _place()
    cps = []
    for a, (g, r) in enumerate(zip(g_refs, r_refs)):
        half = g.shape[1] // 2
        cps.append(pltpu.make_async_remote_copy(
            src_ref=g.at[:, pl.ds((1 - c) * half, half), :], dst_ref=r, send_sem=send_sems.at[a],
            recv_sem=recv_sems.at[a], device_id=(x, y, 1 - c), device_id_type=MESH))
    return cps, cps


def _plan_scatter(p_refs, q_refs, send_sems, recv_sems):
    x, y, c, chips = _place()
    k = 2 * x + y
    sends, recvs = [], []
    for a, (p, q) in enumerate(zip(p_refs, q_refs)):
        for i, (cx, cy) in enumerate(chips):
            kk = 2 * cx + cy
            for dst, out in ((q.at[k], sends), (q.at[kk], recvs)):
                out.append(pltpu.make_async_remote_copy(
                    src_ref=p.at[kk], dst_ref=dst, send_sem=send_sems.at[3 * a + i],
                    recv_sem=recv_sems.at[3 * a + i], device_id=(cx, cy, c), device_id_type=MESH))
    return sends, recvs


def _swap_halves_to_sibling(gs):
    na = len(gs)

    def body(*refs):
        g_refs, r_refs = refs[:na], refs[na:2 * na]
        send_sems, recv_sems = refs[2 * na:]
        x, y, c, _ = _place()
        cps = []
        for a in range(na):
            half = g_refs[a].shape[1] // 2
            cps.append(pltpu.make_async_remote_copy(
                src_ref=g_refs[a].at[:, pl.ds((1 - c) * half, half), :], dst_ref=r_refs[a],
                send_sem=send_sems.at[a], recv_sem=recv_sems.at[a], device_id=(x, y, 1 - c),
                device_id_type=MESH))
        for cp in cps:
            cp.start()
        for cp in cps:
            cp.wait()

    return pl.pallas_call(
        body, name="grads_to_sibling",
        in_specs=[HBM_SPEC] * na, out_specs=[HBM_SPEC] * na,
        out_shape=[jax.ShapeDtypeStruct((g.shape[0], g.shape[1] // 2, g.shape[2]), g.dtype)
                   for g in gs],
        scratch_shapes=[pltpu.SemaphoreType.DMA((na,)), pltpu.SemaphoreType.DMA((na,))],
    )(*gs)


def _row_tile(rows):
    return next(t for t in (256, 128, 64) if rows % t == 0)


def _add_sibling(g, r, c_idx, out_dtype):
    n, rows, cols = g.shape
    half = rows // 2
    tr = _row_tile(half)
    nb = half // tr

    def body(c_ref, g_ref, r_ref, p_ref):
        p_ref[...] = (g_ref[...] + r_ref[...]).astype(out_dtype)

    return pl.pallas_call(
        body, name="grads_add_sibling",
        grid_spec=pltpu.PrefetchScalarGridSpec(
            num_scalar_prefetch=1, grid=(n, nb),
            in_specs=[pl.BlockSpec((1, tr, cols), lambda j, t, c_ref: (j, c_ref[0] * nb + t, 0)),
                      pl.BlockSpec((1, tr, cols), lambda j, t, c_ref: (j, t, 0))],
            out_specs=pl.BlockSpec((1, tr, cols), lambda j, t, c_ref: (j, t, 0))),
        out_shape=jax.ShapeDtypeStruct((n, half, cols), out_dtype),
        compiler_params=_params(("parallel", "parallel")),
    )(c_idx, g, r)


def _scatter_to_chips(ps):
    na = len(ps)

    def body(*refs):
        p_refs, q_refs = refs[:na], refs[na:2 * na]
        send_sems, recv_sems = refs[2 * na:]
        x, y, c, chips = _place()
        k = 2 * x + y
        sends = []
        for i, (cx, cy) in enumerate(chips):
            for a in range(na):
                cp = pltpu.make_async_remote_copy(
                    src_ref=p_refs[a].at[2 * cx + cy], dst_ref=q_refs[a].at[k],
                    send_sem=send_sems.at[3 * a + i], recv_sem=recv_sems.at[3 * a + i],
                    device_id=(cx, cy, c), device_id_type=MESH)
                cp.start()
                sends.append(cp)
        for i, (cx, cy) in enumerate(chips):
            kk = 2 * cx + cy
            for a in range(na):
                pltpu.make_async_remote_copy(
                    src_ref=p_refs[a].at[kk], dst_ref=q_refs[a].at[kk],
                    send_sem=send_sems.at[3 * a + i], recv_sem=recv_sems.at[3 * a + i],
                    device_id=(cx, cy, c), device_id_type=MESH).wait_recv()
        for cp in sends:
            cp.wait_send()

    return pl.pallas_call(
        body, name="grads_scatter_to_chips",
        in_specs=[HBM_SPEC] * na, out_specs=[HBM_SPEC] * na,
        out_shape=[jax.ShapeDtypeStruct(p.shape, p.dtype) for p in ps],
        scratch_shapes=[pltpu.SemaphoreType.DMA((3 * na,)), pltpu.SemaphoreType.DMA((3 * na,))],
    )(*ps)


def _sum_chips(q, c_idx):
    n, half, cols = q.shape
    tr = _row_tile(half)
    nb = half // tr

    def body(c_ref, q_ref, o_ref):
        parts = [q_ref[kk].astype(F32) for kk in range(n)]
        o_ref[...] = ((parts[0] + parts[1]) + parts[2]) + parts[3]

    return pl.pallas_call(
        body, name="grads_sum_chips",
        grid_spec=pltpu.PrefetchScalarGridSpec(
            num_scalar_prefetch=1, grid=(nb,),
            in_specs=[pl.BlockSpec((n, tr, cols), lambda t, c_ref: (0, t, 0))],
            out_specs=pl.BlockSpec((tr, cols), lambda t, c_ref: (c_ref[0] * nb + t, 0))),
        out_shape=jax.ShapeDtypeStruct((2 * half, cols), F32),
        compiler_params=_params(("parallel",)),
    )(c_idx, q)


def _join_halves(fulls):
    na = len(fulls)

    def body(*refs):
        o_refs = refs[na:2 * na]
        send_sems, recv_sems = refs[2 * na:]
        x, y, c, _ = _place()
        sends = []
        for a in range(na):
            half = o_refs[a].shape[0] // 2
            rows = o_refs[a].at[pl.ds(c * half, half), :]
            sends.append(pltpu.make_async_remote_copy(
                src_ref=rows, dst_ref=rows, send_sem=send_sems.at[a], recv_sem=recv_sems.at[a],
                device_id=(x, y, 1 - c), device_id_type=MESH))
        for cp in sends:
            cp.start()
        for a in range(na):
            half = o_refs[a].shape[0] // 2
            other = o_refs[a].at[pl.ds((1 - c) * half, half), :]
            pltpu.make_async_remote_copy(
                src_ref=other, dst_ref=other, send_sem=send_sems.at[a], recv_sem=recv_sems.at[a],
                device_id=(x, y, 1 - c), device_id_type=MESH).wait_recv()
        for cp in sends:
            cp.wait_send()

    return pl.pallas_call(
        body, name="grads_join_halves",
        in_specs=[HBM_SPEC] * na, out_specs=[HBM_SPEC] * na,
        out_shape=[jax.ShapeDtypeStruct(f.shape, f.dtype) for f in fulls],
        input_output_aliases={a: a for a in range(na)},
        scratch_shapes=[pltpu.SemaphoreType.DMA((na,)), pltpu.SemaphoreType.DMA((na,))],
    )(*fulls)


def _part_rows(shape):
    size = 1
    for d in shape:
        size *= d
    rows = -(-size // PACK_COLS)
    return size, -(-rows // PACK_ROW_ALIGN) * PACK_ROW_ALIGN


def _pack_rows(arrays, dtype, total_rows):
    parts, used = [], 0
    for a in arrays:
        size, rows = _part_rows(a.shape)
        flat = a.reshape(-1).astype(dtype)
        parts.append(jnp.pad(flat, (0, rows * PACK_COLS - size)).reshape(rows, PACK_COLS))
        used += rows
    parts.append(jnp.zeros((total_rows - used, PACK_COLS), dtype))
    return jnp.concatenate(parts, axis=0)


def _unpack_rows(buf, shapes):
    lead = buf.shape[:-2]
    out, off = [], 0
    for sh in shapes:
        size, rows = _part_rows(sh)
        part = buf[..., off:off + rows, :].reshape(lead + (-1,))[..., :size]
        out.append(part.reshape(lead + tuple(sh)))
        off += rows
    return out


NEW_ORDER = ((928, 1440), (1440, 1952), (1952, 2464), (416, 928), (2464, 2976), (3744, 4256),
             (2976, 3488), (0, 256), (256, 384), (4256, 4320), (384, 416), (4256, 4288),
             (3488, 3616), (3616, 3744))
OLD_ORDER = ((3584, 3840), (3840, 3968), (4032, 4064), (1536, 2048), (0, 512), (512, 1024),
             (1024, 1536), (2048, 2560), (3072, 3584), (4096, 4224), (4224, 4352), (2560, 3072))


def _cols(sources, ranges):
    parts = []
    for a, b in ranges:
        off = 0
        for src in sources:
            width = src.shape[-1]
            lo, hi = max(a, off), min(b, off + width)
            if lo < hi:
                parts.append(src[..., lo - off:hi - off])
            off += width
    return jnp.concatenate(parts, axis=-1)


def _sub_ranges(ranges, a, b):
    out, off = [], 0
    for lo, hi in ranges:
        width = hi - lo
        s0, s1 = max(a, off), min(b, off + width)
        if s0 < s1:
            out.append((lo + s0 - off, lo + s1 - off))
        off += width
    return out


def _rope_tables(s):
    half = MLA_ROPE // 2
    inv_freq = jnp.power(jnp.float32(ROPE_THETA), -jnp.arange(half, dtype=F32) / half)
    ang = inv_freq[:, None] * jnp.arange(s, dtype=F32)[None, :]
    cos, sin = jnp.cos(ang), jnp.sin(ang)
    z = lambda n: jnp.zeros((n, s), F32)
    c = jnp.concatenate([jnp.ones((MLA_NOPE, s), F32), cos, cos, z(32)], axis=0)
    s1 = jnp.concatenate([z(MLA_NOPE), -sin, z(16), z(32)], axis=0)
    s2 = jnp.concatenate([z(MLA_NOPE), z(16), sin, z(32)], axis=0)
    return c, s1, s2


def _pad_lanes(a, n):
    return jnp.pad(a, ((0, 0), (0, n - a.shape[1])))


SHARDED = ("w_in", "w_out", "mla_w_qb", "mla_w_kvb", "conv_w")
REPLICATED = ("norm_g", "mla_q_a_norm", "mla_kv_a_norm", "mla_q_norm", "mla_k_norm",
              "swa_q_norm", "swa_k_norm", "swa_sinks")
WEIGHT_ORDER = ("norm_g", "w_in", "mla_q_a_norm", "mla_w_qb", "mla_kv_a_norm", "mla_w_kvb",
                "mla_q_norm", "mla_k_norm", "conv_w", "swa_q_norm", "swa_k_norm", "swa_sinks", "w_out")
SHARD_AXIS = {"w_in": 2, "w_out": 1, "mla_w_qb": 2, "mla_w_kvb": 2, "conv_w": 2}


def kernel(x, norm_g, w_in, mla_q_a_norm, mla_w_qb, mla_kv_a_norm, mla_w_kvb, mla_q_norm, mla_k_norm, conv_w, swa_q_norm, swa_k_norm, swa_sinks, w_out, loss_target, m_norm_g, m_w_in, m_mla_q_a_norm, m_mla_w_qb, m_mla_kv_a_norm, m_mla_w_kvb, m_mla_q_norm, m_mla_k_norm, m_conv_w, m_swa_q_norm, m_swa_k_norm, m_swa_sinks, m_w_out, v_norm_g, v_w_in, v_mla_q_a_norm, v_mla_w_qb, v_mla_kv_a_norm, v_mla_w_kvb, v_mla_q_norm, v_mla_k_norm, v_conv_w, v_swa_q_norm, v_swa_k_norm, v_swa_sinks, v_w_out):
    weights = dict(norm_g=norm_g, w_in=w_in, mla_q_a_norm=mla_q_a_norm, mla_w_qb=mla_w_qb,
                   mla_kv_a_norm=mla_kv_a_norm, mla_w_kvb=mla_w_kvb, mla_q_norm=mla_q_norm,
                   mla_k_norm=mla_k_norm, conv_w=conv_w, swa_q_norm=swa_q_norm,
                   swa_k_norm=swa_k_norm, swa_sinks=swa_sinks, w_out=w_out)
    mom_m = dict(norm_g=m_norm_g, w_in=m_w_in, mla_q_a_norm=m_mla_q_a_norm, mla_w_qb=m_mla_w_qb,
                 mla_kv_a_norm=m_mla_kv_a_norm, mla_w_kvb=m_mla_w_kvb, mla_q_norm=m_mla_q_norm,
                 mla_k_norm=m_mla_k_norm, conv_w=m_conv_w, swa_q_norm=m_swa_q_norm,
                 swa_k_norm=m_swa_k_norm, swa_sinks=m_swa_sinks, w_out=m_w_out)
    mom_v = dict(norm_g=v_norm_g, w_in=v_w_in, mla_q_a_norm=v_mla_q_a_norm, mla_w_qb=v_mla_w_qb,
                 mla_kv_a_norm=v_mla_kv_a_norm, mla_w_kvb=v_mla_w_kvb, mla_q_norm=v_mla_q_norm,
                 mla_k_norm=v_mla_k_norm, conv_w=v_conv_w, swa_q_norm=v_swa_q_norm,
                 swa_k_norm=v_swa_k_norm, swa_sinks=v_swa_sinks, w_out=v_w_out)
    xs = x[0]
    target = loss_target[0]
    s = xs.shape[0]
    c_idx = lax.axis_index("c").astype(jnp.int32).reshape(1)
    k_idx = (2 * lax.axis_index("x") + lax.axis_index("y")).astype(jnp.int32).reshape(1)

    conv_bits = lax.bitcast_convert_type(conv_w, BF16)
    small_list = [mla_w_qb, mla_w_kvb, conv_bits]
    shard_cols = w_in.shape[2]
    w_in_b = w_in.astype(BF16)
    late = [w_in_b[1], w_out.astype(BF16).reshape(-1, D_MODEL)]
    own = [w_in_b[0], _pack_rows(small_list, BF16, SMALL_ROWS)]
    gathered_in0, gathered_small = [_fill_own_slab(buf, src, k_idx)
                                    for buf, src in zip(_all_gather(own), own)]
    started = _gather_start(late, gathered_in0)
    parts = _unpack_rows(gathered_small, [a.shape for a in small_list])
    join = lambda p, axis: jnp.concatenate([p[k] for k in range(N_CHIPS)], axis=axis)
    w_in_zeros = jnp.zeros((D_MODEL, 64), BF16)
    permuted = lambda slabs: _cols([slabs[k] for k in range(N_CHIPS)] + [w_in_zeros], NEW_ORDER)
    w_qb_full = join(parts[0], 2)
    w_kvb_full = join(parts[1], 2)
    conv_full = lax.bitcast_convert_type(join(parts[2], 2), F32)

    rope = _rope_tables(s)
    swa_tables = _swa_tables()
    layers = []
    for l in range(DEPTH):
        wq = jnp.pad(w_qb_full[l].reshape(MLA_Q_LORA, MLA_HEADS, MLA_QK),
                     ((0, 0), (0, 0), (0, LANES - MLA_QK))).reshape(MLA_Q_LORA, MLA_HEADS * LANES)
        kv = w_kvb_full[l].reshape(MLA_KV_LORA, MLA_HEADS, MLA_NOPE + MLA_V)
        wk = jnp.pad(kv[:, :, :MLA_NOPE], ((0, 0), (0, 0), (0, LANES - MLA_NOPE)))
        wkv = jnp.concatenate([wk.reshape(MLA_KV_LORA, MLA_HEADS * LANES),
                               kv[:, :, MLA_NOPE:].reshape(MLA_KV_LORA, MLA_HEADS * MLA_V)], axis=1)
        layers.append(dict(
            wq=wq, wkv=wkv,
            cw=jnp.pad(conv_full[l], ((0, HALO - 3), (0, 0))),
            g=norm_g[l][None],
            mla_norms=(mla_q_a_norm[l][:, None], mla_kv_a_norm[l][:, None],
                       _pad_lanes(mla_q_norm[l][None], LANES).T, _pad_lanes(mla_k_norm[l][None], LANES).T),
            mla_weights=(wq, wkv, wq.T, wkv.T),
            sqn=swa_q_norm[l][:, None], skn=swa_k_norm[l][:, None], sinks=swa_sinks[l][None]))

    saved = []
    h_in = xs
    layers[0]["w_in"] = permuted(gathered_in0)
    layers[0]["g"] = layers[0]["g"] + started[-1][0:1, 0:1]
    for l in range(DEPTH):
        p = layers[l]
        proj, hb = _in_proj_fwd(h_in, p["g"], p["w_in"])
        q, k, v, qt, kt, vt = _mla_prep_fwd(proj, p["mla_norms"], p["mla_weights"], rope)
        o_mla, lse = _mla_attn_fwd(q, k, vt)
        if l == 0:
            late_in1, late_out = [_fill_own_slab(buf, src, k_idx) for buf, src in
                                  zip(_gather_wait(started, len(late), o_mla), late)]
            layers[1]["w_in"] = permuted(late_in1)
            w_out_full = join(late_out.reshape(N_CHIPS, DEPTH, -1, D_MODEL), 1)
            for n in range(DEPTH):
                layers[n]["w_out"] = w_out_full[n]
        o_swa = _swa_fwd(proj, p["sqn"], p["skn"], p["sinks"], swa_tables)
        last = l == DEPTH - 1
        y, z, *loss_acc = _out_fwd(proj, o_mla, o_swa, h_in, p["w_out"], p["cw"],
                                   target if last else None)
        saved.append(dict(x=h_in, proj=proj, hb=hb, q=q, k=k, v=v, qt=qt, kt=kt, o_mla=o_mla, lse=lse,
                          o_swa=o_swa, z=z))
        h_in = y

    dy, loss_acc = h_in, loss_acc[0]
    loss = lax.psum(loss_acc[0, 0], ("x", "y", "c"))

    grads = {n: [None] * DEPTH for n in WEIGHT_ORDER}

    def in_chunks(l):
        return jnp.stack([_cols(grads["w_in"][l], _sub_ranges(OLD_ORDER, k * shard_cols,
                                                              (k + 1) * shard_cols))
                          for k in range(N_CHIPS)])

    def out_chunks(l):
        return grads["w_out"][l].reshape(N_CHIPS, -1, D_MODEL)

    for l in reversed(range(DEPTH)):
        p, a = layers[l], saved[l]
        dconv, dgates, do_mla, do_mla_t, delta, do_swa, dcw = _out_bwd(dy, a["proj"], a["o_mla"], a["o_swa"],
                                                             p["w_out"], p["cw"])
        grads["w_out"][l] = _weight_grads(a["z"], [dy], "dw_out")[0]
        grads["conv_w"][l] = dcw[0:3]
        delta_rows = jnp.transpose(delta, (1, 0)).reshape(MLA_HEADS // 2, 2, s)
        dq, dk, dv = _mla_attn_bwd(a["q"], a["qt"], a["k"], a["kt"], a["v"], do_mla, do_mla_t,
                                   a["lse"], delta_rows)
        mla_norms = p["mla_norms"]
        if l == 0:
            late_gs, from_sib = _split_wait("late_grads_to_sibling_wait", late_st1, 2,
                                            _plan_to_sibling, dq)
            late_p = [_add_sibling(g, r, c_idx, BF16) for g, r in zip(late_gs, from_sib)]
            late_st2 = _split_start("late_grads_scatter_start", late_p,
                                    [(v.shape, v.dtype) for v in late_p], 6, _plan_scatter)
            mla_norms = (mla_norms[0] + late_st2[-1][0:1, 0:1],) + tuple(mla_norms[1:])
        dmla, dqan, dkvan, dqn, dkn, dwq_t, dwkv_t = _mla_prep_bwd(
            a["proj"], mla_norms, p["mla_weights"], rope, dq, dk, dv)
        dwq, dwkv = dwq_t.T, dwkv_t.T
        dsq, dskv, dsqn, dskn, dsinks = _swa_bwd(a["proj"], p["sqn"], p["skn"], p["sinks"], swa_tables, do_swa)
        if l == 0:
            late_p, late_q = _split_wait("late_grads_scatter_wait", late_st2, 2, _plan_scatter, dsq)
            late_full = [_sum_chips(_fill_own_slab(q_, p_, k_idx), c_idx)
                         for q_, p_ in zip(late_q, late_p)]
        pieces = [dconv, dgates, dsq, dmla, dskv]
        grads["w_in"][l] = _weight_grads(a["hb"], pieces, "dw_in")
        gain = p["g"]
        if l == 0:
            gs0 = (in_chunks(0), out_chunks(0))
            last_p = [_add_sibling(g, r, c_idx, BF16)
                      for g, r in zip(gs0, _swap_halves_to_sibling(gs0))]
            last_st = _split_start("last_grads_scatter_start", last_p,
                                   [(v.shape, v.dtype) for v in last_p], 6, _plan_scatter)
            gain = gain + last_st[-1][0:1, 0:1]
        dx, dg = _in_proj_bwd(pieces, a["x"], gain, p["w_in"], dy)
        if l == 0:
            last_p, last_q = _split_wait("last_grads_scatter_wait", last_st, 2, _plan_scatter, dx)
            last_full = [_sum_chips(_fill_own_slab(q_, p_, k_idx), c_idx)
                         for q_, p_ in zip(last_q, last_p)]
        grads["norm_g"][l] = dg[0]
        grads["mla_q_a_norm"][l] = dqan[:, 0]
        grads["mla_kv_a_norm"][l] = dkvan[:, 0]
        grads["mla_q_norm"][l] = dqn[:MLA_QK, 0]
        grads["mla_k_norm"][l] = dkn[:MLA_QK, 0]
        grads["mla_w_qb"][l] = dwq.reshape(MLA_Q_LORA, MLA_HEADS, LANES)[:, :, :MLA_QK].reshape(
            MLA_Q_LORA, MLA_HEADS * MLA_QK)
        dwk = dwkv[:, :MLA_HEADS * LANES].reshape(MLA_KV_LORA, MLA_HEADS, LANES)[:, :, :MLA_NOPE]
        dwv = dwkv[:, MLA_HEADS * LANES:].reshape(MLA_KV_LORA, MLA_HEADS, MLA_V)
        grads["mla_w_kvb"][l] = jnp.concatenate([dwk, dwv], axis=2).reshape(
            MLA_KV_LORA, MLA_HEADS * (MLA_NOPE + MLA_V))
        grads["swa_q_norm"][l] = dsqn[:, 0]
        grads["swa_k_norm"][l] = dskn[:, 0]
        grads["swa_sinks"][l] = dsinks[0]
        dy = dx
        if l == DEPTH - 1:
            late_st1 = _split_start(
                "late_grads_to_sibling_start", [in_chunks(l), out_chunks(l)],
                [((N_CHIPS, D_MODEL // 2, shard_cols), F32),
                 ((N_CHIPS, D_MIX // N_CHIPS // 2, D_MODEL), F32)], 2, _plan_to_sibling)
    grad_x = dy[None]
    full_grads = {n: jnp.stack(grads[n]) for n in WEIGHT_ORDER if n not in ("w_in", "w_out")}

    rest = tuple(n for n in SHARDED if n not in ("w_in", "w_out"))
    rep_shapes = [weights[n].shape for n in REPLICATED]
    rep_grads = jnp.concatenate([full_grads[n].reshape(-1) for n in REPLICATED])

    def chunk(g, n, k):
        width = g.shape[SHARD_AXIS[n]] // N_CHIPS
        return lax.slice_in_dim(g, k * width, (k + 1) * width, axis=SHARD_AXIS[n])

    g_small = jnp.stack([_pack_rows([chunk(full_grads[n], n, k) for n in rest] + [rep_grads],
                                    F32, SMALL_ROWS) for k in range(N_CHIPS)])
    gs = (g_small,)
    partial = [_add_sibling(g, r, c_idx, F32) for g, r in zip(gs, _swap_halves_to_sibling(gs))]
    by_chip = [_fill_own_slab(q, p, k_idx) for q, p in zip(_scatter_to_chips(partial), partial)]
    g_small_mine, g_in0, g_out0, g_in1, g_out1 = _join_halves(
        [_sum_chips(q, c_idx) for q in by_chip] + last_full + late_full)

    vals = _unpack_rows(g_small_mine, [weights[n].shape for n in rest] + [(rep_grads.shape[0],)])
    grad = dict(zip(rest, vals[:-1]))
    grad["w_in"] = jnp.stack([g_in0, g_in1])
    grad["w_out"] = jnp.stack([g_out0, g_out1])
    off = 0
    for n, sh in zip(REPLICATED, rep_shapes):
        grad[n] = vals[-1][off:off + sh[0] * sh[1]].reshape(sh)
        off += sh[0] * sh[1]
    results = {}
    for n in ("w_in", "w_out"):
        view = lambda a, n=n: a.reshape(-1, weights[n].shape[-1])
        res = _adamw(view(grad[n]), view(weights[n]), view(mom_m[n]), view(mom_v[n]))
        results[n] = [r.reshape(weights[n].shape) for r in res]
    small = tuple(n for n in WEIGHT_ORDER if n not in results)
    res = _adamw_small(*([d[n] for n in small] for d in (grad, weights, mom_m, mom_v)))
    for a, n in enumerate(small):
        results[n] = [res[kind][a] for kind in range(3)]
    unpacked = [grad] + [{n: results[n][kind] for n in WEIGHT_ORDER} for kind in range(3)]
    outs = [loss, grad_x]
    for group in unpacked:
        outs += [group[n] for n in WEIGHT_ORDER]
    return tuple(outs)
```

```python
import jax
import numpy as np
import jax.numpy as jnp
from jax import lax
from jax.experimental import pallas as pl
from jax.experimental.pallas import tpu as pltpu

F32 = jnp.float32
BF16 = jnp.bfloat16

D_MODEL = 1024
DEPTH = 2
GROUP = 512
D_MIX = 3 * GROUP
BLOCK = 128
RMS_EPS = 1e-6
NEG_INF = -1e30
MLA_HEADS = 8
MLA_QK = 96
MLA_NOPE = 64
MLA_ROPE = 32
MLA_V = 64
V_AUG = 80
MLA_Q_LORA = 256
MLA_KV_LORA = 128
ROPE_THETA = 10000.0
SWA_HEADS = 8
SWA_KV = 2
SWA_GROUP = 4
SWA_DIM = 64
N_CHIPS = 4

NC = 4352
OFF_SQ, OFF_MLA, OFF_SKV = 3072, 3584, 4096

VMEM_LIMIT = 56 * 1024 * 1024
LANES = 128
PACK_COLS = 1024
PACK_ROW_ALIGN = 16
SMALL_ROWS = 256

ADAM_LR = 0.001
ADAM_B1 = 0.9
ADAM_B2 = 0.999
ADAM_EPS = 1e-08
ADAM_WD = 0.01
ADAM_STEP = 10

MESH = pl.DeviceIdType.MESH


def _params(sem, vmem=VMEM_LIMIT):
    return pltpu.CompilerParams(dimension_semantics=sem, vmem_limit_bytes=vmem)


def _dot(a, b, dims):
    return lax.dot_general(a.astype(BF16), b.astype(BF16), (dims, ((), ())),
                           preferred_element_type=F32)


def _mm(a, b):
    return _dot(a, b, ((1,), (0,)))


def _mm_nt(a, b):
    return _dot(a, b, ((1,), (1,)))


def _rms(x, g, n=None):
    n = x.shape[-1] if n is None else n
    ms = jnp.sum(x * x, axis=-1, keepdims=True) * (1.0 / n)
    return x * lax.rsqrt(ms + RMS_EPS) * g


def _sigmoid(x):
    return 1.0 / (1.0 + jnp.exp(-x))


def _in_proj_fwd(x, g, w):
    s = x.shape[0]
    tm = min(512, s)

    def body(x_ref, g_ref, w_ref, proj_ref, hbt_ref):
        h = _rms(x_ref[...], g_ref[...])
        hbt_ref[...] = jnp.transpose(h).astype(BF16)
        proj_ref[...] = jnp.dot(h.astype(BF16), w_ref[...], preferred_element_type=F32)

    return pl.pallas_call(
        body, name="in_proj_fwd", grid=(s // tm,),
        in_specs=[pl.BlockSpec((tm, D_MODEL), lambda i: (i, 0)),
                  pl.BlockSpec((1, D_MODEL), lambda i: (0, 0)),
                  pl.BlockSpec((D_MODEL, NC), lambda i: (0, 0))],
        out_specs=[pl.BlockSpec((tm, NC), lambda i: (i, 0)),
                   pl.BlockSpec((D_MODEL, tm), lambda i: (0, i))],
        out_shape=[jax.ShapeDtypeStruct((s, NC), F32), jax.ShapeDtypeStruct((D_MODEL, s), BF16)],
        compiler_params=_params(("parallel",)),
    )(x, g, w)


def _in_proj_bwd(pieces, x, g, w, dres):
    s = x.shape[0]
    tm = min(512, s)
    n_p = len(pieces)

    def body(*refs):
        p_refs = refs[:n_p]
        x_ref, g_ref, w_ref, dres_ref, dx_ref, dg_ref = refs[n_p:]
        dh = None
        off = 0
        for r in p_refs:
            width = r.shape[1]
            t = _mm_nt(r[...], w_ref[:, off:off + width])
            dh = t if dh is None else dh + t
            off += width
        _, vjp = jax.vjp(_rms, x_ref[...], g_ref[...])
        dx, dg = vjp(dh)
        dx_ref[...] = dx + dres_ref[...]

        @pl.when(pl.program_id(0) == 0)
        def _():
            dg_ref[...] = jnp.zeros_like(dg_ref)

        dg_ref[...] += dg

    in_specs = [pl.BlockSpec((tm, p.shape[1]), lambda i: (i, 0)) for p in pieces]
    in_specs += [pl.BlockSpec((tm, D_MODEL), lambda i: (i, 0)),
                 pl.BlockSpec((1, D_MODEL), lambda i: (0, 0)),
                 pl.BlockSpec((D_MODEL, NC), lambda i: (0, 0)),
                 pl.BlockSpec((tm, D_MODEL), lambda i: (i, 0))]
    return pl.pallas_call(
        body, name="in_proj_bwd", grid=(s // tm,),
        in_specs=in_specs,
        out_specs=[pl.BlockSpec((tm, D_MODEL), lambda i: (i, 0)),
                   pl.BlockSpec((1, D_MODEL), lambda i: (0, 0))],
        out_shape=[jax.ShapeDtypeStruct((s, D_MODEL), F32), jax.ShapeDtypeStruct((1, D_MODEL), F32)],
        compiler_params=_params(("arbitrary",)),
    )(*pieces, x, g, w, dres)


def _weight_grads(at, bs, name):
    m, s = at.shape
    nb = len(bs)
    tk = min(512, s)

    def body(a_ref, *refs):
        b_refs, o_refs = refs[:nb], refs[nb:]

        @pl.when(pl.program_id(0) == 0)
        def _():
            for o_ref in o_refs:
                o_ref[...] = jnp.zeros_like(o_ref)

        a = a_ref[...]
        for b_ref, o_ref in zip(b_refs, o_refs):
            o_ref[...] += _mm(a, b_ref[...])

    return pl.pallas_call(
        body, name=name, grid=(s // tk,),
        in_specs=[pl.BlockSpec((m, tk), lambda k: (0, k))]
        + [pl.BlockSpec((tk, b.shape[1]), lambda k: (k, 0)) for b in bs],
        out_specs=[pl.BlockSpec((m, b.shape[1]), lambda k: (0, 0)) for b in bs],
        out_shape=[jax.ShapeDtypeStruct((m, b.shape[1]), F32) for b in bs],
        compiler_params=_params(("arbitrary",)),
    )(at, *bs)


def _rms0(x, g, n=None):
    n = x.shape[0] if n is None else n
    ms = jnp.sum(x * x, axis=0, keepdims=True) * (1.0 / n)
    return x * lax.rsqrt(ms + RMS_EPS) * g


@jax.custom_vjp
def _rope0(t, c, s1, s2):
    return t * c + pltpu.roll(t, LANES - 16, 0) * s1 + pltpu.roll(t, 16, 0) * s2


def _rope0_fwd(t, c, s1, s2):
    return _rope0(t, c, s1, s2), (c, s1, s2)


def _rope0_bwd(res, g):
    c, s1, s2 = res
    dt = g * c + pltpu.roll(g * s1, 16, 0) + pltpu.roll(g * s2, LANES - 16, 0)
    return dt, jnp.zeros_like(c), jnp.zeros_like(s1), jnp.zeros_like(s2)


_rope0.defvjp(_rope0_fwd, _rope0_bwd)


@jax.custom_vjp
def _mmw(w, wt, x):
    return _mm(w, x)


def _mmw_fwd(w, wt, x):
    return _mm(w, x), (wt, x)


def _mmw_bwd(res, g):
    wt, x = res
    return _mm_nt(g, x), jnp.zeros_like(wt), _mm(wt, g)


_mmw.defvjp(_mmw_fwd, _mmw_bwd)


def _prep_fn(q_lat, kv_lat, kr, qan, kvan, qn, kn, wq, wk, wv, wqt, wkt, wvt, c, s1, s2, mm):
    tokens = q_lat.shape[1]
    rq = _rms0(q_lat, qan)
    rkv = _rms0(kv_lat, kvan)
    qn_b = jnp.broadcast_to(qn, (LANES, tokens))
    kn_b = jnp.broadcast_to(kn, (LANES, tokens))
    qs, ks = [], []
    for h in range(MLA_HEADS):
        qs.append(_rope0(_rms0(mm(wq[h], wqt[h], rq), qn_b, MLA_QK), c, s1, s2))
        ks.append(_rope0(_rms0(mm(wk[h], wkt[h], rkv) + kr, kn_b, MLA_QK), c, s1, s2))
    return tuple(qs), tuple(ks), mm(wv, wvt, rkv)


def _prep_weights(wq_ref, wkv_ref, wqt_ref, wkvt_ref):
    heads = range(MLA_HEADS)
    wq = tuple(wqt_ref[LANES * h:LANES * (h + 1), :].astype(F32) for h in heads)
    wk = tuple(wkvt_ref[LANES * h:LANES * (h + 1), :].astype(F32) for h in heads)
    wv = wkvt_ref[LANES * MLA_HEADS:, :].astype(F32)
    wqt = tuple(wq_ref[:, LANES * h:LANES * (h + 1)].astype(F32) for h in heads)
    wkt = tuple(wkv_ref[:, LANES * h:LANES * (h + 1)].astype(F32) for h in heads)
    wvt = wkv_ref[:, LANES * MLA_HEADS:].astype(F32)
    return wq, wk, wv, wqt, wkt, wvt


def _prep_in_specs(tm):
    const = lambda shape: pl.BlockSpec(shape, lambda i: (0, 0))
    col = lambda height: pl.BlockSpec((height, tm), lambda i: (0, i))
    return [pl.BlockSpec((tm, 512), lambda i: (i, OFF_MLA // 512)),
            const((MLA_Q_LORA, 1)), const((MLA_KV_LORA, 1)), const((LANES, 1)), const((LANES, 1)),
            const((MLA_Q_LORA, 1024)), const((MLA_KV_LORA, 1536)),
            const((1024, MLA_Q_LORA)), const((1536, MLA_KV_LORA)),
            col(LANES), col(LANES), col(LANES)]


def _prep_operands(blk_ref, refs):
    qan_ref, kvan_ref, qn_ref, kn_ref, wq_ref, wkv_ref, wqt_ref, wkvt_ref, c_ref, s1_ref, s2_ref = refs
    blk_t = jnp.transpose(blk_ref[...])
    diff = (blk_t[0:256], blk_t[256:384], blk_t[384:512],
            qan_ref[...], kvan_ref[...], qn_ref[...], kn_ref[...])
    weights = _prep_weights(wq_ref, wkv_ref, wqt_ref, wkvt_ref)
    return diff, weights, (c_ref[...], s1_ref[...], s2_ref[...])


def _mla_prep_fwd(proj, norms, weights, rope):
    s = proj.shape[0]
    tm = min(512, s)

    def body(blk_ref, *refs):
        ins, (q_ref, k_ref, v_ref, qt_ref, kt_ref, vt_ref) = refs[:11], refs[11:]
        diff, (wq, wk, wv, wqt, wkt, wvt), tables = _prep_operands(blk_ref, ins)
        qs, ks, v = _prep_fn(*diff, wq, wk, wv, wqt, wkt, wvt, *tables,
                             lambda w, wt, x: _mm(w, x))
        for h in range(MLA_HEADS):
            q2 = qs[h] * Q_PRESCALE
            qt_ref[LANES * h:LANES * (h + 1), :] = q2.astype(BF16)
            kt_ref[LANES * h:LANES * (h + 1), :] = ks[h].astype(BF16)
            q_ref[:, LANES * h:LANES * (h + 1)] = jnp.transpose(q2).astype(BF16)
            k_ref[:, LANES * h:LANES * (h + 1)] = jnp.transpose(ks[h]).astype(BF16)
        ones_row = (lax.broadcasted_iota(jnp.int32, (V_AUG - MLA_V, v.shape[1]), 0) == 0).astype(BF16)
        for h in range(MLA_HEADS):
            vt_ref[V_AUG * h:V_AUG * h + MLA_V, :] = v[MLA_V * h:MLA_V * (h + 1)].astype(BF16)
            vt_ref[V_AUG * h + MLA_V:V_AUG * (h + 1), :] = ones_row
        v_ref[...] = jnp.transpose(v).astype(BF16)

    row = lambda width: pl.BlockSpec((tm, width), lambda i: (i, 0))
    col = lambda height: pl.BlockSpec((height, tm), lambda i: (0, i))
    return pl.pallas_call(
        body, name="mla_prep_fwd", grid=(s // tm,),
        in_specs=_prep_in_specs(tm),
        out_specs=[row(1024), row(1024), row(512), col(1024), col(1024), col(MLA_HEADS * V_AUG)],
        out_shape=[jax.ShapeDtypeStruct((s, 1024), BF16), jax.ShapeDtypeStruct((s, 1024), BF16),
                   jax.ShapeDtypeStruct((s, 512), BF16), jax.ShapeDtypeStruct((1024, s), BF16),
                   jax.ShapeDtypeStruct((1024, s), BF16),
                   jax.ShapeDtypeStruct((MLA_HEADS * V_AUG, s), BF16)],
        compiler_params=_params(("parallel",)),
    )(proj, *norms, *weights, *rope)


def _mla_prep_bwd(proj, norms, weights, rope, dq, dk, dv):
    s = proj.shape[0]
    tm = min(512, s)

    def body(blk_ref, *refs):
        ins, (dq_ref, dk_ref, dv_ref) = refs[:11], refs[11:14]
        dblk_ref, dqan_ref, dkvan_ref, dqn_ref, dkn_ref, dwq_ref, dwkv_ref = refs[14:]
        diff, (wq, wk, wv, wqt, wkt, wvt), tables = _prep_operands(blk_ref, ins)

        def fn(q_lat, kv_lat, kr, qan, kvan, qn, kn, wq_, wk_, wv_):
            return _prep_fn(q_lat, kv_lat, kr, qan, kvan, qn, kn, wq_, wk_, wv_, wqt, wkt, wvt,
                            *tables, _mmw)

        _, vjp = jax.vjp(fn, *diff, wq, wk, wv)
        heads = range(MLA_HEADS)
        cts = (tuple(dq_ref[LANES * h:LANES * (h + 1), :] for h in heads),
               tuple(dk_ref[LANES * h:LANES * (h + 1), :] for h in heads), dv_ref[...])
        dq_lat, dkv_lat, dkr, dqan, dkvan, dqn, dkn, dwq_h, dwk_h, dwv = vjp(cts)
        dblk_ref[...] = jnp.transpose(
            jnp.concatenate([dq_lat, dkv_lat, dkr], axis=0)).astype(BF16)

        @pl.when(pl.program_id(0) == 0)
        def _():
            for r in (dqan_ref, dkvan_ref, dqn_ref, dkn_ref, dwq_ref, dwkv_ref):
                r[...] = jnp.zeros_like(r)

        dqan_ref[...] += dqan
        dkvan_ref[...] += dkvan
        dqn_ref[...] += dqn
        dkn_ref[...] += dkn
        for h in heads:
            dwq_ref[LANES * h:LANES * (h + 1), :] += dwq_h[h]
            dwkv_ref[LANES * h:LANES * (h + 1), :] += dwk_h[h]
        dwkv_ref[LANES * MLA_HEADS:, :] += dwv

    const = lambda shape: pl.BlockSpec(shape, lambda i: (0, 0))
    col = lambda height: pl.BlockSpec((height, tm), lambda i: (0, i))
    shapes = [(MLA_Q_LORA, 1), (MLA_KV_LORA, 1), (LANES, 1), (LANES, 1),
              (1024, MLA_Q_LORA), (1536, MLA_KV_LORA)]
    return pl.pallas_call(
        body, name="mla_prep_bwd", grid=(s // tm,),
        in_specs=_prep_in_specs(tm) + [col(1024), col(1024), col(512)],
        out_specs=[pl.BlockSpec((tm, 512), lambda i: (i, 0))] + [const(sh) for sh in shapes],
        out_shape=[jax.ShapeDtypeStruct((s, 512), BF16)]
        + [jax.ShapeDtypeStruct(sh, F32) for sh in shapes],
        compiler_params=_params(("arbitrary",)),
    )(proj, *norms, *weights, *rope, dq, dk, dv)


MLA_SCALE = MLA_QK ** -0.5


LOG2E = 1.4426950408889634
LN2 = 0.6931471805599453
Q_PRESCALE = MLA_SCALE * LOG2E
HEAD_GROUPS = ((0, 1),)


def _mla_attn_fwd(q2, k, vt):
    s = q2.shape[0]
    t = min(512, s)
    tk = min(128, s)
    nq = s // t
    r = t // tk

    def body(q_ref, k_ref, vt_ref, o_ref, lse_ref, acc_ref):
        i = pl.program_id(1)
        row = lax.broadcasted_iota(jnp.int32, (tk, t), 0)
        col = lax.broadcasted_iota(jnp.int32, (tk, t), 1)
        qh = [q_ref[:, LANES * hh:LANES * (hh + 1)] for hh in range(2)]
        acc_ref[...] = jnp.zeros_like(acc_ref)

        def scores(j, heads, diag=None):
            r0 = pl.multiple_of(j * tk, tk)
            q_lo = 0 if diag is None else diag * tk
            out = []
            for hh in heads:
                kc = k_ref[pl.ds(r0, tk), LANES * hh:LANES * (hh + 1)]
                sc = lax.dot_general(kc, qh[hh][q_lo:], (((1,), (1,)), ((), ())),
                                     preferred_element_type=F32)
                out.append(sc if diag is None
                           else jnp.where(row[:, :t - q_lo] <= col[:, :t - q_lo], sc, NEG_INF))
            return tuple(out)

        for heads in HEAD_GROUPS:
            stats = tuple(jnp.full((1, t), NEG_INF, F32) for _ in heads)

            def consume(j, scs, stats, q_lo=0, heads=heads):
                r0 = pl.multiple_of(j * tk, tk)
                out, ps, alphas = [], [], []
                for n, hh in enumerate(heads):
                    m_old = stats[n][:, q_lo:]
                    m_new = jnp.maximum(m_old, jnp.max(scs[n], axis=0, keepdims=True))
                    ps.append(jnp.exp2(scs[n] - m_new).astype(BF16))
                    alphas.append(jnp.exp2(m_old - m_new))
                    out.append(m_new if q_lo == 0
                               else jnp.concatenate([stats[n][:, :q_lo], m_new], axis=1))
                for n, hh in enumerate(heads):
                    vc = vt_ref[V_AUG * hh:V_AUG * (hh + 1), pl.ds(r0, tk)]
                    acc_ref[hh, :, q_lo:] = alphas[n] * acc_ref[hh, :, q_lo:] + jnp.dot(
                        vc, ps[n], preferred_element_type=F32)
                return tuple(out)

            def group(j0, stats, diag, heads=heads):
                scs = [scores(j0 + d, heads, d if diag else None) for d in range(r)]
                for d in range(r):
                    stats = consume(j0 + d, scs[d], stats, d * tk if diag else 0)
                return stats

            stats = group(r * i, stats, True)
            odd = i % 2
            stats = lax.fori_loop(0, odd, lambda j, st: group(r * j, st, False), stats)
            stats = lax.fori_loop(
                0, i // 2,
                lambda p, st: group(r * (odd + 2 * p + 1),
                                    group(r * (odd + 2 * p), st, False), False), stats)
            for n, hh in enumerate(heads):
                l = acc_ref[hh, MLA_V:MLA_V + 1, :]
                o_ref[:, MLA_V * hh:MLA_V * (hh + 1)] = jnp.transpose(acc_ref[hh, 0:MLA_V, :] / l)
                lse_ref[0, hh:hh + 1, :] = stats[n] + jnp.log2(l)

    return pl.pallas_call(
        body, name="mla_attn_fwd", grid=(MLA_HEADS // 2, nq),
        in_specs=[pl.BlockSpec((t, 256), lambda p, i: (i, p)),
                  pl.BlockSpec((s, 256), lambda p, i: (0, p)),
                  pl.BlockSpec((2 * V_AUG, s), lambda p, i: (p, 0))],
        out_specs=[pl.BlockSpec((t, 128), lambda p, i: (i, p)),
                   pl.BlockSpec((1, 2, t), lambda p, i: (p, 0, i))],
        out_shape=[jax.ShapeDtypeStruct((s, 512), F32),
                   jax.ShapeDtypeStruct((MLA_HEADS // 2, 2, s), F32)],
        scratch_shapes=[pltpu.VMEM((2, V_AUG, t), F32)],
        compiler_params=_params(("parallel", "arbitrary")),
    )(q2, k, vt)


def _mla_attn_bwd(q2, q2t, k, kt, v, do, dot, lse_rows, delta_rows):
    s = q2.shape[0]
    t = min(512, s)
    nq = s // t

    def body(q_ref, qt_ref, k_ref, kt_ref, v_ref, do_ref, dot_ref, lse_ref, dl_ref,
             dq_ref, dk_ref, dv_ref):
        j = pl.program_id(1)

        @pl.when(j == 0)
        def _():
            dq_ref[...] = jnp.zeros_like(dq_ref)

        dk_ref[...] = jnp.zeros_like(dk_ref)
        dv_ref[...] = jnp.zeros_like(dv_ref)
        row = lax.broadcasted_iota(jnp.int32, (t, t), 0)
        col = lax.broadcasted_iota(jnp.int32, (t, t), 1)
        causal_t = row <= col
        kh = [k_ref[:, LANES * hh:LANES * (hh + 1)] for hh in range(2)]
        kth = [kt_ref[LANES * hh:LANES * (hh + 1), :] for hh in range(2)]
        vh = [v_ref[:, MLA_V * hh:MLA_V * (hh + 1)] for hh in range(2)]
        nt = (((1,), (1,)), ((), ()))

        def step(i, masked):
            r0 = pl.multiple_of(i * t, t)
            sd = []
            for hh in range(2):
                qh = q_ref[pl.ds(r0, t), LANES * hh:LANES * (hh + 1)]
                doh = do_ref[pl.ds(r0, t), MLA_V * hh:MLA_V * (hh + 1)]
                sc_t = lax.dot_general(kh[hh], qh, nt, preferred_element_type=F32)
                sd.append(jnp.where(causal_t, sc_t, NEG_INF) if masked else sc_t)
                sd.append(lax.dot_general(vh[hh], doh, nt, preferred_element_type=F32))
            for hh in range(2):
                lse = lse_ref[0, hh:hh + 1, pl.ds(r0, t)]
                dl = dl_ref[0, hh:hh + 1, pl.ds(r0, t)]
                p_t = jnp.exp2(sd[2 * hh] - lse)
                g_t = (p_t * (sd[2 * hh + 1] - dl)).astype(BF16)
                qth = qt_ref[LANES * hh:LANES * (hh + 1), pl.ds(r0, t)]
                doth = dot_ref[MLA_V * hh:MLA_V * (hh + 1), pl.ds(r0, t)]
                dv_ref[MLA_V * hh:MLA_V * (hh + 1), :] += lax.dot_general(
                    doth, p_t.astype(BF16), nt, preferred_element_type=F32)
                dk_ref[LANES * hh:LANES * (hh + 1), :] += lax.dot_general(
                    qth, g_t, nt, preferred_element_type=F32)
                dq_ref[LANES * hh:LANES * (hh + 1), pl.ds(r0, t)] += jnp.dot(
                    kth[hh], g_t, preferred_element_type=F32)

        step(j, True)

        rest = nq - 1 - j

        @pl.when(rest % 2 == 1)
        def _():
            step(j + 1, False)

        first = j + 1 + rest % 2

        def trip(p, carry):
            step(first + 2 * p, False)
            step(first + 2 * p + 1, False)
            return carry

        lax.fori_loop(0, rest // 2, trip, 0)
        dk_ref[...] = dk_ref[...] * LN2

        @pl.when(j == nq - 1)
        def _():
            dq_ref[...] = dq_ref[...] * MLA_SCALE

    return pl.pallas_call(
        body, name="mla_attn_bwd", grid=(MLA_HEADS // 2, nq),
        in_specs=[pl.BlockSpec((s, 256), lambda p, j: (0, p)),
                  pl.BlockSpec((256, s), lambda p, j: (p, 0)),
                  pl.BlockSpec((t, 256), lambda p, j: (j, p)),
                  pl.BlockSpec((256, t), lambda p, j: (p, j)),
                  pl.BlockSpec((t, 128), lambda p, j: (j, p)),
                  pl.BlockSpec((s, 128), lambda p, j: (0, p)),
                  pl.BlockSpec((128, s), lambda p, j: (p, 0)),
                  pl.BlockSpec((1, 2, s), lambda p, j: (p, 0, 0)),
                  pl.BlockSpec((1, 2, s), lambda p, j: (p, 0, 0))],
        out_specs=[pl.BlockSpec((256, s), lambda p, j: (p, 0)),
                   pl.BlockSpec((256, t), lambda p, j: (p, j)),
                   pl.BlockSpec((128, t), lambda p, j: (p, j))],
        out_shape=[jax.ShapeDtypeStruct((1024, s), F32), jax.ShapeDtypeStruct((1024, s), F32),
                   jax.ShapeDtypeStruct((512, s), F32)],
        compiler_params=_params(("parallel", "arbitrary")),
    )(q2, q2t, k, kt, v, do, dot, lse_rows, delta_rows)


SWA_SCALE = SWA_DIM ** -0.5
SWA_COLS = SWA_GROUP * BLOCK
SWA_LOG2 = SWA_SCALE * LOG2E


def _swa_tables():
    k = np.arange(2 * BLOCK)[:, None]
    col = np.arange(SWA_COLS)[None, :]
    dist = BLOCK + (col % BLOCK) - k
    valid = (dist >= 0) & (dist < BLOCK)
    out = np.zeros((2, SWA_KV, 2 * BLOCK, SWA_COLS), np.float32)
    for first in range(2):
        ok = valid & ((k >= BLOCK) | (first == 0))
        for j in range(SWA_KV):
            slope = 2.0 ** -(SWA_GROUP * j + col // BLOCK + 1)
            out[first, j] = np.where(ok, -slope * dist * LOG2E, NEG_INF)
    return jnp.asarray(out)


def _swa_tile_inputs(sq_ref, skv_ref, halo_ref, qn_ref, kn_ref, sk_ref, add_ref, first):
    tokens = sq_ref.shape[0]
    kv_all = jnp.concatenate([halo_ref[...], skv_ref[...]], axis=0)
    kv_t = jnp.transpose(kv_all)
    sq_t = jnp.transpose(sq_ref[...])
    k_raw = [kv_t[SWA_DIM * j:SWA_DIM * (j + 1)] for j in range(SWA_KV)]
    v_t = [kv_t[128 + SWA_DIM * j:128 + SWA_DIM * (j + 1)] for j in range(SWA_KV)]
    v_nat = [kv_all[:, 128 + SWA_DIM * j:128 + SWA_DIM * (j + 1)] for j in range(SWA_KV)]
    q_raw = [sq_t[SWA_DIM * h:SWA_DIM * (h + 1)] for h in range(SWA_HEADS)]
    qn_b = jnp.broadcast_to(qn_ref[...], (SWA_DIM, tokens))
    kn_b = jnp.broadcast_to(kn_ref[...], (SWA_DIM, tokens + BLOCK))
    lane_grp = lax.broadcasted_iota(jnp.int32, (1, SWA_COLS), 1) // BLOCK
    sinks, adds = [], []
    for j in range(SWA_KV):
        row = jnp.zeros((1, SWA_COLS), F32)
        for g in range(SWA_GROUP):
            h = SWA_GROUP * j + g
            row = jnp.where(lane_grp == g, sk_ref[:, h:h + 1] * LOG2E, row)
        sinks.append(row)
        adds.append((jnp.where(first, add_ref[1, j], add_ref[0, j]), add_ref[0, j]))
    return k_raw, v_t, v_nat, q_raw, qn_b, kn_b, sinks, adds


def _swa_probs(kb, qs_t, add, sink):
    s2 = jnp.dot(kb, qs_t, preferred_element_type=F32) * SWA_LOG2 + add
    m = jnp.maximum(jnp.max(s2, axis=0, keepdims=True), sink)
    e = jnp.exp2(s2 - m)
    es = jnp.exp2(sink - m)
    inv = 1.0 / (jnp.sum(e, axis=0, keepdims=True) + es)
    return e, inv, es


def _swa_queries(qn_t, j, b):
    return jnp.concatenate([qn_t[SWA_GROUP * j + g][:, BLOCK * b:BLOCK * (b + 1)]
                            for g in range(SWA_GROUP)], axis=1)


def _swa_fwd(proj, qn, kn, sinks, tables):
    s = proj.shape[0]
    ts = min(1024, s)
    nb = ts // BLOCK

    def body(sq_ref, skv_ref, halo_ref, qn_ref, kn_ref, sk_ref, add_ref, o_ref, ot_ref):
        first = pl.program_id(0) == 0
        k_raw, v_t, _, q_raw, qn_b, kn_b, sink_rows, adds = _swa_tile_inputs(
            sq_ref, skv_ref, halo_ref, qn_ref, kn_ref, sk_ref, add_ref, first)
        kn_nat = [jnp.transpose(_rms0(k, kn_b)).astype(BF16) for k in k_raw]
        v_t = [v.astype(BF16) for v in v_t]
        qn_t = [_rms0(q, qn_b).astype(BF16) for q in q_raw]
        for b in range(nb):
            band = slice(BLOCK * b, BLOCK * (b + 2))
            for j in range(SWA_KV):
                e, inv, _ = _swa_probs(kn_nat[j][band], _swa_queries(qn_t, j, b),
                                       adds[j][0 if b == 0 else 1], sink_rows[j])
                o_t = jnp.dot(v_t[j][:, band], (e * inv).astype(BF16),
                              preferred_element_type=F32)
                for g in range(SWA_GROUP):
                    h = SWA_GROUP * j + g
                    ot_ref[SWA_DIM * h:SWA_DIM * (h + 1), BLOCK * b:BLOCK * (b + 1)] = (
                        o_t[:, BLOCK * g:BLOCK * (g + 1)])
        o_ref[...] = jnp.transpose(ot_ref[...])

    const = lambda shape: pl.BlockSpec(shape, lambda i: (0,) * len(shape))
    return pl.pallas_call(
        body, name="swa_fwd", grid=(s // ts,),
        in_specs=[pl.BlockSpec((ts, 512), lambda i: (i, OFF_SQ // 512)),
                  pl.BlockSpec((ts, 256), lambda i: (i, OFF_SKV // 256)),
                  pl.BlockSpec((BLOCK, 256), lambda i: (jnp.maximum(i * nb - 1, 0), OFF_SKV // 256)),
                  const((SWA_DIM, 1)), const((SWA_DIM, 1)), const((1, SWA_HEADS)),
                  const(tables.shape)],
        out_specs=pl.BlockSpec((ts, 512), lambda i: (i, 0)),
        out_shape=jax.ShapeDtypeStruct((s, 512), F32),
        scratch_shapes=[pltpu.VMEM((512, ts), F32)],
        compiler_params=_params(("parallel",)),
    )(proj, proj, proj, qn, kn, sinks, tables)


def _swa_bwd(proj, qn, kn, sinks, tables, do):
    s = proj.shape[0]
    ts = min(1024, s)
    nb = ts // BLOCK
    nt = s // ts

    def body(sq_ref, skv_ref, halo_ref, qn_ref, kn_ref, sk_ref, add_ref, do_ref,
             dsq_ref, dskv_ref, dqn_ref, dkn_ref, dsk_ref, carry_ref, dqt_ref, dkvt_ref):
        step = pl.program_id(0)
        first = step == nt - 1

        @pl.when(step == 0)
        def _():
            carry_ref[...] = jnp.zeros_like(carry_ref)
            dqn_ref[...] = jnp.zeros_like(dqn_ref)
            dkn_ref[...] = jnp.zeros_like(dkn_ref)
            dsk_ref[...] = jnp.zeros_like(dsk_ref)

        k_raw, v_t, v_nat, q_raw, qn_b, kn_b, sink_rows, adds = _swa_tile_inputs(
            sq_ref, skv_ref, halo_ref, qn_ref, kn_ref, sk_ref, add_ref, first)
        kn_f = [_rms0(k, kn_b) for k in k_raw]
        kn_t = [k.astype(BF16) for k in kn_f]
        kn_nat = [jnp.transpose(k).astype(BF16) for k in kn_f]
        v_nat = [v.astype(BF16) for v in v_nat]
        qn_t = [_rms0(q, qn_b).astype(BF16) for q in q_raw]
        do_t = jnp.transpose(do_ref[...].astype(F32)).astype(BF16)

        dkvt_ref[...] = jnp.zeros_like(dkvt_ref)
        dsink = [jnp.zeros((1, SWA_COLS), F32) for _ in range(SWA_KV)]
        nt_dims = (((1,), (1,)), ((), ()))
        for b in range(nb):
            rows = slice(BLOCK * b, BLOCK * (b + 1))
            band = slice(BLOCK * b, BLOCK * (b + 2))
            for j in range(SWA_KV):
                heads = [SWA_GROUP * j + g for g in range(SWA_GROUP)]
                qs_t = _swa_queries(qn_t, j, b)
                dos_t = jnp.concatenate([do_t[SWA_DIM * h:SWA_DIM * (h + 1), rows] for h in heads],
                                        axis=1)
                e, inv, es = _swa_probs(kn_nat[j][band], qs_t, adds[j][0 if b == 0 else 1],
                                        sink_rows[j])
                p = e * inv
                dp = jnp.dot(v_nat[j][band], dos_t, preferred_element_type=F32)
                dsum = jnp.sum(p * dp, axis=0, keepdims=True)
                dsink[j] = dsink[j] - es * inv * dsum
                g_t = (p * (dp - dsum) * SWA_SCALE).astype(BF16)
                dv_t = lax.dot_general(dos_t, p.astype(BF16), nt_dims,
                                       preferred_element_type=F32)
                dk_t = lax.dot_general(qs_t, g_t, nt_dims, preferred_element_type=F32)
                dq_t = jnp.dot(kn_t[j][:, band], g_t, preferred_element_type=F32)
                dkvt_ref[SWA_DIM * j:SWA_DIM * (j + 1), band] += dk_t
                dkvt_ref[128 + SWA_DIM * j:128 + SWA_DIM * (j + 1), band] += dv_t
                for g, h in enumerate(heads):
                    dqt_ref[SWA_DIM * h:SWA_DIM * (h + 1), rows] = dq_t[:, BLOCK * g:BLOCK * (g + 1)]

        dqn = jnp.zeros((SWA_DIM, 1), F32)
        for h in range(SWA_HEADS):
            _, vjp = jax.vjp(_rms0, q_raw[h], qn_ref[...])
            dq, dg = vjp(dqt_ref[SWA_DIM * h:SWA_DIM * (h + 1), :])
            dqt_ref[SWA_DIM * h:SWA_DIM * (h + 1), :] = dq
            dqn = dqn + dg
        dqn_ref[...] += dqn
        dsq_ref[...] = jnp.transpose(dqt_ref[...]).astype(BF16)
        dkn = jnp.zeros((SWA_DIM, 1), F32)
        lane_grp = lax.broadcasted_iota(jnp.int32, (1, SWA_COLS), 1) // BLOCK
        for j in range(SWA_KV):
            _, vjp = jax.vjp(_rms0, k_raw[j], kn_ref[...])
            dk, dg = vjp(dkvt_ref[SWA_DIM * j:SWA_DIM * (j + 1), :])
            dkvt_ref[SWA_DIM * j:SWA_DIM * (j + 1), :] = dk
            dkn = dkn + dg
            for g in range(SWA_GROUP):
                h = SWA_GROUP * j + g
                dsk_ref[:, h:h + 1] += jnp.sum(jnp.where(lane_grp == g, dsink[j], 0.0), axis=1,
                                               keepdims=True)
        dkn_ref[...] += dkn
        dkv = jnp.transpose(dkvt_ref[...])
        dskv_ref[0:ts - BLOCK, :] = dkv[BLOCK:ts].astype(BF16)
        dskv_ref[ts - BLOCK:ts, :] = (dkv[ts:ts + BLOCK] + carry_ref[...]).astype(BF16)
        carry_ref[...] = dkv[0:BLOCK]

    const = lambda shape: pl.BlockSpec(shape, lambda st: (0,) * len(shape))
    return pl.pallas_call(
        body, name="swa_bwd", grid=(nt,),
        in_specs=[pl.BlockSpec((ts, 512), lambda st: (nt - 1 - st, OFF_SQ // 512)),
                  pl.BlockSpec((ts, 256), lambda st: (nt - 1 - st, OFF_SKV // 256)),
                  pl.BlockSpec((BLOCK, 256),
                               lambda st: (jnp.maximum((nt - 1 - st) * nb - 1, 0), OFF_SKV // 256)),
                  const((SWA_DIM, 1)), const((SWA_DIM, 1)), const((1, SWA_HEADS)),
                  const(tables.shape),
                  pl.BlockSpec((ts, 512), lambda st: (nt - 1 - st, 0))],
        out_specs=[pl.BlockSpec((ts, 512), lambda st: (nt - 1 - st, 0)),
                   pl.BlockSpec((ts, 256), lambda st: (nt - 1 - st, 0)),
                   const((SWA_DIM, 1)), const((SWA_DIM, 1)), const((1, SWA_HEADS))],
        out_shape=[jax.ShapeDtypeStruct((s, 512), BF16), jax.ShapeDtypeStruct((s, 256), BF16),
                   jax.ShapeDtypeStruct((SWA_DIM, 1), F32), jax.ShapeDtypeStruct((SWA_DIM, 1), F32),
                   jax.ShapeDtypeStruct((1, SWA_HEADS), F32)],
        scratch_shapes=[pltpu.VMEM((BLOCK, 256), F32), pltpu.VMEM((512, ts), F32),
                        pltpu.VMEM((256, ts + BLOCK), F32)],
        compiler_params=_params(("arbitrary",)),
    )(proj, proj, proj, qn, kn, sinks, tables, do)


HALO = 8


def _shift_down(u, halo, k):
    tm = u.shape[0]
    rid = lax.broadcasted_iota(jnp.int32, u.shape, 0)
    out = pltpu.roll(u, k, 0)
    for r in range(k):
        out = jnp.where(rid == r, halo[HALO - k + r:HALO - k + r + 1, :], out)
    return out


def _shift_up(u, halo, k):
    tm = u.shape[0]
    rid = lax.broadcasted_iota(jnp.int32, u.shape, 0)
    out = pltpu.roll(u, tm - k, 0)
    for r in range(k):
        out = jnp.where(rid == tm - k + r, halo[r:r + 1, :], out)
    return out


def _conv_fwd_vals(conv_ref, convp_ref, cw_ref, is_first):
    c_h, c_b, c_c = conv_ref[:, 0:512], conv_ref[:, 512:1024], conv_ref[:, 1024:1536]
    u = c_c * c_h
    up = jnp.where(is_first, 0.0, convp_ref[:, 1024:1536] * convp_ref[:, 0:512])
    u1 = _shift_down(u, up, 1)
    u2 = _shift_down(u, up, 2)
    yc = cw_ref[0:1, :] * u2 + cw_ref[1:2, :] * u1 + cw_ref[2:3, :] * u
    return c_h, c_b, c_c, u, u1, u2, yc


def _out_fwd(proj, o_mla, o_swa, x, w_out, cw, target=None):
    s = proj.shape[0]
    tm = min(512, s)
    nt = s // tm
    with_loss = target is not None

    def body(*refs):
        conv_ref, convp_ref, gates_ref, om_ref, os_ref, x_ref, w_ref, cw_ref = refs[:8]
        if with_loss:
            t_ref, y_ref, zt_ref, loss_ref, z_ref = refs[8:]
        else:
            y_ref, zt_ref, z_ref = refs[8:]
        i = pl.program_id(0)
        _, c_b, _, _, _, _, yc = _conv_fwd_vals(conv_ref, convp_ref, cw_ref, i == 0)
        mix = (om_ref[...], c_b * yc, os_ref[...])
        for n in range(3):
            g = gates_ref[:, GROUP * n:GROUP * (n + 1)]
            z = mix[n] * (g * _sigmoid(g))
            z_ref[:, GROUP * n:GROUP * (n + 1)] = z.astype(BF16)
            zt_ref[GROUP * n:GROUP * (n + 1), :] = jnp.transpose(z).astype(BF16)
        y = x_ref[...] + jnp.dot(z_ref[...], w_ref[...], preferred_element_type=F32)
        if not with_loss:
            y_ref[...] = y
            return
        err = y - t_ref[...]
        y_ref[...] = err * (1.0 / D_MODEL)

        @pl.when(i == 0)
        def _():
            loss_ref[...] = jnp.zeros_like(loss_ref)

        sq = jnp.sum((err * err).reshape(tm // 8, 8, D_MODEL), axis=0)
        part = sq[:, 0:LANES]
        for c in range(1, D_MODEL // LANES):
            part = part + sq[:, LANES * c:LANES * (c + 1)]
        loss_ref[...] += part

        @pl.when(i == nt - 1)
        def _():
            loss_ref[...] = jnp.full(loss_ref.shape, (0.5 / D_MODEL) * jnp.sum(loss_ref[...]), F32)

    row = lambda width: pl.BlockSpec((tm, width), lambda i: (i, 0))
    in_specs = [pl.BlockSpec((tm, 1536), lambda i: (i, 0)),
                pl.BlockSpec((HALO, 1536), lambda i: (jnp.maximum(i * (tm // HALO) - 1, 0), 0)),
                pl.BlockSpec((tm, 1536), lambda i: (i, 1)),
                row(512), row(512), row(D_MODEL),
                pl.BlockSpec((D_MIX, D_MODEL), lambda i: (0, 0)),
                pl.BlockSpec((HALO, 512), lambda i: (0, 0))]
    out_specs = [row(D_MODEL), pl.BlockSpec((D_MIX, tm), lambda i: (0, i))]
    out_shape = [jax.ShapeDtypeStruct((s, D_MODEL), F32), jax.ShapeDtypeStruct((D_MIX, s), BF16)]
    operands = [proj, proj, proj, o_mla, o_swa, x, w_out, cw]
    if with_loss:
        in_specs.append(row(D_MODEL))
        out_specs.append(pl.BlockSpec((8, LANES), lambda i: (0, 0)))
        out_shape.append(jax.ShapeDtypeStruct((8, LANES), F32))
        operands.append(target)
    return pl.pallas_call(
        body, name="out_fwd_loss" if with_loss else "out_fwd", grid=(nt,),
        in_specs=in_specs, out_specs=out_specs, out_shape=out_shape,
        scratch_shapes=[pltpu.VMEM((tm, D_MIX), BF16)],
        compiler_params=_params(("arbitrary",) if with_loss else ("parallel",)),
    )(*operands)


def _out_bwd(dy, proj, o_mla, o_swa, w_out, cw):
    s = proj.shape[0]
    tm = min(512, s)
    nt = s // tm
    hb = tm // HALO

    def body(dy_ref, dyn_ref, conv_ref, convp_ref, convn_ref, gates_ref, gatesn_ref, om_ref, os_ref,
             w_ref, cw_ref,
             dconv_ref, dgates_ref, dom_ref, domt_ref, delta_ref, dos_ref, dcw_ref):
        i = pl.program_id(0)
        dz = _mm_nt(dy_ref[...], w_ref[...])

        def gate(n):
            g = gates_ref[:, GROUP * n:GROUP * (n + 1)]
            sg = _sigmoid(g)
            return g * sg, sg * (1.0 + g * (1.0 - sg))

        for n, o_ref, do_ref in ((0, om_ref, dom_ref), (2, os_ref, dos_ref)):
            silu, dsilu = gate(n)
            dzn = dz[:, GROUP * n:GROUP * (n + 1)]
            o = o_ref[...]
            do = dzn * silu
            do_ref[...] = do.astype(do_ref.dtype)
            dgates_ref[:, GROUP * n:GROUP * (n + 1)] = (dzn * o * dsilu).astype(BF16)
            if n == 0:
                domt_ref[...] = jnp.transpose(do).astype(BF16)
                t = do * o
                for h in range(MLA_HEADS):
                    delta_ref[:, h:h + 1] = jnp.sum(t[:, MLA_V * h:MLA_V * (h + 1)], axis=-1,
                                                    keepdims=True)

        c_h, c_b, c_c, u, u1, u2, yc = _conv_fwd_vals(conv_ref, convp_ref, cw_ref, i == 0)
        silu, dsilu = gate(1)
        dzc = dz[:, GROUP:2 * GROUP]
        dgates_ref[:, GROUP:2 * GROUP] = (dzc * (c_b * yc) * dsilu).astype(BF16)
        dycr = dzc * silu
        dyc = dycr * c_b
        gn = gatesn_ref[:, GROUP:2 * GROUP]
        dzc_n = _mm_nt(dyn_ref[...], w_ref[GROUP:2 * GROUP, :])
        dyc_n = jnp.where(i == nt - 1, 0.0, dzc_n * (gn * _sigmoid(gn)) * convn_ref[:, 512:1024])
        d1 = _shift_up(dyc, dyc_n, 1)
        d2 = _shift_up(dyc, dyc_n, 2)
        du = cw_ref[2:3, :] * dyc + cw_ref[1:2, :] * d1 + cw_ref[0:1, :] * d2
        dconv_ref[:, 0:512] = (du * c_c).astype(BF16)
        dconv_ref[:, 512:1024] = (dycr * yc).astype(BF16)
        dconv_ref[:, 1024:1536] = (du * c_h).astype(BF16)

        @pl.when(i == 0)
        def _():
            dcw_ref[...] = jnp.zeros_like(dcw_ref)

        for k, uk in enumerate((u2, u1, u)):
            dcw_ref[k:k + 1, :] += jnp.sum(dyc * uk, axis=0, keepdims=True)

    row = lambda width: pl.BlockSpec((tm, width), lambda i: (i, 0))
    prev = lambda i: jnp.maximum(i * hb - 1, 0)
    nxt = lambda i: jnp.minimum((i + 1) * hb, s // HALO - 1)
    return pl.pallas_call(
        body, name="out_bwd", grid=(nt,),
        in_specs=[row(D_MODEL),
                  pl.BlockSpec((HALO, D_MODEL), lambda i: (nxt(i), 0)),
                  pl.BlockSpec((tm, 1536), lambda i: (i, 0)),
                  pl.BlockSpec((HALO, 1536), lambda i: (prev(i), 0)),
                  pl.BlockSpec((HALO, 1536), lambda i: (nxt(i), 0)),
                  pl.BlockSpec((tm, 1536), lambda i: (i, 1)),
                  pl.BlockSpec((HALO, 1536), lambda i: (nxt(i), 1)),
                  row(512), row(512),
                  pl.BlockSpec((D_MIX, D_MODEL), lambda i: (0, 0)),
                  pl.BlockSpec((HALO, 512), lambda i: (0, 0))],
        out_specs=[row(1536), row(1536), row(512), pl.BlockSpec((512, tm), lambda i: (0, i)),
                   row(MLA_HEADS), row(512), pl.BlockSpec((HALO, 512), lambda i: (0, 0))],
        out_shape=[jax.ShapeDtypeStruct((s, 1536), BF16), jax.ShapeDtypeStruct((s, 1536), BF16),
                   jax.ShapeDtypeStruct((s, 512), BF16), jax.ShapeDtypeStruct((512, s), BF16),
                   jax.ShapeDtypeStruct((s, MLA_HEADS), F32),
                   jax.ShapeDtypeStruct((s, 512), BF16), jax.ShapeDtypeStruct((HALO, 512), F32)],
        compiler_params=_params(("arbitrary",)),
    )(dy, dy, proj, proj, proj, proj, proj, o_mla, o_swa, w_out, cw)


def _adam_update(g, w, m, v):
    c1 = 1.0 - ADAM_B1
    c2 = 1.0 - ADAM_B2
    bc1 = 1.0 - ADAM_B1 ** ADAM_STEP
    bc2 = 1.0 - ADAM_B2 ** ADAM_STEP
    m_new = ADAM_B1 * m + c1 * g
    v_new = ADAM_B2 * v + c2 * (g * g)
    delta = -ADAM_LR * ((m_new / bc1) / (jnp.sqrt(v_new / bc2) + ADAM_EPS) + ADAM_WD * w)
    return delta, m_new, v_new


def _adamw(g, w, m, v):
    rows = g.shape[0]
    tr = min(256, rows)

    def body(g_ref, w_ref, m_ref, v_ref, d_ref, mo_ref, vo_ref):
        d_ref[...], mo_ref[...], vo_ref[...] = _adam_update(g_ref[...], w_ref[...], m_ref[...],
                                                            v_ref[...])

    spec = pl.BlockSpec((tr, g.shape[1]), lambda i: (i, 0))
    return pl.pallas_call(
        body, name="adamw", grid=(rows // tr,),
        in_specs=[spec] * 4, out_specs=[spec] * 3,
        out_shape=[jax.ShapeDtypeStruct(g.shape, F32)] * 3,
        compiler_params=_params(("parallel",)),
    )(g, w, m, v)


def _adamw_small(gs, ws, ms, vs):
    n = len(gs)

    def body(*refs):
        ins, outs = refs[:4 * n], refs[4 * n:]
        for a in range(n):
            res = _adam_update(*(ins[kind * n + a][...] for kind in range(4)))
            for kind in range(3):
                outs[kind * n + a][...] = res[kind]

    vmem = pl.BlockSpec(memory_space=pltpu.VMEM)
    out = pl.pallas_call(
        body, name="adamw_small",
        in_specs=[vmem] * (4 * n), out_specs=[vmem] * (3 * n),
        out_shape=[jax.ShapeDtypeStruct(g.shape, F32) for _ in range(3) for g in gs],
    )(*gs, *ws, *ms, *vs)
    return out[:n], out[n:2 * n], out[2 * n:]


HBM_SPEC = pl.BlockSpec(memory_space=pltpu.HBM)


def _place():
    x, y, c = lax.axis_index("x"), lax.axis_index("y"), lax.axis_index("c")
    chips = [(1 - x, y), (x, 1 - y), (1 - x, 1 - y)]
    return x, y, c, chips


def _all_gather(shards):
    na = len(shards)
    halves = [sh.shape[0] // 2 for sh in shards]

    def body(*refs):
        w_refs, a_refs = refs[:na], refs[na:2 * na]
        send_sems, recv_sems = refs[2 * na:]
        x, y, c, chips = _place()
        k = 2 * x + y
        sib = (x, y, 1 - c)

        def slab(a, kk, hc):
            return a_refs[a].at[kk, pl.ds(hc * halves[a], halves[a]), :]

        def copy(a, n, src, dst, to):
            return pltpu.make_async_remote_copy(
                src_ref=src, dst_ref=dst, send_sem=send_sems.at[6 * a + n],
                recv_sem=recv_sems.at[6 * a + n], device_id=to, device_id_type=MESH)

        first = [copy(a, n, w_refs[a].at[pl.ds(c * halves[a], halves[a]), :], slab(a, k, c),
                      (cx, cy, c))
                 for n, (cx, cy) in enumerate(chips) for a in range(na)]
        for cp in first:
            cp.start()
        passed = []
        for n, (cx, cy) in enumerate(chips):
            kk = 2 * cx + cy
            for a in range(na):
                copy(a, n, slab(a, kk, c), slab(a, kk, c), (cx, cy, c)).wait_recv()
                fwd = copy(a, 3 + n, slab(a, kk, c), slab(a, kk, c), sib)
                fwd.start()
                passed.append(fwd)
        for n, (cx, cy) in enumerate(chips):
            kk = 2 * cx + cy
            for a in range(na):
                copy(a, 3 + n, slab(a, kk, 1 - c), slab(a, kk, 1 - c), sib).wait_recv()
        for cp in first + passed:
            cp.wait_send()

    return pl.pallas_call(
        body, name="weights_all_gather",
        in_specs=[HBM_SPEC] * na, out_specs=[HBM_SPEC] * na,
        out_shape=[jax.ShapeDtypeStruct((N_CHIPS,) + sh.shape, sh.dtype) for sh in shards],
        scratch_shapes=[pltpu.SemaphoreType.DMA((6 * na,)), pltpu.SemaphoreType.DMA((6 * na,))],
    )(*shards)


def _fill_own_slab(buf, src, k_idx):
    n, rows, cols = buf.shape
    tr = _row_tile(rows)
    slabs = src.ndim == 3

    def body(k_ref, src_ref, buf_ref, out_ref):
        out_ref[0] = src_ref[0] if slabs else src_ref[...]

    if slabs:
        src_spec = pl.BlockSpec((1, tr, cols), lambda t, k_ref: (k_ref[0], t, 0))
    else:
        src_spec = pl.BlockSpec((tr, cols), lambda t, k_ref: (t, 0))
    return pl.pallas_call(
        body, name="fill_own_slab",
        grid_spec=pltpu.PrefetchScalarGridSpec(
            num_scalar_prefetch=1, grid=(rows // tr,),
            in_specs=[src_spec, pl.BlockSpec(memory_space=pl.ANY)],
            out_specs=pl.BlockSpec((1, tr, cols), lambda t, k_ref: (k_ref[0], t, 0))),
        out_shape=jax.ShapeDtypeStruct(buf.shape, buf.dtype),
        input_output_aliases={2: 0},
        compiler_params=_params(("parallel",)),
    )(k_idx, src, buf)


SEM_SPEC = pl.BlockSpec(memory_space=pltpu.SEMAPHORE)
LATE_COPIES = 6


def _late_copy(a, j, peer_core, src, dst, send_sems, recv_sems, to, sender_core):
    return pltpu.make_async_remote_copy(
        src_ref=src, dst_ref=dst, send_sem=send_sems.at[LATE_COPIES * a + 2 * j + peer_core],
        recv_sem=recv_sems.at[LATE_COPIES * a + 2 * j + sender_core], device_id=to,
        device_id_type=MESH)


def _gather_start(shards, after):
    na = len(shards)

    def body(*refs):
        w_refs, land_refs = refs[:na], refs[na:2 * na]
        send_sems, recv_sems = refs[2 * na + 1], refs[2 * na + 2]
        token = refs[-1]
        x, y, c, chips = _place()
        k = 2 * x + y
        for a in range(na):
            half = w_refs[a].shape[0] // 2
            src = w_refs[a].at[pl.ds(c * half, half), :]
            dst = land_refs[a].at[k, pl.ds(c * half, half), :]
            for j, (cx, cy) in enumerate(chips):
                for tc in range(2):
                    _late_copy(a, j, tc, src, dst, send_sems, recv_sems, (cx, cy, tc), c).start()
        token[...] = jnp.zeros_like(token)

    lands = [pltpu.with_memory_space_constraint(lax.empty((N_CHIPS,) + sh.shape, sh.dtype), pltpu.HBM)
             for sh in shards]
    srcs = [pltpu.with_memory_space_constraint(sh, pltpu.HBM) for sh in shards]
    sems = pltpu.SemaphoreType.DMA((LATE_COPIES * na,))
    aliases = {a: 2 + a for a in range(2 * na)}
    return pl.pallas_call(
        body, name="late_weights_gather_start",
        in_specs=[HBM_SPEC] * (2 * na) + [pl.BlockSpec(memory_space=pl.ANY)],
        out_specs=[SEM_SPEC, SEM_SPEC] + [HBM_SPEC] * (2 * na) + [pl.BlockSpec(memory_space=pltpu.VMEM)],
        out_shape=[sems, sems] + [pltpu.HBM(v.shape, v.dtype) for v in srcs + lands]
        + [jax.ShapeDtypeStruct((8, LANES), F32)],
        input_output_aliases=aliases,
        compiler_params=pltpu.CompilerParams(
            has_side_effects=pltpu.SideEffectType.DATAFLOW_SIDE_EFFECTING),
    )(*srcs, *lands, after)


def _gather_wait(started, na, after):
    send_sems, recv_sems = started[0], started[1]
    bufs = started[2:2 + 2 * na]

    def body(*refs):
        w_refs, land_refs = refs[:na], refs[na:2 * na]
        send_sems, recv_sems = refs[2 * na], refs[2 * na + 1]
        x, y, c, chips = _place()
        k = 2 * x + y
        for a in range(na):
            half = w_refs[a].shape[0] // 2
            src = w_refs[a].at[pl.ds(c * half, half), :]
            for j, (cx, cy) in enumerate(chips):
                kk = 2 * cx + cy
                for pc in range(2):
                    _late_copy(a, j, pc, src, land_refs[a].at[k, pl.ds(c * half, half), :],
                               send_sems, recv_sems, (cx, cy, pc), c).wait_send()
                    pltpu.make_async_remote_copy(
                        src_ref=src, dst_ref=land_refs[a].at[kk, pl.ds(pc * half, half), :],
                        send_sem=send_sems.at[LATE_COPIES * a + 2 * j + pc],
                        recv_sem=recv_sems.at[LATE_COPIES * a + 2 * j + pc],
                        device_id=(cx, cy, pc), device_id_type=MESH).wait_recv()

    out = pl.pallas_call(
        body, name="late_weights_gather_wait",
        in_specs=[HBM_SPEC] * (2 * na) + [SEM_SPEC, SEM_SPEC, pl.BlockSpec(memory_space=pl.ANY)],
        out_specs=[HBM_SPEC] * (2 * na),
        out_shape=[pltpu.HBM(v.shape, v.dtype) for v in bufs],
        input_output_aliases={a: a for a in range(2 * na)},
        compiler_params=pltpu.CompilerParams(
            has_side_effects=pltpu.SideEffectType.DATAFLOW_SIDE_EFFECTING),
    )(*bufs, send_sems, recv_sems, after)
    return out[na:]


def _split_start(name, srcs, land_shapes, n_sems, plan):
    na = len(srcs)

    def body(*refs):
        sends, _ = plan(refs[:na], refs[na:2 * na], refs[2 * na], refs[2 * na + 1])
        for cp in sends:
            cp.start()
        refs[-1][...] = jnp.zeros_like(refs[-1])

    lands = [pltpu.with_memory_space_constraint(lax.empty(shape, dtype), pltpu.HBM)
             for shape, dtype in land_shapes]
    srcs = [pltpu.with_memory_space_constraint(v, pltpu.HBM) for v in srcs]
    sems = pltpu.SemaphoreType.DMA((n_sems,))
    return pl.pallas_call(
        body, name=name,
        in_specs=[HBM_SPEC] * (2 * na),
        out_specs=[SEM_SPEC, SEM_SPEC] + [HBM_SPEC] * (2 * na) + [pl.BlockSpec(memory_space=pltpu.VMEM)],
        out_shape=[sems, sems] + [pltpu.HBM(v.shape, v.dtype) for v in srcs + lands]
        + [jax.ShapeDtypeStruct((8, LANES), F32)],
        input_output_aliases={a: 2 + a for a in range(2 * na)},
        compiler_params=pltpu.CompilerParams(
            has_side_effects=pltpu.SideEffectType.DATAFLOW_SIDE_EFFECTING),
    )(*srcs, *lands)


def _split_wait(name, started, na, plan, after):
    bufs = started[2:2 + 2 * na]

    def body(*refs):
        sends, recvs = plan(refs[:na], refs[na:2 * na], refs[2 * na], refs[2 * na + 1])
        for cp in sends:
            cp.wait_send()
        for cp in recvs:
            cp.wait_recv()

    out = pl.pallas_call(
        body, name=name,
        in_specs=[HBM_SPEC] * (2 * na) + [SEM_SPEC, SEM_SPEC, pl.BlockSpec(memory_space=pl.ANY)],
        out_specs=[HBM_SPEC] * (2 * na),
        out_shape=[pltpu.HBM(v.shape, v.dtype) for v in bufs],
        input_output_aliases={a: a for a in range(2 * na)},
        compiler_params=pltpu.CompilerParams(
            has_side_effects=pltpu.SideEffectType.DATAFLOW_SIDE_EFFECTING),
    )(*bufs, started[0], started[1], after)
    return out[:na], out[na:]


def _plan_to_sibling(g_refs, r_refs, send_sems, recv_sems):
    x, y, c, _ = _place()
    cps = []
    for a, (g, r) in enumerate(zip(g_refs, r_refs)):
        half = g.shape[1] // 2
        cps.append(pltpu.make_async_remote_copy(
            src_ref=g.at[:, pl.ds((1 - c) * half, half), :], dst_ref=r, send_sem=send_sems.at[a],
            recv_sem=recv_sems.at[a], device_id=(x, y, 1 - c), device_id_type=MESH))
    return cps, cps


def _plan_scatter(p_refs, q_refs, send_sems, recv_sems):
    x, y, c, chips = _place()
    k = 2 * x + y
    sends, recvs = [], []
    for a, (p, q) in enumerate(zip(p_refs, q_refs)):
        for i, (cx, cy) in enumerate(chips):
            kk = 2 * cx + cy
            for dst, out in ((q.at[k], sends), (q.at[kk], recvs)):
                out.append(pltpu.make_async_remote_copy(
                    src_ref=p.at[kk], dst_ref=dst, send_sem=send_sems.at[3 * a + i],
                    recv_sem=recv_sems.at[3 * a + i], device_id=(cx, cy, c), device_id_type=MESH))
    return sends, recvs


def _swap_halves_to_sibling(gs):
    na = len(gs)

    def body(*refs):
        g_refs, r_refs = refs[:na], refs[na:2 * na]
        send_sems, recv_sems = refs[2 * na:]
        x, y, c, _ = _place()
        cps = []
        for a in range(na):
            half = g_refs[a].shape[1] // 2
            cps.append(pltpu.make_async_remote_copy(
                src_ref=g_refs[a].at[:, pl.ds((1 - c) * half, half), :], dst_ref=r_refs[a],
                send_sem=send_sems.at[a], recv_sem=recv_sems.at[a], device_id=(x, y, 1 - c),
                device_id_type=MESH))
        for cp in cps:
            cp.start()
        for cp in cps:
            cp.wait()

    return pl.pallas_call(
        body, name="grads_to_sibling",
        in_specs=[HBM_SPEC] * na, out_specs=[HBM_SPEC] * na,
        out_shape=[jax.ShapeDtypeStruct((g.shape[0], g.shape[1] // 2, g.shape[2]), g.dtype)
                   for g in gs],
        scratch_shapes=[pltpu.SemaphoreType.DMA((na,)), pltpu.SemaphoreType.DMA((na,))],
    )(*gs)


def _row_tile(rows):
    return next(t for t in (256, 128, 64) if rows % t == 0)


def _add_sibling(g, r, c_idx, out_dtype):
    n, rows, cols = g.shape
    half = rows // 2
    tr = _row_tile(half)
    nb = half // tr

    def body(c_ref, g_ref, r_ref, p_ref):
        p_ref[...] = (g_ref[...] + r_ref[...]).astype(out_dtype)

    return pl.pallas_call(
        body, name="grads_add_sibling",
        grid_spec=pltpu.PrefetchScalarGridSpec(
            num_scalar_prefetch=1, grid=(n, nb),
            in_specs=[pl.BlockSpec((1, tr, cols), lambda j, t, c_ref: (j, c_ref[0] * nb + t, 0)),
                      pl.BlockSpec((1, tr, cols), lambda j, t, c_ref: (j, t, 0))],
            out_specs=pl.BlockSpec((1, tr, cols), lambda j, t, c_ref: (j, t, 0))),
        out_shape=jax.ShapeDtypeStruct((n, half, cols), out_dtype),
        compiler_params=_params(("parallel", "parallel")),
    )(c_idx, g, r)


def _scatter_to_chips(ps):
    na = len(ps)

    def body(*refs):
        p_refs, q_refs = refs[:na], refs[na:2 * na]
        send_sems, recv_sems = refs[2 * na:]
        x, y, c, chips = _place()
        k = 2 * x + y
        sends = []
        for i, (cx, cy) in enumerate(chips):
            for a in range(na):
                cp = pltpu.make_async_remote_copy(
                    src_ref=p_refs[a].at[2 * cx + cy], dst_ref=q_refs[a].at[k],
                    send_sem=send_sems.at[3 * a + i], recv_sem=recv_sems.at[3 * a + i],
                    device_id=(cx, cy, c), device_id_type=MESH)
                cp.start()
                sends.append(cp)
        for i, (cx, cy) in enumerate(chips):
            kk = 2 * cx + cy
            for a in range(na):
                pltpu.make_async_remote_copy(
                    src_ref=p_refs[a].at[kk], dst_ref=q_refs[a].at[kk],
                    send_sem=send_sems.at[3 * a + i], recv_sem=recv_sems.at[3 * a + i],
                    device_id=(cx, cy, c), device_id_type=MESH).wait_recv()
        for cp in sends:
            cp.wait_send()

    return pl.pallas_call(
        body, name="grads_scatter_to_chips",
        in_specs=[HBM_SPEC] * na, out_specs=[HBM_SPEC] * na,
        out_shape=[jax.ShapeDtypeStruct(p.shape, p.dtype) for p in ps],
        scratch_shapes=[pltpu.SemaphoreType.DMA((3 * na,)), pltpu.SemaphoreType.DMA((3 * na,))],
    )(*ps)


def _sum_chips(q, c_idx):
    n, half, cols = q.shape
    tr = _row_tile(half)
    nb = half // tr

    def body(c_ref, q_ref, o_ref):
        parts = [q_ref[kk].astype(F32) for kk in range(n)]
        o_ref[...] = ((parts[0] + parts[1]) + parts[2]) + parts[3]

    return pl.pallas_call(
        body, name="grads_sum_chips",
        grid_spec=pltpu.PrefetchScalarGridSpec(
            num_scalar_prefetch=1, grid=(nb,),
            in_specs=[pl.BlockSpec((n, tr, cols), lambda t, c_ref: (0, t, 0))],
            out_specs=pl.BlockSpec((tr, cols), lambda t, c_ref: (c_ref[0] * nb + t, 0))),
        out_shape=jax.ShapeDtypeStruct((2 * half, cols), F32),
        compiler_params=_params(("parallel",)),
    )(c_idx, q)


def _join_halves(fulls):
    na = len(fulls)

    def body(*refs):
        o_refs = refs[na:2 * na]
        send_sems, recv_sems = refs[2 * na:]
        x, y, c, _ = _place()
        sends = []
        for a in range(na):
            half = o_refs[a].shape[0] // 2
            rows = o_refs[a].at[pl.ds(c * half, half), :]
            sends.append(pltpu.make_async_remote_copy(
                src_ref=rows, dst_ref=rows, send_sem=send_sems.at[a], recv_sem=recv_sems.at[a],
                device_id=(x, y, 1 - c), device_id_type=MESH))
        for cp in sends:
            cp.start()
        for a in range(na):
            half = o_refs[a].shape[0] // 2
            other = o_refs[a].at[pl.ds((1 - c) * half, half), :]
            pltpu.make_async_remote_copy(
                src_ref=other, dst_ref=other, send_sem=send_sems.at[a], recv_sem=recv_sems.at[a],
                device_id=(x, y, 1 - c), device_id_type=MESH).wait_recv()
        for cp in sends:
            cp.wait_send()

    return pl.pallas_call(
        body, name="grads_join_halves",
        in_specs=[HBM_SPEC] * na, out_specs=[HBM_SPEC] * na,
        out_shape=[jax.ShapeDtypeStruct(f.shape, f.dtype) for f in fulls],
        input_output_aliases={a: a for a in range(na)},
        scratch_shapes=[pltpu.SemaphoreType.DMA((na,)), pltpu.SemaphoreType.DMA((na,))],
    )(*fulls)


def _part_rows(shape):
    size = 1
    for d in shape:
        size *= d
    rows = -(-size // PACK_COLS)
    return size, -(-rows // PACK_ROW_ALIGN) * PACK_ROW_ALIGN


def _pack_rows(arrays, dtype, total_rows):
    parts, used = [], 0
    for a in arrays:
        size, rows = _part_rows(a.shape)
        flat = a.reshape(-1).astype(dtype)
        parts.append(jnp.pad(flat, (0, rows * PACK_COLS - size)).reshape(rows, PACK_COLS))
        used += rows
    parts.append(jnp.zeros((total_rows - used, PACK_COLS), dtype))
    return jnp.concatenate(parts, axis=0)


def _unpack_rows(buf, shapes):
    lead = buf.shape[:-2]
    out, off = [], 0
    for sh in shapes:
        size, rows = _part_rows(sh)
        part = buf[..., off:off + rows, :].reshape(lead + (-1,))[..., :size]
        out.append(part.reshape(lead + tuple(sh)))
        off += rows
    return out


NEW_ORDER = ((928, 1440), (1440, 1952), (1952, 2464), (416, 928), (2464, 2976), (3744, 4256),
             (2976, 3488), (0, 256), (256, 384), (4256, 4320), (384, 416), (4256, 4288),
             (3488, 3616), (3616, 3744))
OLD_ORDER = ((3584, 3840), (3840, 3968), (4032, 4064), (1536, 2048), (0, 512), (512, 1024),
             (1024, 1536), (2048, 2560), (3072, 3584), (4096, 4224), (4224, 4352), (2560, 3072))


def _cols(sources, ranges):
    parts = []
    for a, b in ranges:
        off = 0
        for src in sources:
            width = src.shape[-1]
            lo, hi = max(a, off), min(b, off + width)
            if lo < hi:
                parts.append(src[..., lo - off:hi - off])
            off += width
    return jnp.concatenate(parts, axis=-1)


def _sub_ranges(ranges, a, b):
    out, off = [], 0
    for lo, hi in ranges:
        width = hi - lo
        s0, s1 = max(a, off), min(b, off + width)
        if s0 < s1:
            out.append((lo + s0 - off, lo + s1 - off))
        off += width
    return out


def _rope_tables(s):
    half = MLA_ROPE // 2
    inv_freq = jnp.power(jnp.float32(ROPE_THETA), -jnp.arange(half, dtype=F32) / half)
    ang = inv_freq[:, None] * jnp.arange(s, dtype=F32)[None, :]
    cos, sin = jnp.cos(ang), jnp.sin(ang)
    z = lambda n: jnp.zeros((n, s), F32)
    c = jnp.concatenate([jnp.ones((MLA_NOPE, s), F32), cos, cos, z(32)], axis=0)
    s1 = jnp.concatenate([z(MLA_NOPE), -sin, z(16), z(32)], axis=0)
    s2 = jnp.concatenate([z(MLA_NOPE), z(16), sin, z(32)], axis=0)
    return c, s1, s2


def _pad_lanes(a, n):
    return jnp.pad(a, ((0, 0), (0, n - a.shape[1])))


SHARDED = ("w_in", "w_out", "mla_w_qb", "mla_w_kvb", "conv_w")
REPLICATED = ("norm_g", "mla_q_a_norm", "mla_kv_a_norm", "mla_q_norm", "mla_k_norm",
              "swa_q_norm", "swa_k_norm", "swa_sinks")
WEIGHT_ORDER = ("norm_g", "w_in", "mla_q_a_norm", "mla_w_qb", "mla_kv_a_norm", "mla_w_kvb",
                "mla_q_norm", "mla_k_norm", "conv_w", "swa_q_norm", "swa_k_norm", "swa_sinks", "w_out")
SHARD_AXIS = {"w_in": 2, "w_out": 1, "mla_w_qb": 2, "mla_w_kvb": 2, "conv_w": 2}


def kernel(x, norm_g, w_in, mla_q_a_norm, mla_w_qb, mla_kv_a_norm, mla_w_kvb, mla_q_norm, mla_k_norm, conv_w, swa_q_norm, swa_k_norm, swa_sinks, w_out, loss_target, m_norm_g, m_w_in, m_mla_q_a_norm, m_mla_w_qb, m_mla_kv_a_norm, m_mla_w_kvb, m_mla_q_norm, m_mla_k_norm, m_conv_w, m_swa_q_norm, m_swa_k_norm, m_swa_sinks, m_w_out, v_norm_g, v_w_in, v_mla_q_a_norm, v_mla_w_qb, v_mla_kv_a_norm, v_mla_w_kvb, v_mla_q_norm, v_mla_k_norm, v_conv_w, v_swa_q_norm, v_swa_k_norm, v_swa_sinks, v_w_out):
    weights = dict(norm_g=norm_g, w_in=w_in, mla_q_a_norm=mla_q_a_norm, mla_w_qb=mla_w_qb,
                   mla_kv_a_norm=mla_kv_a_norm, mla_w_kvb=mla_w_kvb, mla_q_norm=mla_q_norm,
                   mla_k_norm=mla_k_norm, conv_w=conv_w, swa_q_norm=swa_q_norm,
                   swa_k_norm=swa_k_norm, swa_sinks=swa_sinks, w_out=w_out)
    mom_m = dict(norm_g=m_norm_g, w_in=m_w_in, mla_q_a_norm=m_mla_q_a_norm, mla_w_qb=m_mla_w_qb,
                 mla_kv_a_norm=m_mla_kv_a_norm, mla_w_kvb=m_mla_w_kvb, mla_q_norm=m_mla_q_norm,
                 mla_k_norm=m_mla_k_norm, conv_w=m_conv_w, swa_q_norm=m_swa_q_norm,
                 swa_k_norm=m_swa_k_norm, swa_sinks=m_swa_sinks, w_out=m_w_out)
    mom_v = dict(norm_g=v_norm_g, w_in=v_w_in, mla_q_a_norm=v_mla_q_a_norm, mla_w_qb=v_mla_w_qb,
                 mla_kv_a_norm=v_mla_kv_a_norm, mla_w_kvb=v_mla_w_kvb, mla_q_norm=v_mla_q_norm,
                 mla_k_norm=v_mla_k_norm, conv_w=v_conv_w, swa_q_norm=v_swa_q_norm,
                 swa_k_norm=v_swa_k_norm, swa_sinks=v_swa_sinks, w_out=v_w_out)
    xs = x[0]
    target = loss_target[0]
    s = xs.shape[0]
    c_idx = lax.axis_index("c").astype(jnp.int32).reshape(1)
    k_idx = (2 * lax.axis_index("x") + lax.axis_index("y")).astype(jnp.int32).reshape(1)

    conv_bits = lax.bitcast_convert_type(conv_w, BF16)
    small_list = [mla_w_qb, mla_w_kvb, conv_bits]
    shard_cols = w_in.shape[2]
    w_in_b = w_in.astype(BF16)
    late = [w_in_b[1], w_out.astype(BF16).reshape(-1, D_MODEL)]
    own = [w_in_b[0], _pack_rows(small_list, BF16, SMALL_ROWS)]
    gathered_in0, gathered_small = [_fill_own_slab(buf, src, k_idx)
                                    for buf, src in zip(_all_gather(own), own)]
    started = _gather_start(late, gathered_in0)
    parts = _unpack_rows(gathered_small, [a.shape for a in small_list])
    join = lambda p, axis: jnp.concatenate([p[k] for k in range(N_CHIPS)], axis=axis)
    w_in_zeros = jnp.zeros((D_MODEL, 64), BF16)
    permuted = lambda slabs: _cols([slabs[k] for k in range(N_CHIPS)] + [w_in_zeros], NEW_ORDER)
    w_qb_full = join(parts[0], 2)
    w_kvb_full = join(parts[1], 2)
    conv_full = lax.bitcast_convert_type(join(parts[2], 2), F32)

    rope = _rope_tables(s)
    swa_tables = _swa_tables()
    layers = []
    for l in range(DEPTH):
        wq = jnp.pad(w_qb_full[l].reshape(MLA_Q_LORA, MLA_HEADS, MLA_QK),
                     ((0, 0), (0, 0), (0, LANES - MLA_QK))).reshape(MLA_Q_LORA, MLA_HEADS * LANES)
        kv = w_kvb_full[l].reshape(MLA_KV_LORA, MLA_HEADS, MLA_NOPE + MLA_V)
        wk = jnp.pad(kv[:, :, :MLA_NOPE], ((0, 0), (0, 0), (0, LANES - MLA_NOPE)))
        wkv = jnp.concatenate([wk.reshape(MLA_KV_LORA, MLA_HEADS * LANES),
                               kv[:, :, MLA_NOPE:].reshape(MLA_KV_LORA, MLA_HEADS * MLA_V)], axis=1)
        layers.append(dict(
            wq=wq, wkv=wkv,
            cw=jnp.pad(conv_full[l], ((0, HALO - 3), (0, 0))),
            g=norm_g[l][None],
            mla_norms=(mla_q_a_norm[l][:, None], mla_kv_a_norm[l][:, None],
                       _pad_lanes(mla_q_norm[l][None], LANES).T, _pad_lanes(mla_k_norm[l][None], LANES).T),
            mla_weights=(wq, wkv, wq.T, wkv.T),
            sqn=swa_q_norm[l][:, None], skn=swa_k_norm[l][:, None], sinks=swa_sinks[l][None]))

    saved = []
    h_in = xs
    layers[0]["w_in"] = permuted(gathered_in0)
    layers[0]["g"] = layers[0]["g"] + started[-1][0:1, 0:1]
    for l in range(DEPTH):
        p = layers[l]
        proj, hb = _in_proj_fwd(h_in, p["g"], p["w_in"])
        q, k, v, qt, kt, vt = _mla_prep_fwd(proj, p["mla_norms"], p["mla_weights"], rope)
        o_mla, lse = _mla_attn_fwd(q, k, vt)
        if l == 0:
            late_in1, late_out = [_fill_own_slab(buf, src, k_idx) for buf, src in
                                  zip(_gather_wait(started, len(late), o_mla), late)]
            layers[1]["w_in"] = permuted(late_in1)
            w_out_full = join(late_out.reshape(N_CHIPS, DEPTH, -1, D_MODEL), 1)
            for n in range(DEPTH):
                layers[n]["w_out"] = w_out_full[n]
        o_swa = _swa_fwd(proj, p["sqn"], p["skn"], p["sinks"], swa_tables)
        last = l == DEPTH - 1
        y, z, *loss_acc = _out_fwd(proj, o_mla, o_swa, h_in, p["w_out"], p["cw"],
                                   target if last else None)
        saved.append(dict(x=h_in, proj=proj, hb=hb, q=q, k=k, v=v, qt=qt, kt=kt, o_mla=o_mla, lse=lse,
                          o_swa=o_swa, z=z))
        h_in = y

    dy, loss_acc = h_in, loss_acc[0]
    loss = lax.psum(loss_acc[0, 0], ("x", "y", "c"))

    grads = {n: [None] * DEPTH for n in WEIGHT_ORDER}

    def in_chunks(l):
        return jnp.stack([_cols(grads["w_in"][l], _sub_ranges(OLD_ORDER, k * shard_cols,
                                                              (k + 1) * shard_cols))
                          for k in range(N_CHIPS)])

    def out_chunks(l):
        return grads["w_out"][l].reshape(N_CHIPS, -1, D_MODEL)

    for l in reversed(range(DEPTH)):
        p, a = layers[l], saved[l]
        dconv, dgates, do_mla, do_mla_t, delta, do_swa, dcw = _out_bwd(dy, a["proj"], a["o_mla"], a["o_swa"],
                                                             p["w_out"], p["cw"])
        grads["w_out"][l] = _weight_grads(a["z"], [dy], "dw_out")[0]
        grads["conv_w"][l] = dcw[0:3]
        delta_rows = jnp.transpose(delta, (1, 0)).reshape(MLA_HEADS // 2, 2, s)
        dq, dk, dv = _mla_attn_bwd(a["q"], a["qt"], a["k"], a["kt"], a["v"], do_mla, do_mla_t,
                                   a["lse"], delta_rows)
        mla_norms = p["mla_norms"]
        if l == 0:
            late_gs, from_sib = _split_wait("late_grads_to_sibling_wait", late_st1, 2,
                                            _plan_to_sibling, dq)
            late_p = [_add_sibling(g, r, c_idx, BF16) for g, r in zip(late_gs, from_sib)]
            late_st2 = _split_start("late_grads_scatter_start", late_p,
                                    [(v.shape, v.dtype) for v in late_p], 6, _plan_scatter)
            mla_norms = (mla_norms[0] + late_st2[-1][0:1, 0:1],) + tuple(mla_norms[1:])
        dmla, dqan, dkvan, dqn, dkn, dwq_t, dwkv_t = _mla_prep_bwd(
            a["proj"], mla_norms, p["mla_weights"], rope, dq, dk, dv)
        dwq, dwkv = dwq_t.T, dwkv_t.T
        dsq, dskv, dsqn, dskn, dsinks = _swa_bwd(a["proj"], p["sqn"], p["skn"], p["sinks"], swa_tables, do_swa)
        if l == 0:
            late_p, late_q = _split_wait("late_grads_scatter_wait", late_st2, 2, _plan_scatter, dsq)
            late_full = [_sum_chips(_fill_own_slab(q_, p_, k_idx), c_idx)
                         for q_, p_ in zip(late_q, late_p)]
        pieces = [dconv, dgates, dsq, dmla, dskv]
        grads["w_in"][l] = _weight_grads(a["hb"], pieces, "dw_in")
        gain = p["g"]
        if l == 0:
            gs0 = (in_chunks(0), out_chunks(0))
            last_p = [_add_sibling(g, r, c_idx, BF16)
                      for g, r in zip(gs0, _swap_halves_to_sibling(gs0))]
            last_st = _split_start("last_grads_scatter_start", last_p,
                                   [(v.shape, v.dtype) for v in last_p], 6, _plan_scatter)
            gain = gain + last_st[-1][0:1, 0:1]
        dx, dg = _in_proj_bwd(pieces, a["x"], gain, p["w_in"], dy)
        if l == 0:
            last_p, last_q = _split_wait("last_grads_scatter_wait", last_st, 2, _plan_scatter, dx)
            last_full = [_sum_chips(_fill_own_slab(q_, p_, k_idx), c_idx)
                         for q_, p_ in zip(last_q, last_p)]
        grads["norm_g"][l] = dg[0]
        grads["mla_q_a_norm"][l] = dqan[:, 0]
        grads["mla_kv_a_norm"][l] = dkvan[:, 0]
        grads["mla_q_norm"][l] = dqn[:MLA_QK, 0]
        grads["mla_k_norm"][l] = dkn[:MLA_QK, 0]
        grads["mla_w_qb"][l] = dwq.reshape(MLA_Q_LORA, MLA_HEADS, LANES)[:, :, :MLA_QK].reshape(
            MLA_Q_LORA, MLA_HEADS * MLA_QK)
        dwk = dwkv[:, :MLA_HEADS * LANES].reshape(MLA_KV_LORA, MLA_HEADS, LANES)[:, :, :MLA_NOPE]
        dwv = dwkv[:, MLA_HEADS * LANES:].reshape(MLA_KV_LORA, MLA_HEADS, MLA_V)
        grads["mla_w_kvb"][l] = jnp.concatenate([dwk, dwv], axis=2).reshape(
            MLA_KV_LORA, MLA_HEADS * (MLA_NOPE + MLA_V))
        grads["swa_q_norm"][l] = dsqn[:, 0]
        grads["swa_k_norm"][l] = dskn[:, 0]
        grads["swa_sinks"][l] = dsinks[0]
        dy = dx
        if l == DEPTH - 1:
            late_st1 = _split_start(
                "late_grads_to_sibling_start", [in_chunks(l), out_chunks(l)],
                [((N_CHIPS, D_MODEL // 2, shard_cols), F32),
                 ((N_CHIPS, D_MIX // N_CHIPS // 2, D_MODEL), F32)], 2, _plan_to_sibling)
    grad_x = dy[None]
    full_grads = {n: jnp.stack(grads[n]) for n in WEIGHT_ORDER if n not in ("w_in", "w_out")}

    rest = tuple(n for n in SHARDED if n not in ("w_in", "w_out"))
    rep_shapes = [weights[n].shape for n in REPLICATED]
    rep_grads = jnp.concatenate([full_grads[n].reshape(-1) for n in REPLICATED])

    def chunk(g, n, k):
        width = g.shape[SHARD_AXIS[n]] // N_CHIPS
        return lax.slice_in_dim(g, k * width, (k + 1) * width, axis=SHARD_AXIS[n])

    g_small = jnp.stack([_pack_rows([chunk(full_grads[n], n, k) for n in rest] + [rep_grads],
                                    F32, SMALL_ROWS) for k in range(N_CHIPS)])
    gs = (g_small,)
    partial = [_add_sibling(g, r, c_idx, F32) for g, r in zip(gs, _swap_halves_to_sibling(gs))]
    by_chip = [_fill_own_slab(q, p, k_idx) for q, p in zip(_scatter_to_chips(partial), partial)]
    g_small_mine, g_in0, g_out0, g_in1, g_out1 = _join_halves(
        [_sum_chips(q, c_idx) for q in by_chip] + last_full + late_full)

    vals = _unpack_rows(g_small_mine, [weights[n].shape for n in rest] + [(rep_grads.shape[0],)])
    grad = dict(zip(rest, vals[:-1]))
    grad["w_in"] = jnp.stack([g_in0, g_in1])
    grad["w_out"] = jnp.stack([g_out0, g_out1])
    off = 0
    for n, sh in zip(REPLICATED, rep_shapes):
        grad[n] = vals[-1][off:off + sh[0] * sh[1]].reshape(sh)
        off += sh[0] * sh[1]
    results = {}
    for n in ("w_in", "w_out"):
        view = lambda a, n=n: a.reshape(-1, weights[n].shape[-1])
        res = _adamw(view(grad[n]), view(weights[n]), view(mom_m[n]), view(mom_v[n]))
        results[n] = [r.reshape(weights[n].shape) for r in res]
    small = tuple(n for n in WEIGHT_ORDER if n not in results)
    res = _adamw_small(*([d[n] for n in small] for d in (grad, weights, mom_m, mom_v)))
    for a, n in enumerate(small):
        results[n] = [res[kind][a] for kind in range(3)]
    unpacked = [grad] + [{n: results[n][kind] for n in WEIGHT_ORDER} for kind in range(3)]
    outs = [loss, grad_x]
    for group in unpacked:
        outs += [group[n] for n in WEIGHT_ORDER]
    return tuple(outs)
```

```python
import jax
import numpy as np
import jax.numpy as jnp
from jax import lax
from jax.experimental import pallas as pl
from jax.experimental.pallas import tpu as pltpu

F32 = jnp.float32
BF16 = jnp.bfloat16

D_MODEL = 1024
DEPTH = 2
GROUP = 512
D_MIX = 3 * GROUP
BLOCK = 128
RMS_EPS = 1e-6
NEG_INF = -1e30
MLA_HEADS = 8
MLA_QK = 96
MLA_NOPE = 64
MLA_ROPE = 32
MLA_V = 64
V_AUG = 80
MLA_Q_LORA = 256
MLA_KV_LORA = 128
ROPE_THETA = 10000.0
SWA_HEADS = 8
SWA_KV = 2
SWA_GROUP = 4
SWA_DIM = 64
N_CHIPS = 4

NC = 4352
OFF_SQ, OFF_MLA, OFF_SKV = 3072, 3584, 4096

VMEM_LIMIT = 56 * 1024 * 1024
LANES = 128
PACK_COLS = 1024
PACK_ROW_ALIGN = 16
SMALL_ROWS = 256

ADAM_LR = 0.001
ADAM_B1 = 0.9
ADAM_B2 = 0.999
ADAM_EPS = 1e-08
ADAM_WD = 0.01
ADAM_STEP = 10

MESH = pl.DeviceIdType.MESH


def _params(sem, vmem=VMEM_LIMIT):
    return pltpu.CompilerParams(dimension_semantics=sem, vmem_limit_bytes=vmem)


def _dot(a, b, dims):
    return lax.dot_general(a.astype(BF16), b.astype(BF16), (dims, ((), ())),
                           preferred_element_type=F32)


def _mm(a, b):
    return _dot(a, b, ((1,), (0,)))


def _mm_nt(a, b):
    return _dot(a, b, ((1,), (1,)))


def _rms(x, g, n=None):
    n = x.shape[-1] if n is None else n
    ms = jnp.sum(x * x, axis=-1, keepdims=True) * (1.0 / n)
    return x * lax.rsqrt(ms + RMS_EPS) * g


def _sigmoid(x):
    return 1.0 / (1.0 + jnp.exp(-x))


def _in_proj_fwd(x, g, w):
    s = x.shape[0]
    tm = min(512, s)

    def body(x_ref, g_ref, w_ref, proj_ref, hbt_ref):
        h = _rms(x_ref[...], g_ref[...])
        hbt_ref[...] = jnp.transpose(h).astype(BF16)
        proj_ref[...] = jnp.dot(h.astype(BF16), w_ref[...], preferred_element_type=F32)

    return pl.pallas_call(
        body, name="in_proj_fwd", grid=(s // tm,),
        in_specs=[pl.BlockSpec((tm, D_MODEL), lambda i: (i, 0)),
                  pl.BlockSpec((1, D_MODEL), lambda i: (0, 0)),
                  pl.BlockSpec((D_MODEL, NC), lambda i: (0, 0))],
        out_specs=[pl.BlockSpec((tm, NC), lambda i: (i, 0)),
                   pl.BlockSpec((D_MODEL, tm), lambda i: (0, i))],
        out_shape=[jax.ShapeDtypeStruct((s, NC), F32), jax.ShapeDtypeStruct((D_MODEL, s), BF16)],
        compiler_params=_params(("parallel",)),
    )(x, g, w)


def _in_proj_bwd(pieces, x, g, w, dres):
    s = x.shape[0]
    tm = min(512, s)
    n_p = len(pieces)

    def body(*refs):
        p_refs = refs[:n_p]
        x_ref, g_ref, w_ref, dres_ref, dx_ref, dg_ref = refs[n_p:]
        dh = None
        off = 0
        for r in p_refs:
            width = r.shape[1]
            t = _mm_nt(r[...], w_ref[:, off:off + width])
            dh = t if dh is None else dh + t
            off += width
        _, vjp = jax.vjp(_rms, x_ref[...], g_ref[...])
        dx, dg = vjp(dh)
        dx_ref[...] = dx + dres_ref[...]

        @pl.when(pl.program_id(0) == 0)
        def _():
            dg_ref[...] = jnp.zeros_like(dg_ref)

        dg_ref[...] += dg

    in_specs = [pl.BlockSpec((tm, p.shape[1]), lambda i: (i, 0)) for p in pieces]
    in_specs += [pl.BlockSpec((tm, D_MODEL), lambda i: (i, 0)),
                 pl.BlockSpec((1, D_MODEL), lambda i: (0, 0)),
                 pl.BlockSpec((D_MODEL, NC), lambda i: (0, 0)),
                 pl.BlockSpec((tm, D_MODEL), lambda i: (i, 0))]
    return pl.pallas_call(
        body, name="in_proj_bwd", grid=(s // tm,),
        in_specs=in_specs,
        out_specs=[pl.BlockSpec((tm, D_MODEL), lambda i: (i, 0)),
                   pl.BlockSpec((1, D_MODEL), lambda i: (0, 0))],
        out_shape=[jax.ShapeDtypeStruct((s, D_MODEL), F32), jax.ShapeDtypeStruct((1, D_MODEL), F32)],
        compiler_params=_params(("arbitrary",)),
    )(*pieces, x, g, w, dres)


def _weight_grads(at, bs, name):
    m, s = at.shape
    nb = len(bs)
    tk = min(512, s)

    def body(a_ref, *refs):
        b_refs, o_refs = refs[:nb], refs[nb:]

        @pl.when(pl.program_id(0) == 0)
        def _():
            for o_ref in o_refs:
                o_ref[...] = jnp.zeros_like(o_ref)

        a = a_ref[...]
        for b_ref, o_ref in zip(b_refs, o_refs):
            o_ref[...] += _mm(a, b_ref[...])

    return pl.pallas_call(
        body, name=name, grid=(s // tk,),
        in_specs=[pl.BlockSpec((m, tk), lambda k: (0, k))]
        + [pl.BlockSpec((tk, b.shape[1]), lambda k: (k, 0)) for b in bs],
        out_specs=[pl.BlockSpec((m, b.shape[1]), lambda k: (0, 0)) for b in bs],
        out_shape=[jax.ShapeDtypeStruct((m, b.shape[1]), F32) for b in bs],
        compiler_params=_params(("arbitrary",)),
    )(at, *bs)


def _rms0(x, g, n=None):
    n = x.shape[0] if n is None else n
    ms = jnp.sum(x * x, axis=0, keepdims=True) * (1.0 / n)
    return x * lax.rsqrt(ms + RMS_EPS) * g


@jax.custom_vjp
def _rope0(t, c, s1, s2):
    return t * c + pltpu.roll(t, LANES - 16, 0) * s1 + pltpu.roll(t, 16, 0) * s2


def _rope0_fwd(t, c, s1, s2):
    return _rope0(t, c, s1, s2), (c, s1, s2)


def _rope0_bwd(res, g):
    c, s1, s2 = res
    dt = g * c + pltpu.roll(g * s1, 16, 0) + pltpu.roll(g * s2, LANES - 16, 0)
    return dt, jnp.zeros_like(c), jnp.zeros_like(s1), jnp.zeros_like(s2)


_rope0.defvjp(_rope0_fwd, _rope0_bwd)


@jax.custom_vjp
def _mmw(w, wt, x):
    return _mm(w, x)


def _mmw_fwd(w, wt, x):
    return _mm(w, x), (wt, x)


def _mmw_bwd(res, g):
    wt, x = res
    return _mm_nt(g, x), jnp.zeros_like(wt), _mm(wt, g)


_mmw.defvjp(_mmw_fwd, _mmw_bwd)


def _prep_fn(q_lat, kv_lat, kr, qan, kvan, qn, kn, wq, wk, wv, wqt, wkt, wvt, c, s1, s2, mm):
    tokens = q_lat.shape[1]
    rq = _rms0(q_lat, qan)
    rkv = _rms0(kv_lat, kvan)
    qn_b = jnp.broadcast_to(qn, (LANES, tokens))
    kn_b = jnp.broadcast_to(kn, (LANES, tokens))
    qs, ks = [], []
    for h in range(MLA_HEADS):
        qs.append(_rope0(_rms0(mm(wq[h], wqt[h], rq), qn_b, MLA_QK), c, s1, s2))
        ks.append(_rope0(_rms0(mm(wk[h], wkt[h], rkv) + kr, kn_b, MLA_QK), c, s1, s2))
    return tuple(qs), tuple(ks), mm(wv, wvt, rkv)


def _prep_weights(wq_ref, wkv_ref, wqt_ref, wkvt_ref):
    heads = range(MLA_HEADS)
    wq = tuple(wqt_ref[LANES * h:LANES * (h + 1), :].astype(F32) for h in heads)
    wk = tuple(wkvt_ref[LANES * h:LANES * (h + 1), :].astype(F32) for h in heads)
    wv = wkvt_ref[LANES * MLA_HEADS:, :].astype(F32)
    wqt = tuple(wq_ref[:, LANES * h:LANES * (h + 1)].astype(F32) for h in heads)
    wkt = tuple(wkv_ref[:, LANES * h:LANES * (h + 1)].astype(F32) for h in heads)
    wvt = wkv_ref[:, LANES * MLA_HEADS:].astype(F32)
    return wq, wk, wv, wqt, wkt, wvt


def _prep_in_specs(tm):
    const = lambda shape: pl.BlockSpec(shape, lambda i: (0, 0))
    col = lambda height: pl.BlockSpec((height, tm), lambda i: (0, i))
    return [pl.BlockSpec((tm, 512), lambda i: (i, OFF_MLA // 512)),
            const((MLA_Q_LORA, 1)), const((MLA_KV_LORA, 1)), const((LANES, 1)), const((LANES, 1)),
            const((MLA_Q_LORA, 1024)), const((MLA_KV_LORA, 1536)),
            const((1024, MLA_Q_LORA)), const((1536, MLA_KV_LORA)),
            col(LANES), col(LANES), col(LANES)]


def _prep_operands(blk_ref, refs):
    qan_ref, kvan_ref, qn_ref, kn_ref, wq_ref, wkv_ref, wqt_ref, wkvt_ref, c_ref, s1_ref, s2_ref = refs
    blk_t = jnp.transpose(blk_ref[...])
    diff = (blk_t[0:256], blk_t[256:384], blk_t[384:512],
            qan_ref[...], kvan_ref[...], qn_ref[...], kn_ref[...])
    weights = _prep_weights(wq_ref, wkv_ref, wqt_ref, wkvt_ref)
    return diff, weights, (c_ref[...], s1_ref[...], s2_ref[...])


def _mla_prep_fwd(proj, norms, weights, rope):
    s = proj.shape[0]
    tm = min(512, s)

    def body(blk_ref, *refs):
        ins, (q_ref, k_ref, v_ref, qt_ref, kt_ref, vt_ref) = refs[:11], refs[11:]
        diff, (wq, wk, wv, wqt, wkt, wvt), tables = _prep_operands(blk_ref, ins)
        qs, ks, v = _prep_fn(*diff, wq, wk, wv, wqt, wkt, wvt, *tables,
                             lambda w, wt, x: _mm(w, x))
        for h in range(MLA_HEADS):
            q2 = qs[h] * Q_PRESCALE
            qt_ref[LANES * h:LANES * (h + 1), :] = q2.astype(BF16)
            kt_ref[LANES * h:LANES * (h + 1), :] = ks[h].astype(BF16)
            q_ref[:, LANES * h:LANES * (h + 1)] = jnp.transpose(q2).astype(BF16)
            k_ref[:, LANES * h:LANES * (h + 1)] = jnp.transpose(ks[h]).astype(BF16)
        ones_row = (lax.broadcasted_iota(jnp.int32, (V_AUG - MLA_V, v.shape[1]), 0) == 0).astype(BF16)
        for h in range(MLA_HEADS):
            vt_ref[V_AUG * h:V_AUG * h + MLA_V, :] = v[MLA_V * h:MLA_V * (h + 1)].astype(BF16)
            vt_ref[V_AUG * h + MLA_V:V_AUG * (h + 1), :] = ones_row
        v_ref[...] = jnp.transpose(v).astype(BF16)

    row = lambda width: pl.BlockSpec((tm, width), lambda i: (i, 0))
    col = lambda height: pl.BlockSpec((height, tm), lambda i: (0, i))
    return pl.pallas_call(
        body, name="mla_prep_fwd", grid=(s // tm,),
        in_specs=_prep_in_specs(tm),
        out_specs=[row(1024), row(1024), row(512), col(1024), col(1024), col(MLA_HEADS * V_AUG)],
        out_shape=[jax.ShapeDtypeStruct((s, 1024), BF16), jax.ShapeDtypeStruct((s, 1024), BF16),
                   jax.ShapeDtypeStruct((s, 512), BF16), jax.ShapeDtypeStruct((1024, s), BF16),
                   jax.ShapeDtypeStruct((1024, s), BF16),
                   jax.ShapeDtypeStruct((MLA_HEADS * V_AUG, s), BF16)],
        compiler_params=_params(("parallel",)),
    )(proj, *norms, *weights, *rope)


def _mla_prep_bwd(proj, norms, weights, rope, dq, dk, dv):
    s = proj.shape[0]
    tm = min(512, s)

    def body(blk_ref, *refs):
        ins, (dq_ref, dk_ref, dv_ref) = refs[:11], refs[11:14]
        dblk_ref, dqan_ref, dkvan_ref, dqn_ref, dkn_ref, dwq_ref, dwkv_ref = refs[14:]
        diff, (wq, wk, wv, wqt, wkt, wvt), tables = _prep_operands(blk_ref, ins)

        def fn(q_lat, kv_lat, kr, qan, kvan, qn, kn, wq_, wk_, wv_):
            return _prep_fn(q_lat, kv_lat, kr, qan, kvan, qn, kn, wq_, wk_, wv_, wqt, wkt, wvt,
                            *tables, _mmw)

        _, vjp = jax.vjp(fn, *diff, wq, wk, wv)
        heads = range(MLA_HEADS)
        cts = (tuple(dq_ref[LANES * h:LANES * (h + 1), :] for h in heads),
               tuple(dk_ref[LANES * h:LANES * (h + 1), :] for h in heads), dv_ref[...])
        dq_lat, dkv_lat, dkr, dqan, dkvan, dqn, dkn, dwq_h, dwk_h, dwv = vjp(cts)
        dblk_ref[...] = jnp.transpose(
            jnp.concatenate([dq_lat, dkv_lat, dkr], axis=0)).astype(BF16)

        @pl.when(pl.program_id(0) == 0)
        def _():
            for r in (dqan_ref, dkvan_ref, dqn_ref, dkn_ref, dwq_ref, dwkv_ref):
                r[...] = jnp.zeros_like(r)

        dqan_ref[...] += dqan
        dkvan_ref[...] += dkvan
        dqn_ref[...] += dqn
        dkn_ref[...] += dkn
        for h in heads:
            dwq_ref[LANES * h:LANES * (h + 1), :] += dwq_h[h]
            dwkv_ref[LANES * h:LANES * (h + 1), :] += dwk_h[h]
        dwkv_ref[LANES * MLA_HEADS:, :] += dwv

    const = lambda shape: pl.BlockSpec(shape, lambda i: (0, 0))
    col = lambda height: pl.BlockSpec((height, tm), lambda i: (0, i))
    shapes = [(MLA_Q_LORA, 1), (MLA_KV_LORA, 1), (LANES, 1), (LANES, 1),
              (1024, MLA_Q_LORA), (1536, MLA_KV_LORA)]
    return pl.pallas_call(
        body, name="mla_prep_bwd", grid=(s // tm,),
        in_specs=_prep_in_specs(tm) + [col(1024), col(1024), col(512)],
        out_specs=[pl.BlockSpec((tm, 512), lambda i: (i, 0))] + [const(sh) for sh in shapes],
        out_shape=[jax.ShapeDtypeStruct((s, 512), BF16)]
        + [jax.ShapeDtypeStruct(sh, F32) for sh in shapes],
        compiler_params=_params(("arbitrary",)),
    )(proj, *norms, *weights, *rope, dq, dk, dv)


MLA_SCALE = MLA_QK ** -0.5


LOG2E = 1.4426950408889634
LN2 = 0.6931471805599453
Q_PRESCALE = MLA_SCALE * LOG2E
HEAD_GROUPS = ((0, 1),)


def _mla_attn_fwd(q2, k, vt):
    s = q2.shape[0]
    t = min(512, s)
    tk = min(128, s)
    nq = s // t
    r = t // tk

    def body(q_ref, k_ref, vt_ref, o_ref, lse_ref, acc_ref):
        i = pl.program_id(1)
        row = lax.broadcasted_iota(jnp.int32, (tk, t), 0)
        col = lax.broadcasted_iota(jnp.int32, (tk, t), 1)
        qh = [q_ref[:, LANES * hh:LANES * (hh + 1)] for hh in range(2)]
        acc_ref[...] = jnp.zeros_like(acc_ref)

        def scores(j, heads, diag=None):
            r0 = pl.multiple_of(j * tk, tk)
            q_lo = 0 if diag is None else diag * tk
            out = []
            for hh in heads:
                kc = k_ref[pl.ds(r0, tk), LANES * hh:LANES * (hh + 1)]
                sc = lax.dot_general(kc, qh[hh][q_lo:], (((1,), (1,)), ((), ())),
                                     preferred_element_type=F32)
                out.append(sc if diag is None
                           else jnp.where(row[:, :t - q_lo] <= col[:, :t - q_lo], sc, NEG_INF))
            return tuple(out)

        for heads in HEAD_GROUPS:
            stats = tuple(jnp.full((1, t), NEG_INF, F32) for _ in heads)

            def consume(j, scs, stats, q_lo=0, heads=heads):
                r0 = pl.multiple_of(j * tk, tk)
                out, ps, alphas = [], [], []
                for n, hh in enumerate(heads):
                    m_old = stats[n][:, q_lo:]
                    m_new = jnp.maximum(m_old, jnp.max(scs[n], axis=0, keepdims=True))
                    ps.append(jnp.exp2(scs[n] - m_new).astype(BF16))
                    alphas.append(jnp.exp2(m_old - m_new))
                    out.append(m_new if q_lo == 0
                               else jnp.concatenate([stats[n][:, :q_lo], m_new], axis=1))
                for n, hh in enumerate(heads):
                    vc = vt_ref[V_AUG * hh:V_AUG * (hh + 1), pl.ds(r0, tk)]
                    acc_ref[hh, :, q_lo:] = alphas[n] * acc_ref[hh, :, q_lo:] + jnp.dot(
                        vc, ps[n], preferred_element_type=F32)
                return tuple(out)

            def group(j0, stats, diag, heads=heads):
                scs = [scores(j0 + d, heads, d if diag else None) for d in range(r)]
                for d in range(r):
                    stats = consume(j0 + d, scs[d], stats, d * tk if diag else 0)
                return stats

            stats = group(r * i, stats, True)
            odd = i % 2
            stats = lax.fori_loop(0, odd, lambda j, st: group(r * j, st, False), stats)
            def several(n, j0, st):
                for u in range(n):
                    st = group(r * (j0 + u), st, False)
                return st

            two = (i // 2) % 2
            stats = lax.fori_loop(0, two, lambda p, st: several(2, odd, st), stats)
            first = odd + 2 * two
            stats = lax.fori_loop(0, i // 4, lambda p, st: several(4, first + 4 * p, st), stats)
            for n, hh in enumerate(heads):
                l = acc_ref[hh, MLA_V:MLA_V + 1, :]
                o_ref[:, MLA_V * hh:MLA_V * (hh + 1)] = jnp.transpose(acc_ref[hh, 0:MLA_V, :] / l)
                lse_ref[0, hh:hh + 1, :] = stats[n] + jnp.log2(l)

    return pl.pallas_call(
        body, name="mla_attn_fwd", grid=(MLA_HEADS // 2, nq),
        in_specs=[pl.BlockSpec((t, 256), lambda p, i: (i, p)),
                  pl.BlockSpec((s, 256), lambda p, i: (0, p)),
                  pl.BlockSpec((2 * V_AUG, s), lambda p, i: (p, 0))],
        out_specs=[pl.BlockSpec((t, 128), lambda p, i: (i, p)),
                   pl.BlockSpec((1, 2, t), lambda p, i: (p, 0, i))],
        out_shape=[jax.ShapeDtypeStruct((s, 512), F32),
                   jax.ShapeDtypeStruct((MLA_HEADS // 2, 2, s), F32)],
        scratch_shapes=[pltpu.VMEM((2, V_AUG, t), F32)],
        compiler_params=_params(("parallel", "arbitrary")),
    )(q2, k, vt)


def _mla_attn_bwd(q2, q2t, k, kt, v, do, dot, lse_rows, delta_rows):
    s = q2.shape[0]
    t = min(512, s)
    nq = s // t

    def body(q_ref, qt_ref, k_ref, kt_ref, v_ref, do_ref, dot_ref, lse_ref, dl_ref,
             dq_ref, dk_ref, dv_ref):
        j = pl.program_id(1)

        @pl.when(j == 0)
        def _():
            dq_ref[...] = jnp.zeros_like(dq_ref)

        dk_ref[...] = jnp.zeros_like(dk_ref)
        dv_ref[...] = jnp.zeros_like(dv_ref)
        row = lax.broadcasted_iota(jnp.int32, (t, t), 0)
        col = lax.broadcasted_iota(jnp.int32, (t, t), 1)
        causal_t = row <= col
        kh = [k_ref[:, LANES * hh:LANES * (hh + 1)] for hh in range(2)]
        kth = [kt_ref[LANES * hh:LANES * (hh + 1), :] for hh in range(2)]
        vh = [v_ref[:, MLA_V * hh:MLA_V * (hh + 1)] for hh in range(2)]
        nt = (((1,), (1,)), ((), ()))

        def step(i, masked):
            r0 = pl.multiple_of(i * t, t)
            sd = []
            for hh in range(2):
                qh = q_ref[pl.ds(r0, t), LANES * hh:LANES * (hh + 1)]
                doh = do_ref[pl.ds(r0, t), MLA_V * hh:MLA_V * (hh + 1)]
                sc_t = lax.dot_general(kh[hh], qh, nt, preferred_element_type=F32)
                sd.append(jnp.where(causal_t, sc_t, NEG_INF) if masked else sc_t)
                sd.append(lax.dot_general(vh[hh], doh, nt, preferred_element_type=F32))
            for hh in range(2):
                lse = lse_ref[0, hh:hh + 1, pl.ds(r0, t)]
                dl = dl_ref[0, hh:hh + 1, pl.ds(r0, t)]
                p_t = jnp.exp2(sd[2 * hh] - lse)
                g_t = (p_t * (sd[2 * hh + 1] - dl)).astype(BF16)
                qth = qt_ref[LANES * hh:LANES * (hh + 1), pl.ds(r0, t)]
                doth = dot_ref[MLA_V * hh:MLA_V * (hh + 1), pl.ds(r0, t)]
                dv_ref[MLA_V * hh:MLA_V * (hh + 1), :] += lax.dot_general(
                    doth, p_t.astype(BF16), nt, preferred_element_type=F32)
                dk_ref[LANES * hh:LANES * (hh + 1), :] += lax.dot_general(
                    qth, g_t, nt, preferred_element_type=F32)
                dq_ref[LANES * hh:LANES * (hh + 1), pl.ds(r0, t)] += jnp.dot(
                    kth[hh], g_t, preferred_element_type=F32)

        step(j, True)

        rest = nq - 1 - j

        @pl.when(rest % 2 == 1)
        def _():
            step(j + 1, False)

        pair = j + 1 + rest % 2

        @pl.when((rest // 2) % 2 == 1)
        def _():
            step(pair, False)
            step(pair + 1, False)

        first = pair + 2 * ((rest // 2) % 2)

        def trip(p, carry):
            for u in range(4):
                step(first + 4 * p + u, False)
            return carry

        lax.fori_loop(0, rest // 4, trip, 0)
        dk_ref[...] = dk_ref[...] * LN2

        @pl.when(j == nq - 1)
        def _():
            dq_ref[...] = dq_ref[...] * MLA_SCALE

    return pl.pallas_call(
        body, name="mla_attn_bwd", grid=(MLA_HEADS // 2, nq),
        in_specs=[pl.BlockSpec((s, 256), lambda p, j: (0, p)),
                  pl.BlockSpec((256, s), lambda p, j: (p, 0)),
                  pl.BlockSpec((t, 256), lambda p, j: (j, p)),
                  pl.BlockSpec((256, t), lambda p, j: (p, j)),
                  pl.BlockSpec((t, 128), lambda p, j: (j, p)),
                  pl.BlockSpec((s, 128), lambda p, j: (0, p)),
                  pl.BlockSpec((128, s), lambda p, j: (p, 0)),
                  pl.BlockSpec((1, 2, s), lambda p, j: (p, 0, 0)),
                  pl.BlockSpec((1, 2, s), lambda p, j: (p, 0, 0))],
        out_specs=[pl.BlockSpec((256, s), lambda p, j: (p, 0)),
                   pl.BlockSpec((256, t), lambda p, j: (p, j)),
                   pl.BlockSpec((128, t), lambda p, j: (p, j))],
        out_shape=[jax.ShapeDtypeStruct((1024, s), F32), jax.ShapeDtypeStruct((1024, s), F32),
                   jax.ShapeDtypeStruct((512, s), F32)],
        compiler_params=_params(("parallel", "arbitrary")),
    )(q2, q2t, k, kt, v, do, dot, lse_rows, delta_rows)


SWA_SCALE = SWA_DIM ** -0.5
SWA_COLS = SWA_GROUP * BLOCK
SWA_LOG2 = SWA_SCALE * LOG2E


def _swa_tables():
    k = np.arange(2 * BLOCK)[:, None]
    col = np.arange(SWA_COLS)[None, :]
    dist = BLOCK + (col % BLOCK) - k
    valid = (dist >= 0) & (dist < BLOCK)
    out = np.zeros((2, SWA_KV, 2 * BLOCK, SWA_COLS), np.float32)
    for first in range(2):
        ok = valid & ((k >= BLOCK) | (first == 0))
        for j in range(SWA_KV):
            slope = 2.0 ** -(SWA_GROUP * j + col // BLOCK + 1)
            out[first, j] = np.where(ok, -slope * dist * LOG2E, NEG_INF)
    return jnp.asarray(out)


def _swa_tile_inputs(sq_ref, skv_ref, halo_ref, qn_ref, kn_ref, sk_ref, add_ref, first):
    tokens = sq_ref.shape[0]
    kv_all = jnp.concatenate([halo_ref[...], skv_ref[...]], axis=0)
    kv_t = jnp.transpose(kv_all)
    sq_t = jnp.transpose(sq_ref[...])
    k_raw = [kv_t[SWA_DIM * j:SWA_DIM * (j + 1)] for j in range(SWA_KV)]
    v_t = [kv_t[128 + SWA_DIM * j:128 + SWA_DIM * (j + 1)] for j in range(SWA_KV)]
    v_nat = [kv_all[:, 128 + SWA_DIM * j:128 + SWA_DIM * (j + 1)] for j in range(SWA_KV)]
    q_raw = [sq_t[SWA_DIM * h:SWA_DIM * (h + 1)] for h in range(SWA_HEADS)]
    qn_b = jnp.broadcast_to(qn_ref[...], (SWA_DIM, tokens))
    kn_b = jnp.broadcast_to(kn_ref[...], (SWA_DIM, tokens + BLOCK))
    lane_grp = lax.broadcasted_iota(jnp.int32, (1, SWA_COLS), 1) // BLOCK
    sinks, adds = [], []
    for j in range(SWA_KV):
        row = jnp.zeros((1, SWA_COLS), F32)
        for g in range(SWA_GROUP):
            h = SWA_GROUP * j + g
            row = jnp.where(lane_grp == g, sk_ref[:, h:h + 1] * LOG2E, row)
        sinks.append(row)
        adds.append((jnp.where(first, add_ref[1, j], add_ref[0, j]), add_ref[0, j]))
    return k_raw, v_t, v_nat, q_raw, qn_b, kn_b, sinks, adds


def _swa_probs(kb, qs_t, add, sink):
    s2 = jnp.dot(kb, qs_t, preferred_element_type=F32) * SWA_LOG2 + add
    m = jnp.maximum(jnp.max(s2, axis=0, keepdims=True), sink)
    e = jnp.exp2(s2 - m)
    es = jnp.exp2(sink - m)
    inv = 1.0 / (jnp.sum(e, axis=0, keepdims=True) + es)
    return e, inv, es


def _swa_queries(qn_t, j, b):
    return jnp.concatenate([qn_t[SWA_GROUP * j + g][:, BLOCK * b:BLOCK * (b + 1)]
                            for g in range(SWA_GROUP)], axis=1)


def _swa_fwd(proj, qn, kn, sinks, tables):
    s = proj.shape[0]
    ts = min(1024, s)
    nb = ts // BLOCK

    def body(sq_ref, skv_ref, halo_ref, qn_ref, kn_ref, sk_ref, add_ref, o_ref, ot_ref):
        first = pl.program_id(0) == 0
        k_raw, v_t, _, q_raw, qn_b, kn_b, sink_rows, adds = _swa_tile_inputs(
            sq_ref, skv_ref, halo_ref, qn_ref, kn_ref, sk_ref, add_ref, first)
        kn_nat = [jnp.transpose(_rms0(k, kn_b)).astype(BF16) for k in k_raw]
        v_t = [v.astype(BF16) for v in v_t]
        qn_t = [_rms0(q, qn_b).astype(BF16) for q in q_raw]
        for b in range(nb):
            band = slice(BLOCK * b, BLOCK * (b + 2))
            for j in range(SWA_KV):
                e, inv, _ = _swa_probs(kn_nat[j][band], _swa_queries(qn_t, j, b),
                                       adds[j][0 if b == 0 else 1], sink_rows[j])
                o_t = jnp.dot(v_t[j][:, band], (e * inv).astype(BF16),
                              preferred_element_type=F32)
                for g in range(SWA_GROUP):
                    h = SWA_GROUP * j + g
                    ot_ref[SWA_DIM * h:SWA_DIM * (h + 1), BLOCK * b:BLOCK * (b + 1)] = (
                        o_t[:, BLOCK * g:BLOCK * (g + 1)])
        o_ref[...] = jnp.transpose(ot_ref[...])

    const = lambda shape: pl.BlockSpec(shape, lambda i: (0,) * len(shape))
    return pl.pallas_call(
        body, name="swa_fwd", grid=(s // ts,),
        in_specs=[pl.BlockSpec((ts, 512), lambda i: (i, OFF_SQ // 512)),
                  pl.BlockSpec((ts, 256), lambda i: (i, OFF_SKV // 256)),
                  pl.BlockSpec((BLOCK, 256), lambda i: (jnp.maximum(i * nb - 1, 0), OFF_SKV // 256)),
                  const((SWA_DIM, 1)), const((SWA_DIM, 1)), const((1, SWA_HEADS)),
                  const(tables.shape)],
        out_specs=pl.BlockSpec((ts, 512), lambda i: (i, 0)),
        out_shape=jax.ShapeDtypeStruct((s, 512), F32),
        scratch_shapes=[pltpu.VMEM((512, ts), F32)],
        compiler_params=_params(("parallel",)),
    )(proj, proj, proj, qn, kn, sinks, tables)


def _swa_bwd(proj, qn, kn, sinks, tables, do):
    s = proj.shape[0]
    ts = min(1024, s)
    nb = ts // BLOCK
    nt = s // ts

    def body(sq_ref, skv_ref, halo_ref, qn_ref, kn_ref, sk_ref, add_ref, do_ref,
             dsq_ref, dskv_ref, dqn_ref, dkn_ref, dsk_ref, carry_ref, dqt_ref, dkvt_ref):
        step = pl.program_id(0)
        first = step == nt - 1

        @pl.when(step == 0)
        def _():
            carry_ref[...] = jnp.zeros_like(carry_ref)
            dqn_ref[...] = jnp.zeros_like(dqn_ref)
            dkn_ref[...] = jnp.zeros_like(dkn_ref)
            dsk_ref[...] = jnp.zeros_like(dsk_ref)

        k_raw, v_t, v_nat, q_raw, qn_b, kn_b, sink_rows, adds = _swa_tile_inputs(
            sq_ref, skv_ref, halo_ref, qn_ref, kn_ref, sk_ref, add_ref, first)
        kn_f = [_rms0(k, kn_b) for k in k_raw]
        kn_t = [k.astype(BF16) for k in kn_f]
        kn_nat = [jnp.transpose(k).astype(BF16) for k in kn_f]
        v_nat = [v.astype(BF16) for v in v_nat]
        qn_t = [_rms0(q, qn_b).astype(BF16) for q in q_raw]
        do_t = jnp.transpose(do_ref[...].astype(F32)).astype(BF16)

        dkvt_ref[...] = jnp.zeros_like(dkvt_ref)
        dsink = [jnp.zeros((1, SWA_COLS), F32) for _ in range(SWA_KV)]
        nt_dims = (((1,), (1,)), ((), ()))
        for b in range(nb):
            rows = slice(BLOCK * b, BLOCK * (b + 1))
            band = slice(BLOCK * b, BLOCK * (b + 2))
            for j in range(SWA_KV):
                heads = [SWA_GROUP * j + g for g in range(SWA_GROUP)]
                qs_t = _swa_queries(qn_t, j, b)
                dos_t = jnp.concatenate([do_t[SWA_DIM * h:SWA_DIM * (h + 1), rows] for h in heads],
                                        axis=1)
                e, inv, es = _swa_probs(kn_nat[j][band], qs_t, adds[j][0 if b == 0 else 1],
                                        sink_rows[j])
                p = e * inv
                dp = jnp.dot(v_nat[j][band], dos_t, preferred_element_type=F32)
                dsum = jnp.sum(p * dp, axis=0, keepdims=True)
                dsink[j] = dsink[j] - es * inv * dsum
                g_t = (p * (dp - dsum) * SWA_SCALE).astype(BF16)
                dv_t = lax.dot_general(dos_t, p.astype(BF16), nt_dims,
                                       preferred_element_type=F32)
                dk_t = lax.dot_general(qs_t, g_t, nt_dims, preferred_element_type=F32)
                dq_t = jnp.dot(kn_t[j][:, band], g_t, preferred_element_type=F32)
                dkvt_ref[SWA_DIM * j:SWA_DIM * (j + 1), band] += dk_t
                dkvt_ref[128 + SWA_DIM * j:128 + SWA_DIM * (j + 1), band] += dv_t
                for g, h in enumerate(heads):
                    dqt_ref[SWA_DIM * h:SWA_DIM * (h + 1), rows] = dq_t[:, BLOCK * g:BLOCK * (g + 1)]

        dqn = jnp.zeros((SWA_DIM, 1), F32)
        for h in range(SWA_HEADS):
            _, vjp = jax.vjp(_rms0, q_raw[h], qn_ref[...])
            dq, dg = vjp(dqt_ref[SWA_DIM * h:SWA_DIM * (h + 1), :])
            dqt_ref[SWA_DIM * h:SWA_DIM * (h + 1), :] = dq
            dqn = dqn + dg
        dqn_ref[...] += dqn
        dsq_ref[...] = jnp.transpose(dqt_ref[...]).astype(BF16)
        dkn = jnp.zeros((SWA_DIM, 1), F32)
        lane_grp = lax.broadcasted_iota(jnp.int32, (1, SWA_COLS), 1) // BLOCK
        for j in range(SWA_KV):
            _, vjp = jax.vjp(_rms0, k_raw[j], kn_ref[...])
            dk, dg = vjp(dkvt_ref[SWA_DIM * j:SWA_DIM * (j + 1), :])
            dkvt_ref[SWA_DIM * j:SWA_DIM * (j + 1), :] = dk
            dkn = dkn + dg
            for g in range(SWA_GROUP):
                h = SWA_GROUP * j + g
                dsk_ref[:, h:h + 1] += jnp.sum(jnp.where(lane_grp == g, dsink[j], 0.0), axis=1,
                                               keepdims=True)
        dkn_ref[...] += dkn
        dkv = jnp.transpose(dkvt_ref[...])
        dskv_ref[0:ts - BLOCK, :] = dkv[BLOCK:ts].astype(BF16)
        dskv_ref[ts - BLOCK:ts, :] = (dkv[ts:ts + BLOCK] + carry_ref[...]).astype(BF16)
        carry_ref[...] = dkv[0:BLOCK]

    const = lambda shape: pl.BlockSpec(shape, lambda st: (0,) * len(shape))
    return pl.pallas_call(
        body, name="swa_bwd", grid=(nt,),
        in_specs=[pl.BlockSpec((ts, 512), lambda st: (nt - 1 - st, OFF_SQ // 512)),
                  pl.BlockSpec((ts, 256), lambda st: (nt - 1 - st, OFF_SKV // 256)),
                  pl.BlockSpec((BLOCK, 256),
                               lambda st: (jnp.maximum((nt - 1 - st) * nb - 1, 0), OFF_SKV // 256)),
                  const((SWA_DIM, 1)), const((SWA_DIM, 1)), const((1, SWA_HEADS)),
                  const(tables.shape),
                  pl.BlockSpec((ts, 512), lambda st: (nt - 1 - st, 0))],
        out_specs=[pl.BlockSpec((ts, 512), lambda st: (nt - 1 - st, 0)),
                   pl.BlockSpec((ts, 256), lambda st: (nt - 1 - st, 0)),
                   const((SWA_DIM, 1)), const((SWA_DIM, 1)), const((1, SWA_HEADS))],
        out_shape=[jax.ShapeDtypeStruct((s, 512), BF16), jax.ShapeDtypeStruct((s, 256), BF16),
                   jax.ShapeDtypeStruct((SWA_DIM, 1), F32), jax.ShapeDtypeStruct((SWA_DIM, 1), F32),
                   jax.ShapeDtypeStruct((1, SWA_HEADS), F32)],
        scratch_shapes=[pltpu.VMEM((BLOCK, 256), F32), pltpu.VMEM((512, ts), F32),
                        pltpu.VMEM((256, ts + BLOCK), F32)],
        compiler_params=_params(("arbitrary",)),
    )(proj, proj, proj, qn, kn, sinks, tables, do)


HALO = 8


def _shift_down(u, halo, k):
    tm = u.shape[0]
    rid = lax.broadcasted_iota(jnp.int32, u.shape, 0)
    out = pltpu.roll(u, k, 0)
    for r in range(k):
        out = jnp.where(rid == r, halo[HALO - k + r:HALO - k + r + 1, :], out)
    return out


def _shift_up(u, halo, k):
    tm = u.shape[0]
    rid = lax.broadcasted_iota(jnp.int32, u.shape, 0)
    out = pltpu.roll(u, tm - k, 0)
    for r in range(k):
        out = jnp.where(rid == tm - k + r, halo[r:r + 1, :], out)
    return out


def _conv_fwd_vals(conv_ref, convp_ref, cw_ref, is_first):
    c_h, c_b, c_c = conv_ref[:, 0:512], conv_ref[:, 512:1024], conv_ref[:, 1024:1536]
    u = c_c * c_h
    up = jnp.where(is_first, 0.0, convp_ref[:, 1024:1536] * convp_ref[:, 0:512])
    u1 = _shift_down(u, up, 1)
    u2 = _shift_down(u, up, 2)
    yc = cw_ref[0:1, :] * u2 + cw_ref[1:2, :] * u1 + cw_ref[2:3, :] * u
    return c_h, c_b, c_c, u, u1, u2, yc


def _out_fwd(proj, o_mla, o_swa, x, w_out, cw, target=None):
    s = proj.shape[0]
    tm = min(512, s)
    nt = s // tm
    with_loss = target is not None

    def body(*refs):
        conv_ref, convp_ref, gates_ref, om_ref, os_ref, x_ref, w_ref, cw_ref = refs[:8]
        if with_loss:
            t_ref, y_ref, zt_ref, loss_ref, z_ref = refs[8:]
        else:
            y_ref, zt_ref, z_ref = refs[8:]
        i = pl.program_id(0)
        _, c_b, _, _, _, _, yc = _conv_fwd_vals(conv_ref, convp_ref, cw_ref, i == 0)
        mix = (om_ref[...], c_b * yc, os_ref[...])
        for n in range(3):
            g = gates_ref[:, GROUP * n:GROUP * (n + 1)]
            z = mix[n] * (g * _sigmoid(g))
            z_ref[:, GROUP * n:GROUP * (n + 1)] = z.astype(BF16)
            zt_ref[GROUP * n:GROUP * (n + 1), :] = jnp.transpose(z).astype(BF16)
        y = x_ref[...] + jnp.dot(z_ref[...], w_ref[...], preferred_element_type=F32)
        if not with_loss:
            y_ref[...] = y
            return
        err = y - t_ref[...]
        y_ref[...] = err * (1.0 / D_MODEL)

        @pl.when(i == 0)
        def _():
            loss_ref[...] = jnp.zeros_like(loss_ref)

        sq = jnp.sum((err * err).reshape(tm // 8, 8, D_MODEL), axis=0)
        part = sq[:, 0:LANES]
        for c in range(1, D_MODEL // LANES):
            part = part + sq[:, LANES * c:LANES * (c + 1)]
        loss_ref[...] += part

        @pl.when(i == nt - 1)
        def _():
            loss_ref[...] = jnp.full(loss_ref.shape, (0.5 / D_MODEL) * jnp.sum(loss_ref[...]), F32)

    row = lambda width: pl.BlockSpec((tm, width), lambda i: (i, 0))
    in_specs = [pl.BlockSpec((tm, 1536), lambda i: (i, 0)),
                pl.BlockSpec((HALO, 1536), lambda i: (jnp.maximum(i * (tm // HALO) - 1, 0), 0)),
                pl.BlockSpec((tm, 1536), lambda i: (i, 1)),
                row(512), row(512), row(D_MODEL),
                pl.BlockSpec((D_MIX, D_MODEL), lambda i: (0, 0)),
                pl.BlockSpec((HALO, 512), lambda i: (0, 0))]
    out_specs = [row(D_MODEL), pl.BlockSpec((D_MIX, tm), lambda i: (0, i))]
    out_shape = [jax.ShapeDtypeStruct((s, D_MODEL), F32), jax.ShapeDtypeStruct((D_MIX, s), BF16)]
    operands = [proj, proj, proj, o_mla, o_swa, x, w_out, cw]
    if with_loss:
        in_specs.append(row(D_MODEL))
        out_specs.append(pl.BlockSpec((8, LANES), lambda i: (0, 0)))
        out_shape.append(jax.ShapeDtypeStruct((8, LANES), F32))
        operands.append(target)
    return pl.pallas_call(
        body, name="out_fwd_loss" if with_loss else "out_fwd", grid=(nt,),
        in_specs=in_specs, out_specs=out_specs, out_shape=out_shape,
        scratch_shapes=[pltpu.VMEM((tm, D_MIX), BF16)],
        compiler_params=_params(("arbitrary",) if with_loss else ("parallel",)),
    )(*operands)


def _out_bwd(dy, proj, o_mla, o_swa, w_out, cw):
    s = proj.shape[0]
    tm = min(512, s)
    nt = s // tm
    hb = tm // HALO

    def body(dy_ref, dyn_ref, conv_ref, convp_ref, convn_ref, gates_ref, gatesn_ref, om_ref, os_ref,
             w_ref, cw_ref,
             dconv_ref, dgates_ref, dom_ref, domt_ref, delta_ref, dos_ref, dcw_ref):
        i = pl.program_id(0)
        dz = _mm_nt(dy_ref[...], w_ref[...])

        def gate(n):
            g = gates_ref[:, GROUP * n:GROUP * (n + 1)]
            sg = _sigmoid(g)
            return g * sg, sg * (1.0 + g * (1.0 - sg))

        for n, o_ref, do_ref in ((0, om_ref, dom_ref), (2, os_ref, dos_ref)):
            silu, dsilu = gate(n)
            dzn = dz[:, GROUP * n:GROUP * (n + 1)]
            o = o_ref[...]
            do = dzn * silu
            do_ref[...] = do.astype(do_ref.dtype)
            dgates_ref[:, GROUP * n:GROUP * (n + 1)] = (dzn * o * dsilu).astype(BF16)
            if n == 0:
                domt_ref[...] = jnp.transpose(do).astype(BF16)
                t = do * o
                for h in range(MLA_HEADS):
                    delta_ref[:, h:h + 1] = jnp.sum(t[:, MLA_V * h:MLA_V * (h + 1)], axis=-1,
                                                    keepdims=True)

        c_h, c_b, c_c, u, u1, u2, yc = _conv_fwd_vals(conv_ref, convp_ref, cw_ref, i == 0)
        silu, dsilu = gate(1)
        dzc = dz[:, GROUP:2 * GROUP]
        dgates_ref[:, GROUP:2 * GROUP] = (dzc * (c_b * yc) * dsilu).astype(BF16)
        dycr = dzc * silu
        dyc = dycr * c_b
        gn = gatesn_ref[:, GROUP:2 * GROUP]
        dzc_n = _mm_nt(dyn_ref[...], w_ref[GROUP:2 * GROUP, :])
        dyc_n = jnp.where(i == nt - 1, 0.0, dzc_n * (gn * _sigmoid(gn)) * convn_ref[:, 512:1024])
        d1 = _shift_up(dyc, dyc_n, 1)
        d2 = _shift_up(dyc, dyc_n, 2)
        du = cw_ref[2:3, :] * dyc + cw_ref[1:2, :] * d1 + cw_ref[0:1, :] * d2
        dconv_ref[:, 0:512] = (du * c_c).astype(BF16)
        dconv_ref[:, 512:1024] = (dycr * yc).astype(BF16)
        dconv_ref[:, 1024:1536] = (du * c_h).astype(BF16)

        @pl.when(i == 0)
        def _():
            dcw_ref[...] = jnp.zeros_like(dcw_ref)

        for k, uk in enumerate((u2, u1, u)):
            dcw_ref[k:k + 1, :] += jnp.sum(dyc * uk, axis=0, keepdims=True)

    row = lambda width: pl.BlockSpec((tm, width), lambda i: (i, 0))
    prev = lambda i: jnp.maximum(i * hb - 1, 0)
    nxt = lambda i: jnp.minimum((i + 1) * hb, s // HALO - 1)
    return pl.pallas_call(
        body, name="out_bwd", grid=(nt,),
        in_specs=[row(D_MODEL),
                  pl.BlockSpec((HALO, D_MODEL), lambda i: (nxt(i), 0)),
                  pl.BlockSpec((tm, 1536), lambda i: (i, 0)),
                  pl.BlockSpec((HALO, 1536), lambda i: (prev(i), 0)),
                  pl.BlockSpec((HALO, 1536), lambda i: (nxt(i), 0)),
                  pl.BlockSpec((tm, 1536), lambda i: (i, 1)),
                  pl.BlockSpec((HALO, 1536), lambda i: (nxt(i), 1)),
                  row(512), row(512),
                  pl.BlockSpec((D_MIX, D_MODEL), lambda i: (0, 0)),
                  pl.BlockSpec((HALO, 512), lambda i: (0, 0))],
        out_specs=[row(1536), row(1536), row(512), pl.BlockSpec((512, tm), lambda i: (0, i)),
                   row(MLA_HEADS), row(512), pl.BlockSpec((HALO, 512), lambda i: (0, 0))],
        out_shape=[jax.ShapeDtypeStruct((s, 1536), BF16), jax.ShapeDtypeStruct((s, 1536), BF16),
                   jax.ShapeDtypeStruct((s, 512), BF16), jax.ShapeDtypeStruct((512, s), BF16),
                   jax.ShapeDtypeStruct((s, MLA_HEADS), F32),
                   jax.ShapeDtypeStruct((s, 512), BF16), jax.ShapeDtypeStruct((HALO, 512), F32)],
        compiler_params=_params(("arbitrary",)),
    )(dy, dy, proj, proj, proj, proj, proj, o_mla, o_swa, w_out, cw)


def _adam_update(g, w, m, v):
    c1 = 1.0 - ADAM_B1
    c2 = 1.0 - ADAM_B2
    bc1 = 1.0 - ADAM_B1 ** ADAM_STEP
    bc2 = 1.0 - ADAM_B2 ** ADAM_STEP
    m_new = ADAM_B1 * m + c1 * g
    v_new = ADAM_B2 * v + c2 * (g * g)
    delta = -ADAM_LR * ((m_new / bc1) / (jnp.sqrt(v_new / bc2) + ADAM_EPS) + ADAM_WD * w)
    return delta, m_new, v_new


def _adamw(g, w, m, v):
    rows = g.shape[0]
    tr = min(256, rows)

    def body(g_ref, w_ref, m_ref, v_ref, d_ref, mo_ref, vo_ref):
        d_ref[...], mo_ref[...], vo_ref[...] = _adam_update(g_ref[...], w_ref[...], m_ref[...],
                                                            v_ref[...])

    spec = pl.BlockSpec((tr, g.shape[1]), lambda i: (i, 0))
    return pl.pallas_call(
        body, name="adamw", grid=(rows // tr,),
        in_specs=[spec] * 4, out_specs=[spec] * 3,
        out_shape=[jax.ShapeDtypeStruct(g.shape, F32)] * 3,
        compiler_params=_params(("parallel",)),
    )(g, w, m, v)


def _adamw_small(gs, ws, ms, vs):
    n = len(gs)

    def body(*refs):
        ins, outs = refs[:4 * n], refs[4 * n:]
        for a in range(n):
            res = _adam_update(*(ins[kind * n + a][...] for kind in range(4)))
            for kind in range(3):
                outs[kind * n + a][...] = res[kind]

    vmem = pl.BlockSpec(memory_space=pltpu.VMEM)
    out = pl.pallas_call(
        body, name="adamw_small",
        in_specs=[vmem] * (4 * n), out_specs=[vmem] * (3 * n),
        out_shape=[jax.ShapeDtypeStruct(g.shape, F32) for _ in range(3) for g in gs],
    )(*gs, *ws, *ms, *vs)
    return out[:n], out[n:2 * n], out[2 * n:]


HBM_SPEC = pl.BlockSpec(memory_space=pltpu.HBM)


def _place():
    x, y, c = lax.axis_index("x"), lax.axis_index("y"), lax.axis_index("c")
    chips = [(1 - x, y), (x, 1 - y), (1 - x, 1 - y)]
    return x, y, c, chips


def _all_gather(shards):
    na = len(shards)
    halves = [sh.shape[0] // 2 for sh in shards]

    def body(*refs):
        w_refs, a_refs = refs[:na], refs[na:2 * na]
        send_sems, recv_sems = refs[2 * na:]
        x, y, c, chips = _place()
        k = 2 * x + y
        sib = (x, y, 1 - c)

        def slab(a, kk, hc):
            return a_refs[a].at[kk, pl.ds(hc * halves[a], halves[a]), :]

        def copy(a, n, src, dst, to):
            return pltpu.make_async_remote_copy(
                src_ref=src, dst_ref=dst, send_sem=send_sems.at[6 * a + n],
                recv_sem=recv_sems.at[6 * a + n], device_id=to, device_id_type=MESH)

        first = [copy(a, n, w_refs[a].at[pl.ds(c * halves[a], halves[a]), :], slab(a, k, c),
                      (cx, cy, c))
                 for n, (cx, cy) in enumerate(chips) for a in range(na)]
        for cp in first:
            cp.start()
        passed = []
        for n, (cx, cy) in enumerate(chips):
            kk = 2 * cx + cy
            for a in range(na):
                copy(a, n, slab(a, kk, c), slab(a, kk, c), (cx, cy, c)).wait_recv()
                fwd = copy(a, 3 + n, slab(a, kk, c), slab(a, kk, c), sib)
                fwd.start()
                passed.append(fwd)
        for n, (cx, cy) in enumerate(chips):
            kk = 2 * cx + cy
            for a in range(na):
                copy(a, 3 + n, slab(a, kk, 1 - c), slab(a, kk, 1 - c), sib).wait_recv()
        for cp in first + passed:
            cp.wait_send()

    return pl.pallas_call(
        body, name="weights_all_gather",
        in_specs=[HBM_SPEC] * na, out_specs=[HBM_SPEC] * na,
        out_shape=[jax.ShapeDtypeStruct((N_CHIPS,) + sh.shape, sh.dtype) for sh in shards],
        scratch_shapes=[pltpu.SemaphoreType.DMA((6 * na,)), pltpu.SemaphoreType.DMA((6 * na,))],
    )(*shards)


def _fill_own_slab(buf, src, k_idx):
    n, rows, cols = buf.shape
    tr = _row_tile(rows)
    slabs = src.ndim == 3

    def body(k_ref, src_ref, buf_ref, out_ref):
        out_ref[0] = src_ref[0] if slabs else src_ref[...]

    if slabs:
        src_spec = pl.BlockSpec((1, tr, cols), lambda t, k_ref: (k_ref[0], t, 0))
    else:
        src_spec = pl.BlockSpec((tr, cols), lambda t, k_ref: (t, 0))
    return pl.pallas_call(
        body, name="fill_own_slab",
        grid_spec=pltpu.PrefetchScalarGridSpec(
            num_scalar_prefetch=1, grid=(rows // tr,),
            in_specs=[src_spec, pl.BlockSpec(memory_space=pl.ANY)],
            out_specs=pl.BlockSpec((1, tr, cols), lambda t, k_ref: (k_ref[0], t, 0))),
        out_shape=jax.ShapeDtypeStruct(buf.shape, buf.dtype),
        input_output_aliases={2: 0},
        compiler_params=_params(("parallel",)),
    )(k_idx, src, buf)


SEM_SPEC = pl.BlockSpec(memory_space=pltpu.SEMAPHORE)
LATE_COPIES = 6


def _late_copy(a, j, peer_core, src, dst, send_sems, recv_sems, to, sender_core):
    return pltpu.make_async_remote_copy(
        src_ref=src, dst_ref=dst, send_sem=send_sems.at[LATE_COPIES * a + 2 * j + peer_core],
        recv_sem=recv_sems.at[LATE_COPIES * a + 2 * j + sender_core], device_id=to,
        device_id_type=MESH)


def _gather_start(shards, after):
    na = len(shards)

    def body(*refs):
        w_refs, land_refs = refs[:na], refs[na:2 * na]
        send_sems, recv_sems = refs[2 * na + 1], refs[2 * na + 2]
        token = refs[-1]
        x, y, c, chips = _place()
        k = 2 * x + y
        for a in range(na):
            half = w_refs[a].shape[0] // 2
            src = w_refs[a].at[pl.ds(c * half, half), :]
            dst = land_refs[a].at[k, pl.ds(c * half, half), :]
            for j, (cx, cy) in enumerate(chips):
                for tc in range(2):
                    _late_copy(a, j, tc, src, dst, send_sems, recv_sems, (cx, cy, tc), c).start()
        token[...] = jnp.zeros_like(token)

    lands = [pltpu.with_memory_space_constraint(lax.empty((N_CHIPS,) + sh.shape, sh.dtype), pltpu.HBM)
             for sh in shards]
    srcs = [pltpu.with_memory_space_constraint(sh, pltpu.HBM) for sh in shards]
    sems = pltpu.SemaphoreType.DMA((LATE_COPIES * na,))
    aliases = {a: 2 + a for a in range(2 * na)}
    return pl.pallas_call(
        body, name="late_weights_gather_start",
        in_specs=[HBM_SPEC] * (2 * na) + [pl.BlockSpec(memory_space=pl.ANY)],
        out_specs=[SEM_SPEC, SEM_SPEC] + [HBM_SPEC] * (2 * na) + [pl.BlockSpec(memory_space=pltpu.VMEM)],
        out_shape=[sems, sems] + [pltpu.HBM(v.shape, v.dtype) for v in srcs + lands]
        + [jax.ShapeDtypeStruct((8, LANES), F32)],
        input_output_aliases=aliases,
        compiler_params=pltpu.CompilerParams(
            has_side_effects=pltpu.SideEffectType.DATAFLOW_SIDE_EFFECTING),
    )(*srcs, *lands, after)


def _gather_wait(started, na, after):
    send_sems, recv_sems = started[0], started[1]
    bufs = started[2:2 + 2 * na]

    def body(*refs):
        w_refs, land_refs = refs[:na], refs[na:2 * na]
        send_sems, recv_sems = refs[2 * na], refs[2 * na + 1]
        x, y, c, chips = _place()
        k = 2 * x + y
        for a in range(na):
            half = w_refs[a].shape[0] // 2
            src = w_refs[a].at[pl.ds(c * half, half), :]
            for j, (cx, cy) in enumerate(chips):
                kk = 2 * cx + cy
                for pc in range(2):
                    _late_copy(a, j, pc, src, land_refs[a].at[k, pl.ds(c * half, half), :],
                               send_sems, recv_sems, (cx, cy, pc), c).wait_send()
                    pltpu.make_async_remote_copy(
                        src_ref=src, dst_ref=land_refs[a].at[kk, pl.ds(pc * half, half), :],
                        send_sem=send_sems.at[LATE_COPIES * a + 2 * j + pc],
                        recv_sem=recv_sems.at[LATE_COPIES * a + 2 * j + pc],
                        device_id=(cx, cy, pc), device_id_type=MESH).wait_recv()

    out = pl.pallas_call(
        body, name="late_weights_gather_wait",
        in_specs=[HBM_SPEC] * (2 * na) + [SEM_SPEC, SEM_SPEC, pl.BlockSpec(memory_space=pl.ANY)],
        out_specs=[HBM_SPEC] * (2 * na),
        out_shape=[pltpu.HBM(v.shape, v.dtype) for v in bufs],
        input_output_aliases={a: a for a in range(2 * na)},
        compiler_params=pltpu.CompilerParams(
            has_side_effects=pltpu.SideEffectType.DATAFLOW_SIDE_EFFECTING),
    )(*bufs, send_sems, recv_sems, after)
    return out[na:]


def _split_start(name, srcs, land_shapes, n_sems, plan):
    na = len(srcs)

    def body(*refs):
        sends, _ = plan(refs[:na], refs[na:2 * na], refs[2 * na], refs[2 * na + 1])
        for cp in sends:
            cp.start()
        refs[-1][...] = jnp.zeros_like(refs[-1])

    lands = [pltpu.with_memory_space_constraint(lax.empty(shape, dtype), pltpu.HBM)
             for shape, dtype in land_shapes]
    srcs = [pltpu.with_memory_space_constraint(v, pltpu.HBM) for v in srcs]
    sems = pltpu.SemaphoreType.DMA((n_sems,))
    return pl.pallas_call(
        body, name=name,
        in_specs=[HBM_SPEC] * (2 * na),
        out_specs=[SEM_SPEC, SEM_SPEC] + [HBM_SPEC] * (2 * na) + [pl.BlockSpec(memory_space=pltpu.VMEM)],
        out_shape=[sems, sems] + [pltpu.HBM(v.shape, v.dtype) for v in srcs + lands]
        + [jax.ShapeDtypeStruct((8, LANES), F32)],
        input_output_aliases={a: 2 + a for a in range(2 * na)},
        compiler_params=pltpu.CompilerParams(
            has_side_effects=pltpu.SideEffectType.DATAFLOW_SIDE_EFFECTING),
    )(*srcs, *lands)


def _split_wait(name, started, na, plan, after):
    bufs = started[2:2 + 2 * na]

    def body(*refs):
        sends, recvs = plan(refs[:na], refs[na:2 * na], refs[2 * na], refs[2 * na + 1])
        for cp in sends:
            cp.wait_send()
        for cp in recvs:
            cp.wait_recv()

    out = pl.pallas_call(
        body, name=name,
        in_specs=[HBM_SPEC] * (2 * na) + [SEM_SPEC, SEM_SPEC, pl.BlockSpec(memory_space=pl.ANY)],
        out_specs=[HBM_SPEC] * (2 * na),
        out_shape=[pltpu.HBM(v.shape, v.dtype) for v in bufs],
        input_output_aliases={a: a for a in range(2 * na)},
        compiler_params=pltpu.CompilerParams(
            has_side_effects=pltpu.SideEffectType.DATAFLOW_SIDE_EFFECTING),
    )(*bufs, started[0], started[1], after)
    return out[:na], out[na:]


def _plan_to_sibling(g_refs, r_refs, send_sems, recv_sems):
    x, y, c, _ = _place()
    cps = []
    for a, (g, r) in enumerate(zip(g_refs, r_refs)):
        half = g.shape[1] // 2
        cps.append(pltpu.make_async_remote_copy(
            src_ref=g.at[:, pl.ds((1 - c) * half, half), :], dst_ref=r, send_sem=send_sems.at[a],
            recv_sem=recv_sems.at[a], device_id=(x, y, 1 - c), device_id_type=MESH))
    return cps, cps


def _plan_scatter(p_refs, q_refs, send_sems, recv_sems):
    x, y, c, chips = _place()
    k = 2 * x + y
    sends, recvs = [], []
    for a, (p, q) in enumerate(zip(p_refs, q_refs)):
        for i, (cx, cy) in enumerate(chips):
            kk = 2 * cx + cy
            for dst, out in ((q.at[k], sends), (q.at[kk], recvs)):
                out.append(pltpu.make_async_remote_copy(
                    src_ref=p.at[kk], dst_ref=dst, send_sem=send_sems.at[3 * a + i],
                    recv_sem=recv_sems.at[3 * a + i], device_id=(cx, cy, c), device_id_type=MESH))
    return sends, recvs


def _swap_halves_to_sibling(gs):
    na = len(gs)

    def body(*refs):
        g_refs, r_refs = refs[:na], refs[na:2 * na]
        send_sems, recv_sems = refs[2 * na:]
        x, y, c, _ = _place()
        cps = []
        for a in range(na):
            half = g_refs[a].shape[1] // 2
            cps.append(pltpu.make_async_remote_copy(
                src_ref=g_refs[a].at[:, pl.ds((1 - c) * half, half), :], dst_ref=r_refs[a],
                send_sem=send_sems.at[a], recv_sem=recv_sems.at[a], device_id=(x, y, 1 - c),
                device_id_type=MESH))
        for cp in cps:
            cp.start()
        for cp in cps:
            cp.wait()

    return pl.pallas_call(
        body, name="grads_to_sibling",
        in_specs=[HBM_SPEC] * na, out_specs=[HBM_SPEC] * na,
        out_shape=[jax.ShapeDtypeStruct((g.shape[0], g.shape[1] // 2, g.shape[2]), g.dtype)
                   for g in gs],
        scratch_shapes=[pltpu.SemaphoreType.DMA((na,)), pltpu.SemaphoreType.DMA((na,))],
    )(*gs)


def _row_tile(rows):
    return next(t for t in (256, 128, 64) if rows % t == 0)


def _add_sibling(g, r, c_idx, out_dtype):
    n, rows, cols = g.shape
    half = rows // 2
    tr = _row_tile(half)
    nb = half // tr

    def body(c_ref, g_ref, r_ref, p_ref):
        p_ref[...] = (g_ref[...] + r_ref[...]).astype(out_dtype)

    return pl.pallas_call(
        body, name="grads_add_sibling",
        grid_spec=pltpu.PrefetchScalarGridSpec(
            num_scalar_prefetch=1, grid=(n, nb),
            in_specs=[pl.BlockSpec((1, tr, cols), lambda j, t, c_ref: (j, c_ref[0] * nb + t, 0)),
                      pl.BlockSpec((1, tr, cols), lambda j, t, c_ref: (j, t, 0))],
            out_specs=pl.BlockSpec((1, tr, cols), lambda j, t, c_ref: (j, t, 0))),
        out_shape=jax.ShapeDtypeStruct((n, half, cols), out_dtype),
        compiler_params=_params(("parallel", "parallel")),
    )(c_idx, g, r)


def _scatter_to_chips(ps):
    na = len(ps)

    def body(*refs):
        p_refs, q_refs = refs[:na], refs[na:2 * na]
        send_sems, recv_sems = refs[2 * na:]
        x, y, c, chips = _place()
        k = 2 * x + y
        sends = []
        for i, (cx, cy) in enumerate(chips):
            for a in range(na):
                cp = pltpu.make_async_remote_copy(
                    src_ref=p_refs[a].at[2 * cx + cy], dst_ref=q_refs[a].at[k],
                    send_sem=send_sems.at[3 * a + i], recv_sem=recv_sems.at[3 * a + i],
                    device_id=(cx, cy, c), device_id_type=MESH)
                cp.start()
                sends.append(cp)
        for i, (cx, cy) in enumerate(chips):
            kk = 2 * cx + cy
            for a in range(na):
                pltpu.make_async_remote_copy(
                    src_ref=p_refs[a].at[kk], dst_ref=q_refs[a].at[kk],
                    send_sem=send_sems.at[3 * a + i], recv_sem=recv_sems.at[3 * a + i],
                    device_id=(cx, cy, c), device_id_type=MESH).wait_recv()
        for cp in sends:
            cp.wait_send()

    return pl.pallas_call(
        body, name="grads_scatter_to_chips",
        in_specs=[HBM_SPEC] * na, out_specs=[HBM_SPEC] * na,
        out_shape=[jax.ShapeDtypeStruct(p.shape, p.dtype) for p in ps],
        scratch_shapes=[pltpu.SemaphoreType.DMA((3 * na,)), pltpu.SemaphoreType.DMA((3 * na,))],
    )(*ps)


def _sum_chips(q, c_idx):
    n, half, cols = q.shape
    tr = _row_tile(half)
    nb = half // tr

    def body(c_ref, q_ref, o_ref):
        parts = [q_ref[kk].astype(F32) for kk in range(n)]
        o_ref[...] = ((parts[0] + parts[1]) + parts[2]) + parts[3]

    return pl.pallas_call(
        body, name="grads_sum_chips",
        grid_spec=pltpu.PrefetchScalarGridSpec(
            num_scalar_prefetch=1, grid=(nb,),
            in_specs=[pl.BlockSpec((n, tr, cols), lambda t, c_ref: (0, t, 0))],
            out_specs=pl.BlockSpec((tr, cols), lambda t, c_ref: (c_ref[0] * nb + t, 0))),
        out_shape=jax.ShapeDtypeStruct((2 * half, cols), F32),
        compiler_params=_params(("parallel",)),
    )(c_idx, q)


def _join_halves(fulls):
    na = len(fulls)

    def body(*refs):
        o_refs = refs[na:2 * na]
        send_sems, recv_sems = refs[2 * na:]
        x, y, c, _ = _place()
        sends = []
        for a in range(na):
            half = o_refs[a].shape[0] // 2
            rows = o_refs[a].at[pl.ds(c * half, half), :]
            sends.append(pltpu.make_async_remote_copy(
                src_ref=rows, dst_ref=rows, send_sem=send_sems.at[a], recv_sem=recv_sems.at[a],
                device_id=(x, y, 1 - c), device_id_type=MESH))
        for cp in sends:
            cp.start()
        for a in range(na):
            half = o_refs[a].shape[0] // 2
            other = o_refs[a].at[pl.ds((1 - c) * half, half), :]
            pltpu.make_async_remote_copy(
                src_ref=other, dst_ref=other, send_sem=send_sems.at[a], recv_sem=recv_sems.at[a],
                device_id=(x, y, 1 - c), device_id_type=MESH).wait_recv()
        for cp in sends:
            cp.wait_send()

    return pl.pallas_call(
        body, name="grads_join_halves",
        in_specs=[HBM_SPEC] * na, out_specs=[HBM_SPEC] * na,
        out_shape=[jax.ShapeDtypeStruct(f.shape, f.dtype) for f in fulls],
        input_output_aliases={a: a for a in range(na)},
        scratch_shapes=[pltpu.SemaphoreType.DMA((na,)), pltpu.SemaphoreType.DMA((na,))],
    )(*fulls)


def _part_rows(shape):
    size = 1
    for d in shape:
        size *= d
    rows = -(-size // PACK_COLS)
    return size, -(-rows // PACK_ROW_ALIGN) * PACK_ROW_ALIGN


def _pack_rows(arrays, dtype, total_rows):
    parts, used = [], 0
    for a in arrays:
        size, rows = _part_rows(a.shape)
        flat = a.reshape(-1).astype(dtype)
        parts.append(jnp.pad(flat, (0, rows * PACK_COLS - size)).reshape(rows, PACK_COLS))
        used += rows
    parts.append(jnp.zeros((total_rows - used, PACK_COLS), dtype))
    return jnp.concatenate(parts, axis=0)


def _unpack_rows(buf, shapes):
    lead = buf.shape[:-2]
    out, off = [], 0
    for sh in shapes:
        size, rows = _part_rows(sh)
        part = buf[..., off:off + rows, :].reshape(lead + (-1,))[..., :size]
        out.append(part.reshape(lead + tuple(sh)))
        off += rows
    return out


NEW_ORDER = ((928, 1440), (1440, 1952), (1952, 2464), (416, 928), (2464, 2976), (3744, 4256),
             (2976, 3488), (0, 256), (256, 384), (4256, 4320), (384, 416), (4256, 4288),
             (3488, 3616), (3616, 3744))
OLD_ORDER = ((3584, 3840), (3840, 3968), (4032, 4064), (1536, 2048), (0, 512), (512, 1024),
             (1024, 1536), (2048, 2560), (3072, 3584), (4096, 4224), (4224, 4352), (2560, 3072))


def _cols(sources, ranges):
    parts = []
    for a, b in ranges:
        off = 0
        for src in sources:
            width = src.shape[-1]
            lo, hi = max(a, off), min(b, off + width)
            if lo < hi:
                parts.append(src[..., lo - off:hi - off])
            off += width
    return jnp.concatenate(parts, axis=-1)


def _sub_ranges(ranges, a, b):
    out, off = [], 0
    for lo, hi in ranges:
        width = hi - lo
        s0, s1 = max(a, off), min(b, off + width)
        if s0 < s1:
            out.append((lo + s0 - off, lo + s1 - off))
        off += width
    return out


def _rope_tables(s):
    half = MLA_ROPE // 2
    inv_freq = jnp.power(jnp.float32(ROPE_THETA), -jnp.arange(half, dtype=F32) / half)
    ang = inv_freq[:, None] * jnp.arange(s, dtype=F32)[None, :]
    cos, sin = jnp.cos(ang), jnp.sin(ang)
    z = lambda n: jnp.zeros((n, s), F32)
    c = jnp.concatenate([jnp.ones((MLA_NOPE, s), F32), cos, cos, z(32)], axis=0)
    s1 = jnp.concatenate([z(MLA_NOPE), -sin, z(16), z(32)], axis=0)
    s2 = jnp.concatenate([z(MLA_NOPE), z(16), sin, z(32)], axis=0)
    return c, s1, s2


def _pad_lanes(a, n):
    return jnp.pad(a, ((0, 0), (0, n - a.shape[1])))


SHARDED = ("w_in", "w_out", "mla_w_qb", "mla_w_kvb", "conv_w")
REPLICATED = ("norm_g", "mla_q_a_norm", "mla_kv_a_norm", "mla_q_norm", "mla_k_norm",
              "swa_q_norm", "swa_k_norm", "swa_sinks")
WEIGHT_ORDER = ("norm_g", "w_in", "mla_q_a_norm", "mla_w_qb", "mla_kv_a_norm", "mla_w_kvb",
                "mla_q_norm", "mla_k_norm", "conv_w", "swa_q_norm", "swa_k_norm", "swa_sinks", "w_out")
SHARD_AXIS = {"w_in": 2, "w_out": 1, "mla_w_qb": 2, "mla_w_kvb": 2, "conv_w": 2}


def kernel(x, norm_g, w_in, mla_q_a_norm, mla_w_qb, mla_kv_a_norm, mla_w_kvb, mla_q_norm, mla_k_norm, conv_w, swa_q_norm, swa_k_norm, swa_sinks, w_out, loss_target, m_norm_g, m_w_in, m_mla_q_a_norm, m_mla_w_qb, m_mla_kv_a_norm, m_mla_w_kvb, m_mla_q_norm, m_mla_k_norm, m_conv_w, m_swa_q_norm, m_swa_k_norm, m_swa_sinks, m_w_out, v_norm_g, v_w_in, v_mla_q_a_norm, v_mla_w_qb, v_mla_kv_a_norm, v_mla_w_kvb, v_mla_q_norm, v_mla_k_norm, v_conv_w, v_swa_q_norm, v_swa_k_norm, v_swa_sinks, v_w_out):
    weights = dict(norm_g=norm_g, w_in=w_in, mla_q_a_norm=mla_q_a_norm, mla_w_qb=mla_w_qb,
                   mla_kv_a_norm=mla_kv_a_norm, mla_w_kvb=mla_w_kvb, mla_q_norm=mla_q_norm,
                   mla_k_norm=mla_k_norm, conv_w=conv_w, swa_q_norm=swa_q_norm,
                   swa_k_norm=swa_k_norm, swa_sinks=swa_sinks, w_out=w_out)
    mom_m = dict(norm_g=m_norm_g, w_in=m_w_in, mla_q_a_norm=m_mla_q_a_norm, mla_w_qb=m_mla_w_qb,
                 mla_kv_a_norm=m_mla_kv_a_norm, mla_w_kvb=m_mla_w_kvb, mla_q_norm=m_mla_q_norm,
                 mla_k_norm=m_mla_k_norm, conv_w=m_conv_w, swa_q_norm=m_swa_q_norm,
                 swa_k_norm=m_swa_k_norm, swa_sinks=m_swa_sinks, w_out=m_w_out)
    mom_v = dict(norm_g=v_norm_g, w_in=v_w_in, mla_q_a_norm=v_mla_q_a_norm, mla_w_qb=v_mla_w_qb,
                 mla_kv_a_norm=v_mla_kv_a_norm, mla_w_kvb=v_mla_w_kvb, mla_q_norm=v_mla_q_norm,
                 mla_k_norm=v_mla_k_norm, conv_w=v_conv_w, swa_q_norm=v_swa_q_norm,
                 swa_k_norm=v_swa_k_norm, swa_sinks=v_swa_sinks, w_out=v_w_out)
    xs = x[0]
    target = loss_target[0]
    s = xs.shape[0]
    c_idx = lax.axis_index("c").astype(jnp.int32).reshape(1)
    k_idx = (2 * lax.axis_index("x") + lax.axis_index("y")).astype(jnp.int32).reshape(1)

    conv_bits = lax.bitcast_convert_type(conv_w, BF16)
    small_list = [mla_w_qb, mla_w_kvb, conv_bits]
    shard_cols = w_in.shape[2]
    w_in_b = w_in.astype(BF16)
    late = [w_in_b[1], w_out.astype(BF16).reshape(-1, D_MODEL)]
    own = [w_in_b[0], _pack_rows(small_list, BF16, SMALL_ROWS)]
    gathered_in0, gathered_small = [_fill_own_slab(buf, src, k_idx)
                                    for buf, src in zip(_all_gather(own), own)]
    started = _gather_start(late, gathered_in0)
    parts = _unpack_rows(gathered_small, [a.shape for a in small_list])
    join = lambda p, axis: jnp.concatenate([p[k] for k in range(N_CHIPS)], axis=axis)
    w_in_zeros = jnp.zeros((D_MODEL, 64), BF16)
    permuted = lambda slabs: _cols([slabs[k] for k in range(N_CHIPS)] + [w_in_zeros], NEW_ORDER)
    w_qb_full = join(parts[0], 2)
    w_kvb_full = join(parts[1], 2)
    conv_full = lax.bitcast_convert_type(join(parts[2], 2), F32)

    rope = _rope_tables(s)
    swa_tables = _swa_tables()
    layers = []
    for l in range(DEPTH):
        wq = jnp.pad(w_qb_full[l].reshape(MLA_Q_LORA, MLA_HEADS, MLA_QK),
                     ((0, 0), (0, 0), (0, LANES - MLA_QK))).reshape(MLA_Q_LORA, MLA_HEADS * LANES)
        kv = w_kvb_full[l].reshape(MLA_KV_LORA, MLA_HEADS, MLA_NOPE + MLA_V)
        wk = jnp.pad(kv[:, :, :MLA_NOPE], ((0, 0), (0, 0), (0, LANES - MLA_NOPE)))
        wkv = jnp.concatenate([wk.reshape(MLA_KV_LORA, MLA_HEADS * LANES),
                               kv[:, :, MLA_NOPE:].reshape(MLA_KV_LORA, MLA_HEADS * MLA_V)], axis=1)
        layers.append(dict(
            wq=wq, wkv=wkv,
            cw=jnp.pad(conv_full[l], ((0, HALO - 3), (0, 0))),
            g=norm_g[l][None],
            mla_norms=(mla_q_a_norm[l][:, None], mla_kv_a_norm[l][:, None],
                       _pad_lanes(mla_q_norm[l][None], LANES).T, _pad_lanes(mla_k_norm[l][None], LANES).T),
            mla_weights=(wq, wkv, wq.T, wkv.T),
            sqn=swa_q_norm[l][:, None], skn=swa_k_norm[l][:, None], sinks=swa_sinks[l][None]))

    saved = []
    h_in = xs
    layers[0]["w_in"] = permuted(gathered_in0)
    layers[0]["g"] = layers[0]["g"] + started[-1][0:1, 0:1]
    for l in range(DEPTH):
        p = layers[l]
        proj, hb = _in_proj_fwd(h_in, p["g"], p["w_in"])
        q, k, v, qt, kt, vt = _mla_prep_fwd(proj, p["mla_norms"], p["mla_weights"], rope)
        o_mla, lse = _mla_attn_fwd(q, k, vt)
        if l == 0:
            late_in1, late_out = [_fill_own_slab(buf, src, k_idx) for buf, src in
                                  zip(_gather_wait(started, len(late), o_mla), late)]
            layers[1]["w_in"] = permuted(late_in1)
            w_out_full = join(late_out.reshape(N_CHIPS, DEPTH, -1, D_MODEL), 1)
            for n in range(DEPTH):
                layers[n]["w_out"] = w_out_full[n]
        o_swa = _swa_fwd(proj, p["sqn"], p["skn"], p["sinks"], swa_tables)
        last = l == DEPTH - 1
        y, z, *loss_acc = _out_fwd(proj, o_mla, o_swa, h_in, p["w_out"], p["cw"],
                                   target if last else None)
        saved.append(dict(x=h_in, proj=proj, hb=hb, q=q, k=k, v=v, qt=qt, kt=kt, o_mla=o_mla, lse=lse,
                          o_swa=o_swa, z=z))
        h_in = y

    dy, loss_acc = h_in, loss_acc[0]
    loss = lax.psum(loss_acc[0, 0], ("x", "y", "c"))

    grads = {n: [None] * DEPTH for n in WEIGHT_ORDER}

    def in_chunks(l):
        return jnp.stack([_cols(grads["w_in"][l], _sub_ranges(OLD_ORDER, k * shard_cols,
                                                              (k + 1) * shard_cols))
                          for k in range(N_CHIPS)])

    def out_chunks(l):
        return grads["w_out"][l].reshape(N_CHIPS, -1, D_MODEL)

    for l in reversed(range(DEPTH)):
        p, a = layers[l], saved[l]
        dconv, dgates, do_mla, do_mla_t, delta, do_swa, dcw = _out_bwd(dy, a["proj"], a["o_mla"], a["o_swa"],
                                                             p["w_out"], p["cw"])
        grads["w_out"][l] = _weight_grads(a["z"], [dy], "dw_out")[0]
        grads["conv_w"][l] = dcw[0:3]
        delta_rows = jnp.transpose(delta, (1, 0)).reshape(MLA_HEADS // 2, 2, s)
        dq, dk, dv = _mla_attn_bwd(a["q"], a["qt"], a["k"], a["kt"], a["v"], do_mla, do_mla_t,
                                   a["lse"], delta_rows)
        mla_norms = p["mla_norms"]
        if l == 0:
            late_gs, from_sib = _split_wait("late_grads_to_sibling_wait", late_st1, 2,
                                            _plan_to_sibling, dq)
            late_p = [_add_sibling(g, r, c_idx, BF16) for g, r in zip(late_gs, from_sib)]
            late_st2 = _split_start("late_grads_scatter_start", late_p,
                                    [(v.shape, v.dtype) for v in late_p], 6, _plan_scatter)
            mla_norms = (mla_norms[0] + late_st2[-1][0:1, 0:1],) + tuple(mla_norms[1:])
        dmla, dqan, dkvan, dqn, dkn, dwq_t, dwkv_t = _mla_prep_bwd(
            a["proj"], mla_norms, p["mla_weights"], rope, dq, dk, dv)
        dwq, dwkv = dwq_t.T, dwkv_t.T
        dsq, dskv, dsqn, dskn, dsinks = _swa_bwd(a["proj"], p["sqn"], p["skn"], p["sinks"], swa_tables, do_swa)
        if l == 0:
            late_p, late_q = _split_wait("late_grads_scatter_wait", late_st2, 2, _plan_scatter, dsq)
            late_full = [_sum_chips(_fill_own_slab(q_, p_, k_idx), c_idx)
                         for q_, p_ in zip(late_q, late_p)]
        pieces = [dconv, dgates, dsq, dmla, dskv]
        grads["w_in"][l] = _weight_grads(a["hb"], pieces, "dw_in")
        gain = p["g"]
        if l == 0:
            gs0 = (in_chunks(0), out_chunks(0))
            last_p = [_add_sibling(g, r, c_idx, BF16)
                      for g, r in zip(gs0, _swap_halves_to_sibling(gs0))]
            last_st = _split_start("last_grads_scatter_start", last_p,
                                   [(v.shape, v.dtype) for v in last_p], 6, _plan_scatter)
            gain = gain + last_st[-1][0:1, 0:1]
        dx, dg = _in_proj_bwd(pieces, a["x"], gain, p["w_in"], dy)
        if l == 0:
            last_p, last_q = _split_wait("last_grads_scatter_wait", last_st, 2, _plan_scatter, dx)
            last_full = [_sum_chips(_fill_own_slab(q_, p_, k_idx), c_idx)
                         for q_, p_ in zip(last_q, last_p)]
        grads["norm_g"][l] = dg[0]
        grads["mla_q_a_norm"][l] = dqan[:, 0]
        grads["mla_kv_a_norm"][l] = dkvan[:, 0]
        grads["mla_q_norm"][l] = dqn[:MLA_QK, 0]
        grads["mla_k_norm"][l] = dkn[:MLA_QK, 0]
        grads["mla_w_qb"][l] = dwq.reshape(MLA_Q_LORA, MLA_HEADS, LANES)[:, :, :MLA_QK].reshape(
            MLA_Q_LORA, MLA_HEADS * MLA_QK)
        dwk = dwkv[:, :MLA_HEADS * LANES].reshape(MLA_KV_LORA, MLA_HEADS, LANES)[:, :, :MLA_NOPE]
        dwv = dwkv[:, MLA_HEADS * LANES:].reshape(MLA_KV_LORA, MLA_HEADS, MLA_V)
        grads["mla_w_kvb"][l] = jnp.concatenate([dwk, dwv], axis=2).reshape(
            MLA_KV_LORA, MLA_HEADS * (MLA_NOPE + MLA_V))
        grads["swa_q_norm"][l] = dsqn[:, 0]
        grads["swa_k_norm"][l] = dskn[:, 0]
        grads["swa_sinks"][l] = dsinks[0]
        dy = dx
        if l == DEPTH - 1:
            late_st1 = _split_start(
                "late_grads_to_sibling_start", [in_chunks(l), out_chunks(l)],
                [((N_CHIPS, D_MODEL // 2, shard_cols), F32),
                 ((N_CHIPS, D_MIX // N_CHIPS // 2, D_MODEL), F32)], 2, _plan_to_sibling)
    grad_x = dy[None]
    full_grads = {n: jnp.stack(grads[n]) for n in WEIGHT_ORDER if n not in ("w_in", "w_out")}

    rest = tuple(n for n in SHARDED if n not in ("w_in", "w_out"))
    rep_shapes = [weights[n].shape for n in REPLICATED]
    rep_grads = jnp.concatenate([full_grads[n].reshape(-1) for n in REPLICATED])

    def chunk(g, n, k):
        width = g.shape[SHARD_AXIS[n]] // N_CHIPS
        return lax.slice_in_dim(g, k * width, (k + 1) * width, axis=SHARD_AXIS[n])

    g_small = jnp.stack([_pack_rows([chunk(full_grads[n], n, k) for n in rest] + [rep_grads],
                                    F32, SMALL_ROWS) for k in range(N_CHIPS)])
    gs = (g_small,)
    partial = [_add_sibling(g, r, c_idx, F32) for g, r in zip(gs, _swap_halves_to_sibling(gs))]
    by_chip = [_fill_own_slab(q, p, k_idx) for q, p in zip(_scatter_to_chips(partial), partial)]
    g_small_mine, g_in0, g_out0, g_in1, g_out1 = _join_halves(
        [_sum_chips(q, c_idx) for q in by_chip] + last_full + late_full)

    vals = _unpack_rows(g_small_mine, [weights[n].shape for n in rest] + [(rep_grads.shape[0],)])
    grad = dict(zip(rest, vals[:-1]))
    grad["w_in"] = jnp.stack([g_in0, g_in1])
    grad["w_out"] = jnp.stack([g_out0, g_out1])
    off = 0
    for n, sh in zip(REPLICATED, rep_shapes):
        grad[n] = vals[-1][off:off + sh[0] * sh[1]].reshape(sh)
        off += sh[0] * sh[1]
    results = {}
    for n in ("w_in", "w_out"):
        view = lambda a, n=n: a.reshape(-1, weights[n].shape[-1])
        res = _adamw(view(grad[n]), view(weights[n]), view(mom_m[n]), view(mom_v[n]))
        results[n] = [r.reshape(weights[n].shape) for r in res]
    small = tuple(n for n in WEIGHT_ORDER if n not in results)
    res = _adamw_small(*([d[n] for n in small] for d in (grad, weights, mom_m, mom_v)))
    for a, n in enumerate(small):
        results[n] = [res[kind][a] for kind in range(3)]
    unpacked = [grad] + [{n: results[n][kind] for n in WEIGHT_ORDER} for kind in range(3)]
    outs = [loss, grad_x]
    for group in unpacked:
        outs += [group[n] for n in WEIGHT_ORDER]
    return tuple(outs)
```

```python
import jax
import numpy as np
import jax.numpy as jnp
from jax import lax
from jax.experimental import pallas as pl
from jax.experimental.pallas import tpu as pltpu

F32 = jnp.float32
BF16 = jnp.bfloat16

D_MODEL = 1024
DEPTH = 2
GROUP = 512
D_MIX = 3 * GROUP
BLOCK = 128
RMS_EPS = 1e-6
NEG_INF = -1e30
MLA_HEADS = 8
MLA_QK = 96
MLA_NOPE = 64
MLA_ROPE = 32
MLA_V = 64
V_AUG = 80
MLA_Q_LORA = 256
MLA_KV_LORA = 128
ROPE_THETA = 10000.0
SWA_HEADS = 8
SWA_KV = 2
SWA_GROUP = 4
SWA_DIM = 64
N_CHIPS = 4

NC = 4352
OFF_SQ, OFF_MLA, OFF_SKV = 3072, 3584, 4096

VMEM_LIMIT = 56 * 1024 * 1024
LANES = 128
PACK_COLS = 1024
PACK_ROW_ALIGN = 16
SMALL_ROWS = 256

ADAM_LR = 0.001
ADAM_B1 = 0.9
ADAM_B2 = 0.999
ADAM_EPS = 1e-08
ADAM_WD = 0.01
ADAM_STEP = 10

MESH = pl.DeviceIdType.MESH


def _params(sem, vmem=VMEM_LIMIT):
    return pltpu.CompilerParams(dimension_semantics=sem, vmem_limit_bytes=vmem)


def _dot(a, b, dims):
    return lax.dot_general(a.astype(BF16), b.astype(BF16), (dims, ((), ())),
                           preferred_element_type=F32)


def _mm(a, b):
    return _dot(a, b, ((1,), (0,)))


def _mm_nt(a, b):
    return _dot(a, b, ((1,), (1,)))


def _rms(x, g, n=None):
    n = x.shape[-1] if n is None else n
    ms = jnp.sum(x * x, axis=-1, keepdims=True) * (1.0 / n)
    return x * lax.rsqrt(ms + RMS_EPS) * g


def _sigmoid(x):
    return 1.0 / (1.0 + jnp.exp(-x))


def _in_proj_fwd(x, g, w):
    s = x.shape[0]
    tm = min(512, s)

    def body(x_ref, g_ref, w_ref, proj_ref, hbt_ref):
        h = _rms(x_ref[...], g_ref[...])
        hbt_ref[...] = jnp.transpose(h).astype(BF16)
        proj_ref[...] = jnp.dot(h.astype(BF16), w_ref[...], preferred_element_type=F32)

    return pl.pallas_call(
        body, name="in_proj_fwd", grid=(s // tm,),
        in_specs=[pl.BlockSpec((tm, D_MODEL), lambda i: (i, 0)),
                  pl.BlockSpec((1, D_MODEL), lambda i: (0, 0)),
                  pl.BlockSpec((D_MODEL, NC), lambda i: (0, 0))],
        out_specs=[pl.BlockSpec((tm, NC), lambda i: (i, 0)),
                   pl.BlockSpec((D_MODEL, tm), lambda i: (0, i))],
        out_shape=[jax.ShapeDtypeStruct((s, NC), F32), jax.ShapeDtypeStruct((D_MODEL, s), BF16)],
        compiler_params=_params(("parallel",)),
    )(x, g, w)


def _in_proj_bwd(pieces, x, g, w, dres):
    s = x.shape[0]
    tm = min(512, s)
    n_p = len(pieces)

    def body(*refs):
        p_refs = refs[:n_p]
        x_ref, g_ref, w_ref, dres_ref, dx_ref, dg_ref = refs[n_p:]
        dh = None
        off = 0
        for r in p_refs:
            width = r.shape[1]
            t = _mm_nt(r[...], w_ref[:, off:off + width])
            dh = t if dh is None else dh + t
            off += width
        _, vjp = jax.vjp(_rms, x_ref[...], g_ref[...])
        dx, dg = vjp(dh)
        dx_ref[...] = dx + dres_ref[...]

        @pl.when(pl.program_id(0) == 0)
        def _():
            dg_ref[...] = jnp.zeros_like(dg_ref)

        dg_ref[...] += dg

    in_specs = [pl.BlockSpec((tm, p.shape[1]), lambda i: (i, 0)) for p in pieces]
    in_specs += [pl.BlockSpec((tm, D_MODEL), lambda i: (i, 0)),
                 pl.BlockSpec((1, D_MODEL), lambda i: (0, 0)),
                 pl.BlockSpec((D_MODEL, NC), lambda i: (0, 0)),
                 pl.BlockSpec((tm, D_MODEL), lambda i: (i, 0))]
    return pl.pallas_call(
        body, name="in_proj_bwd", grid=(s // tm,),
        in_specs=in_specs,
        out_specs=[pl.BlockSpec((tm, D_MODEL), lambda i: (i, 0)),
                   pl.BlockSpec((1, D_MODEL), lambda i: (0, 0))],
        out_shape=[jax.ShapeDtypeStruct((s, D_MODEL), F32), jax.ShapeDtypeStruct((1, D_MODEL), F32)],
        compiler_params=_params(("arbitrary",)),
    )(*pieces, x, g, w, dres)


def _weight_grads(at, bs, name):
    m, s = at.shape
    nb = len(bs)
    tk = min(512, s)

    def body(a_ref, *refs):
        b_refs, o_refs = refs[:nb], refs[nb:]

        @pl.when(pl.program_id(0) == 0)
        def _():
            for o_ref in o_refs:
                o_ref[...] = jnp.zeros_like(o_ref)

        a = a_ref[...]
        for b_ref, o_ref in zip(b_refs, o_refs):
            o_ref[...] += _mm(a, b_ref[...])

    return pl.pallas_call(
        body, name=name, grid=(s // tk,),
        in_specs=[pl.BlockSpec((m, tk), lambda k: (0, k))]
        + [pl.BlockSpec((tk, b.shape[1]), lambda k: (k, 0)) for b in bs],
        out_specs=[pl.BlockSpec((m, b.shape[1]), lambda k: (0, 0)) for b in bs],
        out_shape=[jax.ShapeDtypeStruct((m, b.shape[1]), F32) for b in bs],
        compiler_params=_params(("arbitrary",)),
    )(at, *bs)


def _rms0(x, g, n=None):
    n = x.shape[0] if n is None else n
    ms = jnp.sum(x * x, axis=0, keepdims=True) * (1.0 / n)
    return x * lax.rsqrt(ms + RMS_EPS) * g


@jax.custom_vjp
def _rope0(t, c, s1, s2):
    return t * c + pltpu.roll(t, LANES - 16, 0) * s1 + pltpu.roll(t, 16, 0) * s2


def _rope0_fwd(t, c, s1, s2):
    return _rope0(t, c, s1, s2), (c, s1, s2)


def _rope0_bwd(res, g):
    c, s1, s2 = res
    dt = g * c + pltpu.roll(g * s1, 16, 0) + pltpu.roll(g * s2, LANES - 16, 0)
    return dt, jnp.zeros_like(c), jnp.zeros_like(s1), jnp.zeros_like(s2)


_rope0.defvjp(_rope0_fwd, _rope0_bwd)


@jax.custom_vjp
def _mmw(w, wt, x):
    return _mm(w, x)


def _mmw_fwd(w, wt, x):
    return _mm(w, x), (wt, x)


def _mmw_bwd(res, g):
    wt, x = res
    return _mm_nt(g, x), jnp.zeros_like(wt), _mm(wt, g)


_mmw.defvjp(_mmw_fwd, _mmw_bwd)


def _prep_fn(q_lat, kv_lat, kr, qan, kvan, qn, kn, wq, wk, wv, wqt, wkt, wvt, c, s1, s2, mm):
    tokens = q_lat.shape[1]
    rq = _rms0(q_lat, qan)
    rkv = _rms0(kv_lat, kvan)
    qn_b = jnp.broadcast_to(qn, (LANES, tokens))
    kn_b = jnp.broadcast_to(kn, (LANES, tokens))
    qs, ks = [], []
    for h in range(MLA_HEADS):
        qs.append(_rope0(_rms0(mm(wq[h], wqt[h], rq), qn_b, MLA_QK), c, s1, s2))
        ks.append(_rope0(_rms0(mm(wk[h], wkt[h], rkv) + kr, kn_b, MLA_QK), c, s1, s2))
    return tuple(qs), tuple(ks), mm(wv, wvt, rkv)


def _prep_weights(wq_ref, wkv_ref, wqt_ref, wkvt_ref):
    heads = range(MLA_HEADS)
    wq = tuple(wqt_ref[LANES * h:LANES * (h + 1), :].astype(F32) for h in heads)
    wk = tuple(wkvt_ref[LANES * h:LANES * (h + 1), :].astype(F32) for h in heads)
    wv = wkvt_ref[LANES * MLA_HEADS:, :].astype(F32)
    wqt = tuple(wq_ref[:, LANES * h:LANES * (h + 1)].astype(F32) for h in heads)
    wkt = tuple(wkv_ref[:, LANES * h:LANES * (h + 1)].astype(F32) for h in heads)
    wvt = wkv_ref[:, LANES * MLA_HEADS:].astype(F32)
    return wq, wk, wv, wqt, wkt, wvt


def _prep_in_specs(tm):
    const = lambda shape: pl.BlockSpec(shape, lambda i: (0, 0))
    col = lambda height: pl.BlockSpec((height, tm), lambda i: (0, i))
    return [pl.BlockSpec((tm, 512), lambda i: (i, OFF_MLA // 512)),
            const((MLA_Q_LORA, 1)), const((MLA_KV_LORA, 1)), const((LANES, 1)), const((LANES, 1)),
            const((MLA_Q_LORA, 1024)), const((MLA_KV_LORA, 1536)),
            const((1024, MLA_Q_LORA)), const((1536, MLA_KV_LORA)),
            col(LANES), col(LANES), col(LANES)]


def _prep_operands(blk_ref, refs):
    qan_ref, kvan_ref, qn_ref, kn_ref, wq_ref, wkv_ref, wqt_ref, wkvt_ref, c_ref, s1_ref, s2_ref = refs
    blk_t = jnp.transpose(blk_ref[...])
    diff = (blk_t[0:256], blk_t[256:384], blk_t[384:512],
            qan_ref[...], kvan_ref[...], qn_ref[...], kn_ref[...])
    weights = _prep_weights(wq_ref, wkv_ref, wqt_ref, wkvt_ref)
    return diff, weights, (c_ref[...], s1_ref[...], s2_ref[...])


def _mla_prep_fwd(proj, norms, weights, rope):
    s = proj.shape[0]
    tm = min(512, s)

    def body(blk_ref, *refs):
        ins, (q_ref, k_ref, v_ref, qt_ref, kt_ref, vt_ref) = refs[:11], refs[11:]
        diff, (wq, wk, wv, wqt, wkt, wvt), tables = _prep_operands(blk_ref, ins)
        qs, ks, v = _prep_fn(*diff, wq, wk, wv, wqt, wkt, wvt, *tables,
                             lambda w, wt, x: _mm(w, x))
        for h in range(MLA_HEADS):
            q2 = qs[h] * Q_PRESCALE
            qt_ref[LANES * h:LANES * (h + 1), :] = q2.astype(BF16)
            kt_ref[LANES * h:LANES * (h + 1), :] = ks[h].astype(BF16)
            q_ref[:, LANES * h:LANES * (h + 1)] = jnp.transpose(q2).astype(BF16)
            k_ref[:, LANES * h:LANES * (h + 1)] = jnp.transpose(ks[h]).astype(BF16)
        ones_row = (lax.broadcasted_iota(jnp.int32, (V_AUG - MLA_V, v.shape[1]), 0) == 0).astype(BF16)
        for h in range(MLA_HEADS):
            vt_ref[V_AUG * h:V_AUG * h + MLA_V, :] = v[MLA_V * h:MLA_V * (h + 1)].astype(BF16)
            vt_ref[V_AUG * h + MLA_V:V_AUG * (h + 1), :] = ones_row
        v_ref[...] = jnp.transpose(v).astype(BF16)

    row = lambda width: pl.BlockSpec((tm, width), lambda i: (i, 0))
    col = lambda height: pl.BlockSpec((height, tm), lambda i: (0, i))
    return pl.pallas_call(
        body, name="mla_prep_fwd", grid=(s // tm,),
        in_specs=_prep_in_specs(tm),
        out_specs=[row(1024), row(1024), row(512), col(1024), col(1024), col(MLA_HEADS * V_AUG)],
        out_shape=[jax.ShapeDtypeStruct((s, 1024), BF16), jax.ShapeDtypeStruct((s, 1024), BF16),
                   jax.ShapeDtypeStruct((s, 512), BF16), jax.ShapeDtypeStruct((1024, s), BF16),
                   jax.ShapeDtypeStruct((1024, s), BF16),
                   jax.ShapeDtypeStruct((MLA_HEADS * V_AUG, s), BF16)],
        compiler_params=_params(("parallel",)),
    )(proj, *norms, *weights, *rope)


def _mla_prep_bwd(proj, norms, weights, rope, dq, dk, dv):
    s = proj.shape[0]
    tm = min(512, s)

    def body(blk_ref, *refs):
        ins, (dq_ref, dk_ref, dv_ref) = refs[:11], refs[11:14]
        dblk_ref, dqan_ref, dkvan_ref, dqn_ref, dkn_ref, dwq_ref, dwkv_ref = refs[14:]
        diff, (wq, wk, wv, wqt, wkt, wvt), tables = _prep_operands(blk_ref, ins)

        def fn(q_lat, kv_lat, kr, qan, kvan, qn, kn, wq_, wk_, wv_):
            return _prep_fn(q_lat, kv_lat, kr, qan, kvan, qn, kn, wq_, wk_, wv_, wqt, wkt, wvt,
                            *tables, _mmw)

        _, vjp = jax.vjp(fn, *diff, wq, wk, wv)
        heads = range(MLA_HEADS)
        cts = (tuple(dq_ref[LANES * h:LANES * (h + 1), :] for h in heads),
               tuple(dk_ref[LANES * h:LANES * (h + 1), :] for h in heads), dv_ref[...])
        dq_lat, dkv_lat, dkr, dqan, dkvan, dqn, dkn, dwq_h, dwk_h, dwv = vjp(cts)
        dblk_ref[...] = jnp.transpose(
            jnp.concatenate([dq_lat, dkv_lat, dkr], axis=0)).astype(BF16)

        @pl.when(pl.program_id(0) == 0)
        def _():
            for r in (dqan_ref, dkvan_ref, dqn_ref, dkn_ref, dwq_ref, dwkv_ref):
                r[...] = jnp.zeros_like(r)

        dqan_ref[...] += dqan
        dkvan_ref[...] += dkvan
        dqn_ref[...] += dqn
        dkn_ref[...] += dkn
        for h in heads:
            dwq_ref[LANES * h:LANES * (h + 1), :] += dwq_h[h]
            dwkv_ref[LANES * h:LANES * (h + 1), :] += dwk_h[h]
        dwkv_ref[LANES * MLA_HEADS:, :] += dwv

    const = lambda shape: pl.BlockSpec(shape, lambda i: (0, 0))
    col = lambda height: pl.BlockSpec((height, tm), lambda i: (0, i))
    shapes = [(MLA_Q_LORA, 1), (MLA_KV_LORA, 1), (LANES, 1), (LANES, 1),
              (1024, MLA_Q_LORA), (1536, MLA_KV_LORA)]
    return pl.pallas_call(
        body, name="mla_prep_bwd", grid=(s // tm,),
        in_specs=_prep_in_specs(tm) + [col(1024), col(1024), col(512)],
        out_specs=[pl.BlockSpec((tm, 512), lambda i: (i, 0))] + [const(sh) for sh in shapes],
        out_shape=[jax.ShapeDtypeStruct((s, 512), BF16)]
        + [jax.ShapeDtypeStruct(sh, F32) for sh in shapes],
        compiler_params=_params(("arbitrary",)),
    )(proj, *norms, *weights, *rope, dq, dk, dv)


MLA_SCALE = MLA_QK ** -0.5


LOG2E = 1.4426950408889634
LN2 = 0.6931471805599453
Q_PRESCALE = MLA_SCALE * LOG2E
HEAD_GROUPS = ((0, 1),)


def _mla_attn_fwd(q2, k, vt):
    s = q2.shape[0]
    t = min(512, s)
    tk = min(128, s)
    nq = s // t
    r = t // tk

    def body(q_ref, k_ref, vt_ref, o_ref, lse_ref, acc_ref):
        i = pl.program_id(1)
        row = lax.broadcasted_iota(jnp.int32, (tk, t), 0)
        col = lax.broadcasted_iota(jnp.int32, (tk, t), 1)
        qh = [q_ref[:, LANES * hh:LANES * (hh + 1)] for hh in range(2)]
        acc_ref[...] = jnp.zeros_like(acc_ref)

        def scores(j, heads, diag=None):
            r0 = pl.multiple_of(j * tk, tk)
            q_lo = 0 if diag is None else diag * tk
            out = []
            for hh in heads:
                kc = k_ref[pl.ds(r0, tk), LANES * hh:LANES * (hh + 1)]
                sc = lax.dot_general(kc, qh[hh][q_lo:], (((1,), (1,)), ((), ())),
                                     preferred_element_type=F32)
                out.append(sc if diag is None
                           else jnp.where(row[:, :t - q_lo] <= col[:, :t - q_lo], sc, NEG_INF))
            return tuple(out)

        for heads in HEAD_GROUPS:
            stats = tuple(jnp.full((1, t), NEG_INF, F32) for _ in heads)

            def consume(j, scs, stats, q_lo=0, heads=heads):
                r0 = pl.multiple_of(j * tk, tk)
                out, ps, alphas = [], [], []
                for n, hh in enumerate(heads):
                    m_old = stats[n][:, q_lo:]
                    m_new = jnp.maximum(m_old, jnp.max(scs[n], axis=0, keepdims=True))
                    ps.append(jnp.exp2(scs[n] - m_new).astype(BF16))
                    alphas.append(jnp.exp2(m_old - m_new))
                    out.append(m_new if q_lo == 0
                               else jnp.concatenate([stats[n][:, :q_lo], m_new], axis=1))
                for n, hh in enumerate(heads):
                    vc = vt_ref[V_AUG * hh:V_AUG * (hh + 1), pl.ds(r0, tk)]
                    acc_ref[hh, :, q_lo:] = alphas[n] * acc_ref[hh, :, q_lo:] + jnp.dot(
                        vc, ps[n], preferred_element_type=F32)
                return tuple(out)

            def group(j0, stats, diag, heads=heads):
                scs = [scores(j0 + d, heads, d if diag else None) for d in range(r)]
                for d in range(r):
                    stats = consume(j0 + d, scs[d], stats, d * tk if diag else 0)
                return stats

            stats = group(r * i, stats, True)
            odd = i % 2
            stats = lax.fori_loop(0, odd, lambda j, st: group(r * j, st, False), stats)
            def several(n, j0, st):
                for u in range(n):
                    st = group(r * (j0 + u), st, False)
                return st

            two = (i // 2) % 2
            stats = lax.fori_loop(0, two, lambda p, st: several(2, odd, st), stats)
            first = odd + 2 * two
            stats = lax.fori_loop(0, i // 4, lambda p, st: several(4, first + 4 * p, st), stats)
            for n, hh in enumerate(heads):
                l = acc_ref[hh, MLA_V:MLA_V + 1, :]
                o_ref[:, MLA_V * hh:MLA_V * (hh + 1)] = jnp.transpose(acc_ref[hh, 0:MLA_V, :] / l)
                lse_ref[0, hh:hh + 1, :] = stats[n] + jnp.log2(l)

    return pl.pallas_call(
        body, name="mla_attn_fwd", grid=(MLA_HEADS // 2, nq),
        in_specs=[pl.BlockSpec((t, 256), lambda p, i: (i, p)),
                  pl.BlockSpec((s, 256), lambda p, i: (0, p)),
                  pl.BlockSpec((2 * V_AUG, s), lambda p, i: (p, 0))],
        out_specs=[pl.BlockSpec((t, 128), lambda p, i: (i, p)),
                   pl.BlockSpec((1, 2, t), lambda p, i: (p, 0, i))],
        out_shape=[jax.ShapeDtypeStruct((s, 512), F32),
                   jax.ShapeDtypeStruct((MLA_HEADS // 2, 2, s), F32)],
        scratch_shapes=[pltpu.VMEM((2, V_AUG, t), F32)],
        compiler_params=_params(("parallel", "arbitrary")),
    )(q2, k, vt)


def _mla_attn_bwd(q2, q2t, k, kt, v, do, dot, lse_rows, delta_rows):
    s = q2.shape[0]
    t = min(512, s)
    nq = s // t

    def body(q_ref, qt_ref, k_ref, kt_ref, v_ref, do_ref, dot_ref, lse_ref, dl_ref,
             dq_ref, dk_ref, dv_ref):
        j = pl.program_id(1)

        @pl.when(j == 0)
        def _():
            dq_ref[...] = jnp.zeros_like(dq_ref)

        dk_ref[...] = jnp.zeros_like(dk_ref)
        dv_ref[...] = jnp.zeros_like(dv_ref)
        row = lax.broadcasted_iota(jnp.int32, (t, t), 0)
        col = lax.broadcasted_iota(jnp.int32, (t, t), 1)
        causal_t = row <= col
        kh = [k_ref[:, LANES * hh:LANES * (hh + 1)] for hh in range(2)]
        kth = [kt_ref[LANES * hh:LANES * (hh + 1), :] for hh in range(2)]
        vh = [v_ref[:, MLA_V * hh:MLA_V * (hh + 1)] for hh in range(2)]
        nt = (((1,), (1,)), ((), ()))

        def step(i, masked):
            r0 = pl.multiple_of(i * t, t)
            sd = []
            for hh in range(2):
                qh = q_ref[pl.ds(r0, t), LANES * hh:LANES * (hh + 1)]
                doh = do_ref[pl.ds(r0, t), MLA_V * hh:MLA_V * (hh + 1)]
                sc_t = lax.dot_general(kh[hh], qh, nt, preferred_element_type=F32)
                sd.append(jnp.where(causal_t, sc_t, NEG_INF) if masked else sc_t)
                sd.append(lax.dot_general(vh[hh], doh, nt, preferred_element_type=F32))
            for hh in range(2):
                lse = lse_ref[0, hh:hh + 1, pl.ds(r0, t)]
                dl = dl_ref[0, hh:hh + 1, pl.ds(r0, t)]
                p_t = jnp.exp2(sd[2 * hh] - lse)
                g_t = (p_t * (sd[2 * hh + 1] - dl)).astype(BF16)
                qth = qt_ref[LANES * hh:LANES * (hh + 1), pl.ds(r0, t)]
                doth = dot_ref[MLA_V * hh:MLA_V * (hh + 1), pl.ds(r0, t)]
                dv_ref[MLA_V * hh:MLA_V * (hh + 1), :] += lax.dot_general(
                    doth, p_t.astype(BF16), nt, preferred_element_type=F32)
                dk_ref[LANES * hh:LANES * (hh + 1), :] += lax.dot_general(
                    qth, g_t, nt, preferred_element_type=F32)
                dq_ref[LANES * hh:LANES * (hh + 1), pl.ds(r0, t)] += jnp.dot(
                    kth[hh], g_t, preferred_element_type=F32)

        step(j, True)

        rest = nq - 1 - j

        @pl.when(rest % 2 == 1)
        def _():
            step(j + 1, False)

        pair = j + 1 + rest % 2

        @pl.when((rest // 2) % 2 == 1)
        def _():
            step(pair, False)
            step(pair + 1, False)

        quad = pair + 2 * ((rest // 2) % 2)

        @pl.when((rest // 4) % 2 == 1)
        def _():
            for u in range(4):
                step(quad + u, False)

        first = quad + 4 * ((rest // 4) % 2)

        def trip(p, carry):
            for u in range(8):
                step(first + 8 * p + u, False)
            return carry

        lax.fori_loop(0, rest // 8, trip, 0)
        dk_ref[...] = dk_ref[...] * LN2

        @pl.when(j == nq - 1)
        def _():
            dq_ref[...] = dq_ref[...] * MLA_SCALE

    return pl.pallas_call(
        body, name="mla_attn_bwd", grid=(MLA_HEADS // 2, nq),
        in_specs=[pl.BlockSpec((s, 256), lambda p, j: (0, p)),
                  pl.BlockSpec((256, s), lambda p, j: (p, 0)),
                  pl.BlockSpec((t, 256), lambda p, j: (j, p)),
                  pl.BlockSpec((256, t), lambda p, j: (p, j)),
                  pl.BlockSpec((t, 128), lambda p, j: (j, p)),
                  pl.BlockSpec((s, 128), lambda p, j: (0, p)),
                  pl.BlockSpec((128, s), lambda p, j: (p, 0)),
                  pl.BlockSpec((1, 2, s), lambda p, j: (p, 0, 0)),
                  pl.BlockSpec((1, 2, s), lambda p, j: (p, 0, 0))],
        out_specs=[pl.BlockSpec((256, s), lambda p, j: (p, 0)),
                   pl.BlockSpec((256, t), lambda p, j: (p, j)),
                   pl.BlockSpec((128, t), lambda p, j: (p, j))],
        out_shape=[jax.ShapeDtypeStruct((1024, s), F32), jax.ShapeDtypeStruct((1024, s), F32),
                   jax.ShapeDtypeStruct((512, s), F32)],
        compiler_params=_params(("parallel", "arbitrary")),
    )(q2, q2t, k, kt, v, do, dot, lse_rows, delta_rows)


SWA_SCALE = SWA_DIM ** -0.5
SWA_COLS = SWA_GROUP * BLOCK
SWA_LOG2 = SWA_SCALE * LOG2E


def _swa_tables():
    k = np.arange(2 * BLOCK)[:, None]
    col = np.arange(SWA_COLS)[None, :]
    dist = BLOCK + (col % BLOCK) - k
    valid = (dist >= 0) & (dist < BLOCK)
    out = np.zeros((2, SWA_KV, 2 * BLOCK, SWA_COLS), np.float32)
    for first in range(2):
        ok = valid & ((k >= BLOCK) | (first == 0))
        for j in range(SWA_KV):
            slope = 2.0 ** -(SWA_GROUP * j + col // BLOCK + 1)
            out[first, j] = np.where(ok, -slope * dist * LOG2E, NEG_INF)
    return jnp.asarray(out)


def _swa_tile_inputs(sq_ref, skv_ref, halo_ref, qn_ref, kn_ref, sk_ref, add_ref, first):
    tokens = sq_ref.shape[0]
    kv_all = jnp.concatenate([halo_ref[...], skv_ref[...]], axis=0)
    kv_t = jnp.transpose(kv_all)
    sq_t = jnp.transpose(sq_ref[...])
    k_raw = [kv_t[SWA_DIM * j:SWA_DIM * (j + 1)] for j in range(SWA_KV)]
    v_t = [kv_t[128 + SWA_DIM * j:128 + SWA_DIM * (j + 1)] for j in range(SWA_KV)]
    v_nat = [kv_all[:, 128 + SWA_DIM * j:128 + SWA_DIM * (j + 1)] for j in range(SWA_KV)]
    q_raw = [sq_t[SWA_DIM * h:SWA_DIM * (h + 1)] for h in range(SWA_HEADS)]
    qn_b = jnp.broadcast_to(qn_ref[...], (SWA_DIM, tokens))
    kn_b = jnp.broadcast_to(kn_ref[...], (SWA_DIM, tokens + BLOCK))
    lane_grp = lax.broadcasted_iota(jnp.int32, (1, SWA_COLS), 1) // BLOCK
    sinks, adds = [], []
    for j in range(SWA_KV):
        row = jnp.zeros((1, SWA_COLS), F32)
        for g in range(SWA_GROUP):
            h = SWA_GROUP * j + g
            row = jnp.where(lane_grp == g, sk_ref[:, h:h + 1] * LOG2E, row)
        sinks.append(row)
        adds.append((jnp.where(first, add_ref[1, j], add_ref[0, j]), add_ref[0, j]))
    return k_raw, v_t, v_nat, q_raw, qn_b, kn_b, sinks, adds


def _swa_probs(kb, qs_t, add, sink):
    s2 = jnp.dot(kb, qs_t, preferred_element_type=F32) * SWA_LOG2 + add
    m = jnp.maximum(jnp.max(s2, axis=0, keepdims=True), sink)
    e = jnp.exp2(s2 - m)
    es = jnp.exp2(sink - m)
    inv = 1.0 / (jnp.sum(e, axis=0, keepdims=True) + es)
    return e, inv, es


def _swa_queries(qn_t, j, b):
    return jnp.concatenate([qn_t[SWA_GROUP * j + g][:, BLOCK * b:BLOCK * (b + 1)]
                            for g in range(SWA_GROUP)], axis=1)


def _swa_fwd(proj, qn, kn, sinks, tables):
    s = proj.shape[0]
    ts = min(1024, s)
    nb = ts // BLOCK

    def body(sq_ref, skv_ref, halo_ref, qn_ref, kn_ref, sk_ref, add_ref, o_ref, ot_ref):
        first = pl.program_id(0) == 0
        k_raw, v_t, _, q_raw, qn_b, kn_b, sink_rows, adds = _swa_tile_inputs(
            sq_ref, skv_ref, halo_ref, qn_ref, kn_ref, sk_ref, add_ref, first)
        kn_nat = [jnp.transpose(_rms0(k, kn_b)).astype(BF16) for k in k_raw]
        v_t = [v.astype(BF16) for v in v_t]
        qn_t = [_rms0(q, qn_b).astype(BF16) for q in q_raw]
        for b in range(nb):
            band = slice(BLOCK * b, BLOCK * (b + 2))
            for j in range(SWA_KV):
                e, inv, _ = _swa_probs(kn_nat[j][band], _swa_queries(qn_t, j, b),
                                       adds[j][0 if b == 0 else 1], sink_rows[j])
                o_t = jnp.dot(v_t[j][:, band], (e * inv).astype(BF16),
                              preferred_element_type=F32)
                for g in range(SWA_GROUP):
                    h = SWA_GROUP * j + g
                    ot_ref[SWA_DIM * h:SWA_DIM * (h + 1), BLOCK * b:BLOCK * (b + 1)] = (
                        o_t[:, BLOCK * g:BLOCK * (g + 1)])
        o_ref[...] = jnp.transpose(ot_ref[...])

    const = lambda shape: pl.BlockSpec(shape, lambda i: (0,) * len(shape))
    return pl.pallas_call(
        body, name="swa_fwd", grid=(s // ts,),
        in_specs=[pl.BlockSpec((ts, 512), lambda i: (i, OFF_SQ // 512)),
                  pl.BlockSpec((ts, 256), lambda i: (i, OFF_SKV // 256)),
                  pl.BlockSpec((BLOCK, 256), lambda i: (jnp.maximum(i * nb - 1, 0), OFF_SKV // 256)),
                  const((SWA_DIM, 1)), const((SWA_DIM, 1)), const((1, SWA_HEADS)),
                  const(tables.shape)],
        out_specs=pl.BlockSpec((ts, 512), lambda i: (i, 0)),
        out_shape=jax.ShapeDtypeStruct((s, 512), F32),
        scratch_shapes=[pltpu.VMEM((512, ts), F32)],
        compiler_params=_params(("parallel",)),
    )(proj, proj, proj, qn, kn, sinks, tables)


def _swa_bwd(proj, qn, kn, sinks, tables, do):
    s = proj.shape[0]
    ts = min(1024, s)
    nb = ts // BLOCK
    nt = s // ts

    def body(sq_ref, skv_ref, halo_ref, qn_ref, kn_ref, sk_ref, add_ref, do_ref,
             dsq_ref, dskv_ref, dqn_ref, dkn_ref, dsk_ref, carry_ref, dqt_ref, dkvt_ref):
        step = pl.program_id(0)
        first = step == nt - 1

        @pl.when(step == 0)
        def _():
            carry_ref[...] = jnp.zeros_like(carry_ref)
            dqn_ref[...] = jnp.zeros_like(dqn_ref)
            dkn_ref[...] = jnp.zeros_like(dkn_ref)
            dsk_ref[...] = jnp.zeros_like(dsk_ref)

        k_raw, v_t, v_nat, q_raw, qn_b, kn_b, sink_rows, adds = _swa_tile_inputs(
            sq_ref, skv_ref, halo_ref, qn_ref, kn_ref, sk_ref, add_ref, first)
        kn_f = [_rms0(k, kn_b) for k in k_raw]
        kn_t = [k.astype(BF16) for k in kn_f]
        kn_nat = [jnp.transpose(k).astype(BF16) for k in kn_f]
        v_nat = [v.astype(BF16) for v in v_nat]
        qn_t = [_rms0(q, qn_b).astype(BF16) for q in q_raw]
        do_t = jnp.transpose(do_ref[...].astype(F32)).astype(BF16)

        dkvt_ref[...] = jnp.zeros_like(dkvt_ref)
        dsink = [jnp.zeros((1, SWA_COLS), F32) for _ in range(SWA_KV)]
        nt_dims = (((1,), (1,)), ((), ()))
        for b in range(nb):
            rows = slice(BLOCK * b, BLOCK * (b + 1))
            band = slice(BLOCK * b, BLOCK * (b + 2))
            for j in range(SWA_KV):
                heads = [SWA_GROUP * j + g for g in range(SWA_GROUP)]
                qs_t = _swa_queries(qn_t, j, b)
                dos_t = jnp.concatenate([do_t[SWA_DIM * h:SWA_DIM * (h + 1), rows] for h in heads],
                                        axis=1)
                e, inv, es = _swa_probs(kn_nat[j][band], qs_t, adds[j][0 if b == 0 else 1],
                                        sink_rows[j])
                p = e * inv
                dp = jnp.dot(v_nat[j][band], dos_t, preferred_element_type=F32)
                dsum = jnp.sum(p * dp, axis=0, keepdims=True)
                dsink[j] = dsink[j] - es * inv * dsum
                g_t = (p * (dp - dsum) * SWA_SCALE).astype(BF16)
                dv_t = lax.dot_general(dos_t, p.astype(BF16), nt_dims,
                                       preferred_element_type=F32)
                dk_t = lax.dot_general(qs_t, g_t, nt_dims, preferred_element_type=F32)
                dq_t = jnp.dot(kn_t[j][:, band], g_t, preferred_element_type=F32)
                dkvt_ref[SWA_DIM * j:SWA_DIM * (j + 1), band] += dk_t
                dkvt_ref[128 + SWA_DIM * j:128 + SWA_DIM * (j + 1), band] += dv_t
                for g, h in enumerate(heads):
                    dqt_ref[SWA_DIM * h:SWA_DIM * (h + 1), rows] = dq_t[:, BLOCK * g:BLOCK * (g + 1)]

        dqn = jnp.zeros((SWA_DIM, 1), F32)
        for h in range(SWA_HEADS):
            _, vjp = jax.vjp(_rms0, q_raw[h], qn_ref[...])
            dq, dg = vjp(dqt_ref[SWA_DIM * h:SWA_DIM * (h + 1), :])
            dqt_ref[SWA_DIM * h:SWA_DIM * (h + 1), :] = dq
            dqn = dqn + dg
        dqn_ref[...] += dqn
        dsq_ref[...] = jnp.transpose(dqt_ref[...]).astype(BF16)
        dkn = jnp.zeros((SWA_DIM, 1), F32)
        lane_grp = lax.broadcasted_iota(jnp.int32, (1, SWA_COLS), 1) // BLOCK
        for j in range(SWA_KV):
            _, vjp = jax.vjp(_rms0, k_raw[j], kn_ref[...])
            dk, dg = vjp(dkvt_ref[SWA_DIM * j:SWA_DIM * (j + 1), :])
            dkvt_ref[SWA_DIM * j:SWA_DIM * (j + 1), :] = dk
            dkn = dkn + dg
            for g in range(SWA_GROUP):
                h = SWA_GROUP * j + g
                dsk_ref[:, h:h + 1] += jnp.sum(jnp.where(lane_grp == g, dsink[j], 0.0), axis=1,
                                               keepdims=True)
        dkn_ref[...] += dkn
        dkv = jnp.transpose(dkvt_ref[...])
        dskv_ref[0:ts - BLOCK, :] = dkv[BLOCK:ts].astype(BF16)
        dskv_ref[ts - BLOCK:ts, :] = (dkv[ts:ts + BLOCK] + carry_ref[...]).astype(BF16)
        carry_ref[...] = dkv[0:BLOCK]

    const = lambda shape: pl.BlockSpec(shape, lambda st: (0,) * len(shape))
    return pl.pallas_call(
        body, name="swa_bwd", grid=(nt,),
        in_specs=[pl.BlockSpec((ts, 512), lambda st: (nt - 1 - st, OFF_SQ // 512)),
                  pl.BlockSpec((ts, 256), lambda st: (nt - 1 - st, OFF_SKV // 256)),
                  pl.BlockSpec((BLOCK, 256),
                               lambda st: (jnp.maximum((nt - 1 - st) * nb - 1, 0), OFF_SKV // 256)),
                  const((SWA_DIM, 1)), const((SWA_DIM, 1)), const((1, SWA_HEADS)),
                  const(tables.shape),
                  pl.BlockSpec((ts, 512), lambda st: (nt - 1 - st, 0))],
        out_specs=[pl.BlockSpec((ts, 512), lambda st: (nt - 1 - st, 0)),
                   pl.BlockSpec((ts, 256), lambda st: (nt - 1 - st, 0)),
                   const((SWA_DIM, 1)), const((SWA_DIM, 1)), const((1, SWA_HEADS))],
        out_shape=[jax.ShapeDtypeStruct((s, 512), BF16), jax.ShapeDtypeStruct((s, 256), BF16),
                   jax.ShapeDtypeStruct((SWA_DIM, 1), F32), jax.ShapeDtypeStruct((SWA_DIM, 1), F32),
                   jax.ShapeDtypeStruct((1, SWA_HEADS), F32)],
        scratch_shapes=[pltpu.VMEM((BLOCK, 256), F32), pltpu.VMEM((512, ts), F32),
                        pltpu.VMEM((256, ts + BLOCK), F32)],
        compiler_params=_params(("arbitrary",)),
    )(proj, proj, proj, qn, kn, sinks, tables, do)


HALO = 8


def _shift_down(u, halo, k):
    tm = u.shape[0]
    rid = lax.broadcasted_iota(jnp.int32, u.shape, 0)
    out = pltpu.roll(u, k, 0)
    for r in range(k):
        out = jnp.where(rid == r, halo[HALO - k + r:HALO - k + r + 1, :], out)
    return out


def _shift_up(u, halo, k):
    tm = u.shape[0]
    rid = lax.broadcasted_iota(jnp.int32, u.shape, 0)
    out = pltpu.roll(u, tm - k, 0)
    for r in range(k):
        out = jnp.where(rid == tm - k + r, halo[r:r + 1, :], out)
    return out


def _conv_fwd_vals(conv_ref, convp_ref, cw_ref, is_first):
    c_h, c_b, c_c = conv_ref[:, 0:512], conv_ref[:, 512:1024], conv_ref[:, 1024:1536]
    u = c_c * c_h
    up = jnp.where(is_first, 0.0, convp_ref[:, 1024:1536] * convp_ref[:, 0:512])
    u1 = _shift_down(u, up, 1)
    u2 = _shift_down(u, up, 2)
    yc = cw_ref[0:1, :] * u2 + cw_ref[1:2, :] * u1 + cw_ref[2:3, :] * u
    return c_h, c_b, c_c, u, u1, u2, yc


def _out_fwd(proj, o_mla, o_swa, x, w_out, cw, target=None):
    s = proj.shape[0]
    tm = min(512, s)
    nt = s // tm
    with_loss = target is not None

    def body(*refs):
        conv_ref, convp_ref, gates_ref, om_ref, os_ref, x_ref, w_ref, cw_ref = refs[:8]
        if with_loss:
            t_ref, y_ref, zt_ref, loss_ref, z_ref = refs[8:]
        else:
            y_ref, zt_ref, z_ref = refs[8:]
        i = pl.program_id(0)
        _, c_b, _, _, _, _, yc = _conv_fwd_vals(conv_ref, convp_ref, cw_ref, i == 0)
        mix = (om_ref[...], c_b * yc, os_ref[...])
        for n in range(3):
            g = gates_ref[:, GROUP * n:GROUP * (n + 1)]
            z = mix[n] * (g * _sigmoid(g))
            z_ref[:, GROUP * n:GROUP * (n + 1)] = z.astype(BF16)
            zt_ref[GROUP * n:GROUP * (n + 1), :] = jnp.transpose(z).astype(BF16)
        y = x_ref[...] + jnp.dot(z_ref[...], w_ref[...], preferred_element_type=F32)
        if not with_loss:
            y_ref[...] = y
            return
        err = y - t_ref[...]
        y_ref[...] = err * (1.0 / D_MODEL)

        @pl.when(i == 0)
        def _():
            loss_ref[...] = jnp.zeros_like(loss_ref)

        sq = jnp.sum((err * err).reshape(tm // 8, 8, D_MODEL), axis=0)
        part = sq[:, 0:LANES]
        for c in range(1, D_MODEL // LANES):
            part = part + sq[:, LANES * c:LANES * (c + 1)]
        loss_ref[...] += part

        @pl.when(i == nt - 1)
        def _():
            loss_ref[...] = jnp.full(loss_ref.shape, (0.5 / D_MODEL) * jnp.sum(loss_ref[...]), F32)

    row = lambda width: pl.BlockSpec((tm, width), lambda i: (i, 0))
    in_specs = [pl.BlockSpec((tm, 1536), lambda i: (i, 0)),
                pl.BlockSpec((HALO, 1536), lambda i: (jnp.maximum(i * (tm // HALO) - 1, 0), 0)),
                pl.BlockSpec((tm, 1536), lambda i: (i, 1)),
                row(512), row(512), row(D_MODEL),
                pl.BlockSpec((D_MIX, D_MODEL), lambda i: (0, 0)),
                pl.BlockSpec((HALO, 512), lambda i: (0, 0))]
    out_specs = [row(D_MODEL), pl.BlockSpec((D_MIX, tm), lambda i: (0, i))]
    out_shape = [jax.ShapeDtypeStruct((s, D_MODEL), F32), jax.ShapeDtypeStruct((D_MIX, s), BF16)]
    operands = [proj, proj, proj, o_mla, o_swa, x, w_out, cw]
    if with_loss:
        in_specs.append(row(D_MODEL))
        out_specs.append(pl.BlockSpec((8, LANES), lambda i: (0, 0)))
        out_shape.append(jax.ShapeDtypeStruct((8, LANES), F32))
        operands.append(target)
    return pl.pallas_call(
        body, name="out_fwd_loss" if with_loss else "out_fwd", grid=(nt,),
        in_specs=in_specs, out_specs=out_specs, out_shape=out_shape,
        scratch_shapes=[pltpu.VMEM((tm, D_MIX), BF16)],
        compiler_params=_params(("arbitrary",) if with_loss else ("parallel",)),
    )(*operands)


def _out_bwd(dy, proj, o_mla, o_swa, w_out, cw):
    s = proj.shape[0]
    tm = min(512, s)
    nt = s // tm
    hb = tm // HALO

    def body(dy_ref, dyn_ref, conv_ref, convp_ref, convn_ref, gates_ref, gatesn_ref, om_ref, os_ref,
             w_ref, cw_ref,
             dconv_ref, dgates_ref, dom_ref, domt_ref, delta_ref, dos_ref, dcw_ref):
        i = pl.program_id(0)
        dz = _mm_nt(dy_ref[...], w_ref[...])

        def gate(n):
            g = gates_ref[:, GROUP * n:GROUP * (n + 1)]
            sg = _sigmoid(g)
            return g * sg, sg * (1.0 + g * (1.0 - sg))

        for n, o_ref, do_ref in ((0, om_ref, dom_ref), (2, os_ref, dos_ref)):
            silu, dsilu = gate(n)
            dzn = dz[:, GROUP * n:GROUP * (n + 1)]
            o = o_ref[...]
            do = dzn * silu
            do_ref[...] = do.astype(do_ref.dtype)
            dgates_ref[:, GROUP * n:GROUP * (n + 1)] = (dzn * o * dsilu).astype(BF16)
            if n == 0:
                domt_ref[...] = jnp.transpose(do).astype(BF16)
                t = do * o
                for h in range(MLA_HEADS):
                    delta_ref[:, h:h + 1] = jnp.sum(t[:, MLA_V * h:MLA_V * (h + 1)], axis=-1,
                                                    keepdims=True)

        c_h, c_b, c_c, u, u1, u2, yc = _conv_fwd_vals(conv_ref, convp_ref, cw_ref, i == 0)
        silu, dsilu = gate(1)
        dzc = dz[:, GROUP:2 * GROUP]
        dgates_ref[:, GROUP:2 * GROUP] = (dzc * (c_b * yc) * dsilu).astype(BF16)
        dycr = dzc * silu
        dyc = dycr * c_b
        gn = gatesn_ref[:, GROUP:2 * GROUP]
        dzc_n = _mm_nt(dyn_ref[...], w_ref[GROUP:2 * GROUP, :])
        dyc_n = jnp.where(i == nt - 1, 0.0, dzc_n * (gn * _sigmoid(gn)) * convn_ref[:, 512:1024])
        d1 = _shift_up(dyc, dyc_n, 1)
        d2 = _shift_up(dyc, dyc_n, 2)
        du = cw_ref[2:3, :] * dyc + cw_ref[1:2, :] * d1 + cw_ref[0:1, :] * d2
        dconv_ref[:, 0:512] = (du * c_c).astype(BF16)
        dconv_ref[:, 512:1024] = (dycr * yc).astype(BF16)
        dconv_ref[:, 1024:1536] = (du * c_h).astype(BF16)

        @pl.when(i == 0)
        def _():
            dcw_ref[...] = jnp.zeros_like(dcw_ref)

        for k, uk in enumerate((u2, u1, u)):
            dcw_ref[k:k + 1, :] += jnp.sum(dyc * uk, axis=0, keepdims=True)

    row = lambda width: pl.BlockSpec((tm, width), lambda i: (i, 0))
    prev = lambda i: jnp.maximum(i * hb - 1, 0)
    nxt = lambda i: jnp.minimum((i + 1) * hb, s // HALO - 1)
    return pl.pallas_call(
        body, name="out_bwd", grid=(nt,),
        in_specs=[row(D_MODEL),
                  pl.BlockSpec((HALO, D_MODEL), lambda i: (nxt(i), 0)),
                  pl.BlockSpec((tm, 1536), lambda i: (i, 0)),
                  pl.BlockSpec((HALO, 1536), lambda i: (prev(i), 0)),
                  pl.BlockSpec((HALO, 1536), lambda i: (nxt(i), 0)),
                  pl.BlockSpec((tm, 1536), lambda i: (i, 1)),
                  pl.BlockSpec((HALO, 1536), lambda i: (nxt(i), 1)),
                  row(512), row(512),
                  pl.BlockSpec((D_MIX, D_MODEL), lambda i: (0, 0)),
                  pl.BlockSpec((HALO, 512), lambda i: (0, 0))],
        out_specs=[row(1536), row(1536), row(512), pl.BlockSpec((512, tm), lambda i: (0, i)),
                   row(MLA_HEADS), row(512), pl.BlockSpec((HALO, 512), lambda i: (0, 0))],
        out_shape=[jax.ShapeDtypeStruct((s, 1536), BF16), jax.ShapeDtypeStruct((s, 1536), BF16),
                   jax.ShapeDtypeStruct((s, 512), BF16), jax.ShapeDtypeStruct((512, s), BF16),
                   jax.ShapeDtypeStruct((s, MLA_HEADS), F32),
                   jax.ShapeDtypeStruct((s, 512), BF16), jax.ShapeDtypeStruct((HALO, 512), F32)],
        compiler_params=_params(("arbitrary",)),
    )(dy, dy, proj, proj, proj, proj, proj, o_mla, o_swa, w_out, cw)


def _adam_update(g, w, m, v):
    c1 = 1.0 - ADAM_B1
    c2 = 1.0 - ADAM_B2
    bc1 = 1.0 - ADAM_B1 ** ADAM_STEP
    bc2 = 1.0 - ADAM_B2 ** ADAM_STEP
    m_new = ADAM_B1 * m + c1 * g
    v_new = ADAM_B2 * v + c2 * (g * g)
    delta = -ADAM_LR * ((m_new / bc1) / (jnp.sqrt(v_new / bc2) + ADAM_EPS) + ADAM_WD * w)
    return delta, m_new, v_new


def _adamw(g, w, m, v):
    rows = g.shape[0]
    tr = min(256, rows)

    def body(g_ref, w_ref, m_ref, v_ref, d_ref, mo_ref, vo_ref):
        d_ref[...], mo_ref[...], vo_ref[...] = _adam_update(g_ref[...], w_ref[...], m_ref[...],
                                                            v_ref[...])

    spec = pl.BlockSpec((tr, g.shape[1]), lambda i: (i, 0))
    return pl.pallas_call(
        body, name="adamw", grid=(rows // tr,),
        in_specs=[spec] * 4, out_specs=[spec] * 3,
        out_shape=[jax.ShapeDtypeStruct(g.shape, F32)] * 3,
        compiler_params=_params(("parallel",)),
    )(g, w, m, v)


def _adamw_small(gs, ws, ms, vs):
    n = len(gs)

    def body(*refs):
        ins, outs = refs[:4 * n], refs[4 * n:]
        for a in range(n):
            res = _adam_update(*(ins[kind * n + a][...] for kind in range(4)))
            for kind in range(3):
                outs[kind * n + a][...] = res[kind]

    vmem = pl.BlockSpec(memory_space=pltpu.VMEM)
    out = pl.pallas_call(
        body, name="adamw_small",
        in_specs=[vmem] * (4 * n), out_specs=[vmem] * (3 * n),
        out_shape=[jax.ShapeDtypeStruct(g.shape, F32) for _ in range(3) for g in gs],
    )(*gs, *ws, *ms, *vs)
    return out[:n], out[n:2 * n], out[2 * n:]


HBM_SPEC = pl.BlockSpec(memory_space=pltpu.HBM)


def _place():
    x, y, c = lax.axis_index("x"), lax.axis_index("y"), lax.axis_index("c")
    chips = [(1 - x, y), (x, 1 - y), (1 - x, 1 - y)]
    return x, y, c, chips


def _all_gather(shards):
    na = len(shards)
    halves = [sh.shape[0] // 2 for sh in shards]

    def body(*refs):
        w_refs, a_refs = refs[:na], refs[na:2 * na]
        send_sems, recv_sems = refs[2 * na:]
        x, y, c, chips = _place()
        k = 2 * x + y
        sib = (x, y, 1 - c)

        def slab(a, kk, hc):
            return a_refs[a].at[kk, pl.ds(hc * halves[a], halves[a]), :]

        def copy(a, n, src, dst, to):
            return pltpu.make_async_remote_copy(
                src_ref=src, dst_ref=dst, send_sem=send_sems.at[6 * a + n],
                recv_sem=recv_sems.at[6 * a + n], device_id=to, device_id_type=MESH)

        first = [copy(a, n, w_refs[a].at[pl.ds(c * halves[a], halves[a]), :], slab(a, k, c),
                      (cx, cy, c))
                 for n, (cx, cy) in enumerate(chips) for a in range(na)]
        for cp in first:
            cp.start()
        passed = []
        for n, (cx, cy) in enumerate(chips):
            kk = 2 * cx + cy
            for a in range(na):
                copy(a, n, slab(a, kk, c), slab(a, kk, c), (cx, cy, c)).wait_recv()
                fwd = copy(a, 3 + n, slab(a, kk, c), slab(a, kk, c), sib)
                fwd.start()
                passed.append(fwd)
        for n, (cx, cy) in enumerate(chips):
            kk = 2 * cx + cy
            for a in range(na):
                copy(a, 3 + n, slab(a, kk, 1 - c), slab(a, kk, 1 - c), sib).wait_recv()
        for cp in first + passed:
            cp.wait_send()

    return pl.pallas_call(
        body, name="weights_all_gather",
        in_specs=[HBM_SPEC] * na, out_specs=[HBM_SPEC] * na,
        out_shape=[jax.ShapeDtypeStruct((N_CHIPS,) + sh.shape, sh.dtype) for sh in shards],
        scratch_shapes=[pltpu.SemaphoreType.DMA((6 * na,)), pltpu.SemaphoreType.DMA((6 * na,))],
    )(*shards)


def _fill_own_slab(buf, src, k_idx):
    n, rows, cols = buf.shape
    tr = _row_tile(rows)
    slabs = src.ndim == 3

    def body(k_ref, src_ref, buf_ref, out_ref):
        out_ref[0] = src_ref[0] if slabs else src_ref[...]

    if slabs:
        src_spec = pl.BlockSpec((1, tr, cols), lambda t, k_ref: (k_ref[0], t, 0))
    else:
        src_spec = pl.BlockSpec((tr, cols), lambda t, k_ref: (t, 0))
    return pl.pallas_call(
        body, name="fill_own_slab",
        grid_spec=pltpu.PrefetchScalarGridSpec(
            num_scalar_prefetch=1, grid=(rows // tr,),
            in_specs=[src_spec, pl.BlockSpec(memory_space=pl.ANY)],
            out_specs=pl.BlockSpec((1, tr, cols), lambda t, k_ref: (k_ref[0], t, 0))),
        out_shape=jax.ShapeDtypeStruct(buf.shape, buf.dtype),
        input_output_aliases={2: 0},
        compiler_params=_params(("parallel",)),
    )(k_idx, src, buf)


SEM_SPEC = pl.BlockSpec(memory_space=pltpu.SEMAPHORE)
LATE_COPIES = 6


def _late_copy(a, j, peer_core, src, dst, send_sems, recv_sems, to, sender_core):
    return pltpu.make_async_remote_copy(
        src_ref=src, dst_ref=dst, send_sem=send_sems.at[LATE_COPIES * a + 2 * j + peer_core],
        recv_sem=recv_sems.at[LATE_COPIES * a + 2 * j + sender_core], device_id=to,
        device_id_type=MESH)


def _gather_start(shards, after):
    na = len(shards)

    def body(*refs):
        w_refs, land_refs = refs[:na], refs[na:2 * na]
        send_sems, recv_sems = refs[2 * na + 1], refs[2 * na + 2]
        token = refs[-1]
        x, y, c, chips = _place()
        k = 2 * x + y
        for a in range(na):
            half = w_refs[a].shape[0] // 2
            src = w_refs[a].at[pl.ds(c * half, half), :]
            dst = land_refs[a].at[k, pl.ds(c * half, half), :]
            for j, (cx, cy) in enumerate(chips):
                for tc in range(2):
                    _late_copy(a, j, tc, src, dst, send_sems, recv_sems, (cx, cy, tc), c).start()
        token[...] = jnp.zeros_like(token)

    lands = [pltpu.with_memory_space_constraint(lax.empty((N_CHIPS,) + sh.shape, sh.dtype), pltpu.HBM)
             for sh in shards]
    srcs = [pltpu.with_memory_space_constraint(sh, pltpu.HBM) for sh in shards]
    sems = pltpu.SemaphoreType.DMA((LATE_COPIES * na,))
    aliases = {a: 2 + a for a in range(2 * na)}
    return pl.pallas_call(
        body, name="late_weights_gather_start",
        in_specs=[HBM_SPEC] * (2 * na) + [pl.BlockSpec(memory_space=pl.ANY)],
        out_specs=[SEM_SPEC, SEM_SPEC] + [HBM_SPEC] * (2 * na) + [pl.BlockSpec(memory_space=pltpu.VMEM)],
        out_shape=[sems, sems] + [pltpu.HBM(v.shape, v.dtype) for v in srcs + lands]
        + [jax.ShapeDtypeStruct((8, LANES), F32)],
        input_output_aliases=aliases,
        compiler_params=pltpu.CompilerParams(
            has_side_effects=pltpu.SideEffectType.DATAFLOW_SIDE_EFFECTING),
    )(*srcs, *lands, after)


def _gather_wait(started, na, after):
    send_sems, recv_sems = started[0], started[1]
    bufs = started[2:2 + 2 * na]

    def body(*refs):
        w_refs, land_refs = refs[:na], refs[na:2 * na]
        send_sems, recv_sems = refs[2 * na], refs[2 * na + 1]
        x, y, c, chips = _place()
        k = 2 * x + y
        for a in range(na):
            half = w_refs[a].shape[0] // 2
            src = w_refs[a].at[pl.ds(c * half, half), :]
            for j, (cx, cy) in enumerate(chips):
                kk = 2 * cx + cy
                for pc in range(2):
                    _late_copy(a, j, pc, src, land_refs[a].at[k, pl.ds(c * half, half), :],
                               send_sems, recv_sems, (cx, cy, pc), c).wait_send()
                    pltpu.make_async_remote_copy(
                        src_ref=src, dst_ref=land_refs[a].at[kk, pl.ds(pc * half, half), :],
                        send_sem=send_sems.at[LATE_COPIES * a + 2 * j + pc],
                        recv_sem=recv_sems.at[LATE_COPIES * a + 2 * j + pc],
                        device_id=(cx, cy, pc), device_id_type=MESH).wait_recv()

    out = pl.pallas_call(
        body, name="late_weights_gather_wait",
        in_specs=[HBM_SPEC] * (2 * na) + [SEM_SPEC, SEM_SPEC, pl.BlockSpec(memory_space=pl.ANY)],
        out_specs=[HBM_SPEC] * (2 * na),
        out_shape=[pltpu.HBM(v.shape, v.dtype) for v in bufs],
        input_output_aliases={a: a for a in range(2 * na)},
        compiler_params=pltpu.CompilerParams(
            has_side_effects=pltpu.SideEffectType.DATAFLOW_SIDE_EFFECTING),
    )(*bufs, send_sems, recv_sems, after)
    return out[na:]


def _split_start(name, srcs, land_shapes, n_sems, plan):
    na = len(srcs)

    def body(*refs):
        sends, _ = plan(refs[:na], refs[na:2 * na], refs[2 * na], refs[2 * na + 1])
        for cp in sends:
            cp.start()
        refs[-1][...] = jnp.zeros_like(refs[-1])

    lands = [pltpu.with_memory_space_constraint(lax.empty(shape, dtype), pltpu.HBM)
             for shape, dtype in land_shapes]
    srcs = [pltpu.with_memory_space_constraint(v, pltpu.HBM) for v in srcs]
    sems = pltpu.SemaphoreType.DMA((n_sems,))
    return pl.pallas_call(
        body, name=name,
        in_specs=[HBM_SPEC] * (2 * na),
        out_specs=[SEM_SPEC, SEM_SPEC] + [HBM_SPEC] * (2 * na) + [pl.BlockSpec(memory_space=pltpu.VMEM)],
        out_shape=[sems, sems] + [pltpu.HBM(v.shape, v.dtype) for v in srcs + lands]
        + [jax.ShapeDtypeStruct((8, LANES), F32)],
        input_output_aliases={a: 2 + a for a in range(2 * na)},
        compiler_params=pltpu.CompilerParams(
            has_side_effects=pltpu.SideEffectType.DATAFLOW_SIDE_EFFECTING),
    )(*srcs, *lands)


def _split_wait(name, started, na, plan, after):
    bufs = started[2:2 + 2 * na]

    def body(*refs):
        sends, recvs = plan(refs[:na], refs[na:2 * na], refs[2 * na], refs[2 * na + 1])
        for cp in sends:
            cp.wait_send()
        for cp in recvs:
            cp.wait_recv()

    out = pl.pallas_call(
        body, name=name,
        in_specs=[HBM_SPEC] * (2 * na) + [SEM_SPEC, SEM_SPEC, pl.BlockSpec(memory_space=pl.ANY)],
        out_specs=[HBM_SPEC] * (2 * na),
        out_shape=[pltpu.HBM(v.shape, v.dtype) for v in bufs],
        input_output_aliases={a: a for a in range(2 * na)},
        compiler_params=pltpu.CompilerParams(
            has_side_effects=pltpu.SideEffectType.DATAFLOW_SIDE_EFFECTING),
    )(*bufs, started[0], started[1], after)
    return out[:na], out[na:]


def _plan_to_sibling(g_refs, r_refs, send_sems, recv_sems):
    x, y, c, _ = _place()
    cps = []
    for a, (g, r) in enumerate(zip(g_refs, r_refs)):
        half = g.shape[1] // 2
        cps.append(pltpu.make_async_remote_copy(
            src_ref=g.at[:, pl.ds((1 - c) * half, half), :], dst_ref=r, send_sem=send_sems.at[a],
            recv_sem=recv_sems.at[a], device_id=(x, y, 1 - c), device_id_type=MESH))
    return cps, cps


def _plan_scatter(p_refs, q_refs, send_sems, recv_sems):
    x, y, c, chips = _place()
    k = 2 * x + y
    sends, recvs = [], []
    for a, (p, q) in enumerate(zip(p_refs, q_refs)):
        for i, (cx, cy) in enumerate(chips):
            kk = 2 * cx + cy
            for dst, out in ((q.at[k], sends), (q.at[kk], recvs)):
                out.append(pltpu.make_async_remote_copy(
                    src_ref=p.at[kk], dst_ref=dst, send_sem=send_sems.at[3 * a + i],
                    recv_sem=recv_sems.at[3 * a + i], device_id=(cx, cy, c), device_id_type=MESH))
    return sends, recvs


def _swap_halves_to_sibling(gs):
    na = len(gs)

    def body(*refs):
        g_refs, r_refs = refs[:na], refs[na:2 * na]
        send_sems, recv_sems = refs[2 * na:]
        x, y, c, _ = _place()
        cps = []
        for a in range(na):
            half = g_refs[a].shape[1] // 2
            cps.append(pltpu.make_async_remote_copy(
                src_ref=g_refs[a].at[:, pl.ds((1 - c) * half, half), :], dst_ref=r_refs[a],
                send_sem=send_sems.at[a], recv_sem=recv_sems.at[a], device_id=(x, y, 1 - c),
                device_id_type=MESH))
        for cp in cps:
            cp.start()
        for cp in cps:
            cp.wait()

    return pl.pallas_call(
        body, name="grads_to_sibling",
        in_specs=[HBM_SPEC] * na, out_specs=[HBM_SPEC] * na,
        out_shape=[jax.ShapeDtypeStruct((g.shape[0], g.shape[1] // 2, g.shape[2]), g.dtype)
                   for g in gs],
        scratch_shapes=[pltpu.SemaphoreType.DMA((na,)), pltpu.SemaphoreType.DMA((na,))],
    )(*gs)


def _row_tile(rows):
    return next(t for t in (256, 128, 64) if rows % t == 0)


def _add_sibling(g, r, c_idx, out_dtype):
    n, rows, cols = g.shape
    half = rows // 2
    tr = _row_tile(half)
    nb = half // tr

    def body(c_ref, g_ref, r_ref, p_ref):
        p_ref[...] = (g_ref[...] + r_ref[...]).astype(out_dtype)

    return pl.pallas_call(
        body, name="grads_add_sibling",
        grid_spec=pltpu.PrefetchScalarGridSpec(
            num_scalar_prefetch=1, grid=(n, nb),
            in_specs=[pl.BlockSpec((1, tr, cols), lambda j, t, c_ref: (j, c_ref[0] * nb + t, 0)),
                      pl.BlockSpec((1, tr, cols), lambda j, t, c_ref: (j, t, 0))],
            out_specs=pl.BlockSpec((1, tr, cols), lambda j, t, c_ref: (j, t, 0))),
        out_shape=jax.ShapeDtypeStruct((n, half, cols), out_dtype),
        compiler_params=_params(("parallel", "parallel")),
    )(c_idx, g, r)


def _scatter_to_chips(ps):
    na = len(ps)

    def body(*refs):
        p_refs, q_refs = refs[:na], refs[na:2 * na]
        send_sems, recv_sems = refs[2 * na:]
        x, y, c, chips = _place()
        k = 2 * x + y
        sends = []
        for i, (cx, cy) in enumerate(chips):
            for a in range(na):
                cp = pltpu.make_async_remote_copy(
                    src_ref=p_refs[a].at[2 * cx + cy], dst_ref=q_refs[a].at[k],
                    send_sem=send_sems.at[3 * a + i], recv_sem=recv_sems.at[3 * a + i],
                    device_id=(cx, cy, c), device_id_type=MESH)
                cp.start()
                sends.append(cp)
        for i, (cx, cy) in enumerate(chips):
            kk = 2 * cx + cy
            for a in range(na):
                pltpu.make_async_remote_copy(
                    src_ref=p_refs[a].at[kk], dst_ref=q_refs[a].at[kk],
                    send_sem=send_sems.at[3 * a + i], recv_sem=recv_sems.at[3 * a + i],
                    device_id=(cx, cy, c), device_id_type=MESH).wait_recv()
        for cp in sends:
            cp.wait_send()

    return pl.pallas_call(
        body, name="grads_scatter_to_chips",
        in_specs=[HBM_SPEC] * na, out_specs=[HBM_SPEC] * na,
        out_shape=[jax.ShapeDtypeStruct(p.shape, p.dtype) for p in ps],
        scratch_shapes=[pltpu.SemaphoreType.DMA((3 * na,)), pltpu.SemaphoreType.DMA((3 * na,))],
    )(*ps)


def _sum_chips(q, c_idx):
    n, half, cols = q.shape
    tr = _row_tile(half)
    nb = half // tr

    def body(c_ref, q_ref, o_ref):
        parts = [q_ref[kk].astype(F32) for kk in range(n)]
        o_ref[...] = ((parts[0] + parts[1]) + parts[2]) + parts[3]

    return pl.pallas_call(
        body, name="grads_sum_chips",
        grid_spec=pltpu.PrefetchScalarGridSpec(
            num_scalar_prefetch=1, grid=(nb,),
            in_specs=[pl.BlockSpec((n, tr, cols), lambda t, c_ref: (0, t, 0))],
            out_specs=pl.BlockSpec((tr, cols), lambda t, c_ref: (c_ref[0] * nb + t, 0))),
        out_shape=jax.ShapeDtypeStruct((2 * half, cols), F32),
        compiler_params=_params(("parallel",)),
    )(c_idx, q)


def _join_halves(fulls):
    na = len(fulls)

    def body(*refs):
        o_refs = refs[na:2 * na]
        send_sems, recv_sems = refs[2 * na:]
        x, y, c, _ = _place()
        sends = []
        for a in range(na):
            half = o_refs[a].shape[0] // 2
            rows = o_refs[a].at[pl.ds(c * half, half), :]
            sends.append(pltpu.make_async_remote_copy(
                src_ref=rows, dst_ref=rows, send_sem=send_sems.at[a], recv_sem=recv_sems.at[a],
                device_id=(x, y, 1 - c), device_id_type=MESH))
        for cp in sends:
            cp.start()
        for a in range(na):
            half = o_refs[a].shape[0] // 2
            other = o_refs[a].at[pl.ds((1 - c) * half, half), :]
            pltpu.make_async_remote_copy(
                src_ref=other, dst_ref=other, send_sem=send_sems.at[a], recv_sem=recv_sems.at[a],
                device_id=(x, y, 1 - c), device_id_type=MESH).wait_recv()
        for cp in sends:
            cp.wait_send()

    return pl.pallas_call(
        body, name="grads_join_halves",
        in_specs=[HBM_SPEC] * na, out_specs=[HBM_SPEC] * na,
        out_shape=[jax.ShapeDtypeStruct(f.shape, f.dtype) for f in fulls],
        input_output_aliases={a: a for a in range(na)},
        scratch_shapes=[pltpu.SemaphoreType.DMA((na,)), pltpu.SemaphoreType.DMA((na,))],
    )(*fulls)


def _part_rows(shape):
    size = 1
    for d in shape:
        size *= d
    rows = -(-size // PACK_COLS)
    return size, -(-rows // PACK_ROW_ALIGN) * PACK_ROW_ALIGN


def _pack_rows(arrays, dtype, total_rows):
    parts, used = [], 0
    for a in arrays:
        size, rows = _part_rows(a.shape)
        flat = a.reshape(-1).astype(dtype)
        parts.append(jnp.pad(flat, (0, rows * PACK_COLS - size)).reshape(rows, PACK_COLS))
        used += rows
    parts.append(jnp.zeros((total_rows - used, PACK_COLS), dtype))
    return jnp.concatenate(parts, axis=0)


def _unpack_rows(buf, shapes):
    lead = buf.shape[:-2]
    out, off = [], 0
    for sh in shapes:
        size, rows = _part_rows(sh)
        part = buf[..., off:off + rows, :].reshape(lead + (-1,))[..., :size]
        out.append(part.reshape(lead + tuple(sh)))
        off += rows
    return out


NEW_ORDER = ((928, 1440), (1440, 1952), (1952, 2464), (416, 928), (2464, 2976), (3744, 4256),
             (2976, 3488), (0, 256), (256, 384), (4256, 4320), (384, 416), (4256, 4288),
             (3488, 3616), (3616, 3744))
OLD_ORDER = ((3584, 3840), (3840, 3968), (4032, 4064), (1536, 2048), (0, 512), (512, 1024),
             (1024, 1536), (2048, 2560), (3072, 3584), (4096, 4224), (4224, 4352), (2560, 3072))


def _cols(sources, ranges):
    parts = []
    for a, b in ranges:
        off = 0
        for src in sources:
            width = src.shape[-1]
            lo, hi = max(a, off), min(b, off + width)
            if lo < hi:
                parts.append(src[..., lo - off:hi - off])
            off += width
    return jnp.concatenate(parts, axis=-1)


def _sub_ranges(ranges, a, b):
    out, off = [], 0
    for lo, hi in ranges:
        width = hi - lo
        s0, s1 = max(a, off), min(b, off + width)
        if s0 < s1:
            out.append((lo + s0 - off, lo + s1 - off))
        off += width
    return out


def _rope_tables(s):
    half = MLA_ROPE // 2
    inv_freq = jnp.power(jnp.float32(ROPE_THETA), -jnp.arange(half, dtype=F32) / half)
    ang = inv_freq[:, None] * jnp.arange(s, dtype=F32)[None, :]
    cos, sin = jnp.cos(ang), jnp.sin(ang)
    z = lambda n: jnp.zeros((n, s), F32)
    c = jnp.concatenate([jnp.ones((MLA_NOPE, s), F32), cos, cos, z(32)], axis=0)
    s1 = jnp.concatenate([z(MLA_NOPE), -sin, z(16), z(32)], axis=0)
    s2 = jnp.concatenate([z(MLA_NOPE), z(16), sin, z(32)], axis=0)
    return c, s1, s2


def _pad_lanes(a, n):
    return jnp.pad(a, ((0, 0), (0, n - a.shape[1])))


SHARDED = ("w_in", "w_out", "mla_w_qb", "mla_w_kvb", "conv_w")
REPLICATED = ("norm_g", "mla_q_a_norm", "mla_kv_a_norm", "mla_q_norm", "mla_k_norm",
              "swa_q_norm", "swa_k_norm", "swa_sinks")
WEIGHT_ORDER = ("norm_g", "w_in", "mla_q_a_norm", "mla_w_qb", "mla_kv_a_norm", "mla_w_kvb",
                "mla_q_norm", "mla_k_norm", "conv_w", "swa_q_norm", "swa_k_norm", "swa_sinks", "w_out")
SHARD_AXIS = {"w_in": 2, "w_out": 1, "mla_w_qb": 2, "mla_w_kvb": 2, "conv_w": 2}


def kernel(x, norm_g, w_in, mla_q_a_norm, mla_w_qb, mla_kv_a_norm, mla_w_kvb, mla_q_norm, mla_k_norm, conv_w, swa_q_norm, swa_k_norm, swa_sinks, w_out, loss_target, m_norm_g, m_w_in, m_mla_q_a_norm, m_mla_w_qb, m_mla_kv_a_norm, m_mla_w_kvb, m_mla_q_norm, m_mla_k_norm, m_conv_w, m_swa_q_norm, m_swa_k_norm, m_swa_sinks, m_w_out, v_norm_g, v_w_in, v_mla_q_a_norm, v_mla_w_qb, v_mla_kv_a_norm, v_mla_w_kvb, v_mla_q_norm, v_mla_k_norm, v_conv_w, v_swa_q_norm, v_swa_k_norm, v_swa_sinks, v_w_out):
    weights = dict(norm_g=norm_g, w_in=w_in, mla_q_a_norm=mla_q_a_norm, mla_w_qb=mla_w_qb,
                   mla_kv_a_norm=mla_kv_a_norm, mla_w_kvb=mla_w_kvb, mla_q_norm=mla_q_norm,
                   mla_k_norm=mla_k_norm, conv_w=conv_w, swa_q_norm=swa_q_norm,
                   swa_k_norm=swa_k_norm, swa_sinks=swa_sinks, w_out=w_out)
    mom_m = dict(norm_g=m_norm_g, w_in=m_w_in, mla_q_a_norm=m_mla_q_a_norm, mla_w_qb=m_mla_w_qb,
                 mla_kv_a_norm=m_mla_kv_a_norm, mla_w_kvb=m_mla_w_kvb, mla_q_norm=m_mla_q_norm,
                 mla_k_norm=m_mla_k_norm, conv_w=m_conv_w, swa_q_norm=m_swa_q_norm,
                 swa_k_norm=m_swa_k_norm, swa_sinks=m_swa_sinks, w_out=m_w_out)
    mom_v = dict(norm_g=v_norm_g, w_in=v_w_in, mla_q_a_norm=v_mla_q_a_norm, mla_w_qb=v_mla_w_qb,
                 mla_kv_a_norm=v_mla_kv_a_norm, mla_w_kvb=v_mla_w_kvb, mla_q_norm=v_mla_q_norm,
                 mla_k_norm=v_mla_k_norm, conv_w=v_conv_w, swa_q_norm=v_swa_q_norm,
                 swa_k_norm=v_swa_k_norm, swa_sinks=v_swa_sinks, w_out=v_w_out)
    xs = x[0]
    target = loss_target[0]
    s = xs.shape[0]
    c_idx = lax.axis_index("c").astype(jnp.int32).reshape(1)
    k_idx = (2 * lax.axis_index("x") + lax.axis_index("y")).astype(jnp.int32).reshape(1)

    conv_bits = lax.bitcast_convert_type(conv_w, BF16)
    small_list = [mla_w_qb, mla_w_kvb, conv_bits]
    shard_cols = w_in.shape[2]
    w_in_b = w_in.astype(BF16)
    late = [w_in_b[1], w_out.astype(BF16).reshape(-1, D_MODEL)]
    own = [w_in_b[0], _pack_rows(small_list, BF16, SMALL_ROWS)]
    gathered_in0, gathered_small = [_fill_own_slab(buf, src, k_idx)
                                    for buf, src in zip(_all_gather(own), own)]
    started = _gather_start(late, gathered_in0)
    parts = _unpack_rows(gathered_small, [a.shape for a in small_list])
    join = lambda p, axis: jnp.concatenate([p[k] for k in range(N_CHIPS)], axis=axis)
    w_in_zeros = jnp.zeros((D_MODEL, 64), BF16)
    permuted = lambda slabs: _cols([slabs[k] for k in range(N_CHIPS)] + [w_in_zeros], NEW_ORDER)
    w_qb_full = join(parts[0], 2)
    w_kvb_full = join(parts[1], 2)
    conv_full = lax.bitcast_convert_type(join(parts[2], 2), F32)

    rope = _rope_tables(s)
    swa_tables = _swa_tables()
    layers = []
    for l in range(DEPTH):
        wq = jnp.pad(w_qb_full[l].reshape(MLA_Q_LORA, MLA_HEADS, MLA_QK),
                     ((0, 0), (0, 0), (0, LANES - MLA_QK))).reshape(MLA_Q_LORA, MLA_HEADS * LANES)
        kv = w_kvb_full[l].reshape(MLA_KV_LORA, MLA_HEADS, MLA_NOPE + MLA_V)
        wk = jnp.pad(kv[:, :, :MLA_NOPE], ((0, 0), (0, 0), (0, LANES - MLA_NOPE)))
        wkv = jnp.concatenate([wk.reshape(MLA_KV_LORA, MLA_HEADS * LANES),
                               kv[:, :, MLA_NOPE:].reshape(MLA_KV_LORA, MLA_HEADS * MLA_V)], axis=1)
        layers.append(dict(
            wq=wq, wkv=wkv,
            cw=jnp.pad(conv_full[l], ((0, HALO - 3), (0, 0))),
            g=norm_g[l][None],
            mla_norms=(mla_q_a_norm[l][:, None], mla_kv_a_norm[l][:, None],
                       _pad_lanes(mla_q_norm[l][None], LANES).T, _pad_lanes(mla_k_norm[l][None], LANES).T),
            mla_weights=(wq, wkv, wq.T, wkv.T),
            sqn=swa_q_norm[l][:, None], skn=swa_k_norm[l][:, None], sinks=swa_sinks[l][None]))

    saved = []
    h_in = xs
    layers[0]["w_in"] = permuted(gathered_in0)
    layers[0]["g"] = layers[0]["g"] + started[-1][0:1, 0:1]
    for l in range(DEPTH):
        p = layers[l]
        proj, hb = _in_proj_fwd(h_in, p["g"], p["w_in"])
        q, k, v, qt, kt, vt = _mla_prep_fwd(proj, p["mla_norms"], p["mla_weights"], rope)
        o_mla, lse = _mla_attn_fwd(q, k, vt)
        if l == 0:
            late_in1, late_out = [_fill_own_slab(buf, src, k_idx) for buf, src in
                                  zip(_gather_wait(started, len(late), o_mla), late)]
            layers[1]["w_in"] = permuted(late_in1)
            w_out_full = join(late_out.reshape(N_CHIPS, DEPTH, -1, D_MODEL), 1)
            for n in range(DEPTH):
                layers[n]["w_out"] = w_out_full[n]
        o_swa = _swa_fwd(proj, p["sqn"], p["skn"], p["sinks"], swa_tables)
        last = l == DEPTH - 1
        y, z, *loss_acc = _out_fwd(proj, o_mla, o_swa, h_in, p["w_out"], p["cw"],
                                   target if last else None)
        saved.append(dict(x=h_in, proj=proj, hb=hb, q=q, k=k, v=v, qt=qt, kt=kt, o_mla=o_mla, lse=lse,
                          o_swa=o_swa, z=z))
        h_in = y

    dy, loss_acc = h_in, loss_acc[0]
    loss = lax.psum(loss_acc[0, 0], ("x", "y", "c"))

    grads = {n: [None] * DEPTH for n in WEIGHT_ORDER}

    def in_chunks(l):
        return jnp.stack([_cols(grads["w_in"][l], _sub_ranges(OLD_ORDER, k * shard_cols,
                                                              (k + 1) * shard_cols))
                          for k in range(N_CHIPS)])

    def out_chunks(l):
        return grads["w_out"][l].reshape(N_CHIPS, -1, D_MODEL)

    for l in reversed(range(DEPTH)):
        p, a = layers[l], saved[l]
        dconv, dgates, do_mla, do_mla_t, delta, do_swa, dcw = _out_bwd(dy, a["proj"], a["o_mla"], a["o_swa"],
                                                             p["w_out"], p["cw"])
        grads["w_out"][l] = _weight_grads(a["z"], [dy], "dw_out")[0]
        grads["conv_w"][l] = dcw[0:3]
        delta_rows = jnp.transpose(delta, (1, 0)).reshape(MLA_HEADS // 2, 2, s)
        dq, dk, dv = _mla_attn_bwd(a["q"], a["qt"], a["k"], a["kt"], a["v"], do_mla, do_mla_t,
                                   a["lse"], delta_rows)
        mla_norms = p["mla_norms"]
        if l == 0:
            late_gs, from_sib = _split_wait("late_grads_to_sibling_wait", late_st1, 2,
                                            _plan_to_sibling, dq)
            late_p = [_add_sibling(g, r, c_idx, BF16) for g, r in zip(late_gs, from_sib)]
            late_st2 = _split_start("late_grads_scatter_start", late_p,
                                    [(v.shape, v.dtype) for v in late_p], 6, _plan_scatter)
            mla_norms = (mla_norms[0] + late_st2[-1][0:1, 0:1],) + tuple(mla_norms[1:])
        dmla, dqan, dkvan, dqn, dkn, dwq_t, dwkv_t = _mla_prep_bwd(
            a["proj"], mla_norms, p["mla_weights"], rope, dq, dk, dv)
        dwq, dwkv = dwq_t.T, dwkv_t.T
        dsq, dskv, dsqn, dskn, dsinks = _swa_bwd(a["proj"], p["sqn"], p["skn"], p["sinks"], swa_tables, do_swa)
        if l == 0:
            late_p, late_q = _split_wait("late_grads_scatter_wait", late_st2, 2, _plan_scatter, dsq)
            late_full = [_sum_chips(_fill_own_slab(q_, p_, k_idx), c_idx)
                         for q_, p_ in zip(late_q, late_p)]
        pieces = [dconv, dgates, dsq, dmla, dskv]
        grads["w_in"][l] = _weight_grads(a["hb"], pieces, "dw_in")
        gain = p["g"]
        if l == 0:
            gs0 = (in_chunks(0), out_chunks(0))
            last_p = [_add_sibling(g, r, c_idx, BF16)
                      for g, r in zip(gs0, _swap_halves_to_sibling(gs0))]
            last_st = _split_start("last_grads_scatter_start", last_p,
                                   [(v.shape, v.dtype) for v in last_p], 6, _plan_scatter)
            gain = gain + last_st[-1][0:1, 0:1]
        dx, dg = _in_proj_bwd(pieces, a["x"], gain, p["w_in"], dy)
        if l == 0:
            last_p, last_q = _split_wait("last_grads_scatter_wait", last_st, 2, _plan_scatter, dx)
            last_full = [_sum_chips(_fill_own_slab(q_, p_, k_idx), c_idx)
                         for q_, p_ in zip(last_q, last_p)]
        grads["norm_g"][l] = dg[0]
        grads["mla_q_a_norm"][l] = dqan[:, 0]
        grads["mla_kv_a_norm"][l] = dkvan[:, 0]
        grads["mla_q_norm"][l] = dqn[:MLA_QK, 0]
        grads["mla_k_norm"][l] = dkn[:MLA_QK, 0]
        grads["mla_w_qb"][l] = dwq.reshape(MLA_Q_LORA, MLA_HEADS, LANES)[:, :, :MLA_QK].reshape(
            MLA_Q_LORA, MLA_HEADS * MLA_QK)
        dwk = dwkv[:, :MLA_HEADS * LANES].reshape(MLA_KV_LORA, MLA_HEADS, LANES)[:, :, :MLA_NOPE]
        dwv = dwkv[:, MLA_HEADS * LANES:].reshape(MLA_KV_LORA, MLA_HEADS, MLA_V)
        grads["mla_w_kvb"][l] = jnp.concatenate([dwk, dwv], axis=2).reshape(
            MLA_KV_LORA, MLA_HEADS * (MLA_NOPE + MLA_V))
        grads["swa_q_norm"][l] = dsqn[:, 0]
        grads["swa_k_norm"][l] = dskn[:, 0]
        grads["swa_sinks"][l] = dsinks[0]
        dy = dx
        if l == DEPTH - 1:
            late_st1 = _split_start(
                "late_grads_to_sibling_start", [in_chunks(l), out_chunks(l)],
                [((N_CHIPS, D_MODEL // 2, shard_cols), F32),
                 ((N_CHIPS, D_MIX // N_CHIPS // 2, D_MODEL), F32)], 2, _plan_to_sibling)
    grad_x = dy[None]
    full_grads = {n: jnp.stack(grads[n]) for n in WEIGHT_ORDER if n not in ("w_in", "w_out")}

    rest = tuple(n for n in SHARDED if n not in ("w_in", "w_out"))
    rep_shapes = [weights[n].shape for n in REPLICATED]
    rep_grads = jnp.concatenate([full_grads[n].reshape(-1) for n in REPLICATED])

    def chunk(g, n, k):
        width = g.shape[SHARD_AXIS[n]] // N_CHIPS
        return lax.slice_in_dim(g, k * width, (k + 1) * width, axis=SHARD_AXIS[n])

    g_small = jnp.stack([_pack_rows([chunk(full_grads[n], n, k) for n in rest] + [rep_grads],
                                    F32, SMALL_ROWS) for k in range(N_CHIPS)])
    gs = (g_small,)
    partial = [_add_sibling(g, r, c_idx, F32) for g, r in zip(gs, _swap_halves_to_sibling(gs))]
    by_chip = [_fill_own_slab(q, p, k_idx) for q, p in zip(_scatter_to_chips(partial), partial)]
    g_small_mine, g_in0, g_out0, g_in1, g_out1 = _join_halves(
        [_sum_chips(q, c_idx) for q in by_chip] + last_full + late_full)

    vals = _unpack_rows(g_small_mine, [weights[n].shape for n in rest] + [(rep_grads.shape[0],)])
    grad = dict(zip(rest, vals[:-1]))
    grad["w_in"] = jnp.stack([g_in0, g_in1])
    grad["w_out"] = jnp.stack([g_out0, g_out1])
    off = 0
    for n, sh in zip(REPLICATED, rep_shapes):
        grad[n] = vals[-1][off:off + sh[0] * sh[1]].reshape(sh)
        off += sh[0] * sh[1]
    results = {}
    for n in ("w_in", "w_out"):
        view = lambda a, n=n: a.reshape(-1, weights[n].shape[-1])
        res = _adamw(view(grad[n]), view(weights[n]), view(mom_m[n]), view(mom_v[n]))
        results[n] = [r.reshape(weights[n].shape) for r in res]
    small = tuple(n for n in WEIGHT_ORDER if n not in results)
    res = _adamw_small(*([d[n] for n in small] for d in (grad, weights, mom_m, mom_v)))
    for a, n in enumerate(small):
        results[n] = [res[kind][a] for kind in range(3)]
    unpacked = [grad] + [{n: results[n][kind] for n in WEIGHT_ORDER} for kind in range(3)]
    outs = [loss, grad_x]
    for group in unpacked:
        outs += [group[n] for n in WEIGHT_ORDER]
    return tuple(outs)
```
